```python
import jax, jax.numpy as jnp
from jax import lax
import numpy as np

D_MODEL = 1024
BATCH = 8
SEQ = 2048
DEPTH = 1

D_RNN = D_MODEL
N_LRU_HEADS = 16
LRU_HEAD_DIM = D_RNN // N_LRU_HEADS
CONV_WIDTH = 4
LRU_C = 8.0
LRU_A_MIN = 0.9
LRU_A_MAX = 0.999
D_POOL = D_MODEL // 2
POOL_WINDOWS = (2, 4, 8, 16)
N_POOL_GROUPS = len(POOL_WINDOWS)
POOL_GROUP_DIM = D_POOL // N_POOL_GROUPS
N_BRANCHES = 2
D_FF = 4 * D_MODEL
NORM_EPS = 1e-6
D_IN = 2 * D_RNN + D_POOL + N_BRANCHES * D_MODEL

kernel_name = "hawk_pool_gated_hybrid_block"


def rms_norm(x, g):
    xf = x.astype(jnp.float32)
    y = xf * lax.rsqrt(jnp.mean(xf * xf, axis=-1, keepdims=True) + NORM_EPS)
    return (y * g.astype(jnp.float32)).astype(x.dtype)


def causal_depthwise_conv(x, w, b):
    S = x.shape[1]
    xp = jnp.pad(x, ((0, 0), (CONV_WIDTH - 1, 0), (0, 0)))
    out = b
    for k in range(CONV_WIDTH):
        out = out + xp[:, k:k + S] * w[k]
    return out


def rg_lru(x, w_a, b_a, w_x, b_x, lam):
    B, S, _ = x.shape
    xh = x.reshape(B, S, N_LRU_HEADS, LRU_HEAD_DIM)
    r = jnp.einsum('bshi,hij->bshj', xh, w_a).reshape(B, S, D_RNN) + b_a
    i = jnp.einsum('bshi,hij->bshj', xh, w_x).reshape(B, S, D_RNN) + b_x
    r = jax.nn.sigmoid(r.astype(jnp.float32))
    i = jax.nn.sigmoid(i.astype(jnp.float32))
    log_a = -LRU_C * r * jax.nn.softplus(-lam.astype(jnp.float32))
    a = jnp.exp(log_a)
    mult = jnp.sqrt(-jnp.expm1(2.0 * log_a))
    u = mult * (i * x.astype(jnp.float32))

    def combine(left, right):
        a1, b1 = left
        a2, b2 = right
        return a1 * a2, a2 * b1 + b2

    _, h = lax.associative_scan(combine, (a, u), axis=1)
    return h.astype(x.dtype)


def multiscale_pool(x, w_grp, scale):
    B, S, _ = x.shape
    xf = x.astype(jnp.float32)
    cs = jnp.cumsum(xf, axis=1)
    pos = jnp.arange(1, S + 1, dtype=jnp.float32)[None, :, None]
    outs = []
    for g, w in enumerate(POOL_WINDOWS):
        sl = slice(g * POOL_GROUP_DIM, (g + 1) * POOL_GROUP_DIM)
        c = cs[..., sl]
        c_prev = jnp.pad(c, ((0, 0), (w, 0), (0, 0)))[:, :S]
        cnt = jnp.minimum(pos, float(w))
        outs.append((c - c_prev) / cnt - xf[..., sl])
    p = jnp.stack(outs, axis=2)
    y = jnp.einsum('bsgi,gij->bsgj', p, w_grp.astype(jnp.float32)).reshape(B, S, D_POOL)
    return (y * scale.astype(jnp.float32)).astype(x.dtype)


def hybrid_mixer(h, w_in, b_gate, conv_w, conv_b, lru_w_a, lru_b_a, lru_w_x, lru_b_x,
                 lru_lambda, pool_w, pool_scale, w_lru_up, w_pool_up, w_o):
    proj = h @ w_in
    x_lru, g_lru, x_pool, gates = jnp.split(
        proj, [D_RNN, 2 * D_RNN, 2 * D_RNN + D_POOL], axis=-1)
    x_lru = causal_depthwise_conv(x_lru, conv_w, conv_b)
    y_lru = rg_lru(x_lru, lru_w_a, lru_b_a, lru_w_x, lru_b_x, lru_lambda) * jax.nn.gelu(g_lru)
    y_pool = multiscale_pool(x_pool, pool_w, pool_scale)
    br_a = y_lru @ w_lru_up
    br_b = y_pool @ w_pool_up
    gate_a, gate_b = jnp.split(jax.nn.sigmoid(gates + b_gate), N_BRANCHES, axis=-1)
    return (gate_a * br_a + gate_b * br_b) @ w_o


def sq_relu_mlp(h, w_ff1, w_ff2):
    return jnp.square(jax.nn.relu(h @ w_ff1)) @ w_ff2


def _fwd_setup_inputs(seed: int = 0) -> dict:
    key = jax.random.key(seed)
    ks = jax.random.split(key, 24)
    L = DEPTH
    f32 = jnp.float32

    def nrm(k, shape, fan_in):
        return jax.random.normal(k, shape, f32) * (fan_in ** -0.5)

    def gain(k, shape):
        return 1.0 + 0.02 * jax.random.normal(k, shape, f32)

    def bias(k, shape):
        return 0.01 * jax.random.normal(k, shape, f32)

    u = jax.random.uniform(ks[12], (L, D_RNN), f32, LRU_A_MIN, LRU_A_MAX)
    a_base = u ** (1.0 / LRU_C)
    lru_lambda = jnp.log(a_base) - jnp.log1p(-a_base)

    return {
        "x": jax.random.normal(ks[0], (BATCH, SEQ, D_MODEL), f32),
        "norm_mix_pre": gain(ks[1], (L, D_MODEL)),
        "norm_mix_post": gain(ks[2], (L, D_MODEL)),
        "norm_mlp_pre": gain(ks[3], (L, D_MODEL)),
        "norm_mlp_post": gain(ks[4], (L, D_MODEL)),
        "w_in": nrm(ks[5], (L, D_MODEL, D_IN), D_MODEL),
        "b_gate": bias(ks[6], (L, N_BRANCHES * D_MODEL)),
        "conv_w": nrm(ks[7], (L, CONV_WIDTH, D_RNN), CONV_WIDTH),
        "conv_b": bias(ks[8], (L, D_RNN)),
        "lru_w_a": nrm(ks[9], (L, N_LRU_HEADS, LRU_HEAD_DIM, LRU_HEAD_DIM), LRU_HEAD_DIM),
        "lru_b_a": bias(ks[10], (L, D_RNN)),
        "lru_w_x": nrm(ks[11], (L, N_LRU_HEADS, LRU_HEAD_DIM, LRU_HEAD_DIM), LRU_HEAD_DIM),
        "lru_b_x": bias(ks[13], (L, D_RNN)),
        "lru_lambda": lru_lambda,
        "pool_w": nrm(ks[14], (L, N_POOL_GROUPS, POOL_GROUP_DIM, POOL_GROUP_DIM), POOL_GROUP_DIM),
        "pool_scale": gain(ks[15], (L, D_POOL)),
        "w_lru_up": nrm(ks[16], (L, D_RNN, D_MODEL), D_RNN),
        "w_pool_up": nrm(ks[17], (L, D_POOL, D_MODEL), D_POOL),
        "w_o": nrm(ks[18], (L, D_MODEL, D_MODEL), D_MODEL),
        "w_ff1": nrm(ks[19], (L, D_MODEL, D_FF), D_MODEL),
        "w_ff2": nrm(ks[20], (L, D_FF, D_MODEL), D_FF),
    }


def _fwd_reference(x, norm_mix_pre, norm_mix_post, norm_mlp_pre, norm_mlp_post, w_in, b_gate,
              conv_w, conv_b, lru_w_a, lru_b_a, lru_w_x, lru_b_x, lru_lambda, pool_w,
              pool_scale, w_lru_up, w_pool_up, w_o, w_ff1, w_ff2):
    for l in range(DEPTH):
        h = rms_norm(x, norm_mix_pre[l])
        m = hybrid_mixer(h, w_in[l], b_gate[l], conv_w[l], conv_b[l], lru_w_a[l], lru_b_a[l],
                         lru_w_x[l], lru_b_x[l], lru_lambda[l], pool_w[l], pool_scale[l],
                         w_lru_up[l], w_pool_up[l], w_o[l])
        x = x + rms_norm(m, norm_mix_post[l])
        h = rms_norm(x, norm_mlp_pre[l])
        f = sq_relu_mlp(h, w_ff1[l], w_ff2[l])
        x = x + rms_norm(f, norm_mlp_post[l])
    return x


import jax as _jax
import jax.numpy as _jnp

TWIN_FORMAT = 'train_step'
FWD_PARAMS = ['x', 'norm_mix_pre', 'norm_mix_post', 'norm_mlp_pre', 'norm_mlp_post', 'w_in', 'b_gate', 'conv_w', 'conv_b', 'lru_w_a', 'lru_b_a', 'lru_w_x', 'lru_b_x', 'lru_lambda', 'pool_w', 'pool_scale', 'w_lru_up', 'w_pool_up', 'w_o', 'w_ff1', 'w_ff2']
TWIN_WEIGHTS = ['norm_mix_pre', 'norm_mix_post', 'norm_mlp_pre', 'norm_mlp_post', 'w_in', 'b_gate', 'conv_w', 'conv_b', 'lru_w_a', 'lru_b_a', 'lru_w_x', 'lru_b_x', 'lru_lambda', 'pool_w', 'pool_scale', 'w_lru_up', 'w_pool_up', 'w_o', 'w_ff1', 'w_ff2']
TWIN_DIFF_INPUT = 'x'
TWIN_INPUTS = ['x', 'norm_mix_pre', 'norm_mix_post', 'norm_mlp_pre', 'norm_mlp_post', 'w_in', 'b_gate', 'conv_w', 'conv_b', 'lru_w_a', 'lru_b_a', 'lru_w_x', 'lru_b_x', 'lru_lambda', 'pool_w', 'pool_scale', 'w_lru_up', 'w_pool_up', 'w_o', 'w_ff1', 'w_ff2', 'loss_target', 'm_norm_mix_pre', 'm_norm_mix_post', 'm_norm_mlp_pre', 'm_norm_mlp_post', 'm_w_in', 'm_b_gate', 'm_conv_w', 'm_conv_b', 'm_lru_w_a', 'm_lru_b_a', 'm_lru_w_x', 'm_lru_b_x', 'm_lru_lambda', 'm_pool_w', 'm_pool_scale', 'm_w_lru_up', 'm_w_pool_up', 'm_w_o', 'm_w_ff1', 'm_w_ff2', 'v_norm_mix_pre', 'v_norm_mix_post', 'v_norm_mlp_pre', 'v_norm_mlp_post', 'v_w_in', 'v_b_gate', 'v_conv_w', 'v_conv_b', 'v_lru_w_a', 'v_lru_b_a', 'v_lru_w_x', 'v_lru_b_x', 'v_lru_lambda', 'v_pool_w', 'v_pool_scale', 'v_w_lru_up', 'v_w_pool_up', 'v_w_o', 'v_w_ff1', 'v_w_ff2']
TWIN_OUTPUTS = ['loss', 'grad_x', 'grad_norm_mix_pre', 'grad_norm_mix_post', 'grad_norm_mlp_pre', 'grad_norm_mlp_post', 'grad_w_in', 'grad_b_gate', 'grad_conv_w', 'grad_conv_b', 'grad_lru_w_a', 'grad_lru_b_a', 'grad_lru_w_x', 'grad_lru_b_x', 'grad_lru_lambda', 'grad_pool_w', 'grad_pool_scale', 'grad_w_lru_up', 'grad_w_pool_up', 'grad_w_o', 'grad_w_ff1', 'grad_w_ff2', 'delta_norm_mix_pre', 'delta_norm_mix_post', 'delta_norm_mlp_pre', 'delta_norm_mlp_post', 'delta_w_in', 'delta_b_gate', 'delta_conv_w', 'delta_conv_b', 'delta_lru_w_a', 'delta_lru_b_a', 'delta_lru_w_x', 'delta_lru_b_x', 'delta_lru_lambda', 'delta_pool_w', 'delta_pool_scale', 'delta_w_lru_up', 'delta_w_pool_up', 'delta_w_o', 'delta_w_ff1', 'delta_w_ff2', 'new_m_norm_mix_pre', 'new_m_norm_mix_post', 'new_m_norm_mlp_pre', 'new_m_norm_mlp_post', 'new_m_w_in', 'new_m_b_gate', 'new_m_conv_w', 'new_m_conv_b', 'new_m_lru_w_a', 'new_m_lru_b_a', 'new_m_lru_w_x', 'new_m_lru_b_x', 'new_m_lru_lambda', 'new_m_pool_w', 'new_m_pool_scale', 'new_m_w_lru_up', 'new_m_w_pool_up', 'new_m_w_o', 'new_m_w_ff1', 'new_m_w_ff2', 'new_v_norm_mix_pre', 'new_v_norm_mix_post', 'new_v_norm_mlp_pre', 'new_v_norm_mlp_post', 'new_v_w_in', 'new_v_b_gate', 'new_v_conv_w', 'new_v_conv_b', 'new_v_lru_w_a', 'new_v_lru_b_a', 'new_v_lru_w_x', 'new_v_lru_b_x', 'new_v_lru_lambda', 'new_v_pool_w', 'new_v_pool_scale', 'new_v_w_lru_up', 'new_v_w_pool_up', 'new_v_w_o', 'new_v_w_ff1', 'new_v_w_ff2']
TWIN_LEAF_KINDS = {'loss': 'loss', 'grad_x': 'grad_x', 'grad_norm_mix_pre': 'grad_w', 'grad_norm_mix_post': 'grad_w', 'grad_norm_mlp_pre': 'grad_w', 'grad_norm_mlp_post': 'grad_w', 'grad_w_in': 'grad_w', 'grad_b_gate': 'grad_w', 'grad_conv_w': 'grad_w', 'grad_conv_b': 'grad_w', 'grad_lru_w_a': 'grad_w', 'grad_lru_b_a': 'grad_w', 'grad_lru_w_x': 'grad_w', 'grad_lru_b_x': 'grad_w', 'grad_lru_lambda': 'grad_w', 'grad_pool_w': 'grad_w', 'grad_pool_scale': 'grad_w', 'grad_w_lru_up': 'grad_w', 'grad_w_pool_up': 'grad_w', 'grad_w_o': 'grad_w', 'grad_w_ff1': 'grad_w', 'grad_w_ff2': 'grad_w', 'delta_norm_mix_pre': 'delta_w', 'delta_norm_mix_post': 'delta_w', 'delta_norm_mlp_pre': 'delta_w', 'delta_norm_mlp_post': 'delta_w', 'delta_w_in': 'delta_w', 'delta_b_gate': 'delta_w', 'delta_conv_w': 'delta_w', 'delta_conv_b': 'delta_w', 'delta_lru_w_a': 'delta_w', 'delta_lru_b_a': 'delta_w', 'delta_lru_w_x': 'delta_w', 'delta_lru_b_x': 'delta_w', 'delta_lru_lambda': 'delta_w', 'delta_pool_w': 'delta_w', 'delta_pool_scale': 'delta_w', 'delta_w_lru_up': 'delta_w', 'delta_w_pool_up': 'delta_w', 'delta_w_o': 'delta_w', 'delta_w_ff1': 'delta_w', 'delta_w_ff2': 'delta_w', 'new_m_norm_mix_pre': 'new_m', 'new_m_norm_mix_post': 'new_m', 'new_m_norm_mlp_pre': 'new_m', 'new_m_norm_mlp_post': 'new_m', 'new_m_w_in': 'new_m', 'new_m_b_gate': 'new_m', 'new_m_conv_w': 'new_m', 'new_m_conv_b': 'new_m', 'new_m_lru_w_a': 'new_m', 'new_m_lru_b_a': 'new_m', 'new_m_lru_w_x': 'new_m', 'new_m_lru_b_x': 'new_m', 'new_m_lru_lambda': 'new_m', 'new_m_pool_w': 'new_m', 'new_m_pool_scale': 'new_m', 'new_m_w_lru_up': 'new_m', 'new_m_w_pool_up': 'new_m', 'new_m_w_o': 'new_m', 'new_m_w_ff1': 'new_m', 'new_m_w_ff2': 'new_m', 'new_v_norm_mix_pre': 'new_v', 'new_v_norm_mix_post': 'new_v', 'new_v_norm_mlp_pre': 'new_v', 'new_v_norm_mlp_post': 'new_v', 'new_v_w_in': 'new_v', 'new_v_b_gate': 'new_v', 'new_v_conv_w': 'new_v', 'new_v_conv_b': 'new_v', 'new_v_lru_w_a': 'new_v', 'new_v_lru_b_a': 'new_v', 'new_v_lru_w_x': 'new_v', 'new_v_lru_b_x': 'new_v', 'new_v_lru_lambda': 'new_v', 'new_v_pool_w': 'new_v', 'new_v_pool_scale': 'new_v', 'new_v_w_lru_up': 'new_v', 'new_v_w_pool_up': 'new_v', 'new_v_w_o': 'new_v', 'new_v_w_ff1': 'new_v', 'new_v_w_ff2': 'new_v'}


def _forward(args):
    return _fwd_reference(*[args[k] for k in FWD_PARAMS])


def _output_shape():
    out = _jax.eval_shape(lambda: _forward(_fwd_setup_inputs(0)))
    return out.shape, out.dtype

N_MICROBATCH = 1
ADAM_LR = 0.001
ADAM_B1 = 0.9
ADAM_B2 = 0.999
ADAM_EPS = 1e-08
ADAM_WD = 0.01
ADAM_STEP = 10
PER_EXAMPLE_BATCH_AXIS = {'x': 0, 'loss_target': 0}
SHARED_INPUTS = []
_WEIGHT_DTYPES = {'norm_mix_pre': _jnp.float32, 'norm_mix_post': _jnp.float32, 'norm_mlp_pre': _jnp.float32, 'norm_mlp_post': _jnp.float32, 'w_in': _jnp.float32, 'b_gate': _jnp.float32, 'conv_w': _jnp.float32, 'conv_b': _jnp.float32, 'lru_w_a': _jnp.float32, 'lru_b_a': _jnp.float32, 'lru_w_x': _jnp.float32, 'lru_b_x': _jnp.float32, 'lru_lambda': _jnp.float32, 'pool_w': _jnp.float32, 'pool_scale': _jnp.float32, 'w_lru_up': _jnp.float32, 'w_pool_up': _jnp.float32, 'w_o': _jnp.float32, 'w_ff1': _jnp.float32, 'w_ff2': _jnp.float32}
MOMENT_SCALE = {'norm_mix_pre': 5.597457e-01, 'norm_mix_post': 1.598533e+01, 'norm_mlp_pre': 6.437955e-01, 'norm_mlp_post': 1.631971e+01, 'w_in': 2.423294e-01, 'b_gate': 2.582915e-01, 'conv_w': 5.862719e-01, 'conv_b': 7.788198e+00, 'lru_w_a': 2.824940e-01, 'lru_b_a': 1.491719e-01, 'lru_w_x': 5.154820e-01, 'lru_b_x': 1.267790e-01, 'lru_lambda': 2.424672e-01, 'pool_w': 7.178242e-01, 'pool_scale': 9.094477e-01, 'w_lru_up': 7.051365e-01, 'w_pool_up': 5.759223e-01, 'w_o': 9.047316e-01, 'w_ff1': 3.016048e-01, 'w_ff2': 8.807528e-01}


def _to_microbatches(a, axis):
    t = _jnp.moveaxis(a, axis, 0)
    t = t.reshape((N_MICROBATCH, t.shape[0] // N_MICROBATCH) + t.shape[1:])
    return _jnp.moveaxis(t, 1, axis + 1)


def setup_inputs(seed: int = 0) -> dict:
    inp = _fwd_setup_inputs(seed)
    key = _jax.random.fold_in(_jax.random.key(seed), 7919)
    shape, _ = _output_shape()
    out = dict(inp)
    out["loss_target"] = _jax.random.normal(_jax.random.fold_in(key, 0), shape, _jnp.float32)
    for i, name in enumerate(TWIN_WEIGHTS):
        w = inp[name].astype(_jnp.float32)
        if MOMENT_SCALE is None:
            s = _jnp.sqrt(_jnp.mean(_jnp.square(w)) + 1e-30)
        else:
            s = MOMENT_SCALE[name]
        km, kv = _jax.random.split(_jax.random.fold_in(key, i + 1))
        out[name] = w
        out["m_" + name] = s * _jax.random.normal(km, w.shape, _jnp.float32)
        out["v_" + name] = (s * s) * _jax.random.uniform(kv, w.shape, _jnp.float32, 0.5, 1.5)
    if N_MICROBATCH > 1:
        for name, axis in PER_EXAMPLE_BATCH_AXIS.items():
            out[name] = _to_microbatches(out[name], axis)
    return {'x': out['x'], 'norm_mix_pre': out['norm_mix_pre'], 'norm_mix_post': out['norm_mix_post'], 'norm_mlp_pre': out['norm_mlp_pre'], 'norm_mlp_post': out['norm_mlp_post'], 'w_in': out['w_in'], 'b_gate': out['b_gate'], 'conv_w': out['conv_w'], 'conv_b': out['conv_b'], 'lru_w_a': out['lru_w_a'], 'lru_b_a': out['lru_b_a'], 'lru_w_x': out['lru_w_x'], 'lru_b_x': out['lru_b_x'], 'lru_lambda': out['lru_lambda'], 'pool_w': out['pool_w'], 'pool_scale': out['pool_scale'], 'w_lru_up': out['w_lru_up'], 'w_pool_up': out['w_pool_up'], 'w_o': out['w_o'], 'w_ff1': out['w_ff1'], 'w_ff2': out['w_ff2'], 'loss_target': out['loss_target'], 'm_norm_mix_pre': out['m_norm_mix_pre'], 'm_norm_mix_post': out['m_norm_mix_post'], 'm_norm_mlp_pre': out['m_norm_mlp_pre'], 'm_norm_mlp_post': out['m_norm_mlp_post'], 'm_w_in': out['m_w_in'], 'm_b_gate': out['m_b_gate'], 'm_conv_w': out['m_conv_w'], 'm_conv_b': out['m_conv_b'], 'm_lru_w_a': out['m_lru_w_a'], 'm_lru_b_a': out['m_lru_b_a'], 'm_lru_w_x': out['m_lru_w_x'], 'm_lru_b_x': out['m_lru_b_x'], 'm_lru_lambda': out['m_lru_lambda'], 'm_pool_w': out['m_pool_w'], 'm_pool_scale': out['m_pool_scale'], 'm_w_lru_up': out['m_w_lru_up'], 'm_w_pool_up': out['m_w_pool_up'], 'm_w_o': out['m_w_o'], 'm_w_ff1': out['m_w_ff1'], 'm_w_ff2': out['m_w_ff2'], 'v_norm_mix_pre': out['v_norm_mix_pre'], 'v_norm_mix_post': out['v_norm_mix_post'], 'v_norm_mlp_pre': out['v_norm_mlp_pre'], 'v_norm_mlp_post': out['v_norm_mlp_post'], 'v_w_in': out['v_w_in'], 'v_b_gate': out['v_b_gate'], 'v_conv_w': out['v_conv_w'], 'v_conv_b': out['v_conv_b'], 'v_lru_w_a': out['v_lru_w_a'], 'v_lru_b_a': out['v_lru_b_a'], 'v_lru_w_x': out['v_lru_w_x'], 'v_lru_b_x': out['v_lru_b_x'], 'v_lru_lambda': out['v_lru_lambda'], 'v_pool_w': out['v_pool_w'], 'v_pool_scale': out['v_pool_scale'], 'v_w_lru_up': out['v_w_lru_up'], 'v_w_pool_up': out['v_w_pool_up'], 'v_w_o': out['v_w_o'], 'v_w_ff1': out['v_w_ff1'], 'v_w_ff2': out['v_w_ff2']}


def _loss(weights, diff, rest, loss_target):
    with _jax.named_scope("forward"):
        args = {**rest, TWIN_DIFF_INPUT: diff, **{k: w.astype(_WEIGHT_DTYPES[k]) for k, w in weights.items()}}
        y = _forward(args)
    with _jax.named_scope("loss_head"):
        err = _jnp.square(y.astype(_jnp.float32) - loss_target)
        return 0.5 * _jnp.sum(_jnp.mean(err, axis=-1)) if err.ndim else 0.5 * err


def _adamw(w, g, m, v):
    m = ADAM_B1 * m + (1.0 - ADAM_B1) * g
    v = ADAM_B2 * v + (1.0 - ADAM_B2) * _jnp.square(g)
    m_hat = m / (1.0 - ADAM_B1 ** ADAM_STEP)
    v_hat = v / (1.0 - ADAM_B2 ** ADAM_STEP)
    delta = -ADAM_LR * (m_hat / (_jnp.sqrt(v_hat) + ADAM_EPS) + ADAM_WD * w)
    return delta, m, v


def reference(x, norm_mix_pre, norm_mix_post, norm_mlp_pre, norm_mlp_post, w_in, b_gate, conv_w, conv_b, lru_w_a, lru_b_a, lru_w_x, lru_b_x, lru_lambda, pool_w, pool_scale, w_lru_up, w_pool_up, w_o, w_ff1, w_ff2, loss_target, m_norm_mix_pre, m_norm_mix_post, m_norm_mlp_pre, m_norm_mlp_post, m_w_in, m_b_gate, m_conv_w, m_conv_b, m_lru_w_a, m_lru_b_a, m_lru_w_x, m_lru_b_x, m_lru_lambda, m_pool_w, m_pool_scale, m_w_lru_up, m_w_pool_up, m_w_o, m_w_ff1, m_w_ff2, v_norm_mix_pre, v_norm_mix_post, v_norm_mlp_pre, v_norm_mlp_post, v_w_in, v_b_gate, v_conv_w, v_conv_b, v_lru_w_a, v_lru_b_a, v_lru_w_x, v_lru_b_x, v_lru_lambda, v_pool_w, v_pool_scale, v_w_lru_up, v_w_pool_up, v_w_o, v_w_ff1, v_w_ff2):
    given = dict(x=x, norm_mix_pre=norm_mix_pre, norm_mix_post=norm_mix_post, norm_mlp_pre=norm_mlp_pre, norm_mlp_post=norm_mlp_post, w_in=w_in, b_gate=b_gate, conv_w=conv_w, conv_b=conv_b, lru_w_a=lru_w_a, lru_b_a=lru_b_a, lru_w_x=lru_w_x, lru_b_x=lru_b_x, lru_lambda=lru_lambda, pool_w=pool_w, pool_scale=pool_scale, w_lru_up=w_lru_up, w_pool_up=w_pool_up, w_o=w_o, w_ff1=w_ff1, w_ff2=w_ff2, loss_target=loss_target, m_norm_mix_pre=m_norm_mix_pre, m_norm_mix_post=m_norm_mix_post, m_norm_mlp_pre=m_norm_mlp_pre, m_norm_mlp_post=m_norm_mlp_post, m_w_in=m_w_in, m_b_gate=m_b_gate, m_conv_w=m_conv_w, m_conv_b=m_conv_b, m_lru_w_a=m_lru_w_a, m_lru_b_a=m_lru_b_a, m_lru_w_x=m_lru_w_x, m_lru_b_x=m_lru_b_x, m_lru_lambda=m_lru_lambda, m_pool_w=m_pool_w, m_pool_scale=m_pool_scale, m_w_lru_up=m_w_lru_up, m_w_pool_up=m_w_pool_up, m_w_o=m_w_o, m_w_ff1=m_w_ff1, m_w_ff2=m_w_ff2, v_norm_mix_pre=v_norm_mix_pre, v_norm_mix_post=v_norm_mix_post, v_norm_mlp_pre=v_norm_mlp_pre, v_norm_mlp_post=v_norm_mlp_post, v_w_in=v_w_in, v_b_gate=v_b_gate, v_conv_w=v_conv_w, v_conv_b=v_conv_b, v_lru_w_a=v_lru_w_a, v_lru_b_a=v_lru_b_a, v_lru_w_x=v_lru_w_x, v_lru_b_x=v_lru_b_x, v_lru_lambda=v_lru_lambda, v_pool_w=v_pool_w, v_pool_scale=v_pool_scale, v_w_lru_up=v_w_lru_up, v_w_pool_up=v_w_pool_up, v_w_o=v_w_o, v_w_ff1=v_w_ff1, v_w_ff2=v_w_ff2)
    weights = {n: given[n] for n in TWIN_WEIGHTS}
    shared = {n: given[n] for n in SHARED_INPUTS}
    per_example = {n: given[n] for n in ['x']}
    grad_fn = _jax.value_and_grad(_loss, argnums=(0, 1))

    def one_microbatch(ex, loss_target):
        ex = dict(ex)
        diff = ex.pop(TWIN_DIFF_INPUT)
        return grad_fn(weights, diff, {**shared, **ex}, loss_target)

    if N_MICROBATCH == 1:
        loss, (grad_w, grad_x) = one_microbatch(per_example, given["loss_target"])
    else:
        def body(carry, xs):
            loss_sum, grad_sum = carry
            l_k, (gw_k, gx_k) = one_microbatch(xs[0], xs[1])
            with _jax.named_scope("update"):
                return (loss_sum + l_k, _jax.tree.map(_jnp.add, grad_sum, gw_k)), gx_k

        init = (_jnp.zeros((), _jnp.float32), _jax.tree.map(_jnp.zeros_like, weights))
        (loss, grad_w), grad_x = _jax.lax.scan(body, init, (per_example, given["loss_target"]))
    with _jax.named_scope("update"):
        delta_w, new_m, new_v = {}, {}, {}
        for n in TWIN_WEIGHTS:
            delta_w[n], new_m[n], new_v[n] = _adamw(weights[n], grad_w[n], given["m_" + n], given["v_" + n])
    return (loss, grad_x, *[grad_w[n] for n in TWIN_WEIGHTS], *[delta_w[n] for n in TWIN_WEIGHTS],
            *[new_m[n] for n in TWIN_WEIGHTS], *[new_v[n] for n in TWIN_WEIGHTS])
```

```python
import functools
import math

import jax
import jax.numpy as jnp
from jax import lax
from jax.experimental import pallas as pl
from jax.experimental.pallas import tpu as pltpu

F32 = jnp.float32
BF = jnp.bfloat16

T = 2048
D = 1024
DR = 1024
DP = 512
DF = 4096
DIN = 4608
NCHIP = 4
CW_IN = DIN // NCHIP
LANE = 128
CB = 128
NG = DR // CB
PG = 128
POOL_WINDOWS = (2, 4, 8, 16)
NORM_EPS = 1e-6
LRU_C = 8.0
GELU_C = math.sqrt(2.0 / math.pi)
ADAM_LR = 0.001
ADAM_B1 = 0.9
ADAM_B2 = 0.999
ADAM_EPS = 1e-08
ADAM_WD = 0.01
ADAM_STEP = 10
MESH_ID = pl.DeviceIdType.MESH
ANY = pl.BlockSpec(memory_space=pl.ANY)
SMALL_ROWS = 208
MIB = 1 << 20


def _cp(vmem_mib=None):
    if vmem_mib is None:
        return pltpu.CompilerParams()
    return pltpu.CompilerParams(vmem_limit_bytes=vmem_mib * MIB)


def _mm(a, b):
    return jnp.dot(a.astype(BF), b.astype(BF), preferred_element_type=F32)


def _mm_nt(a, b):
    return lax.dot_general(a.astype(BF), b.astype(BF), (((1,), (1,)), ((), ())),
                           preferred_element_type=F32)


def _mm_tn(a, b):
    return lax.dot_general(a.astype(BF), b.astype(BF), (((0,), (0,)), ((), ())),
                           preferred_element_type=F32)


def _rows(v):
    return lax.broadcasted_iota(jnp.int32, v.shape, 0)


def _sd(v, s, fill=0.0):
    return jnp.where(_rows(v) >= s, pltpu.roll(v, s, axis=0), fill)


def _su(v, s, fill=0.0):
    n = v.shape[0]
    return jnp.where(_rows(v) < n - s, pltpu.roll(v, n - s, axis=0), fill)


def _sigmoid(z):
    return 1.0 / (1.0 + jnp.exp(-z))


def _softplus(z):
    e = jnp.exp(-jnp.abs(z))
    u = 1.0 + e
    d = u - 1.0
    log1p = jnp.where(d == 0.0, e, jnp.log(u) * (e / jnp.where(d == 0.0, 1.0, d)))
    return jnp.maximum(z, 0.0) + log1p


def _mean(v):
    return jnp.mean(v, axis=-1, keepdims=True)


def _colsum(v):
    return jnp.sum(v, axis=0, keepdims=True)


def _acc(ref, val, first):
    @pl.when(first)
    def _():
        ref[...] = val

    @pl.when(jnp.logical_not(first))
    def _():
        ref[...] += val


def _conv(xp, cw, cb):
    x1, x2, x3 = _sd(xp, 1), _sd(xp, 2), _sd(xp, 3)
    xc = cb + cw[0:1] * x3 + cw[1:2] * x2 + cw[2:3] * x1 + cw[3:4] * xp
    return xc, x1, x2, x3


def _lru_gates(xc, wa, ba, wx, bx, lam):
    xcb = xc.astype(BF)
    r = _sigmoid(_mm(xcb, wa) + ba)
    ii = _sigmoid(_mm(xcb, wx) + bx)
    sp = _softplus(-lam)
    la = (-LRU_C) * r * sp
    a = jnp.exp(la)
    mult = jnp.sqrt(-jnp.tanh(la) * (a * a + 1.0))
    return xcb, r, ii, sp, a, mult


def _gelu_parts(g):
    th = jnp.tanh(GELU_C * (g + 0.044715 * (g * g * g)))
    gel = 0.5 * g * (1.0 + th)
    dgel = 0.5 * (1.0 + th) + 0.5 * g * (1.0 - th * th) * (GELU_C * (1.0 + 3.0 * 0.044715 * (g * g)))
    return gel, dgel


def _pool_window(x, steps, shift):
    s, sh = x, 1
    for _ in range(steps):
        s = s + shift(s, sh)
        sh *= 2
    return s


def _fwd_inproj(x, g1, w_in):
    tm = 512

    def body(x_ref, g_ref, w_ref, proj_ref, h_ref):
        @pl.when(pl.program_id(1) == 0)
        def _():
            xv = x_ref[...]
            r = lax.rsqrt(_mean(xv * xv) + NORM_EPS)
            h_ref[...] = ((xv * r) * g_ref[...]).astype(BF)

        proj_ref[...] = jnp.dot(h_ref[...], w_ref[0], preferred_element_type=F32)

    return pl.pallas_call(
        body, name="fwd_inproj", grid=(T // tm, NCHIP),
        in_specs=[pl.BlockSpec((tm, D), lambda i, k: (i, 0)),
                  pl.BlockSpec((1, D), lambda i, k: (0, 0)),
                  pl.BlockSpec((1, D, CW_IN), lambda i, k: (k, 0, 0))],
        out_specs=[pl.BlockSpec((tm, CW_IN), lambda i, k: (i, k)),
                   pl.BlockSpec((tm, D), lambda i, k: (i, 0))],
        out_shape=[jax.ShapeDtypeStruct((T, DIN), F32), jax.ShapeDtypeStruct((T, D), BF)],
        compiler_params=_cp(40),
    )(x, g1, w_in)


def _vec_spec():
    return pl.BlockSpec((1, CB), lambda j: (0, j))


def _fwd_lru(proj, conv_w, conv_b, wa, ba, wx, bx, lam):
    def body(xp_ref, g_ref, cw_ref, cb_ref, wa_ref, ba_ref, wx_ref, bx_ref, lam_ref, y_ref, h_ref):
        xc, _, _, _ = _conv(xp_ref[...], cw_ref[...], cb_ref[...])
        _, _, ii, _, a, mult = _lru_gates(xc, wa_ref[0], ba_ref[...], wx_ref[0], bx_ref[...], lam_ref[...])
        b = mult * (ii * xc)
        s = 1
        while s < T:
            b = b + a * _sd(b, s, 0.0)
            if 2 * s < T:
                a = a * _sd(a, s, 1.0)
            s *= 2
        h_ref[...] = b
        gel, _ = _gelu_parts(g_ref[...])
        y_ref[...] = (b * gel).astype(BF)

    return pl.pallas_call(
        body, name="fwd_lru", grid=(NG,),
        in_specs=[pl.BlockSpec((T, CB), lambda j: (0, j)),
                  pl.BlockSpec((T, CB), lambda j: (0, NG + j)),
                  pl.BlockSpec((4, CB), lambda j: (0, j)),
                  _vec_spec(),
                  pl.BlockSpec((1, CB, CB), lambda j: (j, 0, 0)), _vec_spec(),
                  pl.BlockSpec((1, CB, CB), lambda j: (j, 0, 0)), _vec_spec(),
                  _vec_spec()],
        out_specs=[pl.BlockSpec((T, CB), lambda j: (0, j)), pl.BlockSpec((T, CB), lambda j: (0, j))],
        out_shape=[jax.ShapeDtypeStruct((T, DR), BF), jax.ShapeDtypeStruct((T, DR), F32)],
        compiler_params=_cp(48),
    )(proj, proj, conv_w, conv_b, wa, ba, wx, bx, lam)


def _pool_cnt(w):
    t = lax.broadcasted_iota(jnp.int32, (T, 1), 0)
    return jnp.minimum(t + 1, w).astype(F32)


def _fwd_pool(proj, pool_w, pool_scale):
    def body(xp_ref, pw_ref, sc_ref, y_ref):
        for g, w in enumerate(POOL_WINDOWS):
            cols = slice(g * PG, (g + 1) * PG)
            x = xp_ref[:, cols]
            p = _pool_window(x, g + 1, _sd) / _pool_cnt(w) - x
            y_ref[:, cols] = (_mm(p, pw_ref[g]) * sc_ref[:, cols]).astype(BF)

    return pl.pallas_call(
        body, name="fwd_pool", grid=(1,),
        in_specs=[pl.BlockSpec((T, DP), lambda i: (0, 2 * DR // DP)),
                  pl.BlockSpec((4, PG, PG), lambda i: (0, 0, 0)),
                  pl.BlockSpec((1, DP), lambda i: (0, 0))],
        out_specs=pl.BlockSpec((T, DP), lambda i: (0, 0)),
        out_shape=jax.ShapeDtypeStruct((T, DP), BF),
        compiler_params=_cp(48),
    )(proj, pool_w, pool_scale)


GATE_BLK = 512
GATE_BLK0 = (2 * DR + DP) // GATE_BLK


def _gate_specs(tm):
    return [pl.BlockSpec((tm, GATE_BLK), functools.partial(lambda i, q: (i, GATE_BLK0 + q), q=q))
            for q in range(4)]


def _fwd_merge(x, ylru, ypool, proj, b_gate, g2, g3, w_lru_up, w_pool_up, w_o):
    tm = 512

    def body(x_ref, yl_ref, yp_ref, p0, p1, p2, p3, bg_ref, g2_ref, g3_ref, wl_ref, wp_ref, wo_ref,
             x2_ref, h2_ref, m_ref, mrg_ref, bra_ref, brb_ref):
        bra = jnp.dot(yl_ref[...], wl_ref[...], preferred_element_type=F32)
        yp = yp_ref[...]
        brb = jnp.concatenate([jnp.dot(yp, wp_ref[k], preferred_element_type=F32) for k in range(NCHIP)], axis=1)
        bg = bg_ref[...]
        ga = _sigmoid(jnp.concatenate([p0[...], p1[...]], axis=1) + bg[:, :D])
        gb = _sigmoid(jnp.concatenate([p2[...], p3[...]], axis=1) + bg[:, D:])
        mrg = (ga * bra + gb * brb).astype(BF)
        m = jnp.dot(mrg, wo_ref[...], preferred_element_type=F32)
        r2 = lax.rsqrt(_mean(m * m) + NORM_EPS)
        x2 = x_ref[...] + (m * r2) * g2_ref[...]
        r3 = lax.rsqrt(_mean(x2 * x2) + NORM_EPS)
        x2_ref[...] = x2
        h2_ref[...] = ((x2 * r3) * g3_ref[...]).astype(BF)
        m_ref[...] = m
        mrg_ref[...] = mrg
        bra_ref[...] = bra.astype(BF)
        brb_ref[...] = brb.astype(BF)

    row = lambda w: pl.BlockSpec((tm, w), lambda i: (i, 0))
    full2 = lambda a, b: pl.BlockSpec((a, b), lambda i: (0, 0))
    return pl.pallas_call(
        body, name="fwd_merge", grid=(T // tm,),
        in_specs=[row(D), row(DR), row(DP)] + _gate_specs(tm) +
                 [full2(1, 2 * D), full2(1, D), full2(1, D), full2(DR, D),
                  pl.BlockSpec((NCHIP, DP, D // NCHIP), lambda i: (0, 0, 0)), full2(D, D)],
        out_specs=[row(D)] * 6,
        out_shape=[jax.ShapeDtypeStruct((T, D), F32), jax.ShapeDtypeStruct((T, D), BF),
                   jax.ShapeDtypeStruct((T, D), F32), jax.ShapeDtypeStruct((T, D), BF),
                   jax.ShapeDtypeStruct((T, D), BF), jax.ShapeDtypeStruct((T, D), BF)],
        compiler_params=_cp(48),
    )(x, ylru, ypool, proj, proj, proj, proj, b_gate, g2, g3, w_lru_up, w_pool_up, w_o)


FC = 256
FPC = (DF // NCHIP) // FC


def _fwd_mlp(h2, w_ff1, w_ff2):
    def body(h_ref, w1_ref, w2_ref, a1_ref, f_ref):
        a1 = jnp.maximum(jnp.dot(h_ref[...], w1_ref[0], preferred_element_type=F32), 0.0)
        a1_ref[...] = a1
        _acc(f_ref, jnp.dot((a1 * a1).astype(BF), w2_ref[...], preferred_element_type=F32),
             pl.program_id(0) == 0)

    return pl.pallas_call(
        body, name="fwd_mlp", grid=(DF // FC,),
        in_specs=[pl.BlockSpec((T, D), lambda j: (0, 0)),
                  pl.BlockSpec((1, D, FC), lambda j: (j // FPC, 0, j % FPC)),
                  pl.BlockSpec((FC, D), lambda j: (j, 0))],
        out_specs=[pl.BlockSpec((T, FC), lambda j: (0, j)), pl.BlockSpec((T, D), lambda j: (0, 0))],
        out_shape=[jax.ShapeDtypeStruct((T, DF), F32), jax.ShapeDtypeStruct((T, D), F32)],
        compiler_params=_cp(48),
    )(h2, w_ff1, w_ff2)


def _loss_head(f, x2, target, g4):
    tm = 512

    def body(f_ref, x2_ref, t_ref, g_ref, loss_ref, dy_ref, df_ref, dg_ref):
        first = pl.program_id(0) == 0
        f = f_ref[...]
        g4v = g_ref[...]
        r4 = lax.rsqrt(_mean(f * f) + NORM_EPS)
        fn = f * r4
        e = (x2_ref[...] + fn * g4v) - t_ref[...]
        _acc(loss_ref, jnp.sum(_mean(e * e), axis=0, keepdims=True), first)
        dy = e * (1.0 / D)
        dy_ref[...] = dy
        _acc(dg_ref, _colsum(dy * fn), first)
        dfn = dy * g4v
        df_ref[...] = (r4 * (dfn - fn * _mean(dfn * fn))).astype(BF)

    row = pl.BlockSpec((tm, D), lambda i: (i, 0))
    return pl.pallas_call(
        body, name="loss_head", grid=(T // tm,),
        in_specs=[row, row, row, pl.BlockSpec((1, D), lambda i: (0, 0))],
        out_specs=[pl.BlockSpec((1, 1), lambda i: (0, 0)), row, row, pl.BlockSpec((1, D), lambda i: (0, 0))],
        out_shape=[jax.ShapeDtypeStruct((1, 1), F32), jax.ShapeDtypeStruct((T, D), F32),
                   jax.ShapeDtypeStruct((T, D), BF), jax.ShapeDtypeStruct((1, D), F32)],
        compiler_params=_cp(48),
    )(f, x2, target, g4)


def _bwd_mlp(df, h2, a1, w_ff1, w_ff2):
    def body(df_ref, h_ref, a1_ref, w1_ref, w2_ref, dw1_ref, dw2_ref, dh_ref):
        df = df_ref[...]
        a1 = a1_ref[...]
        dact = _mm_nt(df, w2_ref[...])
        df1 = (dact * (2.0 * a1)).astype(BF)
        dw2_ref[...] = _mm_tn((a1 * a1).astype(BF), df)
        dw1_ref[0] = _mm_tn(h_ref[...], df1)
        _acc(dh_ref, _mm_nt(df1, w1_ref[0]), pl.program_id(0) == 0)

    return pl.pallas_call(
        body, name="bwd_mlp", grid=(DF // FC,),
        in_specs=[pl.BlockSpec((T, D), lambda j: (0, 0)),
                  pl.BlockSpec((T, D), lambda j: (0, 0)),
                  pl.BlockSpec((T, FC), lambda j: (0, j)),
                  pl.BlockSpec((1, D, FC), lambda j: (j // FPC, 0, j % FPC)),
                  pl.BlockSpec((FC, D), lambda j: (j, 0))],
        out_specs=[pl.BlockSpec((1, D, FC), lambda j: (j // FPC, 0, j % FPC)),
                   pl.BlockSpec((FC, D), lambda j: (j, 0)),
                   pl.BlockSpec((T, D), lambda j: (0, 0))],
        out_shape=[jax.ShapeDtypeStruct((NCHIP, D, DF // NCHIP), F32),
                   jax.ShapeDtypeStruct((DF, D), F32), jax.ShapeDtypeStruct((T, D), F32)],
        compiler_params=_cp(56),
    )(df, h2, a1, w_ff1, w_ff2)


def _bwd_merge(dh2, dy, x2, m, mrg, bra, brb, proj, b_gate, ylru, ypool, g2, g3, w_lru_up, w_pool_up, w_o):
    tm = 256
    cpu = D // NCHIP

    def body(dh2_ref, dy_ref, x2_ref, m_ref, mrg_ref, bra_ref, brb_ref, p0, p1, p2, p3, bg_ref, yl_ref, yp_ref,
             g2_ref, g3_ref, wl_ref, wp_ref, wo_ref,
             dx_ref, dgt_ref, dyl_ref, dyp_ref, dwo_ref, dwl_ref, dwp_ref, dg2_ref, dg3_ref, dbg_ref):
        first = pl.program_id(0) == 0
        x2 = x2_ref[...]
        r3 = lax.rsqrt(_mean(x2 * x2) + NORM_EPS)
        x2n = x2 * r3
        dh2 = dh2_ref[...]
        t3 = dh2 * g3_ref[...]
        dx2 = dy_ref[...] + r3 * (t3 - x2n * _mean(t3 * x2n))
        dx_ref[...] = dx2
        _acc(dg3_ref, _colsum(dh2 * x2n), first)
        m = m_ref[...]
        r2 = lax.rsqrt(_mean(m * m) + NORM_EPS)
        mn = m * r2
        _acc(dg2_ref, _colsum(dx2 * mn), first)
        dmn = dx2 * g2_ref[...]
        dm = (r2 * (dmn - mn * _mean(dmn * mn))).astype(BF)
        dmrg = _mm_nt(dm, wo_ref[...])
        _acc(dwo_ref, _mm_tn(mrg_ref[...], dm), first)
        bg = bg_ref[...]
        ga = _sigmoid(jnp.concatenate([p0[...], p1[...]], axis=1) + bg[:, :D])
        gb = _sigmoid(jnp.concatenate([p2[...], p3[...]], axis=1) + bg[:, D:])
        dga = dmrg * bra_ref[...].astype(F32) * (ga * (1.0 - ga))
        dgb = dmrg * brb_ref[...].astype(F32) * (gb * (1.0 - gb))
        dgt_ref[:, :D] = dga.astype(BF)
        dgt_ref[:, D:] = dgb.astype(BF)
        _acc(dbg_ref, jnp.concatenate([_colsum(dga), _colsum(dgb)], axis=1), first)
        dbra = (dmrg * ga).astype(BF)
        dbrb = (dmrg * gb).astype(BF)
        dyl_ref[...] = _mm_nt(dbra, wl_ref[...])
        _acc(dwl_ref, _mm_tn(yl_ref[...], dbra), first)
        yp = yp_ref[...]
        dyp = None
        for k in range(NCHIP):
            dk = dbrb[:, k * cpu:(k + 1) * cpu]
            part = _mm_nt(dk, wp_ref[k])
            dyp = part if dyp is None else dyp + part
            _acc(dwp_ref.at[k], _mm_tn(yp, dk), first)
        dyp_ref[...] = dyp

    row = lambda w: pl.BlockSpec((tm, w), lambda i: (i, 0))
    full2 = lambda a, b: pl.BlockSpec((a, b), lambda i: (0, 0))
    wp_spec = pl.BlockSpec((NCHIP, DP, cpu), lambda i: (0, 0, 0))
    return pl.pallas_call(
        body, name="bwd_merge", grid=(T // tm,),
        in_specs=[row(D)] * 7 + _gate_specs(tm) +
                 [full2(1, 2 * D), row(DR), row(DP), full2(1, D), full2(1, D), full2(DR, D), wp_spec, full2(D, D)],
        out_specs=[row(D), row(2 * D), row(DR), row(DP), full2(D, D), full2(DR, D), wp_spec,
                   full2(1, D), full2(1, D), full2(1, 2 * D)],
        out_shape=[jax.ShapeDtypeStruct((T, D), F32), jax.ShapeDtypeStruct((T, 2 * D), BF),
                   jax.ShapeDtypeStruct((T, DR), F32), jax.ShapeDtypeStruct((T, DP), F32),
                   jax.ShapeDtypeStruct((D, D), F32), jax.ShapeDtypeStruct((DR, D), F32),
                   jax.ShapeDtypeStruct((NCHIP, DP, cpu), F32),
                   jax.ShapeDtypeStruct((1, D), F32), jax.ShapeDtypeStruct((1, D), F32),
                   jax.ShapeDtypeStruct((1, 2 * D), F32)],
        compiler_params=_cp(56),
    )(dh2, dy, x2, m, mrg, bra, brb, proj, proj, proj, proj, b_gate, ylru, ypool, g2, g3, w_lru_up, w_pool_up, w_o)


def _bwd_lru(proj, h, dylru, conv_w, conv_b, wa, ba, wx, bx, lam):
    def body(xp_ref, g_ref, h_ref, dy_ref, cw_ref, cb_ref, wa_ref, ba_ref, wx_ref, bx_ref, lam_ref,
             dxp_ref, dg_ref, dcw_ref, dcb_ref, dwa_ref, dba_ref, dwx_ref, dbx_ref, dlam_ref):
        xp = xp_ref[...]
        cw = cw_ref[...]
        lam = lam_ref[...]
        xc, x1, x2, x3 = _conv(xp, cw, cb_ref[...])
        wa, wx = wa_ref[0], wx_ref[0]
        xcb, r, ii, sp, a, mult = _lru_gates(xc, wa, ba_ref[...], wx, bx_ref[...], lam)
        g = g_ref[...]
        gel, dgel = _gelu_parts(g)
        h = h_ref[...]
        dy = dy_ref[...]
        dg_ref[...] = (dy * h * dgel).astype(BF)
        b = dy * gel
        aa = _su(a, 1, 0.0)
        s = 1
        while s < T:
            b = b + aa * _su(b, s, 0.0)
            if 2 * s < T:
                aa = aa * _su(aa, s, 0.0)
            s *= 2
        da = b * _sd(h, 1, 0.0)
        dmult = b * (ii * xc)
        dii = b * (mult * xc)
        dxc = b * (mult * ii)
        dla = da * a - dmult * ((a * a) / mult)
        dr = dla * ((-LRU_C) * sp)
        dsp = _colsum(dla * ((-LRU_C) * r))
        dlam_ref[...] = -dsp / (1.0 + jnp.exp(lam))
        dzr = dr * (r * (1.0 - r))
        dzi = dii * (ii * (1.0 - ii))
        dzrb, dzib = dzr.astype(BF), dzi.astype(BF)
        dxc = dxc + _mm_nt(dzrb, wa) + _mm_nt(dzib, wx)
        dwa_ref[0] = _mm_tn(xcb, dzrb)
        dwx_ref[0] = _mm_tn(xcb, dzib)
        dba_ref[...] = _colsum(dzr)
        dbx_ref[...] = _colsum(dzi)
        dcb_ref[...] = _colsum(dxc)
        dcw_ref[...] = jnp.concatenate([_colsum(dxc * x3), _colsum(dxc * x2), _colsum(dxc * x1),
                                        _colsum(dxc * xp)], axis=0)
        dxp = cw[3:4] * dxc + cw[2:3] * _su(dxc, 1) + cw[1:2] * _su(dxc, 2) + cw[0:1] * _su(dxc, 3)
        dxp_ref[...] = dxp.astype(BF)

    blk = pl.BlockSpec((T, CB), lambda j: (0, j))
    wsp = pl.BlockSpec((1, CB, CB), lambda j: (j, 0, 0))
    return pl.pallas_call(
        body, name="bwd_lru", grid=(NG,),
        in_specs=[blk, pl.BlockSpec((T, CB), lambda j: (0, NG + j)), blk, blk,
                  pl.BlockSpec((4, CB), lambda j: (0, j)), _vec_spec(), wsp, _vec_spec(), wsp, _vec_spec(),
                  _vec_spec()],
        out_specs=[blk, blk, pl.BlockSpec((4, CB), lambda j: (0, j)), _vec_spec(), wsp, _vec_spec(), wsp,
                   _vec_spec(), _vec_spec()],
        out_shape=[jax.ShapeDtypeStruct((T, DR), BF), jax.ShapeDtypeStruct((T, DR), BF),
                   jax.ShapeDtypeStruct((4, DR), F32), jax.ShapeDtypeStruct((1, DR), F32),
                   jax.ShapeDtypeStruct((NG, CB, CB), F32), jax.ShapeDtypeStruct((1, DR), F32),
                   jax.ShapeDtypeStruct((NG, CB, CB), F32), jax.ShapeDtypeStruct((1, DR), F32),
                   jax.ShapeDtypeStruct((1, DR), F32)],
        compiler_params=_cp(56),
    )(proj, proj, h, dylru, conv_w, conv_b, wa, ba, wx, bx, lam)


def _bwd_pool(proj, dypool, pool_w, pool_scale):
    def body(xp_ref, dy_ref, pw_ref, sc_ref, dx_ref, dw_ref, dsc_ref):
        for g, w in enumerate(POOL_WINDOWS):
            cols = slice(g * PG, (g + 1) * PG)
            cnt = _pool_cnt(w)
            x = xp_ref[:, cols]
            pb = (_pool_window(x, g + 1, _sd) / cnt - x).astype(BF)
            wg = pw_ref[g]
            dy = dy_ref[:, cols]
            dsc_ref[:, cols] = _colsum(dy * _mm(pb, wg))
            dyp = (dy * sc_ref[:, cols]).astype(BF)
            dw_ref[g] = _mm_tn(pb, dyp)
            dp = _mm_nt(dyp, wg)
            dx_ref[:, cols] = (_pool_window(dp / cnt, g + 1, _su) - dp).astype(BF)

    return pl.pallas_call(
        body, name="bwd_pool", grid=(1,),
        in_specs=[pl.BlockSpec((T, DP), lambda i: (0, 2 * DR // DP)),
                  pl.BlockSpec((T, DP), lambda i: (0, 0)),
                  pl.BlockSpec((4, PG, PG), lambda i: (0, 0, 0)),
                  pl.BlockSpec((1, DP), lambda i: (0, 0))],
        out_specs=[pl.BlockSpec((T, DP), lambda i: (0, 0)),
                   pl.BlockSpec((4, PG, PG), lambda i: (0, 0, 0)),
                   pl.BlockSpec((1, DP), lambda i: (0, 0))],
        out_shape=[jax.ShapeDtypeStruct((T, DP), BF), jax.ShapeDtypeStruct((4, PG, PG), F32),
                   jax.ShapeDtypeStruct((1, DP), F32)],
        compiler_params=_cp(48),
    )(proj, dypool, pool_w, pool_scale)


def _bwd_inproj(h1, dproj, w_in):
    def body(h_ref, dp_ref, w_ref, dw_ref, dh_ref):
        dp = dp_ref[...]
        dw_ref[0] = _mm_tn(h_ref[...], dp)
        _acc(dh_ref, _mm_nt(dp, w_ref[0]), pl.program_id(0) == 0)

    return pl.pallas_call(
        body, name="bwd_inproj", grid=(NCHIP,),
        in_specs=[pl.BlockSpec((T, D), lambda k: (0, 0)),
                  pl.BlockSpec((T, CW_IN), lambda k: (0, k)),
                  pl.BlockSpec((1, D, CW_IN), lambda k: (k, 0, 0))],
        out_specs=[pl.BlockSpec((1, D, CW_IN), lambda k: (k, 0, 0)), pl.BlockSpec((T, D), lambda k: (0, 0))],
        out_shape=[jax.ShapeDtypeStruct((NCHIP, D, CW_IN), F32), jax.ShapeDtypeStruct((T, D), F32)],
        compiler_params=_cp(56),
    )(h1, dproj, w_in)


def _bwd_prenorm(x, dh1, dxres, g1):
    tm = 512

    def body(x_ref, dh_ref, dr_ref, g_ref, dx_ref, dg_ref):
        xv = x_ref[...]
        r = lax.rsqrt(_mean(xv * xv) + NORM_EPS)
        xn = xv * r
        dh = dh_ref[...]
        t = dh * g_ref[...]
        dx_ref[...] = dr_ref[...] + r * (t - xn * _mean(t * xn))
        _acc(dg_ref, _colsum(dh * xn), pl.program_id(0) == 0)

    row = pl.BlockSpec((tm, D), lambda i: (i, 0))
    vec = pl.BlockSpec((1, D), lambda i: (0, 0))
    return pl.pallas_call(
        body, name="bwd_prenorm", grid=(T // tm,),
        in_specs=[row, row, row, vec], out_specs=[row, vec],
        out_shape=[jax.ShapeDtypeStruct((T, D), F32), jax.ShapeDtypeStruct((1, D), F32)],
        compiler_params=_cp(48),
    )(x, dh1, dxres, g1)


def _block_diag(w):
    hd = w.shape[-1]
    per = CB // hd
    w4 = w.reshape(NG, per, hd, hd)
    eye = jnp.eye(per, dtype=w.dtype)
    return jnp.einsum("gpij,pq->gpiqj", w4, eye).reshape(NG, CB, CB)


def _block_diag_extract(d, hd):
    per = CB // hd
    d5 = d.reshape(NG, per, hd, per, hd)
    return jnp.stack([d5[:, p, :, p, :] for p in range(per)], axis=1).reshape(NG * per, hd, hd)


def _local_step(x, target, g1, g2, g3, g4, w_in, b_gate, conv_w, conv_b, lru_w_a, lru_b_a, lru_w_x, lru_b_x,
                lru_lambda, pool_w, pool_scale, w_lru_up, w_pool_up, w_o, w_ff1, w_ff2):
    hd = lru_w_a.shape[-1]
    wa = _block_diag(lru_w_a).astype(BF)
    wx = _block_diag(lru_w_x).astype(BF)
    pw = pool_w.astype(BF)
    w_lru_up2 = w_lru_up.reshape(DR, D)
    w_o2 = w_o.reshape(D, D)
    w_ff22 = w_ff2.reshape(DF, D)

    proj, h1 = _fwd_inproj(x, g1, w_in)
    ylru, hs = _fwd_lru(proj, conv_w, conv_b, wa, lru_b_a, wx, lru_b_x, lru_lambda)
    ypool = _fwd_pool(proj, pw, pool_scale)
    x2, h2, m, mrg, bra, brb = _fwd_merge(x, ylru, ypool, proj, b_gate, g2, g3, w_lru_up2, w_pool_up, w_o2)
    a1, f = _fwd_mlp(h2, w_ff1, w_ff22)
    lossp, dy, df, dg4 = _loss_head(f, x2, target, g4)

    dw_ff1, dw_ff2, dh2 = _bwd_mlp(df, h2, a1, w_ff1, w_ff22)
    (dxres, dgates, dylru, dypool, dw_o, dw_lru_up, dw_pool_up, dg2, dg3, dbg) = _bwd_merge(
        dh2, dy, x2, m, mrg, bra, brb, proj, b_gate, ylru, ypool, g2, g3, w_lru_up2, w_pool_up, w_o2)
    dxp, dgl, dcw, dcb, dwa, dba, dwx, dbx, dlam = _bwd_lru(proj, hs, dylru, conv_w, conv_b, wa, lru_b_a, wx,
                                                             lru_b_x, lru_lambda)
    dxpool, dpw, dsc = _bwd_pool(proj, dypool, pw, pool_scale)
    dproj = jnp.concatenate([dxp, dgl, dxpool, dgates], axis=1)
    dw_in, dh1 = _bwd_inproj(h1, dproj, w_in)
    grad_x, dg1 = _bwd_prenorm(x, dh1, dxres, g1)

    small = jnp.concatenate([
        dg1, dg2, dg3, dg4, dbg.reshape(2, D), dcb, dba, dbx, dlam,
        jnp.pad(dsc, ((0, 0), (0, D - DP))), jnp.zeros((1, D), F32), dcw,
        _block_diag_extract(dwa, hd).reshape(-1, D), _block_diag_extract(dwx, hd).reshape(-1, D),
        dpw.reshape(-1, D)], axis=0)
    big = [dw_in, dw_lru_up.reshape(NCHIP, DR // NCHIP, D), dw_pool_up, dw_o.reshape(NCHIP, D // NCHIP, D),
           dw_ff1, dw_ff2.reshape(NCHIP, DF // NCHIP, D)]
    return lossp, grad_x, big, small


def _place():
    x, y, c = lax.axis_index("x"), lax.axis_index("y"), lax.axis_index("c")
    chips = [(1 - x, y), (x, 1 - y), (1 - x, 1 - y)]
    return x, y, c, chips


def _rcopy(src, dst, ssem, rsem, dev):
    return pltpu.make_async_remote_copy(src_ref=src, dst_ref=dst, send_sem=ssem, recv_sem=rsem,
                                        device_id=dev, device_id_type=MESH_ID)


def _all_gather(shards, conv_w):
    n = len(shards)

    def body(*refs):
        ins, cw_in = refs[:n], refs[n]
        outs, cw_out = refs[n + 1:2 * n + 1], refs[2 * n + 1]
        ssem, rsem, lsem, cssem, crsem = refs[2 * n + 2:]
        x, y, c, chips = _place()
        me = 2 * x + y
        sib = (x, y, 1 - c)

        def part(i, k, half):
            hr = shards[i].shape[0] // 2
            return outs[i].at[k, pl.ds(half * hr, hr), :]

        local = [pltpu.make_async_copy(ins[i], outs[i].at[me], lsem.at[i]) for i in range(n)]
        local.append(pltpu.make_async_copy(cw_in, cw_out.at[me], lsem.at[n]))
        for cp in local:
            cp.start()
        sends = []
        for i in range(n):
            hr = shards[i].shape[0] // 2
            for j, chip in enumerate(chips):
                sends.append(_rcopy(ins[i].at[pl.ds(c * hr, hr), :], part(i, me, c),
                                    ssem.at[i, j], rsem.at[i, j], (*chip, c)))
        for j, chip in enumerate(chips):
            sends.append(_rcopy(cw_in, cw_out.at[me], cssem.at[j], crsem.at[j], (*chip, c)))
        for cp in sends:
            cp.start()
        passed = []
        for i in range(n):
            for j, chip in enumerate(chips):
                k = 2 * chip[0] + chip[1]
                got = part(i, k, c)
                _rcopy(got, got, ssem.at[i, j], rsem.at[i, j], (*chip, c)).wait_recv()
                fwd = _rcopy(got, got, ssem.at[i, 3 + j], rsem.at[i, 3 + j], sib)
                fwd.start()
                passed.append(fwd)
        for i in range(n):
            for j, chip in enumerate(chips):
                k = 2 * chip[0] + chip[1]
                got = part(i, k, 1 - c)
                _rcopy(got, got, ssem.at[i, 3 + j], rsem.at[i, 3 + j], sib).wait_recv()
        for j, chip in enumerate(chips):
            k = 2 * chip[0] + chip[1]
            _rcopy(cw_in, cw_out.at[k], cssem.at[j], crsem.at[j], (*chip, c)).wait_recv()
        for cp in sends + passed:
            cp.wait_send()
        for cp in local:
            cp.wait()

    out_shape = [jax.ShapeDtypeStruct((NCHIP,) + s.shape, s.dtype) for s in shards]
    out_shape.append(jax.ShapeDtypeStruct((NCHIP,) + conv_w.shape, conv_w.dtype))
    return pl.pallas_call(
        body, name="gather_weights",
        in_specs=[ANY] * (n + 1), out_specs=[ANY] * (n + 1), out_shape=out_shape,
        scratch_shapes=[pltpu.SemaphoreType.DMA((n, 6)), pltpu.SemaphoreType.DMA((n, 6)),
                        pltpu.SemaphoreType.DMA((n + 1,)), pltpu.SemaphoreType.DMA((3,)),
                        pltpu.SemaphoreType.DMA((3,))],
        compiler_params=pltpu.CompilerParams(has_side_effects=True),
    )(*shards, conv_w)


def _sibling_exchange(big, small):
    n = len(big)

    def body(*refs):
        srcs, lands = refs[:n + 1], refs[n + 1:2 * n + 2]
        ssem, rsem = refs[2 * n + 2:]
        x, y, c, _ = _place()
        sib = (x, y, 1 - c)
        cps = [_rcopy(srcs[i].at[:, 1 - c], lands[i], ssem.at[i], rsem.at[i], sib) for i in range(n)]
        cps.append(_rcopy(srcs[n], lands[n], ssem.at[n], rsem.at[n], sib))
        for cp in cps:
            cp.start()
        for cp in cps:
            cp.wait()

    out_shape = [jax.ShapeDtypeStruct((NCHIP,) + g.shape[2:], g.dtype) for g in big]
    out_shape.append(jax.ShapeDtypeStruct(small.shape, small.dtype))
    return pl.pallas_call(
        body, name="reduce_sibling",
        in_specs=[ANY] * (n + 1), out_specs=[ANY] * (n + 1), out_shape=out_shape,
        scratch_shapes=[pltpu.SemaphoreType.DMA((n + 1,)), pltpu.SemaphoreType.DMA((n + 1,))],
        compiler_params=pltpu.CompilerParams(has_side_effects=True),
    )(*big, small)


def _chip_exchange(parts, small):
    n = len(parts)

    def body(*refs):
        srcs, lands = refs[:n + 1], refs[n + 1:2 * n + 2]
        ssem, rsem, lsem = refs[2 * n + 2:]
        x, y, c, chips = _place()
        me = 2 * x + y
        local = [pltpu.make_async_copy(srcs[i].at[me], lands[i].at[me], lsem.at[i]) for i in range(n)]
        local.append(pltpu.make_async_copy(srcs[n].at[c], lands[n].at[me], lsem.at[n]))
        for cp in local:
            cp.start()
        cps = []
        for i in range(n):
            for j, chip in enumerate(chips):
                k = 2 * chip[0] + chip[1]
                cps.append(_rcopy(srcs[i].at[k], lands[i].at[me], ssem.at[i, j], rsem.at[i, j], (*chip, c)))
        for j, chip in enumerate(chips):
            cps.append(_rcopy(srcs[n].at[c], lands[n].at[me], ssem.at[n, j], rsem.at[n, j], (*chip, c)))
        for cp in cps:
            cp.start()
        for cp in cps:
            cp.wait()
        for cp in local:
            cp.wait()

    out_shape = [jax.ShapeDtypeStruct(p.shape, p.dtype) for p in parts]
    out_shape.append(jax.ShapeDtypeStruct((NCHIP,) + small.shape[1:], small.dtype))
    return pl.pallas_call(
        body, name="reduce_chips",
        in_specs=[ANY] * (n + 1), out_specs=[ANY] * (n + 1), out_shape=out_shape,
        scratch_shapes=[pltpu.SemaphoreType.DMA((n + 1, 3)), pltpu.SemaphoreType.DMA((n + 1, 3)),
                        pltpu.SemaphoreType.DMA((n + 1,))],
        compiler_params=pltpu.CompilerParams(has_side_effects=True),
    )(*parts, small)


def _sibling_share(halves):
    n = len(halves)

    def body(*refs):
        srcs, lands = refs[:n], refs[n:2 * n]
        ssem, rsem, lsem = refs[2 * n:]
        x, y, c, _ = _place()
        sib = (x, y, 1 - c)
        local = [pltpu.make_async_copy(srcs[i], lands[i].at[c], lsem.at[i]) for i in range(n)]
        cps = [_rcopy(srcs[i], lands[i].at[c], ssem.at[i], rsem.at[i], sib) for i in range(n)]
        for cp in local + cps:
            cp.start()
        for i in range(n):
            cps[i].wait_send()
            _rcopy(srcs[i], lands[i].at[1 - c], ssem.at[i], rsem.at[i], sib).wait_recv()
        for cp in local:
            cp.wait()

    out_shape = [jax.ShapeDtypeStruct((2,) + h.shape, h.dtype) for h in halves]
    return pl.pallas_call(
        body, name="reduce_share",
        in_specs=[ANY] * n, out_specs=[ANY] * n, out_shape=out_shape,
        scratch_shapes=[pltpu.SemaphoreType.DMA((n,)), pltpu.SemaphoreType.DMA((n,)),
                        pltpu.SemaphoreType.DMA((n,))],
        compiler_params=pltpu.CompilerParams(has_side_effects=True),
    )(*halves)


def _row_block(rows, cols, itemsize=4, target=MIB):
    br = rows
    while br * cols * itemsize > target and br % 16 == 0:
        br //= 2
    return br


def _add_sibling(g, land, cidx, out_dtype, name):
    _, _, hr, cols = g.shape
    br = _row_block(hr, cols)

    def body(c_ref, g_ref, l_ref, o_ref):
        o_ref[...] = (g_ref[0, 0] + l_ref[0]).astype(out_dtype)[None]

    return pl.pallas_call(
        body, name=name,
        grid_spec=pltpu.PrefetchScalarGridSpec(
            num_scalar_prefetch=1, grid=(NCHIP, hr // br),
            in_specs=[pl.BlockSpec((1, 1, br, cols), lambda k, r, c: (k, c[0], r, 0)),
                      pl.BlockSpec((1, br, cols), lambda k, r, c: (k, r, 0))],
            out_specs=pl.BlockSpec((1, br, cols), lambda k, r, c: (k, r, 0))),
        out_shape=jax.ShapeDtypeStruct((NCHIP, hr, cols), out_dtype),
        compiler_params=_cp(32),
    )(cidx, g, land)


def _add_pair(a, b, name):
    rows, cols = a.shape
    br = _row_block(rows, cols)

    def body(a_ref, b_ref, o_ref):
        o_ref[...] = a_ref[...] + b_ref[...]

    spec = pl.BlockSpec((br, cols), lambda r: (r, 0))
    return pl.pallas_call(body, name=name, grid=(rows // br,), in_specs=[spec, spec], out_specs=spec,
                          out_shape=jax.ShapeDtypeStruct(a.shape, a.dtype), compiler_params=_cp(32))(a, b)


def _add_chips(land, name):
    _, hr, cols = land.shape
    br = _row_block(hr, cols)

    def body(l_ref, o_ref):
        v = [l_ref[k].astype(F32) for k in range(NCHIP)]
        o_ref[...] = (v[0] + v[1]) + (v[2] + v[3])

    return pl.pallas_call(
        body, name=name, grid=(hr // br,),
        in_specs=[pl.BlockSpec((NCHIP, br, cols), lambda r: (0, r, 0))],
        out_specs=pl.BlockSpec((br, cols), lambda r: (r, 0)),
        out_shape=jax.ShapeDtypeStruct((hr, cols), F32),
        compiler_params=_cp(32),
    )(land)


def _adamw(w, g, m, v, name):
    rows, cols = w.shape
    br = _row_block(rows, cols)

    def body(w_ref, g_ref, m_ref, v_ref, d_ref, mo_ref, vo_ref):
        gv = g_ref[...]
        mn = ADAM_B1 * m_ref[...] + (1.0 - ADAM_B1) * gv
        vn = ADAM_B2 * v_ref[...] + (1.0 - ADAM_B2) * (gv * gv)
        m_hat = mn / (1.0 - ADAM_B1 ** ADAM_STEP)
        v_hat = vn / (1.0 - ADAM_B2 ** ADAM_STEP)
        d_ref[...] = -ADAM_LR * (m_hat / (jnp.sqrt(v_hat) + ADAM_EPS) + ADAM_WD * w_ref[...])
        mo_ref[...] = mn
        vo_ref[...] = vn

    spec = pl.BlockSpec((br, cols), lambda r: (r, 0))
    shp = jax.ShapeDtypeStruct(w.shape, F32)
    return pl.pallas_call(body, name=name, grid=(rows // br,), in_specs=[spec] * 4, out_specs=[spec] * 3,
                          out_shape=[shp, shp, shp], compiler_params=_cp(32))(w, g, m, v)


W_NAMES = ["norm_mix_pre", "norm_mix_post", "norm_mlp_pre", "norm_mlp_post", "w_in", "b_gate", "conv_w", "conv_b",
           "lru_w_a", "lru_b_a", "lru_w_x", "lru_b_x", "lru_lambda", "pool_w", "pool_scale", "w_lru_up",
           "w_pool_up", "w_o", "w_ff1", "w_ff2"]
BIG = ["w_in", "w_lru_up", "w_pool_up", "w_o", "w_ff1", "w_ff2"]


def _unpack_small(s, hd, chip):
    cw_full = s[12:16]
    ccols = D // NCHIP
    return {
        "norm_mix_pre": s[0:1], "norm_mix_post": s[1:2], "norm_mlp_pre": s[2:3], "norm_mlp_post": s[3:4],
        "b_gate": s[4:6].reshape(1, 2 * D), "conv_b": s[6:7], "lru_b_a": s[7:8], "lru_b_x": s[8:9],
        "lru_lambda": s[9:10], "pool_scale": s[10:11, :DP],
        "conv_w": lax.dynamic_slice(cw_full, (0, chip * ccols), (4, ccols)),
        "lru_w_a": s[16:80].reshape(1, -1, hd, hd), "lru_w_x": s[80:144].reshape(1, -1, hd, hd),
        "pool_w": s[144:208].reshape(1, len(POOL_WINDOWS), PG, PG),
    }


def kernel(x, norm_mix_pre, norm_mix_post, norm_mlp_pre, norm_mlp_post, w_in, b_gate, conv_w, conv_b, lru_w_a, lru_b_a, lru_w_x, lru_b_x, lru_lambda, pool_w, pool_scale, w_lru_up, w_pool_up, w_o, w_ff1, w_ff2, loss_target, m_norm_mix_pre, m_norm_mix_post, m_norm_mlp_pre, m_norm_mlp_post, m_w_in, m_b_gate, m_conv_w, m_conv_b, m_lru_w_a, m_lru_b_a, m_lru_w_x, m_lru_b_x, m_lru_lambda, m_pool_w, m_pool_scale, m_w_lru_up, m_w_pool_up, m_w_o, m_w_ff1, m_w_ff2, v_norm_mix_pre, v_norm_mix_post, v_norm_mlp_pre, v_norm_mlp_post, v_w_in, v_b_gate, v_conv_w, v_conv_b, v_lru_w_a, v_lru_b_a, v_lru_w_x, v_lru_b_x, v_lru_lambda, v_pool_w, v_pool_scale, v_w_lru_up, v_w_pool_up, v_w_o, v_w_ff1, v_w_ff2):
    args = dict(locals())
    w = {n: args[n] for n in W_NAMES}
    mom = {n: args["m_" + n] for n in W_NAMES}
    var = {n: args["v_" + n] for n in W_NAMES}
    chip = 2 * lax.axis_index("x") + lax.axis_index("y")
    cidx = lax.axis_index("c").astype(jnp.int32).reshape(1)
    hd = lru_w_a.shape[-1]

    shards = [w[n][0].astype(BF) for n in BIG]
    gathered = _all_gather(shards, conv_w[0])
    full = dict(zip(BIG, gathered[:-1]))
    conv_w_full = jnp.transpose(gathered[-1], (1, 0, 2)).reshape(4, DR)

    lossp, grad_x, big, small = _local_step(
        x[0], loss_target[0], norm_mix_pre, norm_mix_post, norm_mlp_pre, norm_mlp_post, full["w_in"], b_gate,
        conv_w_full, conv_b, lru_w_a[0], lru_b_a, lru_w_x[0], lru_b_x, lru_lambda, pool_w[0], pool_scale,
        full["w_lru_up"], full["w_pool_up"], full["w_o"], full["w_ff1"], full["w_ff2"])
    loss = 0.5 * lax.psum(lossp[0, 0], ("x", "y", "c"))

    big5 = [g.reshape(NCHIP, 2, g.shape[1] // 2, g.shape[2]) for g in big]
    lands = _sibling_exchange(big5, small)
    parts = [_add_sibling(g, l, cidx, BF, "add_sibling_" + n) for g, l, n in zip(big5, lands[:-1], BIG)]
    small2 = _add_pair(small, lands[-1], "add_sibling_small")
    lands2 = _chip_exchange(parts, small2.reshape(2, SMALL_ROWS // 2, D))
    halves = [_add_chips(l, "add_chips_" + n) for l, n in zip(lands2, BIG + ["small"])]
    shared = _sibling_share(halves)
    grads = {n: s.reshape(1, -1, s.shape[-1]) for n, s in zip(BIG, shared[:-1])}
    grads.update(_unpack_small(shared[-1].reshape(SMALL_ROWS, D), hd, chip))

    delta, new_m, new_v = {}, {}, {}
    for n in W_NAMES:
        shp = w[n].shape
        cols = shp[-1]
        as2 = lambda a: a.reshape(-1, cols)
        d, mo, vo = _adamw(as2(w[n]), as2(grads[n]), as2(mom[n]), as2(var[n]), "adamw_" + n)
        grads[n] = grads[n].reshape(shp)
        delta[n], new_m[n], new_v[n] = d.reshape(shp), mo.reshape(shp), vo.reshape(shp)

    return (loss, grad_x[None], *[grads[n] for n in W_NAMES], *[delta[n] for n in W_NAMES],
            *[new_m[n] for n in W_NAMES], *[new_v[n] for n in W_NAMES])
```

```python
import functools
import math

import jax
import jax.numpy as jnp
from jax import lax
from jax.experimental import pallas as pl
from jax.experimental.pallas import tpu as pltpu

F32 = jnp.float32
BF = jnp.bfloat16

T = 2048
D = 1024
DR = 1024
DP = 512
DF = 4096
DIN = 4608
NCHIP = 4
CW_IN = DIN // NCHIP
LANE = 128
CB = 128
NG = DR // CB
PG = 128
POOL_WINDOWS = (2, 4, 8, 16)
NORM_EPS = 1e-6
LRU_C = 8.0
GELU_C = math.sqrt(2.0 / math.pi)
ADAM_LR = 0.001
ADAM_B1 = 0.9
ADAM_B2 = 0.999
ADAM_EPS = 1e-08
ADAM_WD = 0.01
ADAM_STEP = 10
MESH_ID = pl.DeviceIdType.MESH
ANY = pl.BlockSpec(memory_space=pl.ANY)
SMALL_ROWS = 208
LOSS_ROW = 11
MIB = 1 << 20


def _cp(vmem_mib=None):
    if vmem_mib is None:
        return pltpu.CompilerParams()
    return pltpu.CompilerParams(vmem_limit_bytes=vmem_mib * MIB)


def _mm(a, b):
    return jnp.dot(a.astype(BF), b.astype(BF), preferred_element_type=F32)


def _mm_nt(a, b):
    return lax.dot_general(a.astype(BF), b.astype(BF), (((1,), (1,)), ((), ())),
                           preferred_element_type=F32)


def _mm_tn(a, b):
    return lax.dot_general(a.astype(BF), b.astype(BF), (((0,), (0,)), ((), ())),
                           preferred_element_type=F32)


def _rows(v):
    return lax.broadcasted_iota(jnp.int32, v.shape, 0)


def _sd(v, s, fill=0.0):
    return jnp.where(_rows(v) >= s, pltpu.roll(v, s, axis=0), fill)


def _su(v, s, fill=0.0):
    n = v.shape[0]
    return jnp.where(_rows(v) < n - s, pltpu.roll(v, n - s, axis=0), fill)


def _sigmoid(z):
    return 1.0 / (1.0 + jnp.exp(-z))


def _softplus(z):
    e = jnp.exp(-jnp.abs(z))
    u = 1.0 + e
    d = u - 1.0
    log1p = jnp.where(d == 0.0, e, jnp.log(u) * (e / jnp.where(d == 0.0, 1.0, d)))
    return jnp.maximum(z, 0.0) + log1p


def _mean(v):
    return jnp.mean(v, axis=-1, keepdims=True)


def _colsum(v):
    return jnp.sum(v, axis=0, keepdims=True)


def _acc(ref, val, first):
    @pl.when(first)
    def _():
        ref[...] = val

    @pl.when(jnp.logical_not(first))
    def _():
        ref[...] += val


def _conv(xp, cw, cb):
    x1, x2, x3 = _sd(xp, 1), _sd(xp, 2), _sd(xp, 3)
    xc = cb + cw[0:1] * x3 + cw[1:2] * x2 + cw[2:3] * x1 + cw[3:4] * xp
    return xc, x1, x2, x3


def _lru_gates(xc, wa, ba, wx, bx, lam):
    xcb = xc.astype(BF)
    r = _sigmoid(_mm(xcb, wa) + ba)
    ii = _sigmoid(_mm(xcb, wx) + bx)
    sp = _softplus(-lam)
    la = (-LRU_C) * r * sp
    a = jnp.exp(la)
    mult = jnp.sqrt(-jnp.tanh(la) * (a * a + 1.0))
    return xcb, r, ii, sp, a, mult


def _gelu_parts(g):
    th = jnp.tanh(GELU_C * (g + 0.044715 * (g * g * g)))
    gel = 0.5 * g * (1.0 + th)
    dgel = 0.5 * (1.0 + th) + 0.5 * g * (1.0 - th * th) * (GELU_C * (1.0 + 3.0 * 0.044715 * (g * g)))
    return gel, dgel


def _pool_window(x, steps, shift):
    s, sh = x, 1
    for _ in range(steps):
        s = s + shift(s, sh)
        sh *= 2
    return s


def _fwd_inproj(x, g1, w_in):
    tm = 512

    def body(x_ref, g_ref, w_ref, proj_ref, h_ref):
        @pl.when(pl.program_id(1) == 0)
        def _():
            xv = x_ref[...]
            r = lax.rsqrt(_mean(xv * xv) + NORM_EPS)
            h_ref[...] = ((xv * r) * g_ref[...]).astype(BF)

        proj_ref[...] = jnp.dot(h_ref[...], w_ref[0], preferred_element_type=F32)

    return pl.pallas_call(
        body, name="fwd_inproj", grid=(T // tm, NCHIP),
        in_specs=[pl.BlockSpec((tm, D), lambda i, k: (i, 0)),
                  pl.BlockSpec((1, D), lambda i, k: (0, 0)),
                  pl.BlockSpec((1, D, CW_IN), lambda i, k: (k, 0, 0))],
        out_specs=[pl.BlockSpec((tm, CW_IN), lambda i, k: (i, k)),
                   pl.BlockSpec((tm, D), lambda i, k: (i, 0))],
        out_shape=[jax.ShapeDtypeStruct((T, DIN), F32), jax.ShapeDtypeStruct((T, D), BF)],
        compiler_params=_cp(40),
    )(x, g1, w_in)


def _vec_spec():
    return pl.BlockSpec((1, CB), lambda j: (0, j))


def _fwd_lru(proj, conv_w, conv_b, wa, ba, wx, bx, lam):
    def body(xp_ref, g_ref, cw_ref, cb_ref, wa_ref, ba_ref, wx_ref, bx_ref, lam_ref, y_ref, h_ref):
        xc, _, _, _ = _conv(xp_ref[...], cw_ref[...], cb_ref[...])
        _, _, ii, _, a, mult = _lru_gates(xc, wa_ref[0], ba_ref[...], wx_ref[0], bx_ref[...], lam_ref[...])
        b = mult * (ii * xc)
        s = 1
        while s < T:
            b = b + a * _sd(b, s, 0.0)
            if 2 * s < T:
                a = a * _sd(a, s, 1.0)
            s *= 2
        h_ref[...] = b
        gel, _ = _gelu_parts(g_ref[...])
        y_ref[...] = (b * gel).astype(BF)

    return pl.pallas_call(
        body, name="fwd_lru", grid=(NG,),
        in_specs=[pl.BlockSpec((T, CB), lambda j: (0, j)),
                  pl.BlockSpec((T, CB), lambda j: (0, NG + j)),
                  pl.BlockSpec((4, CB), lambda j: (0, j)),
                  _vec_spec(),
                  pl.BlockSpec((1, CB, CB), lambda j: (j, 0, 0)), _vec_spec(),
                  pl.BlockSpec((1, CB, CB), lambda j: (j, 0, 0)), _vec_spec(),
                  _vec_spec()],
        out_specs=[pl.BlockSpec((T, CB), lambda j: (0, j)), pl.BlockSpec((T, CB), lambda j: (0, j))],
        out_shape=[jax.ShapeDtypeStruct((T, DR), BF), jax.ShapeDtypeStruct((T, DR), F32)],
        compiler_params=_cp(48),
    )(proj, proj, conv_w, conv_b, wa, ba, wx, bx, lam)


def _pool_cnt(w):
    t = lax.broadcasted_iota(jnp.int32, (T, 1), 0)
    return jnp.minimum(t + 1, w).astype(F32)


def _fwd_pool(proj, pool_w, pool_scale):
    def body(xp_ref, pw_ref, sc_ref, y_ref):
        for g, w in enumerate(POOL_WINDOWS):
            cols = slice(g * PG, (g + 1) * PG)
            x = xp_ref[:, cols]
            p = _pool_window(x, g + 1, _sd) / _pool_cnt(w) - x
            y_ref[:, cols] = (_mm(p, pw_ref[g]) * sc_ref[:, cols]).astype(BF)

    return pl.pallas_call(
        body, name="fwd_pool", grid=(1,),
        in_specs=[pl.BlockSpec((T, DP), lambda i: (0, 2 * DR // DP)),
                  pl.BlockSpec((4, PG, PG), lambda i: (0, 0, 0)),
                  pl.BlockSpec((1, DP), lambda i: (0, 0))],
        out_specs=pl.BlockSpec((T, DP), lambda i: (0, 0)),
        out_shape=jax.ShapeDtypeStruct((T, DP), BF),
        compiler_params=_cp(48),
    )(proj, pool_w, pool_scale)


GATE_BLK = 512
GATE_BLK0 = (2 * DR + DP) // GATE_BLK


def _gate_specs(tm):
    return [pl.BlockSpec((tm, GATE_BLK), functools.partial(lambda i, q: (i, GATE_BLK0 + q), q=q))
            for q in range(4)]


def _fwd_merge(x, ylru, ypool, proj, b_gate, g2, g3, w_lru_up, w_pool_up, w_o):
    tm = 512

    def body(x_ref, yl_ref, yp_ref, p0, p1, p2, p3, bg_ref, g2_ref, g3_ref, wl_ref, wp_ref, wo_ref,
             x2_ref, h2_ref, m_ref, mrg_ref, bra_ref, brb_ref):
        bra = jnp.dot(yl_ref[...], wl_ref[...], preferred_element_type=F32)
        yp = yp_ref[...]
        brb = jnp.concatenate([jnp.dot(yp, wp_ref[k], preferred_element_type=F32) for k in range(NCHIP)], axis=1)
        bg = bg_ref[...]
        ga = _sigmoid(jnp.concatenate([p0[...], p1[...]], axis=1) + bg[:, :D])
        gb = _sigmoid(jnp.concatenate([p2[...], p3[...]], axis=1) + bg[:, D:])
        mrg = (ga * bra + gb * brb).astype(BF)
        m = jnp.dot(mrg, wo_ref[...], preferred_element_type=F32)
        r2 = lax.rsqrt(_mean(m * m) + NORM_EPS)
        x2 = x_ref[...] + (m * r2) * g2_ref[...]
        r3 = lax.rsqrt(_mean(x2 * x2) + NORM_EPS)
        x2_ref[...] = x2
        h2_ref[...] = ((x2 * r3) * g3_ref[...]).astype(BF)
        m_ref[...] = m
        mrg_ref[...] = mrg
        bra_ref[...] = bra.astype(BF)
        brb_ref[...] = brb.astype(BF)

    row = lambda w: pl.BlockSpec((tm, w), lambda i: (i, 0))
    full2 = lambda a, b: pl.BlockSpec((a, b), lambda i: (0, 0))
    return pl.pallas_call(
        body, name="fwd_merge", grid=(T // tm,),
        in_specs=[row(D), row(DR), row(DP)] + _gate_specs(tm) +
                 [full2(1, 2 * D), full2(1, D), full2(1, D), full2(DR, D),
                  pl.BlockSpec((NCHIP, DP, D // NCHIP), lambda i: (0, 0, 0)), full2(D, D)],
        out_specs=[row(D)] * 6,
        out_shape=[jax.ShapeDtypeStruct((T, D), F32), jax.ShapeDtypeStruct((T, D), BF),
                   jax.ShapeDtypeStruct((T, D), F32), jax.ShapeDtypeStruct((T, D), BF),
                   jax.ShapeDtypeStruct((T, D), BF), jax.ShapeDtypeStruct((T, D), BF)],
        compiler_params=_cp(48),
    )(x, ylru, ypool, proj, proj, proj, proj, b_gate, g2, g3, w_lru_up, w_pool_up, w_o)


FC = 256
FPC = (DF // NCHIP) // FC


def _fwd_mlp(h2, w_ff1, w_ff2):
    def body(h_ref, w1_ref, w2_ref, a1_ref, f_ref):
        a1 = jnp.maximum(jnp.dot(h_ref[...], w1_ref[0], preferred_element_type=F32), 0.0)
        a1_ref[...] = a1
        _acc(f_ref, jnp.dot((a1 * a1).astype(BF), w2_ref[...], preferred_element_type=F32),
             pl.program_id(0) == 0)

    return pl.pallas_call(
        body, name="fwd_mlp", grid=(DF // FC,),
        in_specs=[pl.BlockSpec((T, D), lambda j: (0, 0)),
                  pl.BlockSpec((1, D, FC), lambda j: (j // FPC, 0, j % FPC)),
                  pl.BlockSpec((FC, D), lambda j: (j, 0))],
        out_specs=[pl.BlockSpec((T, FC), lambda j: (0, j)), pl.BlockSpec((T, D), lambda j: (0, 0))],
        out_shape=[jax.ShapeDtypeStruct((T, DF), F32), jax.ShapeDtypeStruct((T, D), F32)],
        compiler_params=_cp(48),
    )(h2, w_ff1, w_ff2)


def _loss_head(f, x2, target, g4):
    tm = 512

    def body(f_ref, x2_ref, t_ref, g_ref, loss_ref, dy_ref, df_ref, dg_ref):
        first = pl.program_id(0) == 0
        f = f_ref[...]
        g4v = g_ref[...]
        r4 = lax.rsqrt(_mean(f * f) + NORM_EPS)
        fn = f * r4
        e = (x2_ref[...] + fn * g4v) - t_ref[...]
        _acc(loss_ref, jnp.sum(_mean(e * e), axis=0, keepdims=True), first)
        dy = e * (1.0 / D)
        dy_ref[...] = dy
        _acc(dg_ref, _colsum(dy * fn), first)
        dfn = dy * g4v
        df_ref[...] = (r4 * (dfn - fn * _mean(dfn * fn))).astype(BF)

    row = pl.BlockSpec((tm, D), lambda i: (i, 0))
    return pl.pallas_call(
        body, name="loss_head", grid=(T // tm,),
        in_specs=[row, row, row, pl.BlockSpec((1, D), lambda i: (0, 0))],
        out_specs=[pl.BlockSpec((1, 1), lambda i: (0, 0)), row, row, pl.BlockSpec((1, D), lambda i: (0, 0))],
        out_shape=[jax.ShapeDtypeStruct((1, 1), F32), jax.ShapeDtypeStruct((T, D), F32),
                   jax.ShapeDtypeStruct((T, D), BF), jax.ShapeDtypeStruct((1, D), F32)],
        compiler_params=_cp(48),
    )(f, x2, target, g4)


def _bwd_mlp(df, h2, a1, w_ff1, w_ff2):
    def body(df_ref, h_ref, a1_ref, w1_ref, w2_ref, dw1_ref, dw2_ref, dh_ref):
        df = df_ref[...]
        a1 = a1_ref[...]
        dact = _mm_nt(df, w2_ref[...])
        df1 = (dact * (2.0 * a1)).astype(BF)
        dw2_ref[...] = _mm_tn((a1 * a1).astype(BF), df)
        dw1_ref[0] = _mm_tn(h_ref[...], df1)
        _acc(dh_ref, _mm_nt(df1, w1_ref[0]), pl.program_id(0) == 0)

    return pl.pallas_call(
        body, name="bwd_mlp", grid=(DF // FC,),
        in_specs=[pl.BlockSpec((T, D), lambda j: (0, 0)),
                  pl.BlockSpec((T, D), lambda j: (0, 0)),
                  pl.BlockSpec((T, FC), lambda j: (0, j)),
                  pl.BlockSpec((1, D, FC), lambda j: (j // FPC, 0, j % FPC)),
                  pl.BlockSpec((FC, D), lambda j: (j, 0))],
        out_specs=[pl.BlockSpec((1, D, FC), lambda j: (j // FPC, 0, j % FPC)),
                   pl.BlockSpec((FC, D), lambda j: (j, 0)),
                   pl.BlockSpec((T, D), lambda j: (0, 0))],
        out_shape=[jax.ShapeDtypeStruct((NCHIP, D, DF // NCHIP), F32),
                   jax.ShapeDtypeStruct((DF, D), F32), jax.ShapeDtypeStruct((T, D), F32)],
        compiler_params=_cp(56),
    )(df, h2, a1, w_ff1, w_ff2)


def _bwd_merge(dh2, dy, x2, m, mrg, bra, brb, proj, b_gate, ylru, ypool, g2, g3, w_lru_up, w_pool_up, w_o):
    tm = 256
    cpu = D // NCHIP

    def body(dh2_ref, dy_ref, x2_ref, m_ref, mrg_ref, bra_ref, brb_ref, p0, p1, p2, p3, bg_ref, yl_ref, yp_ref,
             g2_ref, g3_ref, wl_ref, wp_ref, wo_ref,
             dx_ref, dgt_ref, dyl_ref, dyp_ref, dwo_ref, dwl_ref, dwp_ref, dg2_ref, dg3_ref, dbg_ref):
        first = pl.program_id(0) == 0
        x2 = x2_ref[...]
        r3 = lax.rsqrt(_mean(x2 * x2) + NORM_EPS)
        x2n = x2 * r3
        dh2 = dh2_ref[...]
        t3 = dh2 * g3_ref[...]
        dx2 = dy_ref[...] + r3 * (t3 - x2n * _mean(t3 * x2n))
        dx_ref[...] = dx2
        _acc(dg3_ref, _colsum(dh2 * x2n), first)
        m = m_ref[...]
        r2 = lax.rsqrt(_mean(m * m) + NORM_EPS)
        mn = m * r2
        _acc(dg2_ref, _colsum(dx2 * mn), first)
        dmn = dx2 * g2_ref[...]
        dm = (r2 * (dmn - mn * _mean(dmn * mn))).astype(BF)
        dmrg = _mm_nt(dm, wo_ref[...])
        _acc(dwo_ref, _mm_tn(mrg_ref[...], dm), first)
        bg = bg_ref[...]
        ga = _sigmoid(jnp.concatenate([p0[...], p1[...]], axis=1) + bg[:, :D])
        gb = _sigmoid(jnp.concatenate([p2[...], p3[...]], axis=1) + bg[:, D:])
        dga = dmrg * bra_ref[...].astype(F32) * (ga * (1.0 - ga))
        dgb = dmrg * brb_ref[...].astype(F32) * (gb * (1.0 - gb))
        dgt_ref[:, :D] = dga.astype(BF)
        dgt_ref[:, D:] = dgb.astype(BF)
        _acc(dbg_ref, jnp.concatenate([_colsum(dga), _colsum(dgb)], axis=1), first)
        dbra = (dmrg * ga).astype(BF)
        dbrb = (dmrg * gb).astype(BF)
        dyl_ref[...] = _mm_nt(dbra, wl_ref[...])
        _acc(dwl_ref, _mm_tn(yl_ref[...], dbra), first)
        yp = yp_ref[...]
        dyp = None
        for k in range(NCHIP):
            dk = dbrb[:, k * cpu:(k + 1) * cpu]
            part = _mm_nt(dk, wp_ref[k])
            dyp = part if dyp is None else dyp + part
            _acc(dwp_ref.at[k], _mm_tn(yp, dk), first)
        dyp_ref[...] = dyp

    row = lambda w: pl.BlockSpec((tm, w), lambda i: (i, 0))
    full2 = lambda a, b: pl.BlockSpec((a, b), lambda i: (0, 0))
    wp_spec = pl.BlockSpec((NCHIP, DP, cpu), lambda i: (0, 0, 0))
    return pl.pallas_call(
        body, name="bwd_merge", grid=(T // tm,),
        in_specs=[row(D)] * 7 + _gate_specs(tm) +
                 [full2(1, 2 * D), row(DR), row(DP), full2(1, D), full2(1, D), full2(DR, D), wp_spec, full2(D, D)],
        out_specs=[row(D), row(2 * D), row(DR), row(DP), full2(D, D), full2(DR, D), wp_spec,
                   full2(1, D), full2(1, D), full2(1, 2 * D)],
        out_shape=[jax.ShapeDtypeStruct((T, D), F32), jax.ShapeDtypeStruct((T, 2 * D), BF),
                   jax.ShapeDtypeStruct((T, DR), F32), jax.ShapeDtypeStruct((T, DP), F32),
                   jax.ShapeDtypeStruct((D, D), F32), jax.ShapeDtypeStruct((DR, D), F32),
                   jax.ShapeDtypeStruct((NCHIP, DP, cpu), F32),
                   jax.ShapeDtypeStruct((1, D), F32), jax.ShapeDtypeStruct((1, D), F32),
                   jax.ShapeDtypeStruct((1, 2 * D), F32)],
        compiler_params=_cp(56),
    )(dh2, dy, x2, m, mrg, bra, brb, proj, proj, proj, proj, b_gate, ylru, ypool, g2, g3, w_lru_up, w_pool_up, w_o)


def _bwd_lru(proj, h, dylru, conv_w, conv_b, wa, ba, wx, bx, lam):
    def body(xp_ref, g_ref, h_ref, dy_ref, cw_ref, cb_ref, wa_ref, ba_ref, wx_ref, bx_ref, lam_ref,
             dxp_ref, dg_ref, dcw_ref, dcb_ref, dwa_ref, dba_ref, dwx_ref, dbx_ref, dlam_ref):
        xp = xp_ref[...]
        cw = cw_ref[...]
        lam = lam_ref[...]
        xc, x1, x2, x3 = _conv(xp, cw, cb_ref[...])
        wa, wx = wa_ref[0], wx_ref[0]
        xcb, r, ii, sp, a, mult = _lru_gates(xc, wa, ba_ref[...], wx, bx_ref[...], lam)
        g = g_ref[...]
        gel, dgel = _gelu_parts(g)
        h = h_ref[...]
        dy = dy_ref[...]
        dg_ref[...] = (dy * h * dgel).astype(BF)
        b = dy * gel
        aa = _su(a, 1, 0.0)
        s = 1
        while s < T:
            b = b + aa * _su(b, s, 0.0)
            if 2 * s < T:
                aa = aa * _su(aa, s, 0.0)
            s *= 2
        da = b * _sd(h, 1, 0.0)
        dmult = b * (ii * xc)
        dii = b * (mult * xc)
        dxc = b * (mult * ii)
        dla = da * a - dmult * ((a * a) / mult)
        dr = dla * ((-LRU_C) * sp)
        dsp = _colsum(dla * ((-LRU_C) * r))
        dlam_ref[...] = -dsp / (1.0 + jnp.exp(lam))
        dzr = dr * (r * (1.0 - r))
        dzi = dii * (ii * (1.0 - ii))
        dzrb, dzib = dzr.astype(BF), dzi.astype(BF)
        dxc = dxc + _mm_nt(dzrb, wa) + _mm_nt(dzib, wx)
        dwa_ref[0] = _mm_tn(xcb, dzrb)
        dwx_ref[0] = _mm_tn(xcb, dzib)
        dba_ref[...] = _colsum(dzr)
        dbx_ref[...] = _colsum(dzi)
        dcb_ref[...] = _colsum(dxc)
        dcw_ref[...] = jnp.concatenate([_colsum(dxc * x3), _colsum(dxc * x2), _colsum(dxc * x1),
                                        _colsum(dxc * xp)], axis=0)
        dxp = cw[3:4] * dxc + cw[2:3] * _su(dxc, 1) + cw[1:2] * _su(dxc, 2) + cw[0:1] * _su(dxc, 3)
        dxp_ref[...] = dxp.astype(BF)

    blk = pl.BlockSpec((T, CB), lambda j: (0, j))
    wsp = pl.BlockSpec((1, CB, CB), lambda j: (j, 0, 0))
    return pl.pallas_call(
        body, name="bwd_lru", grid=(NG,),
        in_specs=[blk, pl.BlockSpec((T, CB), lambda j: (0, NG + j)), blk, blk,
                  pl.BlockSpec((4, CB), lambda j: (0, j)), _vec_spec(), wsp, _vec_spec(), wsp, _vec_spec(),
                  _vec_spec()],
        out_specs=[blk, blk, pl.BlockSpec((4, CB), lambda j: (0, j)), _vec_spec(), wsp, _vec_spec(), wsp,
                   _vec_spec(), _vec_spec()],
        out_shape=[jax.ShapeDtypeStruct((T, DR), BF), jax.ShapeDtypeStruct((T, DR), BF),
                   jax.ShapeDtypeStruct((4, DR), F32), jax.ShapeDtypeStruct((1, DR), F32),
                   jax.ShapeDtypeStruct((NG, CB, CB), F32), jax.ShapeDtypeStruct((1, DR), F32),
                   jax.ShapeDtypeStruct((NG, CB, CB), F32), jax.ShapeDtypeStruct((1, DR), F32),
                   jax.ShapeDtypeStruct((1, DR), F32)],
        compiler_params=_cp(56),
    )(proj, proj, h, dylru, conv_w, conv_b, wa, ba, wx, bx, lam)


def _bwd_pool(proj, dypool, pool_w, pool_scale):
    def body(xp_ref, dy_ref, pw_ref, sc_ref, dx_ref, dw_ref, dsc_ref):
        for g, w in enumerate(POOL_WINDOWS):
            cols = slice(g * PG, (g + 1) * PG)
            cnt = _pool_cnt(w)
            x = xp_ref[:, cols]
            pb = (_pool_window(x, g + 1, _sd) / cnt - x).astype(BF)
            wg = pw_ref[g]
            dy = dy_ref[:, cols]
            dsc_ref[:, cols] = _colsum(dy * _mm(pb, wg))
            dyp = (dy * sc_ref[:, cols]).astype(BF)
            dw_ref[g] = _mm_tn(pb, dyp)
            dp = _mm_nt(dyp, wg)
            dx_ref[:, cols] = (_pool_window(dp / cnt, g + 1, _su) - dp).astype(BF)

    return pl.pallas_call(
        body, name="bwd_pool", grid=(1,),
        in_specs=[pl.BlockSpec((T, DP), lambda i: (0, 2 * DR // DP)),
                  pl.BlockSpec((T, DP), lambda i: (0, 0)),
                  pl.BlockSpec((4, PG, PG), lambda i: (0, 0, 0)),
                  pl.BlockSpec((1, DP), lambda i: (0, 0))],
        out_specs=[pl.BlockSpec((T, DP), lambda i: (0, 0)),
                   pl.BlockSpec((4, PG, PG), lambda i: (0, 0, 0)),
                   pl.BlockSpec((1, DP), lambda i: (0, 0))],
        out_shape=[jax.ShapeDtypeStruct((T, DP), BF), jax.ShapeDtypeStruct((4, PG, PG), F32),
                   jax.ShapeDtypeStruct((1, DP), F32)],
        compiler_params=_cp(48),
    )(proj, dypool, pool_w, pool_scale)


def _bwd_inproj(h1, dproj, w_in):
    def body(h_ref, dp_ref, w_ref, dw_ref, dh_ref):
        dp = dp_ref[...]
        dw_ref[0] = _mm_tn(h_ref[...], dp)
        _acc(dh_ref, _mm_nt(dp, w_ref[0]), pl.program_id(0) == 0)

    return pl.pallas_call(
        body, name="bwd_inproj", grid=(NCHIP,),
        in_specs=[pl.BlockSpec((T, D), lambda k: (0, 0)),
                  pl.BlockSpec((T, CW_IN), lambda k: (0, k)),
                  pl.BlockSpec((1, D, CW_IN), lambda k: (k, 0, 0))],
        out_specs=[pl.BlockSpec((1, D, CW_IN), lambda k: (k, 0, 0)), pl.BlockSpec((T, D), lambda k: (0, 0))],
        out_shape=[jax.ShapeDtypeStruct((NCHIP, D, CW_IN), F32), jax.ShapeDtypeStruct((T, D), F32)],
        compiler_params=_cp(56),
    )(h1, dproj, w_in)


def _bwd_prenorm(x, dh1, dxres, g1):
    tm = 512

    def body(x_ref, dh_ref, dr_ref, g_ref, dx_ref, dg_ref):
        xv = x_ref[...]
        r = lax.rsqrt(_mean(xv * xv) + NORM_EPS)
        xn = xv * r
        dh = dh_ref[...]
        t = dh * g_ref[...]
        dx_ref[...] = dr_ref[...] + r * (t - xn * _mean(t * xn))
        _acc(dg_ref, _colsum(dh * xn), pl.program_id(0) == 0)

    row = pl.BlockSpec((tm, D), lambda i: (i, 0))
    vec = pl.BlockSpec((1, D), lambda i: (0, 0))
    return pl.pallas_call(
        body, name="bwd_prenorm", grid=(T // tm,),
        in_specs=[row, row, row, vec], out_specs=[row, vec],
        out_shape=[jax.ShapeDtypeStruct((T, D), F32), jax.ShapeDtypeStruct((1, D), F32)],
        compiler_params=_cp(48),
    )(x, dh1, dxres, g1)


def _block_diag(w):
    hd = w.shape[-1]
    per = CB // hd
    w4 = w.reshape(NG, per, hd, hd)
    eye = jnp.eye(per, dtype=w.dtype)
    return jnp.einsum("gpij,pq->gpiqj", w4, eye).reshape(NG, CB, CB)


def _block_diag_extract(d, hd):
    per = CB // hd
    d5 = d.reshape(NG, per, hd, per, hd)
    return jnp.stack([d5[:, p, :, p, :] for p in range(per)], axis=1).reshape(NG * per, hd, hd)


def _local_step(x, target, g1, g2, g3, g4, w_in, b_gate, conv_w, conv_b, lru_w_a, lru_b_a, lru_w_x, lru_b_x,
                lru_lambda, pool_w, pool_scale, w_lru_up, w_pool_up, w_o, w_ff1, w_ff2):
    hd = lru_w_a.shape[-1]
    wa = _block_diag(lru_w_a).astype(BF)
    wx = _block_diag(lru_w_x).astype(BF)
    pw = pool_w.astype(BF)
    w_lru_up2 = w_lru_up.reshape(DR, D)
    w_o2 = w_o.reshape(D, D)
    w_ff22 = w_ff2.reshape(DF, D)

    proj, h1 = _fwd_inproj(x, g1, w_in)
    ylru, hs = _fwd_lru(proj, conv_w, conv_b, wa, lru_b_a, wx, lru_b_x, lru_lambda)
    ypool = _fwd_pool(proj, pw, pool_scale)
    x2, h2, m, mrg, bra, brb = _fwd_merge(x, ylru, ypool, proj, b_gate, g2, g3, w_lru_up2, w_pool_up, w_o2)
    a1, f = _fwd_mlp(h2, w_ff1, w_ff22)
    lossp, dy, df, dg4 = _loss_head(f, x2, target, g4)

    dw_ff1, dw_ff2, dh2 = _bwd_mlp(df, h2, a1, w_ff1, w_ff22)
    (dxres, dgates, dylru, dypool, dw_o, dw_lru_up, dw_pool_up, dg2, dg3, dbg) = _bwd_merge(
        dh2, dy, x2, m, mrg, bra, brb, proj, b_gate, ylru, ypool, g2, g3, w_lru_up2, w_pool_up, w_o2)
    dxp, dgl, dcw, dcb, dwa, dba, dwx, dbx, dlam = _bwd_lru(proj, hs, dylru, conv_w, conv_b, wa, lru_b_a, wx,
                                                             lru_b_x, lru_lambda)
    dxpool, dpw, dsc = _bwd_pool(proj, dypool, pw, pool_scale)
    dproj = jnp.concatenate([dxp, dgl, dxpool, dgates], axis=1)
    dw_in, dh1 = _bwd_inproj(h1, dproj, w_in)
    grad_x, dg1 = _bwd_prenorm(x, dh1, dxres, g1)

    small = jnp.concatenate([
        dg1, dg2, dg3, dg4, dbg.reshape(2, D), dcb, dba, dbx, dlam,
        jnp.pad(dsc, ((0, 0), (0, D - DP))), jnp.pad(lossp, ((0, 0), (0, D - 1))), dcw,
        _block_diag_extract(dwa, hd).reshape(-1, D), _block_diag_extract(dwx, hd).reshape(-1, D),
        dpw.reshape(-1, D)], axis=0)
    big = [dw_in, dw_lru_up.reshape(NCHIP, DR // NCHIP, D), dw_pool_up, dw_o.reshape(NCHIP, D // NCHIP, D),
           dw_ff1, dw_ff2.reshape(NCHIP, DF // NCHIP, D)]
    return lossp, grad_x, big, small


def _place():
    x, y, c = lax.axis_index("x"), lax.axis_index("y"), lax.axis_index("c")
    chips = [(1 - x, y), (x, 1 - y), (1 - x, 1 - y)]
    return x, y, c, chips


def _rcopy(src, dst, ssem, rsem, dev):
    return pltpu.make_async_remote_copy(src_ref=src, dst_ref=dst, send_sem=ssem, recv_sem=rsem,
                                        device_id=dev, device_id_type=MESH_ID)


def _cast_place(w, chip_idx, name):
    rows, cols = w.shape
    br = _row_block(rows, cols)

    def body(k_ref, w_ref, o_ref):
        o_ref[0] = w_ref[...].astype(BF)

    return pl.pallas_call(
        body, name=name,
        grid_spec=pltpu.PrefetchScalarGridSpec(
            num_scalar_prefetch=1, grid=(rows // br,),
            in_specs=[pl.BlockSpec((br, cols), lambda r, k: (r, 0))],
            out_specs=pl.BlockSpec((1, br, cols), lambda r, k: (k[0], r, 0))),
        out_shape=jax.ShapeDtypeStruct((NCHIP, rows, cols), BF),
        compiler_params=_cp(32),
    )(chip_idx, w)


def _all_gather(fulls, conv_w):
    n = len(fulls)

    def body(*refs):
        cw_in = refs[n]
        outs, cw_out = refs[n + 1:2 * n + 1], refs[2 * n + 1]
        ssem, rsem, cssem, crsem = refs[2 * n + 2:]
        x, y, c, chips = _place()
        me = 2 * x + y
        sib = (x, y, 1 - c)

        def part(i, k, half):
            hr = fulls[i].shape[1] // 2
            return outs[i].at[k, pl.ds(half * hr, hr), :]

        sends = []
        for i in range(n):
            for j, chip in enumerate(chips):
                sends.append(_rcopy(part(i, me, c), part(i, me, c), ssem.at[i, j], rsem.at[i, j], (*chip, c)))
        for j, chip in enumerate(chips):
            sends.append(_rcopy(cw_in, cw_out.at[me], cssem.at[j], crsem.at[j], (*chip, c)))
        for cp in sends:
            cp.start()
        passed = []
        for i in range(n):
            for j, chip in enumerate(chips):
                k = 2 * chip[0] + chip[1]
                got = part(i, k, c)
                _rcopy(got, got, ssem.at[i, j], rsem.at[i, j], (*chip, c)).wait_recv()
                fwd = _rcopy(got, got, ssem.at[i, 3 + j], rsem.at[i, 3 + j], sib)
                fwd.start()
                passed.append(fwd)
        for i in range(n):
            for j, chip in enumerate(chips):
                k = 2 * chip[0] + chip[1]
                got = part(i, k, 1 - c)
                _rcopy(got, got, ssem.at[i, 3 + j], rsem.at[i, 3 + j], sib).wait_recv()
        for j, chip in enumerate(chips):
            k = 2 * chip[0] + chip[1]
            _rcopy(cw_in, cw_out.at[k], cssem.at[j], crsem.at[j], (*chip, c)).wait_recv()
        for cp in sends + passed:
            cp.wait_send()

    out_shape = [jax.ShapeDtypeStruct(f.shape, f.dtype) for f in fulls]
    out_shape.append(jax.ShapeDtypeStruct((NCHIP,) + conv_w.shape, conv_w.dtype))
    return pl.pallas_call(
        body, name="gather_weights",
        in_specs=[ANY] * (n + 1), out_specs=[ANY] * (n + 1), out_shape=out_shape,
        input_output_aliases={i: i for i in range(n)},
        scratch_shapes=[pltpu.SemaphoreType.DMA((n, 6)), pltpu.SemaphoreType.DMA((n, 6)),
                        pltpu.SemaphoreType.DMA((3,)), pltpu.SemaphoreType.DMA((3,))],
        compiler_params=pltpu.CompilerParams(has_side_effects=True),
    )(*fulls, conv_w)


def _sibling_exchange(big, small):
    n = len(big)

    def body(*refs):
        srcs, lands = refs[:n + 1], refs[n + 1:2 * n + 2]
        ssem, rsem = refs[2 * n + 2:]
        x, y, c, _ = _place()
        sib = (x, y, 1 - c)
        cps = [_rcopy(srcs[i].at[:, 1 - c], lands[i], ssem.at[i], rsem.at[i], sib) for i in range(n)]
        cps.append(_rcopy(srcs[n], lands[n], ssem.at[n], rsem.at[n], sib))
        for cp in cps:
            cp.start()
        for cp in cps:
            cp.wait()

    out_shape = [jax.ShapeDtypeStruct((NCHIP,) + g.shape[2:], g.dtype) for g in big]
    out_shape.append(jax.ShapeDtypeStruct(small.shape, small.dtype))
    return pl.pallas_call(
        body, name="reduce_sibling",
        in_specs=[ANY] * (n + 1), out_specs=[ANY] * (n + 1), out_shape=out_shape,
        scratch_shapes=[pltpu.SemaphoreType.DMA((n + 1,)), pltpu.SemaphoreType.DMA((n + 1,))],
        compiler_params=pltpu.CompilerParams(has_side_effects=True),
    )(*big, small)


def _chip_exchange(parts, small):
    n = len(parts)

    def body(*refs):
        srcs, lands = refs[:n + 1], refs[n + 1:2 * n + 2]
        ssem, rsem = refs[2 * n + 2:]
        x, y, c, chips = _place()
        me = 2 * x + y
        cps = []
        for i in range(n):
            for j, chip in enumerate(chips):
                k = 2 * chip[0] + chip[1]
                cps.append(_rcopy(srcs[i].at[k], lands[i].at[me], ssem.at[i, j], rsem.at[i, j], (*chip, c)))
        for j, chip in enumerate(chips):
            cps.append(_rcopy(srcs[n].at[c], lands[n].at[me], ssem.at[n, j], rsem.at[n, j], (*chip, c)))
        for cp in cps:
            cp.start()
        for cp in cps:
            cp.wait()

    out_shape = [jax.ShapeDtypeStruct(p.shape, p.dtype) for p in parts]
    out_shape.append(jax.ShapeDtypeStruct((NCHIP,) + small.shape[1:], small.dtype))
    return pl.pallas_call(
        body, name="reduce_chips",
        in_specs=[ANY] * (n + 1), out_specs=[ANY] * (n + 1), out_shape=out_shape,
        scratch_shapes=[pltpu.SemaphoreType.DMA((n + 1, 3)), pltpu.SemaphoreType.DMA((n + 1, 3))],
        compiler_params=pltpu.CompilerParams(has_side_effects=True),
    )(*parts, small)


def _sibling_share(pairs):
    n = len(pairs)

    def body(*refs):
        outs = refs[n:2 * n]
        ssem, rsem = refs[2 * n:]
        x, y, c, _ = _place()
        sib = (x, y, 1 - c)
        cps = [_rcopy(outs[i].at[c], outs[i].at[c], ssem.at[i], rsem.at[i], sib) for i in range(n)]
        for cp in cps:
            cp.start()
        for i in range(n):
            cps[i].wait_send()
            _rcopy(outs[i].at[1 - c], outs[i].at[1 - c], ssem.at[i], rsem.at[i], sib).wait_recv()

    out_shape = [jax.ShapeDtypeStruct(p.shape, p.dtype) for p in pairs]
    return pl.pallas_call(
        body, name="reduce_share",
        in_specs=[ANY] * n, out_specs=[ANY] * n, out_shape=out_shape,
        input_output_aliases={i: i for i in range(n)},
        scratch_shapes=[pltpu.SemaphoreType.DMA((n,)), pltpu.SemaphoreType.DMA((n,))],
        compiler_params=pltpu.CompilerParams(has_side_effects=True),
    )(*pairs)


def _row_block(rows, cols, itemsize=4, target=MIB):
    br = rows
    while br * cols * itemsize > target and br % 16 == 0:
        br //= 2
    return br


def _add_sibling(g, land, cidx, out_dtype, name):
    _, _, hr, cols = g.shape
    br = _row_block(hr, cols)

    def body(c_ref, g_ref, l_ref, o_ref):
        o_ref[...] = (g_ref[0, 0] + l_ref[0]).astype(out_dtype)[None]

    return pl.pallas_call(
        body, name=name,
        grid_spec=pltpu.PrefetchScalarGridSpec(
            num_scalar_prefetch=1, grid=(NCHIP, hr // br),
            in_specs=[pl.BlockSpec((1, 1, br, cols), lambda k, r, c: (k, c[0], r, 0)),
                      pl.BlockSpec((1, br, cols), lambda k, r, c: (k, r, 0))],
            out_specs=pl.BlockSpec((1, br, cols), lambda k, r, c: (k, r, 0))),
        out_shape=jax.ShapeDtypeStruct((NCHIP, hr, cols), out_dtype),
        compiler_params=_cp(32),
    )(cidx, g, land)


def _add_pair(a, b, name):
    rows, cols = a.shape
    br = _row_block(rows, cols)

    def body(a_ref, b_ref, o_ref):
        o_ref[...] = a_ref[...] + b_ref[...]

    spec = pl.BlockSpec((br, cols), lambda r: (r, 0))
    return pl.pallas_call(body, name=name, grid=(rows // br,), in_specs=[spec, spec], out_specs=spec,
                          out_shape=jax.ShapeDtypeStruct(a.shape, a.dtype), compiler_params=_cp(32))(a, b)


def _add_chips(own, land, idx, name):
    _, hr, cols = land.shape
    br = _row_block(hr, cols)

    def body(s_ref, a_ref, b_ref, c_ref, d_ref, o_ref):
        o_ref[...] = (a_ref[...].astype(F32) + b_ref[...].astype(F32)) + (c_ref[...].astype(F32) +
                                                                           d_ref[...].astype(F32))

    spec = lambda q: pl.BlockSpec((1, br, cols), functools.partial(lambda r, s, q: (s[q], r, 0), q=q))
    return pl.pallas_call(
        body, name=name,
        grid_spec=pltpu.PrefetchScalarGridSpec(
            num_scalar_prefetch=1, grid=(hr // br,),
            in_specs=[spec(0), spec(1), spec(2), spec(3)], out_specs=spec(4)),
        out_shape=jax.ShapeDtypeStruct((2, hr, cols), F32),
        compiler_params=_cp(32),
    )(idx, own, land, land, land)


def _adamw(w, g, m, v, name):
    rows, cols = w.shape
    br = _row_block(rows, cols)

    def body(w_ref, g_ref, m_ref, v_ref, d_ref, mo_ref, vo_ref):
        gv = g_ref[...]
        mn = ADAM_B1 * m_ref[...] + (1.0 - ADAM_B1) * gv
        vn = ADAM_B2 * v_ref[...] + (1.0 - ADAM_B2) * (gv * gv)
        m_hat = mn / (1.0 - ADAM_B1 ** ADAM_STEP)
        v_hat = vn / (1.0 - ADAM_B2 ** ADAM_STEP)
        d_ref[...] = -ADAM_LR * (m_hat / (jnp.sqrt(v_hat) + ADAM_EPS) + ADAM_WD * w_ref[...])
        mo_ref[...] = mn
        vo_ref[...] = vn

    spec = pl.BlockSpec((br, cols), lambda r: (r, 0))
    shp = jax.ShapeDtypeStruct(w.shape, F32)
    return pl.pallas_call(body, name=name, grid=(rows // br,), in_specs=[spec] * 4, out_specs=[spec] * 3,
                          out_shape=[shp, shp, shp], compiler_params=_cp(32))(w, g, m, v)


W_NAMES = ["norm_mix_pre", "norm_mix_post", "norm_mlp_pre", "norm_mlp_post", "w_in", "b_gate", "conv_w", "conv_b",
           "lru_w_a", "lru_b_a", "lru_w_x", "lru_b_x", "lru_lambda", "pool_w", "pool_scale", "w_lru_up",
           "w_pool_up", "w_o", "w_ff1", "w_ff2"]
BIG = ["w_in", "w_lru_up", "w_pool_up", "w_o", "w_ff1", "w_ff2"]


def _unpack_small(s, hd, chip):
    cw_full = s[12:16]
    ccols = D // NCHIP
    return {
        "norm_mix_pre": s[0:1], "norm_mix_post": s[1:2], "norm_mlp_pre": s[2:3], "norm_mlp_post": s[3:4],
        "b_gate": s[4:6].reshape(1, 2 * D), "conv_b": s[6:7], "lru_b_a": s[7:8], "lru_b_x": s[8:9],
        "lru_lambda": s[9:10], "pool_scale": s[10:11, :DP],
        "conv_w": lax.dynamic_slice(cw_full, (0, chip * ccols), (4, ccols)),
        "lru_w_a": s[16:80].reshape(1, -1, hd, hd), "lru_w_x": s[80:144].reshape(1, -1, hd, hd),
        "pool_w": s[144:208].reshape(1, len(POOL_WINDOWS), PG, PG),
    }


def kernel(x, norm_mix_pre, norm_mix_post, norm_mlp_pre, norm_mlp_post, w_in, b_gate, conv_w, conv_b, lru_w_a, lru_b_a, lru_w_x, lru_b_x, lru_lambda, pool_w, pool_scale, w_lru_up, w_pool_up, w_o, w_ff1, w_ff2, loss_target, m_norm_mix_pre, m_norm_mix_post, m_norm_mlp_pre, m_norm_mlp_post, m_w_in, m_b_gate, m_conv_w, m_conv_b, m_lru_w_a, m_lru_b_a, m_lru_w_x, m_lru_b_x, m_lru_lambda, m_pool_w, m_pool_scale, m_w_lru_up, m_w_pool_up, m_w_o, m_w_ff1, m_w_ff2, v_norm_mix_pre, v_norm_mix_post, v_norm_mlp_pre, v_norm_mlp_post, v_w_in, v_b_gate, v_conv_w, v_conv_b, v_lru_w_a, v_lru_b_a, v_lru_w_x, v_lru_b_x, v_lru_lambda, v_pool_w, v_pool_scale, v_w_lru_up, v_w_pool_up, v_w_o, v_w_ff1, v_w_ff2):
    args = dict(locals())
    w = {n: args[n] for n in W_NAMES}
    mom = {n: args["m_" + n] for n in W_NAMES}
    var = {n: args["v_" + n] for n in W_NAMES}
    i32 = lambda v: jnp.asarray(v, jnp.int32)
    chip = i32(2 * lax.axis_index("x") + lax.axis_index("y"))
    core = i32(lax.axis_index("c"))
    cidx = core.reshape(1)
    hd = lru_w_a.shape[-1]

    placed = [_cast_place(w[n][0], chip.reshape(1), "cast_" + n) for n in BIG]
    gathered = _all_gather(placed, conv_w[0])
    full = dict(zip(BIG, gathered[:-1]))
    conv_w_all = lax.dynamic_update_slice(gathered[-1], conv_w, (chip, i32(0), i32(0)))
    conv_w_full = jnp.transpose(conv_w_all, (1, 0, 2)).reshape(4, DR)

    lossp, grad_x, big, small = _local_step(
        x[0], loss_target[0], norm_mix_pre, norm_mix_post, norm_mlp_pre, norm_mlp_post, full["w_in"], b_gate,
        conv_w_full, conv_b, lru_w_a[0], lru_b_a, lru_w_x[0], lru_b_x, lru_lambda, pool_w[0], pool_scale,
        full["w_lru_up"], full["w_pool_up"], full["w_o"], full["w_ff1"], full["w_ff2"])

    big5 = [g.reshape(NCHIP, 2, g.shape[1] // 2, g.shape[2]) for g in big]
    lands = _sibling_exchange(big5, small)
    parts = [_add_sibling(g, l, cidx, BF, "add_sibling_" + n) for g, l, n in zip(big5, lands[:-1], BIG)]
    small2 = _add_pair(small, lands[-1], "add_sibling_small").reshape(2, SMALL_ROWS // 2, D)
    lands2 = _chip_exchange(parts, small2)
    idx_big = jnp.stack([chip, (chip + 1) % NCHIP, (chip + 2) % NCHIP, (chip + 3) % NCHIP, core])
    pairs = [_add_chips(p, l, idx_big, "add_chips_" + n) for p, l, n in zip(parts, lands2[:-1], BIG)]
    own_small = lax.dynamic_index_in_dim(small2, core, 0, keepdims=True)
    land_small = lax.dynamic_update_slice(lands2[-1], own_small, (chip, i32(0), i32(0)))
    idx_small = jnp.stack([i32(0), i32(1), i32(2), i32(3), core])
    pairs.append(_add_chips(land_small, land_small, idx_small, "add_chips_small"))
    shared = _sibling_share(pairs)
    grads = {n: s.reshape(1, -1, s.shape[-1]) for n, s in zip(BIG, shared[:-1])}
    small_sum = shared[-1].reshape(SMALL_ROWS, D)
    grads.update(_unpack_small(small_sum, hd, chip))
    loss = 0.5 * small_sum[LOSS_ROW, 0]

    delta, new_m, new_v = {}, {}, {}
    for n in W_NAMES:
        shp = w[n].shape
        cols = shp[-1]
        as2 = lambda a: a.reshape(-1, cols)
        d, mo, vo = _adamw(as2(w[n]), as2(grads[n]), as2(mom[n]), as2(var[n]), "adamw_" + n)
        grads[n] = grads[n].reshape(shp)
        delta[n], new_m[n], new_v[n] = d.reshape(shp), mo.reshape(shp), vo.reshape(shp)

    return (loss, grad_x[None], *[grads[n] for n in W_NAMES], *[delta[n] for n in W_NAMES],
            *[new_m[n] for n in W_NAMES], *[new_v[n] for n in W_NAMES])
```

```python
import functools
import math

import jax
import jax.numpy as jnp
from jax import lax
from jax.experimental import pallas as pl
from jax.experimental.pallas import tpu as pltpu

F32 = jnp.float32
BF = jnp.bfloat16

T = 2048
D = 1024
DR = 1024
DP = 512
DF = 4096
DIN = 4608
NCHIP = 4
CW_IN = DIN // NCHIP
LANE = 128
CB = 128
NG = DR // CB
PG = 128
POOL_WINDOWS = (2, 4, 8, 16)
NORM_EPS = 1e-6
LRU_C = 8.0
GELU_C = math.sqrt(2.0 / math.pi)
ADAM_LR = 0.001
ADAM_B1 = 0.9
ADAM_B2 = 0.999
ADAM_EPS = 1e-08
ADAM_WD = 0.01
ADAM_STEP = 10
MESH_ID = pl.DeviceIdType.MESH
ANY = pl.BlockSpec(memory_space=pl.ANY)
SMALL_ROWS = 208
LOSS_ROW = 11
MIB = 1 << 20


def _cp(vmem_mib=None):
    if vmem_mib is None:
        return pltpu.CompilerParams()
    return pltpu.CompilerParams(vmem_limit_bytes=vmem_mib * MIB)


class _Stage:
    def __init__(self, operands, out_shape, alias, sems, start, finish):
        self.operands, self.out_shape, self.alias, self.sems = list(operands), list(out_shape), dict(alias), list(sems)
        self.start, self.finish = start, finish


def _call(body, *, name, grid, in_specs, out_specs, out_shape, args, vmem=None, stages=(), prefetch=None):
    nin, nout = len(in_specs), len(out_specs)
    npre = 0 if prefetch is None else 1
    st_args, st_shapes, st_sems, aliases = [], [], [], {}
    for st in stages:
        for k, v in st.alias.items():
            aliases[npre + nin + len(st_args) + k] = nout + len(st_shapes) + v
        st_args += st.operands
        st_shapes += st.out_shape
        st_sems += st.sems

    def wrapped(*refs):
        pre, refs = refs[:npre], refs[npre:]
        ins, pos = refs[:nin], nin
        st_ins = []
        for st in stages:
            st_ins.append(refs[pos:pos + len(st.operands)])
            pos += len(st.operands)
        outs, pos = refs[pos:pos + nout], pos + nout
        st_outs = []
        for st in stages:
            st_outs.append(refs[pos:pos + len(st.out_shape)])
            pos += len(st.out_shape)
        sems = []
        for st in stages:
            sems.append(refs[pos:pos + len(st.sems)])
            pos += len(st.sems)
        if stages:
            first = functools.reduce(jnp.logical_and, [pl.program_id(a) == 0 for a in range(len(grid))])

            @pl.when(first)
            def _():
                for st, a, b, s in zip(stages, st_ins, st_outs, sems):
                    st.start(a, b, s)

        body(*pre, *ins, *outs)
        if stages:
            last = functools.reduce(jnp.logical_and, [pl.program_id(a) == g - 1 for a, g in enumerate(grid)])

            @pl.when(last)
            def _():
                for st, a, b, s in zip(stages, st_ins, st_outs, sems):
                    st.finish(a, b, s)

    all_in = list(in_specs) + [ANY] * len(st_args)
    all_out = list(out_specs) + [ANY] * len(st_shapes)
    kw = dict(has_side_effects=True) if stages else {}
    if vmem is not None:
        kw["vmem_limit_bytes"] = vmem * MIB
    if prefetch is None:
        gkw = dict(grid=grid, in_specs=all_in, out_specs=all_out, scratch_shapes=st_sems)
    else:
        gkw = dict(grid_spec=pltpu.PrefetchScalarGridSpec(
            num_scalar_prefetch=1, grid=grid, in_specs=all_in, out_specs=all_out, scratch_shapes=st_sems))
    res = pl.pallas_call(
        wrapped, name=name, out_shape=list(out_shape) + st_shapes, input_output_aliases=aliases,
        compiler_params=pltpu.CompilerParams(**kw), **gkw,
    )(*([prefetch] if npre else []), *args, *st_args)
    outs, rest, st_res = list(res[:nout]), list(res[nout:]), []
    for st in stages:
        st_res.append(rest[:len(st.out_shape)])
        rest = rest[len(st.out_shape):]
    return outs, st_res


def _mm(a, b):
    return jnp.dot(a.astype(BF), b.astype(BF), preferred_element_type=F32)


def _mm_nt(a, b):
    return lax.dot_general(a.astype(BF), b.astype(BF), (((1,), (1,)), ((), ())),
                           preferred_element_type=F32)


def _mm_tn(a, b):
    return lax.dot_general(a.astype(BF), b.astype(BF), (((0,), (0,)), ((), ())),
                           preferred_element_type=F32)


def _rows(v):
    return lax.broadcasted_iota(jnp.int32, v.shape, 0)


def _sd(v, s, fill=0.0):
    return jnp.where(_rows(v) >= s, pltpu.roll(v, s, axis=0), fill)


def _su(v, s, fill=0.0):
    n = v.shape[0]
    return jnp.where(_rows(v) < n - s, pltpu.roll(v, n - s, axis=0), fill)


def _sigmoid(z):
    return 1.0 / (1.0 + jnp.exp(-z))


def _softplus(z):
    e = jnp.exp(-jnp.abs(z))
    u = 1.0 + e
    d = u - 1.0
    log1p = jnp.where(d == 0.0, e, jnp.log(u) * (e / jnp.where(d == 0.0, 1.0, d)))
    return jnp.maximum(z, 0.0) + log1p


def _mean(v):
    return jnp.mean(v, axis=-1, keepdims=True)


def _colsum(v):
    return jnp.sum(v, axis=0, keepdims=True)


def _acc(ref, val, first):
    @pl.when(first)
    def _():
        ref[...] = val

    @pl.when(jnp.logical_not(first))
    def _():
        ref[...] += val


def _conv(xp, cw, cb):
    x1, x2, x3 = _sd(xp, 1), _sd(xp, 2), _sd(xp, 3)
    xc = cb + cw[0:1] * x3 + cw[1:2] * x2 + cw[2:3] * x1 + cw[3:4] * xp
    return xc, x1, x2, x3


def _lru_gates(xc, wa, ba, wx, bx, lam):
    xcb = xc.astype(BF)
    r = _sigmoid(_mm(xcb, wa) + ba)
    ii = _sigmoid(_mm(xcb, wx) + bx)
    sp = _softplus(-lam)
    la = (-LRU_C) * r * sp
    a = jnp.exp(la)
    mult = jnp.sqrt(-jnp.tanh(la) * (a * a + 1.0))
    return xcb, r, ii, sp, a, mult


def _gelu_parts(g):
    th = jnp.tanh(GELU_C * (g + 0.044715 * (g * g * g)))
    gel = 0.5 * g * (1.0 + th)
    dgel = 0.5 * (1.0 + th) + 0.5 * g * (1.0 - th * th) * (GELU_C * (1.0 + 3.0 * 0.044715 * (g * g)))
    return gel, dgel


def _pool_window(x, steps, shift):
    s, sh = x, 1
    for _ in range(steps):
        s = s + shift(s, sh)
        sh *= 2
    return s


def _fwd_inproj(x, g1, w_in, stages=()):
    tm = 512

    def body(x_ref, g_ref, w_ref, proj_ref, h_ref):
        @pl.when(pl.program_id(1) == 0)
        def _():
            xv = x_ref[...]
            r = lax.rsqrt(_mean(xv * xv) + NORM_EPS)
            h_ref[...] = ((xv * r) * g_ref[...]).astype(BF)

        proj_ref[...] = jnp.dot(h_ref[...], w_ref[0], preferred_element_type=F32)

    return _call(
        body, name="fwd_inproj", grid=(T // tm, NCHIP),
        in_specs=[pl.BlockSpec((tm, D), lambda i, k: (i, 0)),
                  pl.BlockSpec((1, D), lambda i, k: (0, 0)),
                  pl.BlockSpec((1, D, CW_IN), lambda i, k: (k, 0, 0))],
        out_specs=[pl.BlockSpec((tm, CW_IN), lambda i, k: (i, k)),
                   pl.BlockSpec((tm, D), lambda i, k: (i, 0))],
        out_shape=[jax.ShapeDtypeStruct((T, DIN), F32), jax.ShapeDtypeStruct((T, D), BF)],
        vmem=40, args=[x, g1, w_in], stages=stages)


def _vec_spec():
    return pl.BlockSpec((1, CB), lambda j: (0, j))


def _fwd_lru(proj, conv_w, conv_b, wa, ba, wx, bx, lam, stages=()):
    def body(xp_ref, g_ref, cw_ref, cb_ref, wa_ref, ba_ref, wx_ref, bx_ref, lam_ref, y_ref, h_ref):
        xc, _, _, _ = _conv(xp_ref[...], cw_ref[...], cb_ref[...])
        _, _, ii, _, a, mult = _lru_gates(xc, wa_ref[0], ba_ref[...], wx_ref[0], bx_ref[...], lam_ref[...])
        b = mult * (ii * xc)
        s = 1
        while s < T:
            b = b + a * _sd(b, s, 0.0)
            if 2 * s < T:
                a = a * _sd(a, s, 1.0)
            s *= 2
        h_ref[...] = b
        gel, _ = _gelu_parts(g_ref[...])
        y_ref[...] = (b * gel).astype(BF)

    return _call(
        body, name="fwd_lru", grid=(NG,),
        in_specs=[pl.BlockSpec((T, CB), lambda j: (0, j)),
                  pl.BlockSpec((T, CB), lambda j: (0, NG + j)),
                  pl.BlockSpec((4, CB), lambda j: (0, j)),
                  _vec_spec(),
                  pl.BlockSpec((1, CB, CB), lambda j: (j, 0, 0)), _vec_spec(),
                  pl.BlockSpec((1, CB, CB), lambda j: (j, 0, 0)), _vec_spec(),
                  _vec_spec()],
        out_specs=[pl.BlockSpec((T, CB), lambda j: (0, j)), pl.BlockSpec((T, CB), lambda j: (0, j))],
        out_shape=[jax.ShapeDtypeStruct((T, DR), BF), jax.ShapeDtypeStruct((T, DR), F32)],
        vmem=48, args=[proj, proj, conv_w, conv_b, wa, ba, wx, bx, lam], stages=stages)


def _pool_cnt(w):
    t = lax.broadcasted_iota(jnp.int32, (T, 1), 0)
    return jnp.minimum(t + 1, w).astype(F32)


def _fwd_pool(proj, pool_w, pool_scale):
    def body(xp_ref, pw_ref, sc_ref, y_ref):
        for g, w in enumerate(POOL_WINDOWS):
            cols = slice(g * PG, (g + 1) * PG)
            x = xp_ref[:, cols]
            p = _pool_window(x, g + 1, _sd) / _pool_cnt(w) - x
            y_ref[:, cols] = (_mm(p, pw_ref[g]) * sc_ref[:, cols]).astype(BF)

    return pl.pallas_call(
        body, name="fwd_pool", grid=(1,),
        in_specs=[pl.BlockSpec((T, DP), lambda i: (0, 2 * DR // DP)),
                  pl.BlockSpec((4, PG, PG), lambda i: (0, 0, 0)),
                  pl.BlockSpec((1, DP), lambda i: (0, 0))],
        out_specs=pl.BlockSpec((T, DP), lambda i: (0, 0)),
        out_shape=jax.ShapeDtypeStruct((T, DP), BF),
        compiler_params=_cp(48),
    )(proj, pool_w, pool_scale)


GATE_BLK = 512
GATE_BLK0 = (2 * DR + DP) // GATE_BLK


def _gate_specs(tm):
    return [pl.BlockSpec((tm, GATE_BLK), functools.partial(lambda i, q: (i, GATE_BLK0 + q), q=q))
            for q in range(4)]


def _fwd_merge(x, ylru, ypool, proj, b_gate, g2, g3, w_lru_up, w_pool_up, w_o, stages=()):
    tm = 512

    def body(x_ref, yl_ref, yp_ref, p0, p1, p2, p3, bg_ref, g2_ref, g3_ref, wl_ref, wp_ref, wo_ref,
             x2_ref, h2_ref, m_ref, mrg_ref, bra_ref, brb_ref):
        bra = jnp.dot(yl_ref[...], wl_ref[...], preferred_element_type=F32)
        yp = yp_ref[...]
        brb = jnp.concatenate([jnp.dot(yp, wp_ref[k], preferred_element_type=F32) for k in range(NCHIP)], axis=1)
        bg = bg_ref[...]
        ga = _sigmoid(jnp.concatenate([p0[...], p1[...]], axis=1) + bg[:, :D])
        gb = _sigmoid(jnp.concatenate([p2[...], p3[...]], axis=1) + bg[:, D:])
        mrg = (ga * bra + gb * brb).astype(BF)
        m = jnp.dot(mrg, wo_ref[...], preferred_element_type=F32)
        r2 = lax.rsqrt(_mean(m * m) + NORM_EPS)
        x2 = x_ref[...] + (m * r2) * g2_ref[...]
        r3 = lax.rsqrt(_mean(x2 * x2) + NORM_EPS)
        x2_ref[...] = x2
        h2_ref[...] = ((x2 * r3) * g3_ref[...]).astype(BF)
        m_ref[...] = m
        mrg_ref[...] = mrg
        bra_ref[...] = bra.astype(BF)
        brb_ref[...] = brb.astype(BF)

    row = lambda w: pl.BlockSpec((tm, w), lambda i: (i, 0))
    full2 = lambda a, b: pl.BlockSpec((a, b), lambda i: (0, 0))
    return _call(
        body, name="fwd_merge", grid=(T // tm,),
        in_specs=[row(D), row(DR), row(DP)] + _gate_specs(tm) +
                 [full2(1, 2 * D), full2(1, D), full2(1, D), full2(DR, D),
                  pl.BlockSpec((NCHIP, DP, D // NCHIP), lambda i: (0, 0, 0)), full2(D, D)],
        out_specs=[row(D)] * 6,
        out_shape=[jax.ShapeDtypeStruct((T, D), F32), jax.ShapeDtypeStruct((T, D), BF),
                   jax.ShapeDtypeStruct((T, D), F32), jax.ShapeDtypeStruct((T, D), BF),
                   jax.ShapeDtypeStruct((T, D), BF), jax.ShapeDtypeStruct((T, D), BF)],
        vmem=48, args=[x, ylru, ypool, proj, proj, proj, proj, b_gate, g2, g3, w_lru_up, w_pool_up, w_o],
        stages=stages)


FC = 256
FPC = (DF // NCHIP) // FC


def _fwd_mlp(h2, w_ff1, w_ff2):
    def body(h_ref, w1_ref, w2_ref, a1_ref, f_ref):
        a1 = jnp.maximum(jnp.dot(h_ref[...], w1_ref[0], preferred_element_type=F32), 0.0)
        a1_ref[...] = a1
        _acc(f_ref, jnp.dot((a1 * a1).astype(BF), w2_ref[...], preferred_element_type=F32),
             pl.program_id(0) == 0)

    return pl.pallas_call(
        body, name="fwd_mlp", grid=(DF // FC,),
        in_specs=[pl.BlockSpec((T, D), lambda j: (0, 0)),
                  pl.BlockSpec((1, D, FC), lambda j: (j // FPC, 0, j % FPC)),
                  pl.BlockSpec((FC, D), lambda j: (j, 0))],
        out_specs=[pl.BlockSpec((T, FC), lambda j: (0, j)), pl.BlockSpec((T, D), lambda j: (0, 0))],
        out_shape=[jax.ShapeDtypeStruct((T, DF), F32), jax.ShapeDtypeStruct((T, D), F32)],
        compiler_params=_cp(48),
    )(h2, w_ff1, w_ff2)


def _loss_head(f, x2, target, g4):
    tm = 512

    def body(f_ref, x2_ref, t_ref, g_ref, loss_ref, dy_ref, df_ref, dg_ref):
        first = pl.program_id(0) == 0
        f = f_ref[...]
        g4v = g_ref[...]
        r4 = lax.rsqrt(_mean(f * f) + NORM_EPS)
        fn = f * r4
        e = (x2_ref[...] + fn * g4v) - t_ref[...]
        _acc(loss_ref, jnp.sum(_mean(e * e), axis=0, keepdims=True), first)
        dy = e * (1.0 / D)
        dy_ref[...] = dy
        _acc(dg_ref, _colsum(dy * fn), first)
        dfn = dy * g4v
        df_ref[...] = (r4 * (dfn - fn * _mean(dfn * fn))).astype(BF)

    row = pl.BlockSpec((tm, D), lambda i: (i, 0))
    return pl.pallas_call(
        body, name="loss_head", grid=(T // tm,),
        in_specs=[row, row, row, pl.BlockSpec((1, D), lambda i: (0, 0))],
        out_specs=[pl.BlockSpec((1, 1), lambda i: (0, 0)), row, row, pl.BlockSpec((1, D), lambda i: (0, 0))],
        out_shape=[jax.ShapeDtypeStruct((1, 1), F32), jax.ShapeDtypeStruct((T, D), F32),
                   jax.ShapeDtypeStruct((T, D), BF), jax.ShapeDtypeStruct((1, D), F32)],
        compiler_params=_cp(48),
    )(f, x2, target, g4)


def _bwd_mlp(df, h2, a1, w_ff1, w_ff2):
    def body(df_ref, h_ref, a1_ref, w1_ref, w2_ref, dw1_ref, dw2_ref, dh_ref):
        df = df_ref[...]
        a1 = a1_ref[...]
        dact = _mm_nt(df, w2_ref[...])
        df1 = (dact * (2.0 * a1)).astype(BF)
        dw2_ref[...] = _mm_tn((a1 * a1).astype(BF), df)
        dw1_ref[0] = _mm_tn(h_ref[...], df1)
        _acc(dh_ref, _mm_nt(df1, w1_ref[0]), pl.program_id(0) == 0)

    return pl.pallas_call(
        body, name="bwd_mlp", grid=(DF // FC,),
        in_specs=[pl.BlockSpec((T, D), lambda j: (0, 0)),
                  pl.BlockSpec((T, D), lambda j: (0, 0)),
                  pl.BlockSpec((T, FC), lambda j: (0, j)),
                  pl.BlockSpec((1, D, FC), lambda j: (j // FPC, 0, j % FPC)),
                  pl.BlockSpec((FC, D), lambda j: (j, 0))],
        out_specs=[pl.BlockSpec((1, D, FC), lambda j: (j // FPC, 0, j % FPC)),
                   pl.BlockSpec((FC, D), lambda j: (j, 0)),
                   pl.BlockSpec((T, D), lambda j: (0, 0))],
        out_shape=[jax.ShapeDtypeStruct((NCHIP, D, DF // NCHIP), F32),
                   jax.ShapeDtypeStruct((DF, D), F32), jax.ShapeDtypeStruct((T, D), F32)],
        compiler_params=_cp(56),
    )(df, h2, a1, w_ff1, w_ff2)


def _bwd_merge(dh2, dy, x2, m, mrg, bra, brb, proj, b_gate, ylru, ypool, g2, g3, w_lru_up, w_pool_up, w_o,
               stages=()):
    tm = 256
    cpu = D // NCHIP

    def body(dh2_ref, dy_ref, x2_ref, m_ref, mrg_ref, bra_ref, brb_ref, p0, p1, p2, p3, bg_ref, yl_ref, yp_ref,
             g2_ref, g3_ref, wl_ref, wp_ref, wo_ref,
             dx_ref, dgt_ref, dyl_ref, dyp_ref, dwo_ref, dwl_ref, dwp_ref, dg2_ref, dg3_ref, dbg_ref):
        first = pl.program_id(0) == 0
        x2 = x2_ref[...]
        r3 = lax.rsqrt(_mean(x2 * x2) + NORM_EPS)
        x2n = x2 * r3
        dh2 = dh2_ref[...]
        t3 = dh2 * g3_ref[...]
        dx2 = dy_ref[...] + r3 * (t3 - x2n * _mean(t3 * x2n))
        dx_ref[...] = dx2
        _acc(dg3_ref, _colsum(dh2 * x2n), first)
        m = m_ref[...]
        r2 = lax.rsqrt(_mean(m * m) + NORM_EPS)
        mn = m * r2
        _acc(dg2_ref, _colsum(dx2 * mn), first)
        dmn = dx2 * g2_ref[...]
        dm = (r2 * (dmn - mn * _mean(dmn * mn))).astype(BF)
        dmrg = _mm_nt(dm, wo_ref[...])
        _acc(dwo_ref, _mm_tn(mrg_ref[...], dm), first)
        bg = bg_ref[...]
        ga = _sigmoid(jnp.concatenate([p0[...], p1[...]], axis=1) + bg[:, :D])
        gb = _sigmoid(jnp.concatenate([p2[...], p3[...]], axis=1) + bg[:, D:])
        dga = dmrg * bra_ref[...].astype(F32) * (ga * (1.0 - ga))
        dgb = dmrg * brb_ref[...].astype(F32) * (gb * (1.0 - gb))
        dgt_ref[:, :D] = dga.astype(BF)
        dgt_ref[:, D:] = dgb.astype(BF)
        _acc(dbg_ref, jnp.concatenate([_colsum(dga), _colsum(dgb)], axis=1), first)
        dbra = (dmrg * ga).astype(BF)
        dbrb = (dmrg * gb).astype(BF)
        dyl_ref[...] = _mm_nt(dbra, wl_ref[...])
        _acc(dwl_ref, _mm_tn(yl_ref[...], dbra), first)
        yp = yp_ref[...]
        dyp = None
        for k in range(NCHIP):
            dk = dbrb[:, k * cpu:(k + 1) * cpu]
            part = _mm_nt(dk, wp_ref[k])
            dyp = part if dyp is None else dyp + part
            _acc(dwp_ref.at[k], _mm_tn(yp, dk), first)
        dyp_ref[...] = dyp

    row = lambda w: pl.BlockSpec((tm, w), lambda i: (i, 0))
    full2 = lambda a, b: pl.BlockSpec((a, b), lambda i: (0, 0))
    wp_spec = pl.BlockSpec((NCHIP, DP, cpu), lambda i: (0, 0, 0))
    return _call(
        body, name="bwd_merge", grid=(T // tm,),
        in_specs=[row(D)] * 7 + _gate_specs(tm) +
                 [full2(1, 2 * D), row(DR), row(DP), full2(1, D), full2(1, D), full2(DR, D), wp_spec, full2(D, D)],
        out_specs=[row(D), row(2 * D), row(DR), row(DP), full2(D, D), full2(DR, D), wp_spec,
                   full2(1, D), full2(1, D), full2(1, 2 * D)],
        out_shape=[jax.ShapeDtypeStruct((T, D), F32), jax.ShapeDtypeStruct((T, 2 * D), BF),
                   jax.ShapeDtypeStruct((T, DR), F32), jax.ShapeDtypeStruct((T, DP), F32),
                   jax.ShapeDtypeStruct((D, D), F32), jax.ShapeDtypeStruct((DR, D), F32),
                   jax.ShapeDtypeStruct((NCHIP, DP, cpu), F32),
                   jax.ShapeDtypeStruct((1, D), F32), jax.ShapeDtypeStruct((1, D), F32),
                   jax.ShapeDtypeStruct((1, 2 * D), F32)],
        vmem=56, args=[dh2, dy, x2, m, mrg, bra, brb, proj, proj, proj, proj, b_gate, ylru, ypool, g2, g3, w_lru_up,
                       w_pool_up, w_o], stages=stages)


def _bwd_lru(proj, h, dylru, conv_w, conv_b, wa, ba, wx, bx, lam, stages=()):
    def body(xp_ref, g_ref, h_ref, dy_ref, cw_ref, cb_ref, wa_ref, ba_ref, wx_ref, bx_ref, lam_ref,
             dxp_ref, dg_ref, dcw_ref, dcb_ref, dwa_ref, dba_ref, dwx_ref, dbx_ref, dlam_ref):
        xp = xp_ref[...]
        cw = cw_ref[...]
        lam = lam_ref[...]
        xc, x1, x2, x3 = _conv(xp, cw, cb_ref[...])
        wa, wx = wa_ref[0], wx_ref[0]
        xcb, r, ii, sp, a, mult = _lru_gates(xc, wa, ba_ref[...], wx, bx_ref[...], lam)
        g = g_ref[...]
        gel, dgel = _gelu_parts(g)
        h = h_ref[...]
        dy = dy_ref[...]
        dg_ref[...] = (dy * h * dgel).astype(BF)
        b = dy * gel
        aa = _su(a, 1, 0.0)
        s = 1
        while s < T:
            b = b + aa * _su(b, s, 0.0)
            if 2 * s < T:
                aa = aa * _su(aa, s, 0.0)
            s *= 2
        da = b * _sd(h, 1, 0.0)
        dmult = b * (ii * xc)
        dii = b * (mult * xc)
        dxc = b * (mult * ii)
        dla = da * a - dmult * ((a * a) / mult)
        dr = dla * ((-LRU_C) * sp)
        dsp = _colsum(dla * ((-LRU_C) * r))
        dlam_ref[...] = -dsp / (1.0 + jnp.exp(lam))
        dzr = dr * (r * (1.0 - r))
        dzi = dii * (ii * (1.0 - ii))
        dzrb, dzib = dzr.astype(BF), dzi.astype(BF)
        dxc = dxc + _mm_nt(dzrb, wa) + _mm_nt(dzib, wx)
        dwa_ref[0] = _mm_tn(xcb, dzrb)
        dwx_ref[0] = _mm_tn(xcb, dzib)
        dba_ref[...] = _colsum(dzr)
        dbx_ref[...] = _colsum(dzi)
        dcb_ref[...] = _colsum(dxc)
        dcw_ref[...] = jnp.concatenate([_colsum(dxc * x3), _colsum(dxc * x2), _colsum(dxc * x1),
                                        _colsum(dxc * xp)], axis=0)
        dxp = cw[3:4] * dxc + cw[2:3] * _su(dxc, 1) + cw[1:2] * _su(dxc, 2) + cw[0:1] * _su(dxc, 3)
        dxp_ref[...] = dxp.astype(BF)

    blk = pl.BlockSpec((T, CB), lambda j: (0, j))
    wsp = pl.BlockSpec((1, CB, CB), lambda j: (j, 0, 0))
    return _call(
        body, name="bwd_lru", grid=(NG,),
        in_specs=[blk, pl.BlockSpec((T, CB), lambda j: (0, NG + j)), blk, blk,
                  pl.BlockSpec((4, CB), lambda j: (0, j)), _vec_spec(), wsp, _vec_spec(), wsp, _vec_spec(),
                  _vec_spec()],
        out_specs=[blk, blk, pl.BlockSpec((4, CB), lambda j: (0, j)), _vec_spec(), wsp, _vec_spec(), wsp,
                   _vec_spec(), _vec_spec()],
        out_shape=[jax.ShapeDtypeStruct((T, DR), BF), jax.ShapeDtypeStruct((T, DR), BF),
                   jax.ShapeDtypeStruct((4, DR), F32), jax.ShapeDtypeStruct((1, DR), F32),
                   jax.ShapeDtypeStruct((NG, CB, CB), F32), jax.ShapeDtypeStruct((1, DR), F32),
                   jax.ShapeDtypeStruct((NG, CB, CB), F32), jax.ShapeDtypeStruct((1, DR), F32),
                   jax.ShapeDtypeStruct((1, DR), F32)],
        vmem=56, args=[proj, proj, h, dylru, conv_w, conv_b, wa, ba, wx, bx, lam], stages=stages)


def _bwd_pool(proj, dypool, pool_w, pool_scale):
    def body(xp_ref, dy_ref, pw_ref, sc_ref, dx_ref, dw_ref, dsc_ref):
        for g, w in enumerate(POOL_WINDOWS):
            cols = slice(g * PG, (g + 1) * PG)
            cnt = _pool_cnt(w)
            x = xp_ref[:, cols]
            pb = (_pool_window(x, g + 1, _sd) / cnt - x).astype(BF)
            wg = pw_ref[g]
            dy = dy_ref[:, cols]
            dsc_ref[:, cols] = _colsum(dy * _mm(pb, wg))
            dyp = (dy * sc_ref[:, cols]).astype(BF)
            dw_ref[g] = _mm_tn(pb, dyp)
            dp = _mm_nt(dyp, wg)
            dx_ref[:, cols] = (_pool_window(dp / cnt, g + 1, _su) - dp).astype(BF)

    return pl.pallas_call(
        body, name="bwd_pool", grid=(1,),
        in_specs=[pl.BlockSpec((T, DP), lambda i: (0, 2 * DR // DP)),
                  pl.BlockSpec((T, DP), lambda i: (0, 0)),
                  pl.BlockSpec((4, PG, PG), lambda i: (0, 0, 0)),
                  pl.BlockSpec((1, DP), lambda i: (0, 0))],
        out_specs=[pl.BlockSpec((T, DP), lambda i: (0, 0)),
                   pl.BlockSpec((4, PG, PG), lambda i: (0, 0, 0)),
                   pl.BlockSpec((1, DP), lambda i: (0, 0))],
        out_shape=[jax.ShapeDtypeStruct((T, DP), BF), jax.ShapeDtypeStruct((4, PG, PG), F32),
                   jax.ShapeDtypeStruct((1, DP), F32)],
        compiler_params=_cp(48),
    )(proj, dypool, pool_w, pool_scale)


def _bwd_inproj(h1, dproj, w_in, stages=()):
    def body(h_ref, dp_ref, w_ref, dw_ref, dh_ref):
        dp = dp_ref[...]
        dw_ref[0] = _mm_tn(h_ref[...], dp)
        _acc(dh_ref, _mm_nt(dp, w_ref[0]), pl.program_id(0) == 0)

    return _call(
        body, name="bwd_inproj", grid=(NCHIP,),
        in_specs=[pl.BlockSpec((T, D), lambda k: (0, 0)),
                  pl.BlockSpec((T, CW_IN), lambda k: (0, k)),
                  pl.BlockSpec((1, D, CW_IN), lambda k: (k, 0, 0))],
        out_specs=[pl.BlockSpec((1, D, CW_IN), lambda k: (k, 0, 0)), pl.BlockSpec((T, D), lambda k: (0, 0))],
        out_shape=[jax.ShapeDtypeStruct((NCHIP, D, CW_IN), F32), jax.ShapeDtypeStruct((T, D), F32)],
        vmem=56, args=[h1, dproj, w_in], stages=stages)


def _bwd_prenorm(x, dh1, dxres, g1, stages=()):
    tm = 512

    def body(x_ref, dh_ref, dr_ref, g_ref, dx_ref, dg_ref):
        xv = x_ref[...]
        r = lax.rsqrt(_mean(xv * xv) + NORM_EPS)
        xn = xv * r
        dh = dh_ref[...]
        t = dh * g_ref[...]
        dx_ref[...] = dr_ref[...] + r * (t - xn * _mean(t * xn))
        _acc(dg_ref, _colsum(dh * xn), pl.program_id(0) == 0)

    row = pl.BlockSpec((tm, D), lambda i: (i, 0))
    vec = pl.BlockSpec((1, D), lambda i: (0, 0))
    return _call(
        body, name="bwd_prenorm", grid=(T // tm,),
        in_specs=[row, row, row, vec], out_specs=[row, vec],
        out_shape=[jax.ShapeDtypeStruct((T, D), F32), jax.ShapeDtypeStruct((1, D), F32)],
        vmem=48, args=[x, dh1, dxres, g1], stages=stages)


def _place():
    x, y, c = lax.axis_index("x"), lax.axis_index("y"), lax.axis_index("c")
    chips = [(1 - x, y), (x, 1 - y), (1 - x, 1 - y)]
    return x, y, c, chips


def _rcopy(src, dst, ssem, rsem, dev):
    return pltpu.make_async_remote_copy(src_ref=src, dst_ref=dst, send_sem=ssem, recv_sem=rsem,
                                        device_id=dev, device_id_type=MESH_ID)


def _sds(a):
    return jax.ShapeDtypeStruct(a.shape, a.dtype)


def _sem2(n, m):
    return [pltpu.SemaphoreType.DMA((n, m)), pltpu.SemaphoreType.DMA((n, m))]


def _half(ref, k, half):
    hr = ref.shape[1] // 2
    return ref.at[k, pl.ds(half * hr, hr), :]


def _gather_ici(fulls):
    n = len(fulls)

    def copies(outs, sems):
        x, y, c, chips = _place()
        me = 2 * x + y
        send = [_rcopy(_half(outs[i], me, c), _half(outs[i], me, c), sems[0].at[i, j], sems[1].at[i, j], (*chip, c))
                for i in range(n) for j, chip in enumerate(chips)]
        recv = [_rcopy(_half(outs[i], 2 * chip[0] + chip[1], c), _half(outs[i], 2 * chip[0] + chip[1], c),
                       sems[0].at[i, j], sems[1].at[i, j], (*chip, c))
                for i in range(n) for j, chip in enumerate(chips)]
        return send, recv

    def start(ins, outs, sems):
        for cp in copies(outs, sems)[0]:
            cp.start()

    def finish(ins, outs, sems):
        send, recv = copies(outs, sems)
        for cp in recv:
            cp.wait_recv()
        for cp in send:
            cp.wait_send()

    return _Stage(fulls, [_sds(f) for f in fulls], {i: i for i in range(n)}, _sem2(n, 3), start, finish)


def _gather_d2d(fulls):
    n = len(fulls)

    def copies(outs, sems):
        x, y, c, chips = _place()
        sib = (x, y, 1 - c)
        ks = [2 * chip[0] + chip[1] for chip in chips]
        send = [_rcopy(_half(outs[i], k, c), _half(outs[i], k, c), sems[0].at[i, j], sems[1].at[i, j], sib)
                for i in range(n) for j, k in enumerate(ks)]
        recv = [_rcopy(_half(outs[i], k, 1 - c), _half(outs[i], k, 1 - c), sems[0].at[i, j], sems[1].at[i, j], sib)
                for i in range(n) for j, k in enumerate(ks)]
        return send, recv

    def start(ins, outs, sems):
        for cp in copies(outs, sems)[0]:
            cp.start()

    def finish(ins, outs, sems):
        send, recv = copies(outs, sems)
        for cp in recv:
            cp.wait_recv()
        for cp in send:
            cp.wait_send()

    return _Stage(fulls, [_sds(f) for f in fulls], {i: i for i in range(n)}, _sem2(n, 3), start, finish)


def _gather_first(full, conv_w):
    ici, d2d = _gather_ici([full]), _gather_d2d([full])

    def body(full_in, cw_in, full_out, cw_out, s0, r0, s1, r1, cs, cr):
        x, y, c, chips = _place()
        me = 2 * x + y
        conv = [_rcopy(cw_in, cw_out.at[me], cs.at[j], cr.at[j], (*chip, c)) for j, chip in enumerate(chips)]
        for cp in conv:
            cp.start()
        ici.start(None, [full_out], [s0, r0])
        ici.finish(None, [full_out], [s0, r0])
        d2d.start(None, [full_out], [s1, r1])
        d2d.finish(None, [full_out], [s1, r1])
        for j, chip in enumerate(chips):
            _rcopy(cw_in, cw_out.at[2 * chip[0] + chip[1]], cs.at[j], cr.at[j], (*chip, c)).wait_recv()
        for cp in conv:
            cp.wait_send()

    return pl.pallas_call(
        body, name="gather_first",
        in_specs=[ANY, ANY], out_specs=[ANY, ANY],
        out_shape=[_sds(full), jax.ShapeDtypeStruct((NCHIP,) + conv_w.shape, conv_w.dtype)],
        input_output_aliases={0: 0},
        scratch_shapes=_sem2(1, 3) + _sem2(1, 3) + [pltpu.SemaphoreType.DMA((3,)), pltpu.SemaphoreType.DMA((3,))],
        compiler_params=pltpu.CompilerParams(has_side_effects=True),
    )(full, conv_w)


def _comm_only(name, stage):
    def body(*refs):
        ni, no = len(stage.operands), len(stage.out_shape)
        stage.start(refs[:ni], refs[ni:ni + no], refs[ni + no:])
        stage.finish(refs[:ni], refs[ni:ni + no], refs[ni + no:])

    return pl.pallas_call(
        body, name=name, in_specs=[ANY] * len(stage.operands), out_specs=[ANY] * len(stage.out_shape),
        out_shape=stage.out_shape, input_output_aliases=stage.alias, scratch_shapes=stage.sems,
        compiler_params=pltpu.CompilerParams(has_side_effects=True),
    )(*stage.operands)


def _to_sibling(srcs):
    n = len(srcs)

    def copies(ins, outs, sems):
        x, y, c, _ = _place()
        sib = (x, y, 1 - c)
        return [_rcopy(ins[i].at[:, 1 - c] if srcs[i].ndim == 4 else ins[i], outs[i], sems[0].at[i], sems[1].at[i], sib)
                for i in range(n)]

    def start(ins, outs, sems):
        for cp in copies(ins, outs, sems):
            cp.start()

    def finish(ins, outs, sems):
        for cp in copies(ins, outs, sems):
            cp.wait()

    shapes = [jax.ShapeDtypeStruct((NCHIP,) + s.shape[2:] if s.ndim == 4 else s.shape, s.dtype) for s in srcs]
    return _Stage(srcs, shapes, {}, [pltpu.SemaphoreType.DMA((n,)), pltpu.SemaphoreType.DMA((n,))], start, finish)


def _to_chips(srcs):
    n = len(srcs)

    def copies(ins, outs, sems):
        x, y, c, chips = _place()
        me = 2 * x + y
        return [_rcopy(ins[i].at[2 * chip[0] + chip[1]] if srcs[i].shape[0] == NCHIP else ins[i].at[c],
                       outs[i].at[me], sems[0].at[i, j], sems[1].at[i, j], (*chip, c))
                for i in range(n) for j, chip in enumerate(chips)]

    def start(ins, outs, sems):
        for cp in copies(ins, outs, sems):
            cp.start()

    def finish(ins, outs, sems):
        for cp in copies(ins, outs, sems):
            cp.wait()

    shapes = [jax.ShapeDtypeStruct((NCHIP,) + s.shape[1:], s.dtype) for s in srcs]
    return _Stage(srcs, shapes, {}, _sem2(n, 3), start, finish)


def _share(pairs):
    n = len(pairs)

    def start(ins, outs, sems):
        x, y, c, _ = _place()
        for i in range(n):
            _rcopy(outs[i].at[c], outs[i].at[c], sems[0].at[i], sems[1].at[i], (x, y, 1 - c)).start()

    def finish(ins, outs, sems):
        x, y, c, _ = _place()
        for i in range(n):
            _rcopy(outs[i].at[c], outs[i].at[c], sems[0].at[i], sems[1].at[i], (x, y, 1 - c)).wait_send()
            _rcopy(outs[i].at[1 - c], outs[i].at[1 - c], sems[0].at[i], sems[1].at[i], (x, y, 1 - c)).wait_recv()

    return _Stage(pairs, [_sds(p) for p in pairs], {i: i for i in range(n)},
                  [pltpu.SemaphoreType.DMA((n,)), pltpu.SemaphoreType.DMA((n,))], start, finish)


def _row_block(rows, cols, itemsize=4, target=MIB):
    br = rows
    while br * cols * itemsize > target and br % 16 == 0:
        br //= 2
    return br


def _cast_place(w, chip_idx, name):
    rows, cols = w.shape
    br = _row_block(rows, cols)

    def body(k_ref, w_ref, o_ref):
        o_ref[0] = w_ref[...].astype(BF)

    return _call(
        body, name=name, grid=(rows // br,), prefetch=chip_idx,
        in_specs=[pl.BlockSpec((br, cols), lambda r, k: (r, 0))],
        out_specs=[pl.BlockSpec((1, br, cols), lambda r, k: (k[0], r, 0))],
        out_shape=[jax.ShapeDtypeStruct((NCHIP, rows, cols), BF)], vmem=32, args=[w])[0][0]


def _add_sibling(g, land, cidx, name, stages=()):
    _, _, hr, cols = g.shape
    br = _row_block(hr, cols)

    def body(c_ref, g_ref, l_ref, o_ref):
        o_ref[...] = (g_ref[0, 0] + l_ref[0]).astype(BF)[None]

    outs, st = _call(
        body, name=name, grid=(NCHIP, hr // br), prefetch=cidx,
        in_specs=[pl.BlockSpec((1, 1, br, cols), lambda k, r, c: (k, c[0], r, 0)),
                  pl.BlockSpec((1, br, cols), lambda k, r, c: (k, r, 0))],
        out_specs=[pl.BlockSpec((1, br, cols), lambda k, r, c: (k, r, 0))],
        out_shape=[jax.ShapeDtypeStruct((NCHIP, hr, cols), BF)], vmem=32, args=[g, land], stages=stages)
    return outs[0], st


def _add_pair(a, b, name):
    rows, cols = a.shape

    def body(a_ref, b_ref, o_ref):
        o_ref[...] = a_ref[...] + b_ref[...]

    spec = pl.BlockSpec((rows, cols), lambda r: (0, 0))
    return _call(body, name=name, grid=(1,), in_specs=[spec, spec], out_specs=[spec], out_shape=[_sds(a)],
                 vmem=32, args=[a, b])[0][0]


def _add_chips(own, land, idx, name):
    _, hr, cols = land.shape
    br = _row_block(hr, cols)

    def body(s_ref, a_ref, b_ref, c_ref, d_ref, o_ref):
        o_ref[...] = (a_ref[...].astype(F32) + b_ref[...].astype(F32)) + (c_ref[...].astype(F32) +
                                                                           d_ref[...].astype(F32))

    spec = lambda q: pl.BlockSpec((1, br, cols), functools.partial(lambda r, s, q: (s[q], r, 0), q=q))
    return _call(
        body, name=name, grid=(hr // br,), prefetch=idx,
        in_specs=[spec(0), spec(1), spec(2), spec(3)], out_specs=[spec(4)],
        out_shape=[jax.ShapeDtypeStruct((2, hr, cols), F32)], vmem=32, args=[own, land, land, land])[0][0]


def _adamw_math(w, g, m, v):
    mn = ADAM_B1 * m + (1.0 - ADAM_B1) * g
    vn = ADAM_B2 * v + (1.0 - ADAM_B2) * (g * g)
    m_hat = mn / (1.0 - ADAM_B1 ** ADAM_STEP)
    v_hat = vn / (1.0 - ADAM_B2 ** ADAM_STEP)
    return -ADAM_LR * (m_hat / (jnp.sqrt(v_hat) + ADAM_EPS) + ADAM_WD * w), mn, vn


def _adamw(w, g, m, v, name, stages=()):
    rows, cols = w.shape
    br = _row_block(rows, cols)

    def body(w_ref, g_ref, m_ref, v_ref, d_ref, mo_ref, vo_ref):
        d_ref[...], mo_ref[...], vo_ref[...] = _adamw_math(w_ref[...], g_ref[...], m_ref[...], v_ref[...])

    spec = pl.BlockSpec((br, cols), lambda r: (r, 0))
    return _call(body, name=name, grid=(rows // br,), in_specs=[spec] * 4, out_specs=[spec] * 3,
                 out_shape=[_sds(w)] * 3, vmem=32, args=[w, g, m, v], stages=stages)


SMALL_AT = {"norm_mix_pre": (0, 1, D), "norm_mix_post": (1, 1, D), "norm_mlp_pre": (2, 1, D),
            "norm_mlp_post": (3, 1, D), "b_gate": (4, 2, D), "conv_b": (6, 1, D), "lru_b_a": (7, 1, D),
            "lru_b_x": (8, 1, D), "lru_lambda": (9, 1, D), "pool_scale": (10, 1, DP)}
SMALL_SEPARATE = ["conv_w", "lru_w_a", "lru_w_x", "pool_w"]


def _adamw_small(small_sum, sep_grads, w, m, v):
    packed, sep = list(SMALL_AT), list(SMALL_SEPARATE)
    names = packed + sep

    def body(*refs):
        s_ref, refs = refs[0], refs[1:]
        g_sep, refs = refs[:len(sep)], refs[len(sep):]
        nn = len(names)
        w_r, m_r, v_r, refs = refs[:nn], refs[nn:2 * nn], refs[2 * nn:3 * nn], refs[3 * nn:]
        g_out, refs = refs[:len(packed)], refs[len(packed):]
        d_o, m_o, v_o = refs[:nn], refs[nn:2 * nn], refs[2 * nn:3 * nn]
        for i, n in enumerate(names):
            if n in SMALL_AT:
                r0, nr, nc = SMALL_AT[n]
                g = jnp.concatenate([s_ref[r0 + q:r0 + q + 1, :nc] for q in range(nr)], axis=1)
                g_out[i][...] = g
            else:
                g = g_sep[i - len(packed)][...]
            d_o[i][...], m_o[i][...], v_o[i][...] = _adamw_math(w_r[i][...], g, m_r[i][...], v_r[i][...])

    ws = [w[n] for n in names]
    res = pl.pallas_call(
        body, name="adamw_small",
        out_shape=[_sds(w[n]) for n in packed] + [_sds(a) for a in ws] * 3,
        compiler_params=_cp(32),
    )(small_sum, *sep_grads, *ws, *[m[n] for n in names], *[v[n] for n in names])
    nn, npk = len(names), len(packed)
    grad = dict(zip(packed, res[:npk]))
    delta = dict(zip(names, res[npk:npk + nn]))
    new_m = dict(zip(names, res[npk + nn:npk + 2 * nn]))
    new_v = dict(zip(names, res[npk + 2 * nn:]))
    return grad, delta, new_m, new_v


W_NAMES = ["norm_mix_pre", "norm_mix_post", "norm_mlp_pre", "norm_mlp_post", "w_in", "b_gate", "conv_w", "conv_b",
           "lru_w_a", "lru_b_a", "lru_w_x", "lru_b_x", "lru_lambda", "pool_w", "pool_scale", "w_lru_up",
           "w_pool_up", "w_o", "w_ff1", "w_ff2"]
BIG = ["w_in", "w_lru_up", "w_pool_up", "w_o", "w_ff1", "w_ff2"]


def _block_diag(w):
    hd = w.shape[-1]
    per = CB // hd
    w4 = w.reshape(NG, per, hd, hd)
    eye = jnp.eye(per, dtype=w.dtype)
    return jnp.einsum("gpij,pq->gpiqj", w4, eye).reshape(NG, CB, CB)


def _block_diag_extract(d, hd):
    per = CB // hd
    d5 = d.reshape(NG, per, hd, per, hd)
    return jnp.stack([d5[:, p, :, p, :] for p in range(per)], axis=1).reshape(NG * per, hd, hd)


def _halves(g):
    return g.reshape(NCHIP, 2, g.size // (g.shape[-1] * 2 * NCHIP), g.shape[-1])


def kernel(x, norm_mix_pre, norm_mix_post, norm_mlp_pre, norm_mlp_post, w_in, b_gate, conv_w, conv_b, lru_w_a, lru_b_a, lru_w_x, lru_b_x, lru_lambda, pool_w, pool_scale, w_lru_up, w_pool_up, w_o, w_ff1, w_ff2, loss_target, m_norm_mix_pre, m_norm_mix_post, m_norm_mlp_pre, m_norm_mlp_post, m_w_in, m_b_gate, m_conv_w, m_conv_b, m_lru_w_a, m_lru_b_a, m_lru_w_x, m_lru_b_x, m_lru_lambda, m_pool_w, m_pool_scale, m_w_lru_up, m_w_pool_up, m_w_o, m_w_ff1, m_w_ff2, v_norm_mix_pre, v_norm_mix_post, v_norm_mlp_pre, v_norm_mlp_post, v_w_in, v_b_gate, v_conv_w, v_conv_b, v_lru_w_a, v_lru_b_a, v_lru_w_x, v_lru_b_x, v_lru_lambda, v_pool_w, v_pool_scale, v_w_lru_up, v_w_pool_up, v_w_o, v_w_ff1, v_w_ff2):
    args = dict(locals())
    two_d = lambda a: a.reshape(-1, a.shape[-1])
    w = {n: two_d(args[n]) for n in W_NAMES}
    mom = {n: two_d(args["m_" + n]) for n in W_NAMES}
    var = {n: two_d(args["v_" + n]) for n in W_NAMES}
    i32 = lambda val: jnp.asarray(val, jnp.int32)
    chip = i32(2 * lax.axis_index("x") + lax.axis_index("y"))
    core = i32(lax.axis_index("c"))
    cidx = core.reshape(1)
    zero = i32(0)
    hd = lru_w_a.shape[-1]
    xs, target = x[0], loss_target[0]
    g1, g2, g3, g4 = norm_mix_pre, norm_mix_post, norm_mlp_pre, norm_mlp_post

    full = {n: _cast_place(w[n], chip.reshape(1), "cast_" + n) for n in BIG}
    wa = _block_diag(lru_w_a[0]).astype(BF)
    wx = _block_diag(lru_w_x[0]).astype(BF)
    pw = pool_w[0].astype(BF)

    full["w_in"], conv_all = _gather_first(full["w_in"], w["conv_w"])
    conv_all = lax.dynamic_update_slice(conv_all, w["conv_w"][None], (chip, zero, zero))
    conv_full = jnp.transpose(conv_all, (1, 0, 2)).reshape(4, DR)
    mix = ["w_lru_up", "w_pool_up", "w_o"]
    (proj, h1), (got,) = _fwd_inproj(xs, g1, full["w_in"], stages=[_gather_ici([full[n] for n in mix])])
    (ylru, hs), (got, (ff1,)) = _fwd_lru(proj, conv_full, conv_b, wa, lru_b_a, wx, lru_b_x, lru_lambda,
                                          stages=[_gather_d2d(got), _gather_ici([full["w_ff1"]])])
    w_lru_up_f, w_pool_up_f, w_o_f = got[0].reshape(DR, D), got[1], got[2].reshape(D, D)
    ypool = _fwd_pool(proj, pw, pool_scale)
    (x2, h2, m, mrg, bra, brb), ((ff1,), (ff2,)) = _fwd_merge(
        xs, ylru, ypool, proj, b_gate, g2, g3, w_lru_up_f, w_pool_up_f, w_o_f,
        stages=[_gather_d2d([ff1]), _gather_ici([full["w_ff2"]])])
    ff2 = _comm_only("gather_last", _gather_d2d([ff2]))[0].reshape(DF, D)
    a1, f = _fwd_mlp(h2, ff1, ff2)
    lossp, dy, df, dg4 = _loss_head(f, x2, target, g4)

    idx_big = jnp.stack([chip, (chip + 1) % NCHIP, (chip + 2) % NCHIP, (chip + 3) % NCHIP, core])
    dw_ff1, dw_ff2, dh2 = _bwd_mlp(df, h2, a1, ff1, ff2)
    g_ff = [_halves(dw_ff1), _halves(dw_ff2)]
    (dxres, dgates, dylru, dypool, dw_o, dw_lru_up, dw_pool_up, dg2, dg3, dbg), (l_ff,) = _bwd_merge(
        dh2, dy, x2, m, mrg, bra, brb, proj, b_gate, ylru, ypool, g2, g3, w_lru_up_f, w_pool_up_f, w_o_f,
        stages=[_to_sibling(g_ff)])
    p_ff = [_add_sibling(g, l, cidx, "add_sibling_" + n)[0] for g, l, n in zip(g_ff, l_ff, ["w_ff1", "w_ff2"])]
    g_mix = [_halves(dw_lru_up), _halves(dw_pool_up), _halves(dw_o)]
    (dxp, dgl, dcw, dcb, dwa, dba, dwx, dbx, dlam), (c_ff, l_mix) = _bwd_lru(
        proj, hs, dylru, conv_full, conv_b, wa, lru_b_a, wx, lru_b_x, lru_lambda,
        stages=[_to_chips(p_ff), _to_sibling(g_mix)])
    pair_ff = [_add_chips(p, l, idx_big, "add_chips_" + n) for p, l, n in zip(p_ff, c_ff, ["w_ff1", "w_ff2"])]
    p_mix = [_add_sibling(g, l, cidx, "add_sibling_" + n)[0] for g, l, n in zip(g_mix, l_mix, mix)]
    dxpool, dpw, dsc = _bwd_pool(proj, dypool, pw, pool_scale)
    dproj = jnp.concatenate([dxp, dgl, dxpool, dgates], axis=1)
    (dw_in, dh1), (c_mix, pair_ff) = _bwd_inproj(h1, dproj, full["w_in"], stages=[_to_chips(p_mix), _share(pair_ff)])
    pair_mix = [_add_chips(p, l, idx_big, "add_chips_" + n) for p, l, n in zip(p_mix, c_mix, mix)]
    g_in = _halves(dw_in)
    (grad_x, dg1), ((l_in,),) = _bwd_prenorm(xs, dh1, dxres, g1, stages=[_to_sibling([g_in])])

    small = jnp.concatenate([
        dg1, dg2, dg3, dg4, dbg.reshape(2, D), dcb, dba, dbx, dlam,
        jnp.pad(dsc, ((0, 0), (0, D - DP))), jnp.pad(lossp, ((0, 0), (0, D - 1))), dcw,
        _block_diag_extract(dwa, hd).reshape(-1, D), _block_diag_extract(dwx, hd).reshape(-1, D),
        dpw.reshape(-1, D)], axis=0)
    p_in, ((l_small,),) = _add_sibling(g_in, l_in, cidx, "add_sibling_w_in", stages=[_to_sibling([small])])
    small2 = _add_pair(small, l_small, "add_sibling_small").reshape(2, SMALL_ROWS // 2, D)

    grads, delta, new_m, new_v = {}, {}, {}, {}
    grads["w_ff1"], grads["w_ff2"] = [p.reshape(-1, p.shape[-1]) for p in pair_ff]
    (delta["w_ff1"], new_m["w_ff1"], new_v["w_ff1"]), ((c_in, c_small),) = _adamw(
        w["w_ff1"], grads["w_ff1"], mom["w_ff1"], var["w_ff1"], "adamw_w_ff1", stages=[_to_chips([p_in, small2])])
    pair_in = _add_chips(p_in, c_in, idx_big, "add_chips_w_in")
    own_small = lax.dynamic_index_in_dim(small2, core, 0, keepdims=True)
    c_small = lax.dynamic_update_slice(c_small, own_small, (chip, zero, zero))
    pair_small = _add_chips(c_small, c_small, jnp.stack([zero, zero + 1, zero + 2, zero + 3, core]), "add_chips_small")
    (delta["w_ff2"], new_m["w_ff2"], new_v["w_ff2"]), (shared,) = _adamw(
        w["w_ff2"], grads["w_ff2"], mom["w_ff2"], var["w_ff2"], "adamw_w_ff2",
        stages=[_share([pair_in, pair_small] + pair_mix)])
    for n, p in zip(["w_in"] + mix, [shared[0]] + shared[2:]):
        grads[n] = p.reshape(-1, p.shape[-1])
        (delta[n], new_m[n], new_v[n]), _ = _adamw(w[n], grads[n], mom[n], var[n], "adamw_" + n)
    small_sum = shared[1].reshape(SMALL_ROWS, D)
    loss = 0.5 * small_sum[LOSS_ROW, 0]
    ccols = DR // NCHIP
    sep = [lax.dynamic_slice(small_sum[12:16], (zero, chip * ccols), (4, ccols)),
           small_sum[16:80].reshape(-1, hd), small_sum[80:144].reshape(-1, hd), small_sum[144:208].reshape(-1, PG)]
    g_s, d_s, m_s, v_s = _adamw_small(small_sum, sep, w, mom, var)
    grads.update(g_s)
    grads.update(dict(zip(SMALL_SEPARATE, sep)))
    delta.update(d_s)
    new_m.update(m_s)
    new_v.update(v_s)

    out = lambda d: [d[n].reshape(args[n].shape) for n in W_NAMES]
    return (loss, grad_x[None], *out(grads), *out(delta), *out(new_m), *out(new_v))
```

```python
import functools
import math

import jax
import jax.numpy as jnp
from jax import lax
from jax.experimental import pallas as pl
from jax.experimental.pallas import tpu as pltpu

F32 = jnp.float32
BF = jnp.bfloat16

T = 2048
D = 1024
DR = 1024
DP = 512
DF = 4096
DIN = 4608
NCHIP = 4
CW_IN = DIN // NCHIP
LANE = 128
CB = 128
NG = DR // CB
PG = 128
POOL_WINDOWS = (2, 4, 8, 16)
NORM_EPS = 1e-6
LRU_C = 8.0
GELU_C = math.sqrt(2.0 / math.pi)
ADAM_LR = 0.001
ADAM_B1 = 0.9
ADAM_B2 = 0.999
ADAM_EPS = 1e-08
ADAM_WD = 0.01
ADAM_STEP = 10
MESH_ID = pl.DeviceIdType.MESH
ANY = pl.BlockSpec(memory_space=pl.ANY)
SMALL_ROWS = 208
LOSS_ROW = 11
MIB = 1 << 20


def _cp(vmem_mib=None):
    if vmem_mib is None:
        return pltpu.CompilerParams()
    return pltpu.CompilerParams(vmem_limit_bytes=vmem_mib * MIB)


def _hbm(*arrays):
    return [pltpu.with_memory_space_constraint(a, pltpu.HBM) for a in arrays]


def _hbm_out(shapes):
    return [pltpu.HBM(s.shape, s.dtype) for s in shapes]


class _Stage:
    def __init__(self, operands, out_shape, alias, sems, start, finish):
        self.operands, self.out_shape, self.alias, self.sems = list(operands), list(out_shape), dict(alias), list(sems)
        self.start, self.finish = start, finish


def _call(body, *, name, grid, in_specs, out_specs, out_shape, args, vmem=None, stages=(), prefetch=None):
    nin, nout = len(in_specs), len(out_specs)
    npre = 0 if prefetch is None else 1
    st_args, st_shapes, st_sems, aliases = [], [], [], {}
    for st in stages:
        for k, v in st.alias.items():
            aliases[npre + nin + len(st_args) + k] = nout + len(st_shapes) + v
        st_args += st.operands
        st_shapes += st.out_shape
        st_sems += st.sems

    def wrapped(*refs):
        pre, refs = refs[:npre], refs[npre:]
        ins, pos = refs[:nin], nin
        st_ins = []
        for st in stages:
            st_ins.append(refs[pos:pos + len(st.operands)])
            pos += len(st.operands)
        outs, pos = refs[pos:pos + nout], pos + nout
        st_outs = []
        for st in stages:
            st_outs.append(refs[pos:pos + len(st.out_shape)])
            pos += len(st.out_shape)
        sems = []
        for st in stages:
            sems.append(refs[pos:pos + len(st.sems)])
            pos += len(st.sems)
        if stages:
            first = functools.reduce(jnp.logical_and, [pl.program_id(a) == 0 for a in range(len(grid))])

            @pl.when(first)
            def _():
                for st, a, b, s in zip(stages, st_ins, st_outs, sems):
                    st.start(a, b, s)

        body(*pre, *ins, *outs)
        if stages:
            last = functools.reduce(jnp.logical_and, [pl.program_id(a) == g - 1 for a, g in enumerate(grid)])

            @pl.when(last)
            def _():
                for st, a, b, s in zip(stages, st_ins, st_outs, sems):
                    st.finish(a, b, s)

    all_in = list(in_specs) + [ANY] * len(st_args)
    all_out = list(out_specs) + [ANY] * len(st_shapes)
    kw = dict(has_side_effects=True) if stages else {}
    if vmem is not None:
        kw["vmem_limit_bytes"] = vmem * MIB
    if prefetch is None:
        gkw = dict(grid=grid, in_specs=all_in, out_specs=all_out, scratch_shapes=st_sems)
    else:
        gkw = dict(grid_spec=pltpu.PrefetchScalarGridSpec(
            num_scalar_prefetch=1, grid=grid, in_specs=all_in, out_specs=all_out, scratch_shapes=st_sems))
    res = pl.pallas_call(
        wrapped, name=name, out_shape=_hbm_out(list(out_shape) + st_shapes), input_output_aliases=aliases,
        compiler_params=pltpu.CompilerParams(**kw), **gkw,
    )(*([prefetch] if npre else []), *_hbm(*args, *st_args))
    outs, rest, st_res = list(res[:nout]), list(res[nout:]), []
    for st in stages:
        st_res.append(rest[:len(st.out_shape)])
        rest = rest[len(st.out_shape):]
    return outs, st_res


def _mm(a, b):
    return jnp.dot(a.astype(BF), b.astype(BF), preferred_element_type=F32)


def _mm_nt(a, b):
    return lax.dot_general(a.astype(BF), b.astype(BF), (((1,), (1,)), ((), ())),
                           preferred_element_type=F32)


def _mm_tn(a, b):
    return lax.dot_general(a.astype(BF), b.astype(BF), (((0,), (0,)), ((), ())),
                           preferred_element_type=F32)


def _rows(v):
    return lax.broadcasted_iota(jnp.int32, v.shape, 0)


def _sd(v, s, fill=0.0):
    return jnp.where(_rows(v) >= s, pltpu.roll(v, s, axis=0), fill)


def _su(v, s, fill=0.0):
    n = v.shape[0]
    return jnp.where(_rows(v) < n - s, pltpu.roll(v, n - s, axis=0), fill)


def _sigmoid(z):
    return 1.0 / (1.0 + jnp.exp(-z))


def _softplus(z):
    e = jnp.exp(-jnp.abs(z))
    u = 1.0 + e
    d = u - 1.0
    log1p = jnp.where(d == 0.0, e, jnp.log(u) * (e / jnp.where(d == 0.0, 1.0, d)))
    return jnp.maximum(z, 0.0) + log1p


def _mean(v):
    return jnp.mean(v, axis=-1, keepdims=True)


def _colsum(v):
    return jnp.sum(v, axis=0, keepdims=True)


def _acc(ref, val, first):
    @pl.when(first)
    def _():
        ref[...] = val

    @pl.when(jnp.logical_not(first))
    def _():
        ref[...] += val


def _conv(xp, cw, cb):
    x1, x2, x3 = _sd(xp, 1), _sd(xp, 2), _sd(xp, 3)
    xc = cb + cw[0:1] * x3 + cw[1:2] * x2 + cw[2:3] * x1 + cw[3:4] * xp
    return xc, x1, x2, x3


def _lru_gates(xc, wa, ba, wx, bx, lam):
    xcb = xc.astype(BF)
    r = _sigmoid(_mm(xcb, wa) + ba)
    ii = _sigmoid(_mm(xcb, wx) + bx)
    sp = _softplus(-lam)
    la = (-LRU_C) * r * sp
    a = jnp.exp(la)
    mult = jnp.sqrt(-jnp.tanh(la) * (a * a + 1.0))
    return xcb, r, ii, sp, a, mult


def _gelu_parts(g):
    th = jnp.tanh(GELU_C * (g + 0.044715 * (g * g * g)))
    gel = 0.5 * g * (1.0 + th)
    dgel = 0.5 * (1.0 + th) + 0.5 * g * (1.0 - th * th) * (GELU_C * (1.0 + 3.0 * 0.044715 * (g * g)))
    return gel, dgel


def _pool_window(x, steps, shift):
    s, sh = x, 1
    for _ in range(steps):
        s = s + shift(s, sh)
        sh *= 2
    return s


def _fwd_inproj(x, g1, w_in, stages=()):
    tm = 512

    def body(x_ref, g_ref, w_ref, proj_ref, h_ref):
        @pl.when(pl.program_id(1) == 0)
        def _():
            xv = x_ref[...]
            r = lax.rsqrt(_mean(xv * xv) + NORM_EPS)
            h_ref[...] = ((xv * r) * g_ref[...]).astype(BF)

        proj_ref[...] = jnp.dot(h_ref[...], w_ref[0], preferred_element_type=F32)

    return _call(
        body, name="fwd_inproj", grid=(T // tm, NCHIP),
        in_specs=[pl.BlockSpec((tm, D), lambda i, k: (i, 0)),
                  pl.BlockSpec((1, D), lambda i, k: (0, 0)),
                  pl.BlockSpec((1, D, CW_IN), lambda i, k: (k, 0, 0))],
        out_specs=[pl.BlockSpec((tm, CW_IN), lambda i, k: (i, k)),
                   pl.BlockSpec((tm, D), lambda i, k: (i, 0))],
        out_shape=[jax.ShapeDtypeStruct((T, DIN), F32), jax.ShapeDtypeStruct((T, D), BF)],
        vmem=40, args=[x, g1, w_in], stages=stages)


def _vec_spec():
    return pl.BlockSpec((1, CB), lambda j: (0, j))


def _fwd_lru(proj, conv_w, conv_b, wa, ba, wx, bx, lam, stages=()):
    def body(xp_ref, g_ref, cw_ref, cb_ref, wa_ref, ba_ref, wx_ref, bx_ref, lam_ref, y_ref, h_ref):
        xc, _, _, _ = _conv(xp_ref[...], cw_ref[...], cb_ref[...])
        _, _, ii, _, a, mult = _lru_gates(xc, wa_ref[0], ba_ref[...], wx_ref[0], bx_ref[...], lam_ref[...])
        b = mult * (ii * xc)
        s = 1
        while s < T:
            b = b + a * _sd(b, s, 0.0)
            if 2 * s < T:
                a = a * _sd(a, s, 1.0)
            s *= 2
        h_ref[...] = b
        gel, _ = _gelu_parts(g_ref[...])
        y_ref[...] = (b * gel).astype(BF)

    return _call(
        body, name="fwd_lru", grid=(NG,),
        in_specs=[pl.BlockSpec((T, CB), lambda j: (0, j)),
                  pl.BlockSpec((T, CB), lambda j: (0, NG + j)),
                  pl.BlockSpec((4, CB), lambda j: (0, j)),
                  _vec_spec(),
                  pl.BlockSpec((1, CB, CB), lambda j: (j, 0, 0)), _vec_spec(),
                  pl.BlockSpec((1, CB, CB), lambda j: (j, 0, 0)), _vec_spec(),
                  _vec_spec()],
        out_specs=[pl.BlockSpec((T, CB), lambda j: (0, j)), pl.BlockSpec((T, CB), lambda j: (0, j))],
        out_shape=[jax.ShapeDtypeStruct((T, DR), BF), jax.ShapeDtypeStruct((T, DR), F32)],
        vmem=48, args=[proj, proj, conv_w, conv_b, wa, ba, wx, bx, lam], stages=stages)


def _pool_cnt(w):
    t = lax.broadcasted_iota(jnp.int32, (T, 1), 0)
    return jnp.minimum(t + 1, w).astype(F32)


def _fwd_pool(proj, pool_w, pool_scale):
    def body(xp_ref, pw_ref, sc_ref, y_ref):
        for g, w in enumerate(POOL_WINDOWS):
            cols = slice(g * PG, (g + 1) * PG)
            x = xp_ref[:, cols]
            p = _pool_window(x, g + 1, _sd) / _pool_cnt(w) - x
            y_ref[:, cols] = (_mm(p, pw_ref[g]) * sc_ref[:, cols]).astype(BF)

    return pl.pallas_call(
        body, name="fwd_pool", grid=(1,),
        in_specs=[pl.BlockSpec((T, DP), lambda i: (0, 2 * DR // DP)),
                  pl.BlockSpec((4, PG, PG), lambda i: (0, 0, 0)),
                  pl.BlockSpec((1, DP), lambda i: (0, 0))],
        out_specs=pl.BlockSpec((T, DP), lambda i: (0, 0)),
        out_shape=pltpu.HBM((T, DP), BF),
        compiler_params=_cp(48),
    )(*_hbm(proj, pool_w, pool_scale))


GATE_BLK = 512
GATE_BLK0 = (2 * DR + DP) // GATE_BLK


def _gate_specs(tm):
    return [pl.BlockSpec((tm, GATE_BLK), functools.partial(lambda i, q: (i, GATE_BLK0 + q), q=q))
            for q in range(4)]


def _fwd_merge(x, ylru, ypool, proj, b_gate, g2, g3, w_lru_up, w_pool_up, w_o, stages=()):
    tm = 512

    def body(x_ref, yl_ref, yp_ref, p0, p1, p2, p3, bg_ref, g2_ref, g3_ref, wl_ref, wp_ref, wo_ref,
             x2_ref, h2_ref, m_ref, mrg_ref, bra_ref, brb_ref):
        bra = jnp.dot(yl_ref[...], wl_ref[...], preferred_element_type=F32)
        yp = yp_ref[...]
        brb = jnp.concatenate([jnp.dot(yp, wp_ref[k], preferred_element_type=F32) for k in range(NCHIP)], axis=1)
        bg = bg_ref[...]
        ga = _sigmoid(jnp.concatenate([p0[...], p1[...]], axis=1) + bg[:, :D])
        gb = _sigmoid(jnp.concatenate([p2[...], p3[...]], axis=1) + bg[:, D:])
        mrg = (ga * bra + gb * brb).astype(BF)
        m = jnp.dot(mrg, wo_ref[...], preferred_element_type=F32)
        r2 = lax.rsqrt(_mean(m * m) + NORM_EPS)
        x2 = x_ref[...] + (m * r2) * g2_ref[...]
        r3 = lax.rsqrt(_mean(x2 * x2) + NORM_EPS)
        x2_ref[...] = x2
        h2_ref[...] = ((x2 * r3) * g3_ref[...]).astype(BF)
        m_ref[...] = m
        mrg_ref[...] = mrg
        bra_ref[...] = bra.astype(BF)
        brb_ref[...] = brb.astype(BF)

    row = lambda w: pl.BlockSpec((tm, w), lambda i: (i, 0))
    full2 = lambda a, b: pl.BlockSpec((a, b), lambda i: (0, 0))
    return _call(
        body, name="fwd_merge", grid=(T // tm,),
        in_specs=[row(D), row(DR), row(DP)] + _gate_specs(tm) +
                 [full2(1, 2 * D), full2(1, D), full2(1, D), full2(DR, D),
                  pl.BlockSpec((NCHIP, DP, D // NCHIP), lambda i: (0, 0, 0)), full2(D, D)],
        out_specs=[row(D)] * 6,
        out_shape=[jax.ShapeDtypeStruct((T, D), F32), jax.ShapeDtypeStruct((T, D), BF),
                   jax.ShapeDtypeStruct((T, D), F32), jax.ShapeDtypeStruct((T, D), BF),
                   jax.ShapeDtypeStruct((T, D), BF), jax.ShapeDtypeStruct((T, D), BF)],
        vmem=48, args=[x, ylru, ypool, proj, proj, proj, proj, b_gate, g2, g3, w_lru_up, w_pool_up, w_o],
        stages=stages)


FC = 256
FPC = (DF // NCHIP) // FC


def _fwd_mlp(h2, w_ff1, w_ff2):
    def body(h_ref, w1_ref, w2_ref, a1_ref, f_ref):
        a1 = jnp.maximum(jnp.dot(h_ref[...], w1_ref[0], preferred_element_type=F32), 0.0)
        a1_ref[...] = a1
        _acc(f_ref, jnp.dot((a1 * a1).astype(BF), w2_ref[...], preferred_element_type=F32),
             pl.program_id(0) == 0)

    return pl.pallas_call(
        body, name="fwd_mlp", grid=(DF // FC,),
        in_specs=[pl.BlockSpec((T, D), lambda j: (0, 0)),
                  pl.BlockSpec((1, D, FC), lambda j: (j // FPC, 0, j % FPC)),
                  pl.BlockSpec((FC, D), lambda j: (j, 0))],
        out_specs=[pl.BlockSpec((T, FC), lambda j: (0, j)), pl.BlockSpec((T, D), lambda j: (0, 0))],
        out_shape=_hbm_out([jax.ShapeDtypeStruct((T, DF), F32), jax.ShapeDtypeStruct((T, D), F32)]),
        compiler_params=_cp(48),
    )(*_hbm(h2, w_ff1, w_ff2))


def _loss_head(f, x2, target, g4):
    tm = 512

    def body(f_ref, x2_ref, t_ref, g_ref, loss_ref, dy_ref, df_ref, dg_ref):
        first = pl.program_id(0) == 0
        f = f_ref[...]
        g4v = g_ref[...]
        r4 = lax.rsqrt(_mean(f * f) + NORM_EPS)
        fn = f * r4
        e = (x2_ref[...] + fn * g4v) - t_ref[...]
        _acc(loss_ref, jnp.sum(_mean(e * e), axis=0, keepdims=True), first)
        dy = e * (1.0 / D)
        dy_ref[...] = dy
        _acc(dg_ref, _colsum(dy * fn), first)
        dfn = dy * g4v
        df_ref[...] = (r4 * (dfn - fn * _mean(dfn * fn))).astype(BF)

    row = pl.BlockSpec((tm, D), lambda i: (i, 0))
    return pl.pallas_call(
        body, name="loss_head", grid=(T // tm,),
        in_specs=[row, row, row, pl.BlockSpec((1, D), lambda i: (0, 0))],
        out_specs=[pl.BlockSpec((1, 1), lambda i: (0, 0)), row, row, pl.BlockSpec((1, D), lambda i: (0, 0))],
        out_shape=_hbm_out([jax.ShapeDtypeStruct((1, 1), F32), jax.ShapeDtypeStruct((T, D), F32),
                            jax.ShapeDtypeStruct((T, D), BF), jax.ShapeDtypeStruct((1, D), F32)]),
        compiler_params=_cp(48),
    )(*_hbm(f, x2, target, g4))


def _bwd_mlp(df, h2, a1, w_ff1, w_ff2):
    def body(df_ref, h_ref, a1_ref, w1_ref, w2_ref, dw1_ref, dw2_ref, dh_ref):
        df = df_ref[...]
        a1 = a1_ref[...]
        dact = _mm_nt(df, w2_ref[...])
        df1 = (dact * (2.0 * a1)).astype(BF)
        dw2_ref[...] = _mm_tn((a1 * a1).astype(BF), df)
        dw1_ref[0] = _mm_tn(h_ref[...], df1)
        _acc(dh_ref, _mm_nt(df1, w1_ref[0]), pl.program_id(0) == 0)

    return pl.pallas_call(
        body, name="bwd_mlp", grid=(DF // FC,),
        in_specs=[pl.BlockSpec((T, D), lambda j: (0, 0)),
                  pl.BlockSpec((T, D), lambda j: (0, 0)),
                  pl.BlockSpec((T, FC), lambda j: (0, j)),
                  pl.BlockSpec((1, D, FC), lambda j: (j // FPC, 0, j % FPC)),
                  pl.BlockSpec((FC, D), lambda j: (j, 0))],
        out_specs=[pl.BlockSpec((1, D, FC), lambda j: (j // FPC, 0, j % FPC)),
                   pl.BlockSpec((FC, D), lambda j: (j, 0)),
                   pl.BlockSpec((T, D), lambda j: (0, 0))],
        out_shape=_hbm_out([jax.ShapeDtypeStruct((NCHIP, D, DF // NCHIP), F32),
                            jax.ShapeDtypeStruct((DF, D), F32), jax.ShapeDtypeStruct((T, D), F32)]),
        compiler_params=_cp(56),
    )(*_hbm(df, h2, a1, w_ff1, w_ff2))


def _bwd_merge(dh2, dy, x2, m, mrg, bra, brb, proj, b_gate, ylru, ypool, g2, g3, w_lru_up, w_pool_up, w_o,
               stages=()):
    tm = 256
    cpu = D // NCHIP

    def body(dh2_ref, dy_ref, x2_ref, m_ref, mrg_ref, bra_ref, brb_ref, p0, p1, p2, p3, bg_ref, yl_ref, yp_ref,
             g2_ref, g3_ref, wl_ref, wp_ref, wo_ref,
             dx_ref, dgt_ref, dyl_ref, dyp_ref, dwo_ref, dwl_ref, dwp_ref, dg2_ref, dg3_ref, dbg_ref):
        first = pl.program_id(0) == 0
        x2 = x2_ref[...]
        r3 = lax.rsqrt(_mean(x2 * x2) + NORM_EPS)
        x2n = x2 * r3
        dh2 = dh2_ref[...]
        t3 = dh2 * g3_ref[...]
        dx2 = dy_ref[...] + r3 * (t3 - x2n * _mean(t3 * x2n))
        dx_ref[...] = dx2
        _acc(dg3_ref, _colsum(dh2 * x2n), first)
        m = m_ref[...]
        r2 = lax.rsqrt(_mean(m * m) + NORM_EPS)
        mn = m * r2
        _acc(dg2_ref, _colsum(dx2 * mn), first)
        dmn = dx2 * g2_ref[...]
        dm = (r2 * (dmn - mn * _mean(dmn * mn))).astype(BF)
        dmrg = _mm_nt(dm, wo_ref[...])
        _acc(dwo_ref, _mm_tn(mrg_ref[...], dm), first)
        bg = bg_ref[...]
        ga = _sigmoid(jnp.concatenate([p0[...], p1[...]], axis=1) + bg[:, :D])
        gb = _sigmoid(jnp.concatenate([p2[...], p3[...]], axis=1) + bg[:, D:])
        dga = dmrg * bra_ref[...].astype(F32) * (ga * (1.0 - ga))
        dgb = dmrg * brb_ref[...].astype(F32) * (gb * (1.0 - gb))
        dgt_ref[:, :D] = dga.astype(BF)
        dgt_ref[:, D:] = dgb.astype(BF)
        _acc(dbg_ref, jnp.concatenate([_colsum(dga), _colsum(dgb)], axis=1), first)
        dbra = (dmrg * ga).astype(BF)
        dbrb = (dmrg * gb).astype(BF)
        dyl_ref[...] = _mm_nt(dbra, wl_ref[...])
        _acc(dwl_ref, _mm_tn(yl_ref[...], dbra), first)
        yp = yp_ref[...]
        dyp = None
        for k in range(NCHIP):
            dk = dbrb[:, k * cpu:(k + 1) * cpu]
            part = _mm_nt(dk, wp_ref[k])
            dyp = part if dyp is None else dyp + part
            _acc(dwp_ref.at[k], _mm_tn(yp, dk), first)
        dyp_ref[...] = dyp

    row = lambda w: pl.BlockSpec((tm, w), lambda i: (i, 0))
    full2 = lambda a, b: pl.BlockSpec((a, b), lambda i: (0, 0))
    wp_spec = pl.BlockSpec((NCHIP, DP, cpu), lambda i: (0, 0, 0))
    return _call(
        body, name="bwd_merge", grid=(T // tm,),
        in_specs=[row(D)] * 7 + _gate_specs(tm) +
                 [full2(1, 2 * D), row(DR), row(DP), full2(1, D), full2(1, D), full2(DR, D), wp_spec, full2(D, D)],
        out_specs=[row(D), row(2 * D), row(DR), row(DP), full2(D, D), full2(DR, D), wp_spec,
                   full2(1, D), full2(1, D), full2(1, 2 * D)],
        out_shape=[jax.ShapeDtypeStruct((T, D), F32), jax.ShapeDtypeStruct((T, 2 * D), BF),
                   jax.ShapeDtypeStruct((T, DR), F32), jax.ShapeDtypeStruct((T, DP), F32),
                   jax.ShapeDtypeStruct((D, D), F32), jax.ShapeDtypeStruct((DR, D), F32),
                   jax.ShapeDtypeStruct((NCHIP, DP, cpu), F32),
                   jax.ShapeDtypeStruct((1, D), F32), jax.ShapeDtypeStruct((1, D), F32),
                   jax.ShapeDtypeStruct((1, 2 * D), F32)],
        vmem=56, args=[dh2, dy, x2, m, mrg, bra, brb, proj, proj, proj, proj, b_gate, ylru, ypool, g2, g3, w_lru_up,
                       w_pool_up, w_o], stages=stages)


def _bwd_lru(proj, h, dylru, conv_w, conv_b, wa, ba, wx, bx, lam, stages=()):
    def body(xp_ref, g_ref, h_ref, dy_ref, cw_ref, cb_ref, wa_ref, ba_ref, wx_ref, bx_ref, lam_ref,
             dxp_ref, dg_ref, dcw_ref, dcb_ref, dwa_ref, dba_ref, dwx_ref, dbx_ref, dlam_ref):
        xp = xp_ref[...]
        cw = cw_ref[...]
        lam = lam_ref[...]
        xc, x1, x2, x3 = _conv(xp, cw, cb_ref[...])
        wa, wx = wa_ref[0], wx_ref[0]
        xcb, r, ii, sp, a, mult = _lru_gates(xc, wa, ba_ref[...], wx, bx_ref[...], lam)
        g = g_ref[...]
        gel, dgel = _gelu_parts(g)
        h = h_ref[...]
        dy = dy_ref[...]
        dg_ref[...] = (dy * h * dgel).astype(BF)
        b = dy * gel
        aa = _su(a, 1, 0.0)
        s = 1
        while s < T:
            b = b + aa * _su(b, s, 0.0)
            if 2 * s < T:
                aa = aa * _su(aa, s, 0.0)
            s *= 2
        da = b * _sd(h, 1, 0.0)
        dmult = b * (ii * xc)
        dii = b * (mult * xc)
        dxc = b * (mult * ii)
        dla = da * a - dmult * ((a * a) / mult)
        dr = dla * ((-LRU_C) * sp)
        dsp = _colsum(dla * ((-LRU_C) * r))
        dlam_ref[...] = -dsp / (1.0 + jnp.exp(lam))
        dzr = dr * (r * (1.0 - r))
        dzi = dii * (ii * (1.0 - ii))
        dzrb, dzib = dzr.astype(BF), dzi.astype(BF)
        dxc = dxc + _mm_nt(dzrb, wa) + _mm_nt(dzib, wx)
        dwa_ref[0] = _mm_tn(xcb, dzrb)
        dwx_ref[0] = _mm_tn(xcb, dzib)
        dba_ref[...] = _colsum(dzr)
        dbx_ref[...] = _colsum(dzi)
        dcb_ref[...] = _colsum(dxc)
        dcw_ref[...] = jnp.concatenate([_colsum(dxc * x3), _colsum(dxc * x2), _colsum(dxc * x1),
                                        _colsum(dxc * xp)], axis=0)
        dxp = cw[3:4] * dxc + cw[2:3] * _su(dxc, 1) + cw[1:2] * _su(dxc, 2) + cw[0:1] * _su(dxc, 3)
        dxp_ref[...] = dxp.astype(BF)

    blk = pl.BlockSpec((T, CB), lambda j: (0, j))
    wsp = pl.BlockSpec((1, CB, CB), lambda j: (j, 0, 0))
    return _call(
        body, name="bwd_lru", grid=(NG,),
        in_specs=[blk, pl.BlockSpec((T, CB), lambda j: (0, NG + j)), blk, blk,
                  pl.BlockSpec((4, CB), lambda j: (0, j)), _vec_spec(), wsp, _vec_spec(), wsp, _vec_spec(),
                  _vec_spec()],
        out_specs=[blk, blk, pl.BlockSpec((4, CB), lambda j: (0, j)), _vec_spec(), wsp, _vec_spec(), wsp,
                   _vec_spec(), _vec_spec()],
        out_shape=[jax.ShapeDtypeStruct((T, DR), BF), jax.ShapeDtypeStruct((T, DR), BF),
                   jax.ShapeDtypeStruct((4, DR), F32), jax.ShapeDtypeStruct((1, DR), F32),
                   jax.ShapeDtypeStruct((NG, CB, CB), F32), jax.ShapeDtypeStruct((1, DR), F32),
                   jax.ShapeDtypeStruct((NG, CB, CB), F32), jax.ShapeDtypeStruct((1, DR), F32),
                   jax.ShapeDtypeStruct((1, DR), F32)],
        vmem=56, args=[proj, proj, h, dylru, conv_w, conv_b, wa, ba, wx, bx, lam], stages=stages)


def _bwd_pool(proj, dypool, pool_w, pool_scale):
    def body(xp_ref, dy_ref, pw_ref, sc_ref, dx_ref, dw_ref, dsc_ref):
        for g, w in enumerate(POOL_WINDOWS):
            cols = slice(g * PG, (g + 1) * PG)
            cnt = _pool_cnt(w)
            x = xp_ref[:, cols]
            pb = (_pool_window(x, g + 1, _sd) / cnt - x).astype(BF)
            wg = pw_ref[g]
            dy = dy_ref[:, cols]
            dsc_ref[:, cols] = _colsum(dy * _mm(pb, wg))
            dyp = (dy * sc_ref[:, cols]).astype(BF)
            dw_ref[g] = _mm_tn(pb, dyp)
            dp = _mm_nt(dyp, wg)
            dx_ref[:, cols] = (_pool_window(dp / cnt, g + 1, _su) - dp).astype(BF)

    return pl.pallas_call(
        body, name="bwd_pool", grid=(1,),
        in_specs=[pl.BlockSpec((T, DP), lambda i: (0, 2 * DR // DP)),
                  pl.BlockSpec((T, DP), lambda i: (0, 0)),
                  pl.BlockSpec((4, PG, PG), lambda i: (0, 0, 0)),
                  pl.BlockSpec((1, DP), lambda i: (0, 0))],
        out_specs=[pl.BlockSpec((T, DP), lambda i: (0, 0)),
                   pl.BlockSpec((4, PG, PG), lambda i: (0, 0, 0)),
                   pl.BlockSpec((1, DP), lambda i: (0, 0))],
        out_shape=_hbm_out([jax.ShapeDtypeStruct((T, DP), BF), jax.ShapeDtypeStruct((4, PG, PG), F32),
                            jax.ShapeDtypeStruct((1, DP), F32)]),
        compiler_params=_cp(48),
    )(*_hbm(proj, dypool, pool_w, pool_scale))


def _bwd_inproj(h1, dproj, w_in, stages=()):
    def body(h_ref, dp_ref, w_ref, dw_ref, dh_ref):
        dp = dp_ref[...]
        dw_ref[0] = _mm_tn(h_ref[...], dp)
        _acc(dh_ref, _mm_nt(dp, w_ref[0]), pl.program_id(0) == 0)

    return _call(
        body, name="bwd_inproj", grid=(NCHIP,),
        in_specs=[pl.BlockSpec((T, D), lambda k: (0, 0)),
                  pl.BlockSpec((T, CW_IN), lambda k: (0, k)),
                  pl.BlockSpec((1, D, CW_IN), lambda k: (k, 0, 0))],
        out_specs=[pl.BlockSpec((1, D, CW_IN), lambda k: (k, 0, 0)), pl.BlockSpec((T, D), lambda k: (0, 0))],
        out_shape=[jax.ShapeDtypeStruct((NCHIP, D, CW_IN), F32), jax.ShapeDtypeStruct((T, D), F32)],
        vmem=56, args=[h1, dproj, w_in], stages=stages)


def _bwd_prenorm(x, dh1, dxres, g1, stages=()):
    tm = 512

    def body(x_ref, dh_ref, dr_ref, g_ref, dx_ref, dg_ref):
        xv = x_ref[...]
        r = lax.rsqrt(_mean(xv * xv) + NORM_EPS)
        xn = xv * r
        dh = dh_ref[...]
        t = dh * g_ref[...]
        dx_ref[...] = dr_ref[...] + r * (t - xn * _mean(t * xn))
        _acc(dg_ref, _colsum(dh * xn), pl.program_id(0) == 0)

    row = pl.BlockSpec((tm, D), lambda i: (i, 0))
    vec = pl.BlockSpec((1, D), lambda i: (0, 0))
    return _call(
        body, name="bwd_prenorm", grid=(T // tm,),
        in_specs=[row, row, row, vec], out_specs=[row, vec],
        out_shape=[jax.ShapeDtypeStruct((T, D), F32), jax.ShapeDtypeStruct((1, D), F32)],
        vmem=48, args=[x, dh1, dxres, g1], stages=stages)


def _place():
    x, y, c = lax.axis_index("x"), lax.axis_index("y"), lax.axis_index("c")
    chips = [(1 - x, y), (x, 1 - y), (1 - x, 1 - y)]
    return x, y, c, chips


def _rcopy(src, dst, ssem, rsem, dev):
    return pltpu.make_async_remote_copy(src_ref=src, dst_ref=dst, send_sem=ssem, recv_sem=rsem,
                                        device_id=dev, device_id_type=MESH_ID)


def _sds(a):
    return jax.ShapeDtypeStruct(a.shape, a.dtype)


def _sem2(n, m):
    return [pltpu.SemaphoreType.DMA((n, m)), pltpu.SemaphoreType.DMA((n, m))]


ALL = (0, 1, 1)


def _piece(ref, k, half, part):
    hr = ref.shape[1] // 2
    r0, r1 = hr * part[0] // part[2], hr * part[1] // part[2]
    return ref.at[k, pl.ds(half * hr + r0, r1 - r0), :]


def _gather(fulls, ici=(), d2d=()):
    n = len(fulls)
    ici, d2d = list(ici), list(d2d)

    def copies(outs, sems):
        x, y, c, chips = _place()
        me = 2 * x + y
        sib = (x, y, 1 - c)
        send, recv = [], []
        for q, (i, part) in enumerate(ici):
            for j, chip in enumerate(chips):
                mine, theirs = _piece(outs[i], me, c, part), _piece(outs[i], 2 * chip[0] + chip[1], c, part)
                send.append(_rcopy(mine, mine, sems[0].at[q, j], sems[1].at[q, j], (*chip, c)))
                recv.append(_rcopy(theirs, theirs, sems[0].at[q, j], sems[1].at[q, j], (*chip, c)))
        for q, (i, part) in enumerate(d2d):
            for j, chip in enumerate(chips):
                k = 2 * chip[0] + chip[1]
                got, other = _piece(outs[i], k, c, part), _piece(outs[i], k, 1 - c, part)
                send.append(_rcopy(got, got, sems[2].at[q, j], sems[3].at[q, j], sib))
                recv.append(_rcopy(other, other, sems[2].at[q, j], sems[3].at[q, j], sib))
        return send, recv

    def start(ins, outs, sems):
        for cp in copies(outs, sems)[0]:
            cp.start()

    def finish(ins, outs, sems):
        send, recv = copies(outs, sems)
        for cp in recv:
            cp.wait_recv()
        for cp in send:
            cp.wait_send()

    sems = _sem2(max(len(ici), 1), 3) + _sem2(max(len(d2d), 1), 3)
    return _Stage(fulls, [_sds(f) for f in fulls], {i: i for i in range(n)}, sems, start, finish)


def _gather_first(full, conv_w):
    ici, d2d = _gather([full], ici=[(0, ALL)]), _gather([full], d2d=[(0, ALL)])

    def body(full_in, cw_in, full_out, cw_out, s0, r0, s1, r1, cs, cr):
        x, y, c, chips = _place()
        me = 2 * x + y
        conv = [_rcopy(cw_in, cw_out.at[me], cs.at[j], cr.at[j], (*chip, c)) for j, chip in enumerate(chips)]
        for cp in conv:
            cp.start()
        sems = [s0, r0, s1, r1]
        ici.start(None, [full_out], sems)
        ici.finish(None, [full_out], sems)
        d2d.start(None, [full_out], sems)
        d2d.finish(None, [full_out], sems)
        for j, chip in enumerate(chips):
            _rcopy(cw_in, cw_out.at[2 * chip[0] + chip[1]], cs.at[j], cr.at[j], (*chip, c)).wait_recv()
        for cp in conv:
            cp.wait_send()

    return pl.pallas_call(
        body, name="gather_first",
        in_specs=[ANY, ANY], out_specs=[ANY, ANY],
        out_shape=_hbm_out([full, jax.ShapeDtypeStruct((NCHIP,) + conv_w.shape, conv_w.dtype)]),
        input_output_aliases={0: 0},
        scratch_shapes=_sem2(1, 3) + _sem2(1, 3) + [pltpu.SemaphoreType.DMA((3,)), pltpu.SemaphoreType.DMA((3,))],
        compiler_params=pltpu.CompilerParams(has_side_effects=True),
    )(*_hbm(full, conv_w))


def _comm_only(name, stage):
    def body(*refs):
        ni, no = len(stage.operands), len(stage.out_shape)
        stage.start(refs[:ni], refs[ni:ni + no], refs[ni + no:])
        stage.finish(refs[:ni], refs[ni:ni + no], refs[ni + no:])

    return pl.pallas_call(
        body, name=name, in_specs=[ANY] * len(stage.operands), out_specs=[ANY] * len(stage.out_shape),
        out_shape=_hbm_out(stage.out_shape), input_output_aliases=stage.alias, scratch_shapes=stage.sems,
        compiler_params=pltpu.CompilerParams(has_side_effects=True),
    )(*_hbm(*stage.operands))


def _to_sibling(srcs):
    n = len(srcs)

    def copies(ins, outs, sems):
        x, y, c, _ = _place()
        sib = (x, y, 1 - c)
        return [_rcopy(ins[i].at[:, 1 - c] if srcs[i].ndim == 4 else ins[i], outs[i], sems[0].at[i], sems[1].at[i], sib)
                for i in range(n)]

    def start(ins, outs, sems):
        for cp in copies(ins, outs, sems):
            cp.start()

    def finish(ins, outs, sems):
        for cp in copies(ins, outs, sems):
            cp.wait()

    shapes = [jax.ShapeDtypeStruct((NCHIP,) + s.shape[2:] if s.ndim == 4 else s.shape, s.dtype) for s in srcs]
    return _Stage(srcs, shapes, {}, [pltpu.SemaphoreType.DMA((n,)), pltpu.SemaphoreType.DMA((n,))], start, finish)


def _to_chips(srcs):
    n = len(srcs)

    def copies(ins, outs, sems):
        x, y, c, chips = _place()
        me = 2 * x + y
        return [_rcopy(ins[i].at[2 * chip[0] + chip[1]] if srcs[i].shape[0] == NCHIP else ins[i].at[c],
                       outs[i].at[me], sems[0].at[i, j], sems[1].at[i, j], (*chip, c))
                for i in range(n) for j, chip in enumerate(chips)]

    def start(ins, outs, sems):
        for cp in copies(ins, outs, sems):
            cp.start()

    def finish(ins, outs, sems):
        for cp in copies(ins, outs, sems):
            cp.wait()

    shapes = [jax.ShapeDtypeStruct((NCHIP,) + s.shape[1:], s.dtype) for s in srcs]
    return _Stage(srcs, shapes, {}, _sem2(n, 3), start, finish)


def _share(pairs):
    n = len(pairs)

    def start(ins, outs, sems):
        x, y, c, _ = _place()
        for i in range(n):
            _rcopy(outs[i].at[c], outs[i].at[c], sems[0].at[i], sems[1].at[i], (x, y, 1 - c)).start()

    def finish(ins, outs, sems):
        x, y, c, _ = _place()
        for i in range(n):
            _rcopy(outs[i].at[c], outs[i].at[c], sems[0].at[i], sems[1].at[i], (x, y, 1 - c)).wait_send()
            _rcopy(outs[i].at[1 - c], outs[i].at[1 - c], sems[0].at[i], sems[1].at[i], (x, y, 1 - c)).wait_recv()

    return _Stage(pairs, [_sds(p) for p in pairs], {i: i for i in range(n)},
                  [pltpu.SemaphoreType.DMA((n,)), pltpu.SemaphoreType.DMA((n,))], start, finish)


def _row_block(rows, cols, itemsize=4, target=MIB):
    br = rows
    while br * cols * itemsize > target and br % 16 == 0:
        br //= 2
    return br


def _cast_place(w, chip_idx, name):
    rows, cols = w.shape
    br = _row_block(rows, cols)

    def body(k_ref, w_ref, o_ref):
        o_ref[0] = w_ref[...].astype(BF)

    return _call(
        body, name=name, grid=(rows // br,), prefetch=chip_idx,
        in_specs=[pl.BlockSpec((br, cols), lambda r, k: (r, 0))],
        out_specs=[pl.BlockSpec((1, br, cols), lambda r, k: (k[0], r, 0))],
        out_shape=[jax.ShapeDtypeStruct((NCHIP, rows, cols), BF)], vmem=32, args=[w])[0][0]


def _add_sibling(g, land, cidx, name, stages=()):
    _, _, hr, cols = g.shape
    br = _row_block(hr, cols)

    def body(c_ref, g_ref, l_ref, o_ref):
        o_ref[...] = (g_ref[0, 0] + l_ref[0]).astype(BF)[None]

    outs, st = _call(
        body, name=name, grid=(NCHIP, hr // br), prefetch=cidx,
        in_specs=[pl.BlockSpec((1, 1, br, cols), lambda k, r, c: (k, c[0], r, 0)),
                  pl.BlockSpec((1, br, cols), lambda k, r, c: (k, r, 0))],
        out_specs=[pl.BlockSpec((1, br, cols), lambda k, r, c: (k, r, 0))],
        out_shape=[jax.ShapeDtypeStruct((NCHIP, hr, cols), BF)], vmem=32, args=[g, land], stages=stages)
    return outs[0], st


def _add_pair(a, b, name):
    rows, cols = a.shape

    def body(a_ref, b_ref, o_ref):
        o_ref[...] = a_ref[...] + b_ref[...]

    spec = pl.BlockSpec((rows, cols), lambda r: (0, 0))
    return _call(body, name=name, grid=(1,), in_specs=[spec, spec], out_specs=[spec], out_shape=[_sds(a)],
                 vmem=32, args=[a, b])[0][0]


def _add_chips(own, land, idx, name):
    _, hr, cols = land.shape
    br = _row_block(hr, cols)

    def body(s_ref, a_ref, b_ref, c_ref, d_ref, o_ref):
        o_ref[...] = (a_ref[...].astype(F32) + b_ref[...].astype(F32)) + (c_ref[...].astype(F32) +
                                                                           d_ref[...].astype(F32))

    spec = lambda q: pl.BlockSpec((1, br, cols), functools.partial(lambda r, s, q: (s[q], r, 0), q=q))
    return _call(
        body, name=name, grid=(hr // br,), prefetch=idx,
        in_specs=[spec(0), spec(1), spec(2), spec(3)], out_specs=[spec(4)],
        out_shape=[jax.ShapeDtypeStruct((2, hr, cols), F32)], vmem=32, args=[own, land, land, land])[0][0]


def _adamw_math(w, g, m, v):
    mn = ADAM_B1 * m + (1.0 - ADAM_B1) * g
    vn = ADAM_B2 * v + (1.0 - ADAM_B2) * (g * g)
    m_hat = mn / (1.0 - ADAM_B1 ** ADAM_STEP)
    v_hat = vn / (1.0 - ADAM_B2 ** ADAM_STEP)
    return -ADAM_LR * (m_hat / (jnp.sqrt(v_hat) + ADAM_EPS) + ADAM_WD * w), mn, vn


def _adamw(w, g, m, v, name, stages=()):
    rows, cols = w.shape
    br = _row_block(rows, cols)

    def body(w_ref, g_ref, m_ref, v_ref, d_ref, mo_ref, vo_ref):
        d_ref[...], mo_ref[...], vo_ref[...] = _adamw_math(w_ref[...], g_ref[...], m_ref[...], v_ref[...])

    spec = pl.BlockSpec((br, cols), lambda r: (r, 0))
    return _call(body, name=name, grid=(rows // br,), in_specs=[spec] * 4, out_specs=[spec] * 3,
                 out_shape=[_sds(w)] * 3, vmem=32, args=[w, g, m, v], stages=stages)


SMALL_AT = {"norm_mix_pre": (0, 1, D), "norm_mix_post": (1, 1, D), "norm_mlp_pre": (2, 1, D),
            "norm_mlp_post": (3, 1, D), "b_gate": (4, 2, D), "conv_b": (6, 1, D), "lru_b_a": (7, 1, D),
            "lru_b_x": (8, 1, D), "lru_lambda": (9, 1, D), "pool_scale": (10, 1, DP)}
SMALL_SEPARATE = ["conv_w", "lru_w_a", "lru_w_x", "pool_w"]


def _adamw_small(small_sum, sep_grads, w, m, v):
    packed, sep = list(SMALL_AT), list(SMALL_SEPARATE)
    names = packed + sep

    def body(*refs):
        s_ref, refs = refs[0], refs[1:]
        g_sep, refs = refs[:len(sep)], refs[len(sep):]
        nn = len(names)
        w_r, m_r, v_r, refs = refs[:nn], refs[nn:2 * nn], refs[2 * nn:3 * nn], refs[3 * nn:]
        g_out, refs = refs[:len(packed)], refs[len(packed):]
        d_o, m_o, v_o = refs[:nn], refs[nn:2 * nn], refs[2 * nn:3 * nn]
        for i, n in enumerate(names):
            if n in SMALL_AT:
                r0, nr, nc = SMALL_AT[n]
                g = jnp.concatenate([s_ref[r0 + q:r0 + q + 1, :nc] for q in range(nr)], axis=1)
                g_out[i][...] = g
            else:
                g = g_sep[i - len(packed)][...]
            d_o[i][...], m_o[i][...], v_o[i][...] = _adamw_math(w_r[i][...], g, m_r[i][...], v_r[i][...])

    ws = [w[n] for n in names]
    res = pl.pallas_call(
        body, name="adamw_small",
        out_shape=[_sds(w[n]) for n in packed] + [_sds(a) for a in ws] * 3,
        compiler_params=_cp(32),
    )(*_hbm(small_sum, *sep_grads, *ws, *[m[n] for n in names], *[v[n] for n in names]))
    nn, npk = len(names), len(packed)
    grad = dict(zip(packed, res[:npk]))
    delta = dict(zip(names, res[npk:npk + nn]))
    new_m = dict(zip(names, res[npk + nn:npk + 2 * nn]))
    new_v = dict(zip(names, res[npk + 2 * nn:]))
    return grad, delta, new_m, new_v


W_NAMES = ["norm_mix_pre", "norm_mix_post", "norm_mlp_pre", "norm_mlp_post", "w_in", "b_gate", "conv_w", "conv_b",
           "lru_w_a", "lru_b_a", "lru_w_x", "lru_b_x", "lru_lambda", "pool_w", "pool_scale", "w_lru_up",
           "w_pool_up", "w_o", "w_ff1", "w_ff2"]
BIG = ["w_in", "w_lru_up", "w_pool_up", "w_o", "w_ff1", "w_ff2"]


def _block_diag(w):
    hd = w.shape[-1]
    per = CB // hd
    w4 = w.reshape(NG, per, hd, hd)
    eye = jnp.eye(per, dtype=w.dtype)
    return jnp.einsum("gpij,pq->gpiqj", w4, eye).reshape(NG, CB, CB)


def _block_diag_extract(d, hd):
    per = CB // hd
    d5 = d.reshape(NG, per, hd, per, hd)
    return jnp.stack([d5[:, p, :, p, :] for p in range(per)], axis=1).reshape(NG * per, hd, hd)


def _halves(g):
    return g.reshape(NCHIP, 2, g.size // (g.shape[-1] * 2 * NCHIP), g.shape[-1])


def kernel(x, norm_mix_pre, norm_mix_post, norm_mlp_pre, norm_mlp_post, w_in, b_gate, conv_w, conv_b, lru_w_a, lru_b_a, lru_w_x, lru_b_x, lru_lambda, pool_w, pool_scale, w_lru_up, w_pool_up, w_o, w_ff1, w_ff2, loss_target, m_norm_mix_pre, m_norm_mix_post, m_norm_mlp_pre, m_norm_mlp_post, m_w_in, m_b_gate, m_conv_w, m_conv_b, m_lru_w_a, m_lru_b_a, m_lru_w_x, m_lru_b_x, m_lru_lambda, m_pool_w, m_pool_scale, m_w_lru_up, m_w_pool_up, m_w_o, m_w_ff1, m_w_ff2, v_norm_mix_pre, v_norm_mix_post, v_norm_mlp_pre, v_norm_mlp_post, v_w_in, v_b_gate, v_conv_w, v_conv_b, v_lru_w_a, v_lru_b_a, v_lru_w_x, v_lru_b_x, v_lru_lambda, v_pool_w, v_pool_scale, v_w_lru_up, v_w_pool_up, v_w_o, v_w_ff1, v_w_ff2):
    args = dict(locals())
    two_d = lambda a: a.reshape(-1, a.shape[-1])
    w = {n: two_d(args[n]) for n in W_NAMES}
    mom = {n: two_d(args["m_" + n]) for n in W_NAMES}
    var = {n: two_d(args["v_" + n]) for n in W_NAMES}
    i32 = lambda val: jnp.asarray(val, jnp.int32)
    chip = i32(2 * lax.axis_index("x") + lax.axis_index("y"))
    core = i32(lax.axis_index("c"))
    cidx = core.reshape(1)
    zero = i32(0)
    hd = lru_w_a.shape[-1]
    xs, target = x[0], loss_target[0]
    g1, g2, g3, g4 = norm_mix_pre, norm_mix_post, norm_mlp_pre, norm_mlp_post

    full = {n: _cast_place(w[n], chip.reshape(1), "cast_" + n) for n in BIG}
    wa = _block_diag(lru_w_a[0]).astype(BF)
    wx = _block_diag(lru_w_x[0]).astype(BF)
    pw = pool_w[0].astype(BF)

    full["w_in"], conv_all = _gather_first(full["w_in"], w["conv_w"])
    conv_all = lax.dynamic_update_slice(conv_all, w["conv_w"][None], (chip, zero, zero))
    conv_full = jnp.transpose(conv_all, (1, 0, 2)).reshape(4, DR)
    mix = ["w_lru_up", "w_pool_up", "w_o"]
    ff1_a, ff1_b, ff2_a, ff2_b = (0, 3, 8), (3, 8, 8), (0, 1, 4), (1, 4, 4)
    (proj, h1), (got,) = _fwd_inproj(xs, g1, full["w_in"], stages=[_gather(
        [full[n] for n in mix] + [full["w_ff1"]], ici=[(0, ALL), (1, ALL), (2, ALL), (3, ff1_a)])])
    (ylru, hs), (got,) = _fwd_lru(proj, conv_full, conv_b, wa, lru_b_a, wx, lru_b_x, lru_lambda, stages=[_gather(
        got + [full["w_ff2"]], d2d=[(0, ALL), (1, ALL), (2, ALL), (3, ff1_a)], ici=[(3, ff1_b), (4, ff2_a)])])
    w_lru_up_f, w_pool_up_f, w_o_f = got[0].reshape(DR, D), got[1], got[2].reshape(D, D)
    ypool = _fwd_pool(proj, pw, pool_scale)
    (x2, h2, m, mrg, bra, brb), ((ff1, ff2),) = _fwd_merge(
        xs, ylru, ypool, proj, b_gate, g2, g3, w_lru_up_f, w_pool_up_f, w_o_f,
        stages=[_gather(got[3:], d2d=[(0, ff1_b), (1, ff2_a)], ici=[(1, ff2_b)])])
    ff2 = _comm_only("gather_last", _gather([ff2], d2d=[(0, ff2_b)]))[0].reshape(DF, D)
    a1, f = _fwd_mlp(h2, ff1, ff2)
    lossp, dy, df, dg4 = _loss_head(f, x2, target, g4)

    idx_big = jnp.stack([chip, (chip + 1) % NCHIP, (chip + 2) % NCHIP, (chip + 3) % NCHIP, core])
    dw_ff1, dw_ff2, dh2 = _bwd_mlp(df, h2, a1, ff1, ff2)
    g_ff = [_halves(dw_ff1), _halves(dw_ff2)]
    (dxres, dgates, dylru, dypool, dw_o, dw_lru_up, dw_pool_up, dg2, dg3, dbg), (l_ff,) = _bwd_merge(
        dh2, dy, x2, m, mrg, bra, brb, proj, b_gate, ylru, ypool, g2, g3, w_lru_up_f, w_pool_up_f, w_o_f,
        stages=[_to_sibling(g_ff)])
    p_ff = [_add_sibling(g, l, cidx, "add_sibling_" + n)[0] for g, l, n in zip(g_ff, l_ff, ["w_ff1", "w_ff2"])]
    g_mix = [_halves(dw_lru_up), _halves(dw_pool_up), _halves(dw_o)]
    (dxp, dgl, dcw, dcb, dwa, dba, dwx, dbx, dlam), ((c_ff1,), l_mix) = _bwd_lru(
        proj, hs, dylru, conv_full, conv_b, wa, lru_b_a, wx, lru_b_x, lru_lambda,
        stages=[_to_chips(p_ff[:1]), _to_sibling(g_mix)])
    p_mix = [_add_sibling(g, l, cidx, "add_sibling_" + n)[0] for g, l, n in zip(g_mix, l_mix, mix)]
    dxpool, dpw, dsc = _bwd_pool(proj, dypool, pw, pool_scale)
    dproj = jnp.concatenate([dxp, dgl, dxpool, dgates], axis=1)
    (dw_in, dh1), (c_rest,) = _bwd_inproj(h1, dproj, full["w_in"], stages=[_to_chips(p_ff[1:] + p_mix)])
    done = ["w_ff1", "w_ff2"] + mix
    pairs = [_add_chips(p, l, idx_big, "add_chips_" + n) for p, l, n in zip(p_ff + p_mix, [c_ff1] + c_rest, done)]
    g_in = _halves(dw_in)
    (grad_x, dg1), ((l_in,), pairs) = _bwd_prenorm(xs, dh1, dxres, g1, stages=[_to_sibling([g_in]), _share(pairs)])

    small = jnp.concatenate([
        dg1, dg2, dg3, dg4, dbg.reshape(2, D), dcb, dba, dbx, dlam,
        jnp.pad(dsc, ((0, 0), (0, D - DP))), jnp.pad(lossp, ((0, 0), (0, D - 1))), dcw,
        _block_diag_extract(dwa, hd).reshape(-1, D), _block_diag_extract(dwx, hd).reshape(-1, D),
        dpw.reshape(-1, D)], axis=0)
    p_in, ((l_small,),) = _add_sibling(g_in, l_in, cidx, "add_sibling_w_in", stages=[_to_sibling([small])])
    small2 = _add_pair(small, l_small, "add_sibling_small").reshape(2, SMALL_ROWS // 2, D)

    grads, delta, new_m, new_v = {}, {}, {}, {}
    for n, p in zip(done, pairs):
        grads[n] = p.reshape(-1, p.shape[-1])

    def update(n, stages=()):
        (delta[n], new_m[n], new_v[n]), landed = _adamw(w[n], grads[n], mom[n], var[n], "adamw_" + n, stages=stages)
        return landed

    ((c_in, c_small),) = update("w_pool_up", stages=[_to_chips([p_in, small2])])
    pair_in = _add_chips(p_in, c_in, idx_big, "add_chips_w_in")
    own_small = lax.dynamic_index_in_dim(small2, core, 0, keepdims=True)
    c_small = lax.dynamic_update_slice(c_small, own_small, (chip, zero, zero))
    pair_small = _add_chips(c_small, c_small, jnp.stack([zero, zero + 1, zero + 2, zero + 3, core]), "add_chips_small")
    ((pair_in, pair_small),) = update("w_lru_up", stages=[_share([pair_in, pair_small])])
    grads["w_in"] = pair_in.reshape(-1, pair_in.shape[-1])
    for n in ["w_in", "w_o", "w_ff1", "w_ff2"]:
        update(n)
    small_sum = pair_small.reshape(SMALL_ROWS, D)
    loss = 0.5 * small_sum[LOSS_ROW, 0]
    ccols = DR // NCHIP
    sep = [lax.dynamic_slice(small_sum[12:16], (zero, chip * ccols), (4, ccols)),
           small_sum[16:80].reshape(-1, hd), small_sum[80:144].reshape(-1, hd), small_sum[144:208].reshape(-1, PG)]
    g_s, d_s, m_s, v_s = _adamw_small(small_sum, sep, w, mom, var)
    grads.update(g_s)
    grads.update(dict(zip(SMALL_SEPARATE, sep)))
    delta.update(d_s)
    new_m.update(m_s)
    new_v.update(v_s)

    out = lambda d: [d[n].reshape(args[n].shape) for n in W_NAMES]
    return (loss, grad_x[None], *out(grads), *out(delta), *out(new_m), *out(new_v))
```

```python
import functools
import math

import jax
import jax.numpy as jnp
from jax import lax
from jax.experimental import pallas as pl
from jax.experimental.pallas import tpu as pltpu

F32 = jnp.float32
BF = jnp.bfloat16

T = 2048
D = 1024
DR = 1024
DP = 512
DF = 4096
DIN = 4608
NCHIP = 4
CW_IN = DIN // NCHIP
LANE = 128
CB = 128
NG = DR // CB
PG = 128
POOL_WINDOWS = (2, 4, 8, 16)
NORM_EPS = 1e-6
LRU_C = 8.0
GELU_C = math.sqrt(2.0 / math.pi)
ADAM_LR = 0.001
ADAM_B1 = 0.9
ADAM_B2 = 0.999
ADAM_EPS = 1e-08
ADAM_WD = 0.01
ADAM_STEP = 10
MESH_ID = pl.DeviceIdType.MESH
ANY = pl.BlockSpec(memory_space=pl.ANY)
SMALL_ROWS = 208
LOSS_ROW = 11
MIB = 1 << 20


def _cp(vmem_mib=None):
    if vmem_mib is None:
        return pltpu.CompilerParams()
    return pltpu.CompilerParams(vmem_limit_bytes=vmem_mib * MIB)


def _hbm(*arrays):
    return [pltpu.with_memory_space_constraint(a, pltpu.HBM) for a in arrays]


def _hbm_out(shapes):
    return [pltpu.HBM(s.shape, s.dtype) for s in shapes]


class _Stage:
    def __init__(self, operands, out_shape, alias, sems, start, finish):
        self.operands, self.out_shape, self.alias, self.sems = list(operands), list(out_shape), dict(alias), list(sems)
        self.start, self.finish = start, finish


def _call(body, *, name, grid, in_specs, out_specs, out_shape, args, vmem=None, stages=(), prefetch=None):
    nin, nout = len(in_specs), len(out_specs)
    npre = 0 if prefetch is None else 1
    st_args, st_shapes, st_sems, aliases = [], [], [], {}
    for st in stages:
        for k, v in st.alias.items():
            aliases[npre + nin + len(st_args) + k] = nout + len(st_shapes) + v
        st_args += st.operands
        st_shapes += st.out_shape
        st_sems += st.sems

    def wrapped(*refs):
        pre, refs = refs[:npre], refs[npre:]
        ins, pos = refs[:nin], nin
        st_ins = []
        for st in stages:
            st_ins.append(refs[pos:pos + len(st.operands)])
            pos += len(st.operands)
        outs, pos = refs[pos:pos + nout], pos + nout
        st_outs = []
        for st in stages:
            st_outs.append(refs[pos:pos + len(st.out_shape)])
            pos += len(st.out_shape)
        sems = []
        for st in stages:
            sems.append(refs[pos:pos + len(st.sems)])
            pos += len(st.sems)
        if stages:
            first = functools.reduce(jnp.logical_and, [pl.program_id(a) == 0 for a in range(len(grid))])

            @pl.when(first)
            def _():
                for st, a, b, s in zip(stages, st_ins, st_outs, sems):
                    st.start(a, b, s)

        body(*pre, *ins, *outs)
        if stages:
            last = functools.reduce(jnp.logical_and, [pl.program_id(a) == g - 1 for a, g in enumerate(grid)])

            @pl.when(last)
            def _():
                for st, a, b, s in zip(stages, st_ins, st_outs, sems):
                    st.finish(a, b, s)

    all_in = list(in_specs) + [ANY] * len(st_args)
    all_out = list(out_specs) + [ANY] * len(st_shapes)
    kw = dict(has_side_effects=True) if stages else {}
    if vmem is not None:
        kw["vmem_limit_bytes"] = vmem * MIB
    if prefetch is None:
        gkw = dict(grid=grid, in_specs=all_in, out_specs=all_out, scratch_shapes=st_sems)
    else:
        gkw = dict(grid_spec=pltpu.PrefetchScalarGridSpec(
            num_scalar_prefetch=1, grid=grid, in_specs=all_in, out_specs=all_out, scratch_shapes=st_sems))
    res = pl.pallas_call(
        wrapped, name=name, out_shape=_hbm_out(list(out_shape) + st_shapes), input_output_aliases=aliases,
        compiler_params=pltpu.CompilerParams(**kw), **gkw,
    )(*([prefetch] if npre else []), *_hbm(*args, *st_args))
    outs, rest, st_res = list(res[:nout]), list(res[nout:]), []
    for st in stages:
        st_res.append(rest[:len(st.out_shape)])
        rest = rest[len(st.out_shape):]
    return outs, st_res


def _mm(a, b):
    return jnp.dot(a.astype(BF), b.astype(BF), preferred_element_type=F32)


def _mm_nt(a, b):
    return lax.dot_general(a.astype(BF), b.astype(BF), (((1,), (1,)), ((), ())),
                           preferred_element_type=F32)


def _mm_tn(a, b):
    return lax.dot_general(a.astype(BF), b.astype(BF), (((0,), (0,)), ((), ())),
                           preferred_element_type=F32)


def _rows(v):
    return lax.broadcasted_iota(jnp.int32, v.shape, 0)


def _sd(v, s, fill=0.0):
    return jnp.where(_rows(v) >= s, pltpu.roll(v, s, axis=0), fill)


def _su(v, s, fill=0.0):
    n = v.shape[0]
    return jnp.where(_rows(v) < n - s, pltpu.roll(v, n - s, axis=0), fill)


def _sigmoid(z):
    return 1.0 / (1.0 + jnp.exp(-z))


def _softplus(z):
    e = jnp.exp(-jnp.abs(z))
    u = 1.0 + e
    d = u - 1.0
    log1p = jnp.where(d == 0.0, e, jnp.log(u) * (e / jnp.where(d == 0.0, 1.0, d)))
    return jnp.maximum(z, 0.0) + log1p


def _mean(v):
    return jnp.mean(v, axis=-1, keepdims=True)


def _colsum(v):
    return jnp.sum(v, axis=0, keepdims=True)


def _acc(ref, val, first):
    @pl.when(first)
    def _():
        ref[...] = val

    @pl.when(jnp.logical_not(first))
    def _():
        ref[...] += val


def _conv(xp, cw, cb):
    x1, x2, x3 = _sd(xp, 1), _sd(xp, 2), _sd(xp, 3)
    xc = cb + cw[0:1] * x3 + cw[1:2] * x2 + cw[2:3] * x1 + cw[3:4] * xp
    return xc, x1, x2, x3


def _lru_gates(xc, wa, ba, wx, bx, lam):
    xcb = xc.astype(BF)
    r = _sigmoid(_mm(xcb, wa) + ba)
    ii = _sigmoid(_mm(xcb, wx) + bx)
    sp = _softplus(-lam)
    la = (-LRU_C) * r * sp
    a = jnp.exp(la)
    mult = jnp.sqrt(-jnp.tanh(la) * (a * a + 1.0))
    return xcb, r, ii, sp, a, mult


def _gelu_parts(g):
    th = jnp.tanh(GELU_C * (g + 0.044715 * (g * g * g)))
    gel = 0.5 * g * (1.0 + th)
    dgel = 0.5 * (1.0 + th) + 0.5 * g * (1.0 - th * th) * (GELU_C * (1.0 + 3.0 * 0.044715 * (g * g)))
    return gel, dgel


def _pool_window(x, steps, shift):
    s, sh = x, 1
    for _ in range(steps):
        s = s + shift(s, sh)
        sh *= 2
    return s


def _fwd_inproj(x, g1, w_in, stages=()):
    tm = 512

    def body(x_ref, g_ref, w_ref, proj_ref, h_ref):
        @pl.when(pl.program_id(1) == 0)
        def _():
            xv = x_ref[...]
            r = lax.rsqrt(_mean(xv * xv) + NORM_EPS)
            h_ref[...] = ((xv * r) * g_ref[...]).astype(BF)

        proj_ref[...] = jnp.dot(h_ref[...], w_ref[0], preferred_element_type=F32)

    return _call(
        body, name="fwd_inproj", grid=(T // tm, NCHIP),
        in_specs=[pl.BlockSpec((tm, D), lambda i, k: (i, 0)),
                  pl.BlockSpec((1, D), lambda i, k: (0, 0)),
                  pl.BlockSpec((1, D, CW_IN), lambda i, k: (k, 0, 0))],
        out_specs=[pl.BlockSpec((tm, CW_IN), lambda i, k: (i, k)),
                   pl.BlockSpec((tm, D), lambda i, k: (i, 0))],
        out_shape=[jax.ShapeDtypeStruct((T, DIN), F32), jax.ShapeDtypeStruct((T, D), BF)],
        vmem=40, args=[x, g1, w_in], stages=stages)


def _vec_spec():
    return pl.BlockSpec((1, CB), lambda j: (0, j))


def _fwd_lru(proj, conv_w, conv_b, wa, ba, wx, bx, lam, stages=()):
    def body(xp_ref, g_ref, cw_ref, cb_ref, wa_ref, ba_ref, wx_ref, bx_ref, lam_ref, y_ref, h_ref):
        xc, _, _, _ = _conv(xp_ref[...], cw_ref[...], cb_ref[...])
        _, _, ii, _, a, mult = _lru_gates(xc, wa_ref[0], ba_ref[...], wx_ref[0], bx_ref[...], lam_ref[...])
        b = mult * (ii * xc)
        s = 1
        while s < T:
            b = b + a * _sd(b, s, 0.0)
            if 2 * s < T:
                a = a * _sd(a, s, 1.0)
            s *= 2
        h_ref[...] = b
        gel, _ = _gelu_parts(g_ref[...])
        y_ref[...] = (b * gel).astype(BF)

    return _call(
        body, name="fwd_lru", grid=(NG,),
        in_specs=[pl.BlockSpec((T, CB), lambda j: (0, j)),
                  pl.BlockSpec((T, CB), lambda j: (0, NG + j)),
                  pl.BlockSpec((4, CB), lambda j: (0, j)),
                  _vec_spec(),
                  pl.BlockSpec((1, CB, CB), lambda j: (j, 0, 0)), _vec_spec(),
                  pl.BlockSpec((1, CB, CB), lambda j: (j, 0, 0)), _vec_spec(),
                  _vec_spec()],
        out_specs=[pl.BlockSpec((T, CB), lambda j: (0, j)), pl.BlockSpec((T, CB), lambda j: (0, j))],
        out_shape=[jax.ShapeDtypeStruct((T, DR), BF), jax.ShapeDtypeStruct((T, DR), F32)],
        vmem=48, args=[proj, proj, conv_w, conv_b, wa, ba, wx, bx, lam], stages=stages)


def _pool_cnt(w):
    t = lax.broadcasted_iota(jnp.int32, (T, 1), 0)
    return jnp.minimum(t + 1, w).astype(F32)


def _fwd_pool(proj, pool_w, pool_scale):
    def body(xp_ref, pw_ref, sc_ref, y_ref):
        for g, w in enumerate(POOL_WINDOWS):
            cols = slice(g * PG, (g + 1) * PG)
            x = xp_ref[:, cols]
            p = _pool_window(x, g + 1, _sd) / _pool_cnt(w) - x
            y_ref[:, cols] = (_mm(p, pw_ref[g]) * sc_ref[:, cols]).astype(BF)

    return pl.pallas_call(
        body, name="fwd_pool", grid=(1,),
        in_specs=[pl.BlockSpec((T, DP), lambda i: (0, 2 * DR // DP)),
                  pl.BlockSpec((4, PG, PG), lambda i: (0, 0, 0)),
                  pl.BlockSpec((1, DP), lambda i: (0, 0))],
        out_specs=pl.BlockSpec((T, DP), lambda i: (0, 0)),
        out_shape=pltpu.HBM((T, DP), BF),
        compiler_params=_cp(48),
    )(*_hbm(proj, pool_w, pool_scale))


GATE_BLK = 512
GATE_BLK0 = (2 * DR + DP) // GATE_BLK


def _gate_specs(tm):
    return [pl.BlockSpec((tm, GATE_BLK), functools.partial(lambda i, q: (i, GATE_BLK0 + q), q=q))
            for q in range(4)]


def _fwd_merge(x, ylru, ypool, proj, b_gate, g2, g3, w_lru_up, w_pool_up, w_o, stages=()):
    tm = 512

    def body(x_ref, yl_ref, yp_ref, p0, p1, p2, p3, bg_ref, g2_ref, g3_ref, wl_ref, wp_ref, wo_ref,
             x2_ref, h2_ref, m_ref, mrg_ref, bra_ref, brb_ref):
        bra = jnp.dot(yl_ref[...], wl_ref[...], preferred_element_type=F32)
        yp = yp_ref[...]
        brb = jnp.concatenate([jnp.dot(yp, wp_ref[k], preferred_element_type=F32) for k in range(NCHIP)], axis=1)
        bg = bg_ref[...]
        ga = _sigmoid(jnp.concatenate([p0[...], p1[...]], axis=1) + bg[:, :D])
        gb = _sigmoid(jnp.concatenate([p2[...], p3[...]], axis=1) + bg[:, D:])
        mrg = (ga * bra + gb * brb).astype(BF)
        m = jnp.dot(mrg, wo_ref[...], preferred_element_type=F32)
        r2 = lax.rsqrt(_mean(m * m) + NORM_EPS)
        x2 = x_ref[...] + (m * r2) * g2_ref[...]
        r3 = lax.rsqrt(_mean(x2 * x2) + NORM_EPS)
        x2_ref[...] = x2
        h2_ref[...] = ((x2 * r3) * g3_ref[...]).astype(BF)
        m_ref[...] = m
        mrg_ref[...] = mrg
        bra_ref[...] = bra.astype(BF)
        brb_ref[...] = brb.astype(BF)

    row = lambda w: pl.BlockSpec((tm, w), lambda i: (i, 0))
    full2 = lambda a, b: pl.BlockSpec((a, b), lambda i: (0, 0))
    return _call(
        body, name="fwd_merge", grid=(T // tm,),
        in_specs=[row(D), row(DR), row(DP)] + _gate_specs(tm) +
                 [full2(1, 2 * D), full2(1, D), full2(1, D), full2(DR, D),
                  pl.BlockSpec((NCHIP, DP, D // NCHIP), lambda i: (0, 0, 0)), full2(D, D)],
        out_specs=[row(D)] * 6,
        out_shape=[jax.ShapeDtypeStruct((T, D), F32), jax.ShapeDtypeStruct((T, D), BF),
                   jax.ShapeDtypeStruct((T, D), F32), jax.ShapeDtypeStruct((T, D), BF),
                   jax.ShapeDtypeStruct((T, D), BF), jax.ShapeDtypeStruct((T, D), BF)],
        vmem=48, args=[x, ylru, ypool, proj, proj, proj, proj, b_gate, g2, g3, w_lru_up, w_pool_up, w_o],
        stages=stages)


def _fwd_mlp(h2, w_ff1, w_ff2):
    tm = 512
    fk = DF // NCHIP

    def body(h_ref, w1_ref, w2_ref, a1_ref, f_ref):
        h = h_ref[...]
        f = None
        for k in range(NCHIP):
            a1 = jnp.maximum(jnp.dot(h, w1_ref[k], preferred_element_type=F32), 0.0)
            a1_ref[:, k * fk:(k + 1) * fk] = a1.astype(BF)
            part = jnp.dot((a1 * a1).astype(BF), w2_ref[k * fk:(k + 1) * fk, :], preferred_element_type=F32)
            f = part if f is None else f + part
        f_ref[...] = f

    return pl.pallas_call(
        body, name="fwd_mlp", grid=(T // tm,),
        in_specs=[pl.BlockSpec((tm, D), lambda i: (i, 0)),
                  pl.BlockSpec((NCHIP, D, fk), lambda i: (0, 0, 0)),
                  pl.BlockSpec((DF, D), lambda i: (0, 0))],
        out_specs=[pl.BlockSpec((tm, DF), lambda i: (i, 0)), pl.BlockSpec((tm, D), lambda i: (i, 0))],
        out_shape=_hbm_out([jax.ShapeDtypeStruct((T, DF), BF), jax.ShapeDtypeStruct((T, D), F32)]),
        compiler_params=_cp(56),
    )(*_hbm(h2, w_ff1, w_ff2))


def _loss_head(f, x2, target, g4):
    tm = 512

    def body(f_ref, x2_ref, t_ref, g_ref, loss_ref, dy_ref, df_ref, dg_ref):
        first = pl.program_id(0) == 0
        f = f_ref[...]
        g4v = g_ref[...]
        r4 = lax.rsqrt(_mean(f * f) + NORM_EPS)
        fn = f * r4
        e = (x2_ref[...] + fn * g4v) - t_ref[...]
        _acc(loss_ref, jnp.sum(_mean(e * e), axis=0, keepdims=True), first)
        dy = e * (1.0 / D)
        dy_ref[...] = dy
        _acc(dg_ref, _colsum(dy * fn), first)
        dfn = dy * g4v
        df_ref[...] = (r4 * (dfn - fn * _mean(dfn * fn))).astype(BF)

    row = pl.BlockSpec((tm, D), lambda i: (i, 0))
    return pl.pallas_call(
        body, name="loss_head", grid=(T // tm,),
        in_specs=[row, row, row, pl.BlockSpec((1, D), lambda i: (0, 0))],
        out_specs=[pl.BlockSpec((1, 1), lambda i: (0, 0)), row, row, pl.BlockSpec((1, D), lambda i: (0, 0))],
        out_shape=_hbm_out([jax.ShapeDtypeStruct((1, 1), F32), jax.ShapeDtypeStruct((T, D), F32),
                            jax.ShapeDtypeStruct((T, D), BF), jax.ShapeDtypeStruct((1, D), F32)]),
        compiler_params=_cp(48),
    )(*_hbm(f, x2, target, g4))


def _bwd_mlp_x(df, a1, w_ff1, w_ff2):
    tm = 512
    fk = DF // NCHIP

    def body(df_ref, a1_ref, w1_ref, w2_ref, dh_ref, df1_ref):
        df = df_ref[...]
        dh = None
        for k in range(NCHIP):
            cols = slice(k * fk, (k + 1) * fk)
            dact = _mm_nt(df, w2_ref[cols, :])
            df1 = (dact * (2.0 * a1_ref[:, cols].astype(F32))).astype(BF)
            df1_ref[:, cols] = df1
            part = _mm_nt(df1, w1_ref[k])
            dh = part if dh is None else dh + part
        dh_ref[...] = dh

    return pl.pallas_call(
        body, name="bwd_mlp_x", grid=(T // tm,),
        in_specs=[pl.BlockSpec((tm, D), lambda i: (i, 0)),
                  pl.BlockSpec((tm, DF), lambda i: (i, 0)),
                  pl.BlockSpec((NCHIP, D, fk), lambda i: (0, 0, 0)),
                  pl.BlockSpec((DF, D), lambda i: (0, 0))],
        out_specs=[pl.BlockSpec((tm, D), lambda i: (i, 0)), pl.BlockSpec((tm, DF), lambda i: (i, 0))],
        out_shape=_hbm_out([jax.ShapeDtypeStruct((T, D), F32), jax.ShapeDtypeStruct((T, DF), BF)]),
        compiler_params=_cp(56),
    )(*_hbm(df, a1, w_ff1, w_ff2))


def _bwd_mlp_w(df, h2, a1, df1):
    fc = 512
    per = (DF // NCHIP) // fc

    def body(df_ref, h_ref, a1_ref, df1_ref, dw1_ref, dw2_ref):
        a1 = a1_ref[...].astype(F32)
        dw2_ref[...] = _mm_tn((a1 * a1).astype(BF), df_ref[...])
        dw1_ref[0] = _mm_tn(h_ref[...], df1_ref[...])

    return pl.pallas_call(
        body, name="bwd_mlp_w", grid=(DF // fc,),
        in_specs=[pl.BlockSpec((T, D), lambda j: (0, 0)),
                  pl.BlockSpec((T, D), lambda j: (0, 0)),
                  pl.BlockSpec((T, fc), lambda j: (0, j)),
                  pl.BlockSpec((T, fc), lambda j: (0, j))],
        out_specs=[pl.BlockSpec((1, D, fc), lambda j: (j // per, 0, j % per)),
                   pl.BlockSpec((fc, D), lambda j: (j, 0))],
        out_shape=_hbm_out([jax.ShapeDtypeStruct((NCHIP, D, DF // NCHIP), F32),
                            jax.ShapeDtypeStruct((DF, D), F32)]),
        compiler_params=_cp(56),
    )(*_hbm(df, h2, a1, df1))


def _bwd_merge(dh2, dy, x2, m, mrg, bra, brb, proj, b_gate, ylru, ypool, g2, g3, w_lru_up, w_pool_up, w_o,
               stages=()):
    tm = 256
    cpu = D // NCHIP

    def body(dh2_ref, dy_ref, x2_ref, m_ref, mrg_ref, bra_ref, brb_ref, p0, p1, p2, p3, bg_ref, yl_ref, yp_ref,
             g2_ref, g3_ref, wl_ref, wp_ref, wo_ref,
             dx_ref, dgt_ref, dyl_ref, dyp_ref, dwo_ref, dwl_ref, dwp_ref, dg2_ref, dg3_ref, dbg_ref):
        first = pl.program_id(0) == 0
        x2 = x2_ref[...]
        r3 = lax.rsqrt(_mean(x2 * x2) + NORM_EPS)
        x2n = x2 * r3
        dh2 = dh2_ref[...]
        t3 = dh2 * g3_ref[...]
        dx2 = dy_ref[...] + r3 * (t3 - x2n * _mean(t3 * x2n))
        dx_ref[...] = dx2
        _acc(dg3_ref, _colsum(dh2 * x2n), first)
        m = m_ref[...]
        r2 = lax.rsqrt(_mean(m * m) + NORM_EPS)
        mn = m * r2
        _acc(dg2_ref, _colsum(dx2 * mn), first)
        dmn = dx2 * g2_ref[...]
        dm = (r2 * (dmn - mn * _mean(dmn * mn))).astype(BF)
        dmrg = _mm_nt(dm, wo_ref[...])
        _acc(dwo_ref, _mm_tn(mrg_ref[...], dm), first)
        bg = bg_ref[...]
        ga = _sigmoid(jnp.concatenate([p0[...], p1[...]], axis=1) + bg[:, :D])
        gb = _sigmoid(jnp.concatenate([p2[...], p3[...]], axis=1) + bg[:, D:])
        dga = dmrg * bra_ref[...].astype(F32) * (ga * (1.0 - ga))
        dgb = dmrg * brb_ref[...].astype(F32) * (gb * (1.0 - gb))
        dgt_ref[:, :D] = dga.astype(BF)
        dgt_ref[:, D:] = dgb.astype(BF)
        _acc(dbg_ref, jnp.concatenate([_colsum(dga), _colsum(dgb)], axis=1), first)
        dbra = (dmrg * ga).astype(BF)
        dbrb = (dmrg * gb).astype(BF)
        dyl_ref[...] = _mm_nt(dbra, wl_ref[...])
        _acc(dwl_ref, _mm_tn(yl_ref[...], dbra), first)
        yp = yp_ref[...]
        dyp = None
        for k in range(NCHIP):
            dk = dbrb[:, k * cpu:(k + 1) * cpu]
            part = _mm_nt(dk, wp_ref[k])
            dyp = part if dyp is None else dyp + part
            _acc(dwp_ref.at[k], _mm_tn(yp, dk), first)
        dyp_ref[...] = dyp

    row = lambda w: pl.BlockSpec((tm, w), lambda i: (i, 0))
    full2 = lambda a, b: pl.BlockSpec((a, b), lambda i: (0, 0))
    wp_spec = pl.BlockSpec((NCHIP, DP, cpu), lambda i: (0, 0, 0))
    return _call(
        body, name="bwd_merge", grid=(T // tm,),
        in_specs=[row(D)] * 7 + _gate_specs(tm) +
                 [full2(1, 2 * D), row(DR), row(DP), full2(1, D), full2(1, D), full2(DR, D), wp_spec, full2(D, D)],
        out_specs=[row(D), row(2 * D), row(DR), row(DP), full2(D, D), full2(DR, D), wp_spec,
                   full2(1, D), full2(1, D), full2(1, 2 * D)],
        out_shape=[jax.ShapeDtypeStruct((T, D), F32), jax.ShapeDtypeStruct((T, 2 * D), BF),
                   jax.ShapeDtypeStruct((T, DR), F32), jax.ShapeDtypeStruct((T, DP), F32),
                   jax.ShapeDtypeStruct((D, D), F32), jax.ShapeDtypeStruct((DR, D), F32),
                   jax.ShapeDtypeStruct((NCHIP, DP, cpu), F32),
                   jax.ShapeDtypeStruct((1, D), F32), jax.ShapeDtypeStruct((1, D), F32),
                   jax.ShapeDtypeStruct((1, 2 * D), F32)],
        vmem=56, args=[dh2, dy, x2, m, mrg, bra, brb, proj, proj, proj, proj, b_gate, ylru, ypool, g2, g3, w_lru_up,
                       w_pool_up, w_o], stages=stages)


def _bwd_lru(proj, h, dylru, conv_w, conv_b, wa, ba, wx, bx, lam, stages=()):
    def body(xp_ref, g_ref, h_ref, dy_ref, cw_ref, cb_ref, wa_ref, ba_ref, wx_ref, bx_ref, lam_ref,
             dxp_ref, dg_ref, dcw_ref, dcb_ref, dwa_ref, dba_ref, dwx_ref, dbx_ref, dlam_ref):
        xp = xp_ref[...]
        cw = cw_ref[...]
        lam = lam_ref[...]
        xc, x1, x2, x3 = _conv(xp, cw, cb_ref[...])
        wa, wx = wa_ref[0], wx_ref[0]
        xcb, r, ii, sp, a, mult = _lru_gates(xc, wa, ba_ref[...], wx, bx_ref[...], lam)
        g = g_ref[...]
        gel, dgel = _gelu_parts(g)
        h = h_ref[...]
        dy = dy_ref[...]
        dg_ref[...] = (dy * h * dgel).astype(BF)
        b = dy * gel
        aa = _su(a, 1, 0.0)
        s = 1
        while s < T:
            b = b + aa * _su(b, s, 0.0)
            if 2 * s < T:
                aa = aa * _su(aa, s, 0.0)
            s *= 2
        da = b * _sd(h, 1, 0.0)
        dmult = b * (ii * xc)
        dii = b * (mult * xc)
        dxc = b * (mult * ii)
        dla = da * a - dmult * ((a * a) / mult)
        dr = dla * ((-LRU_C) * sp)
        dsp = _colsum(dla * ((-LRU_C) * r))
        dlam_ref[...] = -dsp / (1.0 + jnp.exp(lam))
        dzr = dr * (r * (1.0 - r))
        dzi = dii * (ii * (1.0 - ii))
        dzrb, dzib = dzr.astype(BF), dzi.astype(BF)
        dxc = dxc + _mm_nt(dzrb, wa) + _mm_nt(dzib, wx)
        dwa_ref[0] = _mm_tn(xcb, dzrb)
        dwx_ref[0] = _mm_tn(xcb, dzib)
        dba_ref[...] = _colsum(dzr)
        dbx_ref[...] = _colsum(dzi)
        dcb_ref[...] = _colsum(dxc)
        dcw_ref[...] = jnp.concatenate([_colsum(dxc * x3), _colsum(dxc * x2), _colsum(dxc * x1),
                                        _colsum(dxc * xp)], axis=0)
        dxp = cw[3:4] * dxc + cw[2:3] * _su(dxc, 1) + cw[1:2] * _su(dxc, 2) + cw[0:1] * _su(dxc, 3)
        dxp_ref[...] = dxp.astype(BF)

    blk = pl.BlockSpec((T, CB), lambda j: (0, j))
    wsp = pl.BlockSpec((1, CB, CB), lambda j: (j, 0, 0))
    return _call(
        body, name="bwd_lru", grid=(NG,),
        in_specs=[blk, pl.BlockSpec((T, CB), lambda j: (0, NG + j)), blk, blk,
                  pl.BlockSpec((4, CB), lambda j: (0, j)), _vec_spec(), wsp, _vec_spec(), wsp, _vec_spec(),
                  _vec_spec()],
        out_specs=[blk, blk, pl.BlockSpec((4, CB), lambda j: (0, j)), _vec_spec(), wsp, _vec_spec(), wsp,
                   _vec_spec(), _vec_spec()],
        out_shape=[jax.ShapeDtypeStruct((T, DR), BF), jax.ShapeDtypeStruct((T, DR), BF),
                   jax.ShapeDtypeStruct((4, DR), F32), jax.ShapeDtypeStruct((1, DR), F32),
                   jax.ShapeDtypeStruct((NG, CB, CB), F32), jax.ShapeDtypeStruct((1, DR), F32),
                   jax.ShapeDtypeStruct((NG, CB, CB), F32), jax.ShapeDtypeStruct((1, DR), F32),
                   jax.ShapeDtypeStruct((1, DR), F32)],
        vmem=56, args=[proj, proj, h, dylru, conv_w, conv_b, wa, ba, wx, bx, lam], stages=stages)


def _bwd_pool(proj, dypool, pool_w, pool_scale):
    def body(xp_ref, dy_ref, pw_ref, sc_ref, dx_ref, dw_ref, dsc_ref):
        for g, w in enumerate(POOL_WINDOWS):
            cols = slice(g * PG, (g + 1) * PG)
            cnt = _pool_cnt(w)
            x = xp_ref[:, cols]
            pb = (_pool_window(x, g + 1, _sd) / cnt - x).astype(BF)
            wg = pw_ref[g]
            dy = dy_ref[:, cols]
            dsc_ref[:, cols] = _colsum(dy * _mm(pb, wg))
            dyp = (dy * sc_ref[:, cols]).astype(BF)
            dw_ref[g] = _mm_tn(pb, dyp)
            dp = _mm_nt(dyp, wg)
            dx_ref[:, cols] = (_pool_window(dp / cnt, g + 1, _su) - dp).astype(BF)

    return pl.pallas_call(
        body, name="bwd_pool", grid=(1,),
        in_specs=[pl.BlockSpec((T, DP), lambda i: (0, 2 * DR // DP)),
                  pl.BlockSpec((T, DP), lambda i: (0, 0)),
                  pl.BlockSpec((4, PG, PG), lambda i: (0, 0, 0)),
                  pl.BlockSpec((1, DP), lambda i: (0, 0))],
        out_specs=[pl.BlockSpec((T, DP), lambda i: (0, 0)),
                   pl.BlockSpec((4, PG, PG), lambda i: (0, 0, 0)),
                   pl.BlockSpec((1, DP), lambda i: (0, 0))],
        out_shape=_hbm_out([jax.ShapeDtypeStruct((T, DP), BF), jax.ShapeDtypeStruct((4, PG, PG), F32),
                            jax.ShapeDtypeStruct((1, DP), F32)]),
        compiler_params=_cp(48),
    )(*_hbm(proj, dypool, pool_w, pool_scale))


def _bwd_inproj(h1, dproj, w_in, stages=()):
    def body(h_ref, dp_ref, w_ref, dw_ref, dh_ref):
        dp = dp_ref[...]
        dw_ref[0] = _mm_tn(h_ref[...], dp)
        _acc(dh_ref, _mm_nt(dp, w_ref[0]), pl.program_id(0) == 0)

    return _call(
        body, name="bwd_inproj", grid=(NCHIP,),
        in_specs=[pl.BlockSpec((T, D), lambda k: (0, 0)),
                  pl.BlockSpec((T, CW_IN), lambda k: (0, k)),
                  pl.BlockSpec((1, D, CW_IN), lambda k: (k, 0, 0))],
        out_specs=[pl.BlockSpec((1, D, CW_IN), lambda k: (k, 0, 0)), pl.BlockSpec((T, D), lambda k: (0, 0))],
        out_shape=[jax.ShapeDtypeStruct((NCHIP, D, CW_IN), F32), jax.ShapeDtypeStruct((T, D), F32)],
        vmem=56, args=[h1, dproj, w_in], stages=stages)


def _bwd_prenorm(x, dh1, dxres, g1, stages=()):
    tm = 512

    def body(x_ref, dh_ref, dr_ref, g_ref, dx_ref, dg_ref):
        xv = x_ref[...]
        r = lax.rsqrt(_mean(xv * xv) + NORM_EPS)
        xn = xv * r
        dh = dh_ref[...]
        t = dh * g_ref[...]
        dx_ref[...] = dr_ref[...] + r * (t - xn * _mean(t * xn))
        _acc(dg_ref, _colsum(dh * xn), pl.program_id(0) == 0)

    row = pl.BlockSpec((tm, D), lambda i: (i, 0))
    vec = pl.BlockSpec((1, D), lambda i: (0, 0))
    return _call(
        body, name="bwd_prenorm", grid=(T // tm,),
        in_specs=[row, row, row, vec], out_specs=[row, vec],
        out_shape=[jax.ShapeDtypeStruct((T, D), F32), jax.ShapeDtypeStruct((1, D), F32)],
        vmem=48, args=[x, dh1, dxres, g1], stages=stages)


def _place():
    x, y, c = lax.axis_index("x"), lax.axis_index("y"), lax.axis_index("c")
    chips = [(1 - x, y), (x, 1 - y), (1 - x, 1 - y)]
    return x, y, c, chips


def _rcopy(src, dst, ssem, rsem, dev):
    return pltpu.make_async_remote_copy(src_ref=src, dst_ref=dst, send_sem=ssem, recv_sem=rsem,
                                        device_id=dev, device_id_type=MESH_ID)


def _sds(a):
    return jax.ShapeDtypeStruct(a.shape, a.dtype)


def _sem2(n, m):
    return [pltpu.SemaphoreType.DMA((n, m)), pltpu.SemaphoreType.DMA((n, m))]


ALL = (0, 1, 1)


def _piece(ref, k, half, part):
    hr = ref.shape[1] // 2
    r0, r1 = hr * part[0] // part[2], hr * part[1] // part[2]
    return ref.at[k, pl.ds(half * hr + r0, r1 - r0), :]


def _gather(fulls, ici=(), d2d=()):
    n = len(fulls)
    ici, d2d = list(ici), list(d2d)

    def copies(outs, sems):
        x, y, c, chips = _place()
        me = 2 * x + y
        sib = (x, y, 1 - c)
        send, recv = [], []
        for q, (i, part) in enumerate(ici):
            for j, chip in enumerate(chips):
                mine, theirs = _piece(outs[i], me, c, part), _piece(outs[i], 2 * chip[0] + chip[1], c, part)
                send.append(_rcopy(mine, mine, sems[0].at[q, j], sems[1].at[q, j], (*chip, c)))
                recv.append(_rcopy(theirs, theirs, sems[0].at[q, j], sems[1].at[q, j], (*chip, c)))
        for q, (i, part) in enumerate(d2d):
            for j, chip in enumerate(chips):
                k = 2 * chip[0] + chip[1]
                got, other = _piece(outs[i], k, c, part), _piece(outs[i], k, 1 - c, part)
                send.append(_rcopy(got, got, sems[2].at[q, j], sems[3].at[q, j], sib))
                recv.append(_rcopy(other, other, sems[2].at[q, j], sems[3].at[q, j], sib))
        return send, recv

    def start(ins, outs, sems):
        for cp in copies(outs, sems)[0]:
            cp.start()

    def finish(ins, outs, sems):
        send, recv = copies(outs, sems)
        for cp in recv:
            cp.wait_recv()
        for cp in send:
            cp.wait_send()

    sems = _sem2(max(len(ici), 1), 3) + _sem2(max(len(d2d), 1), 3)
    return _Stage(fulls, [_sds(f) for f in fulls], {i: i for i in range(n)}, sems, start, finish)


def _gather_first(full, conv_w):
    ici, d2d = _gather([full], ici=[(0, ALL)]), _gather([full], d2d=[(0, ALL)])

    def body(full_in, cw_in, full_out, cw_out, s0, r0, s1, r1, cs, cr):
        x, y, c, chips = _place()
        me = 2 * x + y
        conv = [_rcopy(cw_in, cw_out.at[me], cs.at[j], cr.at[j], (*chip, c)) for j, chip in enumerate(chips)]
        for cp in conv:
            cp.start()
        sems = [s0, r0, s1, r1]
        ici.start(None, [full_out], sems)
        ici.finish(None, [full_out], sems)
        d2d.start(None, [full_out], sems)
        d2d.finish(None, [full_out], sems)
        for j, chip in enumerate(chips):
            _rcopy(cw_in, cw_out.at[2 * chip[0] + chip[1]], cs.at[j], cr.at[j], (*chip, c)).wait_recv()
        for cp in conv:
            cp.wait_send()

    return pl.pallas_call(
        body, name="gather_first",
        in_specs=[ANY, ANY], out_specs=[ANY, ANY],
        out_shape=_hbm_out([full, jax.ShapeDtypeStruct((NCHIP,) + conv_w.shape, conv_w.dtype)]),
        input_output_aliases={0: 0},
        scratch_shapes=_sem2(1, 3) + _sem2(1, 3) + [pltpu.SemaphoreType.DMA((3,)), pltpu.SemaphoreType.DMA((3,))],
        compiler_params=pltpu.CompilerParams(has_side_effects=True),
    )(*_hbm(full, conv_w))


def _comm_only(name, stage):
    def body(*refs):
        ni, no = len(stage.operands), len(stage.out_shape)
        stage.start(refs[:ni], refs[ni:ni + no], refs[ni + no:])
        stage.finish(refs[:ni], refs[ni:ni + no], refs[ni + no:])

    return pl.pallas_call(
        body, name=name, in_specs=[ANY] * len(stage.operands), out_specs=[ANY] * len(stage.out_shape),
        out_shape=_hbm_out(stage.out_shape), input_output_aliases=stage.alias, scratch_shapes=stage.sems,
        compiler_params=pltpu.CompilerParams(has_side_effects=True),
    )(*_hbm(*stage.operands))


def _to_sibling(srcs):
    n = len(srcs)

    def copies(ins, outs, sems):
        x, y, c, _ = _place()
        sib = (x, y, 1 - c)
        return [_rcopy(ins[i].at[:, 1 - c] if srcs[i].ndim == 4 else ins[i], outs[i], sems[0].at[i], sems[1].at[i], sib)
                for i in range(n)]

    def start(ins, outs, sems):
        for cp in copies(ins, outs, sems):
            cp.start()

    def finish(ins, outs, sems):
        for cp in copies(ins, outs, sems):
            cp.wait()

    shapes = [jax.ShapeDtypeStruct((NCHIP,) + s.shape[2:] if s.ndim == 4 else s.shape, s.dtype) for s in srcs]
    return _Stage(srcs, shapes, {}, [pltpu.SemaphoreType.DMA((n,)), pltpu.SemaphoreType.DMA((n,))], start, finish)


def _to_chips(srcs):
    n = len(srcs)

    def copies(ins, outs, sems):
        x, y, c, chips = _place()
        me = 2 * x + y
        return [_rcopy(ins[i].at[2 * chip[0] + chip[1]] if srcs[i].shape[0] == NCHIP else ins[i].at[c],
                       outs[i].at[me], sems[0].at[i, j], sems[1].at[i, j], (*chip, c))
                for i in range(n) for j, chip in enumerate(chips)]

    def start(ins, outs, sems):
        for cp in copies(ins, outs, sems):
            cp.start()

    def finish(ins, outs, sems):
        for cp in copies(ins, outs, sems):
            cp.wait()

    shapes = [jax.ShapeDtypeStruct((NCHIP,) + s.shape[1:], s.dtype) for s in srcs]
    return _Stage(srcs, shapes, {}, _sem2(n, 3), start, finish)


def _share(pairs):
    n = len(pairs)

    def start(ins, outs, sems):
        x, y, c, _ = _place()
        for i in range(n):
            _rcopy(outs[i].at[c], outs[i].at[c], sems[0].at[i], sems[1].at[i], (x, y, 1 - c)).start()

    def finish(ins, outs, sems):
        x, y, c, _ = _place()
        for i in range(n):
            _rcopy(outs[i].at[c], outs[i].at[c], sems[0].at[i], sems[1].at[i], (x, y, 1 - c)).wait_send()
            _rcopy(outs[i].at[1 - c], outs[i].at[1 - c], sems[0].at[i], sems[1].at[i], (x, y, 1 - c)).wait_recv()

    return _Stage(pairs, [_sds(p) for p in pairs], {i: i for i in range(n)},
                  [pltpu.SemaphoreType.DMA((n,)), pltpu.SemaphoreType.DMA((n,))], start, finish)


def _row_block(rows, cols, itemsize=4, target=MIB):
    br = rows
    while br * cols * itemsize > target and br % 16 == 0:
        br //= 2
    return br


def _cast_place(w, chip_idx, name):
    rows, cols = w.shape
    br = _row_block(rows, cols)

    def body(k_ref, w_ref, o_ref):
        o_ref[0] = w_ref[...].astype(BF)

    return _call(
        body, name=name, grid=(rows // br,), prefetch=chip_idx,
        in_specs=[pl.BlockSpec((br, cols), lambda r, k: (r, 0))],
        out_specs=[pl.BlockSpec((1, br, cols), lambda r, k: (k[0], r, 0))],
        out_shape=[jax.ShapeDtypeStruct((NCHIP, rows, cols), BF)], vmem=32, args=[w])[0][0]


def _add_sibling(g, land, cidx, name, stages=()):
    _, _, hr, cols = g.shape
    br = _row_block(hr, cols)

    def body(c_ref, g_ref, l_ref, o_ref):
        o_ref[...] = (g_ref[0, 0] + l_ref[0]).astype(BF)[None]

    outs, st = _call(
        body, name=name, grid=(NCHIP, hr // br), prefetch=cidx,
        in_specs=[pl.BlockSpec((1, 1, br, cols), lambda k, r, c: (k, c[0], r, 0)),
                  pl.BlockSpec((1, br, cols), lambda k, r, c: (k, r, 0))],
        out_specs=[pl.BlockSpec((1, br, cols), lambda k, r, c: (k, r, 0))],
        out_shape=[jax.ShapeDtypeStruct((NCHIP, hr, cols), BF)], vmem=32, args=[g, land], stages=stages)
    return outs[0], st


def _add_pair(a, b, name):
    rows, cols = a.shape

    def body(a_ref, b_ref, o_ref):
        o_ref[...] = a_ref[...] + b_ref[...]

    spec = pl.BlockSpec((rows, cols), lambda r: (0, 0))
    return _call(body, name=name, grid=(1,), in_specs=[spec, spec], out_specs=[spec], out_shape=[_sds(a)],
                 vmem=32, args=[a, b])[0][0]


def _add_chips(own, land, idx, name):
    _, hr, cols = land.shape
    br = _row_block(hr, cols)

    def body(s_ref, a_ref, b_ref, c_ref, d_ref, o_ref):
        o_ref[...] = (a_ref[...].astype(F32) + b_ref[...].astype(F32)) + (c_ref[...].astype(F32) +
                                                                           d_ref[...].astype(F32))

    spec = lambda q: pl.BlockSpec((1, br, cols), functools.partial(lambda r, s, q: (s[q], r, 0), q=q))
    return _call(
        body, name=name, grid=(hr // br,), prefetch=idx,
        in_specs=[spec(0), spec(1), spec(2), spec(3)], out_specs=[spec(4)],
        out_shape=[jax.ShapeDtypeStruct((2, hr, cols), F32)], vmem=32, args=[own, land, land, land])[0][0]


def _adamw_math(w, g, m, v):
    mn = ADAM_B1 * m + (1.0 - ADAM_B1) * g
    vn = ADAM_B2 * v + (1.0 - ADAM_B2) * (g * g)
    m_hat = mn / (1.0 - ADAM_B1 ** ADAM_STEP)
    v_hat = vn / (1.0 - ADAM_B2 ** ADAM_STEP)
    return -ADAM_LR * (m_hat / (jnp.sqrt(v_hat) + ADAM_EPS) + ADAM_WD * w), mn, vn


def _adamw(w, g, m, v, name, stages=()):
    rows, cols = w.shape
    br = _row_block(rows, cols)

    def body(w_ref, g_ref, m_ref, v_ref, d_ref, mo_ref, vo_ref):
        d_ref[...], mo_ref[...], vo_ref[...] = _adamw_math(w_ref[...], g_ref[...], m_ref[...], v_ref[...])

    spec = pl.BlockSpec((br, cols), lambda r: (r, 0))
    return _call(body, name=name, grid=(rows // br,), in_specs=[spec] * 4, out_specs=[spec] * 3,
                 out_shape=[_sds(w)] * 3, vmem=32, args=[w, g, m, v], stages=stages)


SMALL_AT = {"norm_mix_pre": (0, 1, D), "norm_mix_post": (1, 1, D), "norm_mlp_pre": (2, 1, D),
            "norm_mlp_post": (3, 1, D), "b_gate": (4, 2, D), "conv_b": (6, 1, D), "lru_b_a": (7, 1, D),
            "lru_b_x": (8, 1, D), "lru_lambda": (9, 1, D), "pool_scale": (10, 1, DP)}
SMALL_SEPARATE = ["conv_w", "lru_w_a", "lru_w_x", "pool_w"]


def _adamw_small(small_sum, sep_grads, w, m, v):
    packed, sep = list(SMALL_AT), list(SMALL_SEPARATE)
    names = packed + sep

    def body(*refs):
        s_ref, refs = refs[0], refs[1:]
        g_sep, refs = refs[:len(sep)], refs[len(sep):]
        nn = len(names)
        w_r, m_r, v_r, refs = refs[:nn], refs[nn:2 * nn], refs[2 * nn:3 * nn], refs[3 * nn:]
        g_out, refs = refs[:len(packed)], refs[len(packed):]
        d_o, m_o, v_o = refs[:nn], refs[nn:2 * nn], refs[2 * nn:3 * nn]
        for i, n in enumerate(names):
            if n in SMALL_AT:
                r0, nr, nc = SMALL_AT[n]
                g = jnp.concatenate([s_ref[r0 + q:r0 + q + 1, :nc] for q in range(nr)], axis=1)
                g_out[i][...] = g
            else:
                g = g_sep[i - len(packed)][...]
            d_o[i][...], m_o[i][...], v_o[i][...] = _adamw_math(w_r[i][...], g, m_r[i][...], v_r[i][...])

    ws = [w[n] for n in names]
    res = pl.pallas_call(
        body, name="adamw_small",
        out_shape=[_sds(w[n]) for n in packed] + [_sds(a) for a in ws] * 3,
        compiler_params=_cp(32),
    )(*_hbm(small_sum, *sep_grads, *ws, *[m[n] for n in names], *[v[n] for n in names]))
    nn, npk = len(names), len(packed)
    grad = dict(zip(packed, res[:npk]))
    delta = dict(zip(names, res[npk:npk + nn]))
    new_m = dict(zip(names, res[npk + nn:npk + 2 * nn]))
    new_v = dict(zip(names, res[npk + 2 * nn:]))
    return grad, delta, new_m, new_v


W_NAMES = ["norm_mix_pre", "norm_mix_post", "norm_mlp_pre", "norm_mlp_post", "w_in", "b_gate", "conv_w", "conv_b",
           "lru_w_a", "lru_b_a", "lru_w_x", "lru_b_x", "lru_lambda", "pool_w", "pool_scale", "w_lru_up",
           "w_pool_up", "w_o", "w_ff1", "w_ff2"]
BIG = ["w_in", "w_lru_up", "w_pool_up", "w_o", "w_ff1", "w_ff2"]


def _block_diag(w):
    hd = w.shape[-1]
    per = CB // hd
    w4 = w.reshape(NG, per, hd, hd)
    eye = jnp.eye(per, dtype=w.dtype)
    return jnp.einsum("gpij,pq->gpiqj", w4, eye).reshape(NG, CB, CB)


def _block_diag_extract(d, hd):
    per = CB // hd
    d5 = d.reshape(NG, per, hd, per, hd)
    return jnp.stack([d5[:, p, :, p, :] for p in range(per)], axis=1).reshape(NG * per, hd, hd)


def _halves(g):
    return g.reshape(NCHIP, 2, g.size // (g.shape[-1] * 2 * NCHIP), g.shape[-1])


def kernel(x, norm_mix_pre, norm_mix_post, norm_mlp_pre, norm_mlp_post, w_in, b_gate, conv_w, conv_b, lru_w_a, lru_b_a, lru_w_x, lru_b_x, lru_lambda, pool_w, pool_scale, w_lru_up, w_pool_up, w_o, w_ff1, w_ff2, loss_target, m_norm_mix_pre, m_norm_mix_post, m_norm_mlp_pre, m_norm_mlp_post, m_w_in, m_b_gate, m_conv_w, m_conv_b, m_lru_w_a, m_lru_b_a, m_lru_w_x, m_lru_b_x, m_lru_lambda, m_pool_w, m_pool_scale, m_w_lru_up, m_w_pool_up, m_w_o, m_w_ff1, m_w_ff2, v_norm_mix_pre, v_norm_mix_post, v_norm_mlp_pre, v_norm_mlp_post, v_w_in, v_b_gate, v_conv_w, v_conv_b, v_lru_w_a, v_lru_b_a, v_lru_w_x, v_lru_b_x, v_lru_lambda, v_pool_w, v_pool_scale, v_w_lru_up, v_w_pool_up, v_w_o, v_w_ff1, v_w_ff2):
    args = dict(locals())
    two_d = lambda a: a.reshape(-1, a.shape[-1])
    w = {n: two_d(args[n]) for n in W_NAMES}
    mom = {n: two_d(args["m_" + n]) for n in W_NAMES}
    var = {n: two_d(args["v_" + n]) for n in W_NAMES}
    i32 = lambda val: jnp.asarray(val, jnp.int32)
    chip = i32(2 * lax.axis_index("x") + lax.axis_index("y"))
    core = i32(lax.axis_index("c"))
    cidx = core.reshape(1)
    zero = i32(0)
    hd = lru_w_a.shape[-1]
    xs, target = x[0], loss_target[0]
    g1, g2, g3, g4 = norm_mix_pre, norm_mix_post, norm_mlp_pre, norm_mlp_post

    full = {n: _cast_place(w[n], chip.reshape(1), "cast_" + n) for n in BIG}
    wa = _block_diag(lru_w_a[0]).astype(BF)
    wx = _block_diag(lru_w_x[0]).astype(BF)
    pw = pool_w[0].astype(BF)

    full["w_in"], conv_all = _gather_first(full["w_in"], w["conv_w"])
    conv_all = lax.dynamic_update_slice(conv_all, w["conv_w"][None], (chip, zero, zero))
    conv_full = jnp.transpose(conv_all, (1, 0, 2)).reshape(4, DR)
    mix = ["w_lru_up", "w_pool_up", "w_o"]
    ff1_a, ff1_b, ff2_a, ff2_b = (0, 3, 8), (3, 8, 8), (0, 1, 4), (1, 4, 4)
    (proj, h1), (got,) = _fwd_inproj(xs, g1, full["w_in"], stages=[_gather(
        [full[n] for n in mix] + [full["w_ff1"]], ici=[(0, ALL), (1, ALL), (2, ALL), (3, ff1_a)])])
    (ylru, hs), (got,) = _fwd_lru(proj, conv_full, conv_b, wa, lru_b_a, wx, lru_b_x, lru_lambda, stages=[_gather(
        got + [full["w_ff2"]], d2d=[(0, ALL), (1, ALL), (2, ALL), (3, ff1_a)], ici=[(3, ff1_b), (4, ff2_a)])])
    w_lru_up_f, w_pool_up_f, w_o_f = got[0].reshape(DR, D), got[1], got[2].reshape(D, D)
    ypool = _fwd_pool(proj, pw, pool_scale)
    (x2, h2, m, mrg, bra, brb), ((ff1, ff2),) = _fwd_merge(
        xs, ylru, ypool, proj, b_gate, g2, g3, w_lru_up_f, w_pool_up_f, w_o_f,
        stages=[_gather(got[3:], d2d=[(0, ff1_b), (1, ff2_a)], ici=[(1, ff2_b)])])
    ff2 = _comm_only("gather_last", _gather([ff2], d2d=[(0, ff2_b)]))[0].reshape(DF, D)
    a1, f = _fwd_mlp(h2, ff1, ff2)
    lossp, dy, df, dg4 = _loss_head(f, x2, target, g4)

    idx_big = jnp.stack([chip, (chip + 1) % NCHIP, (chip + 2) % NCHIP, (chip + 3) % NCHIP, core])
    dh2, df1 = _bwd_mlp_x(df, a1, ff1, ff2)
    dw_ff1, dw_ff2 = _bwd_mlp_w(df, h2, a1, df1)
    g_ff = [_halves(dw_ff1), _halves(dw_ff2)]
    (dxres, dgates, dylru, dypool, dw_o, dw_lru_up, dw_pool_up, dg2, dg3, dbg), (l_ff,) = _bwd_merge(
        dh2, dy, x2, m, mrg, bra, brb, proj, b_gate, ylru, ypool, g2, g3, w_lru_up_f, w_pool_up_f, w_o_f,
        stages=[_to_sibling(g_ff)])
    p_ff = [_add_sibling(g, l, cidx, "add_sibling_" + n)[0] for g, l, n in zip(g_ff, l_ff, ["w_ff1", "w_ff2"])]
    g_mix = [_halves(dw_lru_up), _halves(dw_pool_up), _halves(dw_o)]
    (dxp, dgl, dcw, dcb, dwa, dba, dwx, dbx, dlam), ((c_ff1,), l_mix) = _bwd_lru(
        proj, hs, dylru, conv_full, conv_b, wa, lru_b_a, wx, lru_b_x, lru_lambda,
        stages=[_to_chips(p_ff[:1]), _to_sibling(g_mix)])
    p_mix = [_add_sibling(g, l, cidx, "add_sibling_" + n)[0] for g, l, n in zip(g_mix, l_mix, mix)]
    dxpool, dpw, dsc = _bwd_pool(proj, dypool, pw, pool_scale)
    dproj = jnp.concatenate([dxp, dgl, dxpool, dgates], axis=1)
    (dw_in, dh1), (c_rest,) = _bwd_inproj(h1, dproj, full["w_in"], stages=[_to_chips(p_ff[1:] + p_mix)])
    done = ["w_ff1", "w_ff2"] + mix
    pairs = [_add_chips(p, l, idx_big, "add_chips_" + n) for p, l, n in zip(p_ff + p_mix, [c_ff1] + c_rest, done)]
    g_in = _halves(dw_in)
    (grad_x, dg1), ((l_in,), pairs) = _bwd_prenorm(xs, dh1, dxres, g1, stages=[_to_sibling([g_in]), _share(pairs)])

    small = jnp.concatenate([
        dg1, dg2, dg3, dg4, dbg.reshape(2, D), dcb, dba, dbx, dlam,
        jnp.pad(dsc, ((0, 0), (0, D - DP))), jnp.pad(lossp, ((0, 0), (0, D - 1))), dcw,
        _block_diag_extract(dwa, hd).reshape(-1, D), _block_diag_extract(dwx, hd).reshape(-1, D),
        dpw.reshape(-1, D)], axis=0)
    p_in, ((l_small,),) = _add_sibling(g_in, l_in, cidx, "add_sibling_w_in", stages=[_to_sibling([small])])
    small2 = _add_pair(small, l_small, "add_sibling_small").reshape(2, SMALL_ROWS // 2, D)

    grads, delta, new_m, new_v = {}, {}, {}, {}
    for n, p in zip(done, pairs):
        grads[n] = p.reshape(-1, p.shape[-1])

    def update(n, stages=()):
        (delta[n], new_m[n], new_v[n]), landed = _adamw(w[n], grads[n], mom[n], var[n], "adamw_" + n, stages=stages)
        return landed

    ((c_in, c_small),) = update("w_pool_up", stages=[_to_chips([p_in, small2])])
    pair_in = _add_chips(p_in, c_in, idx_big, "add_chips_w_in")
    own_small = lax.dynamic_index_in_dim(small2, core, 0, keepdims=True)
    c_small = lax.dynamic_update_slice(c_small, own_small, (chip, zero, zero))
    pair_small = _add_chips(c_small, c_small, jnp.stack([zero, zero + 1, zero + 2, zero + 3, core]), "add_chips_small")
    ((pair_in, pair_small),) = update("w_lru_up", stages=[_share([pair_in, pair_small])])
    grads["w_in"] = pair_in.reshape(-1, pair_in.shape[-1])
    for n in ["w_in", "w_o", "w_ff1", "w_ff2"]:
        update(n)
    small_sum = pair_small.reshape(SMALL_ROWS, D)
    loss = 0.5 * small_sum[LOSS_ROW, 0]
    ccols = DR // NCHIP
    sep = [lax.dynamic_slice(small_sum[12:16], (zero, chip * ccols), (4, ccols)),
           small_sum[16:80].reshape(-1, hd), small_sum[80:144].reshape(-1, hd), small_sum[144:208].reshape(-1, PG)]
    g_s, d_s, m_s, v_s = _adamw_small(small_sum, sep, w, mom, var)
    grads.update(g_s)
    grads.update(dict(zip(SMALL_SEPARATE, sep)))
    delta.update(d_s)
    new_m.update(m_s)
    new_v.update(v_s)

    out = lambda d: [d[n].reshape(args[n].shape) for n in W_NAMES]
    return (loss, grad_x[None], *out(grads), *out(delta), *out(new_m), *out(new_v))
```

```python
import functools
import math

import jax
import jax.numpy as jnp
from jax import lax
from jax.experimental import pallas as pl
from jax.experimental.pallas import tpu as pltpu

F32 = jnp.float32
BF = jnp.bfloat16

T = 2048
D = 1024
DR = 1024
DP = 512
DF = 4096
DIN = 4608
NCHIP = 4
CW_IN = DIN // NCHIP
LANE = 128
CB = 128
NG = DR // CB
PG = 128
POOL_WINDOWS = (2, 4, 8, 16)
NORM_EPS = 1e-6
LRU_C = 8.0
GELU_C = math.sqrt(2.0 / math.pi)
ADAM_LR = 0.001
ADAM_B1 = 0.9
ADAM_B2 = 0.999
ADAM_EPS = 1e-08
ADAM_WD = 0.01
ADAM_STEP = 10
MESH_ID = pl.DeviceIdType.MESH
ANY = pl.BlockSpec(memory_space=pl.ANY)
SMALL_ROWS = 208
LOSS_ROW = 11
MIB = 1 << 20


def _cp(vmem_mib=None):
    if vmem_mib is None:
        return pltpu.CompilerParams()
    return pltpu.CompilerParams(vmem_limit_bytes=vmem_mib * MIB)


def _hbm(*arrays):
    return [pltpu.with_memory_space_constraint(a, pltpu.HBM) for a in arrays]


def _hbm_out(shapes):
    return [pltpu.HBM(s.shape, s.dtype) for s in shapes]


class _Stage:
    def __init__(self, operands, out_shape, alias, sems, start, finish):
        self.operands, self.out_shape, self.alias, self.sems = list(operands), list(out_shape), dict(alias), list(sems)
        self.start, self.finish = start, finish


def _call(body, *, name, grid, in_specs, out_specs, out_shape, args, vmem=None, stages=(), prefetch=None,
          scratch=()):
    nin, nout = len(in_specs), len(out_specs)
    npre = 0 if prefetch is None else 1
    st_args, st_shapes, st_sems, aliases = [], [], list(scratch), {}
    for st in stages:
        for k, v in st.alias.items():
            aliases[npre + nin + len(st_args) + k] = nout + len(st_shapes) + v
        st_args += st.operands
        st_shapes += st.out_shape
        st_sems += st.sems

    def wrapped(*refs):
        pre, refs = refs[:npre], refs[npre:]
        ins, pos = refs[:nin], nin
        st_ins = []
        for st in stages:
            st_ins.append(refs[pos:pos + len(st.operands)])
            pos += len(st.operands)
        outs, pos = refs[pos:pos + nout], pos + nout
        st_outs = []
        for st in stages:
            st_outs.append(refs[pos:pos + len(st.out_shape)])
            pos += len(st.out_shape)
        work, pos = refs[pos:pos + len(scratch)], pos + len(scratch)
        sems = []
        for st in stages:
            sems.append(refs[pos:pos + len(st.sems)])
            pos += len(st.sems)
        if stages:
            first = functools.reduce(jnp.logical_and, [pl.program_id(a) == 0 for a in range(len(grid))])

            @pl.when(first)
            def _():
                for st, a, b, s in zip(stages, st_ins, st_outs, sems):
                    st.start(a, b, s)

        body(*pre, *ins, *outs, *work)
        if stages:
            last = functools.reduce(jnp.logical_and, [pl.program_id(a) == g - 1 for a, g in enumerate(grid)])

            @pl.when(last)
            def _():
                for st, a, b, s in zip(stages, st_ins, st_outs, sems):
                    st.finish(a, b, s)

    all_in = list(in_specs) + [ANY] * len(st_args)
    all_out = list(out_specs) + [ANY] * len(st_shapes)
    kw = dict(has_side_effects=True) if stages else {}
    if vmem is not None:
        kw["vmem_limit_bytes"] = vmem * MIB
    if prefetch is None:
        gkw = dict(grid=grid, in_specs=all_in, out_specs=all_out, scratch_shapes=st_sems)
    else:
        gkw = dict(grid_spec=pltpu.PrefetchScalarGridSpec(
            num_scalar_prefetch=1, grid=grid, in_specs=all_in, out_specs=all_out, scratch_shapes=st_sems))
    res = pl.pallas_call(
        wrapped, name=name, out_shape=_hbm_out(list(out_shape) + st_shapes), input_output_aliases=aliases,
        compiler_params=pltpu.CompilerParams(**kw), **gkw,
    )(*([prefetch] if npre else []), *_hbm(*args, *st_args))
    outs, rest, st_res = list(res[:nout]), list(res[nout:]), []
    for st in stages:
        st_res.append(rest[:len(st.out_shape)])
        rest = rest[len(st.out_shape):]
    return outs, st_res


def _mm(a, b):
    return jnp.dot(a.astype(BF), b.astype(BF), preferred_element_type=F32)


def _mm_nt(a, b):
    return lax.dot_general(a.astype(BF), b.astype(BF), (((1,), (1,)), ((), ())),
                           preferred_element_type=F32)


def _mm_tn(a, b):
    return lax.dot_general(a.astype(BF), b.astype(BF), (((0,), (0,)), ((), ())),
                           preferred_element_type=F32)


def _rows(v):
    return lax.broadcasted_iota(jnp.int32, v.shape, 0)


def _sd(v, s, fill=0.0):
    return jnp.where(_rows(v) >= s, pltpu.roll(v, s, axis=0), fill)


def _su(v, s, fill=0.0):
    n = v.shape[0]
    return jnp.where(_rows(v) < n - s, pltpu.roll(v, n - s, axis=0), fill)


def _sigmoid(z):
    return 1.0 / (1.0 + jnp.exp(-z))


def _softplus(z):
    e = jnp.exp(-jnp.abs(z))
    u = 1.0 + e
    d = u - 1.0
    log1p = jnp.where(d == 0.0, e, jnp.log(u) * (e / jnp.where(d == 0.0, 1.0, d)))
    return jnp.maximum(z, 0.0) + log1p


def _mean(v):
    return jnp.mean(v, axis=-1, keepdims=True)


def _colsum(v):
    return jnp.sum(v, axis=0, keepdims=True)


def _acc(ref, val, first):
    @pl.when(first)
    def _():
        ref[...] = val

    @pl.when(jnp.logical_not(first))
    def _():
        ref[...] += val


def _conv(xp, cw, cb):
    x1, x2, x3 = _sd(xp, 1), _sd(xp, 2), _sd(xp, 3)
    xc = cb + cw[0:1] * x3 + cw[1:2] * x2 + cw[2:3] * x1 + cw[3:4] * xp
    return xc, x1, x2, x3


def _lru_gates(xc, wa, ba, wx, bx, lam):
    xcb = xc.astype(BF)
    r = _sigmoid(_mm(xcb, wa) + ba)
    ii = _sigmoid(_mm(xcb, wx) + bx)
    sp = _softplus(-lam)
    la = (-LRU_C) * r * sp
    a = jnp.exp(la)
    mult = jnp.sqrt(-jnp.tanh(la) * (a * a + 1.0))
    return xcb, r, ii, sp, a, mult


def _gelu_parts(g):
    th = jnp.tanh(GELU_C * (g + 0.044715 * (g * g * g)))
    gel = 0.5 * g * (1.0 + th)
    dgel = 0.5 * (1.0 + th) + 0.5 * g * (1.0 - th * th) * (GELU_C * (1.0 + 3.0 * 0.044715 * (g * g)))
    return gel, dgel


def _pool_window(x, steps, shift):
    s, sh = x, 1
    for _ in range(steps):
        s = s + shift(s, sh)
        sh *= 2
    return s


def _fwd_inproj(x, g1, w_in, stages=()):
    tm = 512

    def body(x_ref, g_ref, w_ref, proj_ref, h_ref):
        @pl.when(pl.program_id(1) == 0)
        def _():
            xv = x_ref[...]
            r = lax.rsqrt(_mean(xv * xv) + NORM_EPS)
            h_ref[...] = ((xv * r) * g_ref[...]).astype(BF)

        proj_ref[...] = jnp.dot(h_ref[...], w_ref[0], preferred_element_type=F32)

    return _call(
        body, name="fwd_inproj", grid=(T // tm, NCHIP),
        in_specs=[pl.BlockSpec((tm, D), lambda i, k: (i, 0)),
                  pl.BlockSpec((1, D), lambda i, k: (0, 0)),
                  pl.BlockSpec((1, D, CW_IN), lambda i, k: (k, 0, 0))],
        out_specs=[pl.BlockSpec((tm, CW_IN), lambda i, k: (i, k)),
                   pl.BlockSpec((tm, D), lambda i, k: (i, 0))],
        out_shape=[jax.ShapeDtypeStruct((T, DIN), F32), jax.ShapeDtypeStruct((T, D), BF)],
        vmem=40, args=[x, g1, w_in], stages=stages)


def _vec_spec():
    return pl.BlockSpec((1, CB), lambda j: (0, j))


def _fwd_lru(proj, conv_w, conv_b, wa, ba, wx, bx, lam, stages=()):
    def body(xp_ref, g_ref, cw_ref, cb_ref, wa_ref, ba_ref, wx_ref, bx_ref, lam_ref, y_ref, h_ref):
        xc, _, _, _ = _conv(xp_ref[...], cw_ref[...], cb_ref[...])
        _, _, ii, _, a, mult = _lru_gates(xc, wa_ref[0], ba_ref[...], wx_ref[0], bx_ref[...], lam_ref[...])
        b = mult * (ii * xc)
        s = 1
        while s < T:
            b = b + a * _sd(b, s, 0.0)
            if 2 * s < T:
                a = a * _sd(a, s, 1.0)
            s *= 2
        h_ref[...] = b
        gel, _ = _gelu_parts(g_ref[...])
        y_ref[...] = (b * gel).astype(BF)

    return _call(
        body, name="fwd_lru", grid=(NG,),
        in_specs=[pl.BlockSpec((T, CB), lambda j: (0, j)),
                  pl.BlockSpec((T, CB), lambda j: (0, NG + j)),
                  pl.BlockSpec((4, CB), lambda j: (0, j)),
                  _vec_spec(),
                  pl.BlockSpec((1, CB, CB), lambda j: (j, 0, 0)), _vec_spec(),
                  pl.BlockSpec((1, CB, CB), lambda j: (j, 0, 0)), _vec_spec(),
                  _vec_spec()],
        out_specs=[pl.BlockSpec((T, CB), lambda j: (0, j)), pl.BlockSpec((T, CB), lambda j: (0, j))],
        out_shape=[jax.ShapeDtypeStruct((T, DR), BF), jax.ShapeDtypeStruct((T, DR), F32)],
        vmem=48, args=[proj, proj, conv_w, conv_b, wa, ba, wx, bx, lam], stages=stages)


def _pool_cnt(w):
    t = lax.broadcasted_iota(jnp.int32, (T, 1), 0)
    return jnp.minimum(t + 1, w).astype(F32)


def _fwd_pool(proj, pool_w, pool_scale):
    def body(xp_ref, pw_ref, sc_ref, y_ref):
        for g, w in enumerate(POOL_WINDOWS):
            cols = slice(g * PG, (g + 1) * PG)
            x = xp_ref[:, cols]
            p = _pool_window(x, g + 1, _sd) / _pool_cnt(w) - x
            y_ref[:, cols] = (_mm(p, pw_ref[g]) * sc_ref[:, cols]).astype(BF)

    return pl.pallas_call(
        body, name="fwd_pool", grid=(1,),
        in_specs=[pl.BlockSpec((T, DP), lambda i: (0, 2 * DR // DP)),
                  pl.BlockSpec((4, PG, PG), lambda i: (0, 0, 0)),
                  pl.BlockSpec((1, DP), lambda i: (0, 0))],
        out_specs=pl.BlockSpec((T, DP), lambda i: (0, 0)),
        out_shape=pltpu.HBM((T, DP), BF),
        compiler_params=_cp(48),
    )(*_hbm(proj, pool_w, pool_scale))


GATE_BLK = 512
GATE_BLK0 = (2 * DR + DP) // GATE_BLK


def _gate_specs(tm):
    return [pl.BlockSpec((tm, GATE_BLK), functools.partial(lambda i, q: (i, GATE_BLK0 + q), q=q))
            for q in range(4)]


def _fwd_merge(x, ylru, ypool, proj, b_gate, g2, g3, w_lru_up, w_pool_up, w_o, stages=()):
    tm = 512

    def body(x_ref, yl_ref, yp_ref, p0, p1, p2, p3, bg_ref, g2_ref, g3_ref, wl_ref, wp_ref, wo_ref,
             x2_ref, h2_ref, m_ref, mrg_ref, bra_ref, brb_ref):
        bra = jnp.dot(yl_ref[...], wl_ref[...], preferred_element_type=F32)
        yp = yp_ref[...]
        brb = jnp.concatenate([jnp.dot(yp, wp_ref[k], preferred_element_type=F32) for k in range(NCHIP)], axis=1)
        bg = bg_ref[...]
        ga = _sigmoid(jnp.concatenate([p0[...], p1[...]], axis=1) + bg[:, :D])
        gb = _sigmoid(jnp.concatenate([p2[...], p3[...]], axis=1) + bg[:, D:])
        mrg = (ga * bra + gb * brb).astype(BF)
        m = jnp.dot(mrg, wo_ref[...], preferred_element_type=F32)
        r2 = lax.rsqrt(_mean(m * m) + NORM_EPS)
        x2 = x_ref[...] + (m * r2) * g2_ref[...]
        r3 = lax.rsqrt(_mean(x2 * x2) + NORM_EPS)
        x2_ref[...] = x2
        h2_ref[...] = ((x2 * r3) * g3_ref[...]).astype(BF)
        m_ref[...] = m
        mrg_ref[...] = mrg
        bra_ref[...] = bra.astype(BF)
        brb_ref[...] = brb.astype(BF)

    row = lambda w: pl.BlockSpec((tm, w), lambda i: (i, 0))
    full2 = lambda a, b: pl.BlockSpec((a, b), lambda i: (0, 0))
    return _call(
        body, name="fwd_merge", grid=(T // tm,),
        in_specs=[row(D), row(DR), row(DP)] + _gate_specs(tm) +
                 [full2(1, 2 * D), full2(1, D), full2(1, D), full2(DR, D),
                  pl.BlockSpec((NCHIP, DP, D // NCHIP), lambda i: (0, 0, 0)), full2(D, D)],
        out_specs=[row(D)] * 6,
        out_shape=[jax.ShapeDtypeStruct((T, D), F32), jax.ShapeDtypeStruct((T, D), BF),
                   jax.ShapeDtypeStruct((T, D), F32), jax.ShapeDtypeStruct((T, D), BF),
                   jax.ShapeDtypeStruct((T, D), BF), jax.ShapeDtypeStruct((T, D), BF)],
        vmem=48, args=[x, ylru, ypool, proj, proj, proj, proj, b_gate, g2, g3, w_lru_up, w_pool_up, w_o],
        stages=stages)


def _fwd_mlp(h2, w_ff1, w_ff2):
    tm = 512
    fk = DF // NCHIP

    def body(h_ref, w1_ref, w2_ref, a1_ref, f_ref):
        h = h_ref[...]
        f = None
        for k in range(NCHIP):
            a1 = jnp.maximum(jnp.dot(h, w1_ref[k], preferred_element_type=F32), 0.0)
            a1_ref[:, k * fk:(k + 1) * fk] = a1.astype(BF)
            part = jnp.dot((a1 * a1).astype(BF), w2_ref[k * fk:(k + 1) * fk, :], preferred_element_type=F32)
            f = part if f is None else f + part
        f_ref[...] = f

    return pl.pallas_call(
        body, name="fwd_mlp", grid=(T // tm,),
        in_specs=[pl.BlockSpec((tm, D), lambda i: (i, 0)),
                  pl.BlockSpec((NCHIP, D, fk), lambda i: (0, 0, 0)),
                  pl.BlockSpec((DF, D), lambda i: (0, 0))],
        out_specs=[pl.BlockSpec((tm, DF), lambda i: (i, 0)), pl.BlockSpec((tm, D), lambda i: (i, 0))],
        out_shape=_hbm_out([jax.ShapeDtypeStruct((T, DF), BF), jax.ShapeDtypeStruct((T, D), F32)]),
        compiler_params=_cp(56),
    )(*_hbm(h2, w_ff1, w_ff2))


def _loss_head(f, x2, target, g4):
    tm = 512

    def body(f_ref, x2_ref, t_ref, g_ref, loss_ref, dy_ref, df_ref, dg_ref):
        first = pl.program_id(0) == 0
        f = f_ref[...]
        g4v = g_ref[...]
        r4 = lax.rsqrt(_mean(f * f) + NORM_EPS)
        fn = f * r4
        e = (x2_ref[...] + fn * g4v) - t_ref[...]
        _acc(loss_ref, jnp.sum(_mean(e * e), axis=0, keepdims=True), first)
        dy = e * (1.0 / D)
        dy_ref[...] = dy
        _acc(dg_ref, _colsum(dy * fn), first)
        dfn = dy * g4v
        df_ref[...] = (r4 * (dfn - fn * _mean(dfn * fn))).astype(BF)

    row = pl.BlockSpec((tm, D), lambda i: (i, 0))
    return pl.pallas_call(
        body, name="loss_head", grid=(T // tm,),
        in_specs=[row, row, row, pl.BlockSpec((1, D), lambda i: (0, 0))],
        out_specs=[pl.BlockSpec((1, 1), lambda i: (0, 0)), row, row, pl.BlockSpec((1, D), lambda i: (0, 0))],
        out_shape=_hbm_out([jax.ShapeDtypeStruct((1, 1), F32), jax.ShapeDtypeStruct((T, D), F32),
                            jax.ShapeDtypeStruct((T, D), BF), jax.ShapeDtypeStruct((1, D), F32)]),
        compiler_params=_cp(48),
    )(*_hbm(f, x2, target, g4))


def _bwd_mlp_x(df, a1, w_ff1, w_ff2):
    tm = 512
    fk = DF // NCHIP

    def body(df_ref, a1_ref, w1_ref, w2_ref, dh_ref, df1_ref):
        df = df_ref[...]
        dh = None
        for k in range(NCHIP):
            cols = slice(k * fk, (k + 1) * fk)
            dact = _mm_nt(df, w2_ref[cols, :])
            df1 = (dact * (2.0 * a1_ref[:, cols].astype(F32))).astype(BF)
            df1_ref[:, cols] = df1
            part = _mm_nt(df1, w1_ref[k])
            dh = part if dh is None else dh + part
        dh_ref[...] = dh

    return pl.pallas_call(
        body, name="bwd_mlp_x", grid=(T // tm,),
        in_specs=[pl.BlockSpec((tm, D), lambda i: (i, 0)),
                  pl.BlockSpec((tm, DF), lambda i: (i, 0)),
                  pl.BlockSpec((NCHIP, D, fk), lambda i: (0, 0, 0)),
                  pl.BlockSpec((DF, D), lambda i: (0, 0))],
        out_specs=[pl.BlockSpec((tm, D), lambda i: (i, 0)), pl.BlockSpec((tm, DF), lambda i: (i, 0))],
        out_shape=_hbm_out([jax.ShapeDtypeStruct((T, D), F32), jax.ShapeDtypeStruct((T, DF), BF)]),
        compiler_params=_cp(56),
    )(*_hbm(df, a1, w_ff1, w_ff2))


def _bwd_mlp_w(df, h2, a1, df1):
    fc = 512
    per = (DF // NCHIP) // fc

    def body(df_ref, h_ref, a1_ref, df1_ref, dw1_ref, dw2_ref):
        a1 = a1_ref[...].astype(F32)
        dw2_ref[...] = _mm_tn((a1 * a1).astype(BF), df_ref[...])
        dw1_ref[0] = _mm_tn(h_ref[...], df1_ref[...])

    return pl.pallas_call(
        body, name="bwd_mlp_w", grid=(DF // fc,),
        in_specs=[pl.BlockSpec((T, D), lambda j: (0, 0)),
                  pl.BlockSpec((T, D), lambda j: (0, 0)),
                  pl.BlockSpec((T, fc), lambda j: (0, j)),
                  pl.BlockSpec((T, fc), lambda j: (0, j))],
        out_specs=[pl.BlockSpec((1, D, fc), lambda j: (j // per, 0, j % per)),
                   pl.BlockSpec((fc, D), lambda j: (j, 0))],
        out_shape=_hbm_out([jax.ShapeDtypeStruct((NCHIP, D, DF // NCHIP), F32),
                            jax.ShapeDtypeStruct((DF, D), F32)]),
        compiler_params=_cp(56),
    )(*_hbm(df, h2, a1, df1))


def _bwd_merge(dh2, dy, x2, m, bra, brb, proj, b_gate, g2, g3, w_lru_up, w_pool_up, w_o, stages=()):
    tm = 256
    cpu = D // NCHIP

    def body(dh2_ref, dy_ref, x2_ref, m_ref, bra_ref, brb_ref, p0, p1, p2, p3, bg_ref,
             g2_ref, g3_ref, wl_ref, wp_ref, wo_ref,
             dx_ref, dgt_ref, dyl_ref, dyp_ref, dm_ref, dbra_ref, dbrb_ref, dg2_ref, dg3_ref, dbg_ref):
        first = pl.program_id(0) == 0
        x2 = x2_ref[...]
        r3 = lax.rsqrt(_mean(x2 * x2) + NORM_EPS)
        x2n = x2 * r3
        dh2 = dh2_ref[...]
        t3 = dh2 * g3_ref[...]
        dx2 = dy_ref[...] + r3 * (t3 - x2n * _mean(t3 * x2n))
        dx_ref[...] = dx2
        _acc(dg3_ref, _colsum(dh2 * x2n), first)
        m = m_ref[...]
        r2 = lax.rsqrt(_mean(m * m) + NORM_EPS)
        mn = m * r2
        _acc(dg2_ref, _colsum(dx2 * mn), first)
        dmn = dx2 * g2_ref[...]
        dm = (r2 * (dmn - mn * _mean(dmn * mn))).astype(BF)
        dm_ref[...] = dm
        dmrg = _mm_nt(dm, wo_ref[...])
        bg = bg_ref[...]
        ga = _sigmoid(jnp.concatenate([p0[...], p1[...]], axis=1) + bg[:, :D])
        gb = _sigmoid(jnp.concatenate([p2[...], p3[...]], axis=1) + bg[:, D:])
        dga = dmrg * bra_ref[...].astype(F32) * (ga * (1.0 - ga))
        dgb = dmrg * brb_ref[...].astype(F32) * (gb * (1.0 - gb))
        dgt_ref[:, :D] = dga.astype(BF)
        dgt_ref[:, D:] = dgb.astype(BF)
        _acc(dbg_ref, jnp.concatenate([_colsum(dga), _colsum(dgb)], axis=1), first)
        dbra = (dmrg * ga).astype(BF)
        dbrb = (dmrg * gb).astype(BF)
        dbra_ref[...] = dbra
        dbrb_ref[...] = dbrb
        dyl_ref[...] = _mm_nt(dbra, wl_ref[...])
        dyp = None
        for k in range(NCHIP):
            part = _mm_nt(dbrb[:, k * cpu:(k + 1) * cpu], wp_ref[k])
            dyp = part if dyp is None else dyp + part
        dyp_ref[...] = dyp

    row = lambda w: pl.BlockSpec((tm, w), lambda i: (i, 0))
    full2 = lambda a, b: pl.BlockSpec((a, b), lambda i: (0, 0))
    wp_spec = pl.BlockSpec((NCHIP, DP, cpu), lambda i: (0, 0, 0))
    return _call(
        body, name="bwd_merge", grid=(T // tm,),
        in_specs=[row(D)] * 6 + _gate_specs(tm) +
                 [full2(1, 2 * D), full2(1, D), full2(1, D), full2(DR, D), wp_spec, full2(D, D)],
        out_specs=[row(D), row(2 * D), row(DR), row(DP), row(D), row(D), row(D),
                   full2(1, D), full2(1, D), full2(1, 2 * D)],
        out_shape=[jax.ShapeDtypeStruct((T, D), F32), jax.ShapeDtypeStruct((T, 2 * D), BF),
                   jax.ShapeDtypeStruct((T, DR), F32), jax.ShapeDtypeStruct((T, DP), F32),
                   jax.ShapeDtypeStruct((T, D), BF), jax.ShapeDtypeStruct((T, D), BF),
                   jax.ShapeDtypeStruct((T, D), BF),
                   jax.ShapeDtypeStruct((1, D), F32), jax.ShapeDtypeStruct((1, D), F32),
                   jax.ShapeDtypeStruct((1, 2 * D), F32)],
        vmem=56, args=[dh2, dy, x2, m, bra, brb, proj, proj, proj, proj, b_gate, g2, g3, w_lru_up, w_pool_up, w_o],
        stages=stages)


def _dw_merge(mrg, dm, ylru, dbra, ypool, dbrb):
    nb = NCHIP
    rb, pb, cpu = D // nb, DP // nb, D // NCHIP

    def body(mrg_ref, dm_ref, yl_ref, dbra_ref, yp_ref, dbrb_ref, dwo_ref, dwl_ref, dwp_ref):
        dwo_ref[...] = _mm_tn(mrg_ref[...], dm_ref[...])
        dwl_ref[...] = _mm_tn(yl_ref[...], dbra_ref[...])
        dwp = _mm_tn(yp_ref[...], dbrb_ref[...])
        for k in range(NCHIP):
            dwp_ref[k] = dwp[:, k * cpu:(k + 1) * cpu]

    cols = lambda w: pl.BlockSpec((T, w), lambda r: (0, r))
    whole = pl.BlockSpec((T, D), lambda r: (0, 0))
    return pl.pallas_call(
        body, name="dw_merge", grid=(nb,),
        in_specs=[cols(rb), whole, cols(rb), whole, cols(pb), whole],
        out_specs=[pl.BlockSpec((rb, D), lambda r: (r, 0)), pl.BlockSpec((rb, D), lambda r: (r, 0)),
                   pl.BlockSpec((NCHIP, pb, cpu), lambda r: (0, r, 0))],
        out_shape=_hbm_out([jax.ShapeDtypeStruct((D, D), F32), jax.ShapeDtypeStruct((DR, D), F32),
                            jax.ShapeDtypeStruct((NCHIP, DP, cpu), F32)]),
        compiler_params=_cp(56),
    )(*_hbm(mrg, dm, ylru, dbra, ypool, dbrb))


def _bwd_lru(proj, h, dylru, conv_w, conv_b, wa, ba, wx, bx, lam, stages=()):
    def body(xp_ref, g_ref, h_ref, dy_ref, cw_ref, cb_ref, wa_ref, ba_ref, wx_ref, bx_ref, lam_ref,
             dxp_ref, dg_ref, dcw_ref, dcb_ref, dwa_ref, dba_ref, dwx_ref, dbx_ref, dlam_ref):
        xp = xp_ref[...]
        cw = cw_ref[...]
        lam = lam_ref[...]
        xc, x1, x2, x3 = _conv(xp, cw, cb_ref[...])
        wa, wx = wa_ref[0], wx_ref[0]
        xcb, r, ii, sp, a, mult = _lru_gates(xc, wa, ba_ref[...], wx, bx_ref[...], lam)
        g = g_ref[...]
        gel, dgel = _gelu_parts(g)
        h = h_ref[...]
        dy = dy_ref[...]
        dg_ref[...] = (dy * h * dgel).astype(BF)
        b = dy * gel
        aa = _su(a, 1, 0.0)
        s = 1
        while s < T:
            b = b + aa * _su(b, s, 0.0)
            if 2 * s < T:
                aa = aa * _su(aa, s, 0.0)
            s *= 2
        da = b * _sd(h, 1, 0.0)
        dmult = b * (ii * xc)
        dii = b * (mult * xc)
        dxc = b * (mult * ii)
        dla = da * a - dmult * ((a * a) / mult)
        dr = dla * ((-LRU_C) * sp)
        dsp = _colsum(dla * ((-LRU_C) * r))
        dlam_ref[...] = -dsp / (1.0 + jnp.exp(lam))
        dzr = dr * (r * (1.0 - r))
        dzi = dii * (ii * (1.0 - ii))
        dzrb, dzib = dzr.astype(BF), dzi.astype(BF)
        dxc = dxc + _mm_nt(dzrb, wa) + _mm_nt(dzib, wx)
        dwa_ref[0] = _mm_tn(xcb, dzrb)
        dwx_ref[0] = _mm_tn(xcb, dzib)
        dba_ref[...] = _colsum(dzr)
        dbx_ref[...] = _colsum(dzi)
        dcb_ref[...] = _colsum(dxc)
        dcw_ref[...] = jnp.concatenate([_colsum(dxc * x3), _colsum(dxc * x2), _colsum(dxc * x1),
                                        _colsum(dxc * xp)], axis=0)
        dxp = cw[3:4] * dxc + cw[2:3] * _su(dxc, 1) + cw[1:2] * _su(dxc, 2) + cw[0:1] * _su(dxc, 3)
        dxp_ref[...] = dxp.astype(BF)

    blk = pl.BlockSpec((T, CB), lambda j: (0, j))
    wsp = pl.BlockSpec((1, CB, CB), lambda j: (j, 0, 0))
    return _call(
        body, name="bwd_lru", grid=(NG,),
        in_specs=[blk, pl.BlockSpec((T, CB), lambda j: (0, NG + j)), blk, blk,
                  pl.BlockSpec((4, CB), lambda j: (0, j)), _vec_spec(), wsp, _vec_spec(), wsp, _vec_spec(),
                  _vec_spec()],
        out_specs=[blk, blk, pl.BlockSpec((4, CB), lambda j: (0, j)), _vec_spec(), wsp, _vec_spec(), wsp,
                   _vec_spec(), _vec_spec()],
        out_shape=[jax.ShapeDtypeStruct((T, DR), BF), jax.ShapeDtypeStruct((T, DR), BF),
                   jax.ShapeDtypeStruct((4, DR), F32), jax.ShapeDtypeStruct((1, DR), F32),
                   jax.ShapeDtypeStruct((NG, CB, CB), F32), jax.ShapeDtypeStruct((1, DR), F32),
                   jax.ShapeDtypeStruct((NG, CB, CB), F32), jax.ShapeDtypeStruct((1, DR), F32),
                   jax.ShapeDtypeStruct((1, DR), F32)],
        vmem=56, args=[proj, proj, h, dylru, conv_w, conv_b, wa, ba, wx, bx, lam], stages=stages)


def _bwd_pool(proj, dypool, pool_w, pool_scale):
    def body(xp_ref, dy_ref, pw_ref, sc_ref, dx_ref, dw_ref, dsc_ref):
        for g, w in enumerate(POOL_WINDOWS):
            cols = slice(g * PG, (g + 1) * PG)
            cnt = _pool_cnt(w)
            x = xp_ref[:, cols]
            pb = (_pool_window(x, g + 1, _sd) / cnt - x).astype(BF)
            wg = pw_ref[g]
            dy = dy_ref[:, cols]
            dsc_ref[:, cols] = _colsum(dy * _mm(pb, wg))
            dyp = (dy * sc_ref[:, cols]).astype(BF)
            dw_ref[g] = _mm_tn(pb, dyp)
            dp = _mm_nt(dyp, wg)
            dx_ref[:, cols] = (_pool_window(dp / cnt, g + 1, _su) - dp).astype(BF)

    return pl.pallas_call(
        body, name="bwd_pool", grid=(1,),
        in_specs=[pl.BlockSpec((T, DP), lambda i: (0, 2 * DR // DP)),
                  pl.BlockSpec((T, DP), lambda i: (0, 0)),
                  pl.BlockSpec((4, PG, PG), lambda i: (0, 0, 0)),
                  pl.BlockSpec((1, DP), lambda i: (0, 0))],
        out_specs=[pl.BlockSpec((T, DP), lambda i: (0, 0)),
                   pl.BlockSpec((4, PG, PG), lambda i: (0, 0, 0)),
                   pl.BlockSpec((1, DP), lambda i: (0, 0))],
        out_shape=_hbm_out([jax.ShapeDtypeStruct((T, DP), BF), jax.ShapeDtypeStruct((4, PG, PG), F32),
                            jax.ShapeDtypeStruct((1, DP), F32)]),
        compiler_params=_cp(48),
    )(*_hbm(proj, dypool, pool_w, pool_scale))


def _bwd_inproj(h1, dproj, w_in, stages=()):
    def body(h_ref, dp_ref, w_ref, dw_ref, dh_ref):
        dp = dp_ref[...]
        dw_ref[0] = _mm_tn(h_ref[...], dp)
        _acc(dh_ref, _mm_nt(dp, w_ref[0]), pl.program_id(0) == 0)

    return _call(
        body, name="bwd_inproj", grid=(NCHIP,),
        in_specs=[pl.BlockSpec((T, D), lambda k: (0, 0)),
                  pl.BlockSpec((T, CW_IN), lambda k: (0, k)),
                  pl.BlockSpec((1, D, CW_IN), lambda k: (k, 0, 0))],
        out_specs=[pl.BlockSpec((1, D, CW_IN), lambda k: (k, 0, 0)), pl.BlockSpec((T, D), lambda k: (0, 0))],
        out_shape=[jax.ShapeDtypeStruct((NCHIP, D, CW_IN), F32), jax.ShapeDtypeStruct((T, D), F32)],
        vmem=56, args=[h1, dproj, w_in], stages=stages)


def _bwd_prenorm(x, dh1, dxres, g1, stages=()):
    tm = 512

    def body(x_ref, dh_ref, dr_ref, g_ref, dx_ref, dg_ref):
        xv = x_ref[...]
        r = lax.rsqrt(_mean(xv * xv) + NORM_EPS)
        xn = xv * r
        dh = dh_ref[...]
        t = dh * g_ref[...]
        dx_ref[...] = dr_ref[...] + r * (t - xn * _mean(t * xn))
        _acc(dg_ref, _colsum(dh * xn), pl.program_id(0) == 0)

    row = pl.BlockSpec((tm, D), lambda i: (i, 0))
    vec = pl.BlockSpec((1, D), lambda i: (0, 0))
    return _call(
        body, name="bwd_prenorm", grid=(T // tm,),
        in_specs=[row, row, row, vec], out_specs=[row, vec],
        out_shape=[jax.ShapeDtypeStruct((T, D), F32), jax.ShapeDtypeStruct((1, D), F32)],
        vmem=48, args=[x, dh1, dxres, g1], stages=stages)


def _place():
    x, y, c = lax.axis_index("x"), lax.axis_index("y"), lax.axis_index("c")
    chips = [(1 - x, y), (x, 1 - y), (1 - x, 1 - y)]
    return x, y, c, chips


def _rcopy(src, dst, ssem, rsem, dev):
    return pltpu.make_async_remote_copy(src_ref=src, dst_ref=dst, send_sem=ssem, recv_sem=rsem,
                                        device_id=dev, device_id_type=MESH_ID)


def _sds(a):
    return jax.ShapeDtypeStruct(a.shape, a.dtype)


def _sem2(n, m):
    return [pltpu.SemaphoreType.DMA((n, m)), pltpu.SemaphoreType.DMA((n, m))]


ALL = (0, 1, 1)


def _piece(ref, k, half, part):
    hr = ref.shape[1] // 2
    r0, r1 = hr * part[0] // part[2], hr * part[1] // part[2]
    return ref.at[k, pl.ds(half * hr + r0, r1 - r0), :]


def _gather(fulls, ici=(), d2d=()):
    n = len(fulls)
    ici, d2d = list(ici), list(d2d)

    def copies(outs, sems):
        x, y, c, chips = _place()
        me = 2 * x + y
        sib = (x, y, 1 - c)
        send, recv = [], []
        for q, (i, part) in enumerate(ici):
            for j, chip in enumerate(chips):
                mine, theirs = _piece(outs[i], me, c, part), _piece(outs[i], 2 * chip[0] + chip[1], c, part)
                send.append(_rcopy(mine, mine, sems[0].at[q, j], sems[1].at[q, j], (*chip, c)))
                recv.append(_rcopy(theirs, theirs, sems[0].at[q, j], sems[1].at[q, j], (*chip, c)))
        for q, (i, part) in enumerate(d2d):
            for j, chip in enumerate(chips):
                k = 2 * chip[0] + chip[1]
                got, other = _piece(outs[i], k, c, part), _piece(outs[i], k, 1 - c, part)
                send.append(_rcopy(got, got, sems[2].at[q, j], sems[3].at[q, j], sib))
                recv.append(_rcopy(other, other, sems[2].at[q, j], sems[3].at[q, j], sib))
        return send, recv

    def start(ins, outs, sems):
        for cp in copies(outs, sems)[0]:
            cp.start()

    def finish(ins, outs, sems):
        send, recv = copies(outs, sems)
        for cp in recv:
            cp.wait_recv()
        for cp in send:
            cp.wait_send()

    sems = _sem2(max(len(ici), 1), 3) + _sem2(max(len(d2d), 1), 3)
    return _Stage(fulls, [_sds(f) for f in fulls], {i: i for i in range(n)}, sems, start, finish)


def _gather_first(full, conv_w):
    ici, d2d = _gather([full], ici=[(0, ALL)]), _gather([full], d2d=[(0, ALL)])

    def body(full_in, cw_in, full_out, cw_out, s0, r0, s1, r1, cs, cr):
        x, y, c, chips = _place()
        me = 2 * x + y
        conv = [_rcopy(cw_in, cw_out.at[me], cs.at[j], cr.at[j], (*chip, c)) for j, chip in enumerate(chips)]
        for cp in conv:
            cp.start()
        sems = [s0, r0, s1, r1]
        ici.start(None, [full_out], sems)
        ici.finish(None, [full_out], sems)
        d2d.start(None, [full_out], sems)
        d2d.finish(None, [full_out], sems)
        for j, chip in enumerate(chips):
            _rcopy(cw_in, cw_out.at[2 * chip[0] + chip[1]], cs.at[j], cr.at[j], (*chip, c)).wait_recv()
        for cp in conv:
            cp.wait_send()

    return pl.pallas_call(
        body, name="gather_first",
        in_specs=[ANY, ANY], out_specs=[ANY, ANY],
        out_shape=_hbm_out([full, jax.ShapeDtypeStruct((NCHIP,) + conv_w.shape, conv_w.dtype)]),
        input_output_aliases={0: 0},
        scratch_shapes=_sem2(1, 3) + _sem2(1, 3) + [pltpu.SemaphoreType.DMA((3,)), pltpu.SemaphoreType.DMA((3,))],
        compiler_params=pltpu.CompilerParams(has_side_effects=True),
    )(*_hbm(full, conv_w))


def _comm_only(name, stage):
    def body(*refs):
        ni, no = len(stage.operands), len(stage.out_shape)
        stage.start(refs[:ni], refs[ni:ni + no], refs[ni + no:])
        stage.finish(refs[:ni], refs[ni:ni + no], refs[ni + no:])

    return pl.pallas_call(
        body, name=name, in_specs=[ANY] * len(stage.operands), out_specs=[ANY] * len(stage.out_shape),
        out_shape=_hbm_out(stage.out_shape), input_output_aliases=stage.alias, scratch_shapes=stage.sems,
        compiler_params=pltpu.CompilerParams(has_side_effects=True),
    )(*_hbm(*stage.operands))


def _to_sibling(srcs):
    n = len(srcs)

    def copies(ins, outs, sems):
        x, y, c, _ = _place()
        sib = (x, y, 1 - c)
        return [_rcopy(ins[i].at[:, 1 - c] if srcs[i].ndim == 4 else ins[i], outs[i], sems[0].at[i], sems[1].at[i], sib)
                for i in range(n)]

    def start(ins, outs, sems):
        for cp in copies(ins, outs, sems):
            cp.start()

    def finish(ins, outs, sems):
        for cp in copies(ins, outs, sems):
            cp.wait()

    shapes = [jax.ShapeDtypeStruct((NCHIP,) + s.shape[2:] if s.ndim == 4 else s.shape, s.dtype) for s in srcs]
    return _Stage(srcs, shapes, {}, [pltpu.SemaphoreType.DMA((n,)), pltpu.SemaphoreType.DMA((n,))], start, finish)


def _to_chips(srcs, parts=None, lands=None):
    n = len(srcs)
    parts = [ALL] * n if parts is None else parts
    lands = [None] * n if lands is None else lands
    given = [i for i in range(n) if lands[i] is not None]

    def rows(ref, i):
        hr = srcs[i].shape[1]
        r0, r1 = hr * parts[i][0] // parts[i][2], hr * parts[i][1] // parts[i][2]
        return ref.at[pl.ds(r0, r1 - r0), :]

    def copies(ins, outs, sems):
        x, y, c, chips = _place()
        me = 2 * x + y
        return [_rcopy(rows(ins[i].at[2 * chip[0] + chip[1]] if srcs[i].shape[0] == NCHIP else ins[i].at[c], i),
                       rows(outs[i].at[me], i), sems[0].at[i, j], sems[1].at[i, j], (*chip, c))
                for i in range(n) for j, chip in enumerate(chips)]

    def start(ins, outs, sems):
        for cp in copies(ins, outs, sems):
            cp.start()

    def finish(ins, outs, sems):
        for cp in copies(ins, outs, sems):
            cp.wait()

    shapes = [jax.ShapeDtypeStruct((NCHIP,) + s.shape[1:], s.dtype) for s in srcs]
    alias = {n + q: i for q, i in enumerate(given)}
    return _Stage(list(srcs) + [lands[i] for i in given], shapes, alias, _sem2(n, 3), start, finish)


def _share(pairs):
    n = len(pairs)

    def start(ins, outs, sems):
        x, y, c, _ = _place()
        for i in range(n):
            _rcopy(outs[i].at[c], outs[i].at[c], sems[0].at[i], sems[1].at[i], (x, y, 1 - c)).start()

    def finish(ins, outs, sems):
        x, y, c, _ = _place()
        for i in range(n):
            _rcopy(outs[i].at[c], outs[i].at[c], sems[0].at[i], sems[1].at[i], (x, y, 1 - c)).wait_send()
            _rcopy(outs[i].at[1 - c], outs[i].at[1 - c], sems[0].at[i], sems[1].at[i], (x, y, 1 - c)).wait_recv()

    return _Stage(pairs, [_sds(p) for p in pairs], {i: i for i in range(n)},
                  [pltpu.SemaphoreType.DMA((n,)), pltpu.SemaphoreType.DMA((n,))], start, finish)


def _row_block(rows, cols, itemsize=4, target=MIB):
    br = rows
    while br * cols * itemsize > target and br % 16 == 0:
        br //= 2
    return br


def _cast_place(w, chip_idx, name):
    rows, cols = w.shape
    br = _row_block(rows, cols)

    def body(k_ref, w_ref, o_ref):
        o_ref[0] = w_ref[...].astype(BF)

    return _call(
        body, name=name, grid=(rows // br,), prefetch=chip_idx,
        in_specs=[pl.BlockSpec((br, cols), lambda r, k: (r, 0))],
        out_specs=[pl.BlockSpec((1, br, cols), lambda r, k: (k[0], r, 0))],
        out_shape=[jax.ShapeDtypeStruct((NCHIP, rows, cols), BF)], vmem=32, args=[w])[0][0]


def _add_sibling(g, land, cidx, name, stages=()):
    _, _, hr, cols = g.shape
    br = _row_block(hr, cols)

    def body(c_ref, g_ref, l_ref, o_ref):
        o_ref[...] = (g_ref[0, 0] + l_ref[0]).astype(BF)[None]

    outs, st = _call(
        body, name=name, grid=(NCHIP, hr // br), prefetch=cidx,
        in_specs=[pl.BlockSpec((1, 1, br, cols), lambda k, r, c: (k, c[0], r, 0)),
                  pl.BlockSpec((1, br, cols), lambda k, r, c: (k, r, 0))],
        out_specs=[pl.BlockSpec((1, br, cols), lambda k, r, c: (k, r, 0))],
        out_shape=[jax.ShapeDtypeStruct((NCHIP, hr, cols), BF)], vmem=32, args=[g, land], stages=stages)
    return outs[0], st


def _add_pair(a, b, name):
    rows, cols = a.shape

    def body(a_ref, b_ref, o_ref):
        o_ref[...] = a_ref[...] + b_ref[...]

    spec = pl.BlockSpec((rows, cols), lambda r: (0, 0))
    return _call(body, name=name, grid=(1,), in_specs=[spec, spec], out_specs=[spec], out_shape=[_sds(a)],
                 vmem=32, args=[a, b])[0][0]


def _add_chips(own, land, idx, name):
    _, hr, cols = land.shape
    br = _row_block(hr, cols)

    def body(s_ref, a_ref, b_ref, c_ref, d_ref, o_ref):
        o_ref[...] = (a_ref[...].astype(F32) + b_ref[...].astype(F32)) + (c_ref[...].astype(F32) +
                                                                           d_ref[...].astype(F32))

    spec = lambda q: pl.BlockSpec((1, br, cols), functools.partial(lambda r, s, q: (s[q], r, 0), q=q))
    return _call(
        body, name=name, grid=(hr // br,), prefetch=idx,
        in_specs=[spec(0), spec(1), spec(2), spec(3)], out_specs=[spec(4)],
        out_shape=[jax.ShapeDtypeStruct((2, hr, cols), F32)], vmem=32, args=[own, land, land, land])[0][0]


def _adamw_math(w, g, m, v):
    mn = ADAM_B1 * m + (1.0 - ADAM_B1) * g
    vn = ADAM_B2 * v + (1.0 - ADAM_B2) * (g * g)
    m_hat = mn / (1.0 - ADAM_B1 ** ADAM_STEP)
    v_hat = vn / (1.0 - ADAM_B2 ** ADAM_STEP)
    return -ADAM_LR * (m_hat / (jnp.sqrt(v_hat) + ADAM_EPS) + ADAM_WD * w), mn, vn


def _adamw(w, g, m, v, name, stages=()):
    rows, cols = w.shape
    br = _row_block(rows, cols)

    def body(w_ref, g_ref, m_ref, v_ref, d_ref, mo_ref, vo_ref):
        d_ref[...], mo_ref[...], vo_ref[...] = _adamw_math(w_ref[...], g_ref[...], m_ref[...], v_ref[...])

    spec = pl.BlockSpec((br, cols), lambda r: (r, 0))
    return _call(body, name=name, grid=(rows // br,), in_specs=[spec] * 4, out_specs=[spec] * 3,
                 out_shape=[_sds(w)] * 3, vmem=32, args=[w, g, m, v], stages=stages)


SMALL_AT = {"norm_mix_pre": (0, 1, D), "norm_mix_post": (1, 1, D), "norm_mlp_pre": (2, 1, D),
            "norm_mlp_post": (3, 1, D), "b_gate": (4, 2, D), "conv_b": (6, 1, D), "lru_b_a": (7, 1, D),
            "lru_b_x": (8, 1, D), "lru_lambda": (9, 1, D), "pool_scale": (10, 1, DP)}
SMALL_SEPARATE = ["conv_w", "lru_w_a", "lru_w_x", "pool_w"]


def _adamw_small(small_sum, sep_grads, w, m, v):
    packed, sep = list(SMALL_AT), list(SMALL_SEPARATE)
    names = packed + sep

    def body(*refs):
        s_ref, refs = refs[0], refs[1:]
        g_sep, refs = refs[:len(sep)], refs[len(sep):]
        nn = len(names)
        w_r, m_r, v_r, refs = refs[:nn], refs[nn:2 * nn], refs[2 * nn:3 * nn], refs[3 * nn:]
        g_out, refs = refs[:len(packed)], refs[len(packed):]
        d_o, m_o, v_o = refs[:nn], refs[nn:2 * nn], refs[2 * nn:3 * nn]
        for i, n in enumerate(names):
            if n in SMALL_AT:
                r0, nr, nc = SMALL_AT[n]
                g = jnp.concatenate([s_ref[r0 + q:r0 + q + 1, :nc] for q in range(nr)], axis=1)
                g_out[i][...] = g
            else:
                g = g_sep[i - len(packed)][...]
            d_o[i][...], m_o[i][...], v_o[i][...] = _adamw_math(w_r[i][...], g, m_r[i][...], v_r[i][...])

    ws = [w[n] for n in names]
    res = pl.pallas_call(
        body, name="adamw_small",
        out_shape=[_sds(w[n]) for n in packed] + [_sds(a) for a in ws] * 3,
        compiler_params=_cp(32),
    )(*_hbm(small_sum, *sep_grads, *ws, *[m[n] for n in names], *[v[n] for n in names]))
    nn, npk = len(names), len(packed)
    grad = dict(zip(packed, res[:npk]))
    delta = dict(zip(names, res[npk:npk + nn]))
    new_m = dict(zip(names, res[npk + nn:npk + 2 * nn]))
    new_v = dict(zip(names, res[npk + 2 * nn:]))
    return grad, delta, new_m, new_v


W_NAMES = ["norm_mix_pre", "norm_mix_post", "norm_mlp_pre", "norm_mlp_post", "w_in", "b_gate", "conv_w", "conv_b",
           "lru_w_a", "lru_b_a", "lru_w_x", "lru_b_x", "lru_lambda", "pool_w", "pool_scale", "w_lru_up",
           "w_pool_up", "w_o", "w_ff1", "w_ff2"]
BIG = ["w_in", "w_lru_up", "w_pool_up", "w_o", "w_ff1", "w_ff2"]


def _block_diag(w):
    hd = w.shape[-1]
    per = CB // hd
    w4 = w.reshape(NG, per, hd, hd)
    eye = jnp.eye(per, dtype=w.dtype)
    return jnp.einsum("gpij,pq->gpiqj", w4, eye).reshape(NG, CB, CB)


def _block_diag_extract(d, hd):
    per = CB // hd
    d5 = d.reshape(NG, per, hd, per, hd)
    return jnp.stack([d5[:, p, :, p, :] for p in range(per)], axis=1).reshape(NG * per, hd, hd)


def _halves(g):
    return g.reshape(NCHIP, 2, g.size // (g.shape[-1] * 2 * NCHIP), g.shape[-1])


def kernel(x, norm_mix_pre, norm_mix_post, norm_mlp_pre, norm_mlp_post, w_in, b_gate, conv_w, conv_b, lru_w_a, lru_b_a, lru_w_x, lru_b_x, lru_lambda, pool_w, pool_scale, w_lru_up, w_pool_up, w_o, w_ff1, w_ff2, loss_target, m_norm_mix_pre, m_norm_mix_post, m_norm_mlp_pre, m_norm_mlp_post, m_w_in, m_b_gate, m_conv_w, m_conv_b, m_lru_w_a, m_lru_b_a, m_lru_w_x, m_lru_b_x, m_lru_lambda, m_pool_w, m_pool_scale, m_w_lru_up, m_w_pool_up, m_w_o, m_w_ff1, m_w_ff2, v_norm_mix_pre, v_norm_mix_post, v_norm_mlp_pre, v_norm_mlp_post, v_w_in, v_b_gate, v_conv_w, v_conv_b, v_lru_w_a, v_lru_b_a, v_lru_w_x, v_lru_b_x, v_lru_lambda, v_pool_w, v_pool_scale, v_w_lru_up, v_w_pool_up, v_w_o, v_w_ff1, v_w_ff2):
    args = dict(locals())
    two_d = lambda a: a.reshape(-1, a.shape[-1])
    w = {n: two_d(args[n]) for n in W_NAMES}
    mom = {n: two_d(args["m_" + n]) for n in W_NAMES}
    var = {n: two_d(args["v_" + n]) for n in W_NAMES}
    i32 = lambda val: jnp.asarray(val, jnp.int32)
    chip = i32(2 * lax.axis_index("x") + lax.axis_index("y"))
    core = i32(lax.axis_index("c"))
    cidx = core.reshape(1)
    zero = i32(0)
    hd = lru_w_a.shape[-1]
    xs, target = x[0], loss_target[0]
    g1, g2, g3, g4 = norm_mix_pre, norm_mix_post, norm_mlp_pre, norm_mlp_post

    full = {n: _cast_place(w[n], chip.reshape(1), "cast_" + n) for n in BIG}
    wa = _block_diag(lru_w_a[0]).astype(BF)
    wx = _block_diag(lru_w_x[0]).astype(BF)
    pw = pool_w[0].astype(BF)

    full["w_in"], conv_all = _gather_first(full["w_in"], w["conv_w"])
    conv_all = lax.dynamic_update_slice(conv_all, w["conv_w"][None], (chip, zero, zero))
    conv_full = jnp.transpose(conv_all, (1, 0, 2)).reshape(4, DR)
    mix = ["w_lru_up", "w_pool_up", "w_o"]
    ff1_a, ff1_b, ff2_a, ff2_b = (0, 3, 8), (3, 8, 8), (0, 1, 4), (1, 4, 4)
    (proj, h1), (got,) = _fwd_inproj(xs, g1, full["w_in"], stages=[_gather(
        [full[n] for n in mix] + [full["w_ff1"]], ici=[(0, ALL), (1, ALL), (2, ALL), (3, ff1_a)])])
    (ylru, hs), (got,) = _fwd_lru(proj, conv_full, conv_b, wa, lru_b_a, wx, lru_b_x, lru_lambda, stages=[_gather(
        got + [full["w_ff2"]], d2d=[(0, ALL), (1, ALL), (2, ALL), (3, ff1_a)], ici=[(3, ff1_b), (4, ff2_a)])])
    w_lru_up_f, w_pool_up_f, w_o_f = got[0].reshape(DR, D), got[1], got[2].reshape(D, D)
    ypool = _fwd_pool(proj, pw, pool_scale)
    (x2, h2, m, mrg, bra, brb), ((ff1, ff2),) = _fwd_merge(
        xs, ylru, ypool, proj, b_gate, g2, g3, w_lru_up_f, w_pool_up_f, w_o_f,
        stages=[_gather(got[3:], d2d=[(0, ff1_b), (1, ff2_a)], ici=[(1, ff2_b)])])
    ff2 = _comm_only("gather_last", _gather([ff2], d2d=[(0, ff2_b)]))[0].reshape(DF, D)
    a1, f = _fwd_mlp(h2, ff1, ff2)
    lossp, dy, df, dg4 = _loss_head(f, x2, target, g4)

    idx_big = jnp.stack([chip, (chip + 1) % NCHIP, (chip + 2) % NCHIP, (chip + 3) % NCHIP, core])
    dh2, df1 = _bwd_mlp_x(df, a1, ff1, ff2)
    dw_ff1, dw_ff2 = _bwd_mlp_w(df, h2, a1, df1)
    g_ff = [_halves(dw_ff1), _halves(dw_ff2)]
    (dxres, dgates, dylru, dypool, dm, dbra, dbrb, dg2, dg3, dbg), (l_ff,) = _bwd_merge(
        dh2, dy, x2, m, bra, brb, proj, b_gate, g2, g3, w_lru_up_f, w_pool_up_f, w_o_f, stages=[_to_sibling(g_ff)])
    dw_o, dw_lru_up, dw_pool_up = _dw_merge(mrg, dm, ylru, dbra, ypool, dbrb)
    p_ff = [_add_sibling(g, l, cidx, "add_sibling_" + n)[0] for g, l, n in zip(g_ff, l_ff, ["w_ff1", "w_ff2"])]
    g_mix = [_halves(dw_lru_up), _halves(dw_pool_up), _halves(dw_o)]
    ff2_head, ff2_tail = (0, 5, 8), (5, 8, 8)
    (dxp, dgl, dcw, dcb, dwa, dba, dwx, dbx, dlam), ((c_ff1, c_ff2), l_mix) = _bwd_lru(
        proj, hs, dylru, conv_full, conv_b, wa, lru_b_a, wx, lru_b_x, lru_lambda,
        stages=[_to_chips(p_ff, parts=[ALL, ff2_head]), _to_sibling(g_mix)])
    p_mix = [_add_sibling(g, l, cidx, "add_sibling_" + n)[0] for g, l, n in zip(g_mix, l_mix, mix)]
    dxpool, dpw, dsc = _bwd_pool(proj, dypool, pw, pool_scale)
    dproj = jnp.concatenate([dxp, dgl, dxpool, dgates], axis=1)
    (dw_in, dh1), (c_rest,) = _bwd_inproj(h1, dproj, full["w_in"], stages=[_to_chips(
        p_ff[1:] + p_mix, parts=[ff2_tail, ALL, ALL, ALL], lands=[c_ff2, None, None, None])])
    done = ["w_ff1", "w_ff2"] + mix
    pairs = [_add_chips(p, l, idx_big, "add_chips_" + n) for p, l, n in zip(p_ff + p_mix, [c_ff1] + c_rest, done)]
    g_in = _halves(dw_in)
    (grad_x, dg1), ((l_in,), pairs) = _bwd_prenorm(xs, dh1, dxres, g1, stages=[_to_sibling([g_in]), _share(pairs)])

    small = jnp.concatenate([
        dg1, dg2, dg3, dg4, dbg.reshape(2, D), dcb, dba, dbx, dlam,
        jnp.pad(dsc, ((0, 0), (0, D - DP))), jnp.pad(lossp, ((0, 0), (0, D - 1))), dcw,
        _block_diag_extract(dwa, hd).reshape(-1, D), _block_diag_extract(dwx, hd).reshape(-1, D),
        dpw.reshape(-1, D)], axis=0)
    p_in, ((l_small,),) = _add_sibling(g_in, l_in, cidx, "add_sibling_w_in", stages=[_to_sibling([small])])
    small2 = _add_pair(small, l_small, "add_sibling_small").reshape(2, SMALL_ROWS // 2, D)

    grads, delta, new_m, new_v = {}, {}, {}, {}
    for n, p in zip(done, pairs):
        grads[n] = p.reshape(-1, p.shape[-1])

    def update(n, stages=()):
        (delta[n], new_m[n], new_v[n]), landed = _adamw(w[n], grads[n], mom[n], var[n], "adamw_" + n, stages=stages)
        return landed

    ((c_in, c_small),) = update("w_pool_up", stages=[_to_chips([p_in, small2])])
    pair_in = _add_chips(p_in, c_in, idx_big, "add_chips_w_in")
    own_small = lax.dynamic_index_in_dim(small2, core, 0, keepdims=True)
    c_small = lax.dynamic_update_slice(c_small, own_small, (chip, zero, zero))
    pair_small = _add_chips(c_small, c_small, jnp.stack([zero, zero + 1, zero + 2, zero + 3, core]), "add_chips_small")
    ((pair_in, pair_small),) = update("w_lru_up", stages=[_share([pair_in, pair_small])])
    grads["w_in"] = pair_in.reshape(-1, pair_in.shape[-1])
    for n in ["w_in", "w_o", "w_ff1", "w_ff2"]:
        update(n)
    small_sum = pair_small.reshape(SMALL_ROWS, D)
    loss = 0.5 * small_sum[LOSS_ROW, 0]
    ccols = DR // NCHIP
    sep = [lax.dynamic_slice(small_sum[12:16], (zero, chip * ccols), (4, ccols)),
           small_sum[16:80].reshape(-1, hd), small_sum[80:144].reshape(-1, hd), small_sum[144:208].reshape(-1, PG)]
    g_s, d_s, m_s, v_s = _adamw_small(small_sum, sep, w, mom, var)
    grads.update(g_s)
    grads.update(dict(zip(SMALL_SEPARATE, sep)))
    delta.update(d_s)
    new_m.update(m_s)
    new_v.update(v_s)

    out = lambda d: [d[n].reshape(args[n].shape) for n in W_NAMES]
    return (loss, grad_x[None], *out(grads), *out(delta), *out(new_m), *out(new_v))
```

```python
import functools
import math

import jax
import jax.numpy as jnp
from jax import lax
from jax.experimental import pallas as pl
from jax.experimental.pallas import tpu as pltpu

F32 = jnp.float32
BF = jnp.bfloat16

T = 2048
D = 1024
DR = 1024
DP = 512
DF = 4096
DIN = 4608
NCHIP = 4
CW_IN = DIN // NCHIP
LANE = 128
CB = 128
NG = DR // CB
PG = 128
POOL_WINDOWS = (2, 4, 8, 16)
NORM_EPS = 1e-6
LRU_C = 8.0
GELU_C = math.sqrt(2.0 / math.pi)
ADAM_LR = 0.001
ADAM_B1 = 0.9
ADAM_B2 = 0.999
ADAM_EPS = 1e-08
ADAM_WD = 0.01
ADAM_STEP = 10
MESH_ID = pl.DeviceIdType.MESH
ANY = pl.BlockSpec(memory_space=pl.ANY)
SMALL_ROWS = 208
LOSS_ROW = 11
MIB = 1 << 20


def _cp(vmem_mib=None):
    if vmem_mib is None:
        return pltpu.CompilerParams()
    return pltpu.CompilerParams(vmem_limit_bytes=vmem_mib * MIB)


def _hbm(*arrays):
    return [pltpu.with_memory_space_constraint(a, pltpu.HBM) for a in arrays]


def _hbm_out(shapes):
    return [pltpu.HBM(s.shape, s.dtype) for s in shapes]


class _Stage:
    def __init__(self, operands, out_shape, alias, sems, start, finish):
        self.operands, self.out_shape, self.alias, self.sems = list(operands), list(out_shape), dict(alias), list(sems)
        self.start, self.finish = start, finish


def _call(body, *, name, grid, in_specs, out_specs, out_shape, args, vmem=None, stages=(), prefetch=None,
          scratch=()):
    nin, nout = len(in_specs), len(out_specs)
    npre = 0 if prefetch is None else 1
    st_args, st_shapes, st_sems, aliases = [], [], list(scratch), {}
    for st in stages:
        for k, v in st.alias.items():
            aliases[npre + nin + len(st_args) + k] = nout + len(st_shapes) + v
        st_args += st.operands
        st_shapes += st.out_shape
        st_sems += st.sems

    def wrapped(*refs):
        pre, refs = refs[:npre], refs[npre:]
        ins, pos = refs[:nin], nin
        st_ins = []
        for st in stages:
            st_ins.append(refs[pos:pos + len(st.operands)])
            pos += len(st.operands)
        outs, pos = refs[pos:pos + nout], pos + nout
        st_outs = []
        for st in stages:
            st_outs.append(refs[pos:pos + len(st.out_shape)])
            pos += len(st.out_shape)
        work, pos = refs[pos:pos + len(scratch)], pos + len(scratch)
        sems = []
        for st in stages:
            sems.append(refs[pos:pos + len(st.sems)])
            pos += len(st.sems)
        if stages:
            first = functools.reduce(jnp.logical_and, [pl.program_id(a) == 0 for a in range(len(grid))])

            @pl.when(first)
            def _():
                for st, a, b, s in zip(stages, st_ins, st_outs, sems):
                    st.start(a, b, s)

        body(*pre, *ins, *outs, *work)
        if stages:
            last = functools.reduce(jnp.logical_and, [pl.program_id(a) == g - 1 for a, g in enumerate(grid)])

            @pl.when(last)
            def _():
                for st, a, b, s in zip(stages, st_ins, st_outs, sems):
                    st.finish(a, b, s)

    all_in = list(in_specs) + [ANY] * len(st_args)
    all_out = list(out_specs) + [ANY] * len(st_shapes)
    kw = dict(has_side_effects=True) if stages else {}
    if vmem is not None:
        kw["vmem_limit_bytes"] = vmem * MIB
    if prefetch is None:
        gkw = dict(grid=grid, in_specs=all_in, out_specs=all_out, scratch_shapes=st_sems)
    else:
        gkw = dict(grid_spec=pltpu.PrefetchScalarGridSpec(
            num_scalar_prefetch=1, grid=grid, in_specs=all_in, out_specs=all_out, scratch_shapes=st_sems))
    res = pl.pallas_call(
        wrapped, name=name, out_shape=_hbm_out(list(out_shape) + st_shapes), input_output_aliases=aliases,
        compiler_params=pltpu.CompilerParams(**kw), **gkw,
    )(*([prefetch] if npre else []), *_hbm(*args, *st_args))
    outs, rest, st_res = list(res[:nout]), list(res[nout:]), []
    for st in stages:
        st_res.append(rest[:len(st.out_shape)])
        rest = rest[len(st.out_shape):]
    return outs, st_res


def _mm(a, b):
    return jnp.dot(a.astype(BF), b.astype(BF), preferred_element_type=F32)


def _mm_nt(a, b):
    return lax.dot_general(a.astype(BF), b.astype(BF), (((1,), (1,)), ((), ())),
                           preferred_element_type=F32)


def _mm_tn(a, b):
    return lax.dot_general(a.astype(BF), b.astype(BF), (((0,), (0,)), ((), ())),
                           preferred_element_type=F32)


def _rows(v):
    return lax.broadcasted_iota(jnp.int32, v.shape, 0)


def _sd(v, s, fill=0.0):
    return jnp.where(_rows(v) >= s, pltpu.roll(v, s, axis=0), fill)


def _su(v, s, fill=0.0):
    n = v.shape[0]
    return jnp.where(_rows(v) < n - s, pltpu.roll(v, n - s, axis=0), fill)


def _sigmoid(z):
    return 1.0 / (1.0 + jnp.exp(-z))


def _softplus(z):
    e = jnp.exp(-jnp.abs(z))
    u = 1.0 + e
    d = u - 1.0
    log1p = jnp.where(d == 0.0, e, jnp.log(u) * (e / jnp.where(d == 0.0, 1.0, d)))
    return jnp.maximum(z, 0.0) + log1p


def _mean(v):
    return jnp.mean(v, axis=-1, keepdims=True)


def _colsum(v):
    return jnp.sum(v, axis=0, keepdims=True)


def _acc(ref, val, first):
    @pl.when(first)
    def _():
        ref[...] = val

    @pl.when(jnp.logical_not(first))
    def _():
        ref[...] += val


def _conv(xp, cw, cb):
    x1, x2, x3 = _sd(xp, 1), _sd(xp, 2), _sd(xp, 3)
    xc = cb + cw[0:1] * x3 + cw[1:2] * x2 + cw[2:3] * x1 + cw[3:4] * xp
    return xc, x1, x2, x3


def _lru_gates(xc, wa, ba, wx, bx, lam):
    xcb = xc.astype(BF)
    r = _sigmoid(_mm(xcb, wa) + ba)
    ii = _sigmoid(_mm(xcb, wx) + bx)
    sp = _softplus(-lam)
    la = (-LRU_C) * r * sp
    a = jnp.exp(la)
    mult = jnp.sqrt(-jnp.tanh(la) * (a * a + 1.0))
    return xcb, r, ii, sp, a, mult


def _gelu_parts(g):
    th = jnp.tanh(GELU_C * (g + 0.044715 * (g * g * g)))
    gel = 0.5 * g * (1.0 + th)
    dgel = 0.5 * (1.0 + th) + 0.5 * g * (1.0 - th * th) * (GELU_C * (1.0 + 3.0 * 0.044715 * (g * g)))
    return gel, dgel


def _pool_window(x, steps, shift):
    s, sh = x, 1
    for _ in range(steps):
        s = s + shift(s, sh)
        sh *= 2
    return s


def _fwd_inproj(x, g1, w_in, stages=()):
    tm = 512

    def body(x_ref, g_ref, w_ref, proj_ref, h_ref):
        @pl.when(pl.program_id(1) == 0)
        def _():
            xv = x_ref[...]
            r = lax.rsqrt(_mean(xv * xv) + NORM_EPS)
            h_ref[...] = ((xv * r) * g_ref[...]).astype(BF)

        proj_ref[...] = jnp.dot(h_ref[...], w_ref[0], preferred_element_type=F32)

    return _call(
        body, name="fwd_inproj", grid=(T // tm, NCHIP),
        in_specs=[pl.BlockSpec((tm, D), lambda i, k: (i, 0)),
                  pl.BlockSpec((1, D), lambda i, k: (0, 0)),
                  pl.BlockSpec((1, D, CW_IN), lambda i, k: (k, 0, 0))],
        out_specs=[pl.BlockSpec((tm, CW_IN), lambda i, k: (i, k)),
                   pl.BlockSpec((tm, D), lambda i, k: (i, 0))],
        out_shape=[jax.ShapeDtypeStruct((T, DIN), F32), jax.ShapeDtypeStruct((T, D), BF)],
        vmem=40, args=[x, g1, w_in], stages=stages)


def _vec_spec():
    return pl.BlockSpec((1, CB), lambda j: (0, j))


def _fwd_lru(proj, conv_w, conv_b, wa, ba, wx, bx, lam, stages=()):
    def body(xp_ref, g_ref, cw_ref, cb_ref, wa_ref, ba_ref, wx_ref, bx_ref, lam_ref, y_ref, h_ref):
        xc, _, _, _ = _conv(xp_ref[...], cw_ref[...], cb_ref[...])
        _, _, ii, _, a, mult = _lru_gates(xc, wa_ref[0], ba_ref[...], wx_ref[0], bx_ref[...], lam_ref[...])
        b = mult * (ii * xc)
        s = 1
        while s < T:
            b = b + a * _sd(b, s, 0.0)
            if 2 * s < T:
                a = a * _sd(a, s, 1.0)
            s *= 2
        h_ref[...] = b
        gel, _ = _gelu_parts(g_ref[...])
        y_ref[...] = (b * gel).astype(BF)

    return _call(
        body, name="fwd_lru", grid=(NG,),
        in_specs=[pl.BlockSpec((T, CB), lambda j: (0, j)),
                  pl.BlockSpec((T, CB), lambda j: (0, NG + j)),
                  pl.BlockSpec((4, CB), lambda j: (0, j)),
                  _vec_spec(),
                  pl.BlockSpec((1, CB, CB), lambda j: (j, 0, 0)), _vec_spec(),
                  pl.BlockSpec((1, CB, CB), lambda j: (j, 0, 0)), _vec_spec(),
                  _vec_spec()],
        out_specs=[pl.BlockSpec((T, CB), lambda j: (0, j)), pl.BlockSpec((T, CB), lambda j: (0, j))],
        out_shape=[jax.ShapeDtypeStruct((T, DR), BF), jax.ShapeDtypeStruct((T, DR), F32)],
        vmem=48, args=[proj, proj, conv_w, conv_b, wa, ba, wx, bx, lam], stages=stages)


def _pool_cnt(w):
    t = lax.broadcasted_iota(jnp.int32, (T, 1), 0)
    return jnp.minimum(t + 1, w).astype(F32)


def _fwd_pool(proj, pool_w, pool_scale):
    def body(xp_ref, pw_ref, sc_ref, y_ref):
        for g, w in enumerate(POOL_WINDOWS):
            cols = slice(g * PG, (g + 1) * PG)
            x = xp_ref[:, cols]
            p = _pool_window(x, g + 1, _sd) / _pool_cnt(w) - x
            y_ref[:, cols] = (_mm(p, pw_ref[g]) * sc_ref[:, cols]).astype(BF)

    return pl.pallas_call(
        body, name="fwd_pool", grid=(1,),
        in_specs=[pl.BlockSpec((T, DP), lambda i: (0, 2 * DR // DP)),
                  pl.BlockSpec((4, PG, PG), lambda i: (0, 0, 0)),
                  pl.BlockSpec((1, DP), lambda i: (0, 0))],
        out_specs=pl.BlockSpec((T, DP), lambda i: (0, 0)),
        out_shape=pltpu.HBM((T, DP), BF),
        compiler_params=_cp(48),
    )(*_hbm(proj, pool_w, pool_scale))


GATE_BLK = 512
GATE_BLK0 = (2 * DR + DP) // GATE_BLK


def _gate_specs(tm):
    return [pl.BlockSpec((tm, GATE_BLK), functools.partial(lambda i, q: (i, GATE_BLK0 + q), q=q))
            for q in range(4)]


def _fwd_merge(x, ylru, ypool, proj, b_gate, g2, g3, w_lru_up, w_pool_up, w_o, stages=()):
    tm = 512

    def body(x_ref, yl_ref, yp_ref, p0, p1, p2, p3, bg_ref, g2_ref, g3_ref, wl_ref, wp_ref, wo_ref,
             x2_ref, h2_ref, m_ref, mrg_ref, bra_ref, brb_ref):
        bra = jnp.dot(yl_ref[...], wl_ref[...], preferred_element_type=F32)
        yp = yp_ref[...]
        brb = jnp.concatenate([jnp.dot(yp, wp_ref[k], preferred_element_type=F32) for k in range(NCHIP)], axis=1)
        bg = bg_ref[...]
        ga = _sigmoid(jnp.concatenate([p0[...], p1[...]], axis=1) + bg[:, :D])
        gb = _sigmoid(jnp.concatenate([p2[...], p3[...]], axis=1) + bg[:, D:])
        mrg = (ga * bra + gb * brb).astype(BF)
        m = jnp.dot(mrg, wo_ref[...], preferred_element_type=F32)
        r2 = lax.rsqrt(_mean(m * m) + NORM_EPS)
        x2 = x_ref[...] + (m * r2) * g2_ref[...]
        r3 = lax.rsqrt(_mean(x2 * x2) + NORM_EPS)
        x2_ref[...] = x2
        h2_ref[...] = ((x2 * r3) * g3_ref[...]).astype(BF)
        m_ref[...] = m
        mrg_ref[...] = mrg
        bra_ref[...] = bra.astype(BF)
        brb_ref[...] = brb.astype(BF)

    row = lambda w: pl.BlockSpec((tm, w), lambda i: (i, 0))
    full2 = lambda a, b: pl.BlockSpec((a, b), lambda i: (0, 0))
    return _call(
        body, name="fwd_merge", grid=(T // tm,),
        in_specs=[row(D), row(DR), row(DP)] + _gate_specs(tm) +
                 [full2(1, 2 * D), full2(1, D), full2(1, D), full2(DR, D),
                  pl.BlockSpec((NCHIP, DP, D // NCHIP), lambda i: (0, 0, 0)), full2(D, D)],
        out_specs=[row(D)] * 6,
        out_shape=[jax.ShapeDtypeStruct((T, D), F32), jax.ShapeDtypeStruct((T, D), BF),
                   jax.ShapeDtypeStruct((T, D), F32), jax.ShapeDtypeStruct((T, D), BF),
                   jax.ShapeDtypeStruct((T, D), BF), jax.ShapeDtypeStruct((T, D), BF)],
        vmem=48, args=[x, ylru, ypool, proj, proj, proj, proj, b_gate, g2, g3, w_lru_up, w_pool_up, w_o],
        stages=stages)


def _fwd_mlp(h2, w_ff1, w_ff2):
    tm = 512
    fk = DF // NCHIP

    def body(h_ref, w1_ref, w2_ref, a1_ref, f_ref):
        h = h_ref[...]
        f = None
        for k in range(NCHIP):
            a1 = jnp.maximum(jnp.dot(h, w1_ref[k], preferred_element_type=F32), 0.0)
            a1_ref[:, k * fk:(k + 1) * fk] = a1.astype(BF)
            part = jnp.dot((a1 * a1).astype(BF), w2_ref[k * fk:(k + 1) * fk, :], preferred_element_type=F32)
            f = part if f is None else f + part
        f_ref[...] = f

    return pl.pallas_call(
        body, name="fwd_mlp", grid=(T // tm,),
        in_specs=[pl.BlockSpec((tm, D), lambda i: (i, 0)),
                  pl.BlockSpec((NCHIP, D, fk), lambda i: (0, 0, 0)),
                  pl.BlockSpec((DF, D), lambda i: (0, 0))],
        out_specs=[pl.BlockSpec((tm, DF), lambda i: (i, 0)), pl.BlockSpec((tm, D), lambda i: (i, 0))],
        out_shape=_hbm_out([jax.ShapeDtypeStruct((T, DF), BF), jax.ShapeDtypeStruct((T, D), F32)]),
        compiler_params=_cp(56),
    )(*_hbm(h2, w_ff1, w_ff2))


def _loss_head(f, x2, target, g4):
    tm = 512

    def body(f_ref, x2_ref, t_ref, g_ref, loss_ref, dy_ref, df_ref, dg_ref):
        first = pl.program_id(0) == 0
        f = f_ref[...]
        g4v = g_ref[...]
        r4 = lax.rsqrt(_mean(f * f) + NORM_EPS)
        fn = f * r4
        e = (x2_ref[...] + fn * g4v) - t_ref[...]
        _acc(loss_ref, jnp.sum(_mean(e * e), axis=0, keepdims=True), first)
        dy = e * (1.0 / D)
        dy_ref[...] = dy
        _acc(dg_ref, _colsum(dy * fn), first)
        dfn = dy * g4v
        df_ref[...] = (r4 * (dfn - fn * _mean(dfn * fn))).astype(BF)

    row = pl.BlockSpec((tm, D), lambda i: (i, 0))
    return pl.pallas_call(
        body, name="loss_head", grid=(T // tm,),
        in_specs=[row, row, row, pl.BlockSpec((1, D), lambda i: (0, 0))],
        out_specs=[pl.BlockSpec((1, 1), lambda i: (0, 0)), row, row, pl.BlockSpec((1, D), lambda i: (0, 0))],
        out_shape=_hbm_out([jax.ShapeDtypeStruct((1, 1), F32), jax.ShapeDtypeStruct((T, D), F32),
                            jax.ShapeDtypeStruct((T, D), BF), jax.ShapeDtypeStruct((1, D), F32)]),
        compiler_params=_cp(48),
    )(*_hbm(f, x2, target, g4))


def _bwd_mlp_x(df, a1, w_ff1, w_ff2):
    tm = 512
    fk = DF // NCHIP

    def body(df_ref, a1_ref, w1_ref, w2_ref, dh_ref, df1_ref):
        df = df_ref[...]
        dh = None
        for k in range(NCHIP):
            cols = slice(k * fk, (k + 1) * fk)
            dact = _mm_nt(df, w2_ref[cols, :])
            df1 = (dact * (2.0 * a1_ref[:, cols].astype(F32))).astype(BF)
            df1_ref[:, cols] = df1
            part = _mm_nt(df1, w1_ref[k])
            dh = part if dh is None else dh + part
        dh_ref[...] = dh

    return pl.pallas_call(
        body, name="bwd_mlp_x", grid=(T // tm,),
        in_specs=[pl.BlockSpec((tm, D), lambda i: (i, 0)),
                  pl.BlockSpec((tm, DF), lambda i: (i, 0)),
                  pl.BlockSpec((NCHIP, D, fk), lambda i: (0, 0, 0)),
                  pl.BlockSpec((DF, D), lambda i: (0, 0))],
        out_specs=[pl.BlockSpec((tm, D), lambda i: (i, 0)), pl.BlockSpec((tm, DF), lambda i: (i, 0))],
        out_shape=_hbm_out([jax.ShapeDtypeStruct((T, D), F32), jax.ShapeDtypeStruct((T, DF), BF)]),
        compiler_params=_cp(56),
    )(*_hbm(df, a1, w_ff1, w_ff2))


def _bwd_mlp_w(df, h2, a1, df1):
    fc = 512
    per = (DF // NCHIP) // fc

    def body(df_ref, h_ref, a1_ref, df1_ref, dw1_ref, dw2_ref):
        a1 = a1_ref[...].astype(F32)
        dw2_ref[...] = _mm_tn((a1 * a1).astype(BF), df_ref[...])
        dw1_ref[0] = _mm_tn(h_ref[...], df1_ref[...])

    return pl.pallas_call(
        body, name="bwd_mlp_w", grid=(DF // fc,),
        in_specs=[pl.BlockSpec((T, D), lambda j: (0, 0)),
                  pl.BlockSpec((T, D), lambda j: (0, 0)),
                  pl.BlockSpec((T, fc), lambda j: (0, j)),
                  pl.BlockSpec((T, fc), lambda j: (0, j))],
        out_specs=[pl.BlockSpec((1, D, fc), lambda j: (j // per, 0, j % per)),
                   pl.BlockSpec((fc, D), lambda j: (j, 0))],
        out_shape=_hbm_out([jax.ShapeDtypeStruct((NCHIP, D, DF // NCHIP), F32),
                            jax.ShapeDtypeStruct((DF, D), F32)]),
        compiler_params=_cp(56),
    )(*_hbm(df, h2, a1, df1))


def _bwd_merge(dh2, dy, x2, m, bra, brb, proj, b_gate, g2, g3, w_lru_up, w_pool_up, w_o, stages=()):
    tm = 256
    cpu = D // NCHIP

    def body(dh2_ref, dy_ref, x2_ref, m_ref, bra_ref, brb_ref, p0, p1, p2, p3, bg_ref,
             g2_ref, g3_ref, wl_ref, wp_ref, wo_ref,
             dx_ref, dgt_ref, dyl_ref, dyp_ref, dm_ref, dbra_ref, dbrb_ref, dg2_ref, dg3_ref, dbg_ref):
        first = pl.program_id(0) == 0
        x2 = x2_ref[...]
        r3 = lax.rsqrt(_mean(x2 * x2) + NORM_EPS)
        x2n = x2 * r3
        dh2 = dh2_ref[...]
        t3 = dh2 * g3_ref[...]
        dx2 = dy_ref[...] + r3 * (t3 - x2n * _mean(t3 * x2n))
        dx_ref[...] = dx2
        _acc(dg3_ref, _colsum(dh2 * x2n), first)
        m = m_ref[...]
        r2 = lax.rsqrt(_mean(m * m) + NORM_EPS)
        mn = m * r2
        _acc(dg2_ref, _colsum(dx2 * mn), first)
        dmn = dx2 * g2_ref[...]
        dm = (r2 * (dmn - mn * _mean(dmn * mn))).astype(BF)
        dm_ref[...] = dm
        dmrg = _mm_nt(dm, wo_ref[...])
        bg = bg_ref[...]
        ga = _sigmoid(jnp.concatenate([p0[...], p1[...]], axis=1) + bg[:, :D])
        gb = _sigmoid(jnp.concatenate([p2[...], p3[...]], axis=1) + bg[:, D:])
        dga = dmrg * bra_ref[...].astype(F32) * (ga * (1.0 - ga))
        dgb = dmrg * brb_ref[...].astype(F32) * (gb * (1.0 - gb))
        dgt_ref[:, :D] = dga.astype(BF)
        dgt_ref[:, D:] = dgb.astype(BF)
        _acc(dbg_ref, jnp.concatenate([_colsum(dga), _colsum(dgb)], axis=1), first)
        dbra = (dmrg * ga).astype(BF)
        dbrb = (dmrg * gb).astype(BF)
        dbra_ref[...] = dbra
        dbrb_ref[...] = dbrb
        dyl_ref[...] = _mm_nt(dbra, wl_ref[...])
        dyp = None
        for k in range(NCHIP):
            part = _mm_nt(dbrb[:, k * cpu:(k + 1) * cpu], wp_ref[k])
            dyp = part if dyp is None else dyp + part
        dyp_ref[...] = dyp

    row = lambda w: pl.BlockSpec((tm, w), lambda i: (i, 0))
    full2 = lambda a, b: pl.BlockSpec((a, b), lambda i: (0, 0))
    wp_spec = pl.BlockSpec((NCHIP, DP, cpu), lambda i: (0, 0, 0))
    return _call(
        body, name="bwd_merge", grid=(T // tm,),
        in_specs=[row(D)] * 6 + _gate_specs(tm) +
                 [full2(1, 2 * D), full2(1, D), full2(1, D), full2(DR, D), wp_spec, full2(D, D)],
        out_specs=[row(D), row(2 * D), row(DR), row(DP), row(D), row(D), row(D),
                   full2(1, D), full2(1, D), full2(1, 2 * D)],
        out_shape=[jax.ShapeDtypeStruct((T, D), F32), jax.ShapeDtypeStruct((T, 2 * D), BF),
                   jax.ShapeDtypeStruct((T, DR), F32), jax.ShapeDtypeStruct((T, DP), F32),
                   jax.ShapeDtypeStruct((T, D), BF), jax.ShapeDtypeStruct((T, D), BF),
                   jax.ShapeDtypeStruct((T, D), BF),
                   jax.ShapeDtypeStruct((1, D), F32), jax.ShapeDtypeStruct((1, D), F32),
                   jax.ShapeDtypeStruct((1, 2 * D), F32)],
        vmem=56, args=[dh2, dy, x2, m, bra, brb, proj, proj, proj, proj, b_gate, g2, g3, w_lru_up, w_pool_up, w_o],
        stages=stages)


def _dw_merge(mrg, dm, ylru, dbra, ypool, dbrb):
    nb = NCHIP
    rb, pb, cpu = D // nb, DP // nb, D // NCHIP

    def body(mrg_ref, dm_ref, yl_ref, dbra_ref, yp_ref, dbrb_ref, dwo_ref, dwl_ref, dwp_ref):
        dwo_ref[...] = _mm_tn(mrg_ref[...], dm_ref[...])
        dwl_ref[...] = _mm_tn(yl_ref[...], dbra_ref[...])
        dwp = _mm_tn(yp_ref[...], dbrb_ref[...])
        for k in range(NCHIP):
            dwp_ref[k] = dwp[:, k * cpu:(k + 1) * cpu]

    cols = lambda w: pl.BlockSpec((T, w), lambda r: (0, r))
    whole = pl.BlockSpec((T, D), lambda r: (0, 0))
    return pl.pallas_call(
        body, name="dw_merge", grid=(nb,),
        in_specs=[cols(rb), whole, cols(rb), whole, cols(pb), whole],
        out_specs=[pl.BlockSpec((rb, D), lambda r: (r, 0)), pl.BlockSpec((rb, D), lambda r: (r, 0)),
                   pl.BlockSpec((NCHIP, pb, cpu), lambda r: (0, r, 0))],
        out_shape=_hbm_out([jax.ShapeDtypeStruct((D, D), F32), jax.ShapeDtypeStruct((DR, D), F32),
                            jax.ShapeDtypeStruct((NCHIP, DP, cpu), F32)]),
        compiler_params=_cp(56),
    )(*_hbm(mrg, dm, ylru, dbra, ypool, dbrb))


def _bwd_lru(proj, h, dylru, conv_w, conv_b, wa, ba, wx, bx, lam, stages=()):
    def body(xp_ref, g_ref, h_ref, dy_ref, cw_ref, cb_ref, wa_ref, ba_ref, wx_ref, bx_ref, lam_ref,
             dxp_ref, dg_ref, dcw_ref, dcb_ref, dwa_ref, dba_ref, dwx_ref, dbx_ref, dlam_ref):
        xp = xp_ref[...]
        cw = cw_ref[...]
        lam = lam_ref[...]
        xc, x1, x2, x3 = _conv(xp, cw, cb_ref[...])
        wa, wx = wa_ref[0], wx_ref[0]
        xcb, r, ii, sp, a, mult = _lru_gates(xc, wa, ba_ref[...], wx, bx_ref[...], lam)
        g = g_ref[...]
        gel, dgel = _gelu_parts(g)
        h = h_ref[...]
        dy = dy_ref[...]
        dg_ref[...] = (dy * h * dgel).astype(BF)
        b = dy * gel
        aa = _su(a, 1, 0.0)
        s = 1
        while s < T:
            b = b + aa * _su(b, s, 0.0)
            if 2 * s < T:
                aa = aa * _su(aa, s, 0.0)
            s *= 2
        da = b * _sd(h, 1, 0.0)
        dmult = b * (ii * xc)
        dii = b * (mult * xc)
        dxc = b * (mult * ii)
        dla = da * a - dmult * ((a * a) / mult)
        dr = dla * ((-LRU_C) * sp)
        dsp = _colsum(dla * ((-LRU_C) * r))
        dlam_ref[...] = -dsp / (1.0 + jnp.exp(lam))
        dzr = dr * (r * (1.0 - r))
        dzi = dii * (ii * (1.0 - ii))
        dzrb, dzib = dzr.astype(BF), dzi.astype(BF)
        dxc = dxc + _mm_nt(dzrb, wa) + _mm_nt(dzib, wx)
        dwa_ref[0] = _mm_tn(xcb, dzrb)
        dwx_ref[0] = _mm_tn(xcb, dzib)
        dba_ref[...] = _colsum(dzr)
        dbx_ref[...] = _colsum(dzi)
        dcb_ref[...] = _colsum(dxc)
        dcw_ref[...] = jnp.concatenate([_colsum(dxc * x3), _colsum(dxc * x2), _colsum(dxc * x1),
                                        _colsum(dxc * xp)], axis=0)
        dxp = cw[3:4] * dxc + cw[2:3] * _su(dxc, 1) + cw[1:2] * _su(dxc, 2) + cw[0:1] * _su(dxc, 3)
        dxp_ref[...] = dxp.astype(BF)

    blk = pl.BlockSpec((T, CB), lambda j: (0, j))
    wsp = pl.BlockSpec((1, CB, CB), lambda j: (j, 0, 0))
    return _call(
        body, name="bwd_lru", grid=(NG,),
        in_specs=[blk, pl.BlockSpec((T, CB), lambda j: (0, NG + j)), blk, blk,
                  pl.BlockSpec((4, CB), lambda j: (0, j)), _vec_spec(), wsp, _vec_spec(), wsp, _vec_spec(),
                  _vec_spec()],
        out_specs=[blk, blk, pl.BlockSpec((4, CB), lambda j: (0, j)), _vec_spec(), wsp, _vec_spec(), wsp,
                   _vec_spec(), _vec_spec()],
        out_shape=[jax.ShapeDtypeStruct((T, DR), BF), jax.ShapeDtypeStruct((T, DR), BF),
                   jax.ShapeDtypeStruct((4, DR), F32), jax.ShapeDtypeStruct((1, DR), F32),
                   jax.ShapeDtypeStruct((NG, CB, CB), F32), jax.ShapeDtypeStruct((1, DR), F32),
                   jax.ShapeDtypeStruct((NG, CB, CB), F32), jax.ShapeDtypeStruct((1, DR), F32),
                   jax.ShapeDtypeStruct((1, DR), F32)],
        vmem=56, args=[proj, proj, h, dylru, conv_w, conv_b, wa, ba, wx, bx, lam], stages=stages)


def _bwd_pool(proj, dypool, pool_w, pool_scale):
    def body(xp_ref, dy_ref, pw_ref, sc_ref, dx_ref, dw_ref, dsc_ref):
        for g, w in enumerate(POOL_WINDOWS):
            cols = slice(g * PG, (g + 1) * PG)
            cnt = _pool_cnt(w)
            x = xp_ref[:, cols]
            pb = (_pool_window(x, g + 1, _sd) / cnt - x).astype(BF)
            wg = pw_ref[g]
            dy = dy_ref[:, cols]
            dsc_ref[:, cols] = _colsum(dy * _mm(pb, wg))
            dyp = (dy * sc_ref[:, cols]).astype(BF)
            dw_ref[g] = _mm_tn(pb, dyp)
            dp = _mm_nt(dyp, wg)
            dx_ref[:, cols] = (_pool_window(dp / cnt, g + 1, _su) - dp).astype(BF)

    return pl.pallas_call(
        body, name="bwd_pool", grid=(1,),
        in_specs=[pl.BlockSpec((T, DP), lambda i: (0, 2 * DR // DP)),
                  pl.BlockSpec((T, DP), lambda i: (0, 0)),
                  pl.BlockSpec((4, PG, PG), lambda i: (0, 0, 0)),
                  pl.BlockSpec((1, DP), lambda i: (0, 0))],
        out_specs=[pl.BlockSpec((T, DP), lambda i: (0, 0)),
                   pl.BlockSpec((4, PG, PG), lambda i: (0, 0, 0)),
                   pl.BlockSpec((1, DP), lambda i: (0, 0))],
        out_shape=_hbm_out([jax.ShapeDtypeStruct((T, DP), BF), jax.ShapeDtypeStruct((4, PG, PG), F32),
                            jax.ShapeDtypeStruct((1, DP), F32)]),
        compiler_params=_cp(48),
    )(*_hbm(proj, dypool, pool_w, pool_scale))


def _bwd_inproj(h1, dproj, w_in, stages=()):
    def body(h_ref, dp_ref, w_ref, dw_ref, dh_ref):
        dp = dp_ref[...]
        dw_ref[0] = _mm_tn(h_ref[...], dp)
        _acc(dh_ref, _mm_nt(dp, w_ref[0]), pl.program_id(0) == 0)

    return _call(
        body, name="bwd_inproj", grid=(NCHIP,),
        in_specs=[pl.BlockSpec((T, D), lambda k: (0, 0)),
                  pl.BlockSpec((T, CW_IN), lambda k: (0, k)),
                  pl.BlockSpec((1, D, CW_IN), lambda k: (k, 0, 0))],
        out_specs=[pl.BlockSpec((1, D, CW_IN), lambda k: (k, 0, 0)), pl.BlockSpec((T, D), lambda k: (0, 0))],
        out_shape=[jax.ShapeDtypeStruct((NCHIP, D, CW_IN), F32), jax.ShapeDtypeStruct((T, D), F32)],
        vmem=56, args=[h1, dproj, w_in], stages=stages)


def _bwd_prenorm(x, dh1, dxres, g1, stages=()):
    tm = 512

    def body(x_ref, dh_ref, dr_ref, g_ref, dx_ref, dg_ref):
        xv = x_ref[...]
        r = lax.rsqrt(_mean(xv * xv) + NORM_EPS)
        xn = xv * r
        dh = dh_ref[...]
        t = dh * g_ref[...]
        dx_ref[...] = dr_ref[...] + r * (t - xn * _mean(t * xn))
        _acc(dg_ref, _colsum(dh * xn), pl.program_id(0) == 0)

    row = pl.BlockSpec((tm, D), lambda i: (i, 0))
    vec = pl.BlockSpec((1, D), lambda i: (0, 0))
    return _call(
        body, name="bwd_prenorm", grid=(T // tm,),
        in_specs=[row, row, row, vec], out_specs=[row, vec],
        out_shape=[jax.ShapeDtypeStruct((T, D), F32), jax.ShapeDtypeStruct((1, D), F32)],
        vmem=48, args=[x, dh1, dxres, g1], stages=stages)


def _place():
    x, y, c = lax.axis_index("x"), lax.axis_index("y"), lax.axis_index("c")
    chips = [(1 - x, y), (x, 1 - y), (1 - x, 1 - y)]
    return x, y, c, chips


def _rcopy(src, dst, ssem, rsem, dev):
    return pltpu.make_async_remote_copy(src_ref=src, dst_ref=dst, send_sem=ssem, recv_sem=rsem,
                                        device_id=dev, device_id_type=MESH_ID)


def _sds(a):
    return jax.ShapeDtypeStruct(a.shape, a.dtype)


def _sem2(n, m):
    return [pltpu.SemaphoreType.DMA((n, m)), pltpu.SemaphoreType.DMA((n, m))]


ALL = (0, 1, 1)


def _piece(ref, k, half, part):
    hr = ref.shape[1] // 2
    r0, r1 = hr * part[0] // part[2], hr * part[1] // part[2]
    return ref.at[k, pl.ds(half * hr + r0, r1 - r0), :]


def _gather(fulls, ici=(), d2d=()):
    n = len(fulls)
    ici, d2d = list(ici), list(d2d)

    def copies(outs, sems):
        x, y, c, chips = _place()
        me = 2 * x + y
        sib = (x, y, 1 - c)
        send, recv = [], []
        for q, (i, part) in enumerate(ici):
            for j, chip in enumerate(chips):
                mine, theirs = _piece(outs[i], me, c, part), _piece(outs[i], 2 * chip[0] + chip[1], c, part)
                send.append(_rcopy(mine, mine, sems[0].at[q, j], sems[1].at[q, j], (*chip, c)))
                recv.append(_rcopy(theirs, theirs, sems[0].at[q, j], sems[1].at[q, j], (*chip, c)))
        for q, (i, part) in enumerate(d2d):
            for j, chip in enumerate(chips):
                k = 2 * chip[0] + chip[1]
                got, other = _piece(outs[i], k, c, part), _piece(outs[i], k, 1 - c, part)
                send.append(_rcopy(got, got, sems[2].at[q, j], sems[3].at[q, j], sib))
                recv.append(_rcopy(other, other, sems[2].at[q, j], sems[3].at[q, j], sib))
        return send, recv

    def start(ins, outs, sems):
        for cp in copies(outs, sems)[0]:
            cp.start()

    def finish(ins, outs, sems):
        send, recv = copies(outs, sems)
        for cp in recv:
            cp.wait_recv()
        for cp in send:
            cp.wait_send()

    sems = _sem2(max(len(ici), 1), 3) + _sem2(max(len(d2d), 1), 3)
    return _Stage(fulls, [_sds(f) for f in fulls], {i: i for i in range(n)}, sems, start, finish)


def _gather_first(full, conv_w):
    ici, d2d = _gather([full], ici=[(0, ALL)]), _gather([full], d2d=[(0, ALL)])

    def body(full_in, cw_in, full_out, cw_out, s0, r0, s1, r1, cs, cr):
        x, y, c, chips = _place()
        me = 2 * x + y
        conv = [_rcopy(cw_in, cw_out.at[me], cs.at[j], cr.at[j], (*chip, c)) for j, chip in enumerate(chips)]
        for cp in conv:
            cp.start()
        sems = [s0, r0, s1, r1]
        ici.start(None, [full_out], sems)
        ici.finish(None, [full_out], sems)
        d2d.start(None, [full_out], sems)
        d2d.finish(None, [full_out], sems)
        for j, chip in enumerate(chips):
            _rcopy(cw_in, cw_out.at[2 * chip[0] + chip[1]], cs.at[j], cr.at[j], (*chip, c)).wait_recv()
        for cp in conv:
            cp.wait_send()

    return pl.pallas_call(
        body, name="gather_first",
        in_specs=[ANY, ANY], out_specs=[ANY, ANY],
        out_shape=_hbm_out([full, jax.ShapeDtypeStruct((NCHIP,) + conv_w.shape, conv_w.dtype)]),
        input_output_aliases={0: 0},
        scratch_shapes=_sem2(1, 3) + _sem2(1, 3) + [pltpu.SemaphoreType.DMA((3,)), pltpu.SemaphoreType.DMA((3,))],
        compiler_params=pltpu.CompilerParams(has_side_effects=True),
    )(*_hbm(full, conv_w))


def _comm_only(name, stage):
    def body(*refs):
        ni, no = len(stage.operands), len(stage.out_shape)
        stage.start(refs[:ni], refs[ni:ni + no], refs[ni + no:])
        stage.finish(refs[:ni], refs[ni:ni + no], refs[ni + no:])

    return pl.pallas_call(
        body, name=name, in_specs=[ANY] * len(stage.operands), out_specs=[ANY] * len(stage.out_shape),
        out_shape=_hbm_out(stage.out_shape), input_output_aliases=stage.alias, scratch_shapes=stage.sems,
        compiler_params=pltpu.CompilerParams(has_side_effects=True),
    )(*_hbm(*stage.operands))


def _to_sibling(srcs):
    n = len(srcs)

    def copies(ins, outs, sems):
        x, y, c, _ = _place()
        sib = (x, y, 1 - c)
        return [_rcopy(ins[i].at[:, 1 - c] if srcs[i].ndim == 4 else ins[i], outs[i], sems[0].at[i], sems[1].at[i], sib)
                for i in range(n)]

    def start(ins, outs, sems):
        for cp in copies(ins, outs, sems):
            cp.start()

    def finish(ins, outs, sems):
        for cp in copies(ins, outs, sems):
            cp.wait()

    shapes = [jax.ShapeDtypeStruct((NCHIP,) + s.shape[2:] if s.ndim == 4 else s.shape, s.dtype) for s in srcs]
    return _Stage(srcs, shapes, {}, [pltpu.SemaphoreType.DMA((n,)), pltpu.SemaphoreType.DMA((n,))], start, finish)


def _to_chips(srcs, parts=None, lands=None):
    n = len(srcs)
    parts = [ALL] * n if parts is None else parts
    lands = [None] * n if lands is None else lands
    given = [i for i in range(n) if lands[i] is not None]

    def rows(ref, i):
        hr = srcs[i].shape[1]
        r0, r1 = hr * parts[i][0] // parts[i][2], hr * parts[i][1] // parts[i][2]
        return ref.at[pl.ds(r0, r1 - r0), :]

    def copies(ins, outs, sems):
        x, y, c, chips = _place()
        me = 2 * x + y
        return [_rcopy(rows(ins[i].at[2 * chip[0] + chip[1]] if srcs[i].shape[0] == NCHIP else ins[i].at[c], i),
                       rows(outs[i].at[me], i), sems[0].at[i, j], sems[1].at[i, j], (*chip, c))
                for i in range(n) for j, chip in enumerate(chips)]

    def start(ins, outs, sems):
        for cp in copies(ins, outs, sems):
            cp.start()

    def finish(ins, outs, sems):
        for cp in copies(ins, outs, sems):
            cp.wait()

    shapes = [jax.ShapeDtypeStruct((NCHIP,) + s.shape[1:], s.dtype) for s in srcs]
    alias = {n + q: i for q, i in enumerate(given)}
    return _Stage(list(srcs) + [lands[i] for i in given], shapes, alias, _sem2(n, 3), start, finish)


def _share(pairs):
    n = len(pairs)

    def start(ins, outs, sems):
        x, y, c, _ = _place()
        for i in range(n):
            _rcopy(outs[i].at[c], outs[i].at[c], sems[0].at[i], sems[1].at[i], (x, y, 1 - c)).start()

    def finish(ins, outs, sems):
        x, y, c, _ = _place()
        for i in range(n):
            _rcopy(outs[i].at[c], outs[i].at[c], sems[0].at[i], sems[1].at[i], (x, y, 1 - c)).wait_send()
            _rcopy(outs[i].at[1 - c], outs[i].at[1 - c], sems[0].at[i], sems[1].at[i], (x, y, 1 - c)).wait_recv()

    return _Stage(pairs, [_sds(p) for p in pairs], {i: i for i in range(n)},
                  [pltpu.SemaphoreType.DMA((n,)), pltpu.SemaphoreType.DMA((n,))], start, finish)


def _row_block(rows, cols, itemsize=4, target=MIB):
    br = rows
    while br * cols * itemsize > target and br % 16 == 0:
        br //= 2
    return br


def _cast_place(w, chip_idx, name):
    rows, cols = w.shape
    br = _row_block(rows, cols)

    def body(k_ref, w_ref, o_ref):
        o_ref[0] = w_ref[...].astype(BF)

    return _call(
        body, name=name, grid=(rows // br,), prefetch=chip_idx,
        in_specs=[pl.BlockSpec((br, cols), lambda r, k: (r, 0))],
        out_specs=[pl.BlockSpec((1, br, cols), lambda r, k: (k[0], r, 0))],
        out_shape=[jax.ShapeDtypeStruct((NCHIP, rows, cols), BF)], vmem=32, args=[w])[0][0]


def _add_sibling(g, land, cidx, name, stages=()):
    _, _, hr, cols = g.shape
    br = _row_block(hr, cols)

    def body(c_ref, g_ref, l_ref, o_ref):
        o_ref[...] = (g_ref[0, 0] + l_ref[0]).astype(BF)[None]

    outs, st = _call(
        body, name=name, grid=(NCHIP, hr // br), prefetch=cidx,
        in_specs=[pl.BlockSpec((1, 1, br, cols), lambda k, r, c: (k, c[0], r, 0)),
                  pl.BlockSpec((1, br, cols), lambda k, r, c: (k, r, 0))],
        out_specs=[pl.BlockSpec((1, br, cols), lambda k, r, c: (k, r, 0))],
        out_shape=[jax.ShapeDtypeStruct((NCHIP, hr, cols), BF)], vmem=32, args=[g, land], stages=stages)
    return outs[0], st


def _add_pair(a, b, name):
    rows, cols = a.shape

    def body(a_ref, b_ref, o_ref):
        o_ref[...] = a_ref[...] + b_ref[...]

    spec = pl.BlockSpec((rows, cols), lambda r: (0, 0))
    return _call(body, name=name, grid=(1,), in_specs=[spec, spec], out_specs=[spec], out_shape=[_sds(a)],
                 vmem=32, args=[a, b])[0][0]


def _add_chips(own, land, idx, name):
    _, hr, cols = land.shape
    br = _row_block(hr, cols)

    def body(s_ref, a_ref, b_ref, c_ref, d_ref, o_ref):
        o_ref[...] = (a_ref[...].astype(F32) + b_ref[...].astype(F32)) + (c_ref[...].astype(F32) +
                                                                           d_ref[...].astype(F32))

    spec = lambda q: pl.BlockSpec((1, br, cols), functools.partial(lambda r, s, q: (s[q], r, 0), q=q))
    return _call(
        body, name=name, grid=(hr // br,), prefetch=idx,
        in_specs=[spec(0), spec(1), spec(2), spec(3)], out_specs=[spec(4)],
        out_shape=[jax.ShapeDtypeStruct((2, hr, cols), F32)], vmem=32, args=[own, land, land, land])[0][0]


def _adamw_math(w, g, m, v):
    mn = ADAM_B1 * m + (1.0 - ADAM_B1) * g
    vn = ADAM_B2 * v + (1.0 - ADAM_B2) * (g * g)
    m_hat = mn / (1.0 - ADAM_B1 ** ADAM_STEP)
    v_hat = vn / (1.0 - ADAM_B2 ** ADAM_STEP)
    return -ADAM_LR * (m_hat / (jnp.sqrt(v_hat) + ADAM_EPS) + ADAM_WD * w), mn, vn


def _adamw(w, g, m, v, name, stages=()):
    rows, cols = w.shape
    br = _row_block(rows, cols)

    def body(w_ref, g_ref, m_ref, v_ref, d_ref, mo_ref, vo_ref):
        d_ref[...], mo_ref[...], vo_ref[...] = _adamw_math(w_ref[...], g_ref[...], m_ref[...], v_ref[...])

    spec = pl.BlockSpec((br, cols), lambda r: (r, 0))
    return _call(body, name=name, grid=(rows // br,), in_specs=[spec] * 4, out_specs=[spec] * 3,
                 out_shape=[_sds(w)] * 3, vmem=32, args=[w, g, m, v], stages=stages)


def _adamw_multi(names, w, g, m, v, stages=()):
    cols = w[names[0]].shape[1]
    br = 128
    nblk = [w[n].shape[0] // br for n in names]
    starts = [sum(nblk[:i]) for i in range(len(names))]

    def body(*refs):
        r = pl.program_id(0)
        for i in range(len(names)):
            w_ref, g_ref, m_ref, v_ref = refs[4 * i:4 * i + 4]
            d_ref, mo_ref, vo_ref = refs[4 * len(names) + 3 * i:4 * len(names) + 3 * i + 3]

            @pl.when(jnp.logical_and(r >= starts[i], r < starts[i] + nblk[i]))
            def _():
                d_ref[...], mo_ref[...], vo_ref[...] = _adamw_math(w_ref[...], g_ref[...], m_ref[...], v_ref[...])

    def spec(i):
        return pl.BlockSpec((br, cols), functools.partial(
            lambda r, s, nb: (jnp.clip(r - s, 0, nb - 1), 0), s=starts[i], nb=nblk[i]))

    outs, landed = _call(
        body, name="adamw_" + "_".join(names), grid=(sum(nblk),),
        in_specs=[spec(i) for i in range(len(names)) for _ in range(4)],
        out_specs=[spec(i) for i in range(len(names)) for _ in range(3)],
        out_shape=[_sds(w[n]) for n in names for _ in range(3)], vmem=48,
        args=[a[n] for n in names for a in (w, g, m, v)], stages=stages)
    return {n: outs[3 * i:3 * i + 3] for i, n in enumerate(names)}, landed


def _to_everyone(v):
    deltas = [(a, b, e) for a in (0, 1) for b in (0, 1) for e in (0, 1)][1:]

    def copies(ins, outs, sems):
        x, y, c, _ = _place()
        me = 4 * x + 2 * y + c
        flip = lambda p, f: 1 - p if f else p
        return [_rcopy(ins[0], outs[0].at[me], sems[0].at[q], sems[1].at[q], (flip(x, a), flip(y, b), flip(c, e)))
                for q, (a, b, e) in enumerate(deltas)]

    def start(ins, outs, sems):
        for cp in copies(ins, outs, sems):
            cp.start()

    def finish(ins, outs, sems):
        for cp in copies(ins, outs, sems):
            cp.wait()

    n = len(deltas)
    return _Stage([v], [jax.ShapeDtypeStruct((2 * NCHIP,) + v.shape, v.dtype)], {},
                  [pltpu.SemaphoreType.DMA((n,)), pltpu.SemaphoreType.DMA((n,))], start, finish)


SMALL_AT = {"norm_mix_pre": (0, 1, D), "norm_mix_post": (1, 1, D), "norm_mlp_pre": (2, 1, D),
            "norm_mlp_post": (3, 1, D), "b_gate": (4, 2, D), "conv_b": (6, 1, D), "lru_b_a": (7, 1, D),
            "lru_b_x": (8, 1, D), "lru_lambda": (9, 1, D), "pool_scale": (10, 1, DP)}
SMALL_SEPARATE = ["conv_w", "lru_w_a", "lru_w_x", "pool_w"]


def _adamw_small(small_sum, first_all, sep_grads, w, m, v):
    packed, sep = list(SMALL_AT), list(SMALL_SEPARATE)
    names = packed + sep

    def body(*refs):
        s_ref, a_ref, refs = refs[0], refs[1], refs[2:]
        g_sep, refs = refs[:len(sep)], refs[len(sep):]
        nn = len(names)
        w_r, m_r, v_r, refs = refs[:nn], refs[nn:2 * nn], refs[2 * nn:3 * nn], refs[3 * nn:]
        g_out, refs = refs[:len(packed)], refs[len(packed):]
        d_o, m_o, v_o = refs[:nn], refs[nn:2 * nn], refs[2 * nn:3 * nn]
        for i, n in enumerate(names):
            if i == 0:
                g = a_ref[0:1, :]
                for q in range(1, 2 * NCHIP):
                    g = g + a_ref[q:q + 1, :]
                g_out[i][...] = g
            elif n in SMALL_AT:
                r0, nr, nc = SMALL_AT[n]
                g = jnp.concatenate([s_ref[r0 + q:r0 + q + 1, :nc] for q in range(nr)], axis=1)
                g_out[i][...] = g
            else:
                g = g_sep[i - len(packed)][...]
            d_o[i][...], m_o[i][...], v_o[i][...] = _adamw_math(w_r[i][...], g, m_r[i][...], v_r[i][...])

    ws = [w[n] for n in names]
    res = pl.pallas_call(
        body, name="adamw_small",
        out_shape=[_sds(w[n]) for n in packed] + [_sds(a) for a in ws] * 3,
        compiler_params=_cp(32),
    )(*_hbm(small_sum, first_all, *sep_grads, *ws, *[m[n] for n in names], *[v[n] for n in names]))
    nn, npk = len(names), len(packed)
    grad = dict(zip(packed, res[:npk]))
    delta = dict(zip(names, res[npk:npk + nn]))
    new_m = dict(zip(names, res[npk + nn:npk + 2 * nn]))
    new_v = dict(zip(names, res[npk + 2 * nn:]))
    return grad, delta, new_m, new_v


W_NAMES = ["norm_mix_pre", "norm_mix_post", "norm_mlp_pre", "norm_mlp_post", "w_in", "b_gate", "conv_w", "conv_b",
           "lru_w_a", "lru_b_a", "lru_w_x", "lru_b_x", "lru_lambda", "pool_w", "pool_scale", "w_lru_up",
           "w_pool_up", "w_o", "w_ff1", "w_ff2"]
BIG = ["w_in", "w_lru_up", "w_pool_up", "w_o", "w_ff1", "w_ff2"]


def _block_diag(w):
    hd = w.shape[-1]
    per = CB // hd
    w4 = w.reshape(NG, per, hd, hd)
    eye = jnp.eye(per, dtype=w.dtype)
    return jnp.einsum("gpij,pq->gpiqj", w4, eye).reshape(NG, CB, CB)


def _block_diag_extract(d, hd):
    per = CB // hd
    d5 = d.reshape(NG, per, hd, per, hd)
    return jnp.stack([d5[:, p, :, p, :] for p in range(per)], axis=1).reshape(NG * per, hd, hd)


def _halves(g):
    return g.reshape(NCHIP, 2, g.size // (g.shape[-1] * 2 * NCHIP), g.shape[-1])


def kernel(x, norm_mix_pre, norm_mix_post, norm_mlp_pre, norm_mlp_post, w_in, b_gate, conv_w, conv_b, lru_w_a, lru_b_a, lru_w_x, lru_b_x, lru_lambda, pool_w, pool_scale, w_lru_up, w_pool_up, w_o, w_ff1, w_ff2, loss_target, m_norm_mix_pre, m_norm_mix_post, m_norm_mlp_pre, m_norm_mlp_post, m_w_in, m_b_gate, m_conv_w, m_conv_b, m_lru_w_a, m_lru_b_a, m_lru_w_x, m_lru_b_x, m_lru_lambda, m_pool_w, m_pool_scale, m_w_lru_up, m_w_pool_up, m_w_o, m_w_ff1, m_w_ff2, v_norm_mix_pre, v_norm_mix_post, v_norm_mlp_pre, v_norm_mlp_post, v_w_in, v_b_gate, v_conv_w, v_conv_b, v_lru_w_a, v_lru_b_a, v_lru_w_x, v_lru_b_x, v_lru_lambda, v_pool_w, v_pool_scale, v_w_lru_up, v_w_pool_up, v_w_o, v_w_ff1, v_w_ff2):
    args = dict(locals())
    two_d = lambda a: a.reshape(-1, a.shape[-1])
    w = {n: two_d(args[n]) for n in W_NAMES}
    mom = {n: two_d(args["m_" + n]) for n in W_NAMES}
    var = {n: two_d(args["v_" + n]) for n in W_NAMES}
    i32 = lambda val: jnp.asarray(val, jnp.int32)
    chip = i32(2 * lax.axis_index("x") + lax.axis_index("y"))
    core = i32(lax.axis_index("c"))
    cidx = core.reshape(1)
    zero = i32(0)
    hd = lru_w_a.shape[-1]
    xs, target = x[0], loss_target[0]
    g1, g2, g3, g4 = norm_mix_pre, norm_mix_post, norm_mlp_pre, norm_mlp_post

    full = {n: _cast_place(w[n], chip.reshape(1), "cast_" + n) for n in BIG}
    wa = _block_diag(lru_w_a[0]).astype(BF)
    wx = _block_diag(lru_w_x[0]).astype(BF)
    pw = pool_w[0].astype(BF)

    full["w_in"], conv_all = _gather_first(full["w_in"], w["conv_w"])
    conv_all = lax.dynamic_update_slice(conv_all, w["conv_w"][None], (chip, zero, zero))
    conv_full = jnp.transpose(conv_all, (1, 0, 2)).reshape(4, DR)
    mix = ["w_lru_up", "w_pool_up", "w_o"]
    ff1_a, ff1_b, ff2_a, ff2_b = (0, 3, 8), (3, 8, 8), (0, 1, 4), (1, 4, 4)
    (proj, h1), (got,) = _fwd_inproj(xs, g1, full["w_in"], stages=[_gather(
        [full[n] for n in mix] + [full["w_ff1"]], ici=[(0, ALL), (1, ALL), (2, ALL), (3, ff1_a)])])
    (ylru, hs), (got,) = _fwd_lru(proj, conv_full, conv_b, wa, lru_b_a, wx, lru_b_x, lru_lambda, stages=[_gather(
        got + [full["w_ff2"]], d2d=[(0, ALL), (1, ALL), (2, ALL), (3, ff1_a)], ici=[(3, ff1_b), (4, ff2_a)])])
    w_lru_up_f, w_pool_up_f, w_o_f = got[0].reshape(DR, D), got[1], got[2].reshape(D, D)
    ypool = _fwd_pool(proj, pw, pool_scale)
    (x2, h2, m, mrg, bra, brb), ((ff1, ff2),) = _fwd_merge(
        xs, ylru, ypool, proj, b_gate, g2, g3, w_lru_up_f, w_pool_up_f, w_o_f,
        stages=[_gather(got[3:], d2d=[(0, ff1_b), (1, ff2_a)], ici=[(1, ff2_b)])])
    ff2 = _comm_only("gather_last", _gather([ff2], d2d=[(0, ff2_b)]))[0].reshape(DF, D)
    a1, f = _fwd_mlp(h2, ff1, ff2)
    lossp, dy, df, dg4 = _loss_head(f, x2, target, g4)

    idx_big = jnp.stack([chip, (chip + 1) % NCHIP, (chip + 2) % NCHIP, (chip + 3) % NCHIP, core])
    dh2, df1 = _bwd_mlp_x(df, a1, ff1, ff2)
    dw_ff1, dw_ff2 = _bwd_mlp_w(df, h2, a1, df1)
    g_ff = [_halves(dw_ff1), _halves(dw_ff2)]
    (dxres, dgates, dylru, dypool, dm, dbra, dbrb, dg2, dg3, dbg), (l_ff,) = _bwd_merge(
        dh2, dy, x2, m, bra, brb, proj, b_gate, g2, g3, w_lru_up_f, w_pool_up_f, w_o_f, stages=[_to_sibling(g_ff)])
    dw_o, dw_lru_up, dw_pool_up = _dw_merge(mrg, dm, ylru, dbra, ypool, dbrb)
    p_ff = [_add_sibling(g, l, cidx, "add_sibling_" + n)[0] for g, l, n in zip(g_ff, l_ff, ["w_ff1", "w_ff2"])]
    g_mix = [_halves(dw_lru_up), _halves(dw_pool_up), _halves(dw_o)]
    ff2_head, ff2_tail = (0, 5, 8), (5, 8, 8)
    (dxp, dgl, dcw, dcb, dwa, dba, dwx, dbx, dlam), ((c_ff1, c_ff2), l_mix) = _bwd_lru(
        proj, hs, dylru, conv_full, conv_b, wa, lru_b_a, wx, lru_b_x, lru_lambda,
        stages=[_to_chips(p_ff, parts=[ALL, ff2_head]), _to_sibling(g_mix)])
    p_mix = [_add_sibling(g, l, cidx, "add_sibling_" + n)[0] for g, l, n in zip(g_mix, l_mix, mix)]
    dxpool, dpw, dsc = _bwd_pool(proj, dypool, pw, pool_scale)
    dproj = jnp.concatenate([dxp, dgl, dxpool, dgates], axis=1)
    small = jnp.concatenate([
        jnp.zeros((1, D), F32), dg2, dg3, dg4, dbg.reshape(2, D), dcb, dba, dbx, dlam,
        jnp.pad(dsc, ((0, 0), (0, D - DP))), jnp.pad(lossp, ((0, 0), (0, D - 1))), dcw,
        _block_diag_extract(dwa, hd).reshape(-1, D), _block_diag_extract(dwx, hd).reshape(-1, D),
        dpw.reshape(-1, D)], axis=0)
    (dw_in, dh1), (c_rest, (l_small,)) = _bwd_inproj(h1, dproj, full["w_in"], stages=[
        _to_chips(p_ff[1:] + p_mix, parts=[ff2_tail, ALL, ALL, ALL], lands=[c_ff2, None, None, None]),
        _to_sibling([small])])
    small2 = _add_pair(small, l_small, "add_sibling_small").reshape(2, SMALL_ROWS // 2, D)
    done = ["w_ff1", "w_ff2"] + mix
    pairs = [_add_chips(p, l, idx_big, "add_chips_" + n) for p, l, n in zip(p_ff + p_mix, [c_ff1] + c_rest, done)]
    g_in = _halves(dw_in)
    (grad_x, dg1), ((l_in,), pairs, (c_small,)) = _bwd_prenorm(
        xs, dh1, dxres, g1, stages=[_to_sibling([g_in]), _share(pairs), _to_chips([small2])])
    own_small = lax.dynamic_index_in_dim(small2, core, 0, keepdims=True)
    c_small = lax.dynamic_update_slice(c_small, own_small, (chip, zero, zero))
    pair_small = _add_chips(c_small, c_small, jnp.stack([zero, zero + 1, zero + 2, zero + 3, core]), "add_chips_small")
    p_in, ((pair_small,), (dg1_all,)) = _add_sibling(
        g_in, l_in, cidx, "add_sibling_w_in", stages=[_share([pair_small]), _to_everyone(dg1)])
    dg1_all = lax.dynamic_update_slice(dg1_all, dg1[None], (2 * chip + core, zero, zero)).reshape(2 * NCHIP, D)

    grads, delta, new_m, new_v = {}, {}, {}, {}
    for n, p in zip(done, pairs):
        grads[n] = p.reshape(-1, p.shape[-1])

    def update(n, stages=()):
        (delta[n], new_m[n], new_v[n]), landed = _adamw(w[n], grads[n], mom[n], var[n], "adamw_" + n, stages=stages)
        return landed

    updated, ((c_in,),) = _adamw_multi(["w_ff1", "w_ff2", "w_o", "w_lru_up"], w, grads, mom, var,
                                      stages=[_to_chips([p_in])])
    for n, (d, mo, vo) in updated.items():
        delta[n], new_m[n], new_v[n] = d, mo, vo
    pair_in = _add_chips(p_in, c_in, idx_big, "add_chips_w_in")
    ((pair_in,),) = update("w_pool_up", stages=[_share([pair_in])])
    grads["w_in"] = pair_in.reshape(-1, pair_in.shape[-1])
    update("w_in")
    small_sum = pair_small.reshape(SMALL_ROWS, D)
    loss = 0.5 * small_sum[LOSS_ROW, 0]
    ccols = DR // NCHIP
    sep = [lax.dynamic_slice(small_sum[12:16], (zero, chip * ccols), (4, ccols)),
           small_sum[16:80].reshape(-1, hd), small_sum[80:144].reshape(-1, hd), small_sum[144:208].reshape(-1, PG)]
    g_s, d_s, m_s, v_s = _adamw_small(small_sum, dg1_all, sep, w, mom, var)
    grads.update(g_s)
    grads.update(dict(zip(SMALL_SEPARATE, sep)))
    delta.update(d_s)
    new_m.update(m_s)
    new_v.update(v_s)

    out = lambda d: [d[n].reshape(args[n].shape) for n in W_NAMES]
    return (loss, grad_x[None], *out(grads), *out(delta), *out(new_m), *out(new_v))
```

```python
import functools
import math

import jax
import jax.numpy as jnp
from jax import lax
from jax.experimental import pallas as pl
from jax.experimental.pallas import tpu as pltpu

F32 = jnp.float32
BF = jnp.bfloat16

T = 2048
D = 1024
DR = 1024
DP = 512
DF = 4096
DIN = 4608
NCHIP = 4
CW_IN = DIN // NCHIP
LANE = 128
CB = 128
NG = DR // CB
PG = 128
POOL_WINDOWS = (2, 4, 8, 16)
NORM_EPS = 1e-6
LRU_C = 8.0
GELU_C = math.sqrt(2.0 / math.pi)
ADAM_LR = 0.001
ADAM_B1 = 0.9
ADAM_B2 = 0.999
ADAM_EPS = 1e-08
ADAM_WD = 0.01
ADAM_STEP = 10
MESH_ID = pl.DeviceIdType.MESH
ANY = pl.BlockSpec(memory_space=pl.ANY)
SMALL_ROWS = 208
LOSS_ROW = 11
MIB = 1 << 20


def _cp(vmem_mib=None):
    if vmem_mib is None:
        return pltpu.CompilerParams()
    return pltpu.CompilerParams(vmem_limit_bytes=vmem_mib * MIB)


def _hbm(*arrays):
    return [pltpu.with_memory_space_constraint(a, pltpu.HBM) for a in arrays]


def _hbm_out(shapes):
    return [pltpu.HBM(s.shape, s.dtype) for s in shapes]


class _Stage:
    def __init__(self, operands, out_shape, alias, sems, start, finish):
        self.operands, self.out_shape, self.alias, self.sems = list(operands), list(out_shape), dict(alias), list(sems)
        self.start, self.finish = start, finish


def _call(body, *, name, grid, in_specs, out_specs, out_shape, args, vmem=None, stages=(), prefetch=None,
          scratch=()):
    nin, nout = len(in_specs), len(out_specs)
    npre = 0 if prefetch is None else 1
    st_args, st_shapes, st_sems, aliases = [], [], list(scratch), {}
    for st in stages:
        for k, v in st.alias.items():
            aliases[npre + nin + len(st_args) + k] = nout + len(st_shapes) + v
        st_args += st.operands
        st_shapes += st.out_shape
        st_sems += st.sems

    def wrapped(*refs):
        pre, refs = refs[:npre], refs[npre:]
        ins, pos = refs[:nin], nin
        st_ins = []
        for st in stages:
            st_ins.append(refs[pos:pos + len(st.operands)])
            pos += len(st.operands)
        outs, pos = refs[pos:pos + nout], pos + nout
        st_outs = []
        for st in stages:
            st_outs.append(refs[pos:pos + len(st.out_shape)])
            pos += len(st.out_shape)
        work, pos = refs[pos:pos + len(scratch)], pos + len(scratch)
        sems = []
        for st in stages:
            sems.append(refs[pos:pos + len(st.sems)])
            pos += len(st.sems)
        if stages:
            first = functools.reduce(jnp.logical_and, [pl.program_id(a) == 0 for a in range(len(grid))])

            @pl.when(first)
            def _():
                for st, a, b, s in zip(stages, st_ins, st_outs, sems):
                    st.start(a, b, s)

        body(*pre, *ins, *outs, *work)
        if stages:
            last = functools.reduce(jnp.logical_and, [pl.program_id(a) == g - 1 for a, g in enumerate(grid)])

            @pl.when(last)
            def _():
                for st, a, b, s in zip(stages, st_ins, st_outs, sems):
                    st.finish(a, b, s)

    all_in = list(in_specs) + [ANY] * len(st_args)
    all_out = list(out_specs) + [ANY] * len(st_shapes)
    kw = dict(has_side_effects=True) if stages else {}
    if vmem is not None:
        kw["vmem_limit_bytes"] = vmem * MIB
    if prefetch is None:
        gkw = dict(grid=grid, in_specs=all_in, out_specs=all_out, scratch_shapes=st_sems)
    else:
        gkw = dict(grid_spec=pltpu.PrefetchScalarGridSpec(
            num_scalar_prefetch=1, grid=grid, in_specs=all_in, out_specs=all_out, scratch_shapes=st_sems))
    res = pl.pallas_call(
        wrapped, name=name, out_shape=_hbm_out(list(out_shape) + st_shapes), input_output_aliases=aliases,
        compiler_params=pltpu.CompilerParams(**kw), **gkw,
    )(*([prefetch] if npre else []), *_hbm(*args, *st_args))
    outs, rest, st_res = list(res[:nout]), list(res[nout:]), []
    for st in stages:
        st_res.append(rest[:len(st.out_shape)])
        rest = rest[len(st.out_shape):]
    return outs, st_res


def _mm(a, b):
    return jnp.dot(a.astype(BF), b.astype(BF), preferred_element_type=F32)


def _mm_nt(a, b):
    return lax.dot_general(a.astype(BF), b.astype(BF), (((1,), (1,)), ((), ())),
                           preferred_element_type=F32)


def _mm_tn(a, b):
    return lax.dot_general(a.astype(BF), b.astype(BF), (((0,), (0,)), ((), ())),
                           preferred_element_type=F32)


def _rows(v):
    return lax.broadcasted_iota(jnp.int32, v.shape, 0)


def _sd(v, s, fill=0.0):
    return jnp.where(_rows(v) >= s, pltpu.roll(v, s, axis=0), fill)


def _su(v, s, fill=0.0):
    n = v.shape[0]
    return jnp.where(_rows(v) < n - s, pltpu.roll(v, n - s, axis=0), fill)


def _sigmoid(z):
    return 1.0 / (1.0 + jnp.exp(-z))


def _softplus(z):
    e = jnp.exp(-jnp.abs(z))
    u = 1.0 + e
    d = u - 1.0
    log1p = jnp.where(d == 0.0, e, jnp.log(u) * (e / jnp.where(d == 0.0, 1.0, d)))
    return jnp.maximum(z, 0.0) + log1p


def _mean(v):
    return jnp.mean(v, axis=-1, keepdims=True)


def _colsum(v):
    return jnp.sum(v, axis=0, keepdims=True)


def _acc(ref, val, first):
    @pl.when(first)
    def _():
        ref[...] = val

    @pl.when(jnp.logical_not(first))
    def _():
        ref[...] += val


def _conv(xp, cw, cb):
    x1, x2, x3 = _sd(xp, 1), _sd(xp, 2), _sd(xp, 3)
    xc = cb + cw[0:1] * x3 + cw[1:2] * x2 + cw[2:3] * x1 + cw[3:4] * xp
    return xc, x1, x2, x3


def _lru_gates(xc, wa, ba, wx, bx, lam):
    xcb = xc.astype(BF)
    r = _sigmoid(_mm(xcb, wa) + ba)
    ii = _sigmoid(_mm(xcb, wx) + bx)
    sp = _softplus(-lam)
    la = (-LRU_C) * r * sp
    a = jnp.exp(la)
    mult = jnp.sqrt(-jnp.tanh(la) * (a * a + 1.0))
    return xcb, r, ii, sp, a, mult


def _gelu_parts(g):
    th = jnp.tanh(GELU_C * (g + 0.044715 * (g * g * g)))
    gel = 0.5 * g * (1.0 + th)
    dgel = 0.5 * (1.0 + th) + 0.5 * g * (1.0 - th * th) * (GELU_C * (1.0 + 3.0 * 0.044715 * (g * g)))
    return gel, dgel


def _pool_window(x, steps, shift):
    s, sh = x, 1
    for _ in range(steps):
        s = s + shift(s, sh)
        sh *= 2
    return s


def _fwd_inproj(x, g1, w_in, stages=()):
    tm = 512

    def body(x_ref, g_ref, w_ref, proj_ref, h_ref):
        @pl.when(pl.program_id(1) == 0)
        def _():
            xv = x_ref[...]
            r = lax.rsqrt(_mean(xv * xv) + NORM_EPS)
            h_ref[...] = ((xv * r) * g_ref[...]).astype(BF)

        proj_ref[...] = jnp.dot(h_ref[...], w_ref[0], preferred_element_type=F32)

    return _call(
        body, name="fwd_inproj", grid=(T // tm, NCHIP),
        in_specs=[pl.BlockSpec((tm, D), lambda i, k: (i, 0)),
                  pl.BlockSpec((1, D), lambda i, k: (0, 0)),
                  pl.BlockSpec((1, D, CW_IN), lambda i, k: (k, 0, 0))],
        out_specs=[pl.BlockSpec((tm, CW_IN), lambda i, k: (i, k)),
                   pl.BlockSpec((tm, D), lambda i, k: (i, 0))],
        out_shape=[jax.ShapeDtypeStruct((T, DIN), F32), jax.ShapeDtypeStruct((T, D), BF)],
        vmem=40, args=[x, g1, w_in], stages=stages)


def _vec_spec():
    return pl.BlockSpec((1, CB), lambda j: (0, j))


def _fwd_lru(proj, conv_w, conv_b, wa, ba, wx, bx, lam, stages=()):
    def body(xp_ref, g_ref, cw_ref, cb_ref, wa_ref, ba_ref, wx_ref, bx_ref, lam_ref, y_ref, h_ref):
        xc, _, _, _ = _conv(xp_ref[...], cw_ref[...], cb_ref[...])
        _, _, ii, _, a, mult = _lru_gates(xc, wa_ref[0], ba_ref[...], wx_ref[0], bx_ref[...], lam_ref[...])
        b = mult * (ii * xc)
        s = 1
        while s < T:
            b = b + a * _sd(b, s, 0.0)
            if 2 * s < T:
                a = a * _sd(a, s, 1.0)
            s *= 2
        h_ref[...] = b
        gel, _ = _gelu_parts(g_ref[...])
        y_ref[...] = (b * gel).astype(BF)

    return _call(
        body, name="fwd_lru", grid=(NG,),
        in_specs=[pl.BlockSpec((T, CB), lambda j: (0, j)),
                  pl.BlockSpec((T, CB), lambda j: (0, NG + j)),
                  pl.BlockSpec((4, CB), lambda j: (0, j)),
                  _vec_spec(),
                  pl.BlockSpec((1, CB, CB), lambda j: (j, 0, 0)), _vec_spec(),
                  pl.BlockSpec((1, CB, CB), lambda j: (j, 0, 0)), _vec_spec(),
                  _vec_spec()],
        out_specs=[pl.BlockSpec((T, CB), lambda j: (0, j)), pl.BlockSpec((T, CB), lambda j: (0, j))],
        out_shape=[jax.ShapeDtypeStruct((T, DR), BF), jax.ShapeDtypeStruct((T, DR), F32)],
        vmem=48, args=[proj, proj, conv_w, conv_b, wa, ba, wx, bx, lam], stages=stages)


def _pool_cnt(w):
    t = lax.broadcasted_iota(jnp.int32, (T, 1), 0)
    return jnp.minimum(t + 1, w).astype(F32)


def _fwd_pool(proj, pool_w, pool_scale):
    def body(xp_ref, pw_ref, sc_ref, y_ref):
        for g, w in enumerate(POOL_WINDOWS):
            cols = slice(g * PG, (g + 1) * PG)
            x = xp_ref[:, cols]
            p = _pool_window(x, g + 1, _sd) / _pool_cnt(w) - x
            y_ref[:, cols] = (_mm(p, pw_ref[g]) * sc_ref[:, cols]).astype(BF)

    return pl.pallas_call(
        body, name="fwd_pool", grid=(1,),
        in_specs=[pl.BlockSpec((T, DP), lambda i: (0, 2 * DR // DP)),
                  pl.BlockSpec((4, PG, PG), lambda i: (0, 0, 0)),
                  pl.BlockSpec((1, DP), lambda i: (0, 0))],
        out_specs=pl.BlockSpec((T, DP), lambda i: (0, 0)),
        out_shape=pltpu.HBM((T, DP), BF),
        compiler_params=_cp(48),
    )(*_hbm(proj, pool_w, pool_scale))


GATE_BLK = 512
GATE_BLK0 = (2 * DR + DP) // GATE_BLK


def _gate_specs(tm):
    return [pl.BlockSpec((tm, GATE_BLK), functools.partial(lambda i, q: (i, GATE_BLK0 + q), q=q))
            for q in range(4)]


def _fwd_merge(x, ylru, ypool, proj, b_gate, g2, g3, w_lru_up, w_pool_up, w_o, stages=()):
    tm = 512

    def body(x_ref, yl_ref, yp_ref, p0, p1, p2, p3, bg_ref, g2_ref, g3_ref, wl_ref, wp_ref, wo_ref,
             x2_ref, h2_ref, m_ref, mrg_ref, bra_ref, brb_ref):
        bra = jnp.dot(yl_ref[...], wl_ref[...], preferred_element_type=F32)
        yp = yp_ref[...]
        brb = jnp.concatenate([jnp.dot(yp, wp_ref[k], preferred_element_type=F32) for k in range(NCHIP)], axis=1)
        bg = bg_ref[...]
        ga = _sigmoid(jnp.concatenate([p0[...], p1[...]], axis=1) + bg[:, :D])
        gb = _sigmoid(jnp.concatenate([p2[...], p3[...]], axis=1) + bg[:, D:])
        mrg = (ga * bra + gb * brb).astype(BF)
        m = jnp.dot(mrg, wo_ref[...], preferred_element_type=F32)
        r2 = lax.rsqrt(_mean(m * m) + NORM_EPS)
        x2 = x_ref[...] + (m * r2) * g2_ref[...]
        r3 = lax.rsqrt(_mean(x2 * x2) + NORM_EPS)
        x2_ref[...] = x2
        h2_ref[...] = ((x2 * r3) * g3_ref[...]).astype(BF)
        m_ref[...] = m
        mrg_ref[...] = mrg
        bra_ref[...] = bra.astype(BF)
        brb_ref[...] = brb.astype(BF)

    row = lambda w: pl.BlockSpec((tm, w), lambda i: (i, 0))
    full2 = lambda a, b: pl.BlockSpec((a, b), lambda i: (0, 0))
    return _call(
        body, name="fwd_merge", grid=(T // tm,),
        in_specs=[row(D), row(DR), row(DP)] + _gate_specs(tm) +
                 [full2(1, 2 * D), full2(1, D), full2(1, D), full2(DR, D),
                  pl.BlockSpec((NCHIP, DP, D // NCHIP), lambda i: (0, 0, 0)), full2(D, D)],
        out_specs=[row(D)] * 6,
        out_shape=[jax.ShapeDtypeStruct((T, D), F32), jax.ShapeDtypeStruct((T, D), BF),
                   jax.ShapeDtypeStruct((T, D), F32), jax.ShapeDtypeStruct((T, D), BF),
                   jax.ShapeDtypeStruct((T, D), BF), jax.ShapeDtypeStruct((T, D), BF)],
        vmem=48, args=[x, ylru, ypool, proj, proj, proj, proj, b_gate, g2, g3, w_lru_up, w_pool_up, w_o],
        stages=stages)


def _fwd_mlp(h2, w_ff1, w_ff2):
    tm = 512
    fk = DF // NCHIP

    def body(h_ref, w1_ref, w2_ref, a1_ref, f_ref):
        h = h_ref[...]
        f = None
        for k in range(NCHIP):
            a1 = jnp.maximum(jnp.dot(h, w1_ref[k], preferred_element_type=F32), 0.0)
            a1_ref[:, k * fk:(k + 1) * fk] = a1.astype(BF)
            part = jnp.dot((a1 * a1).astype(BF), w2_ref[k * fk:(k + 1) * fk, :], preferred_element_type=F32)
            f = part if f is None else f + part
        f_ref[...] = f

    return pl.pallas_call(
        body, name="fwd_mlp", grid=(T // tm,),
        in_specs=[pl.BlockSpec((tm, D), lambda i: (i, 0)),
                  pl.BlockSpec((NCHIP, D, fk), lambda i: (0, 0, 0)),
                  pl.BlockSpec((DF, D), lambda i: (0, 0))],
        out_specs=[pl.BlockSpec((tm, DF), lambda i: (i, 0)), pl.BlockSpec((tm, D), lambda i: (i, 0))],
        out_shape=_hbm_out([jax.ShapeDtypeStruct((T, DF), BF), jax.ShapeDtypeStruct((T, D), F32)]),
        compiler_params=_cp(56),
    )(*_hbm(h2, w_ff1, w_ff2))


def _loss_head(f, x2, target, g4):
    tm = 512

    def body(f_ref, x2_ref, t_ref, g_ref, loss_ref, dy_ref, df_ref, dg_ref):
        first = pl.program_id(0) == 0
        f = f_ref[...]
        g4v = g_ref[...]
        r4 = lax.rsqrt(_mean(f * f) + NORM_EPS)
        fn = f * r4
        e = (x2_ref[...] + fn * g4v) - t_ref[...]
        _acc(loss_ref, jnp.sum(_mean(e * e), axis=0, keepdims=True), first)
        dy = e * (1.0 / D)
        dy_ref[...] = dy
        _acc(dg_ref, _colsum(dy * fn), first)
        dfn = dy * g4v
        df_ref[...] = (r4 * (dfn - fn * _mean(dfn * fn))).astype(BF)

    row = pl.BlockSpec((tm, D), lambda i: (i, 0))
    return pl.pallas_call(
        body, name="loss_head", grid=(T // tm,),
        in_specs=[row, row, row, pl.BlockSpec((1, D), lambda i: (0, 0))],
        out_specs=[pl.BlockSpec((1, 1), lambda i: (0, 0)), row, row, pl.BlockSpec((1, D), lambda i: (0, 0))],
        out_shape=_hbm_out([jax.ShapeDtypeStruct((1, 1), F32), jax.ShapeDtypeStruct((T, D), F32),
                            jax.ShapeDtypeStruct((T, D), BF), jax.ShapeDtypeStruct((1, D), F32)]),
        compiler_params=_cp(48),
    )(*_hbm(f, x2, target, g4))


def _bwd_mlp_x(df, a1, w_ff1, w_ff2):
    tm = 512
    fk = DF // NCHIP

    def body(df_ref, a1_ref, w1_ref, w2_ref, dh_ref, df1_ref):
        df = df_ref[...]
        dh = None
        for k in range(NCHIP):
            cols = slice(k * fk, (k + 1) * fk)
            dact = _mm_nt(df, w2_ref[cols, :])
            df1 = (dact * (2.0 * a1_ref[:, cols].astype(F32))).astype(BF)
            df1_ref[:, cols] = df1
            part = _mm_nt(df1, w1_ref[k])
            dh = part if dh is None else dh + part
        dh_ref[...] = dh

    return pl.pallas_call(
        body, name="bwd_mlp_x", grid=(T // tm,),
        in_specs=[pl.BlockSpec((tm, D), lambda i: (i, 0)),
                  pl.BlockSpec((tm, DF), lambda i: (i, 0)),
                  pl.BlockSpec((NCHIP, D, fk), lambda i: (0, 0, 0)),
                  pl.BlockSpec((DF, D), lambda i: (0, 0))],
        out_specs=[pl.BlockSpec((tm, D), lambda i: (i, 0)), pl.BlockSpec((tm, DF), lambda i: (i, 0))],
        out_shape=_hbm_out([jax.ShapeDtypeStruct((T, D), F32), jax.ShapeDtypeStruct((T, DF), BF)]),
        compiler_params=_cp(56),
    )(*_hbm(df, a1, w_ff1, w_ff2))


def _bwd_mlp_w(df, h2, a1, df1):
    fc = 512
    per = (DF // NCHIP) // fc

    def body(df_ref, h_ref, a1_ref, df1_ref, dw1_ref, dw2_ref):
        a1 = a1_ref[...].astype(F32)
        dw2_ref[...] = _mm_tn((a1 * a1).astype(BF), df_ref[...]).astype(BF)
        dw1_ref[0] = _mm_tn(h_ref[...], df1_ref[...]).astype(BF)

    return pl.pallas_call(
        body, name="bwd_mlp_w", grid=(DF // fc,),
        in_specs=[pl.BlockSpec((T, D), lambda j: (0, 0)),
                  pl.BlockSpec((T, D), lambda j: (0, 0)),
                  pl.BlockSpec((T, fc), lambda j: (0, j)),
                  pl.BlockSpec((T, fc), lambda j: (0, j))],
        out_specs=[pl.BlockSpec((1, D, fc), lambda j: (j // per, 0, j % per)),
                   pl.BlockSpec((fc, D), lambda j: (j, 0))],
        out_shape=_hbm_out([jax.ShapeDtypeStruct((NCHIP, D, DF // NCHIP), BF),
                            jax.ShapeDtypeStruct((DF, D), BF)]),
        compiler_params=_cp(56),
    )(*_hbm(df, h2, a1, df1))


def _bwd_merge(dh2, dy, x2, m, bra, brb, proj, b_gate, g2, g3, w_lru_up, w_pool_up, w_o, stages=()):
    tm = 256
    cpu = D // NCHIP

    def body(dh2_ref, dy_ref, x2_ref, m_ref, bra_ref, brb_ref, p0, p1, p2, p3, bg_ref,
             g2_ref, g3_ref, wl_ref, wp_ref, wo_ref,
             dx_ref, dgt_ref, dyl_ref, dyp_ref, dm_ref, dbra_ref, dbrb_ref, dg2_ref, dg3_ref, dbg_ref):
        first = pl.program_id(0) == 0
        x2 = x2_ref[...]
        r3 = lax.rsqrt(_mean(x2 * x2) + NORM_EPS)
        x2n = x2 * r3
        dh2 = dh2_ref[...]
        t3 = dh2 * g3_ref[...]
        dx2 = dy_ref[...] + r3 * (t3 - x2n * _mean(t3 * x2n))
        dx_ref[...] = dx2
        _acc(dg3_ref, _colsum(dh2 * x2n), first)
        m = m_ref[...]
        r2 = lax.rsqrt(_mean(m * m) + NORM_EPS)
        mn = m * r2
        _acc(dg2_ref, _colsum(dx2 * mn), first)
        dmn = dx2 * g2_ref[...]
        dm = (r2 * (dmn - mn * _mean(dmn * mn))).astype(BF)
        dm_ref[...] = dm
        dmrg = _mm_nt(dm, wo_ref[...])
        bg = bg_ref[...]
        ga = _sigmoid(jnp.concatenate([p0[...], p1[...]], axis=1) + bg[:, :D])
        gb = _sigmoid(jnp.concatenate([p2[...], p3[...]], axis=1) + bg[:, D:])
        dga = dmrg * bra_ref[...].astype(F32) * (ga * (1.0 - ga))
        dgb = dmrg * brb_ref[...].astype(F32) * (gb * (1.0 - gb))
        dgt_ref[:, :D] = dga.astype(BF)
        dgt_ref[:, D:] = dgb.astype(BF)
        _acc(dbg_ref, jnp.concatenate([_colsum(dga), _colsum(dgb)], axis=1), first)
        dbra = (dmrg * ga).astype(BF)
        dbrb = (dmrg * gb).astype(BF)
        dbra_ref[...] = dbra
        dbrb_ref[...] = dbrb
        dyl_ref[...] = _mm_nt(dbra, wl_ref[...])
        dyp = None
        for k in range(NCHIP):
            part = _mm_nt(dbrb[:, k * cpu:(k + 1) * cpu], wp_ref[k])
            dyp = part if dyp is None else dyp + part
        dyp_ref[...] = dyp

    row = lambda w: pl.BlockSpec((tm, w), lambda i: (i, 0))
    full2 = lambda a, b: pl.BlockSpec((a, b), lambda i: (0, 0))
    wp_spec = pl.BlockSpec((NCHIP, DP, cpu), lambda i: (0, 0, 0))
    return _call(
        body, name="bwd_merge", grid=(T // tm,),
        in_specs=[row(D)] * 6 + _gate_specs(tm) +
                 [full2(1, 2 * D), full2(1, D), full2(1, D), full2(DR, D), wp_spec, full2(D, D)],
        out_specs=[row(D), row(2 * D), row(DR), row(DP), row(D), row(D), row(D),
                   full2(1, D), full2(1, D), full2(1, 2 * D)],
        out_shape=[jax.ShapeDtypeStruct((T, D), F32), jax.ShapeDtypeStruct((T, 2 * D), BF),
                   jax.ShapeDtypeStruct((T, DR), F32), jax.ShapeDtypeStruct((T, DP), F32),
                   jax.ShapeDtypeStruct((T, D), BF), jax.ShapeDtypeStruct((T, D), BF),
                   jax.ShapeDtypeStruct((T, D), BF),
                   jax.ShapeDtypeStruct((1, D), F32), jax.ShapeDtypeStruct((1, D), F32),
                   jax.ShapeDtypeStruct((1, 2 * D), F32)],
        vmem=56, args=[dh2, dy, x2, m, bra, brb, proj, proj, proj, proj, b_gate, g2, g3, w_lru_up, w_pool_up, w_o],
        stages=stages)


def _dw_merge(mrg, dm, ylru, dbra, ypool, dbrb):
    nb = NCHIP
    rb, pb, cpu = D // nb, DP // nb, D // NCHIP

    def body(mrg_ref, dm_ref, yl_ref, dbra_ref, yp_ref, dbrb_ref, dwo_ref, dwl_ref, dwp_ref):
        dwo_ref[...] = _mm_tn(mrg_ref[...], dm_ref[...]).astype(BF)
        dwl_ref[...] = _mm_tn(yl_ref[...], dbra_ref[...]).astype(BF)
        dwp = _mm_tn(yp_ref[...], dbrb_ref[...]).astype(BF)
        for k in range(NCHIP):
            dwp_ref[k] = dwp[:, k * cpu:(k + 1) * cpu]

    cols = lambda w: pl.BlockSpec((T, w), lambda r: (0, r))
    whole = pl.BlockSpec((T, D), lambda r: (0, 0))
    return pl.pallas_call(
        body, name="dw_merge", grid=(nb,),
        in_specs=[cols(rb), whole, cols(rb), whole, cols(pb), whole],
        out_specs=[pl.BlockSpec((rb, D), lambda r: (r, 0)), pl.BlockSpec((rb, D), lambda r: (r, 0)),
                   pl.BlockSpec((NCHIP, pb, cpu), lambda r: (0, r, 0))],
        out_shape=_hbm_out([jax.ShapeDtypeStruct((D, D), BF), jax.ShapeDtypeStruct((DR, D), BF),
                            jax.ShapeDtypeStruct((NCHIP, DP, cpu), BF)]),
        compiler_params=_cp(56),
    )(*_hbm(mrg, dm, ylru, dbra, ypool, dbrb))


def _bwd_lru(proj, h, dylru, conv_w, conv_b, wa, ba, wx, bx, lam, stages=()):
    def body(xp_ref, g_ref, h_ref, dy_ref, cw_ref, cb_ref, wa_ref, ba_ref, wx_ref, bx_ref, lam_ref,
             dxp_ref, dg_ref, dcw_ref, dcb_ref, dwa_ref, dba_ref, dwx_ref, dbx_ref, dlam_ref):
        xp = xp_ref[...]
        cw = cw_ref[...]
        lam = lam_ref[...]
        xc, x1, x2, x3 = _conv(xp, cw, cb_ref[...])
        wa, wx = wa_ref[0], wx_ref[0]
        xcb, r, ii, sp, a, mult = _lru_gates(xc, wa, ba_ref[...], wx, bx_ref[...], lam)
        g = g_ref[...]
        gel, dgel = _gelu_parts(g)
        h = h_ref[...]
        dy = dy_ref[...]
        dg_ref[...] = (dy * h * dgel).astype(BF)
        b = dy * gel
        aa = _su(a, 1, 0.0)
        s = 1
        while s < T:
            b = b + aa * _su(b, s, 0.0)
            if 2 * s < T:
                aa = aa * _su(aa, s, 0.0)
            s *= 2
        da = b * _sd(h, 1, 0.0)
        dmult = b * (ii * xc)
        dii = b * (mult * xc)
        dxc = b * (mult * ii)
        dla = da * a - dmult * ((a * a) / mult)
        dr = dla * ((-LRU_C) * sp)
        dsp = _colsum(dla * ((-LRU_C) * r))
        dlam_ref[...] = -dsp / (1.0 + jnp.exp(lam))
        dzr = dr * (r * (1.0 - r))
        dzi = dii * (ii * (1.0 - ii))
        dzrb, dzib = dzr.astype(BF), dzi.astype(BF)
        dxc = dxc + _mm_nt(dzrb, wa) + _mm_nt(dzib, wx)
        dwa_ref[0] = _mm_tn(xcb, dzrb)
        dwx_ref[0] = _mm_tn(xcb, dzib)
        dba_ref[...] = _colsum(dzr)
        dbx_ref[...] = _colsum(dzi)
        dcb_ref[...] = _colsum(dxc)
        dcw_ref[...] = jnp.concatenate([_colsum(dxc * x3), _colsum(dxc * x2), _colsum(dxc * x1),
                                        _colsum(dxc * xp)], axis=0)
        dxp = cw[3:4] * dxc + cw[2:3] * _su(dxc, 1) + cw[1:2] * _su(dxc, 2) + cw[0:1] * _su(dxc, 3)
        dxp_ref[...] = dxp.astype(BF)

    blk = pl.BlockSpec((T, CB), lambda j: (0, j))
    wsp = pl.BlockSpec((1, CB, CB), lambda j: (j, 0, 0))
    return _call(
        body, name="bwd_lru", grid=(NG,),
        in_specs=[blk, pl.BlockSpec((T, CB), lambda j: (0, NG + j)), blk, blk,
                  pl.BlockSpec((4, CB), lambda j: (0, j)), _vec_spec(), wsp, _vec_spec(), wsp, _vec_spec(),
                  _vec_spec()],
        out_specs=[blk, blk, pl.BlockSpec((4, CB), lambda j: (0, j)), _vec_spec(), wsp, _vec_spec(), wsp,
                   _vec_spec(), _vec_spec()],
        out_shape=[jax.ShapeDtypeStruct((T, DR), BF), jax.ShapeDtypeStruct((T, DR), BF),
                   jax.ShapeDtypeStruct((4, DR), F32), jax.ShapeDtypeStruct((1, DR), F32),
                   jax.ShapeDtypeStruct((NG, CB, CB), F32), jax.ShapeDtypeStruct((1, DR), F32),
                   jax.ShapeDtypeStruct((NG, CB, CB), F32), jax.ShapeDtypeStruct((1, DR), F32),
                   jax.ShapeDtypeStruct((1, DR), F32)],
        vmem=56, args=[proj, proj, h, dylru, conv_w, conv_b, wa, ba, wx, bx, lam], stages=stages)


def _bwd_pool(proj, dypool, pool_w, pool_scale):
    def body(xp_ref, dy_ref, pw_ref, sc_ref, dx_ref, dw_ref, dsc_ref):
        for g, w in enumerate(POOL_WINDOWS):
            cols = slice(g * PG, (g + 1) * PG)
            cnt = _pool_cnt(w)
            x = xp_ref[:, cols]
            pb = (_pool_window(x, g + 1, _sd) / cnt - x).astype(BF)
            wg = pw_ref[g]
            dy = dy_ref[:, cols]
            dsc_ref[:, cols] = _colsum(dy * _mm(pb, wg))
            dyp = (dy * sc_ref[:, cols]).astype(BF)
            dw_ref[g] = _mm_tn(pb, dyp)
            dp = _mm_nt(dyp, wg)
            dx_ref[:, cols] = (_pool_window(dp / cnt, g + 1, _su) - dp).astype(BF)

    return pl.pallas_call(
        body, name="bwd_pool", grid=(1,),
        in_specs=[pl.BlockSpec((T, DP), lambda i: (0, 2 * DR // DP)),
                  pl.BlockSpec((T, DP), lambda i: (0, 0)),
                  pl.BlockSpec((4, PG, PG), lambda i: (0, 0, 0)),
                  pl.BlockSpec((1, DP), lambda i: (0, 0))],
        out_specs=[pl.BlockSpec((T, DP), lambda i: (0, 0)),
                   pl.BlockSpec((4, PG, PG), lambda i: (0, 0, 0)),
                   pl.BlockSpec((1, DP), lambda i: (0, 0))],
        out_shape=_hbm_out([jax.ShapeDtypeStruct((T, DP), BF), jax.ShapeDtypeStruct((4, PG, PG), F32),
                            jax.ShapeDtypeStruct((1, DP), F32)]),
        compiler_params=_cp(48),
    )(*_hbm(proj, dypool, pool_w, pool_scale))


def _bwd_inproj(h1, dproj, w_in, stages=()):
    def body(h_ref, dp_ref, w_ref, dw_ref, dh_ref):
        dp = dp_ref[...]
        dw_ref[0] = _mm_tn(h_ref[...], dp).astype(BF)
        _acc(dh_ref, _mm_nt(dp, w_ref[0]), pl.program_id(0) == 0)

    return _call(
        body, name="bwd_inproj", grid=(NCHIP,),
        in_specs=[pl.BlockSpec((T, D), lambda k: (0, 0)),
                  pl.BlockSpec((T, CW_IN), lambda k: (0, k)),
                  pl.BlockSpec((1, D, CW_IN), lambda k: (k, 0, 0))],
        out_specs=[pl.BlockSpec((1, D, CW_IN), lambda k: (k, 0, 0)), pl.BlockSpec((T, D), lambda k: (0, 0))],
        out_shape=[jax.ShapeDtypeStruct((NCHIP, D, CW_IN), BF), jax.ShapeDtypeStruct((T, D), F32)],
        vmem=56, args=[h1, dproj, w_in], stages=stages)


def _bwd_prenorm(x, dh1, dxres, g1, stages=()):
    tm = 512

    def body(x_ref, dh_ref, dr_ref, g_ref, dx_ref, dg_ref):
        xv = x_ref[...]
        r = lax.rsqrt(_mean(xv * xv) + NORM_EPS)
        xn = xv * r
        dh = dh_ref[...]
        t = dh * g_ref[...]
        dx_ref[...] = dr_ref[...] + r * (t - xn * _mean(t * xn))
        _acc(dg_ref, _colsum(dh * xn), pl.program_id(0) == 0)

    row = pl.BlockSpec((tm, D), lambda i: (i, 0))
    vec = pl.BlockSpec((1, D), lambda i: (0, 0))
    return _call(
        body, name="bwd_prenorm", grid=(T // tm,),
        in_specs=[row, row, row, vec], out_specs=[row, vec],
        out_shape=[jax.ShapeDtypeStruct((T, D), F32), jax.ShapeDtypeStruct((1, D), F32)],
        vmem=48, args=[x, dh1, dxres, g1], stages=stages)


def _place():
    x, y, c = lax.axis_index("x"), lax.axis_index("y"), lax.axis_index("c")
    chips = [(1 - x, y), (x, 1 - y), (1 - x, 1 - y)]
    return x, y, c, chips


def _rcopy(src, dst, ssem, rsem, dev):
    return pltpu.make_async_remote_copy(src_ref=src, dst_ref=dst, send_sem=ssem, recv_sem=rsem,
                                        device_id=dev, device_id_type=MESH_ID)


def _sds(a):
    return jax.ShapeDtypeStruct(a.shape, a.dtype)


def _sem2(n, m):
    return [pltpu.SemaphoreType.DMA((n, m)), pltpu.SemaphoreType.DMA((n, m))]


ALL = (0, 1, 1)


def _piece(ref, k, half, part):
    hr = ref.shape[1] // 2
    r0, r1 = hr * part[0] // part[2], hr * part[1] // part[2]
    return ref.at[k, pl.ds(half * hr + r0, r1 - r0), :]


def _gather(fulls, ici=(), d2d=()):
    n = len(fulls)
    ici, d2d = list(ici), list(d2d)

    def copies(outs, sems):
        x, y, c, chips = _place()
        me = 2 * x + y
        sib = (x, y, 1 - c)
        send, recv = [], []
        for q, (i, part) in enumerate(ici):
            for j, chip in enumerate(chips):
                mine, theirs = _piece(outs[i], me, c, part), _piece(outs[i], 2 * chip[0] + chip[1], c, part)
                send.append(_rcopy(mine, mine, sems[0].at[q, j], sems[1].at[q, j], (*chip, c)))
                recv.append(_rcopy(theirs, theirs, sems[0].at[q, j], sems[1].at[q, j], (*chip, c)))
        for q, (i, part) in enumerate(d2d):
            for j, chip in enumerate(chips):
                k = 2 * chip[0] + chip[1]
                got, other = _piece(outs[i], k, c, part), _piece(outs[i], k, 1 - c, part)
                send.append(_rcopy(got, got, sems[2].at[q, j], sems[3].at[q, j], sib))
                recv.append(_rcopy(other, other, sems[2].at[q, j], sems[3].at[q, j], sib))
        return send, recv

    def start(ins, outs, sems):
        for cp in copies(outs, sems)[0]:
            cp.start()

    def finish(ins, outs, sems):
        send, recv = copies(outs, sems)
        for cp in recv:
            cp.wait_recv()
        for cp in send:
            cp.wait_send()

    sems = _sem2(max(len(ici), 1), 3) + _sem2(max(len(d2d), 1), 3)
    return _Stage(fulls, [_sds(f) for f in fulls], {i: i for i in range(n)}, sems, start, finish)


def _gather_first(full, conv_w):
    d2d = _gather([full], d2d=[(0, ALL)])

    def body(full_in, cw_in, full_out, cw_out, s0, r0, s1, r1, cs, cr):
        x, y, c, chips = _place()
        me = 2 * x + y
        conv = [_rcopy(cw_in, cw_out.at[me], cs.at[j], cr.at[j], (*chip, c)) for j, chip in enumerate(chips)]
        for cp in conv:
            cp.start()
        sems = [s0, r0, s1, r1]
        d2d.start(None, [full_out], sems)
        d2d.finish(None, [full_out], sems)
        for j, chip in enumerate(chips):
            _rcopy(cw_in, cw_out.at[2 * chip[0] + chip[1]], cs.at[j], cr.at[j], (*chip, c)).wait_recv()
        for cp in conv:
            cp.wait_send()

    return pl.pallas_call(
        body, name="gather_first",
        in_specs=[ANY, ANY], out_specs=[ANY, ANY],
        out_shape=_hbm_out([full, jax.ShapeDtypeStruct((NCHIP,) + conv_w.shape, conv_w.dtype)]),
        input_output_aliases={0: 0},
        scratch_shapes=_sem2(1, 3) + _sem2(1, 3) + [pltpu.SemaphoreType.DMA((3,)), pltpu.SemaphoreType.DMA((3,))],
        compiler_params=pltpu.CompilerParams(has_side_effects=True),
    )(*_hbm(full, conv_w))


def _comm_only(name, stage):
    def body(*refs):
        ni, no = len(stage.operands), len(stage.out_shape)
        stage.start(refs[:ni], refs[ni:ni + no], refs[ni + no:])
        stage.finish(refs[:ni], refs[ni:ni + no], refs[ni + no:])

    return pl.pallas_call(
        body, name=name, in_specs=[ANY] * len(stage.operands), out_specs=[ANY] * len(stage.out_shape),
        out_shape=_hbm_out(stage.out_shape), input_output_aliases=stage.alias, scratch_shapes=stage.sems,
        compiler_params=pltpu.CompilerParams(has_side_effects=True),
    )(*_hbm(*stage.operands))


def _to_sibling(srcs):
    n = len(srcs)

    def copies(ins, outs, sems):
        x, y, c, _ = _place()
        sib = (x, y, 1 - c)
        return [_rcopy(ins[i].at[:, 1 - c] if srcs[i].ndim == 4 else ins[i], outs[i], sems[0].at[i], sems[1].at[i], sib)
                for i in range(n)]

    def start(ins, outs, sems):
        for cp in copies(ins, outs, sems):
            cp.start()

    def finish(ins, outs, sems):
        for cp in copies(ins, outs, sems):
            cp.wait()

    shapes = [jax.ShapeDtypeStruct((NCHIP,) + s.shape[2:] if s.ndim == 4 else s.shape, s.dtype) for s in srcs]
    return _Stage(srcs, shapes, {}, [pltpu.SemaphoreType.DMA((n,)), pltpu.SemaphoreType.DMA((n,))], start, finish)


def _to_chips(srcs, parts=None, lands=None):
    n = len(srcs)
    parts = [ALL] * n if parts is None else parts
    lands = [None] * n if lands is None else lands
    given = [i for i in range(n) if lands[i] is not None]

    def rows(ref, i):
        hr = srcs[i].shape[1]
        r0, r1 = hr * parts[i][0] // parts[i][2], hr * parts[i][1] // parts[i][2]
        return ref.at[pl.ds(r0, r1 - r0), :]

    def copies(ins, outs, sems):
        x, y, c, chips = _place()
        me = 2 * x + y
        return [_rcopy(rows(ins[i].at[2 * chip[0] + chip[1]] if srcs[i].shape[0] == NCHIP else ins[i].at[c], i),
                       rows(outs[i].at[me], i), sems[0].at[i, j], sems[1].at[i, j], (*chip, c))
                for i in range(n) for j, chip in enumerate(chips)]

    def start(ins, outs, sems):
        for cp in copies(ins, outs, sems):
            cp.start()

    def finish(ins, outs, sems):
        for cp in copies(ins, outs, sems):
            cp.wait()

    shapes = [jax.ShapeDtypeStruct((NCHIP,) + s.shape[1:], s.dtype) for s in srcs]
    alias = {n + q: i for q, i in enumerate(given)}
    return _Stage(list(srcs) + [lands[i] for i in given], shapes, alias, _sem2(n, 3), start, finish)


def _share(pairs):
    n = len(pairs)

    def start(ins, outs, sems):
        x, y, c, _ = _place()
        for i in range(n):
            _rcopy(outs[i].at[c], outs[i].at[c], sems[0].at[i], sems[1].at[i], (x, y, 1 - c)).start()

    def finish(ins, outs, sems):
        x, y, c, _ = _place()
        for i in range(n):
            _rcopy(outs[i].at[c], outs[i].at[c], sems[0].at[i], sems[1].at[i], (x, y, 1 - c)).wait_send()
            _rcopy(outs[i].at[1 - c], outs[i].at[1 - c], sems[0].at[i], sems[1].at[i], (x, y, 1 - c)).wait_recv()

    return _Stage(pairs, [_sds(p) for p in pairs], {i: i for i in range(n)},
                  [pltpu.SemaphoreType.DMA((n,)), pltpu.SemaphoreType.DMA((n,))], start, finish)


def _row_block(rows, cols, itemsize=4, target=MIB):
    br = rows
    while br * cols * itemsize > target and br % 16 == 0:
        br //= 2
    return br


def _cast_place(w, chip_idx, name):
    rows, cols = w.shape
    br = _row_block(rows, cols)

    def body(k_ref, w_ref, o_ref):
        o_ref[0] = w_ref[...].astype(BF)

    return _call(
        body, name=name, grid=(rows // br,), prefetch=chip_idx,
        in_specs=[pl.BlockSpec((br, cols), lambda r, k: (r, 0))],
        out_specs=[pl.BlockSpec((1, br, cols), lambda r, k: (k[0], r, 0))],
        out_shape=[jax.ShapeDtypeStruct((NCHIP, rows, cols), BF)], vmem=32, args=[w])[0][0]


def _cast_place_multi(ws, chip_idx, stages=()):
    br = 128
    nblk = [a.shape[0] // br for a in ws]
    starts = [sum(nblk[:i]) for i in range(len(ws))]

    def body(k_ref, *refs):
        r = pl.program_id(0)
        for i in range(len(ws)):
            @pl.when(jnp.logical_and(r >= starts[i], r < starts[i] + nblk[i]))
            def _(i=i):
                refs[len(ws) + i][0] = refs[i][...].astype(BF)

    def at(i):
        return functools.partial(lambda r, s, nb: jnp.clip(r - s, 0, nb - 1), s=starts[i], nb=nblk[i])

    outs, landed = _call(
        body, name="cast_rest", grid=(sum(nblk),), prefetch=chip_idx,
        in_specs=[pl.BlockSpec((br, a.shape[1]), functools.partial(lambda r, k, f: (f(r), 0), f=at(i)))
                  for i, a in enumerate(ws)],
        out_specs=[pl.BlockSpec((1, br, a.shape[1]), functools.partial(lambda r, k, f: (k[0], f(r), 0), f=at(i)))
                   for i, a in enumerate(ws)],
        out_shape=[jax.ShapeDtypeStruct((NCHIP,) + a.shape, BF) for a in ws], vmem=32, args=list(ws), stages=stages)
    return outs, landed


def _add_sibling(g, land, cidx, name, stages=()):
    _, _, hr, cols = g.shape
    br = _row_block(hr, cols)

    def body(c_ref, g_ref, l_ref, o_ref):
        o_ref[...] = (g_ref[0, 0].astype(F32) + l_ref[0].astype(F32)).astype(BF)[None]

    outs, st = _call(
        body, name=name, grid=(NCHIP, hr // br), prefetch=cidx,
        in_specs=[pl.BlockSpec((1, 1, br, cols), lambda k, r, c: (k, c[0], r, 0)),
                  pl.BlockSpec((1, br, cols), lambda k, r, c: (k, r, 0))],
        out_specs=[pl.BlockSpec((1, br, cols), lambda k, r, c: (k, r, 0))],
        out_shape=[jax.ShapeDtypeStruct((NCHIP, hr, cols), BF)], vmem=32, args=[g, land], stages=stages)
    return outs[0], st


def _add_pair(a, b, name):
    rows, cols = a.shape

    def body(a_ref, b_ref, o_ref):
        o_ref[...] = a_ref[...] + b_ref[...]

    spec = pl.BlockSpec((rows, cols), lambda r: (0, 0))
    return _call(body, name=name, grid=(1,), in_specs=[spec, spec], out_specs=[spec], out_shape=[_sds(a)],
                 vmem=32, args=[a, b])[0][0]


def _add_chips(own, land, idx, name):
    _, hr, cols = land.shape
    br = _row_block(hr, cols)

    def body(s_ref, a_ref, b_ref, c_ref, d_ref, o_ref):
        o_ref[...] = (a_ref[...].astype(F32) + b_ref[...].astype(F32)) + (c_ref[...].astype(F32) +
                                                                           d_ref[...].astype(F32))

    spec = lambda q: pl.BlockSpec((1, br, cols), functools.partial(lambda r, s, q: (s[q], r, 0), q=q))
    return _call(
        body, name=name, grid=(hr // br,), prefetch=idx,
        in_specs=[spec(0), spec(1), spec(2), spec(3)], out_specs=[spec(4)],
        out_shape=[jax.ShapeDtypeStruct((2, hr, cols), F32)], vmem=32, args=[own, land, land, land])[0][0]


def _adamw_math(w, g, m, v):
    mn = ADAM_B1 * m + (1.0 - ADAM_B1) * g
    vn = ADAM_B2 * v + (1.0 - ADAM_B2) * (g * g)
    m_hat = mn / (1.0 - ADAM_B1 ** ADAM_STEP)
    v_hat = vn / (1.0 - ADAM_B2 ** ADAM_STEP)
    return -ADAM_LR * (m_hat / (jnp.sqrt(v_hat) + ADAM_EPS) + ADAM_WD * w), mn, vn


def _adamw(w, g, m, v, name, stages=()):
    rows, cols = w.shape
    br = _row_block(rows, cols)

    def body(w_ref, g_ref, m_ref, v_ref, d_ref, mo_ref, vo_ref):
        d_ref[...], mo_ref[...], vo_ref[...] = _adamw_math(w_ref[...], g_ref[...], m_ref[...], v_ref[...])

    spec = pl.BlockSpec((br, cols), lambda r: (r, 0))
    return _call(body, name=name, grid=(rows // br,), in_specs=[spec] * 4, out_specs=[spec] * 3,
                 out_shape=[_sds(w)] * 3, vmem=32, args=[w, g, m, v], stages=stages)


def _adamw_multi(names, w, g, m, v, stages=()):
    cols = w[names[0]].shape[1]
    br = 128
    nblk = [w[n].shape[0] // br for n in names]
    starts = [sum(nblk[:i]) for i in range(len(names))]

    def body(*refs):
        r = pl.program_id(0)
        for i in range(len(names)):
            w_ref, g_ref, m_ref, v_ref = refs[4 * i:4 * i + 4]
            d_ref, mo_ref, vo_ref = refs[4 * len(names) + 3 * i:4 * len(names) + 3 * i + 3]

            @pl.when(jnp.logical_and(r >= starts[i], r < starts[i] + nblk[i]))
            def _():
                d_ref[...], mo_ref[...], vo_ref[...] = _adamw_math(w_ref[...], g_ref[...], m_ref[...], v_ref[...])

    def spec(i):
        return pl.BlockSpec((br, cols), functools.partial(
            lambda r, s, nb: (jnp.clip(r - s, 0, nb - 1), 0), s=starts[i], nb=nblk[i]))

    outs, landed = _call(
        body, name="adamw_" + "_".join(names), grid=(sum(nblk),),
        in_specs=[spec(i) for i in range(len(names)) for _ in range(4)],
        out_specs=[spec(i) for i in range(len(names)) for _ in range(3)],
        out_shape=[_sds(w[n]) for n in names for _ in range(3)], vmem=48,
        args=[a[n] for n in names for a in (w, g, m, v)], stages=stages)
    return {n: outs[3 * i:3 * i + 3] for i, n in enumerate(names)}, landed


def _to_everyone(v):
    deltas = [(a, b, e) for a in (0, 1) for b in (0, 1) for e in (0, 1)][1:]

    def copies(ins, outs, sems):
        x, y, c, _ = _place()
        me = 4 * x + 2 * y + c
        flip = lambda p, f: 1 - p if f else p
        return [_rcopy(ins[0], outs[0].at[me], sems[0].at[q], sems[1].at[q], (flip(x, a), flip(y, b), flip(c, e)))
                for q, (a, b, e) in enumerate(deltas)]

    def start(ins, outs, sems):
        for cp in copies(ins, outs, sems):
            cp.start()

    def finish(ins, outs, sems):
        for cp in copies(ins, outs, sems):
            cp.wait()

    n = len(deltas)
    return _Stage([v], [jax.ShapeDtypeStruct((2 * NCHIP,) + v.shape, v.dtype)], {},
                  [pltpu.SemaphoreType.DMA((n,)), pltpu.SemaphoreType.DMA((n,))], start, finish)


SMALL_AT = {"norm_mix_pre": (0, 1, D), "norm_mix_post": (1, 1, D), "norm_mlp_pre": (2, 1, D),
            "norm_mlp_post": (3, 1, D), "b_gate": (4, 2, D), "conv_b": (6, 1, D), "lru_b_a": (7, 1, D),
            "lru_b_x": (8, 1, D), "lru_lambda": (9, 1, D), "pool_scale": (10, 1, DP)}
SMALL_SEPARATE = ["conv_w", "lru_w_a", "lru_w_x", "pool_w"]


def _adamw_small(small_sum, first_all, sep_grads, w, m, v):
    packed, sep = list(SMALL_AT), list(SMALL_SEPARATE)
    names = packed + sep

    def body(*refs):
        s_ref, a_ref, refs = refs[0], refs[1], refs[2:]
        g_sep, refs = refs[:len(sep)], refs[len(sep):]
        nn = len(names)
        w_r, m_r, v_r, refs = refs[:nn], refs[nn:2 * nn], refs[2 * nn:3 * nn], refs[3 * nn:]
        g_out, refs = refs[:len(packed)], refs[len(packed):]
        d_o, m_o, v_o = refs[:nn], refs[nn:2 * nn], refs[2 * nn:3 * nn]
        for i, n in enumerate(names):
            if i == 0:
                g = a_ref[0:1, :]
                for q in range(1, 2 * NCHIP):
                    g = g + a_ref[q:q + 1, :]
                g_out[i][...] = g
            elif n in SMALL_AT:
                r0, nr, nc = SMALL_AT[n]
                g = jnp.concatenate([s_ref[r0 + q:r0 + q + 1, :nc] for q in range(nr)], axis=1)
                g_out[i][...] = g
            else:
                g = g_sep[i - len(packed)][...]
            d_o[i][...], m_o[i][...], v_o[i][...] = _adamw_math(w_r[i][...], g, m_r[i][...], v_r[i][...])

    ws = [w[n] for n in names]
    res = pl.pallas_call(
        body, name="adamw_small",
        out_shape=[_sds(w[n]) for n in packed] + [_sds(a) for a in ws] * 3,
        compiler_params=_cp(32),
    )(*_hbm(small_sum, first_all, *sep_grads, *ws, *[m[n] for n in names], *[v[n] for n in names]))
    nn, npk = len(names), len(packed)
    grad = dict(zip(packed, res[:npk]))
    delta = dict(zip(names, res[npk:npk + nn]))
    new_m = dict(zip(names, res[npk + nn:npk + 2 * nn]))
    new_v = dict(zip(names, res[npk + 2 * nn:]))
    return grad, delta, new_m, new_v


W_NAMES = ["norm_mix_pre", "norm_mix_post", "norm_mlp_pre", "norm_mlp_post", "w_in", "b_gate", "conv_w", "conv_b",
           "lru_w_a", "lru_b_a", "lru_w_x", "lru_b_x", "lru_lambda", "pool_w", "pool_scale", "w_lru_up",
           "w_pool_up", "w_o", "w_ff1", "w_ff2"]
BIG = ["w_in", "w_lru_up", "w_pool_up", "w_o", "w_ff1", "w_ff2"]


def _block_diag(w):
    hd = w.shape[-1]
    per = CB // hd
    w4 = w.reshape(NG, per, hd, hd)
    eye = jnp.eye(per, dtype=w.dtype)
    return jnp.einsum("gpij,pq->gpiqj", w4, eye).reshape(NG, CB, CB)


def _block_diag_extract(d, hd):
    per = CB // hd
    d5 = d.reshape(NG, per, hd, per, hd)
    return jnp.stack([d5[:, p, :, p, :] for p in range(per)], axis=1).reshape(NG * per, hd, hd)


def _halves(g):
    return g.reshape(NCHIP, 2, g.size // (g.shape[-1] * 2 * NCHIP), g.shape[-1])


def kernel(x, norm_mix_pre, norm_mix_post, norm_mlp_pre, norm_mlp_post, w_in, b_gate, conv_w, conv_b, lru_w_a, lru_b_a, lru_w_x, lru_b_x, lru_lambda, pool_w, pool_scale, w_lru_up, w_pool_up, w_o, w_ff1, w_ff2, loss_target, m_norm_mix_pre, m_norm_mix_post, m_norm_mlp_pre, m_norm_mlp_post, m_w_in, m_b_gate, m_conv_w, m_conv_b, m_lru_w_a, m_lru_b_a, m_lru_w_x, m_lru_b_x, m_lru_lambda, m_pool_w, m_pool_scale, m_w_lru_up, m_w_pool_up, m_w_o, m_w_ff1, m_w_ff2, v_norm_mix_pre, v_norm_mix_post, v_norm_mlp_pre, v_norm_mlp_post, v_w_in, v_b_gate, v_conv_w, v_conv_b, v_lru_w_a, v_lru_b_a, v_lru_w_x, v_lru_b_x, v_lru_lambda, v_pool_w, v_pool_scale, v_w_lru_up, v_w_pool_up, v_w_o, v_w_ff1, v_w_ff2):
    args = dict(locals())
    two_d = lambda a: a.reshape(-1, a.shape[-1])
    w = {n: two_d(args[n]) for n in W_NAMES}
    mom = {n: two_d(args["m_" + n]) for n in W_NAMES}
    var = {n: two_d(args["v_" + n]) for n in W_NAMES}
    i32 = lambda val: jnp.asarray(val, jnp.int32)
    chip = i32(2 * lax.axis_index("x") + lax.axis_index("y"))
    core = i32(lax.axis_index("c"))
    cidx = core.reshape(1)
    zero = i32(0)
    hd = lru_w_a.shape[-1]
    xs, target = x[0], loss_target[0]
    g1, g2, g3, g4 = norm_mix_pre, norm_mix_post, norm_mlp_pre, norm_mlp_post

    full = {"w_in": _cast_place(w["w_in"], chip.reshape(1), "cast_w_in")}
    casts, ((full["w_in"],),) = _cast_place_multi([w[n] for n in BIG[1:]], chip.reshape(1),
                                                 stages=[_gather([full["w_in"]], ici=[(0, ALL)])])
    full.update(zip(BIG[1:], casts))
    wa = _block_diag(lru_w_a[0]).astype(BF)
    wx = _block_diag(lru_w_x[0]).astype(BF)
    pw = pool_w[0].astype(BF)

    full["w_in"], conv_all = _gather_first(full["w_in"], w["conv_w"])
    conv_all = lax.dynamic_update_slice(conv_all, w["conv_w"][None], (chip, zero, zero))
    conv_full = jnp.transpose(conv_all, (1, 0, 2)).reshape(4, DR)
    mix = ["w_lru_up", "w_pool_up", "w_o"]
    ff1_a, ff1_b, ff2_a, ff2_b = (0, 3, 8), (3, 8, 8), (0, 1, 4), (1, 4, 4)
    (proj, h1), (got,) = _fwd_inproj(xs, g1, full["w_in"], stages=[_gather(
        [full[n] for n in mix] + [full["w_ff1"]], ici=[(0, ALL), (1, ALL), (2, ALL), (3, ff1_a)])])
    (ylru, hs), (got,) = _fwd_lru(proj, conv_full, conv_b, wa, lru_b_a, wx, lru_b_x, lru_lambda, stages=[_gather(
        got + [full["w_ff2"]], d2d=[(0, ALL), (1, ALL), (2, ALL), (3, ff1_a)], ici=[(3, ff1_b), (4, ff2_a)])])
    w_lru_up_f, w_pool_up_f, w_o_f = got[0].reshape(DR, D), got[1], got[2].reshape(D, D)
    ypool = _fwd_pool(proj, pw, pool_scale)
    (x2, h2, m, mrg, bra, brb), ((ff1, ff2),) = _fwd_merge(
        xs, ylru, ypool, proj, b_gate, g2, g3, w_lru_up_f, w_pool_up_f, w_o_f,
        stages=[_gather(got[3:], d2d=[(0, ff1_b), (1, ff2_a)], ici=[(1, ff2_b)])])
    ff2 = _comm_only("gather_last", _gather([ff2], d2d=[(0, ff2_b)]))[0].reshape(DF, D)
    a1, f = _fwd_mlp(h2, ff1, ff2)
    lossp, dy, df, dg4 = _loss_head(f, x2, target, g4)

    idx_big = jnp.stack([chip, (chip + 1) % NCHIP, (chip + 2) % NCHIP, (chip + 3) % NCHIP, core])
    dh2, df1 = _bwd_mlp_x(df, a1, ff1, ff2)
    dw_ff1, dw_ff2 = _bwd_mlp_w(df, h2, a1, df1)
    g_ff = [_halves(dw_ff1), _halves(dw_ff2)]
    (dxres, dgates, dylru, dypool, dm, dbra, dbrb, dg2, dg3, dbg), (l_ff,) = _bwd_merge(
        dh2, dy, x2, m, bra, brb, proj, b_gate, g2, g3, w_lru_up_f, w_pool_up_f, w_o_f, stages=[_to_sibling(g_ff)])
    dw_o, dw_lru_up, dw_pool_up = _dw_merge(mrg, dm, ylru, dbra, ypool, dbrb)
    p_ff = [_add_sibling(g, l, cidx, "add_sibling_" + n)[0] for g, l, n in zip(g_ff, l_ff, ["w_ff1", "w_ff2"])]
    g_mix = [_halves(dw_lru_up), _halves(dw_pool_up), _halves(dw_o)]
    ff2_head, ff2_tail = (0, 5, 8), (5, 8, 8)
    (dxp, dgl, dcw, dcb, dwa, dba, dwx, dbx, dlam), ((c_ff1, c_ff2), l_mix) = _bwd_lru(
        proj, hs, dylru, conv_full, conv_b, wa, lru_b_a, wx, lru_b_x, lru_lambda,
        stages=[_to_chips(p_ff, parts=[ALL, ff2_head]), _to_sibling(g_mix)])
    p_mix = [_add_sibling(g, l, cidx, "add_sibling_" + n)[0] for g, l, n in zip(g_mix, l_mix, mix)]
    dxpool, dpw, dsc = _bwd_pool(proj, dypool, pw, pool_scale)
    dproj = jnp.concatenate([dxp, dgl, dxpool, dgates], axis=1)
    small = jnp.concatenate([
        jnp.zeros((1, D), F32), dg2, dg3, dg4, dbg.reshape(2, D), dcb, dba, dbx, dlam,
        jnp.pad(dsc, ((0, 0), (0, D - DP))), jnp.pad(lossp, ((0, 0), (0, D - 1))), dcw,
        _block_diag_extract(dwa, hd).reshape(-1, D), _block_diag_extract(dwx, hd).reshape(-1, D),
        dpw.reshape(-1, D)], axis=0)
    (dw_in, dh1), (c_rest, (l_small,)) = _bwd_inproj(h1, dproj, full["w_in"], stages=[
        _to_chips(p_ff[1:] + p_mix, parts=[ff2_tail, ALL, ALL, ALL], lands=[c_ff2, None, None, None]),
        _to_sibling([small])])
    small2 = _add_pair(small, l_small, "add_sibling_small").reshape(2, SMALL_ROWS // 2, D)
    done = ["w_ff1", "w_ff2"] + mix
    pairs = [_add_chips(p, l, idx_big, "add_chips_" + n) for p, l, n in zip(p_ff + p_mix, [c_ff1] + c_rest, done)]
    g_in = _halves(dw_in)
    (grad_x, dg1), ((l_in,), pairs, (c_small,)) = _bwd_prenorm(
        xs, dh1, dxres, g1, stages=[_to_sibling([g_in]), _share(pairs), _to_chips([small2])])
    own_small = lax.dynamic_index_in_dim(small2, core, 0, keepdims=True)
    c_small = lax.dynamic_update_slice(c_small, own_small, (chip, zero, zero))
    pair_small = _add_chips(c_small, c_small, jnp.stack([zero, zero + 1, zero + 2, zero + 3, core]), "add_chips_small")
    p_in, ((pair_small,), (dg1_all,)) = _add_sibling(
        g_in, l_in, cidx, "add_sibling_w_in", stages=[_share([pair_small]), _to_everyone(dg1)])
    dg1_all = lax.dynamic_update_slice(dg1_all, dg1[None], (2 * chip + core, zero, zero)).reshape(2 * NCHIP, D)

    grads, delta, new_m, new_v = {}, {}, {}, {}
    for n, p in zip(done, pairs):
        grads[n] = p.reshape(-1, p.shape[-1])

    def update(n, stages=()):
        (delta[n], new_m[n], new_v[n]), landed = _adamw(w[n], grads[n], mom[n], var[n], "adamw_" + n, stages=stages)
        return landed

    updated, ((c_in,),) = _adamw_multi(["w_ff1", "w_ff2", "w_o", "w_lru_up"], w, grads, mom, var,
                                      stages=[_to_chips([p_in])])
    for n, (d, mo, vo) in updated.items():
        delta[n], new_m[n], new_v[n] = d, mo, vo
    pair_in = _add_chips(p_in, c_in, idx_big, "add_chips_w_in")
    ((pair_in,),) = update("w_pool_up", stages=[_share([pair_in])])
    grads["w_in"] = pair_in.reshape(-1, pair_in.shape[-1])
    update("w_in")
    small_sum = pair_small.reshape(SMALL_ROWS, D)
    loss = 0.5 * small_sum[LOSS_ROW, 0]
    ccols = DR // NCHIP
    sep = [lax.dynamic_slice(small_sum[12:16], (zero, chip * ccols), (4, ccols)),
           small_sum[16:80].reshape(-1, hd), small_sum[80:144].reshape(-1, hd), small_sum[144:208].reshape(-1, PG)]
    g_s, d_s, m_s, v_s = _adamw_small(small_sum, dg1_all, sep, w, mom, var)
    grads.update(g_s)
    grads.update(dict(zip(SMALL_SEPARATE, sep)))
    delta.update(d_s)
    new_m.update(m_s)
    new_v.update(v_s)

    out = lambda d: [d[n].reshape(args[n].shape) for n in W_NAMES]
    return (loss, grad_x[None], *out(grads), *out(delta), *out(new_m), *out(new_v))
```

```python
import functools
import math

import jax
import jax.numpy as jnp
from jax import lax
from jax.experimental import pallas as pl
from jax.experimental.pallas import tpu as pltpu

F32 = jnp.float32
BF = jnp.bfloat16

T = 2048
D = 1024
DR = 1024
DP = 512
DF = 4096
DIN = 4608
NCHIP = 4
CW_IN = DIN // NCHIP
LANE = 128
CB = 128
NG = DR // CB
PG = 128
POOL_WINDOWS = (2, 4, 8, 16)
NORM_EPS = 1e-6
LRU_C = 8.0
GELU_C = math.sqrt(2.0 / math.pi)
ADAM_LR = 0.001
ADAM_B1 = 0.9
ADAM_B2 = 0.999
ADAM_EPS = 1e-08
ADAM_WD = 0.01
ADAM_STEP = 10
MESH_ID = pl.DeviceIdType.MESH
ANY = pl.BlockSpec(memory_space=pl.ANY)
SMALL_ROWS = 208
LOSS_ROW = 11
MIB = 1 << 20


def _cp(vmem_mib=None):
    if vmem_mib is None:
        return pltpu.CompilerParams()
    return pltpu.CompilerParams(vmem_limit_bytes=vmem_mib * MIB)


def _hbm(*arrays):
    return [pltpu.with_memory_space_constraint(a, pltpu.HBM) for a in arrays]


def _hbm_out(shapes):
    return [pltpu.HBM(s.shape, s.dtype) for s in shapes]


class _Stage:
    def __init__(self, operands, out_shape, alias, sems, start, finish):
        self.operands, self.out_shape, self.alias, self.sems = list(operands), list(out_shape), dict(alias), list(sems)
        self.start, self.finish = start, finish


def _call(body, *, name, grid, in_specs, out_specs, out_shape, args, vmem=None, stages=(), prefetch=None,
          scratch=()):
    nin, nout = len(in_specs), len(out_specs)
    npre = 0 if prefetch is None else 1
    st_args, st_shapes, st_sems, aliases = [], [], list(scratch), {}
    for st in stages:
        for k, v in st.alias.items():
            aliases[npre + nin + len(st_args) + k] = nout + len(st_shapes) + v
        st_args += st.operands
        st_shapes += st.out_shape
        st_sems += st.sems

    def wrapped(*refs):
        pre, refs = refs[:npre], refs[npre:]
        ins, pos = refs[:nin], nin
        st_ins = []
        for st in stages:
            st_ins.append(refs[pos:pos + len(st.operands)])
            pos += len(st.operands)
        outs, pos = refs[pos:pos + nout], pos + nout
        st_outs = []
        for st in stages:
            st_outs.append(refs[pos:pos + len(st.out_shape)])
            pos += len(st.out_shape)
        work, pos = refs[pos:pos + len(scratch)], pos + len(scratch)
        sems = []
        for st in stages:
            sems.append(refs[pos:pos + len(st.sems)])
            pos += len(st.sems)
        if stages:
            first = functools.reduce(jnp.logical_and, [pl.program_id(a) == 0 for a in range(len(grid))])

            @pl.when(first)
            def _():
                for st, a, b, s in zip(stages, st_ins, st_outs, sems):
                    st.start(a, b, s)

        body(*pre, *ins, *outs, *work)
        if stages:
            last = functools.reduce(jnp.logical_and, [pl.program_id(a) == g - 1 for a, g in enumerate(grid)])

            @pl.when(last)
            def _():
                for st, a, b, s in zip(stages, st_ins, st_outs, sems):
                    st.finish(a, b, s)

    all_in = list(in_specs) + [ANY] * len(st_args)
    all_out = list(out_specs) + [ANY] * len(st_shapes)
    kw = dict(has_side_effects=True) if stages else {}
    if vmem is not None:
        kw["vmem_limit_bytes"] = vmem * MIB
    if prefetch is None:
        gkw = dict(grid=grid, in_specs=all_in, out_specs=all_out, scratch_shapes=st_sems)
    else:
        gkw = dict(grid_spec=pltpu.PrefetchScalarGridSpec(
            num_scalar_prefetch=1, grid=grid, in_specs=all_in, out_specs=all_out, scratch_shapes=st_sems))
    res = pl.pallas_call(
        wrapped, name=name, out_shape=_hbm_out(list(out_shape) + st_shapes), input_output_aliases=aliases,
        compiler_params=pltpu.CompilerParams(**kw), **gkw,
    )(*([prefetch] if npre else []), *_hbm(*args, *st_args))
    outs, rest, st_res = list(res[:nout]), list(res[nout:]), []
    for st in stages:
        st_res.append(rest[:len(st.out_shape)])
        rest = rest[len(st.out_shape):]
    return outs, st_res


def _mm(a, b):
    return jnp.dot(a.astype(BF), b.astype(BF), preferred_element_type=F32)


def _mm_nt(a, b):
    return lax.dot_general(a.astype(BF), b.astype(BF), (((1,), (1,)), ((), ())),
                           preferred_element_type=F32)


def _mm_tn(a, b):
    return lax.dot_general(a.astype(BF), b.astype(BF), (((0,), (0,)), ((), ())),
                           preferred_element_type=F32)


def _rows(v):
    return lax.broadcasted_iota(jnp.int32, v.shape, 0)


def _sd(v, s, fill=0.0):
    return jnp.where(_rows(v) >= s, pltpu.roll(v, s, axis=0), fill)


def _su(v, s, fill=0.0):
    n = v.shape[0]
    return jnp.where(_rows(v) < n - s, pltpu.roll(v, n - s, axis=0), fill)


def _sigmoid(z):
    return 1.0 / (1.0 + jnp.exp(-z))


def _softplus(z):
    e = jnp.exp(-jnp.abs(z))
    u = 1.0 + e
    d = u - 1.0
    log1p = jnp.where(d == 0.0, e, jnp.log(u) * (e / jnp.where(d == 0.0, 1.0, d)))
    return jnp.maximum(z, 0.0) + log1p


def _mean(v):
    return jnp.mean(v, axis=-1, keepdims=True)


def _colsum(v):
    return jnp.sum(v, axis=0, keepdims=True)


def _acc(ref, val, first):
    @pl.when(first)
    def _():
        ref[...] = val

    @pl.when(jnp.logical_not(first))
    def _():
        ref[...] += val


def _conv(xp, cw, cb):
    x1, x2, x3 = _sd(xp, 1), _sd(xp, 2), _sd(xp, 3)
    xc = cb + cw[0:1] * x3 + cw[1:2] * x2 + cw[2:3] * x1 + cw[3:4] * xp
    return xc, x1, x2, x3


def _lru_gates(xc, wa, ba, wx, bx, lam):
    xcb = xc.astype(BF)
    r = _sigmoid(_mm(xcb, wa) + ba)
    ii = _sigmoid(_mm(xcb, wx) + bx)
    sp = _softplus(-lam)
    la = (-LRU_C) * r * sp
    a = jnp.exp(la)
    mult = jnp.sqrt(-jnp.tanh(la) * (a * a + 1.0))
    return xcb, r, ii, sp, a, mult


def _gelu_parts(g):
    th = jnp.tanh(GELU_C * (g + 0.044715 * (g * g * g)))
    gel = 0.5 * g * (1.0 + th)
    dgel = 0.5 * (1.0 + th) + 0.5 * g * (1.0 - th * th) * (GELU_C * (1.0 + 3.0 * 0.044715 * (g * g)))
    return gel, dgel


def _pool_window(x, steps, shift):
    s, sh = x, 1
    for _ in range(steps):
        s = s + shift(s, sh)
        sh *= 2
    return s


def _fwd_inproj(x, g1, w_in, stages=()):
    tm = 512

    def body(x_ref, g_ref, w_ref, proj_ref, h_ref):
        @pl.when(pl.program_id(1) == 0)
        def _():
            xv = x_ref[...]
            r = lax.rsqrt(_mean(xv * xv) + NORM_EPS)
            h_ref[...] = ((xv * r) * g_ref[...]).astype(BF)

        proj_ref[...] = jnp.dot(h_ref[...], w_ref[0], preferred_element_type=F32)

    return _call(
        body, name="fwd_inproj", grid=(T // tm, NCHIP),
        in_specs=[pl.BlockSpec((tm, D), lambda i, k: (i, 0)),
                  pl.BlockSpec((1, D), lambda i, k: (0, 0)),
                  pl.BlockSpec((1, D, CW_IN), lambda i, k: (k, 0, 0))],
        out_specs=[pl.BlockSpec((tm, CW_IN), lambda i, k: (i, k)),
                   pl.BlockSpec((tm, D), lambda i, k: (i, 0))],
        out_shape=[jax.ShapeDtypeStruct((T, DIN), F32), jax.ShapeDtypeStruct((T, D), BF)],
        vmem=40, args=[x, g1, w_in], stages=stages)


def _vec_spec():
    return pl.BlockSpec((1, CB), lambda j: (0, j))


def _fwd_lru(proj, conv_w, conv_b, wa, ba, wx, bx, lam, stages=()):
    def body(xp_ref, g_ref, cw_ref, cb_ref, wa_ref, ba_ref, wx_ref, bx_ref, lam_ref, y_ref, h_ref):
        xc, _, _, _ = _conv(xp_ref[...], cw_ref[...], cb_ref[...])
        _, _, ii, _, a, mult = _lru_gates(xc, wa_ref[0], ba_ref[...], wx_ref[0], bx_ref[...], lam_ref[...])
        b = mult * (ii * xc)
        s = 1
        while s < T:
            b = b + a * _sd(b, s, 0.0)
            if 2 * s < T:
                a = a * _sd(a, s, 1.0)
            s *= 2
        h_ref[...] = b
        gel, _ = _gelu_parts(g_ref[...])
        y_ref[...] = (b * gel).astype(BF)

    return _call(
        body, name="fwd_lru", grid=(NG,),
        in_specs=[pl.BlockSpec((T, CB), lambda j: (0, j)),
                  pl.BlockSpec((T, CB), lambda j: (0, NG + j)),
                  pl.BlockSpec((4, CB), lambda j: (0, j)),
                  _vec_spec(),
                  pl.BlockSpec((1, CB, CB), lambda j: (j, 0, 0)), _vec_spec(),
                  pl.BlockSpec((1, CB, CB), lambda j: (j, 0, 0)), _vec_spec(),
                  _vec_spec()],
        out_specs=[pl.BlockSpec((T, CB), lambda j: (0, j)), pl.BlockSpec((T, CB), lambda j: (0, j))],
        out_shape=[jax.ShapeDtypeStruct((T, DR), BF), jax.ShapeDtypeStruct((T, DR), F32)],
        vmem=48, args=[proj, proj, conv_w, conv_b, wa, ba, wx, bx, lam], stages=stages)


def _pool_cnt(w):
    t = lax.broadcasted_iota(jnp.int32, (T, 1), 0)
    return jnp.minimum(t + 1, w).astype(F32)


def _fwd_pool(proj, pool_w, pool_scale):
    def body(xp_ref, pw_ref, sc_ref, y_ref):
        for g, w in enumerate(POOL_WINDOWS):
            cols = slice(g * PG, (g + 1) * PG)
            x = xp_ref[:, cols]
            p = _pool_window(x, g + 1, _sd) / _pool_cnt(w) - x
            y_ref[:, cols] = (_mm(p, pw_ref[g]) * sc_ref[:, cols]).astype(BF)

    return pl.pallas_call(
        body, name="fwd_pool", grid=(1,),
        in_specs=[pl.BlockSpec((T, DP), lambda i: (0, 2 * DR // DP)),
                  pl.BlockSpec((4, PG, PG), lambda i: (0, 0, 0)),
                  pl.BlockSpec((1, DP), lambda i: (0, 0))],
        out_specs=pl.BlockSpec((T, DP), lambda i: (0, 0)),
        out_shape=pltpu.HBM((T, DP), BF),
        compiler_params=_cp(48),
    )(*_hbm(proj, pool_w, pool_scale))


GATE_BLK = 512
GATE_BLK0 = (2 * DR + DP) // GATE_BLK


def _gate_specs(tm):
    return [pl.BlockSpec((tm, GATE_BLK), functools.partial(lambda i, q: (i, GATE_BLK0 + q), q=q))
            for q in range(4)]


def _fwd_merge(x, ylru, ypool, proj, b_gate, g2, g3, w_lru_up, w_pool_up, w_o, stages=()):
    tm = 512

    def body(x_ref, yl_ref, yp_ref, p0, p1, p2, p3, bg_ref, g2_ref, g3_ref, wl_ref, wp_ref, wo_ref,
             x2_ref, h2_ref, m_ref, mrg_ref, bra_ref, brb_ref):
        bra = jnp.dot(yl_ref[...], wl_ref[...], preferred_element_type=F32)
        yp = yp_ref[...]
        brb = jnp.concatenate([jnp.dot(yp, wp_ref[k], preferred_element_type=F32) for k in range(NCHIP)], axis=1)
        bg = bg_ref[...]
        ga = _sigmoid(jnp.concatenate([p0[...], p1[...]], axis=1) + bg[:, :D])
        gb = _sigmoid(jnp.concatenate([p2[...], p3[...]], axis=1) + bg[:, D:])
        mrg = (ga * bra + gb * brb).astype(BF)
        m = jnp.dot(mrg, wo_ref[...], preferred_element_type=F32)
        r2 = lax.rsqrt(_mean(m * m) + NORM_EPS)
        x2 = x_ref[...] + (m * r2) * g2_ref[...]
        r3 = lax.rsqrt(_mean(x2 * x2) + NORM_EPS)
        x2_ref[...] = x2
        h2_ref[...] = ((x2 * r3) * g3_ref[...]).astype(BF)
        m_ref[...] = m
        mrg_ref[...] = mrg
        bra_ref[...] = bra.astype(BF)
        brb_ref[...] = brb.astype(BF)

    row = lambda w: pl.BlockSpec((tm, w), lambda i: (i, 0))
    full2 = lambda a, b: pl.BlockSpec((a, b), lambda i: (0, 0))
    return _call(
        body, name="fwd_merge", grid=(T // tm,),
        in_specs=[row(D), row(DR), row(DP)] + _gate_specs(tm) +
                 [full2(1, 2 * D), full2(1, D), full2(1, D), full2(DR, D),
                  pl.BlockSpec((NCHIP, DP, D // NCHIP), lambda i: (0, 0, 0)), full2(D, D)],
        out_specs=[row(D)] * 6,
        out_shape=[jax.ShapeDtypeStruct((T, D), F32), jax.ShapeDtypeStruct((T, D), BF),
                   jax.ShapeDtypeStruct((T, D), F32), jax.ShapeDtypeStruct((T, D), BF),
                   jax.ShapeDtypeStruct((T, D), BF), jax.ShapeDtypeStruct((T, D), BF)],
        vmem=48, args=[x, ylru, ypool, proj, proj, proj, proj, b_gate, g2, g3, w_lru_up, w_pool_up, w_o],
        stages=stages)


def _fwd_mlp(h2, w_ff1, w_ff2):
    tm = 512
    fk = DF // NCHIP

    def body(h_ref, w1_ref, w2_ref, a1_ref, f_ref):
        h = h_ref[...]
        f = None
        for k in range(NCHIP):
            a1 = jnp.maximum(jnp.dot(h, w1_ref[k], preferred_element_type=F32), 0.0)
            a1_ref[:, k * fk:(k + 1) * fk] = a1.astype(BF)
            part = jnp.dot((a1 * a1).astype(BF), w2_ref[k * fk:(k + 1) * fk, :], preferred_element_type=F32)
            f = part if f is None else f + part
        f_ref[...] = f

    return pl.pallas_call(
        body, name="fwd_mlp", grid=(T // tm,),
        in_specs=[pl.BlockSpec((tm, D), lambda i: (i, 0)),
                  pl.BlockSpec((NCHIP, D, fk), lambda i: (0, 0, 0)),
                  pl.BlockSpec((DF, D), lambda i: (0, 0))],
        out_specs=[pl.BlockSpec((tm, DF), lambda i: (i, 0)), pl.BlockSpec((tm, D), lambda i: (i, 0))],
        out_shape=_hbm_out([jax.ShapeDtypeStruct((T, DF), BF), jax.ShapeDtypeStruct((T, D), F32)]),
        compiler_params=_cp(56),
    )(*_hbm(h2, w_ff1, w_ff2))


def _loss_head(f, x2, target, g4):
    tm = 512

    def body(f_ref, x2_ref, t_ref, g_ref, loss_ref, dy_ref, df_ref, dg_ref):
        first = pl.program_id(0) == 0
        f = f_ref[...]
        g4v = g_ref[...]
        r4 = lax.rsqrt(_mean(f * f) + NORM_EPS)
        fn = f * r4
        e = (x2_ref[...] + fn * g4v) - t_ref[...]
        _acc(loss_ref, jnp.sum(_mean(e * e), axis=0, keepdims=True), first)
        dy = e * (1.0 / D)
        dy_ref[...] = dy
        _acc(dg_ref, _colsum(dy * fn), first)
        dfn = dy * g4v
        df_ref[...] = (r4 * (dfn - fn * _mean(dfn * fn))).astype(BF)

    row = pl.BlockSpec((tm, D), lambda i: (i, 0))
    return pl.pallas_call(
        body, name="loss_head", grid=(T // tm,),
        in_specs=[row, row, row, pl.BlockSpec((1, D), lambda i: (0, 0))],
        out_specs=[pl.BlockSpec((1, 1), lambda i: (0, 0)), row, row, pl.BlockSpec((1, D), lambda i: (0, 0))],
        out_shape=_hbm_out([jax.ShapeDtypeStruct((1, 1), F32), jax.ShapeDtypeStruct((T, D), F32),
                            jax.ShapeDtypeStruct((T, D), BF), jax.ShapeDtypeStruct((1, D), F32)]),
        compiler_params=_cp(48),
    )(*_hbm(f, x2, target, g4))


def _bwd_mlp_x(df, a1, w_ff1, w_ff2):
    tm = 512
    fk = DF // NCHIP

    def body(df_ref, a1_ref, w1_ref, w2_ref, dh_ref, df1_ref):
        df = df_ref[...]
        dh = None
        for k in range(NCHIP):
            cols = slice(k * fk, (k + 1) * fk)
            dact = _mm_nt(df, w2_ref[cols, :])
            df1 = (dact * (2.0 * a1_ref[:, cols].astype(F32))).astype(BF)
            df1_ref[:, cols] = df1
            part = _mm_nt(df1, w1_ref[k])
            dh = part if dh is None else dh + part
        dh_ref[...] = dh

    return pl.pallas_call(
        body, name="bwd_mlp_x", grid=(T // tm,),
        in_specs=[pl.BlockSpec((tm, D), lambda i: (i, 0)),
                  pl.BlockSpec((tm, DF), lambda i: (i, 0)),
                  pl.BlockSpec((NCHIP, D, fk), lambda i: (0, 0, 0)),
                  pl.BlockSpec((DF, D), lambda i: (0, 0))],
        out_specs=[pl.BlockSpec((tm, D), lambda i: (i, 0)), pl.BlockSpec((tm, DF), lambda i: (i, 0))],
        out_shape=_hbm_out([jax.ShapeDtypeStruct((T, D), F32), jax.ShapeDtypeStruct((T, DF), BF)]),
        compiler_params=_cp(56),
    )(*_hbm(df, a1, w_ff1, w_ff2))


def _bwd_mlp_w(df, h2, a1, df1):
    fc = 512
    per = (DF // NCHIP) // fc

    def body(df_ref, h_ref, a1_ref, df1_ref, dw1_ref, dw2_ref):
        a1 = a1_ref[...].astype(F32)
        dw2_ref[...] = _mm_tn((a1 * a1).astype(BF), df_ref[...]).astype(BF)
        dw1_ref[0] = _mm_tn(h_ref[...], df1_ref[...]).astype(BF)

    return pl.pallas_call(
        body, name="bwd_mlp_w", grid=(DF // fc,),
        in_specs=[pl.BlockSpec((T, D), lambda j: (0, 0)),
                  pl.BlockSpec((T, D), lambda j: (0, 0)),
                  pl.BlockSpec((T, fc), lambda j: (0, j)),
                  pl.BlockSpec((T, fc), lambda j: (0, j))],
        out_specs=[pl.BlockSpec((1, D, fc), lambda j: (j // per, 0, j % per)),
                   pl.BlockSpec((fc, D), lambda j: (j, 0))],
        out_shape=_hbm_out([jax.ShapeDtypeStruct((NCHIP, D, DF // NCHIP), BF),
                            jax.ShapeDtypeStruct((DF, D), BF)]),
        compiler_params=_cp(56),
    )(*_hbm(df, h2, a1, df1))


def _bwd_merge(dh2, dy, x2, m, bra, brb, proj, b_gate, g2, g3, w_lru_up, w_pool_up, w_o, stages=()):
    tm = 256
    cpu = D // NCHIP

    def body(dh2_ref, dy_ref, x2_ref, m_ref, bra_ref, brb_ref, p0, p1, p2, p3, bg_ref,
             g2_ref, g3_ref, wl_ref, wp_ref, wo_ref,
             dx_ref, dgt_ref, dyl_ref, dyp_ref, dm_ref, dbra_ref, dbrb_ref, dg2_ref, dg3_ref, dbg_ref):
        first = pl.program_id(0) == 0
        x2 = x2_ref[...]
        r3 = lax.rsqrt(_mean(x2 * x2) + NORM_EPS)
        x2n = x2 * r3
        dh2 = dh2_ref[...]
        t3 = dh2 * g3_ref[...]
        dx2 = dy_ref[...] + r3 * (t3 - x2n * _mean(t3 * x2n))
        dx_ref[...] = dx2
        _acc(dg3_ref, _colsum(dh2 * x2n), first)
        m = m_ref[...]
        r2 = lax.rsqrt(_mean(m * m) + NORM_EPS)
        mn = m * r2
        _acc(dg2_ref, _colsum(dx2 * mn), first)
        dmn = dx2 * g2_ref[...]
        dm = (r2 * (dmn - mn * _mean(dmn * mn))).astype(BF)
        dm_ref[...] = dm
        dmrg = _mm_nt(dm, wo_ref[...])
        bg = bg_ref[...]
        ga = _sigmoid(jnp.concatenate([p0[...], p1[...]], axis=1) + bg[:, :D])
        gb = _sigmoid(jnp.concatenate([p2[...], p3[...]], axis=1) + bg[:, D:])
        dga = dmrg * bra_ref[...].astype(F32) * (ga * (1.0 - ga))
        dgb = dmrg * brb_ref[...].astype(F32) * (gb * (1.0 - gb))
        dgt_ref[:, :D] = dga.astype(BF)
        dgt_ref[:, D:] = dgb.astype(BF)
        _acc(dbg_ref, jnp.concatenate([_colsum(dga), _colsum(dgb)], axis=1), first)
        dbra = (dmrg * ga).astype(BF)
        dbrb = (dmrg * gb).astype(BF)
        dbra_ref[...] = dbra
        dbrb_ref[...] = dbrb
        dyl_ref[...] = _mm_nt(dbra, wl_ref[...])
        dyp = None
        for k in range(NCHIP):
            part = _mm_nt(dbrb[:, k * cpu:(k + 1) * cpu], wp_ref[k])
            dyp = part if dyp is None else dyp + part
        dyp_ref[...] = dyp

    row = lambda w: pl.BlockSpec((tm, w), lambda i: (i, 0))
    full2 = lambda a, b: pl.BlockSpec((a, b), lambda i: (0, 0))
    wp_spec = pl.BlockSpec((NCHIP, DP, cpu), lambda i: (0, 0, 0))
    return _call(
        body, name="bwd_merge", grid=(T // tm,),
        in_specs=[row(D)] * 6 + _gate_specs(tm) +
                 [full2(1, 2 * D), full2(1, D), full2(1, D), full2(DR, D), wp_spec, full2(D, D)],
        out_specs=[row(D), row(2 * D), row(DR), row(DP), row(D), row(D), row(D),
                   full2(1, D), full2(1, D), full2(1, 2 * D)],
        out_shape=[jax.ShapeDtypeStruct((T, D), F32), jax.ShapeDtypeStruct((T, 2 * D), BF),
                   jax.ShapeDtypeStruct((T, DR), F32), jax.ShapeDtypeStruct((T, DP), F32),
                   jax.ShapeDtypeStruct((T, D), BF), jax.ShapeDtypeStruct((T, D), BF),
                   jax.ShapeDtypeStruct((T, D), BF),
                   jax.ShapeDtypeStruct((1, D), F32), jax.ShapeDtypeStruct((1, D), F32),
                   jax.ShapeDtypeStruct((1, 2 * D), F32)],
        vmem=56, args=[dh2, dy, x2, m, bra, brb, proj, proj, proj, proj, b_gate, g2, g3, w_lru_up, w_pool_up, w_o],
        stages=stages)


def _dw_merge(mrg, dm, ylru, dbra, ypool, dbrb, stages=()):
    nb = NCHIP
    rb, pb, cpu = D // nb, DP // nb, D // NCHIP

    def body(mrg_ref, dm_ref, yl_ref, dbra_ref, yp_ref, dbrb_ref, dwo_ref, dwl_ref, dwp_ref):
        dwo_ref[...] = _mm_tn(mrg_ref[...], dm_ref[...]).astype(BF)
        dwl_ref[...] = _mm_tn(yl_ref[...], dbra_ref[...]).astype(BF)
        dwp = _mm_tn(yp_ref[...], dbrb_ref[...]).astype(BF)
        for k in range(NCHIP):
            dwp_ref[k] = dwp[:, k * cpu:(k + 1) * cpu]

    cols = lambda w: pl.BlockSpec((T, w), lambda r: (0, r))
    whole = pl.BlockSpec((T, D), lambda r: (0, 0))
    return _call(
        body, name="dw_merge", grid=(nb,),
        in_specs=[cols(rb), whole, cols(rb), whole, cols(pb), whole],
        out_specs=[pl.BlockSpec((rb, D), lambda r: (r, 0)), pl.BlockSpec((rb, D), lambda r: (r, 0)),
                   pl.BlockSpec((NCHIP, pb, cpu), lambda r: (0, r, 0))],
        out_shape=[jax.ShapeDtypeStruct((D, D), BF), jax.ShapeDtypeStruct((DR, D), BF),
                   jax.ShapeDtypeStruct((NCHIP, DP, cpu), BF)],
        vmem=56, args=[mrg, dm, ylru, dbra, ypool, dbrb], stages=stages)


def _bwd_lru(proj, h, dylru, conv_w, conv_b, wa, ba, wx, bx, lam, stages=()):
    def body(xp_ref, g_ref, h_ref, dy_ref, cw_ref, cb_ref, wa_ref, ba_ref, wx_ref, bx_ref, lam_ref,
             dxp_ref, dg_ref, dcw_ref, dcb_ref, dwa_ref, dba_ref, dwx_ref, dbx_ref, dlam_ref):
        xp = xp_ref[...]
        cw = cw_ref[...]
        lam = lam_ref[...]
        xc, x1, x2, x3 = _conv(xp, cw, cb_ref[...])
        wa, wx = wa_ref[0], wx_ref[0]
        xcb, r, ii, sp, a, mult = _lru_gates(xc, wa, ba_ref[...], wx, bx_ref[...], lam)
        g = g_ref[...]
        gel, dgel = _gelu_parts(g)
        h = h_ref[...]
        dy = dy_ref[...]
        dg_ref[...] = (dy * h * dgel).astype(BF)
        b = dy * gel
        aa = _su(a, 1, 0.0)
        s = 1
        while s < T:
            b = b + aa * _su(b, s, 0.0)
            if 2 * s < T:
                aa = aa * _su(aa, s, 0.0)
            s *= 2
        da = b * _sd(h, 1, 0.0)
        dmult = b * (ii * xc)
        dii = b * (mult * xc)
        dxc = b * (mult * ii)
        dla = da * a - dmult * ((a * a) / mult)
        dr = dla * ((-LRU_C) * sp)
        dsp = _colsum(dla * ((-LRU_C) * r))
        dlam_ref[...] = -dsp / (1.0 + jnp.exp(lam))
        dzr = dr * (r * (1.0 - r))
        dzi = dii * (ii * (1.0 - ii))
        dzrb, dzib = dzr.astype(BF), dzi.astype(BF)
        dxc = dxc + _mm_nt(dzrb, wa) + _mm_nt(dzib, wx)
        dwa_ref[0] = _mm_tn(xcb, dzrb)
        dwx_ref[0] = _mm_tn(xcb, dzib)
        dba_ref[...] = _colsum(dzr)
        dbx_ref[...] = _colsum(dzi)
        dcb_ref[...] = _colsum(dxc)
        dcw_ref[...] = jnp.concatenate([_colsum(dxc * x3), _colsum(dxc * x2), _colsum(dxc * x1),
                                        _colsum(dxc * xp)], axis=0)
        dxp = cw[3:4] * dxc + cw[2:3] * _su(dxc, 1) + cw[1:2] * _su(dxc, 2) + cw[0:1] * _su(dxc, 3)
        dxp_ref[...] = dxp.astype(BF)

    blk = pl.BlockSpec((T, CB), lambda j: (0, j))
    wsp = pl.BlockSpec((1, CB, CB), lambda j: (j, 0, 0))
    return _call(
        body, name="bwd_lru", grid=(NG,),
        in_specs=[blk, pl.BlockSpec((T, CB), lambda j: (0, NG + j)), blk, blk,
                  pl.BlockSpec((4, CB), lambda j: (0, j)), _vec_spec(), wsp, _vec_spec(), wsp, _vec_spec(),
                  _vec_spec()],
        out_specs=[blk, blk, pl.BlockSpec((4, CB), lambda j: (0, j)), _vec_spec(), wsp, _vec_spec(), wsp,
                   _vec_spec(), _vec_spec()],
        out_shape=[jax.ShapeDtypeStruct((T, DR), BF), jax.ShapeDtypeStruct((T, DR), BF),
                   jax.ShapeDtypeStruct((4, DR), F32), jax.ShapeDtypeStruct((1, DR), F32),
                   jax.ShapeDtypeStruct((NG, CB, CB), F32), jax.ShapeDtypeStruct((1, DR), F32),
                   jax.ShapeDtypeStruct((NG, CB, CB), F32), jax.ShapeDtypeStruct((1, DR), F32),
                   jax.ShapeDtypeStruct((1, DR), F32)],
        vmem=56, args=[proj, proj, h, dylru, conv_w, conv_b, wa, ba, wx, bx, lam], stages=stages)


def _bwd_pool(proj, dypool, pool_w, pool_scale):
    def body(xp_ref, dy_ref, pw_ref, sc_ref, dx_ref, dw_ref, dsc_ref):
        for g, w in enumerate(POOL_WINDOWS):
            cols = slice(g * PG, (g + 1) * PG)
            cnt = _pool_cnt(w)
            x = xp_ref[:, cols]
            pb = (_pool_window(x, g + 1, _sd) / cnt - x).astype(BF)
            wg = pw_ref[g]
            dy = dy_ref[:, cols]
            dsc_ref[:, cols] = _colsum(dy * _mm(pb, wg))
            dyp = (dy * sc_ref[:, cols]).astype(BF)
            dw_ref[g] = _mm_tn(pb, dyp)
            dp = _mm_nt(dyp, wg)
            dx_ref[:, cols] = (_pool_window(dp / cnt, g + 1, _su) - dp).astype(BF)

    return pl.pallas_call(
        body, name="bwd_pool", grid=(1,),
        in_specs=[pl.BlockSpec((T, DP), lambda i: (0, 2 * DR // DP)),
                  pl.BlockSpec((T, DP), lambda i: (0, 0)),
                  pl.BlockSpec((4, PG, PG), lambda i: (0, 0, 0)),
                  pl.BlockSpec((1, DP), lambda i: (0, 0))],
        out_specs=[pl.BlockSpec((T, DP), lambda i: (0, 0)),
                   pl.BlockSpec((4, PG, PG), lambda i: (0, 0, 0)),
                   pl.BlockSpec((1, DP), lambda i: (0, 0))],
        out_shape=_hbm_out([jax.ShapeDtypeStruct((T, DP), BF), jax.ShapeDtypeStruct((4, PG, PG), F32),
                            jax.ShapeDtypeStruct((1, DP), F32)]),
        compiler_params=_cp(48),
    )(*_hbm(proj, dypool, pool_w, pool_scale))


def _bwd_inproj(h1, dproj, w_in, stages=()):
    def body(h_ref, dp_ref, w_ref, dw_ref, dh_ref):
        dp = dp_ref[...]
        dw_ref[0] = _mm_tn(h_ref[...], dp).astype(BF)
        _acc(dh_ref, _mm_nt(dp, w_ref[0]), pl.program_id(0) == 0)

    return _call(
        body, name="bwd_inproj", grid=(NCHIP,),
        in_specs=[pl.BlockSpec((T, D), lambda k: (0, 0)),
                  pl.BlockSpec((T, CW_IN), lambda k: (0, k)),
                  pl.BlockSpec((1, D, CW_IN), lambda k: (k, 0, 0))],
        out_specs=[pl.BlockSpec((1, D, CW_IN), lambda k: (k, 0, 0)), pl.BlockSpec((T, D), lambda k: (0, 0))],
        out_shape=[jax.ShapeDtypeStruct((NCHIP, D, CW_IN), BF), jax.ShapeDtypeStruct((T, D), F32)],
        vmem=56, args=[h1, dproj, w_in], stages=stages)


def _bwd_prenorm(x, dh1, dxres, g1, stages=()):
    tm = 512

    def body(x_ref, dh_ref, dr_ref, g_ref, dx_ref, dg_ref):
        xv = x_ref[...]
        r = lax.rsqrt(_mean(xv * xv) + NORM_EPS)
        xn = xv * r
        dh = dh_ref[...]
        t = dh * g_ref[...]
        dx_ref[...] = dr_ref[...] + r * (t - xn * _mean(t * xn))
        _acc(dg_ref, _colsum(dh * xn), pl.program_id(0) == 0)

    row = pl.BlockSpec((tm, D), lambda i: (i, 0))
    vec = pl.BlockSpec((1, D), lambda i: (0, 0))
    return _call(
        body, name="bwd_prenorm", grid=(T // tm,),
        in_specs=[row, row, row, vec], out_specs=[row, vec],
        out_shape=[jax.ShapeDtypeStruct((T, D), F32), jax.ShapeDtypeStruct((1, D), F32)],
        vmem=48, args=[x, dh1, dxres, g1], stages=stages)


def _place():
    x, y, c = lax.axis_index("x"), lax.axis_index("y"), lax.axis_index("c")
    chips = [(1 - x, y), (x, 1 - y), (1 - x, 1 - y)]
    return x, y, c, chips


def _rcopy(src, dst, ssem, rsem, dev):
    return pltpu.make_async_remote_copy(src_ref=src, dst_ref=dst, send_sem=ssem, recv_sem=rsem,
                                        device_id=dev, device_id_type=MESH_ID)


def _sds(a):
    return jax.ShapeDtypeStruct(a.shape, a.dtype)


def _sem2(n, m):
    return [pltpu.SemaphoreType.DMA((n, m)), pltpu.SemaphoreType.DMA((n, m))]


ALL = (0, 1, 1)


def _piece(ref, k, half, part):
    hr = ref.shape[1] // 2
    r0, r1 = hr * part[0] // part[2], hr * part[1] // part[2]
    return ref.at[k, pl.ds(half * hr + r0, r1 - r0), :]


def _gather(fulls, ici=(), d2d=()):
    n = len(fulls)
    ici, d2d = list(ici), list(d2d)

    def copies(outs, sems):
        x, y, c, chips = _place()
        me = 2 * x + y
        sib = (x, y, 1 - c)
        send, recv = [], []
        for q, (i, part) in enumerate(ici):
            for j, chip in enumerate(chips):
                mine, theirs = _piece(outs[i], me, c, part), _piece(outs[i], 2 * chip[0] + chip[1], c, part)
                send.append(_rcopy(mine, mine, sems[0].at[q, j], sems[1].at[q, j], (*chip, c)))
                recv.append(_rcopy(theirs, theirs, sems[0].at[q, j], sems[1].at[q, j], (*chip, c)))
        for q, (i, part) in enumerate(d2d):
            for j, chip in enumerate(chips):
                k = 2 * chip[0] + chip[1]
                got, other = _piece(outs[i], k, c, part), _piece(outs[i], k, 1 - c, part)
                send.append(_rcopy(got, got, sems[2].at[q, j], sems[3].at[q, j], sib))
                recv.append(_rcopy(other, other, sems[2].at[q, j], sems[3].at[q, j], sib))
        return send, recv

    def start(ins, outs, sems):
        for cp in copies(outs, sems)[0]:
            cp.start()

    def finish(ins, outs, sems):
        send, recv = copies(outs, sems)
        for cp in recv:
            cp.wait_recv()
        for cp in send:
            cp.wait_send()

    sems = _sem2(max(len(ici), 1), 3) + _sem2(max(len(d2d), 1), 3)
    return _Stage(fulls, [_sds(f) for f in fulls], {i: i for i in range(n)}, sems, start, finish)


def _gather_first(full, conv_w):
    d2d = _gather([full], d2d=[(0, ALL)])

    def body(full_in, cw_in, full_out, cw_out, s0, r0, s1, r1, cs, cr):
        x, y, c, chips = _place()
        me = 2 * x + y
        conv = [_rcopy(cw_in, cw_out.at[me], cs.at[j], cr.at[j], (*chip, c)) for j, chip in enumerate(chips)]
        for cp in conv:
            cp.start()
        sems = [s0, r0, s1, r1]
        d2d.start(None, [full_out], sems)
        d2d.finish(None, [full_out], sems)
        for j, chip in enumerate(chips):
            _rcopy(cw_in, cw_out.at[2 * chip[0] + chip[1]], cs.at[j], cr.at[j], (*chip, c)).wait_recv()
        for cp in conv:
            cp.wait_send()

    return pl.pallas_call(
        body, name="gather_first",
        in_specs=[ANY, ANY], out_specs=[ANY, ANY],
        out_shape=_hbm_out([full, jax.ShapeDtypeStruct((NCHIP,) + conv_w.shape, conv_w.dtype)]),
        input_output_aliases={0: 0},
        scratch_shapes=_sem2(1, 3) + _sem2(1, 3) + [pltpu.SemaphoreType.DMA((3,)), pltpu.SemaphoreType.DMA((3,))],
        compiler_params=pltpu.CompilerParams(has_side_effects=True),
    )(*_hbm(full, conv_w))


def _comm_only(name, stage):
    def body(*refs):
        ni, no = len(stage.operands), len(stage.out_shape)
        stage.start(refs[:ni], refs[ni:ni + no], refs[ni + no:])
        stage.finish(refs[:ni], refs[ni:ni + no], refs[ni + no:])

    return pl.pallas_call(
        body, name=name, in_specs=[ANY] * len(stage.operands), out_specs=[ANY] * len(stage.out_shape),
        out_shape=_hbm_out(stage.out_shape), input_output_aliases=stage.alias, scratch_shapes=stage.sems,
        compiler_params=pltpu.CompilerParams(has_side_effects=True),
    )(*_hbm(*stage.operands))


def _to_sibling(srcs):
    n = len(srcs)

    def copies(ins, outs, sems):
        x, y, c, _ = _place()
        sib = (x, y, 1 - c)
        return [_rcopy(ins[i].at[:, 1 - c] if srcs[i].ndim == 4 else ins[i], outs[i], sems[0].at[i], sems[1].at[i], sib)
                for i in range(n)]

    def start(ins, outs, sems):
        for cp in copies(ins, outs, sems):
            cp.start()

    def finish(ins, outs, sems):
        for cp in copies(ins, outs, sems):
            cp.wait()

    shapes = [jax.ShapeDtypeStruct((NCHIP,) + s.shape[2:] if s.ndim == 4 else s.shape, s.dtype) for s in srcs]
    return _Stage(srcs, shapes, {}, [pltpu.SemaphoreType.DMA((n,)), pltpu.SemaphoreType.DMA((n,))], start, finish)


def _to_chips(srcs, parts=None, lands=None):
    n = len(srcs)
    parts = [ALL] * n if parts is None else parts
    lands = [None] * n if lands is None else lands
    given = [i for i in range(n) if lands[i] is not None]

    def rows(ref, i):
        hr = srcs[i].shape[1]
        r0, r1 = hr * parts[i][0] // parts[i][2], hr * parts[i][1] // parts[i][2]
        return ref.at[pl.ds(r0, r1 - r0), :]

    def copies(ins, outs, sems):
        x, y, c, chips = _place()
        me = 2 * x + y
        return [_rcopy(rows(ins[i].at[2 * chip[0] + chip[1]] if srcs[i].shape[0] == NCHIP else ins[i].at[c], i),
                       rows(outs[i].at[me], i), sems[0].at[i, j], sems[1].at[i, j], (*chip, c))
                for i in range(n) for j, chip in enumerate(chips)]

    def start(ins, outs, sems):
        for cp in copies(ins, outs, sems):
            cp.start()

    def finish(ins, outs, sems):
        for cp in copies(ins, outs, sems):
            cp.wait()

    shapes = [jax.ShapeDtypeStruct((NCHIP,) + s.shape[1:], s.dtype) for s in srcs]
    alias = {n + q: i for q, i in enumerate(given)}
    return _Stage(list(srcs) + [lands[i] for i in given], shapes, alias, _sem2(n, 3), start, finish)


def _share(pairs):
    n = len(pairs)

    def start(ins, outs, sems):
        x, y, c, _ = _place()
        for i in range(n):
            _rcopy(outs[i].at[c], outs[i].at[c], sems[0].at[i], sems[1].at[i], (x, y, 1 - c)).start()

    def finish(ins, outs, sems):
        x, y, c, _ = _place()
        for i in range(n):
            _rcopy(outs[i].at[c], outs[i].at[c], sems[0].at[i], sems[1].at[i], (x, y, 1 - c)).wait_send()
            _rcopy(outs[i].at[1 - c], outs[i].at[1 - c], sems[0].at[i], sems[1].at[i], (x, y, 1 - c)).wait_recv()

    return _Stage(pairs, [_sds(p) for p in pairs], {i: i for i in range(n)},
                  [pltpu.SemaphoreType.DMA((n,)), pltpu.SemaphoreType.DMA((n,))], start, finish)


def _row_block(rows, cols, itemsize=4, target=2 * MIB):
    br = rows
    while br * cols * itemsize > target and br % 32 == 0:
        br //= 2
    return br


def _cast_place(w, chip_idx, name):
    rows, cols = w.shape
    br = _row_block(rows, cols)

    def body(k_ref, w_ref, o_ref):
        o_ref[0] = w_ref[...].astype(BF)

    return _call(
        body, name=name, grid=(rows // br,), prefetch=chip_idx,
        in_specs=[pl.BlockSpec((br, cols), lambda r, k: (r, 0))],
        out_specs=[pl.BlockSpec((1, br, cols), lambda r, k: (k[0], r, 0))],
        out_shape=[jax.ShapeDtypeStruct((NCHIP, rows, cols), BF)], vmem=32, args=[w])[0][0]


def _cast_place_multi(ws, chip_idx, stages=()):
    br = 128
    nblk = [a.shape[0] // br for a in ws]
    starts = [sum(nblk[:i]) for i in range(len(ws))]

    def body(k_ref, *refs):
        r = pl.program_id(0)
        for i in range(len(ws)):
            @pl.when(jnp.logical_and(r >= starts[i], r < starts[i] + nblk[i]))
            def _(i=i):
                refs[len(ws) + i][0] = refs[i][...].astype(BF)

    def at(i):
        return functools.partial(lambda r, s, nb: jnp.clip(r - s, 0, nb - 1), s=starts[i], nb=nblk[i])

    outs, landed = _call(
        body, name="cast_rest", grid=(sum(nblk),), prefetch=chip_idx,
        in_specs=[pl.BlockSpec((br, a.shape[1]), functools.partial(lambda r, k, f: (f(r), 0), f=at(i)))
                  for i, a in enumerate(ws)],
        out_specs=[pl.BlockSpec((1, br, a.shape[1]), functools.partial(lambda r, k, f: (k[0], f(r), 0), f=at(i)))
                   for i, a in enumerate(ws)],
        out_shape=[jax.ShapeDtypeStruct((NCHIP,) + a.shape, BF) for a in ws], vmem=32, args=list(ws), stages=stages)
    return outs, landed


def _add_sibling(g, land, cidx, name, stages=()):
    _, _, hr, cols = g.shape
    br = _row_block(hr, cols)

    def body(c_ref, g_ref, l_ref, o_ref):
        o_ref[...] = (g_ref[0, 0].astype(F32) + l_ref[0].astype(F32)).astype(BF)[None]

    outs, st = _call(
        body, name=name, grid=(NCHIP, hr // br), prefetch=cidx,
        in_specs=[pl.BlockSpec((1, 1, br, cols), lambda k, r, c: (k, c[0], r, 0)),
                  pl.BlockSpec((1, br, cols), lambda k, r, c: (k, r, 0))],
        out_specs=[pl.BlockSpec((1, br, cols), lambda k, r, c: (k, r, 0))],
        out_shape=[jax.ShapeDtypeStruct((NCHIP, hr, cols), BF)], vmem=32, args=[g, land], stages=stages)
    return outs[0], st


def _add_pair(a, b, name):
    rows, cols = a.shape

    def body(a_ref, b_ref, o_ref):
        o_ref[...] = a_ref[...] + b_ref[...]

    spec = pl.BlockSpec((rows, cols), lambda r: (0, 0))
    return _call(body, name=name, grid=(1,), in_specs=[spec, spec], out_specs=[spec], out_shape=[_sds(a)],
                 vmem=32, args=[a, b])[0][0]


def _add_chips(own, land, idx, name, stages=None):
    _, hr, cols = land.shape
    br = _row_block(hr, cols)

    def body(s_ref, a_ref, b_ref, c_ref, d_ref, o_ref):
        o_ref[...] = (a_ref[...].astype(F32) + b_ref[...].astype(F32)) + (c_ref[...].astype(F32) +
                                                                           d_ref[...].astype(F32))

    spec = lambda q: pl.BlockSpec((1, br, cols), functools.partial(lambda r, s, q: (s[q], r, 0), q=q))
    outs, landed = _call(
        body, name=name, grid=(hr // br,), prefetch=idx,
        in_specs=[spec(0), spec(1), spec(2), spec(3)], out_specs=[spec(4)],
        out_shape=[jax.ShapeDtypeStruct((2, hr, cols), F32)], vmem=48, args=[own, land, land, land],
        stages=stages or ())
    return outs[0] if stages is None else (outs[0], landed)


def _adamw_math(w, g, m, v):
    mn = ADAM_B1 * m + (1.0 - ADAM_B1) * g
    vn = ADAM_B2 * v + (1.0 - ADAM_B2) * (g * g)
    m_hat = mn / (1.0 - ADAM_B1 ** ADAM_STEP)
    v_hat = vn / (1.0 - ADAM_B2 ** ADAM_STEP)
    return -ADAM_LR * (m_hat / (jnp.sqrt(v_hat) + ADAM_EPS) + ADAM_WD * w), mn, vn


def _adamw(w, g, m, v, name, stages=()):
    rows, cols = w.shape
    br = _row_block(rows, cols)

    def body(w_ref, g_ref, m_ref, v_ref, d_ref, mo_ref, vo_ref):
        d_ref[...], mo_ref[...], vo_ref[...] = _adamw_math(w_ref[...], g_ref[...], m_ref[...], v_ref[...])

    spec = pl.BlockSpec((br, cols), lambda r: (r, 0))
    return _call(body, name=name, grid=(rows // br,), in_specs=[spec] * 4, out_specs=[spec] * 3,
                 out_shape=[_sds(w)] * 3, vmem=48, args=[w, g, m, v], stages=stages)


def _adamw_multi(names, w, g, m, v, stages=()):
    cols = w[names[0]].shape[1]
    br = 128
    nblk = [w[n].shape[0] // br for n in names]
    starts = [sum(nblk[:i]) for i in range(len(names))]

    def body(*refs):
        r = pl.program_id(0)
        for i in range(len(names)):
            w_ref, g_ref, m_ref, v_ref = refs[4 * i:4 * i + 4]
            d_ref, mo_ref, vo_ref = refs[4 * len(names) + 3 * i:4 * len(names) + 3 * i + 3]

            @pl.when(jnp.logical_and(r >= starts[i], r < starts[i] + nblk[i]))
            def _():
                d_ref[...], mo_ref[...], vo_ref[...] = _adamw_math(w_ref[...], g_ref[...], m_ref[...], v_ref[...])

    def spec(i):
        return pl.BlockSpec((br, cols), functools.partial(
            lambda r, s, nb: (jnp.clip(r - s, 0, nb - 1), 0), s=starts[i], nb=nblk[i]))

    outs, landed = _call(
        body, name="adamw_" + "_".join(names), grid=(sum(nblk),),
        in_specs=[spec(i) for i in range(len(names)) for _ in range(4)],
        out_specs=[spec(i) for i in range(len(names)) for _ in range(3)],
        out_shape=[_sds(w[n]) for n in names for _ in range(3)], vmem=48,
        args=[a[n] for n in names for a in (w, g, m, v)], stages=stages)
    return {n: outs[3 * i:3 * i + 3] for i, n in enumerate(names)}, landed


def _to_everyone(v):
    deltas = [(a, b, e) for a in (0, 1) for b in (0, 1) for e in (0, 1)][1:]

    def copies(ins, outs, sems):
        x, y, c, _ = _place()
        me = 4 * x + 2 * y + c
        flip = lambda p, f: 1 - p if f else p
        return [_rcopy(ins[0], outs[0].at[me], sems[0].at[q], sems[1].at[q], (flip(x, a), flip(y, b), flip(c, e)))
                for q, (a, b, e) in enumerate(deltas)]

    def start(ins, outs, sems):
        for cp in copies(ins, outs, sems):
            cp.start()

    def finish(ins, outs, sems):
        for cp in copies(ins, outs, sems):
            cp.wait()

    n = len(deltas)
    return _Stage([v], [jax.ShapeDtypeStruct((2 * NCHIP,) + v.shape, v.dtype)], {},
                  [pltpu.SemaphoreType.DMA((n,)), pltpu.SemaphoreType.DMA((n,))], start, finish)


SMALL_AT = {"norm_mix_pre": (0, 1, D), "norm_mix_post": (1, 1, D), "norm_mlp_pre": (2, 1, D),
            "norm_mlp_post": (3, 1, D), "b_gate": (4, 2, D), "conv_b": (6, 1, D), "lru_b_a": (7, 1, D),
            "lru_b_x": (8, 1, D), "lru_lambda": (9, 1, D), "pool_scale": (10, 1, DP)}
SMALL_SEPARATE = ["conv_w", "lru_w_a", "lru_w_x", "pool_w"]


def _adamw_small(small_sum, first_all, sep_grads, w, m, v):
    packed, sep = list(SMALL_AT), list(SMALL_SEPARATE)
    names = packed + sep

    def body(*refs):
        s_ref, a_ref, refs = refs[0], refs[1], refs[2:]
        g_sep, refs = refs[:len(sep)], refs[len(sep):]
        nn = len(names)
        w_r, m_r, v_r, refs = refs[:nn], refs[nn:2 * nn], refs[2 * nn:3 * nn], refs[3 * nn:]
        g_out, refs = refs[:len(packed)], refs[len(packed):]
        d_o, m_o, v_o = refs[:nn], refs[nn:2 * nn], refs[2 * nn:3 * nn]
        for i, n in enumerate(names):
            if i == 0:
                g = a_ref[0:1, :]
                for q in range(1, 2 * NCHIP):
                    g = g + a_ref[q:q + 1, :]
                g_out[i][...] = g
            elif n in SMALL_AT:
                r0, nr, nc = SMALL_AT[n]
                g = jnp.concatenate([s_ref[r0 + q:r0 + q + 1, :nc] for q in range(nr)], axis=1)
                g_out[i][...] = g
            else:
                g = g_sep[i - len(packed)][...]
            d_o[i][...], m_o[i][...], v_o[i][...] = _adamw_math(w_r[i][...], g, m_r[i][...], v_r[i][...])

    ws = [w[n] for n in names]
    res = pl.pallas_call(
        body, name="adamw_small",
        out_shape=[_sds(w[n]) for n in packed] + [_sds(a) for a in ws] * 3,
        compiler_params=_cp(32),
    )(*_hbm(small_sum, first_all, *sep_grads, *ws, *[m[n] for n in names], *[v[n] for n in names]))
    nn, npk = len(names), len(packed)
    grad = dict(zip(packed, res[:npk]))
    delta = dict(zip(names, res[npk:npk + nn]))
    new_m = dict(zip(names, res[npk + nn:npk + 2 * nn]))
    new_v = dict(zip(names, res[npk + 2 * nn:]))
    return grad, delta, new_m, new_v


W_NAMES = ["norm_mix_pre", "norm_mix_post", "norm_mlp_pre", "norm_mlp_post", "w_in", "b_gate", "conv_w", "conv_b",
           "lru_w_a", "lru_b_a", "lru_w_x", "lru_b_x", "lru_lambda", "pool_w", "pool_scale", "w_lru_up",
           "w_pool_up", "w_o", "w_ff1", "w_ff2"]
BIG = ["w_in", "w_lru_up", "w_pool_up", "w_o", "w_ff1", "w_ff2"]


def _block_diag(w):
    hd = w.shape[-1]
    per = CB // hd
    w4 = w.reshape(NG, per, hd, hd)
    eye = jnp.eye(per, dtype=w.dtype)
    return jnp.einsum("gpij,pq->gpiqj", w4, eye).reshape(NG, CB, CB)


def _block_diag_extract(d, hd):
    per = CB // hd
    d5 = d.reshape(NG, per, hd, per, hd)
    return jnp.stack([d5[:, p, :, p, :] for p in range(per)], axis=1).reshape(NG * per, hd, hd)


def _halves(g):
    return g.reshape(NCHIP, 2, g.size // (g.shape[-1] * 2 * NCHIP), g.shape[-1])


def kernel(x, norm_mix_pre, norm_mix_post, norm_mlp_pre, norm_mlp_post, w_in, b_gate, conv_w, conv_b, lru_w_a, lru_b_a, lru_w_x, lru_b_x, lru_lambda, pool_w, pool_scale, w_lru_up, w_pool_up, w_o, w_ff1, w_ff2, loss_target, m_norm_mix_pre, m_norm_mix_post, m_norm_mlp_pre, m_norm_mlp_post, m_w_in, m_b_gate, m_conv_w, m_conv_b, m_lru_w_a, m_lru_b_a, m_lru_w_x, m_lru_b_x, m_lru_lambda, m_pool_w, m_pool_scale, m_w_lru_up, m_w_pool_up, m_w_o, m_w_ff1, m_w_ff2, v_norm_mix_pre, v_norm_mix_post, v_norm_mlp_pre, v_norm_mlp_post, v_w_in, v_b_gate, v_conv_w, v_conv_b, v_lru_w_a, v_lru_b_a, v_lru_w_x, v_lru_b_x, v_lru_lambda, v_pool_w, v_pool_scale, v_w_lru_up, v_w_pool_up, v_w_o, v_w_ff1, v_w_ff2):
    args = dict(locals())
    two_d = lambda a: a.reshape(-1, a.shape[-1])
    w = {n: two_d(args[n]) for n in W_NAMES}
    mom = {n: two_d(args["m_" + n]) for n in W_NAMES}
    var = {n: two_d(args["v_" + n]) for n in W_NAMES}
    i32 = lambda val: jnp.asarray(val, jnp.int32)
    chip = i32(2 * lax.axis_index("x") + lax.axis_index("y"))
    core = i32(lax.axis_index("c"))
    cidx = core.reshape(1)
    zero = i32(0)
    hd = lru_w_a.shape[-1]
    xs, target = x[0], loss_target[0]
    g1, g2, g3, g4 = norm_mix_pre, norm_mix_post, norm_mlp_pre, norm_mlp_post

    full = {"w_in": _cast_place(w["w_in"], chip.reshape(1), "cast_w_in")}
    casts, ((full["w_in"],),) = _cast_place_multi([w[n] for n in BIG[1:]], chip.reshape(1),
                                                 stages=[_gather([full["w_in"]], ici=[(0, ALL)])])
    full.update(zip(BIG[1:], casts))
    wa = _block_diag(lru_w_a[0]).astype(BF)
    wx = _block_diag(lru_w_x[0]).astype(BF)
    pw = pool_w[0].astype(BF)

    full["w_in"], conv_all = _gather_first(full["w_in"], w["conv_w"])
    conv_all = lax.dynamic_update_slice(conv_all, w["conv_w"][None], (chip, zero, zero))
    conv_full = jnp.transpose(conv_all, (1, 0, 2)).reshape(4, DR)
    mix = ["w_lru_up", "w_pool_up", "w_o"]
    ff1_a, ff1_b, ff2_a, ff2_b = (0, 3, 8), (3, 8, 8), (0, 1, 4), (1, 4, 4)
    (proj, h1), (got,) = _fwd_inproj(xs, g1, full["w_in"], stages=[_gather(
        [full[n] for n in mix] + [full["w_ff1"]], ici=[(0, ALL), (1, ALL), (2, ALL), (3, ff1_a)])])
    (ylru, hs), (got,) = _fwd_lru(proj, conv_full, conv_b, wa, lru_b_a, wx, lru_b_x, lru_lambda, stages=[_gather(
        got + [full["w_ff2"]], d2d=[(0, ALL), (1, ALL), (2, ALL), (3, ff1_a)], ici=[(3, ff1_b), (4, ff2_a)])])
    w_lru_up_f, w_pool_up_f, w_o_f = got[0].reshape(DR, D), got[1], got[2].reshape(D, D)
    ypool = _fwd_pool(proj, pw, pool_scale)
    (x2, h2, m, mrg, bra, brb), ((ff1, ff2),) = _fwd_merge(
        xs, ylru, ypool, proj, b_gate, g2, g3, w_lru_up_f, w_pool_up_f, w_o_f,
        stages=[_gather(got[3:], d2d=[(0, ff1_b), (1, ff2_a)], ici=[(1, ff2_b)])])
    ff2 = _comm_only("gather_last", _gather([ff2], d2d=[(0, ff2_b)]))[0].reshape(DF, D)
    a1, f = _fwd_mlp(h2, ff1, ff2)
    lossp, dy, df, dg4 = _loss_head(f, x2, target, g4)

    idx_big = jnp.stack([chip, (chip + 1) % NCHIP, (chip + 2) % NCHIP, (chip + 3) % NCHIP, core])
    dh2, df1 = _bwd_mlp_x(df, a1, ff1, ff2)
    dw_ff1, dw_ff2 = _bwd_mlp_w(df, h2, a1, df1)
    g_ff = [_halves(dw_ff1), _halves(dw_ff2)]
    (dxres, dgates, dylru, dypool, dm, dbra, dbrb, dg2, dg3, dbg), (l_ff,) = _bwd_merge(
        dh2, dy, x2, m, bra, brb, proj, b_gate, g2, g3, w_lru_up_f, w_pool_up_f, w_o_f, stages=[_to_sibling(g_ff)])
    p_ff = [_add_sibling(g, l, cidx, "add_sibling_" + n)[0] for g, l, n in zip(g_ff, l_ff, ["w_ff1", "w_ff2"])]
    ff1_head, ff1_tail = (0, 7, 16), (7, 16, 16)
    (dw_o, dw_lru_up, dw_pool_up), ((c_ff1,),) = _dw_merge(
        mrg, dm, ylru, dbra, ypool, dbrb, stages=[_to_chips(p_ff[:1], parts=[ff1_head])])
    g_mix = [_halves(dw_lru_up), _halves(dw_pool_up), _halves(dw_o)]
    (dxp, dgl, dcw, dcb, dwa, dba, dwx, dbx, dlam), ((c_ff1, c_ff2), l_mix) = _bwd_lru(
        proj, hs, dylru, conv_full, conv_b, wa, lru_b_a, wx, lru_b_x, lru_lambda,
        stages=[_to_chips(p_ff, parts=[ff1_tail, ALL], lands=[c_ff1, None]), _to_sibling(g_mix)])
    p_mix = [_add_sibling(g, l, cidx, "add_sibling_" + n)[0] for g, l, n in zip(g_mix, l_mix, mix)]
    dxpool, dpw, dsc = _bwd_pool(proj, dypool, pw, pool_scale)
    dproj = jnp.concatenate([dxp, dgl, dxpool, dgates], axis=1)
    small = jnp.concatenate([
        jnp.zeros((1, D), F32), dg2, dg3, dg4, dbg.reshape(2, D), dcb, dba, dbx, dlam,
        jnp.pad(dsc, ((0, 0), (0, D - DP))), jnp.pad(lossp, ((0, 0), (0, D - 1))), dcw,
        _block_diag_extract(dwa, hd).reshape(-1, D), _block_diag_extract(dwx, hd).reshape(-1, D),
        dpw.reshape(-1, D)], axis=0)
    (dw_in, dh1), (c_mix, (l_small,)) = _bwd_inproj(h1, dproj, full["w_in"],
                                                     stages=[_to_chips(p_mix), _to_sibling([small])])
    small2 = _add_pair(small, l_small, "add_sibling_small").reshape(2, SMALL_ROWS // 2, D)
    g_in = _halves(dw_in)
    done = ["w_ff1", "w_ff2"] + mix
    pair_ff1, ((l_in,), (c_small,)) = _add_chips(p_ff[0], c_ff1, idx_big, "add_chips_w_ff1",
                                                 stages=[_to_sibling([g_in]), _to_chips([small2])])
    pairs = [pair_ff1] + [_add_chips(p, l, idx_big, "add_chips_" + n)
                          for p, l, n in zip(p_ff[1:] + p_mix, [c_ff2] + c_mix, done[1:])]
    own_small = lax.dynamic_index_in_dim(small2, core, 0, keepdims=True)
    c_small = lax.dynamic_update_slice(c_small, own_small, (chip, zero, zero))
    pair_small = _add_chips(c_small, c_small, jnp.stack([zero, zero + 1, zero + 2, zero + 3, core]), "add_chips_small")
    (grad_x, dg1), _ = _bwd_prenorm(xs, dh1, dxres, g1)
    p_in, (shared, (dg1_all,)) = _add_sibling(
        g_in, l_in, cidx, "add_sibling_w_in", stages=[_share(pairs + [pair_small]), _to_everyone(dg1)])
    pairs, pair_small = shared[:-1], shared[-1]
    dg1_all = lax.dynamic_update_slice(dg1_all, dg1[None], (2 * chip + core, zero, zero)).reshape(2 * NCHIP, D)

    grads, delta, new_m, new_v = {}, {}, {}, {}
    for n, p in zip(done, pairs):
        grads[n] = p.reshape(-1, p.shape[-1])

    def update(n, stages=()):
        (delta[n], new_m[n], new_v[n]), landed = _adamw(w[n], grads[n], mom[n], var[n], "adamw_" + n, stages=stages)
        return landed

    updated, ((c_in,),) = _adamw_multi(["w_ff1", "w_ff2", "w_o", "w_lru_up"], w, grads, mom, var,
                                      stages=[_to_chips([p_in])])
    for n, (d, mo, vo) in updated.items():
        delta[n], new_m[n], new_v[n] = d, mo, vo
    pair_in = _add_chips(p_in, c_in, idx_big, "add_chips_w_in")
    ((pair_in,),) = update("w_pool_up", stages=[_share([pair_in])])
    grads["w_in"] = pair_in.reshape(-1, pair_in.shape[-1])
    update("w_in")
    small_sum = pair_small.reshape(SMALL_ROWS, D)
    loss = 0.5 * small_sum[LOSS_ROW, 0]
    ccols = DR // NCHIP
    sep = [lax.dynamic_slice(small_sum[12:16], (zero, chip * ccols), (4, ccols)),
           small_sum[16:80].reshape(-1, hd), small_sum[80:144].reshape(-1, hd), small_sum[144:208].reshape(-1, PG)]
    g_s, d_s, m_s, v_s = _adamw_small(small_sum, dg1_all, sep, w, mom, var)
    grads.update(g_s)
    grads.update(dict(zip(SMALL_SEPARATE, sep)))
    delta.update(d_s)
    new_m.update(m_s)
    new_v.update(v_s)

    out = lambda d: [d[n].reshape(args[n].shape) for n in W_NAMES]
    return (loss, grad_x[None], *out(grads), *out(delta), *out(new_m), *out(new_v))
```

```python
import functools
import math

import jax
import jax.numpy as jnp
from jax import lax
from jax.experimental import pallas as pl
from jax.experimental.pallas import tpu as pltpu

F32 = jnp.float32
BF = jnp.bfloat16

T = 2048
D = 1024
DR = 1024
DP = 512
DF = 4096
DIN = 4608
NCHIP = 4
CW_IN = DIN // NCHIP
LANE = 128
CB = 128
NG = DR // CB
PG = 128
POOL_WINDOWS = (2, 4, 8, 16)
NORM_EPS = 1e-6
LRU_C = 8.0
GELU_C = math.sqrt(2.0 / math.pi)
ADAM_LR = 0.001
ADAM_B1 = 0.9
ADAM_B2 = 0.999
ADAM_EPS = 1e-08
ADAM_WD = 0.01
ADAM_STEP = 10
MESH_ID = pl.DeviceIdType.MESH
ANY = pl.BlockSpec(memory_space=pl.ANY)
SMALL_ROWS = 208
LOSS_ROW = 11
MIB = 1 << 20


def _cp(vmem_mib=None):
    if vmem_mib is None:
        return pltpu.CompilerParams()
    return pltpu.CompilerParams(vmem_limit_bytes=vmem_mib * MIB)


def _hbm(*arrays):
    return [pltpu.with_memory_space_constraint(a, pltpu.HBM) for a in arrays]


def _hbm_out(shapes):
    return [pltpu.HBM(s.shape, s.dtype) for s in shapes]


class _Stage:
    def __init__(self, operands, out_shape, alias, sems, start, finish):
        self.operands, self.out_shape, self.alias, self.sems = list(operands), list(out_shape), dict(alias), list(sems)
        self.start, self.finish = start, finish


def _call(body, *, name, grid, in_specs, out_specs, out_shape, args, vmem=None, stages=(), prefetch=None,
          scratch=()):
    nin, nout = len(in_specs), len(out_specs)
    npre = 0 if prefetch is None else 1
    st_args, st_shapes, st_sems, aliases = [], [], list(scratch), {}
    for st in stages:
        for k, v in st.alias.items():
            aliases[npre + nin + len(st_args) + k] = nout + len(st_shapes) + v
        st_args += st.operands
        st_shapes += st.out_shape
        st_sems += st.sems

    def wrapped(*refs):
        pre, refs = refs[:npre], refs[npre:]
        ins, pos = refs[:nin], nin
        st_ins = []
        for st in stages:
            st_ins.append(refs[pos:pos + len(st.operands)])
            pos += len(st.operands)
        outs, pos = refs[pos:pos + nout], pos + nout
        st_outs = []
        for st in stages:
            st_outs.append(refs[pos:pos + len(st.out_shape)])
            pos += len(st.out_shape)
        work, pos = refs[pos:pos + len(scratch)], pos + len(scratch)
        sems = []
        for st in stages:
            sems.append(refs[pos:pos + len(st.sems)])
            pos += len(st.sems)
        if stages:
            first = functools.reduce(jnp.logical_and, [pl.program_id(a) == 0 for a in range(len(grid))])

            @pl.when(first)
            def _():
                for st, a, b, s in zip(stages, st_ins, st_outs, sems):
                    st.start(a, b, s)

        body(*pre, *ins, *outs, *work)
        if stages:
            last = functools.reduce(jnp.logical_and, [pl.program_id(a) == g - 1 for a, g in enumerate(grid)])

            @pl.when(last)
            def _():
                for st, a, b, s in zip(stages, st_ins, st_outs, sems):
                    st.finish(a, b, s)

    all_in = list(in_specs) + [ANY] * len(st_args)
    all_out = list(out_specs) + [ANY] * len(st_shapes)
    kw = dict(has_side_effects=True) if stages else {}
    if vmem is not None:
        kw["vmem_limit_bytes"] = vmem * MIB
    if prefetch is None:
        gkw = dict(grid=grid, in_specs=all_in, out_specs=all_out, scratch_shapes=st_sems)
    else:
        gkw = dict(grid_spec=pltpu.PrefetchScalarGridSpec(
            num_scalar_prefetch=1, grid=grid, in_specs=all_in, out_specs=all_out, scratch_shapes=st_sems))
    res = pl.pallas_call(
        wrapped, name=name, out_shape=_hbm_out(list(out_shape) + st_shapes), input_output_aliases=aliases,
        compiler_params=pltpu.CompilerParams(**kw), **gkw,
    )(*([prefetch] if npre else []), *_hbm(*args, *st_args))
    outs, rest, st_res = list(res[:nout]), list(res[nout:]), []
    for st in stages:
        st_res.append(rest[:len(st.out_shape)])
        rest = rest[len(st.out_shape):]
    return outs, st_res


def _mm(a, b):
    return jnp.dot(a.astype(BF), b.astype(BF), preferred_element_type=F32)


def _mm_nt(a, b):
    return lax.dot_general(a.astype(BF), b.astype(BF), (((1,), (1,)), ((), ())),
                           preferred_element_type=F32)


def _mm_tn(a, b):
    return lax.dot_general(a.astype(BF), b.astype(BF), (((0,), (0,)), ((), ())),
                           preferred_element_type=F32)


def _rows(v):
    return lax.broadcasted_iota(jnp.int32, v.shape, 0)


def _sd(v, s, fill=0.0):
    return jnp.where(_rows(v) >= s, pltpu.roll(v, s, axis=0), fill)


def _su(v, s, fill=0.0):
    n = v.shape[0]
    return jnp.where(_rows(v) < n - s, pltpu.roll(v, n - s, axis=0), fill)


def _sigmoid(z):
    return 1.0 / (1.0 + jnp.exp(-z))


def _softplus(z):
    e = jnp.exp(-jnp.abs(z))
    u = 1.0 + e
    d = u - 1.0
    log1p = jnp.where(d == 0.0, e, jnp.log(u) * (e / jnp.where(d == 0.0, 1.0, d)))
    return jnp.maximum(z, 0.0) + log1p


def _mean(v):
    return jnp.mean(v, axis=-1, keepdims=True)


def _colsum(v):
    return jnp.sum(v, axis=0, keepdims=True)


def _acc(ref, val, first):
    @pl.when(first)
    def _():
        ref[...] = val

    @pl.when(jnp.logical_not(first))
    def _():
        ref[...] += val


def _conv(xp, cw, cb):
    x1, x2, x3 = _sd(xp, 1), _sd(xp, 2), _sd(xp, 3)
    xc = cb + cw[0:1] * x3 + cw[1:2] * x2 + cw[2:3] * x1 + cw[3:4] * xp
    return xc, x1, x2, x3


def _lru_gates(xc, wa, ba, wx, bx, lam):
    xcb = xc.astype(BF)
    r = _sigmoid(_mm(xcb, wa) + ba)
    ii = _sigmoid(_mm(xcb, wx) + bx)
    sp = _softplus(-lam)
    la = (-LRU_C) * r * sp
    a = jnp.exp(la)
    mult = jnp.sqrt(-jnp.tanh(la) * (a * a + 1.0))
    return xcb, r, ii, sp, a, mult


def _gelu_parts(g):
    th = jnp.tanh(GELU_C * (g + 0.044715 * (g * g * g)))
    gel = 0.5 * g * (1.0 + th)
    dgel = 0.5 * (1.0 + th) + 0.5 * g * (1.0 - th * th) * (GELU_C * (1.0 + 3.0 * 0.044715 * (g * g)))
    return gel, dgel


def _pool_window(x, steps, shift):
    s, sh = x, 1
    for _ in range(steps):
        s = s + shift(s, sh)
        sh *= 2
    return s


def _fwd_inproj(x, g1, w_in, stages=()):
    tm = 512

    def body(x_ref, g_ref, w_ref, proj_ref, h_ref):
        @pl.when(pl.program_id(1) == 0)
        def _():
            xv = x_ref[...]
            r = lax.rsqrt(_mean(xv * xv) + NORM_EPS)
            h_ref[...] = ((xv * r) * g_ref[...]).astype(BF)

        proj_ref[...] = jnp.dot(h_ref[...], w_ref[0], preferred_element_type=F32)

    return _call(
        body, name="fwd_inproj", grid=(T // tm, NCHIP),
        in_specs=[pl.BlockSpec((tm, D), lambda i, k: (i, 0)),
                  pl.BlockSpec((1, D), lambda i, k: (0, 0)),
                  pl.BlockSpec((1, D, CW_IN), lambda i, k: (k, 0, 0))],
        out_specs=[pl.BlockSpec((tm, CW_IN), lambda i, k: (i, k)),
                   pl.BlockSpec((tm, D), lambda i, k: (i, 0))],
        out_shape=[jax.ShapeDtypeStruct((T, DIN), F32), jax.ShapeDtypeStruct((T, D), BF)],
        vmem=40, args=[x, g1, w_in], stages=stages)


def _vec_spec():
    return pl.BlockSpec((1, CB), lambda j: (0, j))


def _fwd_lru(proj, conv_w, conv_b, wa, ba, wx, bx, lam, stages=()):
    def body(xp_ref, g_ref, cw_ref, cb_ref, wa_ref, ba_ref, wx_ref, bx_ref, lam_ref, y_ref, h_ref):
        xc, _, _, _ = _conv(xp_ref[...], cw_ref[...], cb_ref[...])
        _, _, ii, _, a, mult = _lru_gates(xc, wa_ref[0], ba_ref[...], wx_ref[0], bx_ref[...], lam_ref[...])
        b = mult * (ii * xc)
        s = 1
        while s < T:
            b = b + a * _sd(b, s, 0.0)
            if 2 * s < T:
                a = a * _sd(a, s, 1.0)
            s *= 2
        h_ref[...] = b
        gel, _ = _gelu_parts(g_ref[...])
        y_ref[...] = (b * gel).astype(BF)

    return _call(
        body, name="fwd_lru", grid=(NG,),
        in_specs=[pl.BlockSpec((T, CB), lambda j: (0, j)),
                  pl.BlockSpec((T, CB), lambda j: (0, NG + j)),
                  pl.BlockSpec((4, CB), lambda j: (0, j)),
                  _vec_spec(),
                  pl.BlockSpec((1, CB, CB), lambda j: (j, 0, 0)), _vec_spec(),
                  pl.BlockSpec((1, CB, CB), lambda j: (j, 0, 0)), _vec_spec(),
                  _vec_spec()],
        out_specs=[pl.BlockSpec((T, CB), lambda j: (0, j)), pl.BlockSpec((T, CB), lambda j: (0, j))],
        out_shape=[jax.ShapeDtypeStruct((T, DR), BF), jax.ShapeDtypeStruct((T, DR), F32)],
        vmem=48, args=[proj, proj, conv_w, conv_b, wa, ba, wx, bx, lam], stages=stages)


def _pool_cnt(w):
    t = lax.broadcasted_iota(jnp.int32, (T, 1), 0)
    return jnp.minimum(t + 1, w).astype(F32)


def _fwd_pool(proj, pool_w, pool_scale):
    def body(xp_ref, pw_ref, sc_ref, y_ref):
        for g, w in enumerate(POOL_WINDOWS):
            cols = slice(g * PG, (g + 1) * PG)
            x = xp_ref[:, cols]
            p = _pool_window(x, g + 1, _sd) / _pool_cnt(w) - x
            y_ref[:, cols] = (_mm(p, pw_ref[g]) * sc_ref[:, cols]).astype(BF)

    return pl.pallas_call(
        body, name="fwd_pool", grid=(1,),
        in_specs=[pl.BlockSpec((T, DP), lambda i: (0, 2 * DR // DP)),
                  pl.BlockSpec((4, PG, PG), lambda i: (0, 0, 0)),
                  pl.BlockSpec((1, DP), lambda i: (0, 0))],
        out_specs=pl.BlockSpec((T, DP), lambda i: (0, 0)),
        out_shape=pltpu.HBM((T, DP), BF),
        compiler_params=_cp(48),
    )(*_hbm(proj, pool_w, pool_scale))


GATE_BLK = 512
GATE_BLK0 = (2 * DR + DP) // GATE_BLK


def _gate_specs(tm):
    return [pl.BlockSpec((tm, GATE_BLK), functools.partial(lambda i, q: (i, GATE_BLK0 + q), q=q))
            for q in range(4)]


def _fwd_merge(x, ylru, ypool, proj, b_gate, g2, g3, w_lru_up, w_pool_up, w_o, stages=()):
    tm = 512

    def body(x_ref, yl_ref, yp_ref, p0, p1, p2, p3, bg_ref, g2_ref, g3_ref, wl_ref, wp_ref, wo_ref,
             x2_ref, h2_ref, m_ref, mrg_ref, bra_ref, brb_ref):
        bra = jnp.dot(yl_ref[...], wl_ref[...], preferred_element_type=F32)
        yp = yp_ref[...]
        brb = jnp.concatenate([jnp.dot(yp, wp_ref[k], preferred_element_type=F32) for k in range(NCHIP)], axis=1)
        bg = bg_ref[...]
        ga = _sigmoid(jnp.concatenate([p0[...], p1[...]], axis=1) + bg[:, :D])
        gb = _sigmoid(jnp.concatenate([p2[...], p3[...]], axis=1) + bg[:, D:])
        mrg = (ga * bra + gb * brb).astype(BF)
        m = jnp.dot(mrg, wo_ref[...], preferred_element_type=F32)
        r2 = lax.rsqrt(_mean(m * m) + NORM_EPS)
        x2 = x_ref[...] + (m * r2) * g2_ref[...]
        r3 = lax.rsqrt(_mean(x2 * x2) + NORM_EPS)
        x2_ref[...] = x2
        h2_ref[...] = ((x2 * r3) * g3_ref[...]).astype(BF)
        m_ref[...] = m
        mrg_ref[...] = mrg
        bra_ref[...] = bra.astype(BF)
        brb_ref[...] = brb.astype(BF)

    row = lambda w: pl.BlockSpec((tm, w), lambda i: (i, 0))
    full2 = lambda a, b: pl.BlockSpec((a, b), lambda i: (0, 0))
    return _call(
        body, name="fwd_merge", grid=(T // tm,),
        in_specs=[row(D), row(DR), row(DP)] + _gate_specs(tm) +
                 [full2(1, 2 * D), full2(1, D), full2(1, D), full2(DR, D),
                  pl.BlockSpec((NCHIP, DP, D // NCHIP), lambda i: (0, 0, 0)), full2(D, D)],
        out_specs=[row(D)] * 6,
        out_shape=[jax.ShapeDtypeStruct((T, D), F32), jax.ShapeDtypeStruct((T, D), BF),
                   jax.ShapeDtypeStruct((T, D), F32), jax.ShapeDtypeStruct((T, D), BF),
                   jax.ShapeDtypeStruct((T, D), BF), jax.ShapeDtypeStruct((T, D), BF)],
        vmem=48, args=[x, ylru, ypool, proj, proj, proj, proj, b_gate, g2, g3, w_lru_up, w_pool_up, w_o],
        stages=stages)


def _fwd_mlp(h2, w_ff1, w_ff2):
    tm = 512
    fk = DF // NCHIP

    def body(h_ref, w1_ref, w2_ref, a1_ref, f_ref):
        h = h_ref[...]
        f = None
        for k in range(NCHIP):
            a1 = jnp.maximum(jnp.dot(h, w1_ref[k], preferred_element_type=F32), 0.0)
            a1_ref[:, k * fk:(k + 1) * fk] = a1.astype(BF)
            part = jnp.dot((a1 * a1).astype(BF), w2_ref[k * fk:(k + 1) * fk, :], preferred_element_type=F32)
            f = part if f is None else f + part
        f_ref[...] = f

    return pl.pallas_call(
        body, name="fwd_mlp", grid=(T // tm,),
        in_specs=[pl.BlockSpec((tm, D), lambda i: (i, 0)),
                  pl.BlockSpec((NCHIP, D, fk), lambda i: (0, 0, 0)),
                  pl.BlockSpec((DF, D), lambda i: (0, 0))],
        out_specs=[pl.BlockSpec((tm, DF), lambda i: (i, 0)), pl.BlockSpec((tm, D), lambda i: (i, 0))],
        out_shape=_hbm_out([jax.ShapeDtypeStruct((T, DF), BF), jax.ShapeDtypeStruct((T, D), F32)]),
        compiler_params=_cp(56),
    )(*_hbm(h2, w_ff1, w_ff2))


def _loss_head(f, x2, target, g4):
    tm = 512

    def body(f_ref, x2_ref, t_ref, g_ref, loss_ref, dy_ref, df_ref, dg_ref):
        first = pl.program_id(0) == 0
        f = f_ref[...]
        g4v = g_ref[...]
        r4 = lax.rsqrt(_mean(f * f) + NORM_EPS)
        fn = f * r4
        e = (x2_ref[...] + fn * g4v) - t_ref[...]
        _acc(loss_ref, jnp.sum(_mean(e * e), axis=0, keepdims=True), first)
        dy = e * (1.0 / D)
        dy_ref[...] = dy
        _acc(dg_ref, _colsum(dy * fn), first)
        dfn = dy * g4v
        df_ref[...] = (r4 * (dfn - fn * _mean(dfn * fn))).astype(BF)

    row = pl.BlockSpec((tm, D), lambda i: (i, 0))
    return pl.pallas_call(
        body, name="loss_head", grid=(T // tm,),
        in_specs=[row, row, row, pl.BlockSpec((1, D), lambda i: (0, 0))],
        out_specs=[pl.BlockSpec((1, 1), lambda i: (0, 0)), row, row, pl.BlockSpec((1, D), lambda i: (0, 0))],
        out_shape=_hbm_out([jax.ShapeDtypeStruct((1, 1), F32), jax.ShapeDtypeStruct((T, D), F32),
                            jax.ShapeDtypeStruct((T, D), BF), jax.ShapeDtypeStruct((1, D), F32)]),
        compiler_params=_cp(48),
    )(*_hbm(f, x2, target, g4))


def _bwd_mlp_x(df, a1, w_ff1, w_ff2):
    tm = 512
    fk = DF // NCHIP

    def body(df_ref, a1_ref, w1_ref, w2_ref, dh_ref, df1_ref):
        df = df_ref[...]
        dh = None
        for k in range(NCHIP):
            cols = slice(k * fk, (k + 1) * fk)
            dact = _mm_nt(df, w2_ref[cols, :])
            df1 = (dact * (2.0 * a1_ref[:, cols].astype(F32))).astype(BF)
            df1_ref[:, cols] = df1
            part = _mm_nt(df1, w1_ref[k])
            dh = part if dh is None else dh + part
        dh_ref[...] = dh

    return pl.pallas_call(
        body, name="bwd_mlp_x", grid=(T // tm,),
        in_specs=[pl.BlockSpec((tm, D), lambda i: (i, 0)),
                  pl.BlockSpec((tm, DF), lambda i: (i, 0)),
                  pl.BlockSpec((NCHIP, D, fk), lambda i: (0, 0, 0)),
                  pl.BlockSpec((DF, D), lambda i: (0, 0))],
        out_specs=[pl.BlockSpec((tm, D), lambda i: (i, 0)), pl.BlockSpec((tm, DF), lambda i: (i, 0))],
        out_shape=_hbm_out([jax.ShapeDtypeStruct((T, D), F32), jax.ShapeDtypeStruct((T, DF), BF)]),
        compiler_params=_cp(56),
    )(*_hbm(df, a1, w_ff1, w_ff2))


def _bwd_mlp_w(df, h2, a1, df1):
    fc = 512
    per = (DF // NCHIP) // fc

    def body(df_ref, h_ref, a1_ref, df1_ref, dw1_ref, dw2_ref):
        a1 = a1_ref[...].astype(F32)
        dw2_ref[...] = _mm_tn((a1 * a1).astype(BF), df_ref[...]).astype(BF)
        dw1_ref[0] = _mm_tn(h_ref[...], df1_ref[...]).astype(BF)

    return pl.pallas_call(
        body, name="bwd_mlp_w", grid=(DF // fc,),
        in_specs=[pl.BlockSpec((T, D), lambda j: (0, 0)),
                  pl.BlockSpec((T, D), lambda j: (0, 0)),
                  pl.BlockSpec((T, fc), lambda j: (0, j)),
                  pl.BlockSpec((T, fc), lambda j: (0, j))],
        out_specs=[pl.BlockSpec((1, D, fc), lambda j: (j // per, 0, j % per)),
                   pl.BlockSpec((fc, D), lambda j: (j, 0))],
        out_shape=_hbm_out([jax.ShapeDtypeStruct((NCHIP, D, DF // NCHIP), BF),
                            jax.ShapeDtypeStruct((DF, D), BF)]),
        compiler_params=_cp(56),
    )(*_hbm(df, h2, a1, df1))


def _bwd_merge(dh2, dy, x2, m, bra, brb, proj, b_gate, g2, g3, w_lru_up, w_pool_up, w_o, stages=()):
    tm = 256
    cpu = D // NCHIP

    def body(dh2_ref, dy_ref, x2_ref, m_ref, bra_ref, brb_ref, p0, p1, p2, p3, bg_ref,
             g2_ref, g3_ref, wl_ref, wp_ref, wo_ref,
             dx_ref, dgt_ref, dyl_ref, dyp_ref, dm_ref, dbra_ref, dbrb_ref, dg2_ref, dg3_ref, dbg_ref):
        first = pl.program_id(0) == 0
        x2 = x2_ref[...]
        r3 = lax.rsqrt(_mean(x2 * x2) + NORM_EPS)
        x2n = x2 * r3
        dh2 = dh2_ref[...]
        t3 = dh2 * g3_ref[...]
        dx2 = dy_ref[...] + r3 * (t3 - x2n * _mean(t3 * x2n))
        dx_ref[...] = dx2
        _acc(dg3_ref, _colsum(dh2 * x2n), first)
        m = m_ref[...]
        r2 = lax.rsqrt(_mean(m * m) + NORM_EPS)
        mn = m * r2
        _acc(dg2_ref, _colsum(dx2 * mn), first)
        dmn = dx2 * g2_ref[...]
        dm = (r2 * (dmn - mn * _mean(dmn * mn))).astype(BF)
        dm_ref[...] = dm
        dmrg = _mm_nt(dm, wo_ref[...])
        bg = bg_ref[...]
        ga = _sigmoid(jnp.concatenate([p0[...], p1[...]], axis=1) + bg[:, :D])
        gb = _sigmoid(jnp.concatenate([p2[...], p3[...]], axis=1) + bg[:, D:])
        dga = dmrg * bra_ref[...].astype(F32) * (ga * (1.0 - ga))
        dgb = dmrg * brb_ref[...].astype(F32) * (gb * (1.0 - gb))
        dgt_ref[:, :D] = dga.astype(BF)
        dgt_ref[:, D:] = dgb.astype(BF)
        _acc(dbg_ref, jnp.concatenate([_colsum(dga), _colsum(dgb)], axis=1), first)
        dbra = (dmrg * ga).astype(BF)
        dbrb = (dmrg * gb).astype(BF)
        dbra_ref[...] = dbra
        dbrb_ref[...] = dbrb
        dyl_ref[...] = _mm_nt(dbra, wl_ref[...])
        dyp = None
        for k in range(NCHIP):
            part = _mm_nt(dbrb[:, k * cpu:(k + 1) * cpu], wp_ref[k])
            dyp = part if dyp is None else dyp + part
        dyp_ref[...] = dyp

    row = lambda w: pl.BlockSpec((tm, w), lambda i: (i, 0))
    full2 = lambda a, b: pl.BlockSpec((a, b), lambda i: (0, 0))
    wp_spec = pl.BlockSpec((NCHIP, DP, cpu), lambda i: (0, 0, 0))
    return _call(
        body, name="bwd_merge", grid=(T // tm,),
        in_specs=[row(D)] * 6 + _gate_specs(tm) +
                 [full2(1, 2 * D), full2(1, D), full2(1, D), full2(DR, D), wp_spec, full2(D, D)],
        out_specs=[row(D), row(2 * D), row(DR), row(DP), row(D), row(D), row(D),
                   full2(1, D), full2(1, D), full2(1, 2 * D)],
        out_shape=[jax.ShapeDtypeStruct((T, D), F32), jax.ShapeDtypeStruct((T, 2 * D), BF),
                   jax.ShapeDtypeStruct((T, DR), F32), jax.ShapeDtypeStruct((T, DP), F32),
                   jax.ShapeDtypeStruct((T, D), BF), jax.ShapeDtypeStruct((T, D), BF),
                   jax.ShapeDtypeStruct((T, D), BF),
                   jax.ShapeDtypeStruct((1, D), F32), jax.ShapeDtypeStruct((1, D), F32),
                   jax.ShapeDtypeStruct((1, 2 * D), F32)],
        vmem=56, args=[dh2, dy, x2, m, bra, brb, proj, proj, proj, proj, b_gate, g2, g3, w_lru_up, w_pool_up, w_o],
        stages=stages)


def _dw_merge(mrg, dm, ylru, dbra, ypool, dbrb, stages=()):
    nb = NCHIP
    rb, pb, cpu = D // nb, DP // nb, D // NCHIP

    def body(mrg_ref, dm_ref, yl_ref, dbra_ref, yp_ref, dbrb_ref, dwo_ref, dwl_ref, dwp_ref):
        dwo_ref[...] = _mm_tn(mrg_ref[...], dm_ref[...]).astype(BF)
        dwl_ref[...] = _mm_tn(yl_ref[...], dbra_ref[...]).astype(BF)
        dwp = _mm_tn(yp_ref[...], dbrb_ref[...]).astype(BF)
        for k in range(NCHIP):
            dwp_ref[k] = dwp[:, k * cpu:(k + 1) * cpu]

    cols = lambda w: pl.BlockSpec((T, w), lambda r: (0, r))
    whole = pl.BlockSpec((T, D), lambda r: (0, 0))
    return _call(
        body, name="dw_merge", grid=(nb,),
        in_specs=[cols(rb), whole, cols(rb), whole, cols(pb), whole],
        out_specs=[pl.BlockSpec((rb, D), lambda r: (r, 0)), pl.BlockSpec((rb, D), lambda r: (r, 0)),
                   pl.BlockSpec((NCHIP, pb, cpu), lambda r: (0, r, 0))],
        out_shape=[jax.ShapeDtypeStruct((D, D), BF), jax.ShapeDtypeStruct((DR, D), BF),
                   jax.ShapeDtypeStruct((NCHIP, DP, cpu), BF)],
        vmem=56, args=[mrg, dm, ylru, dbra, ypool, dbrb], stages=stages)


def _bwd_lru(proj, h, dylru, conv_w, conv_b, wa, ba, wx, bx, lam, stages=()):
    def body(xp_ref, g_ref, h_ref, dy_ref, cw_ref, cb_ref, wa_ref, ba_ref, wx_ref, bx_ref, lam_ref,
             dxp_ref, dg_ref, dcw_ref, dcb_ref, dwa_ref, dba_ref, dwx_ref, dbx_ref, dlam_ref):
        xp = xp_ref[...]
        cw = cw_ref[...]
        lam = lam_ref[...]
        xc, x1, x2, x3 = _conv(xp, cw, cb_ref[...])
        wa, wx = wa_ref[0], wx_ref[0]
        xcb, r, ii, sp, a, mult = _lru_gates(xc, wa, ba_ref[...], wx, bx_ref[...], lam)
        g = g_ref[...]
        gel, dgel = _gelu_parts(g)
        h = h_ref[...]
        dy = dy_ref[...]
        dg_ref[...] = (dy * h * dgel).astype(BF)
        b = dy * gel
        aa = _su(a, 1, 0.0)
        s = 1
        while s < T:
            b = b + aa * _su(b, s, 0.0)
            if 2 * s < T:
                aa = aa * _su(aa, s, 0.0)
            s *= 2
        da = b * _sd(h, 1, 0.0)
        dmult = b * (ii * xc)
        dii = b * (mult * xc)
        dxc = b * (mult * ii)
        dla = da * a - dmult * ((a * a) / mult)
        dr = dla * ((-LRU_C) * sp)
        dsp = _colsum(dla * ((-LRU_C) * r))
        dlam_ref[...] = -dsp / (1.0 + jnp.exp(lam))
        dzr = dr * (r * (1.0 - r))
        dzi = dii * (ii * (1.0 - ii))
        dzrb, dzib = dzr.astype(BF), dzi.astype(BF)
        dxc = dxc + _mm_nt(dzrb, wa) + _mm_nt(dzib, wx)
        dwa_ref[0] = _mm_tn(xcb, dzrb)
        dwx_ref[0] = _mm_tn(xcb, dzib)
        dba_ref[...] = _colsum(dzr)
        dbx_ref[...] = _colsum(dzi)
        dcb_ref[...] = _colsum(dxc)
        dcw_ref[...] = jnp.concatenate([_colsum(dxc * x3), _colsum(dxc * x2), _colsum(dxc * x1),
                                        _colsum(dxc * xp)], axis=0)
        dxp = cw[3:4] * dxc + cw[2:3] * _su(dxc, 1) + cw[1:2] * _su(dxc, 2) + cw[0:1] * _su(dxc, 3)
        dxp_ref[...] = dxp.astype(BF)

    blk = pl.BlockSpec((T, CB), lambda j: (0, j))
    wsp = pl.BlockSpec((1, CB, CB), lambda j: (j, 0, 0))
    return _call(
        body, name="bwd_lru", grid=(NG,),
        in_specs=[blk, pl.BlockSpec((T, CB), lambda j: (0, NG + j)), blk, blk,
                  pl.BlockSpec((4, CB), lambda j: (0, j)), _vec_spec(), wsp, _vec_spec(), wsp, _vec_spec(),
                  _vec_spec()],
        out_specs=[blk, blk, pl.BlockSpec((4, CB), lambda j: (0, j)), _vec_spec(), wsp, _vec_spec(), wsp,
                   _vec_spec(), _vec_spec()],
        out_shape=[jax.ShapeDtypeStruct((T, DR), BF), jax.ShapeDtypeStruct((T, DR), BF),
                   jax.ShapeDtypeStruct((4, DR), F32), jax.ShapeDtypeStruct((1, DR), F32),
                   jax.ShapeDtypeStruct((NG, CB, CB), F32), jax.ShapeDtypeStruct((1, DR), F32),
                   jax.ShapeDtypeStruct((NG, CB, CB), F32), jax.ShapeDtypeStruct((1, DR), F32),
                   jax.ShapeDtypeStruct((1, DR), F32)],
        vmem=56, args=[proj, proj, h, dylru, conv_w, conv_b, wa, ba, wx, bx, lam], stages=stages)


def _bwd_pool(proj, dypool, pool_w, pool_scale):
    def body(xp_ref, dy_ref, pw_ref, sc_ref, dx_ref, dw_ref, dsc_ref):
        for g, w in enumerate(POOL_WINDOWS):
            cols = slice(g * PG, (g + 1) * PG)
            cnt = _pool_cnt(w)
            x = xp_ref[:, cols]
            pb = (_pool_window(x, g + 1, _sd) / cnt - x).astype(BF)
            wg = pw_ref[g]
            dy = dy_ref[:, cols]
            dsc_ref[:, cols] = _colsum(dy * _mm(pb, wg))
            dyp = (dy * sc_ref[:, cols]).astype(BF)
            dw_ref[g] = _mm_tn(pb, dyp)
            dp = _mm_nt(dyp, wg)
            dx_ref[:, cols] = (_pool_window(dp / cnt, g + 1, _su) - dp).astype(BF)

    return pl.pallas_call(
        body, name="bwd_pool", grid=(1,),
        in_specs=[pl.BlockSpec((T, DP), lambda i: (0, 2 * DR // DP)),
                  pl.BlockSpec((T, DP), lambda i: (0, 0)),
                  pl.BlockSpec((4, PG, PG), lambda i: (0, 0, 0)),
                  pl.BlockSpec((1, DP), lambda i: (0, 0))],
        out_specs=[pl.BlockSpec((T, DP), lambda i: (0, 0)),
                   pl.BlockSpec((4, PG, PG), lambda i: (0, 0, 0)),
                   pl.BlockSpec((1, DP), lambda i: (0, 0))],
        out_shape=_hbm_out([jax.ShapeDtypeStruct((T, DP), BF), jax.ShapeDtypeStruct((4, PG, PG), F32),
                            jax.ShapeDtypeStruct((1, DP), F32)]),
        compiler_params=_cp(48),
    )(*_hbm(proj, dypool, pool_w, pool_scale))


def _bwd_inproj(h1, dproj, w_in, stages=()):
    def body(h_ref, dp_ref, w_ref, dw_ref, dh_ref):
        dp = dp_ref[...]
        dw_ref[0] = _mm_tn(h_ref[...], dp).astype(BF)
        _acc(dh_ref, _mm_nt(dp, w_ref[0]), pl.program_id(0) == 0)

    return _call(
        body, name="bwd_inproj", grid=(NCHIP,),
        in_specs=[pl.BlockSpec((T, D), lambda k: (0, 0)),
                  pl.BlockSpec((T, CW_IN), lambda k: (0, k)),
                  pl.BlockSpec((1, D, CW_IN), lambda k: (k, 0, 0))],
        out_specs=[pl.BlockSpec((1, D, CW_IN), lambda k: (k, 0, 0)), pl.BlockSpec((T, D), lambda k: (0, 0))],
        out_shape=[jax.ShapeDtypeStruct((NCHIP, D, CW_IN), BF), jax.ShapeDtypeStruct((T, D), F32)],
        vmem=56, args=[h1, dproj, w_in], stages=stages)


def _bwd_prenorm(x, dh1, dxres, g1, stages=()):
    tm = 512

    def body(x_ref, dh_ref, dr_ref, g_ref, dx_ref, dg_ref):
        xv = x_ref[...]
        r = lax.rsqrt(_mean(xv * xv) + NORM_EPS)
        xn = xv * r
        dh = dh_ref[...]
        t = dh * g_ref[...]
        dx_ref[...] = dr_ref[...] + r * (t - xn * _mean(t * xn))
        _acc(dg_ref, _colsum(dh * xn), pl.program_id(0) == 0)

    row = pl.BlockSpec((tm, D), lambda i: (i, 0))
    vec = pl.BlockSpec((1, D), lambda i: (0, 0))
    return _call(
        body, name="bwd_prenorm", grid=(T // tm,),
        in_specs=[row, row, row, vec], out_specs=[row, vec],
        out_shape=[jax.ShapeDtypeStruct((T, D), F32), jax.ShapeDtypeStruct((1, D), F32)],
        vmem=48, args=[x, dh1, dxres, g1], stages=stages)


def _place():
    x, y, c = lax.axis_index("x"), lax.axis_index("y"), lax.axis_index("c")
    chips = [(1 - x, y), (x, 1 - y), (1 - x, 1 - y)]
    return x, y, c, chips


def _rcopy(src, dst, ssem, rsem, dev):
    return pltpu.make_async_remote_copy(src_ref=src, dst_ref=dst, send_sem=ssem, recv_sem=rsem,
                                        device_id=dev, device_id_type=MESH_ID)


def _sds(a):
    return jax.ShapeDtypeStruct(a.shape, a.dtype)


def _sem2(n, m):
    return [pltpu.SemaphoreType.DMA((n, m)), pltpu.SemaphoreType.DMA((n, m))]


ALL = (0, 1, 1)


def _piece(ref, k, half, part):
    hr = ref.shape[1] // 2
    r0, r1 = hr * part[0] // part[2], hr * part[1] // part[2]
    return ref.at[k, pl.ds(half * hr + r0, r1 - r0), :]


def _gather(fulls, ici=(), d2d=()):
    n = len(fulls)
    ici, d2d = list(ici), list(d2d)

    def copies(outs, sems):
        x, y, c, chips = _place()
        me = 2 * x + y
        sib = (x, y, 1 - c)
        send, recv = [], []
        for q, (i, part) in enumerate(ici):
            for j, chip in enumerate(chips):
                mine, theirs = _piece(outs[i], me, c, part), _piece(outs[i], 2 * chip[0] + chip[1], c, part)
                send.append(_rcopy(mine, mine, sems[0].at[q, j], sems[1].at[q, j], (*chip, c)))
                recv.append(_rcopy(theirs, theirs, sems[0].at[q, j], sems[1].at[q, j], (*chip, c)))
        for q, (i, part) in enumerate(d2d):
            for j, chip in enumerate(chips):
                k = 2 * chip[0] + chip[1]
                got, other = _piece(outs[i], k, c, part), _piece(outs[i], k, 1 - c, part)
                send.append(_rcopy(got, got, sems[2].at[q, j], sems[3].at[q, j], sib))
                recv.append(_rcopy(other, other, sems[2].at[q, j], sems[3].at[q, j], sib))
        return send, recv

    def start(ins, outs, sems):
        for cp in copies(outs, sems)[0]:
            cp.start()

    def finish(ins, outs, sems):
        send, recv = copies(outs, sems)
        for cp in recv:
            cp.wait_recv()
        for cp in send:
            cp.wait_send()

    sems = _sem2(max(len(ici), 1), 3) + _sem2(max(len(d2d), 1), 3)
    return _Stage(fulls, [_sds(f) for f in fulls], {i: i for i in range(n)}, sems, start, finish)


def _gather_first(full, conv_w):
    d2d = _gather([full], d2d=[(0, ALL)])

    def body(full_in, cw_in, full_out, cw_out, s0, r0, s1, r1, cs, cr):
        x, y, c, chips = _place()
        me = 2 * x + y
        conv = [_rcopy(cw_in, cw_out.at[me], cs.at[j], cr.at[j], (*chip, c)) for j, chip in enumerate(chips)]
        for cp in conv:
            cp.start()
        sems = [s0, r0, s1, r1]
        d2d.start(None, [full_out], sems)
        d2d.finish(None, [full_out], sems)
        for j, chip in enumerate(chips):
            _rcopy(cw_in, cw_out.at[2 * chip[0] + chip[1]], cs.at[j], cr.at[j], (*chip, c)).wait_recv()
        for cp in conv:
            cp.wait_send()

    return pl.pallas_call(
        body, name="gather_first",
        in_specs=[ANY, ANY], out_specs=[ANY, ANY],
        out_shape=_hbm_out([full, jax.ShapeDtypeStruct((NCHIP,) + conv_w.shape, conv_w.dtype)]),
        input_output_aliases={0: 0},
        scratch_shapes=_sem2(1, 3) + _sem2(1, 3) + [pltpu.SemaphoreType.DMA((3,)), pltpu.SemaphoreType.DMA((3,))],
        compiler_params=pltpu.CompilerParams(has_side_effects=True),
    )(*_hbm(full, conv_w))


def _comm_only(name, stage):
    def body(*refs):
        ni, no = len(stage.operands), len(stage.out_shape)
        stage.start(refs[:ni], refs[ni:ni + no], refs[ni + no:])
        stage.finish(refs[:ni], refs[ni:ni + no], refs[ni + no:])

    return pl.pallas_call(
        body, name=name, in_specs=[ANY] * len(stage.operands), out_specs=[ANY] * len(stage.out_shape),
        out_shape=_hbm_out(stage.out_shape), input_output_aliases=stage.alias, scratch_shapes=stage.sems,
        compiler_params=pltpu.CompilerParams(has_side_effects=True),
    )(*_hbm(*stage.operands))


def _to_sibling(srcs):
    n = len(srcs)

    def copies(ins, outs, sems):
        x, y, c, _ = _place()
        sib = (x, y, 1 - c)
        return [_rcopy(ins[i].at[:, 1 - c] if srcs[i].ndim == 4 else ins[i], outs[i], sems[0].at[i], sems[1].at[i], sib)
                for i in range(n)]

    def start(ins, outs, sems):
        for cp in copies(ins, outs, sems):
            cp.start()

    def finish(ins, outs, sems):
        for cp in copies(ins, outs, sems):
            cp.wait()

    shapes = [jax.ShapeDtypeStruct((NCHIP,) + s.shape[2:] if s.ndim == 4 else s.shape, s.dtype) for s in srcs]
    return _Stage(srcs, shapes, {}, [pltpu.SemaphoreType.DMA((n,)), pltpu.SemaphoreType.DMA((n,))], start, finish)


def _to_chips(srcs, parts=None, lands=None):
    n = len(srcs)
    parts = [ALL] * n if parts is None else parts
    lands = [None] * n if lands is None else lands
    given = [i for i in range(n) if lands[i] is not None]

    def rows(ref, i):
        hr = srcs[i].shape[1]
        r0, r1 = hr * parts[i][0] // parts[i][2], hr * parts[i][1] // parts[i][2]
        return ref.at[pl.ds(r0, r1 - r0), :]

    def copies(ins, outs, sems):
        x, y, c, chips = _place()
        me = 2 * x + y
        return [_rcopy(rows(ins[i].at[2 * chip[0] + chip[1]] if srcs[i].shape[0] == NCHIP else ins[i].at[c], i),
                       rows(outs[i].at[me], i), sems[0].at[i, j], sems[1].at[i, j], (*chip, c))
                for i in range(n) for j, chip in enumerate(chips)]

    def start(ins, outs, sems):
        for cp in copies(ins, outs, sems):
            cp.start()

    def finish(ins, outs, sems):
        for cp in copies(ins, outs, sems):
            cp.wait()

    shapes = [jax.ShapeDtypeStruct((NCHIP,) + s.shape[1:], s.dtype) for s in srcs]
    alias = {n + q: i for q, i in enumerate(given)}
    return _Stage(list(srcs) + [lands[i] for i in given], shapes, alias, _sem2(n, 3), start, finish)


HBM_REF = pl.BlockSpec(memory_space=pltpu.HBM)
SEM_REF = pl.BlockSpec(memory_space=pltpu.SEMAPHORE)
DATAFLOW = pltpu.SideEffectType.DATAFLOW_SIDE_EFFECTING


def _last_copies(p_ref, land_ref, ssem, rsem):
    x, y, c, chips = _place()
    me = 2 * x + y
    send = [_rcopy(p_ref.at[2 * chip[0] + chip[1]], land_ref.at[me], ssem.at[j], rsem.at[j], (*chip, c))
            for j, chip in enumerate(chips)]
    recv = [_rcopy(p_ref.at[2 * chip[0] + chip[1]], land_ref.at[2 * chip[0] + chip[1]], ssem.at[j], rsem.at[j],
                   (*chip, c)) for j, chip in enumerate(chips)]
    return send, recv


def _chips_start(p):
    def body(p_ref, land_ref, ssem, rsem, p_thru, land_thru, token):
        for cp in _last_copies(p_ref, land_ref, ssem, rsem)[0]:
            cp.start()
        token[...] = jnp.zeros_like(token)

    return pl.pallas_call(
        body, name="reduce_last_start",
        out_shape=(pltpu.SemaphoreType.DMA((3,)), pltpu.SemaphoreType.DMA((3,)), pltpu.HBM(p.shape, p.dtype),
                   pltpu.HBM(p.shape, p.dtype), jax.ShapeDtypeStruct((8, LANE), F32)),
        in_specs=(HBM_REF, HBM_REF),
        out_specs=(SEM_REF, SEM_REF, HBM_REF, HBM_REF, pl.BlockSpec(memory_space=pltpu.VMEM)),
        input_output_aliases={0: 2, 1: 3},
        compiler_params=pltpu.CompilerParams(has_side_effects=DATAFLOW),
    )(*_hbm(p, lax.empty(p.shape, p.dtype)))


def _chips_wait(ssem, rsem, p_thru, land_thru, after):
    def body(p_ref, land_ref, ssem, rsem, after_ref, p_dead, got_ref):
        send, recv = _last_copies(p_ref, land_ref, ssem, rsem)
        for cp in send:
            cp.wait_send()
        for cp in recv:
            cp.wait_recv()

    return pl.pallas_call(
        body, name="reduce_last_wait",
        out_shape=(pltpu.HBM(p_thru.shape, p_thru.dtype), pltpu.HBM(land_thru.shape, land_thru.dtype)),
        in_specs=(HBM_REF, HBM_REF, SEM_REF, SEM_REF, pl.BlockSpec(memory_space=pl.ANY)),
        out_specs=(HBM_REF, HBM_REF), input_output_aliases={0: 0, 1: 1},
        compiler_params=pltpu.CompilerParams(has_side_effects=DATAFLOW),
    )(p_thru, land_thru, ssem, rsem, after)


def _share(pairs):
    n = len(pairs)

    def start(ins, outs, sems):
        x, y, c, _ = _place()
        for i in range(n):
            _rcopy(outs[i].at[c], outs[i].at[c], sems[0].at[i], sems[1].at[i], (x, y, 1 - c)).start()

    def finish(ins, outs, sems):
        x, y, c, _ = _place()
        for i in range(n):
            _rcopy(outs[i].at[c], outs[i].at[c], sems[0].at[i], sems[1].at[i], (x, y, 1 - c)).wait_send()
            _rcopy(outs[i].at[1 - c], outs[i].at[1 - c], sems[0].at[i], sems[1].at[i], (x, y, 1 - c)).wait_recv()

    return _Stage(pairs, [_sds(p) for p in pairs], {i: i for i in range(n)},
                  [pltpu.SemaphoreType.DMA((n,)), pltpu.SemaphoreType.DMA((n,))], start, finish)


def _row_block(rows, cols, itemsize=4, target=2 * MIB):
    br = rows
    while br * cols * itemsize > target and br % 32 == 0:
        br //= 2
    return br


def _cast_place(w, chip_idx, name):
    rows, cols = w.shape
    br = _row_block(rows, cols)

    def body(k_ref, w_ref, o_ref):
        o_ref[0] = w_ref[...].astype(BF)

    return _call(
        body, name=name, grid=(rows // br,), prefetch=chip_idx,
        in_specs=[pl.BlockSpec((br, cols), lambda r, k: (r, 0))],
        out_specs=[pl.BlockSpec((1, br, cols), lambda r, k: (k[0], r, 0))],
        out_shape=[jax.ShapeDtypeStruct((NCHIP, rows, cols), BF)], vmem=32, args=[w])[0][0]


def _cast_place_multi(ws, chip_idx, stages=()):
    br = 128
    nblk = [a.shape[0] // br for a in ws]
    starts = [sum(nblk[:i]) for i in range(len(ws))]

    def body(k_ref, *refs):
        r = pl.program_id(0)
        for i in range(len(ws)):
            @pl.when(jnp.logical_and(r >= starts[i], r < starts[i] + nblk[i]))
            def _(i=i):
                refs[len(ws) + i][0] = refs[i][...].astype(BF)

    def at(i):
        return functools.partial(lambda r, s, nb: jnp.clip(r - s, 0, nb - 1), s=starts[i], nb=nblk[i])

    outs, landed = _call(
        body, name="cast_rest", grid=(sum(nblk),), prefetch=chip_idx,
        in_specs=[pl.BlockSpec((br, a.shape[1]), functools.partial(lambda r, k, f: (f(r), 0), f=at(i)))
                  for i, a in enumerate(ws)],
        out_specs=[pl.BlockSpec((1, br, a.shape[1]), functools.partial(lambda r, k, f: (k[0], f(r), 0), f=at(i)))
                   for i, a in enumerate(ws)],
        out_shape=[jax.ShapeDtypeStruct((NCHIP,) + a.shape, BF) for a in ws], vmem=32, args=list(ws), stages=stages)
    return outs, landed


def _add_sibling(g, land, cidx, name, stages=()):
    _, _, hr, cols = g.shape
    br = _row_block(hr, cols)

    def body(c_ref, g_ref, l_ref, o_ref):
        o_ref[...] = (g_ref[0, 0].astype(F32) + l_ref[0].astype(F32)).astype(BF)[None]

    outs, st = _call(
        body, name=name, grid=(NCHIP, hr // br), prefetch=cidx,
        in_specs=[pl.BlockSpec((1, 1, br, cols), lambda k, r, c: (k, c[0], r, 0)),
                  pl.BlockSpec((1, br, cols), lambda k, r, c: (k, r, 0))],
        out_specs=[pl.BlockSpec((1, br, cols), lambda k, r, c: (k, r, 0))],
        out_shape=[jax.ShapeDtypeStruct((NCHIP, hr, cols), BF)], vmem=32, args=[g, land], stages=stages)
    return outs[0], st


def _add_pair(a, b, name):
    rows, cols = a.shape

    def body(a_ref, b_ref, o_ref):
        o_ref[...] = a_ref[...] + b_ref[...]

    spec = pl.BlockSpec((rows, cols), lambda r: (0, 0))
    return _call(body, name=name, grid=(1,), in_specs=[spec, spec], out_specs=[spec], out_shape=[_sds(a)],
                 vmem=32, args=[a, b])[0][0]


def _add_chips(own, land, idx, name, stages=None):
    _, hr, cols = land.shape
    br = _row_block(hr, cols)

    def body(s_ref, a_ref, b_ref, c_ref, d_ref, o_ref):
        o_ref[...] = (a_ref[...].astype(F32) + b_ref[...].astype(F32)) + (c_ref[...].astype(F32) +
                                                                           d_ref[...].astype(F32))

    spec = lambda q: pl.BlockSpec((1, br, cols), functools.partial(lambda r, s, q: (s[q], r, 0), q=q))
    outs, landed = _call(
        body, name=name, grid=(hr // br,), prefetch=idx,
        in_specs=[spec(0), spec(1), spec(2), spec(3)], out_specs=[spec(4)],
        out_shape=[jax.ShapeDtypeStruct((2, hr, cols), F32)], vmem=48, args=[own, land, land, land],
        stages=stages or ())
    return outs[0] if stages is None else (outs[0], landed)


def _adamw_math(w, g, m, v):
    mn = ADAM_B1 * m + (1.0 - ADAM_B1) * g
    vn = ADAM_B2 * v + (1.0 - ADAM_B2) * (g * g)
    m_hat = mn / (1.0 - ADAM_B1 ** ADAM_STEP)
    v_hat = vn / (1.0 - ADAM_B2 ** ADAM_STEP)
    return -ADAM_LR * (m_hat / (jnp.sqrt(v_hat) + ADAM_EPS) + ADAM_WD * w), mn, vn


def _adamw(w, g, m, v, name, stages=()):
    rows, cols = w.shape
    br = _row_block(rows, cols)

    def body(w_ref, g_ref, m_ref, v_ref, d_ref, mo_ref, vo_ref):
        d_ref[...], mo_ref[...], vo_ref[...] = _adamw_math(w_ref[...], g_ref[...], m_ref[...], v_ref[...])

    spec = pl.BlockSpec((br, cols), lambda r: (r, 0))
    return _call(body, name=name, grid=(rows // br,), in_specs=[spec] * 4, out_specs=[spec] * 3,
                 out_shape=[_sds(w)] * 3, vmem=48, args=[w, g, m, v], stages=stages)


def _adamw_multi(names, w, g, m, v, stages=()):
    cols = w[names[0]].shape[1]
    br = 128
    nblk = [w[n].shape[0] // br for n in names]
    starts = [sum(nblk[:i]) for i in range(len(names))]

    def body(*refs):
        r = pl.program_id(0)
        for i in range(len(names)):
            w_ref, g_ref, m_ref, v_ref = refs[4 * i:4 * i + 4]
            d_ref, mo_ref, vo_ref = refs[4 * len(names) + 3 * i:4 * len(names) + 3 * i + 3]

            @pl.when(jnp.logical_and(r >= starts[i], r < starts[i] + nblk[i]))
            def _():
                d_ref[...], mo_ref[...], vo_ref[...] = _adamw_math(w_ref[...], g_ref[...], m_ref[...], v_ref[...])

    def spec(i):
        return pl.BlockSpec((br, cols), functools.partial(
            lambda r, s, nb: (jnp.clip(r - s, 0, nb - 1), 0), s=starts[i], nb=nblk[i]))

    outs, landed = _call(
        body, name="adamw_" + "_".join(names), grid=(sum(nblk),),
        in_specs=[spec(i) for i in range(len(names)) for _ in range(4)],
        out_specs=[spec(i) for i in range(len(names)) for _ in range(3)],
        out_shape=[_sds(w[n]) for n in names for _ in range(3)], vmem=48,
        args=[a[n] for n in names for a in (w, g, m, v)], stages=stages)
    return {n: outs[3 * i:3 * i + 3] for i, n in enumerate(names)}, landed


def _to_everyone(v):
    deltas = [(a, b, e) for a in (0, 1) for b in (0, 1) for e in (0, 1)][1:]

    def copies(ins, outs, sems):
        x, y, c, _ = _place()
        me = 4 * x + 2 * y + c
        flip = lambda p, f: 1 - p if f else p
        return [_rcopy(ins[0], outs[0].at[me], sems[0].at[q], sems[1].at[q], (flip(x, a), flip(y, b), flip(c, e)))
                for q, (a, b, e) in enumerate(deltas)]

    def start(ins, outs, sems):
        for cp in copies(ins, outs, sems):
            cp.start()

    def finish(ins, outs, sems):
        for cp in copies(ins, outs, sems):
            cp.wait()

    n = len(deltas)
    return _Stage([v], [jax.ShapeDtypeStruct((2 * NCHIP,) + v.shape, v.dtype)], {},
                  [pltpu.SemaphoreType.DMA((n,)), pltpu.SemaphoreType.DMA((n,))], start, finish)


SMALL_AT = {"norm_mix_pre": (0, 1, D), "norm_mix_post": (1, 1, D), "norm_mlp_pre": (2, 1, D),
            "norm_mlp_post": (3, 1, D), "b_gate": (4, 2, D), "conv_b": (6, 1, D), "lru_b_a": (7, 1, D),
            "lru_b_x": (8, 1, D), "lru_lambda": (9, 1, D), "pool_scale": (10, 1, DP)}
SMALL_SEPARATE = ["conv_w", "lru_w_a", "lru_w_x", "pool_w"]


def _adamw_small(small_sum, first_all, sep_grads, w, m, v):
    packed, sep = list(SMALL_AT), list(SMALL_SEPARATE)
    names = packed + sep

    def body(*refs):
        s_ref, a_ref, refs = refs[0], refs[1], refs[2:]
        g_sep, refs = refs[:len(sep)], refs[len(sep):]
        nn = len(names)
        w_r, m_r, v_r, refs = refs[:nn], refs[nn:2 * nn], refs[2 * nn:3 * nn], refs[3 * nn:]
        g_out, refs = refs[:len(packed)], refs[len(packed):]
        d_o, m_o, v_o = refs[:nn], refs[nn:2 * nn], refs[2 * nn:3 * nn]
        for i, n in enumerate(names):
            if i == 0:
                g = a_ref[0:1, :]
                for q in range(1, 2 * NCHIP):
                    g = g + a_ref[q:q + 1, :]
                g_out[i][...] = g
            elif n in SMALL_AT:
                r0, nr, nc = SMALL_AT[n]
                g = jnp.concatenate([s_ref[r0 + q:r0 + q + 1, :nc] for q in range(nr)], axis=1)
                g_out[i][...] = g
            else:
                g = g_sep[i - len(packed)][...]
            d_o[i][...], m_o[i][...], v_o[i][...] = _adamw_math(w_r[i][...], g, m_r[i][...], v_r[i][...])

    ws = [w[n] for n in names]
    res = pl.pallas_call(
        body, name="adamw_small",
        out_shape=[_sds(w[n]) for n in packed] + [_sds(a) for a in ws] * 3,
        compiler_params=_cp(32),
    )(*_hbm(small_sum, first_all, *sep_grads, *ws, *[m[n] for n in names], *[v[n] for n in names]))
    nn, npk = len(names), len(packed)
    grad = dict(zip(packed, res[:npk]))
    delta = dict(zip(names, res[npk:npk + nn]))
    new_m = dict(zip(names, res[npk + nn:npk + 2 * nn]))
    new_v = dict(zip(names, res[npk + 2 * nn:]))
    return grad, delta, new_m, new_v


W_NAMES = ["norm_mix_pre", "norm_mix_post", "norm_mlp_pre", "norm_mlp_post", "w_in", "b_gate", "conv_w", "conv_b",
           "lru_w_a", "lru_b_a", "lru_w_x", "lru_b_x", "lru_lambda", "pool_w", "pool_scale", "w_lru_up",
           "w_pool_up", "w_o", "w_ff1", "w_ff2"]
BIG = ["w_in", "w_lru_up", "w_pool_up", "w_o", "w_ff1", "w_ff2"]


def _block_diag(w):
    hd = w.shape[-1]
    per = CB // hd
    w4 = w.reshape(NG, per, hd, hd)
    eye = jnp.eye(per, dtype=w.dtype)
    return jnp.einsum("gpij,pq->gpiqj", w4, eye).reshape(NG, CB, CB)


def _block_diag_extract(d, hd):
    per = CB // hd
    d5 = d.reshape(NG, per, hd, per, hd)
    return jnp.stack([d5[:, p, :, p, :] for p in range(per)], axis=1).reshape(NG * per, hd, hd)


def _halves(g):
    return g.reshape(NCHIP, 2, g.size // (g.shape[-1] * 2 * NCHIP), g.shape[-1])


def kernel(x, norm_mix_pre, norm_mix_post, norm_mlp_pre, norm_mlp_post, w_in, b_gate, conv_w, conv_b, lru_w_a, lru_b_a, lru_w_x, lru_b_x, lru_lambda, pool_w, pool_scale, w_lru_up, w_pool_up, w_o, w_ff1, w_ff2, loss_target, m_norm_mix_pre, m_norm_mix_post, m_norm_mlp_pre, m_norm_mlp_post, m_w_in, m_b_gate, m_conv_w, m_conv_b, m_lru_w_a, m_lru_b_a, m_lru_w_x, m_lru_b_x, m_lru_lambda, m_pool_w, m_pool_scale, m_w_lru_up, m_w_pool_up, m_w_o, m_w_ff1, m_w_ff2, v_norm_mix_pre, v_norm_mix_post, v_norm_mlp_pre, v_norm_mlp_post, v_w_in, v_b_gate, v_conv_w, v_conv_b, v_lru_w_a, v_lru_b_a, v_lru_w_x, v_lru_b_x, v_lru_lambda, v_pool_w, v_pool_scale, v_w_lru_up, v_w_pool_up, v_w_o, v_w_ff1, v_w_ff2):
    args = dict(locals())
    two_d = lambda a: a.reshape(-1, a.shape[-1])
    w = {n: two_d(args[n]) for n in W_NAMES}
    mom = {n: two_d(args["m_" + n]) for n in W_NAMES}
    var = {n: two_d(args["v_" + n]) for n in W_NAMES}
    i32 = lambda val: jnp.asarray(val, jnp.int32)
    chip = i32(2 * lax.axis_index("x") + lax.axis_index("y"))
    core = i32(lax.axis_index("c"))
    cidx = core.reshape(1)
    zero = i32(0)
    hd = lru_w_a.shape[-1]
    xs, target = x[0], loss_target[0]
    g1, g2, g3, g4 = norm_mix_pre, norm_mix_post, norm_mlp_pre, norm_mlp_post

    full = {"w_in": _cast_place(w["w_in"], chip.reshape(1), "cast_w_in")}
    casts, ((full["w_in"],),) = _cast_place_multi([w[n] for n in BIG[1:]], chip.reshape(1),
                                                 stages=[_gather([full["w_in"]], ici=[(0, ALL)])])
    full.update(zip(BIG[1:], casts))
    wa = _block_diag(lru_w_a[0]).astype(BF)
    wx = _block_diag(lru_w_x[0]).astype(BF)
    pw = pool_w[0].astype(BF)

    full["w_in"], conv_all = _gather_first(full["w_in"], w["conv_w"])
    conv_all = lax.dynamic_update_slice(conv_all, w["conv_w"][None], (chip, zero, zero))
    conv_full = jnp.transpose(conv_all, (1, 0, 2)).reshape(4, DR)
    mix = ["w_lru_up", "w_pool_up", "w_o"]
    ff1_a, ff1_b, ff2_a, ff2_b = (0, 3, 8), (3, 8, 8), (0, 1, 4), (1, 4, 4)
    (proj, h1), (got,) = _fwd_inproj(xs, g1, full["w_in"], stages=[_gather(
        [full[n] for n in mix] + [full["w_ff1"]], ici=[(0, ALL), (1, ALL), (2, ALL), (3, ff1_a)])])
    (ylru, hs), (got,) = _fwd_lru(proj, conv_full, conv_b, wa, lru_b_a, wx, lru_b_x, lru_lambda, stages=[_gather(
        got + [full["w_ff2"]], d2d=[(0, ALL), (1, ALL), (2, ALL), (3, ff1_a)], ici=[(3, ff1_b), (4, ff2_a)])])
    w_lru_up_f, w_pool_up_f, w_o_f = got[0].reshape(DR, D), got[1], got[2].reshape(D, D)
    ypool = _fwd_pool(proj, pw, pool_scale)
    (x2, h2, m, mrg, bra, brb), ((ff1, ff2),) = _fwd_merge(
        xs, ylru, ypool, proj, b_gate, g2, g3, w_lru_up_f, w_pool_up_f, w_o_f,
        stages=[_gather(got[3:], d2d=[(0, ff1_b), (1, ff2_a)], ici=[(1, ff2_b)])])
    ff2 = _comm_only("gather_last", _gather([ff2], d2d=[(0, ff2_b)]))[0].reshape(DF, D)
    a1, f = _fwd_mlp(h2, ff1, ff2)
    lossp, dy, df, dg4 = _loss_head(f, x2, target, g4)

    idx_big = jnp.stack([chip, (chip + 1) % NCHIP, (chip + 2) % NCHIP, (chip + 3) % NCHIP, core])
    dh2, df1 = _bwd_mlp_x(df, a1, ff1, ff2)
    dw_ff1, dw_ff2 = _bwd_mlp_w(df, h2, a1, df1)
    g_ff = [_halves(dw_ff1), _halves(dw_ff2)]
    (dxres, dgates, dylru, dypool, dm, dbra, dbrb, dg2, dg3, dbg), (l_ff,) = _bwd_merge(
        dh2, dy, x2, m, bra, brb, proj, b_gate, g2, g3, w_lru_up_f, w_pool_up_f, w_o_f, stages=[_to_sibling(g_ff)])
    p_ff = [_add_sibling(g, l, cidx, "add_sibling_" + n)[0] for g, l, n in zip(g_ff, l_ff, ["w_ff1", "w_ff2"])]
    ff1_head, ff1_tail = (0, 7, 16), (7, 16, 16)
    (dw_o, dw_lru_up, dw_pool_up), ((c_ff1,),) = _dw_merge(
        mrg, dm, ylru, dbra, ypool, dbrb, stages=[_to_chips(p_ff[:1], parts=[ff1_head])])
    g_mix = [_halves(dw_lru_up), _halves(dw_pool_up), _halves(dw_o)]
    (dxp, dgl, dcw, dcb, dwa, dba, dwx, dbx, dlam), ((c_ff1, c_ff2), l_mix) = _bwd_lru(
        proj, hs, dylru, conv_full, conv_b, wa, lru_b_a, wx, lru_b_x, lru_lambda,
        stages=[_to_chips(p_ff, parts=[ff1_tail, ALL], lands=[c_ff1, None]), _to_sibling(g_mix)])
    p_mix = [_add_sibling(g, l, cidx, "add_sibling_" + n)[0] for g, l, n in zip(g_mix, l_mix, mix)]
    dxpool, dpw, dsc = _bwd_pool(proj, dypool, pw, pool_scale)
    dproj = jnp.concatenate([dxp, dgl, dxpool, dgates], axis=1)
    small = jnp.concatenate([
        jnp.zeros((1, D), F32), dg2, dg3, dg4, dbg.reshape(2, D), dcb, dba, dbx, dlam,
        jnp.pad(dsc, ((0, 0), (0, D - DP))), jnp.pad(lossp, ((0, 0), (0, D - 1))), dcw,
        _block_diag_extract(dwa, hd).reshape(-1, D), _block_diag_extract(dwx, hd).reshape(-1, D),
        dpw.reshape(-1, D)], axis=0)
    (dw_in, dh1), (c_mix, (l_small,)) = _bwd_inproj(h1, dproj, full["w_in"],
                                                     stages=[_to_chips(p_mix), _to_sibling([small])])
    small2 = _add_pair(small, l_small, "add_sibling_small").reshape(2, SMALL_ROWS // 2, D)
    g_in = _halves(dw_in)
    done = ["w_ff1", "w_ff2"] + mix
    pair_ff1, ((l_in,), (c_small,)) = _add_chips(p_ff[0], c_ff1, idx_big, "add_chips_w_ff1",
                                                 stages=[_to_sibling([g_in]), _to_chips([small2])])
    p_in = _add_sibling(g_in, l_in, cidx, "add_sibling_w_in")[0]
    ssem, rsem, p_in, c_in, token = _chips_start(p_in)
    g1_after = g1 + token[0:1, 0:1]
    pairs = [pair_ff1] + [_add_chips(p, l, idx_big, "add_chips_" + n)
                          for p, l, n in zip(p_ff[1:] + p_mix, [c_ff2] + c_mix, done[1:])]
    own_small = lax.dynamic_index_in_dim(small2, core, 0, keepdims=True)
    c_small = lax.dynamic_update_slice(c_small, own_small, (chip, zero, zero))
    pair_small = _add_chips(c_small, c_small, jnp.stack([zero, zero + 1, zero + 2, zero + 3, core]), "add_chips_small")
    (grad_x, dg1), _ = _bwd_prenorm(xs, dh1, dxres, g1_after)
    _, (shared, (dg1_all,)) = _call(lambda: None, name="reduce_share", grid=(1,), in_specs=[], out_specs=[],
                                    out_shape=[], args=[], stages=[_share(pairs + [pair_small]), _to_everyone(dg1)])
    pairs, pair_small = shared[:-1], shared[-1]
    dg1_all = lax.dynamic_update_slice(dg1_all, dg1[None], (2 * chip + core, zero, zero)).reshape(2 * NCHIP, D)

    grads, delta, new_m, new_v = {}, {}, {}, {}
    for n, p in zip(done, pairs):
        grads[n] = p.reshape(-1, p.shape[-1])

    def update(n, stages=()):
        (delta[n], new_m[n], new_v[n]), landed = _adamw(w[n], grads[n], mom[n], var[n], "adamw_" + n, stages=stages)
        return landed

    small_sum = pair_small.reshape(SMALL_ROWS, D)
    loss = 0.5 * small_sum[LOSS_ROW, 0]
    ccols = DR // NCHIP
    sep = [lax.dynamic_slice(small_sum[12:16], (zero, chip * ccols), (4, ccols)),
           small_sum[16:80].reshape(-1, hd), small_sum[80:144].reshape(-1, hd), small_sum[144:208].reshape(-1, PG)]
    g_s, d_s, m_s, v_s = _adamw_small(small_sum, dg1_all, sep, w, mom, var)
    grads.update(g_s)
    grads.update(dict(zip(SMALL_SEPARATE, sep)))
    delta.update(d_s)
    new_m.update(m_s)
    new_v.update(v_s)
    updated, _ = _adamw_multi(["w_ff1", "w_ff2", "w_o", "w_lru_up"], w, grads, mom, var)
    for n, (d, mo, vo) in updated.items():
        delta[n], new_m[n], new_v[n] = d, mo, vo
    p_in, c_in = _chips_wait(ssem, rsem, p_in, c_in, new_v["w_lru_up"])
    pair_in = _add_chips(p_in, c_in, idx_big, "add_chips_w_in")
    ((pair_in,),) = update("w_pool_up", stages=[_share([pair_in])])
    grads["w_in"] = pair_in.reshape(-1, pair_in.shape[-1])
    update("w_in")

    out = lambda d: [d[n].reshape(args[n].shape) for n in W_NAMES]
    return (loss, grad_x[None], *out(grads), *out(delta), *out(new_m), *out(new_v))
```

```python
import functools
import math

import jax
import jax.numpy as jnp
from jax import lax
from jax.experimental import pallas as pl
from jax.experimental.pallas import tpu as pltpu

F32 = jnp.float32
BF = jnp.bfloat16

T = 2048
D = 1024
DR = 1024
DP = 512
DF = 4096
DIN = 4608
NCHIP = 4
CW_IN = DIN // NCHIP
LANE = 128
CB = 128
NG = DR // CB
PG = 128
POOL_WINDOWS = (2, 4, 8, 16)
NORM_EPS = 1e-6
LRU_C = 8.0
GELU_C = math.sqrt(2.0 / math.pi)
ADAM_LR = 0.001
ADAM_B1 = 0.9
ADAM_B2 = 0.999
ADAM_EPS = 1e-08
ADAM_WD = 0.01
ADAM_STEP = 10
MESH_ID = pl.DeviceIdType.MESH
ANY = pl.BlockSpec(memory_space=pl.ANY)
SMALL_ROWS = 208
LOSS_ROW = 11
MIB = 1 << 20


def _cp(vmem_mib=None):
    if vmem_mib is None:
        return pltpu.CompilerParams()
    return pltpu.CompilerParams(vmem_limit_bytes=vmem_mib * MIB)


def _hbm(*arrays):
    return [pltpu.with_memory_space_constraint(a, pltpu.HBM) for a in arrays]


def _hbm_out(shapes):
    return [pltpu.HBM(s.shape, s.dtype) for s in shapes]


class _Stage:
    def __init__(self, operands, out_shape, alias, sems, start, finish):
        self.operands, self.out_shape, self.alias, self.sems = list(operands), list(out_shape), dict(alias), list(sems)
        self.start, self.finish = start, finish


def _call(body, *, name, grid, in_specs, out_specs, out_shape, args, vmem=None, stages=(), prefetch=None,
          scratch=()):
    nin, nout = len(in_specs), len(out_specs)
    npre = 0 if prefetch is None else 1
    st_args, st_shapes, st_sems, aliases = [], [], list(scratch), {}
    for st in stages:
        for k, v in st.alias.items():
            aliases[npre + nin + len(st_args) + k] = nout + len(st_shapes) + v
        st_args += st.operands
        st_shapes += st.out_shape
        st_sems += st.sems

    def wrapped(*refs):
        pre, refs = refs[:npre], refs[npre:]
        ins, pos = refs[:nin], nin
        st_ins = []
        for st in stages:
            st_ins.append(refs[pos:pos + len(st.operands)])
            pos += len(st.operands)
        outs, pos = refs[pos:pos + nout], pos + nout
        st_outs = []
        for st in stages:
            st_outs.append(refs[pos:pos + len(st.out_shape)])
            pos += len(st.out_shape)
        work, pos = refs[pos:pos + len(scratch)], pos + len(scratch)
        sems = []
        for st in stages:
            sems.append(refs[pos:pos + len(st.sems)])
            pos += len(st.sems)
        if stages:
            first = functools.reduce(jnp.logical_and, [pl.program_id(a) == 0 for a in range(len(grid))])

            @pl.when(first)
            def _():
                for st, a, b, s in zip(stages, st_ins, st_outs, sems):
                    st.start(a, b, s)

        body(*pre, *ins, *outs, *work)
        if stages:
            last = functools.reduce(jnp.logical_and, [pl.program_id(a) == g - 1 for a, g in enumerate(grid)])

            @pl.when(last)
            def _():
                for st, a, b, s in zip(stages, st_ins, st_outs, sems):
                    st.finish(a, b, s)

    all_in = list(in_specs) + [ANY] * len(st_args)
    all_out = list(out_specs) + [ANY] * len(st_shapes)
    kw = dict(has_side_effects=True) if stages else {}
    if vmem is not None:
        kw["vmem_limit_bytes"] = vmem * MIB
    if prefetch is None:
        gkw = dict(grid=grid, in_specs=all_in, out_specs=all_out, scratch_shapes=st_sems)
    else:
        gkw = dict(grid_spec=pltpu.PrefetchScalarGridSpec(
            num_scalar_prefetch=1, grid=grid, in_specs=all_in, out_specs=all_out, scratch_shapes=st_sems))
    res = pl.pallas_call(
        wrapped, name=name, out_shape=_hbm_out(list(out_shape) + st_shapes), input_output_aliases=aliases,
        compiler_params=pltpu.CompilerParams(**kw), **gkw,
    )(*([prefetch] if npre else []), *_hbm(*args, *st_args))
    outs, rest, st_res = list(res[:nout]), list(res[nout:]), []
    for st in stages:
        st_res.append(rest[:len(st.out_shape)])
        rest = rest[len(st.out_shape):]
    return outs, st_res


def _mm(a, b):
    return jnp.dot(a.astype(BF), b.astype(BF), preferred_element_type=F32)


def _mm_nt(a, b):
    return lax.dot_general(a.astype(BF), b.astype(BF), (((1,), (1,)), ((), ())),
                           preferred_element_type=F32)


def _mm_tn(a, b):
    return lax.dot_general(a.astype(BF), b.astype(BF), (((0,), (0,)), ((), ())),
                           preferred_element_type=F32)


def _rows(v):
    return lax.broadcasted_iota(jnp.int32, v.shape, 0)


def _sd(v, s, fill=0.0):
    return jnp.where(_rows(v) >= s, pltpu.roll(v, s, axis=0), fill)


def _su(v, s, fill=0.0):
    n = v.shape[0]
    return jnp.where(_rows(v) < n - s, pltpu.roll(v, n - s, axis=0), fill)


def _sigmoid(z):
    return 1.0 / (1.0 + jnp.exp(-z))


def _softplus(z):
    e = jnp.exp(-jnp.abs(z))
    u = 1.0 + e
    d = u - 1.0
    log1p = jnp.where(d == 0.0, e, jnp.log(u) * (e / jnp.where(d == 0.0, 1.0, d)))
    return jnp.maximum(z, 0.0) + log1p


def _mean(v):
    return jnp.mean(v, axis=-1, keepdims=True)


def _colsum(v):
    return jnp.sum(v, axis=0, keepdims=True)


def _acc(ref, val, first):
    @pl.when(first)
    def _():
        ref[...] = val

    @pl.when(jnp.logical_not(first))
    def _():
        ref[...] += val


def _conv(xp, cw, cb):
    x1, x2, x3 = _sd(xp, 1), _sd(xp, 2), _sd(xp, 3)
    xc = cb + cw[0:1] * x3 + cw[1:2] * x2 + cw[2:3] * x1 + cw[3:4] * xp
    return xc, x1, x2, x3


def _lru_gates(xc, wa, ba, wx, bx, lam):
    xcb = xc.astype(BF)
    r = _sigmoid(_mm(xcb, wa) + ba)
    ii = _sigmoid(_mm(xcb, wx) + bx)
    sp = _softplus(-lam)
    la = (-LRU_C) * r * sp
    a = jnp.exp(la)
    mult = jnp.sqrt(-jnp.tanh(la) * (a * a + 1.0))
    return xcb, r, ii, sp, a, mult


def _gelu_parts(g):
    th = jnp.tanh(GELU_C * (g + 0.044715 * (g * g * g)))
    gel = 0.5 * g * (1.0 + th)
    dgel = 0.5 * (1.0 + th) + 0.5 * g * (1.0 - th * th) * (GELU_C * (1.0 + 3.0 * 0.044715 * (g * g)))
    return gel, dgel


def _pool_window(x, steps, shift):
    s, sh = x, 1
    for _ in range(steps):
        s = s + shift(s, sh)
        sh *= 2
    return s


def _fwd_inproj(x, g1, w_in, stages=()):
    tm = 512

    def body(x_ref, g_ref, w_ref, proj_ref, h_ref):
        @pl.when(pl.program_id(1) == 0)
        def _():
            xv = x_ref[...]
            r = lax.rsqrt(_mean(xv * xv) + NORM_EPS)
            h_ref[...] = ((xv * r) * g_ref[...]).astype(BF)

        proj_ref[...] = jnp.dot(h_ref[...], w_ref[0], preferred_element_type=F32)

    return _call(
        body, name="fwd_inproj", grid=(T // tm, NCHIP),
        in_specs=[pl.BlockSpec((tm, D), lambda i, k: (i, 0)),
                  pl.BlockSpec((1, D), lambda i, k: (0, 0)),
                  pl.BlockSpec((1, D, CW_IN), lambda i, k: (k, 0, 0))],
        out_specs=[pl.BlockSpec((tm, CW_IN), lambda i, k: (i, k)),
                   pl.BlockSpec((tm, D), lambda i, k: (i, 0))],
        out_shape=[jax.ShapeDtypeStruct((T, DIN), F32), jax.ShapeDtypeStruct((T, D), BF)],
        vmem=40, args=[x, g1, w_in], stages=stages)


def _vec_spec():
    return pl.BlockSpec((1, CB), lambda j: (0, j))


def _fwd_lru(proj, conv_w, conv_b, wa, ba, wx, bx, lam, stages=()):
    def body(xp_ref, g_ref, cw_ref, cb_ref, wa_ref, ba_ref, wx_ref, bx_ref, lam_ref, y_ref, h_ref):
        xc, _, _, _ = _conv(xp_ref[...], cw_ref[...], cb_ref[...])
        _, _, ii, _, a, mult = _lru_gates(xc, wa_ref[0], ba_ref[...], wx_ref[0], bx_ref[...], lam_ref[...])
        b = mult * (ii * xc)
        s = 1
        while s < T:
            b = b + a * _sd(b, s, 0.0)
            if 2 * s < T:
                a = a * _sd(a, s, 1.0)
            s *= 2
        h_ref[...] = b
        gel, _ = _gelu_parts(g_ref[...])
        y_ref[...] = (b * gel).astype(BF)

    return _call(
        body, name="fwd_lru", grid=(NG,),
        in_specs=[pl.BlockSpec((T, CB), lambda j: (0, j)),
                  pl.BlockSpec((T, CB), lambda j: (0, NG + j)),
                  pl.BlockSpec((4, CB), lambda j: (0, j)),
                  _vec_spec(),
                  pl.BlockSpec((1, CB, CB), lambda j: (j, 0, 0)), _vec_spec(),
                  pl.BlockSpec((1, CB, CB), lambda j: (j, 0, 0)), _vec_spec(),
                  _vec_spec()],
        out_specs=[pl.BlockSpec((T, CB), lambda j: (0, j)), pl.BlockSpec((T, CB), lambda j: (0, j))],
        out_shape=[jax.ShapeDtypeStruct((T, DR), BF), jax.ShapeDtypeStruct((T, DR), F32)],
        vmem=48, args=[proj, proj, conv_w, conv_b, wa, ba, wx, bx, lam], stages=stages)


def _pool_cnt(w):
    t = lax.broadcasted_iota(jnp.int32, (T, 1), 0)
    return jnp.minimum(t + 1, w).astype(F32)


def _fwd_pool(proj, pool_w, pool_scale):
    def body(xp_ref, pw_ref, sc_ref, y_ref):
        for g, w in enumerate(POOL_WINDOWS):
            cols = slice(g * PG, (g + 1) * PG)
            x = xp_ref[:, cols]
            p = _pool_window(x, g + 1, _sd) / _pool_cnt(w) - x
            y_ref[:, cols] = (_mm(p, pw_ref[g]) * sc_ref[:, cols]).astype(BF)

    return pl.pallas_call(
        body, name="fwd_pool", grid=(1,),
        in_specs=[pl.BlockSpec((T, DP), lambda i: (0, 2 * DR // DP)),
                  pl.BlockSpec((4, PG, PG), lambda i: (0, 0, 0)),
                  pl.BlockSpec((1, DP), lambda i: (0, 0))],
        out_specs=pl.BlockSpec((T, DP), lambda i: (0, 0)),
        out_shape=pltpu.HBM((T, DP), BF),
        compiler_params=_cp(48),
    )(*_hbm(proj, pool_w, pool_scale))


GATE_BLK = 512
GATE_BLK0 = (2 * DR + DP) // GATE_BLK


def _gate_specs(tm):
    return [pl.BlockSpec((tm, GATE_BLK), functools.partial(lambda i, q: (i, GATE_BLK0 + q), q=q))
            for q in range(4)]


def _fwd_merge(x, ylru, ypool, proj, b_gate, g2, g3, w_lru_up, w_pool_up, w_o, stages=()):
    tm = 512

    def body(x_ref, yl_ref, yp_ref, p0, p1, p2, p3, bg_ref, g2_ref, g3_ref, wl_ref, wp_ref, wo_ref,
             x2_ref, h2_ref, m_ref, mrg_ref, bra_ref, brb_ref):
        bra = jnp.dot(yl_ref[...], wl_ref[...], preferred_element_type=F32)
        yp = yp_ref[...]
        brb = jnp.concatenate([jnp.dot(yp, wp_ref[k], preferred_element_type=F32) for k in range(NCHIP)], axis=1)
        bg = bg_ref[...]
        ga = _sigmoid(jnp.concatenate([p0[...], p1[...]], axis=1) + bg[:, :D])
        gb = _sigmoid(jnp.concatenate([p2[...], p3[...]], axis=1) + bg[:, D:])
        mrg = (ga * bra + gb * brb).astype(BF)
        m = jnp.dot(mrg, wo_ref[...], preferred_element_type=F32)
        r2 = lax.rsqrt(_mean(m * m) + NORM_EPS)
        x2 = x_ref[...] + (m * r2) * g2_ref[...]
        r3 = lax.rsqrt(_mean(x2 * x2) + NORM_EPS)
        x2_ref[...] = x2
        h2_ref[...] = ((x2 * r3) * g3_ref[...]).astype(BF)
        m_ref[...] = m
        mrg_ref[...] = mrg
        bra_ref[...] = bra.astype(BF)
        brb_ref[...] = brb.astype(BF)

    row = lambda w: pl.BlockSpec((tm, w), lambda i: (i, 0))
    full2 = lambda a, b: pl.BlockSpec((a, b), lambda i: (0, 0))
    return _call(
        body, name="fwd_merge", grid=(T // tm,),
        in_specs=[row(D), row(DR), row(DP)] + _gate_specs(tm) +
                 [full2(1, 2 * D), full2(1, D), full2(1, D), full2(DR, D),
                  pl.BlockSpec((NCHIP, DP, D // NCHIP), lambda i: (0, 0, 0)), full2(D, D)],
        out_specs=[row(D)] * 6,
        out_shape=[jax.ShapeDtypeStruct((T, D), F32), jax.ShapeDtypeStruct((T, D), BF),
                   jax.ShapeDtypeStruct((T, D), F32), jax.ShapeDtypeStruct((T, D), BF),
                   jax.ShapeDtypeStruct((T, D), BF), jax.ShapeDtypeStruct((T, D), BF)],
        vmem=48, args=[x, ylru, ypool, proj, proj, proj, proj, b_gate, g2, g3, w_lru_up, w_pool_up, w_o],
        stages=stages)


def _fwd_mlp(h2, w_ff1, w_ff2):
    tm = 512
    fk = DF // NCHIP

    def body(h_ref, w1_ref, w2_ref, a1_ref, f_ref):
        h = h_ref[...]
        f = None
        for k in range(NCHIP):
            a1 = jnp.maximum(jnp.dot(h, w1_ref[k], preferred_element_type=F32), 0.0)
            a1_ref[:, k * fk:(k + 1) * fk] = a1.astype(BF)
            part = jnp.dot((a1 * a1).astype(BF), w2_ref[k * fk:(k + 1) * fk, :], preferred_element_type=F32)
            f = part if f is None else f + part
        f_ref[...] = f

    return pl.pallas_call(
        body, name="fwd_mlp", grid=(T // tm,),
        in_specs=[pl.BlockSpec((tm, D), lambda i: (i, 0)),
                  pl.BlockSpec((NCHIP, D, fk), lambda i: (0, 0, 0)),
                  pl.BlockSpec((DF, D), lambda i: (0, 0))],
        out_specs=[pl.BlockSpec((tm, DF), lambda i: (i, 0)), pl.BlockSpec((tm, D), lambda i: (i, 0))],
        out_shape=_hbm_out([jax.ShapeDtypeStruct((T, DF), BF), jax.ShapeDtypeStruct((T, D), F32)]),
        compiler_params=_cp(56),
    )(*_hbm(h2, w_ff1, w_ff2))


def _loss_head(f, x2, target, g4):
    tm = 512

    def body(f_ref, x2_ref, t_ref, g_ref, loss_ref, dy_ref, df_ref, dg_ref):
        first = pl.program_id(0) == 0
        f = f_ref[...]
        g4v = g_ref[...]
        r4 = lax.rsqrt(_mean(f * f) + NORM_EPS)
        fn = f * r4
        e = (x2_ref[...] + fn * g4v) - t_ref[...]
        _acc(loss_ref, jnp.sum(_mean(e * e), axis=0, keepdims=True), first)
        dy = e * (1.0 / D)
        dy_ref[...] = dy
        _acc(dg_ref, _colsum(dy * fn), first)
        dfn = dy * g4v
        df_ref[...] = (r4 * (dfn - fn * _mean(dfn * fn))).astype(BF)

    row = pl.BlockSpec((tm, D), lambda i: (i, 0))
    return pl.pallas_call(
        body, name="loss_head", grid=(T // tm,),
        in_specs=[row, row, row, pl.BlockSpec((1, D), lambda i: (0, 0))],
        out_specs=[pl.BlockSpec((1, 1), lambda i: (0, 0)), row, row, pl.BlockSpec((1, D), lambda i: (0, 0))],
        out_shape=_hbm_out([jax.ShapeDtypeStruct((1, 1), F32), jax.ShapeDtypeStruct((T, D), F32),
                            jax.ShapeDtypeStruct((T, D), BF), jax.ShapeDtypeStruct((1, D), F32)]),
        compiler_params=_cp(48),
    )(*_hbm(f, x2, target, g4))


def _bwd_mlp_x(df, a1, w_ff1, w_ff2):
    tm = 512
    fk = DF // NCHIP

    def body(df_ref, a1_ref, w1_ref, w2_ref, dh_ref, df1_ref):
        df = df_ref[...]
        dh = None
        for k in range(NCHIP):
            cols = slice(k * fk, (k + 1) * fk)
            dact = _mm_nt(df, w2_ref[cols, :])
            df1 = (dact * (2.0 * a1_ref[:, cols].astype(F32))).astype(BF)
            df1_ref[:, cols] = df1
            part = _mm_nt(df1, w1_ref[k])
            dh = part if dh is None else dh + part
        dh_ref[...] = dh

    return pl.pallas_call(
        body, name="bwd_mlp_x", grid=(T // tm,),
        in_specs=[pl.BlockSpec((tm, D), lambda i: (i, 0)),
                  pl.BlockSpec((tm, DF), lambda i: (i, 0)),
                  pl.BlockSpec((NCHIP, D, fk), lambda i: (0, 0, 0)),
                  pl.BlockSpec((DF, D), lambda i: (0, 0))],
        out_specs=[pl.BlockSpec((tm, D), lambda i: (i, 0)), pl.BlockSpec((tm, DF), lambda i: (i, 0))],
        out_shape=_hbm_out([jax.ShapeDtypeStruct((T, D), F32), jax.ShapeDtypeStruct((T, DF), BF)]),
        compiler_params=_cp(56),
    )(*_hbm(df, a1, w_ff1, w_ff2))


def _bwd_mlp_w(df, h2, a1, df1):
    fc = 512
    per = (DF // NCHIP) // fc

    def body(df_ref, h_ref, a1_ref, df1_ref, dw1_ref, dw2_ref):
        a1 = a1_ref[...].astype(F32)
        dw2_ref[...] = _mm_tn((a1 * a1).astype(BF), df_ref[...]).astype(BF)
        dw1_ref[0] = _mm_tn(h_ref[...], df1_ref[...]).astype(BF)

    return pl.pallas_call(
        body, name="bwd_mlp_w", grid=(DF // fc,),
        in_specs=[pl.BlockSpec((T, D), lambda j: (0, 0)),
                  pl.BlockSpec((T, D), lambda j: (0, 0)),
                  pl.BlockSpec((T, fc), lambda j: (0, j)),
                  pl.BlockSpec((T, fc), lambda j: (0, j))],
        out_specs=[pl.BlockSpec((1, D, fc), lambda j: (j // per, 0, j % per)),
                   pl.BlockSpec((fc, D), lambda j: (j, 0))],
        out_shape=_hbm_out([jax.ShapeDtypeStruct((NCHIP, D, DF // NCHIP), BF),
                            jax.ShapeDtypeStruct((DF, D), BF)]),
        compiler_params=_cp(56),
    )(*_hbm(df, h2, a1, df1))


def _bwd_merge(dh2, dy, x2, m, bra, brb, proj, b_gate, g2, g3, w_lru_up, w_pool_up, w_o, stages=()):
    tm = 256
    cpu = D // NCHIP

    def body(dh2_ref, dy_ref, x2_ref, m_ref, bra_ref, brb_ref, p0, p1, p2, p3, bg_ref,
             g2_ref, g3_ref, wl_ref, wp_ref, wo_ref,
             dx_ref, dgt_ref, dyl_ref, dyp_ref, dm_ref, dbra_ref, dbrb_ref, dg2_ref, dg3_ref, dbg_ref):
        first = pl.program_id(0) == 0
        x2 = x2_ref[...]
        r3 = lax.rsqrt(_mean(x2 * x2) + NORM_EPS)
        x2n = x2 * r3
        dh2 = dh2_ref[...]
        t3 = dh2 * g3_ref[...]
        dx2 = dy_ref[...] + r3 * (t3 - x2n * _mean(t3 * x2n))
        dx_ref[...] = dx2
        _acc(dg3_ref, _colsum(dh2 * x2n), first)
        m = m_ref[...]
        r2 = lax.rsqrt(_mean(m * m) + NORM_EPS)
        mn = m * r2
        _acc(dg2_ref, _colsum(dx2 * mn), first)
        dmn = dx2 * g2_ref[...]
        dm = (r2 * (dmn - mn * _mean(dmn * mn))).astype(BF)
        dm_ref[...] = dm
        dmrg = _mm_nt(dm, wo_ref[...])
        bg = bg_ref[...]
        ga = _sigmoid(jnp.concatenate([p0[...], p1[...]], axis=1) + bg[:, :D])
        gb = _sigmoid(jnp.concatenate([p2[...], p3[...]], axis=1) + bg[:, D:])
        dga = dmrg * bra_ref[...].astype(F32) * (ga * (1.0 - ga))
        dgb = dmrg * brb_ref[...].astype(F32) * (gb * (1.0 - gb))
        dgt_ref[:, :D] = dga.astype(BF)
        dgt_ref[:, D:] = dgb.astype(BF)
        _acc(dbg_ref, jnp.concatenate([_colsum(dga), _colsum(dgb)], axis=1), first)
        dbra = (dmrg * ga).astype(BF)
        dbrb = (dmrg * gb).astype(BF)
        dbra_ref[...] = dbra
        dbrb_ref[...] = dbrb
        dyl_ref[...] = _mm_nt(dbra, wl_ref[...])
        dyp = None
        for k in range(NCHIP):
            part = _mm_nt(dbrb[:, k * cpu:(k + 1) * cpu], wp_ref[k])
            dyp = part if dyp is None else dyp + part
        dyp_ref[...] = dyp

    row = lambda w: pl.BlockSpec((tm, w), lambda i: (i, 0))
    full2 = lambda a, b: pl.BlockSpec((a, b), lambda i: (0, 0))
    wp_spec = pl.BlockSpec((NCHIP, DP, cpu), lambda i: (0, 0, 0))
    return _call(
        body, name="bwd_merge", grid=(T // tm,),
        in_specs=[row(D)] * 6 + _gate_specs(tm) +
                 [full2(1, 2 * D), full2(1, D), full2(1, D), full2(DR, D), wp_spec, full2(D, D)],
        out_specs=[row(D), row(2 * D), row(DR), row(DP), row(D), row(D), row(D),
                   full2(1, D), full2(1, D), full2(1, 2 * D)],
        out_shape=[jax.ShapeDtypeStruct((T, D), F32), jax.ShapeDtypeStruct((T, 2 * D), BF),
                   jax.ShapeDtypeStruct((T, DR), F32), jax.ShapeDtypeStruct((T, DP), F32),
                   jax.ShapeDtypeStruct((T, D), BF), jax.ShapeDtypeStruct((T, D), BF),
                   jax.ShapeDtypeStruct((T, D), BF),
                   jax.ShapeDtypeStruct((1, D), F32), jax.ShapeDtypeStruct((1, D), F32),
                   jax.ShapeDtypeStruct((1, 2 * D), F32)],
        vmem=56, args=[dh2, dy, x2, m, bra, brb, proj, proj, proj, proj, b_gate, g2, g3, w_lru_up, w_pool_up, w_o],
        stages=stages)


def _dw_merge(mrg, dm, ylru, dbra, ypool, dbrb, stages=()):
    nb = NCHIP
    rb, pb, cpu = D // nb, DP // nb, D // NCHIP

    def body(mrg_ref, dm_ref, yl_ref, dbra_ref, yp_ref, dbrb_ref, dwo_ref, dwl_ref, dwp_ref):
        dwo_ref[...] = _mm_tn(mrg_ref[...], dm_ref[...]).astype(BF)
        dwl_ref[...] = _mm_tn(yl_ref[...], dbra_ref[...]).astype(BF)
        dwp = _mm_tn(yp_ref[...], dbrb_ref[...]).astype(BF)
        for k in range(NCHIP):
            dwp_ref[k] = dwp[:, k * cpu:(k + 1) * cpu]

    cols = lambda w: pl.BlockSpec((T, w), lambda r: (0, r))
    whole = pl.BlockSpec((T, D), lambda r: (0, 0))
    return _call(
        body, name="dw_merge", grid=(nb,),
        in_specs=[cols(rb), whole, cols(rb), whole, cols(pb), whole],
        out_specs=[pl.BlockSpec((rb, D), lambda r: (r, 0)), pl.BlockSpec((rb, D), lambda r: (r, 0)),
                   pl.BlockSpec((NCHIP, pb, cpu), lambda r: (0, r, 0))],
        out_shape=[jax.ShapeDtypeStruct((D, D), BF), jax.ShapeDtypeStruct((DR, D), BF),
                   jax.ShapeDtypeStruct((NCHIP, DP, cpu), BF)],
        vmem=56, args=[mrg, dm, ylru, dbra, ypool, dbrb], stages=stages)


def _bwd_lru(proj, h, dylru, conv_w, conv_b, wa, ba, wx, bx, lam, stages=()):
    def body(xp_ref, g_ref, h_ref, dy_ref, cw_ref, cb_ref, wa_ref, ba_ref, wx_ref, bx_ref, lam_ref,
             dxp_ref, dg_ref, dcw_ref, dcb_ref, dwa_ref, dba_ref, dwx_ref, dbx_ref, dlam_ref):
        xp = xp_ref[...]
        cw = cw_ref[...]
        lam = lam_ref[...]
        xc, x1, x2, x3 = _conv(xp, cw, cb_ref[...])
        wa, wx = wa_ref[0], wx_ref[0]
        xcb, r, ii, sp, a, mult = _lru_gates(xc, wa, ba_ref[...], wx, bx_ref[...], lam)
        g = g_ref[...]
        gel, dgel = _gelu_parts(g)
        h = h_ref[...]
        dy = dy_ref[...]
        dg_ref[...] = (dy * h * dgel).astype(BF)
        b = dy * gel
        aa = _su(a, 1, 0.0)
        s = 1
        while s < T:
            b = b + aa * _su(b, s, 0.0)
            if 2 * s < T:
                aa = aa * _su(aa, s, 0.0)
            s *= 2
        da = b * _sd(h, 1, 0.0)
        dmult = b * (ii * xc)
        dii = b * (mult * xc)
        dxc = b * (mult * ii)
        dla = da * a - dmult * ((a * a) / mult)
        dr = dla * ((-LRU_C) * sp)
        dsp = _colsum(dla * ((-LRU_C) * r))
        dlam_ref[...] = -dsp / (1.0 + jnp.exp(lam))
        dzr = dr * (r * (1.0 - r))
        dzi = dii * (ii * (1.0 - ii))
        dzrb, dzib = dzr.astype(BF), dzi.astype(BF)
        dxc = dxc + _mm_nt(dzrb, wa) + _mm_nt(dzib, wx)
        dwa_ref[0] = _mm_tn(xcb, dzrb)
        dwx_ref[0] = _mm_tn(xcb, dzib)
        dba_ref[...] = _colsum(dzr)
        dbx_ref[...] = _colsum(dzi)
        dcb_ref[...] = _colsum(dxc)
        dcw_ref[...] = jnp.concatenate([_colsum(dxc * x3), _colsum(dxc * x2), _colsum(dxc * x1),
                                        _colsum(dxc * xp)], axis=0)
        dxp = cw[3:4] * dxc + cw[2:3] * _su(dxc, 1) + cw[1:2] * _su(dxc, 2) + cw[0:1] * _su(dxc, 3)
        dxp_ref[...] = dxp.astype(BF)

    blk = pl.BlockSpec((T, CB), lambda j: (0, j))
    wsp = pl.BlockSpec((1, CB, CB), lambda j: (j, 0, 0))
    return _call(
        body, name="bwd_lru", grid=(NG,),
        in_specs=[blk, pl.BlockSpec((T, CB), lambda j: (0, NG + j)), blk, blk,
                  pl.BlockSpec((4, CB), lambda j: (0, j)), _vec_spec(), wsp, _vec_spec(), wsp, _vec_spec(),
                  _vec_spec()],
        out_specs=[blk, blk, pl.BlockSpec((4, CB), lambda j: (0, j)), _vec_spec(), wsp, _vec_spec(), wsp,
                   _vec_spec(), _vec_spec()],
        out_shape=[jax.ShapeDtypeStruct((T, DR), BF), jax.ShapeDtypeStruct((T, DR), BF),
                   jax.ShapeDtypeStruct((4, DR), F32), jax.ShapeDtypeStruct((1, DR), F32),
                   jax.ShapeDtypeStruct((NG, CB, CB), F32), jax.ShapeDtypeStruct((1, DR), F32),
                   jax.ShapeDtypeStruct((NG, CB, CB), F32), jax.ShapeDtypeStruct((1, DR), F32),
                   jax.ShapeDtypeStruct((1, DR), F32)],
        vmem=56, args=[proj, proj, h, dylru, conv_w, conv_b, wa, ba, wx, bx, lam], stages=stages)


def _bwd_pool(proj, dypool, pool_w, pool_scale):
    def body(xp_ref, dy_ref, pw_ref, sc_ref, dx_ref, dw_ref, dsc_ref):
        for g, w in enumerate(POOL_WINDOWS):
            cols = slice(g * PG, (g + 1) * PG)
            cnt = _pool_cnt(w)
            x = xp_ref[:, cols]
            pb = (_pool_window(x, g + 1, _sd) / cnt - x).astype(BF)
            wg = pw_ref[g]
            dy = dy_ref[:, cols]
            dsc_ref[:, cols] = _colsum(dy * _mm(pb, wg))
            dyp = (dy * sc_ref[:, cols]).astype(BF)
            dw_ref[g] = _mm_tn(pb, dyp)
            dp = _mm_nt(dyp, wg)
            dx_ref[:, cols] = (_pool_window(dp / cnt, g + 1, _su) - dp).astype(BF)

    return pl.pallas_call(
        body, name="bwd_pool", grid=(1,),
        in_specs=[pl.BlockSpec((T, DP), lambda i: (0, 2 * DR // DP)),
                  pl.BlockSpec((T, DP), lambda i: (0, 0)),
                  pl.BlockSpec((4, PG, PG), lambda i: (0, 0, 0)),
                  pl.BlockSpec((1, DP), lambda i: (0, 0))],
        out_specs=[pl.BlockSpec((T, DP), lambda i: (0, 0)),
                   pl.BlockSpec((4, PG, PG), lambda i: (0, 0, 0)),
                   pl.BlockSpec((1, DP), lambda i: (0, 0))],
        out_shape=_hbm_out([jax.ShapeDtypeStruct((T, DP), BF), jax.ShapeDtypeStruct((4, PG, PG), F32),
                            jax.ShapeDtypeStruct((1, DP), F32)]),
        compiler_params=_cp(48),
    )(*_hbm(proj, dypool, pool_w, pool_scale))


def _bwd_inproj(h1, dproj, w_in, stages=()):
    def body(h_ref, dp_ref, w_ref, dw_ref, dh_ref):
        dp = dp_ref[...]
        dw_ref[0] = _mm_tn(h_ref[...], dp).astype(BF)
        _acc(dh_ref, _mm_nt(dp, w_ref[0]), pl.program_id(0) == 0)

    return _call(
        body, name="bwd_inproj", grid=(NCHIP,),
        in_specs=[pl.BlockSpec((T, D), lambda k: (0, 0)),
                  pl.BlockSpec((T, CW_IN), lambda k: (0, k)),
                  pl.BlockSpec((1, D, CW_IN), lambda k: (k, 0, 0))],
        out_specs=[pl.BlockSpec((1, D, CW_IN), lambda k: (k, 0, 0)), pl.BlockSpec((T, D), lambda k: (0, 0))],
        out_shape=[jax.ShapeDtypeStruct((NCHIP, D, CW_IN), BF), jax.ShapeDtypeStruct((T, D), F32)],
        vmem=56, args=[h1, dproj, w_in], stages=stages)


def _bwd_prenorm(x, dh1, dxres, g1, stages=()):
    tm = 512

    def body(x_ref, dh_ref, dr_ref, g_ref, dx_ref, dg_ref):
        xv = x_ref[...]
        r = lax.rsqrt(_mean(xv * xv) + NORM_EPS)
        xn = xv * r
        dh = dh_ref[...]
        t = dh * g_ref[...]
        dx_ref[...] = dr_ref[...] + r * (t - xn * _mean(t * xn))
        _acc(dg_ref, _colsum(dh * xn), pl.program_id(0) == 0)

    row = pl.BlockSpec((tm, D), lambda i: (i, 0))
    vec = pl.BlockSpec((1, D), lambda i: (0, 0))
    return _call(
        body, name="bwd_prenorm", grid=(T // tm,),
        in_specs=[row, row, row, vec], out_specs=[row, vec],
        out_shape=[jax.ShapeDtypeStruct((T, D), F32), jax.ShapeDtypeStruct((1, D), F32)],
        vmem=48, args=[x, dh1, dxres, g1], stages=stages)


def _place():
    x, y, c = lax.axis_index("x"), lax.axis_index("y"), lax.axis_index("c")
    chips = [(1 - x, y), (x, 1 - y), (1 - x, 1 - y)]
    return x, y, c, chips


def _rcopy(src, dst, ssem, rsem, dev):
    return pltpu.make_async_remote_copy(src_ref=src, dst_ref=dst, send_sem=ssem, recv_sem=rsem,
                                        device_id=dev, device_id_type=MESH_ID)


def _sds(a):
    return jax.ShapeDtypeStruct(a.shape, a.dtype)


def _sem2(n, m):
    return [pltpu.SemaphoreType.DMA((n, m)), pltpu.SemaphoreType.DMA((n, m))]


ALL = (0, 1, 1)


def _piece(ref, k, half, part):
    hr = ref.shape[1] // 2
    r0, r1 = hr * part[0] // part[2], hr * part[1] // part[2]
    return ref.at[k, pl.ds(half * hr + r0, r1 - r0), :]


def _gather(fulls, ici=(), d2d=()):
    n = len(fulls)
    ici, d2d = list(ici), list(d2d)
    pieces = [("ici", i, part) for i, part in ici] + [("d2d", i, part) for i, part in d2d]

    def copies(outs, sems):
        x, y, c, chips = _place()
        me = 2 * x + y
        sib = (x, y, 1 - c)
        send, recv = [], []
        for q, (kind, i, part) in enumerate(pieces):
            for j, chip in enumerate(chips):
                k, s = 2 * chip[0] + chip[1], 3 * q + j
                if kind == "ici":
                    mine, theirs, dev = _piece(outs[i], me, c, part), _piece(outs[i], k, c, part), (*chip, c)
                else:
                    mine, theirs, dev = _piece(outs[i], k, c, part), _piece(outs[i], k, 1 - c, part), sib
                send.append(_rcopy(mine, mine, sems[0].at[s], sems[1].at[s], dev))
                recv.append(_rcopy(theirs, theirs, sems[0].at[s], sems[1].at[s], dev))
        return send, recv

    def start(ins, outs, sems):
        for cp in copies(outs, sems)[0]:
            cp.start()

    def finish(ins, outs, sems):
        send, recv = copies(outs, sems)
        for cp in recv:
            cp.wait_recv()
        for cp in send:
            cp.wait_send()

    sems = [pltpu.SemaphoreType.DMA((3 * len(pieces),)), pltpu.SemaphoreType.DMA((3 * len(pieces),))]
    return _Stage(fulls, [_sds(f) for f in fulls], {i: i for i in range(n)}, sems, start, finish)


def _gather_whole(v):
    def copies(ins, outs, sems):
        x, y, c, chips = _place()
        me = 2 * x + y
        send = [_rcopy(ins[0], outs[0].at[me], sems[0].at[j], sems[1].at[j], (*chip, c))
                for j, chip in enumerate(chips)]
        recv = [_rcopy(ins[0], outs[0].at[2 * chip[0] + chip[1]], sems[0].at[j], sems[1].at[j], (*chip, c))
                for j, chip in enumerate(chips)]
        return send, recv

    def start(ins, outs, sems):
        for cp in copies(ins, outs, sems)[0]:
            cp.start()

    def finish(ins, outs, sems):
        send, recv = copies(ins, outs, sems)
        for cp in recv:
            cp.wait_recv()
        for cp in send:
            cp.wait_send()

    return _Stage([v], [jax.ShapeDtypeStruct((NCHIP,) + v.shape, v.dtype)], {},
                  [pltpu.SemaphoreType.DMA((3,)), pltpu.SemaphoreType.DMA((3,))], start, finish)


def _to_sibling(srcs):
    n = len(srcs)

    def copies(ins, outs, sems):
        x, y, c, _ = _place()
        sib = (x, y, 1 - c)
        return [_rcopy(ins[i].at[:, 1 - c] if srcs[i].ndim == 4 else ins[i], outs[i], sems[0].at[i], sems[1].at[i], sib)
                for i in range(n)]

    def start(ins, outs, sems):
        for cp in copies(ins, outs, sems):
            cp.start()

    def finish(ins, outs, sems):
        for cp in copies(ins, outs, sems):
            cp.wait()

    shapes = [jax.ShapeDtypeStruct((NCHIP,) + s.shape[2:] if s.ndim == 4 else s.shape, s.dtype) for s in srcs]
    return _Stage(srcs, shapes, {}, [pltpu.SemaphoreType.DMA((n,)), pltpu.SemaphoreType.DMA((n,))], start, finish)


def _to_chips(srcs, parts=None, lands=None):
    n = len(srcs)
    parts = [ALL] * n if parts is None else parts
    lands = [None] * n if lands is None else lands
    given = [i for i in range(n) if lands[i] is not None]

    def rows(ref, i):
        hr = srcs[i].shape[1]
        r0, r1 = hr * parts[i][0] // parts[i][2], hr * parts[i][1] // parts[i][2]
        return ref.at[pl.ds(r0, r1 - r0), :]

    def copies(ins, outs, sems):
        x, y, c, chips = _place()
        me = 2 * x + y
        return [_rcopy(rows(ins[i].at[2 * chip[0] + chip[1]] if srcs[i].shape[0] == NCHIP else ins[i].at[c], i),
                       rows(outs[i].at[me], i), sems[0].at[i, j], sems[1].at[i, j], (*chip, c))
                for i in range(n) for j, chip in enumerate(chips)]

    def start(ins, outs, sems):
        for cp in copies(ins, outs, sems):
            cp.start()

    def finish(ins, outs, sems):
        for cp in copies(ins, outs, sems):
            cp.wait()

    shapes = [jax.ShapeDtypeStruct((NCHIP,) + s.shape[1:], s.dtype) for s in srcs]
    alias = {n + q: i for q, i in enumerate(given)}
    return _Stage(list(srcs) + [lands[i] for i in given], shapes, alias, _sem2(n, 3), start, finish)


HBM_REF = pl.BlockSpec(memory_space=pltpu.HBM)
SEM_REF = pl.BlockSpec(memory_space=pltpu.SEMAPHORE)
DATAFLOW = pltpu.SideEffectType.DATAFLOW_SIDE_EFFECTING


class _Flight:
    def __init__(self, stage, sems, bufs):
        self.stage, self.sems, self.bufs = stage, list(sems), list(bufs)

    def landed(self):
        st, n = self.stage, len(self.stage.operands)
        fresh = [j for j in range(len(st.out_shape)) if j not in st.alias.values()]
        back = {v: k for k, v in st.alias.items()}
        return [self.bufs[back[j]] if j in back else self.bufs[n + fresh.index(j)] for j in range(len(st.out_shape))]


def _split_call(name, finish=(), start=(), after=None):
    bufs, stage_bufs = [], []

    def slot(a):
        for i, b in enumerate(bufs):
            if b is a:
                return i
        bufs.append(a)
        return len(bufs) - 1

    fin_slots = [[slot(b) for b in fl.bufs] for fl in finish]
    for st in start:
        fresh = [lax.empty(o.shape, o.dtype) for j, o in enumerate(st.out_shape) if j not in st.alias.values()]
        stage_bufs.append([slot(a) for a in list(st.operands) + fresh])
    old_sems = [s for fl in finish for s in fl.sems]
    new_sems = [s for st in start for s in st.sems]
    nb, no, nn = len(bufs), len(old_sems), len(new_sems)

    def refs_of(st, slots, buf_refs):
        n = len(st.operands)
        ins = [buf_refs[i] for i in slots[:n]]
        fresh = [j for j in range(len(st.out_shape)) if j not in st.alias.values()]
        back = {v: k for k, v in st.alias.items()}
        outs = [ins[back[j]] if j in back else buf_refs[slots[n + fresh.index(j)]] for j in range(len(st.out_shape))]
        return ins, outs

    def body(*refs):
        buf_refs, sem_in = refs[:nb], refs[nb:nb + no]
        sem_out = refs[nb + no + (after is not None):][:nn]
        token = refs[-1]
        pos = 0
        for fl, slots in zip(finish, fin_slots):
            ins, outs = refs_of(fl.stage, slots, buf_refs)
            fl.stage.finish(ins, outs, sem_in[pos:pos + len(fl.sems)])
            pos += len(fl.sems)
        pos = 0
        for st, slots in zip(start, stage_bufs):
            ins, outs = refs_of(st, slots, buf_refs)
            st.start(ins, outs, sem_out[pos:pos + len(st.sems)])
            pos += len(st.sems)
        token[...] = jnp.zeros_like(token)

    res = pl.pallas_call(
        body, name=name,
        out_shape=tuple(new_sems) + tuple(pltpu.HBM(b.shape, b.dtype) for b in bufs) +
                  (jax.ShapeDtypeStruct((8, LANE), F32),),
        in_specs=(HBM_REF,) * nb + (SEM_REF,) * no + ((pl.BlockSpec(memory_space=pl.ANY),) if after is not None else ()),
        out_specs=(SEM_REF,) * nn + (HBM_REF,) * nb + (pl.BlockSpec(memory_space=pltpu.VMEM),),
        input_output_aliases={i: nn + i for i in range(nb)},
        compiler_params=pltpu.CompilerParams(has_side_effects=DATAFLOW),
    )(*_hbm(*bufs), *old_sems, *([after] if after is not None else []))
    sems, thru, token = res[:nn], res[nn:nn + nb], res[-1]
    for fl, slots in zip(finish, fin_slots):
        fl.bufs = [thru[i] for i in slots]
    flights, pos = [], 0
    for st, slots in zip(start, stage_bufs):
        flights.append(_Flight(st, sems[pos:pos + len(st.sems)], [thru[i] for i in slots]))
        pos += len(st.sems)
    return flights, token


def _last_copies(p_ref, land_ref, ssem, rsem):
    x, y, c, chips = _place()
    me = 2 * x + y
    send = [_rcopy(p_ref.at[2 * chip[0] + chip[1]], land_ref.at[me], ssem.at[j], rsem.at[j], (*chip, c))
            for j, chip in enumerate(chips)]
    recv = [_rcopy(p_ref.at[2 * chip[0] + chip[1]], land_ref.at[2 * chip[0] + chip[1]], ssem.at[j], rsem.at[j],
                   (*chip, c)) for j, chip in enumerate(chips)]
    return send, recv


def _chips_start(p):
    def body(p_ref, land_ref, ssem, rsem, p_thru, land_thru, token):
        for cp in _last_copies(p_ref, land_ref, ssem, rsem)[0]:
            cp.start()
        token[...] = jnp.zeros_like(token)

    return pl.pallas_call(
        body, name="reduce_last_start",
        out_shape=(pltpu.SemaphoreType.DMA((3,)), pltpu.SemaphoreType.DMA((3,)), pltpu.HBM(p.shape, p.dtype),
                   pltpu.HBM(p.shape, p.dtype), jax.ShapeDtypeStruct((8, LANE), F32)),
        in_specs=(HBM_REF, HBM_REF),
        out_specs=(SEM_REF, SEM_REF, HBM_REF, HBM_REF, pl.BlockSpec(memory_space=pltpu.VMEM)),
        input_output_aliases={0: 2, 1: 3},
        compiler_params=pltpu.CompilerParams(has_side_effects=DATAFLOW),
    )(*_hbm(p, lax.empty(p.shape, p.dtype)))


def _chips_wait(ssem, rsem, p_thru, land_thru, after):
    def body(p_ref, land_ref, ssem, rsem, after_ref, p_dead, got_ref):
        send, recv = _last_copies(p_ref, land_ref, ssem, rsem)
        for cp in send:
            cp.wait_send()
        for cp in recv:
            cp.wait_recv()

    return pl.pallas_call(
        body, name="reduce_last_wait",
        out_shape=(pltpu.HBM(p_thru.shape, p_thru.dtype), pltpu.HBM(land_thru.shape, land_thru.dtype)),
        in_specs=(HBM_REF, HBM_REF, SEM_REF, SEM_REF, pl.BlockSpec(memory_space=pl.ANY)),
        out_specs=(HBM_REF, HBM_REF), input_output_aliases={0: 0, 1: 1},
        compiler_params=pltpu.CompilerParams(has_side_effects=DATAFLOW),
    )(p_thru, land_thru, ssem, rsem, after)


def _share(pairs):
    n = len(pairs)

    def start(ins, outs, sems):
        x, y, c, _ = _place()
        for i in range(n):
            _rcopy(outs[i].at[c], outs[i].at[c], sems[0].at[i], sems[1].at[i], (x, y, 1 - c)).start()

    def finish(ins, outs, sems):
        x, y, c, _ = _place()
        for i in range(n):
            _rcopy(outs[i].at[c], outs[i].at[c], sems[0].at[i], sems[1].at[i], (x, y, 1 - c)).wait_send()
            _rcopy(outs[i].at[1 - c], outs[i].at[1 - c], sems[0].at[i], sems[1].at[i], (x, y, 1 - c)).wait_recv()

    return _Stage(pairs, [_sds(p) for p in pairs], {i: i for i in range(n)},
                  [pltpu.SemaphoreType.DMA((n,)), pltpu.SemaphoreType.DMA((n,))], start, finish)


def _row_block(rows, cols, itemsize=4, target=2 * MIB):
    br = rows
    while br * cols * itemsize > target and br % 32 == 0:
        br //= 2
    return br


def _cast_place(w, chip_idx, name):
    rows, cols = w.shape
    br = _row_block(rows, cols)

    def body(k_ref, w_ref, o_ref):
        o_ref[0] = w_ref[...].astype(BF)

    return _call(
        body, name=name, grid=(rows // br,), prefetch=chip_idx,
        in_specs=[pl.BlockSpec((br, cols), lambda r, k: (r, 0))],
        out_specs=[pl.BlockSpec((1, br, cols), lambda r, k: (k[0], r, 0))],
        out_shape=[jax.ShapeDtypeStruct((NCHIP, rows, cols), BF)], vmem=32, args=[w])[0][0]


def _cast_place_multi(ws, chip_idx, stages=()):
    br = 128
    nblk = [a.shape[0] // br for a in ws]
    starts = [sum(nblk[:i]) for i in range(len(ws))]

    def body(k_ref, *refs):
        r = pl.program_id(0)
        for i in range(len(ws)):
            @pl.when(jnp.logical_and(r >= starts[i], r < starts[i] + nblk[i]))
            def _(i=i):
                refs[len(ws) + i][0] = refs[i][...].astype(BF)

    def at(i):
        return functools.partial(lambda r, s, nb: jnp.clip(r - s, 0, nb - 1), s=starts[i], nb=nblk[i])

    outs, landed = _call(
        body, name="cast_rest", grid=(sum(nblk),), prefetch=chip_idx,
        in_specs=[pl.BlockSpec((br, a.shape[1]), functools.partial(lambda r, k, f: (f(r), 0), f=at(i)))
                  for i, a in enumerate(ws)],
        out_specs=[pl.BlockSpec((1, br, a.shape[1]), functools.partial(lambda r, k, f: (k[0], f(r), 0), f=at(i)))
                   for i, a in enumerate(ws)],
        out_shape=[jax.ShapeDtypeStruct((NCHIP,) + a.shape, BF) for a in ws], vmem=32, args=list(ws), stages=stages)
    return outs, landed


def _add_sibling(g, land, cidx, name, stages=()):
    _, _, hr, cols = g.shape
    br = _row_block(hr, cols)

    def body(c_ref, g_ref, l_ref, o_ref):
        o_ref[...] = (g_ref[0, 0].astype(F32) + l_ref[0].astype(F32)).astype(BF)[None]

    outs, st = _call(
        body, name=name, grid=(NCHIP, hr // br), prefetch=cidx,
        in_specs=[pl.BlockSpec((1, 1, br, cols), lambda k, r, c: (k, c[0], r, 0)),
                  pl.BlockSpec((1, br, cols), lambda k, r, c: (k, r, 0))],
        out_specs=[pl.BlockSpec((1, br, cols), lambda k, r, c: (k, r, 0))],
        out_shape=[jax.ShapeDtypeStruct((NCHIP, hr, cols), BF)], vmem=32, args=[g, land], stages=stages)
    return outs[0], st


def _add_pair(a, b, name):
    rows, cols = a.shape

    def body(a_ref, b_ref, o_ref):
        o_ref[...] = a_ref[...] + b_ref[...]

    spec = pl.BlockSpec((rows, cols), lambda r: (0, 0))
    return _call(body, name=name, grid=(1,), in_specs=[spec, spec], out_specs=[spec], out_shape=[_sds(a)],
                 vmem=32, args=[a, b])[0][0]


def _add_chips(own, land, idx, name, stages=None):
    _, hr, cols = land.shape
    br = _row_block(hr, cols)

    def body(s_ref, a_ref, b_ref, c_ref, d_ref, o_ref):
        o_ref[...] = (a_ref[...].astype(F32) + b_ref[...].astype(F32)) + (c_ref[...].astype(F32) +
                                                                           d_ref[...].astype(F32))

    spec = lambda q: pl.BlockSpec((1, br, cols), functools.partial(lambda r, s, q: (s[q], r, 0), q=q))
    outs, landed = _call(
        body, name=name, grid=(hr // br,), prefetch=idx,
        in_specs=[spec(0), spec(1), spec(2), spec(3)], out_specs=[spec(4)],
        out_shape=[jax.ShapeDtypeStruct((2, hr, cols), F32)], vmem=48, args=[own, land, land, land],
        stages=stages or ())
    return outs[0] if stages is None else (outs[0], landed)


def _adamw_math(w, g, m, v):
    mn = ADAM_B1 * m + (1.0 - ADAM_B1) * g
    vn = ADAM_B2 * v + (1.0 - ADAM_B2) * (g * g)
    m_hat = mn / (1.0 - ADAM_B1 ** ADAM_STEP)
    v_hat = vn / (1.0 - ADAM_B2 ** ADAM_STEP)
    return -ADAM_LR * (m_hat / (jnp.sqrt(v_hat) + ADAM_EPS) + ADAM_WD * w), mn, vn


def _adamw(w, g, m, v, name, stages=()):
    rows, cols = w.shape
    br = _row_block(rows, cols)

    def body(w_ref, g_ref, m_ref, v_ref, d_ref, mo_ref, vo_ref):
        d_ref[...], mo_ref[...], vo_ref[...] = _adamw_math(w_ref[...], g_ref[...], m_ref[...], v_ref[...])

    spec = pl.BlockSpec((br, cols), lambda r: (r, 0))
    return _call(body, name=name, grid=(rows // br,), in_specs=[spec] * 4, out_specs=[spec] * 3,
                 out_shape=[_sds(w)] * 3, vmem=48, args=[w, g, m, v], stages=stages)


def _adamw_multi(names, w, g, m, v, stages=()):
    cols = w[names[0]].shape[1]
    br = 128
    nblk = [w[n].shape[0] // br for n in names]
    starts = [sum(nblk[:i]) for i in range(len(names))]

    def body(*refs):
        r = pl.program_id(0)
        for i in range(len(names)):
            w_ref, g_ref, m_ref, v_ref = refs[4 * i:4 * i + 4]
            d_ref, mo_ref, vo_ref = refs[4 * len(names) + 3 * i:4 * len(names) + 3 * i + 3]

            @pl.when(jnp.logical_and(r >= starts[i], r < starts[i] + nblk[i]))
            def _():
                d_ref[...], mo_ref[...], vo_ref[...] = _adamw_math(w_ref[...], g_ref[...], m_ref[...], v_ref[...])

    def spec(i):
        return pl.BlockSpec((br, cols), functools.partial(
            lambda r, s, nb: (jnp.clip(r - s, 0, nb - 1), 0), s=starts[i], nb=nblk[i]))

    outs, landed = _call(
        body, name="adamw_" + "_".join(names), grid=(sum(nblk),),
        in_specs=[spec(i) for i in range(len(names)) for _ in range(4)],
        out_specs=[spec(i) for i in range(len(names)) for _ in range(3)],
        out_shape=[_sds(w[n]) for n in names for _ in range(3)], vmem=48,
        args=[a[n] for n in names for a in (w, g, m, v)], stages=stages)
    return {n: outs[3 * i:3 * i + 3] for i, n in enumerate(names)}, landed


def _to_everyone(v):
    deltas = [(a, b, e) for a in (0, 1) for b in (0, 1) for e in (0, 1)][1:]

    def copies(ins, outs, sems):
        x, y, c, _ = _place()
        me = 4 * x + 2 * y + c
        flip = lambda p, f: 1 - p if f else p
        return [_rcopy(ins[0], outs[0].at[me], sems[0].at[q], sems[1].at[q], (flip(x, a), flip(y, b), flip(c, e)))
                for q, (a, b, e) in enumerate(deltas)]

    def start(ins, outs, sems):
        for cp in copies(ins, outs, sems):
            cp.start()

    def finish(ins, outs, sems):
        for cp in copies(ins, outs, sems):
            cp.wait()

    n = len(deltas)
    return _Stage([v], [jax.ShapeDtypeStruct((2 * NCHIP,) + v.shape, v.dtype)], {},
                  [pltpu.SemaphoreType.DMA((n,)), pltpu.SemaphoreType.DMA((n,))], start, finish)


SMALL_AT = {"norm_mix_pre": (0, 1, D), "norm_mix_post": (1, 1, D), "norm_mlp_pre": (2, 1, D),
            "norm_mlp_post": (3, 1, D), "b_gate": (4, 2, D), "conv_b": (6, 1, D), "lru_b_a": (7, 1, D),
            "lru_b_x": (8, 1, D), "lru_lambda": (9, 1, D), "pool_scale": (10, 1, DP)}
SMALL_SEPARATE = ["conv_w", "lru_w_a", "lru_w_x", "pool_w"]


def _adamw_small(small_sum, first_all, sep_grads, w, m, v):
    packed, sep = list(SMALL_AT), list(SMALL_SEPARATE)
    names = packed + sep

    def body(*refs):
        s_ref, a_ref, refs = refs[0], refs[1], refs[2:]
        g_sep, refs = refs[:len(sep)], refs[len(sep):]
        nn = len(names)
        w_r, m_r, v_r, refs = refs[:nn], refs[nn:2 * nn], refs[2 * nn:3 * nn], refs[3 * nn:]
        g_out, refs = refs[:len(packed)], refs[len(packed):]
        d_o, m_o, v_o = refs[:nn], refs[nn:2 * nn], refs[2 * nn:3 * nn]
        for i, n in enumerate(names):
            if i == 0:
                g = a_ref[0:1, :]
                for q in range(1, 2 * NCHIP):
                    g = g + a_ref[q:q + 1, :]
                g_out[i][...] = g
            elif n in SMALL_AT:
                r0, nr, nc = SMALL_AT[n]
                g = jnp.concatenate([s_ref[r0 + q:r0 + q + 1, :nc] for q in range(nr)], axis=1)
                g_out[i][...] = g
            else:
                g = g_sep[i - len(packed)][...]
            d_o[i][...], m_o[i][...], v_o[i][...] = _adamw_math(w_r[i][...], g, m_r[i][...], v_r[i][...])

    ws = [w[n] for n in names]
    res = pl.pallas_call(
        body, name="adamw_small",
        out_shape=[_sds(w[n]) for n in packed] + [_sds(a) for a in ws] * 3,
        compiler_params=_cp(32),
    )(*_hbm(small_sum, first_all, *sep_grads, *ws, *[m[n] for n in names], *[v[n] for n in names]))
    nn, npk = len(names), len(packed)
    grad = dict(zip(packed, res[:npk]))
    delta = dict(zip(names, res[npk:npk + nn]))
    new_m = dict(zip(names, res[npk + nn:npk + 2 * nn]))
    new_v = dict(zip(names, res[npk + 2 * nn:]))
    return grad, delta, new_m, new_v


W_NAMES = ["norm_mix_pre", "norm_mix_post", "norm_mlp_pre", "norm_mlp_post", "w_in", "b_gate", "conv_w", "conv_b",
           "lru_w_a", "lru_b_a", "lru_w_x", "lru_b_x", "lru_lambda", "pool_w", "pool_scale", "w_lru_up",
           "w_pool_up", "w_o", "w_ff1", "w_ff2"]
BIG = ["w_in", "w_lru_up", "w_pool_up", "w_o", "w_ff1", "w_ff2"]


def _block_diag(w):
    hd = w.shape[-1]
    per = CB // hd
    w4 = w.reshape(NG, per, hd, hd)
    eye = jnp.eye(per, dtype=w.dtype)
    return jnp.einsum("gpij,pq->gpiqj", w4, eye).reshape(NG, CB, CB)


def _block_diag_extract(d, hd):
    per = CB // hd
    d5 = d.reshape(NG, per, hd, per, hd)
    return jnp.stack([d5[:, p, :, p, :] for p in range(per)], axis=1).reshape(NG * per, hd, hd)


def _halves(g):
    return g.reshape(NCHIP, 2, g.size // (g.shape[-1] * 2 * NCHIP), g.shape[-1])


def kernel(x, norm_mix_pre, norm_mix_post, norm_mlp_pre, norm_mlp_post, w_in, b_gate, conv_w, conv_b, lru_w_a, lru_b_a, lru_w_x, lru_b_x, lru_lambda, pool_w, pool_scale, w_lru_up, w_pool_up, w_o, w_ff1, w_ff2, loss_target, m_norm_mix_pre, m_norm_mix_post, m_norm_mlp_pre, m_norm_mlp_post, m_w_in, m_b_gate, m_conv_w, m_conv_b, m_lru_w_a, m_lru_b_a, m_lru_w_x, m_lru_b_x, m_lru_lambda, m_pool_w, m_pool_scale, m_w_lru_up, m_w_pool_up, m_w_o, m_w_ff1, m_w_ff2, v_norm_mix_pre, v_norm_mix_post, v_norm_mlp_pre, v_norm_mlp_post, v_w_in, v_b_gate, v_conv_w, v_conv_b, v_lru_w_a, v_lru_b_a, v_lru_w_x, v_lru_b_x, v_lru_lambda, v_pool_w, v_pool_scale, v_w_lru_up, v_w_pool_up, v_w_o, v_w_ff1, v_w_ff2):
    args = dict(locals())
    two_d = lambda a: a.reshape(-1, a.shape[-1])
    w = {n: two_d(args[n]) for n in W_NAMES}
    mom = {n: two_d(args["m_" + n]) for n in W_NAMES}
    var = {n: two_d(args["v_" + n]) for n in W_NAMES}
    i32 = lambda val: jnp.asarray(val, jnp.int32)
    chip = i32(2 * lax.axis_index("x") + lax.axis_index("y"))
    core = i32(lax.axis_index("c"))
    cidx = core.reshape(1)
    zero = i32(0)
    hd = lru_w_a.shape[-1]
    xs, target = x[0], loss_target[0]
    g1, g2, g3, g4 = norm_mix_pre, norm_mix_post, norm_mlp_pre, norm_mlp_post

    mix = ["w_lru_up", "w_pool_up", "w_o"]
    full = {"w_in": _cast_place(w["w_in"], chip.reshape(1), "cast_w_in")}
    (fl_in, fl_conv), first = _split_call("gather_start_first", start=[
        _gather([full["w_in"]], ici=[(0, ALL)]), _gather_whole(w["conv_w"])])
    casts, _ = _cast_place_multi([w[n] for n in BIG[1:]], chip.reshape(1))
    full.update(zip(BIG[1:], casts))
    (fl_mix, fl_ff), started = _split_call("gather_start_rest", after=first, start=[
        _gather([full[n] for n in mix], ici=[(0, ALL), (1, ALL), (2, ALL)]),
        _gather([full["w_ff1"], full["w_ff2"]], ici=[(0, ALL), (1, ALL)])])
    wa = _block_diag(lru_w_a[0]).astype(BF)
    wx = _block_diag(lru_w_x[0]).astype(BF)
    pw = pool_w[0].astype(BF)

    def to_sibling(name, flight, after=None):
        fl, = _split_call(name + "_pass", finish=[flight], after=after,
                          start=[_gather(flight.landed(), d2d=[(i, ALL) for i in range(len(flight.bufs))])])[0]
        return fl

    def arrived(name, flight, after=None):
        _split_call(name + "_done", finish=[flight], after=after)
        return flight.landed()

    fl_in = to_sibling("gather_w_in", fl_in, after=started)
    w_in_f, = arrived("gather_w_in", fl_in)
    conv_all, = arrived("gather_conv", fl_conv)
    full["w_in"] = w_in_f
    conv_all = lax.dynamic_update_slice(conv_all, w["conv_w"][None], (chip, zero, zero))
    conv_full = jnp.transpose(conv_all, (1, 0, 2)).reshape(4, DR)
    (proj, h1), _ = _fwd_inproj(xs, g1, w_in_f)
    fl_mix = to_sibling("gather_mix", fl_mix, after=h1)
    (ylru, hs), _ = _fwd_lru(proj, conv_full, conv_b, wa, lru_b_a, wx, lru_b_x, lru_lambda)
    got = arrived("gather_mix", fl_mix, after=ylru)
    w_lru_up_f, w_pool_up_f, w_o_f = got[0].reshape(DR, D), got[1], got[2].reshape(D, D)
    ypool = _fwd_pool(proj, pw, pool_scale)
    (x2, h2, m, mrg, bra, brb), _ = _fwd_merge(xs, ylru, ypool, proj, b_gate, g2, g3, w_lru_up_f, w_pool_up_f, w_o_f)
    fl_ff = to_sibling("gather_ff", fl_ff, after=h2)
    ff1, ff2 = arrived("gather_ff", fl_ff)
    ff2 = ff2.reshape(DF, D)
    a1, f = _fwd_mlp(h2, ff1, ff2)
    lossp, dy, df, dg4 = _loss_head(f, x2, target, g4)

    idx_big = jnp.stack([chip, (chip + 1) % NCHIP, (chip + 2) % NCHIP, (chip + 3) % NCHIP, core])
    dh2, df1 = _bwd_mlp_x(df, a1, ff1, ff2)
    dw_ff1, dw_ff2 = _bwd_mlp_w(df, h2, a1, df1)
    g_ff = [_halves(dw_ff1), _halves(dw_ff2)]
    (dxres, dgates, dylru, dypool, dm, dbra, dbrb, dg2, dg3, dbg), (l_ff,) = _bwd_merge(
        dh2, dy, x2, m, bra, brb, proj, b_gate, g2, g3, w_lru_up_f, w_pool_up_f, w_o_f, stages=[_to_sibling(g_ff)])
    p_ff = [_add_sibling(g, l, cidx, "add_sibling_" + n)[0] for g, l, n in zip(g_ff, l_ff, ["w_ff1", "w_ff2"])]
    ff1_head, ff1_tail = (0, 7, 16), (7, 16, 16)
    (dw_o, dw_lru_up, dw_pool_up), ((c_ff1,),) = _dw_merge(
        mrg, dm, ylru, dbra, ypool, dbrb, stages=[_to_chips(p_ff[:1], parts=[ff1_head])])
    g_mix = [_halves(dw_lru_up), _halves(dw_pool_up), _halves(dw_o)]
    (dxp, dgl, dcw, dcb, dwa, dba, dwx, dbx, dlam), ((c_ff1, c_ff2), l_mix) = _bwd_lru(
        proj, hs, dylru, conv_full, conv_b, wa, lru_b_a, wx, lru_b_x, lru_lambda,
        stages=[_to_chips(p_ff, parts=[ff1_tail, ALL], lands=[c_ff1, None]), _to_sibling(g_mix)])
    p_mix = [_add_sibling(g, l, cidx, "add_sibling_" + n)[0] for g, l, n in zip(g_mix, l_mix, mix)]
    dxpool, dpw, dsc = _bwd_pool(proj, dypool, pw, pool_scale)
    dproj = jnp.concatenate([dxp, dgl, dxpool, dgates], axis=1)
    small = jnp.concatenate([
        jnp.zeros((1, D), F32), dg2, dg3, dg4, dbg.reshape(2, D), dcb, dba, dbx, dlam,
        jnp.pad(dsc, ((0, 0), (0, D - DP))), jnp.pad(lossp, ((0, 0), (0, D - 1))), dcw,
        _block_diag_extract(dwa, hd).reshape(-1, D), _block_diag_extract(dwx, hd).reshape(-1, D),
        dpw.reshape(-1, D)], axis=0)
    (dw_in, dh1), (c_mix, (l_small,)) = _bwd_inproj(h1, dproj, full["w_in"],
                                                     stages=[_to_chips(p_mix), _to_sibling([small])])
    small2 = _add_pair(small, l_small, "add_sibling_small").reshape(2, SMALL_ROWS // 2, D)
    g_in = _halves(dw_in)
    done = ["w_ff1", "w_ff2"] + mix
    pair_ff1, ((l_in,), (c_small,)) = _add_chips(p_ff[0], c_ff1, idx_big, "add_chips_w_ff1",
                                                 stages=[_to_sibling([g_in]), _to_chips([small2])])
    p_in = _add_sibling(g_in, l_in, cidx, "add_sibling_w_in")[0]
    ssem, rsem, p_in, c_in, token = _chips_start(p_in)
    g1_after = g1 + token[0:1, 0:1]
    pairs = [pair_ff1] + [_add_chips(p, l, idx_big, "add_chips_" + n)
                          for p, l, n in zip(p_ff[1:] + p_mix, [c_ff2] + c_mix, done[1:])]
    own_small = lax.dynamic_index_in_dim(small2, core, 0, keepdims=True)
    c_small = lax.dynamic_update_slice(c_small, own_small, (chip, zero, zero))
    pair_small = _add_chips(c_small, c_small, jnp.stack([zero, zero + 1, zero + 2, zero + 3, core]), "add_chips_small")
    (grad_x, dg1), _ = _bwd_prenorm(xs, dh1, dxres, g1_after)
    _, (shared, (dg1_all,)) = _call(lambda: None, name="reduce_share", grid=(1,), in_specs=[], out_specs=[],
                                    out_shape=[], args=[], stages=[_share(pairs + [pair_small]), _to_everyone(dg1)])
    pairs, pair_small = shared[:-1], shared[-1]
    dg1_all = lax.dynamic_update_slice(dg1_all, dg1[None], (2 * chip + core, zero, zero)).reshape(2 * NCHIP, D)

    grads, delta, new_m, new_v = {}, {}, {}, {}
    for n, p in zip(done, pairs):
        grads[n] = p.reshape(-1, p.shape[-1])

    def update(n, stages=()):
        (delta[n], new_m[n], new_v[n]), landed = _adamw(w[n], grads[n], mom[n], var[n], "adamw_" + n, stages=stages)
        return landed

    small_sum = pair_small.reshape(SMALL_ROWS, D)
    loss = 0.5 * small_sum[LOSS_ROW, 0]
    ccols = DR // NCHIP
    sep = [lax.dynamic_slice(small_sum[12:16], (zero, chip * ccols), (4, ccols)),
           small_sum[16:80].reshape(-1, hd), small_sum[80:144].reshape(-1, hd), small_sum[144:208].reshape(-1, PG)]
    g_s, d_s, m_s, v_s = _adamw_small(small_sum, dg1_all, sep, w, mom, var)
    grads.update(g_s)
    grads.update(dict(zip(SMALL_SEPARATE, sep)))
    delta.update(d_s)
    new_m.update(m_s)
    new_v.update(v_s)
    updated, _ = _adamw_multi(["w_ff1", "w_ff2", "w_o", "w_lru_up"], w, grads, mom, var)
    for n, (d, mo, vo) in updated.items():
        delta[n], new_m[n], new_v[n] = d, mo, vo
    p_in, c_in = _chips_wait(ssem, rsem, p_in, c_in, new_v["w_lru_up"])
    pair_in = _add_chips(p_in, c_in, idx_big, "add_chips_w_in")
    ((pair_in,),) = update("w_pool_up", stages=[_share([pair_in])])
    grads["w_in"] = pair_in.reshape(-1, pair_in.shape[-1])
    update("w_in")

    out = lambda d: [d[n].reshape(args[n].shape) for n in W_NAMES]
    return (loss, grad_x[None], *out(grads), *out(delta), *out(new_m), *out(new_v))
```

```python
import functools
import math

import jax
import jax.numpy as jnp
from jax import lax
from jax.experimental import pallas as pl
from jax.experimental.pallas import tpu as pltpu

F32 = jnp.float32
BF = jnp.bfloat16

T = 2048
D = 1024
DR = 1024
DP = 512
DF = 4096
DIN = 4608
NCHIP = 4
CW_IN = DIN // NCHIP
LANE = 128
CB = 128
NG = DR // CB
PG = 128
POOL_WINDOWS = (2, 4, 8, 16)
NORM_EPS = 1e-6
LRU_C = 8.0
GELU_C = math.sqrt(2.0 / math.pi)
ADAM_LR = 0.001
ADAM_B1 = 0.9
ADAM_B2 = 0.999
ADAM_EPS = 1e-08
ADAM_WD = 0.01
ADAM_STEP = 10
MESH_ID = pl.DeviceIdType.MESH
ANY = pl.BlockSpec(memory_space=pl.ANY)
SMALL_ROWS = 208
LOSS_ROW = 11
MIB = 1 << 20


def _cp(vmem_mib=None):
    if vmem_mib is None:
        return pltpu.CompilerParams()
    return pltpu.CompilerParams(vmem_limit_bytes=vmem_mib * MIB)


def _hbm(*arrays):
    return [pltpu.with_memory_space_constraint(a, pltpu.HBM) for a in arrays]


def _hbm_out(shapes):
    return [pltpu.HBM(s.shape, s.dtype) for s in shapes]


class _Stage:
    def __init__(self, operands, out_shape, alias, sems, start, finish):
        self.operands, self.out_shape, self.alias, self.sems = list(operands), list(out_shape), dict(alias), list(sems)
        self.start, self.finish = start, finish


def _call(body, *, name, grid, in_specs, out_specs, out_shape, args, vmem=None, stages=(), prefetch=None,
          scratch=()):
    nin, nout = len(in_specs), len(out_specs)
    npre = 0 if prefetch is None else 1
    st_args, st_shapes, st_sems, aliases = [], [], list(scratch), {}
    for st in stages:
        for k, v in st.alias.items():
            aliases[npre + nin + len(st_args) + k] = nout + len(st_shapes) + v
        st_args += st.operands
        st_shapes += st.out_shape
        st_sems += st.sems

    def wrapped(*refs):
        pre, refs = refs[:npre], refs[npre:]
        ins, pos = refs[:nin], nin
        st_ins = []
        for st in stages:
            st_ins.append(refs[pos:pos + len(st.operands)])
            pos += len(st.operands)
        outs, pos = refs[pos:pos + nout], pos + nout
        st_outs = []
        for st in stages:
            st_outs.append(refs[pos:pos + len(st.out_shape)])
            pos += len(st.out_shape)
        work, pos = refs[pos:pos + len(scratch)], pos + len(scratch)
        sems = []
        for st in stages:
            sems.append(refs[pos:pos + len(st.sems)])
            pos += len(st.sems)
        if stages:
            first = functools.reduce(jnp.logical_and, [pl.program_id(a) == 0 for a in range(len(grid))])

            @pl.when(first)
            def _():
                for st, a, b, s in zip(stages, st_ins, st_outs, sems):
                    st.start(a, b, s)

        body(*pre, *ins, *outs, *work)
        if stages:
            last = functools.reduce(jnp.logical_and, [pl.program_id(a) == g - 1 for a, g in enumerate(grid)])

            @pl.when(last)
            def _():
                for st, a, b, s in zip(stages, st_ins, st_outs, sems):
                    st.finish(a, b, s)

    all_in = list(in_specs) + [ANY] * len(st_args)
    all_out = list(out_specs) + [ANY] * len(st_shapes)
    kw = dict(has_side_effects=True) if stages else {}
    if vmem is not None:
        kw["vmem_limit_bytes"] = vmem * MIB
    if prefetch is None:
        gkw = dict(grid=grid, in_specs=all_in, out_specs=all_out, scratch_shapes=st_sems)
    else:
        gkw = dict(grid_spec=pltpu.PrefetchScalarGridSpec(
            num_scalar_prefetch=1, grid=grid, in_specs=all_in, out_specs=all_out, scratch_shapes=st_sems))
    res = pl.pallas_call(
        wrapped, name=name, out_shape=_hbm_out(list(out_shape) + st_shapes), input_output_aliases=aliases,
        compiler_params=pltpu.CompilerParams(**kw), **gkw,
    )(*([prefetch] if npre else []), *_hbm(*args, *st_args))
    outs, rest, st_res = list(res[:nout]), list(res[nout:]), []
    for st in stages:
        st_res.append(rest[:len(st.out_shape)])
        rest = rest[len(st.out_shape):]
    return outs, st_res


def _mm(a, b):
    return jnp.dot(a.astype(BF), b.astype(BF), preferred_element_type=F32)


def _mm_nt(a, b):
    return lax.dot_general(a.astype(BF), b.astype(BF), (((1,), (1,)), ((), ())),
                           preferred_element_type=F32)


def _mm_tn(a, b):
    return lax.dot_general(a.astype(BF), b.astype(BF), (((0,), (0,)), ((), ())),
                           preferred_element_type=F32)


def _rows(v):
    return lax.broadcasted_iota(jnp.int32, v.shape, 0)


def _sd(v, s, fill=0.0):
    return jnp.where(_rows(v) >= s, pltpu.roll(v, s, axis=0), fill)


def _su(v, s, fill=0.0):
    n = v.shape[0]
    return jnp.where(_rows(v) < n - s, pltpu.roll(v, n - s, axis=0), fill)


def _sigmoid(z):
    return 1.0 / (1.0 + jnp.exp(-z))


def _softplus(z):
    e = jnp.exp(-jnp.abs(z))
    u = 1.0 + e
    d = u - 1.0
    log1p = jnp.where(d == 0.0, e, jnp.log(u) * (e / jnp.where(d == 0.0, 1.0, d)))
    return jnp.maximum(z, 0.0) + log1p


def _mean(v):
    return jnp.mean(v, axis=-1, keepdims=True)


def _colsum(v):
    return jnp.sum(v, axis=0, keepdims=True)


def _acc(ref, val, first):
    @pl.when(first)
    def _():
        ref[...] = val

    @pl.when(jnp.logical_not(first))
    def _():
        ref[...] += val


def _conv(xp, cw, cb):
    x1, x2, x3 = _sd(xp, 1), _sd(xp, 2), _sd(xp, 3)
    xc = cb + cw[0:1] * x3 + cw[1:2] * x2 + cw[2:3] * x1 + cw[3:4] * xp
    return xc, x1, x2, x3


def _lru_gates(xc, wa, ba, wx, bx, lam):
    xcb = xc.astype(BF)
    r = _sigmoid(_mm(xcb, wa) + ba)
    ii = _sigmoid(_mm(xcb, wx) + bx)
    sp = _softplus(-lam)
    la = (-LRU_C) * r * sp
    a = jnp.exp(la)
    mult = jnp.sqrt(-jnp.tanh(la) * (a * a + 1.0))
    return xcb, r, ii, sp, a, mult


def _gelu_parts(g):
    th = jnp.tanh(GELU_C * (g + 0.044715 * (g * g * g)))
    gel = 0.5 * g * (1.0 + th)
    dgel = 0.5 * (1.0 + th) + 0.5 * g * (1.0 - th * th) * (GELU_C * (1.0 + 3.0 * 0.044715 * (g * g)))
    return gel, dgel


def _pool_window(x, steps, shift):
    s, sh = x, 1
    for _ in range(steps):
        s = s + shift(s, sh)
        sh *= 2
    return s


def _fwd_inproj(x, g1, w_in, stages=()):
    tm = 512

    def body(x_ref, g_ref, w_ref, proj_ref, h_ref):
        @pl.when(pl.program_id(1) == 0)
        def _():
            xv = x_ref[...]
            r = lax.rsqrt(_mean(xv * xv) + NORM_EPS)
            h_ref[...] = ((xv * r) * g_ref[...]).astype(BF)

        proj_ref[...] = jnp.dot(h_ref[...], w_ref[0], preferred_element_type=F32)

    return _call(
        body, name="fwd_inproj", grid=(T // tm, NCHIP),
        in_specs=[pl.BlockSpec((tm, D), lambda i, k: (i, 0)),
                  pl.BlockSpec((1, D), lambda i, k: (0, 0)),
                  pl.BlockSpec((1, D, CW_IN), lambda i, k: (k, 0, 0))],
        out_specs=[pl.BlockSpec((tm, CW_IN), lambda i, k: (i, k)),
                   pl.BlockSpec((tm, D), lambda i, k: (i, 0))],
        out_shape=[jax.ShapeDtypeStruct((T, DIN), F32), jax.ShapeDtypeStruct((T, D), BF)],
        vmem=40, args=[x, g1, w_in], stages=stages)


def _vec_spec():
    return pl.BlockSpec((1, CB), lambda j: (0, j))


def _fwd_lru(proj, conv_w, conv_b, wa, ba, wx, bx, lam, stages=()):
    def body(xp_ref, g_ref, cw_ref, cb_ref, wa_ref, ba_ref, wx_ref, bx_ref, lam_ref, y_ref, h_ref):
        xc, _, _, _ = _conv(xp_ref[...], cw_ref[...], cb_ref[...])
        _, _, ii, _, a, mult = _lru_gates(xc, wa_ref[0], ba_ref[...], wx_ref[0], bx_ref[...], lam_ref[...])
        b = mult * (ii * xc)
        s = 1
        while s < T:
            b = b + a * _sd(b, s, 0.0)
            if 2 * s < T:
                a = a * _sd(a, s, 1.0)
            s *= 2
        h_ref[...] = b
        gel, _ = _gelu_parts(g_ref[...])
        y_ref[...] = (b * gel).astype(BF)

    return _call(
        body, name="fwd_lru", grid=(NG,),
        in_specs=[pl.BlockSpec((T, CB), lambda j: (0, j)),
                  pl.BlockSpec((T, CB), lambda j: (0, NG + j)),
                  pl.BlockSpec((4, CB), lambda j: (0, j)),
                  _vec_spec(),
                  pl.BlockSpec((1, CB, CB), lambda j: (j, 0, 0)), _vec_spec(),
                  pl.BlockSpec((1, CB, CB), lambda j: (j, 0, 0)), _vec_spec(),
                  _vec_spec()],
        out_specs=[pl.BlockSpec((T, CB), lambda j: (0, j)), pl.BlockSpec((T, CB), lambda j: (0, j))],
        out_shape=[jax.ShapeDtypeStruct((T, DR), BF), jax.ShapeDtypeStruct((T, DR), F32)],
        vmem=48, args=[proj, proj, conv_w, conv_b, wa, ba, wx, bx, lam], stages=stages)


def _pool_cnt(w):
    t = lax.broadcasted_iota(jnp.int32, (T, 1), 0)
    return jnp.minimum(t + 1, w).astype(F32)


def _fwd_pool(proj, pool_w, pool_scale):
    def body(xp_ref, pw_ref, sc_ref, y_ref):
        for g, w in enumerate(POOL_WINDOWS):
            cols = slice(g * PG, (g + 1) * PG)
            x = xp_ref[:, cols]
            p = _pool_window(x, g + 1, _sd) / _pool_cnt(w) - x
            y_ref[:, cols] = (_mm(p, pw_ref[g]) * sc_ref[:, cols]).astype(BF)

    return pl.pallas_call(
        body, name="fwd_pool", grid=(1,),
        in_specs=[pl.BlockSpec((T, DP), lambda i: (0, 2 * DR // DP)),
                  pl.BlockSpec((4, PG, PG), lambda i: (0, 0, 0)),
                  pl.BlockSpec((1, DP), lambda i: (0, 0))],
        out_specs=pl.BlockSpec((T, DP), lambda i: (0, 0)),
        out_shape=pltpu.HBM((T, DP), BF),
        compiler_params=_cp(48),
    )(*_hbm(proj, pool_w, pool_scale))


GATE_BLK = 512
GATE_BLK0 = (2 * DR + DP) // GATE_BLK


def _gate_specs(tm):
    return [pl.BlockSpec((tm, GATE_BLK), functools.partial(lambda i, q: (i, GATE_BLK0 + q), q=q))
            for q in range(4)]


def _fwd_merge(x, ylru, ypool, proj, b_gate, g2, g3, w_lru_up, w_pool_up, w_o, stages=()):
    tm = 512

    def body(x_ref, yl_ref, yp_ref, p0, p1, p2, p3, bg_ref, g2_ref, g3_ref, wl_ref, wp_ref, wo_ref,
             x2_ref, h2_ref, m_ref, mrg_ref, bra_ref, brb_ref):
        bra = jnp.dot(yl_ref[...], wl_ref[...], preferred_element_type=F32)
        yp = yp_ref[...]
        brb = jnp.concatenate([jnp.dot(yp, wp_ref[k], preferred_element_type=F32) for k in range(NCHIP)], axis=1)
        bg = bg_ref[...]
        ga = _sigmoid(jnp.concatenate([p0[...], p1[...]], axis=1) + bg[:, :D])
        gb = _sigmoid(jnp.concatenate([p2[...], p3[...]], axis=1) + bg[:, D:])
        mrg = (ga * bra + gb * brb).astype(BF)
        m = jnp.dot(mrg, wo_ref[...], preferred_element_type=F32)
        r2 = lax.rsqrt(_mean(m * m) + NORM_EPS)
        x2 = x_ref[...] + (m * r2) * g2_ref[...]
        r3 = lax.rsqrt(_mean(x2 * x2) + NORM_EPS)
        x2_ref[...] = x2
        h2_ref[...] = ((x2 * r3) * g3_ref[...]).astype(BF)
        m_ref[...] = m
        mrg_ref[...] = mrg
        bra_ref[...] = bra.astype(BF)
        brb_ref[...] = brb.astype(BF)

    row = lambda w: pl.BlockSpec((tm, w), lambda i: (i, 0))
    full2 = lambda a, b: pl.BlockSpec((a, b), lambda i: (0, 0))
    return _call(
        body, name="fwd_merge", grid=(T // tm,),
        in_specs=[row(D), row(DR), row(DP)] + _gate_specs(tm) +
                 [full2(1, 2 * D), full2(1, D), full2(1, D), full2(DR, D),
                  pl.BlockSpec((NCHIP, DP, D // NCHIP), lambda i: (0, 0, 0)), full2(D, D)],
        out_specs=[row(D)] * 6,
        out_shape=[jax.ShapeDtypeStruct((T, D), F32), jax.ShapeDtypeStruct((T, D), BF),
                   jax.ShapeDtypeStruct((T, D), F32), jax.ShapeDtypeStruct((T, D), BF),
                   jax.ShapeDtypeStruct((T, D), BF), jax.ShapeDtypeStruct((T, D), BF)],
        vmem=48, args=[x, ylru, ypool, proj, proj, proj, proj, b_gate, g2, g3, w_lru_up, w_pool_up, w_o],
        stages=stages)


def _fwd_mlp(h2, w_ff1, w_ff2):
    tm = 512
    fk = DF // NCHIP

    def body(h_ref, w1_ref, w2_ref, a1_ref, f_ref):
        h = h_ref[...]
        f = None
        for k in range(NCHIP):
            a1 = jnp.maximum(jnp.dot(h, w1_ref[k], preferred_element_type=F32), 0.0)
            a1_ref[:, k * fk:(k + 1) * fk] = a1.astype(BF)
            part = jnp.dot((a1 * a1).astype(BF), w2_ref[k * fk:(k + 1) * fk, :], preferred_element_type=F32)
            f = part if f is None else f + part
        f_ref[...] = f

    return pl.pallas_call(
        body, name="fwd_mlp", grid=(T // tm,),
        in_specs=[pl.BlockSpec((tm, D), lambda i: (i, 0)),
                  pl.BlockSpec((NCHIP, D, fk), lambda i: (0, 0, 0)),
                  pl.BlockSpec((DF, D), lambda i: (0, 0))],
        out_specs=[pl.BlockSpec((tm, DF), lambda i: (i, 0)), pl.BlockSpec((tm, D), lambda i: (i, 0))],
        out_shape=_hbm_out([jax.ShapeDtypeStruct((T, DF), BF), jax.ShapeDtypeStruct((T, D), F32)]),
        compiler_params=_cp(56),
    )(*_hbm(h2, w_ff1, w_ff2))


def _loss_head(f, x2, target, g4):
    tm = 512

    def body(f_ref, x2_ref, t_ref, g_ref, loss_ref, dy_ref, df_ref, dg_ref):
        first = pl.program_id(0) == 0
        f = f_ref[...]
        g4v = g_ref[...]
        r4 = lax.rsqrt(_mean(f * f) + NORM_EPS)
        fn = f * r4
        e = (x2_ref[...] + fn * g4v) - t_ref[...]
        _acc(loss_ref, jnp.sum(_mean(e * e), axis=0, keepdims=True), first)
        dy = e * (1.0 / D)
        dy_ref[...] = dy
        _acc(dg_ref, _colsum(dy * fn), first)
        dfn = dy * g4v
        df_ref[...] = (r4 * (dfn - fn * _mean(dfn * fn))).astype(BF)

    row = pl.BlockSpec((tm, D), lambda i: (i, 0))
    return pl.pallas_call(
        body, name="loss_head", grid=(T // tm,),
        in_specs=[row, row, row, pl.BlockSpec((1, D), lambda i: (0, 0))],
        out_specs=[pl.BlockSpec((1, 1), lambda i: (0, 0)), row, row, pl.BlockSpec((1, D), lambda i: (0, 0))],
        out_shape=_hbm_out([jax.ShapeDtypeStruct((1, 1), F32), jax.ShapeDtypeStruct((T, D), F32),
                            jax.ShapeDtypeStruct((T, D), BF), jax.ShapeDtypeStruct((1, D), F32)]),
        compiler_params=_cp(48),
    )(*_hbm(f, x2, target, g4))


def _bwd_mlp_x(df, a1, w_ff1, w_ff2):
    tm = 512
    fk = DF // NCHIP

    def body(df_ref, a1_ref, w1_ref, w2_ref, dh_ref, df1_ref):
        df = df_ref[...]
        dh = None
        for k in range(NCHIP):
            cols = slice(k * fk, (k + 1) * fk)
            dact = _mm_nt(df, w2_ref[cols, :])
            df1 = (dact * (2.0 * a1_ref[:, cols].astype(F32))).astype(BF)
            df1_ref[:, cols] = df1
            part = _mm_nt(df1, w1_ref[k])
            dh = part if dh is None else dh + part
        dh_ref[...] = dh

    return pl.pallas_call(
        body, name="bwd_mlp_x", grid=(T // tm,),
        in_specs=[pl.BlockSpec((tm, D), lambda i: (i, 0)),
                  pl.BlockSpec((tm, DF), lambda i: (i, 0)),
                  pl.BlockSpec((NCHIP, D, fk), lambda i: (0, 0, 0)),
                  pl.BlockSpec((DF, D), lambda i: (0, 0))],
        out_specs=[pl.BlockSpec((tm, D), lambda i: (i, 0)), pl.BlockSpec((tm, DF), lambda i: (i, 0))],
        out_shape=_hbm_out([jax.ShapeDtypeStruct((T, D), F32), jax.ShapeDtypeStruct((T, DF), BF)]),
        compiler_params=_cp(56),
    )(*_hbm(df, a1, w_ff1, w_ff2))


def _bwd_mlp_w(df, h2, a1, df1):
    fc = 512
    per = (DF // NCHIP) // fc

    def body(df_ref, h_ref, a1_ref, df1_ref, dw1_ref, dw2_ref):
        a1 = a1_ref[...].astype(F32)
        dw2_ref[...] = _mm_tn((a1 * a1).astype(BF), df_ref[...]).astype(BF)
        dw1_ref[0] = _mm_tn(h_ref[...], df1_ref[...]).astype(BF)

    return pl.pallas_call(
        body, name="bwd_mlp_w", grid=(DF // fc,),
        in_specs=[pl.BlockSpec((T, D), lambda j: (0, 0)),
                  pl.BlockSpec((T, D), lambda j: (0, 0)),
                  pl.BlockSpec((T, fc), lambda j: (0, j)),
                  pl.BlockSpec((T, fc), lambda j: (0, j))],
        out_specs=[pl.BlockSpec((1, D, fc), lambda j: (j // per, 0, j % per)),
                   pl.BlockSpec((fc, D), lambda j: (j, 0))],
        out_shape=_hbm_out([jax.ShapeDtypeStruct((NCHIP, D, DF // NCHIP), BF),
                            jax.ShapeDtypeStruct((DF, D), BF)]),
        compiler_params=_cp(56),
    )(*_hbm(df, h2, a1, df1))


def _bwd_merge(dh2, dy, x2, m, bra, brb, proj, b_gate, g2, g3, w_lru_up, w_pool_up, w_o, stages=()):
    tm = 256
    cpu = D // NCHIP

    def body(dh2_ref, dy_ref, x2_ref, m_ref, bra_ref, brb_ref, p0, p1, p2, p3, bg_ref,
             g2_ref, g3_ref, wl_ref, wp_ref, wo_ref,
             dx_ref, dgt_ref, dyl_ref, dyp_ref, dm_ref, dbra_ref, dbrb_ref, dg2_ref, dg3_ref, dbg_ref):
        first = pl.program_id(0) == 0
        x2 = x2_ref[...]
        r3 = lax.rsqrt(_mean(x2 * x2) + NORM_EPS)
        x2n = x2 * r3
        dh2 = dh2_ref[...]
        t3 = dh2 * g3_ref[...]
        dx2 = dy_ref[...] + r3 * (t3 - x2n * _mean(t3 * x2n))
        dx_ref[...] = dx2
        _acc(dg3_ref, _colsum(dh2 * x2n), first)
        m = m_ref[...]
        r2 = lax.rsqrt(_mean(m * m) + NORM_EPS)
        mn = m * r2
        _acc(dg2_ref, _colsum(dx2 * mn), first)
        dmn = dx2 * g2_ref[...]
        dm = (r2 * (dmn - mn * _mean(dmn * mn))).astype(BF)
        dm_ref[...] = dm
        dmrg = _mm_nt(dm, wo_ref[...])
        bg = bg_ref[...]
        ga = _sigmoid(jnp.concatenate([p0[...], p1[...]], axis=1) + bg[:, :D])
        gb = _sigmoid(jnp.concatenate([p2[...], p3[...]], axis=1) + bg[:, D:])
        dga = dmrg * bra_ref[...].astype(F32) * (ga * (1.0 - ga))
        dgb = dmrg * brb_ref[...].astype(F32) * (gb * (1.0 - gb))
        dgt_ref[:, :D] = dga.astype(BF)
        dgt_ref[:, D:] = dgb.astype(BF)
        _acc(dbg_ref, jnp.concatenate([_colsum(dga), _colsum(dgb)], axis=1), first)
        dbra = (dmrg * ga).astype(BF)
        dbrb = (dmrg * gb).astype(BF)
        dbra_ref[...] = dbra
        dbrb_ref[...] = dbrb
        dyl_ref[...] = _mm_nt(dbra, wl_ref[...])
        dyp = None
        for k in range(NCHIP):
            part = _mm_nt(dbrb[:, k * cpu:(k + 1) * cpu], wp_ref[k])
            dyp = part if dyp is None else dyp + part
        dyp_ref[...] = dyp

    row = lambda w: pl.BlockSpec((tm, w), lambda i: (i, 0))
    full2 = lambda a, b: pl.BlockSpec((a, b), lambda i: (0, 0))
    wp_spec = pl.BlockSpec((NCHIP, DP, cpu), lambda i: (0, 0, 0))
    return _call(
        body, name="bwd_merge", grid=(T // tm,),
        in_specs=[row(D)] * 6 + _gate_specs(tm) +
                 [full2(1, 2 * D), full2(1, D), full2(1, D), full2(DR, D), wp_spec, full2(D, D)],
        out_specs=[row(D), row(2 * D), row(DR), row(DP), row(D), row(D), row(D),
                   full2(1, D), full2(1, D), full2(1, 2 * D)],
        out_shape=[jax.ShapeDtypeStruct((T, D), F32), jax.ShapeDtypeStruct((T, 2 * D), BF),
                   jax.ShapeDtypeStruct((T, DR), F32), jax.ShapeDtypeStruct((T, DP), F32),
                   jax.ShapeDtypeStruct((T, D), BF), jax.ShapeDtypeStruct((T, D), BF),
                   jax.ShapeDtypeStruct((T, D), BF),
                   jax.ShapeDtypeStruct((1, D), F32), jax.ShapeDtypeStruct((1, D), F32),
                   jax.ShapeDtypeStruct((1, 2 * D), F32)],
        vmem=56, args=[dh2, dy, x2, m, bra, brb, proj, proj, proj, proj, b_gate, g2, g3, w_lru_up, w_pool_up, w_o],
        stages=stages)


def _dw_merge(mrg, dm, ylru, dbra, ypool, dbrb, stages=()):
    nb = NCHIP
    rb, pb, cpu = D // nb, DP // nb, D // NCHIP

    def body(mrg_ref, dm_ref, yl_ref, dbra_ref, yp_ref, dbrb_ref, dwo_ref, dwl_ref, dwp_ref):
        dwo_ref[...] = _mm_tn(mrg_ref[...], dm_ref[...]).astype(BF)
        dwl_ref[...] = _mm_tn(yl_ref[...], dbra_ref[...]).astype(BF)
        dwp = _mm_tn(yp_ref[...], dbrb_ref[...]).astype(BF)
        for k in range(NCHIP):
            dwp_ref[k] = dwp[:, k * cpu:(k + 1) * cpu]

    cols = lambda w: pl.BlockSpec((T, w), lambda r: (0, r))
    whole = pl.BlockSpec((T, D), lambda r: (0, 0))
    return _call(
        body, name="dw_merge", grid=(nb,),
        in_specs=[cols(rb), whole, cols(rb), whole, cols(pb), whole],
        out_specs=[pl.BlockSpec((rb, D), lambda r: (r, 0)), pl.BlockSpec((rb, D), lambda r: (r, 0)),
                   pl.BlockSpec((NCHIP, pb, cpu), lambda r: (0, r, 0))],
        out_shape=[jax.ShapeDtypeStruct((D, D), BF), jax.ShapeDtypeStruct((DR, D), BF),
                   jax.ShapeDtypeStruct((NCHIP, DP, cpu), BF)],
        vmem=56, args=[mrg, dm, ylru, dbra, ypool, dbrb], stages=stages)


def _bwd_lru(proj, h, dylru, conv_w, conv_b, wa, ba, wx, bx, lam, stages=()):
    def body(xp_ref, g_ref, h_ref, dy_ref, cw_ref, cb_ref, wa_ref, ba_ref, wx_ref, bx_ref, lam_ref,
             dxp_ref, dg_ref, dcw_ref, dcb_ref, dwa_ref, dba_ref, dwx_ref, dbx_ref, dlam_ref):
        xp = xp_ref[...]
        cw = cw_ref[...]
        lam = lam_ref[...]
        xc, x1, x2, x3 = _conv(xp, cw, cb_ref[...])
        wa, wx = wa_ref[0], wx_ref[0]
        xcb, r, ii, sp, a, mult = _lru_gates(xc, wa, ba_ref[...], wx, bx_ref[...], lam)
        g = g_ref[...]
        gel, dgel = _gelu_parts(g)
        h = h_ref[...]
        dy = dy_ref[...]
        dg_ref[...] = (dy * h * dgel).astype(BF)
        b = dy * gel
        aa = _su(a, 1, 0.0)
        s = 1
        while s < T:
            b = b + aa * _su(b, s, 0.0)
            if 2 * s < T:
                aa = aa * _su(aa, s, 0.0)
            s *= 2
        da = b * _sd(h, 1, 0.0)
        dmult = b * (ii * xc)
        dii = b * (mult * xc)
        dxc = b * (mult * ii)
        dla = da * a - dmult * ((a * a) / mult)
        dr = dla * ((-LRU_C) * sp)
        dsp = _colsum(dla * ((-LRU_C) * r))
        dlam_ref[...] = -dsp / (1.0 + jnp.exp(lam))
        dzr = dr * (r * (1.0 - r))
        dzi = dii * (ii * (1.0 - ii))
        dzrb, dzib = dzr.astype(BF), dzi.astype(BF)
        dxc = dxc + _mm_nt(dzrb, wa) + _mm_nt(dzib, wx)
        dwa_ref[0] = _mm_tn(xcb, dzrb)
        dwx_ref[0] = _mm_tn(xcb, dzib)
        dba_ref[...] = _colsum(dzr)
        dbx_ref[...] = _colsum(dzi)
        dcb_ref[...] = _colsum(dxc)
        dcw_ref[...] = jnp.concatenate([_colsum(dxc * x3), _colsum(dxc * x2), _colsum(dxc * x1),
                                        _colsum(dxc * xp)], axis=0)
        dxp = cw[3:4] * dxc + cw[2:3] * _su(dxc, 1) + cw[1:2] * _su(dxc, 2) + cw[0:1] * _su(dxc, 3)
        dxp_ref[...] = dxp.astype(BF)

    blk = pl.BlockSpec((T, CB), lambda j: (0, j))
    wsp = pl.BlockSpec((1, CB, CB), lambda j: (j, 0, 0))
    return _call(
        body, name="bwd_lru", grid=(NG,),
        in_specs=[blk, pl.BlockSpec((T, CB), lambda j: (0, NG + j)), blk, blk,
                  pl.BlockSpec((4, CB), lambda j: (0, j)), _vec_spec(), wsp, _vec_spec(), wsp, _vec_spec(),
                  _vec_spec()],
        out_specs=[blk, blk, pl.BlockSpec((4, CB), lambda j: (0, j)), _vec_spec(), wsp, _vec_spec(), wsp,
                   _vec_spec(), _vec_spec()],
        out_shape=[jax.ShapeDtypeStruct((T, DR), BF), jax.ShapeDtypeStruct((T, DR), BF),
                   jax.ShapeDtypeStruct((4, DR), F32), jax.ShapeDtypeStruct((1, DR), F32),
                   jax.ShapeDtypeStruct((NG, CB, CB), F32), jax.ShapeDtypeStruct((1, DR), F32),
                   jax.ShapeDtypeStruct((NG, CB, CB), F32), jax.ShapeDtypeStruct((1, DR), F32),
                   jax.ShapeDtypeStruct((1, DR), F32)],
        vmem=56, args=[proj, proj, h, dylru, conv_w, conv_b, wa, ba, wx, bx, lam], stages=stages)


def _bwd_pool(proj, dypool, pool_w, pool_scale):
    def body(xp_ref, dy_ref, pw_ref, sc_ref, dx_ref, dw_ref, dsc_ref):
        for g, w in enumerate(POOL_WINDOWS):
            cols = slice(g * PG, (g + 1) * PG)
            cnt = _pool_cnt(w)
            x = xp_ref[:, cols]
            pb = (_pool_window(x, g + 1, _sd) / cnt - x).astype(BF)
            wg = pw_ref[g]
            dy = dy_ref[:, cols]
            dsc_ref[:, cols] = _colsum(dy * _mm(pb, wg))
            dyp = (dy * sc_ref[:, cols]).astype(BF)
            dw_ref[g] = _mm_tn(pb, dyp)
            dp = _mm_nt(dyp, wg)
            dx_ref[:, cols] = (_pool_window(dp / cnt, g + 1, _su) - dp).astype(BF)

    return pl.pallas_call(
        body, name="bwd_pool", grid=(1,),
        in_specs=[pl.BlockSpec((T, DP), lambda i: (0, 2 * DR // DP)),
                  pl.BlockSpec((T, DP), lambda i: (0, 0)),
                  pl.BlockSpec((4, PG, PG), lambda i: (0, 0, 0)),
                  pl.BlockSpec((1, DP), lambda i: (0, 0))],
        out_specs=[pl.BlockSpec((T, DP), lambda i: (0, 0)),
                   pl.BlockSpec((4, PG, PG), lambda i: (0, 0, 0)),
                   pl.BlockSpec((1, DP), lambda i: (0, 0))],
        out_shape=_hbm_out([jax.ShapeDtypeStruct((T, DP), BF), jax.ShapeDtypeStruct((4, PG, PG), F32),
                            jax.ShapeDtypeStruct((1, DP), F32)]),
        compiler_params=_cp(48),
    )(*_hbm(proj, dypool, pool_w, pool_scale))


def _bwd_inproj(h1, dproj, w_in, stages=()):
    def body(h_ref, dp_ref, w_ref, dw_ref, dh_ref):
        dp = dp_ref[...]
        dw_ref[0] = _mm_tn(h_ref[...], dp).astype(BF)
        _acc(dh_ref, _mm_nt(dp, w_ref[0]), pl.program_id(0) == 0)

    return _call(
        body, name="bwd_inproj", grid=(NCHIP,),
        in_specs=[pl.BlockSpec((T, D), lambda k: (0, 0)),
                  pl.BlockSpec((T, CW_IN), lambda k: (0, k)),
                  pl.BlockSpec((1, D, CW_IN), lambda k: (k, 0, 0))],
        out_specs=[pl.BlockSpec((1, D, CW_IN), lambda k: (k, 0, 0)), pl.BlockSpec((T, D), lambda k: (0, 0))],
        out_shape=[jax.ShapeDtypeStruct((NCHIP, D, CW_IN), BF), jax.ShapeDtypeStruct((T, D), F32)],
        vmem=56, args=[h1, dproj, w_in], stages=stages)


def _bwd_prenorm(x, dh1, dxres, g1, stages=()):
    tm = 512

    def body(x_ref, dh_ref, dr_ref, g_ref, dx_ref, dg_ref):
        xv = x_ref[...]
        r = lax.rsqrt(_mean(xv * xv) + NORM_EPS)
        xn = xv * r
        dh = dh_ref[...]
        t = dh * g_ref[...]
        dx_ref[...] = dr_ref[...] + r * (t - xn * _mean(t * xn))
        _acc(dg_ref, _colsum(dh * xn), pl.program_id(0) == 0)

    row = pl.BlockSpec((tm, D), lambda i: (i, 0))
    vec = pl.BlockSpec((1, D), lambda i: (0, 0))
    return _call(
        body, name="bwd_prenorm", grid=(T // tm,),
        in_specs=[row, row, row, vec], out_specs=[row, vec],
        out_shape=[jax.ShapeDtypeStruct((T, D), F32), jax.ShapeDtypeStruct((1, D), F32)],
        vmem=48, args=[x, dh1, dxres, g1], stages=stages)


def _place():
    x, y, c = lax.axis_index("x"), lax.axis_index("y"), lax.axis_index("c")
    chips = [(1 - x, y), (x, 1 - y), (1 - x, 1 - y)]
    return x, y, c, chips


def _rcopy(src, dst, ssem, rsem, dev):
    return pltpu.make_async_remote_copy(src_ref=src, dst_ref=dst, send_sem=ssem, recv_sem=rsem,
                                        device_id=dev, device_id_type=MESH_ID)


def _sds(a):
    return jax.ShapeDtypeStruct(a.shape, a.dtype)


def _sem2(n, m):
    return [pltpu.SemaphoreType.DMA((n * m,)), pltpu.SemaphoreType.DMA((n * m,))]


ALL = (0, 1, 1)


def _piece(ref, k, half, part):
    hr = ref.shape[1] // 2
    r0, r1 = hr * part[0] // part[2], hr * part[1] // part[2]
    return ref.at[k, pl.ds(half * hr + r0, r1 - r0), :]


def _gather(fulls, ici=(), d2d=()):
    n = len(fulls)
    ici, d2d = list(ici), list(d2d)
    pieces = [("ici", i, part) for i, part in ici] + [("d2d", i, part) for i, part in d2d]

    def copies(outs, sems):
        x, y, c, chips = _place()
        me = 2 * x + y
        sib = (x, y, 1 - c)
        send, recv = [], []
        for q, (kind, i, part) in enumerate(pieces):
            for j, chip in enumerate(chips):
                k, s = 2 * chip[0] + chip[1], 3 * q + j
                if kind == "ici":
                    mine, theirs, dev = _piece(outs[i], me, c, part), _piece(outs[i], k, c, part), (*chip, c)
                else:
                    mine, theirs, dev = _piece(outs[i], k, c, part), _piece(outs[i], k, 1 - c, part), sib
                send.append(_rcopy(mine, mine, sems[0].at[s], sems[1].at[s], dev))
                recv.append(_rcopy(theirs, theirs, sems[0].at[s], sems[1].at[s], dev))
        return send, recv

    def start(ins, outs, sems):
        for cp in copies(outs, sems)[0]:
            cp.start()

    def finish(ins, outs, sems):
        send, recv = copies(outs, sems)
        for cp in recv:
            cp.wait_recv()
        for cp in send:
            cp.wait_send()

    sems = [pltpu.SemaphoreType.DMA((3 * len(pieces),)), pltpu.SemaphoreType.DMA((3 * len(pieces),))]
    return _Stage(fulls, [_sds(f) for f in fulls], {i: i for i in range(n)}, sems, start, finish)


def _gather_whole(v):
    def copies(ins, outs, sems):
        x, y, c, chips = _place()
        me = 2 * x + y
        send = [_rcopy(ins[0], outs[0].at[me], sems[0].at[j], sems[1].at[j], (*chip, c))
                for j, chip in enumerate(chips)]
        recv = [_rcopy(ins[0], outs[0].at[2 * chip[0] + chip[1]], sems[0].at[j], sems[1].at[j], (*chip, c))
                for j, chip in enumerate(chips)]
        return send, recv

    def start(ins, outs, sems):
        for cp in copies(ins, outs, sems)[0]:
            cp.start()

    def finish(ins, outs, sems):
        send, recv = copies(ins, outs, sems)
        for cp in recv:
            cp.wait_recv()
        for cp in send:
            cp.wait_send()

    return _Stage([v], [jax.ShapeDtypeStruct((NCHIP,) + v.shape, v.dtype)], {},
                  [pltpu.SemaphoreType.DMA((3,)), pltpu.SemaphoreType.DMA((3,))], start, finish)


def _to_sibling(srcs):
    n = len(srcs)

    def copies(ins, outs, sems):
        x, y, c, _ = _place()
        sib = (x, y, 1 - c)
        return [_rcopy(ins[i].at[:, 1 - c] if srcs[i].ndim == 4 else ins[i], outs[i], sems[0].at[i], sems[1].at[i], sib)
                for i in range(n)]

    def start(ins, outs, sems):
        for cp in copies(ins, outs, sems):
            cp.start()

    def finish(ins, outs, sems):
        for cp in copies(ins, outs, sems):
            cp.wait()

    shapes = [jax.ShapeDtypeStruct((NCHIP,) + s.shape[2:] if s.ndim == 4 else s.shape, s.dtype) for s in srcs]
    return _Stage(srcs, shapes, {}, [pltpu.SemaphoreType.DMA((n,)), pltpu.SemaphoreType.DMA((n,))], start, finish)


def _to_chips(srcs, parts=None, lands=None):
    n = len(srcs)
    parts = [ALL] * n if parts is None else parts
    lands = [None] * n if lands is None else lands
    given = [i for i in range(n) if lands[i] is not None]

    def rows(ref, i):
        hr = srcs[i].shape[1]
        r0, r1 = hr * parts[i][0] // parts[i][2], hr * parts[i][1] // parts[i][2]
        return ref.at[pl.ds(r0, r1 - r0), :]

    def copies(ins, outs, sems):
        x, y, c, chips = _place()
        me = 2 * x + y
        return [_rcopy(rows(ins[i].at[2 * chip[0] + chip[1]] if srcs[i].shape[0] == NCHIP else ins[i].at[c], i),
                       rows(outs[i].at[me], i), sems[0].at[3 * i + j], sems[1].at[3 * i + j], (*chip, c))
                for i in range(n) for j, chip in enumerate(chips)]

    def start(ins, outs, sems):
        for cp in copies(ins, outs, sems):
            cp.start()

    def finish(ins, outs, sems):
        for cp in copies(ins, outs, sems):
            cp.wait()

    shapes = [jax.ShapeDtypeStruct((NCHIP,) + s.shape[1:], s.dtype) for s in srcs]
    alias = {n + q: i for q, i in enumerate(given)}
    return _Stage(list(srcs) + [lands[i] for i in given], shapes, alias, _sem2(n, 3), start, finish)


HBM_REF = pl.BlockSpec(memory_space=pltpu.HBM)
SEM_REF = pl.BlockSpec(memory_space=pltpu.SEMAPHORE)
DATAFLOW = pltpu.SideEffectType.DATAFLOW_SIDE_EFFECTING


def _after(x):
    return _Stage([x], [], {}, [], lambda *a: None, lambda *a: None)


class _Flight:
    def __init__(self, stage, sems, bufs):
        self.stage, self.sems, self.bufs = stage, list(sems), list(bufs)

    def landed(self):
        st, n = self.stage, len(self.stage.operands)
        fresh = [j for j in range(len(st.out_shape)) if j not in st.alias.values()]
        back = {v: k for k, v in st.alias.items()}
        return [self.bufs[back[j]] if j in back else self.bufs[n + fresh.index(j)] for j in range(len(st.out_shape))]


def _split_call(name, finish=(), start=(), after=None):
    bufs, stage_bufs = [], []

    def slot(a):
        for i, b in enumerate(bufs):
            if b is a:
                return i
        bufs.append(a)
        return len(bufs) - 1

    fin_slots = [[slot(b) for b in fl.bufs] for fl in finish]
    for st in start:
        fresh = [lax.empty(o.shape, o.dtype) for j, o in enumerate(st.out_shape) if j not in st.alias.values()]
        stage_bufs.append([slot(a) for a in list(st.operands) + fresh])
    old_sems = [s for fl in finish for s in fl.sems]
    new_sems = [s for st in start for s in st.sems]
    nb, no, nn = len(bufs), len(old_sems), len(new_sems)

    def refs_of(st, slots, buf_refs):
        n = len(st.operands)
        ins = [buf_refs[i] for i in slots[:n]]
        fresh = [j for j in range(len(st.out_shape)) if j not in st.alias.values()]
        back = {v: k for k, v in st.alias.items()}
        outs = [ins[back[j]] if j in back else buf_refs[slots[n + fresh.index(j)]] for j in range(len(st.out_shape))]
        return ins, outs

    def body(*refs):
        buf_refs, sem_in = refs[:nb], refs[nb:nb + no]
        sem_out = refs[nb + no + (after is not None):][:nn]
        token = refs[-1]
        pos = 0
        for fl, slots in zip(finish, fin_slots):
            ins, outs = refs_of(fl.stage, slots, buf_refs)
            fl.stage.finish(ins, outs, sem_in[pos:pos + len(fl.sems)])
            pos += len(fl.sems)
        pos = 0
        for st, slots in zip(start, stage_bufs):
            ins, outs = refs_of(st, slots, buf_refs)
            st.start(ins, outs, sem_out[pos:pos + len(st.sems)])
            pos += len(st.sems)
        token[...] = jnp.zeros_like(token)

    res = pl.pallas_call(
        body, name=name,
        out_shape=tuple(new_sems) + tuple(pltpu.HBM(b.shape, b.dtype) for b in bufs) +
                  (jax.ShapeDtypeStruct((8, LANE), F32),),
        in_specs=(HBM_REF,) * nb + (SEM_REF,) * no + ((pl.BlockSpec(memory_space=pl.ANY),) if after is not None else ()),
        out_specs=(SEM_REF,) * nn + (HBM_REF,) * nb + (pl.BlockSpec(memory_space=pltpu.VMEM),),
        input_output_aliases={i: nn + i for i in range(nb)},
        compiler_params=pltpu.CompilerParams(has_side_effects=DATAFLOW),
    )(*_hbm(*bufs), *old_sems, *([after] if after is not None else []))
    sems, thru, token = res[:nn], res[nn:nn + nb], res[-1]
    for fl, slots in zip(finish, fin_slots):
        fl.bufs = [thru[i] for i in slots]
    flights, pos = [], 0
    for st, slots in zip(start, stage_bufs):
        flights.append(_Flight(st, sems[pos:pos + len(st.sems)], [thru[i] for i in slots]))
        pos += len(st.sems)
    return flights, token


def _last_copies(p_ref, land_ref, ssem, rsem):
    x, y, c, chips = _place()
    me = 2 * x + y
    send = [_rcopy(p_ref.at[2 * chip[0] + chip[1]], land_ref.at[me], ssem.at[j], rsem.at[j], (*chip, c))
            for j, chip in enumerate(chips)]
    recv = [_rcopy(p_ref.at[2 * chip[0] + chip[1]], land_ref.at[2 * chip[0] + chip[1]], ssem.at[j], rsem.at[j],
                   (*chip, c)) for j, chip in enumerate(chips)]
    return send, recv


def _chips_start(p):
    def body(p_ref, land_ref, ssem, rsem, p_thru, land_thru, token):
        for cp in _last_copies(p_ref, land_ref, ssem, rsem)[0]:
            cp.start()
        token[...] = jnp.zeros_like(token)

    return pl.pallas_call(
        body, name="reduce_last_start",
        out_shape=(pltpu.SemaphoreType.DMA((3,)), pltpu.SemaphoreType.DMA((3,)), pltpu.HBM(p.shape, p.dtype),
                   pltpu.HBM(p.shape, p.dtype), jax.ShapeDtypeStruct((8, LANE), F32)),
        in_specs=(HBM_REF, HBM_REF),
        out_specs=(SEM_REF, SEM_REF, HBM_REF, HBM_REF, pl.BlockSpec(memory_space=pltpu.VMEM)),
        input_output_aliases={0: 2, 1: 3},
        compiler_params=pltpu.CompilerParams(has_side_effects=DATAFLOW),
    )(*_hbm(p, lax.empty(p.shape, p.dtype)))


def _chips_wait(ssem, rsem, p_thru, land_thru, after):
    def body(p_ref, land_ref, ssem, rsem, after_ref, p_dead, got_ref):
        send, recv = _last_copies(p_ref, land_ref, ssem, rsem)
        for cp in send:
            cp.wait_send()
        for cp in recv:
            cp.wait_recv()

    return pl.pallas_call(
        body, name="reduce_last_wait",
        out_shape=(pltpu.HBM(p_thru.shape, p_thru.dtype), pltpu.HBM(land_thru.shape, land_thru.dtype)),
        in_specs=(HBM_REF, HBM_REF, SEM_REF, SEM_REF, pl.BlockSpec(memory_space=pl.ANY)),
        out_specs=(HBM_REF, HBM_REF), input_output_aliases={0: 0, 1: 1},
        compiler_params=pltpu.CompilerParams(has_side_effects=DATAFLOW),
    )(p_thru, land_thru, ssem, rsem, after)


def _share(pairs):
    n = len(pairs)

    def start(ins, outs, sems):
        x, y, c, _ = _place()
        for i in range(n):
            _rcopy(outs[i].at[c], outs[i].at[c], sems[0].at[i], sems[1].at[i], (x, y, 1 - c)).start()

    def finish(ins, outs, sems):
        x, y, c, _ = _place()
        for i in range(n):
            _rcopy(outs[i].at[c], outs[i].at[c], sems[0].at[i], sems[1].at[i], (x, y, 1 - c)).wait_send()
            _rcopy(outs[i].at[1 - c], outs[i].at[1 - c], sems[0].at[i], sems[1].at[i], (x, y, 1 - c)).wait_recv()

    return _Stage(pairs, [_sds(p) for p in pairs], {i: i for i in range(n)},
                  [pltpu.SemaphoreType.DMA((n,)), pltpu.SemaphoreType.DMA((n,))], start, finish)


def _row_block(rows, cols, itemsize=4, target=2 * MIB):
    br = rows
    while br * cols * itemsize > target and br % 32 == 0:
        br //= 2
    return br


def _cast_place(w, chip_idx, name):
    rows, cols = w.shape
    br = _row_block(rows, cols)

    def body(k_ref, w_ref, o_ref):
        o_ref[0] = w_ref[...].astype(BF)

    return _call(
        body, name=name, grid=(rows // br,), prefetch=chip_idx,
        in_specs=[pl.BlockSpec((br, cols), lambda r, k: (r, 0))],
        out_specs=[pl.BlockSpec((1, br, cols), lambda r, k: (k[0], r, 0))],
        out_shape=[jax.ShapeDtypeStruct((NCHIP, rows, cols), BF)], vmem=32, args=[w])[0][0]


def _cast_place_multi(ws, chip_idx, stages=()):
    br = 128
    nblk = [a.shape[0] // br for a in ws]
    starts = [sum(nblk[:i]) for i in range(len(ws))]

    def body(k_ref, *refs):
        r = pl.program_id(0)
        for i in range(len(ws)):
            @pl.when(jnp.logical_and(r >= starts[i], r < starts[i] + nblk[i]))
            def _(i=i):
                refs[len(ws) + i][0] = refs[i][...].astype(BF)

    def at(i):
        return functools.partial(lambda r, s, nb: jnp.clip(r - s, 0, nb - 1), s=starts[i], nb=nblk[i])

    outs, landed = _call(
        body, name="cast_rest", grid=(sum(nblk),), prefetch=chip_idx,
        in_specs=[pl.BlockSpec((br, a.shape[1]), functools.partial(lambda r, k, f: (f(r), 0), f=at(i)))
                  for i, a in enumerate(ws)],
        out_specs=[pl.BlockSpec((1, br, a.shape[1]), functools.partial(lambda r, k, f: (k[0], f(r), 0), f=at(i)))
                   for i, a in enumerate(ws)],
        out_shape=[jax.ShapeDtypeStruct((NCHIP,) + a.shape, BF) for a in ws], vmem=32, args=list(ws), stages=stages)
    return outs, landed


def _add_sibling(g, land, cidx, name, stages=()):
    _, _, hr, cols = g.shape
    br = _row_block(hr, cols)

    def body(c_ref, g_ref, l_ref, o_ref):
        o_ref[...] = (g_ref[0, 0].astype(F32) + l_ref[0].astype(F32)).astype(BF)[None]

    outs, st = _call(
        body, name=name, grid=(NCHIP, hr // br), prefetch=cidx,
        in_specs=[pl.BlockSpec((1, 1, br, cols), lambda k, r, c: (k, c[0], r, 0)),
                  pl.BlockSpec((1, br, cols), lambda k, r, c: (k, r, 0))],
        out_specs=[pl.BlockSpec((1, br, cols), lambda k, r, c: (k, r, 0))],
        out_shape=[jax.ShapeDtypeStruct((NCHIP, hr, cols), BF)], vmem=32, args=[g, land], stages=stages)
    return outs[0], st


def _add_pair(a, b, name):
    rows, cols = a.shape

    def body(a_ref, b_ref, o_ref):
        o_ref[...] = a_ref[...] + b_ref[...]

    spec = pl.BlockSpec((rows, cols), lambda r: (0, 0))
    return _call(body, name=name, grid=(1,), in_specs=[spec, spec], out_specs=[spec], out_shape=[_sds(a)],
                 vmem=32, args=[a, b])[0][0]


def _add_chips(own, land, idx, name, stages=None):
    _, hr, cols = land.shape
    br = _row_block(hr, cols)

    def body(s_ref, a_ref, b_ref, c_ref, d_ref, o_ref):
        o_ref[...] = (a_ref[...].astype(F32) + b_ref[...].astype(F32)) + (c_ref[...].astype(F32) +
                                                                           d_ref[...].astype(F32))

    spec = lambda q: pl.BlockSpec((1, br, cols), functools.partial(lambda r, s, q: (s[q], r, 0), q=q))
    outs, landed = _call(
        body, name=name, grid=(hr // br,), prefetch=idx,
        in_specs=[spec(0), spec(1), spec(2), spec(3)], out_specs=[spec(4)],
        out_shape=[jax.ShapeDtypeStruct((2, hr, cols), F32)], vmem=48, args=[own, land, land, land],
        stages=stages or ())
    return outs[0] if stages is None else (outs[0], landed)


def _adamw_math(w, g, m, v):
    mn = ADAM_B1 * m + (1.0 - ADAM_B1) * g
    vn = ADAM_B2 * v + (1.0 - ADAM_B2) * (g * g)
    m_hat = mn / (1.0 - ADAM_B1 ** ADAM_STEP)
    v_hat = vn / (1.0 - ADAM_B2 ** ADAM_STEP)
    return -ADAM_LR * (m_hat / (jnp.sqrt(v_hat) + ADAM_EPS) + ADAM_WD * w), mn, vn


def _adamw(w, g, m, v, name, stages=()):
    rows, cols = w.shape
    br = _row_block(rows, cols)

    def body(w_ref, g_ref, m_ref, v_ref, d_ref, mo_ref, vo_ref):
        d_ref[...], mo_ref[...], vo_ref[...] = _adamw_math(w_ref[...], g_ref[...], m_ref[...], v_ref[...])

    spec = pl.BlockSpec((br, cols), lambda r: (r, 0))
    return _call(body, name=name, grid=(rows // br,), in_specs=[spec] * 4, out_specs=[spec] * 3,
                 out_shape=[_sds(w)] * 3, vmem=48, args=[w, g, m, v], stages=stages)


def _adamw_multi(names, w, g, m, v, stages=()):
    cols = w[names[0]].shape[1]
    br = 128
    nblk = [w[n].shape[0] // br for n in names]
    starts = [sum(nblk[:i]) for i in range(len(names))]

    def body(*refs):
        r = pl.program_id(0)
        for i in range(len(names)):
            w_ref, g_ref, m_ref, v_ref = refs[4 * i:4 * i + 4]
            d_ref, mo_ref, vo_ref = refs[4 * len(names) + 3 * i:4 * len(names) + 3 * i + 3]

            @pl.when(jnp.logical_and(r >= starts[i], r < starts[i] + nblk[i]))
            def _():
                d_ref[...], mo_ref[...], vo_ref[...] = _adamw_math(w_ref[...], g_ref[...], m_ref[...], v_ref[...])

    def spec(i):
        return pl.BlockSpec((br, cols), functools.partial(
            lambda r, s, nb: (jnp.clip(r - s, 0, nb - 1), 0), s=starts[i], nb=nblk[i]))

    outs, landed = _call(
        body, name="adamw_" + "_".join(names), grid=(sum(nblk),),
        in_specs=[spec(i) for i in range(len(names)) for _ in range(4)],
        out_specs=[spec(i) for i in range(len(names)) for _ in range(3)],
        out_shape=[_sds(w[n]) for n in names for _ in range(3)], vmem=48,
        args=[a[n] for n in names for a in (w, g, m, v)], stages=stages)
    return {n: outs[3 * i:3 * i + 3] for i, n in enumerate(names)}, landed


def _to_everyone(v):
    deltas = [(a, b, e) for a in (0, 1) for b in (0, 1) for e in (0, 1)][1:]

    def copies(ins, outs, sems):
        x, y, c, _ = _place()
        me = 4 * x + 2 * y + c
        flip = lambda p, f: 1 - p if f else p
        return [_rcopy(ins[0], outs[0].at[me], sems[0].at[q], sems[1].at[q], (flip(x, a), flip(y, b), flip(c, e)))
                for q, (a, b, e) in enumerate(deltas)]

    def start(ins, outs, sems):
        for cp in copies(ins, outs, sems):
            cp.start()

    def finish(ins, outs, sems):
        for cp in copies(ins, outs, sems):
            cp.wait()

    n = len(deltas)
    return _Stage([v], [jax.ShapeDtypeStruct((2 * NCHIP,) + v.shape, v.dtype)], {},
                  [pltpu.SemaphoreType.DMA((n,)), pltpu.SemaphoreType.DMA((n,))], start, finish)


SMALL_AT = {"norm_mix_pre": (0, 1, D), "norm_mix_post": (1, 1, D), "norm_mlp_pre": (2, 1, D),
            "norm_mlp_post": (3, 1, D), "b_gate": (4, 2, D), "conv_b": (6, 1, D), "lru_b_a": (7, 1, D),
            "lru_b_x": (8, 1, D), "lru_lambda": (9, 1, D), "pool_scale": (10, 1, DP)}
SMALL_SEPARATE = ["conv_w", "lru_w_a", "lru_w_x", "pool_w"]


def _adamw_small(small_sum, first_all, sep_grads, w, m, v):
    packed, sep = list(SMALL_AT), list(SMALL_SEPARATE)
    names = packed + sep

    def body(*refs):
        s_ref, a_ref, refs = refs[0], refs[1], refs[2:]
        g_sep, refs = refs[:len(sep)], refs[len(sep):]
        nn = len(names)
        w_r, m_r, v_r, refs = refs[:nn], refs[nn:2 * nn], refs[2 * nn:3 * nn], refs[3 * nn:]
        g_out, refs = refs[:len(packed)], refs[len(packed):]
        d_o, m_o, v_o = refs[:nn], refs[nn:2 * nn], refs[2 * nn:3 * nn]
        for i, n in enumerate(names):
            if i == 0:
                g = a_ref[0:1, :]
                for q in range(1, 2 * NCHIP):
                    g = g + a_ref[q:q + 1, :]
                g_out[i][...] = g
            elif n in SMALL_AT:
                r0, nr, nc = SMALL_AT[n]
                g = jnp.concatenate([s_ref[r0 + q:r0 + q + 1, :nc] for q in range(nr)], axis=1)
                g_out[i][...] = g
            else:
                g = g_sep[i - len(packed)][...]
            d_o[i][...], m_o[i][...], v_o[i][...] = _adamw_math(w_r[i][...], g, m_r[i][...], v_r[i][...])

    ws = [w[n] for n in names]
    res = pl.pallas_call(
        body, name="adamw_small",
        out_shape=[_sds(w[n]) for n in packed] + [_sds(a) for a in ws] * 3,
        compiler_params=_cp(32),
    )(*_hbm(small_sum, first_all, *sep_grads, *ws, *[m[n] for n in names], *[v[n] for n in names]))
    nn, npk = len(names), len(packed)
    grad = dict(zip(packed, res[:npk]))
    delta = dict(zip(names, res[npk:npk + nn]))
    new_m = dict(zip(names, res[npk + nn:npk + 2 * nn]))
    new_v = dict(zip(names, res[npk + 2 * nn:]))
    return grad, delta, new_m, new_v


W_NAMES = ["norm_mix_pre", "norm_mix_post", "norm_mlp_pre", "norm_mlp_post", "w_in", "b_gate", "conv_w", "conv_b",
           "lru_w_a", "lru_b_a", "lru_w_x", "lru_b_x", "lru_lambda", "pool_w", "pool_scale", "w_lru_up",
           "w_pool_up", "w_o", "w_ff1", "w_ff2"]
BIG = ["w_in", "w_lru_up", "w_pool_up", "w_o", "w_ff1", "w_ff2"]


def _block_diag(w):
    hd = w.shape[-1]
    per = CB // hd
    w4 = w.reshape(NG, per, hd, hd)
    eye = jnp.eye(per, dtype=w.dtype)
    return jnp.einsum("gpij,pq->gpiqj", w4, eye).reshape(NG, CB, CB)


def _block_diag_extract(d, hd):
    per = CB // hd
    d5 = d.reshape(NG, per, hd, per, hd)
    return jnp.stack([d5[:, p, :, p, :] for p in range(per)], axis=1).reshape(NG * per, hd, hd)


def _halves(g):
    return g.reshape(NCHIP, 2, g.size // (g.shape[-1] * 2 * NCHIP), g.shape[-1])


def kernel(x, norm_mix_pre, norm_mix_post, norm_mlp_pre, norm_mlp_post, w_in, b_gate, conv_w, conv_b, lru_w_a, lru_b_a, lru_w_x, lru_b_x, lru_lambda, pool_w, pool_scale, w_lru_up, w_pool_up, w_o, w_ff1, w_ff2, loss_target, m_norm_mix_pre, m_norm_mix_post, m_norm_mlp_pre, m_norm_mlp_post, m_w_in, m_b_gate, m_conv_w, m_conv_b, m_lru_w_a, m_lru_b_a, m_lru_w_x, m_lru_b_x, m_lru_lambda, m_pool_w, m_pool_scale, m_w_lru_up, m_w_pool_up, m_w_o, m_w_ff1, m_w_ff2, v_norm_mix_pre, v_norm_mix_post, v_norm_mlp_pre, v_norm_mlp_post, v_w_in, v_b_gate, v_conv_w, v_conv_b, v_lru_w_a, v_lru_b_a, v_lru_w_x, v_lru_b_x, v_lru_lambda, v_pool_w, v_pool_scale, v_w_lru_up, v_w_pool_up, v_w_o, v_w_ff1, v_w_ff2):
    args = dict(locals())
    two_d = lambda a: a.reshape(-1, a.shape[-1])
    w = {n: two_d(args[n]) for n in W_NAMES}
    mom = {n: two_d(args["m_" + n]) for n in W_NAMES}
    var = {n: two_d(args["v_" + n]) for n in W_NAMES}
    i32 = lambda val: jnp.asarray(val, jnp.int32)
    chip = i32(2 * lax.axis_index("x") + lax.axis_index("y"))
    core = i32(lax.axis_index("c"))
    cidx = core.reshape(1)
    zero = i32(0)
    hd = lru_w_a.shape[-1]
    xs, target = x[0], loss_target[0]
    g1, g2, g3, g4 = norm_mix_pre, norm_mix_post, norm_mlp_pre, norm_mlp_post

    mix = ["w_lru_up", "w_pool_up", "w_o"]
    full = {"w_in": _cast_place(w["w_in"], chip.reshape(1), "cast_w_in")}
    (fl_in, fl_conv), first = _split_call("gather_start_first", start=[
        _gather([full["w_in"]], ici=[(0, ALL)]), _gather_whole(w["conv_w"])])
    casts, _ = _cast_place_multi([w[n] for n in BIG[1:]], chip.reshape(1), stages=[_after(first)])
    full.update(zip(BIG[1:], casts))
    (fl_mix, fl_ff1, fl_ff2), started = _split_call("gather_start_rest", start=[
        _gather([full[n] for n in mix], ici=[(0, ALL), (1, ALL), (2, ALL)]),
        _gather([full["w_ff1"]], ici=[(0, ALL)]), _gather([full["w_ff2"]], ici=[(0, ALL)])])
    wa = _block_diag(lru_w_a[0]).astype(BF)
    wx = _block_diag(lru_w_x[0]).astype(BF)
    pw = pool_w[0].astype(BF)

    def to_sibling(name, flight, after=None):
        fl, = _split_call(name + "_pass", finish=[flight], after=after,
                          start=[_gather(flight.landed(), d2d=[(i, ALL) for i in range(len(flight.bufs))])])[0]
        return fl

    def arrived(name, flight, after=None):
        _split_call(name + "_done", finish=[flight], after=after)
        return flight.landed()

    fl_in = to_sibling("gather_w_in", fl_in, after=started)
    w_in_f, = arrived("gather_w_in", fl_in)
    conv_all, = arrived("gather_conv", fl_conv)
    full["w_in"] = w_in_f
    conv_all = lax.dynamic_update_slice(conv_all, w["conv_w"][None], (chip, zero, zero))
    conv_full = jnp.transpose(conv_all, (1, 0, 2)).reshape(4, DR)
    (proj, h1), _ = _fwd_inproj(xs, g1, w_in_f)
    fl_mix = to_sibling("gather_mix", fl_mix, after=h1)
    (ylru, hs), _ = _fwd_lru(proj, conv_full, conv_b, wa, lru_b_a, wx, lru_b_x, lru_lambda)
    got = arrived("gather_mix", fl_mix, after=ylru)
    fl_ff1 = to_sibling("gather_ff1", fl_ff1, after=ylru)
    w_lru_up_f, w_pool_up_f, w_o_f = got[0].reshape(DR, D), got[1], got[2].reshape(D, D)
    ypool = _fwd_pool(proj, pw, pool_scale)
    (x2, h2, m, mrg, bra, brb), _ = _fwd_merge(xs, ylru, ypool, proj, b_gate, g2, g3, w_lru_up_f, w_pool_up_f, w_o_f)
    fl_ff2 = to_sibling("gather_ff2", fl_ff2, after=h2)
    ff1, = arrived("gather_ff1", fl_ff1, after=h2)
    ff2, = arrived("gather_ff2", fl_ff2)
    ff2 = ff2.reshape(DF, D)
    a1, f = _fwd_mlp(h2, ff1, ff2)
    lossp, dy, df, dg4 = _loss_head(f, x2, target, g4)

    idx_big = jnp.stack([chip, (chip + 1) % NCHIP, (chip + 2) % NCHIP, (chip + 3) % NCHIP, core])
    dh2, df1 = _bwd_mlp_x(df, a1, ff1, ff2)
    dw_ff1, dw_ff2 = _bwd_mlp_w(df, h2, a1, df1)
    g_ff = [_halves(dw_ff1), _halves(dw_ff2)]
    (dxres, dgates, dylru, dypool, dm, dbra, dbrb, dg2, dg3, dbg), (l_ff,) = _bwd_merge(
        dh2, dy, x2, m, bra, brb, proj, b_gate, g2, g3, w_lru_up_f, w_pool_up_f, w_o_f, stages=[_to_sibling(g_ff)])
    p_ff = [_add_sibling(g, l, cidx, "add_sibling_" + n)[0] for g, l, n in zip(g_ff, l_ff, ["w_ff1", "w_ff2"])]
    (fl_ff,), sent_ff = _split_call("reduce_ff_start", start=[_to_chips(p_ff)])
    (dw_o, dw_lru_up, dw_pool_up), _ = _dw_merge(mrg, dm, ylru, dbra, ypool, dbrb, stages=[_after(sent_ff)])
    g_mix = [_halves(dw_lru_up), _halves(dw_pool_up), _halves(dw_o)]
    (dxp, dgl, dcw, dcb, dwa, dba, dwx, dbx, dlam), (l_mix,) = _bwd_lru(
        proj, hs, dylru, conv_full, conv_b, wa, lru_b_a, wx, lru_b_x, lru_lambda, stages=[_to_sibling(g_mix)])
    p_mix = [_add_sibling(g, l, cidx, "add_sibling_" + n)[0] for g, l, n in zip(g_mix, l_mix, mix)]
    dxpool, dpw, dsc = _bwd_pool(proj, dypool, pw, pool_scale)
    dproj = jnp.concatenate([dxp, dgl, dxpool, dgates], axis=1)
    small = jnp.concatenate([
        jnp.zeros((1, D), F32), dg2, dg3, dg4, dbg.reshape(2, D), dcb, dba, dbx, dlam,
        jnp.pad(dsc, ((0, 0), (0, D - DP))), jnp.pad(lossp, ((0, 0), (0, D - 1))), dcw,
        _block_diag_extract(dwa, hd).reshape(-1, D), _block_diag_extract(dwx, hd).reshape(-1, D),
        dpw.reshape(-1, D)], axis=0)
    (dw_in, dh1), (c_mix, (l_small,)) = _bwd_inproj(h1, dproj, full["w_in"],
                                                     stages=[_to_chips(p_mix), _to_sibling([small])])
    small2 = _add_pair(small, l_small, "add_sibling_small").reshape(2, SMALL_ROWS // 2, D)
    g_in = _halves(dw_in)
    done = ["w_ff1", "w_ff2"] + mix
    (fl_gin, fl_small), _ = _split_call("reduce_in_sibling_start", start=[_to_sibling([g_in]), _to_chips([small2])])
    _split_call("reduce_in_sibling_done", finish=[fl_gin, fl_ff])
    (g_in, l_in), (p_ff1, p_ff2, c_ff1, c_ff2) = fl_gin.bufs, fl_ff.bufs
    p_in = _add_sibling(g_in, l_in, cidx, "add_sibling_w_in")[0]
    ssem, rsem, p_in, c_in, token = _chips_start(p_in)
    pairs = [_add_chips(p, l, idx_big, "add_chips_" + n, stages=[_after(token)])[0]
             for p, l, n in zip([p_ff1, p_ff2] + p_mix, [c_ff1, c_ff2] + c_mix, done)]
    _split_call("reduce_small_done", finish=[fl_small], after=pairs[-1])
    small2, c_small = fl_small.bufs
    own_small = lax.dynamic_index_in_dim(small2, core, 0, keepdims=True)
    c_small = lax.dynamic_update_slice(c_small, own_small, (chip, zero, zero))
    pair_small = _add_chips(c_small, c_small, jnp.stack([zero, zero + 1, zero + 2, zero + 3, core]), "add_chips_small")
    (fl_share,), shared_start = _split_call("reduce_share_start", start=[_share(pairs + [pair_small])])
    (grad_x, dg1), _ = _bwd_prenorm(xs, dh1, dxres, g1, stages=[_after(shared_start)])
    _split_call("reduce_share_done", finish=[fl_share], after=dg1)
    shared = fl_share.landed()
    pairs, pair_small = shared[:-1], shared[-1]

    grads, delta, new_m, new_v = {}, {}, {}, {}
    for n, p in zip(done, pairs):
        grads[n] = p.reshape(-1, p.shape[-1])

    def update(n, stages=()):
        (delta[n], new_m[n], new_v[n]), landed = _adamw(w[n], grads[n], mom[n], var[n], "adamw_" + n, stages=stages)
        return landed

    updated, _ = _adamw_multi(["w_ff1", "w_ff2", "w_o", "w_lru_up"], w, grads, mom, var)
    for n, (d, mo, vo) in updated.items():
        delta[n], new_m[n], new_v[n] = d, mo, vo
    p_in, c_in = _chips_wait(ssem, rsem, p_in, c_in, new_v["w_lru_up"])
    pair_in = _add_chips(p_in, c_in, idx_big, "add_chips_w_in")
    ((pair_in,), (dg1_all,)) = update("w_pool_up", stages=[_share([pair_in]), _to_everyone(dg1)])
    dg1_all = lax.dynamic_update_slice(dg1_all, dg1[None], (2 * chip + core, zero, zero)).reshape(2 * NCHIP, D)
    grads["w_in"] = pair_in.reshape(-1, pair_in.shape[-1])
    update("w_in")
    small_sum = pair_small.reshape(SMALL_ROWS, D)
    loss = 0.5 * small_sum[LOSS_ROW, 0]
    ccols = DR // NCHIP
    sep = [lax.dynamic_slice(small_sum[12:16], (zero, chip * ccols), (4, ccols)),
           small_sum[16:80].reshape(-1, hd), small_sum[80:144].reshape(-1, hd), small_sum[144:208].reshape(-1, PG)]
    g_s, d_s, m_s, v_s = _adamw_small(small_sum, dg1_all, sep, w, mom, var)
    grads.update(g_s)
    grads.update(dict(zip(SMALL_SEPARATE, sep)))
    delta.update(d_s)
    new_m.update(m_s)
    new_v.update(v_s)

    out = lambda d: [d[n].reshape(args[n].shape) for n in W_NAMES]
    return (loss, grad_x[None], *out(grads), *out(delta), *out(new_m), *out(new_v))
```

```python
import functools
import math

import jax
import jax.numpy as jnp
from jax import lax
from jax.experimental import pallas as pl
from jax.experimental.pallas import tpu as pltpu

F32 = jnp.float32
BF = jnp.bfloat16

T = 2048
D = 1024
DR = 1024
DP = 512
DF = 4096
DIN = 4608
NCHIP = 4
CW_IN = DIN // NCHIP
LANE = 128
CB = 128
NG = DR // CB
PG = 128
POOL_WINDOWS = (2, 4, 8, 16)
NORM_EPS = 1e-6
LRU_C = 8.0
GELU_C = math.sqrt(2.0 / math.pi)
ADAM_LR = 0.001
ADAM_B1 = 0.9
ADAM_B2 = 0.999
ADAM_EPS = 1e-08
ADAM_WD = 0.01
ADAM_STEP = 10
MESH_ID = pl.DeviceIdType.MESH
ANY = pl.BlockSpec(memory_space=pl.ANY)
SMALL_ROWS = 208
LOSS_ROW = 11
MIB = 1 << 20


def _cp(vmem_mib=None):
    if vmem_mib is None:
        return pltpu.CompilerParams()
    return pltpu.CompilerParams(vmem_limit_bytes=vmem_mib * MIB)


def _hbm(*arrays):
    return [pltpu.with_memory_space_constraint(a, pltpu.HBM) for a in arrays]


def _hbm_out(shapes):
    return [pltpu.HBM(s.shape, s.dtype) for s in shapes]


class _Stage:
    def __init__(self, operands, out_shape, alias, sems, start, finish):
        self.operands, self.out_shape, self.alias, self.sems = list(operands), list(out_shape), dict(alias), list(sems)
        self.start, self.finish = start, finish


def _call(body, *, name, grid, in_specs, out_specs, out_shape, args, vmem=None, stages=(), prefetch=None,
          scratch=()):
    nin, nout = len(in_specs), len(out_specs)
    npre = 0 if prefetch is None else 1
    st_args, st_shapes, st_sems, aliases = [], [], list(scratch), {}
    for st in stages:
        for k, v in st.alias.items():
            aliases[npre + nin + len(st_args) + k] = nout + len(st_shapes) + v
        st_args += st.operands
        st_shapes += st.out_shape
        st_sems += st.sems

    def wrapped(*refs):
        pre, refs = refs[:npre], refs[npre:]
        ins, pos = refs[:nin], nin
        st_ins = []
        for st in stages:
            st_ins.append(refs[pos:pos + len(st.operands)])
            pos += len(st.operands)
        outs, pos = refs[pos:pos + nout], pos + nout
        st_outs = []
        for st in stages:
            st_outs.append(refs[pos:pos + len(st.out_shape)])
            pos += len(st.out_shape)
        work, pos = refs[pos:pos + len(scratch)], pos + len(scratch)
        sems = []
        for st in stages:
            sems.append(refs[pos:pos + len(st.sems)])
            pos += len(st.sems)
        if stages:
            first = functools.reduce(jnp.logical_and, [pl.program_id(a) == 0 for a in range(len(grid))])

            @pl.when(first)
            def _():
                for st, a, b, s in zip(stages, st_ins, st_outs, sems):
                    st.start(a, b, s)

        body(*pre, *ins, *outs, *work)
        if stages:
            last = functools.reduce(jnp.logical_and, [pl.program_id(a) == g - 1 for a, g in enumerate(grid)])

            @pl.when(last)
            def _():
                for st, a, b, s in zip(stages, st_ins, st_outs, sems):
                    st.finish(a, b, s)

    all_in = list(in_specs) + [ANY] * len(st_args)
    all_out = list(out_specs) + [ANY] * len(st_shapes)
    kw = dict(has_side_effects=True) if stages else {}
    if vmem is not None:
        kw["vmem_limit_bytes"] = vmem * MIB
    if prefetch is None:
        gkw = dict(grid=grid, in_specs=all_in, out_specs=all_out, scratch_shapes=st_sems)
    else:
        gkw = dict(grid_spec=pltpu.PrefetchScalarGridSpec(
            num_scalar_prefetch=1, grid=grid, in_specs=all_in, out_specs=all_out, scratch_shapes=st_sems))
    res = pl.pallas_call(
        wrapped, name=name, out_shape=_hbm_out(list(out_shape) + st_shapes), input_output_aliases=aliases,
        compiler_params=pltpu.CompilerParams(**kw), **gkw,
    )(*([prefetch] if npre else []), *_hbm(*args, *st_args))
    outs, rest, st_res = list(res[:nout]), list(res[nout:]), []
    for st in stages:
        st_res.append(rest[:len(st.out_shape)])
        rest = rest[len(st.out_shape):]
    return outs, st_res


def _mm(a, b):
    return jnp.dot(a.astype(BF), b.astype(BF), preferred_element_type=F32)


def _mm_nt(a, b):
    return lax.dot_general(a.astype(BF), b.astype(BF), (((1,), (1,)), ((), ())),
                           preferred_element_type=F32)


def _mm_tn(a, b):
    return lax.dot_general(a.astype(BF), b.astype(BF), (((0,), (0,)), ((), ())),
                           preferred_element_type=F32)


def _rows(v):
    return lax.broadcasted_iota(jnp.int32, v.shape, 0)


def _sd(v, s, fill=0.0):
    return jnp.where(_rows(v) >= s, pltpu.roll(v, s, axis=0), fill)


def _su(v, s, fill=0.0):
    n = v.shape[0]
    return jnp.where(_rows(v) < n - s, pltpu.roll(v, n - s, axis=0), fill)


def _sigmoid(z):
    return 1.0 / (1.0 + jnp.exp(-z))


def _softplus(z):
    e = jnp.exp(-jnp.abs(z))
    u = 1.0 + e
    d = u - 1.0
    log1p = jnp.where(d == 0.0, e, jnp.log(u) * (e / jnp.where(d == 0.0, 1.0, d)))
    return jnp.maximum(z, 0.0) + log1p


def _mean(v):
    return jnp.mean(v, axis=-1, keepdims=True)


def _colsum(v):
    return jnp.sum(v, axis=0, keepdims=True)


def _acc(ref, val, first):
    @pl.when(first)
    def _():
        ref[...] = val

    @pl.when(jnp.logical_not(first))
    def _():
        ref[...] += val


def _conv(xp, cw, cb):
    x1, x2, x3 = _sd(xp, 1), _sd(xp, 2), _sd(xp, 3)
    xc = cb + cw[0:1] * x3 + cw[1:2] * x2 + cw[2:3] * x1 + cw[3:4] * xp
    return xc, x1, x2, x3


def _lru_gates(xc, wa, ba, wx, bx, lam):
    xcb = xc.astype(BF)
    r = _sigmoid(_mm(xcb, wa) + ba)
    ii = _sigmoid(_mm(xcb, wx) + bx)
    sp = _softplus(-lam)
    la = (-LRU_C) * r * sp
    a = jnp.exp(la)
    mult = jnp.sqrt(-jnp.tanh(la) * (a * a + 1.0))
    return xcb, r, ii, sp, a, mult


def _gelu_parts(g):
    th = jnp.tanh(GELU_C * (g + 0.044715 * (g * g * g)))
    gel = 0.5 * g * (1.0 + th)
    dgel = 0.5 * (1.0 + th) + 0.5 * g * (1.0 - th * th) * (GELU_C * (1.0 + 3.0 * 0.044715 * (g * g)))
    return gel, dgel


def _tile_scan(a, b, a_s, b_s, out_ref, reverse):
    n = a.shape[0]
    nt = n // 8
    sub = jnp.bitwise_and(_rows(a), 7)
    s = 1
    while s < 8:
        keep = sub < 8 - s if reverse else sub >= s
        amount = n - s if reverse else s
        b = b + a * jnp.where(keep, pltpu.roll(b, amount, axis=0), 0.0)
        a = a * jnp.where(keep, pltpu.roll(a, amount, axis=0), 1.0)
        s *= 2
    a_s[...] = a
    b_s[...] = b
    edge = pl.ds(0 if reverse else 7, nt, stride=8)
    ta, tb = a_s[edge, :], b_s[edge, :]
    shift = _su if reverse else _sd
    s = 1
    while s < nt:
        tb = tb + ta * shift(tb, s, 0.0)
        if 2 * s < nt:
            ta = ta * shift(ta, s, 1.0)
        s *= 2
    enters = shift(tb, 1, 0.0)
    for o in range(8):
        rows = pl.ds(o, nt, stride=8)
        out_ref[rows, :] = b_s[rows, :] + a_s[rows, :] * enters


def _pool_window(x, steps, shift):
    s, sh = x, 1
    for _ in range(steps):
        s = s + shift(s, sh)
        sh *= 2
    return s


def _fwd_inproj(x, g1, w_in, stages=()):
    tm = 512

    def body(x_ref, g_ref, w_ref, proj_ref, h_ref):
        @pl.when(pl.program_id(1) == 0)
        def _():
            xv = x_ref[...]
            r = lax.rsqrt(_mean(xv * xv) + NORM_EPS)
            h_ref[...] = ((xv * r) * g_ref[...]).astype(BF)

        proj_ref[...] = jnp.dot(h_ref[...], w_ref[0], preferred_element_type=F32)

    return _call(
        body, name="fwd_inproj", grid=(T // tm, NCHIP),
        in_specs=[pl.BlockSpec((tm, D), lambda i, k: (i, 0)),
                  pl.BlockSpec((1, D), lambda i, k: (0, 0)),
                  pl.BlockSpec((1, D, CW_IN), lambda i, k: (k, 0, 0))],
        out_specs=[pl.BlockSpec((tm, CW_IN), lambda i, k: (i, k)),
                   pl.BlockSpec((tm, D), lambda i, k: (i, 0))],
        out_shape=[jax.ShapeDtypeStruct((T, DIN), F32), jax.ShapeDtypeStruct((T, D), BF)],
        vmem=40, args=[x, g1, w_in], stages=stages)


def _vec_spec():
    return pl.BlockSpec((1, CB), lambda j: (0, j))


def _fwd_lru(proj, conv_w, conv_b, wa, ba, wx, bx, lam, stages=()):
    def body(xp_ref, g_ref, cw_ref, cb_ref, wa_ref, ba_ref, wx_ref, bx_ref, lam_ref, y_ref, h_ref, a_s, b_s):
        xc, _, _, _ = _conv(xp_ref[...], cw_ref[...], cb_ref[...])
        _, _, ii, _, a, mult = _lru_gates(xc, wa_ref[0], ba_ref[...], wx_ref[0], bx_ref[...], lam_ref[...])
        _tile_scan(a, mult * (ii * xc), a_s, b_s, h_ref, reverse=False)
        gel, _ = _gelu_parts(g_ref[...])
        y_ref[...] = (h_ref[...] * gel).astype(BF)

    return _call(
        body, name="fwd_lru", grid=(NG,),
        in_specs=[pl.BlockSpec((T, CB), lambda j: (0, j)),
                  pl.BlockSpec((T, CB), lambda j: (0, NG + j)),
                  pl.BlockSpec((4, CB), lambda j: (0, j)),
                  _vec_spec(),
                  pl.BlockSpec((1, CB, CB), lambda j: (j, 0, 0)), _vec_spec(),
                  pl.BlockSpec((1, CB, CB), lambda j: (j, 0, 0)), _vec_spec(),
                  _vec_spec()],
        out_specs=[pl.BlockSpec((T, CB), lambda j: (0, j)), pl.BlockSpec((T, CB), lambda j: (0, j))],
        out_shape=[jax.ShapeDtypeStruct((T, DR), BF), jax.ShapeDtypeStruct((T, DR), F32)],
        vmem=48, args=[proj, proj, conv_w, conv_b, wa, ba, wx, bx, lam], stages=stages,
        scratch=[pltpu.VMEM((T, CB), F32)] * 2)


def _pool_cnt(w):
    t = lax.broadcasted_iota(jnp.int32, (T, 1), 0)
    return jnp.minimum(t + 1, w).astype(F32)


def _fwd_pool(proj, pool_w, pool_scale):
    def body(xp_ref, pw_ref, sc_ref, y_ref):
        for g, w in enumerate(POOL_WINDOWS):
            cols = slice(g * PG, (g + 1) * PG)
            x = xp_ref[:, cols]
            p = _pool_window(x, g + 1, _sd) / _pool_cnt(w) - x
            y_ref[:, cols] = (_mm(p, pw_ref[g]) * sc_ref[:, cols]).astype(BF)

    return pl.pallas_call(
        body, name="fwd_pool", grid=(1,),
        in_specs=[pl.BlockSpec((T, DP), lambda i: (0, 2 * DR // DP)),
                  pl.BlockSpec((4, PG, PG), lambda i: (0, 0, 0)),
                  pl.BlockSpec((1, DP), lambda i: (0, 0))],
        out_specs=pl.BlockSpec((T, DP), lambda i: (0, 0)),
        out_shape=pltpu.HBM((T, DP), BF),
        compiler_params=_cp(48),
    )(*_hbm(proj, pool_w, pool_scale))


GATE_BLK = 512
GATE_BLK0 = (2 * DR + DP) // GATE_BLK


def _gate_specs(tm):
    return [pl.BlockSpec((tm, GATE_BLK), functools.partial(lambda i, q: (i, GATE_BLK0 + q), q=q))
            for q in range(4)]


def _fwd_merge(x, ylru, ypool, proj, b_gate, g2, g3, w_lru_up, w_pool_up, w_o, stages=()):
    tm = 512

    def body(x_ref, yl_ref, yp_ref, p0, p1, p2, p3, bg_ref, g2_ref, g3_ref, wl_ref, wp_ref, wo_ref,
             x2_ref, h2_ref, m_ref, mrg_ref, bra_ref, brb_ref):
        bra = jnp.dot(yl_ref[...], wl_ref[...], preferred_element_type=F32)
        yp = yp_ref[...]
        brb = jnp.concatenate([jnp.dot(yp, wp_ref[k], preferred_element_type=F32) for k in range(NCHIP)], axis=1)
        bg = bg_ref[...]
        ga = _sigmoid(jnp.concatenate([p0[...], p1[...]], axis=1) + bg[:, :D])
        gb = _sigmoid(jnp.concatenate([p2[...], p3[...]], axis=1) + bg[:, D:])
        mrg = (ga * bra + gb * brb).astype(BF)
        m = jnp.dot(mrg, wo_ref[...], preferred_element_type=F32)
        r2 = lax.rsqrt(_mean(m * m) + NORM_EPS)
        x2 = x_ref[...] + (m * r2) * g2_ref[...]
        r3 = lax.rsqrt(_mean(x2 * x2) + NORM_EPS)
        x2_ref[...] = x2
        h2_ref[...] = ((x2 * r3) * g3_ref[...]).astype(BF)
        m_ref[...] = m
        mrg_ref[...] = mrg
        bra_ref[...] = bra.astype(BF)
        brb_ref[...] = brb.astype(BF)

    row = lambda w: pl.BlockSpec((tm, w), lambda i: (i, 0))
    full2 = lambda a, b: pl.BlockSpec((a, b), lambda i: (0, 0))
    return _call(
        body, name="fwd_merge", grid=(T // tm,),
        in_specs=[row(D), row(DR), row(DP)] + _gate_specs(tm) +
                 [full2(1, 2 * D), full2(1, D), full2(1, D), full2(DR, D),
                  pl.BlockSpec((NCHIP, DP, D // NCHIP), lambda i: (0, 0, 0)), full2(D, D)],
        out_specs=[row(D)] * 6,
        out_shape=[jax.ShapeDtypeStruct((T, D), F32), jax.ShapeDtypeStruct((T, D), BF),
                   jax.ShapeDtypeStruct((T, D), F32), jax.ShapeDtypeStruct((T, D), BF),
                   jax.ShapeDtypeStruct((T, D), BF), jax.ShapeDtypeStruct((T, D), BF)],
        vmem=48, args=[x, ylru, ypool, proj, proj, proj, proj, b_gate, g2, g3, w_lru_up, w_pool_up, w_o],
        stages=stages)


def _fwd_mlp(h2, w_ff1, w_ff2):
    tm = 512
    fk = DF // NCHIP

    def body(h_ref, w1_ref, w2_ref, a1_ref, f_ref):
        h = h_ref[...]
        f = None
        for k in range(NCHIP):
            a1 = jnp.maximum(jnp.dot(h, w1_ref[k], preferred_element_type=F32), 0.0)
            a1_ref[:, k * fk:(k + 1) * fk] = a1.astype(BF)
            part = jnp.dot((a1 * a1).astype(BF), w2_ref[k * fk:(k + 1) * fk, :], preferred_element_type=F32)
            f = part if f is None else f + part
        f_ref[...] = f

    return pl.pallas_call(
        body, name="fwd_mlp", grid=(T // tm,),
        in_specs=[pl.BlockSpec((tm, D), lambda i: (i, 0)),
                  pl.BlockSpec((NCHIP, D, fk), lambda i: (0, 0, 0)),
                  pl.BlockSpec((DF, D), lambda i: (0, 0))],
        out_specs=[pl.BlockSpec((tm, DF), lambda i: (i, 0)), pl.BlockSpec((tm, D), lambda i: (i, 0))],
        out_shape=_hbm_out([jax.ShapeDtypeStruct((T, DF), BF), jax.ShapeDtypeStruct((T, D), F32)]),
        compiler_params=_cp(56),
    )(*_hbm(h2, w_ff1, w_ff2))


def _loss_head(f, x2, target, g4):
    tm = 512

    def body(f_ref, x2_ref, t_ref, g_ref, loss_ref, dy_ref, df_ref, dg_ref):
        first = pl.program_id(0) == 0
        f = f_ref[...]
        g4v = g_ref[...]
        r4 = lax.rsqrt(_mean(f * f) + NORM_EPS)
        fn = f * r4
        e = (x2_ref[...] + fn * g4v) - t_ref[...]
        _acc(loss_ref, jnp.sum(_mean(e * e), axis=0, keepdims=True), first)
        dy = e * (1.0 / D)
        dy_ref[...] = dy
        _acc(dg_ref, _colsum(dy * fn), first)
        dfn = dy * g4v
        df_ref[...] = (r4 * (dfn - fn * _mean(dfn * fn))).astype(BF)

    row = pl.BlockSpec((tm, D), lambda i: (i, 0))
    return pl.pallas_call(
        body, name="loss_head", grid=(T // tm,),
        in_specs=[row, row, row, pl.BlockSpec((1, D), lambda i: (0, 0))],
        out_specs=[pl.BlockSpec((1, 1), lambda i: (0, 0)), row, row, pl.BlockSpec((1, D), lambda i: (0, 0))],
        out_shape=_hbm_out([jax.ShapeDtypeStruct((1, 1), F32), jax.ShapeDtypeStruct((T, D), F32),
                            jax.ShapeDtypeStruct((T, D), BF), jax.ShapeDtypeStruct((1, D), F32)]),
        compiler_params=_cp(48),
    )(*_hbm(f, x2, target, g4))


def _bwd_mlp_x(df, a1, w_ff1, w_ff2):
    tm = 512
    fk = DF // NCHIP

    def body(df_ref, a1_ref, w1_ref, w2_ref, dh_ref, df1_ref):
        df = df_ref[...]
        dh = None
        for k in range(NCHIP):
            cols = slice(k * fk, (k + 1) * fk)
            dact = _mm_nt(df, w2_ref[cols, :])
            df1 = (dact * (2.0 * a1_ref[:, cols].astype(F32))).astype(BF)
            df1_ref[:, cols] = df1
            part = _mm_nt(df1, w1_ref[k])
            dh = part if dh is None else dh + part
        dh_ref[...] = dh

    return pl.pallas_call(
        body, name="bwd_mlp_x", grid=(T // tm,),
        in_specs=[pl.BlockSpec((tm, D), lambda i: (i, 0)),
                  pl.BlockSpec((tm, DF), lambda i: (i, 0)),
                  pl.BlockSpec((NCHIP, D, fk), lambda i: (0, 0, 0)),
                  pl.BlockSpec((DF, D), lambda i: (0, 0))],
        out_specs=[pl.BlockSpec((tm, D), lambda i: (i, 0)), pl.BlockSpec((tm, DF), lambda i: (i, 0))],
        out_shape=_hbm_out([jax.ShapeDtypeStruct((T, D), F32), jax.ShapeDtypeStruct((T, DF), BF)]),
        compiler_params=_cp(56),
    )(*_hbm(df, a1, w_ff1, w_ff2))


def _bwd_mlp_w(df, h2, a1, df1):
    fc = 512
    per = (DF // NCHIP) // fc

    def body(df_ref, h_ref, a1_ref, df1_ref, dw1_ref, dw2_ref):
        a1 = a1_ref[...].astype(F32)
        dw2_ref[...] = _mm_tn((a1 * a1).astype(BF), df_ref[...]).astype(BF)
        dw1_ref[0] = _mm_tn(h_ref[...], df1_ref[...]).astype(BF)

    return pl.pallas_call(
        body, name="bwd_mlp_w", grid=(DF // fc,),
        in_specs=[pl.BlockSpec((T, D), lambda j: (0, 0)),
                  pl.BlockSpec((T, D), lambda j: (0, 0)),
                  pl.BlockSpec((T, fc), lambda j: (0, j)),
                  pl.BlockSpec((T, fc), lambda j: (0, j))],
        out_specs=[pl.BlockSpec((1, D, fc), lambda j: (j // per, 0, j % per)),
                   pl.BlockSpec((fc, D), lambda j: (j, 0))],
        out_shape=_hbm_out([jax.ShapeDtypeStruct((NCHIP, D, DF // NCHIP), BF),
                            jax.ShapeDtypeStruct((DF, D), BF)]),
        compiler_params=_cp(56),
    )(*_hbm(df, h2, a1, df1))


def _bwd_merge(dh2, dy, x2, m, bra, brb, proj, b_gate, g2, g3, w_lru_up, w_pool_up, w_o, stages=()):
    tm = 256
    cpu = D // NCHIP

    def body(dh2_ref, dy_ref, x2_ref, m_ref, bra_ref, brb_ref, p0, p1, p2, p3, bg_ref,
             g2_ref, g3_ref, wl_ref, wp_ref, wo_ref,
             dx_ref, dgt_ref, dyl_ref, dyp_ref, dm_ref, dbra_ref, dbrb_ref, dg2_ref, dg3_ref, dbg_ref):
        first = pl.program_id(0) == 0
        x2 = x2_ref[...]
        r3 = lax.rsqrt(_mean(x2 * x2) + NORM_EPS)
        x2n = x2 * r3
        dh2 = dh2_ref[...]
        t3 = dh2 * g3_ref[...]
        dx2 = dy_ref[...] + r3 * (t3 - x2n * _mean(t3 * x2n))
        dx_ref[...] = dx2
        _acc(dg3_ref, _colsum(dh2 * x2n), first)
        m = m_ref[...]
        r2 = lax.rsqrt(_mean(m * m) + NORM_EPS)
        mn = m * r2
        _acc(dg2_ref, _colsum(dx2 * mn), first)
        dmn = dx2 * g2_ref[...]
        dm = (r2 * (dmn - mn * _mean(dmn * mn))).astype(BF)
        dm_ref[...] = dm
        dmrg = _mm_nt(dm, wo_ref[...])
        bg = bg_ref[...]
        ga = _sigmoid(jnp.concatenate([p0[...], p1[...]], axis=1) + bg[:, :D])
        gb = _sigmoid(jnp.concatenate([p2[...], p3[...]], axis=1) + bg[:, D:])
        dga = dmrg * bra_ref[...].astype(F32) * (ga * (1.0 - ga))
        dgb = dmrg * brb_ref[...].astype(F32) * (gb * (1.0 - gb))
        dgt_ref[:, :D] = dga.astype(BF)
        dgt_ref[:, D:] = dgb.astype(BF)
        _acc(dbg_ref, jnp.concatenate([_colsum(dga), _colsum(dgb)], axis=1), first)
        dbra = (dmrg * ga).astype(BF)
        dbrb = (dmrg * gb).astype(BF)
        dbra_ref[...] = dbra
        dbrb_ref[...] = dbrb
        dyl_ref[...] = _mm_nt(dbra, wl_ref[...])
        dyp = None
        for k in range(NCHIP):
            part = _mm_nt(dbrb[:, k * cpu:(k + 1) * cpu], wp_ref[k])
            dyp = part if dyp is None else dyp + part
        dyp_ref[...] = dyp

    row = lambda w: pl.BlockSpec((tm, w), lambda i: (i, 0))
    full2 = lambda a, b: pl.BlockSpec((a, b), lambda i: (0, 0))
    wp_spec = pl.BlockSpec((NCHIP, DP, cpu), lambda i: (0, 0, 0))
    return _call(
        body, name="bwd_merge", grid=(T // tm,),
        in_specs=[row(D)] * 6 + _gate_specs(tm) +
                 [full2(1, 2 * D), full2(1, D), full2(1, D), full2(DR, D), wp_spec, full2(D, D)],
        out_specs=[row(D), row(2 * D), row(DR), row(DP), row(D), row(D), row(D),
                   full2(1, D), full2(1, D), full2(1, 2 * D)],
        out_shape=[jax.ShapeDtypeStruct((T, D), F32), jax.ShapeDtypeStruct((T, 2 * D), BF),
                   jax.ShapeDtypeStruct((T, DR), F32), jax.ShapeDtypeStruct((T, DP), F32),
                   jax.ShapeDtypeStruct((T, D), BF), jax.ShapeDtypeStruct((T, D), BF),
                   jax.ShapeDtypeStruct((T, D), BF),
                   jax.ShapeDtypeStruct((1, D), F32), jax.ShapeDtypeStruct((1, D), F32),
                   jax.ShapeDtypeStruct((1, 2 * D), F32)],
        vmem=56, args=[dh2, dy, x2, m, bra, brb, proj, proj, proj, proj, b_gate, g2, g3, w_lru_up, w_pool_up, w_o],
        stages=stages)


def _dw_merge(mrg, dm, ylru, dbra, ypool, dbrb, stages=()):
    nb = NCHIP
    rb, pb, cpu = D // nb, DP // nb, D // NCHIP

    def body(mrg_ref, dm_ref, yl_ref, dbra_ref, yp_ref, dbrb_ref, dwo_ref, dwl_ref, dwp_ref):
        dwo_ref[...] = _mm_tn(mrg_ref[...], dm_ref[...]).astype(BF)
        dwl_ref[...] = _mm_tn(yl_ref[...], dbra_ref[...]).astype(BF)
        dwp = _mm_tn(yp_ref[...], dbrb_ref[...]).astype(BF)
        for k in range(NCHIP):
            dwp_ref[k] = dwp[:, k * cpu:(k + 1) * cpu]

    cols = lambda w: pl.BlockSpec((T, w), lambda r: (0, r))
    whole = pl.BlockSpec((T, D), lambda r: (0, 0))
    return _call(
        body, name="dw_merge", grid=(nb,),
        in_specs=[cols(rb), whole, cols(rb), whole, cols(pb), whole],
        out_specs=[pl.BlockSpec((rb, D), lambda r: (r, 0)), pl.BlockSpec((rb, D), lambda r: (r, 0)),
                   pl.BlockSpec((NCHIP, pb, cpu), lambda r: (0, r, 0))],
        out_shape=[jax.ShapeDtypeStruct((D, D), BF), jax.ShapeDtypeStruct((DR, D), BF),
                   jax.ShapeDtypeStruct((NCHIP, DP, cpu), BF)],
        vmem=56, args=[mrg, dm, ylru, dbra, ypool, dbrb], stages=stages)


def _bwd_lru(proj, h, dylru, conv_w, conv_b, wa, ba, wx, bx, lam, stages=()):
    def body(xp_ref, g_ref, h_ref, dy_ref, cw_ref, cb_ref, wa_ref, ba_ref, wx_ref, bx_ref, lam_ref,
             dxp_ref, dg_ref, dcw_ref, dcb_ref, dwa_ref, dba_ref, dwx_ref, dbx_ref, dlam_ref, a_s, b_s, l_s):
        xp = xp_ref[...]
        cw = cw_ref[...]
        lam = lam_ref[...]
        xc, x1, x2, x3 = _conv(xp, cw, cb_ref[...])
        wa, wx = wa_ref[0], wx_ref[0]
        xcb, r, ii, sp, a, mult = _lru_gates(xc, wa, ba_ref[...], wx, bx_ref[...], lam)
        g = g_ref[...]
        gel, dgel = _gelu_parts(g)
        h = h_ref[...]
        dy = dy_ref[...]
        dg_ref[...] = (dy * h * dgel).astype(BF)
        _tile_scan(_su(a, 1, 0.0), dy * gel, a_s, b_s, l_s, reverse=True)
        b = l_s[...]
        da = b * _sd(h, 1, 0.0)
        dmult = b * (ii * xc)
        dii = b * (mult * xc)
        dxc = b * (mult * ii)
        dla = da * a - dmult * ((a * a) / mult)
        dr = dla * ((-LRU_C) * sp)
        dsp = _colsum(dla * ((-LRU_C) * r))
        dlam_ref[...] = -dsp / (1.0 + jnp.exp(lam))
        dzr = dr * (r * (1.0 - r))
        dzi = dii * (ii * (1.0 - ii))
        dzrb, dzib = dzr.astype(BF), dzi.astype(BF)
        dxc = dxc + _mm_nt(dzrb, wa) + _mm_nt(dzib, wx)
        dwa_ref[0] = _mm_tn(xcb, dzrb)
        dwx_ref[0] = _mm_tn(xcb, dzib)
        dba_ref[...] = _colsum(dzr)
        dbx_ref[...] = _colsum(dzi)
        dcb_ref[...] = _colsum(dxc)
        dcw_ref[...] = jnp.concatenate([_colsum(dxc * x3), _colsum(dxc * x2), _colsum(dxc * x1),
                                        _colsum(dxc * xp)], axis=0)
        dxp = cw[3:4] * dxc + cw[2:3] * _su(dxc, 1) + cw[1:2] * _su(dxc, 2) + cw[0:1] * _su(dxc, 3)
        dxp_ref[...] = dxp.astype(BF)

    blk = pl.BlockSpec((T, CB), lambda j: (0, j))
    wsp = pl.BlockSpec((1, CB, CB), lambda j: (j, 0, 0))
    return _call(
        body, name="bwd_lru", grid=(NG,),
        in_specs=[blk, pl.BlockSpec((T, CB), lambda j: (0, NG + j)), blk, blk,
                  pl.BlockSpec((4, CB), lambda j: (0, j)), _vec_spec(), wsp, _vec_spec(), wsp, _vec_spec(),
                  _vec_spec()],
        out_specs=[blk, blk, pl.BlockSpec((4, CB), lambda j: (0, j)), _vec_spec(), wsp, _vec_spec(), wsp,
                   _vec_spec(), _vec_spec()],
        out_shape=[jax.ShapeDtypeStruct((T, DR), BF), jax.ShapeDtypeStruct((T, DR), BF),
                   jax.ShapeDtypeStruct((4, DR), F32), jax.ShapeDtypeStruct((1, DR), F32),
                   jax.ShapeDtypeStruct((NG, CB, CB), F32), jax.ShapeDtypeStruct((1, DR), F32),
                   jax.ShapeDtypeStruct((NG, CB, CB), F32), jax.ShapeDtypeStruct((1, DR), F32),
                   jax.ShapeDtypeStruct((1, DR), F32)],
        vmem=56, args=[proj, proj, h, dylru, conv_w, conv_b, wa, ba, wx, bx, lam], stages=stages,
        scratch=[pltpu.VMEM((T, CB), F32)] * 3)


def _bwd_pool(proj, dypool, pool_w, pool_scale):
    def body(xp_ref, dy_ref, pw_ref, sc_ref, dx_ref, dw_ref, dsc_ref):
        for g, w in enumerate(POOL_WINDOWS):
            cols = slice(g * PG, (g + 1) * PG)
            cnt = _pool_cnt(w)
            x = xp_ref[:, cols]
            pb = (_pool_window(x, g + 1, _sd) / cnt - x).astype(BF)
            wg = pw_ref[g]
            dy = dy_ref[:, cols]
            dsc_ref[:, cols] = _colsum(dy * _mm(pb, wg))
            dyp = (dy * sc_ref[:, cols]).astype(BF)
            dw_ref[g] = _mm_tn(pb, dyp)
            dp = _mm_nt(dyp, wg)
            dx_ref[:, cols] = (_pool_window(dp / cnt, g + 1, _su) - dp).astype(BF)

    return pl.pallas_call(
        body, name="bwd_pool", grid=(1,),
        in_specs=[pl.BlockSpec((T, DP), lambda i: (0, 2 * DR // DP)),
                  pl.BlockSpec((T, DP), lambda i: (0, 0)),
                  pl.BlockSpec((4, PG, PG), lambda i: (0, 0, 0)),
                  pl.BlockSpec((1, DP), lambda i: (0, 0))],
        out_specs=[pl.BlockSpec((T, DP), lambda i: (0, 0)),
                   pl.BlockSpec((4, PG, PG), lambda i: (0, 0, 0)),
                   pl.BlockSpec((1, DP), lambda i: (0, 0))],
        out_shape=_hbm_out([jax.ShapeDtypeStruct((T, DP), BF), jax.ShapeDtypeStruct((4, PG, PG), F32),
                            jax.ShapeDtypeStruct((1, DP), F32)]),
        compiler_params=_cp(48),
    )(*_hbm(proj, dypool, pool_w, pool_scale))


def _bwd_inproj(h1, dproj, w_in, stages=()):
    def body(h_ref, dp_ref, w_ref, dw_ref, dh_ref):
        dp = dp_ref[...]
        dw_ref[0] = _mm_tn(h_ref[...], dp).astype(BF)
        _acc(dh_ref, _mm_nt(dp, w_ref[0]), pl.program_id(0) == 0)

    return _call(
        body, name="bwd_inproj", grid=(NCHIP,),
        in_specs=[pl.BlockSpec((T, D), lambda k: (0, 0)),
                  pl.BlockSpec((T, CW_IN), lambda k: (0, k)),
                  pl.BlockSpec((1, D, CW_IN), lambda k: (k, 0, 0))],
        out_specs=[pl.BlockSpec((1, D, CW_IN), lambda k: (k, 0, 0)), pl.BlockSpec((T, D), lambda k: (0, 0))],
        out_shape=[jax.ShapeDtypeStruct((NCHIP, D, CW_IN), BF), jax.ShapeDtypeStruct((T, D), F32)],
        vmem=56, args=[h1, dproj, w_in], stages=stages)


def _bwd_prenorm(x, dh1, dxres, g1, stages=()):
    tm = 512

    def body(x_ref, dh_ref, dr_ref, g_ref, dx_ref, dg_ref):
        xv = x_ref[...]
        r = lax.rsqrt(_mean(xv * xv) + NORM_EPS)
        xn = xv * r
        dh = dh_ref[...]
        t = dh * g_ref[...]
        dx_ref[...] = dr_ref[...] + r * (t - xn * _mean(t * xn))
        _acc(dg_ref, _colsum(dh * xn), pl.program_id(0) == 0)

    row = pl.BlockSpec((tm, D), lambda i: (i, 0))
    vec = pl.BlockSpec((1, D), lambda i: (0, 0))
    return _call(
        body, name="bwd_prenorm", grid=(T // tm,),
        in_specs=[row, row, row, vec], out_specs=[row, vec],
        out_shape=[jax.ShapeDtypeStruct((T, D), F32), jax.ShapeDtypeStruct((1, D), F32)],
        vmem=48, args=[x, dh1, dxres, g1], stages=stages)


def _place():
    x, y, c = lax.axis_index("x"), lax.axis_index("y"), lax.axis_index("c")
    chips = [(1 - x, y), (x, 1 - y), (1 - x, 1 - y)]
    return x, y, c, chips


def _rcopy(src, dst, ssem, rsem, dev):
    return pltpu.make_async_remote_copy(src_ref=src, dst_ref=dst, send_sem=ssem, recv_sem=rsem,
                                        device_id=dev, device_id_type=MESH_ID)


def _sds(a):
    return jax.ShapeDtypeStruct(a.shape, a.dtype)


def _sem2(n, m):
    return [pltpu.SemaphoreType.DMA((n * m,)), pltpu.SemaphoreType.DMA((n * m,))]


ALL = (0, 1, 1)


def _piece(ref, k, half, part):
    hr = ref.shape[1] // 2
    r0, r1 = hr * part[0] // part[2], hr * part[1] // part[2]
    return ref.at[k, pl.ds(half * hr + r0, r1 - r0), :]


def _gather(fulls, ici=(), d2d=()):
    n = len(fulls)
    ici, d2d = list(ici), list(d2d)
    pieces = [("ici", i, part) for i, part in ici] + [("d2d", i, part) for i, part in d2d]

    def copies(outs, sems):
        x, y, c, chips = _place()
        me = 2 * x + y
        sib = (x, y, 1 - c)
        send, recv = [], []
        for q, (kind, i, part) in enumerate(pieces):
            for j, chip in enumerate(chips):
                k, s = 2 * chip[0] + chip[1], 3 * q + j
                if kind == "ici":
                    mine, theirs, dev = _piece(outs[i], me, c, part), _piece(outs[i], k, c, part), (*chip, c)
                else:
                    mine, theirs, dev = _piece(outs[i], k, c, part), _piece(outs[i], k, 1 - c, part), sib
                send.append(_rcopy(mine, mine, sems[0].at[s], sems[1].at[s], dev))
                recv.append(_rcopy(theirs, theirs, sems[0].at[s], sems[1].at[s], dev))
        return send, recv

    def start(ins, outs, sems):
        for cp in copies(outs, sems)[0]:
            cp.start()

    def finish(ins, outs, sems):
        send, recv = copies(outs, sems)
        for cp in recv:
            cp.wait_recv()
        for cp in send:
            cp.wait_send()

    sems = [pltpu.SemaphoreType.DMA((3 * len(pieces),)), pltpu.SemaphoreType.DMA((3 * len(pieces),))]
    return _Stage(fulls, [_sds(f) for f in fulls], {i: i for i in range(n)}, sems, start, finish)


def _gather_whole(v):
    def copies(ins, outs, sems):
        x, y, c, chips = _place()
        me = 2 * x + y
        send = [_rcopy(ins[0], outs[0].at[me], sems[0].at[j], sems[1].at[j], (*chip, c))
                for j, chip in enumerate(chips)]
        recv = [_rcopy(ins[0], outs[0].at[2 * chip[0] + chip[1]], sems[0].at[j], sems[1].at[j], (*chip, c))
                for j, chip in enumerate(chips)]
        return send, recv

    def start(ins, outs, sems):
        for cp in copies(ins, outs, sems)[0]:
            cp.start()

    def finish(ins, outs, sems):
        send, recv = copies(ins, outs, sems)
        for cp in recv:
            cp.wait_recv()
        for cp in send:
            cp.wait_send()

    return _Stage([v], [jax.ShapeDtypeStruct((NCHIP,) + v.shape, v.dtype)], {},
                  [pltpu.SemaphoreType.DMA((3,)), pltpu.SemaphoreType.DMA((3,))], start, finish)


def _to_sibling(srcs):
    n = len(srcs)

    def copies(ins, outs, sems):
        x, y, c, _ = _place()
        sib = (x, y, 1 - c)
        return [_rcopy(ins[i].at[:, 1 - c] if srcs[i].ndim == 4 else ins[i], outs[i], sems[0].at[i], sems[1].at[i], sib)
                for i in range(n)]

    def start(ins, outs, sems):
        for cp in copies(ins, outs, sems):
            cp.start()

    def finish(ins, outs, sems):
        for cp in copies(ins, outs, sems):
            cp.wait()

    shapes = [jax.ShapeDtypeStruct((NCHIP,) + s.shape[2:] if s.ndim == 4 else s.shape, s.dtype) for s in srcs]
    return _Stage(srcs, shapes, {}, [pltpu.SemaphoreType.DMA((n,)), pltpu.SemaphoreType.DMA((n,))], start, finish)


def _to_chips(srcs, parts=None, lands=None):
    n = len(srcs)
    parts = [ALL] * n if parts is None else parts
    lands = [None] * n if lands is None else lands
    given = [i for i in range(n) if lands[i] is not None]

    def rows(ref, i):
        hr = srcs[i].shape[1]
        r0, r1 = hr * parts[i][0] // parts[i][2], hr * parts[i][1] // parts[i][2]
        return ref.at[pl.ds(r0, r1 - r0), :]

    def copies(ins, outs, sems):
        x, y, c, chips = _place()
        me = 2 * x + y
        return [_rcopy(rows(ins[i].at[2 * chip[0] + chip[1]] if srcs[i].shape[0] == NCHIP else ins[i].at[c], i),
                       rows(outs[i].at[me], i), sems[0].at[3 * i + j], sems[1].at[3 * i + j], (*chip, c))
                for i in range(n) for j, chip in enumerate(chips)]

    def start(ins, outs, sems):
        for cp in copies(ins, outs, sems):
            cp.start()

    def finish(ins, outs, sems):
        for cp in copies(ins, outs, sems):
            cp.wait()

    shapes = [jax.ShapeDtypeStruct((NCHIP,) + s.shape[1:], s.dtype) for s in srcs]
    alias = {n + q: i for q, i in enumerate(given)}
    return _Stage(list(srcs) + [lands[i] for i in given], shapes, alias, _sem2(n, 3), start, finish)


HBM_REF = pl.BlockSpec(memory_space=pltpu.HBM)
SEM_REF = pl.BlockSpec(memory_space=pltpu.SEMAPHORE)
DATAFLOW = pltpu.SideEffectType.DATAFLOW_SIDE_EFFECTING


def _after(x):
    return _Stage([x], [], {}, [], lambda *a: None, lambda *a: None)


class _Flight:
    def __init__(self, stage, sems, bufs):
        self.stage, self.sems, self.bufs = stage, list(sems), list(bufs)

    def landed(self):
        st, n = self.stage, len(self.stage.operands)
        fresh = [j for j in range(len(st.out_shape)) if j not in st.alias.values()]
        back = {v: k for k, v in st.alias.items()}
        return [self.bufs[back[j]] if j in back else self.bufs[n + fresh.index(j)] for j in range(len(st.out_shape))]


def _split_call(name, finish=(), start=(), after=None):
    bufs, stage_bufs = [], []

    def slot(a):
        for i, b in enumerate(bufs):
            if b is a:
                return i
        bufs.append(a)
        return len(bufs) - 1

    fin_slots = [[slot(b) for b in fl.bufs] for fl in finish]
    for st in start:
        fresh = [lax.empty(o.shape, o.dtype) for j, o in enumerate(st.out_shape) if j not in st.alias.values()]
        stage_bufs.append([slot(a) for a in list(st.operands) + fresh])
    old_sems = [s for fl in finish for s in fl.sems]
    new_sems = [s for st in start for s in st.sems]
    nb, no, nn = len(bufs), len(old_sems), len(new_sems)

    def refs_of(st, slots, buf_refs):
        n = len(st.operands)
        ins = [buf_refs[i] for i in slots[:n]]
        fresh = [j for j in range(len(st.out_shape)) if j not in st.alias.values()]
        back = {v: k for k, v in st.alias.items()}
        outs = [ins[back[j]] if j in back else buf_refs[slots[n + fresh.index(j)]] for j in range(len(st.out_shape))]
        return ins, outs

    def body(*refs):
        buf_refs, sem_in = refs[:nb], refs[nb:nb + no]
        sem_out = refs[nb + no + (after is not None):][:nn]
        token = refs[-1]
        pos = 0
        for fl, slots in zip(finish, fin_slots):
            ins, outs = refs_of(fl.stage, slots, buf_refs)
            fl.stage.finish(ins, outs, sem_in[pos:pos + len(fl.sems)])
            pos += len(fl.sems)
        pos = 0
        for st, slots in zip(start, stage_bufs):
            ins, outs = refs_of(st, slots, buf_refs)
            st.start(ins, outs, sem_out[pos:pos + len(st.sems)])
            pos += len(st.sems)
        token[...] = jnp.zeros_like(token)

    res = pl.pallas_call(
        body, name=name,
        out_shape=tuple(new_sems) + tuple(pltpu.HBM(b.shape, b.dtype) for b in bufs) +
                  (jax.ShapeDtypeStruct((8, LANE), F32),),
        in_specs=(HBM_REF,) * nb + (SEM_REF,) * no + ((pl.BlockSpec(memory_space=pl.ANY),) if after is not None else ()),
        out_specs=(SEM_REF,) * nn + (HBM_REF,) * nb + (pl.BlockSpec(memory_space=pltpu.VMEM),),
        input_output_aliases={i: nn + i for i in range(nb)},
        compiler_params=pltpu.CompilerParams(has_side_effects=DATAFLOW),
    )(*_hbm(*bufs), *old_sems, *([after] if after is not None else []))
    sems, thru, token = res[:nn], res[nn:nn + nb], res[-1]
    for fl, slots in zip(finish, fin_slots):
        fl.bufs = [thru[i] for i in slots]
    flights, pos = [], 0
    for st, slots in zip(start, stage_bufs):
        flights.append(_Flight(st, sems[pos:pos + len(st.sems)], [thru[i] for i in slots]))
        pos += len(st.sems)
    return flights, token


def _last_copies(p_ref, land_ref, ssem, rsem):
    x, y, c, chips = _place()
    me = 2 * x + y
    send = [_rcopy(p_ref.at[2 * chip[0] + chip[1]], land_ref.at[me], ssem.at[j], rsem.at[j], (*chip, c))
            for j, chip in enumerate(chips)]
    recv = [_rcopy(p_ref.at[2 * chip[0] + chip[1]], land_ref.at[2 * chip[0] + chip[1]], ssem.at[j], rsem.at[j],
                   (*chip, c)) for j, chip in enumerate(chips)]
    return send, recv


def _chips_start(p):
    def body(p_ref, land_ref, ssem, rsem, p_thru, land_thru, token):
        for cp in _last_copies(p_ref, land_ref, ssem, rsem)[0]:
            cp.start()
        token[...] = jnp.zeros_like(token)

    return pl.pallas_call(
        body, name="reduce_last_start",
        out_shape=(pltpu.SemaphoreType.DMA((3,)), pltpu.SemaphoreType.DMA((3,)), pltpu.HBM(p.shape, p.dtype),
                   pltpu.HBM(p.shape, p.dtype), jax.ShapeDtypeStruct((8, LANE), F32)),
        in_specs=(HBM_REF, HBM_REF),
        out_specs=(SEM_REF, SEM_REF, HBM_REF, HBM_REF, pl.BlockSpec(memory_space=pltpu.VMEM)),
        input_output_aliases={0: 2, 1: 3},
        compiler_params=pltpu.CompilerParams(has_side_effects=DATAFLOW),
    )(*_hbm(p, lax.empty(p.shape, p.dtype)))


def _chips_wait(ssem, rsem, p_thru, land_thru, after):
    def body(p_ref, land_ref, ssem, rsem, after_ref, p_dead, got_ref):
        send, recv = _last_copies(p_ref, land_ref, ssem, rsem)
        for cp in send:
            cp.wait_send()
        for cp in recv:
            cp.wait_recv()

    return pl.pallas_call(
        body, name="reduce_last_wait",
        out_shape=(pltpu.HBM(p_thru.shape, p_thru.dtype), pltpu.HBM(land_thru.shape, land_thru.dtype)),
        in_specs=(HBM_REF, HBM_REF, SEM_REF, SEM_REF, pl.BlockSpec(memory_space=pl.ANY)),
        out_specs=(HBM_REF, HBM_REF), input_output_aliases={0: 0, 1: 1},
        compiler_params=pltpu.CompilerParams(has_side_effects=DATAFLOW),
    )(p_thru, land_thru, ssem, rsem, after)


def _share(pairs):
    n = len(pairs)

    def start(ins, outs, sems):
        x, y, c, _ = _place()
        for i in range(n):
            _rcopy(outs[i].at[c], outs[i].at[c], sems[0].at[i], sems[1].at[i], (x, y, 1 - c)).start()

    def finish(ins, outs, sems):
        x, y, c, _ = _place()
        for i in range(n):
            _rcopy(outs[i].at[c], outs[i].at[c], sems[0].at[i], sems[1].at[i], (x, y, 1 - c)).wait_send()
            _rcopy(outs[i].at[1 - c], outs[i].at[1 - c], sems[0].at[i], sems[1].at[i], (x, y, 1 - c)).wait_recv()

    return _Stage(pairs, [_sds(p) for p in pairs], {i: i for i in range(n)},
                  [pltpu.SemaphoreType.DMA((n,)), pltpu.SemaphoreType.DMA((n,))], start, finish)


def _row_block(rows, cols, itemsize=4, target=2 * MIB):
    br = rows
    while br * cols * itemsize > target and br % 32 == 0:
        br //= 2
    return br


def _cast_place(w, chip_idx, name):
    rows, cols = w.shape
    br = _row_block(rows, cols)

    def body(k_ref, w_ref, o_ref):
        o_ref[0] = w_ref[...].astype(BF)

    return _call(
        body, name=name, grid=(rows // br,), prefetch=chip_idx,
        in_specs=[pl.BlockSpec((br, cols), lambda r, k: (r, 0))],
        out_specs=[pl.BlockSpec((1, br, cols), lambda r, k: (k[0], r, 0))],
        out_shape=[jax.ShapeDtypeStruct((NCHIP, rows, cols), BF)], vmem=32, args=[w])[0][0]


def _cast_place_multi(ws, chip_idx, stages=()):
    br = 128
    nblk = [a.shape[0] // br for a in ws]
    starts = [sum(nblk[:i]) for i in range(len(ws))]

    def body(k_ref, *refs):
        r = pl.program_id(0)
        for i in range(len(ws)):
            @pl.when(jnp.logical_and(r >= starts[i], r < starts[i] + nblk[i]))
            def _(i=i):
                refs[len(ws) + i][0] = refs[i][...].astype(BF)

    def at(i):
        return functools.partial(lambda r, s, nb: jnp.clip(r - s, 0, nb - 1), s=starts[i], nb=nblk[i])

    outs, landed = _call(
        body, name="cast_rest", grid=(sum(nblk),), prefetch=chip_idx,
        in_specs=[pl.BlockSpec((br, a.shape[1]), functools.partial(lambda r, k, f: (f(r), 0), f=at(i)))
                  for i, a in enumerate(ws)],
        out_specs=[pl.BlockSpec((1, br, a.shape[1]), functools.partial(lambda r, k, f: (k[0], f(r), 0), f=at(i)))
                   for i, a in enumerate(ws)],
        out_shape=[jax.ShapeDtypeStruct((NCHIP,) + a.shape, BF) for a in ws], vmem=32, args=list(ws), stages=stages)
    return outs, landed


def _add_sibling(g, land, cidx, name, stages=()):
    _, _, hr, cols = g.shape
    br = _row_block(hr, cols)

    def body(c_ref, g_ref, l_ref, o_ref):
        o_ref[...] = (g_ref[0, 0].astype(F32) + l_ref[0].astype(F32)).astype(BF)[None]

    outs, st = _call(
        body, name=name, grid=(NCHIP, hr // br), prefetch=cidx,
        in_specs=[pl.BlockSpec((1, 1, br, cols), lambda k, r, c: (k, c[0], r, 0)),
                  pl.BlockSpec((1, br, cols), lambda k, r, c: (k, r, 0))],
        out_specs=[pl.BlockSpec((1, br, cols), lambda k, r, c: (k, r, 0))],
        out_shape=[jax.ShapeDtypeStruct((NCHIP, hr, cols), BF)], vmem=32, args=[g, land], stages=stages)
    return outs[0], st


def _add_pair(a, b, name):
    rows, cols = a.shape

    def body(a_ref, b_ref, o_ref):
        o_ref[...] = a_ref[...] + b_ref[...]

    spec = pl.BlockSpec((rows, cols), lambda r: (0, 0))
    return _call(body, name=name, grid=(1,), in_specs=[spec, spec], out_specs=[spec], out_shape=[_sds(a)],
                 vmem=32, args=[a, b])[0][0]


def _add_chips(own, land, idx, name, stages=None):
    _, hr, cols = land.shape
    br = _row_block(hr, cols)

    def body(s_ref, a_ref, b_ref, c_ref, d_ref, o_ref):
        o_ref[...] = (a_ref[...].astype(F32) + b_ref[...].astype(F32)) + (c_ref[...].astype(F32) +
                                                                           d_ref[...].astype(F32))

    spec = lambda q: pl.BlockSpec((1, br, cols), functools.partial(lambda r, s, q: (s[q], r, 0), q=q))
    outs, landed = _call(
        body, name=name, grid=(hr // br,), prefetch=idx,
        in_specs=[spec(0), spec(1), spec(2), spec(3)], out_specs=[spec(4)],
        out_shape=[jax.ShapeDtypeStruct((2, hr, cols), F32)], vmem=48, args=[own, land, land, land],
        stages=stages or ())
    return outs[0] if stages is None else (outs[0], landed)


def _adamw_math(w, g, m, v):
    mn = ADAM_B1 * m + (1.0 - ADAM_B1) * g
    vn = ADAM_B2 * v + (1.0 - ADAM_B2) * (g * g)
    m_hat = mn / (1.0 - ADAM_B1 ** ADAM_STEP)
    v_hat = vn / (1.0 - ADAM_B2 ** ADAM_STEP)
    return -ADAM_LR * (m_hat / (jnp.sqrt(v_hat) + ADAM_EPS) + ADAM_WD * w), mn, vn


def _adamw(w, g, m, v, name, stages=()):
    rows, cols = w.shape
    br = _row_block(rows, cols)

    def body(w_ref, g_ref, m_ref, v_ref, d_ref, mo_ref, vo_ref):
        d_ref[...], mo_ref[...], vo_ref[...] = _adamw_math(w_ref[...], g_ref[...], m_ref[...], v_ref[...])

    spec = pl.BlockSpec((br, cols), lambda r: (r, 0))
    return _call(body, name=name, grid=(rows // br,), in_specs=[spec] * 4, out_specs=[spec] * 3,
                 out_shape=[_sds(w)] * 3, vmem=48, args=[w, g, m, v], stages=stages)


def _adamw_multi(names, w, g, m, v, stages=()):
    cols = w[names[0]].shape[1]
    br = 128
    nblk = [w[n].shape[0] // br for n in names]
    starts = [sum(nblk[:i]) for i in range(len(names))]

    def body(*refs):
        r = pl.program_id(0)
        for i in range(len(names)):
            w_ref, g_ref, m_ref, v_ref = refs[4 * i:4 * i + 4]
            d_ref, mo_ref, vo_ref = refs[4 * len(names) + 3 * i:4 * len(names) + 3 * i + 3]

            @pl.when(jnp.logical_and(r >= starts[i], r < starts[i] + nblk[i]))
            def _():
                d_ref[...], mo_ref[...], vo_ref[...] = _adamw_math(w_ref[...], g_ref[...], m_ref[...], v_ref[...])

    def spec(i):
        return pl.BlockSpec((br, cols), functools.partial(
            lambda r, s, nb: (jnp.clip(r - s, 0, nb - 1), 0), s=starts[i], nb=nblk[i]))

    outs, landed = _call(
        body, name="adamw_" + "_".join(names), grid=(sum(nblk),),
        in_specs=[spec(i) for i in range(len(names)) for _ in range(4)],
        out_specs=[spec(i) for i in range(len(names)) for _ in range(3)],
        out_shape=[_sds(w[n]) for n in names for _ in range(3)], vmem=48,
        args=[a[n] for n in names for a in (w, g, m, v)], stages=stages)
    return {n: outs[3 * i:3 * i + 3] for i, n in enumerate(names)}, landed


def _to_everyone(v):
    deltas = [(a, b, e) for a in (0, 1) for b in (0, 1) for e in (0, 1)][1:]

    def copies(ins, outs, sems):
        x, y, c, _ = _place()
        me = 4 * x + 2 * y + c
        flip = lambda p, f: 1 - p if f else p
        return [_rcopy(ins[0], outs[0].at[me], sems[0].at[q], sems[1].at[q], (flip(x, a), flip(y, b), flip(c, e)))
                for q, (a, b, e) in enumerate(deltas)]

    def start(ins, outs, sems):
        for cp in copies(ins, outs, sems):
            cp.start()

    def finish(ins, outs, sems):
        for cp in copies(ins, outs, sems):
            cp.wait()

    n = len(deltas)
    return _Stage([v], [jax.ShapeDtypeStruct((2 * NCHIP,) + v.shape, v.dtype)], {},
                  [pltpu.SemaphoreType.DMA((n,)), pltpu.SemaphoreType.DMA((n,))], start, finish)


SMALL_AT = {"norm_mix_pre": (0, 1, D), "norm_mix_post": (1, 1, D), "norm_mlp_pre": (2, 1, D),
            "norm_mlp_post": (3, 1, D), "b_gate": (4, 2, D), "conv_b": (6, 1, D), "lru_b_a": (7, 1, D),
            "lru_b_x": (8, 1, D), "lru_lambda": (9, 1, D), "pool_scale": (10, 1, DP)}
SMALL_SEPARATE = ["conv_w", "lru_w_a", "lru_w_x", "pool_w"]


def _adamw_small(small_sum, first_all, sep_grads, w, m, v):
    packed, sep = list(SMALL_AT), list(SMALL_SEPARATE)
    names = packed + sep

    def body(*refs):
        s_ref, a_ref, refs = refs[0], refs[1], refs[2:]
        g_sep, refs = refs[:len(sep)], refs[len(sep):]
        nn = len(names)
        w_r, m_r, v_r, refs = refs[:nn], refs[nn:2 * nn], refs[2 * nn:3 * nn], refs[3 * nn:]
        g_out, refs = refs[:len(packed)], refs[len(packed):]
        d_o, m_o, v_o = refs[:nn], refs[nn:2 * nn], refs[2 * nn:3 * nn]
        for i, n in enumerate(names):
            if i == 0:
                g = a_ref[0:1, :]
                for q in range(1, 2 * NCHIP):
                    g = g + a_ref[q:q + 1, :]
                g_out[i][...] = g
            elif n in SMALL_AT:
                r0, nr, nc = SMALL_AT[n]
                g = jnp.concatenate([s_ref[r0 + q:r0 + q + 1, :nc] for q in range(nr)], axis=1)
                g_out[i][...] = g
            else:
                g = g_sep[i - len(packed)][...]
            d_o[i][...], m_o[i][...], v_o[i][...] = _adamw_math(w_r[i][...], g, m_r[i][...], v_r[i][...])

    ws = [w[n] for n in names]
    res = pl.pallas_call(
        body, name="adamw_small",
        out_shape=[_sds(w[n]) for n in packed] + [_sds(a) for a in ws] * 3,
        compiler_params=_cp(32),
    )(*_hbm(small_sum, first_all, *sep_grads, *ws, *[m[n] for n in names], *[v[n] for n in names]))
    nn, npk = len(names), len(packed)
    grad = dict(zip(packed, res[:npk]))
    delta = dict(zip(names, res[npk:npk + nn]))
    new_m = dict(zip(names, res[npk + nn:npk + 2 * nn]))
    new_v = dict(zip(names, res[npk + 2 * nn:]))
    return grad, delta, new_m, new_v


W_NAMES = ["norm_mix_pre", "norm_mix_post", "norm_mlp_pre", "norm_mlp_post", "w_in", "b_gate", "conv_w", "conv_b",
           "lru_w_a", "lru_b_a", "lru_w_x", "lru_b_x", "lru_lambda", "pool_w", "pool_scale", "w_lru_up",
           "w_pool_up", "w_o", "w_ff1", "w_ff2"]
BIG = ["w_in", "w_lru_up", "w_pool_up", "w_o", "w_ff1", "w_ff2"]


def _block_diag(w):
    hd = w.shape[-1]
    per = CB // hd
    w4 = w.reshape(NG, per, hd, hd)
    eye = jnp.eye(per, dtype=w.dtype)
    return jnp.einsum("gpij,pq->gpiqj", w4, eye).reshape(NG, CB, CB)


def _block_diag_extract(d, hd):
    per = CB // hd
    d5 = d.reshape(NG, per, hd, per, hd)
    return jnp.stack([d5[:, p, :, p, :] for p in range(per)], axis=1).reshape(NG * per, hd, hd)


def _halves(g):
    return g.reshape(NCHIP, 2, g.size // (g.shape[-1] * 2 * NCHIP), g.shape[-1])


def kernel(x, norm_mix_pre, norm_mix_post, norm_mlp_pre, norm_mlp_post, w_in, b_gate, conv_w, conv_b, lru_w_a, lru_b_a, lru_w_x, lru_b_x, lru_lambda, pool_w, pool_scale, w_lru_up, w_pool_up, w_o, w_ff1, w_ff2, loss_target, m_norm_mix_pre, m_norm_mix_post, m_norm_mlp_pre, m_norm_mlp_post, m_w_in, m_b_gate, m_conv_w, m_conv_b, m_lru_w_a, m_lru_b_a, m_lru_w_x, m_lru_b_x, m_lru_lambda, m_pool_w, m_pool_scale, m_w_lru_up, m_w_pool_up, m_w_o, m_w_ff1, m_w_ff2, v_norm_mix_pre, v_norm_mix_post, v_norm_mlp_pre, v_norm_mlp_post, v_w_in, v_b_gate, v_conv_w, v_conv_b, v_lru_w_a, v_lru_b_a, v_lru_w_x, v_lru_b_x, v_lru_lambda, v_pool_w, v_pool_scale, v_w_lru_up, v_w_pool_up, v_w_o, v_w_ff1, v_w_ff2):
    args = dict(locals())
    two_d = lambda a: a.reshape(-1, a.shape[-1])
    w = {n: two_d(args[n]) for n in W_NAMES}
    mom = {n: two_d(args["m_" + n]) for n in W_NAMES}
    var = {n: two_d(args["v_" + n]) for n in W_NAMES}
    i32 = lambda val: jnp.asarray(val, jnp.int32)
    chip = i32(2 * lax.axis_index("x") + lax.axis_index("y"))
    core = i32(lax.axis_index("c"))
    cidx = core.reshape(1)
    zero = i32(0)
    hd = lru_w_a.shape[-1]
    xs, target = x[0], loss_target[0]
    g1, g2, g3, g4 = norm_mix_pre, norm_mix_post, norm_mlp_pre, norm_mlp_post

    mix = ["w_lru_up", "w_pool_up", "w_o"]
    full = {"w_in": _cast_place(w["w_in"], chip.reshape(1), "cast_w_in")}
    (fl_in, fl_conv), first = _split_call("gather_start_first", start=[
        _gather([full["w_in"]], ici=[(0, ALL)]), _gather_whole(w["conv_w"])])
    casts, _ = _cast_place_multi([w[n] for n in BIG[1:]], chip.reshape(1), stages=[_after(first)])
    full.update(zip(BIG[1:], casts))
    (fl_mix, fl_ff1, fl_ff2), started = _split_call("gather_start_rest", start=[
        _gather([full[n] for n in mix], ici=[(0, ALL), (1, ALL), (2, ALL)]),
        _gather([full["w_ff1"]], ici=[(0, ALL)]), _gather([full["w_ff2"]], ici=[(0, ALL)])])
    wa = _block_diag(lru_w_a[0]).astype(BF)
    wx = _block_diag(lru_w_x[0]).astype(BF)
    pw = pool_w[0].astype(BF)

    def to_sibling(name, flight, after=None):
        (fl,), passed = _split_call(name + "_pass", finish=[flight], after=after,
                                    start=[_gather(flight.landed(), d2d=[(i, ALL) for i in range(len(flight.bufs))])])
        passed_on.append(passed)
        return fl

    passed_on = []

    def arrived(name, flight, after=None):
        _split_call(name + "_done", finish=[flight], after=after)
        return flight.landed()

    fl_in = to_sibling("gather_w_in", fl_in, after=started)
    w_in_f, = arrived("gather_w_in", fl_in)
    conv_all, = arrived("gather_conv", fl_conv)
    full["w_in"] = w_in_f
    conv_all = lax.dynamic_update_slice(conv_all, w["conv_w"][None], (chip, zero, zero))
    conv_full = jnp.transpose(conv_all, (1, 0, 2)).reshape(4, DR)
    (proj, h1), _ = _fwd_inproj(xs, g1, w_in_f)
    fl_mix = to_sibling("gather_mix", fl_mix, after=h1)
    (ylru, hs), _ = _fwd_lru(proj, conv_full, conv_b, wa, lru_b_a, wx, lru_b_x, lru_lambda)
    got = arrived("gather_mix", fl_mix, after=ylru)
    fl_ff1 = to_sibling("gather_ff1", fl_ff1, after=ylru)
    w_lru_up_f, w_pool_up_f, w_o_f = got[0].reshape(DR, D), got[1], got[2].reshape(D, D)
    ypool = _fwd_pool(proj, pw, pool_scale)
    (x2, h2, m, mrg, bra, brb), _ = _fwd_merge(xs, ylru, ypool, proj, b_gate, g2, g3, w_lru_up_f, w_pool_up_f, w_o_f,
                                               stages=[_after(passed_on[-1])])
    fl_ff2 = to_sibling("gather_ff2", fl_ff2, after=h2)
    ff1, = arrived("gather_ff1", fl_ff1, after=h2)
    ff2, = arrived("gather_ff2", fl_ff2)
    ff2 = ff2.reshape(DF, D)
    a1, f = _fwd_mlp(h2, ff1, ff2)
    lossp, dy, df, dg4 = _loss_head(f, x2, target, g4)

    idx_big = jnp.stack([chip, (chip + 1) % NCHIP, (chip + 2) % NCHIP, (chip + 3) % NCHIP, core])
    dh2, df1 = _bwd_mlp_x(df, a1, ff1, ff2)
    dw_ff1, dw_ff2 = _bwd_mlp_w(df, h2, a1, df1)
    g_ff = [_halves(dw_ff1), _halves(dw_ff2)]
    (dxres, dgates, dylru, dypool, dm, dbra, dbrb, dg2, dg3, dbg), (l_ff,) = _bwd_merge(
        dh2, dy, x2, m, bra, brb, proj, b_gate, g2, g3, w_lru_up_f, w_pool_up_f, w_o_f, stages=[_to_sibling(g_ff)])
    p_ff = [_add_sibling(g, l, cidx, "add_sibling_" + n)[0] for g, l, n in zip(g_ff, l_ff, ["w_ff1", "w_ff2"])]
    (fl_ff,), sent_ff = _split_call("reduce_ff_start", start=[_to_chips(p_ff)])
    (dw_o, dw_lru_up, dw_pool_up), _ = _dw_merge(mrg, dm, ylru, dbra, ypool, dbrb, stages=[_after(sent_ff)])
    g_mix = [_halves(dw_lru_up), _halves(dw_pool_up), _halves(dw_o)]
    (dxp, dgl, dcw, dcb, dwa, dba, dwx, dbx, dlam), (l_mix,) = _bwd_lru(
        proj, hs, dylru, conv_full, conv_b, wa, lru_b_a, wx, lru_b_x, lru_lambda, stages=[_to_sibling(g_mix)])
    p_mix = [_add_sibling(g, l, cidx, "add_sibling_" + n)[0] for g, l, n in zip(g_mix, l_mix, mix)]
    dxpool, dpw, dsc = _bwd_pool(proj, dypool, pw, pool_scale)
    dproj = jnp.concatenate([dxp, dgl, dxpool, dgates], axis=1)
    small = jnp.concatenate([
        jnp.zeros((1, D), F32), dg2, dg3, dg4, dbg.reshape(2, D), dcb, dba, dbx, dlam,
        jnp.pad(dsc, ((0, 0), (0, D - DP))), jnp.pad(lossp, ((0, 0), (0, D - 1))), dcw,
        _block_diag_extract(dwa, hd).reshape(-1, D), _block_diag_extract(dwx, hd).reshape(-1, D),
        dpw.reshape(-1, D)], axis=0)
    (dw_in, dh1), (c_mix, (l_small,)) = _bwd_inproj(h1, dproj, full["w_in"],
                                                     stages=[_to_chips(p_mix), _to_sibling([small])])
    small2 = _add_pair(small, l_small, "add_sibling_small").reshape(2, SMALL_ROWS // 2, D)
    g_in = _halves(dw_in)
    done = ["w_ff1", "w_ff2"] + mix
    (fl_gin, fl_small), _ = _split_call("reduce_in_sibling_start", start=[_to_sibling([g_in]), _to_chips([small2])])
    _split_call("reduce_in_sibling_done", finish=[fl_gin, fl_ff])
    (g_in, l_in), (p_ff1, p_ff2, c_ff1, c_ff2) = fl_gin.bufs, fl_ff.bufs
    p_in = _add_sibling(g_in, l_in, cidx, "add_sibling_w_in")[0]
    ssem, rsem, p_in, c_in, token = _chips_start(p_in)
    pairs = [_add_chips(p, l, idx_big, "add_chips_" + n, stages=[_after(token)])[0]
             for p, l, n in zip([p_ff1, p_ff2] + p_mix, [c_ff1, c_ff2] + c_mix, done)]
    _split_call("reduce_small_done", finish=[fl_small], after=pairs[-1])
    small2, c_small = fl_small.bufs
    own_small = lax.dynamic_index_in_dim(small2, core, 0, keepdims=True)
    c_small = lax.dynamic_update_slice(c_small, own_small, (chip, zero, zero))
    pair_small = _add_chips(c_small, c_small, jnp.stack([zero, zero + 1, zero + 2, zero + 3, core]), "add_chips_small")
    (fl_share,), shared_start = _split_call("reduce_share_start", start=[_share(pairs + [pair_small])])
    (grad_x, dg1), _ = _bwd_prenorm(xs, dh1, dxres, g1, stages=[_after(shared_start)])
    _split_call("reduce_share_done", finish=[fl_share], after=dg1)
    shared = fl_share.landed()
    pairs, pair_small = shared[:-1], shared[-1]

    grads, delta, new_m, new_v = {}, {}, {}, {}
    for n, p in zip(done, pairs):
        grads[n] = p.reshape(-1, p.shape[-1])

    def update(n, stages=()):
        (delta[n], new_m[n], new_v[n]), landed = _adamw(w[n], grads[n], mom[n], var[n], "adamw_" + n, stages=stages)
        return landed

    updated, _ = _adamw_multi(["w_ff1", "w_ff2", "w_o", "w_lru_up"], w, grads, mom, var)
    for n, (d, mo, vo) in updated.items():
        delta[n], new_m[n], new_v[n] = d, mo, vo
    p_in, c_in = _chips_wait(ssem, rsem, p_in, c_in, new_v["w_lru_up"])
    pair_in = _add_chips(p_in, c_in, idx_big, "add_chips_w_in")
    ((pair_in,), (dg1_all,)) = update("w_pool_up", stages=[_share([pair_in]), _to_everyone(dg1)])
    dg1_all = lax.dynamic_update_slice(dg1_all, dg1[None], (2 * chip + core, zero, zero)).reshape(2 * NCHIP, D)
    grads["w_in"] = pair_in.reshape(-1, pair_in.shape[-1])
    update("w_in")
    small_sum = pair_small.reshape(SMALL_ROWS, D)
    loss = 0.5 * small_sum[LOSS_ROW, 0]
    ccols = DR // NCHIP
    sep = [lax.dynamic_slice(small_sum[12:16], (zero, chip * ccols), (4, ccols)),
           small_sum[16:80].reshape(-1, hd), small_sum[80:144].reshape(-1, hd), small_sum[144:208].reshape(-1, PG)]
    g_s, d_s, m_s, v_s = _adamw_small(small_sum, dg1_all, sep, w, mom, var)
    grads.update(g_s)
    grads.update(dict(zip(SMALL_SEPARATE, sep)))
    delta.update(d_s)
    new_m.update(m_s)
    new_v.update(v_s)

    out = lambda d: [d[n].reshape(args[n].shape) for n in W_NAMES]
    return (loss, grad_x[None], *out(grads), *out(delta), *out(new_m), *out(new_v))
```

```python
import functools
import math

import jax
import jax.numpy as jnp
from jax import lax
from jax.experimental import pallas as pl
from jax.experimental.pallas import tpu as pltpu

F32 = jnp.float32
BF = jnp.bfloat16

T = 2048
D = 1024
DR = 1024
DP = 512
DF = 4096
DIN = 4608
NCHIP = 4
CW_IN = DIN // NCHIP
LANE = 128
CB = 128
NG = DR // CB
PG = 128
POOL_WINDOWS = (2, 4, 8, 16)
NORM_EPS = 1e-6
LRU_C = 8.0
GELU_C = math.sqrt(2.0 / math.pi)
ADAM_LR = 0.001
ADAM_B1 = 0.9
ADAM_B2 = 0.999
ADAM_EPS = 1e-08
ADAM_WD = 0.01
ADAM_STEP = 10
MESH_ID = pl.DeviceIdType.MESH
ANY = pl.BlockSpec(memory_space=pl.ANY)
SMALL_ROWS = 208
LOSS_ROW = 11
MIB = 1 << 20


def _cp(vmem_mib=None):
    if vmem_mib is None:
        return pltpu.CompilerParams()
    return pltpu.CompilerParams(vmem_limit_bytes=vmem_mib * MIB)


def _hbm(*arrays):
    return [pltpu.with_memory_space_constraint(a, pltpu.HBM) for a in arrays]


def _hbm_out(shapes):
    return [pltpu.HBM(s.shape, s.dtype) for s in shapes]


class _Stage:
    def __init__(self, operands, out_shape, alias, sems, start, finish):
        self.operands, self.out_shape, self.alias, self.sems = list(operands), list(out_shape), dict(alias), list(sems)
        self.start, self.finish = start, finish


def _call(body, *, name, grid, in_specs, out_specs, out_shape, args, vmem=None, stages=(), prefetch=None,
          scratch=()):
    nin, nout = len(in_specs), len(out_specs)
    npre = 0 if prefetch is None else 1
    st_args, st_shapes, st_sems, aliases = [], [], list(scratch), {}
    for st in stages:
        for k, v in st.alias.items():
            aliases[npre + nin + len(st_args) + k] = nout + len(st_shapes) + v
        st_args += st.operands
        st_shapes += st.out_shape
        st_sems += st.sems

    def wrapped(*refs):
        pre, refs = refs[:npre], refs[npre:]
        ins, pos = refs[:nin], nin
        st_ins = []
        for st in stages:
            st_ins.append(refs[pos:pos + len(st.operands)])
            pos += len(st.operands)
        outs, pos = refs[pos:pos + nout], pos + nout
        st_outs = []
        for st in stages:
            st_outs.append(refs[pos:pos + len(st.out_shape)])
            pos += len(st.out_shape)
        work, pos = refs[pos:pos + len(scratch)], pos + len(scratch)
        sems = []
        for st in stages:
            sems.append(refs[pos:pos + len(st.sems)])
            pos += len(st.sems)
        if stages:
            first = functools.reduce(jnp.logical_and, [pl.program_id(a) == 0 for a in range(len(grid))])

            @pl.when(first)
            def _():
                for st, a, b, s in zip(stages, st_ins, st_outs, sems):
                    st.start(a, b, s)

        body(*pre, *ins, *outs, *work)
        if stages:
            last = functools.reduce(jnp.logical_and, [pl.program_id(a) == g - 1 for a, g in enumerate(grid)])

            @pl.when(last)
            def _():
                for st, a, b, s in zip(stages, st_ins, st_outs, sems):
                    st.finish(a, b, s)

    all_in = list(in_specs) + [ANY] * len(st_args)
    all_out = list(out_specs) + [ANY] * len(st_shapes)
    kw = dict(has_side_effects=True) if stages else {}
    if vmem is not None:
        kw["vmem_limit_bytes"] = vmem * MIB
    if prefetch is None:
        gkw = dict(grid=grid, in_specs=all_in, out_specs=all_out, scratch_shapes=st_sems)
    else:
        gkw = dict(grid_spec=pltpu.PrefetchScalarGridSpec(
            num_scalar_prefetch=1, grid=grid, in_specs=all_in, out_specs=all_out, scratch_shapes=st_sems))
    res = pl.pallas_call(
        wrapped, name=name, out_shape=_hbm_out(list(out_shape) + st_shapes), input_output_aliases=aliases,
        compiler_params=pltpu.CompilerParams(**kw), **gkw,
    )(*([prefetch] if npre else []), *_hbm(*args, *st_args))
    outs, rest, st_res = list(res[:nout]), list(res[nout:]), []
    for st in stages:
        st_res.append(rest[:len(st.out_shape)])
        rest = rest[len(st.out_shape):]
    return outs, st_res


def _mm(a, b):
    return jnp.dot(a.astype(BF), b.astype(BF), preferred_element_type=F32)


def _mm_nt(a, b):
    return lax.dot_general(a.astype(BF), b.astype(BF), (((1,), (1,)), ((), ())),
                           preferred_element_type=F32)


def _mm_tn(a, b):
    return lax.dot_general(a.astype(BF), b.astype(BF), (((0,), (0,)), ((), ())),
                           preferred_element_type=F32)


def _rows(v):
    return lax.broadcasted_iota(jnp.int32, v.shape, 0)


def _sd(v, s, fill=0.0):
    return jnp.where(_rows(v) >= s, pltpu.roll(v, s, axis=0), fill)


def _su(v, s, fill=0.0):
    n = v.shape[0]
    return jnp.where(_rows(v) < n - s, pltpu.roll(v, n - s, axis=0), fill)


def _sigmoid(z):
    return 1.0 / (1.0 + jnp.exp(-z))


def _softplus(z):
    e = jnp.exp(-jnp.abs(z))
    u = 1.0 + e
    d = u - 1.0
    log1p = jnp.where(d == 0.0, e, jnp.log(u) * (e / jnp.where(d == 0.0, 1.0, d)))
    return jnp.maximum(z, 0.0) + log1p


def _mean(v):
    return jnp.mean(v, axis=-1, keepdims=True)


def _colsum(v):
    return jnp.sum(v, axis=0, keepdims=True)


def _acc(ref, val, first):
    @pl.when(first)
    def _():
        ref[...] = val

    @pl.when(jnp.logical_not(first))
    def _():
        ref[...] += val


def _conv(xp, cw, cb):
    x1, x2, x3 = _sd(xp, 1), _sd(xp, 2), _sd(xp, 3)
    xc = cb + cw[0:1] * x3 + cw[1:2] * x2 + cw[2:3] * x1 + cw[3:4] * xp
    return xc, x1, x2, x3


def _lru_gates(xc, wa, ba, wx, bx, lam):
    xcb = xc.astype(BF)
    r = _sigmoid(_mm(xcb, wa) + ba)
    ii = _sigmoid(_mm(xcb, wx) + bx)
    sp = _softplus(-lam)
    la = (-LRU_C) * r * sp
    a = jnp.exp(la)
    mult = jnp.sqrt(-jnp.tanh(la) * (a * a + 1.0))
    return xcb, r, ii, sp, a, mult


def _gelu_parts(g):
    th = jnp.tanh(GELU_C * (g + 0.044715 * (g * g * g)))
    gel = 0.5 * g * (1.0 + th)
    dgel = 0.5 * (1.0 + th) + 0.5 * g * (1.0 - th * th) * (GELU_C * (1.0 + 3.0 * 0.044715 * (g * g)))
    return gel, dgel


def _tile_scan(a, b, a_s, b_s, out_ref, reverse):
    n = a.shape[0]
    nt = n // 8
    sub = jnp.bitwise_and(_rows(a), 7)
    s = 1
    while s < 8:
        keep = sub < 8 - s if reverse else sub >= s
        amount = n - s if reverse else s
        b = b + a * jnp.where(keep, pltpu.roll(b, amount, axis=0), 0.0)
        a = a * jnp.where(keep, pltpu.roll(a, amount, axis=0), 1.0)
        s *= 2
    a_s[...] = a
    b_s[...] = b
    edge = pl.ds(0 if reverse else 7, nt, stride=8)
    ta, tb = a_s[edge, :], b_s[edge, :]
    shift = _su if reverse else _sd
    s = 1
    while s < nt:
        tb = tb + ta * shift(tb, s, 0.0)
        if 2 * s < nt:
            ta = ta * shift(ta, s, 1.0)
        s *= 2
    enters = shift(tb, 1, 0.0)
    for o in range(8):
        rows = pl.ds(o, nt, stride=8)
        out_ref[rows, :] = b_s[rows, :] + a_s[rows, :] * enters


def _pool_window(x, steps, shift):
    s, sh = x, 1
    for _ in range(steps):
        s = s + shift(s, sh)
        sh *= 2
    return s


def _fwd_inproj(x, g1, w_in, stages=()):
    tm = 512

    def body(x_ref, g_ref, w_ref, proj_ref, h_ref):
        @pl.when(pl.program_id(1) == 0)
        def _():
            xv = x_ref[...]
            r = lax.rsqrt(_mean(xv * xv) + NORM_EPS)
            h_ref[...] = ((xv * r) * g_ref[...]).astype(BF)

        proj_ref[...] = jnp.dot(h_ref[...], w_ref[0], preferred_element_type=F32)

    return _call(
        body, name="fwd_inproj", grid=(T // tm, NCHIP),
        in_specs=[pl.BlockSpec((tm, D), lambda i, k: (i, 0)),
                  pl.BlockSpec((1, D), lambda i, k: (0, 0)),
                  pl.BlockSpec((1, D, CW_IN), lambda i, k: (k, 0, 0))],
        out_specs=[pl.BlockSpec((tm, CW_IN), lambda i, k: (i, k)),
                   pl.BlockSpec((tm, D), lambda i, k: (i, 0))],
        out_shape=[jax.ShapeDtypeStruct((T, DIN), F32), jax.ShapeDtypeStruct((T, D), BF)],
        vmem=40, args=[x, g1, w_in], stages=stages)


def _vec_spec():
    return pl.BlockSpec((1, CB), lambda j: (0, j))


def _fwd_lru(proj, conv_w, conv_b, wa, ba, wx, bx, lam, stages=()):
    def body(xp_ref, g_ref, cw_ref, cb_ref, wa_ref, ba_ref, wx_ref, bx_ref, lam_ref, y_ref, h_ref, a_s, b_s):
        xc, _, _, _ = _conv(xp_ref[...], cw_ref[...], cb_ref[...])
        _, _, ii, _, a, mult = _lru_gates(xc, wa_ref[0], ba_ref[...], wx_ref[0], bx_ref[...], lam_ref[...])
        _tile_scan(a, mult * (ii * xc), a_s, b_s, h_ref, reverse=False)
        gel, _ = _gelu_parts(g_ref[...])
        y_ref[...] = (h_ref[...] * gel).astype(BF)

    return _call(
        body, name="fwd_lru", grid=(NG,),
        in_specs=[pl.BlockSpec((T, CB), lambda j: (0, j)),
                  pl.BlockSpec((T, CB), lambda j: (0, NG + j)),
                  pl.BlockSpec((4, CB), lambda j: (0, j)),
                  _vec_spec(),
                  pl.BlockSpec((1, CB, CB), lambda j: (j, 0, 0)), _vec_spec(),
                  pl.BlockSpec((1, CB, CB), lambda j: (j, 0, 0)), _vec_spec(),
                  _vec_spec()],
        out_specs=[pl.BlockSpec((T, CB), lambda j: (0, j)), pl.BlockSpec((T, CB), lambda j: (0, j))],
        out_shape=[jax.ShapeDtypeStruct((T, DR), BF), jax.ShapeDtypeStruct((T, DR), F32)],
        vmem=48, args=[proj, proj, conv_w, conv_b, wa, ba, wx, bx, lam], stages=stages,
        scratch=[pltpu.VMEM((T, CB), F32)] * 2)


def _pool_cnt(w):
    t = lax.broadcasted_iota(jnp.int32, (T, 1), 0)
    return jnp.minimum(t + 1, w).astype(F32)


def _fwd_pool(proj, pool_w, pool_scale):
    def body(xp_ref, pw_ref, sc_ref, y_ref):
        for g, w in enumerate(POOL_WINDOWS):
            cols = slice(g * PG, (g + 1) * PG)
            x = xp_ref[:, cols]
            p = _pool_window(x, g + 1, _sd) / _pool_cnt(w) - x
            y_ref[:, cols] = (_mm(p, pw_ref[g]) * sc_ref[:, cols]).astype(BF)

    return pl.pallas_call(
        body, name="fwd_pool", grid=(1,),
        in_specs=[pl.BlockSpec((T, DP), lambda i: (0, 2 * DR // DP)),
                  pl.BlockSpec((4, PG, PG), lambda i: (0, 0, 0)),
                  pl.BlockSpec((1, DP), lambda i: (0, 0))],
        out_specs=pl.BlockSpec((T, DP), lambda i: (0, 0)),
        out_shape=pltpu.HBM((T, DP), BF),
        compiler_params=_cp(48),
    )(*_hbm(proj, pool_w, pool_scale))


GATE_BLK = 512
GATE_BLK0 = (2 * DR + DP) // GATE_BLK


def _gate_specs(tm):
    return [pl.BlockSpec((tm, GATE_BLK), functools.partial(lambda i, q: (i, GATE_BLK0 + q), q=q))
            for q in range(4)]


def _fwd_merge(x, ylru, ypool, proj, b_gate, g2, g3, w_lru_up, w_pool_up, w_o, stages=()):
    tm = 512

    def body(x_ref, yl_ref, yp_ref, p0, p1, p2, p3, bg_ref, g2_ref, g3_ref, wl_ref, wp_ref, wo_ref,
             x2_ref, h2_ref, m_ref, mrg_ref, bra_ref, brb_ref):
        bra = jnp.dot(yl_ref[...], wl_ref[...], preferred_element_type=F32)
        yp = yp_ref[...]
        brb = jnp.concatenate([jnp.dot(yp, wp_ref[k], preferred_element_type=F32) for k in range(NCHIP)], axis=1)
        bg = bg_ref[...]
        ga = _sigmoid(jnp.concatenate([p0[...], p1[...]], axis=1) + bg[:, :D])
        gb = _sigmoid(jnp.concatenate([p2[...], p3[...]], axis=1) + bg[:, D:])
        mrg = (ga * bra + gb * brb).astype(BF)
        m = jnp.dot(mrg, wo_ref[...], preferred_element_type=F32)
        r2 = lax.rsqrt(_mean(m * m) + NORM_EPS)
        x2 = x_ref[...] + (m * r2) * g2_ref[...]
        r3 = lax.rsqrt(_mean(x2 * x2) + NORM_EPS)
        x2_ref[...] = x2
        h2_ref[...] = ((x2 * r3) * g3_ref[...]).astype(BF)
        m_ref[...] = m
        mrg_ref[...] = mrg
        bra_ref[...] = bra.astype(BF)
        brb_ref[...] = brb.astype(BF)

    row = lambda w: pl.BlockSpec((tm, w), lambda i: (i, 0))
    full2 = lambda a, b: pl.BlockSpec((a, b), lambda i: (0, 0))
    return _call(
        body, name="fwd_merge", grid=(T // tm,),
        in_specs=[row(D), row(DR), row(DP)] + _gate_specs(tm) +
                 [full2(1, 2 * D), full2(1, D), full2(1, D), full2(DR, D),
                  pl.BlockSpec((NCHIP, DP, D // NCHIP), lambda i: (0, 0, 0)), full2(D, D)],
        out_specs=[row(D)] * 6,
        out_shape=[jax.ShapeDtypeStruct((T, D), F32), jax.ShapeDtypeStruct((T, D), BF),
                   jax.ShapeDtypeStruct((T, D), F32), jax.ShapeDtypeStruct((T, D), BF),
                   jax.ShapeDtypeStruct((T, D), BF), jax.ShapeDtypeStruct((T, D), BF)],
        vmem=48, args=[x, ylru, ypool, proj, proj, proj, proj, b_gate, g2, g3, w_lru_up, w_pool_up, w_o],
        stages=stages)


def _fwd_mlp(h2, w_ff1, w_ff2):
    tm = 512
    fk = DF // NCHIP

    def body(h_ref, w1_ref, w2_ref, a1_ref, f_ref):
        h = h_ref[...]
        f = None
        for k in range(NCHIP):
            a1 = jnp.maximum(jnp.dot(h, w1_ref[k], preferred_element_type=F32), 0.0)
            a1_ref[:, k * fk:(k + 1) * fk] = a1.astype(BF)
            part = jnp.dot((a1 * a1).astype(BF), w2_ref[k * fk:(k + 1) * fk, :], preferred_element_type=F32)
            f = part if f is None else f + part
        f_ref[...] = f

    return pl.pallas_call(
        body, name="fwd_mlp", grid=(T // tm,),
        in_specs=[pl.BlockSpec((tm, D), lambda i: (i, 0)),
                  pl.BlockSpec((NCHIP, D, fk), lambda i: (0, 0, 0)),
                  pl.BlockSpec((DF, D), lambda i: (0, 0))],
        out_specs=[pl.BlockSpec((tm, DF), lambda i: (i, 0)), pl.BlockSpec((tm, D), lambda i: (i, 0))],
        out_shape=_hbm_out([jax.ShapeDtypeStruct((T, DF), BF), jax.ShapeDtypeStruct((T, D), F32)]),
        compiler_params=_cp(56),
    )(*_hbm(h2, w_ff1, w_ff2))


def _loss_head(f, x2, target, g4):
    tm = 512

    def body(f_ref, x2_ref, t_ref, g_ref, loss_ref, dy_ref, df_ref, dg_ref):
        first = pl.program_id(0) == 0
        f = f_ref[...]
        g4v = g_ref[...]
        r4 = lax.rsqrt(_mean(f * f) + NORM_EPS)
        fn = f * r4
        e = (x2_ref[...] + fn * g4v) - t_ref[...]
        _acc(loss_ref, jnp.sum(_mean(e * e), axis=0, keepdims=True), first)
        dy = e * (1.0 / D)
        dy_ref[...] = dy
        _acc(dg_ref, _colsum(dy * fn), first)
        dfn = dy * g4v
        df_ref[...] = (r4 * (dfn - fn * _mean(dfn * fn))).astype(BF)

    row = pl.BlockSpec((tm, D), lambda i: (i, 0))
    return pl.pallas_call(
        body, name="loss_head", grid=(T // tm,),
        in_specs=[row, row, row, pl.BlockSpec((1, D), lambda i: (0, 0))],
        out_specs=[pl.BlockSpec((1, 1), lambda i: (0, 0)), row, row, pl.BlockSpec((1, D), lambda i: (0, 0))],
        out_shape=_hbm_out([jax.ShapeDtypeStruct((1, 1), F32), jax.ShapeDtypeStruct((T, D), F32),
                            jax.ShapeDtypeStruct((T, D), BF), jax.ShapeDtypeStruct((1, D), F32)]),
        compiler_params=_cp(48),
    )(*_hbm(f, x2, target, g4))


def _bwd_mlp_x(df, a1, w_ff1, w_ff2):
    tm = 512
    fk = DF // NCHIP

    def body(df_ref, a1_ref, w1_ref, w2_ref, dh_ref, df1_ref):
        df = df_ref[...]
        dh = None
        for k in range(NCHIP):
            cols = slice(k * fk, (k + 1) * fk)
            dact = _mm_nt(df, w2_ref[cols, :])
            df1 = (dact * (2.0 * a1_ref[:, cols].astype(F32))).astype(BF)
            df1_ref[:, cols] = df1
            part = _mm_nt(df1, w1_ref[k])
            dh = part if dh is None else dh + part
        dh_ref[...] = dh

    return pl.pallas_call(
        body, name="bwd_mlp_x", grid=(T // tm,),
        in_specs=[pl.BlockSpec((tm, D), lambda i: (i, 0)),
                  pl.BlockSpec((tm, DF), lambda i: (i, 0)),
                  pl.BlockSpec((NCHIP, D, fk), lambda i: (0, 0, 0)),
                  pl.BlockSpec((DF, D), lambda i: (0, 0))],
        out_specs=[pl.BlockSpec((tm, D), lambda i: (i, 0)), pl.BlockSpec((tm, DF), lambda i: (i, 0))],
        out_shape=_hbm_out([jax.ShapeDtypeStruct((T, D), F32), jax.ShapeDtypeStruct((T, DF), BF)]),
        compiler_params=_cp(56),
    )(*_hbm(df, a1, w_ff1, w_ff2))


def _bwd_mlp_w(df, h2, a1, df1):
    fc = 512
    per = (DF // NCHIP) // fc

    def body(df_ref, h_ref, a1_ref, df1_ref, dw1_ref, dw2_ref):
        a1 = a1_ref[...].astype(F32)
        dw2_ref[...] = _mm_tn((a1 * a1).astype(BF), df_ref[...]).astype(BF)
        dw1_ref[0] = _mm_tn(h_ref[...], df1_ref[...]).astype(BF)

    return pl.pallas_call(
        body, name="bwd_mlp_w", grid=(DF // fc,),
        in_specs=[pl.BlockSpec((T, D), lambda j: (0, 0)),
                  pl.BlockSpec((T, D), lambda j: (0, 0)),
                  pl.BlockSpec((T, fc), lambda j: (0, j)),
                  pl.BlockSpec((T, fc), lambda j: (0, j))],
        out_specs=[pl.BlockSpec((1, D, fc), lambda j: (j // per, 0, j % per)),
                   pl.BlockSpec((fc, D), lambda j: (j, 0))],
        out_shape=_hbm_out([jax.ShapeDtypeStruct((NCHIP, D, DF // NCHIP), BF),
                            jax.ShapeDtypeStruct((DF, D), BF)]),
        compiler_params=_cp(56),
    )(*_hbm(df, h2, a1, df1))


def _bwd_merge(dh2, dy, x2, m, bra, brb, proj, b_gate, g2, g3, w_lru_up, w_pool_up, w_o, stages=()):
    tm = 256
    cpu = D // NCHIP

    def body(dh2_ref, dy_ref, x2_ref, m_ref, bra_ref, brb_ref, p0, p1, p2, p3, bg_ref,
             g2_ref, g3_ref, wl_ref, wp_ref, wo_ref,
             dx_ref, dgt_ref, dyl_ref, dyp_ref, dm_ref, dbra_ref, dbrb_ref, dg2_ref, dg3_ref, dbg_ref):
        first = pl.program_id(0) == 0
        x2 = x2_ref[...]
        r3 = lax.rsqrt(_mean(x2 * x2) + NORM_EPS)
        x2n = x2 * r3
        dh2 = dh2_ref[...]
        t3 = dh2 * g3_ref[...]
        dx2 = dy_ref[...] + r3 * (t3 - x2n * _mean(t3 * x2n))
        dx_ref[...] = dx2
        _acc(dg3_ref, _colsum(dh2 * x2n), first)
        m = m_ref[...]
        r2 = lax.rsqrt(_mean(m * m) + NORM_EPS)
        mn = m * r2
        _acc(dg2_ref, _colsum(dx2 * mn), first)
        dmn = dx2 * g2_ref[...]
        dm = (r2 * (dmn - mn * _mean(dmn * mn))).astype(BF)
        dm_ref[...] = dm
        dmrg = _mm_nt(dm, wo_ref[...])
        bg = bg_ref[...]
        ga = _sigmoid(jnp.concatenate([p0[...], p1[...]], axis=1) + bg[:, :D])
        gb = _sigmoid(jnp.concatenate([p2[...], p3[...]], axis=1) + bg[:, D:])
        dga = dmrg * bra_ref[...].astype(F32) * (ga * (1.0 - ga))
        dgb = dmrg * brb_ref[...].astype(F32) * (gb * (1.0 - gb))
        dgt_ref[:, :D] = dga.astype(BF)
        dgt_ref[:, D:] = dgb.astype(BF)
        _acc(dbg_ref, jnp.concatenate([_colsum(dga), _colsum(dgb)], axis=1), first)
        dbra = (dmrg * ga).astype(BF)
        dbrb = (dmrg * gb).astype(BF)
        dbra_ref[...] = dbra
        dbrb_ref[...] = dbrb
        dyl_ref[...] = _mm_nt(dbra, wl_ref[...])
        dyp = None
        for k in range(NCHIP):
            part = _mm_nt(dbrb[:, k * cpu:(k + 1) * cpu], wp_ref[k])
            dyp = part if dyp is None else dyp + part
        dyp_ref[...] = dyp

    row = lambda w: pl.BlockSpec((tm, w), lambda i: (i, 0))
    full2 = lambda a, b: pl.BlockSpec((a, b), lambda i: (0, 0))
    wp_spec = pl.BlockSpec((NCHIP, DP, cpu), lambda i: (0, 0, 0))
    return _call(
        body, name="bwd_merge", grid=(T // tm,),
        in_specs=[row(D)] * 6 + _gate_specs(tm) +
                 [full2(1, 2 * D), full2(1, D), full2(1, D), full2(DR, D), wp_spec, full2(D, D)],
        out_specs=[row(D), row(2 * D), row(DR), row(DP), row(D), row(D), row(D),
                   full2(1, D), full2(1, D), full2(1, 2 * D)],
        out_shape=[jax.ShapeDtypeStruct((T, D), F32), jax.ShapeDtypeStruct((T, 2 * D), BF),
                   jax.ShapeDtypeStruct((T, DR), F32), jax.ShapeDtypeStruct((T, DP), F32),
                   jax.ShapeDtypeStruct((T, D), BF), jax.ShapeDtypeStruct((T, D), BF),
                   jax.ShapeDtypeStruct((T, D), BF),
                   jax.ShapeDtypeStruct((1, D), F32), jax.ShapeDtypeStruct((1, D), F32),
                   jax.ShapeDtypeStruct((1, 2 * D), F32)],
        vmem=56, args=[dh2, dy, x2, m, bra, brb, proj, proj, proj, proj, b_gate, g2, g3, w_lru_up, w_pool_up, w_o],
        stages=stages)


def _dw_merge(mrg, dm, ylru, dbra, ypool, dbrb, stages=()):
    nb = NCHIP
    rb, pb, cpu = D // nb, DP // nb, D // NCHIP

    def body(mrg_ref, dm_ref, yl_ref, dbra_ref, yp_ref, dbrb_ref, dwo_ref, dwl_ref, dwp_ref):
        dwo_ref[...] = _mm_tn(mrg_ref[...], dm_ref[...]).astype(BF)
        dwl_ref[...] = _mm_tn(yl_ref[...], dbra_ref[...]).astype(BF)
        dwp = _mm_tn(yp_ref[...], dbrb_ref[...]).astype(BF)
        for k in range(NCHIP):
            dwp_ref[k] = dwp[:, k * cpu:(k + 1) * cpu]

    cols = lambda w: pl.BlockSpec((T, w), lambda r: (0, r))
    whole = pl.BlockSpec((T, D), lambda r: (0, 0))
    return _call(
        body, name="dw_merge", grid=(nb,),
        in_specs=[cols(rb), whole, cols(rb), whole, cols(pb), whole],
        out_specs=[pl.BlockSpec((rb, D), lambda r: (r, 0)), pl.BlockSpec((rb, D), lambda r: (r, 0)),
                   pl.BlockSpec((NCHIP, pb, cpu), lambda r: (0, r, 0))],
        out_shape=[jax.ShapeDtypeStruct((D, D), BF), jax.ShapeDtypeStruct((DR, D), BF),
                   jax.ShapeDtypeStruct((NCHIP, DP, cpu), BF)],
        vmem=56, args=[mrg, dm, ylru, dbra, ypool, dbrb], stages=stages)


def _bwd_lru(proj, h, dylru, conv_w, conv_b, wa, ba, wx, bx, lam, stages=()):
    def body(xp_ref, g_ref, h_ref, dy_ref, cw_ref, cb_ref, wa_ref, ba_ref, wx_ref, bx_ref, lam_ref,
             dxp_ref, dg_ref, dcw_ref, dcb_ref, dwa_ref, dba_ref, dwx_ref, dbx_ref, dlam_ref, a_s, b_s, l_s):
        xp = xp_ref[...]
        cw = cw_ref[...]
        lam = lam_ref[...]
        xc, x1, x2, x3 = _conv(xp, cw, cb_ref[...])
        wa, wx = wa_ref[0], wx_ref[0]
        xcb, r, ii, sp, a, mult = _lru_gates(xc, wa, ba_ref[...], wx, bx_ref[...], lam)
        g = g_ref[...]
        gel, dgel = _gelu_parts(g)
        h = h_ref[...]
        dy = dy_ref[...]
        dg_ref[...] = (dy * h * dgel).astype(BF)
        _tile_scan(_su(a, 1, 0.0), dy * gel, a_s, b_s, l_s, reverse=True)
        b = l_s[...]
        da = b * _sd(h, 1, 0.0)
        dmult = b * (ii * xc)
        dii = b * (mult * xc)
        dxc = b * (mult * ii)
        dla = da * a - dmult * ((a * a) / mult)
        dr = dla * ((-LRU_C) * sp)
        dsp = _colsum(dla * ((-LRU_C) * r))
        dlam_ref[...] = -dsp / (1.0 + jnp.exp(lam))
        dzr = dr * (r * (1.0 - r))
        dzi = dii * (ii * (1.0 - ii))
        dzrb, dzib = dzr.astype(BF), dzi.astype(BF)
        dxc = dxc + _mm_nt(dzrb, wa) + _mm_nt(dzib, wx)
        dwa_ref[0] = _mm_tn(xcb, dzrb)
        dwx_ref[0] = _mm_tn(xcb, dzib)
        dba_ref[...] = _colsum(dzr)
        dbx_ref[...] = _colsum(dzi)
        dcb_ref[...] = _colsum(dxc)
        dcw_ref[...] = jnp.concatenate([_colsum(dxc * x3), _colsum(dxc * x2), _colsum(dxc * x1),
                                        _colsum(dxc * xp)], axis=0)
        dxp = cw[3:4] * dxc + cw[2:3] * _su(dxc, 1) + cw[1:2] * _su(dxc, 2) + cw[0:1] * _su(dxc, 3)
        dxp_ref[...] = dxp.astype(BF)

    blk = pl.BlockSpec((T, CB), lambda j: (0, j))
    wsp = pl.BlockSpec((1, CB, CB), lambda j: (j, 0, 0))
    return _call(
        body, name="bwd_lru", grid=(NG,),
        in_specs=[blk, pl.BlockSpec((T, CB), lambda j: (0, NG + j)), blk, blk,
                  pl.BlockSpec((4, CB), lambda j: (0, j)), _vec_spec(), wsp, _vec_spec(), wsp, _vec_spec(),
                  _vec_spec()],
        out_specs=[blk, blk, pl.BlockSpec((4, CB), lambda j: (0, j)), _vec_spec(), wsp, _vec_spec(), wsp,
                   _vec_spec(), _vec_spec()],
        out_shape=[jax.ShapeDtypeStruct((T, DR), BF), jax.ShapeDtypeStruct((T, DR), BF),
                   jax.ShapeDtypeStruct((4, DR), F32), jax.ShapeDtypeStruct((1, DR), F32),
                   jax.ShapeDtypeStruct((NG, CB, CB), F32), jax.ShapeDtypeStruct((1, DR), F32),
                   jax.ShapeDtypeStruct((NG, CB, CB), F32), jax.ShapeDtypeStruct((1, DR), F32),
                   jax.ShapeDtypeStruct((1, DR), F32)],
        vmem=56, args=[proj, proj, h, dylru, conv_w, conv_b, wa, ba, wx, bx, lam], stages=stages,
        scratch=[pltpu.VMEM((T, CB), F32)] * 3)


def _bwd_pool(proj, dypool, pool_w, pool_scale):
    def body(xp_ref, dy_ref, pw_ref, sc_ref, dx_ref, dw_ref, dsc_ref):
        for g, w in enumerate(POOL_WINDOWS):
            cols = slice(g * PG, (g + 1) * PG)
            cnt = _pool_cnt(w)
            x = xp_ref[:, cols]
            pb = (_pool_window(x, g + 1, _sd) / cnt - x).astype(BF)
            wg = pw_ref[g]
            dy = dy_ref[:, cols]
            dsc_ref[:, cols] = _colsum(dy * _mm(pb, wg))
            dyp = (dy * sc_ref[:, cols]).astype(BF)
            dw_ref[g] = _mm_tn(pb, dyp)
            dp = _mm_nt(dyp, wg)
            dx_ref[:, cols] = (_pool_window(dp / cnt, g + 1, _su) - dp).astype(BF)

    return pl.pallas_call(
        body, name="bwd_pool", grid=(1,),
        in_specs=[pl.BlockSpec((T, DP), lambda i: (0, 2 * DR // DP)),
                  pl.BlockSpec((T, DP), lambda i: (0, 0)),
                  pl.BlockSpec((4, PG, PG), lambda i: (0, 0, 0)),
                  pl.BlockSpec((1, DP), lambda i: (0, 0))],
        out_specs=[pl.BlockSpec((T, DP), lambda i: (0, 0)),
                   pl.BlockSpec((4, PG, PG), lambda i: (0, 0, 0)),
                   pl.BlockSpec((1, DP), lambda i: (0, 0))],
        out_shape=_hbm_out([jax.ShapeDtypeStruct((T, DP), BF), jax.ShapeDtypeStruct((4, PG, PG), F32),
                            jax.ShapeDtypeStruct((1, DP), F32)]),
        compiler_params=_cp(48),
    )(*_hbm(proj, dypool, pool_w, pool_scale))


def _bwd_inproj(h1, dproj, w_in, stages=()):
    def body(h_ref, dp_ref, w_ref, dw_ref, dh_ref):
        dp = dp_ref[...]
        dw_ref[0] = _mm_tn(h_ref[...], dp).astype(BF)
        _acc(dh_ref, _mm_nt(dp, w_ref[0]), pl.program_id(0) == 0)

    return _call(
        body, name="bwd_inproj", grid=(NCHIP,),
        in_specs=[pl.BlockSpec((T, D), lambda k: (0, 0)),
                  pl.BlockSpec((T, CW_IN), lambda k: (0, k)),
                  pl.BlockSpec((1, D, CW_IN), lambda k: (k, 0, 0))],
        out_specs=[pl.BlockSpec((1, D, CW_IN), lambda k: (k, 0, 0)), pl.BlockSpec((T, D), lambda k: (0, 0))],
        out_shape=[jax.ShapeDtypeStruct((NCHIP, D, CW_IN), BF), jax.ShapeDtypeStruct((T, D), F32)],
        vmem=56, args=[h1, dproj, w_in], stages=stages)


def _bwd_prenorm(x, dh1, dxres, g1, stages=()):
    tm = 512

    def body(x_ref, dh_ref, dr_ref, g_ref, dx_ref, dg_ref):
        xv = x_ref[...]
        r = lax.rsqrt(_mean(xv * xv) + NORM_EPS)
        xn = xv * r
        dh = dh_ref[...]
        t = dh * g_ref[...]
        dx_ref[...] = dr_ref[...] + r * (t - xn * _mean(t * xn))
        _acc(dg_ref, _colsum(dh * xn), pl.program_id(0) == 0)

    row = pl.BlockSpec((tm, D), lambda i: (i, 0))
    vec = pl.BlockSpec((1, D), lambda i: (0, 0))
    return _call(
        body, name="bwd_prenorm", grid=(T // tm,),
        in_specs=[row, row, row, vec], out_specs=[row, vec],
        out_shape=[jax.ShapeDtypeStruct((T, D), F32), jax.ShapeDtypeStruct((1, D), F32)],
        vmem=48, args=[x, dh1, dxres, g1], stages=stages)


def _place():
    x, y, c = lax.axis_index("x"), lax.axis_index("y"), lax.axis_index("c")
    chips = [(1 - x, y), (x, 1 - y), (1 - x, 1 - y)]
    return x, y, c, chips


def _rcopy(src, dst, ssem, rsem, dev):
    return pltpu.make_async_remote_copy(src_ref=src, dst_ref=dst, send_sem=ssem, recv_sem=rsem,
                                        device_id=dev, device_id_type=MESH_ID)


def _sds(a):
    return jax.ShapeDtypeStruct(a.shape, a.dtype)


def _sem2(n, m):
    return [pltpu.SemaphoreType.DMA((n * m,)), pltpu.SemaphoreType.DMA((n * m,))]


ALL = (0, 1, 1)


def _piece(ref, k, half, part):
    hr = ref.shape[1] // 2
    r0, r1 = hr * part[0] // part[2], hr * part[1] // part[2]
    return ref.at[k, pl.ds(half * hr + r0, r1 - r0), :]


def _gather(fulls, ici=(), d2d=()):
    n = len(fulls)
    ici, d2d = list(ici), list(d2d)
    pieces = [("ici", i, part) for i, part in ici] + [("d2d", i, part) for i, part in d2d]

    def copies(outs, sems):
        x, y, c, chips = _place()
        me = 2 * x + y
        sib = (x, y, 1 - c)
        send, recv = [], []
        for q, (kind, i, part) in enumerate(pieces):
            for j, chip in enumerate(chips):
                k, s = 2 * chip[0] + chip[1], 3 * q + j
                if kind == "ici":
                    mine, theirs, dev = _piece(outs[i], me, c, part), _piece(outs[i], k, c, part), (*chip, c)
                else:
                    mine, theirs, dev = _piece(outs[i], k, c, part), _piece(outs[i], k, 1 - c, part), sib
                send.append(_rcopy(mine, mine, sems[0].at[s], sems[1].at[s], dev))
                recv.append(_rcopy(theirs, theirs, sems[0].at[s], sems[1].at[s], dev))
        return send, recv

    def start(ins, outs, sems):
        for cp in copies(outs, sems)[0]:
            cp.start()

    def finish(ins, outs, sems):
        send, recv = copies(outs, sems)
        for cp in recv:
            cp.wait_recv()
        for cp in send:
            cp.wait_send()

    sems = [pltpu.SemaphoreType.DMA((3 * len(pieces),)), pltpu.SemaphoreType.DMA((3 * len(pieces),))]
    return _Stage(fulls, [_sds(f) for f in fulls], {i: i for i in range(n)}, sems, start, finish)


def _gather_whole(v):
    def copies(ins, outs, sems):
        x, y, c, chips = _place()
        me = 2 * x + y
        send = [_rcopy(ins[0], outs[0].at[me], sems[0].at[j], sems[1].at[j], (*chip, c))
                for j, chip in enumerate(chips)]
        recv = [_rcopy(ins[0], outs[0].at[2 * chip[0] + chip[1]], sems[0].at[j], sems[1].at[j], (*chip, c))
                for j, chip in enumerate(chips)]
        return send, recv

    def start(ins, outs, sems):
        for cp in copies(ins, outs, sems)[0]:
            cp.start()

    def finish(ins, outs, sems):
        send, recv = copies(ins, outs, sems)
        for cp in recv:
            cp.wait_recv()
        for cp in send:
            cp.wait_send()

    return _Stage([v], [jax.ShapeDtypeStruct((NCHIP,) + v.shape, v.dtype)], {},
                  [pltpu.SemaphoreType.DMA((3,)), pltpu.SemaphoreType.DMA((3,))], start, finish)


def _to_sibling(srcs):
    n = len(srcs)

    def copies(ins, outs, sems):
        x, y, c, _ = _place()
        sib = (x, y, 1 - c)
        return [_rcopy(ins[i].at[:, 1 - c] if srcs[i].ndim == 4 else ins[i], outs[i], sems[0].at[i], sems[1].at[i], sib)
                for i in range(n)]

    def start(ins, outs, sems):
        for cp in copies(ins, outs, sems):
            cp.start()

    def finish(ins, outs, sems):
        for cp in copies(ins, outs, sems):
            cp.wait()

    shapes = [jax.ShapeDtypeStruct((NCHIP,) + s.shape[2:] if s.ndim == 4 else s.shape, s.dtype) for s in srcs]
    return _Stage(srcs, shapes, {}, [pltpu.SemaphoreType.DMA((n,)), pltpu.SemaphoreType.DMA((n,))], start, finish)


def _to_chips(srcs, parts=None, lands=None):
    n = len(srcs)
    parts = [ALL] * n if parts is None else parts
    lands = [None] * n if lands is None else lands
    given = [i for i in range(n) if lands[i] is not None]

    def rows(ref, i):
        hr = srcs[i].shape[1]
        r0, r1 = hr * parts[i][0] // parts[i][2], hr * parts[i][1] // parts[i][2]
        return ref.at[pl.ds(r0, r1 - r0), :]

    def copies(ins, outs, sems):
        x, y, c, chips = _place()
        me = 2 * x + y
        return [_rcopy(rows(ins[i].at[2 * chip[0] + chip[1]] if srcs[i].shape[0] == NCHIP else ins[i].at[c], i),
                       rows(outs[i].at[me], i), sems[0].at[3 * i + j], sems[1].at[3 * i + j], (*chip, c))
                for i in range(n) for j, chip in enumerate(chips)]

    def start(ins, outs, sems):
        for cp in copies(ins, outs, sems):
            cp.start()

    def finish(ins, outs, sems):
        for cp in copies(ins, outs, sems):
            cp.wait()

    shapes = [jax.ShapeDtypeStruct((NCHIP,) + s.shape[1:], s.dtype) for s in srcs]
    alias = {n + q: i for q, i in enumerate(given)}
    return _Stage(list(srcs) + [lands[i] for i in given], shapes, alias, _sem2(n, 3), start, finish)


HBM_REF = pl.BlockSpec(memory_space=pltpu.HBM)
SEM_REF = pl.BlockSpec(memory_space=pltpu.SEMAPHORE)
DATAFLOW = pltpu.SideEffectType.DATAFLOW_SIDE_EFFECTING


def _after(x):
    return _Stage([x], [], {}, [], lambda *a: None, lambda *a: None)


class _Flight:
    def __init__(self, stage, sems, bufs):
        self.stage, self.sems, self.bufs = stage, list(sems), list(bufs)

    def landed(self):
        st, n = self.stage, len(self.stage.operands)
        fresh = [j for j in range(len(st.out_shape)) if j not in st.alias.values()]
        back = {v: k for k, v in st.alias.items()}
        return [self.bufs[back[j]] if j in back else self.bufs[n + fresh.index(j)] for j in range(len(st.out_shape))]


def _split_call(name, finish=(), start=(), after=None):
    bufs, stage_bufs = [], []

    def slot(a):
        for i, b in enumerate(bufs):
            if b is a:
                return i
        bufs.append(a)
        return len(bufs) - 1

    fin_slots = [[slot(b) for b in fl.bufs] for fl in finish]
    for st in start:
        fresh = [lax.empty(o.shape, o.dtype) for j, o in enumerate(st.out_shape) if j not in st.alias.values()]
        stage_bufs.append([slot(a) for a in list(st.operands) + fresh])
    old_sems = [s for fl in finish for s in fl.sems]
    new_sems = [s for st in start for s in st.sems]
    nb, no, nn = len(bufs), len(old_sems), len(new_sems)

    def refs_of(st, slots, buf_refs):
        n = len(st.operands)
        ins = [buf_refs[i] for i in slots[:n]]
        fresh = [j for j in range(len(st.out_shape)) if j not in st.alias.values()]
        back = {v: k for k, v in st.alias.items()}
        outs = [ins[back[j]] if j in back else buf_refs[slots[n + fresh.index(j)]] for j in range(len(st.out_shape))]
        return ins, outs

    def body(*refs):
        buf_refs, sem_in = refs[:nb], refs[nb:nb + no]
        sem_out = refs[nb + no + (after is not None):][:nn]
        token = refs[-1]
        pos = 0
        for fl, slots in zip(finish, fin_slots):
            ins, outs = refs_of(fl.stage, slots, buf_refs)
            fl.stage.finish(ins, outs, sem_in[pos:pos + len(fl.sems)])
            pos += len(fl.sems)
        pos = 0
        for st, slots in zip(start, stage_bufs):
            ins, outs = refs_of(st, slots, buf_refs)
            st.start(ins, outs, sem_out[pos:pos + len(st.sems)])
            pos += len(st.sems)
        token[...] = jnp.zeros_like(token)

    res = pl.pallas_call(
        body, name=name,
        out_shape=tuple(new_sems) + tuple(pltpu.HBM(b.shape, b.dtype) for b in bufs) +
                  (jax.ShapeDtypeStruct((8, LANE), F32),),
        in_specs=(HBM_REF,) * nb + (SEM_REF,) * no + ((pl.BlockSpec(memory_space=pl.ANY),) if after is not None else ()),
        out_specs=(SEM_REF,) * nn + (HBM_REF,) * nb + (pl.BlockSpec(memory_space=pltpu.VMEM),),
        input_output_aliases={i: nn + i for i in range(nb)},
        compiler_params=pltpu.CompilerParams(has_side_effects=DATAFLOW),
    )(*_hbm(*bufs), *old_sems, *([after] if after is not None else []))
    sems, thru, token = res[:nn], res[nn:nn + nb], res[-1]
    for fl, slots in zip(finish, fin_slots):
        fl.bufs = [thru[i] for i in slots]
    flights, pos = [], 0
    for st, slots in zip(start, stage_bufs):
        flights.append(_Flight(st, sems[pos:pos + len(st.sems)], [thru[i] for i in slots]))
        pos += len(st.sems)
    return flights, token


def _last_copies(p_ref, land_ref, ssem, rsem):
    x, y, c, chips = _place()
    me = 2 * x + y
    send = [_rcopy(p_ref.at[2 * chip[0] + chip[1]], land_ref.at[me], ssem.at[j], rsem.at[j], (*chip, c))
            for j, chip in enumerate(chips)]
    recv = [_rcopy(p_ref.at[2 * chip[0] + chip[1]], land_ref.at[2 * chip[0] + chip[1]], ssem.at[j], rsem.at[j],
                   (*chip, c)) for j, chip in enumerate(chips)]
    return send, recv


def _chips_start(p):
    def body(p_ref, land_ref, ssem, rsem, p_thru, land_thru, token):
        for cp in _last_copies(p_ref, land_ref, ssem, rsem)[0]:
            cp.start()
        token[...] = jnp.zeros_like(token)

    return pl.pallas_call(
        body, name="reduce_last_start",
        out_shape=(pltpu.SemaphoreType.DMA((3,)), pltpu.SemaphoreType.DMA((3,)), pltpu.HBM(p.shape, p.dtype),
                   pltpu.HBM(p.shape, p.dtype), jax.ShapeDtypeStruct((8, LANE), F32)),
        in_specs=(HBM_REF, HBM_REF),
        out_specs=(SEM_REF, SEM_REF, HBM_REF, HBM_REF, pl.BlockSpec(memory_space=pltpu.VMEM)),
        input_output_aliases={0: 2, 1: 3},
        compiler_params=pltpu.CompilerParams(has_side_effects=DATAFLOW),
    )(*_hbm(p, lax.empty(p.shape, p.dtype)))


def _chips_wait(ssem, rsem, p_thru, land_thru, after):
    def body(p_ref, land_ref, ssem, rsem, after_ref, p_dead, got_ref):
        send, recv = _last_copies(p_ref, land_ref, ssem, rsem)
        for cp in send:
            cp.wait_send()
        for cp in recv:
            cp.wait_recv()

    return pl.pallas_call(
        body, name="reduce_last_wait",
        out_shape=(pltpu.HBM(p_thru.shape, p_thru.dtype), pltpu.HBM(land_thru.shape, land_thru.dtype)),
        in_specs=(HBM_REF, HBM_REF, SEM_REF, SEM_REF, pl.BlockSpec(memory_space=pl.ANY)),
        out_specs=(HBM_REF, HBM_REF), input_output_aliases={0: 0, 1: 1},
        compiler_params=pltpu.CompilerParams(has_side_effects=DATAFLOW),
    )(p_thru, land_thru, ssem, rsem, after)


def _share(pairs):
    n = len(pairs)

    def start(ins, outs, sems):
        x, y, c, _ = _place()
        for i in range(n):
            _rcopy(outs[i].at[c], outs[i].at[c], sems[0].at[i], sems[1].at[i], (x, y, 1 - c)).start()

    def finish(ins, outs, sems):
        x, y, c, _ = _place()
        for i in range(n):
            _rcopy(outs[i].at[c], outs[i].at[c], sems[0].at[i], sems[1].at[i], (x, y, 1 - c)).wait_send()
            _rcopy(outs[i].at[1 - c], outs[i].at[1 - c], sems[0].at[i], sems[1].at[i], (x, y, 1 - c)).wait_recv()

    return _Stage(pairs, [_sds(p) for p in pairs], {i: i for i in range(n)},
                  [pltpu.SemaphoreType.DMA((n,)), pltpu.SemaphoreType.DMA((n,))], start, finish)


def _row_block(rows, cols, itemsize=4, target=2 * MIB):
    br = rows
    while br * cols * itemsize > target and br % 32 == 0:
        br //= 2
    return br


def _cast_place(w, chip_idx, name):
    rows, cols = w.shape
    br = _row_block(rows, cols)

    def body(k_ref, w_ref, o_ref):
        o_ref[0] = w_ref[...].astype(BF)

    return _call(
        body, name=name, grid=(rows // br,), prefetch=chip_idx,
        in_specs=[pl.BlockSpec((br, cols), lambda r, k: (r, 0))],
        out_specs=[pl.BlockSpec((1, br, cols), lambda r, k: (k[0], r, 0))],
        out_shape=[jax.ShapeDtypeStruct((NCHIP, rows, cols), BF)], vmem=32, args=[w])[0][0]


def _cast_place_multi(ws, chip_idx, stages=()):
    br = 128
    nblk = [a.shape[0] // br for a in ws]
    starts = [sum(nblk[:i]) for i in range(len(ws))]

    def body(k_ref, *refs):
        r = pl.program_id(0)
        for i in range(len(ws)):
            @pl.when(jnp.logical_and(r >= starts[i], r < starts[i] + nblk[i]))
            def _(i=i):
                refs[len(ws) + i][0] = refs[i][...].astype(BF)

    def at(i):
        return functools.partial(lambda r, s, nb: jnp.clip(r - s, 0, nb - 1), s=starts[i], nb=nblk[i])

    outs, landed = _call(
        body, name="cast_rest", grid=(sum(nblk),), prefetch=chip_idx,
        in_specs=[pl.BlockSpec((br, a.shape[1]), functools.partial(lambda r, k, f: (f(r), 0), f=at(i)))
                  for i, a in enumerate(ws)],
        out_specs=[pl.BlockSpec((1, br, a.shape[1]), functools.partial(lambda r, k, f: (k[0], f(r), 0), f=at(i)))
                   for i, a in enumerate(ws)],
        out_shape=[jax.ShapeDtypeStruct((NCHIP,) + a.shape, BF) for a in ws], vmem=32, args=list(ws), stages=stages)
    return outs, landed


def _add_sibling(g, land, cidx, name, stages=()):
    _, _, hr, cols = g.shape
    br = _row_block(hr, cols)

    def body(c_ref, g_ref, l_ref, o_ref):
        o_ref[...] = (g_ref[0, 0].astype(F32) + l_ref[0].astype(F32)).astype(BF)[None]

    outs, st = _call(
        body, name=name, grid=(NCHIP, hr // br), prefetch=cidx,
        in_specs=[pl.BlockSpec((1, 1, br, cols), lambda k, r, c: (k, c[0], r, 0)),
                  pl.BlockSpec((1, br, cols), lambda k, r, c: (k, r, 0))],
        out_specs=[pl.BlockSpec((1, br, cols), lambda k, r, c: (k, r, 0))],
        out_shape=[jax.ShapeDtypeStruct((NCHIP, hr, cols), BF)], vmem=32, args=[g, land], stages=stages)
    return outs[0], st


def _add_pair(a, b, name):
    rows, cols = a.shape

    def body(a_ref, b_ref, o_ref):
        o_ref[...] = a_ref[...] + b_ref[...]

    spec = pl.BlockSpec((rows, cols), lambda r: (0, 0))
    return _call(body, name=name, grid=(1,), in_specs=[spec, spec], out_specs=[spec], out_shape=[_sds(a)],
                 vmem=32, args=[a, b])[0][0]


def _add_chips(own, land, idx, name, stages=None):
    _, hr, cols = land.shape
    br = _row_block(hr, cols)

    def body(s_ref, a_ref, b_ref, c_ref, d_ref, o_ref):
        o_ref[...] = (a_ref[...].astype(F32) + b_ref[...].astype(F32)) + (c_ref[...].astype(F32) +
                                                                           d_ref[...].astype(F32))

    spec = lambda q: pl.BlockSpec((1, br, cols), functools.partial(lambda r, s, q: (s[q], r, 0), q=q))
    outs, landed = _call(
        body, name=name, grid=(hr // br,), prefetch=idx,
        in_specs=[spec(0), spec(1), spec(2), spec(3)], out_specs=[spec(4)],
        out_shape=[jax.ShapeDtypeStruct((2, hr, cols), F32)], vmem=48, args=[own, land, land, land],
        stages=stages or ())
    return outs[0] if stages is None else (outs[0], landed)


def _add_chips_multi(owns, lands, idx, name, stages=()):
    n = len(owns)
    brs = [_row_block(l.shape[1], l.shape[2]) for l in lands]
    nblk = [l.shape[1] // b for l, b in zip(lands, brs)]
    starts = [sum(nblk[:i]) for i in range(n)]

    def body(s_ref, *refs):
        r = pl.program_id(0)
        for i in range(n):
            a_ref, b_ref, c_ref, d_ref = refs[4 * i:4 * i + 4]
            o_ref = refs[4 * n + i]

            @pl.when(jnp.logical_and(r >= starts[i], r < starts[i] + nblk[i]))
            def _():
                o_ref[...] = (a_ref[...].astype(F32) + b_ref[...].astype(F32)) + (c_ref[...].astype(F32) +
                                                                                   d_ref[...].astype(F32))

    def spec(i, q):
        return pl.BlockSpec((1, brs[i], lands[i].shape[2]), functools.partial(
            lambda r, s, q, st, nb: (s[q], jnp.clip(r - st, 0, nb - 1), 0), q=q, st=starts[i], nb=nblk[i]))

    outs, landed = _call(
        body, name=name, grid=(sum(nblk),), prefetch=idx,
        in_specs=[spec(i, q) for i in range(n) for q in range(4)], out_specs=[spec(i, 4) for i in range(n)],
        out_shape=[jax.ShapeDtypeStruct((2,) + l.shape[1:], F32) for l in lands], vmem=48,
        args=[a for i in range(n) for a in (owns[i], lands[i], lands[i], lands[i])], stages=stages)
    return outs, landed


def _adamw_math(w, g, m, v):
    mn = ADAM_B1 * m + (1.0 - ADAM_B1) * g
    vn = ADAM_B2 * v + (1.0 - ADAM_B2) * (g * g)
    m_hat = mn / (1.0 - ADAM_B1 ** ADAM_STEP)
    v_hat = vn / (1.0 - ADAM_B2 ** ADAM_STEP)
    return -ADAM_LR * (m_hat / (jnp.sqrt(v_hat) + ADAM_EPS) + ADAM_WD * w), mn, vn


def _adamw(w, g, m, v, name, stages=()):
    rows, cols = w.shape
    br = _row_block(rows, cols)

    def body(w_ref, g_ref, m_ref, v_ref, go_ref, d_ref, mo_ref, vo_ref):
        gv = g_ref[...]
        go_ref[...] = gv
        d_ref[...], mo_ref[...], vo_ref[...] = _adamw_math(w_ref[...], gv, m_ref[...], v_ref[...])

    spec = pl.BlockSpec((br, cols), lambda r: (r, 0))
    return _call(body, name=name, grid=(rows // br,), in_specs=[spec] * 4, out_specs=[spec] * 4,
                 out_shape=[_sds(w)] * 4, vmem=56, args=[w, g, m, v], stages=stages)


def _adamw_multi(names, w, g, m, v, stages=()):
    cols = w[names[0]].shape[1]
    br = 128
    nblk = [w[n].shape[0] // br for n in names]
    starts = [sum(nblk[:i]) for i in range(len(names))]

    def body(*refs):
        r = pl.program_id(0)
        for i in range(len(names)):
            w_ref, g_ref, m_ref, v_ref = refs[4 * i:4 * i + 4]
            go_ref, d_ref, mo_ref, vo_ref = refs[4 * len(names) + 4 * i:4 * len(names) + 4 * i + 4]

            @pl.when(jnp.logical_and(r >= starts[i], r < starts[i] + nblk[i]))
            def _():
                gv = g_ref[...]
                go_ref[...] = gv
                d_ref[...], mo_ref[...], vo_ref[...] = _adamw_math(w_ref[...], gv, m_ref[...], v_ref[...])

    def spec(i):
        return pl.BlockSpec((br, cols), functools.partial(
            lambda r, s, nb: (jnp.clip(r - s, 0, nb - 1), 0), s=starts[i], nb=nblk[i]))

    outs, landed = _call(
        body, name="adamw_" + "_".join(names), grid=(sum(nblk),),
        in_specs=[spec(i) for i in range(len(names)) for _ in range(4)],
        out_specs=[spec(i) for i in range(len(names)) for _ in range(4)],
        out_shape=[_sds(w[n]) for n in names for _ in range(4)], vmem=56,
        args=[a[n] for n in names for a in (w, g, m, v)], stages=stages)
    return {n: outs[4 * i:4 * i + 4] for i, n in enumerate(names)}, landed


def _to_everyone(v):
    deltas = [(a, b, e) for a in (0, 1) for b in (0, 1) for e in (0, 1)][1:]

    def copies(ins, outs, sems):
        x, y, c, _ = _place()
        me = 4 * x + 2 * y + c
        flip = lambda p, f: 1 - p if f else p
        return [_rcopy(ins[0], outs[0].at[me], sems[0].at[q], sems[1].at[q], (flip(x, a), flip(y, b), flip(c, e)))
                for q, (a, b, e) in enumerate(deltas)]

    def start(ins, outs, sems):
        for cp in copies(ins, outs, sems):
            cp.start()

    def finish(ins, outs, sems):
        for cp in copies(ins, outs, sems):
            cp.wait()

    n = len(deltas)
    return _Stage([v], [jax.ShapeDtypeStruct((2 * NCHIP,) + v.shape, v.dtype)], {},
                  [pltpu.SemaphoreType.DMA((n,)), pltpu.SemaphoreType.DMA((n,))], start, finish)


SMALL_AT = {"norm_mix_pre": (0, 1, D), "norm_mix_post": (1, 1, D), "norm_mlp_pre": (2, 1, D),
            "norm_mlp_post": (3, 1, D), "b_gate": (4, 2, D), "conv_b": (6, 1, D), "lru_b_a": (7, 1, D),
            "lru_b_x": (8, 1, D), "lru_lambda": (9, 1, D), "pool_scale": (10, 1, DP)}
SMALL_SEPARATE = ["conv_w", "lru_w_a", "lru_w_x", "pool_w"]


def _adamw_small(small_sum, first_all, sep_grads, w, m, v):
    packed, sep = list(SMALL_AT), list(SMALL_SEPARATE)
    names = packed + sep

    def body(*refs):
        s_ref, a_ref, refs = refs[0], refs[1], refs[2:]
        g_sep, refs = refs[:len(sep)], refs[len(sep):]
        nn = len(names)
        w_r, m_r, v_r, refs = refs[:nn], refs[nn:2 * nn], refs[2 * nn:3 * nn], refs[3 * nn:]
        g_out, refs = refs[:len(packed)], refs[len(packed):]
        d_o, m_o, v_o = refs[:nn], refs[nn:2 * nn], refs[2 * nn:3 * nn]
        for i, n in enumerate(names):
            if i == 0:
                g = a_ref[0:1, :]
                for q in range(1, 2 * NCHIP):
                    g = g + a_ref[q:q + 1, :]
                g_out[i][...] = g
            elif n in SMALL_AT:
                r0, nr, nc = SMALL_AT[n]
                g = jnp.concatenate([s_ref[r0 + q:r0 + q + 1, :nc] for q in range(nr)], axis=1)
                g_out[i][...] = g
            else:
                g = g_sep[i - len(packed)][...]
            d_o[i][...], m_o[i][...], v_o[i][...] = _adamw_math(w_r[i][...], g, m_r[i][...], v_r[i][...])

    ws = [w[n] for n in names]
    res = pl.pallas_call(
        body, name="adamw_small",
        out_shape=[_sds(w[n]) for n in packed] + [_sds(a) for a in ws] * 3,
        compiler_params=_cp(32),
    )(*_hbm(small_sum, first_all, *sep_grads, *ws, *[m[n] for n in names], *[v[n] for n in names]))
    nn, npk = len(names), len(packed)
    grad = dict(zip(packed, res[:npk]))
    delta = dict(zip(names, res[npk:npk + nn]))
    new_m = dict(zip(names, res[npk + nn:npk + 2 * nn]))
    new_v = dict(zip(names, res[npk + 2 * nn:]))
    return grad, delta, new_m, new_v


W_NAMES = ["norm_mix_pre", "norm_mix_post", "norm_mlp_pre", "norm_mlp_post", "w_in", "b_gate", "conv_w", "conv_b",
           "lru_w_a", "lru_b_a", "lru_w_x", "lru_b_x", "lru_lambda", "pool_w", "pool_scale", "w_lru_up",
           "w_pool_up", "w_o", "w_ff1", "w_ff2"]
BIG = ["w_in", "w_lru_up", "w_pool_up", "w_o", "w_ff1", "w_ff2"]


def _block_diag(w):
    hd = w.shape[-1]
    per = CB // hd
    w4 = w.reshape(NG, per, hd, hd)
    eye = jnp.eye(per, dtype=w.dtype)
    return jnp.einsum("gpij,pq->gpiqj", w4, eye).reshape(NG, CB, CB)


def _block_diag_extract(d, hd):
    per = CB // hd
    d5 = d.reshape(NG, per, hd, per, hd)
    return jnp.stack([d5[:, p, :, p, :] for p in range(per)], axis=1).reshape(NG * per, hd, hd)


def _halves(g):
    return g.reshape(NCHIP, 2, g.size // (g.shape[-1] * 2 * NCHIP), g.shape[-1])


def kernel(x, norm_mix_pre, norm_mix_post, norm_mlp_pre, norm_mlp_post, w_in, b_gate, conv_w, conv_b, lru_w_a, lru_b_a, lru_w_x, lru_b_x, lru_lambda, pool_w, pool_scale, w_lru_up, w_pool_up, w_o, w_ff1, w_ff2, loss_target, m_norm_mix_pre, m_norm_mix_post, m_norm_mlp_pre, m_norm_mlp_post, m_w_in, m_b_gate, m_conv_w, m_conv_b, m_lru_w_a, m_lru_b_a, m_lru_w_x, m_lru_b_x, m_lru_lambda, m_pool_w, m_pool_scale, m_w_lru_up, m_w_pool_up, m_w_o, m_w_ff1, m_w_ff2, v_norm_mix_pre, v_norm_mix_post, v_norm_mlp_pre, v_norm_mlp_post, v_w_in, v_b_gate, v_conv_w, v_conv_b, v_lru_w_a, v_lru_b_a, v_lru_w_x, v_lru_b_x, v_lru_lambda, v_pool_w, v_pool_scale, v_w_lru_up, v_w_pool_up, v_w_o, v_w_ff1, v_w_ff2):
    args = dict(locals())
    two_d = lambda a: a.reshape(-1, a.shape[-1])
    w = {n: two_d(args[n]) for n in W_NAMES}
    mom = {n: two_d(args["m_" + n]) for n in W_NAMES}
    var = {n: two_d(args["v_" + n]) for n in W_NAMES}
    i32 = lambda val: jnp.asarray(val, jnp.int32)
    chip = i32(2 * lax.axis_index("x") + lax.axis_index("y"))
    core = i32(lax.axis_index("c"))
    cidx = core.reshape(1)
    zero = i32(0)
    hd = lru_w_a.shape[-1]
    xs, target = x[0], loss_target[0]
    g1, g2, g3, g4 = norm_mix_pre, norm_mix_post, norm_mlp_pre, norm_mlp_post

    mix = ["w_lru_up", "w_pool_up", "w_o"]
    full = {"w_in": _cast_place(w["w_in"], chip.reshape(1), "cast_w_in")}
    (fl_in, fl_conv), first = _split_call("gather_start_first", start=[
        _gather([full["w_in"]], ici=[(0, ALL)]), _gather_whole(w["conv_w"])])
    casts, _ = _cast_place_multi([w[n] for n in BIG[1:]], chip.reshape(1), stages=[_after(first)])
    full.update(zip(BIG[1:], casts))
    (fl_mix, fl_ff1, fl_ff2), started = _split_call("gather_start_rest", start=[
        _gather([full[n] for n in mix], ici=[(0, ALL), (1, ALL), (2, ALL)]),
        _gather([full["w_ff1"]], ici=[(0, ALL)]), _gather([full["w_ff2"]], ici=[(0, ALL)])])
    wa = _block_diag(lru_w_a[0]).astype(BF)
    wx = _block_diag(lru_w_x[0]).astype(BF)
    pw = pool_w[0].astype(BF)

    def to_sibling(name, flight, after=None):
        (fl,), passed = _split_call(name + "_pass", finish=[flight], after=after,
                                    start=[_gather(flight.landed(), d2d=[(i, ALL) for i in range(len(flight.bufs))])])
        passed_on.append(passed)
        return fl

    passed_on = []

    def arrived(name, flight, after=None):
        _split_call(name + "_done", finish=[flight], after=after)
        return flight.landed()

    fl_in = to_sibling("gather_w_in", fl_in, after=started)
    w_in_f, = arrived("gather_w_in", fl_in)
    conv_all, = arrived("gather_conv", fl_conv)
    full["w_in"] = w_in_f
    conv_all = lax.dynamic_update_slice(conv_all, w["conv_w"][None], (chip, zero, zero))
    conv_full = jnp.transpose(conv_all, (1, 0, 2)).reshape(4, DR)
    (proj, h1), _ = _fwd_inproj(xs, g1, w_in_f)
    fl_mix = to_sibling("gather_mix", fl_mix, after=h1)
    (ylru, hs), _ = _fwd_lru(proj, conv_full, conv_b, wa, lru_b_a, wx, lru_b_x, lru_lambda,
                             stages=[_after(passed_on[-1])])
    got = arrived("gather_mix", fl_mix, after=ylru)
    fl_ff1 = to_sibling("gather_ff1", fl_ff1, after=ylru)
    w_lru_up_f, w_pool_up_f, w_o_f = got[0].reshape(DR, D), got[1], got[2].reshape(D, D)
    ypool = _fwd_pool(proj, pw, pool_scale)
    fl_ff2 = to_sibling("gather_ff2", fl_ff2, after=ypool)
    (x2, h2, m, mrg, bra, brb), _ = _fwd_merge(xs, ylru, ypool, proj, b_gate, g2, g3, w_lru_up_f, w_pool_up_f, w_o_f,
                                               stages=[_after(passed_on[-1])])
    ff1, = arrived("gather_ff1", fl_ff1, after=h2)
    ff2, = arrived("gather_ff2", fl_ff2)
    ff2 = ff2.reshape(DF, D)
    a1, f = _fwd_mlp(h2, ff1, ff2)
    lossp, dy, df, dg4 = _loss_head(f, x2, target, g4)

    idx_big = jnp.stack([chip, (chip + 1) % NCHIP, (chip + 2) % NCHIP, (chip + 3) % NCHIP, core])
    dh2, df1 = _bwd_mlp_x(df, a1, ff1, ff2)
    dw_ff1, dw_ff2 = _bwd_mlp_w(df, h2, a1, df1)
    g_ff = [_halves(dw_ff1), _halves(dw_ff2)]
    (dxres, dgates, dylru, dypool, dm, dbra, dbrb, dg2, dg3, dbg), (l_ff,) = _bwd_merge(
        dh2, dy, x2, m, bra, brb, proj, b_gate, g2, g3, w_lru_up_f, w_pool_up_f, w_o_f, stages=[_to_sibling(g_ff)])
    p_ff = [_add_sibling(g, l, cidx, "add_sibling_" + n)[0] for g, l, n in zip(g_ff, l_ff, ["w_ff1", "w_ff2"])]
    (fl_ff,), sent_ff = _split_call("reduce_ff_start", start=[_to_chips(p_ff)])
    (dw_o, dw_lru_up, dw_pool_up), _ = _dw_merge(mrg, dm, ylru, dbra, ypool, dbrb, stages=[_after(sent_ff)])
    g_mix = [_halves(dw_lru_up), _halves(dw_pool_up), _halves(dw_o)]
    (dxp, dgl, dcw, dcb, dwa, dba, dwx, dbx, dlam), (l_mix,) = _bwd_lru(
        proj, hs, dylru, conv_full, conv_b, wa, lru_b_a, wx, lru_b_x, lru_lambda, stages=[_to_sibling(g_mix)])
    p_mix = [_add_sibling(g, l, cidx, "add_sibling_" + n)[0] for g, l, n in zip(g_mix, l_mix, mix)]
    dxpool, dpw, dsc = _bwd_pool(proj, dypool, pw, pool_scale)
    dproj = jnp.concatenate([dxp, dgl, dxpool, dgates], axis=1)
    small = jnp.concatenate([
        jnp.zeros((1, D), F32), dg2, dg3, dg4, dbg.reshape(2, D), dcb, dba, dbx, dlam,
        jnp.pad(dsc, ((0, 0), (0, D - DP))), jnp.pad(lossp, ((0, 0), (0, D - 1))), dcw,
        _block_diag_extract(dwa, hd).reshape(-1, D), _block_diag_extract(dwx, hd).reshape(-1, D),
        dpw.reshape(-1, D)], axis=0)
    (dw_in, dh1), (c_mix, (l_small,)) = _bwd_inproj(h1, dproj, full["w_in"],
                                                     stages=[_to_chips(p_mix), _to_sibling([small])])
    small2 = _add_pair(small, l_small, "add_sibling_small").reshape(2, SMALL_ROWS // 2, D)
    g_in = _halves(dw_in)
    done = ["w_ff1", "w_ff2"] + mix
    (fl_gin, fl_small), _ = _split_call("reduce_in_sibling_start", start=[_to_sibling([g_in]), _to_chips([small2])])
    _split_call("reduce_in_sibling_done", finish=[fl_gin, fl_ff])
    (g_in, l_in), (p_ff1, p_ff2, c_ff1, c_ff2) = fl_gin.bufs, fl_ff.bufs
    p_in = _add_sibling(g_in, l_in, cidx, "add_sibling_w_in")[0]
    ssem, rsem, p_in, c_in, token = _chips_start(p_in)
    pairs, _ = _add_chips_multi([p_ff1, p_ff2] + p_mix, [c_ff1, c_ff2] + c_mix, idx_big, "add_chips_done",
                                stages=[_after(token)])
    _split_call("reduce_small_done", finish=[fl_small], after=pairs[-1])
    small2, c_small = fl_small.bufs
    own_small = lax.dynamic_index_in_dim(small2, core, 0, keepdims=True)
    c_small = lax.dynamic_update_slice(c_small, own_small, (chip, zero, zero))
    pair_small = _add_chips(c_small, c_small, jnp.stack([zero, zero + 1, zero + 2, zero + 3, core]), "add_chips_small")
    (fl_share,), shared_start = _split_call("reduce_share_start", start=[_share(pairs + [pair_small])])
    (grad_x, dg1), _ = _bwd_prenorm(xs, dh1, dxres, g1, stages=[_after(shared_start)])
    _split_call("reduce_share_done", finish=[fl_share], after=dg1)
    shared = fl_share.landed()
    pairs, pair_small = shared[:-1], shared[-1]

    grads, delta, new_m, new_v = {}, {}, {}, {}
    for n, p in zip(done, pairs):
        grads[n] = p.reshape(-1, p.shape[-1])

    def update(n, stages=()):
        (grads[n], delta[n], new_m[n], new_v[n]), landed = _adamw(w[n], grads[n], mom[n], var[n], "adamw_" + n,
                                                                  stages=stages)
        return landed

    updated, _ = _adamw_multi(["w_ff1", "w_ff2", "w_o", "w_lru_up"], w, grads, mom, var)
    for n, (go, d, mo, vo) in updated.items():
        grads[n], delta[n], new_m[n], new_v[n] = go, d, mo, vo
    p_in, c_in = _chips_wait(ssem, rsem, p_in, c_in, new_v["w_lru_up"])
    pair_in = _add_chips(p_in, c_in, idx_big, "add_chips_w_in")
    ((pair_in,), (dg1_all,)) = update("w_pool_up", stages=[_share([pair_in]), _to_everyone(dg1)])
    dg1_all = lax.dynamic_update_slice(dg1_all, dg1[None], (2 * chip + core, zero, zero)).reshape(2 * NCHIP, D)
    grads["w_in"] = pair_in.reshape(-1, pair_in.shape[-1])
    update("w_in")
    small_sum = pair_small.reshape(SMALL_ROWS, D)
    loss = 0.5 * small_sum[LOSS_ROW, 0]
    ccols = DR // NCHIP
    sep = [lax.dynamic_slice(small_sum[12:16], (zero, chip * ccols), (4, ccols)),
           small_sum[16:80].reshape(-1, hd), small_sum[80:144].reshape(-1, hd), small_sum[144:208].reshape(-1, PG)]
    g_s, d_s, m_s, v_s = _adamw_small(small_sum, dg1_all, sep, w, mom, var)
    grads.update(g_s)
    grads.update(dict(zip(SMALL_SEPARATE, sep)))
    delta.update(d_s)
    new_m.update(m_s)
    new_v.update(v_s)

    out = lambda d: [d[n].reshape(args[n].shape) for n in W_NAMES]
    return (loss, grad_x[None], *out(grads), *out(delta), *out(new_m), *out(new_v))
```

```python
import functools
import math

import jax
import jax.numpy as jnp
from jax import lax
from jax.experimental import pallas as pl
from jax.experimental.pallas import tpu as pltpu

F32 = jnp.float32
BF = jnp.bfloat16

T = 2048
D = 1024
DR = 1024
DP = 512
DF = 4096
DIN = 4608
NCHIP = 4
CW_IN = DIN // NCHIP
LANE = 128
CB = 128
NG = DR // CB
PG = 128
POOL_WINDOWS = (2, 4, 8, 16)
NORM_EPS = 1e-6
LRU_C = 8.0
GELU_C = math.sqrt(2.0 / math.pi)
ADAM_LR = 0.001
ADAM_B1 = 0.9
ADAM_B2 = 0.999
ADAM_EPS = 1e-08
ADAM_WD = 0.01
ADAM_STEP = 10
MESH_ID = pl.DeviceIdType.MESH
ANY = pl.BlockSpec(memory_space=pl.ANY)
SMALL_ROWS = 208
LOSS_ROW = 11
MIB = 1 << 20


def _cp(vmem_mib=None):
    if vmem_mib is None:
        return pltpu.CompilerParams()
    return pltpu.CompilerParams(vmem_limit_bytes=vmem_mib * MIB)


def _hbm(*arrays):
    return [pltpu.with_memory_space_constraint(a, pltpu.HBM) for a in arrays]


def _hbm_out(shapes):
    return [pltpu.HBM(s.shape, s.dtype) for s in shapes]


class _Stage:
    def __init__(self, operands, out_shape, alias, sems, start, finish):
        self.operands, self.out_shape, self.alias, self.sems = list(operands), list(out_shape), dict(alias), list(sems)
        self.start, self.finish = start, finish


def _call(body, *, name, grid, in_specs, out_specs, out_shape, args, vmem=None, stages=(), prefetch=None,
          scratch=()):
    nin, nout = len(in_specs), len(out_specs)
    npre = 0 if prefetch is None else 1
    st_args, st_shapes, st_sems, aliases = [], [], list(scratch), {}
    for st in stages:
        for k, v in st.alias.items():
            aliases[npre + nin + len(st_args) + k] = nout + len(st_shapes) + v
        st_args += st.operands
        st_shapes += st.out_shape
        st_sems += st.sems

    def wrapped(*refs):
        pre, refs = refs[:npre], refs[npre:]
        ins, pos = refs[:nin], nin
        st_ins = []
        for st in stages:
            st_ins.append(refs[pos:pos + len(st.operands)])
            pos += len(st.operands)
        outs, pos = refs[pos:pos + nout], pos + nout
        st_outs = []
        for st in stages:
            st_outs.append(refs[pos:pos + len(st.out_shape)])
            pos += len(st.out_shape)
        work, pos = refs[pos:pos + len(scratch)], pos + len(scratch)
        sems = []
        for st in stages:
            sems.append(refs[pos:pos + len(st.sems)])
            pos += len(st.sems)
        if stages:
            first = functools.reduce(jnp.logical_and, [pl.program_id(a) == 0 for a in range(len(grid))])

            @pl.when(first)
            def _():
                for st, a, b, s in zip(stages, st_ins, st_outs, sems):
                    st.start(a, b, s)

        body(*pre, *ins, *outs, *work)
        if stages:
            last = functools.reduce(jnp.logical_and, [pl.program_id(a) == g - 1 for a, g in enumerate(grid)])

            @pl.when(last)
            def _():
                for st, a, b, s in zip(stages, st_ins, st_outs, sems):
                    st.finish(a, b, s)

    all_in = list(in_specs) + [ANY] * len(st_args)
    all_out = list(out_specs) + [ANY] * len(st_shapes)
    kw = dict(has_side_effects=True) if stages else {}
    if vmem is not None:
        kw["vmem_limit_bytes"] = vmem * MIB
    if prefetch is None:
        gkw = dict(grid=grid, in_specs=all_in, out_specs=all_out, scratch_shapes=st_sems)
    else:
        gkw = dict(grid_spec=pltpu.PrefetchScalarGridSpec(
            num_scalar_prefetch=1, grid=grid, in_specs=all_in, out_specs=all_out, scratch_shapes=st_sems))
    res = pl.pallas_call(
        wrapped, name=name, out_shape=_hbm_out(list(out_shape) + st_shapes), input_output_aliases=aliases,
        compiler_params=pltpu.CompilerParams(**kw), **gkw,
    )(*([prefetch] if npre else []), *_hbm(*args, *st_args))
    outs, rest, st_res = list(res[:nout]), list(res[nout:]), []
    for st in stages:
        st_res.append(rest[:len(st.out_shape)])
        rest = rest[len(st.out_shape):]
    return outs, st_res


def _mm(a, b):
    return jnp.dot(a.astype(BF), b.astype(BF), preferred_element_type=F32)


def _mm_nt(a, b):
    return lax.dot_general(a.astype(BF), b.astype(BF), (((1,), (1,)), ((), ())),
                           preferred_element_type=F32)


def _mm_tn(a, b):
    return lax.dot_general(a.astype(BF), b.astype(BF), (((0,), (0,)), ((), ())),
                           preferred_element_type=F32)


def _rows(v):
    return lax.broadcasted_iota(jnp.int32, v.shape, 0)


def _sd(v, s, fill=0.0):
    return jnp.where(_rows(v) >= s, pltpu.roll(v, s, axis=0), fill)


def _su(v, s, fill=0.0):
    n = v.shape[0]
    return jnp.where(_rows(v) < n - s, pltpu.roll(v, n - s, axis=0), fill)


def _sigmoid(z):
    return 1.0 / (1.0 + jnp.exp(-z))


def _softplus(z):
    e = jnp.exp(-jnp.abs(z))
    u = 1.0 + e
    d = u - 1.0
    log1p = jnp.where(d == 0.0, e, jnp.log(u) * (e / jnp.where(d == 0.0, 1.0, d)))
    return jnp.maximum(z, 0.0) + log1p


def _mean(v):
    return jnp.mean(v, axis=-1, keepdims=True)


def _colsum(v):
    return jnp.sum(v, axis=0, keepdims=True)


def _acc(ref, val, first):
    @pl.when(first)
    def _():
        ref[...] = val

    @pl.when(jnp.logical_not(first))
    def _():
        ref[...] += val


def _conv(xp, cw, cb):
    x1, x2, x3 = _sd(xp, 1), _sd(xp, 2), _sd(xp, 3)
    xc = cb + cw[0:1] * x3 + cw[1:2] * x2 + cw[2:3] * x1 + cw[3:4] * xp
    return xc, x1, x2, x3


def _lru_gates(xc, wa, ba, wx, bx, lam):
    xcb = xc.astype(BF)
    r = _sigmoid(_mm(xcb, wa) + ba)
    ii = _sigmoid(_mm(xcb, wx) + bx)
    sp = _softplus(-lam)
    la = (-LRU_C) * r * sp
    a = jnp.exp(la)
    mult = jnp.sqrt(-jnp.tanh(la) * (a * a + 1.0))
    return xcb, r, ii, sp, a, mult


def _gelu_parts(g):
    th = jnp.tanh(GELU_C * (g + 0.044715 * (g * g * g)))
    gel = 0.5 * g * (1.0 + th)
    dgel = 0.5 * (1.0 + th) + 0.5 * g * (1.0 - th * th) * (GELU_C * (1.0 + 3.0 * 0.044715 * (g * g)))
    return gel, dgel


def _tile_scan(a, b, a_s, b_s, out_ref, reverse):
    n = a.shape[0]
    nt = n // 8
    sub = jnp.bitwise_and(_rows(a), 7)
    s = 1
    while s < 8:
        keep = sub < 8 - s if reverse else sub >= s
        amount = n - s if reverse else s
        b = b + a * jnp.where(keep, pltpu.roll(b, amount, axis=0), 0.0)
        a = a * jnp.where(keep, pltpu.roll(a, amount, axis=0), 1.0)
        s *= 2
    a_s[...] = a
    b_s[...] = b
    edge = pl.ds(0 if reverse else 7, nt, stride=8)
    ta, tb = a_s[edge, :], b_s[edge, :]
    shift = _su if reverse else _sd
    s = 1
    while s < nt:
        tb = tb + ta * shift(tb, s, 0.0)
        if 2 * s < nt:
            ta = ta * shift(ta, s, 1.0)
        s *= 2
    enters = shift(tb, 1, 0.0)
    for o in range(8):
        rows = pl.ds(o, nt, stride=8)
        out_ref[rows, :] = b_s[rows, :] + a_s[rows, :] * enters


def _pool_window(x, steps, shift):
    s, sh = x, 1
    for _ in range(steps):
        s = s + shift(s, sh)
        sh *= 2
    return s


def _fwd_inproj(x, g1, w_in, stages=()):
    tm = 512

    def body(x_ref, g_ref, w_ref, proj_ref, h_ref):
        @pl.when(pl.program_id(1) == 0)
        def _():
            xv = x_ref[...]
            r = lax.rsqrt(_mean(xv * xv) + NORM_EPS)
            h_ref[...] = ((xv * r) * g_ref[...]).astype(BF)

        proj_ref[...] = jnp.dot(h_ref[...], w_ref[0], preferred_element_type=F32)

    return _call(
        body, name="fwd_inproj", grid=(T // tm, NCHIP),
        in_specs=[pl.BlockSpec((tm, D), lambda i, k: (i, 0)),
                  pl.BlockSpec((1, D), lambda i, k: (0, 0)),
                  pl.BlockSpec((1, D, CW_IN), lambda i, k: (k, 0, 0))],
        out_specs=[pl.BlockSpec((tm, CW_IN), lambda i, k: (i, k)),
                   pl.BlockSpec((tm, D), lambda i, k: (i, 0))],
        out_shape=[jax.ShapeDtypeStruct((T, DIN), F32), jax.ShapeDtypeStruct((T, D), BF)],
        vmem=40, args=[x, g1, w_in], stages=stages)


def _vec_spec():
    return pl.BlockSpec((1, CB), lambda j: (0, j))


def _fwd_lru(proj, conv_w, conv_b, wa, ba, wx, bx, lam, stages=()):
    def body(xp_ref, g_ref, cw_ref, cb_ref, wa_ref, ba_ref, wx_ref, bx_ref, lam_ref, y_ref, h_ref, a_s, b_s):
        xc, _, _, _ = _conv(xp_ref[...], cw_ref[...], cb_ref[...])
        _, _, ii, _, a, mult = _lru_gates(xc, wa_ref[0], ba_ref[...], wx_ref[0], bx_ref[...], lam_ref[...])
        _tile_scan(a, mult * (ii * xc), a_s, b_s, h_ref, reverse=False)
        gel, _ = _gelu_parts(g_ref[...])
        y_ref[...] = (h_ref[...] * gel).astype(BF)

    return _call(
        body, name="fwd_lru", grid=(NG,),
        in_specs=[pl.BlockSpec((T, CB), lambda j: (0, j)),
                  pl.BlockSpec((T, CB), lambda j: (0, NG + j)),
                  pl.BlockSpec((4, CB), lambda j: (0, j)),
                  _vec_spec(),
                  pl.BlockSpec((1, CB, CB), lambda j: (j, 0, 0)), _vec_spec(),
                  pl.BlockSpec((1, CB, CB), lambda j: (j, 0, 0)), _vec_spec(),
                  _vec_spec()],
        out_specs=[pl.BlockSpec((T, CB), lambda j: (0, j)), pl.BlockSpec((T, CB), lambda j: (0, j))],
        out_shape=[jax.ShapeDtypeStruct((T, DR), BF), jax.ShapeDtypeStruct((T, DR), F32)],
        vmem=48, args=[proj, proj, conv_w, conv_b, wa, ba, wx, bx, lam], stages=stages,
        scratch=[pltpu.VMEM((T, CB), F32)] * 2)


def _pool_cnt(w):
    t = lax.broadcasted_iota(jnp.int32, (T, 1), 0)
    return jnp.minimum(t + 1, w).astype(F32)


def _fwd_pool(proj, pool_w, pool_scale):
    def body(xp_ref, pw_ref, sc_ref, y_ref):
        for g, w in enumerate(POOL_WINDOWS):
            cols = slice(g * PG, (g + 1) * PG)
            x = xp_ref[:, cols]
            p = _pool_window(x, g + 1, _sd) / _pool_cnt(w) - x
            y_ref[:, cols] = (_mm(p, pw_ref[g]) * sc_ref[:, cols]).astype(BF)

    return pl.pallas_call(
        body, name="fwd_pool", grid=(1,),
        in_specs=[pl.BlockSpec((T, DP), lambda i: (0, 2 * DR // DP)),
                  pl.BlockSpec((4, PG, PG), lambda i: (0, 0, 0)),
                  pl.BlockSpec((1, DP), lambda i: (0, 0))],
        out_specs=pl.BlockSpec((T, DP), lambda i: (0, 0)),
        out_shape=pltpu.HBM((T, DP), BF),
        compiler_params=_cp(48),
    )(*_hbm(proj, pool_w, pool_scale))


GATE_BLK = 512
GATE_BLK0 = (2 * DR + DP) // GATE_BLK


def _gate_specs(tm):
    return [pl.BlockSpec((tm, GATE_BLK), functools.partial(lambda i, q: (i, GATE_BLK0 + q), q=q))
            for q in range(4)]


def _fwd_merge(x, ylru, ypool, proj, b_gate, g2, g3, w_lru_up, w_pool_up, w_o, stages=()):
    tm = 512

    def body(x_ref, yl_ref, yp_ref, p0, p1, p2, p3, bg_ref, g2_ref, g3_ref, wl_ref, wp_ref, wo_ref,
             x2_ref, h2_ref, m_ref, mrg_ref, bra_ref, brb_ref):
        bra = jnp.dot(yl_ref[...], wl_ref[...], preferred_element_type=F32)
        yp = yp_ref[...]
        brb = jnp.concatenate([jnp.dot(yp, wp_ref[k], preferred_element_type=F32) for k in range(NCHIP)], axis=1)
        bg = bg_ref[...]
        ga = _sigmoid(jnp.concatenate([p0[...], p1[...]], axis=1) + bg[:, :D])
        gb = _sigmoid(jnp.concatenate([p2[...], p3[...]], axis=1) + bg[:, D:])
        mrg = (ga * bra + gb * brb).astype(BF)
        m = jnp.dot(mrg, wo_ref[...], preferred_element_type=F32)
        r2 = lax.rsqrt(_mean(m * m) + NORM_EPS)
        x2 = x_ref[...] + (m * r2) * g2_ref[...]
        r3 = lax.rsqrt(_mean(x2 * x2) + NORM_EPS)
        x2_ref[...] = x2
        h2_ref[...] = ((x2 * r3) * g3_ref[...]).astype(BF)
        m_ref[...] = m
        mrg_ref[...] = mrg
        bra_ref[...] = bra.astype(BF)
        brb_ref[...] = brb.astype(BF)

    row = lambda w: pl.BlockSpec((tm, w), lambda i: (i, 0))
    full2 = lambda a, b: pl.BlockSpec((a, b), lambda i: (0, 0))
    return _call(
        body, name="fwd_merge", grid=(T // tm,),
        in_specs=[row(D), row(DR), row(DP)] + _gate_specs(tm) +
                 [full2(1, 2 * D), full2(1, D), full2(1, D), full2(DR, D),
                  pl.BlockSpec((NCHIP, DP, D // NCHIP), lambda i: (0, 0, 0)), full2(D, D)],
        out_specs=[row(D)] * 6,
        out_shape=[jax.ShapeDtypeStruct((T, D), F32), jax.ShapeDtypeStruct((T, D), BF),
                   jax.ShapeDtypeStruct((T, D), F32), jax.ShapeDtypeStruct((T, D), BF),
                   jax.ShapeDtypeStruct((T, D), BF), jax.ShapeDtypeStruct((T, D), BF)],
        vmem=48, args=[x, ylru, ypool, proj, proj, proj, proj, b_gate, g2, g3, w_lru_up, w_pool_up, w_o],
        stages=stages)


def _fwd_mlp(h2, w_ff1, w_ff2):
    tm = 512
    fk = DF // NCHIP

    def body(h_ref, w1_ref, w2_ref, a1_ref, f_ref):
        h = h_ref[...]
        f = None
        for k in range(NCHIP):
            a1 = jnp.maximum(jnp.dot(h, w1_ref[k], preferred_element_type=F32), 0.0)
            a1_ref[:, k * fk:(k + 1) * fk] = a1.astype(BF)
            part = jnp.dot((a1 * a1).astype(BF), w2_ref[k * fk:(k + 1) * fk, :], preferred_element_type=F32)
            f = part if f is None else f + part
        f_ref[...] = f

    return pl.pallas_call(
        body, name="fwd_mlp", grid=(T // tm,),
        in_specs=[pl.BlockSpec((tm, D), lambda i: (i, 0)),
                  pl.BlockSpec((NCHIP, D, fk), lambda i: (0, 0, 0)),
                  pl.BlockSpec((DF, D), lambda i: (0, 0))],
        out_specs=[pl.BlockSpec((tm, DF), lambda i: (i, 0)), pl.BlockSpec((tm, D), lambda i: (i, 0))],
        out_shape=_hbm_out([jax.ShapeDtypeStruct((T, DF), BF), jax.ShapeDtypeStruct((T, D), F32)]),
        compiler_params=_cp(56),
    )(*_hbm(h2, w_ff1, w_ff2))


def _loss_head(f, x2, target, g4):
    tm = 512

    def body(f_ref, x2_ref, t_ref, g_ref, loss_ref, dy_ref, df_ref, dg_ref):
        first = pl.program_id(0) == 0
        f = f_ref[...]
        g4v = g_ref[...]
        r4 = lax.rsqrt(_mean(f * f) + NORM_EPS)
        fn = f * r4
        e = (x2_ref[...] + fn * g4v) - t_ref[...]
        _acc(loss_ref, jnp.sum(_mean(e * e), axis=0, keepdims=True), first)
        dy = e * (1.0 / D)
        dy_ref[...] = dy
        _acc(dg_ref, _colsum(dy * fn), first)
        dfn = dy * g4v
        df_ref[...] = (r4 * (dfn - fn * _mean(dfn * fn))).astype(BF)

    row = pl.BlockSpec((tm, D), lambda i: (i, 0))
    return pl.pallas_call(
        body, name="loss_head", grid=(T // tm,),
        in_specs=[row, row, row, pl.BlockSpec((1, D), lambda i: (0, 0))],
        out_specs=[pl.BlockSpec((1, 1), lambda i: (0, 0)), row, row, pl.BlockSpec((1, D), lambda i: (0, 0))],
        out_shape=_hbm_out([jax.ShapeDtypeStruct((1, 1), F32), jax.ShapeDtypeStruct((T, D), F32),
                            jax.ShapeDtypeStruct((T, D), BF), jax.ShapeDtypeStruct((1, D), F32)]),
        compiler_params=_cp(48),
    )(*_hbm(f, x2, target, g4))


def _bwd_mlp_x(df, a1, w_ff1, w_ff2):
    tm = 512
    fk = DF // NCHIP

    def body(df_ref, a1_ref, w1_ref, w2_ref, dh_ref, df1_ref):
        df = df_ref[...]
        dh = None
        for k in range(NCHIP):
            cols = slice(k * fk, (k + 1) * fk)
            dact = _mm_nt(df, w2_ref[cols, :])
            df1 = (dact * (2.0 * a1_ref[:, cols].astype(F32))).astype(BF)
            df1_ref[:, cols] = df1
            part = _mm_nt(df1, w1_ref[k])
            dh = part if dh is None else dh + part
        dh_ref[...] = dh

    return pl.pallas_call(
        body, name="bwd_mlp_x", grid=(T // tm,),
        in_specs=[pl.BlockSpec((tm, D), lambda i: (i, 0)),
                  pl.BlockSpec((tm, DF), lambda i: (i, 0)),
                  pl.BlockSpec((NCHIP, D, fk), lambda i: (0, 0, 0)),
                  pl.BlockSpec((DF, D), lambda i: (0, 0))],
        out_specs=[pl.BlockSpec((tm, D), lambda i: (i, 0)), pl.BlockSpec((tm, DF), lambda i: (i, 0))],
        out_shape=_hbm_out([jax.ShapeDtypeStruct((T, D), F32), jax.ShapeDtypeStruct((T, DF), BF)]),
        compiler_params=_cp(56),
    )(*_hbm(df, a1, w_ff1, w_ff2))


def _bwd_mlp_w(df, h2, a1, df1):
    fc = 512
    per = (DF // NCHIP) // fc

    def body(df_ref, h_ref, a1_ref, df1_ref, dw1_ref, dw2_ref):
        a1 = a1_ref[...].astype(F32)
        dw2_ref[...] = _mm_tn((a1 * a1).astype(BF), df_ref[...]).astype(BF)
        dw1_ref[0] = _mm_tn(h_ref[...], df1_ref[...]).astype(BF)

    return pl.pallas_call(
        body, name="bwd_mlp_w", grid=(DF // fc,),
        in_specs=[pl.BlockSpec((T, D), lambda j: (0, 0)),
                  pl.BlockSpec((T, D), lambda j: (0, 0)),
                  pl.BlockSpec((T, fc), lambda j: (0, j)),
                  pl.BlockSpec((T, fc), lambda j: (0, j))],
        out_specs=[pl.BlockSpec((1, D, fc), lambda j: (j // per, 0, j % per)),
                   pl.BlockSpec((fc, D), lambda j: (j, 0))],
        out_shape=_hbm_out([jax.ShapeDtypeStruct((NCHIP, D, DF // NCHIP), BF),
                            jax.ShapeDtypeStruct((DF, D), BF)]),
        compiler_params=_cp(56),
    )(*_hbm(df, h2, a1, df1))


def _bwd_merge(dh2, dy, x2, m, bra, brb, proj, b_gate, g2, g3, w_lru_up, w_pool_up, w_o, stages=()):
    tm = 256
    cpu = D // NCHIP

    def body(dh2_ref, dy_ref, x2_ref, m_ref, bra_ref, brb_ref, p0, p1, p2, p3, bg_ref,
             g2_ref, g3_ref, wl_ref, wp_ref, wo_ref,
             dx_ref, dgt_ref, dyl_ref, dyp_ref, dm_ref, dbra_ref, dbrb_ref, dg2_ref, dg3_ref, dbg_ref):
        first = pl.program_id(0) == 0
        x2 = x2_ref[...]
        r3 = lax.rsqrt(_mean(x2 * x2) + NORM_EPS)
        x2n = x2 * r3
        dh2 = dh2_ref[...]
        t3 = dh2 * g3_ref[...]
        dx2 = dy_ref[...] + r3 * (t3 - x2n * _mean(t3 * x2n))
        dx_ref[...] = dx2
        _acc(dg3_ref, _colsum(dh2 * x2n), first)
        m = m_ref[...]
        r2 = lax.rsqrt(_mean(m * m) + NORM_EPS)
        mn = m * r2
        _acc(dg2_ref, _colsum(dx2 * mn), first)
        dmn = dx2 * g2_ref[...]
        dm = (r2 * (dmn - mn * _mean(dmn * mn))).astype(BF)
        dm_ref[...] = dm
        dmrg = _mm_nt(dm, wo_ref[...])
        bg = bg_ref[...]
        ga = _sigmoid(jnp.concatenate([p0[...], p1[...]], axis=1) + bg[:, :D])
        gb = _sigmoid(jnp.concatenate([p2[...], p3[...]], axis=1) + bg[:, D:])
        dga = dmrg * bra_ref[...].astype(F32) * (ga * (1.0 - ga))
        dgb = dmrg * brb_ref[...].astype(F32) * (gb * (1.0 - gb))
        dgt_ref[:, :D] = dga.astype(BF)
        dgt_ref[:, D:] = dgb.astype(BF)
        _acc(dbg_ref, jnp.concatenate([_colsum(dga), _colsum(dgb)], axis=1), first)
        dbra = (dmrg * ga).astype(BF)
        dbrb = (dmrg * gb).astype(BF)
        dbra_ref[...] = dbra
        dbrb_ref[...] = dbrb
        dyl_ref[...] = _mm_nt(dbra, wl_ref[...])
        dyp = None
        for k in range(NCHIP):
            part = _mm_nt(dbrb[:, k * cpu:(k + 1) * cpu], wp_ref[k])
            dyp = part if dyp is None else dyp + part
        dyp_ref[...] = dyp

    row = lambda w: pl.BlockSpec((tm, w), lambda i: (i, 0))
    full2 = lambda a, b: pl.BlockSpec((a, b), lambda i: (0, 0))
    wp_spec = pl.BlockSpec((NCHIP, DP, cpu), lambda i: (0, 0, 0))
    return _call(
        body, name="bwd_merge", grid=(T // tm,),
        in_specs=[row(D)] * 6 + _gate_specs(tm) +
                 [full2(1, 2 * D), full2(1, D), full2(1, D), full2(DR, D), wp_spec, full2(D, D)],
        out_specs=[row(D), row(2 * D), row(DR), row(DP), row(D), row(D), row(D),
                   full2(1, D), full2(1, D), full2(1, 2 * D)],
        out_shape=[jax.ShapeDtypeStruct((T, D), F32), jax.ShapeDtypeStruct((T, 2 * D), BF),
                   jax.ShapeDtypeStruct((T, DR), F32), jax.ShapeDtypeStruct((T, DP), F32),
                   jax.ShapeDtypeStruct((T, D), BF), jax.ShapeDtypeStruct((T, D), BF),
                   jax.ShapeDtypeStruct((T, D), BF),
                   jax.ShapeDtypeStruct((1, D), F32), jax.ShapeDtypeStruct((1, D), F32),
                   jax.ShapeDtypeStruct((1, 2 * D), F32)],
        vmem=56, args=[dh2, dy, x2, m, bra, brb, proj, proj, proj, proj, b_gate, g2, g3, w_lru_up, w_pool_up, w_o],
        stages=stages)


def _dw_merge(mrg, dm, ylru, dbra, ypool, dbrb, stages=()):
    nb = NCHIP
    rb, pb, cpu = D // nb, DP // nb, D // NCHIP

    def body(mrg_ref, dm_ref, yl_ref, dbra_ref, yp_ref, dbrb_ref, dwo_ref, dwl_ref, dwp_ref):
        dwo_ref[...] = _mm_tn(mrg_ref[...], dm_ref[...]).astype(BF)
        dwl_ref[...] = _mm_tn(yl_ref[...], dbra_ref[...]).astype(BF)
        dwp = _mm_tn(yp_ref[...], dbrb_ref[...]).astype(BF)
        for k in range(NCHIP):
            dwp_ref[k] = dwp[:, k * cpu:(k + 1) * cpu]

    cols = lambda w: pl.BlockSpec((T, w), lambda r: (0, r))
    whole = pl.BlockSpec((T, D), lambda r: (0, 0))
    return _call(
        body, name="dw_merge", grid=(nb,),
        in_specs=[cols(rb), whole, cols(rb), whole, cols(pb), whole],
        out_specs=[pl.BlockSpec((rb, D), lambda r: (r, 0)), pl.BlockSpec((rb, D), lambda r: (r, 0)),
                   pl.BlockSpec((NCHIP, pb, cpu), lambda r: (0, r, 0))],
        out_shape=[jax.ShapeDtypeStruct((D, D), BF), jax.ShapeDtypeStruct((DR, D), BF),
                   jax.ShapeDtypeStruct((NCHIP, DP, cpu), BF)],
        vmem=56, args=[mrg, dm, ylru, dbra, ypool, dbrb], stages=stages)


def _bwd_lru(proj, h, dylru, conv_w, conv_b, wa, ba, wx, bx, lam, stages=()):
    def body(xp_ref, g_ref, h_ref, dy_ref, cw_ref, cb_ref, wa_ref, ba_ref, wx_ref, bx_ref, lam_ref,
             dxp_ref, dg_ref, dcw_ref, dcb_ref, dwa_ref, dba_ref, dwx_ref, dbx_ref, dlam_ref, a_s, b_s, l_s):
        xp = xp_ref[...]
        cw = cw_ref[...]
        lam = lam_ref[...]
        xc, x1, x2, x3 = _conv(xp, cw, cb_ref[...])
        wa, wx = wa_ref[0], wx_ref[0]
        xcb, r, ii, sp, a, mult = _lru_gates(xc, wa, ba_ref[...], wx, bx_ref[...], lam)
        g = g_ref[...]
        gel, dgel = _gelu_parts(g)
        h = h_ref[...]
        dy = dy_ref[...]
        dg_ref[...] = (dy * h * dgel).astype(BF)
        _tile_scan(_su(a, 1, 0.0), dy * gel, a_s, b_s, l_s, reverse=True)
        b = l_s[...]
        da = b * _sd(h, 1, 0.0)
        dmult = b * (ii * xc)
        dii = b * (mult * xc)
        dxc = b * (mult * ii)
        dla = da * a - dmult * ((a * a) / mult)
        dr = dla * ((-LRU_C) * sp)
        dsp = _colsum(dla * ((-LRU_C) * r))
        dlam_ref[...] = -dsp / (1.0 + jnp.exp(lam))
        dzr = dr * (r * (1.0 - r))
        dzi = dii * (ii * (1.0 - ii))
        dzrb, dzib = dzr.astype(BF), dzi.astype(BF)
        dxc = dxc + _mm_nt(dzrb, wa) + _mm_nt(dzib, wx)
        dwa_ref[0] = _mm_tn(xcb, dzrb)
        dwx_ref[0] = _mm_tn(xcb, dzib)
        dba_ref[...] = _colsum(dzr)
        dbx_ref[...] = _colsum(dzi)
        dcb_ref[...] = _colsum(dxc)
        dcw_ref[...] = jnp.concatenate([_colsum(dxc * x3), _colsum(dxc * x2), _colsum(dxc * x1),
                                        _colsum(dxc * xp)], axis=0)
        dxp = cw[3:4] * dxc + cw[2:3] * _su(dxc, 1) + cw[1:2] * _su(dxc, 2) + cw[0:1] * _su(dxc, 3)
        dxp_ref[...] = dxp.astype(BF)

    blk = pl.BlockSpec((T, CB), lambda j: (0, j))
    wsp = pl.BlockSpec((1, CB, CB), lambda j: (j, 0, 0))
    return _call(
        body, name="bwd_lru", grid=(NG,),
        in_specs=[blk, pl.BlockSpec((T, CB), lambda j: (0, NG + j)), blk, blk,
                  pl.BlockSpec((4, CB), lambda j: (0, j)), _vec_spec(), wsp, _vec_spec(), wsp, _vec_spec(),
                  _vec_spec()],
        out_specs=[blk, blk, pl.BlockSpec((4, CB), lambda j: (0, j)), _vec_spec(), wsp, _vec_spec(), wsp,
                   _vec_spec(), _vec_spec()],
        out_shape=[jax.ShapeDtypeStruct((T, DR), BF), jax.ShapeDtypeStruct((T, DR), BF),
                   jax.ShapeDtypeStruct((4, DR), F32), jax.ShapeDtypeStruct((1, DR), F32),
                   jax.ShapeDtypeStruct((NG, CB, CB), F32), jax.ShapeDtypeStruct((1, DR), F32),
                   jax.ShapeDtypeStruct((NG, CB, CB), F32), jax.ShapeDtypeStruct((1, DR), F32),
                   jax.ShapeDtypeStruct((1, DR), F32)],
        vmem=56, args=[proj, proj, h, dylru, conv_w, conv_b, wa, ba, wx, bx, lam], stages=stages,
        scratch=[pltpu.VMEM((T, CB), F32)] * 3)


def _bwd_pool(proj, dypool, pool_w, pool_scale):
    def body(xp_ref, dy_ref, pw_ref, sc_ref, dx_ref, dw_ref, dsc_ref):
        for g, w in enumerate(POOL_WINDOWS):
            cols = slice(g * PG, (g + 1) * PG)
            cnt = _pool_cnt(w)
            x = xp_ref[:, cols]
            pb = (_pool_window(x, g + 1, _sd) / cnt - x).astype(BF)
            wg = pw_ref[g]
            dy = dy_ref[:, cols]
            dsc_ref[:, cols] = _colsum(dy * _mm(pb, wg))
            dyp = (dy * sc_ref[:, cols]).astype(BF)
            dw_ref[g] = _mm_tn(pb, dyp)
            dp = _mm_nt(dyp, wg)
            dx_ref[:, cols] = (_pool_window(dp / cnt, g + 1, _su) - dp).astype(BF)

    return pl.pallas_call(
        body, name="bwd_pool", grid=(1,),
        in_specs=[pl.BlockSpec((T, DP), lambda i: (0, 2 * DR // DP)),
                  pl.BlockSpec((T, DP), lambda i: (0, 0)),
                  pl.BlockSpec((4, PG, PG), lambda i: (0, 0, 0)),
                  pl.BlockSpec((1, DP), lambda i: (0, 0))],
        out_specs=[pl.BlockSpec((T, DP), lambda i: (0, 0)),
                   pl.BlockSpec((4, PG, PG), lambda i: (0, 0, 0)),
                   pl.BlockSpec((1, DP), lambda i: (0, 0))],
        out_shape=_hbm_out([jax.ShapeDtypeStruct((T, DP), BF), jax.ShapeDtypeStruct((4, PG, PG), F32),
                            jax.ShapeDtypeStruct((1, DP), F32)]),
        compiler_params=_cp(48),
    )(*_hbm(proj, dypool, pool_w, pool_scale))


def _bwd_inproj(h1, dproj, w_in, stages=()):
    def body(h_ref, dp_ref, w_ref, dw_ref, dh_ref):
        dp = dp_ref[...]
        dw_ref[0] = _mm_tn(h_ref[...], dp).astype(BF)
        _acc(dh_ref, _mm_nt(dp, w_ref[0]), pl.program_id(0) == 0)

    return _call(
        body, name="bwd_inproj", grid=(NCHIP,),
        in_specs=[pl.BlockSpec((T, D), lambda k: (0, 0)),
                  pl.BlockSpec((T, CW_IN), lambda k: (0, k)),
                  pl.BlockSpec((1, D, CW_IN), lambda k: (k, 0, 0))],
        out_specs=[pl.BlockSpec((1, D, CW_IN), lambda k: (k, 0, 0)), pl.BlockSpec((T, D), lambda k: (0, 0))],
        out_shape=[jax.ShapeDtypeStruct((NCHIP, D, CW_IN), BF), jax.ShapeDtypeStruct((T, D), F32)],
        vmem=56, args=[h1, dproj, w_in], stages=stages)


def _bwd_prenorm(x, dh1, dxres, g1, stages=()):
    tm = 512

    def body(x_ref, dh_ref, dr_ref, g_ref, dx_ref, dg_ref):
        xv = x_ref[...]
        r = lax.rsqrt(_mean(xv * xv) + NORM_EPS)
        xn = xv * r
        dh = dh_ref[...]
        t = dh * g_ref[...]
        dx_ref[...] = dr_ref[...] + r * (t - xn * _mean(t * xn))
        _acc(dg_ref, _colsum(dh * xn), pl.program_id(0) == 0)

    row = pl.BlockSpec((tm, D), lambda i: (i, 0))
    vec = pl.BlockSpec((1, D), lambda i: (0, 0))
    return _call(
        body, name="bwd_prenorm", grid=(T // tm,),
        in_specs=[row, row, row, vec], out_specs=[row, vec],
        out_shape=[jax.ShapeDtypeStruct((T, D), F32), jax.ShapeDtypeStruct((1, D), F32)],
        vmem=48, args=[x, dh1, dxres, g1], stages=stages)


def _place():
    x, y, c = lax.axis_index("x"), lax.axis_index("y"), lax.axis_index("c")
    chips = [(1 - x, y), (x, 1 - y), (1 - x, 1 - y)]
    return x, y, c, chips


def _rcopy(src, dst, ssem, rsem, dev):
    return pltpu.make_async_remote_copy(src_ref=src, dst_ref=dst, send_sem=ssem, recv_sem=rsem,
                                        device_id=dev, device_id_type=MESH_ID)


def _sds(a):
    return jax.ShapeDtypeStruct(a.shape, a.dtype)


def _sem2(n, m):
    return [pltpu.SemaphoreType.DMA((n * m,)), pltpu.SemaphoreType.DMA((n * m,))]


ALL = (0, 1, 1)


def _piece(ref, k, half, part):
    hr = ref.shape[1] // 2
    r0, r1 = hr * part[0] // part[2], hr * part[1] // part[2]
    return ref.at[k, pl.ds(half * hr + r0, r1 - r0), :]


def _gather(fulls, ici=(), d2d=()):
    n = len(fulls)
    ici, d2d = list(ici), list(d2d)
    pieces = [("ici", i, part) for i, part in ici] + [("d2d", i, part) for i, part in d2d]

    def copies(outs, sems):
        x, y, c, chips = _place()
        me = 2 * x + y
        sib = (x, y, 1 - c)
        send, recv = [], []
        for q, (kind, i, part) in enumerate(pieces):
            for j, chip in enumerate(chips):
                k, s = 2 * chip[0] + chip[1], 3 * q + j
                if kind == "ici":
                    mine, theirs, dev = _piece(outs[i], me, c, part), _piece(outs[i], k, c, part), (*chip, c)
                else:
                    mine, theirs, dev = _piece(outs[i], k, c, part), _piece(outs[i], k, 1 - c, part), sib
                send.append(_rcopy(mine, mine, sems[0].at[s], sems[1].at[s], dev))
                recv.append(_rcopy(theirs, theirs, sems[0].at[s], sems[1].at[s], dev))
        return send, recv

    def start(ins, outs, sems):
        for cp in copies(outs, sems)[0]:
            cp.start()

    def finish(ins, outs, sems):
        send, recv = copies(outs, sems)
        for cp in recv:
            cp.wait_recv()
        for cp in send:
            cp.wait_send()

    sems = [pltpu.SemaphoreType.DMA((3 * len(pieces),)), pltpu.SemaphoreType.DMA((3 * len(pieces),))]
    return _Stage(fulls, [_sds(f) for f in fulls], {i: i for i in range(n)}, sems, start, finish)


def _gather_whole(v):
    def copies(ins, outs, sems):
        x, y, c, chips = _place()
        me = 2 * x + y
        send = [_rcopy(ins[0], outs[0].at[me], sems[0].at[j], sems[1].at[j], (*chip, c))
                for j, chip in enumerate(chips)]
        recv = [_rcopy(ins[0], outs[0].at[2 * chip[0] + chip[1]], sems[0].at[j], sems[1].at[j], (*chip, c))
                for j, chip in enumerate(chips)]
        return send, recv

    def start(ins, outs, sems):
        for cp in copies(ins, outs, sems)[0]:
            cp.start()

    def finish(ins, outs, sems):
        send, recv = copies(ins, outs, sems)
        for cp in recv:
            cp.wait_recv()
        for cp in send:
            cp.wait_send()

    return _Stage([v], [jax.ShapeDtypeStruct((NCHIP,) + v.shape, v.dtype)], {},
                  [pltpu.SemaphoreType.DMA((3,)), pltpu.SemaphoreType.DMA((3,))], start, finish)


def _to_sibling(srcs):
    n = len(srcs)

    def copies(ins, outs, sems):
        x, y, c, _ = _place()
        sib = (x, y, 1 - c)
        return [_rcopy(ins[i].at[:, 1 - c] if srcs[i].ndim == 4 else ins[i], outs[i], sems[0].at[i], sems[1].at[i], sib)
                for i in range(n)]

    def start(ins, outs, sems):
        for cp in copies(ins, outs, sems):
            cp.start()

    def finish(ins, outs, sems):
        for cp in copies(ins, outs, sems):
            cp.wait()

    shapes = [jax.ShapeDtypeStruct((NCHIP,) + s.shape[2:] if s.ndim == 4 else s.shape, s.dtype) for s in srcs]
    return _Stage(srcs, shapes, {}, [pltpu.SemaphoreType.DMA((n,)), pltpu.SemaphoreType.DMA((n,))], start, finish)


def _to_chips(srcs, parts=None, lands=None):
    n = len(srcs)
    parts = [ALL] * n if parts is None else parts
    lands = [None] * n if lands is None else lands
    given = [i for i in range(n) if lands[i] is not None]

    def rows(ref, i):
        hr = srcs[i].shape[1]
        r0, r1 = hr * parts[i][0] // parts[i][2], hr * parts[i][1] // parts[i][2]
        return ref.at[pl.ds(r0, r1 - r0), :]

    def copies(ins, outs, sems):
        x, y, c, chips = _place()
        me = 2 * x + y
        return [_rcopy(rows(ins[i].at[2 * chip[0] + chip[1]] if srcs[i].shape[0] == NCHIP else ins[i].at[c], i),
                       rows(outs[i].at[me], i), sems[0].at[3 * i + j], sems[1].at[3 * i + j], (*chip, c))
                for i in range(n) for j, chip in enumerate(chips)]

    def start(ins, outs, sems):
        for cp in copies(ins, outs, sems):
            cp.start()

    def finish(ins, outs, sems):
        for cp in copies(ins, outs, sems):
            cp.wait()

    shapes = [jax.ShapeDtypeStruct((NCHIP,) + s.shape[1:], s.dtype) for s in srcs]
    alias = {n + q: i for q, i in enumerate(given)}
    return _Stage(list(srcs) + [lands[i] for i in given], shapes, alias, _sem2(n, 3), start, finish)


HBM_REF = pl.BlockSpec(memory_space=pltpu.HBM)
SEM_REF = pl.BlockSpec(memory_space=pltpu.SEMAPHORE)
DATAFLOW = pltpu.SideEffectType.DATAFLOW_SIDE_EFFECTING


def _after(x):
    return _Stage([x], [], {}, [], lambda *a: None, lambda *a: None)


class _Flight:
    def __init__(self, stage, sems, bufs):
        self.stage, self.sems, self.bufs = stage, list(sems), list(bufs)

    def landed(self):
        st, n = self.stage, len(self.stage.operands)
        fresh = [j for j in range(len(st.out_shape)) if j not in st.alias.values()]
        back = {v: k for k, v in st.alias.items()}
        return [self.bufs[back[j]] if j in back else self.bufs[n + fresh.index(j)] for j in range(len(st.out_shape))]


def _split_call(name, finish=(), start=(), after=None):
    bufs, stage_bufs = [], []

    def slot(a):
        for i, b in enumerate(bufs):
            if b is a:
                return i
        bufs.append(a)
        return len(bufs) - 1

    fin_slots = [[slot(b) for b in fl.bufs] for fl in finish]
    for st in start:
        fresh = [lax.empty(o.shape, o.dtype) for j, o in enumerate(st.out_shape) if j not in st.alias.values()]
        stage_bufs.append([slot(a) for a in list(st.operands) + fresh])
    old_sems = [s for fl in finish for s in fl.sems]
    new_sems = [s for st in start for s in st.sems]
    nb, no, nn = len(bufs), len(old_sems), len(new_sems)

    def refs_of(st, slots, buf_refs):
        n = len(st.operands)
        ins = [buf_refs[i] for i in slots[:n]]
        fresh = [j for j in range(len(st.out_shape)) if j not in st.alias.values()]
        back = {v: k for k, v in st.alias.items()}
        outs = [ins[back[j]] if j in back else buf_refs[slots[n + fresh.index(j)]] for j in range(len(st.out_shape))]
        return ins, outs

    def body(*refs):
        buf_refs, sem_in = refs[:nb], refs[nb:nb + no]
        sem_out = refs[nb + no + (after is not None):][:nn]
        token = refs[-1]
        pos = 0
        for fl, slots in zip(finish, fin_slots):
            ins, outs = refs_of(fl.stage, slots, buf_refs)
            fl.stage.finish(ins, outs, sem_in[pos:pos + len(fl.sems)])
            pos += len(fl.sems)
        pos = 0
        for st, slots in zip(start, stage_bufs):
            ins, outs = refs_of(st, slots, buf_refs)
            st.start(ins, outs, sem_out[pos:pos + len(st.sems)])
            pos += len(st.sems)
        token[...] = jnp.zeros_like(token)

    res = pl.pallas_call(
        body, name=name,
        out_shape=tuple(new_sems) + tuple(pltpu.HBM(b.shape, b.dtype) for b in bufs) +
                  (jax.ShapeDtypeStruct((8, LANE), F32),),
        in_specs=(HBM_REF,) * nb + (SEM_REF,) * no + ((pl.BlockSpec(memory_space=pl.ANY),) if after is not None else ()),
        out_specs=(SEM_REF,) * nn + (HBM_REF,) * nb + (pl.BlockSpec(memory_space=pltpu.VMEM),),
        input_output_aliases={i: nn + i for i in range(nb)},
        compiler_params=pltpu.CompilerParams(has_side_effects=DATAFLOW),
    )(*_hbm(*bufs), *old_sems, *([after] if after is not None else []))
    sems, thru, token = res[:nn], res[nn:nn + nb], res[-1]
    for fl, slots in zip(finish, fin_slots):
        fl.bufs = [thru[i] for i in slots]
    flights, pos = [], 0
    for st, slots in zip(start, stage_bufs):
        flights.append(_Flight(st, sems[pos:pos + len(st.sems)], [thru[i] for i in slots]))
        pos += len(st.sems)
    return flights, token


def _last_copies(p_ref, land_ref, ssem, rsem):
    x, y, c, chips = _place()
    me = 2 * x + y
    send = [_rcopy(p_ref.at[2 * chip[0] + chip[1]], land_ref.at[me], ssem.at[j], rsem.at[j], (*chip, c))
            for j, chip in enumerate(chips)]
    recv = [_rcopy(p_ref.at[2 * chip[0] + chip[1]], land_ref.at[2 * chip[0] + chip[1]], ssem.at[j], rsem.at[j],
                   (*chip, c)) for j, chip in enumerate(chips)]
    return send, recv


def _chips_start(p):
    def body(p_ref, land_ref, ssem, rsem, p_thru, land_thru, token):
        for cp in _last_copies(p_ref, land_ref, ssem, rsem)[0]:
            cp.start()
        token[...] = jnp.zeros_like(token)

    return pl.pallas_call(
        body, name="reduce_last_start",
        out_shape=(pltpu.SemaphoreType.DMA((3,)), pltpu.SemaphoreType.DMA((3,)), pltpu.HBM(p.shape, p.dtype),
                   pltpu.HBM(p.shape, p.dtype), jax.ShapeDtypeStruct((8, LANE), F32)),
        in_specs=(HBM_REF, HBM_REF),
        out_specs=(SEM_REF, SEM_REF, HBM_REF, HBM_REF, pl.BlockSpec(memory_space=pltpu.VMEM)),
        input_output_aliases={0: 2, 1: 3},
        compiler_params=pltpu.CompilerParams(has_side_effects=DATAFLOW),
    )(*_hbm(p, lax.empty(p.shape, p.dtype)))


def _chips_wait(ssem, rsem, p_thru, land_thru, after):
    def body(p_ref, land_ref, ssem, rsem, after_ref, p_dead, got_ref):
        send, recv = _last_copies(p_ref, land_ref, ssem, rsem)
        for cp in send:
            cp.wait_send()
        for cp in recv:
            cp.wait_recv()

    return pl.pallas_call(
        body, name="reduce_last_wait",
        out_shape=(pltpu.HBM(p_thru.shape, p_thru.dtype), pltpu.HBM(land_thru.shape, land_thru.dtype)),
        in_specs=(HBM_REF, HBM_REF, SEM_REF, SEM_REF, pl.BlockSpec(memory_space=pl.ANY)),
        out_specs=(HBM_REF, HBM_REF), input_output_aliases={0: 0, 1: 1},
        compiler_params=pltpu.CompilerParams(has_side_effects=DATAFLOW),
    )(p_thru, land_thru, ssem, rsem, after)


def _share(pairs):
    n = len(pairs)

    def start(ins, outs, sems):
        x, y, c, _ = _place()
        for i in range(n):
            _rcopy(outs[i].at[c], outs[i].at[c], sems[0].at[i], sems[1].at[i], (x, y, 1 - c)).start()

    def finish(ins, outs, sems):
        x, y, c, _ = _place()
        for i in range(n):
            _rcopy(outs[i].at[c], outs[i].at[c], sems[0].at[i], sems[1].at[i], (x, y, 1 - c)).wait_send()
            _rcopy(outs[i].at[1 - c], outs[i].at[1 - c], sems[0].at[i], sems[1].at[i], (x, y, 1 - c)).wait_recv()

    return _Stage(pairs, [_sds(p) for p in pairs], {i: i for i in range(n)},
                  [pltpu.SemaphoreType.DMA((n,)), pltpu.SemaphoreType.DMA((n,))], start, finish)


def _row_block(rows, cols, itemsize=4, target=2 * MIB):
    br = rows
    while br * cols * itemsize > target and br % 32 == 0:
        br //= 2
    return br


def _cast_place(w, chip_idx, name):
    rows, cols = w.shape
    br = _row_block(rows, cols)

    def body(k_ref, w_ref, o_ref):
        o_ref[0] = w_ref[...].astype(BF)

    return _call(
        body, name=name, grid=(rows // br,), prefetch=chip_idx,
        in_specs=[pl.BlockSpec((br, cols), lambda r, k: (r, 0))],
        out_specs=[pl.BlockSpec((1, br, cols), lambda r, k: (k[0], r, 0))],
        out_shape=[jax.ShapeDtypeStruct((NCHIP, rows, cols), BF)], vmem=32, args=[w])[0][0]


def _cast_place_multi(ws, chip_idx, stages=()):
    br = 128
    nblk = [a.shape[0] // br for a in ws]
    starts = [sum(nblk[:i]) for i in range(len(ws))]

    def body(k_ref, *refs):
        r = pl.program_id(0)
        for i in range(len(ws)):
            @pl.when(jnp.logical_and(r >= starts[i], r < starts[i] + nblk[i]))
            def _(i=i):
                refs[len(ws) + i][0] = refs[i][...].astype(BF)

    def at(i):
        return functools.partial(lambda r, s, nb: jnp.clip(r - s, 0, nb - 1), s=starts[i], nb=nblk[i])

    outs, landed = _call(
        body, name="cast_rest", grid=(sum(nblk),), prefetch=chip_idx,
        in_specs=[pl.BlockSpec((br, a.shape[1]), functools.partial(lambda r, k, f: (f(r), 0), f=at(i)))
                  for i, a in enumerate(ws)],
        out_specs=[pl.BlockSpec((1, br, a.shape[1]), functools.partial(lambda r, k, f: (k[0], f(r), 0), f=at(i)))
                   for i, a in enumerate(ws)],
        out_shape=[jax.ShapeDtypeStruct((NCHIP,) + a.shape, BF) for a in ws], vmem=32, args=list(ws), stages=stages)
    return outs, landed


def _add_sibling(g, land, cidx, name, stages=()):
    _, _, hr, cols = g.shape
    br = _row_block(hr, cols)

    def body(c_ref, g_ref, l_ref, o_ref):
        o_ref[...] = (g_ref[0, 0].astype(F32) + l_ref[0].astype(F32)).astype(BF)[None]

    outs, st = _call(
        body, name=name, grid=(NCHIP, hr // br), prefetch=cidx,
        in_specs=[pl.BlockSpec((1, 1, br, cols), lambda k, r, c: (k, c[0], r, 0)),
                  pl.BlockSpec((1, br, cols), lambda k, r, c: (k, r, 0))],
        out_specs=[pl.BlockSpec((1, br, cols), lambda k, r, c: (k, r, 0))],
        out_shape=[jax.ShapeDtypeStruct((NCHIP, hr, cols), BF)], vmem=32, args=[g, land], stages=stages)
    return outs[0], st


def _add_pair(a, b, name):
    rows, cols = a.shape

    def body(a_ref, b_ref, o_ref):
        o_ref[...] = a_ref[...] + b_ref[...]

    spec = pl.BlockSpec((rows, cols), lambda r: (0, 0))
    return _call(body, name=name, grid=(1,), in_specs=[spec, spec], out_specs=[spec], out_shape=[_sds(a)],
                 vmem=32, args=[a, b])[0][0]


def _add_chips(own, land, idx, name, stages=None):
    _, hr, cols = land.shape
    br = _row_block(hr, cols)

    def body(s_ref, a_ref, b_ref, c_ref, d_ref, o_ref):
        o_ref[...] = (a_ref[...].astype(F32) + b_ref[...].astype(F32)) + (c_ref[...].astype(F32) +
                                                                           d_ref[...].astype(F32))

    spec = lambda q: pl.BlockSpec((1, br, cols), functools.partial(lambda r, s, q: (s[q], r, 0), q=q))
    outs, landed = _call(
        body, name=name, grid=(hr // br,), prefetch=idx,
        in_specs=[spec(0), spec(1), spec(2), spec(3)], out_specs=[spec(4)],
        out_shape=[jax.ShapeDtypeStruct((2, hr, cols), F32)], vmem=48, args=[own, land, land, land],
        stages=stages or ())
    return outs[0] if stages is None else (outs[0], landed)


def _add_chips_multi(owns, lands, idx, name, stages=()):
    n = len(owns)
    brs = [_row_block(l.shape[1], l.shape[2]) for l in lands]
    nblk = [l.shape[1] // b for l, b in zip(lands, brs)]
    starts = [sum(nblk[:i]) for i in range(n)]

    def body(s_ref, *refs):
        r = pl.program_id(0)
        for i in range(n):
            a_ref, b_ref, c_ref, d_ref = refs[4 * i:4 * i + 4]
            o_ref = refs[4 * n + i]

            @pl.when(jnp.logical_and(r >= starts[i], r < starts[i] + nblk[i]))
            def _():
                o_ref[...] = (a_ref[...].astype(F32) + b_ref[...].astype(F32)) + (c_ref[...].astype(F32) +
                                                                                   d_ref[...].astype(F32))

    def spec(i, q):
        return pl.BlockSpec((1, brs[i], lands[i].shape[2]), functools.partial(
            lambda r, s, q, st, nb: (s[q], jnp.clip(r - st, 0, nb - 1), 0), q=q, st=starts[i], nb=nblk[i]))

    outs, landed = _call(
        body, name=name, grid=(sum(nblk),), prefetch=idx,
        in_specs=[spec(i, q) for i in range(n) for q in range(4)], out_specs=[spec(i, 4) for i in range(n)],
        out_shape=[jax.ShapeDtypeStruct((2,) + l.shape[1:], F32) for l in lands], vmem=48,
        args=[a for i in range(n) for a in (owns[i], lands[i], lands[i], lands[i])], stages=stages)
    return outs, landed


def _adamw_math(w, g, m, v):
    mn = ADAM_B1 * m + (1.0 - ADAM_B1) * g
    vn = ADAM_B2 * v + (1.0 - ADAM_B2) * (g * g)
    m_hat = mn / (1.0 - ADAM_B1 ** ADAM_STEP)
    v_hat = vn / (1.0 - ADAM_B2 ** ADAM_STEP)
    return -ADAM_LR * (m_hat / (jnp.sqrt(v_hat) + ADAM_EPS) + ADAM_WD * w), mn, vn


def _adamw(w, g, m, v, name, stages=()):
    rows, cols = w.shape
    br = _row_block(rows, cols)

    def body(w_ref, g_ref, m_ref, v_ref, go_ref, d_ref, mo_ref, vo_ref):
        gv = g_ref[...]
        go_ref[...] = gv
        d_ref[...], mo_ref[...], vo_ref[...] = _adamw_math(w_ref[...], gv, m_ref[...], v_ref[...])

    spec = pl.BlockSpec((br, cols), lambda r: (r, 0))
    return _call(body, name=name, grid=(rows // br,), in_specs=[spec] * 4, out_specs=[spec] * 4,
                 out_shape=[_sds(w)] * 4, vmem=56, args=[w, g, m, v], stages=stages)


def _adamw_multi(names, w, g, m, v, stages=()):
    cols = w[names[0]].shape[1]
    br = 128
    nblk = [w[n].shape[0] // br for n in names]
    starts = [sum(nblk[:i]) for i in range(len(names))]

    def body(*refs):
        r = pl.program_id(0)
        for i in range(len(names)):
            w_ref, g_ref, m_ref, v_ref = refs[4 * i:4 * i + 4]
            go_ref, d_ref, mo_ref, vo_ref = refs[4 * len(names) + 4 * i:4 * len(names) + 4 * i + 4]

            @pl.when(jnp.logical_and(r >= starts[i], r < starts[i] + nblk[i]))
            def _():
                gv = g_ref[...]
                go_ref[...] = gv
                d_ref[...], mo_ref[...], vo_ref[...] = _adamw_math(w_ref[...], gv, m_ref[...], v_ref[...])

    def spec(i):
        return pl.BlockSpec((br, cols), functools.partial(
            lambda r, s, nb: (jnp.clip(r - s, 0, nb - 1), 0), s=starts[i], nb=nblk[i]))

    outs, landed = _call(
        body, name="adamw_" + "_".join(names), grid=(sum(nblk),),
        in_specs=[spec(i) for i in range(len(names)) for _ in range(4)],
        out_specs=[spec(i) for i in range(len(names)) for _ in range(4)],
        out_shape=[_sds(w[n]) for n in names for _ in range(4)], vmem=56,
        args=[a[n] for n in names for a in (w, g, m, v)], stages=stages)
    return {n: outs[4 * i:4 * i + 4] for i, n in enumerate(names)}, landed


def _to_everyone(v):
    deltas = [(a, b, e) for a in (0, 1) for b in (0, 1) for e in (0, 1)][1:]

    def copies(ins, outs, sems):
        x, y, c, _ = _place()
        me = 4 * x + 2 * y + c
        flip = lambda p, f: 1 - p if f else p
        return [_rcopy(ins[0], outs[0].at[me], sems[0].at[q], sems[1].at[q], (flip(x, a), flip(y, b), flip(c, e)))
                for q, (a, b, e) in enumerate(deltas)]

    def start(ins, outs, sems):
        for cp in copies(ins, outs, sems):
            cp.start()

    def finish(ins, outs, sems):
        for cp in copies(ins, outs, sems):
            cp.wait()

    n = len(deltas)
    return _Stage([v], [jax.ShapeDtypeStruct((2 * NCHIP,) + v.shape, v.dtype)], {},
                  [pltpu.SemaphoreType.DMA((n,)), pltpu.SemaphoreType.DMA((n,))], start, finish)


SMALL_AT = {"norm_mix_pre": (0, 1, D), "norm_mix_post": (1, 1, D), "norm_mlp_pre": (2, 1, D),
            "norm_mlp_post": (3, 1, D), "b_gate": (4, 2, D), "conv_b": (6, 1, D), "lru_b_a": (7, 1, D),
            "lru_b_x": (8, 1, D), "lru_lambda": (9, 1, D), "pool_scale": (10, 1, DP)}
SMALL_SEPARATE = ["conv_w", "lru_w_a", "lru_w_x", "pool_w"]


def _adamw_small(small_sum, first_all, sep_grads, w, m, v):
    packed, sep = list(SMALL_AT), list(SMALL_SEPARATE)
    names = packed + sep

    def body(*refs):
        s_ref, a_ref, refs = refs[0], refs[1], refs[2:]
        g_sep, refs = refs[:len(sep)], refs[len(sep):]
        nn = len(names)
        w_r, m_r, v_r, refs = refs[:nn], refs[nn:2 * nn], refs[2 * nn:3 * nn], refs[3 * nn:]
        g_out, refs = refs[:len(packed)], refs[len(packed):]
        d_o, m_o, v_o = refs[:nn], refs[nn:2 * nn], refs[2 * nn:3 * nn]
        for i, n in enumerate(names):
            if i == 0:
                g = a_ref[0:1, :]
                for q in range(1, 2 * NCHIP):
                    g = g + a_ref[q:q + 1, :]
                g_out[i][...] = g
            elif n in SMALL_AT:
                r0, nr, nc = SMALL_AT[n]
                g = jnp.concatenate([s_ref[r0 + q:r0 + q + 1, :nc] for q in range(nr)], axis=1)
                g_out[i][...] = g
            else:
                g = g_sep[i - len(packed)][...]
            d_o[i][...], m_o[i][...], v_o[i][...] = _adamw_math(w_r[i][...], g, m_r[i][...], v_r[i][...])

    ws = [w[n] for n in names]
    res = pl.pallas_call(
        body, name="adamw_small",
        out_shape=[_sds(w[n]) for n in packed] + [_sds(a) for a in ws] * 3,
        compiler_params=_cp(32),
    )(*_hbm(small_sum, first_all, *sep_grads, *ws, *[m[n] for n in names], *[v[n] for n in names]))
    nn, npk = len(names), len(packed)
    grad = dict(zip(packed, res[:npk]))
    delta = dict(zip(names, res[npk:npk + nn]))
    new_m = dict(zip(names, res[npk + nn:npk + 2 * nn]))
    new_v = dict(zip(names, res[npk + 2 * nn:]))
    return grad, delta, new_m, new_v


W_NAMES = ["norm_mix_pre", "norm_mix_post", "norm_mlp_pre", "norm_mlp_post", "w_in", "b_gate", "conv_w", "conv_b",
           "lru_w_a", "lru_b_a", "lru_w_x", "lru_b_x", "lru_lambda", "pool_w", "pool_scale", "w_lru_up",
           "w_pool_up", "w_o", "w_ff1", "w_ff2"]
BIG = ["w_in", "w_lru_up", "w_pool_up", "w_o", "w_ff1", "w_ff2"]


def _block_diag(w):
    hd = w.shape[-1]
    per = CB // hd
    w4 = w.reshape(NG, per, hd, hd)
    eye = jnp.eye(per, dtype=w.dtype)
    return jnp.einsum("gpij,pq->gpiqj", w4, eye).reshape(NG, CB, CB)


def _block_diag_extract(d, hd):
    per = CB // hd
    d5 = d.reshape(NG, per, hd, per, hd)
    return jnp.stack([d5[:, p, :, p, :] for p in range(per)], axis=1).reshape(NG * per, hd, hd)


def _halves(g):
    return g.reshape(NCHIP, 2, g.size // (g.shape[-1] * 2 * NCHIP), g.shape[-1])


def kernel(x, norm_mix_pre, norm_mix_post, norm_mlp_pre, norm_mlp_post, w_in, b_gate, conv_w, conv_b, lru_w_a, lru_b_a, lru_w_x, lru_b_x, lru_lambda, pool_w, pool_scale, w_lru_up, w_pool_up, w_o, w_ff1, w_ff2, loss_target, m_norm_mix_pre, m_norm_mix_post, m_norm_mlp_pre, m_norm_mlp_post, m_w_in, m_b_gate, m_conv_w, m_conv_b, m_lru_w_a, m_lru_b_a, m_lru_w_x, m_lru_b_x, m_lru_lambda, m_pool_w, m_pool_scale, m_w_lru_up, m_w_pool_up, m_w_o, m_w_ff1, m_w_ff2, v_norm_mix_pre, v_norm_mix_post, v_norm_mlp_pre, v_norm_mlp_post, v_w_in, v_b_gate, v_conv_w, v_conv_b, v_lru_w_a, v_lru_b_a, v_lru_w_x, v_lru_b_x, v_lru_lambda, v_pool_w, v_pool_scale, v_w_lru_up, v_w_pool_up, v_w_o, v_w_ff1, v_w_ff2):
    args = dict(locals())
    two_d = lambda a: a.reshape(-1, a.shape[-1])
    w = {n: two_d(args[n]) for n in W_NAMES}
    mom = {n: two_d(args["m_" + n]) for n in W_NAMES}
    var = {n: two_d(args["v_" + n]) for n in W_NAMES}
    i32 = lambda val: jnp.asarray(val, jnp.int32)
    chip = i32(2 * lax.axis_index("x") + lax.axis_index("y"))
    core = i32(lax.axis_index("c"))
    cidx = core.reshape(1)
    zero = i32(0)
    hd = lru_w_a.shape[-1]
    xs, target = x[0], loss_target[0]
    g1, g2, g3, g4 = norm_mix_pre, norm_mix_post, norm_mlp_pre, norm_mlp_post

    mix = ["w_lru_up", "w_pool_up", "w_o"]
    full = {"w_in": _cast_place(w["w_in"], chip.reshape(1), "cast_w_in")}
    (fl_in, fl_conv), first = _split_call("gather_start_first", start=[
        _gather([full["w_in"]], ici=[(0, ALL)]), _gather_whole(w["conv_w"])])
    casts, _ = _cast_place_multi([w[n] for n in BIG[1:]], chip.reshape(1), stages=[_after(first)])
    full.update(zip(BIG[1:], casts))
    (fl_mix, fl_ff1, fl_ff2), started = _split_call("gather_start_rest", start=[
        _gather([full[n] for n in mix], ici=[(0, ALL), (1, ALL), (2, ALL)]),
        _gather([full["w_ff1"]], ici=[(0, ALL)]), _gather([full["w_ff2"]], ici=[(0, ALL)])])
    wa = _block_diag(lru_w_a[0]).astype(BF)
    wx = _block_diag(lru_w_x[0]).astype(BF)
    pw = pool_w[0].astype(BF)

    def to_sibling(name, flight, after=None):
        (fl,), passed = _split_call(name + "_pass", finish=[flight], after=after,
                                    start=[_gather(flight.landed(), d2d=[(i, ALL) for i in range(len(flight.bufs))])])
        passed_on.append(passed)
        return fl

    passed_on = []

    def arrived(name, flight, after=None):
        _split_call(name + "_done", finish=[flight], after=after)
        return flight.landed()

    fl_in = to_sibling("gather_w_in", fl_in, after=started)
    w_in_f, = arrived("gather_w_in", fl_in)
    conv_all, = arrived("gather_conv", fl_conv)
    full["w_in"] = w_in_f
    conv_all = lax.dynamic_update_slice(conv_all, w["conv_w"][None], (chip, zero, zero))
    conv_full = jnp.transpose(conv_all, (1, 0, 2)).reshape(4, DR)
    (proj, h1), _ = _fwd_inproj(xs, g1, w_in_f)
    fl_mix = to_sibling("gather_mix", fl_mix, after=h1)
    (ylru, hs), _ = _fwd_lru(proj, conv_full, conv_b, wa, lru_b_a, wx, lru_b_x, lru_lambda,
                             stages=[_after(passed_on[-1])])
    got = arrived("gather_mix", fl_mix, after=ylru)
    fl_ff1 = to_sibling("gather_ff1", fl_ff1, after=ylru)
    w_lru_up_f, w_pool_up_f, w_o_f = got[0].reshape(DR, D), got[1], got[2].reshape(D, D)
    ypool = _fwd_pool(proj, pw, pool_scale)
    (x2, h2, m, mrg, bra, brb), _ = _fwd_merge(xs, ylru, ypool, proj, b_gate, g2, g3, w_lru_up_f, w_pool_up_f, w_o_f,
                                               stages=[_after(passed_on[-1])])
    fl_ff2 = to_sibling("gather_ff2", fl_ff2, after=h2)
    ff1, = arrived("gather_ff1", fl_ff1, after=h2)
    ff2, = arrived("gather_ff2", fl_ff2)
    ff2 = ff2.reshape(DF, D)
    a1, f = _fwd_mlp(h2, ff1, ff2)
    lossp, dy, df, dg4 = _loss_head(f, x2, target, g4)

    idx_big = jnp.stack([chip, (chip + 1) % NCHIP, (chip + 2) % NCHIP, (chip + 3) % NCHIP, core])
    dh2, df1 = _bwd_mlp_x(df, a1, ff1, ff2)
    dw_ff1, dw_ff2 = _bwd_mlp_w(df, h2, a1, df1)
    g_ff = [_halves(dw_ff1), _halves(dw_ff2)]
    (dxres, dgates, dylru, dypool, dm, dbra, dbrb, dg2, dg3, dbg), (l_ff,) = _bwd_merge(
        dh2, dy, x2, m, bra, brb, proj, b_gate, g2, g3, w_lru_up_f, w_pool_up_f, w_o_f, stages=[_to_sibling(g_ff)])
    p_ff = [_add_sibling(g, l, cidx, "add_sibling_" + n)[0] for g, l, n in zip(g_ff, l_ff, ["w_ff1", "w_ff2"])]
    (fl_ff,), sent_ff = _split_call("reduce_ff_start", start=[_to_chips(p_ff)])
    (dw_o, dw_lru_up, dw_pool_up), _ = _dw_merge(mrg, dm, ylru, dbra, ypool, dbrb, stages=[_after(sent_ff)])
    g_mix = [_halves(dw_lru_up), _halves(dw_pool_up), _halves(dw_o)]
    (dxp, dgl, dcw, dcb, dwa, dba, dwx, dbx, dlam), (l_mix,) = _bwd_lru(
        proj, hs, dylru, conv_full, conv_b, wa, lru_b_a, wx, lru_b_x, lru_lambda, stages=[_to_sibling(g_mix)])
    p_mix = [_add_sibling(g, l, cidx, "add_sibling_" + n)[0] for g, l, n in zip(g_mix, l_mix, mix)]
    dxpool, dpw, dsc = _bwd_pool(proj, dypool, pw, pool_scale)
    dproj = jnp.concatenate([dxp, dgl, dxpool, dgates], axis=1)
    small = jnp.concatenate([
        jnp.zeros((1, D), F32), dg2, dg3, dg4, dbg.reshape(2, D), dcb, dba, dbx, dlam,
        jnp.pad(dsc, ((0, 0), (0, D - DP))), jnp.pad(lossp, ((0, 0), (0, D - 1))), dcw,
        _block_diag_extract(dwa, hd).reshape(-1, D), _block_diag_extract(dwx, hd).reshape(-1, D),
        dpw.reshape(-1, D)], axis=0)
    (dw_in, dh1), (c_mix, (l_small,)) = _bwd_inproj(h1, dproj, full["w_in"],
                                                     stages=[_to_chips(p_mix), _to_sibling([small])])
    small2 = _add_pair(small, l_small, "add_sibling_small").reshape(2, SMALL_ROWS // 2, D)
    g_in = _halves(dw_in)
    done = ["w_ff1", "w_ff2"] + mix
    (fl_gin, fl_small), _ = _split_call("reduce_in_sibling_start", start=[_to_sibling([g_in]), _to_chips([small2])])
    _split_call("reduce_in_sibling_done", finish=[fl_gin, fl_ff])
    (g_in, l_in), (p_ff1, p_ff2, c_ff1, c_ff2) = fl_gin.bufs, fl_ff.bufs
    p_in = _add_sibling(g_in, l_in, cidx, "add_sibling_w_in")[0]
    ssem, rsem, p_in, c_in, token = _chips_start(p_in)
    pairs, _ = _add_chips_multi([p_ff1, p_ff2] + p_mix, [c_ff1, c_ff2] + c_mix, idx_big, "add_chips_done",
                                stages=[_after(token)])
    _split_call("reduce_small_done", finish=[fl_small], after=pairs[-1])
    small2, c_small = fl_small.bufs
    own_small = lax.dynamic_index_in_dim(small2, core, 0, keepdims=True)
    c_small = lax.dynamic_update_slice(c_small, own_small, (chip, zero, zero))
    pair_small = _add_chips(c_small, c_small, jnp.stack([zero, zero + 1, zero + 2, zero + 3, core]), "add_chips_small")
    (fl_share,), shared_start = _split_call("reduce_share_start", start=[_share(pairs + [pair_small])])
    (grad_x, dg1), _ = _bwd_prenorm(xs, dh1, dxres, g1, stages=[_after(shared_start)])
    _split_call("reduce_share_done", finish=[fl_share], after=dg1)
    shared = fl_share.landed()
    pairs, pair_small = shared[:-1], shared[-1]

    grads, delta, new_m, new_v = {}, {}, {}, {}
    for n, p in zip(done, pairs):
        grads[n] = p.reshape(-1, p.shape[-1])

    def update(n, stages=()):
        (grads[n], delta[n], new_m[n], new_v[n]), landed = _adamw(w[n], grads[n], mom[n], var[n], "adamw_" + n,
                                                                  stages=stages)
        return landed

    updated, _ = _adamw_multi(["w_ff1", "w_ff2", "w_o", "w_lru_up"], w, grads, mom, var)
    for n, (go, d, mo, vo) in updated.items():
        grads[n], delta[n], new_m[n], new_v[n] = go, d, mo, vo
    p_in, c_in = _chips_wait(ssem, rsem, p_in, c_in, new_v["w_lru_up"])
    pair_in = _add_chips(p_in, c_in, idx_big, "add_chips_w_in")
    ((pair_in,), (dg1_all,)) = update("w_pool_up", stages=[_share([pair_in]), _to_everyone(dg1)])
    dg1_all = lax.dynamic_update_slice(dg1_all, dg1[None], (2 * chip + core, zero, zero)).reshape(2 * NCHIP, D)
    grads["w_in"] = pair_in.reshape(-1, pair_in.shape[-1])
    update("w_in")
    small_sum = pair_small.reshape(SMALL_ROWS, D)
    loss = 0.5 * small_sum[LOSS_ROW, 0]
    ccols = DR // NCHIP
    sep = [lax.dynamic_slice(small_sum[12:16], (zero, chip * ccols), (4, ccols)),
           small_sum[16:80].reshape(-1, hd), small_sum[80:144].reshape(-1, hd), small_sum[144:208].reshape(-1, PG)]
    g_s, d_s, m_s, v_s = _adamw_small(small_sum, dg1_all, sep, w, mom, var)
    grads.update(g_s)
    grads.update(dict(zip(SMALL_SEPARATE, sep)))
    delta.update(d_s)
    new_m.update(m_s)
    new_v.update(v_s)

    out = lambda d: [d[n].reshape(args[n].shape) for n in W_NAMES]
    return (loss, grad_x[None], *out(grads), *out(delta), *out(new_m), *out(new_v))
```

```python
import functools
import math

import jax
import jax.numpy as jnp
from jax import lax
from jax.experimental import pallas as pl
from jax.experimental.pallas import tpu as pltpu

F32 = jnp.float32
BF = jnp.bfloat16

T = 2048
D = 1024
DR = 1024
DP = 512
DF = 4096
DIN = 4608
NCHIP = 4
CW_IN = DIN // NCHIP
LANE = 128
CB = 128
NG = DR // CB
PG = 128
POOL_WINDOWS = (2, 4, 8, 16)
NORM_EPS = 1e-6
LRU_C = 8.0
GELU_C = math.sqrt(2.0 / math.pi)
ADAM_LR = 0.001
ADAM_B1 = 0.9
ADAM_B2 = 0.999
ADAM_EPS = 1e-08
ADAM_WD = 0.01
ADAM_STEP = 10
MESH_ID = pl.DeviceIdType.MESH
ANY = pl.BlockSpec(memory_space=pl.ANY)
SMALL_ROWS = 208
LOSS_ROW = 11
MIB = 1 << 20


def _cp(vmem_mib=None):
    if vmem_mib is None:
        return pltpu.CompilerParams()
    return pltpu.CompilerParams(vmem_limit_bytes=vmem_mib * MIB)


def _hbm(*arrays):
    return [pltpu.with_memory_space_constraint(a, pltpu.HBM) for a in arrays]


def _hbm_out(shapes):
    return [pltpu.HBM(s.shape, s.dtype) for s in shapes]


class _Stage:
    def __init__(self, operands, out_shape, alias, sems, start, finish):
        self.operands, self.out_shape, self.alias, self.sems = list(operands), list(out_shape), dict(alias), list(sems)
        self.start, self.finish = start, finish


def _call(body, *, name, grid, in_specs, out_specs, out_shape, args, vmem=None, stages=(), prefetch=None,
          scratch=()):
    nin, nout = len(in_specs), len(out_specs)
    npre = 0 if prefetch is None else 1
    st_args, st_shapes, st_sems, aliases = [], [], list(scratch), {}
    for st in stages:
        for k, v in st.alias.items():
            aliases[npre + nin + len(st_args) + k] = nout + len(st_shapes) + v
        st_args += st.operands
        st_shapes += st.out_shape
        st_sems += st.sems

    def wrapped(*refs):
        pre, refs = refs[:npre], refs[npre:]
        ins, pos = refs[:nin], nin
        st_ins = []
        for st in stages:
            st_ins.append(refs[pos:pos + len(st.operands)])
            pos += len(st.operands)
        outs, pos = refs[pos:pos + nout], pos + nout
        st_outs = []
        for st in stages:
            st_outs.append(refs[pos:pos + len(st.out_shape)])
            pos += len(st.out_shape)
        work, pos = refs[pos:pos + len(scratch)], pos + len(scratch)
        sems = []
        for st in stages:
            sems.append(refs[pos:pos + len(st.sems)])
            pos += len(st.sems)
        if stages:
            first = functools.reduce(jnp.logical_and, [pl.program_id(a) == 0 for a in range(len(grid))])

            @pl.when(first)
            def _():
                for st, a, b, s in zip(stages, st_ins, st_outs, sems):
                    st.start(a, b, s)

        body(*pre, *ins, *outs, *work)
        if stages:
            last = functools.reduce(jnp.logical_and, [pl.program_id(a) == g - 1 for a, g in enumerate(grid)])

            @pl.when(last)
            def _():
                for st, a, b, s in zip(stages, st_ins, st_outs, sems):
                    st.finish(a, b, s)

    all_in = list(in_specs) + [ANY] * len(st_args)
    all_out = list(out_specs) + [ANY] * len(st_shapes)
    kw = dict(has_side_effects=True) if stages else {}
    if vmem is not None:
        kw["vmem_limit_bytes"] = vmem * MIB
    if prefetch is None:
        gkw = dict(grid=grid, in_specs=all_in, out_specs=all_out, scratch_shapes=st_sems)
    else:
        gkw = dict(grid_spec=pltpu.PrefetchScalarGridSpec(
            num_scalar_prefetch=1, grid=grid, in_specs=all_in, out_specs=all_out, scratch_shapes=st_sems))
    res = pl.pallas_call(
        wrapped, name=name, out_shape=_hbm_out(list(out_shape) + st_shapes), input_output_aliases=aliases,
        compiler_params=pltpu.CompilerParams(**kw), **gkw,
    )(*([prefetch] if npre else []), *_hbm(*args, *st_args))
    outs, rest, st_res = list(res[:nout]), list(res[nout:]), []
    for st in stages:
        st_res.append(rest[:len(st.out_shape)])
        rest = rest[len(st.out_shape):]
    return outs, st_res


def _mm(a, b):
    return jnp.dot(a.astype(BF), b.astype(BF), preferred_element_type=F32)


def _mm_nt(a, b):
    return lax.dot_general(a.astype(BF), b.astype(BF), (((1,), (1,)), ((), ())),
                           preferred_element_type=F32)


def _mm_tn(a, b):
    return lax.dot_general(a.astype(BF), b.astype(BF), (((0,), (0,)), ((), ())),
                           preferred_element_type=F32)


def _rows(v):
    return lax.broadcasted_iota(jnp.int32, v.shape, 0)


def _sd(v, s, fill=0.0):
    return jnp.where(_rows(v) >= s, pltpu.roll(v, s, axis=0), fill)


def _su(v, s, fill=0.0):
    n = v.shape[0]
    return jnp.where(_rows(v) < n - s, pltpu.roll(v, n - s, axis=0), fill)


def _sigmoid(z):
    return 1.0 / (1.0 + jnp.exp(-z))


def _softplus(z):
    e = jnp.exp(-jnp.abs(z))
    u = 1.0 + e
    d = u - 1.0
    log1p = jnp.where(d == 0.0, e, jnp.log(u) * (e / jnp.where(d == 0.0, 1.0, d)))
    return jnp.maximum(z, 0.0) + log1p


def _mean(v):
    return jnp.mean(v, axis=-1, keepdims=True)


def _colsum(v):
    return jnp.sum(v, axis=0, keepdims=True)


def _acc(ref, val, first):
    @pl.when(first)
    def _():
        ref[...] = val

    @pl.when(jnp.logical_not(first))
    def _():
        ref[...] += val


def _conv(xp, cw, cb):
    x1, x2, x3 = _sd(xp, 1), _sd(xp, 2), _sd(xp, 3)
    xc = cb + cw[0:1] * x3 + cw[1:2] * x2 + cw[2:3] * x1 + cw[3:4] * xp
    return xc, x1, x2, x3


def _lru_gates(xc, wa, ba, wx, bx, lam):
    xcb = xc.astype(BF)
    r = _sigmoid(_mm(xcb, wa) + ba)
    ii = _sigmoid(_mm(xcb, wx) + bx)
    sp = _softplus(-lam)
    la = (-LRU_C) * r * sp
    a = jnp.exp(la)
    mult = jnp.sqrt(-jnp.tanh(la) * (a * a + 1.0))
    return xcb, r, ii, sp, a, mult


def _gelu_parts(g):
    th = jnp.tanh(GELU_C * (g + 0.044715 * (g * g * g)))
    gel = 0.5 * g * (1.0 + th)
    dgel = 0.5 * (1.0 + th) + 0.5 * g * (1.0 - th * th) * (GELU_C * (1.0 + 3.0 * 0.044715 * (g * g)))
    return gel, dgel


def _tile_scan(a, b, a_s, b_s, out_ref, reverse):
    n = a.shape[0]
    nt = n // 8
    sub = jnp.bitwise_and(_rows(a), 7)
    s = 1
    while s < 8:
        keep = sub < 8 - s if reverse else sub >= s
        amount = n - s if reverse else s
        b = b + a * jnp.where(keep, pltpu.roll(b, amount, axis=0), 0.0)
        a = a * jnp.where(keep, pltpu.roll(a, amount, axis=0), 1.0)
        s *= 2
    a_s[...] = a
    b_s[...] = b
    edge = pl.ds(0 if reverse else 7, nt, stride=8)
    ta, tb = a_s[edge, :], b_s[edge, :]
    shift = _su if reverse else _sd
    s = 1
    while s < nt:
        tb = tb + ta * shift(tb, s, 0.0)
        if 2 * s < nt:
            ta = ta * shift(ta, s, 1.0)
        s *= 2
    enters = shift(tb, 1, 0.0)
    for o in range(8):
        rows = pl.ds(o, nt, stride=8)
        out_ref[rows, :] = b_s[rows, :] + a_s[rows, :] * enters


def _pool_window(x, steps, shift):
    s, sh = x, 1
    for _ in range(steps):
        s = s + shift(s, sh)
        sh *= 2
    return s


def _fwd_inproj_own(x, g1, w_in, slots, stages=()):
    tm = 512

    def body(s_ref, x_ref, g_ref, w_ref, proj_ref, h_ref):
        xv = x_ref[...]
        r = lax.rsqrt(_mean(xv * xv) + NORM_EPS)
        h = ((xv * r) * g_ref[...]).astype(BF)
        h_ref[...] = h
        proj_ref[...] = jnp.dot(h, w_ref[0], preferred_element_type=F32)

    return _call(
        body, name="fwd_inproj_own", grid=(T // tm,), prefetch=slots,
        in_specs=[pl.BlockSpec((tm, D), lambda i, s: (i, 0)),
                  pl.BlockSpec((1, D), lambda i, s: (0, 0)),
                  pl.BlockSpec((1, D, CW_IN), lambda i, s: (s[0], 0, 0))],
        out_specs=[pl.BlockSpec((tm, CW_IN), lambda i, s: (i, s[0])),
                   pl.BlockSpec((tm, D), lambda i, s: (i, 0))],
        out_shape=[jax.ShapeDtypeStruct((T, DIN), F32), jax.ShapeDtypeStruct((T, D), BF)],
        vmem=40, args=[x, g1, w_in], stages=stages)[0]


def _fwd_inproj_rest(h1, w_in, proj, slots):
    tm = 512

    def body(s_ref, h_ref, w_ref, p_in, proj_ref):
        proj_ref[...] = jnp.dot(h_ref[...], w_ref[0], preferred_element_type=F32)

    res = pl.pallas_call(
        body, name="fwd_inproj_rest",
        grid_spec=pltpu.PrefetchScalarGridSpec(
            num_scalar_prefetch=1, grid=(T // tm, NCHIP - 1),
            in_specs=[pl.BlockSpec((tm, D), lambda i, k, s: (i, 0)),
                      pl.BlockSpec((1, D, CW_IN), lambda i, k, s: (s[1 + k], 0, 0)), ANY],
            out_specs=pl.BlockSpec((tm, CW_IN), lambda i, k, s: (i, s[1 + k]))),
        out_shape=pltpu.HBM((T, DIN), F32), input_output_aliases={3: 0},
        compiler_params=_cp(40),
    )(slots, *_hbm(h1, w_in, proj))
    return res


def _vec_spec():
    return pl.BlockSpec((1, CB), lambda j: (0, j))


def _fwd_lru(proj, conv_w, conv_b, wa, ba, wx, bx, lam, stages=()):
    def body(xp_ref, g_ref, cw_ref, cb_ref, wa_ref, ba_ref, wx_ref, bx_ref, lam_ref, y_ref, h_ref, a_s, b_s):
        xc, _, _, _ = _conv(xp_ref[...], cw_ref[...], cb_ref[...])
        _, _, ii, _, a, mult = _lru_gates(xc, wa_ref[0], ba_ref[...], wx_ref[0], bx_ref[...], lam_ref[...])
        _tile_scan(a, mult * (ii * xc), a_s, b_s, h_ref, reverse=False)
        gel, _ = _gelu_parts(g_ref[...])
        y_ref[...] = (h_ref[...] * gel).astype(BF)

    return _call(
        body, name="fwd_lru", grid=(NG,),
        in_specs=[pl.BlockSpec((T, CB), lambda j: (0, j)),
                  pl.BlockSpec((T, CB), lambda j: (0, NG + j)),
                  pl.BlockSpec((4, CB), lambda j: (0, j)),
                  _vec_spec(),
                  pl.BlockSpec((1, CB, CB), lambda j: (j, 0, 0)), _vec_spec(),
                  pl.BlockSpec((1, CB, CB), lambda j: (j, 0, 0)), _vec_spec(),
                  _vec_spec()],
        out_specs=[pl.BlockSpec((T, CB), lambda j: (0, j)), pl.BlockSpec((T, CB), lambda j: (0, j))],
        out_shape=[jax.ShapeDtypeStruct((T, DR), BF), jax.ShapeDtypeStruct((T, DR), F32)],
        vmem=48, args=[proj, proj, conv_w, conv_b, wa, ba, wx, bx, lam], stages=stages,
        scratch=[pltpu.VMEM((T, CB), F32)] * 2)


def _pool_cnt(w):
    t = lax.broadcasted_iota(jnp.int32, (T, 1), 0)
    return jnp.minimum(t + 1, w).astype(F32)


def _fwd_pool(proj, pool_w, pool_scale):
    def body(xp_ref, pw_ref, sc_ref, y_ref):
        for g, w in enumerate(POOL_WINDOWS):
            cols = slice(g * PG, (g + 1) * PG)
            x = xp_ref[:, cols]
            p = _pool_window(x, g + 1, _sd) / _pool_cnt(w) - x
            y_ref[:, cols] = (_mm(p, pw_ref[g]) * sc_ref[:, cols]).astype(BF)

    return pl.pallas_call(
        body, name="fwd_pool", grid=(1,),
        in_specs=[pl.BlockSpec((T, DP), lambda i: (0, 2 * DR // DP)),
                  pl.BlockSpec((4, PG, PG), lambda i: (0, 0, 0)),
                  pl.BlockSpec((1, DP), lambda i: (0, 0))],
        out_specs=pl.BlockSpec((T, DP), lambda i: (0, 0)),
        out_shape=pltpu.HBM((T, DP), BF),
        compiler_params=_cp(48),
    )(*_hbm(proj, pool_w, pool_scale))


GATE_BLK = 512
GATE_BLK0 = (2 * DR + DP) // GATE_BLK


def _gate_specs(tm):
    return [pl.BlockSpec((tm, GATE_BLK), functools.partial(lambda i, q: (i, GATE_BLK0 + q), q=q))
            for q in range(4)]


def _fwd_merge(x, ylru, ypool, proj, b_gate, g2, g3, w_lru_up, w_pool_up, w_o, stages=()):
    tm = 512

    def body(x_ref, yl_ref, yp_ref, p0, p1, p2, p3, bg_ref, g2_ref, g3_ref, wl_ref, wp_ref, wo_ref,
             x2_ref, h2_ref, m_ref, mrg_ref, bra_ref, brb_ref):
        bra = jnp.dot(yl_ref[...], wl_ref[...], preferred_element_type=F32)
        yp = yp_ref[...]
        brb = jnp.concatenate([jnp.dot(yp, wp_ref[k], preferred_element_type=F32) for k in range(NCHIP)], axis=1)
        bg = bg_ref[...]
        ga = _sigmoid(jnp.concatenate([p0[...], p1[...]], axis=1) + bg[:, :D])
        gb = _sigmoid(jnp.concatenate([p2[...], p3[...]], axis=1) + bg[:, D:])
        mrg = (ga * bra + gb * brb).astype(BF)
        m = jnp.dot(mrg, wo_ref[...], preferred_element_type=F32)
        r2 = lax.rsqrt(_mean(m * m) + NORM_EPS)
        x2 = x_ref[...] + (m * r2) * g2_ref[...]
        r3 = lax.rsqrt(_mean(x2 * x2) + NORM_EPS)
        x2_ref[...] = x2
        h2_ref[...] = ((x2 * r3) * g3_ref[...]).astype(BF)
        m_ref[...] = m
        mrg_ref[...] = mrg
        bra_ref[...] = bra.astype(BF)
        brb_ref[...] = brb.astype(BF)

    row = lambda w: pl.BlockSpec((tm, w), lambda i: (i, 0))
    full2 = lambda a, b: pl.BlockSpec((a, b), lambda i: (0, 0))
    return _call(
        body, name="fwd_merge", grid=(T // tm,),
        in_specs=[row(D), row(DR), row(DP)] + _gate_specs(tm) +
                 [full2(1, 2 * D), full2(1, D), full2(1, D), full2(DR, D),
                  pl.BlockSpec((NCHIP, DP, D // NCHIP), lambda i: (0, 0, 0)), full2(D, D)],
        out_specs=[row(D)] * 6,
        out_shape=[jax.ShapeDtypeStruct((T, D), F32), jax.ShapeDtypeStruct((T, D), BF),
                   jax.ShapeDtypeStruct((T, D), F32), jax.ShapeDtypeStruct((T, D), BF),
                   jax.ShapeDtypeStruct((T, D), BF), jax.ShapeDtypeStruct((T, D), BF)],
        vmem=48, args=[x, ylru, ypool, proj, proj, proj, proj, b_gate, g2, g3, w_lru_up, w_pool_up, w_o],
        stages=stages)


def _fwd_mlp(h2, w_ff1, w_ff2):
    tm = 512
    fk = DF // NCHIP

    def body(h_ref, w1_ref, w2_ref, a1_ref, f_ref):
        h = h_ref[...]
        f = None
        for k in range(NCHIP):
            a1 = jnp.maximum(jnp.dot(h, w1_ref[k], preferred_element_type=F32), 0.0)
            a1_ref[:, k * fk:(k + 1) * fk] = a1.astype(BF)
            part = jnp.dot((a1 * a1).astype(BF), w2_ref[k * fk:(k + 1) * fk, :], preferred_element_type=F32)
            f = part if f is None else f + part
        f_ref[...] = f

    return pl.pallas_call(
        body, name="fwd_mlp", grid=(T // tm,),
        in_specs=[pl.BlockSpec((tm, D), lambda i: (i, 0)),
                  pl.BlockSpec((NCHIP, D, fk), lambda i: (0, 0, 0)),
                  pl.BlockSpec((DF, D), lambda i: (0, 0))],
        out_specs=[pl.BlockSpec((tm, DF), lambda i: (i, 0)), pl.BlockSpec((tm, D), lambda i: (i, 0))],
        out_shape=_hbm_out([jax.ShapeDtypeStruct((T, DF), BF), jax.ShapeDtypeStruct((T, D), F32)]),
        compiler_params=_cp(56),
    )(*_hbm(h2, w_ff1, w_ff2))


def _loss_head(f, x2, target, g4):
    tm = 512

    def body(f_ref, x2_ref, t_ref, g_ref, loss_ref, dy_ref, df_ref, dg_ref):
        first = pl.program_id(0) == 0
        f = f_ref[...]
        g4v = g_ref[...]
        r4 = lax.rsqrt(_mean(f * f) + NORM_EPS)
        fn = f * r4
        e = (x2_ref[...] + fn * g4v) - t_ref[...]
        _acc(loss_ref, jnp.sum(_mean(e * e), axis=0, keepdims=True), first)
        dy = e * (1.0 / D)
        dy_ref[...] = dy
        _acc(dg_ref, _colsum(dy * fn), first)
        dfn = dy * g4v
        df_ref[...] = (r4 * (dfn - fn * _mean(dfn * fn))).astype(BF)

    row = pl.BlockSpec((tm, D), lambda i: (i, 0))
    return pl.pallas_call(
        body, name="loss_head", grid=(T // tm,),
        in_specs=[row, row, row, pl.BlockSpec((1, D), lambda i: (0, 0))],
        out_specs=[pl.BlockSpec((1, 1), lambda i: (0, 0)), row, row, pl.BlockSpec((1, D), lambda i: (0, 0))],
        out_shape=_hbm_out([jax.ShapeDtypeStruct((1, 1), F32), jax.ShapeDtypeStruct((T, D), F32),
                            jax.ShapeDtypeStruct((T, D), BF), jax.ShapeDtypeStruct((1, D), F32)]),
        compiler_params=_cp(48),
    )(*_hbm(f, x2, target, g4))


def _bwd_mlp_x(df, a1, w_ff1, w_ff2):
    tm = 512
    fk = DF // NCHIP

    def body(df_ref, a1_ref, w1_ref, w2_ref, dh_ref, df1_ref):
        df = df_ref[...]
        dh = None
        for k in range(NCHIP):
            cols = slice(k * fk, (k + 1) * fk)
            dact = _mm_nt(df, w2_ref[cols, :])
            df1 = (dact * (2.0 * a1_ref[:, cols].astype(F32))).astype(BF)
            df1_ref[:, cols] = df1
            part = _mm_nt(df1, w1_ref[k])
            dh = part if dh is None else dh + part
        dh_ref[...] = dh

    return pl.pallas_call(
        body, name="bwd_mlp_x", grid=(T // tm,),
        in_specs=[pl.BlockSpec((tm, D), lambda i: (i, 0)),
                  pl.BlockSpec((tm, DF), lambda i: (i, 0)),
                  pl.BlockSpec((NCHIP, D, fk), lambda i: (0, 0, 0)),
                  pl.BlockSpec((DF, D), lambda i: (0, 0))],
        out_specs=[pl.BlockSpec((tm, D), lambda i: (i, 0)), pl.BlockSpec((tm, DF), lambda i: (i, 0))],
        out_shape=_hbm_out([jax.ShapeDtypeStruct((T, D), F32), jax.ShapeDtypeStruct((T, DF), BF)]),
        compiler_params=_cp(56),
    )(*_hbm(df, a1, w_ff1, w_ff2))


def _bwd_mlp_w(df, h2, a1, df1):
    fc = 512
    per = (DF // NCHIP) // fc

    def body(df_ref, h_ref, a1_ref, df1_ref, dw1_ref, dw2_ref):
        a1 = a1_ref[...].astype(F32)
        dw2_ref[...] = _mm_tn((a1 * a1).astype(BF), df_ref[...]).astype(BF)
        dw1_ref[0] = _mm_tn(h_ref[...], df1_ref[...]).astype(BF)

    return pl.pallas_call(
        body, name="bwd_mlp_w", grid=(DF // fc,),
        in_specs=[pl.BlockSpec((T, D), lambda j: (0, 0)),
                  pl.BlockSpec((T, D), lambda j: (0, 0)),
                  pl.BlockSpec((T, fc), lambda j: (0, j)),
                  pl.BlockSpec((T, fc), lambda j: (0, j))],
        out_specs=[pl.BlockSpec((1, D, fc), lambda j: (j // per, 0, j % per)),
                   pl.BlockSpec((fc, D), lambda j: (j, 0))],
        out_shape=_hbm_out([jax.ShapeDtypeStruct((NCHIP, D, DF // NCHIP), BF),
                            jax.ShapeDtypeStruct((DF, D), BF)]),
        compiler_params=_cp(56),
    )(*_hbm(df, h2, a1, df1))


def _bwd_merge(dh2, dy, x2, m, bra, brb, proj, b_gate, g2, g3, w_lru_up, w_pool_up, w_o, stages=()):
    tm = 256
    cpu = D // NCHIP

    def body(dh2_ref, dy_ref, x2_ref, m_ref, bra_ref, brb_ref, p0, p1, p2, p3, bg_ref,
             g2_ref, g3_ref, wl_ref, wp_ref, wo_ref,
             dx_ref, dgt_ref, dyl_ref, dyp_ref, dm_ref, dbra_ref, dbrb_ref, dg2_ref, dg3_ref, dbg_ref):
        first = pl.program_id(0) == 0
        x2 = x2_ref[...]
        r3 = lax.rsqrt(_mean(x2 * x2) + NORM_EPS)
        x2n = x2 * r3
        dh2 = dh2_ref[...]
        t3 = dh2 * g3_ref[...]
        dx2 = dy_ref[...] + r3 * (t3 - x2n * _mean(t3 * x2n))
        dx_ref[...] = dx2
        _acc(dg3_ref, _colsum(dh2 * x2n), first)
        m = m_ref[...]
        r2 = lax.rsqrt(_mean(m * m) + NORM_EPS)
        mn = m * r2
        _acc(dg2_ref, _colsum(dx2 * mn), first)
        dmn = dx2 * g2_ref[...]
        dm = (r2 * (dmn - mn * _mean(dmn * mn))).astype(BF)
        dm_ref[...] = dm
        dmrg = _mm_nt(dm, wo_ref[...])
        bg = bg_ref[...]
        ga = _sigmoid(jnp.concatenate([p0[...], p1[...]], axis=1) + bg[:, :D])
        gb = _sigmoid(jnp.concatenate([p2[...], p3[...]], axis=1) + bg[:, D:])
        dga = dmrg * bra_ref[...].astype(F32) * (ga * (1.0 - ga))
        dgb = dmrg * brb_ref[...].astype(F32) * (gb * (1.0 - gb))
        dgt_ref[:, :D] = dga.astype(BF)
        dgt_ref[:, D:] = dgb.astype(BF)
        _acc(dbg_ref, jnp.concatenate([_colsum(dga), _colsum(dgb)], axis=1), first)
        dbra = (dmrg * ga).astype(BF)
        dbrb = (dmrg * gb).astype(BF)
        dbra_ref[...] = dbra
        dbrb_ref[...] = dbrb
        dyl_ref[...] = _mm_nt(dbra, wl_ref[...])
        dyp = None
        for k in range(NCHIP):
            part = _mm_nt(dbrb[:, k * cpu:(k + 1) * cpu], wp_ref[k])
            dyp = part if dyp is None else dyp + part
        dyp_ref[...] = dyp

    row = lambda w: pl.BlockSpec((tm, w), lambda i: (i, 0))
    full2 = lambda a, b: pl.BlockSpec((a, b), lambda i: (0, 0))
    wp_spec = pl.BlockSpec((NCHIP, DP, cpu), lambda i: (0, 0, 0))
    return _call(
        body, name="bwd_merge", grid=(T // tm,),
        in_specs=[row(D)] * 6 + _gate_specs(tm) +
                 [full2(1, 2 * D), full2(1, D), full2(1, D), full2(DR, D), wp_spec, full2(D, D)],
        out_specs=[row(D), row(2 * D), row(DR), row(DP), row(D), row(D), row(D),
                   full2(1, D), full2(1, D), full2(1, 2 * D)],
        out_shape=[jax.ShapeDtypeStruct((T, D), F32), jax.ShapeDtypeStruct((T, 2 * D), BF),
                   jax.ShapeDtypeStruct((T, DR), F32), jax.ShapeDtypeStruct((T, DP), F32),
                   jax.ShapeDtypeStruct((T, D), BF), jax.ShapeDtypeStruct((T, D), BF),
                   jax.ShapeDtypeStruct((T, D), BF),
                   jax.ShapeDtypeStruct((1, D), F32), jax.ShapeDtypeStruct((1, D), F32),
                   jax.ShapeDtypeStruct((1, 2 * D), F32)],
        vmem=56, args=[dh2, dy, x2, m, bra, brb, proj, proj, proj, proj, b_gate, g2, g3, w_lru_up, w_pool_up, w_o],
        stages=stages)


def _dw_merge(mrg, dm, ylru, dbra, ypool, dbrb, stages=()):
    nb = NCHIP
    rb, pb, cpu = D // nb, DP // nb, D // NCHIP

    def body(mrg_ref, dm_ref, yl_ref, dbra_ref, yp_ref, dbrb_ref, dwo_ref, dwl_ref, dwp_ref):
        dwo_ref[...] = _mm_tn(mrg_ref[...], dm_ref[...]).astype(BF)
        dwl_ref[...] = _mm_tn(yl_ref[...], dbra_ref[...]).astype(BF)
        dwp = _mm_tn(yp_ref[...], dbrb_ref[...]).astype(BF)
        for k in range(NCHIP):
            dwp_ref[k] = dwp[:, k * cpu:(k + 1) * cpu]

    cols = lambda w: pl.BlockSpec((T, w), lambda r: (0, r))
    whole = pl.BlockSpec((T, D), lambda r: (0, 0))
    return _call(
        body, name="dw_merge", grid=(nb,),
        in_specs=[cols(rb), whole, cols(rb), whole, cols(pb), whole],
        out_specs=[pl.BlockSpec((rb, D), lambda r: (r, 0)), pl.BlockSpec((rb, D), lambda r: (r, 0)),
                   pl.BlockSpec((NCHIP, pb, cpu), lambda r: (0, r, 0))],
        out_shape=[jax.ShapeDtypeStruct((D, D), BF), jax.ShapeDtypeStruct((DR, D), BF),
                   jax.ShapeDtypeStruct((NCHIP, DP, cpu), BF)],
        vmem=56, args=[mrg, dm, ylru, dbra, ypool, dbrb], stages=stages)


def _bwd_lru(proj, h, dylru, conv_w, conv_b, wa, ba, wx, bx, lam, stages=()):
    def body(xp_ref, g_ref, h_ref, dy_ref, cw_ref, cb_ref, wa_ref, ba_ref, wx_ref, bx_ref, lam_ref,
             dxp_ref, dg_ref, dcw_ref, dcb_ref, dwa_ref, dba_ref, dwx_ref, dbx_ref, dlam_ref, a_s, b_s, l_s):
        xp = xp_ref[...]
        cw = cw_ref[...]
        lam = lam_ref[...]
        xc, x1, x2, x3 = _conv(xp, cw, cb_ref[...])
        wa, wx = wa_ref[0], wx_ref[0]
        xcb, r, ii, sp, a, mult = _lru_gates(xc, wa, ba_ref[...], wx, bx_ref[...], lam)
        g = g_ref[...]
        gel, dgel = _gelu_parts(g)
        h = h_ref[...]
        dy = dy_ref[...]
        dg_ref[...] = (dy * h * dgel).astype(BF)
        _tile_scan(_su(a, 1, 0.0), dy * gel, a_s, b_s, l_s, reverse=True)
        b = l_s[...]
        da = b * _sd(h, 1, 0.0)
        dmult = b * (ii * xc)
        dii = b * (mult * xc)
        dxc = b * (mult * ii)
        dla = da * a - dmult * ((a * a) / mult)
        dr = dla * ((-LRU_C) * sp)
        dsp = _colsum(dla * ((-LRU_C) * r))
        dlam_ref[...] = -dsp / (1.0 + jnp.exp(lam))
        dzr = dr * (r * (1.0 - r))
        dzi = dii * (ii * (1.0 - ii))
        dzrb, dzib = dzr.astype(BF), dzi.astype(BF)
        dxc = dxc + _mm_nt(dzrb, wa) + _mm_nt(dzib, wx)
        dwa_ref[0] = _mm_tn(xcb, dzrb)
        dwx_ref[0] = _mm_tn(xcb, dzib)
        dba_ref[...] = _colsum(dzr)
        dbx_ref[...] = _colsum(dzi)
        dcb_ref[...] = _colsum(dxc)
        dcw_ref[...] = jnp.concatenate([_colsum(dxc * x3), _colsum(dxc * x2), _colsum(dxc * x1),
                                        _colsum(dxc * xp)], axis=0)
        dxp = cw[3:4] * dxc + cw[2:3] * _su(dxc, 1) + cw[1:2] * _su(dxc, 2) + cw[0:1] * _su(dxc, 3)
        dxp_ref[...] = dxp.astype(BF)

    blk = pl.BlockSpec((T, CB), lambda j: (0, j))
    wsp = pl.BlockSpec((1, CB, CB), lambda j: (j, 0, 0))
    return _call(
        body, name="bwd_lru", grid=(NG,),
        in_specs=[blk, pl.BlockSpec((T, CB), lambda j: (0, NG + j)), blk, blk,
                  pl.BlockSpec((4, CB), lambda j: (0, j)), _vec_spec(), wsp, _vec_spec(), wsp, _vec_spec(),
                  _vec_spec()],
        out_specs=[blk, blk, pl.BlockSpec((4, CB), lambda j: (0, j)), _vec_spec(), wsp, _vec_spec(), wsp,
                   _vec_spec(), _vec_spec()],
        out_shape=[jax.ShapeDtypeStruct((T, DR), BF), jax.ShapeDtypeStruct((T, DR), BF),
                   jax.ShapeDtypeStruct((4, DR), F32), jax.ShapeDtypeStruct((1, DR), F32),
                   jax.ShapeDtypeStruct((NG, CB, CB), F32), jax.ShapeDtypeStruct((1, DR), F32),
                   jax.ShapeDtypeStruct((NG, CB, CB), F32), jax.ShapeDtypeStruct((1, DR), F32),
                   jax.ShapeDtypeStruct((1, DR), F32)],
        vmem=56, args=[proj, proj, h, dylru, conv_w, conv_b, wa, ba, wx, bx, lam], stages=stages,
        scratch=[pltpu.VMEM((T, CB), F32)] * 3)


def _bwd_pool(proj, dypool, pool_w, pool_scale):
    def body(xp_ref, dy_ref, pw_ref, sc_ref, dx_ref, dw_ref, dsc_ref):
        for g, w in enumerate(POOL_WINDOWS):
            cols = slice(g * PG, (g + 1) * PG)
            cnt = _pool_cnt(w)
            x = xp_ref[:, cols]
            pb = (_pool_window(x, g + 1, _sd) / cnt - x).astype(BF)
            wg = pw_ref[g]
            dy = dy_ref[:, cols]
            dsc_ref[:, cols] = _colsum(dy * _mm(pb, wg))
            dyp = (dy * sc_ref[:, cols]).astype(BF)
            dw_ref[g] = _mm_tn(pb, dyp)
            dp = _mm_nt(dyp, wg)
            dx_ref[:, cols] = (_pool_window(dp / cnt, g + 1, _su) - dp).astype(BF)

    return pl.pallas_call(
        body, name="bwd_pool", grid=(1,),
        in_specs=[pl.BlockSpec((T, DP), lambda i: (0, 2 * DR // DP)),
                  pl.BlockSpec((T, DP), lambda i: (0, 0)),
                  pl.BlockSpec((4, PG, PG), lambda i: (0, 0, 0)),
                  pl.BlockSpec((1, DP), lambda i: (0, 0))],
        out_specs=[pl.BlockSpec((T, DP), lambda i: (0, 0)),
                   pl.BlockSpec((4, PG, PG), lambda i: (0, 0, 0)),
                   pl.BlockSpec((1, DP), lambda i: (0, 0))],
        out_shape=_hbm_out([jax.ShapeDtypeStruct((T, DP), BF), jax.ShapeDtypeStruct((4, PG, PG), F32),
                            jax.ShapeDtypeStruct((1, DP), F32)]),
        compiler_params=_cp(48),
    )(*_hbm(proj, dypool, pool_w, pool_scale))


def _bwd_inproj(h1, dproj, w_in, stages=()):
    def body(h_ref, dp_ref, w_ref, dw_ref, dh_ref):
        dp = dp_ref[...]
        dw_ref[0] = _mm_tn(h_ref[...], dp).astype(BF)
        _acc(dh_ref, _mm_nt(dp, w_ref[0]), pl.program_id(0) == 0)

    return _call(
        body, name="bwd_inproj", grid=(NCHIP,),
        in_specs=[pl.BlockSpec((T, D), lambda k: (0, 0)),
                  pl.BlockSpec((T, CW_IN), lambda k: (0, k)),
                  pl.BlockSpec((1, D, CW_IN), lambda k: (k, 0, 0))],
        out_specs=[pl.BlockSpec((1, D, CW_IN), lambda k: (k, 0, 0)), pl.BlockSpec((T, D), lambda k: (0, 0))],
        out_shape=[jax.ShapeDtypeStruct((NCHIP, D, CW_IN), BF), jax.ShapeDtypeStruct((T, D), F32)],
        vmem=56, args=[h1, dproj, w_in], stages=stages)


def _bwd_prenorm(x, dh1, dxres, g1, stages=()):
    tm = 512

    def body(x_ref, dh_ref, dr_ref, g_ref, dx_ref, dg_ref):
        xv = x_ref[...]
        r = lax.rsqrt(_mean(xv * xv) + NORM_EPS)
        xn = xv * r
        dh = dh_ref[...]
        t = dh * g_ref[...]
        dx_ref[...] = dr_ref[...] + r * (t - xn * _mean(t * xn))
        _acc(dg_ref, _colsum(dh * xn), pl.program_id(0) == 0)

    row = pl.BlockSpec((tm, D), lambda i: (i, 0))
    vec = pl.BlockSpec((1, D), lambda i: (0, 0))
    return _call(
        body, name="bwd_prenorm", grid=(T // tm,),
        in_specs=[row, row, row, vec], out_specs=[row, vec],
        out_shape=[jax.ShapeDtypeStruct((T, D), F32), jax.ShapeDtypeStruct((1, D), F32)],
        vmem=48, args=[x, dh1, dxres, g1], stages=stages)


def _place():
    x, y, c = lax.axis_index("x"), lax.axis_index("y"), lax.axis_index("c")
    chips = [(1 - x, y), (x, 1 - y), (1 - x, 1 - y)]
    return x, y, c, chips


def _rcopy(src, dst, ssem, rsem, dev):
    return pltpu.make_async_remote_copy(src_ref=src, dst_ref=dst, send_sem=ssem, recv_sem=rsem,
                                        device_id=dev, device_id_type=MESH_ID)


def _sds(a):
    return jax.ShapeDtypeStruct(a.shape, a.dtype)


def _sem2(n, m):
    return [pltpu.SemaphoreType.DMA((n * m,)), pltpu.SemaphoreType.DMA((n * m,))]


ALL = (0, 1, 1)


def _piece(ref, k, half, part):
    hr = ref.shape[1] // 2
    r0, r1 = hr * part[0] // part[2], hr * part[1] // part[2]
    return ref.at[k, pl.ds(half * hr + r0, r1 - r0), :]


def _gather(fulls, ici=(), d2d=()):
    n = len(fulls)
    ici, d2d = list(ici), list(d2d)
    pieces = [("ici", i, part) for i, part in ici] + [("d2d", i, part) for i, part in d2d]

    def copies(outs, sems):
        x, y, c, chips = _place()
        me = 2 * x + y
        sib = (x, y, 1 - c)
        send, recv = [], []
        for q, (kind, i, part) in enumerate(pieces):
            for j, chip in enumerate(chips):
                k, s = 2 * chip[0] + chip[1], 3 * q + j
                if kind == "ici":
                    mine, theirs, dev = _piece(outs[i], me, c, part), _piece(outs[i], k, c, part), (*chip, c)
                else:
                    mine, theirs, dev = _piece(outs[i], k, c, part), _piece(outs[i], k, 1 - c, part), sib
                send.append(_rcopy(mine, mine, sems[0].at[s], sems[1].at[s], dev))
                recv.append(_rcopy(theirs, theirs, sems[0].at[s], sems[1].at[s], dev))
        return send, recv

    def start(ins, outs, sems):
        for cp in copies(outs, sems)[0]:
            cp.start()

    def finish(ins, outs, sems):
        send, recv = copies(outs, sems)
        for cp in recv:
            cp.wait_recv()
        for cp in send:
            cp.wait_send()

    sems = [pltpu.SemaphoreType.DMA((3 * len(pieces),)), pltpu.SemaphoreType.DMA((3 * len(pieces),))]
    return _Stage(fulls, [_sds(f) for f in fulls], {i: i for i in range(n)}, sems, start, finish)


def _gather_whole(v):
    def copies(ins, outs, sems):
        x, y, c, chips = _place()
        me = 2 * x + y
        send = [_rcopy(ins[0], outs[0].at[me], sems[0].at[j], sems[1].at[j], (*chip, c))
                for j, chip in enumerate(chips)]
        recv = [_rcopy(ins[0], outs[0].at[2 * chip[0] + chip[1]], sems[0].at[j], sems[1].at[j], (*chip, c))
                for j, chip in enumerate(chips)]
        return send, recv

    def start(ins, outs, sems):
        for cp in copies(ins, outs, sems)[0]:
            cp.start()

    def finish(ins, outs, sems):
        send, recv = copies(ins, outs, sems)
        for cp in recv:
            cp.wait_recv()
        for cp in send:
            cp.wait_send()

    return _Stage([v], [jax.ShapeDtypeStruct((NCHIP,) + v.shape, v.dtype)], {},
                  [pltpu.SemaphoreType.DMA((3,)), pltpu.SemaphoreType.DMA((3,))], start, finish)


def _to_sibling(srcs):
    n = len(srcs)

    def copies(ins, outs, sems):
        x, y, c, _ = _place()
        sib = (x, y, 1 - c)
        return [_rcopy(ins[i].at[:, 1 - c] if srcs[i].ndim == 4 else ins[i], outs[i], sems[0].at[i], sems[1].at[i], sib)
                for i in range(n)]

    def start(ins, outs, sems):
        for cp in copies(ins, outs, sems):
            cp.start()

    def finish(ins, outs, sems):
        for cp in copies(ins, outs, sems):
            cp.wait()

    shapes = [jax.ShapeDtypeStruct((NCHIP,) + s.shape[2:] if s.ndim == 4 else s.shape, s.dtype) for s in srcs]
    return _Stage(srcs, shapes, {}, [pltpu.SemaphoreType.DMA((n,)), pltpu.SemaphoreType.DMA((n,))], start, finish)


def _to_chips(srcs, parts=None, lands=None):
    n = len(srcs)
    parts = [ALL] * n if parts is None else parts
    lands = [None] * n if lands is None else lands
    given = [i for i in range(n) if lands[i] is not None]

    def rows(ref, i):
        hr = srcs[i].shape[1]
        r0, r1 = hr * parts[i][0] // parts[i][2], hr * parts[i][1] // parts[i][2]
        return ref.at[pl.ds(r0, r1 - r0), :]

    def copies(ins, outs, sems):
        x, y, c, chips = _place()
        me = 2 * x + y
        return [_rcopy(rows(ins[i].at[2 * chip[0] + chip[1]] if srcs[i].shape[0] == NCHIP else ins[i].at[c], i),
                       rows(outs[i].at[me], i), sems[0].at[3 * i + j], sems[1].at[3 * i + j], (*chip, c))
                for i in range(n) for j, chip in enumerate(chips)]

    def start(ins, outs, sems):
        for cp in copies(ins, outs, sems):
            cp.start()

    def finish(ins, outs, sems):
        for cp in copies(ins, outs, sems):
            cp.wait()

    shapes = [jax.ShapeDtypeStruct((NCHIP,) + s.shape[1:], s.dtype) for s in srcs]
    alias = {n + q: i for q, i in enumerate(given)}
    return _Stage(list(srcs) + [lands[i] for i in given], shapes, alias, _sem2(n, 3), start, finish)


HBM_REF = pl.BlockSpec(memory_space=pltpu.HBM)
SEM_REF = pl.BlockSpec(memory_space=pltpu.SEMAPHORE)
DATAFLOW = pltpu.SideEffectType.DATAFLOW_SIDE_EFFECTING


def _after(x):
    return _Stage([x], [], {}, [], lambda *a: None, lambda *a: None)


class _Flight:
    def __init__(self, stage, sems, bufs):
        self.stage, self.sems, self.bufs = stage, list(sems), list(bufs)

    def landed(self):
        st, n = self.stage, len(self.stage.operands)
        fresh = [j for j in range(len(st.out_shape)) if j not in st.alias.values()]
        back = {v: k for k, v in st.alias.items()}
        return [self.bufs[back[j]] if j in back else self.bufs[n + fresh.index(j)] for j in range(len(st.out_shape))]


def _split_call(name, finish=(), start=(), after=None):
    bufs, stage_bufs = [], []

    def slot(a):
        for i, b in enumerate(bufs):
            if b is a:
                return i
        bufs.append(a)
        return len(bufs) - 1

    fin_slots = [[slot(b) for b in fl.bufs] for fl in finish]
    for st in start:
        fresh = [lax.empty(o.shape, o.dtype) for j, o in enumerate(st.out_shape) if j not in st.alias.values()]
        stage_bufs.append([slot(a) for a in list(st.operands) + fresh])
    old_sems = [s for fl in finish for s in fl.sems]
    new_sems = [s for st in start for s in st.sems]
    nb, no, nn = len(bufs), len(old_sems), len(new_sems)

    def refs_of(st, slots, buf_refs):
        n = len(st.operands)
        ins = [buf_refs[i] for i in slots[:n]]
        fresh = [j for j in range(len(st.out_shape)) if j not in st.alias.values()]
        back = {v: k for k, v in st.alias.items()}
        outs = [ins[back[j]] if j in back else buf_refs[slots[n + fresh.index(j)]] for j in range(len(st.out_shape))]
        return ins, outs

    def body(*refs):
        buf_refs, sem_in = refs[:nb], refs[nb:nb + no]
        sem_out = refs[nb + no + (after is not None):][:nn]
        token = refs[-1]
        pos = 0
        for fl, slots in zip(finish, fin_slots):
            ins, outs = refs_of(fl.stage, slots, buf_refs)
            fl.stage.finish(ins, outs, sem_in[pos:pos + len(fl.sems)])
            pos += len(fl.sems)
        pos = 0
        for st, slots in zip(start, stage_bufs):
            ins, outs = refs_of(st, slots, buf_refs)
            st.start(ins, outs, sem_out[pos:pos + len(st.sems)])
            pos += len(st.sems)
        token[...] = jnp.zeros_like(token)

    res = pl.pallas_call(
        body, name=name,
        out_shape=tuple(new_sems) + tuple(pltpu.HBM(b.shape, b.dtype) for b in bufs) +
                  (jax.ShapeDtypeStruct((8, LANE), F32),),
        in_specs=(HBM_REF,) * nb + (SEM_REF,) * no + ((pl.BlockSpec(memory_space=pl.ANY),) if after is not None else ()),
        out_specs=(SEM_REF,) * nn + (HBM_REF,) * nb + (pl.BlockSpec(memory_space=pltpu.VMEM),),
        input_output_aliases={i: nn + i for i in range(nb)},
        compiler_params=pltpu.CompilerParams(has_side_effects=DATAFLOW),
    )(*_hbm(*bufs), *old_sems, *([after] if after is not None else []))
    sems, thru, token = res[:nn], res[nn:nn + nb], res[-1]
    for fl, slots in zip(finish, fin_slots):
        fl.bufs = [thru[i] for i in slots]
    flights, pos = [], 0
    for st, slots in zip(start, stage_bufs):
        flights.append(_Flight(st, sems[pos:pos + len(st.sems)], [thru[i] for i in slots]))
        pos += len(st.sems)
    return flights, token


def _last_copies(p_ref, land_ref, ssem, rsem):
    x, y, c, chips = _place()
    me = 2 * x + y
    send = [_rcopy(p_ref.at[2 * chip[0] + chip[1]], land_ref.at[me], ssem.at[j], rsem.at[j], (*chip, c))
            for j, chip in enumerate(chips)]
    recv = [_rcopy(p_ref.at[2 * chip[0] + chip[1]], land_ref.at[2 * chip[0] + chip[1]], ssem.at[j], rsem.at[j],
                   (*chip, c)) for j, chip in enumerate(chips)]
    return send, recv


def _chips_start(p):
    def body(p_ref, land_ref, ssem, rsem, p_thru, land_thru, token):
        for cp in _last_copies(p_ref, land_ref, ssem, rsem)[0]:
            cp.start()
        token[...] = jnp.zeros_like(token)

    return pl.pallas_call(
        body, name="reduce_last_start",
        out_shape=(pltpu.SemaphoreType.DMA((3,)), pltpu.SemaphoreType.DMA((3,)), pltpu.HBM(p.shape, p.dtype),
                   pltpu.HBM(p.shape, p.dtype), jax.ShapeDtypeStruct((8, LANE), F32)),
        in_specs=(HBM_REF, HBM_REF),
        out_specs=(SEM_REF, SEM_REF, HBM_REF, HBM_REF, pl.BlockSpec(memory_space=pltpu.VMEM)),
        input_output_aliases={0: 2, 1: 3},
        compiler_params=pltpu.CompilerParams(has_side_effects=DATAFLOW),
    )(*_hbm(p, lax.empty(p.shape, p.dtype)))


def _chips_wait(ssem, rsem, p_thru, land_thru, after):
    def body(p_ref, land_ref, ssem, rsem, after_ref, p_dead, got_ref):
        send, recv = _last_copies(p_ref, land_ref, ssem, rsem)
        for cp in send:
            cp.wait_send()
        for cp in recv:
            cp.wait_recv()

    return pl.pallas_call(
        body, name="reduce_last_wait",
        out_shape=(pltpu.HBM(p_thru.shape, p_thru.dtype), pltpu.HBM(land_thru.shape, land_thru.dtype)),
        in_specs=(HBM_REF, HBM_REF, SEM_REF, SEM_REF, pl.BlockSpec(memory_space=pl.ANY)),
        out_specs=(HBM_REF, HBM_REF), input_output_aliases={0: 0, 1: 1},
        compiler_params=pltpu.CompilerParams(has_side_effects=DATAFLOW),
    )(p_thru, land_thru, ssem, rsem, after)


def _share(pairs):
    n = len(pairs)

    def start(ins, outs, sems):
        x, y, c, _ = _place()
        for i in range(n):
            _rcopy(outs[i].at[c], outs[i].at[c], sems[0].at[i], sems[1].at[i], (x, y, 1 - c)).start()

    def finish(ins, outs, sems):
        x, y, c, _ = _place()
        for i in range(n):
            _rcopy(outs[i].at[c], outs[i].at[c], sems[0].at[i], sems[1].at[i], (x, y, 1 - c)).wait_send()
            _rcopy(outs[i].at[1 - c], outs[i].at[1 - c], sems[0].at[i], sems[1].at[i], (x, y, 1 - c)).wait_recv()

    return _Stage(pairs, [_sds(p) for p in pairs], {i: i for i in range(n)},
                  [pltpu.SemaphoreType.DMA((n,)), pltpu.SemaphoreType.DMA((n,))], start, finish)


def _row_block(rows, cols, itemsize=4, target=2 * MIB):
    br = rows
    while br * cols * itemsize > target and br % 32 == 0:
        br //= 2
    return br


def _cast_place(w, chip_idx, name):
    rows, cols = w.shape
    br = _row_block(rows, cols)

    def body(k_ref, w_ref, o_ref):
        o_ref[0] = w_ref[...].astype(BF)

    return _call(
        body, name=name, grid=(rows // br,), prefetch=chip_idx,
        in_specs=[pl.BlockSpec((br, cols), lambda r, k: (r, 0))],
        out_specs=[pl.BlockSpec((1, br, cols), lambda r, k: (k[0], r, 0))],
        out_shape=[jax.ShapeDtypeStruct((NCHIP, rows, cols), BF)], vmem=32, args=[w])[0][0]


def _cast_place_multi(ws, chip_idx, stages=()):
    br = 128
    nblk = [a.shape[0] // br for a in ws]
    starts = [sum(nblk[:i]) for i in range(len(ws))]

    def body(k_ref, *refs):
        r = pl.program_id(0)
        for i in range(len(ws)):
            @pl.when(jnp.logical_and(r >= starts[i], r < starts[i] + nblk[i]))
            def _(i=i):
                refs[len(ws) + i][0] = refs[i][...].astype(BF)

    def at(i):
        return functools.partial(lambda r, s, nb: jnp.clip(r - s, 0, nb - 1), s=starts[i], nb=nblk[i])

    outs, landed = _call(
        body, name="cast_rest", grid=(sum(nblk),), prefetch=chip_idx,
        in_specs=[pl.BlockSpec((br, a.shape[1]), functools.partial(lambda r, k, f: (f(r), 0), f=at(i)))
                  for i, a in enumerate(ws)],
        out_specs=[pl.BlockSpec((1, br, a.shape[1]), functools.partial(lambda r, k, f: (k[0], f(r), 0), f=at(i)))
                   for i, a in enumerate(ws)],
        out_shape=[jax.ShapeDtypeStruct((NCHIP,) + a.shape, BF) for a in ws], vmem=32, args=list(ws), stages=stages)
    return outs, landed


def _add_sibling(g, land, cidx, name, stages=()):
    _, _, hr, cols = g.shape
    br = _row_block(hr, cols)

    def body(c_ref, g_ref, l_ref, o_ref):
        o_ref[...] = (g_ref[0, 0].astype(F32) + l_ref[0].astype(F32)).astype(BF)[None]

    outs, st = _call(
        body, name=name, grid=(NCHIP, hr // br), prefetch=cidx,
        in_specs=[pl.BlockSpec((1, 1, br, cols), lambda k, r, c: (k, c[0], r, 0)),
                  pl.BlockSpec((1, br, cols), lambda k, r, c: (k, r, 0))],
        out_specs=[pl.BlockSpec((1, br, cols), lambda k, r, c: (k, r, 0))],
        out_shape=[jax.ShapeDtypeStruct((NCHIP, hr, cols), BF)], vmem=32, args=[g, land], stages=stages)
    return outs[0], st


def _add_sibling_multi(gs, lands, cidx, name):
    n = len(gs)
    brs = [_row_block(g.shape[2], g.shape[3]) for g in gs]
    nrb = [g.shape[2] // b for g, b in zip(gs, brs)]
    nblk = [NCHIP * q for q in nrb]
    starts = [sum(nblk[:i]) for i in range(n)]

    def body(c_ref, *refs):
        r = pl.program_id(0)
        for i in range(n):
            g_ref, l_ref, o_ref = refs[2 * i], refs[2 * i + 1], refs[2 * n + i]

            @pl.when(jnp.logical_and(r >= starts[i], r < starts[i] + nblk[i]))
            def _():
                o_ref[...] = (g_ref[0, 0].astype(F32) + l_ref[0].astype(F32)).astype(BF)[None]

    def at(i, r):
        q = jnp.clip(r - starts[i], 0, nblk[i] - 1)
        return q // nrb[i], q % nrb[i]

    def g_spec(i):
        return pl.BlockSpec((1, 1, brs[i], gs[i].shape[3]),
                            functools.partial(lambda r, c, i: (at(i, r)[0], c[0], at(i, r)[1], 0), i=i))

    def l_spec(i):
        return pl.BlockSpec((1, brs[i], gs[i].shape[3]),
                            functools.partial(lambda r, c, i: (at(i, r)[0], at(i, r)[1], 0), i=i))

    return _call(
        body, name=name, grid=(sum(nblk),), prefetch=cidx,
        in_specs=[s for i in range(n) for s in (g_spec(i), l_spec(i))], out_specs=[l_spec(i) for i in range(n)],
        out_shape=[jax.ShapeDtypeStruct(l.shape, BF) for l in lands], vmem=32,
        args=[a for i in range(n) for a in (gs[i], lands[i])])[0]


def _add_pair(a, b, name):
    rows, cols = a.shape

    def body(a_ref, b_ref, o_ref):
        o_ref[...] = a_ref[...] + b_ref[...]

    spec = pl.BlockSpec((rows, cols), lambda r: (0, 0))
    return _call(body, name=name, grid=(1,), in_specs=[spec, spec], out_specs=[spec], out_shape=[_sds(a)],
                 vmem=32, args=[a, b])[0][0]


def _add_chips(own, land, idx, name, stages=None):
    _, hr, cols = land.shape
    br = _row_block(hr, cols)

    def body(s_ref, a_ref, b_ref, c_ref, d_ref, o_ref):
        o_ref[...] = (a_ref[...].astype(F32) + b_ref[...].astype(F32)) + (c_ref[...].astype(F32) +
                                                                           d_ref[...].astype(F32))

    spec = lambda q: pl.BlockSpec((1, br, cols), functools.partial(lambda r, s, q: (s[q], r, 0), q=q))
    outs, landed = _call(
        body, name=name, grid=(hr // br,), prefetch=idx,
        in_specs=[spec(0), spec(1), spec(2), spec(3)], out_specs=[spec(4)],
        out_shape=[jax.ShapeDtypeStruct((2, hr, cols), F32)], vmem=48, args=[own, land, land, land],
        stages=stages or ())
    return outs[0] if stages is None else (outs[0], landed)


def _add_chips_multi(owns, lands, idx, name, stages=()):
    n = len(owns)
    brs = [_row_block(l.shape[1], l.shape[2]) for l in lands]
    nblk = [l.shape[1] // b for l, b in zip(lands, brs)]
    starts = [sum(nblk[:i]) for i in range(n)]

    def body(s_ref, *refs):
        r = pl.program_id(0)
        for i in range(n):
            a_ref, b_ref, c_ref, d_ref = refs[4 * i:4 * i + 4]
            o_ref = refs[4 * n + i]

            @pl.when(jnp.logical_and(r >= starts[i], r < starts[i] + nblk[i]))
            def _():
                o_ref[...] = (a_ref[...].astype(F32) + b_ref[...].astype(F32)) + (c_ref[...].astype(F32) +
                                                                                   d_ref[...].astype(F32))

    def spec(i, q):
        return pl.BlockSpec((1, brs[i], lands[i].shape[2]), functools.partial(
            lambda r, s, q, st, nb: (s[q], jnp.clip(r - st, 0, nb - 1), 0), q=q, st=starts[i], nb=nblk[i]))

    outs, landed = _call(
        body, name=name, grid=(sum(nblk),), prefetch=idx,
        in_specs=[spec(i, q) for i in range(n) for q in range(4)], out_specs=[spec(i, 4) for i in range(n)],
        out_shape=[jax.ShapeDtypeStruct((2,) + l.shape[1:], F32) for l in lands], vmem=48,
        args=[a for i in range(n) for a in (owns[i], lands[i], lands[i], lands[i])], stages=stages)
    return outs, landed


def _adamw_math(w, g, m, v):
    mn = ADAM_B1 * m + (1.0 - ADAM_B1) * g
    vn = ADAM_B2 * v + (1.0 - ADAM_B2) * (g * g)
    m_hat = mn / (1.0 - ADAM_B1 ** ADAM_STEP)
    v_hat = vn / (1.0 - ADAM_B2 ** ADAM_STEP)
    return -ADAM_LR * (m_hat / (jnp.sqrt(v_hat) + ADAM_EPS) + ADAM_WD * w), mn, vn


def _adamw(w, g, m, v, name, stages=()):
    rows, cols = w.shape
    br = _row_block(rows, cols)

    def body(w_ref, g_ref, m_ref, v_ref, go_ref, d_ref, mo_ref, vo_ref):
        gv = g_ref[...]
        go_ref[...] = gv
        d_ref[...], mo_ref[...], vo_ref[...] = _adamw_math(w_ref[...], gv, m_ref[...], v_ref[...])

    spec = pl.BlockSpec((br, cols), lambda r: (r, 0))
    return _call(body, name=name, grid=(rows // br,), in_specs=[spec] * 4, out_specs=[spec] * 4,
                 out_shape=[_sds(w)] * 4, vmem=56, args=[w, g, m, v], stages=stages)


def _adamw_multi(names, w, g, m, v, stages=()):
    cols = w[names[0]].shape[1]
    br = 128
    nblk = [w[n].shape[0] // br for n in names]
    starts = [sum(nblk[:i]) for i in range(len(names))]

    def body(*refs):
        r = pl.program_id(0)
        for i in range(len(names)):
            w_ref, g_ref, m_ref, v_ref = refs[4 * i:4 * i + 4]
            go_ref, d_ref, mo_ref, vo_ref = refs[4 * len(names) + 4 * i:4 * len(names) + 4 * i + 4]

            @pl.when(jnp.logical_and(r >= starts[i], r < starts[i] + nblk[i]))
            def _():
                gv = g_ref[...]
                go_ref[...] = gv
                d_ref[...], mo_ref[...], vo_ref[...] = _adamw_math(w_ref[...], gv, m_ref[...], v_ref[...])

    def spec(i):
        return pl.BlockSpec((br, cols), functools.partial(
            lambda r, s, nb: (jnp.clip(r - s, 0, nb - 1), 0), s=starts[i], nb=nblk[i]))

    outs, landed = _call(
        body, name="adamw_" + "_".join(names), grid=(sum(nblk),),
        in_specs=[spec(i) for i in range(len(names)) for _ in range(4)],
        out_specs=[spec(i) for i in range(len(names)) for _ in range(4)],
        out_shape=[_sds(w[n]) for n in names for _ in range(4)], vmem=56,
        args=[a[n] for n in names for a in (w, g, m, v)], stages=stages)
    return {n: outs[4 * i:4 * i + 4] for i, n in enumerate(names)}, landed


def _to_everyone(v):
    deltas = [(a, b, e) for a in (0, 1) for b in (0, 1) for e in (0, 1)][1:]

    def copies(ins, outs, sems):
        x, y, c, _ = _place()
        me = 4 * x + 2 * y + c
        flip = lambda p, f: 1 - p if f else p
        return [_rcopy(ins[0], outs[0].at[me], sems[0].at[q], sems[1].at[q], (flip(x, a), flip(y, b), flip(c, e)))
                for q, (a, b, e) in enumerate(deltas)]

    def start(ins, outs, sems):
        for cp in copies(ins, outs, sems):
            cp.start()

    def finish(ins, outs, sems):
        for cp in copies(ins, outs, sems):
            cp.wait()

    n = len(deltas)
    return _Stage([v], [jax.ShapeDtypeStruct((2 * NCHIP,) + v.shape, v.dtype)], {},
                  [pltpu.SemaphoreType.DMA((n,)), pltpu.SemaphoreType.DMA((n,))], start, finish)


SMALL_AT = {"norm_mix_pre": (0, 1, D), "norm_mix_post": (1, 1, D), "norm_mlp_pre": (2, 1, D),
            "norm_mlp_post": (3, 1, D), "b_gate": (4, 2, D), "conv_b": (6, 1, D), "lru_b_a": (7, 1, D),
            "lru_b_x": (8, 1, D), "lru_lambda": (9, 1, D), "pool_scale": (10, 1, DP)}
SMALL_SEPARATE = ["conv_w", "lru_w_a", "lru_w_x", "pool_w"]


def _adamw_small(small_sum, first_all, sep_grads, w, m, v):
    packed, sep = list(SMALL_AT), list(SMALL_SEPARATE)
    names = packed + sep

    def body(*refs):
        s_ref, a_ref, refs = refs[0], refs[1], refs[2:]
        g_sep, refs = refs[:len(sep)], refs[len(sep):]
        nn = len(names)
        w_r, m_r, v_r, refs = refs[:nn], refs[nn:2 * nn], refs[2 * nn:3 * nn], refs[3 * nn:]
        g_out, refs = refs[:len(packed)], refs[len(packed):]
        d_o, m_o, v_o = refs[:nn], refs[nn:2 * nn], refs[2 * nn:3 * nn]
        for i, n in enumerate(names):
            if i == 0:
                g = a_ref[0:1, :]
                for q in range(1, 2 * NCHIP):
                    g = g + a_ref[q:q + 1, :]
                g_out[i][...] = g
            elif n in SMALL_AT:
                r0, nr, nc = SMALL_AT[n]
                g = jnp.concatenate([s_ref[r0 + q:r0 + q + 1, :nc] for q in range(nr)], axis=1)
                g_out[i][...] = g
            else:
                g = g_sep[i - len(packed)][...]
            d_o[i][...], m_o[i][...], v_o[i][...] = _adamw_math(w_r[i][...], g, m_r[i][...], v_r[i][...])

    ws = [w[n] for n in names]
    res = pl.pallas_call(
        body, name="adamw_small",
        out_shape=[_sds(w[n]) for n in packed] + [_sds(a) for a in ws] * 3,
        compiler_params=_cp(32),
    )(*_hbm(small_sum, first_all, *sep_grads, *ws, *[m[n] for n in names], *[v[n] for n in names]))
    nn, npk = len(names), len(packed)
    grad = dict(zip(packed, res[:npk]))
    delta = dict(zip(names, res[npk:npk + nn]))
    new_m = dict(zip(names, res[npk + nn:npk + 2 * nn]))
    new_v = dict(zip(names, res[npk + 2 * nn:]))
    return grad, delta, new_m, new_v


W_NAMES = ["norm_mix_pre", "norm_mix_post", "norm_mlp_pre", "norm_mlp_post", "w_in", "b_gate", "conv_w", "conv_b",
           "lru_w_a", "lru_b_a", "lru_w_x", "lru_b_x", "lru_lambda", "pool_w", "pool_scale", "w_lru_up",
           "w_pool_up", "w_o", "w_ff1", "w_ff2"]
BIG = ["w_in", "w_lru_up", "w_pool_up", "w_o", "w_ff1", "w_ff2"]


def _block_diag(w):
    hd = w.shape[-1]
    per = CB // hd
    w4 = w.reshape(NG, per, hd, hd)
    eye = jnp.eye(per, dtype=w.dtype)
    return jnp.einsum("gpij,pq->gpiqj", w4, eye).reshape(NG, CB, CB)


def _block_diag_extract(d, hd):
    per = CB // hd
    d5 = d.reshape(NG, per, hd, per, hd)
    return jnp.stack([d5[:, p, :, p, :] for p in range(per)], axis=1).reshape(NG * per, hd, hd)


def _halves(g):
    return g.reshape(NCHIP, 2, g.size // (g.shape[-1] * 2 * NCHIP), g.shape[-1])


def kernel(x, norm_mix_pre, norm_mix_post, norm_mlp_pre, norm_mlp_post, w_in, b_gate, conv_w, conv_b, lru_w_a, lru_b_a, lru_w_x, lru_b_x, lru_lambda, pool_w, pool_scale, w_lru_up, w_pool_up, w_o, w_ff1, w_ff2, loss_target, m_norm_mix_pre, m_norm_mix_post, m_norm_mlp_pre, m_norm_mlp_post, m_w_in, m_b_gate, m_conv_w, m_conv_b, m_lru_w_a, m_lru_b_a, m_lru_w_x, m_lru_b_x, m_lru_lambda, m_pool_w, m_pool_scale, m_w_lru_up, m_w_pool_up, m_w_o, m_w_ff1, m_w_ff2, v_norm_mix_pre, v_norm_mix_post, v_norm_mlp_pre, v_norm_mlp_post, v_w_in, v_b_gate, v_conv_w, v_conv_b, v_lru_w_a, v_lru_b_a, v_lru_w_x, v_lru_b_x, v_lru_lambda, v_pool_w, v_pool_scale, v_w_lru_up, v_w_pool_up, v_w_o, v_w_ff1, v_w_ff2):
    args = dict(locals())
    two_d = lambda a: a.reshape(-1, a.shape[-1])
    w = {n: two_d(args[n]) for n in W_NAMES}
    mom = {n: two_d(args["m_" + n]) for n in W_NAMES}
    var = {n: two_d(args["v_" + n]) for n in W_NAMES}
    i32 = lambda val: jnp.asarray(val, jnp.int32)
    chip = i32(2 * lax.axis_index("x") + lax.axis_index("y"))
    core = i32(lax.axis_index("c"))
    cidx = core.reshape(1)
    zero = i32(0)
    hd = lru_w_a.shape[-1]
    xs, target = x[0], loss_target[0]
    g1, g2, g3, g4 = norm_mix_pre, norm_mix_post, norm_mlp_pre, norm_mlp_post

    mix = ["w_lru_up", "w_pool_up", "w_o"]
    full = {"w_in": _cast_place(w["w_in"], chip.reshape(1), "cast_w_in")}
    (fl_in, fl_conv), first = _split_call("gather_start_first", start=[
        _gather([full["w_in"]], ici=[(0, ALL)]), _gather_whole(w["conv_w"])])
    casts, _ = _cast_place_multi([w[n] for n in BIG[1:]], chip.reshape(1), stages=[_after(first)])
    full.update(zip(BIG[1:], casts))
    (fl_mix, fl_ff1, fl_ff2), started = _split_call("gather_start_rest", start=[
        _gather([full[n] for n in mix], ici=[(0, ALL), (1, ALL), (2, ALL)]),
        _gather([full["w_ff1"]], ici=[(0, ALL)]), _gather([full["w_ff2"]], ici=[(0, ALL)])])
    wa = _block_diag(lru_w_a[0]).astype(BF)
    wx = _block_diag(lru_w_x[0]).astype(BF)
    pw = pool_w[0].astype(BF)

    def to_sibling(name, flight, after=None):
        (fl,), passed = _split_call(name + "_pass", finish=[flight], after=after,
                                    start=[_gather(flight.landed(), d2d=[(i, ALL) for i in range(len(flight.bufs))])])
        passed_on.append(passed)
        return fl

    passed_on = []

    def arrived(name, flight, after=None):
        _split_call(name + "_done", finish=[flight], after=after)
        return flight.landed()

    idx_big = jnp.stack([chip, (chip + 1) % NCHIP, (chip + 2) % NCHIP, (chip + 3) % NCHIP, core])
    proj, h1 = _fwd_inproj_own(xs, g1, fl_in.bufs[0], idx_big, stages=[_after(started)])
    fl_in = to_sibling("gather_w_in", fl_in, after=h1)
    w_in_f, = arrived("gather_w_in", fl_in)
    conv_all, = arrived("gather_conv", fl_conv)
    full["w_in"] = w_in_f
    conv_all = lax.dynamic_update_slice(conv_all, w["conv_w"][None], (chip, zero, zero))
    conv_full = jnp.transpose(conv_all, (1, 0, 2)).reshape(4, DR)
    proj = _fwd_inproj_rest(h1, w_in_f, proj, idx_big)
    fl_mix = to_sibling("gather_mix", fl_mix, after=proj)
    (ylru, hs), _ = _fwd_lru(proj, conv_full, conv_b, wa, lru_b_a, wx, lru_b_x, lru_lambda,
                             stages=[_after(passed_on[-1])])
    got = arrived("gather_mix", fl_mix, after=ylru)
    fl_ff1 = to_sibling("gather_ff1", fl_ff1, after=ylru)
    w_lru_up_f, w_pool_up_f, w_o_f = got[0].reshape(DR, D), got[1], got[2].reshape(D, D)
    ypool = _fwd_pool(proj, pw, pool_scale)
    (x2, h2, m, mrg, bra, brb), _ = _fwd_merge(xs, ylru, ypool, proj, b_gate, g2, g3, w_lru_up_f, w_pool_up_f, w_o_f,
                                               stages=[_after(passed_on[-1])])
    fl_ff2 = to_sibling("gather_ff2", fl_ff2, after=h2)
    ff1, = arrived("gather_ff1", fl_ff1, after=h2)
    ff2, = arrived("gather_ff2", fl_ff2)
    ff2 = ff2.reshape(DF, D)
    a1, f = _fwd_mlp(h2, ff1, ff2)
    lossp, dy, df, dg4 = _loss_head(f, x2, target, g4)

    dh2, df1 = _bwd_mlp_x(df, a1, ff1, ff2)
    dw_ff1, dw_ff2 = _bwd_mlp_w(df, h2, a1, df1)
    g_ff = [_halves(dw_ff1), _halves(dw_ff2)]
    (dxres, dgates, dylru, dypool, dm, dbra, dbrb, dg2, dg3, dbg), (l_ff,) = _bwd_merge(
        dh2, dy, x2, m, bra, brb, proj, b_gate, g2, g3, w_lru_up_f, w_pool_up_f, w_o_f, stages=[_to_sibling(g_ff)])
    p_ff = _add_sibling_multi(g_ff, l_ff, cidx, "add_sibling_ff")
    (fl_ff,), sent_ff = _split_call("reduce_ff_start", start=[_to_chips(p_ff)])
    (dw_o, dw_lru_up, dw_pool_up), _ = _dw_merge(mrg, dm, ylru, dbra, ypool, dbrb, stages=[_after(sent_ff)])
    g_mix = [_halves(dw_lru_up), _halves(dw_pool_up), _halves(dw_o)]
    (dxp, dgl, dcw, dcb, dwa, dba, dwx, dbx, dlam), (l_mix,) = _bwd_lru(
        proj, hs, dylru, conv_full, conv_b, wa, lru_b_a, wx, lru_b_x, lru_lambda, stages=[_to_sibling(g_mix)])
    p_mix = _add_sibling_multi(g_mix, l_mix, cidx, "add_sibling_mix")
    dxpool, dpw, dsc = _bwd_pool(proj, dypool, pw, pool_scale)
    dproj = jnp.concatenate([dxp, dgl, dxpool, dgates], axis=1)
    small = jnp.concatenate([
        jnp.zeros((1, D), F32), dg2, dg3, dg4, dbg.reshape(2, D), dcb, dba, dbx, dlam,
        jnp.pad(dsc, ((0, 0), (0, D - DP))), jnp.pad(lossp, ((0, 0), (0, D - 1))), dcw,
        _block_diag_extract(dwa, hd).reshape(-1, D), _block_diag_extract(dwx, hd).reshape(-1, D),
        dpw.reshape(-1, D)], axis=0)
    (dw_in, dh1), (c_mix, (l_small,)) = _bwd_inproj(h1, dproj, full["w_in"],
                                                     stages=[_to_chips(p_mix), _to_sibling([small])])
    small2 = _add_pair(small, l_small, "add_sibling_small").reshape(2, SMALL_ROWS // 2, D)
    g_in = _halves(dw_in)
    done = ["w_ff1", "w_ff2"] + mix
    (fl_gin, fl_small), _ = _split_call("reduce_in_sibling_start", start=[_to_sibling([g_in]), _to_chips([small2])])
    _split_call("reduce_in_sibling_done", finish=[fl_gin, fl_ff])
    (g_in, l_in), (p_ff1, p_ff2, c_ff1, c_ff2) = fl_gin.bufs, fl_ff.bufs
    p_in = _add_sibling(g_in, l_in, cidx, "add_sibling_w_in")[0]
    ssem, rsem, p_in, c_in, token = _chips_start(p_in)
    pairs, _ = _add_chips_multi([p_ff1, p_ff2] + p_mix, [c_ff1, c_ff2] + c_mix, idx_big, "add_chips_done",
                                stages=[_after(token)])
    _split_call("reduce_small_done", finish=[fl_small], after=pairs[-1])
    small2, c_small = fl_small.bufs
    own_small = lax.dynamic_index_in_dim(small2, core, 0, keepdims=True)
    c_small = lax.dynamic_update_slice(c_small, own_small, (chip, zero, zero))
    pair_small = _add_chips(c_small, c_small, jnp.stack([zero, zero + 1, zero + 2, zero + 3, core]), "add_chips_small")
    (fl_share,), shared_start = _split_call("reduce_share_start", start=[_share(pairs + [pair_small])])
    (grad_x, dg1), _ = _bwd_prenorm(xs, dh1, dxres, g1, stages=[_after(shared_start)])
    _split_call("reduce_share_done", finish=[fl_share], after=dg1)
    shared = fl_share.landed()
    pairs, pair_small = shared[:-1], shared[-1]

    grads, delta, new_m, new_v = {}, {}, {}, {}
    for n, p in zip(done, pairs):
        grads[n] = p.reshape(-1, p.shape[-1])

    def update(n, stages=()):
        (grads[n], delta[n], new_m[n], new_v[n]), landed = _adamw(w[n], grads[n], mom[n], var[n], "adamw_" + n,
                                                                  stages=stages)
        return landed

    updated, _ = _adamw_multi(["w_ff1", "w_ff2", "w_o", "w_lru_up"], w, grads, mom, var)
    for n, (go, d, mo, vo) in updated.items():
        grads[n], delta[n], new_m[n], new_v[n] = go, d, mo, vo
    p_in, c_in = _chips_wait(ssem, rsem, p_in, c_in, new_v["w_lru_up"])
    pair_in = _add_chips(p_in, c_in, idx_big, "add_chips_w_in")
    ((pair_in,), (dg1_all,)) = update("w_pool_up", stages=[_share([pair_in]), _to_everyone(dg1)])
    dg1_all = lax.dynamic_update_slice(dg1_all, dg1[None], (2 * chip + core, zero, zero)).reshape(2 * NCHIP, D)
    grads["w_in"] = pair_in.reshape(-1, pair_in.shape[-1])
    update("w_in")
    small_sum = pair_small.reshape(SMALL_ROWS, D)
    loss = 0.5 * small_sum[LOSS_ROW, 0]
    ccols = DR // NCHIP
    sep = [lax.dynamic_slice(small_sum[12:16], (zero, chip * ccols), (4, ccols)),
           small_sum[16:80].reshape(-1, hd), small_sum[80:144].reshape(-1, hd), small_sum[144:208].reshape(-1, PG)]
    g_s, d_s, m_s, v_s = _adamw_small(small_sum, dg1_all, sep, w, mom, var)
    grads.update(g_s)
    grads.update(dict(zip(SMALL_SEPARATE, sep)))
    delta.update(d_s)
    new_m.update(m_s)
    new_v.update(v_s)

    out = lambda d: [d[n].reshape(args[n].shape) for n in W_NAMES]
    return (loss, grad_x[None], *out(grads), *out(delta), *out(new_m), *out(new_v))
```

```python
import functools
import math

import jax
import jax.numpy as jnp
from jax import lax
from jax.experimental import pallas as pl
from jax.experimental.pallas import tpu as pltpu

F32 = jnp.float32
BF = jnp.bfloat16

T = 2048
D = 1024
DR = 1024
DP = 512
DF = 4096
DIN = 4608
NCHIP = 4
CW_IN = DIN // NCHIP
LANE = 128
CB = 128
NG = DR // CB
PG = 128
POOL_WINDOWS = (2, 4, 8, 16)
NORM_EPS = 1e-6
LRU_C = 8.0
GELU_C = math.sqrt(2.0 / math.pi)
ADAM_LR = 0.001
ADAM_B1 = 0.9
ADAM_B2 = 0.999
ADAM_EPS = 1e-08
ADAM_WD = 0.01
ADAM_STEP = 10
MESH_ID = pl.DeviceIdType.MESH
ANY = pl.BlockSpec(memory_space=pl.ANY)
SMALL_ROWS = 208
LOSS_ROW = 11
MIB = 1 << 20


def _cp(vmem_mib=None):
    if vmem_mib is None:
        return pltpu.CompilerParams()
    return pltpu.CompilerParams(vmem_limit_bytes=vmem_mib * MIB)


def _hbm(*arrays):
    return [pltpu.with_memory_space_constraint(a, pltpu.HBM) for a in arrays]


def _hbm_out(shapes):
    return [pltpu.HBM(s.shape, s.dtype) for s in shapes]


class _Stage:
    def __init__(self, operands, out_shape, alias, sems, start, finish):
        self.operands, self.out_shape, self.alias, self.sems = list(operands), list(out_shape), dict(alias), list(sems)
        self.start, self.finish = start, finish


def _call(body, *, name, grid, in_specs, out_specs, out_shape, args, vmem=None, stages=(), prefetch=None,
          scratch=()):
    nin, nout = len(in_specs), len(out_specs)
    npre = 0 if prefetch is None else 1
    st_args, st_shapes, st_sems, aliases = [], [], list(scratch), {}
    for st in stages:
        for k, v in st.alias.items():
            aliases[npre + nin + len(st_args) + k] = nout + len(st_shapes) + v
        st_args += st.operands
        st_shapes += st.out_shape
        st_sems += st.sems

    def wrapped(*refs):
        pre, refs = refs[:npre], refs[npre:]
        ins, pos = refs[:nin], nin
        st_ins = []
        for st in stages:
            st_ins.append(refs[pos:pos + len(st.operands)])
            pos += len(st.operands)
        outs, pos = refs[pos:pos + nout], pos + nout
        st_outs = []
        for st in stages:
            st_outs.append(refs[pos:pos + len(st.out_shape)])
            pos += len(st.out_shape)
        work, pos = refs[pos:pos + len(scratch)], pos + len(scratch)
        sems = []
        for st in stages:
            sems.append(refs[pos:pos + len(st.sems)])
            pos += len(st.sems)
        if stages:
            first = functools.reduce(jnp.logical_and, [pl.program_id(a) == 0 for a in range(len(grid))])

            @pl.when(first)
            def _():
                for st, a, b, s in zip(stages, st_ins, st_outs, sems):
                    st.start(a, b, s)

        body(*pre, *ins, *outs, *work)
        if stages:
            last = functools.reduce(jnp.logical_and, [pl.program_id(a) == g - 1 for a, g in enumerate(grid)])

            @pl.when(last)
            def _():
                for st, a, b, s in zip(stages, st_ins, st_outs, sems):
                    st.finish(a, b, s)

    all_in = list(in_specs) + [ANY] * len(st_args)
    all_out = list(out_specs) + [ANY] * len(st_shapes)
    kw = dict(has_side_effects=True) if stages else {}
    if vmem is not None:
        kw["vmem_limit_bytes"] = vmem * MIB
    if prefetch is None:
        gkw = dict(grid=grid, in_specs=all_in, out_specs=all_out, scratch_shapes=st_sems)
    else:
        gkw = dict(grid_spec=pltpu.PrefetchScalarGridSpec(
            num_scalar_prefetch=1, grid=grid, in_specs=all_in, out_specs=all_out, scratch_shapes=st_sems))
    res = pl.pallas_call(
        wrapped, name=name, out_shape=_hbm_out(list(out_shape) + st_shapes), input_output_aliases=aliases,
        compiler_params=pltpu.CompilerParams(**kw), **gkw,
    )(*([prefetch] if npre else []), *_hbm(*args, *st_args))
    outs, rest, st_res = list(res[:nout]), list(res[nout:]), []
    for st in stages:
        st_res.append(rest[:len(st.out_shape)])
        rest = rest[len(st.out_shape):]
    return outs, st_res


def _mm(a, b):
    return jnp.dot(a.astype(BF), b.astype(BF), preferred_element_type=F32)


def _mm_nt(a, b):
    return lax.dot_general(a.astype(BF), b.astype(BF), (((1,), (1,)), ((), ())),
                           preferred_element_type=F32)


def _mm_tn(a, b):
    return lax.dot_general(a.astype(BF), b.astype(BF), (((0,), (0,)), ((), ())),
                           preferred_element_type=F32)


def _rows(v):
    return lax.broadcasted_iota(jnp.int32, v.shape, 0)


def _sd(v, s, fill=0.0):
    return jnp.where(_rows(v) >= s, pltpu.roll(v, s, axis=0), fill)


def _su(v, s, fill=0.0):
    n = v.shape[0]
    return jnp.where(_rows(v) < n - s, pltpu.roll(v, n - s, axis=0), fill)


def _sigmoid(z):
    return 1.0 / (1.0 + jnp.exp(-z))


def _softplus(z):
    e = jnp.exp(-jnp.abs(z))
    u = 1.0 + e
    d = u - 1.0
    log1p = jnp.where(d == 0.0, e, jnp.log(u) * (e / jnp.where(d == 0.0, 1.0, d)))
    return jnp.maximum(z, 0.0) + log1p


def _mean(v):
    return jnp.mean(v, axis=-1, keepdims=True)


def _colsum(v):
    return jnp.sum(v, axis=0, keepdims=True)


def _acc(ref, val, first):
    @pl.when(first)
    def _():
        ref[...] = val

    @pl.when(jnp.logical_not(first))
    def _():
        ref[...] += val


def _conv(xp, cw, cb):
    x1, x2, x3 = _sd(xp, 1), _sd(xp, 2), _sd(xp, 3)
    xc = cb + cw[0:1] * x3 + cw[1:2] * x2 + cw[2:3] * x1 + cw[3:4] * xp
    return xc, x1, x2, x3


def _lru_gates(xc, wa, ba, wx, bx, lam):
    xcb = xc.astype(BF)
    r = _sigmoid(_mm(xcb, wa) + ba)
    ii = _sigmoid(_mm(xcb, wx) + bx)
    sp = _softplus(-lam)
    la = (-LRU_C) * r * sp
    a = jnp.exp(la)
    mult = jnp.sqrt(-jnp.tanh(la) * (a * a + 1.0))
    return xcb, r, ii, sp, a, mult


def _gelu_parts(g):
    th = jnp.tanh(GELU_C * (g + 0.044715 * (g * g * g)))
    gel = 0.5 * g * (1.0 + th)
    dgel = 0.5 * (1.0 + th) + 0.5 * g * (1.0 - th * th) * (GELU_C * (1.0 + 3.0 * 0.044715 * (g * g)))
    return gel, dgel


def _tile_scan(a, b, a_s, b_s, out_ref, reverse):
    n = a.shape[0]
    nt = n // 8
    sub = jnp.bitwise_and(_rows(a), 7)
    s = 1
    while s < 8:
        keep = sub < 8 - s if reverse else sub >= s
        amount = n - s if reverse else s
        b = b + a * jnp.where(keep, pltpu.roll(b, amount, axis=0), 0.0)
        a = a * jnp.where(keep, pltpu.roll(a, amount, axis=0), 1.0)
        s *= 2
    a_s[...] = a
    b_s[...] = b
    edge = pl.ds(0 if reverse else 7, nt, stride=8)
    ta, tb = a_s[edge, :], b_s[edge, :]
    shift = _su if reverse else _sd
    s = 1
    while s < nt:
        tb = tb + ta * shift(tb, s, 0.0)
        if 2 * s < nt:
            ta = ta * shift(ta, s, 1.0)
        s *= 2
    enters = shift(tb, 1, 0.0)
    for o in range(8):
        rows = pl.ds(o, nt, stride=8)
        out_ref[rows, :] = b_s[rows, :] + a_s[rows, :] * enters


def _pool_window(x, steps, shift):
    s, sh = x, 1
    for _ in range(steps):
        s = s + shift(s, sh)
        sh *= 2
    return s


def _fwd_inproj_own(x, g1, w_in, slots, stages=()):
    tm = 512

    def body(s_ref, x_ref, g_ref, w_ref, proj_ref, h_ref):
        xv = x_ref[...]
        r = lax.rsqrt(_mean(xv * xv) + NORM_EPS)
        h = ((xv * r) * g_ref[...]).astype(BF)
        h_ref[...] = h
        proj_ref[...] = jnp.dot(h, w_ref[0], preferred_element_type=F32)

    return _call(
        body, name="fwd_inproj_own", grid=(T // tm,), prefetch=slots,
        in_specs=[pl.BlockSpec((tm, D), lambda i, s: (i, 0)),
                  pl.BlockSpec((1, D), lambda i, s: (0, 0)),
                  pl.BlockSpec((1, D, CW_IN), lambda i, s: (s[0], 0, 0))],
        out_specs=[pl.BlockSpec((tm, CW_IN), lambda i, s: (i, s[0])),
                   pl.BlockSpec((tm, D), lambda i, s: (i, 0))],
        out_shape=[jax.ShapeDtypeStruct((T, DIN), F32), jax.ShapeDtypeStruct((T, D), BF)],
        vmem=40, args=[x, g1, w_in], stages=stages)[0]


def _fwd_inproj_rest(h1, w_in, proj, slots):
    tm = 512

    def body(s_ref, h_ref, w_ref, p_in, proj_ref):
        proj_ref[...] = jnp.dot(h_ref[...], w_ref[0], preferred_element_type=F32)

    res = pl.pallas_call(
        body, name="fwd_inproj_rest",
        grid_spec=pltpu.PrefetchScalarGridSpec(
            num_scalar_prefetch=1, grid=(T // tm, NCHIP - 1),
            in_specs=[pl.BlockSpec((tm, D), lambda i, k, s: (i, 0)),
                      pl.BlockSpec((1, D, CW_IN), lambda i, k, s: (s[1 + k], 0, 0)), ANY],
            out_specs=pl.BlockSpec((tm, CW_IN), lambda i, k, s: (i, s[1 + k]))),
        out_shape=pltpu.HBM((T, DIN), F32), input_output_aliases={3: 0},
        compiler_params=_cp(40),
    )(slots, *_hbm(h1, w_in, proj))
    return res


def _vec_spec():
    return pl.BlockSpec((1, CB), lambda j: (0, j))


def _fwd_lru(proj, conv_w, conv_b, wa, ba, wx, bx, lam, stages=()):
    def body(xp_ref, g_ref, cw_ref, cb_ref, wa_ref, ba_ref, wx_ref, bx_ref, lam_ref, y_ref, h_ref, a_s, b_s):
        xc, _, _, _ = _conv(xp_ref[...], cw_ref[...], cb_ref[...])
        _, _, ii, _, a, mult = _lru_gates(xc, wa_ref[0], ba_ref[...], wx_ref[0], bx_ref[...], lam_ref[...])
        _tile_scan(a, mult * (ii * xc), a_s, b_s, h_ref, reverse=False)
        gel, _ = _gelu_parts(g_ref[...])
        y_ref[...] = (h_ref[...] * gel).astype(BF)

    return _call(
        body, name="fwd_lru", grid=(NG,),
        in_specs=[pl.BlockSpec((T, CB), lambda j: (0, j)),
                  pl.BlockSpec((T, CB), lambda j: (0, NG + j)),
                  pl.BlockSpec((4, CB), lambda j: (0, j)),
                  _vec_spec(),
                  pl.BlockSpec((1, CB, CB), lambda j: (j, 0, 0)), _vec_spec(),
                  pl.BlockSpec((1, CB, CB), lambda j: (j, 0, 0)), _vec_spec(),
                  _vec_spec()],
        out_specs=[pl.BlockSpec((T, CB), lambda j: (0, j)), pl.BlockSpec((T, CB), lambda j: (0, j))],
        out_shape=[jax.ShapeDtypeStruct((T, DR), BF), jax.ShapeDtypeStruct((T, DR), F32)],
        vmem=48, args=[proj, proj, conv_w, conv_b, wa, ba, wx, bx, lam], stages=stages,
        scratch=[pltpu.VMEM((T, CB), F32)] * 2)


def _pool_cnt(w):
    t = lax.broadcasted_iota(jnp.int32, (T, 1), 0)
    return jnp.minimum(t + 1, w).astype(F32)


def _fwd_pool(proj, pool_w, pool_scale):
    def body(xp_ref, pw_ref, sc_ref, y_ref):
        for g, w in enumerate(POOL_WINDOWS):
            cols = slice(g * PG, (g + 1) * PG)
            x = xp_ref[:, cols]
            p = _pool_window(x, g + 1, _sd) / _pool_cnt(w) - x
            y_ref[:, cols] = (_mm(p, pw_ref[g]) * sc_ref[:, cols]).astype(BF)

    return pl.pallas_call(
        body, name="fwd_pool", grid=(1,),
        in_specs=[pl.BlockSpec((T, DP), lambda i: (0, 2 * DR // DP)),
                  pl.BlockSpec((4, PG, PG), lambda i: (0, 0, 0)),
                  pl.BlockSpec((1, DP), lambda i: (0, 0))],
        out_specs=pl.BlockSpec((T, DP), lambda i: (0, 0)),
        out_shape=pltpu.HBM((T, DP), BF),
        compiler_params=_cp(48),
    )(*_hbm(proj, pool_w, pool_scale))


GATE_BLK = 512
GATE_BLK0 = (2 * DR + DP) // GATE_BLK


def _gate_specs(tm):
    return [pl.BlockSpec((tm, GATE_BLK), functools.partial(lambda i, q: (i, GATE_BLK0 + q), q=q))
            for q in range(4)]


def _fwd_merge(x, ylru, ypool, proj, b_gate, g2, g3, w_lru_up, w_pool_up, w_o, stages=()):
    tm = 512

    def body(x_ref, yl_ref, yp_ref, p0, p1, p2, p3, bg_ref, g2_ref, g3_ref, wl_ref, wp_ref, wo_ref,
             x2_ref, h2_ref, m_ref, mrg_ref, bra_ref, brb_ref):
        bra = jnp.dot(yl_ref[...], wl_ref[...], preferred_element_type=F32)
        yp = yp_ref[...]
        brb = jnp.concatenate([jnp.dot(yp, wp_ref[k], preferred_element_type=F32) for k in range(NCHIP)], axis=1)
        bg = bg_ref[...]
        ga = _sigmoid(jnp.concatenate([p0[...], p1[...]], axis=1) + bg[:, :D])
        gb = _sigmoid(jnp.concatenate([p2[...], p3[...]], axis=1) + bg[:, D:])
        mrg = (ga * bra + gb * brb).astype(BF)
        m = jnp.dot(mrg, wo_ref[...], preferred_element_type=F32)
        r2 = lax.rsqrt(_mean(m * m) + NORM_EPS)
        x2 = x_ref[...] + (m * r2) * g2_ref[...]
        r3 = lax.rsqrt(_mean(x2 * x2) + NORM_EPS)
        x2_ref[...] = x2
        h2_ref[...] = ((x2 * r3) * g3_ref[...]).astype(BF)
        m_ref[...] = m
        mrg_ref[...] = mrg
        bra_ref[...] = bra.astype(BF)
        brb_ref[...] = brb.astype(BF)

    row = lambda w: pl.BlockSpec((tm, w), lambda i: (i, 0))
    full2 = lambda a, b: pl.BlockSpec((a, b), lambda i: (0, 0))
    return _call(
        body, name="fwd_merge", grid=(T // tm,),
        in_specs=[row(D), row(DR), row(DP)] + _gate_specs(tm) +
                 [full2(1, 2 * D), full2(1, D), full2(1, D), full2(DR, D),
                  pl.BlockSpec((NCHIP, DP, D // NCHIP), lambda i: (0, 0, 0)), full2(D, D)],
        out_specs=[row(D)] * 6,
        out_shape=[jax.ShapeDtypeStruct((T, D), F32), jax.ShapeDtypeStruct((T, D), BF),
                   jax.ShapeDtypeStruct((T, D), F32), jax.ShapeDtypeStruct((T, D), BF),
                   jax.ShapeDtypeStruct((T, D), BF), jax.ShapeDtypeStruct((T, D), BF)],
        vmem=48, args=[x, ylru, ypool, proj, proj, proj, proj, b_gate, g2, g3, w_lru_up, w_pool_up, w_o],
        stages=stages)


def _fwd_mlp(h2, w_ff1, w_ff2):
    tm = 512
    fk = DF // NCHIP

    def body(h_ref, w1_ref, w2_ref, a1_ref, f_ref):
        h = h_ref[...]
        f = None
        for k in range(NCHIP):
            a1 = jnp.maximum(jnp.dot(h, w1_ref[k], preferred_element_type=F32), 0.0)
            a1_ref[:, k * fk:(k + 1) * fk] = a1.astype(BF)
            part = jnp.dot((a1 * a1).astype(BF), w2_ref[k * fk:(k + 1) * fk, :], preferred_element_type=F32)
            f = part if f is None else f + part
        f_ref[...] = f

    return pl.pallas_call(
        body, name="fwd_mlp", grid=(T // tm,),
        in_specs=[pl.BlockSpec((tm, D), lambda i: (i, 0)),
                  pl.BlockSpec((NCHIP, D, fk), lambda i: (0, 0, 0)),
                  pl.BlockSpec((DF, D), lambda i: (0, 0))],
        out_specs=[pl.BlockSpec((tm, DF), lambda i: (i, 0)), pl.BlockSpec((tm, D), lambda i: (i, 0))],
        out_shape=_hbm_out([jax.ShapeDtypeStruct((T, DF), BF), jax.ShapeDtypeStruct((T, D), F32)]),
        compiler_params=_cp(56),
    )(*_hbm(h2, w_ff1, w_ff2))


def _loss_head(f, x2, target, g4):
    tm = 512

    def body(f_ref, x2_ref, t_ref, g_ref, loss_ref, dy_ref, df_ref, dg_ref):
        first = pl.program_id(0) == 0
        f = f_ref[...]
        g4v = g_ref[...]
        r4 = lax.rsqrt(_mean(f * f) + NORM_EPS)
        fn = f * r4
        e = (x2_ref[...] + fn * g4v) - t_ref[...]
        _acc(loss_ref, jnp.sum(_mean(e * e), axis=0, keepdims=True), first)
        dy = e * (1.0 / D)
        dy_ref[...] = dy
        _acc(dg_ref, _colsum(dy * fn), first)
        dfn = dy * g4v
        df_ref[...] = (r4 * (dfn - fn * _mean(dfn * fn))).astype(BF)

    row = pl.BlockSpec((tm, D), lambda i: (i, 0))
    return pl.pallas_call(
        body, name="loss_head", grid=(T // tm,),
        in_specs=[row, row, row, pl.BlockSpec((1, D), lambda i: (0, 0))],
        out_specs=[pl.BlockSpec((1, 1), lambda i: (0, 0)), row, row, pl.BlockSpec((1, D), lambda i: (0, 0))],
        out_shape=_hbm_out([jax.ShapeDtypeStruct((1, 1), F32), jax.ShapeDtypeStruct((T, D), F32),
                            jax.ShapeDtypeStruct((T, D), BF), jax.ShapeDtypeStruct((1, D), F32)]),
        compiler_params=_cp(48),
    )(*_hbm(f, x2, target, g4))


def _bwd_mlp_x(df, a1, w_ff1, w_ff2):
    tm = 512
    fk = DF // NCHIP

    def body(df_ref, a1_ref, w1_ref, w2_ref, dh_ref, df1_ref):
        df = df_ref[...]
        dh = None
        for k in range(NCHIP):
            cols = slice(k * fk, (k + 1) * fk)
            dact = _mm_nt(df, w2_ref[cols, :])
            df1 = (dact * (2.0 * a1_ref[:, cols].astype(F32))).astype(BF)
            df1_ref[:, cols] = df1
            part = _mm_nt(df1, w1_ref[k])
            dh = part if dh is None else dh + part
        dh_ref[...] = dh

    return pl.pallas_call(
        body, name="bwd_mlp_x", grid=(T // tm,),
        in_specs=[pl.BlockSpec((tm, D), lambda i: (i, 0)),
                  pl.BlockSpec((tm, DF), lambda i: (i, 0)),
                  pl.BlockSpec((NCHIP, D, fk), lambda i: (0, 0, 0)),
                  pl.BlockSpec((DF, D), lambda i: (0, 0))],
        out_specs=[pl.BlockSpec((tm, D), lambda i: (i, 0)), pl.BlockSpec((tm, DF), lambda i: (i, 0))],
        out_shape=_hbm_out([jax.ShapeDtypeStruct((T, D), F32), jax.ShapeDtypeStruct((T, DF), BF)]),
        compiler_params=_cp(56),
    )(*_hbm(df, a1, w_ff1, w_ff2))


def _bwd_mlp_w(df, h2, a1, df1):
    fc = 512
    per = (DF // NCHIP) // fc

    def body(df_ref, h_ref, a1_ref, df1_ref, dw1_ref, dw2_ref):
        a1 = a1_ref[...].astype(F32)
        dw2_ref[...] = _mm_tn((a1 * a1).astype(BF), df_ref[...]).astype(BF)
        dw1_ref[0] = _mm_tn(h_ref[...], df1_ref[...]).astype(BF)

    return pl.pallas_call(
        body, name="bwd_mlp_w", grid=(DF // fc,),
        in_specs=[pl.BlockSpec((T, D), lambda j: (0, 0)),
                  pl.BlockSpec((T, D), lambda j: (0, 0)),
                  pl.BlockSpec((T, fc), lambda j: (0, j)),
                  pl.BlockSpec((T, fc), lambda j: (0, j))],
        out_specs=[pl.BlockSpec((1, D, fc), lambda j: (j // per, 0, j % per)),
                   pl.BlockSpec((fc, D), lambda j: (j, 0))],
        out_shape=_hbm_out([jax.ShapeDtypeStruct((NCHIP, D, DF // NCHIP), BF),
                            jax.ShapeDtypeStruct((DF, D), BF)]),
        compiler_params=_cp(56),
    )(*_hbm(df, h2, a1, df1))


def _bwd_merge(dh2, dy, x2, m, bra, brb, proj, b_gate, g2, g3, w_lru_up, w_pool_up, w_o, stages=()):
    tm = 256
    cpu = D // NCHIP

    def body(dh2_ref, dy_ref, x2_ref, m_ref, bra_ref, brb_ref, p0, p1, p2, p3, bg_ref,
             g2_ref, g3_ref, wl_ref, wp_ref, wo_ref,
             dx_ref, dgt_ref, dyl_ref, dyp_ref, dm_ref, dbra_ref, dbrb_ref, dg2_ref, dg3_ref, dbg_ref):
        first = pl.program_id(0) == 0
        x2 = x2_ref[...]
        r3 = lax.rsqrt(_mean(x2 * x2) + NORM_EPS)
        x2n = x2 * r3
        dh2 = dh2_ref[...]
        t3 = dh2 * g3_ref[...]
        dx2 = dy_ref[...] + r3 * (t3 - x2n * _mean(t3 * x2n))
        dx_ref[...] = dx2
        _acc(dg3_ref, _colsum(dh2 * x2n), first)
        m = m_ref[...]
        r2 = lax.rsqrt(_mean(m * m) + NORM_EPS)
        mn = m * r2
        _acc(dg2_ref, _colsum(dx2 * mn), first)
        dmn = dx2 * g2_ref[...]
        dm = (r2 * (dmn - mn * _mean(dmn * mn))).astype(BF)
        dm_ref[...] = dm
        dmrg = _mm_nt(dm, wo_ref[...])
        bg = bg_ref[...]
        ga = _sigmoid(jnp.concatenate([p0[...], p1[...]], axis=1) + bg[:, :D])
        gb = _sigmoid(jnp.concatenate([p2[...], p3[...]], axis=1) + bg[:, D:])
        dga = dmrg * bra_ref[...].astype(F32) * (ga * (1.0 - ga))
        dgb = dmrg * brb_ref[...].astype(F32) * (gb * (1.0 - gb))
        dgt_ref[:, :D] = dga.astype(BF)
        dgt_ref[:, D:] = dgb.astype(BF)
        _acc(dbg_ref, jnp.concatenate([_colsum(dga), _colsum(dgb)], axis=1), first)
        dbra = (dmrg * ga).astype(BF)
        dbrb = (dmrg * gb).astype(BF)
        dbra_ref[...] = dbra
        dbrb_ref[...] = dbrb
        dyl_ref[...] = _mm_nt(dbra, wl_ref[...])
        dyp = None
        for k in range(NCHIP):
            part = _mm_nt(dbrb[:, k * cpu:(k + 1) * cpu], wp_ref[k])
            dyp = part if dyp is None else dyp + part
        dyp_ref[...] = dyp

    row = lambda w: pl.BlockSpec((tm, w), lambda i: (i, 0))
    full2 = lambda a, b: pl.BlockSpec((a, b), lambda i: (0, 0))
    wp_spec = pl.BlockSpec((NCHIP, DP, cpu), lambda i: (0, 0, 0))
    return _call(
        body, name="bwd_merge", grid=(T // tm,),
        in_specs=[row(D)] * 6 + _gate_specs(tm) +
                 [full2(1, 2 * D), full2(1, D), full2(1, D), full2(DR, D), wp_spec, full2(D, D)],
        out_specs=[row(D), row(2 * D), row(DR), row(DP), row(D), row(D), row(D),
                   full2(1, D), full2(1, D), full2(1, 2 * D)],
        out_shape=[jax.ShapeDtypeStruct((T, D), F32), jax.ShapeDtypeStruct((T, 2 * D), BF),
                   jax.ShapeDtypeStruct((T, DR), F32), jax.ShapeDtypeStruct((T, DP), F32),
                   jax.ShapeDtypeStruct((T, D), BF), jax.ShapeDtypeStruct((T, D), BF),
                   jax.ShapeDtypeStruct((T, D), BF),
                   jax.ShapeDtypeStruct((1, D), F32), jax.ShapeDtypeStruct((1, D), F32),
                   jax.ShapeDtypeStruct((1, 2 * D), F32)],
        vmem=56, args=[dh2, dy, x2, m, bra, brb, proj, proj, proj, proj, b_gate, g2, g3, w_lru_up, w_pool_up, w_o],
        stages=stages)


def _dw_merge(mrg, dm, ylru, dbra, ypool, dbrb, stages=()):
    nb = NCHIP
    rb, pb, cpu = D // nb, DP // nb, D // NCHIP

    def body(mrg_ref, dm_ref, yl_ref, dbra_ref, yp_ref, dbrb_ref, dwo_ref, dwl_ref, dwp_ref):
        dwo_ref[...] = _mm_tn(mrg_ref[...], dm_ref[...]).astype(BF)
        dwl_ref[...] = _mm_tn(yl_ref[...], dbra_ref[...]).astype(BF)
        dwp = _mm_tn(yp_ref[...], dbrb_ref[...]).astype(BF)
        for k in range(NCHIP):
            dwp_ref[k] = dwp[:, k * cpu:(k + 1) * cpu]

    cols = lambda w: pl.BlockSpec((T, w), lambda r: (0, r))
    whole = pl.BlockSpec((T, D), lambda r: (0, 0))
    return _call(
        body, name="dw_merge", grid=(nb,),
        in_specs=[cols(rb), whole, cols(rb), whole, cols(pb), whole],
        out_specs=[pl.BlockSpec((rb, D), lambda r: (r, 0)), pl.BlockSpec((rb, D), lambda r: (r, 0)),
                   pl.BlockSpec((NCHIP, pb, cpu), lambda r: (0, r, 0))],
        out_shape=[jax.ShapeDtypeStruct((D, D), BF), jax.ShapeDtypeStruct((DR, D), BF),
                   jax.ShapeDtypeStruct((NCHIP, DP, cpu), BF)],
        vmem=56, args=[mrg, dm, ylru, dbra, ypool, dbrb], stages=stages)


def _bwd_lru(proj, h, dylru, conv_w, conv_b, wa, ba, wx, bx, lam, stages=()):
    def body(xp_ref, g_ref, h_ref, dy_ref, cw_ref, cb_ref, wa_ref, ba_ref, wx_ref, bx_ref, lam_ref,
             dxp_ref, dg_ref, dcw_ref, dcb_ref, dwa_ref, dba_ref, dwx_ref, dbx_ref, dlam_ref, a_s, b_s, l_s):
        xp = xp_ref[...]
        cw = cw_ref[...]
        lam = lam_ref[...]
        xc, x1, x2, x3 = _conv(xp, cw, cb_ref[...])
        wa, wx = wa_ref[0], wx_ref[0]
        xcb, r, ii, sp, a, mult = _lru_gates(xc, wa, ba_ref[...], wx, bx_ref[...], lam)
        g = g_ref[...]
        gel, dgel = _gelu_parts(g)
        h = h_ref[...]
        dy = dy_ref[...]
        dg_ref[...] = (dy * h * dgel).astype(BF)
        _tile_scan(_su(a, 1, 0.0), dy * gel, a_s, b_s, l_s, reverse=True)
        b = l_s[...]
        da = b * _sd(h, 1, 0.0)
        dmult = b * (ii * xc)
        dii = b * (mult * xc)
        dxc = b * (mult * ii)
        dla = da * a - dmult * ((a * a) / mult)
        dr = dla * ((-LRU_C) * sp)
        dsp = _colsum(dla * ((-LRU_C) * r))
        dlam_ref[...] = -dsp / (1.0 + jnp.exp(lam))
        dzr = dr * (r * (1.0 - r))
        dzi = dii * (ii * (1.0 - ii))
        dzrb, dzib = dzr.astype(BF), dzi.astype(BF)
        dxc = dxc + _mm_nt(dzrb, wa) + _mm_nt(dzib, wx)
        dwa_ref[0] = _mm_tn(xcb, dzrb)
        dwx_ref[0] = _mm_tn(xcb, dzib)
        dba_ref[...] = _colsum(dzr)
        dbx_ref[...] = _colsum(dzi)
        dcb_ref[...] = _colsum(dxc)
        dcw_ref[...] = jnp.concatenate([_colsum(dxc * x3), _colsum(dxc * x2), _colsum(dxc * x1),
                                        _colsum(dxc * xp)], axis=0)
        dxp = cw[3:4] * dxc + cw[2:3] * _su(dxc, 1) + cw[1:2] * _su(dxc, 2) + cw[0:1] * _su(dxc, 3)
        dxp_ref[...] = dxp.astype(BF)

    blk = pl.BlockSpec((T, CB), lambda j: (0, j))
    wsp = pl.BlockSpec((1, CB, CB), lambda j: (j, 0, 0))
    return _call(
        body, name="bwd_lru", grid=(NG,),
        in_specs=[blk, pl.BlockSpec((T, CB), lambda j: (0, NG + j)), blk, blk,
                  pl.BlockSpec((4, CB), lambda j: (0, j)), _vec_spec(), wsp, _vec_spec(), wsp, _vec_spec(),
                  _vec_spec()],
        out_specs=[blk, blk, pl.BlockSpec((4, CB), lambda j: (0, j)), _vec_spec(), wsp, _vec_spec(), wsp,
                   _vec_spec(), _vec_spec()],
        out_shape=[jax.ShapeDtypeStruct((T, DR), BF), jax.ShapeDtypeStruct((T, DR), BF),
                   jax.ShapeDtypeStruct((4, DR), F32), jax.ShapeDtypeStruct((1, DR), F32),
                   jax.ShapeDtypeStruct((NG, CB, CB), F32), jax.ShapeDtypeStruct((1, DR), F32),
                   jax.ShapeDtypeStruct((NG, CB, CB), F32), jax.ShapeDtypeStruct((1, DR), F32),
                   jax.ShapeDtypeStruct((1, DR), F32)],
        vmem=56, args=[proj, proj, h, dylru, conv_w, conv_b, wa, ba, wx, bx, lam], stages=stages,
        scratch=[pltpu.VMEM((T, CB), F32)] * 3)


def _bwd_pool(proj, dypool, pool_w, pool_scale):
    def body(xp_ref, dy_ref, pw_ref, sc_ref, dx_ref, dw_ref, dsc_ref):
        for g, w in enumerate(POOL_WINDOWS):
            cols = slice(g * PG, (g + 1) * PG)
            cnt = _pool_cnt(w)
            x = xp_ref[:, cols]
            pb = (_pool_window(x, g + 1, _sd) / cnt - x).astype(BF)
            wg = pw_ref[g]
            dy = dy_ref[:, cols]
            dsc_ref[:, cols] = _colsum(dy * _mm(pb, wg))
            dyp = (dy * sc_ref[:, cols]).astype(BF)
            dw_ref[g] = _mm_tn(pb, dyp)
            dp = _mm_nt(dyp, wg)
            dx_ref[:, cols] = (_pool_window(dp / cnt, g + 1, _su) - dp).astype(BF)

    return pl.pallas_call(
        body, name="bwd_pool", grid=(1,),
        in_specs=[pl.BlockSpec((T, DP), lambda i: (0, 2 * DR // DP)),
                  pl.BlockSpec((T, DP), lambda i: (0, 0)),
                  pl.BlockSpec((4, PG, PG), lambda i: (0, 0, 0)),
                  pl.BlockSpec((1, DP), lambda i: (0, 0))],
        out_specs=[pl.BlockSpec((T, DP), lambda i: (0, 0)),
                   pl.BlockSpec((4, PG, PG), lambda i: (0, 0, 0)),
                   pl.BlockSpec((1, DP), lambda i: (0, 0))],
        out_shape=_hbm_out([jax.ShapeDtypeStruct((T, DP), BF), jax.ShapeDtypeStruct((4, PG, PG), F32),
                            jax.ShapeDtypeStruct((1, DP), F32)]),
        compiler_params=_cp(48),
    )(*_hbm(proj, dypool, pool_w, pool_scale))


def _bwd_inproj_w(h1, dproj, stages=()):
    def body(h_ref, dp_ref, dw_ref):
        dw_ref[0] = _mm_tn(h_ref[...], dp_ref[...]).astype(BF)

    outs, landed = _call(
        body, name="bwd_inproj_w", grid=(NCHIP,),
        in_specs=[pl.BlockSpec((T, D), lambda k: (0, 0)), pl.BlockSpec((T, CW_IN), lambda k: (0, k))],
        out_specs=[pl.BlockSpec((1, D, CW_IN), lambda k: (k, 0, 0))],
        out_shape=[jax.ShapeDtypeStruct((NCHIP, D, CW_IN), BF)], vmem=48, args=[h1, dproj], stages=stages)
    return outs[0], landed


def _bwd_inproj_x(dproj, w_in, x, dxres, g1, stages=()):
    tm = 512

    def body(dp_ref, w_ref, x_ref, dr_ref, g_ref, dx_ref, dg_ref):
        dh = None
        for k in range(NCHIP):
            part = _mm_nt(dp_ref[:, k * CW_IN:(k + 1) * CW_IN], w_ref[k])
            dh = part if dh is None else dh + part
        xv = x_ref[...]
        r = lax.rsqrt(_mean(xv * xv) + NORM_EPS)
        xn = xv * r
        t = dh * g_ref[...]
        dx_ref[...] = dr_ref[...] + r * (t - xn * _mean(t * xn))
        _acc(dg_ref, _colsum(dh * xn), pl.program_id(0) == 0)

    row = pl.BlockSpec((tm, D), lambda i: (i, 0))
    vec = pl.BlockSpec((1, D), lambda i: (0, 0))
    return _call(
        body, name="bwd_inproj_x", grid=(T // tm,),
        in_specs=[pl.BlockSpec((tm, DIN), lambda i: (i, 0)), pl.BlockSpec((NCHIP, D, CW_IN), lambda i: (0, 0, 0)),
                  row, row, vec],
        out_specs=[row, vec],
        out_shape=[jax.ShapeDtypeStruct((T, D), F32), jax.ShapeDtypeStruct((1, D), F32)],
        vmem=56, args=[dproj, w_in, x, dxres, g1], stages=stages)[0]


def _place():
    x, y, c = lax.axis_index("x"), lax.axis_index("y"), lax.axis_index("c")
    chips = [(1 - x, y), (x, 1 - y), (1 - x, 1 - y)]
    return x, y, c, chips


def _rcopy(src, dst, ssem, rsem, dev):
    return pltpu.make_async_remote_copy(src_ref=src, dst_ref=dst, send_sem=ssem, recv_sem=rsem,
                                        device_id=dev, device_id_type=MESH_ID)


def _sds(a):
    return jax.ShapeDtypeStruct(a.shape, a.dtype)


def _sem2(n, m):
    return [pltpu.SemaphoreType.DMA((n * m,)), pltpu.SemaphoreType.DMA((n * m,))]


ALL = (0, 1, 1)


def _piece(ref, k, half, part):
    hr = ref.shape[1] // 2
    r0, r1 = hr * part[0] // part[2], hr * part[1] // part[2]
    return ref.at[k, pl.ds(half * hr + r0, r1 - r0), :]


def _gather(fulls, ici=(), d2d=()):
    n = len(fulls)
    ici, d2d = list(ici), list(d2d)
    pieces = [("ici", i, part) for i, part in ici] + [("d2d", i, part) for i, part in d2d]

    def copies(outs, sems):
        x, y, c, chips = _place()
        me = 2 * x + y
        sib = (x, y, 1 - c)
        send, recv = [], []
        for q, (kind, i, part) in enumerate(pieces):
            for j, chip in enumerate(chips):
                k, s = 2 * chip[0] + chip[1], 3 * q + j
                if kind == "ici":
                    mine, theirs, dev = _piece(outs[i], me, c, part), _piece(outs[i], k, c, part), (*chip, c)
                else:
                    mine, theirs, dev = _piece(outs[i], k, c, part), _piece(outs[i], k, 1 - c, part), sib
                send.append(_rcopy(mine, mine, sems[0].at[s], sems[1].at[s], dev))
                recv.append(_rcopy(theirs, theirs, sems[0].at[s], sems[1].at[s], dev))
        return send, recv

    def start(ins, outs, sems):
        for cp in copies(outs, sems)[0]:
            cp.start()

    def finish(ins, outs, sems):
        send, recv = copies(outs, sems)
        for cp in recv:
            cp.wait_recv()
        for cp in send:
            cp.wait_send()

    sems = [pltpu.SemaphoreType.DMA((3 * len(pieces),)), pltpu.SemaphoreType.DMA((3 * len(pieces),))]
    return _Stage(fulls, [_sds(f) for f in fulls], {i: i for i in range(n)}, sems, start, finish)


def _gather_whole(v):
    def copies(ins, outs, sems):
        x, y, c, chips = _place()
        me = 2 * x + y
        send = [_rcopy(ins[0], outs[0].at[me], sems[0].at[j], sems[1].at[j], (*chip, c))
                for j, chip in enumerate(chips)]
        recv = [_rcopy(ins[0], outs[0].at[2 * chip[0] + chip[1]], sems[0].at[j], sems[1].at[j], (*chip, c))
                for j, chip in enumerate(chips)]
        return send, recv

    def start(ins, outs, sems):
        for cp in copies(ins, outs, sems)[0]:
            cp.start()

    def finish(ins, outs, sems):
        send, recv = copies(ins, outs, sems)
        for cp in recv:
            cp.wait_recv()
        for cp in send:
            cp.wait_send()

    return _Stage([v], [jax.ShapeDtypeStruct((NCHIP,) + v.shape, v.dtype)], {},
                  [pltpu.SemaphoreType.DMA((3,)), pltpu.SemaphoreType.DMA((3,))], start, finish)


def _to_sibling(srcs):
    n = len(srcs)

    def copies(ins, outs, sems):
        x, y, c, _ = _place()
        sib = (x, y, 1 - c)
        return [_rcopy(ins[i].at[:, 1 - c] if srcs[i].ndim == 4 else ins[i], outs[i], sems[0].at[i], sems[1].at[i], sib)
                for i in range(n)]

    def start(ins, outs, sems):
        for cp in copies(ins, outs, sems):
            cp.start()

    def finish(ins, outs, sems):
        for cp in copies(ins, outs, sems):
            cp.wait()

    shapes = [jax.ShapeDtypeStruct((NCHIP,) + s.shape[2:] if s.ndim == 4 else s.shape, s.dtype) for s in srcs]
    return _Stage(srcs, shapes, {}, [pltpu.SemaphoreType.DMA((n,)), pltpu.SemaphoreType.DMA((n,))], start, finish)


def _to_chips(srcs, parts=None, lands=None):
    n = len(srcs)
    parts = [ALL] * n if parts is None else parts
    lands = [None] * n if lands is None else lands
    given = [i for i in range(n) if lands[i] is not None]

    def rows(ref, i):
        hr = srcs[i].shape[1]
        r0, r1 = hr * parts[i][0] // parts[i][2], hr * parts[i][1] // parts[i][2]
        return ref.at[pl.ds(r0, r1 - r0), :]

    def copies(ins, outs, sems):
        x, y, c, chips = _place()
        me = 2 * x + y
        return [_rcopy(rows(ins[i].at[2 * chip[0] + chip[1]] if srcs[i].shape[0] == NCHIP else ins[i].at[c], i),
                       rows(outs[i].at[me], i), sems[0].at[3 * i + j], sems[1].at[3 * i + j], (*chip, c))
                for i in range(n) for j, chip in enumerate(chips)]

    def start(ins, outs, sems):
        for cp in copies(ins, outs, sems):
            cp.start()

    def finish(ins, outs, sems):
        for cp in copies(ins, outs, sems):
            cp.wait()

    shapes = [jax.ShapeDtypeStruct((NCHIP,) + s.shape[1:], s.dtype) for s in srcs]
    alias = {n + q: i for q, i in enumerate(given)}
    return _Stage(list(srcs) + [lands[i] for i in given], shapes, alias, _sem2(n, 3), start, finish)


HBM_REF = pl.BlockSpec(memory_space=pltpu.HBM)
SEM_REF = pl.BlockSpec(memory_space=pltpu.SEMAPHORE)
DATAFLOW = pltpu.SideEffectType.DATAFLOW_SIDE_EFFECTING


def _after(x):
    return _Stage([x], [], {}, [], lambda *a: None, lambda *a: None)


class _Flight:
    def __init__(self, stage, sems, bufs):
        self.stage, self.sems, self.bufs = stage, list(sems), list(bufs)

    def landed(self):
        st, n = self.stage, len(self.stage.operands)
        fresh = [j for j in range(len(st.out_shape)) if j not in st.alias.values()]
        back = {v: k for k, v in st.alias.items()}
        return [self.bufs[back[j]] if j in back else self.bufs[n + fresh.index(j)] for j in range(len(st.out_shape))]


def _split_call(name, finish=(), start=(), after=None):
    bufs, stage_bufs = [], []

    def slot(a):
        for i, b in enumerate(bufs):
            if b is a:
                return i
        bufs.append(a)
        return len(bufs) - 1

    fin_slots = [[slot(b) for b in fl.bufs] for fl in finish]
    for st in start:
        fresh = [lax.empty(o.shape, o.dtype) for j, o in enumerate(st.out_shape) if j not in st.alias.values()]
        stage_bufs.append([slot(a) for a in list(st.operands) + fresh])
    old_sems = [s for fl in finish for s in fl.sems]
    new_sems = [s for st in start for s in st.sems]
    nb, no, nn = len(bufs), len(old_sems), len(new_sems)

    def refs_of(st, slots, buf_refs):
        n = len(st.operands)
        ins = [buf_refs[i] for i in slots[:n]]
        fresh = [j for j in range(len(st.out_shape)) if j not in st.alias.values()]
        back = {v: k for k, v in st.alias.items()}
        outs = [ins[back[j]] if j in back else buf_refs[slots[n + fresh.index(j)]] for j in range(len(st.out_shape))]
        return ins, outs

    def body(*refs):
        buf_refs, sem_in = refs[:nb], refs[nb:nb + no]
        sem_out = refs[nb + no + (after is not None):][:nn]
        token = refs[-1]
        pos = 0
        for fl, slots in zip(finish, fin_slots):
            ins, outs = refs_of(fl.stage, slots, buf_refs)
            fl.stage.finish(ins, outs, sem_in[pos:pos + len(fl.sems)])
            pos += len(fl.sems)
        pos = 0
        for st, slots in zip(start, stage_bufs):
            ins, outs = refs_of(st, slots, buf_refs)
            st.start(ins, outs, sem_out[pos:pos + len(st.sems)])
            pos += len(st.sems)
        token[...] = jnp.zeros_like(token)

    res = pl.pallas_call(
        body, name=name,
        out_shape=tuple(new_sems) + tuple(pltpu.HBM(b.shape, b.dtype) for b in bufs) +
                  (jax.ShapeDtypeStruct((8, LANE), F32),),
        in_specs=(HBM_REF,) * nb + (SEM_REF,) * no + ((pl.BlockSpec(memory_space=pl.ANY),) if after is not None else ()),
        out_specs=(SEM_REF,) * nn + (HBM_REF,) * nb + (pl.BlockSpec(memory_space=pltpu.VMEM),),
        input_output_aliases={i: nn + i for i in range(nb)},
        compiler_params=pltpu.CompilerParams(has_side_effects=DATAFLOW),
    )(*_hbm(*bufs), *old_sems, *([after] if after is not None else []))
    sems, thru, token = res[:nn], res[nn:nn + nb], res[-1]
    for fl, slots in zip(finish, fin_slots):
        fl.bufs = [thru[i] for i in slots]
    flights, pos = [], 0
    for st, slots in zip(start, stage_bufs):
        flights.append(_Flight(st, sems[pos:pos + len(st.sems)], [thru[i] for i in slots]))
        pos += len(st.sems)
    return flights, token


def _last_copies(p_ref, land_ref, ssem, rsem):
    x, y, c, chips = _place()
    me = 2 * x + y
    send = [_rcopy(p_ref.at[2 * chip[0] + chip[1]], land_ref.at[me], ssem.at[j], rsem.at[j], (*chip, c))
            for j, chip in enumerate(chips)]
    recv = [_rcopy(p_ref.at[2 * chip[0] + chip[1]], land_ref.at[2 * chip[0] + chip[1]], ssem.at[j], rsem.at[j],
                   (*chip, c)) for j, chip in enumerate(chips)]
    return send, recv


def _chips_start(p):
    def body(p_ref, land_ref, ssem, rsem, p_thru, land_thru, token):
        for cp in _last_copies(p_ref, land_ref, ssem, rsem)[0]:
            cp.start()
        token[...] = jnp.zeros_like(token)

    return pl.pallas_call(
        body, name="reduce_last_start",
        out_shape=(pltpu.SemaphoreType.DMA((3,)), pltpu.SemaphoreType.DMA((3,)), pltpu.HBM(p.shape, p.dtype),
                   pltpu.HBM(p.shape, p.dtype), jax.ShapeDtypeStruct((8, LANE), F32)),
        in_specs=(HBM_REF, HBM_REF),
        out_specs=(SEM_REF, SEM_REF, HBM_REF, HBM_REF, pl.BlockSpec(memory_space=pltpu.VMEM)),
        input_output_aliases={0: 2, 1: 3},
        compiler_params=pltpu.CompilerParams(has_side_effects=DATAFLOW),
    )(*_hbm(p, lax.empty(p.shape, p.dtype)))


def _chips_wait(ssem, rsem, p_thru, land_thru, after):
    def body(p_ref, land_ref, ssem, rsem, after_ref, p_dead, got_ref):
        send, recv = _last_copies(p_ref, land_ref, ssem, rsem)
        for cp in send:
            cp.wait_send()
        for cp in recv:
            cp.wait_recv()

    return pl.pallas_call(
        body, name="reduce_last_wait",
        out_shape=(pltpu.HBM(p_thru.shape, p_thru.dtype), pltpu.HBM(land_thru.shape, land_thru.dtype)),
        in_specs=(HBM_REF, HBM_REF, SEM_REF, SEM_REF, pl.BlockSpec(memory_space=pl.ANY)),
        out_specs=(HBM_REF, HBM_REF), input_output_aliases={0: 0, 1: 1},
        compiler_params=pltpu.CompilerParams(has_side_effects=DATAFLOW),
    )(p_thru, land_thru, ssem, rsem, after)


def _share(pairs):
    n = len(pairs)

    def start(ins, outs, sems):
        x, y, c, _ = _place()
        for i in range(n):
            _rcopy(outs[i].at[c], outs[i].at[c], sems[0].at[i], sems[1].at[i], (x, y, 1 - c)).start()

    def finish(ins, outs, sems):
        x, y, c, _ = _place()
        for i in range(n):
            _rcopy(outs[i].at[c], outs[i].at[c], sems[0].at[i], sems[1].at[i], (x, y, 1 - c)).wait_send()
            _rcopy(outs[i].at[1 - c], outs[i].at[1 - c], sems[0].at[i], sems[1].at[i], (x, y, 1 - c)).wait_recv()

    return _Stage(pairs, [_sds(p) for p in pairs], {i: i for i in range(n)},
                  [pltpu.SemaphoreType.DMA((n,)), pltpu.SemaphoreType.DMA((n,))], start, finish)


def _row_block(rows, cols, itemsize=4, target=2 * MIB):
    br = rows
    while br * cols * itemsize > target and br % 32 == 0:
        br //= 2
    return br


def _cast_place(w, chip_idx, name):
    rows, cols = w.shape
    br = _row_block(rows, cols)

    def body(k_ref, w_ref, o_ref):
        o_ref[0] = w_ref[...].astype(BF)

    return _call(
        body, name=name, grid=(rows // br,), prefetch=chip_idx,
        in_specs=[pl.BlockSpec((br, cols), lambda r, k: (r, 0))],
        out_specs=[pl.BlockSpec((1, br, cols), lambda r, k: (k[0], r, 0))],
        out_shape=[jax.ShapeDtypeStruct((NCHIP, rows, cols), BF)], vmem=32, args=[w])[0][0]


def _cast_place_multi(ws, chip_idx, stages=()):
    br = 128
    nblk = [a.shape[0] // br for a in ws]
    starts = [sum(nblk[:i]) for i in range(len(ws))]

    def body(k_ref, *refs):
        r = pl.program_id(0)
        for i in range(len(ws)):
            @pl.when(jnp.logical_and(r >= starts[i], r < starts[i] + nblk[i]))
            def _(i=i):
                refs[len(ws) + i][0] = refs[i][...].astype(BF)

    def at(i):
        return functools.partial(lambda r, s, nb: jnp.clip(r - s, 0, nb - 1), s=starts[i], nb=nblk[i])

    outs, landed = _call(
        body, name="cast_rest", grid=(sum(nblk),), prefetch=chip_idx,
        in_specs=[pl.BlockSpec((br, a.shape[1]), functools.partial(lambda r, k, f: (f(r), 0), f=at(i)))
                  for i, a in enumerate(ws)],
        out_specs=[pl.BlockSpec((1, br, a.shape[1]), functools.partial(lambda r, k, f: (k[0], f(r), 0), f=at(i)))
                   for i, a in enumerate(ws)],
        out_shape=[jax.ShapeDtypeStruct((NCHIP,) + a.shape, BF) for a in ws], vmem=32, args=list(ws), stages=stages)
    return outs, landed


def _add_sibling(g, land, cidx, name, stages=()):
    _, _, hr, cols = g.shape
    br = _row_block(hr, cols)

    def body(c_ref, g_ref, l_ref, o_ref):
        o_ref[...] = (g_ref[0, 0].astype(F32) + l_ref[0].astype(F32)).astype(BF)[None]

    outs, st = _call(
        body, name=name, grid=(NCHIP, hr // br), prefetch=cidx,
        in_specs=[pl.BlockSpec((1, 1, br, cols), lambda k, r, c: (k, c[0], r, 0)),
                  pl.BlockSpec((1, br, cols), lambda k, r, c: (k, r, 0))],
        out_specs=[pl.BlockSpec((1, br, cols), lambda k, r, c: (k, r, 0))],
        out_shape=[jax.ShapeDtypeStruct((NCHIP, hr, cols), BF)], vmem=32, args=[g, land], stages=stages)
    return outs[0], st


def _add_sibling_multi(gs, lands, cidx, name):
    n = len(gs)
    brs = [_row_block(g.shape[2], g.shape[3]) for g in gs]
    nrb = [g.shape[2] // b for g, b in zip(gs, brs)]
    nblk = [NCHIP * q for q in nrb]
    starts = [sum(nblk[:i]) for i in range(n)]

    def body(c_ref, *refs):
        r = pl.program_id(0)
        for i in range(n):
            g_ref, l_ref, o_ref = refs[2 * i], refs[2 * i + 1], refs[2 * n + i]

            @pl.when(jnp.logical_and(r >= starts[i], r < starts[i] + nblk[i]))
            def _():
                o_ref[...] = (g_ref[0, 0].astype(F32) + l_ref[0].astype(F32)).astype(BF)[None]

    def at(i, r):
        q = jnp.clip(r - starts[i], 0, nblk[i] - 1)
        return q // nrb[i], q % nrb[i]

    def g_spec(i):
        return pl.BlockSpec((1, 1, brs[i], gs[i].shape[3]),
                            functools.partial(lambda r, c, i: (at(i, r)[0], c[0], at(i, r)[1], 0), i=i))

    def l_spec(i):
        return pl.BlockSpec((1, brs[i], gs[i].shape[3]),
                            functools.partial(lambda r, c, i: (at(i, r)[0], at(i, r)[1], 0), i=i))

    return _call(
        body, name=name, grid=(sum(nblk),), prefetch=cidx,
        in_specs=[s for i in range(n) for s in (g_spec(i), l_spec(i))], out_specs=[l_spec(i) for i in range(n)],
        out_shape=[jax.ShapeDtypeStruct(l.shape, BF) for l in lands], vmem=32,
        args=[a for i in range(n) for a in (gs[i], lands[i])])[0]


def _add_pair(a, b, name):
    rows, cols = a.shape

    def body(a_ref, b_ref, o_ref):
        o_ref[...] = a_ref[...] + b_ref[...]

    spec = pl.BlockSpec((rows, cols), lambda r: (0, 0))
    return _call(body, name=name, grid=(1,), in_specs=[spec, spec], out_specs=[spec], out_shape=[_sds(a)],
                 vmem=32, args=[a, b])[0][0]


def _add_chips(own, land, idx, name, stages=None):
    _, hr, cols = land.shape
    br = _row_block(hr, cols)

    def body(s_ref, a_ref, b_ref, c_ref, d_ref, o_ref):
        o_ref[...] = (a_ref[...].astype(F32) + b_ref[...].astype(F32)) + (c_ref[...].astype(F32) +
                                                                           d_ref[...].astype(F32))

    spec = lambda q: pl.BlockSpec((1, br, cols), functools.partial(lambda r, s, q: (s[q], r, 0), q=q))
    outs, landed = _call(
        body, name=name, grid=(hr // br,), prefetch=idx,
        in_specs=[spec(0), spec(1), spec(2), spec(3)], out_specs=[spec(4)],
        out_shape=[jax.ShapeDtypeStruct((2, hr, cols), F32)], vmem=48, args=[own, land, land, land],
        stages=stages or ())
    return outs[0] if stages is None else (outs[0], landed)


def _add_chips_multi(owns, lands, idx, name, stages=()):
    n = len(owns)
    brs = [_row_block(l.shape[1], l.shape[2]) for l in lands]
    nblk = [l.shape[1] // b for l, b in zip(lands, brs)]
    starts = [sum(nblk[:i]) for i in range(n)]

    def body(s_ref, *refs):
        r = pl.program_id(0)
        for i in range(n):
            a_ref, b_ref, c_ref, d_ref = refs[4 * i:4 * i + 4]
            o_ref = refs[4 * n + i]

            @pl.when(jnp.logical_and(r >= starts[i], r < starts[i] + nblk[i]))
            def _():
                o_ref[...] = (a_ref[...].astype(F32) + b_ref[...].astype(F32)) + (c_ref[...].astype(F32) +
                                                                                   d_ref[...].astype(F32))

    def spec(i, q):
        return pl.BlockSpec((1, brs[i], lands[i].shape[2]), functools.partial(
            lambda r, s, q, st, nb: (s[q], jnp.clip(r - st, 0, nb - 1), 0), q=q, st=starts[i], nb=nblk[i]))

    outs, landed = _call(
        body, name=name, grid=(sum(nblk),), prefetch=idx,
        in_specs=[spec(i, q) for i in range(n) for q in range(4)], out_specs=[spec(i, 4) for i in range(n)],
        out_shape=[jax.ShapeDtypeStruct((2,) + l.shape[1:], F32) for l in lands], vmem=48,
        args=[a for i in range(n) for a in (owns[i], lands[i], lands[i], lands[i])], stages=stages)
    return outs, landed


def _adamw_math(w, g, m, v):
    mn = ADAM_B1 * m + (1.0 - ADAM_B1) * g
    vn = ADAM_B2 * v + (1.0 - ADAM_B2) * (g * g)
    m_hat = mn / (1.0 - ADAM_B1 ** ADAM_STEP)
    v_hat = vn / (1.0 - ADAM_B2 ** ADAM_STEP)
    return -ADAM_LR * (m_hat / (jnp.sqrt(v_hat) + ADAM_EPS) + ADAM_WD * w), mn, vn


def _adamw(w, g, m, v, name, stages=()):
    rows, cols = w.shape
    br = _row_block(rows, cols)

    def body(w_ref, g_ref, m_ref, v_ref, go_ref, d_ref, mo_ref, vo_ref):
        gv = g_ref[...]
        go_ref[...] = gv
        d_ref[...], mo_ref[...], vo_ref[...] = _adamw_math(w_ref[...], gv, m_ref[...], v_ref[...])

    spec = pl.BlockSpec((br, cols), lambda r: (r, 0))
    return _call(body, name=name, grid=(rows // br,), in_specs=[spec] * 4, out_specs=[spec] * 4,
                 out_shape=[_sds(w)] * 4, vmem=56, args=[w, g, m, v], stages=stages)


def _adamw_multi(names, w, g, m, v, stages=()):
    cols = w[names[0]].shape[1]
    br = 128
    nblk = [w[n].shape[0] // br for n in names]
    starts = [sum(nblk[:i]) for i in range(len(names))]

    def body(*refs):
        r = pl.program_id(0)
        for i in range(len(names)):
            w_ref, g_ref, m_ref, v_ref = refs[4 * i:4 * i + 4]
            go_ref, d_ref, mo_ref, vo_ref = refs[4 * len(names) + 4 * i:4 * len(names) + 4 * i + 4]

            @pl.when(jnp.logical_and(r >= starts[i], r < starts[i] + nblk[i]))
            def _():
                gv = g_ref[...]
                go_ref[...] = gv
                d_ref[...], mo_ref[...], vo_ref[...] = _adamw_math(w_ref[...], gv, m_ref[...], v_ref[...])

    def spec(i):
        return pl.BlockSpec((br, cols), functools.partial(
            lambda r, s, nb: (jnp.clip(r - s, 0, nb - 1), 0), s=starts[i], nb=nblk[i]))

    outs, landed = _call(
        body, name="adamw_" + "_".join(names), grid=(sum(nblk),),
        in_specs=[spec(i) for i in range(len(names)) for _ in range(4)],
        out_specs=[spec(i) for i in range(len(names)) for _ in range(4)],
        out_shape=[_sds(w[n]) for n in names for _ in range(4)], vmem=56,
        args=[a[n] for n in names for a in (w, g, m, v)], stages=stages)
    return {n: outs[4 * i:4 * i + 4] for i, n in enumerate(names)}, landed


def _to_everyone(v):
    deltas = [(a, b, e) for a in (0, 1) for b in (0, 1) for e in (0, 1)][1:]

    def copies(ins, outs, sems):
        x, y, c, _ = _place()
        me = 4 * x + 2 * y + c
        flip = lambda p, f: 1 - p if f else p
        return [_rcopy(ins[0], outs[0].at[me], sems[0].at[q], sems[1].at[q], (flip(x, a), flip(y, b), flip(c, e)))
                for q, (a, b, e) in enumerate(deltas)]

    def start(ins, outs, sems):
        for cp in copies(ins, outs, sems):
            cp.start()

    def finish(ins, outs, sems):
        for cp in copies(ins, outs, sems):
            cp.wait()

    n = len(deltas)
    return _Stage([v], [jax.ShapeDtypeStruct((2 * NCHIP,) + v.shape, v.dtype)], {},
                  [pltpu.SemaphoreType.DMA((n,)), pltpu.SemaphoreType.DMA((n,))], start, finish)


SMALL_AT = {"norm_mix_pre": (0, 1, D), "norm_mix_post": (1, 1, D), "norm_mlp_pre": (2, 1, D),
            "norm_mlp_post": (3, 1, D), "b_gate": (4, 2, D), "conv_b": (6, 1, D), "lru_b_a": (7, 1, D),
            "lru_b_x": (8, 1, D), "lru_lambda": (9, 1, D), "pool_scale": (10, 1, DP)}
SMALL_SEPARATE = ["conv_w", "lru_w_a", "lru_w_x", "pool_w"]


def _adamw_small(small_sum, first_all, sep_grads, w, m, v):
    packed, sep = list(SMALL_AT), list(SMALL_SEPARATE)
    names = packed + sep

    def body(*refs):
        s_ref, a_ref, refs = refs[0], refs[1], refs[2:]
        g_sep, refs = refs[:len(sep)], refs[len(sep):]
        nn = len(names)
        w_r, m_r, v_r, refs = refs[:nn], refs[nn:2 * nn], refs[2 * nn:3 * nn], refs[3 * nn:]
        g_out, refs = refs[:len(packed)], refs[len(packed):]
        d_o, m_o, v_o = refs[:nn], refs[nn:2 * nn], refs[2 * nn:3 * nn]
        for i, n in enumerate(names):
            if i == 0:
                g = a_ref[0:1, :]
                for q in range(1, 2 * NCHIP):
                    g = g + a_ref[q:q + 1, :]
                g_out[i][...] = g
            elif n in SMALL_AT:
                r0, nr, nc = SMALL_AT[n]
                g = jnp.concatenate([s_ref[r0 + q:r0 + q + 1, :nc] for q in range(nr)], axis=1)
                g_out[i][...] = g
            else:
                g = g_sep[i - len(packed)][...]
            d_o[i][...], m_o[i][...], v_o[i][...] = _adamw_math(w_r[i][...], g, m_r[i][...], v_r[i][...])

    ws = [w[n] for n in names]
    res = pl.pallas_call(
        body, name="adamw_small",
        out_shape=[_sds(w[n]) for n in packed] + [_sds(a) for a in ws] * 3,
        compiler_params=_cp(32),
    )(*_hbm(small_sum, first_all, *sep_grads, *ws, *[m[n] for n in names], *[v[n] for n in names]))
    nn, npk = len(names), len(packed)
    grad = dict(zip(packed, res[:npk]))
    delta = dict(zip(names, res[npk:npk + nn]))
    new_m = dict(zip(names, res[npk + nn:npk + 2 * nn]))
    new_v = dict(zip(names, res[npk + 2 * nn:]))
    return grad, delta, new_m, new_v


W_NAMES = ["norm_mix_pre", "norm_mix_post", "norm_mlp_pre", "norm_mlp_post", "w_in", "b_gate", "conv_w", "conv_b",
           "lru_w_a", "lru_b_a", "lru_w_x", "lru_b_x", "lru_lambda", "pool_w", "pool_scale", "w_lru_up",
           "w_pool_up", "w_o", "w_ff1", "w_ff2"]
BIG = ["w_in", "w_lru_up", "w_pool_up", "w_o", "w_ff1", "w_ff2"]


def _block_diag(w):
    hd = w.shape[-1]
    per = CB // hd
    w4 = w.reshape(NG, per, hd, hd)
    eye = jnp.eye(per, dtype=w.dtype)
    return jnp.einsum("gpij,pq->gpiqj", w4, eye).reshape(NG, CB, CB)


def _block_diag_extract(d, hd):
    per = CB // hd
    d5 = d.reshape(NG, per, hd, per, hd)
    return jnp.stack([d5[:, p, :, p, :] for p in range(per)], axis=1).reshape(NG * per, hd, hd)


def _halves(g):
    return g.reshape(NCHIP, 2, g.size // (g.shape[-1] * 2 * NCHIP), g.shape[-1])


def kernel(x, norm_mix_pre, norm_mix_post, norm_mlp_pre, norm_mlp_post, w_in, b_gate, conv_w, conv_b, lru_w_a, lru_b_a, lru_w_x, lru_b_x, lru_lambda, pool_w, pool_scale, w_lru_up, w_pool_up, w_o, w_ff1, w_ff2, loss_target, m_norm_mix_pre, m_norm_mix_post, m_norm_mlp_pre, m_norm_mlp_post, m_w_in, m_b_gate, m_conv_w, m_conv_b, m_lru_w_a, m_lru_b_a, m_lru_w_x, m_lru_b_x, m_lru_lambda, m_pool_w, m_pool_scale, m_w_lru_up, m_w_pool_up, m_w_o, m_w_ff1, m_w_ff2, v_norm_mix_pre, v_norm_mix_post, v_norm_mlp_pre, v_norm_mlp_post, v_w_in, v_b_gate, v_conv_w, v_conv_b, v_lru_w_a, v_lru_b_a, v_lru_w_x, v_lru_b_x, v_lru_lambda, v_pool_w, v_pool_scale, v_w_lru_up, v_w_pool_up, v_w_o, v_w_ff1, v_w_ff2):
    args = dict(locals())
    two_d = lambda a: a.reshape(-1, a.shape[-1])
    w = {n: two_d(args[n]) for n in W_NAMES}
    mom = {n: two_d(args["m_" + n]) for n in W_NAMES}
    var = {n: two_d(args["v_" + n]) for n in W_NAMES}
    i32 = lambda val: jnp.asarray(val, jnp.int32)
    chip = i32(2 * lax.axis_index("x") + lax.axis_index("y"))
    core = i32(lax.axis_index("c"))
    cidx = core.reshape(1)
    zero = i32(0)
    hd = lru_w_a.shape[-1]
    xs, target = x[0], loss_target[0]
    g1, g2, g3, g4 = norm_mix_pre, norm_mix_post, norm_mlp_pre, norm_mlp_post

    mix = ["w_lru_up", "w_pool_up", "w_o"]
    full = {"w_in": _cast_place(w["w_in"], chip.reshape(1), "cast_w_in")}
    (fl_in, fl_conv), first = _split_call("gather_start_first", start=[
        _gather([full["w_in"]], ici=[(0, ALL)]), _gather_whole(w["conv_w"])])
    casts, _ = _cast_place_multi([w[n] for n in BIG[1:]], chip.reshape(1), stages=[_after(first)])
    full.update(zip(BIG[1:], casts))
    (fl_mix, fl_ff1, fl_ff2), started = _split_call("gather_start_rest", start=[
        _gather([full[n] for n in mix], ici=[(0, ALL), (1, ALL), (2, ALL)]),
        _gather([full["w_ff1"]], ici=[(0, ALL)]), _gather([full["w_ff2"]], ici=[(0, ALL)])])
    wa = _block_diag(lru_w_a[0]).astype(BF)
    wx = _block_diag(lru_w_x[0]).astype(BF)
    pw = pool_w[0].astype(BF)

    def to_sibling(name, flight, after=None):
        (fl,), passed = _split_call(name + "_pass", finish=[flight], after=after,
                                    start=[_gather(flight.landed(), d2d=[(i, ALL) for i in range(len(flight.bufs))])])
        passed_on.append(passed)
        return fl

    passed_on = []

    def arrived(name, flight, after=None):
        _split_call(name + "_done", finish=[flight], after=after)
        return flight.landed()

    idx_big = jnp.stack([chip, (chip + 1) % NCHIP, (chip + 2) % NCHIP, (chip + 3) % NCHIP, core])
    proj, h1 = _fwd_inproj_own(xs, g1, fl_in.bufs[0], idx_big, stages=[_after(started)])
    fl_in = to_sibling("gather_w_in", fl_in, after=h1)
    w_in_f, = arrived("gather_w_in", fl_in)
    conv_all, = arrived("gather_conv", fl_conv)
    full["w_in"] = w_in_f
    conv_all = lax.dynamic_update_slice(conv_all, w["conv_w"][None], (chip, zero, zero))
    conv_full = jnp.transpose(conv_all, (1, 0, 2)).reshape(4, DR)
    proj = _fwd_inproj_rest(h1, w_in_f, proj, idx_big)
    fl_mix = to_sibling("gather_mix", fl_mix, after=proj)
    (ylru, hs), _ = _fwd_lru(proj, conv_full, conv_b, wa, lru_b_a, wx, lru_b_x, lru_lambda,
                             stages=[_after(passed_on[-1])])
    got = arrived("gather_mix", fl_mix, after=ylru)
    fl_ff1 = to_sibling("gather_ff1", fl_ff1, after=ylru)
    w_lru_up_f, w_pool_up_f, w_o_f = got[0].reshape(DR, D), got[1], got[2].reshape(D, D)
    ypool = _fwd_pool(proj, pw, pool_scale)
    (x2, h2, m, mrg, bra, brb), _ = _fwd_merge(xs, ylru, ypool, proj, b_gate, g2, g3, w_lru_up_f, w_pool_up_f, w_o_f,
                                               stages=[_after(passed_on[-1])])
    fl_ff2 = to_sibling("gather_ff2", fl_ff2, after=h2)
    ff1, = arrived("gather_ff1", fl_ff1, after=h2)
    ff2, = arrived("gather_ff2", fl_ff2)
    ff2 = ff2.reshape(DF, D)
    a1, f = _fwd_mlp(h2, ff1, ff2)
    lossp, dy, df, dg4 = _loss_head(f, x2, target, g4)

    dh2, df1 = _bwd_mlp_x(df, a1, ff1, ff2)
    dw_ff1, dw_ff2 = _bwd_mlp_w(df, h2, a1, df1)
    g_ff = [_halves(dw_ff1), _halves(dw_ff2)]
    (dxres, dgates, dylru, dypool, dm, dbra, dbrb, dg2, dg3, dbg), (l_ff,) = _bwd_merge(
        dh2, dy, x2, m, bra, brb, proj, b_gate, g2, g3, w_lru_up_f, w_pool_up_f, w_o_f, stages=[_to_sibling(g_ff)])
    p_ff = _add_sibling_multi(g_ff, l_ff, cidx, "add_sibling_ff")
    (fl_ff,), sent_ff = _split_call("reduce_ff_start", start=[_to_chips(p_ff)])
    (dw_o, dw_lru_up, dw_pool_up), _ = _dw_merge(mrg, dm, ylru, dbra, ypool, dbrb, stages=[_after(sent_ff)])
    g_mix = [_halves(dw_lru_up), _halves(dw_pool_up), _halves(dw_o)]
    (dxp, dgl, dcw, dcb, dwa, dba, dwx, dbx, dlam), (l_mix,) = _bwd_lru(
        proj, hs, dylru, conv_full, conv_b, wa, lru_b_a, wx, lru_b_x, lru_lambda, stages=[_to_sibling(g_mix)])
    p_mix = _add_sibling_multi(g_mix, l_mix, cidx, "add_sibling_mix")
    dxpool, dpw, dsc = _bwd_pool(proj, dypool, pw, pool_scale)
    dproj = jnp.concatenate([dxp, dgl, dxpool, dgates], axis=1)
    small = jnp.concatenate([
        jnp.zeros((1, D), F32), dg2, dg3, dg4, dbg.reshape(2, D), dcb, dba, dbx, dlam,
        jnp.pad(dsc, ((0, 0), (0, D - DP))), jnp.pad(lossp, ((0, 0), (0, D - 1))), dcw,
        _block_diag_extract(dwa, hd).reshape(-1, D), _block_diag_extract(dwx, hd).reshape(-1, D),
        dpw.reshape(-1, D)], axis=0)
    dw_in, (c_mix, (l_small,)) = _bwd_inproj_w(h1, dproj, stages=[_to_chips(p_mix), _to_sibling([small])])
    small2 = _add_pair(small, l_small, "add_sibling_small").reshape(2, SMALL_ROWS // 2, D)
    g_in = _halves(dw_in)
    done = ["w_ff1", "w_ff2"] + mix
    (fl_gin, fl_small), _ = _split_call("reduce_in_sibling_start", start=[_to_sibling([g_in]), _to_chips([small2])])
    _split_call("reduce_in_sibling_done", finish=[fl_gin, fl_ff])
    (g_in, l_in), (p_ff1, p_ff2, c_ff1, c_ff2) = fl_gin.bufs, fl_ff.bufs
    p_in = _add_sibling(g_in, l_in, cidx, "add_sibling_w_in")[0]
    ssem, rsem, p_in, c_in, token = _chips_start(p_in)
    pairs, _ = _add_chips_multi([p_ff1, p_ff2] + p_mix, [c_ff1, c_ff2] + c_mix, idx_big, "add_chips_done",
                                stages=[_after(token)])
    _split_call("reduce_small_done", finish=[fl_small], after=pairs[-1])
    small2, c_small = fl_small.bufs
    own_small = lax.dynamic_index_in_dim(small2, core, 0, keepdims=True)
    c_small = lax.dynamic_update_slice(c_small, own_small, (chip, zero, zero))
    pair_small = _add_chips(c_small, c_small, jnp.stack([zero, zero + 1, zero + 2, zero + 3, core]), "add_chips_small")
    (fl_share,), shared_start = _split_call("reduce_share_start", start=[_share(pairs + [pair_small])])
    grad_x, dg1 = _bwd_inproj_x(dproj, full["w_in"], xs, dxres, g1, stages=[_after(shared_start)])
    _split_call("reduce_share_done", finish=[fl_share], after=dg1)
    shared = fl_share.landed()
    pairs, pair_small = shared[:-1], shared[-1]

    grads, delta, new_m, new_v = {}, {}, {}, {}
    for n, p in zip(done, pairs):
        grads[n] = p.reshape(-1, p.shape[-1])

    def update(n, stages=()):
        (grads[n], delta[n], new_m[n], new_v[n]), landed = _adamw(w[n], grads[n], mom[n], var[n], "adamw_" + n,
                                                                  stages=stages)
        return landed

    updated, _ = _adamw_multi(["w_ff1", "w_ff2", "w_o", "w_lru_up"], w, grads, mom, var)
    for n, (go, d, mo, vo) in updated.items():
        grads[n], delta[n], new_m[n], new_v[n] = go, d, mo, vo
    p_in, c_in = _chips_wait(ssem, rsem, p_in, c_in, new_v["w_lru_up"])
    pair_in = _add_chips(p_in, c_in, idx_big, "add_chips_w_in")
    ((pair_in,), (dg1_all,)) = update("w_pool_up", stages=[_share([pair_in]), _to_everyone(dg1)])
    dg1_all = lax.dynamic_update_slice(dg1_all, dg1[None], (2 * chip + core, zero, zero)).reshape(2 * NCHIP, D)
    grads["w_in"] = pair_in.reshape(-1, pair_in.shape[-1])
    update("w_in")
    small_sum = pair_small.reshape(SMALL_ROWS, D)
    loss = 0.5 * small_sum[LOSS_ROW, 0]
    ccols = DR // NCHIP
    sep = [lax.dynamic_slice(small_sum[12:16], (zero, chip * ccols), (4, ccols)),
           small_sum[16:80].reshape(-1, hd), small_sum[80:144].reshape(-1, hd), small_sum[144:208].reshape(-1, PG)]
    g_s, d_s, m_s, v_s = _adamw_small(small_sum, dg1_all, sep, w, mom, var)
    grads.update(g_s)
    grads.update(dict(zip(SMALL_SEPARATE, sep)))
    delta.update(d_s)
    new_m.update(m_s)
    new_v.update(v_s)

    out = lambda d: [d[n].reshape(args[n].shape) for n in W_NAMES]
    return (loss, grad_x[None], *out(grads), *out(delta), *out(new_m), *out(new_v))
```

```python
import functools
import math

import jax
import jax.numpy as jnp
from jax import lax
from jax.experimental import pallas as pl
from jax.experimental.pallas import tpu as pltpu

F32 = jnp.float32
BF = jnp.bfloat16

T = 2048
D = 1024
DR = 1024
DP = 512
DF = 4096
DIN = 4608
NCHIP = 4
CW_IN = DIN // NCHIP
LANE = 128
CB = 128
NG = DR // CB
PG = 128
POOL_WINDOWS = (2, 4, 8, 16)
NORM_EPS = 1e-6
LRU_C = 8.0
GELU_C = math.sqrt(2.0 / math.pi)
ADAM_LR = 0.001
ADAM_B1 = 0.9
ADAM_B2 = 0.999
ADAM_EPS = 1e-08
ADAM_WD = 0.01
ADAM_STEP = 10
MESH_ID = pl.DeviceIdType.MESH
ANY = pl.BlockSpec(memory_space=pl.ANY)
SMALL_ROWS = 208
LOSS_ROW = 11
MIB = 1 << 20


def _cp(vmem_mib=None):
    if vmem_mib is None:
        return pltpu.CompilerParams()
    return pltpu.CompilerParams(vmem_limit_bytes=vmem_mib * MIB)


def _hbm(*arrays):
    return [pltpu.with_memory_space_constraint(a, pltpu.HBM) for a in arrays]


def _hbm_out(shapes):
    return [pltpu.HBM(s.shape, s.dtype) for s in shapes]


class _Stage:
    def __init__(self, operands, out_shape, alias, sems, start, finish):
        self.operands, self.out_shape, self.alias, self.sems = list(operands), list(out_shape), dict(alias), list(sems)
        self.start, self.finish = start, finish


def _call(body, *, name, grid, in_specs, out_specs, out_shape, args, vmem=None, stages=(), prefetch=None,
          scratch=()):
    nin, nout = len(in_specs), len(out_specs)
    npre = 0 if prefetch is None else 1
    st_args, st_shapes, st_sems, aliases = [], [], list(scratch), {}
    for st in stages:
        for k, v in st.alias.items():
            aliases[npre + nin + len(st_args) + k] = nout + len(st_shapes) + v
        st_args += st.operands
        st_shapes += st.out_shape
        st_sems += st.sems

    def wrapped(*refs):
        pre, refs = refs[:npre], refs[npre:]
        ins, pos = refs[:nin], nin
        st_ins = []
        for st in stages:
            st_ins.append(refs[pos:pos + len(st.operands)])
            pos += len(st.operands)
        outs, pos = refs[pos:pos + nout], pos + nout
        st_outs = []
        for st in stages:
            st_outs.append(refs[pos:pos + len(st.out_shape)])
            pos += len(st.out_shape)
        work, pos = refs[pos:pos + len(scratch)], pos + len(scratch)
        sems = []
        for st in stages:
            sems.append(refs[pos:pos + len(st.sems)])
            pos += len(st.sems)
        if stages:
            first = functools.reduce(jnp.logical_and, [pl.program_id(a) == 0 for a in range(len(grid))])

            @pl.when(first)
            def _():
                for st, a, b, s in zip(stages, st_ins, st_outs, sems):
                    st.start(a, b, s)

        body(*pre, *ins, *outs, *work)
        if stages:
            last = functools.reduce(jnp.logical_and, [pl.program_id(a) == g - 1 for a, g in enumerate(grid)])

            @pl.when(last)
            def _():
                for st, a, b, s in zip(stages, st_ins, st_outs, sems):
                    st.finish(a, b, s)

    all_in = list(in_specs) + [ANY] * len(st_args)
    all_out = list(out_specs) + [ANY] * len(st_shapes)
    kw = dict(has_side_effects=True) if stages else {}
    if vmem is not None:
        kw["vmem_limit_bytes"] = vmem * MIB
    if prefetch is None:
        gkw = dict(grid=grid, in_specs=all_in, out_specs=all_out, scratch_shapes=st_sems)
    else:
        gkw = dict(grid_spec=pltpu.PrefetchScalarGridSpec(
            num_scalar_prefetch=1, grid=grid, in_specs=all_in, out_specs=all_out, scratch_shapes=st_sems))
    res = pl.pallas_call(
        wrapped, name=name, out_shape=_hbm_out(list(out_shape) + st_shapes), input_output_aliases=aliases,
        compiler_params=pltpu.CompilerParams(**kw), **gkw,
    )(*([prefetch] if npre else []), *_hbm(*args, *st_args))
    outs, rest, st_res = list(res[:nout]), list(res[nout:]), []
    for st in stages:
        st_res.append(rest[:len(st.out_shape)])
        rest = rest[len(st.out_shape):]
    return outs, st_res


def _mm(a, b):
    return jnp.dot(a.astype(BF), b.astype(BF), preferred_element_type=F32)


def _mm_nt(a, b):
    return lax.dot_general(a.astype(BF), b.astype(BF), (((1,), (1,)), ((), ())),
                           preferred_element_type=F32)


def _mm_tn(a, b):
    return lax.dot_general(a.astype(BF), b.astype(BF), (((0,), (0,)), ((), ())),
                           preferred_element_type=F32)


def _rows(v):
    return lax.broadcasted_iota(jnp.int32, v.shape, 0)


def _sd(v, s, fill=0.0):
    return jnp.where(_rows(v) >= s, pltpu.roll(v, s, axis=0), fill)


def _su(v, s, fill=0.0):
    n = v.shape[0]
    return jnp.where(_rows(v) < n - s, pltpu.roll(v, n - s, axis=0), fill)


def _sigmoid(z):
    return 1.0 / (1.0 + jnp.exp(-z))


def _softplus(z):
    e = jnp.exp(-jnp.abs(z))
    u = 1.0 + e
    d = u - 1.0
    log1p = jnp.where(d == 0.0, e, jnp.log(u) * (e / jnp.where(d == 0.0, 1.0, d)))
    return jnp.maximum(z, 0.0) + log1p


def _mean(v):
    return jnp.mean(v, axis=-1, keepdims=True)


def _colsum(v):
    return jnp.sum(v, axis=0, keepdims=True)


def _acc(ref, val, first):
    @pl.when(first)
    def _():
        ref[...] = val

    @pl.when(jnp.logical_not(first))
    def _():
        ref[...] += val


def _conv(xp, cw, cb):
    x1, x2, x3 = _sd(xp, 1), _sd(xp, 2), _sd(xp, 3)
    xc = cb + cw[0:1] * x3 + cw[1:2] * x2 + cw[2:3] * x1 + cw[3:4] * xp
    return xc, x1, x2, x3


def _lru_gates(xc, wa, ba, wx, bx, lam):
    xcb = xc.astype(BF)
    r = _sigmoid(_mm(xcb, wa) + ba)
    ii = _sigmoid(_mm(xcb, wx) + bx)
    sp = _softplus(-lam)
    la = (-LRU_C) * r * sp
    a = jnp.exp(la)
    mult = jnp.sqrt(-jnp.tanh(la) * (a * a + 1.0))
    return xcb, r, ii, sp, a, mult


def _gelu_parts(g):
    th = jnp.tanh(GELU_C * (g + 0.044715 * (g * g * g)))
    gel = 0.5 * g * (1.0 + th)
    dgel = 0.5 * (1.0 + th) + 0.5 * g * (1.0 - th * th) * (GELU_C * (1.0 + 3.0 * 0.044715 * (g * g)))
    return gel, dgel


def _tile_scan(a, b, a_s, b_s, out_ref, reverse):
    n = a.shape[0]
    nt = n // 8
    sub = jnp.bitwise_and(_rows(a), 7)
    s = 1
    while s < 8:
        keep = sub < 8 - s if reverse else sub >= s
        amount = n - s if reverse else s
        b = b + a * jnp.where(keep, pltpu.roll(b, amount, axis=0), 0.0)
        a = a * jnp.where(keep, pltpu.roll(a, amount, axis=0), 1.0)
        s *= 2
    a_s[...] = a
    b_s[...] = b
    edge = pl.ds(0 if reverse else 7, nt, stride=8)
    ta, tb = a_s[edge, :], b_s[edge, :]
    shift = _su if reverse else _sd
    s = 1
    while s < nt:
        tb = tb + ta * shift(tb, s, 0.0)
        if 2 * s < nt:
            ta = ta * shift(ta, s, 1.0)
        s *= 2
    enters = shift(tb, 1, 0.0)
    for o in range(8):
        rows = pl.ds(o, nt, stride=8)
        out_ref[rows, :] = b_s[rows, :] + a_s[rows, :] * enters


def _pool_window(x, steps, shift):
    s, sh = x, 1
    for _ in range(steps):
        s = s + shift(s, sh)
        sh *= 2
    return s


def _fwd_inproj_own(x, g1, w_in, slots, stages=()):
    tm = 512

    def body(s_ref, x_ref, g_ref, w_ref, proj_ref, h_ref):
        xv = x_ref[...]
        r = lax.rsqrt(_mean(xv * xv) + NORM_EPS)
        h = ((xv * r) * g_ref[...]).astype(BF)
        h_ref[...] = h
        proj_ref[...] = jnp.dot(h, w_ref[0], preferred_element_type=F32)

    return _call(
        body, name="fwd_inproj_own", grid=(T // tm,), prefetch=slots,
        in_specs=[pl.BlockSpec((tm, D), lambda i, s: (i, 0)),
                  pl.BlockSpec((1, D), lambda i, s: (0, 0)),
                  pl.BlockSpec((1, D, CW_IN), lambda i, s: (s[0], 0, 0))],
        out_specs=[pl.BlockSpec((tm, CW_IN), lambda i, s: (i, s[0])),
                   pl.BlockSpec((tm, D), lambda i, s: (i, 0))],
        out_shape=[jax.ShapeDtypeStruct((T, DIN), F32), jax.ShapeDtypeStruct((T, D), BF)],
        vmem=40, args=[x, g1, w_in], stages=stages)[0]


def _fwd_inproj_rest(h1, w_in, proj, slots):
    tm = 512

    def body(s_ref, h_ref, w_ref, p_in, proj_ref):
        proj_ref[...] = jnp.dot(h_ref[...], w_ref[0], preferred_element_type=F32)

    res = pl.pallas_call(
        body, name="fwd_inproj_rest",
        grid_spec=pltpu.PrefetchScalarGridSpec(
            num_scalar_prefetch=1, grid=(T // tm, NCHIP - 1),
            in_specs=[pl.BlockSpec((tm, D), lambda i, k, s: (i, 0)),
                      pl.BlockSpec((1, D, CW_IN), lambda i, k, s: (s[1 + k], 0, 0)), ANY],
            out_specs=pl.BlockSpec((tm, CW_IN), lambda i, k, s: (i, s[1 + k]))),
        out_shape=pltpu.HBM((T, DIN), F32), input_output_aliases={3: 0},
        compiler_params=_cp(40),
    )(slots, *_hbm(h1, w_in, proj))
    return res


def _vec_spec():
    return pl.BlockSpec((1, CB), lambda j: (0, j))


def _fwd_lru(proj, conv_w, conv_b, wa, ba, wx, bx, lam, stages=()):
    def body(xp_ref, g_ref, cw_ref, cb_ref, wa_ref, ba_ref, wx_ref, bx_ref, lam_ref, y_ref, h_ref, a_s, b_s):
        xc, _, _, _ = _conv(xp_ref[...], cw_ref[...], cb_ref[...])
        _, _, ii, _, a, mult = _lru_gates(xc, wa_ref[0], ba_ref[...], wx_ref[0], bx_ref[...], lam_ref[...])
        _tile_scan(a, mult * (ii * xc), a_s, b_s, h_ref, reverse=False)
        gel, _ = _gelu_parts(g_ref[...])
        y_ref[...] = (h_ref[...] * gel).astype(BF)

    return _call(
        body, name="fwd_lru", grid=(NG,),
        in_specs=[pl.BlockSpec((T, CB), lambda j: (0, j)),
                  pl.BlockSpec((T, CB), lambda j: (0, NG + j)),
                  pl.BlockSpec((4, CB), lambda j: (0, j)),
                  _vec_spec(),
                  pl.BlockSpec((1, CB, CB), lambda j: (j, 0, 0)), _vec_spec(),
                  pl.BlockSpec((1, CB, CB), lambda j: (j, 0, 0)), _vec_spec(),
                  _vec_spec()],
        out_specs=[pl.BlockSpec((T, CB), lambda j: (0, j)), pl.BlockSpec((T, CB), lambda j: (0, j))],
        out_shape=[jax.ShapeDtypeStruct((T, DR), BF), jax.ShapeDtypeStruct((T, DR), F32)],
        vmem=48, args=[proj, proj, conv_w, conv_b, wa, ba, wx, bx, lam], stages=stages,
        scratch=[pltpu.VMEM((T, CB), F32)] * 2)


def _pool_cnt(w):
    t = lax.broadcasted_iota(jnp.int32, (T, 1), 0)
    return jnp.minimum(t + 1, w).astype(F32)


def _fwd_pool(proj, pool_w, pool_scale):
    def body(xp_ref, pw_ref, sc_ref, y_ref):
        for g, w in enumerate(POOL_WINDOWS):
            cols = slice(g * PG, (g + 1) * PG)
            x = xp_ref[:, cols]
            p = _pool_window(x, g + 1, _sd) / _pool_cnt(w) - x
            y_ref[:, cols] = (_mm(p, pw_ref[g]) * sc_ref[:, cols]).astype(BF)

    return pl.pallas_call(
        body, name="fwd_pool", grid=(1,),
        in_specs=[pl.BlockSpec((T, DP), lambda i: (0, 2 * DR // DP)),
                  pl.BlockSpec((4, PG, PG), lambda i: (0, 0, 0)),
                  pl.BlockSpec((1, DP), lambda i: (0, 0))],
        out_specs=pl.BlockSpec((T, DP), lambda i: (0, 0)),
        out_shape=pltpu.HBM((T, DP), BF),
        compiler_params=_cp(48),
    )(*_hbm(proj, pool_w, pool_scale))


GATE_BLK = 512
GATE_BLK0 = (2 * DR + DP) // GATE_BLK


def _gate_specs(tm):
    return [pl.BlockSpec((tm, GATE_BLK), functools.partial(lambda i, q: (i, GATE_BLK0 + q), q=q))
            for q in range(4)]


def _fwd_merge(x, ylru, ypool, proj, b_gate, g2, g3, w_lru_up, w_pool_up, w_o, stages=()):
    tm = 512

    def body(x_ref, yl_ref, yp_ref, p0, p1, p2, p3, bg_ref, g2_ref, g3_ref, wl_ref, wp_ref, wo_ref,
             x2_ref, h2_ref, m_ref, mrg_ref, bra_ref, brb_ref):
        bra = jnp.dot(yl_ref[...], wl_ref[...], preferred_element_type=F32)
        yp = yp_ref[...]
        brb = jnp.concatenate([jnp.dot(yp, wp_ref[k], preferred_element_type=F32) for k in range(NCHIP)], axis=1)
        bg = bg_ref[...]
        ga = _sigmoid(jnp.concatenate([p0[...], p1[...]], axis=1) + bg[:, :D])
        gb = _sigmoid(jnp.concatenate([p2[...], p3[...]], axis=1) + bg[:, D:])
        mrg = (ga * bra + gb * brb).astype(BF)
        m = jnp.dot(mrg, wo_ref[...], preferred_element_type=F32)
        r2 = lax.rsqrt(_mean(m * m) + NORM_EPS)
        x2 = x_ref[...] + (m * r2) * g2_ref[...]
        r3 = lax.rsqrt(_mean(x2 * x2) + NORM_EPS)
        x2_ref[...] = x2
        h2_ref[...] = ((x2 * r3) * g3_ref[...]).astype(BF)
        m_ref[...] = m
        mrg_ref[...] = mrg
        bra_ref[...] = bra.astype(BF)
        brb_ref[...] = brb.astype(BF)

    row = lambda w: pl.BlockSpec((tm, w), lambda i: (i, 0))
    full2 = lambda a, b: pl.BlockSpec((a, b), lambda i: (0, 0))
    return _call(
        body, name="fwd_merge", grid=(T // tm,),
        in_specs=[row(D), row(DR), row(DP)] + _gate_specs(tm) +
                 [full2(1, 2 * D), full2(1, D), full2(1, D), full2(DR, D),
                  pl.BlockSpec((NCHIP, DP, D // NCHIP), lambda i: (0, 0, 0)), full2(D, D)],
        out_specs=[row(D)] * 6,
        out_shape=[jax.ShapeDtypeStruct((T, D), F32), jax.ShapeDtypeStruct((T, D), BF),
                   jax.ShapeDtypeStruct((T, D), F32), jax.ShapeDtypeStruct((T, D), BF),
                   jax.ShapeDtypeStruct((T, D), BF), jax.ShapeDtypeStruct((T, D), BF)],
        vmem=48, args=[x, ylru, ypool, proj, proj, proj, proj, b_gate, g2, g3, w_lru_up, w_pool_up, w_o],
        stages=stages)


def _fwd_mlp(h2, w_ff1, w_ff2):
    tm = 512
    fk = DF // NCHIP

    def body(h_ref, w1_ref, w2_ref, a1_ref, f_ref):
        h = h_ref[...]
        f = None
        for k in range(NCHIP):
            a1 = jnp.maximum(jnp.dot(h, w1_ref[k], preferred_element_type=F32), 0.0)
            a1_ref[:, k * fk:(k + 1) * fk] = a1.astype(BF)
            part = jnp.dot((a1 * a1).astype(BF), w2_ref[k * fk:(k + 1) * fk, :], preferred_element_type=F32)
            f = part if f is None else f + part
        f_ref[...] = f

    return pl.pallas_call(
        body, name="fwd_mlp", grid=(T // tm,),
        in_specs=[pl.BlockSpec((tm, D), lambda i: (i, 0)),
                  pl.BlockSpec((NCHIP, D, fk), lambda i: (0, 0, 0)),
                  pl.BlockSpec((DF, D), lambda i: (0, 0))],
        out_specs=[pl.BlockSpec((tm, DF), lambda i: (i, 0)), pl.BlockSpec((tm, D), lambda i: (i, 0))],
        out_shape=_hbm_out([jax.ShapeDtypeStruct((T, DF), BF), jax.ShapeDtypeStruct((T, D), F32)]),
        compiler_params=_cp(56),
    )(*_hbm(h2, w_ff1, w_ff2))


def _loss_head(f, x2, target, g4):
    tm = 512

    def body(f_ref, x2_ref, t_ref, g_ref, loss_ref, dy_ref, df_ref, dg_ref):
        first = pl.program_id(0) == 0
        f = f_ref[...]
        g4v = g_ref[...]
        r4 = lax.rsqrt(_mean(f * f) + NORM_EPS)
        fn = f * r4
        e = (x2_ref[...] + fn * g4v) - t_ref[...]
        _acc(loss_ref, jnp.sum(_mean(e * e), axis=0, keepdims=True), first)
        dy = e * (1.0 / D)
        dy_ref[...] = dy
        _acc(dg_ref, _colsum(dy * fn), first)
        dfn = dy * g4v
        df_ref[...] = (r4 * (dfn - fn * _mean(dfn * fn))).astype(BF)

    row = pl.BlockSpec((tm, D), lambda i: (i, 0))
    return pl.pallas_call(
        body, name="loss_head", grid=(T // tm,),
        in_specs=[row, row, row, pl.BlockSpec((1, D), lambda i: (0, 0))],
        out_specs=[pl.BlockSpec((1, 1), lambda i: (0, 0)), row, row, pl.BlockSpec((1, D), lambda i: (0, 0))],
        out_shape=_hbm_out([jax.ShapeDtypeStruct((1, 1), F32), jax.ShapeDtypeStruct((T, D), F32),
                            jax.ShapeDtypeStruct((T, D), BF), jax.ShapeDtypeStruct((1, D), F32)]),
        compiler_params=_cp(48),
    )(*_hbm(f, x2, target, g4))


def _bwd_mlp_x(df, a1, w_ff1, w_ff2):
    tm = 512
    fk = DF // NCHIP

    def body(df_ref, a1_ref, w1_ref, w2_ref, dh_ref, df1_ref):
        df = df_ref[...]
        dh = None
        for k in range(NCHIP):
            cols = slice(k * fk, (k + 1) * fk)
            dact = _mm_nt(df, w2_ref[cols, :])
            df1 = (dact * (2.0 * a1_ref[:, cols].astype(F32))).astype(BF)
            df1_ref[:, cols] = df1
            part = _mm_nt(df1, w1_ref[k])
            dh = part if dh is None else dh + part
        dh_ref[...] = dh

    return pl.pallas_call(
        body, name="bwd_mlp_x", grid=(T // tm,),
        in_specs=[pl.BlockSpec((tm, D), lambda i: (i, 0)),
                  pl.BlockSpec((tm, DF), lambda i: (i, 0)),
                  pl.BlockSpec((NCHIP, D, fk), lambda i: (0, 0, 0)),
                  pl.BlockSpec((DF, D), lambda i: (0, 0))],
        out_specs=[pl.BlockSpec((tm, D), lambda i: (i, 0)), pl.BlockSpec((tm, DF), lambda i: (i, 0))],
        out_shape=_hbm_out([jax.ShapeDtypeStruct((T, D), F32), jax.ShapeDtypeStruct((T, DF), BF)]),
        compiler_params=_cp(56),
    )(*_hbm(df, a1, w_ff1, w_ff2))


def _bwd_mlp_w(df, h2, a1, df1):
    fc = 512
    per = (DF // NCHIP) // fc

    def body(df_ref, h_ref, a1_ref, df1_ref, dw1_ref, dw2_ref):
        a1 = a1_ref[...].astype(F32)
        dw2_ref[...] = _mm_tn((a1 * a1).astype(BF), df_ref[...]).astype(BF)
        dw1_ref[0] = _mm_tn(h_ref[...], df1_ref[...]).astype(BF)

    return pl.pallas_call(
        body, name="bwd_mlp_w", grid=(DF // fc,),
        in_specs=[pl.BlockSpec((T, D), lambda j: (0, 0)),
                  pl.BlockSpec((T, D), lambda j: (0, 0)),
                  pl.BlockSpec((T, fc), lambda j: (0, j)),
                  pl.BlockSpec((T, fc), lambda j: (0, j))],
        out_specs=[pl.BlockSpec((1, D, fc), lambda j: (j // per, 0, j % per)),
                   pl.BlockSpec((fc, D), lambda j: (j, 0))],
        out_shape=_hbm_out([jax.ShapeDtypeStruct((NCHIP, D, DF // NCHIP), BF),
                            jax.ShapeDtypeStruct((DF, D), BF)]),
        compiler_params=_cp(56),
    )(*_hbm(df, h2, a1, df1))


def _bwd_merge(dh2, dy, x2, m, bra, brb, proj, b_gate, g2, g3, w_lru_up, w_pool_up, w_o, stages=()):
    tm = 256
    cpu = D // NCHIP

    def body(dh2_ref, dy_ref, x2_ref, m_ref, bra_ref, brb_ref, p0, p1, p2, p3, bg_ref,
             g2_ref, g3_ref, wl_ref, wp_ref, wo_ref,
             dx_ref, dgt_ref, dyl_ref, dyp_ref, dm_ref, dbra_ref, dbrb_ref, dg2_ref, dg3_ref, dbg_ref):
        first = pl.program_id(0) == 0
        x2 = x2_ref[...]
        r3 = lax.rsqrt(_mean(x2 * x2) + NORM_EPS)
        x2n = x2 * r3
        dh2 = dh2_ref[...]
        t3 = dh2 * g3_ref[...]
        dx2 = dy_ref[...] + r3 * (t3 - x2n * _mean(t3 * x2n))
        dx_ref[...] = dx2
        _acc(dg3_ref, _colsum(dh2 * x2n), first)
        m = m_ref[...]
        r2 = lax.rsqrt(_mean(m * m) + NORM_EPS)
        mn = m * r2
        _acc(dg2_ref, _colsum(dx2 * mn), first)
        dmn = dx2 * g2_ref[...]
        dm = (r2 * (dmn - mn * _mean(dmn * mn))).astype(BF)
        dm_ref[...] = dm
        dmrg = _mm_nt(dm, wo_ref[...])
        bg = bg_ref[...]
        ga = _sigmoid(jnp.concatenate([p0[...], p1[...]], axis=1) + bg[:, :D])
        gb = _sigmoid(jnp.concatenate([p2[...], p3[...]], axis=1) + bg[:, D:])
        dga = dmrg * bra_ref[...].astype(F32) * (ga * (1.0 - ga))
        dgb = dmrg * brb_ref[...].astype(F32) * (gb * (1.0 - gb))
        dgt_ref[:, :D] = dga.astype(BF)
        dgt_ref[:, D:] = dgb.astype(BF)
        _acc(dbg_ref, jnp.concatenate([_colsum(dga), _colsum(dgb)], axis=1), first)
        dbra = (dmrg * ga).astype(BF)
        dbrb = (dmrg * gb).astype(BF)
        dbra_ref[...] = dbra
        dbrb_ref[...] = dbrb
        dyl_ref[...] = _mm_nt(dbra, wl_ref[...])
        dyp = None
        for k in range(NCHIP):
            part = _mm_nt(dbrb[:, k * cpu:(k + 1) * cpu], wp_ref[k])
            dyp = part if dyp is None else dyp + part
        dyp_ref[...] = dyp

    row = lambda w: pl.BlockSpec((tm, w), lambda i: (i, 0))
    full2 = lambda a, b: pl.BlockSpec((a, b), lambda i: (0, 0))
    wp_spec = pl.BlockSpec((NCHIP, DP, cpu), lambda i: (0, 0, 0))
    return _call(
        body, name="bwd_merge", grid=(T // tm,),
        in_specs=[row(D)] * 6 + _gate_specs(tm) +
                 [full2(1, 2 * D), full2(1, D), full2(1, D), full2(DR, D), wp_spec, full2(D, D)],
        out_specs=[row(D), row(2 * D), row(DR), row(DP), row(D), row(D), row(D),
                   full2(1, D), full2(1, D), full2(1, 2 * D)],
        out_shape=[jax.ShapeDtypeStruct((T, D), F32), jax.ShapeDtypeStruct((T, 2 * D), BF),
                   jax.ShapeDtypeStruct((T, DR), F32), jax.ShapeDtypeStruct((T, DP), F32),
                   jax.ShapeDtypeStruct((T, D), BF), jax.ShapeDtypeStruct((T, D), BF),
                   jax.ShapeDtypeStruct((T, D), BF),
                   jax.ShapeDtypeStruct((1, D), F32), jax.ShapeDtypeStruct((1, D), F32),
                   jax.ShapeDtypeStruct((1, 2 * D), F32)],
        vmem=56, args=[dh2, dy, x2, m, bra, brb, proj, proj, proj, proj, b_gate, g2, g3, w_lru_up, w_pool_up, w_o],
        stages=stages)


def _dw_merge(mrg, dm, ylru, dbra, ypool, dbrb, stages=()):
    nb = NCHIP
    rb, pb, cpu = D // nb, DP // nb, D // NCHIP

    def body(mrg_ref, dm_ref, yl_ref, dbra_ref, yp_ref, dbrb_ref, dwo_ref, dwl_ref, dwp_ref):
        dwo_ref[...] = _mm_tn(mrg_ref[...], dm_ref[...]).astype(BF)
        dwl_ref[...] = _mm_tn(yl_ref[...], dbra_ref[...]).astype(BF)
        dwp = _mm_tn(yp_ref[...], dbrb_ref[...]).astype(BF)
        for k in range(NCHIP):
            dwp_ref[k] = dwp[:, k * cpu:(k + 1) * cpu]

    cols = lambda w: pl.BlockSpec((T, w), lambda r: (0, r))
    whole = pl.BlockSpec((T, D), lambda r: (0, 0))
    return _call(
        body, name="dw_merge", grid=(nb,),
        in_specs=[cols(rb), whole, cols(rb), whole, cols(pb), whole],
        out_specs=[pl.BlockSpec((rb, D), lambda r: (r, 0)), pl.BlockSpec((rb, D), lambda r: (r, 0)),
                   pl.BlockSpec((NCHIP, pb, cpu), lambda r: (0, r, 0))],
        out_shape=[jax.ShapeDtypeStruct((D, D), BF), jax.ShapeDtypeStruct((DR, D), BF),
                   jax.ShapeDtypeStruct((NCHIP, DP, cpu), BF)],
        vmem=56, args=[mrg, dm, ylru, dbra, ypool, dbrb], stages=stages)


def _bwd_lru(proj, h, dylru, conv_w, conv_b, wa, ba, wx, bx, lam, stages=()):
    def body(xp_ref, g_ref, h_ref, dy_ref, cw_ref, cb_ref, wa_ref, ba_ref, wx_ref, bx_ref, lam_ref,
             dxp_ref, dg_ref, dcw_ref, dcb_ref, dwa_ref, dba_ref, dwx_ref, dbx_ref, dlam_ref, a_s, b_s, l_s):
        xp = xp_ref[...]
        cw = cw_ref[...]
        lam = lam_ref[...]
        xc, x1, x2, x3 = _conv(xp, cw, cb_ref[...])
        wa, wx = wa_ref[0], wx_ref[0]
        xcb, r, ii, sp, a, mult = _lru_gates(xc, wa, ba_ref[...], wx, bx_ref[...], lam)
        g = g_ref[...]
        gel, dgel = _gelu_parts(g)
        h = h_ref[...]
        dy = dy_ref[...]
        dg_ref[...] = (dy * h * dgel).astype(BF)
        _tile_scan(_su(a, 1, 0.0), dy * gel, a_s, b_s, l_s, reverse=True)
        b = l_s[...]
        da = b * _sd(h, 1, 0.0)
        dmult = b * (ii * xc)
        dii = b * (mult * xc)
        dxc = b * (mult * ii)
        dla = da * a - dmult * ((a * a) / mult)
        dr = dla * ((-LRU_C) * sp)
        dsp = _colsum(dla * ((-LRU_C) * r))
        dlam_ref[...] = -dsp / (1.0 + jnp.exp(lam))
        dzr = dr * (r * (1.0 - r))
        dzi = dii * (ii * (1.0 - ii))
        dzrb, dzib = dzr.astype(BF), dzi.astype(BF)
        dxc = dxc + _mm_nt(dzrb, wa) + _mm_nt(dzib, wx)
        dwa_ref[0] = _mm_tn(xcb, dzrb)
        dwx_ref[0] = _mm_tn(xcb, dzib)
        dba_ref[...] = _colsum(dzr)
        dbx_ref[...] = _colsum(dzi)
        dcb_ref[...] = _colsum(dxc)
        dcw_ref[...] = jnp.concatenate([_colsum(dxc * x3), _colsum(dxc * x2), _colsum(dxc * x1),
                                        _colsum(dxc * xp)], axis=0)
        dxp = cw[3:4] * dxc + cw[2:3] * _su(dxc, 1) + cw[1:2] * _su(dxc, 2) + cw[0:1] * _su(dxc, 3)
        dxp_ref[...] = dxp.astype(BF)

    blk = pl.BlockSpec((T, CB), lambda j: (0, j))
    wsp = pl.BlockSpec((1, CB, CB), lambda j: (j, 0, 0))
    return _call(
        body, name="bwd_lru", grid=(NG,),
        in_specs=[blk, pl.BlockSpec((T, CB), lambda j: (0, NG + j)), blk, blk,
                  pl.BlockSpec((4, CB), lambda j: (0, j)), _vec_spec(), wsp, _vec_spec(), wsp, _vec_spec(),
                  _vec_spec()],
        out_specs=[blk, blk, pl.BlockSpec((4, CB), lambda j: (0, j)), _vec_spec(), wsp, _vec_spec(), wsp,
                   _vec_spec(), _vec_spec()],
        out_shape=[jax.ShapeDtypeStruct((T, DR), BF), jax.ShapeDtypeStruct((T, DR), BF),
                   jax.ShapeDtypeStruct((4, DR), F32), jax.ShapeDtypeStruct((1, DR), F32),
                   jax.ShapeDtypeStruct((NG, CB, CB), F32), jax.ShapeDtypeStruct((1, DR), F32),
                   jax.ShapeDtypeStruct((NG, CB, CB), F32), jax.ShapeDtypeStruct((1, DR), F32),
                   jax.ShapeDtypeStruct((1, DR), F32)],
        vmem=56, args=[proj, proj, h, dylru, conv_w, conv_b, wa, ba, wx, bx, lam], stages=stages,
        scratch=[pltpu.VMEM((T, CB), F32)] * 3)


def _bwd_pool(proj, dypool, pool_w, pool_scale):
    def body(xp_ref, dy_ref, pw_ref, sc_ref, dx_ref, dw_ref, dsc_ref):
        for g, w in enumerate(POOL_WINDOWS):
            cols = slice(g * PG, (g + 1) * PG)
            cnt = _pool_cnt(w)
            x = xp_ref[:, cols]
            pb = (_pool_window(x, g + 1, _sd) / cnt - x).astype(BF)
            wg = pw_ref[g]
            dy = dy_ref[:, cols]
            dsc_ref[:, cols] = _colsum(dy * _mm(pb, wg))
            dyp = (dy * sc_ref[:, cols]).astype(BF)
            dw_ref[g] = _mm_tn(pb, dyp)
            dp = _mm_nt(dyp, wg)
            dx_ref[:, cols] = (_pool_window(dp / cnt, g + 1, _su) - dp).astype(BF)

    return pl.pallas_call(
        body, name="bwd_pool", grid=(1,),
        in_specs=[pl.BlockSpec((T, DP), lambda i: (0, 2 * DR // DP)),
                  pl.BlockSpec((T, DP), lambda i: (0, 0)),
                  pl.BlockSpec((4, PG, PG), lambda i: (0, 0, 0)),
                  pl.BlockSpec((1, DP), lambda i: (0, 0))],
        out_specs=[pl.BlockSpec((T, DP), lambda i: (0, 0)),
                   pl.BlockSpec((4, PG, PG), lambda i: (0, 0, 0)),
                   pl.BlockSpec((1, DP), lambda i: (0, 0))],
        out_shape=_hbm_out([jax.ShapeDtypeStruct((T, DP), BF), jax.ShapeDtypeStruct((4, PG, PG), F32),
                            jax.ShapeDtypeStruct((1, DP), F32)]),
        compiler_params=_cp(48),
    )(*_hbm(proj, dypool, pool_w, pool_scale))


def _bwd_inproj_w(h1, dproj, stages=()):
    def body(h_ref, dp_ref, dw_ref):
        dw_ref[0] = _mm_tn(h_ref[...], dp_ref[...]).astype(BF)

    outs, landed = _call(
        body, name="bwd_inproj_w", grid=(NCHIP,),
        in_specs=[pl.BlockSpec((T, D), lambda k: (0, 0)), pl.BlockSpec((T, CW_IN), lambda k: (0, k))],
        out_specs=[pl.BlockSpec((1, D, CW_IN), lambda k: (k, 0, 0))],
        out_shape=[jax.ShapeDtypeStruct((NCHIP, D, CW_IN), BF)], vmem=48, args=[h1, dproj], stages=stages)
    return outs[0], landed


def _bwd_inproj_x(dproj, w_in, x, dxres, g1, stages=()):
    tm = 512

    def body(dp_ref, w_ref, x_ref, dr_ref, g_ref, dx_ref, dg_ref):
        dh = None
        for k in range(NCHIP):
            part = _mm_nt(dp_ref[:, k * CW_IN:(k + 1) * CW_IN], w_ref[k])
            dh = part if dh is None else dh + part
        xv = x_ref[...]
        r = lax.rsqrt(_mean(xv * xv) + NORM_EPS)
        xn = xv * r
        t = dh * g_ref[...]
        dx_ref[...] = dr_ref[...] + r * (t - xn * _mean(t * xn))
        _acc(dg_ref, _colsum(dh * xn), pl.program_id(0) == 0)

    row = pl.BlockSpec((tm, D), lambda i: (i, 0))
    vec = pl.BlockSpec((1, D), lambda i: (0, 0))
    return _call(
        body, name="bwd_inproj_x", grid=(T // tm,),
        in_specs=[pl.BlockSpec((tm, DIN), lambda i: (i, 0)), pl.BlockSpec((NCHIP, D, CW_IN), lambda i: (0, 0, 0)),
                  row, row, vec],
        out_specs=[row, vec],
        out_shape=[jax.ShapeDtypeStruct((T, D), F32), jax.ShapeDtypeStruct((1, D), F32)],
        vmem=56, args=[dproj, w_in, x, dxres, g1], stages=stages)[0]


def _place():
    x, y, c = lax.axis_index("x"), lax.axis_index("y"), lax.axis_index("c")
    chips = [(1 - x, y), (x, 1 - y), (1 - x, 1 - y)]
    return x, y, c, chips


def _rcopy(src, dst, ssem, rsem, dev):
    return pltpu.make_async_remote_copy(src_ref=src, dst_ref=dst, send_sem=ssem, recv_sem=rsem,
                                        device_id=dev, device_id_type=MESH_ID)


def _sds(a):
    return jax.ShapeDtypeStruct(a.shape, a.dtype)


def _sem2(n, m):
    return [pltpu.SemaphoreType.DMA((n * m,)), pltpu.SemaphoreType.DMA((n * m,))]


ALL = (0, 1, 1)


def _piece(ref, k, half, part):
    hr = ref.shape[1] // 2
    r0, r1 = hr * part[0] // part[2], hr * part[1] // part[2]
    return ref.at[k, pl.ds(half * hr + r0, r1 - r0), :]


def _gather(fulls, ici=(), d2d=()):
    n = len(fulls)
    ici, d2d = list(ici), list(d2d)
    pieces = [("ici", i, part) for i, part in ici] + [("d2d", i, part) for i, part in d2d]

    def copies(outs, sems):
        x, y, c, chips = _place()
        me = 2 * x + y
        sib = (x, y, 1 - c)
        send, recv = [], []
        for q, (kind, i, part) in enumerate(pieces):
            for j, chip in enumerate(chips):
                k, s = 2 * chip[0] + chip[1], 3 * q + j
                if kind == "ici":
                    mine, theirs, dev = _piece(outs[i], me, c, part), _piece(outs[i], k, c, part), (*chip, c)
                else:
                    mine, theirs, dev = _piece(outs[i], k, c, part), _piece(outs[i], k, 1 - c, part), sib
                send.append(_rcopy(mine, mine, sems[0].at[s], sems[1].at[s], dev))
                recv.append(_rcopy(theirs, theirs, sems[0].at[s], sems[1].at[s], dev))
        return send, recv

    def start(ins, outs, sems):
        for cp in copies(outs, sems)[0]:
            cp.start()

    def finish(ins, outs, sems):
        send, recv = copies(outs, sems)
        for cp in recv:
            cp.wait_recv()
        for cp in send:
            cp.wait_send()

    sems = [pltpu.SemaphoreType.DMA((3 * len(pieces),)), pltpu.SemaphoreType.DMA((3 * len(pieces),))]
    return _Stage(fulls, [_sds(f) for f in fulls], {i: i for i in range(n)}, sems, start, finish)


def _gather_whole(v):
    def copies(ins, outs, sems):
        x, y, c, chips = _place()
        me = 2 * x + y
        send = [_rcopy(ins[0], outs[0].at[me], sems[0].at[j], sems[1].at[j], (*chip, c))
                for j, chip in enumerate(chips)]
        recv = [_rcopy(ins[0], outs[0].at[2 * chip[0] + chip[1]], sems[0].at[j], sems[1].at[j], (*chip, c))
                for j, chip in enumerate(chips)]
        return send, recv

    def start(ins, outs, sems):
        for cp in copies(ins, outs, sems)[0]:
            cp.start()

    def finish(ins, outs, sems):
        send, recv = copies(ins, outs, sems)
        for cp in recv:
            cp.wait_recv()
        for cp in send:
            cp.wait_send()

    return _Stage([v], [jax.ShapeDtypeStruct((NCHIP,) + v.shape, v.dtype)], {},
                  [pltpu.SemaphoreType.DMA((3,)), pltpu.SemaphoreType.DMA((3,))], start, finish)


def _to_sibling(srcs):
    n = len(srcs)

    def copies(ins, outs, sems):
        x, y, c, _ = _place()
        sib = (x, y, 1 - c)
        return [_rcopy(ins[i].at[:, 1 - c] if srcs[i].ndim == 4 else ins[i], outs[i], sems[0].at[i], sems[1].at[i], sib)
                for i in range(n)]

    def start(ins, outs, sems):
        for cp in copies(ins, outs, sems):
            cp.start()

    def finish(ins, outs, sems):
        for cp in copies(ins, outs, sems):
            cp.wait()

    shapes = [jax.ShapeDtypeStruct((NCHIP,) + s.shape[2:] if s.ndim == 4 else s.shape, s.dtype) for s in srcs]
    return _Stage(srcs, shapes, {}, [pltpu.SemaphoreType.DMA((n,)), pltpu.SemaphoreType.DMA((n,))], start, finish)


def _to_chips(srcs, parts=None, lands=None):
    n = len(srcs)
    parts = [ALL] * n if parts is None else parts
    lands = [None] * n if lands is None else lands
    given = [i for i in range(n) if lands[i] is not None]

    def rows(ref, i):
        hr = srcs[i].shape[1]
        r0, r1 = hr * parts[i][0] // parts[i][2], hr * parts[i][1] // parts[i][2]
        return ref.at[pl.ds(r0, r1 - r0), :]

    def copies(ins, outs, sems):
        x, y, c, chips = _place()
        me = 2 * x + y
        return [_rcopy(rows(ins[i].at[2 * chip[0] + chip[1]] if srcs[i].shape[0] == NCHIP else ins[i].at[c], i),
                       rows(outs[i].at[me], i), sems[0].at[3 * i + j], sems[1].at[3 * i + j], (*chip, c))
                for i in range(n) for j, chip in enumerate(chips)]

    def start(ins, outs, sems):
        for cp in copies(ins, outs, sems):
            cp.start()

    def finish(ins, outs, sems):
        for cp in copies(ins, outs, sems):
            cp.wait()

    shapes = [jax.ShapeDtypeStruct((NCHIP,) + s.shape[1:], s.dtype) for s in srcs]
    alias = {n + q: i for q, i in enumerate(given)}
    return _Stage(list(srcs) + [lands[i] for i in given], shapes, alias, _sem2(n, 3), start, finish)


HBM_REF = pl.BlockSpec(memory_space=pltpu.HBM)
SEM_REF = pl.BlockSpec(memory_space=pltpu.SEMAPHORE)
DATAFLOW = pltpu.SideEffectType.DATAFLOW_SIDE_EFFECTING


def _after(x):
    return _Stage([x], [], {}, [], lambda *a: None, lambda *a: None)


class _Flight:
    def __init__(self, stage, sems, bufs):
        self.stage, self.sems, self.bufs = stage, list(sems), list(bufs)

    def landed(self):
        st, n = self.stage, len(self.stage.operands)
        fresh = [j for j in range(len(st.out_shape)) if j not in st.alias.values()]
        back = {v: k for k, v in st.alias.items()}
        return [self.bufs[back[j]] if j in back else self.bufs[n + fresh.index(j)] for j in range(len(st.out_shape))]


def _split_call(name, finish=(), start=(), after=None):
    bufs, stage_bufs = [], []

    def slot(a):
        for i, b in enumerate(bufs):
            if b is a:
                return i
        bufs.append(a)
        return len(bufs) - 1

    fin_slots = [[slot(b) for b in fl.bufs] for fl in finish]
    for st in start:
        fresh = [lax.empty(o.shape, o.dtype) for j, o in enumerate(st.out_shape) if j not in st.alias.values()]
        stage_bufs.append([slot(a) for a in list(st.operands) + fresh])
    old_sems = [s for fl in finish for s in fl.sems]
    new_sems = [s for st in start for s in st.sems]
    nb, no, nn = len(bufs), len(old_sems), len(new_sems)

    def refs_of(st, slots, buf_refs):
        n = len(st.operands)
        ins = [buf_refs[i] for i in slots[:n]]
        fresh = [j for j in range(len(st.out_shape)) if j not in st.alias.values()]
        back = {v: k for k, v in st.alias.items()}
        outs = [ins[back[j]] if j in back else buf_refs[slots[n + fresh.index(j)]] for j in range(len(st.out_shape))]
        return ins, outs

    def body(*refs):
        buf_refs, sem_in = refs[:nb], refs[nb:nb + no]
        sem_out = refs[nb + no + (after is not None):][:nn]
        token = refs[-1]
        pos = 0
        for fl, slots in zip(finish, fin_slots):
            ins, outs = refs_of(fl.stage, slots, buf_refs)
            fl.stage.finish(ins, outs, sem_in[pos:pos + len(fl.sems)])
            pos += len(fl.sems)
        pos = 0
        for st, slots in zip(start, stage_bufs):
            ins, outs = refs_of(st, slots, buf_refs)
            st.start(ins, outs, sem_out[pos:pos + len(st.sems)])
            pos += len(st.sems)
        token[...] = jnp.zeros_like(token)

    res = pl.pallas_call(
        body, name=name,
        out_shape=tuple(new_sems) + tuple(pltpu.HBM(b.shape, b.dtype) for b in bufs) +
                  (jax.ShapeDtypeStruct((8, LANE), F32),),
        in_specs=(HBM_REF,) * nb + (SEM_REF,) * no + ((pl.BlockSpec(memory_space=pl.ANY),) if after is not None else ()),
        out_specs=(SEM_REF,) * nn + (HBM_REF,) * nb + (pl.BlockSpec(memory_space=pltpu.VMEM),),
        input_output_aliases={i: nn + i for i in range(nb)},
        compiler_params=pltpu.CompilerParams(has_side_effects=DATAFLOW),
    )(*_hbm(*bufs), *old_sems, *([after] if after is not None else []))
    sems, thru, token = res[:nn], res[nn:nn + nb], res[-1]
    for fl, slots in zip(finish, fin_slots):
        fl.bufs = [thru[i] for i in slots]
    flights, pos = [], 0
    for st, slots in zip(start, stage_bufs):
        flights.append(_Flight(st, sems[pos:pos + len(st.sems)], [thru[i] for i in slots]))
        pos += len(st.sems)
    return flights, token


def _last_copies(p_ref, land_ref, ssem, rsem):
    x, y, c, chips = _place()
    me = 2 * x + y
    send = [_rcopy(p_ref.at[2 * chip[0] + chip[1]], land_ref.at[me], ssem.at[j], rsem.at[j], (*chip, c))
            for j, chip in enumerate(chips)]
    recv = [_rcopy(p_ref.at[2 * chip[0] + chip[1]], land_ref.at[2 * chip[0] + chip[1]], ssem.at[j], rsem.at[j],
                   (*chip, c)) for j, chip in enumerate(chips)]
    return send, recv


def _chips_start(p):
    def body(p_ref, land_ref, ssem, rsem, p_thru, land_thru, token):
        for cp in _last_copies(p_ref, land_ref, ssem, rsem)[0]:
            cp.start()
        token[...] = jnp.zeros_like(token)

    return pl.pallas_call(
        body, name="reduce_last_start",
        out_shape=(pltpu.SemaphoreType.DMA((3,)), pltpu.SemaphoreType.DMA((3,)), pltpu.HBM(p.shape, p.dtype),
                   pltpu.HBM(p.shape, p.dtype), jax.ShapeDtypeStruct((8, LANE), F32)),
        in_specs=(HBM_REF, HBM_REF),
        out_specs=(SEM_REF, SEM_REF, HBM_REF, HBM_REF, pl.BlockSpec(memory_space=pltpu.VMEM)),
        input_output_aliases={0: 2, 1: 3},
        compiler_params=pltpu.CompilerParams(has_side_effects=DATAFLOW),
    )(*_hbm(p, lax.empty(p.shape, p.dtype)))


def _chips_wait(ssem, rsem, p_thru, land_thru, after):
    def body(p_ref, land_ref, ssem, rsem, after_ref, p_dead, got_ref):
        send, recv = _last_copies(p_ref, land_ref, ssem, rsem)
        for cp in send:
            cp.wait_send()
        for cp in recv:
            cp.wait_recv()

    return pl.pallas_call(
        body, name="reduce_last_wait",
        out_shape=(pltpu.HBM(p_thru.shape, p_thru.dtype), pltpu.HBM(land_thru.shape, land_thru.dtype)),
        in_specs=(HBM_REF, HBM_REF, SEM_REF, SEM_REF, pl.BlockSpec(memory_space=pl.ANY)),
        out_specs=(HBM_REF, HBM_REF), input_output_aliases={0: 0, 1: 1},
        compiler_params=pltpu.CompilerParams(has_side_effects=DATAFLOW),
    )(p_thru, land_thru, ssem, rsem, after)


def _share(pairs):
    n = len(pairs)

    def start(ins, outs, sems):
        x, y, c, _ = _place()
        for i in range(n):
            _rcopy(outs[i].at[c], outs[i].at[c], sems[0].at[i], sems[1].at[i], (x, y, 1 - c)).start()

    def finish(ins, outs, sems):
        x, y, c, _ = _place()
        for i in range(n):
            _rcopy(outs[i].at[c], outs[i].at[c], sems[0].at[i], sems[1].at[i], (x, y, 1 - c)).wait_send()
            _rcopy(outs[i].at[1 - c], outs[i].at[1 - c], sems[0].at[i], sems[1].at[i], (x, y, 1 - c)).wait_recv()

    return _Stage(pairs, [_sds(p) for p in pairs], {i: i for i in range(n)},
                  [pltpu.SemaphoreType.DMA((n,)), pltpu.SemaphoreType.DMA((n,))], start, finish)


def _row_block(rows, cols, itemsize=4, target=2 * MIB):
    br = rows
    while br * cols * itemsize > target and br % 32 == 0:
        br //= 2
    return br


def _cast_place(w, chip_idx, name):
    rows, cols = w.shape
    br = _row_block(rows, cols)

    def body(k_ref, w_ref, o_ref):
        o_ref[0] = w_ref[...].astype(BF)

    return _call(
        body, name=name, grid=(rows // br,), prefetch=chip_idx,
        in_specs=[pl.BlockSpec((br, cols), lambda r, k: (r, 0))],
        out_specs=[pl.BlockSpec((1, br, cols), lambda r, k: (k[0], r, 0))],
        out_shape=[jax.ShapeDtypeStruct((NCHIP, rows, cols), BF)], vmem=32, args=[w])[0][0]


def _cast_place_multi(ws, chip_idx, stages=()):
    br = 128
    nblk = [a.shape[0] // br for a in ws]
    starts = [sum(nblk[:i]) for i in range(len(ws))]

    def body(k_ref, *refs):
        r = pl.program_id(0)
        for i in range(len(ws)):
            @pl.when(jnp.logical_and(r >= starts[i], r < starts[i] + nblk[i]))
            def _(i=i):
                refs[len(ws) + i][0] = refs[i][...].astype(BF)

    def at(i):
        return functools.partial(lambda r, s, nb: jnp.clip(r - s, 0, nb - 1), s=starts[i], nb=nblk[i])

    outs, landed = _call(
        body, name="cast_rest", grid=(sum(nblk),), prefetch=chip_idx,
        in_specs=[pl.BlockSpec((br, a.shape[1]), functools.partial(lambda r, k, f: (f(r), 0), f=at(i)))
                  for i, a in enumerate(ws)],
        out_specs=[pl.BlockSpec((1, br, a.shape[1]), functools.partial(lambda r, k, f: (k[0], f(r), 0), f=at(i)))
                   for i, a in enumerate(ws)],
        out_shape=[jax.ShapeDtypeStruct((NCHIP,) + a.shape, BF) for a in ws], vmem=32, args=list(ws), stages=stages)
    return outs, landed


def _add_sibling(g, land, cidx, name, stages=()):
    _, _, hr, cols = g.shape
    br = _row_block(hr, cols)

    def body(c_ref, g_ref, l_ref, o_ref):
        o_ref[...] = (g_ref[0, 0].astype(F32) + l_ref[0].astype(F32)).astype(BF)[None]

    outs, st = _call(
        body, name=name, grid=(NCHIP, hr // br), prefetch=cidx,
        in_specs=[pl.BlockSpec((1, 1, br, cols), lambda k, r, c: (k, c[0], r, 0)),
                  pl.BlockSpec((1, br, cols), lambda k, r, c: (k, r, 0))],
        out_specs=[pl.BlockSpec((1, br, cols), lambda k, r, c: (k, r, 0))],
        out_shape=[jax.ShapeDtypeStruct((NCHIP, hr, cols), BF)], vmem=32, args=[g, land], stages=stages)
    return outs[0], st


def _add_sibling_multi(gs, lands, cidx, name):
    n = len(gs)
    brs = [_row_block(g.shape[2], g.shape[3]) for g in gs]
    nrb = [g.shape[2] // b for g, b in zip(gs, brs)]
    nblk = [NCHIP * q for q in nrb]
    starts = [sum(nblk[:i]) for i in range(n)]

    def body(c_ref, *refs):
        r = pl.program_id(0)
        for i in range(n):
            g_ref, l_ref, o_ref = refs[2 * i], refs[2 * i + 1], refs[2 * n + i]

            @pl.when(jnp.logical_and(r >= starts[i], r < starts[i] + nblk[i]))
            def _():
                o_ref[...] = (g_ref[0, 0].astype(F32) + l_ref[0].astype(F32)).astype(BF)[None]

    def at(i, r):
        q = jnp.clip(r - starts[i], 0, nblk[i] - 1)
        return q // nrb[i], q % nrb[i]

    def g_spec(i):
        return pl.BlockSpec((1, 1, brs[i], gs[i].shape[3]),
                            functools.partial(lambda r, c, i: (at(i, r)[0], c[0], at(i, r)[1], 0), i=i))

    def l_spec(i):
        return pl.BlockSpec((1, brs[i], gs[i].shape[3]),
                            functools.partial(lambda r, c, i: (at(i, r)[0], at(i, r)[1], 0), i=i))

    return _call(
        body, name=name, grid=(sum(nblk),), prefetch=cidx,
        in_specs=[s for i in range(n) for s in (g_spec(i), l_spec(i))], out_specs=[l_spec(i) for i in range(n)],
        out_shape=[jax.ShapeDtypeStruct(l.shape, BF) for l in lands], vmem=32,
        args=[a for i in range(n) for a in (gs[i], lands[i])])[0]


def _add_pair(a, b, name):
    rows, cols = a.shape

    def body(a_ref, b_ref, o_ref):
        o_ref[...] = a_ref[...] + b_ref[...]

    spec = pl.BlockSpec((rows, cols), lambda r: (0, 0))
    return _call(body, name=name, grid=(1,), in_specs=[spec, spec], out_specs=[spec], out_shape=[_sds(a)],
                 vmem=32, args=[a, b])[0][0]


def _add_chips(own, land, idx, name, stages=None):
    _, hr, cols = land.shape
    br = _row_block(hr, cols)

    def body(s_ref, a_ref, b_ref, c_ref, d_ref, o_ref):
        o_ref[...] = (a_ref[...].astype(F32) + b_ref[...].astype(F32)) + (c_ref[...].astype(F32) +
                                                                           d_ref[...].astype(F32))

    spec = lambda q: pl.BlockSpec((1, br, cols), functools.partial(lambda r, s, q: (s[q], r, 0), q=q))
    outs, landed = _call(
        body, name=name, grid=(hr // br,), prefetch=idx,
        in_specs=[spec(0), spec(1), spec(2), spec(3)], out_specs=[spec(4)],
        out_shape=[jax.ShapeDtypeStruct((2, hr, cols), F32)], vmem=48, args=[own, land, land, land],
        stages=stages or ())
    return outs[0] if stages is None else (outs[0], landed)


def _add_chips_multi(owns, lands, idx, name, stages=()):
    n = len(owns)
    brs = [_row_block(l.shape[1], l.shape[2]) for l in lands]
    nblk = [l.shape[1] // b for l, b in zip(lands, brs)]
    starts = [sum(nblk[:i]) for i in range(n)]

    def body(s_ref, *refs):
        r = pl.program_id(0)
        for i in range(n):
            a_ref, b_ref, c_ref, d_ref = refs[4 * i:4 * i + 4]
            o_ref = refs[4 * n + i]

            @pl.when(jnp.logical_and(r >= starts[i], r < starts[i] + nblk[i]))
            def _():
                o_ref[...] = (a_ref[...].astype(F32) + b_ref[...].astype(F32)) + (c_ref[...].astype(F32) +
                                                                                   d_ref[...].astype(F32))

    def spec(i, q):
        return pl.BlockSpec((1, brs[i], lands[i].shape[2]), functools.partial(
            lambda r, s, q, st, nb: (s[q], jnp.clip(r - st, 0, nb - 1), 0), q=q, st=starts[i], nb=nblk[i]))

    outs, landed = _call(
        body, name=name, grid=(sum(nblk),), prefetch=idx,
        in_specs=[spec(i, q) for i in range(n) for q in range(4)], out_specs=[spec(i, 4) for i in range(n)],
        out_shape=[jax.ShapeDtypeStruct((2,) + l.shape[1:], F32) for l in lands], vmem=48,
        args=[a for i in range(n) for a in (owns[i], lands[i], lands[i], lands[i])], stages=stages)
    return outs, landed


def _adamw_math(w, g, m, v):
    mn = ADAM_B1 * m + (1.0 - ADAM_B1) * g
    vn = ADAM_B2 * v + (1.0 - ADAM_B2) * (g * g)
    m_hat = mn / (1.0 - ADAM_B1 ** ADAM_STEP)
    v_hat = vn / (1.0 - ADAM_B2 ** ADAM_STEP)
    return -ADAM_LR * (m_hat / (jnp.sqrt(v_hat) + ADAM_EPS) + ADAM_WD * w), mn, vn


def _adamw(w, g, m, v, name, stages=()):
    rows, cols = w.shape
    br = _row_block(rows, cols)

    def body(w_ref, g_ref, m_ref, v_ref, go_ref, d_ref, mo_ref, vo_ref):
        gv = g_ref[...]
        go_ref[...] = gv
        d_ref[...], mo_ref[...], vo_ref[...] = _adamw_math(w_ref[...], gv, m_ref[...], v_ref[...])

    spec = pl.BlockSpec((br, cols), lambda r: (r, 0))
    return _call(body, name=name, grid=(rows // br,), in_specs=[spec] * 4, out_specs=[spec] * 4,
                 out_shape=[_sds(w)] * 4, vmem=56, args=[w, g, m, v], stages=stages)


def _adamw_multi(names, w, g, m, v, stages=()):
    cols = w[names[0]].shape[1]
    br = 128
    nblk = [w[n].shape[0] // br for n in names]
    starts = [sum(nblk[:i]) for i in range(len(names))]

    def body(*refs):
        r = pl.program_id(0)
        for i in range(len(names)):
            w_ref, g_ref, m_ref, v_ref = refs[4 * i:4 * i + 4]
            go_ref, d_ref, mo_ref, vo_ref = refs[4 * len(names) + 4 * i:4 * len(names) + 4 * i + 4]

            @pl.when(jnp.logical_and(r >= starts[i], r < starts[i] + nblk[i]))
            def _():
                gv = g_ref[...]
                go_ref[...] = gv
                d_ref[...], mo_ref[...], vo_ref[...] = _adamw_math(w_ref[...], gv, m_ref[...], v_ref[...])

    def spec(i):
        return pl.BlockSpec((br, cols), functools.partial(
            lambda r, s, nb: (jnp.clip(r - s, 0, nb - 1), 0), s=starts[i], nb=nblk[i]))

    outs, landed = _call(
        body, name="adamw_" + "_".join(names), grid=(sum(nblk),),
        in_specs=[spec(i) for i in range(len(names)) for _ in range(4)],
        out_specs=[spec(i) for i in range(len(names)) for _ in range(4)],
        out_shape=[_sds(w[n]) for n in names for _ in range(4)], vmem=56,
        args=[a[n] for n in names for a in (w, g, m, v)], stages=stages)
    return {n: outs[4 * i:4 * i + 4] for i, n in enumerate(names)}, landed


def _to_everyone(v):
    deltas = [(a, b, e) for a in (0, 1) for b in (0, 1) for e in (0, 1)][1:]

    def copies(ins, outs, sems):
        x, y, c, _ = _place()
        me = 4 * x + 2 * y + c
        flip = lambda p, f: 1 - p if f else p
        return [_rcopy(ins[0], outs[0].at[me], sems[0].at[q], sems[1].at[q], (flip(x, a), flip(y, b), flip(c, e)))
                for q, (a, b, e) in enumerate(deltas)]

    def start(ins, outs, sems):
        for cp in copies(ins, outs, sems):
            cp.start()

    def finish(ins, outs, sems):
        for cp in copies(ins, outs, sems):
            cp.wait()

    n = len(deltas)
    return _Stage([v], [jax.ShapeDtypeStruct((2 * NCHIP,) + v.shape, v.dtype)], {},
                  [pltpu.SemaphoreType.DMA((n,)), pltpu.SemaphoreType.DMA((n,))], start, finish)


SMALL_AT = {"norm_mix_pre": (0, 1, D), "norm_mix_post": (1, 1, D), "norm_mlp_pre": (2, 1, D),
            "norm_mlp_post": (3, 1, D), "b_gate": (4, 2, D), "conv_b": (6, 1, D), "lru_b_a": (7, 1, D),
            "lru_b_x": (8, 1, D), "lru_lambda": (9, 1, D), "pool_scale": (10, 1, DP)}
SMALL_SEPARATE = ["conv_w", "lru_w_a", "lru_w_x", "pool_w"]


def _adamw_small(small_sum, first_all, sep_grads, w, m, v):
    packed, sep = list(SMALL_AT), list(SMALL_SEPARATE)
    names = packed + sep

    def body(*refs):
        s_ref, a_ref, refs = refs[0], refs[1], refs[2:]
        g_sep, refs = refs[:len(sep)], refs[len(sep):]
        nn = len(names)
        w_r, m_r, v_r, refs = refs[:nn], refs[nn:2 * nn], refs[2 * nn:3 * nn], refs[3 * nn:]
        g_out, refs = refs[:len(packed)], refs[len(packed):]
        d_o, m_o, v_o = refs[:nn], refs[nn:2 * nn], refs[2 * nn:3 * nn]
        for i, n in enumerate(names):
            if i == 0:
                g = a_ref[0:1, :]
                for q in range(1, 2 * NCHIP):
                    g = g + a_ref[q:q + 1, :]
                g_out[i][...] = g
            elif n in SMALL_AT:
                r0, nr, nc = SMALL_AT[n]
                g = jnp.concatenate([s_ref[r0 + q:r0 + q + 1, :nc] for q in range(nr)], axis=1)
                g_out[i][...] = g
            else:
                g = g_sep[i - len(packed)][...]
            d_o[i][...], m_o[i][...], v_o[i][...] = _adamw_math(w_r[i][...], g, m_r[i][...], v_r[i][...])

    ws = [w[n] for n in names]
    res = pl.pallas_call(
        body, name="adamw_small",
        out_shape=[_sds(w[n]) for n in packed] + [_sds(a) for a in ws] * 3,
        compiler_params=_cp(32),
    )(*_hbm(small_sum, first_all, *sep_grads, *ws, *[m[n] for n in names], *[v[n] for n in names]))
    nn, npk = len(names), len(packed)
    grad = dict(zip(packed, res[:npk]))
    delta = dict(zip(names, res[npk:npk + nn]))
    new_m = dict(zip(names, res[npk + nn:npk + 2 * nn]))
    new_v = dict(zip(names, res[npk + 2 * nn:]))
    return grad, delta, new_m, new_v


W_NAMES = ["norm_mix_pre", "norm_mix_post", "norm_mlp_pre", "norm_mlp_post", "w_in", "b_gate", "conv_w", "conv_b",
           "lru_w_a", "lru_b_a", "lru_w_x", "lru_b_x", "lru_lambda", "pool_w", "pool_scale", "w_lru_up",
           "w_pool_up", "w_o", "w_ff1", "w_ff2"]
BIG = ["w_in", "w_lru_up", "w_pool_up", "w_o", "w_ff1", "w_ff2"]


def _block_diag(w):
    hd = w.shape[-1]
    per = CB // hd
    w4 = w.reshape(NG, per, hd, hd)
    eye = jnp.eye(per, dtype=w.dtype)
    return jnp.einsum("gpij,pq->gpiqj", w4, eye).reshape(NG, CB, CB)


def _block_diag_extract(d, hd):
    per = CB // hd
    d5 = d.reshape(NG, per, hd, per, hd)
    return jnp.stack([d5[:, p, :, p, :] for p in range(per)], axis=1).reshape(NG * per, hd, hd)


def _halves(g):
    return g.reshape(NCHIP, 2, g.size // (g.shape[-1] * 2 * NCHIP), g.shape[-1])


def kernel(x, norm_mix_pre, norm_mix_post, norm_mlp_pre, norm_mlp_post, w_in, b_gate, conv_w, conv_b, lru_w_a, lru_b_a, lru_w_x, lru_b_x, lru_lambda, pool_w, pool_scale, w_lru_up, w_pool_up, w_o, w_ff1, w_ff2, loss_target, m_norm_mix_pre, m_norm_mix_post, m_norm_mlp_pre, m_norm_mlp_post, m_w_in, m_b_gate, m_conv_w, m_conv_b, m_lru_w_a, m_lru_b_a, m_lru_w_x, m_lru_b_x, m_lru_lambda, m_pool_w, m_pool_scale, m_w_lru_up, m_w_pool_up, m_w_o, m_w_ff1, m_w_ff2, v_norm_mix_pre, v_norm_mix_post, v_norm_mlp_pre, v_norm_mlp_post, v_w_in, v_b_gate, v_conv_w, v_conv_b, v_lru_w_a, v_lru_b_a, v_lru_w_x, v_lru_b_x, v_lru_lambda, v_pool_w, v_pool_scale, v_w_lru_up, v_w_pool_up, v_w_o, v_w_ff1, v_w_ff2):
    args = dict(locals())
    two_d = lambda a: a.reshape(-1, a.shape[-1])
    w = {n: two_d(args[n]) for n in W_NAMES}
    mom = {n: two_d(args["m_" + n]) for n in W_NAMES}
    var = {n: two_d(args["v_" + n]) for n in W_NAMES}
    i32 = lambda val: jnp.asarray(val, jnp.int32)
    chip = i32(2 * lax.axis_index("x") + lax.axis_index("y"))
    core = i32(lax.axis_index("c"))
    cidx = core.reshape(1)
    zero = i32(0)
    hd = lru_w_a.shape[-1]
    xs, target = x[0], loss_target[0]
    g1, g2, g3, g4 = norm_mix_pre, norm_mix_post, norm_mlp_pre, norm_mlp_post

    mix = ["w_lru_up", "w_pool_up", "w_o"]
    full = {"w_in": _cast_place(w["w_in"], chip.reshape(1), "cast_w_in")}
    (fl_in, fl_conv), first = _split_call("gather_start_first", start=[
        _gather([full["w_in"]], ici=[(0, ALL)]), _gather_whole(w["conv_w"])])
    casts, _ = _cast_place_multi([w[n] for n in BIG[1:]], chip.reshape(1), stages=[_after(first)])
    full.update(zip(BIG[1:], casts))
    (fl_mix, fl_ff1, fl_ff2), started = _split_call("gather_start_rest", start=[
        _gather([full[n] for n in mix], ici=[(0, ALL), (1, ALL), (2, ALL)]),
        _gather([full["w_ff1"]], ici=[(0, ALL)]), _gather([full["w_ff2"]], ici=[(0, ALL)])])
    wa = _block_diag(lru_w_a[0]).astype(BF)
    wx = _block_diag(lru_w_x[0]).astype(BF)
    pw = pool_w[0].astype(BF)

    def to_sibling(name, flight, after=None):
        (fl,), passed = _split_call(name + "_pass", finish=[flight], after=after,
                                    start=[_gather(flight.landed(), d2d=[(i, ALL) for i in range(len(flight.bufs))])])
        passed_on.append(passed)
        return fl

    passed_on = []

    def arrived(name, flight, after=None):
        _split_call(name + "_done", finish=[flight], after=after)
        return flight.landed()

    idx_big = jnp.stack([chip, (chip + 1) % NCHIP, (chip + 2) % NCHIP, (chip + 3) % NCHIP, core])
    proj, h1 = _fwd_inproj_own(xs, g1, fl_in.bufs[0], idx_big, stages=[_after(started)])
    fl_in = to_sibling("gather_w_in", fl_in, after=h1)
    w_in_f, = arrived("gather_w_in", fl_in)
    conv_all, = arrived("gather_conv", fl_conv)
    full["w_in"] = w_in_f
    conv_all = lax.dynamic_update_slice(conv_all, w["conv_w"][None], (chip, zero, zero))
    conv_full = jnp.transpose(conv_all, (1, 0, 2)).reshape(4, DR)
    proj = _fwd_inproj_rest(h1, w_in_f, proj, idx_big)
    fl_mix = to_sibling("gather_mix", fl_mix, after=proj)
    (ylru, hs), _ = _fwd_lru(proj, conv_full, conv_b, wa, lru_b_a, wx, lru_b_x, lru_lambda,
                             stages=[_after(passed_on[-1])])
    got = arrived("gather_mix", fl_mix, after=ylru)
    fl_ff1 = to_sibling("gather_ff1", fl_ff1, after=ylru)
    w_lru_up_f, w_pool_up_f, w_o_f = got[0].reshape(DR, D), got[1], got[2].reshape(D, D)
    ypool = _fwd_pool(proj, pw, pool_scale)
    (x2, h2, m, mrg, bra, brb), _ = _fwd_merge(xs, ylru, ypool, proj, b_gate, g2, g3, w_lru_up_f, w_pool_up_f, w_o_f,
                                               stages=[_after(passed_on[-1])])
    fl_ff2 = to_sibling("gather_ff2", fl_ff2, after=h2)
    ff1, = arrived("gather_ff1", fl_ff1, after=h2)
    ff2, = arrived("gather_ff2", fl_ff2)
    ff2 = ff2.reshape(DF, D)
    a1, f = _fwd_mlp(h2, ff1, ff2)
    lossp, dy, df, dg4 = _loss_head(f, x2, target, g4)

    dh2, df1 = _bwd_mlp_x(df, a1, ff1, ff2)
    dw_ff1, dw_ff2 = _bwd_mlp_w(df, h2, a1, df1)
    g_ff = [_halves(dw_ff1), _halves(dw_ff2)]
    (dxres, dgates, dylru, dypool, dm, dbra, dbrb, dg2, dg3, dbg), (l_ff,) = _bwd_merge(
        dh2, dy, x2, m, bra, brb, proj, b_gate, g2, g3, w_lru_up_f, w_pool_up_f, w_o_f, stages=[_to_sibling(g_ff)])
    p_ff = _add_sibling_multi(g_ff, l_ff, cidx, "add_sibling_ff")
    (fl_ff,), sent_ff = _split_call("reduce_ff_start", start=[_to_chips(p_ff)])
    (dw_o, dw_lru_up, dw_pool_up), _ = _dw_merge(mrg, dm, ylru, dbra, ypool, dbrb, stages=[_after(sent_ff)])
    g_mix = [_halves(dw_lru_up), _halves(dw_pool_up), _halves(dw_o)]
    (dxp, dgl, dcw, dcb, dwa, dba, dwx, dbx, dlam), (l_mix,) = _bwd_lru(
        proj, hs, dylru, conv_full, conv_b, wa, lru_b_a, wx, lru_b_x, lru_lambda, stages=[_to_sibling(g_mix)])
    p_mix = _add_sibling_multi(g_mix, l_mix, cidx, "add_sibling_mix")
    dxpool, dpw, dsc = _bwd_pool(proj, dypool, pw, pool_scale)
    dproj = jnp.concatenate([dxp, dgl, dxpool, dgates], axis=1)
    small = jnp.concatenate([
        jnp.zeros((1, D), F32), dg2, dg3, dg4, dbg.reshape(2, D), dcb, dba, dbx, dlam,
        jnp.pad(dsc, ((0, 0), (0, D - DP))), jnp.pad(lossp, ((0, 0), (0, D - 1))), dcw,
        _block_diag_extract(dwa, hd).reshape(-1, D), _block_diag_extract(dwx, hd).reshape(-1, D),
        dpw.reshape(-1, D)], axis=0)
    (fl_mixr, fl_smalls), sent_mix = _split_call("reduce_mix_start", start=[_to_chips(p_mix), _to_sibling([small])])
    dw_in, _ = _bwd_inproj_w(h1, dproj, stages=[_after(sent_mix)])
    _split_call("reduce_small_sibling_done", finish=[fl_smalls], after=dw_in)
    small, l_small = fl_smalls.bufs
    small2 = _add_pair(small, l_small, "add_sibling_small").reshape(2, SMALL_ROWS // 2, D)
    g_in = _halves(dw_in)
    done = ["w_ff1", "w_ff2"] + mix
    (fl_gin, fl_small), _ = _split_call("reduce_in_sibling_start", start=[_to_sibling([g_in]), _to_chips([small2])])
    _split_call("reduce_in_sibling_done", finish=[fl_gin, fl_ff, fl_mixr])
    (g_in, l_in), (p_ff1, p_ff2, c_ff1, c_ff2) = fl_gin.bufs, fl_ff.bufs
    p_mix, c_mix = fl_mixr.bufs[:3], fl_mixr.bufs[3:]
    p_in = _add_sibling(g_in, l_in, cidx, "add_sibling_w_in")[0]
    ssem, rsem, p_in, c_in, token = _chips_start(p_in)
    pairs, _ = _add_chips_multi([p_ff1, p_ff2] + p_mix, [c_ff1, c_ff2] + c_mix, idx_big, "add_chips_done",
                                stages=[_after(token)])
    _split_call("reduce_small_done", finish=[fl_small], after=pairs[-1])
    small2, c_small = fl_small.bufs
    own_small = lax.dynamic_index_in_dim(small2, core, 0, keepdims=True)
    c_small = lax.dynamic_update_slice(c_small, own_small, (chip, zero, zero))
    pair_small = _add_chips(c_small, c_small, jnp.stack([zero, zero + 1, zero + 2, zero + 3, core]), "add_chips_small")
    (fl_share,), shared_start = _split_call("reduce_share_start", start=[_share(pairs + [pair_small])])
    grad_x, dg1 = _bwd_inproj_x(dproj, full["w_in"], xs, dxres, g1, stages=[_after(shared_start)])
    _split_call("reduce_share_done", finish=[fl_share], after=dg1)
    shared = fl_share.landed()
    pairs, pair_small = shared[:-1], shared[-1]

    grads, delta, new_m, new_v = {}, {}, {}, {}
    for n, p in zip(done, pairs):
        grads[n] = p.reshape(-1, p.shape[-1])

    def update(n, stages=()):
        (grads[n], delta[n], new_m[n], new_v[n]), landed = _adamw(w[n], grads[n], mom[n], var[n], "adamw_" + n,
                                                                  stages=stages)
        return landed

    updated, _ = _adamw_multi(["w_ff1", "w_ff2", "w_o", "w_lru_up"], w, grads, mom, var)
    for n, (go, d, mo, vo) in updated.items():
        grads[n], delta[n], new_m[n], new_v[n] = go, d, mo, vo
    p_in, c_in = _chips_wait(ssem, rsem, p_in, c_in, new_v["w_lru_up"])
    pair_in = _add_chips(p_in, c_in, idx_big, "add_chips_w_in")
    ((pair_in,), (dg1_all,)) = update("w_pool_up", stages=[_share([pair_in]), _to_everyone(dg1)])
    dg1_all = lax.dynamic_update_slice(dg1_all, dg1[None], (2 * chip + core, zero, zero)).reshape(2 * NCHIP, D)
    grads["w_in"] = pair_in.reshape(-1, pair_in.shape[-1])
    update("w_in")
    small_sum = pair_small.reshape(SMALL_ROWS, D)
    loss = 0.5 * small_sum[LOSS_ROW, 0]
    ccols = DR // NCHIP
    sep = [lax.dynamic_slice(small_sum[12:16], (zero, chip * ccols), (4, ccols)),
           small_sum[16:80].reshape(-1, hd), small_sum[80:144].reshape(-1, hd), small_sum[144:208].reshape(-1, PG)]
    g_s, d_s, m_s, v_s = _adamw_small(small_sum, dg1_all, sep, w, mom, var)
    grads.update(g_s)
    grads.update(dict(zip(SMALL_SEPARATE, sep)))
    delta.update(d_s)
    new_m.update(m_s)
    new_v.update(v_s)

    out = lambda d: [d[n].reshape(args[n].shape) for n in W_NAMES]
    return (loss, grad_x[None], *out(grads), *out(delta), *out(new_m), *out(new_v))
```

```python
import functools
import math

import jax
import jax.numpy as jnp
from jax import lax
from jax.experimental import pallas as pl
from jax.experimental.pallas import tpu as pltpu

F32 = jnp.float32
BF = jnp.bfloat16

T = 2048
D = 1024
DR = 1024
DP = 512
DF = 4096
DIN = 4608
NCHIP = 4
CW_IN = DIN // NCHIP
LANE = 128
CB = 128
NG = DR // CB
PG = 128
POOL_WINDOWS = (2, 4, 8, 16)
NORM_EPS = 1e-6
LRU_C = 8.0
GELU_C = math.sqrt(2.0 / math.pi)
ADAM_LR = 0.001
ADAM_B1 = 0.9
ADAM_B2 = 0.999
ADAM_EPS = 1e-08
ADAM_WD = 0.01
ADAM_STEP = 10
MESH_ID = pl.DeviceIdType.MESH
ANY = pl.BlockSpec(memory_space=pl.ANY)
SMALL_ROWS = 208
LOSS_ROW = 11
MIB = 1 << 20


def _cp(vmem_mib=None):
    if vmem_mib is None:
        return pltpu.CompilerParams()
    return pltpu.CompilerParams(vmem_limit_bytes=vmem_mib * MIB)


def _hbm(*arrays):
    return [pltpu.with_memory_space_constraint(a, pltpu.HBM) for a in arrays]


def _hbm_out(shapes):
    return [pltpu.HBM(s.shape, s.dtype) for s in shapes]


class _Stage:
    def __init__(self, operands, out_shape, alias, sems, start, finish):
        self.operands, self.out_shape, self.alias, self.sems = list(operands), list(out_shape), dict(alias), list(sems)
        self.start, self.finish = start, finish


def _call(body, *, name, grid, in_specs, out_specs, out_shape, args, vmem=None, stages=(), prefetch=None,
          scratch=()):
    nin, nout = len(in_specs), len(out_specs)
    npre = 0 if prefetch is None else 1
    st_args, st_shapes, st_sems, aliases = [], [], list(scratch), {}
    for st in stages:
        for k, v in st.alias.items():
            aliases[npre + nin + len(st_args) + k] = nout + len(st_shapes) + v
        st_args += st.operands
        st_shapes += st.out_shape
        st_sems += st.sems

    def wrapped(*refs):
        pre, refs = refs[:npre], refs[npre:]
        ins, pos = refs[:nin], nin
        st_ins = []
        for st in stages:
            st_ins.append(refs[pos:pos + len(st.operands)])
            pos += len(st.operands)
        outs, pos = refs[pos:pos + nout], pos + nout
        st_outs = []
        for st in stages:
            st_outs.append(refs[pos:pos + len(st.out_shape)])
            pos += len(st.out_shape)
        work, pos = refs[pos:pos + len(scratch)], pos + len(scratch)
        sems = []
        for st in stages:
            sems.append(refs[pos:pos + len(st.sems)])
            pos += len(st.sems)
        if stages:
            first = functools.reduce(jnp.logical_and, [pl.program_id(a) == 0 for a in range(len(grid))])

            @pl.when(first)
            def _():
                for st, a, b, s in zip(stages, st_ins, st_outs, sems):
                    st.start(a, b, s)

        body(*pre, *ins, *outs, *work)
        if stages:
            last = functools.reduce(jnp.logical_and, [pl.program_id(a) == g - 1 for a, g in enumerate(grid)])

            @pl.when(last)
            def _():
                for st, a, b, s in zip(stages, st_ins, st_outs, sems):
                    st.finish(a, b, s)

    all_in = list(in_specs) + [ANY] * len(st_args)
    all_out = list(out_specs) + [ANY] * len(st_shapes)
    kw = dict(has_side_effects=True) if stages else {}
    if vmem is not None:
        kw["vmem_limit_bytes"] = vmem * MIB
    if prefetch is None:
        gkw = dict(grid=grid, in_specs=all_in, out_specs=all_out, scratch_shapes=st_sems)
    else:
        gkw = dict(grid_spec=pltpu.PrefetchScalarGridSpec(
            num_scalar_prefetch=1, grid=grid, in_specs=all_in, out_specs=all_out, scratch_shapes=st_sems))
    res = pl.pallas_call(
        wrapped, name=name, out_shape=_hbm_out(list(out_shape) + st_shapes), input_output_aliases=aliases,
        compiler_params=pltpu.CompilerParams(**kw), **gkw,
    )(*([prefetch] if npre else []), *_hbm(*args, *st_args))
    outs, rest, st_res = list(res[:nout]), list(res[nout:]), []
    for st in stages:
        st_res.append(rest[:len(st.out_shape)])
        rest = rest[len(st.out_shape):]
    return outs, st_res


def _mm(a, b):
    return jnp.dot(a.astype(BF), b.astype(BF), preferred_element_type=F32)


def _mm_nt(a, b):
    return lax.dot_general(a.astype(BF), b.astype(BF), (((1,), (1,)), ((), ())),
                           preferred_element_type=F32)


def _mm_tn(a, b):
    return lax.dot_general(a.astype(BF), b.astype(BF), (((0,), (0,)), ((), ())),
                           preferred_element_type=F32)


def _rows(v):
    return lax.broadcasted_iota(jnp.int32, v.shape, 0)


def _sd(v, s, fill=0.0):
    return jnp.where(_rows(v) >= s, pltpu.roll(v, s, axis=0), fill)


def _su(v, s, fill=0.0):
    n = v.shape[0]
    return jnp.where(_rows(v) < n - s, pltpu.roll(v, n - s, axis=0), fill)


def _sigmoid(z):
    return 1.0 / (1.0 + jnp.exp(-z))


def _softplus(z):
    e = jnp.exp(-jnp.abs(z))
    u = 1.0 + e
    d = u - 1.0
    log1p = jnp.where(d == 0.0, e, jnp.log(u) * (e / jnp.where(d == 0.0, 1.0, d)))
    return jnp.maximum(z, 0.0) + log1p


def _mean(v):
    return jnp.mean(v, axis=-1, keepdims=True)


def _colsum(v):
    return jnp.sum(v, axis=0, keepdims=True)


def _acc(ref, val, first):
    @pl.when(first)
    def _():
        ref[...] = val

    @pl.when(jnp.logical_not(first))
    def _():
        ref[...] += val


def _conv(xp, cw, cb):
    x1, x2, x3 = _sd(xp, 1), _sd(xp, 2), _sd(xp, 3)
    xc = cb + cw[0:1] * x3 + cw[1:2] * x2 + cw[2:3] * x1 + cw[3:4] * xp
    return xc, x1, x2, x3


def _lru_gates(xc, wa, ba, wx, bx, lam):
    xcb = xc.astype(BF)
    r = _sigmoid(_mm(xcb, wa) + ba)
    ii = _sigmoid(_mm(xcb, wx) + bx)
    sp = _softplus(-lam)
    la = (-LRU_C) * r * sp
    a = jnp.exp(la)
    mult = jnp.sqrt(-jnp.tanh(la) * (a * a + 1.0))
    return xcb, r, ii, sp, a, mult


def _gelu_parts(g):
    th = jnp.tanh(GELU_C * (g + 0.044715 * (g * g * g)))
    gel = 0.5 * g * (1.0 + th)
    dgel = 0.5 * (1.0 + th) + 0.5 * g * (1.0 - th * th) * (GELU_C * (1.0 + 3.0 * 0.044715 * (g * g)))
    return gel, dgel


def _tile_scan(a, b, a_s, b_s, out_ref, reverse):
    n = a.shape[0]
    nt = n // 8
    sub = jnp.bitwise_and(_rows(a), 7)
    s = 1
    while s < 8:
        keep = sub < 8 - s if reverse else sub >= s
        amount = n - s if reverse else s
        b = b + a * jnp.where(keep, pltpu.roll(b, amount, axis=0), 0.0)
        a = a * jnp.where(keep, pltpu.roll(a, amount, axis=0), 1.0)
        s *= 2
    a_s[...] = a
    b_s[...] = b
    edge = pl.ds(0 if reverse else 7, nt, stride=8)
    ta, tb = a_s[edge, :], b_s[edge, :]
    shift = _su if reverse else _sd
    s = 1
    while s < nt:
        tb = tb + ta * shift(tb, s, 0.0)
        if 2 * s < nt:
            ta = ta * shift(ta, s, 1.0)
        s *= 2
    enters = shift(tb, 1, 0.0)
    for o in range(8):
        rows = pl.ds(o, nt, stride=8)
        out_ref[rows, :] = b_s[rows, :] + a_s[rows, :] * enters


def _pool_window(x, steps, shift):
    s, sh = x, 1
    for _ in range(steps):
        s = s + shift(s, sh)
        sh *= 2
    return s


def _fwd_inproj_own(x, g1, w_in, slots, stages=()):
    tm = 512

    def body(s_ref, x_ref, g_ref, w_ref, proj_ref, h_ref):
        xv = x_ref[...]
        r = lax.rsqrt(_mean(xv * xv) + NORM_EPS)
        h = ((xv * r) * g_ref[...]).astype(BF)
        h_ref[...] = h
        proj_ref[...] = jnp.dot(h, w_ref[0], preferred_element_type=F32)

    return _call(
        body, name="fwd_inproj_own", grid=(T // tm,), prefetch=slots,
        in_specs=[pl.BlockSpec((tm, D), lambda i, s: (i, 0)),
                  pl.BlockSpec((1, D), lambda i, s: (0, 0)),
                  pl.BlockSpec((1, D, CW_IN), lambda i, s: (s[0], 0, 0))],
        out_specs=[pl.BlockSpec((tm, CW_IN), lambda i, s: (i, s[0])),
                   pl.BlockSpec((tm, D), lambda i, s: (i, 0))],
        out_shape=[jax.ShapeDtypeStruct((T, DIN), F32), jax.ShapeDtypeStruct((T, D), BF)],
        vmem=40, args=[x, g1, w_in], stages=stages)[0]


def _fwd_inproj_rest(h1, w_in, proj, slots):
    tm = 512

    def body(s_ref, h_ref, w_ref, p_in, proj_ref):
        proj_ref[...] = jnp.dot(h_ref[...], w_ref[0], preferred_element_type=F32)

    res = pl.pallas_call(
        body, name="fwd_inproj_rest",
        grid_spec=pltpu.PrefetchScalarGridSpec(
            num_scalar_prefetch=1, grid=(T // tm, NCHIP - 1),
            in_specs=[pl.BlockSpec((tm, D), lambda i, k, s: (i, 0)),
                      pl.BlockSpec((1, D, CW_IN), lambda i, k, s: (s[1 + k], 0, 0)), ANY],
            out_specs=pl.BlockSpec((tm, CW_IN), lambda i, k, s: (i, s[1 + k]))),
        out_shape=pltpu.HBM((T, DIN), F32), input_output_aliases={3: 0},
        compiler_params=_cp(40),
    )(slots, *_hbm(h1, w_in, proj))
    return res


def _vec_spec():
    return pl.BlockSpec((1, CB), lambda j: (0, j))


def _fwd_lru(proj, conv_w, conv_b, wa, ba, wx, bx, lam, stages=()):
    def body(xp_ref, g_ref, cw_ref, cb_ref, wa_ref, ba_ref, wx_ref, bx_ref, lam_ref, y_ref, h_ref, a_s, b_s):
        xc, _, _, _ = _conv(xp_ref[...], cw_ref[...], cb_ref[...])
        _, _, ii, _, a, mult = _lru_gates(xc, wa_ref[0], ba_ref[...], wx_ref[0], bx_ref[...], lam_ref[...])
        _tile_scan(a, mult * (ii * xc), a_s, b_s, h_ref, reverse=False)
        gel, _ = _gelu_parts(g_ref[...])
        y_ref[...] = (h_ref[...] * gel).astype(BF)

    return _call(
        body, name="fwd_lru", grid=(NG,),
        in_specs=[pl.BlockSpec((T, CB), lambda j: (0, j)),
                  pl.BlockSpec((T, CB), lambda j: (0, NG + j)),
                  pl.BlockSpec((4, CB), lambda j: (0, j)),
                  _vec_spec(),
                  pl.BlockSpec((1, CB, CB), lambda j: (j, 0, 0)), _vec_spec(),
                  pl.BlockSpec((1, CB, CB), lambda j: (j, 0, 0)), _vec_spec(),
                  _vec_spec()],
        out_specs=[pl.BlockSpec((T, CB), lambda j: (0, j)), pl.BlockSpec((T, CB), lambda j: (0, j))],
        out_shape=[jax.ShapeDtypeStruct((T, DR), BF), jax.ShapeDtypeStruct((T, DR), F32)],
        vmem=48, args=[proj, proj, conv_w, conv_b, wa, ba, wx, bx, lam], stages=stages,
        scratch=[pltpu.VMEM((T, CB), F32)] * 2)


def _pool_cnt(w):
    t = lax.broadcasted_iota(jnp.int32, (T, 1), 0)
    return jnp.minimum(t + 1, w).astype(F32)


def _fwd_pool(proj, pool_w, pool_scale):
    def body(xp_ref, pw_ref, sc_ref, y_ref):
        for g, w in enumerate(POOL_WINDOWS):
            cols = slice(g * PG, (g + 1) * PG)
            x = xp_ref[:, cols]
            p = _pool_window(x, g + 1, _sd) / _pool_cnt(w) - x
            y_ref[:, cols] = (_mm(p, pw_ref[g]) * sc_ref[:, cols]).astype(BF)

    return pl.pallas_call(
        body, name="fwd_pool", grid=(1,),
        in_specs=[pl.BlockSpec((T, DP), lambda i: (0, 2 * DR // DP)),
                  pl.BlockSpec((4, PG, PG), lambda i: (0, 0, 0)),
                  pl.BlockSpec((1, DP), lambda i: (0, 0))],
        out_specs=pl.BlockSpec((T, DP), lambda i: (0, 0)),
        out_shape=pltpu.HBM((T, DP), BF),
        compiler_params=_cp(48),
    )(*_hbm(proj, pool_w, pool_scale))


GATE_BLK = 512
GATE_BLK0 = (2 * DR + DP) // GATE_BLK


def _gate_specs(tm):
    return [pl.BlockSpec((tm, GATE_BLK), functools.partial(lambda i, q: (i, GATE_BLK0 + q), q=q))
            for q in range(4)]


def _fwd_merge(x, ylru, ypool, proj, b_gate, g2, g3, w_lru_up, w_pool_up, w_o, stages=()):
    tm = 512

    def body(x_ref, yl_ref, yp_ref, p0, p1, p2, p3, bg_ref, g2_ref, g3_ref, wl_ref, wp_ref, wo_ref,
             x2_ref, h2_ref, m_ref, mrg_ref, bra_ref, brb_ref):
        bra = jnp.dot(yl_ref[...], wl_ref[...], preferred_element_type=F32)
        yp = yp_ref[...]
        brb = jnp.concatenate([jnp.dot(yp, wp_ref[k], preferred_element_type=F32) for k in range(NCHIP)], axis=1)
        bg = bg_ref[...]
        ga = _sigmoid(jnp.concatenate([p0[...], p1[...]], axis=1) + bg[:, :D])
        gb = _sigmoid(jnp.concatenate([p2[...], p3[...]], axis=1) + bg[:, D:])
        mrg = (ga * bra + gb * brb).astype(BF)
        m = jnp.dot(mrg, wo_ref[...], preferred_element_type=F32)
        r2 = lax.rsqrt(_mean(m * m) + NORM_EPS)
        x2 = x_ref[...] + (m * r2) * g2_ref[...]
        r3 = lax.rsqrt(_mean(x2 * x2) + NORM_EPS)
        x2_ref[...] = x2
        h2_ref[...] = ((x2 * r3) * g3_ref[...]).astype(BF)
        m_ref[...] = m
        mrg_ref[...] = mrg
        bra_ref[...] = bra.astype(BF)
        brb_ref[...] = brb.astype(BF)

    row = lambda w: pl.BlockSpec((tm, w), lambda i: (i, 0))
    full2 = lambda a, b: pl.BlockSpec((a, b), lambda i: (0, 0))
    return _call(
        body, name="fwd_merge", grid=(T // tm,),
        in_specs=[row(D), row(DR), row(DP)] + _gate_specs(tm) +
                 [full2(1, 2 * D), full2(1, D), full2(1, D), full2(DR, D),
                  pl.BlockSpec((NCHIP, DP, D // NCHIP), lambda i: (0, 0, 0)), full2(D, D)],
        out_specs=[row(D)] * 6,
        out_shape=[jax.ShapeDtypeStruct((T, D), F32), jax.ShapeDtypeStruct((T, D), BF),
                   jax.ShapeDtypeStruct((T, D), F32), jax.ShapeDtypeStruct((T, D), BF),
                   jax.ShapeDtypeStruct((T, D), BF), jax.ShapeDtypeStruct((T, D), BF)],
        vmem=48, args=[x, ylru, ypool, proj, proj, proj, proj, b_gate, g2, g3, w_lru_up, w_pool_up, w_o],
        stages=stages)


def _fwd_mlp(h2, w_ff1, w_ff2):
    tm = 512
    fk = DF // NCHIP

    def body(h_ref, w1_ref, w2_ref, a1_ref, f_ref):
        h = h_ref[...]
        f = None
        for k in range(NCHIP):
            a1 = jnp.maximum(jnp.dot(h, w1_ref[k], preferred_element_type=F32), 0.0)
            a1_ref[:, k * fk:(k + 1) * fk] = a1.astype(BF)
            part = jnp.dot((a1 * a1).astype(BF), w2_ref[k * fk:(k + 1) * fk, :], preferred_element_type=F32)
            f = part if f is None else f + part
        f_ref[...] = f

    return pl.pallas_call(
        body, name="fwd_mlp", grid=(T // tm,),
        in_specs=[pl.BlockSpec((tm, D), lambda i: (i, 0)),
                  pl.BlockSpec((NCHIP, D, fk), lambda i: (0, 0, 0)),
                  pl.BlockSpec((DF, D), lambda i: (0, 0))],
        out_specs=[pl.BlockSpec((tm, DF), lambda i: (i, 0)), pl.BlockSpec((tm, D), lambda i: (i, 0))],
        out_shape=_hbm_out([jax.ShapeDtypeStruct((T, DF), BF), jax.ShapeDtypeStruct((T, D), F32)]),
        compiler_params=_cp(56),
    )(*_hbm(h2, w_ff1, w_ff2))


def _loss_head(f, x2, target, g4):
    tm = 512

    def body(f_ref, x2_ref, t_ref, g_ref, loss_ref, dy_ref, df_ref, dg_ref):
        first = pl.program_id(0) == 0
        f = f_ref[...]
        g4v = g_ref[...]
        r4 = lax.rsqrt(_mean(f * f) + NORM_EPS)
        fn = f * r4
        e = (x2_ref[...] + fn * g4v) - t_ref[...]
        _acc(loss_ref, jnp.sum(_mean(e * e), axis=0, keepdims=True), first)
        dy = e * (1.0 / D)
        dy_ref[...] = dy
        _acc(dg_ref, _colsum(dy * fn), first)
        dfn = dy * g4v
        df_ref[...] = (r4 * (dfn - fn * _mean(dfn * fn))).astype(BF)

    row = pl.BlockSpec((tm, D), lambda i: (i, 0))
    return pl.pallas_call(
        body, name="loss_head", grid=(T // tm,),
        in_specs=[row, row, row, pl.BlockSpec((1, D), lambda i: (0, 0))],
        out_specs=[pl.BlockSpec((1, 1), lambda i: (0, 0)), row, row, pl.BlockSpec((1, D), lambda i: (0, 0))],
        out_shape=_hbm_out([jax.ShapeDtypeStruct((1, 1), F32), jax.ShapeDtypeStruct((T, D), F32),
                            jax.ShapeDtypeStruct((T, D), BF), jax.ShapeDtypeStruct((1, D), F32)]),
        compiler_params=_cp(48),
    )(*_hbm(f, x2, target, g4))


def _bwd_mlp_x(df, a1, w_ff1, w_ff2):
    tm = 512
    fk = DF // NCHIP

    def body(df_ref, a1_ref, w1_ref, w2_ref, dh_ref, df1_ref):
        df = df_ref[...]
        dh = None
        for k in range(NCHIP):
            cols = slice(k * fk, (k + 1) * fk)
            dact = _mm_nt(df, w2_ref[cols, :])
            df1 = (dact * (2.0 * a1_ref[:, cols].astype(F32))).astype(BF)
            df1_ref[:, cols] = df1
            part = _mm_nt(df1, w1_ref[k])
            dh = part if dh is None else dh + part
        dh_ref[...] = dh

    return pl.pallas_call(
        body, name="bwd_mlp_x", grid=(T // tm,),
        in_specs=[pl.BlockSpec((tm, D), lambda i: (i, 0)),
                  pl.BlockSpec((tm, DF), lambda i: (i, 0)),
                  pl.BlockSpec((NCHIP, D, fk), lambda i: (0, 0, 0)),
                  pl.BlockSpec((DF, D), lambda i: (0, 0))],
        out_specs=[pl.BlockSpec((tm, D), lambda i: (i, 0)), pl.BlockSpec((tm, DF), lambda i: (i, 0))],
        out_shape=_hbm_out([jax.ShapeDtypeStruct((T, D), F32), jax.ShapeDtypeStruct((T, DF), BF)]),
        compiler_params=_cp(56),
    )(*_hbm(df, a1, w_ff1, w_ff2))


def _bwd_mlp_w(df, h2, a1, df1):
    fc = 512
    per = (DF // NCHIP) // fc

    def body(df_ref, h_ref, a1_ref, df1_ref, dw1_ref, dw2_ref):
        a1 = a1_ref[...].astype(F32)
        dw2_ref[...] = _mm_tn((a1 * a1).astype(BF), df_ref[...]).astype(BF)
        dw1_ref[0] = _mm_tn(h_ref[...], df1_ref[...]).astype(BF)

    return pl.pallas_call(
        body, name="bwd_mlp_w", grid=(DF // fc,),
        in_specs=[pl.BlockSpec((T, D), lambda j: (0, 0)),
                  pl.BlockSpec((T, D), lambda j: (0, 0)),
                  pl.BlockSpec((T, fc), lambda j: (0, j)),
                  pl.BlockSpec((T, fc), lambda j: (0, j))],
        out_specs=[pl.BlockSpec((1, D, fc), lambda j: (j // per, 0, j % per)),
                   pl.BlockSpec((fc, D), lambda j: (j, 0))],
        out_shape=_hbm_out([jax.ShapeDtypeStruct((NCHIP, D, DF // NCHIP), BF),
                            jax.ShapeDtypeStruct((DF, D), BF)]),
        compiler_params=_cp(56),
    )(*_hbm(df, h2, a1, df1))


def _bwd_merge(dh2, dy, x2, m, bra, brb, proj, b_gate, g2, g3, w_lru_up, w_pool_up, w_o, stages=()):
    tm = 256
    cpu = D // NCHIP

    def body(dh2_ref, dy_ref, x2_ref, m_ref, bra_ref, brb_ref, p0, p1, p2, p3, bg_ref,
             g2_ref, g3_ref, wl_ref, wp_ref, wo_ref,
             dx_ref, dgt_ref, dyl_ref, dyp_ref, dm_ref, dbra_ref, dbrb_ref, dg2_ref, dg3_ref, dbg_ref):
        first = pl.program_id(0) == 0
        x2 = x2_ref[...]
        r3 = lax.rsqrt(_mean(x2 * x2) + NORM_EPS)
        x2n = x2 * r3
        dh2 = dh2_ref[...]
        t3 = dh2 * g3_ref[...]
        dx2 = dy_ref[...] + r3 * (t3 - x2n * _mean(t3 * x2n))
        dx_ref[...] = dx2
        _acc(dg3_ref, _colsum(dh2 * x2n), first)
        m = m_ref[...]
        r2 = lax.rsqrt(_mean(m * m) + NORM_EPS)
        mn = m * r2
        _acc(dg2_ref, _colsum(dx2 * mn), first)
        dmn = dx2 * g2_ref[...]
        dm = (r2 * (dmn - mn * _mean(dmn * mn))).astype(BF)
        dm_ref[...] = dm
        dmrg = _mm_nt(dm, wo_ref[...])
        bg = bg_ref[...]
        ga = _sigmoid(jnp.concatenate([p0[...], p1[...]], axis=1) + bg[:, :D])
        gb = _sigmoid(jnp.concatenate([p2[...], p3[...]], axis=1) + bg[:, D:])
        dga = dmrg * bra_ref[...].astype(F32) * (ga * (1.0 - ga))
        dgb = dmrg * brb_ref[...].astype(F32) * (gb * (1.0 - gb))
        dgt_ref[:, :D] = dga.astype(BF)
        dgt_ref[:, D:] = dgb.astype(BF)
        _acc(dbg_ref, jnp.concatenate([_colsum(dga), _colsum(dgb)], axis=1), first)
        dbra = (dmrg * ga).astype(BF)
        dbrb = (dmrg * gb).astype(BF)
        dbra_ref[...] = dbra
        dbrb_ref[...] = dbrb
        dyl_ref[...] = _mm_nt(dbra, wl_ref[...])
        dyp = None
        for k in range(NCHIP):
            part = _mm_nt(dbrb[:, k * cpu:(k + 1) * cpu], wp_ref[k])
            dyp = part if dyp is None else dyp + part
        dyp_ref[...] = dyp

    row = lambda w: pl.BlockSpec((tm, w), lambda i: (i, 0))
    full2 = lambda a, b: pl.BlockSpec((a, b), lambda i: (0, 0))
    wp_spec = pl.BlockSpec((NCHIP, DP, cpu), lambda i: (0, 0, 0))
    return _call(
        body, name="bwd_merge", grid=(T // tm,),
        in_specs=[row(D)] * 6 + _gate_specs(tm) +
                 [full2(1, 2 * D), full2(1, D), full2(1, D), full2(DR, D), wp_spec, full2(D, D)],
        out_specs=[row(D), row(2 * D), row(DR), row(DP), row(D), row(D), row(D),
                   full2(1, D), full2(1, D), full2(1, 2 * D)],
        out_shape=[jax.ShapeDtypeStruct((T, D), F32), jax.ShapeDtypeStruct((T, 2 * D), BF),
                   jax.ShapeDtypeStruct((T, DR), F32), jax.ShapeDtypeStruct((T, DP), F32),
                   jax.ShapeDtypeStruct((T, D), BF), jax.ShapeDtypeStruct((T, D), BF),
                   jax.ShapeDtypeStruct((T, D), BF),
                   jax.ShapeDtypeStruct((1, D), F32), jax.ShapeDtypeStruct((1, D), F32),
                   jax.ShapeDtypeStruct((1, 2 * D), F32)],
        vmem=56, args=[dh2, dy, x2, m, bra, brb, proj, proj, proj, proj, b_gate, g2, g3, w_lru_up, w_pool_up, w_o],
        stages=stages)


def _dw_merge(mrg, dm, ylru, dbra, ypool, dbrb, stages=()):
    nb = NCHIP
    rb, pb, cpu = D // nb, DP // nb, D // NCHIP

    def body(mrg_ref, dm_ref, yl_ref, dbra_ref, yp_ref, dbrb_ref, dwo_ref, dwl_ref, dwp_ref):
        dwo_ref[...] = _mm_tn(mrg_ref[...], dm_ref[...]).astype(BF)
        dwl_ref[...] = _mm_tn(yl_ref[...], dbra_ref[...]).astype(BF)
        dwp = _mm_tn(yp_ref[...], dbrb_ref[...]).astype(BF)
        for k in range(NCHIP):
            dwp_ref[k] = dwp[:, k * cpu:(k + 1) * cpu]

    cols = lambda w: pl.BlockSpec((T, w), lambda r: (0, r))
    whole = pl.BlockSpec((T, D), lambda r: (0, 0))
    return _call(
        body, name="dw_merge", grid=(nb,),
        in_specs=[cols(rb), whole, cols(rb), whole, cols(pb), whole],
        out_specs=[pl.BlockSpec((rb, D), lambda r: (r, 0)), pl.BlockSpec((rb, D), lambda r: (r, 0)),
                   pl.BlockSpec((NCHIP, pb, cpu), lambda r: (0, r, 0))],
        out_shape=[jax.ShapeDtypeStruct((D, D), BF), jax.ShapeDtypeStruct((DR, D), BF),
                   jax.ShapeDtypeStruct((NCHIP, DP, cpu), BF)],
        vmem=56, args=[mrg, dm, ylru, dbra, ypool, dbrb], stages=stages)


def _bwd_lru(proj, h, dylru, conv_w, conv_b, wa, ba, wx, bx, lam, stages=()):
    def body(xp_ref, g_ref, h_ref, dy_ref, cw_ref, cb_ref, wa_ref, ba_ref, wx_ref, bx_ref, lam_ref,
             dxp_ref, dg_ref, dcw_ref, dcb_ref, dwa_ref, dba_ref, dwx_ref, dbx_ref, dlam_ref, a_s, b_s, l_s):
        xp = xp_ref[...]
        cw = cw_ref[...]
        lam = lam_ref[...]
        xc, x1, x2, x3 = _conv(xp, cw, cb_ref[...])
        wa, wx = wa_ref[0], wx_ref[0]
        xcb, r, ii, sp, a, mult = _lru_gates(xc, wa, ba_ref[...], wx, bx_ref[...], lam)
        g = g_ref[...]
        gel, dgel = _gelu_parts(g)
        h = h_ref[...]
        dy = dy_ref[...]
        dg_ref[...] = (dy * h * dgel).astype(BF)
        _tile_scan(_su(a, 1, 0.0), dy * gel, a_s, b_s, l_s, reverse=True)
        b = l_s[...]
        da = b * _sd(h, 1, 0.0)
        dmult = b * (ii * xc)
        dii = b * (mult * xc)
        dxc = b * (mult * ii)
        dla = da * a - dmult * ((a * a) / mult)
        dr = dla * ((-LRU_C) * sp)
        dsp = _colsum(dla * ((-LRU_C) * r))
        dlam_ref[...] = -dsp / (1.0 + jnp.exp(lam))
        dzr = dr * (r * (1.0 - r))
        dzi = dii * (ii * (1.0 - ii))
        dzrb, dzib = dzr.astype(BF), dzi.astype(BF)
        dxc = dxc + _mm_nt(dzrb, wa) + _mm_nt(dzib, wx)
        dwa_ref[0] = _mm_tn(xcb, dzrb)
        dwx_ref[0] = _mm_tn(xcb, dzib)
        dba_ref[...] = _colsum(dzr)
        dbx_ref[...] = _colsum(dzi)
        dcb_ref[...] = _colsum(dxc)
        dcw_ref[...] = jnp.concatenate([_colsum(dxc * x3), _colsum(dxc * x2), _colsum(dxc * x1),
                                        _colsum(dxc * xp)], axis=0)
        dxp = cw[3:4] * dxc + cw[2:3] * _su(dxc, 1) + cw[1:2] * _su(dxc, 2) + cw[0:1] * _su(dxc, 3)
        dxp_ref[...] = dxp.astype(BF)

    blk = pl.BlockSpec((T, CB), lambda j: (0, j))
    wsp = pl.BlockSpec((1, CB, CB), lambda j: (j, 0, 0))
    return _call(
        body, name="bwd_lru", grid=(NG,),
        in_specs=[blk, pl.BlockSpec((T, CB), lambda j: (0, NG + j)), blk, blk,
                  pl.BlockSpec((4, CB), lambda j: (0, j)), _vec_spec(), wsp, _vec_spec(), wsp, _vec_spec(),
                  _vec_spec()],
        out_specs=[blk, blk, pl.BlockSpec((4, CB), lambda j: (0, j)), _vec_spec(), wsp, _vec_spec(), wsp,
                   _vec_spec(), _vec_spec()],
        out_shape=[jax.ShapeDtypeStruct((T, DR), BF), jax.ShapeDtypeStruct((T, DR), BF),
                   jax.ShapeDtypeStruct((4, DR), F32), jax.ShapeDtypeStruct((1, DR), F32),
                   jax.ShapeDtypeStruct((NG, CB, CB), F32), jax.ShapeDtypeStruct((1, DR), F32),
                   jax.ShapeDtypeStruct((NG, CB, CB), F32), jax.ShapeDtypeStruct((1, DR), F32),
                   jax.ShapeDtypeStruct((1, DR), F32)],
        vmem=56, args=[proj, proj, h, dylru, conv_w, conv_b, wa, ba, wx, bx, lam], stages=stages,
        scratch=[pltpu.VMEM((T, CB), F32)] * 3)


def _bwd_pool(proj, dypool, pool_w, pool_scale):
    def body(xp_ref, dy_ref, pw_ref, sc_ref, dx_ref, dw_ref, dsc_ref):
        for g, w in enumerate(POOL_WINDOWS):
            cols = slice(g * PG, (g + 1) * PG)
            cnt = _pool_cnt(w)
            x = xp_ref[:, cols]
            pb = (_pool_window(x, g + 1, _sd) / cnt - x).astype(BF)
            wg = pw_ref[g]
            dy = dy_ref[:, cols]
            dsc_ref[:, cols] = _colsum(dy * _mm(pb, wg))
            dyp = (dy * sc_ref[:, cols]).astype(BF)
            dw_ref[g] = _mm_tn(pb, dyp)
            dp = _mm_nt(dyp, wg)
            dx_ref[:, cols] = (_pool_window(dp / cnt, g + 1, _su) - dp).astype(BF)

    return pl.pallas_call(
        body, name="bwd_pool", grid=(1,),
        in_specs=[pl.BlockSpec((T, DP), lambda i: (0, 2 * DR // DP)),
                  pl.BlockSpec((T, DP), lambda i: (0, 0)),
                  pl.BlockSpec((4, PG, PG), lambda i: (0, 0, 0)),
                  pl.BlockSpec((1, DP), lambda i: (0, 0))],
        out_specs=[pl.BlockSpec((T, DP), lambda i: (0, 0)),
                   pl.BlockSpec((4, PG, PG), lambda i: (0, 0, 0)),
                   pl.BlockSpec((1, DP), lambda i: (0, 0))],
        out_shape=_hbm_out([jax.ShapeDtypeStruct((T, DP), BF), jax.ShapeDtypeStruct((4, PG, PG), F32),
                            jax.ShapeDtypeStruct((1, DP), F32)]),
        compiler_params=_cp(48),
    )(*_hbm(proj, dypool, pool_w, pool_scale))


def _bwd_inproj_w(h1, dproj, stages=()):
    def body(h_ref, dp_ref, dw_ref):
        dw_ref[0] = _mm_tn(h_ref[...], dp_ref[...]).astype(BF)

    outs, landed = _call(
        body, name="bwd_inproj_w", grid=(NCHIP,),
        in_specs=[pl.BlockSpec((T, D), lambda k: (0, 0)), pl.BlockSpec((T, CW_IN), lambda k: (0, k))],
        out_specs=[pl.BlockSpec((1, D, CW_IN), lambda k: (k, 0, 0))],
        out_shape=[jax.ShapeDtypeStruct((NCHIP, D, CW_IN), BF)], vmem=48, args=[h1, dproj], stages=stages)
    return outs[0], landed


def _bwd_inproj_x(dproj, w_in, x, dxres, g1, stages=()):
    tm = 512

    def body(dp_ref, w_ref, x_ref, dr_ref, g_ref, dx_ref, dg_ref):
        dh = None
        for k in range(NCHIP):
            part = _mm_nt(dp_ref[:, k * CW_IN:(k + 1) * CW_IN], w_ref[k])
            dh = part if dh is None else dh + part
        xv = x_ref[...]
        r = lax.rsqrt(_mean(xv * xv) + NORM_EPS)
        xn = xv * r
        t = dh * g_ref[...]
        dx_ref[...] = dr_ref[...] + r * (t - xn * _mean(t * xn))
        _acc(dg_ref, _colsum(dh * xn), pl.program_id(0) == 0)

    row = pl.BlockSpec((tm, D), lambda i: (i, 0))
    vec = pl.BlockSpec((1, D), lambda i: (0, 0))
    return _call(
        body, name="bwd_inproj_x", grid=(T // tm,),
        in_specs=[pl.BlockSpec((tm, DIN), lambda i: (i, 0)), pl.BlockSpec((NCHIP, D, CW_IN), lambda i: (0, 0, 0)),
                  row, row, vec],
        out_specs=[row, vec],
        out_shape=[jax.ShapeDtypeStruct((T, D), F32), jax.ShapeDtypeStruct((1, D), F32)],
        vmem=56, args=[dproj, w_in, x, dxres, g1], stages=stages)[0]


def _place():
    x, y, c = lax.axis_index("x"), lax.axis_index("y"), lax.axis_index("c")
    chips = [(1 - x, y), (x, 1 - y), (1 - x, 1 - y)]
    return x, y, c, chips


def _rcopy(src, dst, ssem, rsem, dev):
    return pltpu.make_async_remote_copy(src_ref=src, dst_ref=dst, send_sem=ssem, recv_sem=rsem,
                                        device_id=dev, device_id_type=MESH_ID)


def _sds(a):
    return jax.ShapeDtypeStruct(a.shape, a.dtype)


def _sem2(n, m):
    return [pltpu.SemaphoreType.DMA((n * m,)), pltpu.SemaphoreType.DMA((n * m,))]


ALL = (0, 1, 1)


def _piece(ref, k, half, part):
    hr = ref.shape[1] // 2
    r0, r1 = hr * part[0] // part[2], hr * part[1] // part[2]
    return ref.at[k, pl.ds(half * hr + r0, r1 - r0), :]


def _gather(fulls, ici=(), d2d=()):
    n = len(fulls)
    ici, d2d = list(ici), list(d2d)
    pieces = [("ici", i, part) for i, part in ici] + [("d2d", i, part) for i, part in d2d]

    def copies(outs, sems):
        x, y, c, chips = _place()
        me = 2 * x + y
        sib = (x, y, 1 - c)
        send, recv = [], []
        for q, (kind, i, part) in enumerate(pieces):
            for j, chip in enumerate(chips):
                k, s = 2 * chip[0] + chip[1], 3 * q + j
                if kind == "ici":
                    mine, theirs, dev = _piece(outs[i], me, c, part), _piece(outs[i], k, c, part), (*chip, c)
                else:
                    mine, theirs, dev = _piece(outs[i], k, c, part), _piece(outs[i], k, 1 - c, part), sib
                send.append(_rcopy(mine, mine, sems[0].at[s], sems[1].at[s], dev))
                recv.append(_rcopy(theirs, theirs, sems[0].at[s], sems[1].at[s], dev))
        return send, recv

    def start(ins, outs, sems):
        for cp in copies(outs, sems)[0]:
            cp.start()

    def finish(ins, outs, sems):
        send, recv = copies(outs, sems)
        for cp in recv:
            cp.wait_recv()
        for cp in send:
            cp.wait_send()

    sems = [pltpu.SemaphoreType.DMA((3 * len(pieces),)), pltpu.SemaphoreType.DMA((3 * len(pieces),))]
    return _Stage(fulls, [_sds(f) for f in fulls], {i: i for i in range(n)}, sems, start, finish)


def _gather_whole(v):
    def copies(ins, outs, sems):
        x, y, c, chips = _place()
        me = 2 * x + y
        send = [_rcopy(ins[0], outs[0].at[me], sems[0].at[j], sems[1].at[j], (*chip, c))
                for j, chip in enumerate(chips)]
        recv = [_rcopy(ins[0], outs[0].at[2 * chip[0] + chip[1]], sems[0].at[j], sems[1].at[j], (*chip, c))
                for j, chip in enumerate(chips)]
        return send, recv

    def start(ins, outs, sems):
        for cp in copies(ins, outs, sems)[0]:
            cp.start()

    def finish(ins, outs, sems):
        send, recv = copies(ins, outs, sems)
        for cp in recv:
            cp.wait_recv()
        for cp in send:
            cp.wait_send()

    return _Stage([v], [jax.ShapeDtypeStruct((NCHIP,) + v.shape, v.dtype)], {},
                  [pltpu.SemaphoreType.DMA((3,)), pltpu.SemaphoreType.DMA((3,))], start, finish)


def _to_sibling(srcs):
    n = len(srcs)

    def copies(ins, outs, sems):
        x, y, c, _ = _place()
        sib = (x, y, 1 - c)
        return [_rcopy(ins[i].at[:, 1 - c] if srcs[i].ndim == 4 else ins[i], outs[i], sems[0].at[i], sems[1].at[i], sib)
                for i in range(n)]

    def start(ins, outs, sems):
        for cp in copies(ins, outs, sems):
            cp.start()

    def finish(ins, outs, sems):
        for cp in copies(ins, outs, sems):
            cp.wait()

    shapes = [jax.ShapeDtypeStruct((NCHIP,) + s.shape[2:] if s.ndim == 4 else s.shape, s.dtype) for s in srcs]
    return _Stage(srcs, shapes, {}, [pltpu.SemaphoreType.DMA((n,)), pltpu.SemaphoreType.DMA((n,))], start, finish)


def _to_chips(srcs, parts=None, lands=None):
    n = len(srcs)
    parts = [ALL] * n if parts is None else parts
    lands = [None] * n if lands is None else lands
    given = [i for i in range(n) if lands[i] is not None]

    def rows(ref, i):
        hr = srcs[i].shape[1]
        r0, r1 = hr * parts[i][0] // parts[i][2], hr * parts[i][1] // parts[i][2]
        return ref.at[pl.ds(r0, r1 - r0), :]

    def copies(ins, outs, sems):
        x, y, c, chips = _place()
        me = 2 * x + y
        return [_rcopy(rows(ins[i].at[2 * chip[0] + chip[1]] if srcs[i].shape[0] == NCHIP else ins[i].at[c], i),
                       rows(outs[i].at[me], i), sems[0].at[3 * i + j], sems[1].at[3 * i + j], (*chip, c))
                for i in range(n) for j, chip in enumerate(chips)]

    def start(ins, outs, sems):
        for cp in copies(ins, outs, sems):
            cp.start()

    def finish(ins, outs, sems):
        for cp in copies(ins, outs, sems):
            cp.wait()

    shapes = [jax.ShapeDtypeStruct((NCHIP,) + s.shape[1:], s.dtype) for s in srcs]
    alias = {n + q: i for q, i in enumerate(given)}
    return _Stage(list(srcs) + [lands[i] for i in given], shapes, alias, _sem2(n, 3), start, finish)


HBM_REF = pl.BlockSpec(memory_space=pltpu.HBM)
SEM_REF = pl.BlockSpec(memory_space=pltpu.SEMAPHORE)
DATAFLOW = pltpu.SideEffectType.DATAFLOW_SIDE_EFFECTING


def _after(x):
    return _Stage([x], [], {}, [], lambda *a: None, lambda *a: None)


class _Flight:
    def __init__(self, stage, sems, bufs):
        self.stage, self.sems, self.bufs = stage, list(sems), list(bufs)

    def landed(self):
        st, n = self.stage, len(self.stage.operands)
        fresh = [j for j in range(len(st.out_shape)) if j not in st.alias.values()]
        back = {v: k for k, v in st.alias.items()}
        return [self.bufs[back[j]] if j in back else self.bufs[n + fresh.index(j)] for j in range(len(st.out_shape))]


def _split_call(name, finish=(), start=(), after=None):
    bufs, stage_bufs = [], []

    def slot(a):
        for i, b in enumerate(bufs):
            if b is a:
                return i
        bufs.append(a)
        return len(bufs) - 1

    fin_slots = [[slot(b) for b in fl.bufs] for fl in finish]
    for st in start:
        fresh = [lax.empty(o.shape, o.dtype) for j, o in enumerate(st.out_shape) if j not in st.alias.values()]
        stage_bufs.append([slot(a) for a in list(st.operands) + fresh])
    old_sems = [s for fl in finish for s in fl.sems]
    new_sems = [s for st in start for s in st.sems]
    nb, no, nn = len(bufs), len(old_sems), len(new_sems)

    def refs_of(st, slots, buf_refs):
        n = len(st.operands)
        ins = [buf_refs[i] for i in slots[:n]]
        fresh = [j for j in range(len(st.out_shape)) if j not in st.alias.values()]
        back = {v: k for k, v in st.alias.items()}
        outs = [ins[back[j]] if j in back else buf_refs[slots[n + fresh.index(j)]] for j in range(len(st.out_shape))]
        return ins, outs

    def body(*refs):
        buf_refs, sem_in = refs[:nb], refs[nb:nb + no]
        sem_out = refs[nb + no + (after is not None):][:nn]
        token = refs[-1]
        pos = 0
        for fl, slots in zip(finish, fin_slots):
            ins, outs = refs_of(fl.stage, slots, buf_refs)
            fl.stage.finish(ins, outs, sem_in[pos:pos + len(fl.sems)])
            pos += len(fl.sems)
        pos = 0
        for st, slots in zip(start, stage_bufs):
            ins, outs = refs_of(st, slots, buf_refs)
            st.start(ins, outs, sem_out[pos:pos + len(st.sems)])
            pos += len(st.sems)
        token[...] = jnp.zeros_like(token)

    res = pl.pallas_call(
        body, name=name,
        out_shape=tuple(new_sems) + tuple(pltpu.HBM(b.shape, b.dtype) for b in bufs) +
                  (jax.ShapeDtypeStruct((8, LANE), F32),),
        in_specs=(HBM_REF,) * nb + (SEM_REF,) * no + ((pl.BlockSpec(memory_space=pl.ANY),) if after is not None else ()),
        out_specs=(SEM_REF,) * nn + (HBM_REF,) * nb + (pl.BlockSpec(memory_space=pltpu.VMEM),),
        input_output_aliases={i: nn + i for i in range(nb)},
        compiler_params=pltpu.CompilerParams(has_side_effects=DATAFLOW),
    )(*_hbm(*bufs), *old_sems, *([after] if after is not None else []))
    sems, thru, token = res[:nn], res[nn:nn + nb], res[-1]
    for fl, slots in zip(finish, fin_slots):
        fl.bufs = [thru[i] for i in slots]
    flights, pos = [], 0
    for st, slots in zip(start, stage_bufs):
        flights.append(_Flight(st, sems[pos:pos + len(st.sems)], [thru[i] for i in slots]))
        pos += len(st.sems)
    return flights, token


def _last_copies(p_ref, land_ref, ssem, rsem):
    x, y, c, chips = _place()
    me = 2 * x + y
    send = [_rcopy(p_ref.at[2 * chip[0] + chip[1]], land_ref.at[me], ssem.at[j], rsem.at[j], (*chip, c))
            for j, chip in enumerate(chips)]
    recv = [_rcopy(p_ref.at[2 * chip[0] + chip[1]], land_ref.at[2 * chip[0] + chip[1]], ssem.at[j], rsem.at[j],
                   (*chip, c)) for j, chip in enumerate(chips)]
    return send, recv


def _chips_start(p):
    def body(p_ref, land_ref, ssem, rsem, p_thru, land_thru, token):
        for cp in _last_copies(p_ref, land_ref, ssem, rsem)[0]:
            cp.start()
        token[...] = jnp.zeros_like(token)

    return pl.pallas_call(
        body, name="reduce_last_start",
        out_shape=(pltpu.SemaphoreType.DMA((3,)), pltpu.SemaphoreType.DMA((3,)), pltpu.HBM(p.shape, p.dtype),
                   pltpu.HBM(p.shape, p.dtype), jax.ShapeDtypeStruct((8, LANE), F32)),
        in_specs=(HBM_REF, HBM_REF),
        out_specs=(SEM_REF, SEM_REF, HBM_REF, HBM_REF, pl.BlockSpec(memory_space=pltpu.VMEM)),
        input_output_aliases={0: 2, 1: 3},
        compiler_params=pltpu.CompilerParams(has_side_effects=DATAFLOW),
    )(*_hbm(p, lax.empty(p.shape, p.dtype)))


def _chips_wait(ssem, rsem, p_thru, land_thru, after):
    def body(p_ref, land_ref, ssem, rsem, after_ref, p_dead, got_ref):
        send, recv = _last_copies(p_ref, land_ref, ssem, rsem)
        for cp in send:
            cp.wait_send()
        for cp in recv:
            cp.wait_recv()

    return pl.pallas_call(
        body, name="reduce_last_wait",
        out_shape=(pltpu.HBM(p_thru.shape, p_thru.dtype), pltpu.HBM(land_thru.shape, land_thru.dtype)),
        in_specs=(HBM_REF, HBM_REF, SEM_REF, SEM_REF, pl.BlockSpec(memory_space=pl.ANY)),
        out_specs=(HBM_REF, HBM_REF), input_output_aliases={0: 0, 1: 1},
        compiler_params=pltpu.CompilerParams(has_side_effects=DATAFLOW),
    )(p_thru, land_thru, ssem, rsem, after)


def _share(pairs):
    n = len(pairs)

    def start(ins, outs, sems):
        x, y, c, _ = _place()
        for i in range(n):
            _rcopy(outs[i].at[c], outs[i].at[c], sems[0].at[i], sems[1].at[i], (x, y, 1 - c)).start()

    def finish(ins, outs, sems):
        x, y, c, _ = _place()
        for i in range(n):
            _rcopy(outs[i].at[c], outs[i].at[c], sems[0].at[i], sems[1].at[i], (x, y, 1 - c)).wait_send()
            _rcopy(outs[i].at[1 - c], outs[i].at[1 - c], sems[0].at[i], sems[1].at[i], (x, y, 1 - c)).wait_recv()

    return _Stage(pairs, [_sds(p) for p in pairs], {i: i for i in range(n)},
                  [pltpu.SemaphoreType.DMA((n,)), pltpu.SemaphoreType.DMA((n,))], start, finish)


def _row_block(rows, cols, itemsize=4, target=2 * MIB):
    br = rows
    while br * cols * itemsize > target and br % 32 == 0:
        br //= 2
    return br


def _cast_place(w, chip_idx, name):
    rows, cols = w.shape
    br = _row_block(rows, cols)

    def body(k_ref, w_ref, o_ref):
        o_ref[0] = w_ref[...].astype(BF)

    return _call(
        body, name=name, grid=(rows // br,), prefetch=chip_idx,
        in_specs=[pl.BlockSpec((br, cols), lambda r, k: (r, 0))],
        out_specs=[pl.BlockSpec((1, br, cols), lambda r, k: (k[0], r, 0))],
        out_shape=[jax.ShapeDtypeStruct((NCHIP, rows, cols), BF)], vmem=32, args=[w])[0][0]


def _cast_place_multi(ws, chip_idx, stages=()):
    br = 128
    nblk = [a.shape[0] // br for a in ws]
    starts = [sum(nblk[:i]) for i in range(len(ws))]

    def body(k_ref, *refs):
        r = pl.program_id(0)
        for i in range(len(ws)):
            @pl.when(jnp.logical_and(r >= starts[i], r < starts[i] + nblk[i]))
            def _(i=i):
                refs[len(ws) + i][0] = refs[i][...].astype(BF)

    def at(i):
        return functools.partial(lambda r, s, nb: jnp.clip(r - s, 0, nb - 1), s=starts[i], nb=nblk[i])

    outs, landed = _call(
        body, name="cast_rest", grid=(sum(nblk),), prefetch=chip_idx,
        in_specs=[pl.BlockSpec((br, a.shape[1]), functools.partial(lambda r, k, f: (f(r), 0), f=at(i)))
                  for i, a in enumerate(ws)],
        out_specs=[pl.BlockSpec((1, br, a.shape[1]), functools.partial(lambda r, k, f: (k[0], f(r), 0), f=at(i)))
                   for i, a in enumerate(ws)],
        out_shape=[jax.ShapeDtypeStruct((NCHIP,) + a.shape, BF) for a in ws], vmem=32, args=list(ws), stages=stages)
    return outs, landed


def _add_sibling(g, land, cidx, name, stages=()):
    _, _, hr, cols = g.shape
    br = _row_block(hr, cols)

    def body(c_ref, g_ref, l_ref, o_ref):
        o_ref[...] = (g_ref[0, 0].astype(F32) + l_ref[0].astype(F32)).astype(BF)[None]

    outs, st = _call(
        body, name=name, grid=(NCHIP, hr // br), prefetch=cidx,
        in_specs=[pl.BlockSpec((1, 1, br, cols), lambda k, r, c: (k, c[0], r, 0)),
                  pl.BlockSpec((1, br, cols), lambda k, r, c: (k, r, 0))],
        out_specs=[pl.BlockSpec((1, br, cols), lambda k, r, c: (k, r, 0))],
        out_shape=[jax.ShapeDtypeStruct((NCHIP, hr, cols), BF)], vmem=32, args=[g, land], stages=stages)
    return outs[0], st


def _add_sibling_multi(gs, lands, cidx, name):
    n = len(gs)
    brs = [_row_block(g.shape[2], g.shape[3]) for g in gs]
    nrb = [g.shape[2] // b for g, b in zip(gs, brs)]
    nblk = [NCHIP * q for q in nrb]
    starts = [sum(nblk[:i]) for i in range(n)]

    def body(c_ref, *refs):
        r = pl.program_id(0)
        for i in range(n):
            g_ref, l_ref, o_ref = refs[2 * i], refs[2 * i + 1], refs[2 * n + i]

            @pl.when(jnp.logical_and(r >= starts[i], r < starts[i] + nblk[i]))
            def _():
                o_ref[...] = (g_ref[0, 0].astype(F32) + l_ref[0].astype(F32)).astype(BF)[None]

    def at(i, r):
        q = jnp.clip(r - starts[i], 0, nblk[i] - 1)
        return q // nrb[i], q % nrb[i]

    def g_spec(i):
        return pl.BlockSpec((1, 1, brs[i], gs[i].shape[3]),
                            functools.partial(lambda r, c, i: (at(i, r)[0], c[0], at(i, r)[1], 0), i=i))

    def l_spec(i):
        return pl.BlockSpec((1, brs[i], gs[i].shape[3]),
                            functools.partial(lambda r, c, i: (at(i, r)[0], at(i, r)[1], 0), i=i))

    return _call(
        body, name=name, grid=(sum(nblk),), prefetch=cidx,
        in_specs=[s for i in range(n) for s in (g_spec(i), l_spec(i))], out_specs=[l_spec(i) for i in range(n)],
        out_shape=[jax.ShapeDtypeStruct(l.shape, BF) for l in lands], vmem=32,
        args=[a for i in range(n) for a in (gs[i], lands[i])])[0]


def _add_pair(a, b, name):
    rows, cols = a.shape

    def body(a_ref, b_ref, o_ref):
        o_ref[...] = a_ref[...] + b_ref[...]

    spec = pl.BlockSpec((rows, cols), lambda r: (0, 0))
    return _call(body, name=name, grid=(1,), in_specs=[spec, spec], out_specs=[spec], out_shape=[_sds(a)],
                 vmem=32, args=[a, b])[0][0]


def _add_chips(own, land, idx, name, stages=None):
    _, hr, cols = land.shape
    br = _row_block(hr, cols)

    def body(s_ref, a_ref, b_ref, c_ref, d_ref, o_ref):
        o_ref[...] = (a_ref[...].astype(F32) + b_ref[...].astype(F32)) + (c_ref[...].astype(F32) +
                                                                           d_ref[...].astype(F32))

    spec = lambda q: pl.BlockSpec((1, br, cols), functools.partial(lambda r, s, q: (s[q], r, 0), q=q))
    outs, landed = _call(
        body, name=name, grid=(hr // br,), prefetch=idx,
        in_specs=[spec(0), spec(1), spec(2), spec(3)], out_specs=[spec(4)],
        out_shape=[jax.ShapeDtypeStruct((2, hr, cols), F32)], vmem=48, args=[own, land, land, land],
        stages=stages or ())
    return outs[0] if stages is None else (outs[0], landed)


def _add_chips_multi(owns, lands, idx, name, stages=()):
    n = len(owns)
    brs = [_row_block(l.shape[1], l.shape[2]) for l in lands]
    nblk = [l.shape[1] // b for l, b in zip(lands, brs)]
    starts = [sum(nblk[:i]) for i in range(n)]

    def body(s_ref, *refs):
        r = pl.program_id(0)
        for i in range(n):
            a_ref, b_ref, c_ref, d_ref = refs[4 * i:4 * i + 4]
            o_ref = refs[4 * n + i]

            @pl.when(jnp.logical_and(r >= starts[i], r < starts[i] + nblk[i]))
            def _():
                o_ref[...] = (a_ref[...].astype(F32) + b_ref[...].astype(F32)) + (c_ref[...].astype(F32) +
                                                                                   d_ref[...].astype(F32))

    def spec(i, q):
        return pl.BlockSpec((1, brs[i], lands[i].shape[2]), functools.partial(
            lambda r, s, q, st, nb: (s[q], jnp.clip(r - st, 0, nb - 1), 0), q=q, st=starts[i], nb=nblk[i]))

    outs, landed = _call(
        body, name=name, grid=(sum(nblk),), prefetch=idx,
        in_specs=[spec(i, q) for i in range(n) for q in range(4)], out_specs=[spec(i, 4) for i in range(n)],
        out_shape=[jax.ShapeDtypeStruct((2,) + l.shape[1:], F32) for l in lands], vmem=48,
        args=[a for i in range(n) for a in (owns[i], lands[i], lands[i], lands[i])], stages=stages)
    return outs, landed


def _adamw_math(w, g, m, v):
    mn = ADAM_B1 * m + (1.0 - ADAM_B1) * g
    vn = ADAM_B2 * v + (1.0 - ADAM_B2) * (g * g)
    m_hat = mn / (1.0 - ADAM_B1 ** ADAM_STEP)
    v_hat = vn / (1.0 - ADAM_B2 ** ADAM_STEP)
    return -ADAM_LR * (m_hat / (jnp.sqrt(v_hat) + ADAM_EPS) + ADAM_WD * w), mn, vn


def _adamw(w, g, m, v, name, stages=()):
    rows, cols = w.shape
    br = _row_block(rows, cols)

    def body(w_ref, g_ref, m_ref, v_ref, go_ref, d_ref, mo_ref, vo_ref):
        gv = g_ref[...]
        go_ref[...] = gv
        d_ref[...], mo_ref[...], vo_ref[...] = _adamw_math(w_ref[...], gv, m_ref[...], v_ref[...])

    spec = pl.BlockSpec((br, cols), lambda r: (r, 0))
    return _call(body, name=name, grid=(rows // br,), in_specs=[spec] * 4, out_specs=[spec] * 4,
                 out_shape=[_sds(w)] * 4, vmem=56, args=[w, g, m, v], stages=stages)


def _adamw_multi(names, w, g, m, v, stages=()):
    cols = w[names[0]].shape[1]
    br = 128
    nblk = [w[n].shape[0] // br for n in names]
    starts = [sum(nblk[:i]) for i in range(len(names))]

    def body(*refs):
        r = pl.program_id(0)
        for i in range(len(names)):
            w_ref, g_ref, m_ref, v_ref = refs[4 * i:4 * i + 4]
            go_ref, d_ref, mo_ref, vo_ref = refs[4 * len(names) + 4 * i:4 * len(names) + 4 * i + 4]

            @pl.when(jnp.logical_and(r >= starts[i], r < starts[i] + nblk[i]))
            def _():
                gv = g_ref[...]
                go_ref[...] = gv
                d_ref[...], mo_ref[...], vo_ref[...] = _adamw_math(w_ref[...], gv, m_ref[...], v_ref[...])

    def spec(i):
        return pl.BlockSpec((br, cols), functools.partial(
            lambda r, s, nb: (jnp.clip(r - s, 0, nb - 1), 0), s=starts[i], nb=nblk[i]))

    outs, landed = _call(
        body, name="adamw_" + "_".join(names), grid=(sum(nblk),),
        in_specs=[spec(i) for i in range(len(names)) for _ in range(4)],
        out_specs=[spec(i) for i in range(len(names)) for _ in range(4)],
        out_shape=[_sds(w[n]) for n in names for _ in range(4)], vmem=56,
        args=[a[n] for n in names for a in (w, g, m, v)], stages=stages)
    return {n: outs[4 * i:4 * i + 4] for i, n in enumerate(names)}, landed


def _to_everyone(v):
    deltas = [(a, b, e) for a in (0, 1) for b in (0, 1) for e in (0, 1)][1:]

    def copies(ins, outs, sems):
        x, y, c, _ = _place()
        me = 4 * x + 2 * y + c
        flip = lambda p, f: 1 - p if f else p
        return [_rcopy(ins[0], outs[0].at[me], sems[0].at[q], sems[1].at[q], (flip(x, a), flip(y, b), flip(c, e)))
                for q, (a, b, e) in enumerate(deltas)]

    def start(ins, outs, sems):
        for cp in copies(ins, outs, sems):
            cp.start()

    def finish(ins, outs, sems):
        for cp in copies(ins, outs, sems):
            cp.wait()

    n = len(deltas)
    return _Stage([v], [jax.ShapeDtypeStruct((2 * NCHIP,) + v.shape, v.dtype)], {},
                  [pltpu.SemaphoreType.DMA((n,)), pltpu.SemaphoreType.DMA((n,))], start, finish)


SMALL_AT = {"norm_mix_pre": (0, 1, D), "norm_mix_post": (1, 1, D), "norm_mlp_pre": (2, 1, D),
            "norm_mlp_post": (3, 1, D), "b_gate": (4, 2, D), "conv_b": (6, 1, D), "lru_b_a": (7, 1, D),
            "lru_b_x": (8, 1, D), "lru_lambda": (9, 1, D), "pool_scale": (10, 1, DP)}
SMALL_SEPARATE = ["conv_w", "lru_w_a", "lru_w_x", "pool_w"]


def _adamw_small(small_sum, first_all, sep_grads, w, m, v):
    packed, sep = list(SMALL_AT), list(SMALL_SEPARATE)
    names = packed + sep

    def body(*refs):
        s_ref, a_ref, refs = refs[0], refs[1], refs[2:]
        g_sep, refs = refs[:len(sep)], refs[len(sep):]
        nn = len(names)
        w_r, m_r, v_r, refs = refs[:nn], refs[nn:2 * nn], refs[2 * nn:3 * nn], refs[3 * nn:]
        g_out, refs = refs[:len(packed)], refs[len(packed):]
        d_o, m_o, v_o = refs[:nn], refs[nn:2 * nn], refs[2 * nn:3 * nn]
        for i, n in enumerate(names):
            if i == 0:
                g = a_ref[0:1, :]
                for q in range(1, 2 * NCHIP):
                    g = g + a_ref[q:q + 1, :]
                g_out[i][...] = g
            elif n in SMALL_AT:
                r0, nr, nc = SMALL_AT[n]
                g = jnp.concatenate([s_ref[r0 + q:r0 + q + 1, :nc] for q in range(nr)], axis=1)
                g_out[i][...] = g
            else:
                g = g_sep[i - len(packed)][...]
            d_o[i][...], m_o[i][...], v_o[i][...] = _adamw_math(w_r[i][...], g, m_r[i][...], v_r[i][...])

    ws = [w[n] for n in names]
    res = pl.pallas_call(
        body, name="adamw_small",
        out_shape=[_sds(w[n]) for n in packed] + [_sds(a) for a in ws] * 3,
        compiler_params=_cp(32),
    )(*_hbm(small_sum, first_all, *sep_grads, *ws, *[m[n] for n in names], *[v[n] for n in names]))
    nn, npk = len(names), len(packed)
    grad = dict(zip(packed, res[:npk]))
    delta = dict(zip(names, res[npk:npk + nn]))
    new_m = dict(zip(names, res[npk + nn:npk + 2 * nn]))
    new_v = dict(zip(names, res[npk + 2 * nn:]))
    return grad, delta, new_m, new_v


W_NAMES = ["norm_mix_pre", "norm_mix_post", "norm_mlp_pre", "norm_mlp_post", "w_in", "b_gate", "conv_w", "conv_b",
           "lru_w_a", "lru_b_a", "lru_w_x", "lru_b_x", "lru_lambda", "pool_w", "pool_scale", "w_lru_up",
           "w_pool_up", "w_o", "w_ff1", "w_ff2"]
BIG = ["w_in", "w_lru_up", "w_pool_up", "w_o", "w_ff1", "w_ff2"]


def _block_diag(w):
    hd = w.shape[-1]
    per = CB // hd
    w4 = w.reshape(NG, per, hd, hd)
    eye = jnp.eye(per, dtype=w.dtype)
    return jnp.einsum("gpij,pq->gpiqj", w4, eye).reshape(NG, CB, CB)


def _block_diag_extract(d, hd):
    per = CB // hd
    d5 = d.reshape(NG, per, hd, per, hd)
    return jnp.stack([d5[:, p, :, p, :] for p in range(per)], axis=1).reshape(NG * per, hd, hd)


def _halves(g):
    return g.reshape(NCHIP, 2, g.size // (g.shape[-1] * 2 * NCHIP), g.shape[-1])


def kernel(x, norm_mix_pre, norm_mix_post, norm_mlp_pre, norm_mlp_post, w_in, b_gate, conv_w, conv_b, lru_w_a, lru_b_a, lru_w_x, lru_b_x, lru_lambda, pool_w, pool_scale, w_lru_up, w_pool_up, w_o, w_ff1, w_ff2, loss_target, m_norm_mix_pre, m_norm_mix_post, m_norm_mlp_pre, m_norm_mlp_post, m_w_in, m_b_gate, m_conv_w, m_conv_b, m_lru_w_a, m_lru_b_a, m_lru_w_x, m_lru_b_x, m_lru_lambda, m_pool_w, m_pool_scale, m_w_lru_up, m_w_pool_up, m_w_o, m_w_ff1, m_w_ff2, v_norm_mix_pre, v_norm_mix_post, v_norm_mlp_pre, v_norm_mlp_post, v_w_in, v_b_gate, v_conv_w, v_conv_b, v_lru_w_a, v_lru_b_a, v_lru_w_x, v_lru_b_x, v_lru_lambda, v_pool_w, v_pool_scale, v_w_lru_up, v_w_pool_up, v_w_o, v_w_ff1, v_w_ff2):
    args = dict(locals())
    two_d = lambda a: a.reshape(-1, a.shape[-1])
    w = {n: two_d(args[n]) for n in W_NAMES}
    mom = {n: two_d(args["m_" + n]) for n in W_NAMES}
    var = {n: two_d(args["v_" + n]) for n in W_NAMES}
    i32 = lambda val: jnp.asarray(val, jnp.int32)
    chip = i32(2 * lax.axis_index("x") + lax.axis_index("y"))
    core = i32(lax.axis_index("c"))
    cidx = core.reshape(1)
    zero = i32(0)
    hd = lru_w_a.shape[-1]
    xs, target = x[0], loss_target[0]
    g1, g2, g3, g4 = norm_mix_pre, norm_mix_post, norm_mlp_pre, norm_mlp_post

    mix = ["w_lru_up", "w_pool_up", "w_o"]
    full = {"w_in": _cast_place(w["w_in"], chip.reshape(1), "cast_w_in")}
    (fl_in, fl_conv), first = _split_call("gather_start_first", start=[
        _gather([full["w_in"]], ici=[(0, ALL)]), _gather_whole(w["conv_w"])])
    casts, _ = _cast_place_multi([w[n] for n in BIG[1:]], chip.reshape(1), stages=[_after(first)])
    full.update(zip(BIG[1:], casts))
    (fl_mix, fl_ff1, fl_ff2), started = _split_call("gather_start_rest", start=[
        _gather([full[n] for n in mix], ici=[(0, ALL), (1, ALL), (2, ALL)]),
        _gather([full["w_ff1"]], ici=[(0, ALL)]), _gather([full["w_ff2"]], ici=[(0, ALL)])])
    wa = _block_diag(lru_w_a[0]).astype(BF)
    wx = _block_diag(lru_w_x[0]).astype(BF)
    pw = pool_w[0].astype(BF)

    def to_sibling(name, flight, after=None):
        (fl,), passed = _split_call(name + "_pass", finish=[flight], after=after,
                                    start=[_gather(flight.landed(), d2d=[(i, ALL) for i in range(len(flight.bufs))])])
        passed_on.append(passed)
        return fl

    passed_on = []

    def arrived(name, flight, after=None):
        _split_call(name + "_done", finish=[flight], after=after)
        return flight.landed()

    idx_big = jnp.stack([chip, (chip + 1) % NCHIP, (chip + 2) % NCHIP, (chip + 3) % NCHIP, core])
    proj, h1 = _fwd_inproj_own(xs, g1, fl_in.bufs[0], idx_big, stages=[_after(started)])
    fl_in = to_sibling("gather_w_in", fl_in, after=h1)
    w_in_f, = arrived("gather_w_in", fl_in)
    conv_all, = arrived("gather_conv", fl_conv)
    full["w_in"] = w_in_f
    conv_all = lax.dynamic_update_slice(conv_all, w["conv_w"][None], (chip, zero, zero))
    conv_full = jnp.transpose(conv_all, (1, 0, 2)).reshape(4, DR)
    proj = _fwd_inproj_rest(h1, w_in_f, proj, idx_big)
    fl_mix = to_sibling("gather_mix", fl_mix, after=proj)
    (ylru, hs), _ = _fwd_lru(proj, conv_full, conv_b, wa, lru_b_a, wx, lru_b_x, lru_lambda,
                             stages=[_after(passed_on[-1])])
    got = arrived("gather_mix", fl_mix, after=ylru)
    fl_ff1 = to_sibling("gather_ff1", fl_ff1, after=ylru)
    w_lru_up_f, w_pool_up_f, w_o_f = got[0].reshape(DR, D), got[1], got[2].reshape(D, D)
    ypool = _fwd_pool(proj, pw, pool_scale)
    (x2, h2, m, mrg, bra, brb), _ = _fwd_merge(xs, ylru, ypool, proj, b_gate, g2, g3, w_lru_up_f, w_pool_up_f, w_o_f,
                                               stages=[_after(passed_on[-1])])
    fl_ff2 = to_sibling("gather_ff2", fl_ff2, after=h2)
    ff1, = arrived("gather_ff1", fl_ff1, after=h2)
    ff2, = arrived("gather_ff2", fl_ff2)
    ff2 = ff2.reshape(DF, D)
    a1, f = _fwd_mlp(h2, ff1, ff2)
    lossp, dy, df, dg4 = _loss_head(f, x2, target, g4)

    dh2, df1 = _bwd_mlp_x(df, a1, ff1, ff2)
    dw_ff1, dw_ff2 = _bwd_mlp_w(df, h2, a1, df1)
    g_ff = [_halves(dw_ff1), _halves(dw_ff2)]
    (dxres, dgates, dylru, dypool, dm, dbra, dbrb, dg2, dg3, dbg), (l_ff,) = _bwd_merge(
        dh2, dy, x2, m, bra, brb, proj, b_gate, g2, g3, w_lru_up_f, w_pool_up_f, w_o_f, stages=[_to_sibling(g_ff)])
    p_ff = _add_sibling_multi(g_ff, l_ff, cidx, "add_sibling_ff")
    (fl_ff,), sent_ff = _split_call("reduce_ff_start", start=[_to_chips(p_ff)])
    (dw_o, dw_lru_up, dw_pool_up), _ = _dw_merge(mrg, dm, ylru, dbra, ypool, dbrb, stages=[_after(sent_ff)])
    g_mix = [_halves(dw_lru_up), _halves(dw_pool_up), _halves(dw_o)]
    (dxp, dgl, dcw, dcb, dwa, dba, dwx, dbx, dlam), (l_mix,) = _bwd_lru(
        proj, hs, dylru, conv_full, conv_b, wa, lru_b_a, wx, lru_b_x, lru_lambda, stages=[_to_sibling(g_mix)])
    p_mix = _add_sibling_multi(g_mix, l_mix, cidx, "add_sibling_mix")
    dxpool, dpw, dsc = _bwd_pool(proj, dypool, pw, pool_scale)
    dproj = jnp.concatenate([dxp, dgl, dxpool, dgates], axis=1)
    small = jnp.concatenate([
        jnp.zeros((1, D), F32), dg2, dg3, dg4, dbg.reshape(2, D), dcb, dba, dbx, dlam,
        jnp.pad(dsc, ((0, 0), (0, D - DP))), jnp.pad(lossp, ((0, 0), (0, D - 1))), dcw,
        _block_diag_extract(dwa, hd).reshape(-1, D), _block_diag_extract(dwx, hd).reshape(-1, D),
        dpw.reshape(-1, D)], axis=0)
    (fl_mixr, fl_smalls), sent_mix = _split_call("reduce_mix_start", start=[_to_chips(p_mix), _to_sibling([small])])
    dw_in, _ = _bwd_inproj_w(h1, dproj, stages=[_after(sent_mix)])
    _split_call("reduce_small_sibling_done", finish=[fl_smalls], after=dw_in)
    small, l_small = fl_smalls.bufs
    small2 = _add_pair(small, l_small, "add_sibling_small").reshape(2, SMALL_ROWS // 2, D)
    g_in = _halves(dw_in)
    done = ["w_ff1", "w_ff2"] + mix
    (fl_gin, fl_small), sib_started = _split_call("reduce_in_sibling_start",
                                                  start=[_to_sibling([g_in]), _to_chips([small2])])
    _, chips_done = _split_call("reduce_chips_done", finish=[fl_ff, fl_mixr], after=sib_started)
    p_ff1, p_ff2, c_ff1, c_ff2 = fl_ff.bufs
    p_mix, c_mix = fl_mixr.bufs[:3], fl_mixr.bufs[3:]
    pairs, _ = _add_chips_multi([p_ff1, p_ff2] + p_mix, [c_ff1, c_ff2] + c_mix, idx_big, "add_chips_done",
                                stages=[_after(chips_done)])
    _split_call("reduce_in_sibling_done", finish=[fl_gin], after=pairs[-1])
    g_in, l_in = fl_gin.bufs
    p_in = _add_sibling(g_in, l_in, cidx, "add_sibling_w_in")[0]
    ssem, rsem, p_in, c_in, token = _chips_start(p_in)
    _split_call("reduce_small_done", finish=[fl_small], after=token)
    small2, c_small = fl_small.bufs
    own_small = lax.dynamic_index_in_dim(small2, core, 0, keepdims=True)
    c_small = lax.dynamic_update_slice(c_small, own_small, (chip, zero, zero))
    pair_small = _add_chips(c_small, c_small, jnp.stack([zero, zero + 1, zero + 2, zero + 3, core]), "add_chips_small")
    (fl_share,), shared_start = _split_call("reduce_share_start", start=[_share(pairs + [pair_small])])
    grad_x, dg1 = _bwd_inproj_x(dproj, full["w_in"], xs, dxres, g1, stages=[_after(shared_start)])
    _split_call("reduce_share_done", finish=[fl_share], after=dg1)
    shared = fl_share.landed()
    pairs, pair_small = shared[:-1], shared[-1]

    grads, delta, new_m, new_v = {}, {}, {}, {}
    for n, p in zip(done, pairs):
        grads[n] = p.reshape(-1, p.shape[-1])

    def update(n, stages=()):
        (grads[n], delta[n], new_m[n], new_v[n]), landed = _adamw(w[n], grads[n], mom[n], var[n], "adamw_" + n,
                                                                  stages=stages)
        return landed

    p_in, c_in = _chips_wait(ssem, rsem, p_in, c_in, pairs[0])
    pair_in = _add_chips(p_in, c_in, idx_big, "add_chips_w_in")
    (fl_last, fl_dg1), last_start = _split_call("reduce_last_share_start", start=[_share([pair_in]), _to_everyone(dg1)])
    updated, _ = _adamw_multi(["w_ff1", "w_ff2", "w_o", "w_lru_up"], w, grads, mom, var, stages=[_after(last_start)])
    for n, (go, d, mo, vo) in updated.items():
        grads[n], delta[n], new_m[n], new_v[n] = go, d, mo, vo
    _split_call("reduce_last_share_done", finish=[fl_last, fl_dg1], after=new_v["w_lru_up"])
    (pair_in,), (dg1, dg1_all) = fl_last.landed(), fl_dg1.bufs
    dg1_all = lax.dynamic_update_slice(dg1_all, dg1[None], (2 * chip + core, zero, zero)).reshape(2 * NCHIP, D)
    grads["w_in"] = pair_in.reshape(-1, pair_in.shape[-1])
    update("w_pool_up")
    update("w_in")
    small_sum = pair_small.reshape(SMALL_ROWS, D)
    loss = 0.5 * small_sum[LOSS_ROW, 0]
    ccols = DR // NCHIP
    sep = [lax.dynamic_slice(small_sum[12:16], (zero, chip * ccols), (4, ccols)),
           small_sum[16:80].reshape(-1, hd), small_sum[80:144].reshape(-1, hd), small_sum[144:208].reshape(-1, PG)]
    g_s, d_s, m_s, v_s = _adamw_small(small_sum, dg1_all, sep, w, mom, var)
    grads.update(g_s)
    grads.update(dict(zip(SMALL_SEPARATE, sep)))
    delta.update(d_s)
    new_m.update(m_s)
    new_v.update(v_s)

    out = lambda d: [d[n].reshape(args[n].shape) for n in W_NAMES]
    return (loss, grad_x[None], *out(grads), *out(delta), *out(new_m), *out(new_v))
```

```python
import functools
import math

import jax
import jax.numpy as jnp
from jax import lax
from jax.experimental import pallas as pl
from jax.experimental.pallas import tpu as pltpu

F32 = jnp.float32
BF = jnp.bfloat16

T = 2048
D = 1024
DR = 1024
DP = 512
DF = 4096
DIN = 4608
NCHIP = 4
CW_IN = DIN // NCHIP
LANE = 128
CB = 128
NG = DR // CB
PG = 128
POOL_WINDOWS = (2, 4, 8, 16)
NORM_EPS = 1e-6
LRU_C = 8.0
GELU_C = math.sqrt(2.0 / math.pi)
ADAM_LR = 0.001
ADAM_B1 = 0.9
ADAM_B2 = 0.999
ADAM_EPS = 1e-08
ADAM_WD = 0.01
ADAM_STEP = 10
MESH_ID = pl.DeviceIdType.MESH
ANY = pl.BlockSpec(memory_space=pl.ANY)
SMALL_ROWS = 208
LOSS_ROW = 11
MIB = 1 << 20


def _cp(vmem_mib=None):
    if vmem_mib is None:
        return pltpu.CompilerParams()
    return pltpu.CompilerParams(vmem_limit_bytes=vmem_mib * MIB)


def _hbm(*arrays):
    return [pltpu.with_memory_space_constraint(a, pltpu.HBM) for a in arrays]


def _hbm_out(shapes):
    return [pltpu.HBM(s.shape, s.dtype) for s in shapes]


class _Stage:
    def __init__(self, operands, out_shape, alias, sems, start, finish):
        self.operands, self.out_shape, self.alias, self.sems = list(operands), list(out_shape), dict(alias), list(sems)
        self.start, self.finish = start, finish


def _call(body, *, name, grid, in_specs, out_specs, out_shape, args, vmem=None, stages=(), prefetch=None,
          scratch=()):
    nin, nout = len(in_specs), len(out_specs)
    npre = 0 if prefetch is None else 1
    st_args, st_shapes, st_sems, aliases = [], [], list(scratch), {}
    for st in stages:
        for k, v in st.alias.items():
            aliases[npre + nin + len(st_args) + k] = nout + len(st_shapes) + v
        st_args += st.operands
        st_shapes += st.out_shape
        st_sems += st.sems

    def wrapped(*refs):
        pre, refs = refs[:npre], refs[npre:]
        ins, pos = refs[:nin], nin
        st_ins = []
        for st in stages:
            st_ins.append(refs[pos:pos + len(st.operands)])
            pos += len(st.operands)
        outs, pos = refs[pos:pos + nout], pos + nout
        st_outs = []
        for st in stages:
            st_outs.append(refs[pos:pos + len(st.out_shape)])
            pos += len(st.out_shape)
        work, pos = refs[pos:pos + len(scratch)], pos + len(scratch)
        sems = []
        for st in stages:
            sems.append(refs[pos:pos + len(st.sems)])
            pos += len(st.sems)
        if stages:
            first = functools.reduce(jnp.logical_and, [pl.program_id(a) == 0 for a in range(len(grid))])

            @pl.when(first)
            def _():
                for st, a, b, s in zip(stages, st_ins, st_outs, sems):
                    st.start(a, b, s)

        body(*pre, *ins, *outs, *work)
        if stages:
            last = functools.reduce(jnp.logical_and, [pl.program_id(a) == g - 1 for a, g in enumerate(grid)])

            @pl.when(last)
            def _():
                for st, a, b, s in zip(stages, st_ins, st_outs, sems):
                    st.finish(a, b, s)

    all_in = list(in_specs) + [ANY] * len(st_args)
    all_out = list(out_specs) + [ANY] * len(st_shapes)
    kw = dict(has_side_effects=True) if stages else {}
    if vmem is not None:
        kw["vmem_limit_bytes"] = vmem * MIB
    if prefetch is None:
        gkw = dict(grid=grid, in_specs=all_in, out_specs=all_out, scratch_shapes=st_sems)
    else:
        gkw = dict(grid_spec=pltpu.PrefetchScalarGridSpec(
            num_scalar_prefetch=1, grid=grid, in_specs=all_in, out_specs=all_out, scratch_shapes=st_sems))
    res = pl.pallas_call(
        wrapped, name=name, out_shape=_hbm_out(list(out_shape) + st_shapes), input_output_aliases=aliases,
        compiler_params=pltpu.CompilerParams(**kw), **gkw,
    )(*([prefetch] if npre else []), *_hbm(*args, *st_args))
    outs, rest, st_res = list(res[:nout]), list(res[nout:]), []
    for st in stages:
        st_res.append(rest[:len(st.out_shape)])
        rest = rest[len(st.out_shape):]
    return outs, st_res


def _mm(a, b):
    return jnp.dot(a.astype(BF), b.astype(BF), preferred_element_type=F32)


def _mm_nt(a, b):
    return lax.dot_general(a.astype(BF), b.astype(BF), (((1,), (1,)), ((), ())),
                           preferred_element_type=F32)


def _mm_tn(a, b):
    return lax.dot_general(a.astype(BF), b.astype(BF), (((0,), (0,)), ((), ())),
                           preferred_element_type=F32)


def _rows(v):
    return lax.broadcasted_iota(jnp.int32, v.shape, 0)


def _sd(v, s, fill=0.0):
    return jnp.where(_rows(v) >= s, pltpu.roll(v, s, axis=0), fill)


def _su(v, s, fill=0.0):
    n = v.shape[0]
    return jnp.where(_rows(v) < n - s, pltpu.roll(v, n - s, axis=0), fill)


def _sigmoid(z):
    return 1.0 / (1.0 + jnp.exp(-z))


def _softplus(z):
    e = jnp.exp(-jnp.abs(z))
    u = 1.0 + e
    d = u - 1.0
    log1p = jnp.where(d == 0.0, e, jnp.log(u) * (e / jnp.where(d == 0.0, 1.0, d)))
    return jnp.maximum(z, 0.0) + log1p


def _mean(v):
    return jnp.mean(v, axis=-1, keepdims=True)


def _colsum(v):
    return jnp.sum(v, axis=0, keepdims=True)


def _acc(ref, val, first):
    @pl.when(first)
    def _():
        ref[...] = val

    @pl.when(jnp.logical_not(first))
    def _():
        ref[...] += val


def _conv(xp, cw, cb):
    x1, x2, x3 = _sd(xp, 1), _sd(xp, 2), _sd(xp, 3)
    xc = cb + cw[0:1] * x3 + cw[1:2] * x2 + cw[2:3] * x1 + cw[3:4] * xp
    return xc, x1, x2, x3


def _lru_gates(xc, wa, ba, wx, bx, lam):
    xcb = xc.astype(BF)
    r = _sigmoid(_mm(xcb, wa) + ba)
    ii = _sigmoid(_mm(xcb, wx) + bx)
    sp = _softplus(-lam)
    la = (-LRU_C) * r * sp
    a = jnp.exp(la)
    mult = jnp.sqrt(-jnp.tanh(la) * (a * a + 1.0))
    return xcb, r, ii, sp, a, mult


def _gelu_parts(g):
    th = jnp.tanh(GELU_C * (g + 0.044715 * (g * g * g)))
    gel = 0.5 * g * (1.0 + th)
    dgel = 0.5 * (1.0 + th) + 0.5 * g * (1.0 - th * th) * (GELU_C * (1.0 + 3.0 * 0.044715 * (g * g)))
    return gel, dgel


def _tile_scan(a, b, a_s, b_s, out_ref, reverse):
    n = a.shape[0]
    nt = n // 8
    sub = jnp.bitwise_and(_rows(a), 7)
    s = 1
    while s < 8:
        keep = sub < 8 - s if reverse else sub >= s
        amount = n - s if reverse else s
        b = b + a * jnp.where(keep, pltpu.roll(b, amount, axis=0), 0.0)
        a = a * jnp.where(keep, pltpu.roll(a, amount, axis=0), 1.0)
        s *= 2
    a_s[...] = a
    b_s[...] = b
    edge = pl.ds(0 if reverse else 7, nt, stride=8)
    ta, tb = a_s[edge, :], b_s[edge, :]
    shift = _su if reverse else _sd
    s = 1
    while s < nt:
        tb = tb + ta * shift(tb, s, 0.0)
        if 2 * s < nt:
            ta = ta * shift(ta, s, 1.0)
        s *= 2
    enters = shift(tb, 1, 0.0)
    for o in range(8):
        rows = pl.ds(o, nt, stride=8)
        out_ref[rows, :] = b_s[rows, :] + a_s[rows, :] * enters


def _pool_window(x, steps, shift):
    s, sh = x, 1
    for _ in range(steps):
        s = s + shift(s, sh)
        sh *= 2
    return s


def _fwd_inproj_own(x, g1, w_in, slots, stages=()):
    tm = 512

    def body(s_ref, x_ref, g_ref, w_ref, proj_ref, h_ref):
        xv = x_ref[...]
        r = lax.rsqrt(_mean(xv * xv) + NORM_EPS)
        h = ((xv * r) * g_ref[...]).astype(BF)
        h_ref[...] = h
        proj_ref[...] = jnp.dot(h, w_ref[0], preferred_element_type=F32)

    return _call(
        body, name="fwd_inproj_own", grid=(T // tm,), prefetch=slots,
        in_specs=[pl.BlockSpec((tm, D), lambda i, s: (i, 0)),
                  pl.BlockSpec((1, D), lambda i, s: (0, 0)),
                  pl.BlockSpec((1, D, CW_IN), lambda i, s: (s[0], 0, 0))],
        out_specs=[pl.BlockSpec((tm, CW_IN), lambda i, s: (i, s[0])),
                   pl.BlockSpec((tm, D), lambda i, s: (i, 0))],
        out_shape=[jax.ShapeDtypeStruct((T, DIN), F32), jax.ShapeDtypeStruct((T, D), BF)],
        vmem=40, args=[x, g1, w_in], stages=stages)[0]


def _fwd_inproj_rest(h1, w_in, proj, slots):
    tm = 512

    def body(s_ref, h_ref, w_ref, p_in, proj_ref):
        proj_ref[...] = jnp.dot(h_ref[...], w_ref[0], preferred_element_type=F32)

    res = pl.pallas_call(
        body, name="fwd_inproj_rest",
        grid_spec=pltpu.PrefetchScalarGridSpec(
            num_scalar_prefetch=1, grid=(T // tm, NCHIP - 1),
            in_specs=[pl.BlockSpec((tm, D), lambda i, k, s: (i, 0)),
                      pl.BlockSpec((1, D, CW_IN), lambda i, k, s: (s[1 + k], 0, 0)), ANY],
            out_specs=pl.BlockSpec((tm, CW_IN), lambda i, k, s: (i, s[1 + k]))),
        out_shape=pltpu.HBM((T, DIN), F32), input_output_aliases={3: 0},
        compiler_params=_cp(40),
    )(slots, *_hbm(h1, w_in, proj))
    return res


def _vec_spec():
    return pl.BlockSpec((1, CB), lambda j: (0, j))


def _fwd_lru(proj, conv_w, conv_b, wa, ba, wx, bx, lam, stages=()):
    def body(xp_ref, g_ref, cw_ref, cb_ref, wa_ref, ba_ref, wx_ref, bx_ref, lam_ref, y_ref, h_ref, a_s, b_s):
        xc, _, _, _ = _conv(xp_ref[...], cw_ref[...], cb_ref[...])
        _, _, ii, _, a, mult = _lru_gates(xc, wa_ref[0], ba_ref[...], wx_ref[0], bx_ref[...], lam_ref[...])
        _tile_scan(a, mult * (ii * xc), a_s, b_s, h_ref, reverse=False)
        gel, _ = _gelu_parts(g_ref[...])
        y_ref[...] = (h_ref[...] * gel).astype(BF)

    return _call(
        body, name="fwd_lru", grid=(NG,),
        in_specs=[pl.BlockSpec((T, CB), lambda j: (0, j)),
                  pl.BlockSpec((T, CB), lambda j: (0, NG + j)),
                  pl.BlockSpec((4, CB), lambda j: (0, j)),
                  _vec_spec(),
                  pl.BlockSpec((1, CB, CB), lambda j: (j, 0, 0)), _vec_spec(),
                  pl.BlockSpec((1, CB, CB), lambda j: (j, 0, 0)), _vec_spec(),
                  _vec_spec()],
        out_specs=[pl.BlockSpec((T, CB), lambda j: (0, j)), pl.BlockSpec((T, CB), lambda j: (0, j))],
        out_shape=[jax.ShapeDtypeStruct((T, DR), BF), jax.ShapeDtypeStruct((T, DR), F32)],
        vmem=48, args=[proj, proj, conv_w, conv_b, wa, ba, wx, bx, lam], stages=stages,
        scratch=[pltpu.VMEM((T, CB), F32)] * 2)


def _pool_cnt(w):
    t = lax.broadcasted_iota(jnp.int32, (T, 1), 0)
    return jnp.minimum(t + 1, w).astype(F32)


def _fwd_pool(proj, pool_w, pool_scale):
    def body(xp_ref, pw_ref, sc_ref, y_ref):
        for g, w in enumerate(POOL_WINDOWS):
            cols = slice(g * PG, (g + 1) * PG)
            x = xp_ref[:, cols]
            p = _pool_window(x, g + 1, _sd) / _pool_cnt(w) - x
            y_ref[:, cols] = (_mm(p, pw_ref[g]) * sc_ref[:, cols]).astype(BF)

    return pl.pallas_call(
        body, name="fwd_pool", grid=(1,),
        in_specs=[pl.BlockSpec((T, DP), lambda i: (0, 2 * DR // DP)),
                  pl.BlockSpec((4, PG, PG), lambda i: (0, 0, 0)),
                  pl.BlockSpec((1, DP), lambda i: (0, 0))],
        out_specs=pl.BlockSpec((T, DP), lambda i: (0, 0)),
        out_shape=pltpu.HBM((T, DP), BF),
        compiler_params=_cp(48),
    )(*_hbm(proj, pool_w, pool_scale))


GATE_BLK = 512
GATE_BLK0 = (2 * DR + DP) // GATE_BLK


def _gate_specs(tm):
    return [pl.BlockSpec((tm, GATE_BLK), functools.partial(lambda i, q: (i, GATE_BLK0 + q), q=q))
            for q in range(4)]


def _fwd_merge(x, ylru, ypool, proj, b_gate, g2, g3, w_lru_up, w_pool_up, w_o, stages=()):
    tm = 512

    def body(x_ref, yl_ref, yp_ref, p0, p1, p2, p3, bg_ref, g2_ref, g3_ref, wl_ref, wp_ref, wo_ref,
             x2_ref, h2_ref, m_ref, mrg_ref, bra_ref, brb_ref):
        bra = jnp.dot(yl_ref[...], wl_ref[...], preferred_element_type=F32)
        yp = yp_ref[...]
        brb = jnp.concatenate([jnp.dot(yp, wp_ref[k], preferred_element_type=F32) for k in range(NCHIP)], axis=1)
        bg = bg_ref[...]
        ga = _sigmoid(jnp.concatenate([p0[...], p1[...]], axis=1) + bg[:, :D])
        gb = _sigmoid(jnp.concatenate([p2[...], p3[...]], axis=1) + bg[:, D:])
        mrg = (ga * bra + gb * brb).astype(BF)
        m = jnp.dot(mrg, wo_ref[...], preferred_element_type=F32)
        r2 = lax.rsqrt(_mean(m * m) + NORM_EPS)
        x2 = x_ref[...] + (m * r2) * g2_ref[...]
        r3 = lax.rsqrt(_mean(x2 * x2) + NORM_EPS)
        x2_ref[...] = x2
        h2_ref[...] = ((x2 * r3) * g3_ref[...]).astype(BF)
        m_ref[...] = m
        mrg_ref[...] = mrg
        bra_ref[...] = bra.astype(BF)
        brb_ref[...] = brb.astype(BF)

    row = lambda w: pl.BlockSpec((tm, w), lambda i: (i, 0))
    full2 = lambda a, b: pl.BlockSpec((a, b), lambda i: (0, 0))
    return _call(
        body, name="fwd_merge", grid=(T // tm,),
        in_specs=[row(D), row(DR), row(DP)] + _gate_specs(tm) +
                 [full2(1, 2 * D), full2(1, D), full2(1, D), full2(DR, D),
                  pl.BlockSpec((NCHIP, DP, D // NCHIP), lambda i: (0, 0, 0)), full2(D, D)],
        out_specs=[row(D)] * 6,
        out_shape=[jax.ShapeDtypeStruct((T, D), F32), jax.ShapeDtypeStruct((T, D), BF),
                   jax.ShapeDtypeStruct((T, D), F32), jax.ShapeDtypeStruct((T, D), BF),
                   jax.ShapeDtypeStruct((T, D), BF), jax.ShapeDtypeStruct((T, D), BF)],
        vmem=48, args=[x, ylru, ypool, proj, proj, proj, proj, b_gate, g2, g3, w_lru_up, w_pool_up, w_o],
        stages=stages)


def _fwd_mlp(h2, w_ff1, w_ff2):
    tm = 512
    fk = DF // NCHIP

    def body(h_ref, w1_ref, w2_ref, a1_ref, f_ref):
        h = h_ref[...]
        f = None
        for k in range(NCHIP):
            a1 = jnp.maximum(jnp.dot(h, w1_ref[k], preferred_element_type=F32), 0.0)
            a1_ref[:, k * fk:(k + 1) * fk] = a1.astype(BF)
            part = jnp.dot((a1 * a1).astype(BF), w2_ref[k * fk:(k + 1) * fk, :], preferred_element_type=F32)
            f = part if f is None else f + part
        f_ref[...] = f

    return pl.pallas_call(
        body, name="fwd_mlp", grid=(T // tm,),
        in_specs=[pl.BlockSpec((tm, D), lambda i: (i, 0)),
                  pl.BlockSpec((NCHIP, D, fk), lambda i: (0, 0, 0)),
                  pl.BlockSpec((DF, D), lambda i: (0, 0))],
        out_specs=[pl.BlockSpec((tm, DF), lambda i: (i, 0)), pl.BlockSpec((tm, D), lambda i: (i, 0))],
        out_shape=_hbm_out([jax.ShapeDtypeStruct((T, DF), BF), jax.ShapeDtypeStruct((T, D), F32)]),
        compiler_params=_cp(56),
    )(*_hbm(h2, w_ff1, w_ff2))


def _loss_head(f, x2, target, g4):
    tm = 512

    def body(f_ref, x2_ref, t_ref, g_ref, loss_ref, dy_ref, df_ref, dg_ref):
        first = pl.program_id(0) == 0
        f = f_ref[...]
        g4v = g_ref[...]
        r4 = lax.rsqrt(_mean(f * f) + NORM_EPS)
        fn = f * r4
        e = (x2_ref[...] + fn * g4v) - t_ref[...]
        _acc(loss_ref, jnp.sum(_mean(e * e), axis=0, keepdims=True), first)
        dy = e * (1.0 / D)
        dy_ref[...] = dy
        _acc(dg_ref, _colsum(dy * fn), first)
        dfn = dy * g4v
        df_ref[...] = (r4 * (dfn - fn * _mean(dfn * fn))).astype(BF)

    row = pl.BlockSpec((tm, D), lambda i: (i, 0))
    return pl.pallas_call(
        body, name="loss_head", grid=(T // tm,),
        in_specs=[row, row, row, pl.BlockSpec((1, D), lambda i: (0, 0))],
        out_specs=[pl.BlockSpec((1, 1), lambda i: (0, 0)), row, row, pl.BlockSpec((1, D), lambda i: (0, 0))],
        out_shape=_hbm_out([jax.ShapeDtypeStruct((1, 1), F32), jax.ShapeDtypeStruct((T, D), F32),
                            jax.ShapeDtypeStruct((T, D), BF), jax.ShapeDtypeStruct((1, D), F32)]),
        compiler_params=_cp(48),
    )(*_hbm(f, x2, target, g4))


def _bwd_mlp_x(df, a1, w_ff1, w_ff2):
    tm = 512
    fk = DF // NCHIP

    def body(df_ref, a1_ref, w1_ref, w2_ref, dh_ref, df1_ref):
        df = df_ref[...]
        dh = None
        for k in range(NCHIP):
            cols = slice(k * fk, (k + 1) * fk)
            dact = _mm_nt(df, w2_ref[cols, :])
            df1 = (dact * (2.0 * a1_ref[:, cols].astype(F32))).astype(BF)
            df1_ref[:, cols] = df1
            part = _mm_nt(df1, w1_ref[k])
            dh = part if dh is None else dh + part
        dh_ref[...] = dh

    return pl.pallas_call(
        body, name="bwd_mlp_x", grid=(T // tm,),
        in_specs=[pl.BlockSpec((tm, D), lambda i: (i, 0)),
                  pl.BlockSpec((tm, DF), lambda i: (i, 0)),
                  pl.BlockSpec((NCHIP, D, fk), lambda i: (0, 0, 0)),
                  pl.BlockSpec((DF, D), lambda i: (0, 0))],
        out_specs=[pl.BlockSpec((tm, D), lambda i: (i, 0)), pl.BlockSpec((tm, DF), lambda i: (i, 0))],
        out_shape=_hbm_out([jax.ShapeDtypeStruct((T, D), F32), jax.ShapeDtypeStruct((T, DF), BF)]),
        compiler_params=_cp(56),
    )(*_hbm(df, a1, w_ff1, w_ff2))


def _bwd_mlp_w(df, h2, a1, df1):
    fc = 512
    per = (DF // NCHIP) // fc

    def body(df_ref, h_ref, a1_ref, df1_ref, dw1_ref, dw2_ref):
        a1 = a1_ref[...].astype(F32)
        dw2_ref[...] = _mm_tn((a1 * a1).astype(BF), df_ref[...]).astype(BF)
        dw1_ref[0] = _mm_tn(h_ref[...], df1_ref[...]).astype(BF)

    return pl.pallas_call(
        body, name="bwd_mlp_w", grid=(DF // fc,),
        in_specs=[pl.BlockSpec((T, D), lambda j: (0, 0)),
                  pl.BlockSpec((T, D), lambda j: (0, 0)),
                  pl.BlockSpec((T, fc), lambda j: (0, j)),
                  pl.BlockSpec((T, fc), lambda j: (0, j))],
        out_specs=[pl.BlockSpec((1, D, fc), lambda j: (j // per, 0, j % per)),
                   pl.BlockSpec((fc, D), lambda j: (j, 0))],
        out_shape=_hbm_out([jax.ShapeDtypeStruct((NCHIP, D, DF // NCHIP), BF),
                            jax.ShapeDtypeStruct((DF, D), BF)]),
        compiler_params=_cp(56),
    )(*_hbm(df, h2, a1, df1))


def _bwd_merge(dh2, dy, x2, m, bra, brb, proj, b_gate, g2, g3, w_lru_up, w_pool_up, w_o, stages=()):
    tm = 256
    cpu = D // NCHIP

    def body(dh2_ref, dy_ref, x2_ref, m_ref, bra_ref, brb_ref, p0, p1, p2, p3, bg_ref,
             g2_ref, g3_ref, wl_ref, wp_ref, wo_ref,
             dx_ref, dgt_ref, dyl_ref, dyp_ref, dm_ref, dbra_ref, dbrb_ref, dg2_ref, dg3_ref, dbg_ref):
        first = pl.program_id(0) == 0
        x2 = x2_ref[...]
        r3 = lax.rsqrt(_mean(x2 * x2) + NORM_EPS)
        x2n = x2 * r3
        dh2 = dh2_ref[...]
        t3 = dh2 * g3_ref[...]
        dx2 = dy_ref[...] + r3 * (t3 - x2n * _mean(t3 * x2n))
        dx_ref[...] = dx2
        _acc(dg3_ref, _colsum(dh2 * x2n), first)
        m = m_ref[...]
        r2 = lax.rsqrt(_mean(m * m) + NORM_EPS)
        mn = m * r2
        _acc(dg2_ref, _colsum(dx2 * mn), first)
        dmn = dx2 * g2_ref[...]
        dm = (r2 * (dmn - mn * _mean(dmn * mn))).astype(BF)
        dm_ref[...] = dm
        dmrg = _mm_nt(dm, wo_ref[...])
        bg = bg_ref[...]
        ga = _sigmoid(jnp.concatenate([p0[...], p1[...]], axis=1) + bg[:, :D])
        gb = _sigmoid(jnp.concatenate([p2[...], p3[...]], axis=1) + bg[:, D:])
        dga = dmrg * bra_ref[...].astype(F32) * (ga * (1.0 - ga))
        dgb = dmrg * brb_ref[...].astype(F32) * (gb * (1.0 - gb))
        dgt_ref[:, :D] = dga.astype(BF)
        dgt_ref[:, D:] = dgb.astype(BF)
        _acc(dbg_ref, jnp.concatenate([_colsum(dga), _colsum(dgb)], axis=1), first)
        dbra = (dmrg * ga).astype(BF)
        dbrb = (dmrg * gb).astype(BF)
        dbra_ref[...] = dbra
        dbrb_ref[...] = dbrb
        dyl_ref[...] = _mm_nt(dbra, wl_ref[...])
        dyp = None
        for k in range(NCHIP):
            part = _mm_nt(dbrb[:, k * cpu:(k + 1) * cpu], wp_ref[k])
            dyp = part if dyp is None else dyp + part
        dyp_ref[...] = dyp

    row = lambda w: pl.BlockSpec((tm, w), lambda i: (i, 0))
    full2 = lambda a, b: pl.BlockSpec((a, b), lambda i: (0, 0))
    wp_spec = pl.BlockSpec((NCHIP, DP, cpu), lambda i: (0, 0, 0))
    return _call(
        body, name="bwd_merge", grid=(T // tm,),
        in_specs=[row(D)] * 6 + _gate_specs(tm) +
                 [full2(1, 2 * D), full2(1, D), full2(1, D), full2(DR, D), wp_spec, full2(D, D)],
        out_specs=[row(D), row(2 * D), row(DR), row(DP), row(D), row(D), row(D),
                   full2(1, D), full2(1, D), full2(1, 2 * D)],
        out_shape=[jax.ShapeDtypeStruct((T, D), F32), jax.ShapeDtypeStruct((T, 2 * D), BF),
                   jax.ShapeDtypeStruct((T, DR), F32), jax.ShapeDtypeStruct((T, DP), F32),
                   jax.ShapeDtypeStruct((T, D), BF), jax.ShapeDtypeStruct((T, D), BF),
                   jax.ShapeDtypeStruct((T, D), BF),
                   jax.ShapeDtypeStruct((1, D), F32), jax.ShapeDtypeStruct((1, D), F32),
                   jax.ShapeDtypeStruct((1, 2 * D), F32)],
        vmem=56, args=[dh2, dy, x2, m, bra, brb, proj, proj, proj, proj, b_gate, g2, g3, w_lru_up, w_pool_up, w_o],
        stages=stages)


def _dw_merge(mrg, dm, ylru, dbra, ypool, dbrb, stages=()):
    nb = NCHIP
    rb, pb, cpu = D // nb, DP // nb, D // NCHIP

    def body(mrg_ref, dm_ref, yl_ref, dbra_ref, yp_ref, dbrb_ref, dwo_ref, dwl_ref, dwp_ref):
        dwo_ref[...] = _mm_tn(mrg_ref[...], dm_ref[...]).astype(BF)
        dwl_ref[...] = _mm_tn(yl_ref[...], dbra_ref[...]).astype(BF)
        dwp = _mm_tn(yp_ref[...], dbrb_ref[...]).astype(BF)
        for k in range(NCHIP):
            dwp_ref[k] = dwp[:, k * cpu:(k + 1) * cpu]

    cols = lambda w: pl.BlockSpec((T, w), lambda r: (0, r))
    whole = pl.BlockSpec((T, D), lambda r: (0, 0))
    return _call(
        body, name="dw_merge", grid=(nb,),
        in_specs=[cols(rb), whole, cols(rb), whole, cols(pb), whole],
        out_specs=[pl.BlockSpec((rb, D), lambda r: (r, 0)), pl.BlockSpec((rb, D), lambda r: (r, 0)),
                   pl.BlockSpec((NCHIP, pb, cpu), lambda r: (0, r, 0))],
        out_shape=[jax.ShapeDtypeStruct((D, D), BF), jax.ShapeDtypeStruct((DR, D), BF),
                   jax.ShapeDtypeStruct((NCHIP, DP, cpu), BF)],
        vmem=56, args=[mrg, dm, ylru, dbra, ypool, dbrb], stages=stages)


def _bwd_lru(proj, h, dylru, conv_w, conv_b, wa, ba, wx, bx, lam, stages=()):
    def body(xp_ref, g_ref, h_ref, dy_ref, cw_ref, cb_ref, wa_ref, ba_ref, wx_ref, bx_ref, lam_ref,
             dxp_ref, dg_ref, dcw_ref, dcb_ref, dwa_ref, dba_ref, dwx_ref, dbx_ref, dlam_ref, a_s, b_s, l_s):
        xp = xp_ref[...]
        cw = cw_ref[...]
        lam = lam_ref[...]
        xc, x1, x2, x3 = _conv(xp, cw, cb_ref[...])
        wa, wx = wa_ref[0], wx_ref[0]
        xcb, r, ii, sp, a, mult = _lru_gates(xc, wa, ba_ref[...], wx, bx_ref[...], lam)
        g = g_ref[...]
        gel, dgel = _gelu_parts(g)
        h = h_ref[...]
        dy = dy_ref[...]
        dg_ref[...] = (dy * h * dgel).astype(BF)
        _tile_scan(_su(a, 1, 0.0), dy * gel, a_s, b_s, l_s, reverse=True)
        b = l_s[...]
        da = b * _sd(h, 1, 0.0)
        dmult = b * (ii * xc)
        dii = b * (mult * xc)
        dxc = b * (mult * ii)
        dla = da * a - dmult * ((a * a) / mult)
        dr = dla * ((-LRU_C) * sp)
        dsp = _colsum(dla * ((-LRU_C) * r))
        dlam_ref[...] = -dsp / (1.0 + jnp.exp(lam))
        dzr = dr * (r * (1.0 - r))
        dzi = dii * (ii * (1.0 - ii))
        dzrb, dzib = dzr.astype(BF), dzi.astype(BF)
        dxc = dxc + _mm_nt(dzrb, wa) + _mm_nt(dzib, wx)
        dwa_ref[0] = _mm_tn(xcb, dzrb)
        dwx_ref[0] = _mm_tn(xcb, dzib)
        dba_ref[...] = _colsum(dzr)
        dbx_ref[...] = _colsum(dzi)
        dcb_ref[...] = _colsum(dxc)
        dcw_ref[...] = jnp.concatenate([_colsum(dxc * x3), _colsum(dxc * x2), _colsum(dxc * x1),
                                        _colsum(dxc * xp)], axis=0)
        dxp = cw[3:4] * dxc + cw[2:3] * _su(dxc, 1) + cw[1:2] * _su(dxc, 2) + cw[0:1] * _su(dxc, 3)
        dxp_ref[...] = dxp.astype(BF)

    blk = pl.BlockSpec((T, CB), lambda j: (0, j))
    wsp = pl.BlockSpec((1, CB, CB), lambda j: (j, 0, 0))
    return _call(
        body, name="bwd_lru", grid=(NG,),
        in_specs=[blk, pl.BlockSpec((T, CB), lambda j: (0, NG + j)), blk, blk,
                  pl.BlockSpec((4, CB), lambda j: (0, j)), _vec_spec(), wsp, _vec_spec(), wsp, _vec_spec(),
                  _vec_spec()],
        out_specs=[blk, blk, pl.BlockSpec((4, CB), lambda j: (0, j)), _vec_spec(), wsp, _vec_spec(), wsp,
                   _vec_spec(), _vec_spec()],
        out_shape=[jax.ShapeDtypeStruct((T, DR), BF), jax.ShapeDtypeStruct((T, DR), BF),
                   jax.ShapeDtypeStruct((4, DR), F32), jax.ShapeDtypeStruct((1, DR), F32),
                   jax.ShapeDtypeStruct((NG, CB, CB), F32), jax.ShapeDtypeStruct((1, DR), F32),
                   jax.ShapeDtypeStruct((NG, CB, CB), F32), jax.ShapeDtypeStruct((1, DR), F32),
                   jax.ShapeDtypeStruct((1, DR), F32)],
        vmem=56, args=[proj, proj, h, dylru, conv_w, conv_b, wa, ba, wx, bx, lam], stages=stages,
        scratch=[pltpu.VMEM((T, CB), F32)] * 3)


def _bwd_pool(proj, dypool, pool_w, pool_scale):
    def body(xp_ref, dy_ref, pw_ref, sc_ref, dx_ref, dw_ref, dsc_ref):
        for g, w in enumerate(POOL_WINDOWS):
            cols = slice(g * PG, (g + 1) * PG)
            cnt = _pool_cnt(w)
            x = xp_ref[:, cols]
            pb = (_pool_window(x, g + 1, _sd) / cnt - x).astype(BF)
            wg = pw_ref[g]
            dy = dy_ref[:, cols]
            dsc_ref[:, cols] = _colsum(dy * _mm(pb, wg))
            dyp = (dy * sc_ref[:, cols]).astype(BF)
            dw_ref[g] = _mm_tn(pb, dyp)
            dp = _mm_nt(dyp, wg)
            dx_ref[:, cols] = (_pool_window(dp / cnt, g + 1, _su) - dp).astype(BF)

    return pl.pallas_call(
        body, name="bwd_pool", grid=(1,),
        in_specs=[pl.BlockSpec((T, DP), lambda i: (0, 2 * DR // DP)),
                  pl.BlockSpec((T, DP), lambda i: (0, 0)),
                  pl.BlockSpec((4, PG, PG), lambda i: (0, 0, 0)),
                  pl.BlockSpec((1, DP), lambda i: (0, 0))],
        out_specs=[pl.BlockSpec((T, DP), lambda i: (0, 0)),
                   pl.BlockSpec((4, PG, PG), lambda i: (0, 0, 0)),
                   pl.BlockSpec((1, DP), lambda i: (0, 0))],
        out_shape=_hbm_out([jax.ShapeDtypeStruct((T, DP), BF), jax.ShapeDtypeStruct((4, PG, PG), F32),
                            jax.ShapeDtypeStruct((1, DP), F32)]),
        compiler_params=_cp(48),
    )(*_hbm(proj, dypool, pool_w, pool_scale))


PART_COLS = (DR, DR, DP, 2 * D)


def _shard_pieces():
    starts = [sum(PART_COLS[:p]) for p in range(len(PART_COLS))]
    shards = []
    for k in range(NCHIP):
        lo, hi = k * CW_IN, (k + 1) * CW_IN
        shards.append([(p, max(lo, s) - s, min(hi, s + wd) - s, max(lo, s) - lo)
                       for p, (s, wd) in enumerate(zip(starts, PART_COLS)) if max(lo, s) < min(hi, s + wd)])
    return shards


def _bwd_inproj_w(h1, parts, after):
    def body(h_ref, p0, p1, p2, p3, after_ref, dw_ref):
        part_refs = (p0, p1, p2, p3)
        for k, pieces in enumerate(_shard_pieces()):
            for p, a, b, c0 in pieces:
                dw_ref[k, :, c0:c0 + b - a] = _mm_tn(h_ref[...], part_refs[p][:, a:b]).astype(BF)

    vmem = pl.BlockSpec(memory_space=pltpu.VMEM)
    return pl.pallas_call(
        body, name="bwd_inproj_w", in_specs=[vmem] * 5 + [ANY], out_specs=vmem,
        out_shape=pltpu.HBM((NCHIP, D, CW_IN), BF), compiler_params=_cp(48),
    )(*_hbm(h1, *parts), after)


def _bwd_inproj_x(parts, w_in, x, dxres, g1, stages=()):
    tm = 512

    def body(p0, p1, p2, p3, w_ref, x_ref, dr_ref, g_ref, dx_ref, dg_ref):
        part_refs = (p0, p1, p2, p3)
        dh = None
        for k, pieces in enumerate(_shard_pieces()):
            for p, a, b, c0 in pieces:
                part = _mm_nt(part_refs[p][:, a:b], w_ref[k, :, c0:c0 + b - a])
                dh = part if dh is None else dh + part
        xv = x_ref[...]
        r = lax.rsqrt(_mean(xv * xv) + NORM_EPS)
        xn = xv * r
        t = dh * g_ref[...]
        dx_ref[...] = dr_ref[...] + r * (t - xn * _mean(t * xn))
        _acc(dg_ref, _colsum(dh * xn), pl.program_id(0) == 0)

    row = pl.BlockSpec((tm, D), lambda i: (i, 0))
    vec = pl.BlockSpec((1, D), lambda i: (0, 0))
    return _call(
        body, name="bwd_inproj_x", grid=(T // tm,),
        in_specs=[pl.BlockSpec((tm, wd), lambda i: (i, 0)) for wd in PART_COLS] +
                 [pl.BlockSpec((NCHIP, D, CW_IN), lambda i: (0, 0, 0)), row, row, vec],
        out_specs=[row, vec],
        out_shape=[jax.ShapeDtypeStruct((T, D), F32), jax.ShapeDtypeStruct((1, D), F32)],
        vmem=56, args=[*parts, w_in, x, dxres, g1], stages=stages)[0]


def _place():
    x, y, c = lax.axis_index("x"), lax.axis_index("y"), lax.axis_index("c")
    chips = [(1 - x, y), (x, 1 - y), (1 - x, 1 - y)]
    return x, y, c, chips


def _rcopy(src, dst, ssem, rsem, dev):
    return pltpu.make_async_remote_copy(src_ref=src, dst_ref=dst, send_sem=ssem, recv_sem=rsem,
                                        device_id=dev, device_id_type=MESH_ID)


def _sds(a):
    return jax.ShapeDtypeStruct(a.shape, a.dtype)


def _sem2(n, m):
    return [pltpu.SemaphoreType.DMA((n * m,)), pltpu.SemaphoreType.DMA((n * m,))]


ALL = (0, 1, 1)


def _piece(ref, k, half, part):
    hr = ref.shape[1] // 2
    r0, r1 = hr * part[0] // part[2], hr * part[1] // part[2]
    return ref.at[k, pl.ds(half * hr + r0, r1 - r0), :]


def _gather(fulls, ici=(), d2d=()):
    n = len(fulls)
    ici, d2d = list(ici), list(d2d)
    pieces = [("ici", i, part) for i, part in ici] + [("d2d", i, part) for i, part in d2d]

    def copies(outs, sems):
        x, y, c, chips = _place()
        me = 2 * x + y
        sib = (x, y, 1 - c)
        send, recv = [], []
        for q, (kind, i, part) in enumerate(pieces):
            for j, chip in enumerate(chips):
                k, s = 2 * chip[0] + chip[1], 3 * q + j
                if kind == "ici":
                    mine, theirs, dev = _piece(outs[i], me, c, part), _piece(outs[i], k, c, part), (*chip, c)
                else:
                    mine, theirs, dev = _piece(outs[i], k, c, part), _piece(outs[i], k, 1 - c, part), sib
                send.append(_rcopy(mine, mine, sems[0].at[s], sems[1].at[s], dev))
                recv.append(_rcopy(theirs, theirs, sems[0].at[s], sems[1].at[s], dev))
        return send, recv

    def start(ins, outs, sems):
        for cp in copies(outs, sems)[0]:
            cp.start()

    def finish(ins, outs, sems):
        send, recv = copies(outs, sems)
        for cp in recv:
            cp.wait_recv()
        for cp in send:
            cp.wait_send()

    sems = [pltpu.SemaphoreType.DMA((3 * len(pieces),)), pltpu.SemaphoreType.DMA((3 * len(pieces),))]
    return _Stage(fulls, [_sds(f) for f in fulls], {i: i for i in range(n)}, sems, start, finish)


def _gather_whole(v):
    def copies(ins, outs, sems):
        x, y, c, chips = _place()
        me = 2 * x + y
        send = [_rcopy(ins[0], outs[0].at[me], sems[0].at[j], sems[1].at[j], (*chip, c))
                for j, chip in enumerate(chips)]
        recv = [_rcopy(ins[0], outs[0].at[2 * chip[0] + chip[1]], sems[0].at[j], sems[1].at[j], (*chip, c))
                for j, chip in enumerate(chips)]
        return send, recv

    def start(ins, outs, sems):
        for cp in copies(ins, outs, sems)[0]:
            cp.start()

    def finish(ins, outs, sems):
        send, recv = copies(ins, outs, sems)
        for cp in recv:
            cp.wait_recv()
        for cp in send:
            cp.wait_send()

    return _Stage([v], [jax.ShapeDtypeStruct((NCHIP,) + v.shape, v.dtype)], {},
                  [pltpu.SemaphoreType.DMA((3,)), pltpu.SemaphoreType.DMA((3,))], start, finish)


def _to_sibling(srcs):
    n = len(srcs)

    def copies(ins, outs, sems):
        x, y, c, _ = _place()
        sib = (x, y, 1 - c)
        return [_rcopy(ins[i].at[:, 1 - c] if srcs[i].ndim == 4 else ins[i], outs[i], sems[0].at[i], sems[1].at[i], sib)
                for i in range(n)]

    def start(ins, outs, sems):
        for cp in copies(ins, outs, sems):
            cp.start()

    def finish(ins, outs, sems):
        for cp in copies(ins, outs, sems):
            cp.wait()

    shapes = [jax.ShapeDtypeStruct((NCHIP,) + s.shape[2:] if s.ndim == 4 else s.shape, s.dtype) for s in srcs]
    return _Stage(srcs, shapes, {}, [pltpu.SemaphoreType.DMA((n,)), pltpu.SemaphoreType.DMA((n,))], start, finish)


def _to_chips(srcs, parts=None, lands=None):
    n = len(srcs)
    parts = [ALL] * n if parts is None else parts
    lands = [None] * n if lands is None else lands
    given = [i for i in range(n) if lands[i] is not None]

    def rows(ref, i):
        hr = srcs[i].shape[1]
        r0, r1 = hr * parts[i][0] // parts[i][2], hr * parts[i][1] // parts[i][2]
        return ref.at[pl.ds(r0, r1 - r0), :]

    def copies(ins, outs, sems):
        x, y, c, chips = _place()
        me = 2 * x + y
        return [_rcopy(rows(ins[i].at[2 * chip[0] + chip[1]] if srcs[i].shape[0] == NCHIP else ins[i].at[c], i),
                       rows(outs[i].at[me], i), sems[0].at[3 * i + j], sems[1].at[3 * i + j], (*chip, c))
                for i in range(n) for j, chip in enumerate(chips)]

    def start(ins, outs, sems):
        for cp in copies(ins, outs, sems):
            cp.start()

    def finish(ins, outs, sems):
        for cp in copies(ins, outs, sems):
            cp.wait()

    shapes = [jax.ShapeDtypeStruct((NCHIP,) + s.shape[1:], s.dtype) for s in srcs]
    alias = {n + q: i for q, i in enumerate(given)}
    return _Stage(list(srcs) + [lands[i] for i in given], shapes, alias, _sem2(n, 3), start, finish)


HBM_REF = pl.BlockSpec(memory_space=pltpu.HBM)
SEM_REF = pl.BlockSpec(memory_space=pltpu.SEMAPHORE)
DATAFLOW = pltpu.SideEffectType.DATAFLOW_SIDE_EFFECTING


def _after(x):
    return _Stage([x], [], {}, [], lambda *a: None, lambda *a: None)


class _Flight:
    def __init__(self, stage, sems, bufs):
        self.stage, self.sems, self.bufs = stage, list(sems), list(bufs)

    def landed(self):
        st, n = self.stage, len(self.stage.operands)
        fresh = [j for j in range(len(st.out_shape)) if j not in st.alias.values()]
        back = {v: k for k, v in st.alias.items()}
        return [self.bufs[back[j]] if j in back else self.bufs[n + fresh.index(j)] for j in range(len(st.out_shape))]


def _split_call(name, finish=(), start=(), after=None):
    bufs, stage_bufs = [], []

    def slot(a):
        for i, b in enumerate(bufs):
            if b is a:
                return i
        bufs.append(a)
        return len(bufs) - 1

    fin_slots = [[slot(b) for b in fl.bufs] for fl in finish]
    for st in start:
        fresh = [lax.empty(o.shape, o.dtype) for j, o in enumerate(st.out_shape) if j not in st.alias.values()]
        stage_bufs.append([slot(a) for a in list(st.operands) + fresh])
    old_sems = [s for fl in finish for s in fl.sems]
    new_sems = [s for st in start for s in st.sems]
    nb, no, nn = len(bufs), len(old_sems), len(new_sems)

    def refs_of(st, slots, buf_refs):
        n = len(st.operands)
        ins = [buf_refs[i] for i in slots[:n]]
        fresh = [j for j in range(len(st.out_shape)) if j not in st.alias.values()]
        back = {v: k for k, v in st.alias.items()}
        outs = [ins[back[j]] if j in back else buf_refs[slots[n + fresh.index(j)]] for j in range(len(st.out_shape))]
        return ins, outs

    def body(*refs):
        buf_refs, sem_in = refs[:nb], refs[nb:nb + no]
        sem_out = refs[nb + no + (after is not None):][:nn]
        token = refs[-1]
        pos = 0
        for fl, slots in zip(finish, fin_slots):
            ins, outs = refs_of(fl.stage, slots, buf_refs)
            fl.stage.finish(ins, outs, sem_in[pos:pos + len(fl.sems)])
            pos += len(fl.sems)
        pos = 0
        for st, slots in zip(start, stage_bufs):
            ins, outs = refs_of(st, slots, buf_refs)
            st.start(ins, outs, sem_out[pos:pos + len(st.sems)])
            pos += len(st.sems)
        token[...] = jnp.zeros_like(token)

    res = pl.pallas_call(
        body, name=name,
        out_shape=tuple(new_sems) + tuple(pltpu.HBM(b.shape, b.dtype) for b in bufs) +
                  (jax.ShapeDtypeStruct((8, LANE), F32),),
        in_specs=(HBM_REF,) * nb + (SEM_REF,) * no + ((pl.BlockSpec(memory_space=pl.ANY),) if after is not None else ()),
        out_specs=(SEM_REF,) * nn + (HBM_REF,) * nb + (pl.BlockSpec(memory_space=pltpu.VMEM),),
        input_output_aliases={i: nn + i for i in range(nb)},
        compiler_params=pltpu.CompilerParams(has_side_effects=DATAFLOW),
    )(*_hbm(*bufs), *old_sems, *([after] if after is not None else []))
    sems, thru, token = res[:nn], res[nn:nn + nb], res[-1]
    for fl, slots in zip(finish, fin_slots):
        fl.bufs = [thru[i] for i in slots]
    flights, pos = [], 0
    for st, slots in zip(start, stage_bufs):
        flights.append(_Flight(st, sems[pos:pos + len(st.sems)], [thru[i] for i in slots]))
        pos += len(st.sems)
    return flights, token


def _last_copies(p_ref, land_ref, ssem, rsem):
    x, y, c, chips = _place()
    me = 2 * x + y
    send = [_rcopy(p_ref.at[2 * chip[0] + chip[1]], land_ref.at[me], ssem.at[j], rsem.at[j], (*chip, c))
            for j, chip in enumerate(chips)]
    recv = [_rcopy(p_ref.at[2 * chip[0] + chip[1]], land_ref.at[2 * chip[0] + chip[1]], ssem.at[j], rsem.at[j],
                   (*chip, c)) for j, chip in enumerate(chips)]
    return send, recv


def _chips_start(p):
    def body(p_ref, land_ref, ssem, rsem, p_thru, land_thru, token):
        for cp in _last_copies(p_ref, land_ref, ssem, rsem)[0]:
            cp.start()
        token[...] = jnp.zeros_like(token)

    return pl.pallas_call(
        body, name="reduce_last_start",
        out_shape=(pltpu.SemaphoreType.DMA((3,)), pltpu.SemaphoreType.DMA((3,)), pltpu.HBM(p.shape, p.dtype),
                   pltpu.HBM(p.shape, p.dtype), jax.ShapeDtypeStruct((8, LANE), F32)),
        in_specs=(HBM_REF, HBM_REF),
        out_specs=(SEM_REF, SEM_REF, HBM_REF, HBM_REF, pl.BlockSpec(memory_space=pltpu.VMEM)),
        input_output_aliases={0: 2, 1: 3},
        compiler_params=pltpu.CompilerParams(has_side_effects=DATAFLOW),
    )(*_hbm(p, lax.empty(p.shape, p.dtype)))


def _chips_wait(ssem, rsem, p_thru, land_thru, after):
    def body(p_ref, land_ref, ssem, rsem, after_ref, p_dead, got_ref):
        send, recv = _last_copies(p_ref, land_ref, ssem, rsem)
        for cp in send:
            cp.wait_send()
        for cp in recv:
            cp.wait_recv()

    return pl.pallas_call(
        body, name="reduce_last_wait",
        out_shape=(pltpu.HBM(p_thru.shape, p_thru.dtype), pltpu.HBM(land_thru.shape, land_thru.dtype)),
        in_specs=(HBM_REF, HBM_REF, SEM_REF, SEM_REF, pl.BlockSpec(memory_space=pl.ANY)),
        out_specs=(HBM_REF, HBM_REF), input_output_aliases={0: 0, 1: 1},
        compiler_params=pltpu.CompilerParams(has_side_effects=DATAFLOW),
    )(p_thru, land_thru, ssem, rsem, after)


def _share(pairs):
    n = len(pairs)

    def start(ins, outs, sems):
        x, y, c, _ = _place()
        for i in range(n):
            _rcopy(outs[i].at[c], outs[i].at[c], sems[0].at[i], sems[1].at[i], (x, y, 1 - c)).start()

    def finish(ins, outs, sems):
        x, y, c, _ = _place()
        for i in range(n):
            _rcopy(outs[i].at[c], outs[i].at[c], sems[0].at[i], sems[1].at[i], (x, y, 1 - c)).wait_send()
            _rcopy(outs[i].at[1 - c], outs[i].at[1 - c], sems[0].at[i], sems[1].at[i], (x, y, 1 - c)).wait_recv()

    return _Stage(pairs, [_sds(p) for p in pairs], {i: i for i in range(n)},
                  [pltpu.SemaphoreType.DMA((n,)), pltpu.SemaphoreType.DMA((n,))], start, finish)


def _row_block(rows, cols, itemsize=4, target=2 * MIB):
    br = rows
    while br * cols * itemsize > target and br % 32 == 0:
        br //= 2
    return br


def _cast_place(w, chip_idx, name):
    rows, cols = w.shape
    br = _row_block(rows, cols)

    def body(k_ref, w_ref, o_ref):
        o_ref[0] = w_ref[...].astype(BF)

    return _call(
        body, name=name, grid=(rows // br,), prefetch=chip_idx,
        in_specs=[pl.BlockSpec((br, cols), lambda r, k: (r, 0))],
        out_specs=[pl.BlockSpec((1, br, cols), lambda r, k: (k[0], r, 0))],
        out_shape=[jax.ShapeDtypeStruct((NCHIP, rows, cols), BF)], vmem=32, args=[w])[0][0]


def _cast_place_multi(ws, chip_idx, stages=()):
    br = 128
    nblk = [a.shape[0] // br for a in ws]
    starts = [sum(nblk[:i]) for i in range(len(ws))]

    def body(k_ref, *refs):
        r = pl.program_id(0)
        for i in range(len(ws)):
            @pl.when(jnp.logical_and(r >= starts[i], r < starts[i] + nblk[i]))
            def _(i=i):
                refs[len(ws) + i][0] = refs[i][...].astype(BF)

    def at(i):
        return functools.partial(lambda r, s, nb: jnp.clip(r - s, 0, nb - 1), s=starts[i], nb=nblk[i])

    outs, landed = _call(
        body, name="cast_rest", grid=(sum(nblk),), prefetch=chip_idx,
        in_specs=[pl.BlockSpec((br, a.shape[1]), functools.partial(lambda r, k, f: (f(r), 0), f=at(i)))
                  for i, a in enumerate(ws)],
        out_specs=[pl.BlockSpec((1, br, a.shape[1]), functools.partial(lambda r, k, f: (k[0], f(r), 0), f=at(i)))
                   for i, a in enumerate(ws)],
        out_shape=[jax.ShapeDtypeStruct((NCHIP,) + a.shape, BF) for a in ws], vmem=32, args=list(ws), stages=stages)
    return outs, landed


def _add_sibling(g, land, cidx, name, stages=()):
    _, _, hr, cols = g.shape
    br = _row_block(hr, cols)

    def body(c_ref, g_ref, l_ref, o_ref):
        o_ref[...] = (g_ref[0, 0].astype(F32) + l_ref[0].astype(F32)).astype(BF)[None]

    outs, st = _call(
        body, name=name, grid=(NCHIP, hr // br), prefetch=cidx,
        in_specs=[pl.BlockSpec((1, 1, br, cols), lambda k, r, c: (k, c[0], r, 0)),
                  pl.BlockSpec((1, br, cols), lambda k, r, c: (k, r, 0))],
        out_specs=[pl.BlockSpec((1, br, cols), lambda k, r, c: (k, r, 0))],
        out_shape=[jax.ShapeDtypeStruct((NCHIP, hr, cols), BF)], vmem=32, args=[g, land], stages=stages)
    return outs[0], st


def _add_sibling_multi(gs, lands, cidx, name):
    n = len(gs)
    brs = [_row_block(g.shape[2], g.shape[3]) for g in gs]
    nrb = [g.shape[2] // b for g, b in zip(gs, brs)]
    nblk = [NCHIP * q for q in nrb]
    starts = [sum(nblk[:i]) for i in range(n)]

    def body(c_ref, *refs):
        r = pl.program_id(0)
        for i in range(n):
            g_ref, l_ref, o_ref = refs[2 * i], refs[2 * i + 1], refs[2 * n + i]

            @pl.when(jnp.logical_and(r >= starts[i], r < starts[i] + nblk[i]))
            def _():
                o_ref[...] = (g_ref[0, 0].astype(F32) + l_ref[0].astype(F32)).astype(BF)[None]

    def at(i, r):
        q = jnp.clip(r - starts[i], 0, nblk[i] - 1)
        return q // nrb[i], q % nrb[i]

    def g_spec(i):
        return pl.BlockSpec((1, 1, brs[i], gs[i].shape[3]),
                            functools.partial(lambda r, c, i: (at(i, r)[0], c[0], at(i, r)[1], 0), i=i))

    def l_spec(i):
        return pl.BlockSpec((1, brs[i], gs[i].shape[3]),
                            functools.partial(lambda r, c, i: (at(i, r)[0], at(i, r)[1], 0), i=i))

    return _call(
        body, name=name, grid=(sum(nblk),), prefetch=cidx,
        in_specs=[s for i in range(n) for s in (g_spec(i), l_spec(i))], out_specs=[l_spec(i) for i in range(n)],
        out_shape=[jax.ShapeDtypeStruct(l.shape, BF) for l in lands], vmem=32,
        args=[a for i in range(n) for a in (gs[i], lands[i])])[0]


def _add_pair(a, b, name):
    rows, cols = a.shape

    def body(a_ref, b_ref, o_ref):
        o_ref[...] = a_ref[...] + b_ref[...]

    spec = pl.BlockSpec((rows, cols), lambda r: (0, 0))
    return _call(body, name=name, grid=(1,), in_specs=[spec, spec], out_specs=[spec], out_shape=[_sds(a)],
                 vmem=32, args=[a, b])[0][0]


def _add_chips(own, land, idx, name, stages=None):
    _, hr, cols = land.shape
    br = _row_block(hr, cols)

    def body(s_ref, a_ref, b_ref, c_ref, d_ref, o_ref):
        o_ref[...] = (a_ref[...].astype(F32) + b_ref[...].astype(F32)) + (c_ref[...].astype(F32) +
                                                                           d_ref[...].astype(F32))

    spec = lambda q: pl.BlockSpec((1, br, cols), functools.partial(lambda r, s, q: (s[q], r, 0), q=q))
    outs, landed = _call(
        body, name=name, grid=(hr // br,), prefetch=idx,
        in_specs=[spec(0), spec(1), spec(2), spec(3)], out_specs=[spec(4)],
        out_shape=[jax.ShapeDtypeStruct((2, hr, cols), F32)], vmem=48, args=[own, land, land, land],
        stages=stages or ())
    return outs[0] if stages is None else (outs[0], landed)


def _add_chips_multi(owns, lands, idx, name, stages=()):
    n = len(owns)
    brs = [_row_block(l.shape[1], l.shape[2]) for l in lands]
    nblk = [l.shape[1] // b for l, b in zip(lands, brs)]
    starts = [sum(nblk[:i]) for i in range(n)]

    def body(s_ref, *refs):
        r = pl.program_id(0)
        for i in range(n):
            a_ref, b_ref, c_ref, d_ref = refs[4 * i:4 * i + 4]
            o_ref = refs[4 * n + i]

            @pl.when(jnp.logical_and(r >= starts[i], r < starts[i] + nblk[i]))
            def _():
                o_ref[...] = (a_ref[...].astype(F32) + b_ref[...].astype(F32)) + (c_ref[...].astype(F32) +
                                                                                   d_ref[...].astype(F32))

    def spec(i, q):
        return pl.BlockSpec((1, brs[i], lands[i].shape[2]), functools.partial(
            lambda r, s, q, st, nb: (s[q], jnp.clip(r - st, 0, nb - 1), 0), q=q, st=starts[i], nb=nblk[i]))

    outs, landed = _call(
        body, name=name, grid=(sum(nblk),), prefetch=idx,
        in_specs=[spec(i, q) for i in range(n) for q in range(4)], out_specs=[spec(i, 4) for i in range(n)],
        out_shape=[jax.ShapeDtypeStruct((2,) + l.shape[1:], F32) for l in lands], vmem=48,
        args=[a for i in range(n) for a in (owns[i], lands[i], lands[i], lands[i])], stages=stages)
    return outs, landed


def _adamw_math(w, g, m, v):
    mn = ADAM_B1 * m + (1.0 - ADAM_B1) * g
    vn = ADAM_B2 * v + (1.0 - ADAM_B2) * (g * g)
    m_hat = mn / (1.0 - ADAM_B1 ** ADAM_STEP)
    v_hat = vn / (1.0 - ADAM_B2 ** ADAM_STEP)
    return -ADAM_LR * (m_hat / (jnp.sqrt(v_hat) + ADAM_EPS) + ADAM_WD * w), mn, vn


def _adamw(w, g, m, v, name, stages=()):
    rows, cols = w.shape
    br = _row_block(rows, cols)

    def body(w_ref, g_ref, m_ref, v_ref, go_ref, d_ref, mo_ref, vo_ref):
        gv = g_ref[...]
        go_ref[...] = gv
        d_ref[...], mo_ref[...], vo_ref[...] = _adamw_math(w_ref[...], gv, m_ref[...], v_ref[...])

    spec = pl.BlockSpec((br, cols), lambda r: (r, 0))
    return _call(body, name=name, grid=(rows // br,), in_specs=[spec] * 4, out_specs=[spec] * 4,
                 out_shape=[_sds(w)] * 4, vmem=56, args=[w, g, m, v], stages=stages)


def _adamw_multi(names, w, g, m, v, stages=()):
    cols = w[names[0]].shape[1]
    br = 128
    nblk = [w[n].shape[0] // br for n in names]
    starts = [sum(nblk[:i]) for i in range(len(names))]

    def body(*refs):
        r = pl.program_id(0)
        for i in range(len(names)):
            w_ref, g_ref, m_ref, v_ref = refs[4 * i:4 * i + 4]
            go_ref, d_ref, mo_ref, vo_ref = refs[4 * len(names) + 4 * i:4 * len(names) + 4 * i + 4]

            @pl.when(jnp.logical_and(r >= starts[i], r < starts[i] + nblk[i]))
            def _():
                gv = g_ref[...]
                go_ref[...] = gv
                d_ref[...], mo_ref[...], vo_ref[...] = _adamw_math(w_ref[...], gv, m_ref[...], v_ref[...])

    def spec(i):
        return pl.BlockSpec((br, cols), functools.partial(
            lambda r, s, nb: (jnp.clip(r - s, 0, nb - 1), 0), s=starts[i], nb=nblk[i]))

    outs, landed = _call(
        body, name="adamw_" + "_".join(names), grid=(sum(nblk),),
        in_specs=[spec(i) for i in range(len(names)) for _ in range(4)],
        out_specs=[spec(i) for i in range(len(names)) for _ in range(4)],
        out_shape=[_sds(w[n]) for n in names for _ in range(4)], vmem=56,
        args=[a[n] for n in names for a in (w, g, m, v)], stages=stages)
    return {n: outs[4 * i:4 * i + 4] for i, n in enumerate(names)}, landed


def _to_everyone(v):
    deltas = [(a, b, e) for a in (0, 1) for b in (0, 1) for e in (0, 1)][1:]

    def copies(ins, outs, sems):
        x, y, c, _ = _place()
        me = 4 * x + 2 * y + c
        flip = lambda p, f: 1 - p if f else p
        return [_rcopy(ins[0], outs[0].at[me], sems[0].at[q], sems[1].at[q], (flip(x, a), flip(y, b), flip(c, e)))
                for q, (a, b, e) in enumerate(deltas)]

    def start(ins, outs, sems):
        for cp in copies(ins, outs, sems):
            cp.start()

    def finish(ins, outs, sems):
        for cp in copies(ins, outs, sems):
            cp.wait()

    n = len(deltas)
    return _Stage([v], [jax.ShapeDtypeStruct((2 * NCHIP,) + v.shape, v.dtype)], {},
                  [pltpu.SemaphoreType.DMA((n,)), pltpu.SemaphoreType.DMA((n,))], start, finish)


SMALL_AT = {"norm_mix_pre": (0, 1, D), "norm_mix_post": (1, 1, D), "norm_mlp_pre": (2, 1, D),
            "norm_mlp_post": (3, 1, D), "b_gate": (4, 2, D), "conv_b": (6, 1, D), "lru_b_a": (7, 1, D),
            "lru_b_x": (8, 1, D), "lru_lambda": (9, 1, D), "pool_scale": (10, 1, DP)}
SMALL_SEPARATE = ["conv_w", "lru_w_a", "lru_w_x", "pool_w"]


def _adamw_small(small_sum, first_all, sep_grads, w, m, v):
    packed, sep = list(SMALL_AT), list(SMALL_SEPARATE)
    names = packed + sep

    def body(*refs):
        s_ref, a_ref, refs = refs[0], refs[1], refs[2:]
        g_sep, refs = refs[:len(sep)], refs[len(sep):]
        nn = len(names)
        w_r, m_r, v_r, refs = refs[:nn], refs[nn:2 * nn], refs[2 * nn:3 * nn], refs[3 * nn:]
        g_out, refs = refs[:len(packed)], refs[len(packed):]
        d_o, m_o, v_o = refs[:nn], refs[nn:2 * nn], refs[2 * nn:3 * nn]
        for i, n in enumerate(names):
            if i == 0:
                g = a_ref[0:1, :]
                for q in range(1, 2 * NCHIP):
                    g = g + a_ref[q:q + 1, :]
                g_out[i][...] = g
            elif n in SMALL_AT:
                r0, nr, nc = SMALL_AT[n]
                g = jnp.concatenate([s_ref[r0 + q:r0 + q + 1, :nc] for q in range(nr)], axis=1)
                g_out[i][...] = g
            else:
                g = g_sep[i - len(packed)][...]
            d_o[i][...], m_o[i][...], v_o[i][...] = _adamw_math(w_r[i][...], g, m_r[i][...], v_r[i][...])

    ws = [w[n] for n in names]
    res = pl.pallas_call(
        body, name="adamw_small",
        out_shape=[_sds(w[n]) for n in packed] + [_sds(a) for a in ws] * 3,
        compiler_params=_cp(32),
    )(*_hbm(small_sum, first_all, *sep_grads, *ws, *[m[n] for n in names], *[v[n] for n in names]))
    nn, npk = len(names), len(packed)
    grad = dict(zip(packed, res[:npk]))
    delta = dict(zip(names, res[npk:npk + nn]))
    new_m = dict(zip(names, res[npk + nn:npk + 2 * nn]))
    new_v = dict(zip(names, res[npk + 2 * nn:]))
    return grad, delta, new_m, new_v


W_NAMES = ["norm_mix_pre", "norm_mix_post", "norm_mlp_pre", "norm_mlp_post", "w_in", "b_gate", "conv_w", "conv_b",
           "lru_w_a", "lru_b_a", "lru_w_x", "lru_b_x", "lru_lambda", "pool_w", "pool_scale", "w_lru_up",
           "w_pool_up", "w_o", "w_ff1", "w_ff2"]
BIG = ["w_in", "w_lru_up", "w_pool_up", "w_o", "w_ff1", "w_ff2"]


def _block_diag(w):
    hd = w.shape[-1]
    per = CB // hd
    w4 = w.reshape(NG, per, hd, hd)
    eye = jnp.eye(per, dtype=w.dtype)
    return jnp.einsum("gpij,pq->gpiqj", w4, eye).reshape(NG, CB, CB)


def _block_diag_extract(d, hd):
    per = CB // hd
    d5 = d.reshape(NG, per, hd, per, hd)
    return jnp.stack([d5[:, p, :, p, :] for p in range(per)], axis=1).reshape(NG * per, hd, hd)


def _halves(g):
    return g.reshape(NCHIP, 2, g.size // (g.shape[-1] * 2 * NCHIP), g.shape[-1])


def kernel(x, norm_mix_pre, norm_mix_post, norm_mlp_pre, norm_mlp_post, w_in, b_gate, conv_w, conv_b, lru_w_a, lru_b_a, lru_w_x, lru_b_x, lru_lambda, pool_w, pool_scale, w_lru_up, w_pool_up, w_o, w_ff1, w_ff2, loss_target, m_norm_mix_pre, m_norm_mix_post, m_norm_mlp_pre, m_norm_mlp_post, m_w_in, m_b_gate, m_conv_w, m_conv_b, m_lru_w_a, m_lru_b_a, m_lru_w_x, m_lru_b_x, m_lru_lambda, m_pool_w, m_pool_scale, m_w_lru_up, m_w_pool_up, m_w_o, m_w_ff1, m_w_ff2, v_norm_mix_pre, v_norm_mix_post, v_norm_mlp_pre, v_norm_mlp_post, v_w_in, v_b_gate, v_conv_w, v_conv_b, v_lru_w_a, v_lru_b_a, v_lru_w_x, v_lru_b_x, v_lru_lambda, v_pool_w, v_pool_scale, v_w_lru_up, v_w_pool_up, v_w_o, v_w_ff1, v_w_ff2):
    args = dict(locals())
    two_d = lambda a: a.reshape(-1, a.shape[-1])
    w = {n: two_d(args[n]) for n in W_NAMES}
    mom = {n: two_d(args["m_" + n]) for n in W_NAMES}
    var = {n: two_d(args["v_" + n]) for n in W_NAMES}
    i32 = lambda val: jnp.asarray(val, jnp.int32)
    chip = i32(2 * lax.axis_index("x") + lax.axis_index("y"))
    core = i32(lax.axis_index("c"))
    cidx = core.reshape(1)
    zero = i32(0)
    hd = lru_w_a.shape[-1]
    xs, target = x[0], loss_target[0]
    g1, g2, g3, g4 = norm_mix_pre, norm_mix_post, norm_mlp_pre, norm_mlp_post

    mix = ["w_lru_up", "w_pool_up", "w_o"]
    full = {"w_in": _cast_place(w["w_in"], chip.reshape(1), "cast_w_in")}
    (fl_in, fl_conv), first = _split_call("gather_start_first", start=[
        _gather([full["w_in"]], ici=[(0, ALL)]), _gather_whole(w["conv_w"])])
    casts, _ = _cast_place_multi([w[n] for n in BIG[1:]], chip.reshape(1), stages=[_after(first)])
    full.update(zip(BIG[1:], casts))
    (fl_mix, fl_ff1, fl_ff2), started = _split_call("gather_start_rest", start=[
        _gather([full[n] for n in mix], ici=[(0, ALL), (1, ALL), (2, ALL)]),
        _gather([full["w_ff1"]], ici=[(0, ALL)]), _gather([full["w_ff2"]], ici=[(0, ALL)])])
    wa = _block_diag(lru_w_a[0]).astype(BF)
    wx = _block_diag(lru_w_x[0]).astype(BF)
    pw = pool_w[0].astype(BF)

    def to_sibling(name, flight, after=None):
        (fl,), passed = _split_call(name + "_pass", finish=[flight], after=after,
                                    start=[_gather(flight.landed(), d2d=[(i, ALL) for i in range(len(flight.bufs))])])
        passed_on.append(passed)
        return fl

    passed_on = []

    def arrived(name, flight, after=None):
        _split_call(name + "_done", finish=[flight], after=after)
        return flight.landed()

    idx_big = jnp.stack([chip, (chip + 1) % NCHIP, (chip + 2) % NCHIP, (chip + 3) % NCHIP, core])
    proj, h1 = _fwd_inproj_own(xs, g1, fl_in.bufs[0], idx_big, stages=[_after(started)])
    fl_in = to_sibling("gather_w_in", fl_in, after=h1)
    w_in_f, = arrived("gather_w_in", fl_in)
    conv_all, = arrived("gather_conv", fl_conv)
    full["w_in"] = w_in_f
    conv_all = lax.dynamic_update_slice(conv_all, w["conv_w"][None], (chip, zero, zero))
    conv_full = jnp.transpose(conv_all, (1, 0, 2)).reshape(4, DR)
    proj = _fwd_inproj_rest(h1, w_in_f, proj, idx_big)
    fl_mix = to_sibling("gather_mix", fl_mix, after=proj)
    (ylru, hs), _ = _fwd_lru(proj, conv_full, conv_b, wa, lru_b_a, wx, lru_b_x, lru_lambda,
                             stages=[_after(passed_on[-1])])
    got = arrived("gather_mix", fl_mix, after=ylru)
    fl_ff1 = to_sibling("gather_ff1", fl_ff1, after=ylru)
    w_lru_up_f, w_pool_up_f, w_o_f = got[0].reshape(DR, D), got[1], got[2].reshape(D, D)
    ypool = _fwd_pool(proj, pw, pool_scale)
    (x2, h2, m, mrg, bra, brb), _ = _fwd_merge(xs, ylru, ypool, proj, b_gate, g2, g3, w_lru_up_f, w_pool_up_f, w_o_f,
                                               stages=[_after(passed_on[-1])])
    fl_ff2 = to_sibling("gather_ff2", fl_ff2, after=h2)
    ff1, = arrived("gather_ff1", fl_ff1, after=h2)
    ff2, = arrived("gather_ff2", fl_ff2)
    ff2 = ff2.reshape(DF, D)
    a1, f = _fwd_mlp(h2, ff1, ff2)
    lossp, dy, df, dg4 = _loss_head(f, x2, target, g4)

    dh2, df1 = _bwd_mlp_x(df, a1, ff1, ff2)
    dw_ff1, dw_ff2 = _bwd_mlp_w(df, h2, a1, df1)
    g_ff = [_halves(dw_ff1), _halves(dw_ff2)]
    (dxres, dgates, dylru, dypool, dm, dbra, dbrb, dg2, dg3, dbg), (l_ff,) = _bwd_merge(
        dh2, dy, x2, m, bra, brb, proj, b_gate, g2, g3, w_lru_up_f, w_pool_up_f, w_o_f, stages=[_to_sibling(g_ff)])
    p_ff = _add_sibling_multi(g_ff, l_ff, cidx, "add_sibling_ff")
    (fl_ff,), sent_ff = _split_call("reduce_ff_start", start=[_to_chips(p_ff)])
    (dw_o, dw_lru_up, dw_pool_up), _ = _dw_merge(mrg, dm, ylru, dbra, ypool, dbrb, stages=[_after(sent_ff)])
    g_mix = [_halves(dw_lru_up), _halves(dw_pool_up), _halves(dw_o)]
    (dxp, dgl, dcw, dcb, dwa, dba, dwx, dbx, dlam), (l_mix,) = _bwd_lru(
        proj, hs, dylru, conv_full, conv_b, wa, lru_b_a, wx, lru_b_x, lru_lambda, stages=[_to_sibling(g_mix)])
    p_mix = _add_sibling_multi(g_mix, l_mix, cidx, "add_sibling_mix")
    dxpool, dpw, dsc = _bwd_pool(proj, dypool, pw, pool_scale)
    dproj = [dxp, dgl, dxpool, dgates]
    small = jnp.concatenate([
        jnp.zeros((1, D), F32), dg2, dg3, dg4, dbg.reshape(2, D), dcb, dba, dbx, dlam,
        jnp.pad(dsc, ((0, 0), (0, D - DP))), jnp.pad(lossp, ((0, 0), (0, D - 1))), dcw,
        _block_diag_extract(dwa, hd).reshape(-1, D), _block_diag_extract(dwx, hd).reshape(-1, D),
        dpw.reshape(-1, D)], axis=0)
    (fl_mixr, fl_smalls), sent_mix = _split_call("reduce_mix_start", start=[_to_chips(p_mix), _to_sibling([small])])
    dw_in = _bwd_inproj_w(h1, dproj, sent_mix)
    _split_call("reduce_small_sibling_done", finish=[fl_smalls], after=dw_in)
    small, l_small = fl_smalls.bufs
    small2 = _add_pair(small, l_small, "add_sibling_small").reshape(2, SMALL_ROWS // 2, D)
    g_in = _halves(dw_in)
    done = ["w_ff1", "w_ff2"] + mix
    (fl_gin, fl_small), sib_started = _split_call("reduce_in_sibling_start",
                                                  start=[_to_sibling([g_in]), _to_chips([small2])])
    _, chips_done = _split_call("reduce_chips_done", finish=[fl_ff, fl_mixr], after=sib_started)
    p_ff1, p_ff2, c_ff1, c_ff2 = fl_ff.bufs
    p_mix, c_mix = fl_mixr.bufs[:3], fl_mixr.bufs[3:]
    pairs, _ = _add_chips_multi([p_ff1, p_ff2] + p_mix, [c_ff1, c_ff2] + c_mix, idx_big, "add_chips_done",
                                stages=[_after(chips_done)])
    _split_call("reduce_in_sibling_done", finish=[fl_gin], after=pairs[-1])
    g_in, l_in = fl_gin.bufs
    p_in = _add_sibling(g_in, l_in, cidx, "add_sibling_w_in")[0]
    ssem, rsem, p_in, c_in, token = _chips_start(p_in)
    _split_call("reduce_small_done", finish=[fl_small], after=token)
    small2, c_small = fl_small.bufs
    own_small = lax.dynamic_index_in_dim(small2, core, 0, keepdims=True)
    c_small = lax.dynamic_update_slice(c_small, own_small, (chip, zero, zero))
    pair_small = _add_chips(c_small, c_small, jnp.stack([zero, zero + 1, zero + 2, zero + 3, core]), "add_chips_small")
    (fl_share,), shared_start = _split_call("reduce_share_start", start=[_share(pairs + [pair_small])])
    grad_x, dg1 = _bwd_inproj_x(dproj, full["w_in"], xs, dxres, g1, stages=[_after(shared_start)])
    _split_call("reduce_share_done", finish=[fl_share], after=dg1)
    shared = fl_share.landed()
    pairs, pair_small = shared[:-1], shared[-1]

    grads, delta, new_m, new_v = {}, {}, {}, {}
    for n, p in zip(done, pairs):
        grads[n] = p.reshape(-1, p.shape[-1])

    def update(n, stages=()):
        (grads[n], delta[n], new_m[n], new_v[n]), landed = _adamw(w[n], grads[n], mom[n], var[n], "adamw_" + n,
                                                                  stages=stages)
        return landed

    p_in, c_in = _chips_wait(ssem, rsem, p_in, c_in, pairs[0])
    pair_in = _add_chips(p_in, c_in, idx_big, "add_chips_w_in")
    (fl_last, fl_dg1), last_start = _split_call("reduce_last_share_start", start=[_share([pair_in]), _to_everyone(dg1)])
    updated, _ = _adamw_multi(["w_ff1", "w_ff2", "w_o", "w_lru_up"], w, grads, mom, var, stages=[_after(last_start)])
    for n, (go, d, mo, vo) in updated.items():
        grads[n], delta[n], new_m[n], new_v[n] = go, d, mo, vo
    _split_call("reduce_last_share_done", finish=[fl_last, fl_dg1], after=new_v["w_lru_up"])
    (pair_in,), (dg1, dg1_all) = fl_last.landed(), fl_dg1.bufs
    dg1_all = lax.dynamic_update_slice(dg1_all, dg1[None], (2 * chip + core, zero, zero)).reshape(2 * NCHIP, D)
    grads["w_in"] = pair_in.reshape(-1, pair_in.shape[-1])
    update("w_pool_up")
    update("w_in")
    small_sum = pair_small.reshape(SMALL_ROWS, D)
    loss = 0.5 * small_sum[LOSS_ROW, 0]
    ccols = DR // NCHIP
    sep = [lax.dynamic_slice(small_sum[12:16], (zero, chip * ccols), (4, ccols)),
           small_sum[16:80].reshape(-1, hd), small_sum[80:144].reshape(-1, hd), small_sum[144:208].reshape(-1, PG)]
    g_s, d_s, m_s, v_s = _adamw_small(small_sum, dg1_all, sep, w, mom, var)
    grads.update(g_s)
    grads.update(dict(zip(SMALL_SEPARATE, sep)))
    delta.update(d_s)
    new_m.update(m_s)
    new_v.update(v_s)

    out = lambda d: [d[n].reshape(args[n].shape) for n in W_NAMES]
    return (loss, grad_x[None], *out(grads), *out(delta), *out(new_m), *out(new_v))
```

```python
import functools
import math

import jax
import jax.numpy as jnp
from jax import lax
from jax.experimental import pallas as pl
from jax.experimental.pallas import tpu as pltpu

F32 = jnp.float32
BF = jnp.bfloat16

T = 2048
D = 1024
DR = 1024
DP = 512
DF = 4096
DIN = 4608
NCHIP = 4
CW_IN = DIN // NCHIP
LANE = 128
CB = 128
NG = DR // CB
PG = 128
POOL_WINDOWS = (2, 4, 8, 16)
NORM_EPS = 1e-6
LRU_C = 8.0
GELU_C = math.sqrt(2.0 / math.pi)
ADAM_LR = 0.001
ADAM_B1 = 0.9
ADAM_B2 = 0.999
ADAM_EPS = 1e-08
ADAM_WD = 0.01
ADAM_STEP = 10
MESH_ID = pl.DeviceIdType.MESH
ANY = pl.BlockSpec(memory_space=pl.ANY)
SMALL_ROWS = 208
LOSS_ROW = 11
MIB = 1 << 20


def _cp(vmem_mib=None):
    if vmem_mib is None:
        return pltpu.CompilerParams()
    return pltpu.CompilerParams(vmem_limit_bytes=vmem_mib * MIB)


def _hbm(*arrays):
    return [pltpu.with_memory_space_constraint(a, pltpu.HBM) for a in arrays]


def _hbm_out(shapes):
    return [pltpu.HBM(s.shape, s.dtype) for s in shapes]


class _Stage:
    def __init__(self, operands, out_shape, alias, sems, start, finish):
        self.operands, self.out_shape, self.alias, self.sems = list(operands), list(out_shape), dict(alias), list(sems)
        self.start, self.finish = start, finish


def _call(body, *, name, grid, in_specs, out_specs, out_shape, args, vmem=None, stages=(), prefetch=None,
          scratch=()):
    nin, nout = len(in_specs), len(out_specs)
    npre = 0 if prefetch is None else 1
    st_args, st_shapes, st_sems, aliases = [], [], list(scratch), {}
    for st in stages:
        for k, v in st.alias.items():
            aliases[npre + nin + len(st_args) + k] = nout + len(st_shapes) + v
        st_args += st.operands
        st_shapes += st.out_shape
        st_sems += st.sems

    def wrapped(*refs):
        pre, refs = refs[:npre], refs[npre:]
        ins, pos = refs[:nin], nin
        st_ins = []
        for st in stages:
            st_ins.append(refs[pos:pos + len(st.operands)])
            pos += len(st.operands)
        outs, pos = refs[pos:pos + nout], pos + nout
        st_outs = []
        for st in stages:
            st_outs.append(refs[pos:pos + len(st.out_shape)])
            pos += len(st.out_shape)
        work, pos = refs[pos:pos + len(scratch)], pos + len(scratch)
        sems = []
        for st in stages:
            sems.append(refs[pos:pos + len(st.sems)])
            pos += len(st.sems)
        if stages:
            first = functools.reduce(jnp.logical_and, [pl.program_id(a) == 0 for a in range(len(grid))])

            @pl.when(first)
            def _():
                for st, a, b, s in zip(stages, st_ins, st_outs, sems):
                    st.start(a, b, s)

        body(*pre, *ins, *outs, *work)
        if stages:
            last = functools.reduce(jnp.logical_and, [pl.program_id(a) == g - 1 for a, g in enumerate(grid)])

            @pl.when(last)
            def _():
                for st, a, b, s in zip(stages, st_ins, st_outs, sems):
                    st.finish(a, b, s)

    all_in = list(in_specs) + [ANY] * len(st_args)
    all_out = list(out_specs) + [ANY] * len(st_shapes)
    kw = dict(has_side_effects=True) if stages else {}
    if vmem is not None:
        kw["vmem_limit_bytes"] = vmem * MIB
    if prefetch is None:
        gkw = dict(grid=grid, in_specs=all_in, out_specs=all_out, scratch_shapes=st_sems)
    else:
        gkw = dict(grid_spec=pltpu.PrefetchScalarGridSpec(
            num_scalar_prefetch=1, grid=grid, in_specs=all_in, out_specs=all_out, scratch_shapes=st_sems))
    res = pl.pallas_call(
        wrapped, name=name, out_shape=_hbm_out(list(out_shape) + st_shapes), input_output_aliases=aliases,
        compiler_params=pltpu.CompilerParams(**kw), **gkw,
    )(*([prefetch] if npre else []), *_hbm(*args, *st_args))
    outs, rest, st_res = list(res[:nout]), list(res[nout:]), []
    for st in stages:
        st_res.append(rest[:len(st.out_shape)])
        rest = rest[len(st.out_shape):]
    return outs, st_res


def _mm(a, b):
    return jnp.dot(a.astype(BF), b.astype(BF), preferred_element_type=F32)


def _mm_nt(a, b):
    return lax.dot_general(a.astype(BF), b.astype(BF), (((1,), (1,)), ((), ())),
                           preferred_element_type=F32)


def _mm_tn(a, b):
    return lax.dot_general(a.astype(BF), b.astype(BF), (((0,), (0,)), ((), ())),
                           preferred_element_type=F32)


def _rows(v):
    return lax.broadcasted_iota(jnp.int32, v.shape, 0)


def _sd(v, s, fill=0.0):
    return jnp.where(_rows(v) >= s, pltpu.roll(v, s, axis=0), fill)


def _su(v, s, fill=0.0):
    n = v.shape[0]
    return jnp.where(_rows(v) < n - s, pltpu.roll(v, n - s, axis=0), fill)


def _sigmoid(z):
    return 1.0 / (1.0 + jnp.exp(-z))


def _softplus(z):
    e = jnp.exp(-jnp.abs(z))
    u = 1.0 + e
    d = u - 1.0
    log1p = jnp.where(d == 0.0, e, jnp.log(u) * (e / jnp.where(d == 0.0, 1.0, d)))
    return jnp.maximum(z, 0.0) + log1p


def _mean(v):
    return jnp.mean(v, axis=-1, keepdims=True)


def _colsum(v):
    return jnp.sum(v, axis=0, keepdims=True)


def _acc(ref, val, first):
    @pl.when(first)
    def _():
        ref[...] = val

    @pl.when(jnp.logical_not(first))
    def _():
        ref[...] += val


def _conv(xp, cw, cb):
    x1, x2, x3 = _sd(xp, 1), _sd(xp, 2), _sd(xp, 3)
    xc = cb + cw[0:1] * x3 + cw[1:2] * x2 + cw[2:3] * x1 + cw[3:4] * xp
    return xc, x1, x2, x3


def _lru_gates(xc, wa, ba, wx, bx, lam):
    xcb = xc.astype(BF)
    r = _sigmoid(_mm(xcb, wa) + ba)
    ii = _sigmoid(_mm(xcb, wx) + bx)
    sp = _softplus(-lam)
    la = (-LRU_C) * r * sp
    a = jnp.exp(la)
    mult = jnp.sqrt(-jnp.tanh(la) * (a * a + 1.0))
    return xcb, r, ii, sp, a, mult


def _gelu_parts(g):
    th = jnp.tanh(GELU_C * (g + 0.044715 * (g * g * g)))
    gel = 0.5 * g * (1.0 + th)
    dgel = 0.5 * (1.0 + th) + 0.5 * g * (1.0 - th * th) * (GELU_C * (1.0 + 3.0 * 0.044715 * (g * g)))
    return gel, dgel


def _tile_scan(a, b, a_s, b_s, out_ref, reverse):
    n, lanes = a.shape
    nt = n // 8
    a, b = a.reshape(nt, 8, lanes), b.reshape(nt, 8, lanes)
    sub = lax.broadcasted_iota(jnp.int32, a.shape, 1)
    s = 1
    while s < 8:
        keep = sub < 8 - s if reverse else sub >= s
        amount = 8 - s if reverse else s
        b = b + a * jnp.where(keep, pltpu.roll(b, amount, axis=1), 0.0)
        a = a * jnp.where(keep, pltpu.roll(a, amount, axis=1), 1.0)
        s *= 2
    a_s[...] = a.reshape(n, lanes)
    b_s[...] = b.reshape(n, lanes)
    edge = pl.ds(0 if reverse else 7, nt, stride=8)
    ta, tb = a_s[edge, :], b_s[edge, :]
    shift = _su if reverse else _sd
    s = 1
    while s < nt:
        tb = tb + ta * shift(tb, s, 0.0)
        if 2 * s < nt:
            ta = ta * shift(ta, s, 1.0)
        s *= 2
    enters = shift(tb, 1, 0.0)
    for o in range(8):
        rows = pl.ds(o, nt, stride=8)
        out_ref[rows, :] = b_s[rows, :] + a_s[rows, :] * enters


def _pool_window(x, steps, shift):
    s, sh = x, 1
    for _ in range(steps):
        s = s + shift(s, sh)
        sh *= 2
    return s


def _fwd_inproj_own(x, g1, w_in, slots, stages=()):
    tm = 512

    def body(s_ref, x_ref, g_ref, w_ref, proj_ref, h_ref):
        xv = x_ref[...]
        r = lax.rsqrt(_mean(xv * xv) + NORM_EPS)
        h = ((xv * r) * g_ref[...]).astype(BF)
        h_ref[...] = h
        proj_ref[...] = jnp.dot(h, w_ref[0], preferred_element_type=F32)

    return _call(
        body, name="fwd_inproj_own", grid=(T // tm,), prefetch=slots,
        in_specs=[pl.BlockSpec((tm, D), lambda i, s: (i, 0)),
                  pl.BlockSpec((1, D), lambda i, s: (0, 0)),
                  pl.BlockSpec((1, D, CW_IN), lambda i, s: (s[0], 0, 0))],
        out_specs=[pl.BlockSpec((tm, CW_IN), lambda i, s: (i, s[0])),
                   pl.BlockSpec((tm, D), lambda i, s: (i, 0))],
        out_shape=[jax.ShapeDtypeStruct((T, DIN), F32), jax.ShapeDtypeStruct((T, D), BF)],
        vmem=40, args=[x, g1, w_in], stages=stages)[0]


def _fwd_inproj_rest(h1, w_in, proj, slots):
    tm = 512

    def body(s_ref, h_ref, w_ref, p_in, proj_ref):
        proj_ref[...] = jnp.dot(h_ref[...], w_ref[0], preferred_element_type=F32)

    res = pl.pallas_call(
        body, name="fwd_inproj_rest",
        grid_spec=pltpu.PrefetchScalarGridSpec(
            num_scalar_prefetch=1, grid=(T // tm, NCHIP - 1),
            in_specs=[pl.BlockSpec((tm, D), lambda i, k, s: (i, 0)),
                      pl.BlockSpec((1, D, CW_IN), lambda i, k, s: (s[1 + k], 0, 0)), ANY],
            out_specs=pl.BlockSpec((tm, CW_IN), lambda i, k, s: (i, s[1 + k]))),
        out_shape=pltpu.HBM((T, DIN), F32), input_output_aliases={3: 0},
        compiler_params=_cp(40),
    )(slots, *_hbm(h1, w_in, proj))
    return res


def _vec_spec():
    return pl.BlockSpec((1, CB), lambda j: (0, j))


def _fwd_lru(proj, conv_w, conv_b, wa, ba, wx, bx, lam, stages=()):
    def body(xp_ref, g_ref, cw_ref, cb_ref, wa_ref, ba_ref, wx_ref, bx_ref, lam_ref, y_ref, h_ref, a_s, b_s):
        xc, _, _, _ = _conv(xp_ref[...], cw_ref[...], cb_ref[...])
        _, _, ii, _, a, mult = _lru_gates(xc, wa_ref[0], ba_ref[...], wx_ref[0], bx_ref[...], lam_ref[...])
        _tile_scan(a, mult * (ii * xc), a_s, b_s, h_ref, reverse=False)
        gel, _ = _gelu_parts(g_ref[...])
        y_ref[...] = (h_ref[...] * gel).astype(BF)

    return _call(
        body, name="fwd_lru", grid=(NG,),
        in_specs=[pl.BlockSpec((T, CB), lambda j: (0, j)),
                  pl.BlockSpec((T, CB), lambda j: (0, NG + j)),
                  pl.BlockSpec((4, CB), lambda j: (0, j)),
                  _vec_spec(),
                  pl.BlockSpec((1, CB, CB), lambda j: (j, 0, 0)), _vec_spec(),
                  pl.BlockSpec((1, CB, CB), lambda j: (j, 0, 0)), _vec_spec(),
                  _vec_spec()],
        out_specs=[pl.BlockSpec((T, CB), lambda j: (0, j)), pl.BlockSpec((T, CB), lambda j: (0, j))],
        out_shape=[jax.ShapeDtypeStruct((T, DR), BF), jax.ShapeDtypeStruct((T, DR), F32)],
        vmem=48, args=[proj, proj, conv_w, conv_b, wa, ba, wx, bx, lam], stages=stages,
        scratch=[pltpu.VMEM((T, CB), F32)] * 2)


def _pool_cnt(w):
    t = lax.broadcasted_iota(jnp.int32, (T, 1), 0)
    return jnp.minimum(t + 1, w).astype(F32)


def _fwd_pool(proj, pool_w, pool_scale):
    def body(xp_ref, pw_ref, sc_ref, y_ref):
        for g, w in enumerate(POOL_WINDOWS):
            cols = slice(g * PG, (g + 1) * PG)
            x = xp_ref[:, cols]
            p = _pool_window(x, g + 1, _sd) / _pool_cnt(w) - x
            y_ref[:, cols] = (_mm(p, pw_ref[g]) * sc_ref[:, cols]).astype(BF)

    return pl.pallas_call(
        body, name="fwd_pool", grid=(1,),
        in_specs=[pl.BlockSpec((T, DP), lambda i: (0, 2 * DR // DP)),
                  pl.BlockSpec((4, PG, PG), lambda i: (0, 0, 0)),
                  pl.BlockSpec((1, DP), lambda i: (0, 0))],
        out_specs=pl.BlockSpec((T, DP), lambda i: (0, 0)),
        out_shape=pltpu.HBM((T, DP), BF),
        compiler_params=_cp(48),
    )(*_hbm(proj, pool_w, pool_scale))


GATE_BLK = 512
GATE_BLK0 = (2 * DR + DP) // GATE_BLK


def _gate_specs(tm):
    return [pl.BlockSpec((tm, GATE_BLK), functools.partial(lambda i, q: (i, GATE_BLK0 + q), q=q))
            for q in range(4)]


def _fwd_merge(x, ylru, ypool, proj, b_gate, g2, g3, w_lru_up, w_pool_up, w_o, stages=()):
    tm = 512

    def body(x_ref, yl_ref, yp_ref, p0, p1, p2, p3, bg_ref, g2_ref, g3_ref, wl_ref, wp_ref, wo_ref,
             x2_ref, h2_ref, m_ref, mrg_ref, bra_ref, brb_ref):
        bra = jnp.dot(yl_ref[...], wl_ref[...], preferred_element_type=F32)
        yp = yp_ref[...]
        brb = jnp.concatenate([jnp.dot(yp, wp_ref[k], preferred_element_type=F32) for k in range(NCHIP)], axis=1)
        bg = bg_ref[...]
        ga = _sigmoid(jnp.concatenate([p0[...], p1[...]], axis=1) + bg[:, :D])
        gb = _sigmoid(jnp.concatenate([p2[...], p3[...]], axis=1) + bg[:, D:])
        mrg = (ga * bra + gb * brb).astype(BF)
        m = jnp.dot(mrg, wo_ref[...], preferred_element_type=F32)
        r2 = lax.rsqrt(_mean(m * m) + NORM_EPS)
        x2 = x_ref[...] + (m * r2) * g2_ref[...]
        r3 = lax.rsqrt(_mean(x2 * x2) + NORM_EPS)
        x2_ref[...] = x2
        h2_ref[...] = ((x2 * r3) * g3_ref[...]).astype(BF)
        m_ref[...] = m
        mrg_ref[...] = mrg
        bra_ref[...] = bra.astype(BF)
        brb_ref[...] = brb.astype(BF)

    row = lambda w: pl.BlockSpec((tm, w), lambda i: (i, 0))
    full2 = lambda a, b: pl.BlockSpec((a, b), lambda i: (0, 0))
    return _call(
        body, name="fwd_merge", grid=(T // tm,),
        in_specs=[row(D), row(DR), row(DP)] + _gate_specs(tm) +
                 [full2(1, 2 * D), full2(1, D), full2(1, D), full2(DR, D),
                  pl.BlockSpec((NCHIP, DP, D // NCHIP), lambda i: (0, 0, 0)), full2(D, D)],
        out_specs=[row(D)] * 6,
        out_shape=[jax.ShapeDtypeStruct((T, D), F32), jax.ShapeDtypeStruct((T, D), BF),
                   jax.ShapeDtypeStruct((T, D), F32), jax.ShapeDtypeStruct((T, D), BF),
                   jax.ShapeDtypeStruct((T, D), BF), jax.ShapeDtypeStruct((T, D), BF)],
        vmem=48, args=[x, ylru, ypool, proj, proj, proj, proj, b_gate, g2, g3, w_lru_up, w_pool_up, w_o],
        stages=stages)


def _fwd_mlp(h2, w_ff1, w_ff2):
    tm = 512
    fk = DF // NCHIP

    def body(h_ref, w1_ref, w2_ref, a1_ref, f_ref):
        h = h_ref[...]
        f = None
        for k in range(NCHIP):
            a1 = jnp.maximum(jnp.dot(h, w1_ref[k], preferred_element_type=F32), 0.0)
            a1_ref[:, k * fk:(k + 1) * fk] = a1.astype(BF)
            part = jnp.dot((a1 * a1).astype(BF), w2_ref[k * fk:(k + 1) * fk, :], preferred_element_type=F32)
            f = part if f is None else f + part
        f_ref[...] = f

    return pl.pallas_call(
        body, name="fwd_mlp", grid=(T // tm,),
        in_specs=[pl.BlockSpec((tm, D), lambda i: (i, 0)),
                  pl.BlockSpec((NCHIP, D, fk), lambda i: (0, 0, 0)),
                  pl.BlockSpec((DF, D), lambda i: (0, 0))],
        out_specs=[pl.BlockSpec((tm, DF), lambda i: (i, 0)), pl.BlockSpec((tm, D), lambda i: (i, 0))],
        out_shape=_hbm_out([jax.ShapeDtypeStruct((T, DF), BF), jax.ShapeDtypeStruct((T, D), F32)]),
        compiler_params=_cp(56),
    )(*_hbm(h2, w_ff1, w_ff2))


def _loss_head(f, x2, target, g4):
    tm = 512

    def body(f_ref, x2_ref, t_ref, g_ref, loss_ref, dy_ref, df_ref, dg_ref):
        first = pl.program_id(0) == 0
        f = f_ref[...]
        g4v = g_ref[...]
        r4 = lax.rsqrt(_mean(f * f) + NORM_EPS)
        fn = f * r4
        e = (x2_ref[...] + fn * g4v) - t_ref[...]
        _acc(loss_ref, jnp.sum(_mean(e * e), axis=0, keepdims=True), first)
        dy = e * (1.0 / D)
        dy_ref[...] = dy
        _acc(dg_ref, _colsum(dy * fn), first)
        dfn = dy * g4v
        df_ref[...] = (r4 * (dfn - fn * _mean(dfn * fn))).astype(BF)

    row = pl.BlockSpec((tm, D), lambda i: (i, 0))
    return pl.pallas_call(
        body, name="loss_head", grid=(T // tm,),
        in_specs=[row, row, row, pl.BlockSpec((1, D), lambda i: (0, 0))],
        out_specs=[pl.BlockSpec((1, 1), lambda i: (0, 0)), row, row, pl.BlockSpec((1, D), lambda i: (0, 0))],
        out_shape=_hbm_out([jax.ShapeDtypeStruct((1, 1), F32), jax.ShapeDtypeStruct((T, D), F32),
                            jax.ShapeDtypeStruct((T, D), BF), jax.ShapeDtypeStruct((1, D), F32)]),
        compiler_params=_cp(48),
    )(*_hbm(f, x2, target, g4))


def _bwd_mlp_x(df, a1, w_ff1, w_ff2):
    tm = 512
    fk = DF // NCHIP

    def body(df_ref, a1_ref, w1_ref, w2_ref, dh_ref, df1_ref):
        df = df_ref[...]
        dh = None
        for k in range(NCHIP):
            cols = slice(k * fk, (k + 1) * fk)
            dact = _mm_nt(df, w2_ref[cols, :])
            df1 = (dact * (2.0 * a1_ref[:, cols].astype(F32))).astype(BF)
            df1_ref[:, cols] = df1
            part = _mm_nt(df1, w1_ref[k])
            dh = part if dh is None else dh + part
        dh_ref[...] = dh

    return pl.pallas_call(
        body, name="bwd_mlp_x", grid=(T // tm,),
        in_specs=[pl.BlockSpec((tm, D), lambda i: (i, 0)),
                  pl.BlockSpec((tm, DF), lambda i: (i, 0)),
                  pl.BlockSpec((NCHIP, D, fk), lambda i: (0, 0, 0)),
                  pl.BlockSpec((DF, D), lambda i: (0, 0))],
        out_specs=[pl.BlockSpec((tm, D), lambda i: (i, 0)), pl.BlockSpec((tm, DF), lambda i: (i, 0))],
        out_shape=_hbm_out([jax.ShapeDtypeStruct((T, D), F32), jax.ShapeDtypeStruct((T, DF), BF)]),
        compiler_params=_cp(56),
    )(*_hbm(df, a1, w_ff1, w_ff2))


def _bwd_mlp_w(df, h2, a1, df1):
    fc = 512
    per = (DF // NCHIP) // fc

    def body(df_ref, h_ref, a1_ref, df1_ref, dw1_ref, dw2_ref):
        a1 = a1_ref[...].astype(F32)
        dw2_ref[...] = _mm_tn((a1 * a1).astype(BF), df_ref[...]).astype(BF)
        dw1_ref[0] = _mm_tn(h_ref[...], df1_ref[...]).astype(BF)

    return pl.pallas_call(
        body, name="bwd_mlp_w", grid=(DF // fc,),
        in_specs=[pl.BlockSpec((T, D), lambda j: (0, 0)),
                  pl.BlockSpec((T, D), lambda j: (0, 0)),
                  pl.BlockSpec((T, fc), lambda j: (0, j)),
                  pl.BlockSpec((T, fc), lambda j: (0, j))],
        out_specs=[pl.BlockSpec((1, D, fc), lambda j: (j // per, 0, j % per)),
                   pl.BlockSpec((fc, D), lambda j: (j, 0))],
        out_shape=_hbm_out([jax.ShapeDtypeStruct((NCHIP, D, DF // NCHIP), BF),
                            jax.ShapeDtypeStruct((DF, D), BF)]),
        compiler_params=_cp(56),
    )(*_hbm(df, h2, a1, df1))


def _bwd_merge(dh2, dy, x2, m, bra, brb, proj, b_gate, g2, g3, w_lru_up, w_pool_up, w_o, stages=()):
    tm = 256
    cpu = D // NCHIP

    def body(dh2_ref, dy_ref, x2_ref, m_ref, bra_ref, brb_ref, p0, p1, p2, p3, bg_ref,
             g2_ref, g3_ref, wl_ref, wp_ref, wo_ref,
             dx_ref, dgt_ref, dyl_ref, dyp_ref, dm_ref, dbra_ref, dbrb_ref, dg2_ref, dg3_ref, dbg_ref):
        first = pl.program_id(0) == 0
        x2 = x2_ref[...]
        r3 = lax.rsqrt(_mean(x2 * x2) + NORM_EPS)
        x2n = x2 * r3
        dh2 = dh2_ref[...]
        t3 = dh2 * g3_ref[...]
        dx2 = dy_ref[...] + r3 * (t3 - x2n * _mean(t3 * x2n))
        dx_ref[...] = dx2
        _acc(dg3_ref, _colsum(dh2 * x2n), first)
        m = m_ref[...]
        r2 = lax.rsqrt(_mean(m * m) + NORM_EPS)
        mn = m * r2
        _acc(dg2_ref, _colsum(dx2 * mn), first)
        dmn = dx2 * g2_ref[...]
        dm = (r2 * (dmn - mn * _mean(dmn * mn))).astype(BF)
        dm_ref[...] = dm
        dmrg = _mm_nt(dm, wo_ref[...])
        bg = bg_ref[...]
        ga = _sigmoid(jnp.concatenate([p0[...], p1[...]], axis=1) + bg[:, :D])
        gb = _sigmoid(jnp.concatenate([p2[...], p3[...]], axis=1) + bg[:, D:])
        dga = dmrg * bra_ref[...].astype(F32) * (ga * (1.0 - ga))
        dgb = dmrg * brb_ref[...].astype(F32) * (gb * (1.0 - gb))
        dgt_ref[:, :D] = dga.astype(BF)
        dgt_ref[:, D:] = dgb.astype(BF)
        _acc(dbg_ref, jnp.concatenate([_colsum(dga), _colsum(dgb)], axis=1), first)
        dbra = (dmrg * ga).astype(BF)
        dbrb = (dmrg * gb).astype(BF)
        dbra_ref[...] = dbra
        dbrb_ref[...] = dbrb
        dyl_ref[...] = _mm_nt(dbra, wl_ref[...])
        dyp = None
        for k in range(NCHIP):
            part = _mm_nt(dbrb[:, k * cpu:(k + 1) * cpu], wp_ref[k])
            dyp = part if dyp is None else dyp + part
        dyp_ref[...] = dyp

    row = lambda w: pl.BlockSpec((tm, w), lambda i: (i, 0))
    full2 = lambda a, b: pl.BlockSpec((a, b), lambda i: (0, 0))
    wp_spec = pl.BlockSpec((NCHIP, DP, cpu), lambda i: (0, 0, 0))
    return _call(
        body, name="bwd_merge", grid=(T // tm,),
        in_specs=[row(D)] * 6 + _gate_specs(tm) +
                 [full2(1, 2 * D), full2(1, D), full2(1, D), full2(DR, D), wp_spec, full2(D, D)],
        out_specs=[row(D), row(2 * D), row(DR), row(DP), row(D), row(D), row(D),
                   full2(1, D), full2(1, D), full2(1, 2 * D)],
        out_shape=[jax.ShapeDtypeStruct((T, D), F32), jax.ShapeDtypeStruct((T, 2 * D), BF),
                   jax.ShapeDtypeStruct((T, DR), F32), jax.ShapeDtypeStruct((T, DP), F32),
                   jax.ShapeDtypeStruct((T, D), BF), jax.ShapeDtypeStruct((T, D), BF),
                   jax.ShapeDtypeStruct((T, D), BF),
                   jax.ShapeDtypeStruct((1, D), F32), jax.ShapeDtypeStruct((1, D), F32),
                   jax.ShapeDtypeStruct((1, 2 * D), F32)],
        vmem=56, args=[dh2, dy, x2, m, bra, brb, proj, proj, proj, proj, b_gate, g2, g3, w_lru_up, w_pool_up, w_o],
        stages=stages)


def _dw_merge(mrg, dm, ylru, dbra, ypool, dbrb, stages=()):
    nb = NCHIP
    rb, pb, cpu = D // nb, DP // nb, D // NCHIP

    def body(mrg_ref, dm_ref, yl_ref, dbra_ref, yp_ref, dbrb_ref, dwo_ref, dwl_ref, dwp_ref):
        dwo_ref[...] = _mm_tn(mrg_ref[...], dm_ref[...]).astype(BF)
        dwl_ref[...] = _mm_tn(yl_ref[...], dbra_ref[...]).astype(BF)
        dwp = _mm_tn(yp_ref[...], dbrb_ref[...]).astype(BF)
        for k in range(NCHIP):
            dwp_ref[k] = dwp[:, k * cpu:(k + 1) * cpu]

    cols = lambda w: pl.BlockSpec((T, w), lambda r: (0, r))
    whole = pl.BlockSpec((T, D), lambda r: (0, 0))
    return _call(
        body, name="dw_merge", grid=(nb,),
        in_specs=[cols(rb), whole, cols(rb), whole, cols(pb), whole],
        out_specs=[pl.BlockSpec((rb, D), lambda r: (r, 0)), pl.BlockSpec((rb, D), lambda r: (r, 0)),
                   pl.BlockSpec((NCHIP, pb, cpu), lambda r: (0, r, 0))],
        out_shape=[jax.ShapeDtypeStruct((D, D), BF), jax.ShapeDtypeStruct((DR, D), BF),
                   jax.ShapeDtypeStruct((NCHIP, DP, cpu), BF)],
        vmem=56, args=[mrg, dm, ylru, dbra, ypool, dbrb], stages=stages)


def _bwd_lru(proj, h, dylru, conv_w, conv_b, wa, ba, wx, bx, lam, stages=()):
    def body(xp_ref, g_ref, h_ref, dy_ref, cw_ref, cb_ref, wa_ref, ba_ref, wx_ref, bx_ref, lam_ref,
             dxp_ref, dg_ref, dcw_ref, dcb_ref, dwa_ref, dba_ref, dwx_ref, dbx_ref, dlam_ref, a_s, b_s, l_s):
        xp = xp_ref[...]
        cw = cw_ref[...]
        lam = lam_ref[...]
        xc, x1, x2, x3 = _conv(xp, cw, cb_ref[...])
        wa, wx = wa_ref[0], wx_ref[0]
        xcb, r, ii, sp, a, mult = _lru_gates(xc, wa, ba_ref[...], wx, bx_ref[...], lam)
        g = g_ref[...]
        gel, dgel = _gelu_parts(g)
        h = h_ref[...]
        dy = dy_ref[...]
        dg_ref[...] = (dy * h * dgel).astype(BF)
        _tile_scan(_su(a, 1, 0.0), dy * gel, a_s, b_s, l_s, reverse=True)
        b = l_s[...]
        da = b * _sd(h, 1, 0.0)
        dmult = b * (ii * xc)
        dii = b * (mult * xc)
        dxc = b * (mult * ii)
        dla = da * a - dmult * ((a * a) / mult)
        dr = dla * ((-LRU_C) * sp)
        dsp = _colsum(dla * ((-LRU_C) * r))
        dlam_ref[...] = -dsp / (1.0 + jnp.exp(lam))
        dzr = dr * (r * (1.0 - r))
        dzi = dii * (ii * (1.0 - ii))
        dzrb, dzib = dzr.astype(BF), dzi.astype(BF)
        dxc = dxc + _mm_nt(dzrb, wa) + _mm_nt(dzib, wx)
        dwa_ref[0] = _mm_tn(xcb, dzrb)
        dwx_ref[0] = _mm_tn(xcb, dzib)
        dba_ref[...] = _colsum(dzr)
        dbx_ref[...] = _colsum(dzi)
        dcb_ref[...] = _colsum(dxc)
        dcw_ref[...] = jnp.concatenate([_colsum(dxc * x3), _colsum(dxc * x2), _colsum(dxc * x1),
                                        _colsum(dxc * xp)], axis=0)
        dxp = cw[3:4] * dxc + cw[2:3] * _su(dxc, 1) + cw[1:2] * _su(dxc, 2) + cw[0:1] * _su(dxc, 3)
        dxp_ref[...] = dxp.astype(BF)

    blk = pl.BlockSpec((T, CB), lambda j: (0, j))
    wsp = pl.BlockSpec((1, CB, CB), lambda j: (j, 0, 0))
    return _call(
        body, name="bwd_lru", grid=(NG,),
        in_specs=[blk, pl.BlockSpec((T, CB), lambda j: (0, NG + j)), blk, blk,
                  pl.BlockSpec((4, CB), lambda j: (0, j)), _vec_spec(), wsp, _vec_spec(), wsp, _vec_spec(),
                  _vec_spec()],
        out_specs=[blk, blk, pl.BlockSpec((4, CB), lambda j: (0, j)), _vec_spec(), wsp, _vec_spec(), wsp,
                   _vec_spec(), _vec_spec()],
        out_shape=[jax.ShapeDtypeStruct((T, DR), BF), jax.ShapeDtypeStruct((T, DR), BF),
                   jax.ShapeDtypeStruct((4, DR), F32), jax.ShapeDtypeStruct((1, DR), F32),
                   jax.ShapeDtypeStruct((NG, CB, CB), F32), jax.ShapeDtypeStruct((1, DR), F32),
                   jax.ShapeDtypeStruct((NG, CB, CB), F32), jax.ShapeDtypeStruct((1, DR), F32),
                   jax.ShapeDtypeStruct((1, DR), F32)],
        vmem=56, args=[proj, proj, h, dylru, conv_w, conv_b, wa, ba, wx, bx, lam], stages=stages,
        scratch=[pltpu.VMEM((T, CB), F32)] * 3)


def _bwd_pool(proj, dypool, pool_w, pool_scale):
    def body(xp_ref, dy_ref, pw_ref, sc_ref, dx_ref, dw_ref, dsc_ref):
        for g, w in enumerate(POOL_WINDOWS):
            cols = slice(g * PG, (g + 1) * PG)
            cnt = _pool_cnt(w)
            x = xp_ref[:, cols]
            pb = (_pool_window(x, g + 1, _sd) / cnt - x).astype(BF)
            wg = pw_ref[g]
            dy = dy_ref[:, cols]
            dsc_ref[:, cols] = _colsum(dy * _mm(pb, wg))
            dyp = (dy * sc_ref[:, cols]).astype(BF)
            dw_ref[g] = _mm_tn(pb, dyp)
            dp = _mm_nt(dyp, wg)
            dx_ref[:, cols] = (_pool_window(dp / cnt, g + 1, _su) - dp).astype(BF)

    return pl.pallas_call(
        body, name="bwd_pool", grid=(1,),
        in_specs=[pl.BlockSpec((T, DP), lambda i: (0, 2 * DR // DP)),
                  pl.BlockSpec((T, DP), lambda i: (0, 0)),
                  pl.BlockSpec((4, PG, PG), lambda i: (0, 0, 0)),
                  pl.BlockSpec((1, DP), lambda i: (0, 0))],
        out_specs=[pl.BlockSpec((T, DP), lambda i: (0, 0)),
                   pl.BlockSpec((4, PG, PG), lambda i: (0, 0, 0)),
                   pl.BlockSpec((1, DP), lambda i: (0, 0))],
        out_shape=_hbm_out([jax.ShapeDtypeStruct((T, DP), BF), jax.ShapeDtypeStruct((4, PG, PG), F32),
                            jax.ShapeDtypeStruct((1, DP), F32)]),
        compiler_params=_cp(48),
    )(*_hbm(proj, dypool, pool_w, pool_scale))


PART_COLS = (DR, DR, DP, 2 * D)


def _shard_pieces():
    starts = [sum(PART_COLS[:p]) for p in range(len(PART_COLS))]
    shards = []
    for k in range(NCHIP):
        lo, hi = k * CW_IN, (k + 1) * CW_IN
        shards.append([(p, max(lo, s) - s, min(hi, s + wd) - s, max(lo, s) - lo)
                       for p, (s, wd) in enumerate(zip(starts, PART_COLS)) if max(lo, s) < min(hi, s + wd)])
    return shards


def _bwd_inproj_w(h1, parts, after):
    def body(h_ref, p0, p1, p2, p3, after_ref, dw_ref):
        part_refs = (p0, p1, p2, p3)
        for k, pieces in enumerate(_shard_pieces()):
            for p, a, b, c0 in pieces:
                dw_ref[k, :, c0:c0 + b - a] = _mm_tn(h_ref[...], part_refs[p][:, a:b]).astype(BF)

    vmem = pl.BlockSpec(memory_space=pltpu.VMEM)
    return pl.pallas_call(
        body, name="bwd_inproj_w", in_specs=[vmem] * 5 + [ANY], out_specs=vmem,
        out_shape=pltpu.HBM((NCHIP, D, CW_IN), BF), compiler_params=_cp(48),
    )(*_hbm(h1, *parts), after)


def _bwd_inproj_x(parts, w_in, x, dxres, g1, stages=()):
    tm = 512

    def body(p0, p1, p2, p3, w_ref, x_ref, dr_ref, g_ref, dx_ref, dg_ref):
        part_refs = (p0, p1, p2, p3)
        dh = None
        for k, pieces in enumerate(_shard_pieces()):
            for p, a, b, c0 in pieces:
                part = _mm_nt(part_refs[p][:, a:b], w_ref[k, :, c0:c0 + b - a])
                dh = part if dh is None else dh + part
        xv = x_ref[...]
        r = lax.rsqrt(_mean(xv * xv) + NORM_EPS)
        xn = xv * r
        t = dh * g_ref[...]
        dx_ref[...] = dr_ref[...] + r * (t - xn * _mean(t * xn))
        _acc(dg_ref, _colsum(dh * xn), pl.program_id(0) == 0)

    row = pl.BlockSpec((tm, D), lambda i: (i, 0))
    vec = pl.BlockSpec((1, D), lambda i: (0, 0))
    return _call(
        body, name="bwd_inproj_x", grid=(T // tm,),
        in_specs=[pl.BlockSpec((tm, wd), lambda i: (i, 0)) for wd in PART_COLS] +
                 [pl.BlockSpec((NCHIP, D, CW_IN), lambda i: (0, 0, 0)), row, row, vec],
        out_specs=[row, vec],
        out_shape=[jax.ShapeDtypeStruct((T, D), F32), jax.ShapeDtypeStruct((1, D), F32)],
        vmem=56, args=[*parts, w_in, x, dxres, g1], stages=stages)[0]


def _place():
    x, y, c = lax.axis_index("x"), lax.axis_index("y"), lax.axis_index("c")
    chips = [(1 - x, y), (x, 1 - y), (1 - x, 1 - y)]
    return x, y, c, chips


def _rcopy(src, dst, ssem, rsem, dev):
    return pltpu.make_async_remote_copy(src_ref=src, dst_ref=dst, send_sem=ssem, recv_sem=rsem,
                                        device_id=dev, device_id_type=MESH_ID)


def _sds(a):
    return jax.ShapeDtypeStruct(a.shape, a.dtype)


def _sem2(n, m):
    return [pltpu.SemaphoreType.DMA((n * m,)), pltpu.SemaphoreType.DMA((n * m,))]


ALL = (0, 1, 1)


def _piece(ref, k, half, part):
    hr = ref.shape[1] // 2
    r0, r1 = hr * part[0] // part[2], hr * part[1] // part[2]
    return ref.at[k, pl.ds(half * hr + r0, r1 - r0), :]


def _gather(fulls, ici=(), d2d=()):
    n = len(fulls)
    ici, d2d = list(ici), list(d2d)
    pieces = [("ici", i, part) for i, part in ici] + [("d2d", i, part) for i, part in d2d]

    def copies(outs, sems):
        x, y, c, chips = _place()
        me = 2 * x + y
        sib = (x, y, 1 - c)
        send, recv = [], []
        for q, (kind, i, part) in enumerate(pieces):
            for j, chip in enumerate(chips):
                k, s = 2 * chip[0] + chip[1], 3 * q + j
                if kind == "ici":
                    mine, theirs, dev = _piece(outs[i], me, c, part), _piece(outs[i], k, c, part), (*chip, c)
                else:
                    mine, theirs, dev = _piece(outs[i], k, c, part), _piece(outs[i], k, 1 - c, part), sib
                send.append(_rcopy(mine, mine, sems[0].at[s], sems[1].at[s], dev))
                recv.append(_rcopy(theirs, theirs, sems[0].at[s], sems[1].at[s], dev))
        return send, recv

    def start(ins, outs, sems):
        for cp in copies(outs, sems)[0]:
            cp.start()

    def finish(ins, outs, sems):
        send, recv = copies(outs, sems)
        for cp in recv:
            cp.wait_recv()
        for cp in send:
            cp.wait_send()

    sems = [pltpu.SemaphoreType.DMA((3 * len(pieces),)), pltpu.SemaphoreType.DMA((3 * len(pieces),))]
    return _Stage(fulls, [_sds(f) for f in fulls], {i: i for i in range(n)}, sems, start, finish)


def _gather_whole(v):
    def copies(ins, outs, sems):
        x, y, c, chips = _place()
        me = 2 * x + y
        send = [_rcopy(ins[0], outs[0].at[me], sems[0].at[j], sems[1].at[j], (*chip, c))
                for j, chip in enumerate(chips)]
        recv = [_rcopy(ins[0], outs[0].at[2 * chip[0] + chip[1]], sems[0].at[j], sems[1].at[j], (*chip, c))
                for j, chip in enumerate(chips)]
        return send, recv

    def start(ins, outs, sems):
        for cp in copies(ins, outs, sems)[0]:
            cp.start()

    def finish(ins, outs, sems):
        send, recv = copies(ins, outs, sems)
        for cp in recv:
            cp.wait_recv()
        for cp in send:
            cp.wait_send()

    return _Stage([v], [jax.ShapeDtypeStruct((NCHIP,) + v.shape, v.dtype)], {},
                  [pltpu.SemaphoreType.DMA((3,)), pltpu.SemaphoreType.DMA((3,))], start, finish)


def _to_sibling(srcs):
    n = len(srcs)

    def copies(ins, outs, sems):
        x, y, c, _ = _place()
        sib = (x, y, 1 - c)
        return [_rcopy(ins[i].at[:, 1 - c] if srcs[i].ndim == 4 else ins[i], outs[i], sems[0].at[i], sems[1].at[i], sib)
                for i in range(n)]

    def start(ins, outs, sems):
        for cp in copies(ins, outs, sems):
            cp.start()

    def finish(ins, outs, sems):
        for cp in copies(ins, outs, sems):
            cp.wait()

    shapes = [jax.ShapeDtypeStruct((NCHIP,) + s.shape[2:] if s.ndim == 4 else s.shape, s.dtype) for s in srcs]
    return _Stage(srcs, shapes, {}, [pltpu.SemaphoreType.DMA((n,)), pltpu.SemaphoreType.DMA((n,))], start, finish)


def _to_chips(srcs, parts=None, lands=None):
    n = len(srcs)
    parts = [ALL] * n if parts is None else parts
    lands = [None] * n if lands is None else lands
    given = [i for i in range(n) if lands[i] is not None]

    def rows(ref, i):
        hr = srcs[i].shape[1]
        r0, r1 = hr * parts[i][0] // parts[i][2], hr * parts[i][1] // parts[i][2]
        return ref.at[pl.ds(r0, r1 - r0), :]

    def copies(ins, outs, sems):
        x, y, c, chips = _place()
        me = 2 * x + y
        return [_rcopy(rows(ins[i].at[2 * chip[0] + chip[1]] if srcs[i].shape[0] == NCHIP else ins[i].at[c], i),
                       rows(outs[i].at[me], i), sems[0].at[3 * i + j], sems[1].at[3 * i + j], (*chip, c))
                for i in range(n) for j, chip in enumerate(chips)]

    def start(ins, outs, sems):
        for cp in copies(ins, outs, sems):
            cp.start()

    def finish(ins, outs, sems):
        for cp in copies(ins, outs, sems):
            cp.wait()

    shapes = [jax.ShapeDtypeStruct((NCHIP,) + s.shape[1:], s.dtype) for s in srcs]
    alias = {n + q: i for q, i in enumerate(given)}
    return _Stage(list(srcs) + [lands[i] for i in given], shapes, alias, _sem2(n, 3), start, finish)


HBM_REF = pl.BlockSpec(memory_space=pltpu.HBM)
SEM_REF = pl.BlockSpec(memory_space=pltpu.SEMAPHORE)
DATAFLOW = pltpu.SideEffectType.DATAFLOW_SIDE_EFFECTING


def _after(x):
    return _Stage([x], [], {}, [], lambda *a: None, lambda *a: None)


class _Flight:
    def __init__(self, stage, sems, bufs):
        self.stage, self.sems, self.bufs = stage, list(sems), list(bufs)

    def landed(self):
        st, n = self.stage, len(self.stage.operands)
        fresh = [j for j in range(len(st.out_shape)) if j not in st.alias.values()]
        back = {v: k for k, v in st.alias.items()}
        return [self.bufs[back[j]] if j in back else self.bufs[n + fresh.index(j)] for j in range(len(st.out_shape))]


def _split_call(name, finish=(), start=(), after=None):
    bufs, stage_bufs = [], []

    def slot(a):
        for i, b in enumerate(bufs):
            if b is a:
                return i
        bufs.append(a)
        return len(bufs) - 1

    fin_slots = [[slot(b) for b in fl.bufs] for fl in finish]
    for st in start:
        fresh = [lax.empty(o.shape, o.dtype) for j, o in enumerate(st.out_shape) if j not in st.alias.values()]
        stage_bufs.append([slot(a) for a in list(st.operands) + fresh])
    old_sems = [s for fl in finish for s in fl.sems]
    new_sems = [s for st in start for s in st.sems]
    nb, no, nn = len(bufs), len(old_sems), len(new_sems)

    def refs_of(st, slots, buf_refs):
        n = len(st.operands)
        ins = [buf_refs[i] for i in slots[:n]]
        fresh = [j for j in range(len(st.out_shape)) if j not in st.alias.values()]
        back = {v: k for k, v in st.alias.items()}
        outs = [ins[back[j]] if j in back else buf_refs[slots[n + fresh.index(j)]] for j in range(len(st.out_shape))]
        return ins, outs

    def body(*refs):
        buf_refs, sem_in = refs[:nb], refs[nb:nb + no]
        sem_out = refs[nb + no + (after is not None):][:nn]
        token = refs[-1]
        pos = 0
        for fl, slots in zip(finish, fin_slots):
            ins, outs = refs_of(fl.stage, slots, buf_refs)
            fl.stage.finish(ins, outs, sem_in[pos:pos + len(fl.sems)])
            pos += len(fl.sems)
        pos = 0
        for st, slots in zip(start, stage_bufs):
            ins, outs = refs_of(st, slots, buf_refs)
            st.start(ins, outs, sem_out[pos:pos + len(st.sems)])
            pos += len(st.sems)
        token[...] = jnp.zeros_like(token)

    res = pl.pallas_call(
        body, name=name,
        out_shape=tuple(new_sems) + tuple(pltpu.HBM(b.shape, b.dtype) for b in bufs) +
                  (jax.ShapeDtypeStruct((8, LANE), F32),),
        in_specs=(HBM_REF,) * nb + (SEM_REF,) * no + ((pl.BlockSpec(memory_space=pl.ANY),) if after is not None else ()),
        out_specs=(SEM_REF,) * nn + (HBM_REF,) * nb + (pl.BlockSpec(memory_space=pltpu.VMEM),),
        input_output_aliases={i: nn + i for i in range(nb)},
        compiler_params=pltpu.CompilerParams(has_side_effects=DATAFLOW),
    )(*_hbm(*bufs), *old_sems, *([after] if after is not None else []))
    sems, thru, token = res[:nn], res[nn:nn + nb], res[-1]
    for fl, slots in zip(finish, fin_slots):
        fl.bufs = [thru[i] for i in slots]
    flights, pos = [], 0
    for st, slots in zip(start, stage_bufs):
        flights.append(_Flight(st, sems[pos:pos + len(st.sems)], [thru[i] for i in slots]))
        pos += len(st.sems)
    return flights, token


def _last_copies(p_ref, land_ref, ssem, rsem):
    x, y, c, chips = _place()
    me = 2 * x + y
    send = [_rcopy(p_ref.at[2 * chip[0] + chip[1]], land_ref.at[me], ssem.at[j], rsem.at[j], (*chip, c))
            for j, chip in enumerate(chips)]
    recv = [_rcopy(p_ref.at[2 * chip[0] + chip[1]], land_ref.at[2 * chip[0] + chip[1]], ssem.at[j], rsem.at[j],
                   (*chip, c)) for j, chip in enumerate(chips)]
    return send, recv


def _chips_start(p):
    def body(p_ref, land_ref, ssem, rsem, p_thru, land_thru, token):
        for cp in _last_copies(p_ref, land_ref, ssem, rsem)[0]:
            cp.start()
        token[...] = jnp.zeros_like(token)

    return pl.pallas_call(
        body, name="reduce_last_start",
        out_shape=(pltpu.SemaphoreType.DMA((3,)), pltpu.SemaphoreType.DMA((3,)), pltpu.HBM(p.shape, p.dtype),
                   pltpu.HBM(p.shape, p.dtype), jax.ShapeDtypeStruct((8, LANE), F32)),
        in_specs=(HBM_REF, HBM_REF),
        out_specs=(SEM_REF, SEM_REF, HBM_REF, HBM_REF, pl.BlockSpec(memory_space=pltpu.VMEM)),
        input_output_aliases={0: 2, 1: 3},
        compiler_params=pltpu.CompilerParams(has_side_effects=DATAFLOW),
    )(*_hbm(p, lax.empty(p.shape, p.dtype)))


def _chips_wait(ssem, rsem, p_thru, land_thru, after):
    def body(p_ref, land_ref, ssem, rsem, after_ref, p_dead, got_ref):
        send, recv = _last_copies(p_ref, land_ref, ssem, rsem)
        for cp in send:
            cp.wait_send()
        for cp in recv:
            cp.wait_recv()

    return pl.pallas_call(
        body, name="reduce_last_wait",
        out_shape=(pltpu.HBM(p_thru.shape, p_thru.dtype), pltpu.HBM(land_thru.shape, land_thru.dtype)),
        in_specs=(HBM_REF, HBM_REF, SEM_REF, SEM_REF, pl.BlockSpec(memory_space=pl.ANY)),
        out_specs=(HBM_REF, HBM_REF), input_output_aliases={0: 0, 1: 1},
        compiler_params=pltpu.CompilerParams(has_side_effects=DATAFLOW),
    )(p_thru, land_thru, ssem, rsem, after)


def _share(pairs):
    n = len(pairs)

    def start(ins, outs, sems):
        x, y, c, _ = _place()
        for i in range(n):
            _rcopy(outs[i].at[c], outs[i].at[c], sems[0].at[i], sems[1].at[i], (x, y, 1 - c)).start()

    def finish(ins, outs, sems):
        x, y, c, _ = _place()
        for i in range(n):
            _rcopy(outs[i].at[c], outs[i].at[c], sems[0].at[i], sems[1].at[i], (x, y, 1 - c)).wait_send()
            _rcopy(outs[i].at[1 - c], outs[i].at[1 - c], sems[0].at[i], sems[1].at[i], (x, y, 1 - c)).wait_recv()

    return _Stage(pairs, [_sds(p) for p in pairs], {i: i for i in range(n)},
                  [pltpu.SemaphoreType.DMA((n,)), pltpu.SemaphoreType.DMA((n,))], start, finish)


def _row_block(rows, cols, itemsize=4, target=2 * MIB):
    br = rows
    while br * cols * itemsize > target and br % 32 == 0:
        br //= 2
    return br


def _cast_place(w, chip_idx, name):
    rows, cols = w.shape
    br = _row_block(rows, cols)

    def body(k_ref, w_ref, o_ref):
        o_ref[0] = w_ref[...].astype(BF)

    return _call(
        body, name=name, grid=(rows // br,), prefetch=chip_idx,
        in_specs=[pl.BlockSpec((br, cols), lambda r, k: (r, 0))],
        out_specs=[pl.BlockSpec((1, br, cols), lambda r, k: (k[0], r, 0))],
        out_shape=[jax.ShapeDtypeStruct((NCHIP, rows, cols), BF)], vmem=32, args=[w])[0][0]


def _cast_place_multi(ws, chip_idx, stages=()):
    br = 128
    nblk = [a.shape[0] // br for a in ws]
    starts = [sum(nblk[:i]) for i in range(len(ws))]

    def body(k_ref, *refs):
        r = pl.program_id(0)
        for i in range(len(ws)):
            @pl.when(jnp.logical_and(r >= starts[i], r < starts[i] + nblk[i]))
            def _(i=i):
                refs[len(ws) + i][0] = refs[i][...].astype(BF)

    def at(i):
        return functools.partial(lambda r, s, nb: jnp.clip(r - s, 0, nb - 1), s=starts[i], nb=nblk[i])

    outs, landed = _call(
        body, name="cast_rest", grid=(sum(nblk),), prefetch=chip_idx,
        in_specs=[pl.BlockSpec((br, a.shape[1]), functools.partial(lambda r, k, f: (f(r), 0), f=at(i)))
                  for i, a in enumerate(ws)],
        out_specs=[pl.BlockSpec((1, br, a.shape[1]), functools.partial(lambda r, k, f: (k[0], f(r), 0), f=at(i)))
                   for i, a in enumerate(ws)],
        out_shape=[jax.ShapeDtypeStruct((NCHIP,) + a.shape, BF) for a in ws], vmem=32, args=list(ws), stages=stages)
    return outs, landed


def _add_sibling(g, land, cidx, name, stages=()):
    _, _, hr, cols = g.shape
    br = _row_block(hr, cols)

    def body(c_ref, g_ref, l_ref, o_ref):
        o_ref[...] = (g_ref[0, 0].astype(F32) + l_ref[0].astype(F32)).astype(BF)[None]

    outs, st = _call(
        body, name=name, grid=(NCHIP, hr // br), prefetch=cidx,
        in_specs=[pl.BlockSpec((1, 1, br, cols), lambda k, r, c: (k, c[0], r, 0)),
                  pl.BlockSpec((1, br, cols), lambda k, r, c: (k, r, 0))],
        out_specs=[pl.BlockSpec((1, br, cols), lambda k, r, c: (k, r, 0))],
        out_shape=[jax.ShapeDtypeStruct((NCHIP, hr, cols), BF)], vmem=32, args=[g, land], stages=stages)
    return outs[0], st


def _add_sibling_multi(gs, lands, cidx, name):
    n = len(gs)
    brs = [_row_block(g.shape[2], g.shape[3]) for g in gs]
    nrb = [g.shape[2] // b for g, b in zip(gs, brs)]
    nblk = [NCHIP * q for q in nrb]
    starts = [sum(nblk[:i]) for i in range(n)]

    def body(c_ref, *refs):
        r = pl.program_id(0)
        for i in range(n):
            g_ref, l_ref, o_ref = refs[2 * i], refs[2 * i + 1], refs[2 * n + i]

            @pl.when(jnp.logical_and(r >= starts[i], r < starts[i] + nblk[i]))
            def _():
                o_ref[...] = (g_ref[0, 0].astype(F32) + l_ref[0].astype(F32)).astype(BF)[None]

    def at(i, r):
        q = jnp.clip(r - starts[i], 0, nblk[i] - 1)
        return q // nrb[i], q % nrb[i]

    def g_spec(i):
        return pl.BlockSpec((1, 1, brs[i], gs[i].shape[3]),
                            functools.partial(lambda r, c, i: (at(i, r)[0], c[0], at(i, r)[1], 0), i=i))

    def l_spec(i):
        return pl.BlockSpec((1, brs[i], gs[i].shape[3]),
                            functools.partial(lambda r, c, i: (at(i, r)[0], at(i, r)[1], 0), i=i))

    return _call(
        body, name=name, grid=(sum(nblk),), prefetch=cidx,
        in_specs=[s for i in range(n) for s in (g_spec(i), l_spec(i))], out_specs=[l_spec(i) for i in range(n)],
        out_shape=[jax.ShapeDtypeStruct(l.shape, BF) for l in lands], vmem=32,
        args=[a for i in range(n) for a in (gs[i], lands[i])])[0]


def _add_pair(a, b, name):
    rows, cols = a.shape

    def body(a_ref, b_ref, o_ref):
        o_ref[...] = a_ref[...] + b_ref[...]

    spec = pl.BlockSpec((rows, cols), lambda r: (0, 0))
    return _call(body, name=name, grid=(1,), in_specs=[spec, spec], out_specs=[spec], out_shape=[_sds(a)],
                 vmem=32, args=[a, b])[0][0]


def _add_chips(own, land, idx, name, stages=None):
    _, hr, cols = land.shape
    br = _row_block(hr, cols)

    def body(s_ref, a_ref, b_ref, c_ref, d_ref, o_ref):
        o_ref[...] = (a_ref[...].astype(F32) + b_ref[...].astype(F32)) + (c_ref[...].astype(F32) +
                                                                           d_ref[...].astype(F32))

    spec = lambda q: pl.BlockSpec((1, br, cols), functools.partial(lambda r, s, q: (s[q], r, 0), q=q))
    outs, landed = _call(
        body, name=name, grid=(hr // br,), prefetch=idx,
        in_specs=[spec(0), spec(1), spec(2), spec(3)], out_specs=[spec(4)],
        out_shape=[jax.ShapeDtypeStruct((2, hr, cols), F32)], vmem=48, args=[own, land, land, land],
        stages=stages or ())
    return outs[0] if stages is None else (outs[0], landed)


def _add_chips_multi(owns, lands, idx, name, stages=()):
    n = len(owns)
    brs = [_row_block(l.shape[1], l.shape[2]) for l in lands]
    nblk = [l.shape[1] // b for l, b in zip(lands, brs)]
    starts = [sum(nblk[:i]) for i in range(n)]

    def body(s_ref, *refs):
        r = pl.program_id(0)
        for i in range(n):
            a_ref, b_ref, c_ref, d_ref = refs[4 * i:4 * i + 4]
            o_ref = refs[4 * n + i]

            @pl.when(jnp.logical_and(r >= starts[i], r < starts[i] + nblk[i]))
            def _():
                o_ref[...] = (a_ref[...].astype(F32) + b_ref[...].astype(F32)) + (c_ref[...].astype(F32) +
                                                                                   d_ref[...].astype(F32))

    def spec(i, q):
        return pl.BlockSpec((1, brs[i], lands[i].shape[2]), functools.partial(
            lambda r, s, q, st, nb: (s[q], jnp.clip(r - st, 0, nb - 1), 0), q=q, st=starts[i], nb=nblk[i]))

    outs, landed = _call(
        body, name=name, grid=(sum(nblk),), prefetch=idx,
        in_specs=[spec(i, q) for i in range(n) for q in range(4)], out_specs=[spec(i, 4) for i in range(n)],
        out_shape=[jax.ShapeDtypeStruct((2,) + l.shape[1:], F32) for l in lands], vmem=48,
        args=[a for i in range(n) for a in (owns[i], lands[i], lands[i], lands[i])], stages=stages)
    return outs, landed


def _adamw_math(w, g, m, v):
    mn = ADAM_B1 * m + (1.0 - ADAM_B1) * g
    vn = ADAM_B2 * v + (1.0 - ADAM_B2) * (g * g)
    m_hat = mn / (1.0 - ADAM_B1 ** ADAM_STEP)
    v_hat = vn / (1.0 - ADAM_B2 ** ADAM_STEP)
    return -ADAM_LR * (m_hat / (jnp.sqrt(v_hat) + ADAM_EPS) + ADAM_WD * w), mn, vn


def _adamw(w, g, m, v, name, stages=()):
    rows, cols = w.shape
    br = _row_block(rows, cols)

    def body(w_ref, g_ref, m_ref, v_ref, go_ref, d_ref, mo_ref, vo_ref):
        gv = g_ref[...]
        go_ref[...] = gv
        d_ref[...], mo_ref[...], vo_ref[...] = _adamw_math(w_ref[...], gv, m_ref[...], v_ref[...])

    spec = pl.BlockSpec((br, cols), lambda r: (r, 0))
    return _call(body, name=name, grid=(rows // br,), in_specs=[spec] * 4, out_specs=[spec] * 4,
                 out_shape=[_sds(w)] * 4, vmem=56, args=[w, g, m, v], stages=stages)


def _adamw_multi(names, w, g, m, v, stages=()):
    cols = w[names[0]].shape[1]
    br = 128
    nblk = [w[n].shape[0] // br for n in names]
    starts = [sum(nblk[:i]) for i in range(len(names))]

    def body(*refs):
        r = pl.program_id(0)
        for i in range(len(names)):
            w_ref, g_ref, m_ref, v_ref = refs[4 * i:4 * i + 4]
            go_ref, d_ref, mo_ref, vo_ref = refs[4 * len(names) + 4 * i:4 * len(names) + 4 * i + 4]

            @pl.when(jnp.logical_and(r >= starts[i], r < starts[i] + nblk[i]))
            def _():
                gv = g_ref[...]
                go_ref[...] = gv
                d_ref[...], mo_ref[...], vo_ref[...] = _adamw_math(w_ref[...], gv, m_ref[...], v_ref[...])

    def spec(i):
        return pl.BlockSpec((br, cols), functools.partial(
            lambda r, s, nb: (jnp.clip(r - s, 0, nb - 1), 0), s=starts[i], nb=nblk[i]))

    outs, landed = _call(
        body, name="adamw_" + "_".join(names), grid=(sum(nblk),),
        in_specs=[spec(i) for i in range(len(names)) for _ in range(4)],
        out_specs=[spec(i) for i in range(len(names)) for _ in range(4)],
        out_shape=[_sds(w[n]) for n in names for _ in range(4)], vmem=56,
        args=[a[n] for n in names for a in (w, g, m, v)], stages=stages)
    return {n: outs[4 * i:4 * i + 4] for i, n in enumerate(names)}, landed


def _to_everyone(v):
    deltas = [(a, b, e) for a in (0, 1) for b in (0, 1) for e in (0, 1)][1:]

    def copies(ins, outs, sems):
        x, y, c, _ = _place()
        me = 4 * x + 2 * y + c
        flip = lambda p, f: 1 - p if f else p
        return [_rcopy(ins[0], outs[0].at[me], sems[0].at[q], sems[1].at[q], (flip(x, a), flip(y, b), flip(c, e)))
                for q, (a, b, e) in enumerate(deltas)]

    def start(ins, outs, sems):
        for cp in copies(ins, outs, sems):
            cp.start()

    def finish(ins, outs, sems):
        for cp in copies(ins, outs, sems):
            cp.wait()

    n = len(deltas)
    return _Stage([v], [jax.ShapeDtypeStruct((2 * NCHIP,) + v.shape, v.dtype)], {},
                  [pltpu.SemaphoreType.DMA((n,)), pltpu.SemaphoreType.DMA((n,))], start, finish)


SMALL_AT = {"norm_mix_pre": (0, 1, D), "norm_mix_post": (1, 1, D), "norm_mlp_pre": (2, 1, D),
            "norm_mlp_post": (3, 1, D), "b_gate": (4, 2, D), "conv_b": (6, 1, D), "lru_b_a": (7, 1, D),
            "lru_b_x": (8, 1, D), "lru_lambda": (9, 1, D), "pool_scale": (10, 1, DP)}
SMALL_SEPARATE = ["conv_w", "lru_w_a", "lru_w_x", "pool_w"]


def _adamw_small(small_sum, first_all, sep_grads, w, m, v):
    packed, sep = list(SMALL_AT), list(SMALL_SEPARATE)
    names = packed + sep

    def body(*refs):
        s_ref, a_ref, refs = refs[0], refs[1], refs[2:]
        g_sep, refs = refs[:len(sep)], refs[len(sep):]
        nn = len(names)
        w_r, m_r, v_r, refs = refs[:nn], refs[nn:2 * nn], refs[2 * nn:3 * nn], refs[3 * nn:]
        g_out, refs = refs[:len(packed)], refs[len(packed):]
        d_o, m_o, v_o = refs[:nn], refs[nn:2 * nn], refs[2 * nn:3 * nn]
        for i, n in enumerate(names):
            if i == 0:
                g = a_ref[0:1, :]
                for q in range(1, 2 * NCHIP):
                    g = g + a_ref[q:q + 1, :]
                g_out[i][...] = g
            elif n in SMALL_AT:
                r0, nr, nc = SMALL_AT[n]
                g = jnp.concatenate([s_ref[r0 + q:r0 + q + 1, :nc] for q in range(nr)], axis=1)
                g_out[i][...] = g
            else:
                g = g_sep[i - len(packed)][...]
            d_o[i][...], m_o[i][...], v_o[i][...] = _adamw_math(w_r[i][...], g, m_r[i][...], v_r[i][...])

    ws = [w[n] for n in names]
    res = pl.pallas_call(
        body, name="adamw_small",
        out_shape=[_sds(w[n]) for n in packed] + [_sds(a) for a in ws] * 3,
        compiler_params=_cp(32),
    )(*_hbm(small_sum, first_all, *sep_grads, *ws, *[m[n] for n in names], *[v[n] for n in names]))
    nn, npk = len(names), len(packed)
    grad = dict(zip(packed, res[:npk]))
    delta = dict(zip(names, res[npk:npk + nn]))
    new_m = dict(zip(names, res[npk + nn:npk + 2 * nn]))
    new_v = dict(zip(names, res[npk + 2 * nn:]))
    return grad, delta, new_m, new_v


W_NAMES = ["norm_mix_pre", "norm_mix_post", "norm_mlp_pre", "norm_mlp_post", "w_in", "b_gate", "conv_w", "conv_b",
           "lru_w_a", "lru_b_a", "lru_w_x", "lru_b_x", "lru_lambda", "pool_w", "pool_scale", "w_lru_up",
           "w_pool_up", "w_o", "w_ff1", "w_ff2"]
BIG = ["w_in", "w_lru_up", "w_pool_up", "w_o", "w_ff1", "w_ff2"]


def _block_diag(w):
    hd = w.shape[-1]
    per = CB // hd
    w4 = w.reshape(NG, per, hd, hd)
    eye = jnp.eye(per, dtype=w.dtype)
    return jnp.einsum("gpij,pq->gpiqj", w4, eye).reshape(NG, CB, CB)


def _block_diag_extract(d, hd):
    per = CB // hd
    d5 = d.reshape(NG, per, hd, per, hd)
    return jnp.stack([d5[:, p, :, p, :] for p in range(per)], axis=1).reshape(NG * per, hd, hd)


def _halves(g):
    return g.reshape(NCHIP, 2, g.size // (g.shape[-1] * 2 * NCHIP), g.shape[-1])


def kernel(x, norm_mix_pre, norm_mix_post, norm_mlp_pre, norm_mlp_post, w_in, b_gate, conv_w, conv_b, lru_w_a, lru_b_a, lru_w_x, lru_b_x, lru_lambda, pool_w, pool_scale, w_lru_up, w_pool_up, w_o, w_ff1, w_ff2, loss_target, m_norm_mix_pre, m_norm_mix_post, m_norm_mlp_pre, m_norm_mlp_post, m_w_in, m_b_gate, m_conv_w, m_conv_b, m_lru_w_a, m_lru_b_a, m_lru_w_x, m_lru_b_x, m_lru_lambda, m_pool_w, m_pool_scale, m_w_lru_up, m_w_pool_up, m_w_o, m_w_ff1, m_w_ff2, v_norm_mix_pre, v_norm_mix_post, v_norm_mlp_pre, v_norm_mlp_post, v_w_in, v_b_gate, v_conv_w, v_conv_b, v_lru_w_a, v_lru_b_a, v_lru_w_x, v_lru_b_x, v_lru_lambda, v_pool_w, v_pool_scale, v_w_lru_up, v_w_pool_up, v_w_o, v_w_ff1, v_w_ff2):
    args = dict(locals())
    two_d = lambda a: a.reshape(-1, a.shape[-1])
    w = {n: two_d(args[n]) for n in W_NAMES}
    mom = {n: two_d(args["m_" + n]) for n in W_NAMES}
    var = {n: two_d(args["v_" + n]) for n in W_NAMES}
    i32 = lambda val: jnp.asarray(val, jnp.int32)
    chip = i32(2 * lax.axis_index("x") + lax.axis_index("y"))
    core = i32(lax.axis_index("c"))
    cidx = core.reshape(1)
    zero = i32(0)
    hd = lru_w_a.shape[-1]
    xs, target = x[0], loss_target[0]
    g1, g2, g3, g4 = norm_mix_pre, norm_mix_post, norm_mlp_pre, norm_mlp_post

    mix = ["w_lru_up", "w_pool_up", "w_o"]
    full = {"w_in": _cast_place(w["w_in"], chip.reshape(1), "cast_w_in")}
    (fl_in, fl_conv), first = _split_call("gather_start_first", start=[
        _gather([full["w_in"]], ici=[(0, ALL)]), _gather_whole(w["conv_w"])])
    casts, _ = _cast_place_multi([w[n] for n in BIG[1:]], chip.reshape(1), stages=[_after(first)])
    full.update(zip(BIG[1:], casts))
    (fl_mix, fl_ff1, fl_ff2), started = _split_call("gather_start_rest", start=[
        _gather([full[n] for n in mix], ici=[(0, ALL), (1, ALL), (2, ALL)]),
        _gather([full["w_ff1"]], ici=[(0, ALL)]), _gather([full["w_ff2"]], ici=[(0, ALL)])])
    wa = _block_diag(lru_w_a[0]).astype(BF)
    wx = _block_diag(lru_w_x[0]).astype(BF)
    pw = pool_w[0].astype(BF)

    def to_sibling(name, flight, after=None):
        (fl,), passed = _split_call(name + "_pass", finish=[flight], after=after,
                                    start=[_gather(flight.landed(), d2d=[(i, ALL) for i in range(len(flight.bufs))])])
        passed_on.append(passed)
        return fl

    passed_on = []

    def arrived(name, flight, after=None):
        _split_call(name + "_done", finish=[flight], after=after)
        return flight.landed()

    idx_big = jnp.stack([chip, (chip + 1) % NCHIP, (chip + 2) % NCHIP, (chip + 3) % NCHIP, core])
    proj, h1 = _fwd_inproj_own(xs, g1, fl_in.bufs[0], idx_big, stages=[_after(started)])
    fl_in = to_sibling("gather_w_in", fl_in, after=h1)
    w_in_f, = arrived("gather_w_in", fl_in)
    conv_all, = arrived("gather_conv", fl_conv)
    full["w_in"] = w_in_f
    conv_all = lax.dynamic_update_slice(conv_all, w["conv_w"][None], (chip, zero, zero))
    conv_full = jnp.transpose(conv_all, (1, 0, 2)).reshape(4, DR)
    proj = _fwd_inproj_rest(h1, w_in_f, proj, idx_big)
    fl_mix = to_sibling("gather_mix", fl_mix, after=proj)
    (ylru, hs), _ = _fwd_lru(proj, conv_full, conv_b, wa, lru_b_a, wx, lru_b_x, lru_lambda,
                             stages=[_after(passed_on[-1])])
    got = arrived("gather_mix", fl_mix, after=ylru)
    fl_ff1 = to_sibling("gather_ff1", fl_ff1, after=ylru)
    w_lru_up_f, w_pool_up_f, w_o_f = got[0].reshape(DR, D), got[1], got[2].reshape(D, D)
    ypool = _fwd_pool(proj, pw, pool_scale)
    (x2, h2, m, mrg, bra, brb), _ = _fwd_merge(xs, ylru, ypool, proj, b_gate, g2, g3, w_lru_up_f, w_pool_up_f, w_o_f,
                                               stages=[_after(passed_on[-1])])
    fl_ff2 = to_sibling("gather_ff2", fl_ff2, after=h2)
    ff1, = arrived("gather_ff1", fl_ff1, after=h2)
    ff2, = arrived("gather_ff2", fl_ff2)
    ff2 = ff2.reshape(DF, D)
    a1, f = _fwd_mlp(h2, ff1, ff2)
    lossp, dy, df, dg4 = _loss_head(f, x2, target, g4)

    dh2, df1 = _bwd_mlp_x(df, a1, ff1, ff2)
    dw_ff1, dw_ff2 = _bwd_mlp_w(df, h2, a1, df1)
    g_ff = [_halves(dw_ff1), _halves(dw_ff2)]
    (dxres, dgates, dylru, dypool, dm, dbra, dbrb, dg2, dg3, dbg), (l_ff,) = _bwd_merge(
        dh2, dy, x2, m, bra, brb, proj, b_gate, g2, g3, w_lru_up_f, w_pool_up_f, w_o_f, stages=[_to_sibling(g_ff)])
    p_ff = _add_sibling_multi(g_ff, l_ff, cidx, "add_sibling_ff")
    (fl_ff,), sent_ff = _split_call("reduce_ff_start", start=[_to_chips(p_ff)])
    (dw_o, dw_lru_up, dw_pool_up), _ = _dw_merge(mrg, dm, ylru, dbra, ypool, dbrb, stages=[_after(sent_ff)])
    g_mix = [_halves(dw_lru_up), _halves(dw_pool_up), _halves(dw_o)]
    (dxp, dgl, dcw, dcb, dwa, dba, dwx, dbx, dlam), (l_mix,) = _bwd_lru(
        proj, hs, dylru, conv_full, conv_b, wa, lru_b_a, wx, lru_b_x, lru_lambda, stages=[_to_sibling(g_mix)])
    p_mix = _add_sibling_multi(g_mix, l_mix, cidx, "add_sibling_mix")
    dxpool, dpw, dsc = _bwd_pool(proj, dypool, pw, pool_scale)
    dproj = [dxp, dgl, dxpool, dgates]
    small = jnp.concatenate([
        jnp.zeros((1, D), F32), dg2, dg3, dg4, dbg.reshape(2, D), dcb, dba, dbx, dlam,
        jnp.pad(dsc, ((0, 0), (0, D - DP))), jnp.pad(lossp, ((0, 0), (0, D - 1))), dcw,
        _block_diag_extract(dwa, hd).reshape(-1, D), _block_diag_extract(dwx, hd).reshape(-1, D),
        dpw.reshape(-1, D)], axis=0)
    (fl_mixr, fl_smalls), sent_mix = _split_call("reduce_mix_start", start=[_to_chips(p_mix), _to_sibling([small])])
    dw_in = _bwd_inproj_w(h1, dproj, sent_mix)
    _split_call("reduce_small_sibling_done", finish=[fl_smalls], after=dw_in)
    small, l_small = fl_smalls.bufs
    small2 = _add_pair(small, l_small, "add_sibling_small").reshape(2, SMALL_ROWS // 2, D)
    g_in = _halves(dw_in)
    done = ["w_ff1", "w_ff2"] + mix
    (fl_gin, fl_small), sib_started = _split_call("reduce_in_sibling_start",
                                                  start=[_to_sibling([g_in]), _to_chips([small2])])
    _, chips_done = _split_call("reduce_chips_done", finish=[fl_ff, fl_mixr], after=sib_started)
    p_ff1, p_ff2, c_ff1, c_ff2 = fl_ff.bufs
    p_mix, c_mix = fl_mixr.bufs[:3], fl_mixr.bufs[3:]
    pairs, _ = _add_chips_multi([p_ff1, p_ff2] + p_mix, [c_ff1, c_ff2] + c_mix, idx_big, "add_chips_done",
                                stages=[_after(chips_done)])
    _split_call("reduce_in_sibling_done", finish=[fl_gin], after=pairs[-1])
    g_in, l_in = fl_gin.bufs
    p_in = _add_sibling(g_in, l_in, cidx, "add_sibling_w_in")[0]
    ssem, rsem, p_in, c_in, token = _chips_start(p_in)
    _split_call("reduce_small_done", finish=[fl_small], after=token)
    small2, c_small = fl_small.bufs
    own_small = lax.dynamic_index_in_dim(small2, core, 0, keepdims=True)
    c_small = lax.dynamic_update_slice(c_small, own_small, (chip, zero, zero))
    pair_small = _add_chips(c_small, c_small, jnp.stack([zero, zero + 1, zero + 2, zero + 3, core]), "add_chips_small")
    (fl_share,), shared_start = _split_call("reduce_share_start", start=[_share(pairs + [pair_small])])
    grad_x, dg1 = _bwd_inproj_x(dproj, full["w_in"], xs, dxres, g1, stages=[_after(shared_start)])
    _split_call("reduce_share_done", finish=[fl_share], after=dg1)
    shared = fl_share.landed()
    pairs, pair_small = shared[:-1], shared[-1]

    grads, delta, new_m, new_v = {}, {}, {}, {}
    for n, p in zip(done, pairs):
        grads[n] = p.reshape(-1, p.shape[-1])

    def update(n, stages=()):
        (grads[n], delta[n], new_m[n], new_v[n]), landed = _adamw(w[n], grads[n], mom[n], var[n], "adamw_" + n,
                                                                  stages=stages)
        return landed

    p_in, c_in = _chips_wait(ssem, rsem, p_in, c_in, pairs[0])
    pair_in = _add_chips(p_in, c_in, idx_big, "add_chips_w_in")
    (fl_last, fl_dg1), last_start = _split_call("reduce_last_share_start", start=[_share([pair_in]), _to_everyone(dg1)])
    updated, _ = _adamw_multi(["w_ff1", "w_ff2", "w_o", "w_lru_up"], w, grads, mom, var, stages=[_after(last_start)])
    for n, (go, d, mo, vo) in updated.items():
        grads[n], delta[n], new_m[n], new_v[n] = go, d, mo, vo
    _split_call("reduce_last_share_done", finish=[fl_last, fl_dg1], after=new_v["w_lru_up"])
    (pair_in,), (dg1, dg1_all) = fl_last.landed(), fl_dg1.bufs
    dg1_all = lax.dynamic_update_slice(dg1_all, dg1[None], (2 * chip + core, zero, zero)).reshape(2 * NCHIP, D)
    grads["w_in"] = pair_in.reshape(-1, pair_in.shape[-1])
    update("w_pool_up")
    update("w_in")
    small_sum = pair_small.reshape(SMALL_ROWS, D)
    loss = 0.5 * small_sum[LOSS_ROW, 0]
    ccols = DR // NCHIP
    sep = [lax.dynamic_slice(small_sum[12:16], (zero, chip * ccols), (4, ccols)),
           small_sum[16:80].reshape(-1, hd), small_sum[80:144].reshape(-1, hd), small_sum[144:208].reshape(-1, PG)]
    g_s, d_s, m_s, v_s = _adamw_small(small_sum, dg1_all, sep, w, mom, var)
    grads.update(g_s)
    grads.update(dict(zip(SMALL_SEPARATE, sep)))
    delta.update(d_s)
    new_m.update(m_s)
    new_v.update(v_s)

    out = lambda d: [d[n].reshape(args[n].shape) for n in W_NAMES]
    return (loss, grad_x[None], *out(grads), *out(delta), *out(new_m), *out(new_v))
```

```python
import functools
import math

import jax
import jax.numpy as jnp
from jax import lax
from jax.experimental import pallas as pl
from jax.experimental.pallas import tpu as pltpu

F32 = jnp.float32
BF = jnp.bfloat16

T = 2048
D = 1024
DR = 1024
DP = 512
DF = 4096
DIN = 4608
NCHIP = 4
CW_IN = DIN // NCHIP
LANE = 128
CB = 128
NG = DR // CB
PG = 128
POOL_WINDOWS = (2, 4, 8, 16)
NORM_EPS = 1e-6
LRU_C = 8.0
GELU_C = math.sqrt(2.0 / math.pi)
ADAM_LR = 0.001
ADAM_B1 = 0.9
ADAM_B2 = 0.999
ADAM_EPS = 1e-08
ADAM_WD = 0.01
ADAM_STEP = 10
MESH_ID = pl.DeviceIdType.MESH
ANY = pl.BlockSpec(memory_space=pl.ANY)
SMALL_ROWS = 208
LOSS_ROW = 11
MIB = 1 << 20


def _cp(vmem_mib=None):
    if vmem_mib is None:
        return pltpu.CompilerParams()
    return pltpu.CompilerParams(vmem_limit_bytes=vmem_mib * MIB)


def _hbm(*arrays):
    return [pltpu.with_memory_space_constraint(a, pltpu.HBM) for a in arrays]


def _hbm_out(shapes):
    return [pltpu.HBM(s.shape, s.dtype) for s in shapes]


class _Stage:
    def __init__(self, operands, out_shape, alias, sems, start, finish):
        self.operands, self.out_shape, self.alias, self.sems = list(operands), list(out_shape), dict(alias), list(sems)
        self.start, self.finish = start, finish


def _call(body, *, name, grid, in_specs, out_specs, out_shape, args, vmem=None, stages=(), prefetch=None,
          scratch=()):
    nin, nout = len(in_specs), len(out_specs)
    npre = 0 if prefetch is None else 1
    st_args, st_shapes, st_sems, aliases = [], [], list(scratch), {}
    for st in stages:
        for k, v in st.alias.items():
            aliases[npre + nin + len(st_args) + k] = nout + len(st_shapes) + v
        st_args += st.operands
        st_shapes += st.out_shape
        st_sems += st.sems

    def wrapped(*refs):
        pre, refs = refs[:npre], refs[npre:]
        ins, pos = refs[:nin], nin
        st_ins = []
        for st in stages:
            st_ins.append(refs[pos:pos + len(st.operands)])
            pos += len(st.operands)
        outs, pos = refs[pos:pos + nout], pos + nout
        st_outs = []
        for st in stages:
            st_outs.append(refs[pos:pos + len(st.out_shape)])
            pos += len(st.out_shape)
        work, pos = refs[pos:pos + len(scratch)], pos + len(scratch)
        sems = []
        for st in stages:
            sems.append(refs[pos:pos + len(st.sems)])
            pos += len(st.sems)
        if stages:
            first = functools.reduce(jnp.logical_and, [pl.program_id(a) == 0 for a in range(len(grid))])

            @pl.when(first)
            def _():
                for st, a, b, s in zip(stages, st_ins, st_outs, sems):
                    st.start(a, b, s)

        body(*pre, *ins, *outs, *work)
        if stages:
            last = functools.reduce(jnp.logical_and, [pl.program_id(a) == g - 1 for a, g in enumerate(grid)])

            @pl.when(last)
            def _():
                for st, a, b, s in zip(stages, st_ins, st_outs, sems):
                    st.finish(a, b, s)

    all_in = list(in_specs) + [ANY] * len(st_args)
    all_out = list(out_specs) + [ANY] * len(st_shapes)
    kw = dict(has_side_effects=True) if stages else {}
    if vmem is not None:
        kw["vmem_limit_bytes"] = vmem * MIB
    if prefetch is None:
        gkw = dict(grid=grid, in_specs=all_in, out_specs=all_out, scratch_shapes=st_sems)
    else:
        gkw = dict(grid_spec=pltpu.PrefetchScalarGridSpec(
            num_scalar_prefetch=1, grid=grid, in_specs=all_in, out_specs=all_out, scratch_shapes=st_sems))
    res = pl.pallas_call(
        wrapped, name=name, out_shape=_hbm_out(list(out_shape) + st_shapes), input_output_aliases=aliases,
        compiler_params=pltpu.CompilerParams(**kw), **gkw,
    )(*([prefetch] if npre else []), *_hbm(*args, *st_args))
    outs, rest, st_res = list(res[:nout]), list(res[nout:]), []
    for st in stages:
        st_res.append(rest[:len(st.out_shape)])
        rest = rest[len(st.out_shape):]
    return outs, st_res


def _mm(a, b):
    return jnp.dot(a.astype(BF), b.astype(BF), preferred_element_type=F32)


def _mm_nt(a, b):
    return lax.dot_general(a.astype(BF), b.astype(BF), (((1,), (1,)), ((), ())),
                           preferred_element_type=F32)


def _mm_tn(a, b):
    return lax.dot_general(a.astype(BF), b.astype(BF), (((0,), (0,)), ((), ())),
                           preferred_element_type=F32)


def _rows(v):
    return lax.broadcasted_iota(jnp.int32, v.shape, 0)


def _sd(v, s, fill=0.0):
    return jnp.where(_rows(v) >= s, pltpu.roll(v, s, axis=0), fill)


def _su(v, s, fill=0.0):
    n = v.shape[0]
    return jnp.where(_rows(v) < n - s, pltpu.roll(v, n - s, axis=0), fill)


def _sigmoid(z):
    return 1.0 / (1.0 + jnp.exp(-z))


def _softplus(z):
    e = jnp.exp(-jnp.abs(z))
    u = 1.0 + e
    d = u - 1.0
    log1p = jnp.where(d == 0.0, e, jnp.log(u) * (e / jnp.where(d == 0.0, 1.0, d)))
    return jnp.maximum(z, 0.0) + log1p


def _mean(v):
    return jnp.mean(v, axis=-1, keepdims=True)


def _colsum(v):
    return jnp.sum(v, axis=0, keepdims=True)


def _acc(ref, val, first):
    @pl.when(first)
    def _():
        ref[...] = val

    @pl.when(jnp.logical_not(first))
    def _():
        ref[...] += val


def _conv(xp, cw, cb):
    x1, x2, x3 = _sd(xp, 1), _sd(xp, 2), _sd(xp, 3)
    xc = cb + cw[0:1] * x3 + cw[1:2] * x2 + cw[2:3] * x1 + cw[3:4] * xp
    return xc, x1, x2, x3


def _lru_gates(xc, wa, ba, wx, bx, lam):
    xcb = xc.astype(BF)
    r = _sigmoid(_mm(xcb, wa) + ba)
    ii = _sigmoid(_mm(xcb, wx) + bx)
    sp = _softplus(-lam)
    la = (-LRU_C) * r * sp
    a = jnp.exp(la)
    mult = jnp.sqrt(-jnp.tanh(la) * (a * a + 1.0))
    return xcb, r, ii, sp, a, mult


def _gelu_parts(g):
    th = jnp.tanh(GELU_C * (g + 0.044715 * (g * g * g)))
    gel = 0.5 * g * (1.0 + th)
    dgel = 0.5 * (1.0 + th) + 0.5 * g * (1.0 - th * th) * (GELU_C * (1.0 + 3.0 * 0.044715 * (g * g)))
    return gel, dgel


def _tile_scan(a, b, a_s, b_s, out_ref, reverse):
    n, lanes = a.shape
    nt = n // 8
    a, b = a.reshape(nt, 8, lanes), b.reshape(nt, 8, lanes)
    sub = lax.broadcasted_iota(jnp.int32, a.shape, 1)
    s = 1
    while s < 8:
        keep = sub < 8 - s if reverse else sub >= s
        amount = 8 - s if reverse else s
        b = b + a * jnp.where(keep, pltpu.roll(b, amount, axis=1), 0.0)
        a = a * jnp.where(keep, pltpu.roll(a, amount, axis=1), 1.0)
        s *= 2
    a_s[...] = a.reshape(n, lanes)
    b_s[...] = b.reshape(n, lanes)
    edge = pl.ds(0 if reverse else 7, nt, stride=8)
    ta, tb = a_s[edge, :], b_s[edge, :]
    shift = _su if reverse else _sd
    s = 1
    while s < nt:
        tb = tb + ta * shift(tb, s, 0.0)
        if 2 * s < nt:
            ta = ta * shift(ta, s, 1.0)
        s *= 2
    enters = shift(tb, 1, 0.0)
    for o in range(8):
        rows = pl.ds(o, nt, stride=8)
        out_ref[rows, :] = b_s[rows, :] + a_s[rows, :] * enters


def _pool_window(x, steps, shift):
    s, sh = x, 1
    for _ in range(steps):
        s = s + shift(s, sh)
        sh *= 2
    return s


def _fwd_inproj_own(x, g1, w_in, slots, stages=()):
    tm = 512

    def body(s_ref, x_ref, g_ref, w_ref, proj_ref, h_ref):
        xv = x_ref[...]
        r = lax.rsqrt(_mean(xv * xv) + NORM_EPS)
        h = ((xv * r) * g_ref[...]).astype(BF)
        h_ref[...] = h
        proj_ref[...] = jnp.dot(h, w_ref[0], preferred_element_type=F32)

    return _call(
        body, name="fwd_inproj_own", grid=(T // tm,), prefetch=slots,
        in_specs=[pl.BlockSpec((tm, D), lambda i, s: (i, 0)),
                  pl.BlockSpec((1, D), lambda i, s: (0, 0)),
                  pl.BlockSpec((1, D, CW_IN), lambda i, s: (s[0], 0, 0))],
        out_specs=[pl.BlockSpec((tm, CW_IN), lambda i, s: (i, s[0])),
                   pl.BlockSpec((tm, D), lambda i, s: (i, 0))],
        out_shape=[jax.ShapeDtypeStruct((T, DIN), F32), jax.ShapeDtypeStruct((T, D), BF)],
        vmem=40, args=[x, g1, w_in], stages=stages)[0]


def _fwd_inproj_rest(h1, w_in, proj, slots):
    tm = 1024

    def body(s_ref, h_ref, w_ref, p_in, proj_ref):
        proj_ref[...] = jnp.dot(h_ref[...], w_ref[0], preferred_element_type=F32)

    res = pl.pallas_call(
        body, name="fwd_inproj_rest",
        grid_spec=pltpu.PrefetchScalarGridSpec(
            num_scalar_prefetch=1, grid=(NCHIP - 1, T // tm),
            in_specs=[pl.BlockSpec((tm, D), lambda k, i, s: (i, 0)),
                      pl.BlockSpec((1, D, CW_IN), lambda k, i, s: (s[1 + k], 0, 0)), ANY],
            out_specs=pl.BlockSpec((tm, CW_IN), lambda k, i, s: (i, s[1 + k]))),
        out_shape=pltpu.HBM((T, DIN), F32), input_output_aliases={3: 0},
        compiler_params=_cp(40),
    )(slots, *_hbm(h1, w_in, proj))
    return res


def _vec_spec():
    return pl.BlockSpec((1, CB), lambda j: (0, j))


def _fwd_lru(proj, conv_w, conv_b, wa, ba, wx, bx, lam, stages=()):
    def body(xp_ref, g_ref, cw_ref, cb_ref, wa_ref, ba_ref, wx_ref, bx_ref, lam_ref, y_ref, h_ref, a_s, b_s):
        xc, _, _, _ = _conv(xp_ref[...], cw_ref[...], cb_ref[...])
        _, _, ii, _, a, mult = _lru_gates(xc, wa_ref[0], ba_ref[...], wx_ref[0], bx_ref[...], lam_ref[...])
        _tile_scan(a, mult * (ii * xc), a_s, b_s, h_ref, reverse=False)
        gel, _ = _gelu_parts(g_ref[...])
        y_ref[...] = (h_ref[...] * gel).astype(BF)

    return _call(
        body, name="fwd_lru", grid=(NG,),
        in_specs=[pl.BlockSpec((T, CB), lambda j: (0, j)),
                  pl.BlockSpec((T, CB), lambda j: (0, NG + j)),
                  pl.BlockSpec((4, CB), lambda j: (0, j)),
                  _vec_spec(),
                  pl.BlockSpec((1, CB, CB), lambda j: (j, 0, 0)), _vec_spec(),
                  pl.BlockSpec((1, CB, CB), lambda j: (j, 0, 0)), _vec_spec(),
                  _vec_spec()],
        out_specs=[pl.BlockSpec((T, CB), lambda j: (0, j)), pl.BlockSpec((T, CB), lambda j: (0, j))],
        out_shape=[jax.ShapeDtypeStruct((T, DR), BF), jax.ShapeDtypeStruct((T, DR), F32)],
        vmem=48, args=[proj, proj, conv_w, conv_b, wa, ba, wx, bx, lam], stages=stages,
        scratch=[pltpu.VMEM((T, CB), F32)] * 2)


def _pool_cnt(w):
    t = lax.broadcasted_iota(jnp.int32, (T, 1), 0)
    return jnp.minimum(t + 1, w).astype(F32)


def _fwd_pool(proj, pool_w, pool_scale):
    def body(xp_ref, pw_ref, sc_ref, y_ref):
        for g, w in enumerate(POOL_WINDOWS):
            cols = slice(g * PG, (g + 1) * PG)
            x = xp_ref[:, cols]
            p = _pool_window(x, g + 1, _sd) / _pool_cnt(w) - x
            y_ref[:, cols] = (_mm(p, pw_ref[g]) * sc_ref[:, cols]).astype(BF)

    return pl.pallas_call(
        body, name="fwd_pool", grid=(1,),
        in_specs=[pl.BlockSpec((T, DP), lambda i: (0, 2 * DR // DP)),
                  pl.BlockSpec((4, PG, PG), lambda i: (0, 0, 0)),
                  pl.BlockSpec((1, DP), lambda i: (0, 0))],
        out_specs=pl.BlockSpec((T, DP), lambda i: (0, 0)),
        out_shape=pltpu.HBM((T, DP), BF),
        compiler_params=_cp(48),
    )(*_hbm(proj, pool_w, pool_scale))


GATE_BLK = 512
GATE_BLK0 = (2 * DR + DP) // GATE_BLK


def _gate_specs(tm):
    return [pl.BlockSpec((tm, GATE_BLK), functools.partial(lambda i, q: (i, GATE_BLK0 + q), q=q))
            for q in range(4)]


def _fwd_merge(x, ylru, ypool, proj, b_gate, g2, g3, w_lru_up, w_pool_up, w_o, stages=()):
    tm = 512

    def body(x_ref, yl_ref, yp_ref, p0, p1, p2, p3, bg_ref, g2_ref, g3_ref, wl_ref, wp_ref, wo_ref,
             x2_ref, h2_ref, m_ref, mrg_ref, bra_ref, brb_ref):
        bra = jnp.dot(yl_ref[...], wl_ref[...], preferred_element_type=F32)
        yp = yp_ref[...]
        brb = jnp.concatenate([jnp.dot(yp, wp_ref[k], preferred_element_type=F32) for k in range(NCHIP)], axis=1)
        bg = bg_ref[...]
        ga = _sigmoid(jnp.concatenate([p0[...], p1[...]], axis=1) + bg[:, :D])
        gb = _sigmoid(jnp.concatenate([p2[...], p3[...]], axis=1) + bg[:, D:])
        mrg = (ga * bra + gb * brb).astype(BF)
        m = jnp.dot(mrg, wo_ref[...], preferred_element_type=F32)
        r2 = lax.rsqrt(_mean(m * m) + NORM_EPS)
        x2 = x_ref[...] + (m * r2) * g2_ref[...]
        r3 = lax.rsqrt(_mean(x2 * x2) + NORM_EPS)
        x2_ref[...] = x2
        h2_ref[...] = ((x2 * r3) * g3_ref[...]).astype(BF)
        m_ref[...] = m
        mrg_ref[...] = mrg
        bra_ref[...] = bra.astype(BF)
        brb_ref[...] = brb.astype(BF)

    row = lambda w: pl.BlockSpec((tm, w), lambda i: (i, 0))
    full2 = lambda a, b: pl.BlockSpec((a, b), lambda i: (0, 0))
    return _call(
        body, name="fwd_merge", grid=(T // tm,),
        in_specs=[row(D), row(DR), row(DP)] + _gate_specs(tm) +
                 [full2(1, 2 * D), full2(1, D), full2(1, D), full2(DR, D),
                  pl.BlockSpec((NCHIP, DP, D // NCHIP), lambda i: (0, 0, 0)), full2(D, D)],
        out_specs=[row(D)] * 6,
        out_shape=[jax.ShapeDtypeStruct((T, D), F32), jax.ShapeDtypeStruct((T, D), BF),
                   jax.ShapeDtypeStruct((T, D), F32), jax.ShapeDtypeStruct((T, D), BF),
                   jax.ShapeDtypeStruct((T, D), BF), jax.ShapeDtypeStruct((T, D), BF)],
        vmem=48, args=[x, ylru, ypool, proj, proj, proj, proj, b_gate, g2, g3, w_lru_up, w_pool_up, w_o],
        stages=stages)


def _fwd_mlp(h2, w_ff1, w_ff2):
    tm = 512
    fk = DF // NCHIP

    def body(h_ref, w1_ref, w2_ref, a1_ref, f_ref):
        h = h_ref[...]
        f = None
        for k in range(NCHIP):
            a1 = jnp.maximum(jnp.dot(h, w1_ref[k], preferred_element_type=F32), 0.0)
            a1_ref[:, k * fk:(k + 1) * fk] = a1.astype(BF)
            part = jnp.dot((a1 * a1).astype(BF), w2_ref[k * fk:(k + 1) * fk, :], preferred_element_type=F32)
            f = part if f is None else f + part
        f_ref[...] = f

    return pl.pallas_call(
        body, name="fwd_mlp", grid=(T // tm,),
        in_specs=[pl.BlockSpec((tm, D), lambda i: (i, 0)),
                  pl.BlockSpec((NCHIP, D, fk), lambda i: (0, 0, 0)),
                  pl.BlockSpec((DF, D), lambda i: (0, 0))],
        out_specs=[pl.BlockSpec((tm, DF), lambda i: (i, 0)), pl.BlockSpec((tm, D), lambda i: (i, 0))],
        out_shape=_hbm_out([jax.ShapeDtypeStruct((T, DF), BF), jax.ShapeDtypeStruct((T, D), F32)]),
        compiler_params=_cp(56),
    )(*_hbm(h2, w_ff1, w_ff2))


def _loss_head(f, x2, target, g4):
    tm = 512

    def body(f_ref, x2_ref, t_ref, g_ref, loss_ref, dy_ref, df_ref, dg_ref):
        first = pl.program_id(0) == 0
        f = f_ref[...]
        g4v = g_ref[...]
        r4 = lax.rsqrt(_mean(f * f) + NORM_EPS)
        fn = f * r4
        e = (x2_ref[...] + fn * g4v) - t_ref[...]
        _acc(loss_ref, jnp.sum(_mean(e * e), axis=0, keepdims=True), first)
        dy = e * (1.0 / D)
        dy_ref[...] = dy
        _acc(dg_ref, _colsum(dy * fn), first)
        dfn = dy * g4v
        df_ref[...] = (r4 * (dfn - fn * _mean(dfn * fn))).astype(BF)

    row = pl.BlockSpec((tm, D), lambda i: (i, 0))
    return pl.pallas_call(
        body, name="loss_head", grid=(T // tm,),
        in_specs=[row, row, row, pl.BlockSpec((1, D), lambda i: (0, 0))],
        out_specs=[pl.BlockSpec((1, 1), lambda i: (0, 0)), row, row, pl.BlockSpec((1, D), lambda i: (0, 0))],
        out_shape=_hbm_out([jax.ShapeDtypeStruct((1, 1), F32), jax.ShapeDtypeStruct((T, D), F32),
                            jax.ShapeDtypeStruct((T, D), BF), jax.ShapeDtypeStruct((1, D), F32)]),
        compiler_params=_cp(48),
    )(*_hbm(f, x2, target, g4))


def _bwd_mlp_x(df, a1, w_ff1, w_ff2):
    tm = 512
    fk = DF // NCHIP

    def body(df_ref, a1_ref, w1_ref, w2_ref, dh_ref, df1_ref):
        df = df_ref[...]
        dh = None
        for k in range(NCHIP):
            cols = slice(k * fk, (k + 1) * fk)
            dact = _mm_nt(df, w2_ref[cols, :])
            df1 = (dact * (2.0 * a1_ref[:, cols].astype(F32))).astype(BF)
            df1_ref[:, cols] = df1
            part = _mm_nt(df1, w1_ref[k])
            dh = part if dh is None else dh + part
        dh_ref[...] = dh

    return pl.pallas_call(
        body, name="bwd_mlp_x", grid=(T // tm,),
        in_specs=[pl.BlockSpec((tm, D), lambda i: (i, 0)),
                  pl.BlockSpec((tm, DF), lambda i: (i, 0)),
                  pl.BlockSpec((NCHIP, D, fk), lambda i: (0, 0, 0)),
                  pl.BlockSpec((DF, D), lambda i: (0, 0))],
        out_specs=[pl.BlockSpec((tm, D), lambda i: (i, 0)), pl.BlockSpec((tm, DF), lambda i: (i, 0))],
        out_shape=_hbm_out([jax.ShapeDtypeStruct((T, D), F32), jax.ShapeDtypeStruct((T, DF), BF)]),
        compiler_params=_cp(56),
    )(*_hbm(df, a1, w_ff1, w_ff2))


def _bwd_mlp_w(df, h2, a1, df1):
    fc = 512
    per = (DF // NCHIP) // fc

    def body(df_ref, h_ref, a1_ref, df1_ref, dw1_ref, dw2_ref):
        a1 = a1_ref[...].astype(F32)
        dw2_ref[...] = _mm_tn((a1 * a1).astype(BF), df_ref[...]).astype(BF)
        dw1_ref[0] = _mm_tn(h_ref[...], df1_ref[...]).astype(BF)

    return pl.pallas_call(
        body, name="bwd_mlp_w", grid=(DF // fc,),
        in_specs=[pl.BlockSpec((T, D), lambda j: (0, 0)),
                  pl.BlockSpec((T, D), lambda j: (0, 0)),
                  pl.BlockSpec((T, fc), lambda j: (0, j)),
                  pl.BlockSpec((T, fc), lambda j: (0, j))],
        out_specs=[pl.BlockSpec((1, D, fc), lambda j: (j // per, 0, j % per)),
                   pl.BlockSpec((fc, D), lambda j: (j, 0))],
        out_shape=_hbm_out([jax.ShapeDtypeStruct((NCHIP, D, DF // NCHIP), BF),
                            jax.ShapeDtypeStruct((DF, D), BF)]),
        compiler_params=_cp(56),
    )(*_hbm(df, h2, a1, df1))


def _bwd_merge(dh2, dy, x2, m, bra, brb, proj, b_gate, g2, g3, w_lru_up, w_pool_up, w_o, stages=()):
    tm = 256
    cpu = D // NCHIP

    def body(dh2_ref, dy_ref, x2_ref, m_ref, bra_ref, brb_ref, p0, p1, p2, p3, bg_ref,
             g2_ref, g3_ref, wl_ref, wp_ref, wo_ref,
             dx_ref, dgt_ref, dyl_ref, dyp_ref, dm_ref, dbra_ref, dbrb_ref, dg2_ref, dg3_ref, dbg_ref):
        first = pl.program_id(0) == 0
        x2 = x2_ref[...]
        r3 = lax.rsqrt(_mean(x2 * x2) + NORM_EPS)
        x2n = x2 * r3
        dh2 = dh2_ref[...]
        t3 = dh2 * g3_ref[...]
        dx2 = dy_ref[...] + r3 * (t3 - x2n * _mean(t3 * x2n))
        dx_ref[...] = dx2
        _acc(dg3_ref, _colsum(dh2 * x2n), first)
        m = m_ref[...]
        r2 = lax.rsqrt(_mean(m * m) + NORM_EPS)
        mn = m * r2
        _acc(dg2_ref, _colsum(dx2 * mn), first)
        dmn = dx2 * g2_ref[...]
        dm = (r2 * (dmn - mn * _mean(dmn * mn))).astype(BF)
        dm_ref[...] = dm
        dmrg = _mm_nt(dm, wo_ref[...])
        bg = bg_ref[...]
        ga = _sigmoid(jnp.concatenate([p0[...], p1[...]], axis=1) + bg[:, :D])
        gb = _sigmoid(jnp.concatenate([p2[...], p3[...]], axis=1) + bg[:, D:])
        dga = dmrg * bra_ref[...].astype(F32) * (ga * (1.0 - ga))
        dgb = dmrg * brb_ref[...].astype(F32) * (gb * (1.0 - gb))
        dgt_ref[:, :D] = dga.astype(BF)
        dgt_ref[:, D:] = dgb.astype(BF)
        _acc(dbg_ref, jnp.concatenate([_colsum(dga), _colsum(dgb)], axis=1), first)
        dbra = (dmrg * ga).astype(BF)
        dbrb = (dmrg * gb).astype(BF)
        dbra_ref[...] = dbra
        dbrb_ref[...] = dbrb
        dyl_ref[...] = _mm_nt(dbra, wl_ref[...])
        dyp = None
        for k in range(NCHIP):
            part = _mm_nt(dbrb[:, k * cpu:(k + 1) * cpu], wp_ref[k])
            dyp = part if dyp is None else dyp + part
        dyp_ref[...] = dyp

    row = lambda w: pl.BlockSpec((tm, w), lambda i: (i, 0))
    full2 = lambda a, b: pl.BlockSpec((a, b), lambda i: (0, 0))
    wp_spec = pl.BlockSpec((NCHIP, DP, cpu), lambda i: (0, 0, 0))
    return _call(
        body, name="bwd_merge", grid=(T // tm,),
        in_specs=[row(D)] * 6 + _gate_specs(tm) +
                 [full2(1, 2 * D), full2(1, D), full2(1, D), full2(DR, D), wp_spec, full2(D, D)],
        out_specs=[row(D), row(2 * D), row(DR), row(DP), row(D), row(D), row(D),
                   full2(1, D), full2(1, D), full2(1, 2 * D)],
        out_shape=[jax.ShapeDtypeStruct((T, D), F32), jax.ShapeDtypeStruct((T, 2 * D), BF),
                   jax.ShapeDtypeStruct((T, DR), F32), jax.ShapeDtypeStruct((T, DP), F32),
                   jax.ShapeDtypeStruct((T, D), BF), jax.ShapeDtypeStruct((T, D), BF),
                   jax.ShapeDtypeStruct((T, D), BF),
                   jax.ShapeDtypeStruct((1, D), F32), jax.ShapeDtypeStruct((1, D), F32),
                   jax.ShapeDtypeStruct((1, 2 * D), F32)],
        vmem=56, args=[dh2, dy, x2, m, bra, brb, proj, proj, proj, proj, b_gate, g2, g3, w_lru_up, w_pool_up, w_o],
        stages=stages)


def _dw_merge(mrg, dm, ylru, dbra, ypool, dbrb, stages=()):
    nb = NCHIP
    rb, pb, cpu = D // nb, DP // nb, D // NCHIP

    def body(mrg_ref, dm_ref, yl_ref, dbra_ref, yp_ref, dbrb_ref, dwo_ref, dwl_ref, dwp_ref):
        dwo_ref[...] = _mm_tn(mrg_ref[...], dm_ref[...]).astype(BF)
        dwl_ref[...] = _mm_tn(yl_ref[...], dbra_ref[...]).astype(BF)
        dwp = _mm_tn(yp_ref[...], dbrb_ref[...]).astype(BF)
        for k in range(NCHIP):
            dwp_ref[k] = dwp[:, k * cpu:(k + 1) * cpu]

    cols = lambda w: pl.BlockSpec((T, w), lambda r: (0, r))
    whole = pl.BlockSpec((T, D), lambda r: (0, 0))
    return _call(
        body, name="dw_merge", grid=(nb,),
        in_specs=[cols(rb), whole, cols(rb), whole, cols(pb), whole],
        out_specs=[pl.BlockSpec((rb, D), lambda r: (r, 0)), pl.BlockSpec((rb, D), lambda r: (r, 0)),
                   pl.BlockSpec((NCHIP, pb, cpu), lambda r: (0, r, 0))],
        out_shape=[jax.ShapeDtypeStruct((D, D), BF), jax.ShapeDtypeStruct((DR, D), BF),
                   jax.ShapeDtypeStruct((NCHIP, DP, cpu), BF)],
        vmem=56, args=[mrg, dm, ylru, dbra, ypool, dbrb], stages=stages)


def _bwd_lru(proj, h, dylru, conv_w, conv_b, wa, ba, wx, bx, lam, stages=()):
    def body(xp_ref, g_ref, h_ref, dy_ref, cw_ref, cb_ref, wa_ref, ba_ref, wx_ref, bx_ref, lam_ref,
             dxp_ref, dg_ref, dcw_ref, dcb_ref, dwa_ref, dba_ref, dwx_ref, dbx_ref, dlam_ref, a_s, b_s, l_s):
        xp = xp_ref[...]
        cw = cw_ref[...]
        lam = lam_ref[...]
        xc, x1, x2, x3 = _conv(xp, cw, cb_ref[...])
        wa, wx = wa_ref[0], wx_ref[0]
        xcb, r, ii, sp, a, mult = _lru_gates(xc, wa, ba_ref[...], wx, bx_ref[...], lam)
        g = g_ref[...]
        gel, dgel = _gelu_parts(g)
        h = h_ref[...]
        dy = dy_ref[...]
        dg_ref[...] = (dy * h * dgel).astype(BF)
        _tile_scan(_su(a, 1, 0.0), dy * gel, a_s, b_s, l_s, reverse=True)
        b = l_s[...]
        da = b * _sd(h, 1, 0.0)
        dmult = b * (ii * xc)
        dii = b * (mult * xc)
        dxc = b * (mult * ii)
        dla = da * a - dmult * ((a * a) / mult)
        dr = dla * ((-LRU_C) * sp)
        dsp = _colsum(dla * ((-LRU_C) * r))
        dlam_ref[...] = -dsp / (1.0 + jnp.exp(lam))
        dzr = dr * (r * (1.0 - r))
        dzi = dii * (ii * (1.0 - ii))
        dzrb, dzib = dzr.astype(BF), dzi.astype(BF)
        dxc = dxc + _mm_nt(dzrb, wa) + _mm_nt(dzib, wx)
        dwa_ref[0] = _mm_tn(xcb, dzrb)
        dwx_ref[0] = _mm_tn(xcb, dzib)
        dba_ref[...] = _colsum(dzr)
        dbx_ref[...] = _colsum(dzi)
        dcb_ref[...] = _colsum(dxc)
        dcw_ref[...] = jnp.concatenate([_colsum(dxc * x3), _colsum(dxc * x2), _colsum(dxc * x1),
                                        _colsum(dxc * xp)], axis=0)
        dxp = cw[3:4] * dxc + cw[2:3] * _su(dxc, 1) + cw[1:2] * _su(dxc, 2) + cw[0:1] * _su(dxc, 3)
        dxp_ref[...] = dxp.astype(BF)

    blk = pl.BlockSpec((T, CB), lambda j: (0, j))
    wsp = pl.BlockSpec((1, CB, CB), lambda j: (j, 0, 0))
    return _call(
        body, name="bwd_lru", grid=(NG,),
        in_specs=[blk, pl.BlockSpec((T, CB), lambda j: (0, NG + j)), blk, blk,
                  pl.BlockSpec((4, CB), lambda j: (0, j)), _vec_spec(), wsp, _vec_spec(), wsp, _vec_spec(),
                  _vec_spec()],
        out_specs=[blk, blk, pl.BlockSpec((4, CB), lambda j: (0, j)), _vec_spec(), wsp, _vec_spec(), wsp,
                   _vec_spec(), _vec_spec()],
        out_shape=[jax.ShapeDtypeStruct((T, DR), BF), jax.ShapeDtypeStruct((T, DR), BF),
                   jax.ShapeDtypeStruct((4, DR), F32), jax.ShapeDtypeStruct((1, DR), F32),
                   jax.ShapeDtypeStruct((NG, CB, CB), F32), jax.ShapeDtypeStruct((1, DR), F32),
                   jax.ShapeDtypeStruct((NG, CB, CB), F32), jax.ShapeDtypeStruct((1, DR), F32),
                   jax.ShapeDtypeStruct((1, DR), F32)],
        vmem=56, args=[proj, proj, h, dylru, conv_w, conv_b, wa, ba, wx, bx, lam], stages=stages,
        scratch=[pltpu.VMEM((T, CB), F32)] * 3)


def _bwd_pool(proj, dypool, pool_w, pool_scale):
    def body(xp_ref, dy_ref, pw_ref, sc_ref, dx_ref, dw_ref, dsc_ref):
        for g, w in enumerate(POOL_WINDOWS):
            cols = slice(g * PG, (g + 1) * PG)
            cnt = _pool_cnt(w)
            x = xp_ref[:, cols]
            pb = (_pool_window(x, g + 1, _sd) / cnt - x).astype(BF)
            wg = pw_ref[g]
            dy = dy_ref[:, cols]
            dsc_ref[:, cols] = _colsum(dy * _mm(pb, wg))
            dyp = (dy * sc_ref[:, cols]).astype(BF)
            dw_ref[g] = _mm_tn(pb, dyp)
            dp = _mm_nt(dyp, wg)
            dx_ref[:, cols] = (_pool_window(dp / cnt, g + 1, _su) - dp).astype(BF)

    return pl.pallas_call(
        body, name="bwd_pool", grid=(1,),
        in_specs=[pl.BlockSpec((T, DP), lambda i: (0, 2 * DR // DP)),
                  pl.BlockSpec((T, DP), lambda i: (0, 0)),
                  pl.BlockSpec((4, PG, PG), lambda i: (0, 0, 0)),
                  pl.BlockSpec((1, DP), lambda i: (0, 0))],
        out_specs=[pl.BlockSpec((T, DP), lambda i: (0, 0)),
                   pl.BlockSpec((4, PG, PG), lambda i: (0, 0, 0)),
                   pl.BlockSpec((1, DP), lambda i: (0, 0))],
        out_shape=_hbm_out([jax.ShapeDtypeStruct((T, DP), BF), jax.ShapeDtypeStruct((4, PG, PG), F32),
                            jax.ShapeDtypeStruct((1, DP), F32)]),
        compiler_params=_cp(48),
    )(*_hbm(proj, dypool, pool_w, pool_scale))


PART_COLS = (DR, DR, DP, 2 * D)


def _shard_pieces():
    starts = [sum(PART_COLS[:p]) for p in range(len(PART_COLS))]
    shards = []
    for k in range(NCHIP):
        lo, hi = k * CW_IN, (k + 1) * CW_IN
        shards.append([(p, max(lo, s) - s, min(hi, s + wd) - s, max(lo, s) - lo)
                       for p, (s, wd) in enumerate(zip(starts, PART_COLS)) if max(lo, s) < min(hi, s + wd)])
    return shards


def _bwd_inproj_w(h1, parts, after):
    def body(h_ref, p0, p1, p2, p3, after_ref, dw_ref):
        part_refs = (p0, p1, p2, p3)
        for k, pieces in enumerate(_shard_pieces()):
            for p, a, b, c0 in pieces:
                dw_ref[k, :, c0:c0 + b - a] = _mm_tn(h_ref[...], part_refs[p][:, a:b]).astype(BF)

    vmem = pl.BlockSpec(memory_space=pltpu.VMEM)
    return pl.pallas_call(
        body, name="bwd_inproj_w", in_specs=[vmem] * 5 + [ANY], out_specs=vmem,
        out_shape=pltpu.HBM((NCHIP, D, CW_IN), BF), compiler_params=_cp(48),
    )(*_hbm(h1, *parts), after)


def _bwd_inproj_x(parts, w_in, x, dxres, g1, stages=()):
    tm = 512

    def body(p0, p1, p2, p3, w_ref, x_ref, dr_ref, g_ref, dx_ref, dg_ref):
        part_refs = (p0, p1, p2, p3)
        dh = None
        for k, pieces in enumerate(_shard_pieces()):
            for p, a, b, c0 in pieces:
                part = _mm_nt(part_refs[p][:, a:b], w_ref[k, :, c0:c0 + b - a])
                dh = part if dh is None else dh + part
        xv = x_ref[...]
        r = lax.rsqrt(_mean(xv * xv) + NORM_EPS)
        xn = xv * r
        t = dh * g_ref[...]
        dx_ref[...] = dr_ref[...] + r * (t - xn * _mean(t * xn))
        _acc(dg_ref, _colsum(dh * xn), pl.program_id(0) == 0)

    row = pl.BlockSpec((tm, D), lambda i: (i, 0))
    vec = pl.BlockSpec((1, D), lambda i: (0, 0))
    return _call(
        body, name="bwd_inproj_x", grid=(T // tm,),
        in_specs=[pl.BlockSpec((tm, wd), lambda i: (i, 0)) for wd in PART_COLS] +
                 [pl.BlockSpec((NCHIP, D, CW_IN), lambda i: (0, 0, 0)), row, row, vec],
        out_specs=[row, vec],
        out_shape=[jax.ShapeDtypeStruct((T, D), F32), jax.ShapeDtypeStruct((1, D), F32)],
        vmem=56, args=[*parts, w_in, x, dxres, g1], stages=stages)[0]


def _place():
    x, y, c = lax.axis_index("x"), lax.axis_index("y"), lax.axis_index("c")
    chips = [(1 - x, y), (x, 1 - y), (1 - x, 1 - y)]
    return x, y, c, chips


def _rcopy(src, dst, ssem, rsem, dev):
    return pltpu.make_async_remote_copy(src_ref=src, dst_ref=dst, send_sem=ssem, recv_sem=rsem,
                                        device_id=dev, device_id_type=MESH_ID)


def _sds(a):
    return jax.ShapeDtypeStruct(a.shape, a.dtype)


def _sem2(n, m):
    return [pltpu.SemaphoreType.DMA((n * m,)), pltpu.SemaphoreType.DMA((n * m,))]


ALL = (0, 1, 1)


def _piece(ref, k, half, part):
    hr = ref.shape[1] // 2
    r0, r1 = hr * part[0] // part[2], hr * part[1] // part[2]
    return ref.at[k, pl.ds(half * hr + r0, r1 - r0), :]


def _gather(fulls, ici=(), d2d=()):
    n = len(fulls)
    ici, d2d = list(ici), list(d2d)
    pieces = [("ici", i, part) for i, part in ici] + [("d2d", i, part) for i, part in d2d]

    def copies(outs, sems):
        x, y, c, chips = _place()
        me = 2 * x + y
        sib = (x, y, 1 - c)
        send, recv = [], []
        for q, (kind, i, part) in enumerate(pieces):
            for j, chip in enumerate(chips):
                k, s = 2 * chip[0] + chip[1], 3 * q + j
                if kind == "ici":
                    mine, theirs, dev = _piece(outs[i], me, c, part), _piece(outs[i], k, c, part), (*chip, c)
                else:
                    mine, theirs, dev = _piece(outs[i], k, c, part), _piece(outs[i], k, 1 - c, part), sib
                send.append(_rcopy(mine, mine, sems[0].at[s], sems[1].at[s], dev))
                recv.append(_rcopy(theirs, theirs, sems[0].at[s], sems[1].at[s], dev))
        return send, recv

    def start(ins, outs, sems):
        for cp in copies(outs, sems)[0]:
            cp.start()

    def finish(ins, outs, sems):
        send, recv = copies(outs, sems)
        for cp in recv:
            cp.wait_recv()
        for cp in send:
            cp.wait_send()

    sems = [pltpu.SemaphoreType.DMA((3 * len(pieces),)), pltpu.SemaphoreType.DMA((3 * len(pieces),))]
    return _Stage(fulls, [_sds(f) for f in fulls], {i: i for i in range(n)}, sems, start, finish)


def _gather_whole(v):
    def copies(ins, outs, sems):
        x, y, c, chips = _place()
        me = 2 * x + y
        send = [_rcopy(ins[0], outs[0].at[me], sems[0].at[j], sems[1].at[j], (*chip, c))
                for j, chip in enumerate(chips)]
        recv = [_rcopy(ins[0], outs[0].at[2 * chip[0] + chip[1]], sems[0].at[j], sems[1].at[j], (*chip, c))
                for j, chip in enumerate(chips)]
        return send, recv

    def start(ins, outs, sems):
        for cp in copies(ins, outs, sems)[0]:
            cp.start()

    def finish(ins, outs, sems):
        send, recv = copies(ins, outs, sems)
        for cp in recv:
            cp.wait_recv()
        for cp in send:
            cp.wait_send()

    return _Stage([v], [jax.ShapeDtypeStruct((NCHIP,) + v.shape, v.dtype)], {},
                  [pltpu.SemaphoreType.DMA((3,)), pltpu.SemaphoreType.DMA((3,))], start, finish)


def _to_sibling(srcs):
    n = len(srcs)

    def copies(ins, outs, sems):
        x, y, c, _ = _place()
        sib = (x, y, 1 - c)
        return [_rcopy(ins[i].at[:, 1 - c] if srcs[i].ndim == 4 else ins[i], outs[i], sems[0].at[i], sems[1].at[i], sib)
                for i in range(n)]

    def start(ins, outs, sems):
        for cp in copies(ins, outs, sems):
            cp.start()

    def finish(ins, outs, sems):
        for cp in copies(ins, outs, sems):
            cp.wait()

    shapes = [jax.ShapeDtypeStruct((NCHIP,) + s.shape[2:] if s.ndim == 4 else s.shape, s.dtype) for s in srcs]
    return _Stage(srcs, shapes, {}, [pltpu.SemaphoreType.DMA((n,)), pltpu.SemaphoreType.DMA((n,))], start, finish)


def _to_chips(srcs, parts=None, lands=None):
    n = len(srcs)
    parts = [ALL] * n if parts is None else parts
    lands = [None] * n if lands is None else lands
    given = [i for i in range(n) if lands[i] is not None]

    def rows(ref, i):
        hr = srcs[i].shape[1]
        r0, r1 = hr * parts[i][0] // parts[i][2], hr * parts[i][1] // parts[i][2]
        return ref.at[pl.ds(r0, r1 - r0), :]

    def copies(ins, outs, sems):
        x, y, c, chips = _place()
        me = 2 * x + y
        return [_rcopy(rows(ins[i].at[2 * chip[0] + chip[1]] if srcs[i].shape[0] == NCHIP else ins[i].at[c], i),
                       rows(outs[i].at[me], i), sems[0].at[3 * i + j], sems[1].at[3 * i + j], (*chip, c))
                for i in range(n) for j, chip in enumerate(chips)]

    def start(ins, outs, sems):
        for cp in copies(ins, outs, sems):
            cp.start()

    def finish(ins, outs, sems):
        for cp in copies(ins, outs, sems):
            cp.wait()

    shapes = [jax.ShapeDtypeStruct((NCHIP,) + s.shape[1:], s.dtype) for s in srcs]
    alias = {n + q: i for q, i in enumerate(given)}
    return _Stage(list(srcs) + [lands[i] for i in given], shapes, alias, _sem2(n, 3), start, finish)


HBM_REF = pl.BlockSpec(memory_space=pltpu.HBM)
SEM_REF = pl.BlockSpec(memory_space=pltpu.SEMAPHORE)
DATAFLOW = pltpu.SideEffectType.DATAFLOW_SIDE_EFFECTING


def _after(x):
    return _Stage([x], [], {}, [], lambda *a: None, lambda *a: None)


class _Flight:
    def __init__(self, stage, sems, bufs):
        self.stage, self.sems, self.bufs = stage, list(sems), list(bufs)

    def landed(self):
        st, n = self.stage, len(self.stage.operands)
        fresh = [j for j in range(len(st.out_shape)) if j not in st.alias.values()]
        back = {v: k for k, v in st.alias.items()}
        return [self.bufs[back[j]] if j in back else self.bufs[n + fresh.index(j)] for j in range(len(st.out_shape))]


def _split_call(name, finish=(), start=(), after=None):
    bufs, stage_bufs = [], []

    def slot(a):
        for i, b in enumerate(bufs):
            if b is a:
                return i
        bufs.append(a)
        return len(bufs) - 1

    fin_slots = [[slot(b) for b in fl.bufs] for fl in finish]
    for st in start:
        fresh = [lax.empty(o.shape, o.dtype) for j, o in enumerate(st.out_shape) if j not in st.alias.values()]
        stage_bufs.append([slot(a) for a in list(st.operands) + fresh])
    old_sems = [s for fl in finish for s in fl.sems]
    new_sems = [s for st in start for s in st.sems]
    nb, no, nn = len(bufs), len(old_sems), len(new_sems)

    def refs_of(st, slots, buf_refs):
        n = len(st.operands)
        ins = [buf_refs[i] for i in slots[:n]]
        fresh = [j for j in range(len(st.out_shape)) if j not in st.alias.values()]
        back = {v: k for k, v in st.alias.items()}
        outs = [ins[back[j]] if j in back else buf_refs[slots[n + fresh.index(j)]] for j in range(len(st.out_shape))]
        return ins, outs

    def body(*refs):
        buf_refs, sem_in = refs[:nb], refs[nb:nb + no]
        sem_out = refs[nb + no + (after is not None):][:nn]
        token = refs[-1]
        pos = 0
        for fl, slots in zip(finish, fin_slots):
            ins, outs = refs_of(fl.stage, slots, buf_refs)
            fl.stage.finish(ins, outs, sem_in[pos:pos + len(fl.sems)])
            pos += len(fl.sems)
        pos = 0
        for st, slots in zip(start, stage_bufs):
            ins, outs = refs_of(st, slots, buf_refs)
            st.start(ins, outs, sem_out[pos:pos + len(st.sems)])
            pos += len(st.sems)
        token[...] = jnp.zeros_like(token)

    res = pl.pallas_call(
        body, name=name,
        out_shape=tuple(new_sems) + tuple(pltpu.HBM(b.shape, b.dtype) for b in bufs) +
                  (jax.ShapeDtypeStruct((8, LANE), F32),),
        in_specs=(HBM_REF,) * nb + (SEM_REF,) * no + ((pl.BlockSpec(memory_space=pl.ANY),) if after is not None else ()),
        out_specs=(SEM_REF,) * nn + (HBM_REF,) * nb + (pl.BlockSpec(memory_space=pltpu.VMEM),),
        input_output_aliases={i: nn + i for i in range(nb)},
        compiler_params=pltpu.CompilerParams(has_side_effects=DATAFLOW),
    )(*_hbm(*bufs), *old_sems, *([after] if after is not None else []))
    sems, thru, token = res[:nn], res[nn:nn + nb], res[-1]
    for fl, slots in zip(finish, fin_slots):
        fl.bufs = [thru[i] for i in slots]
    flights, pos = [], 0
    for st, slots in zip(start, stage_bufs):
        flights.append(_Flight(st, sems[pos:pos + len(st.sems)], [thru[i] for i in slots]))
        pos += len(st.sems)
    return flights, token


def _last_copies(p_ref, land_ref, ssem, rsem):
    x, y, c, chips = _place()
    me = 2 * x + y
    send = [_rcopy(p_ref.at[2 * chip[0] + chip[1]], land_ref.at[me], ssem.at[j], rsem.at[j], (*chip, c))
            for j, chip in enumerate(chips)]
    recv = [_rcopy(p_ref.at[2 * chip[0] + chip[1]], land_ref.at[2 * chip[0] + chip[1]], ssem.at[j], rsem.at[j],
                   (*chip, c)) for j, chip in enumerate(chips)]
    return send, recv


def _chips_start(p):
    def body(p_ref, land_ref, ssem, rsem, p_thru, land_thru, token):
        for cp in _last_copies(p_ref, land_ref, ssem, rsem)[0]:
            cp.start()
        token[...] = jnp.zeros_like(token)

    return pl.pallas_call(
        body, name="reduce_last_start",
        out_shape=(pltpu.SemaphoreType.DMA((3,)), pltpu.SemaphoreType.DMA((3,)), pltpu.HBM(p.shape, p.dtype),
                   pltpu.HBM(p.shape, p.dtype), jax.ShapeDtypeStruct((8, LANE), F32)),
        in_specs=(HBM_REF, HBM_REF),
        out_specs=(SEM_REF, SEM_REF, HBM_REF, HBM_REF, pl.BlockSpec(memory_space=pltpu.VMEM)),
        input_output_aliases={0: 2, 1: 3},
        compiler_params=pltpu.CompilerParams(has_side_effects=DATAFLOW),
    )(*_hbm(p, lax.empty(p.shape, p.dtype)))


def _chips_wait(ssem, rsem, p_thru, land_thru, after):
    def body(p_ref, land_ref, ssem, rsem, after_ref, p_dead, got_ref):
        send, recv = _last_copies(p_ref, land_ref, ssem, rsem)
        for cp in send:
            cp.wait_send()
        for cp in recv:
            cp.wait_recv()

    return pl.pallas_call(
        body, name="reduce_last_wait",
        out_shape=(pltpu.HBM(p_thru.shape, p_thru.dtype), pltpu.HBM(land_thru.shape, land_thru.dtype)),
        in_specs=(HBM_REF, HBM_REF, SEM_REF, SEM_REF, pl.BlockSpec(memory_space=pl.ANY)),
        out_specs=(HBM_REF, HBM_REF), input_output_aliases={0: 0, 1: 1},
        compiler_params=pltpu.CompilerParams(has_side_effects=DATAFLOW),
    )(p_thru, land_thru, ssem, rsem, after)


def _share(pairs):
    n = len(pairs)

    def start(ins, outs, sems):
        x, y, c, _ = _place()
        for i in range(n):
            _rcopy(outs[i].at[c], outs[i].at[c], sems[0].at[i], sems[1].at[i], (x, y, 1 - c)).start()

    def finish(ins, outs, sems):
        x, y, c, _ = _place()
        for i in range(n):
            _rcopy(outs[i].at[c], outs[i].at[c], sems[0].at[i], sems[1].at[i], (x, y, 1 - c)).wait_send()
            _rcopy(outs[i].at[1 - c], outs[i].at[1 - c], sems[0].at[i], sems[1].at[i], (x, y, 1 - c)).wait_recv()

    return _Stage(pairs, [_sds(p) for p in pairs], {i: i for i in range(n)},
                  [pltpu.SemaphoreType.DMA((n,)), pltpu.SemaphoreType.DMA((n,))], start, finish)


def _row_block(rows, cols, itemsize=4, target=2 * MIB):
    br = rows
    while br * cols * itemsize > target and br % 32 == 0:
        br //= 2
    return br


def _cast_place(w, chip_idx, name):
    rows, cols = w.shape
    br = _row_block(rows, cols)

    def body(k_ref, w_ref, o_ref):
        o_ref[0] = w_ref[...].astype(BF)

    return _call(
        body, name=name, grid=(rows // br,), prefetch=chip_idx,
        in_specs=[pl.BlockSpec((br, cols), lambda r, k: (r, 0))],
        out_specs=[pl.BlockSpec((1, br, cols), lambda r, k: (k[0], r, 0))],
        out_shape=[jax.ShapeDtypeStruct((NCHIP, rows, cols), BF)], vmem=32, args=[w])[0][0]


def _cast_place_multi(ws, chip_idx, stages=()):
    br = 128
    nblk = [a.shape[0] // br for a in ws]
    starts = [sum(nblk[:i]) for i in range(len(ws))]

    def body(k_ref, *refs):
        r = pl.program_id(0)
        for i in range(len(ws)):
            @pl.when(jnp.logical_and(r >= starts[i], r < starts[i] + nblk[i]))
            def _(i=i):
                refs[len(ws) + i][0] = refs[i][...].astype(BF)

    def at(i):
        return functools.partial(lambda r, s, nb: jnp.clip(r - s, 0, nb - 1), s=starts[i], nb=nblk[i])

    outs, landed = _call(
        body, name="cast_rest", grid=(sum(nblk),), prefetch=chip_idx,
        in_specs=[pl.BlockSpec((br, a.shape[1]), functools.partial(lambda r, k, f: (f(r), 0), f=at(i)))
                  for i, a in enumerate(ws)],
        out_specs=[pl.BlockSpec((1, br, a.shape[1]), functools.partial(lambda r, k, f: (k[0], f(r), 0), f=at(i)))
                   for i, a in enumerate(ws)],
        out_shape=[jax.ShapeDtypeStruct((NCHIP,) + a.shape, BF) for a in ws], vmem=32, args=list(ws), stages=stages)
    return outs, landed


def _add_sibling(g, land, cidx, name, stages=()):
    _, _, hr, cols = g.shape
    br = _row_block(hr, cols)

    def body(c_ref, g_ref, l_ref, o_ref):
        o_ref[...] = (g_ref[0, 0].astype(F32) + l_ref[0].astype(F32)).astype(BF)[None]

    outs, st = _call(
        body, name=name, grid=(NCHIP, hr // br), prefetch=cidx,
        in_specs=[pl.BlockSpec((1, 1, br, cols), lambda k, r, c: (k, c[0], r, 0)),
                  pl.BlockSpec((1, br, cols), lambda k, r, c: (k, r, 0))],
        out_specs=[pl.BlockSpec((1, br, cols), lambda k, r, c: (k, r, 0))],
        out_shape=[jax.ShapeDtypeStruct((NCHIP, hr, cols), BF)], vmem=32, args=[g, land], stages=stages)
    return outs[0], st


def _add_sibling_multi(gs, lands, cidx, name):
    n = len(gs)
    brs = [_row_block(g.shape[2], g.shape[3]) for g in gs]
    nrb = [g.shape[2] // b for g, b in zip(gs, brs)]
    nblk = [NCHIP * q for q in nrb]
    starts = [sum(nblk[:i]) for i in range(n)]

    def body(c_ref, *refs):
        r = pl.program_id(0)
        for i in range(n):
            g_ref, l_ref, o_ref = refs[2 * i], refs[2 * i + 1], refs[2 * n + i]

            @pl.when(jnp.logical_and(r >= starts[i], r < starts[i] + nblk[i]))
            def _():
                o_ref[...] = (g_ref[0, 0].astype(F32) + l_ref[0].astype(F32)).astype(BF)[None]

    def at(i, r):
        q = jnp.clip(r - starts[i], 0, nblk[i] - 1)
        return q // nrb[i], q % nrb[i]

    def g_spec(i):
        return pl.BlockSpec((1, 1, brs[i], gs[i].shape[3]),
                            functools.partial(lambda r, c, i: (at(i, r)[0], c[0], at(i, r)[1], 0), i=i))

    def l_spec(i):
        return pl.BlockSpec((1, brs[i], gs[i].shape[3]),
                            functools.partial(lambda r, c, i: (at(i, r)[0], at(i, r)[1], 0), i=i))

    return _call(
        body, name=name, grid=(sum(nblk),), prefetch=cidx,
        in_specs=[s for i in range(n) for s in (g_spec(i), l_spec(i))], out_specs=[l_spec(i) for i in range(n)],
        out_shape=[jax.ShapeDtypeStruct(l.shape, BF) for l in lands], vmem=32,
        args=[a for i in range(n) for a in (gs[i], lands[i])])[0]


def _add_pair(a, b, name):
    rows, cols = a.shape

    def body(a_ref, b_ref, o_ref):
        o_ref[...] = a_ref[...] + b_ref[...]

    spec = pl.BlockSpec((rows, cols), lambda r: (0, 0))
    return _call(body, name=name, grid=(1,), in_specs=[spec, spec], out_specs=[spec], out_shape=[_sds(a)],
                 vmem=32, args=[a, b])[0][0]


def _add_chips(own, land, idx, name, stages=None):
    _, hr, cols = land.shape
    br = _row_block(hr, cols)

    def body(s_ref, a_ref, b_ref, c_ref, d_ref, o_ref):
        o_ref[...] = (a_ref[...].astype(F32) + b_ref[...].astype(F32)) + (c_ref[...].astype(F32) +
                                                                           d_ref[...].astype(F32))

    spec = lambda q: pl.BlockSpec((1, br, cols), functools.partial(lambda r, s, q: (s[q], r, 0), q=q))
    outs, landed = _call(
        body, name=name, grid=(hr // br,), prefetch=idx,
        in_specs=[spec(0), spec(1), spec(2), spec(3)], out_specs=[spec(4)],
        out_shape=[jax.ShapeDtypeStruct((2, hr, cols), F32)], vmem=48, args=[own, land, land, land],
        stages=stages or ())
    return outs[0] if stages is None else (outs[0], landed)


def _add_chips_multi(owns, lands, idx, name, stages=()):
    n = len(owns)
    brs = [_row_block(l.shape[1], l.shape[2]) for l in lands]
    nblk = [l.shape[1] // b for l, b in zip(lands, brs)]
    starts = [sum(nblk[:i]) for i in range(n)]

    def body(s_ref, *refs):
        r = pl.program_id(0)
        for i in range(n):
            a_ref, b_ref, c_ref, d_ref = refs[4 * i:4 * i + 4]
            o_ref = refs[4 * n + i]

            @pl.when(jnp.logical_and(r >= starts[i], r < starts[i] + nblk[i]))
            def _():
                o_ref[...] = (a_ref[...].astype(F32) + b_ref[...].astype(F32)) + (c_ref[...].astype(F32) +
                                                                                   d_ref[...].astype(F32))

    def spec(i, q):
        return pl.BlockSpec((1, brs[i], lands[i].shape[2]), functools.partial(
            lambda r, s, q, st, nb: (s[q], jnp.clip(r - st, 0, nb - 1), 0), q=q, st=starts[i], nb=nblk[i]))

    outs, landed = _call(
        body, name=name, grid=(sum(nblk),), prefetch=idx,
        in_specs=[spec(i, q) for i in range(n) for q in range(4)], out_specs=[spec(i, 4) for i in range(n)],
        out_shape=[jax.ShapeDtypeStruct((2,) + l.shape[1:], F32) for l in lands], vmem=48,
        args=[a for i in range(n) for a in (owns[i], lands[i], lands[i], lands[i])], stages=stages)
    return outs, landed


def _adamw_math(w, g, m, v):
    mn = ADAM_B1 * m + (1.0 - ADAM_B1) * g
    vn = ADAM_B2 * v + (1.0 - ADAM_B2) * (g * g)
    m_hat = mn / (1.0 - ADAM_B1 ** ADAM_STEP)
    v_hat = vn / (1.0 - ADAM_B2 ** ADAM_STEP)
    return -ADAM_LR * (m_hat / (jnp.sqrt(v_hat) + ADAM_EPS) + ADAM_WD * w), mn, vn


def _adamw(w, g, m, v, name, stages=()):
    rows, cols = w.shape
    br = _row_block(rows, cols)

    def body(w_ref, g_ref, m_ref, v_ref, go_ref, d_ref, mo_ref, vo_ref):
        gv = g_ref[...]
        go_ref[...] = gv
        d_ref[...], mo_ref[...], vo_ref[...] = _adamw_math(w_ref[...], gv, m_ref[...], v_ref[...])

    spec = pl.BlockSpec((br, cols), lambda r: (r, 0))
    return _call(body, name=name, grid=(rows // br,), in_specs=[spec] * 4, out_specs=[spec] * 4,
                 out_shape=[_sds(w)] * 4, vmem=56, args=[w, g, m, v], stages=stages)


def _adamw_multi(names, w, g, m, v, stages=()):
    cols = w[names[0]].shape[1]
    br = 128
    nblk = [w[n].shape[0] // br for n in names]
    starts = [sum(nblk[:i]) for i in range(len(names))]

    def body(*refs):
        r = pl.program_id(0)
        for i in range(len(names)):
            w_ref, g_ref, m_ref, v_ref = refs[4 * i:4 * i + 4]
            go_ref, d_ref, mo_ref, vo_ref = refs[4 * len(names) + 4 * i:4 * len(names) + 4 * i + 4]

            @pl.when(jnp.logical_and(r >= starts[i], r < starts[i] + nblk[i]))
            def _():
                gv = g_ref[...]
                go_ref[...] = gv
                d_ref[...], mo_ref[...], vo_ref[...] = _adamw_math(w_ref[...], gv, m_ref[...], v_ref[...])

    def spec(i):
        return pl.BlockSpec((br, cols), functools.partial(
            lambda r, s, nb: (jnp.clip(r - s, 0, nb - 1), 0), s=starts[i], nb=nblk[i]))

    outs, landed = _call(
        body, name="adamw_" + "_".join(names), grid=(sum(nblk),),
        in_specs=[spec(i) for i in range(len(names)) for _ in range(4)],
        out_specs=[spec(i) for i in range(len(names)) for _ in range(4)],
        out_shape=[_sds(w[n]) for n in names for _ in range(4)], vmem=56,
        args=[a[n] for n in names for a in (w, g, m, v)], stages=stages)
    return {n: outs[4 * i:4 * i + 4] for i, n in enumerate(names)}, landed


def _to_everyone(v):
    deltas = [(a, b, e) for a in (0, 1) for b in (0, 1) for e in (0, 1)][1:]

    def copies(ins, outs, sems):
        x, y, c, _ = _place()
        me = 4 * x + 2 * y + c
        flip = lambda p, f: 1 - p if f else p
        return [_rcopy(ins[0], outs[0].at[me], sems[0].at[q], sems[1].at[q], (flip(x, a), flip(y, b), flip(c, e)))
                for q, (a, b, e) in enumerate(deltas)]

    def start(ins, outs, sems):
        for cp in copies(ins, outs, sems):
            cp.start()

    def finish(ins, outs, sems):
        for cp in copies(ins, outs, sems):
            cp.wait()

    n = len(deltas)
    return _Stage([v], [jax.ShapeDtypeStruct((2 * NCHIP,) + v.shape, v.dtype)], {},
                  [pltpu.SemaphoreType.DMA((n,)), pltpu.SemaphoreType.DMA((n,))], start, finish)


SMALL_AT = {"norm_mix_pre": (0, 1, D), "norm_mix_post": (1, 1, D), "norm_mlp_pre": (2, 1, D),
            "norm_mlp_post": (3, 1, D), "b_gate": (4, 2, D), "conv_b": (6, 1, D), "lru_b_a": (7, 1, D),
            "lru_b_x": (8, 1, D), "lru_lambda": (9, 1, D), "pool_scale": (10, 1, DP)}
SMALL_SEPARATE = ["conv_w", "lru_w_a", "lru_w_x", "pool_w"]


def _adamw_small(small_sum, first_all, sep_grads, w, m, v):
    packed, sep = list(SMALL_AT), list(SMALL_SEPARATE)
    names = packed + sep

    def body(*refs):
        s_ref, a_ref, refs = refs[0], refs[1], refs[2:]
        g_sep, refs = refs[:len(sep)], refs[len(sep):]
        nn = len(names)
        w_r, m_r, v_r, refs = refs[:nn], refs[nn:2 * nn], refs[2 * nn:3 * nn], refs[3 * nn:]
        g_out, refs = refs[:len(packed)], refs[len(packed):]
        d_o, m_o, v_o = refs[:nn], refs[nn:2 * nn], refs[2 * nn:3 * nn]
        for i, n in enumerate(names):
            if i == 0:
                g = a_ref[0:1, :]
                for q in range(1, 2 * NCHIP):
                    g = g + a_ref[q:q + 1, :]
                g_out[i][...] = g
            elif n in SMALL_AT:
                r0, nr, nc = SMALL_AT[n]
                g = jnp.concatenate([s_ref[r0 + q:r0 + q + 1, :nc] for q in range(nr)], axis=1)
                g_out[i][...] = g
            else:
                g = g_sep[i - len(packed)][...]
            d_o[i][...], m_o[i][...], v_o[i][...] = _adamw_math(w_r[i][...], g, m_r[i][...], v_r[i][...])

    ws = [w[n] for n in names]
    res = pl.pallas_call(
        body, name="adamw_small",
        out_shape=[_sds(w[n]) for n in packed] + [_sds(a) for a in ws] * 3,
        compiler_params=_cp(32),
    )(*_hbm(small_sum, first_all, *sep_grads, *ws, *[m[n] for n in names], *[v[n] for n in names]))
    nn, npk = len(names), len(packed)
    grad = dict(zip(packed, res[:npk]))
    delta = dict(zip(names, res[npk:npk + nn]))
    new_m = dict(zip(names, res[npk + nn:npk + 2 * nn]))
    new_v = dict(zip(names, res[npk + 2 * nn:]))
    return grad, delta, new_m, new_v


W_NAMES = ["norm_mix_pre", "norm_mix_post", "norm_mlp_pre", "norm_mlp_post", "w_in", "b_gate", "conv_w", "conv_b",
           "lru_w_a", "lru_b_a", "lru_w_x", "lru_b_x", "lru_lambda", "pool_w", "pool_scale", "w_lru_up",
           "w_pool_up", "w_o", "w_ff1", "w_ff2"]
BIG = ["w_in", "w_lru_up", "w_pool_up", "w_o", "w_ff1", "w_ff2"]


def _block_diag(w):
    hd = w.shape[-1]
    per = CB // hd
    w4 = w.reshape(NG, per, hd, hd)
    eye = jnp.eye(per, dtype=w.dtype)
    return jnp.einsum("gpij,pq->gpiqj", w4, eye).reshape(NG, CB, CB)


def _block_diag_extract(d, hd):
    per = CB // hd
    d5 = d.reshape(NG, per, hd, per, hd)
    return jnp.stack([d5[:, p, :, p, :] for p in range(per)], axis=1).reshape(NG * per, hd, hd)


def _halves(g):
    return g.reshape(NCHIP, 2, g.size // (g.shape[-1] * 2 * NCHIP), g.shape[-1])


def kernel(x, norm_mix_pre, norm_mix_post, norm_mlp_pre, norm_mlp_post, w_in, b_gate, conv_w, conv_b, lru_w_a, lru_b_a, lru_w_x, lru_b_x, lru_lambda, pool_w, pool_scale, w_lru_up, w_pool_up, w_o, w_ff1, w_ff2, loss_target, m_norm_mix_pre, m_norm_mix_post, m_norm_mlp_pre, m_norm_mlp_post, m_w_in, m_b_gate, m_conv_w, m_conv_b, m_lru_w_a, m_lru_b_a, m_lru_w_x, m_lru_b_x, m_lru_lambda, m_pool_w, m_pool_scale, m_w_lru_up, m_w_pool_up, m_w_o, m_w_ff1, m_w_ff2, v_norm_mix_pre, v_norm_mix_post, v_norm_mlp_pre, v_norm_mlp_post, v_w_in, v_b_gate, v_conv_w, v_conv_b, v_lru_w_a, v_lru_b_a, v_lru_w_x, v_lru_b_x, v_lru_lambda, v_pool_w, v_pool_scale, v_w_lru_up, v_w_pool_up, v_w_o, v_w_ff1, v_w_ff2):
    args = dict(locals())
    two_d = lambda a: a.reshape(-1, a.shape[-1])
    w = {n: two_d(args[n]) for n in W_NAMES}
    mom = {n: two_d(args["m_" + n]) for n in W_NAMES}
    var = {n: two_d(args["v_" + n]) for n in W_NAMES}
    i32 = lambda val: jnp.asarray(val, jnp.int32)
    chip = i32(2 * lax.axis_index("x") + lax.axis_index("y"))
    core = i32(lax.axis_index("c"))
    cidx = core.reshape(1)
    zero = i32(0)
    hd = lru_w_a.shape[-1]
    xs, target = x[0], loss_target[0]
    g1, g2, g3, g4 = norm_mix_pre, norm_mix_post, norm_mlp_pre, norm_mlp_post

    mix = ["w_lru_up", "w_pool_up", "w_o"]
    full = {"w_in": _cast_place(w["w_in"], chip.reshape(1), "cast_w_in")}
    (fl_in, fl_conv), first = _split_call("gather_start_first", start=[
        _gather([full["w_in"]], ici=[(0, ALL)]), _gather_whole(w["conv_w"])])
    casts, _ = _cast_place_multi([w[n] for n in BIG[1:]], chip.reshape(1), stages=[_after(first)])
    full.update(zip(BIG[1:], casts))
    (fl_mix, fl_ff1, fl_ff2), started = _split_call("gather_start_rest", start=[
        _gather([full[n] for n in mix], ici=[(0, ALL), (1, ALL), (2, ALL)]),
        _gather([full["w_ff1"]], ici=[(0, ALL)]), _gather([full["w_ff2"]], ici=[(0, ALL)])])
    wa = _block_diag(lru_w_a[0]).astype(BF)
    wx = _block_diag(lru_w_x[0]).astype(BF)
    pw = pool_w[0].astype(BF)

    def to_sibling(name, flight, after=None):
        (fl,), passed = _split_call(name + "_pass", finish=[flight], after=after,
                                    start=[_gather(flight.landed(), d2d=[(i, ALL) for i in range(len(flight.bufs))])])
        passed_on.append(passed)
        return fl

    passed_on = []

    def arrived(name, flight, after=None):
        _split_call(name + "_done", finish=[flight], after=after)
        return flight.landed()

    idx_big = jnp.stack([chip, (chip + 1) % NCHIP, (chip + 2) % NCHIP, (chip + 3) % NCHIP, core])
    proj, h1 = _fwd_inproj_own(xs, g1, fl_in.bufs[0], idx_big, stages=[_after(started)])
    fl_in = to_sibling("gather_w_in", fl_in, after=h1)
    w_in_f, = arrived("gather_w_in", fl_in)
    conv_all, = arrived("gather_conv", fl_conv)
    full["w_in"] = w_in_f
    conv_all = lax.dynamic_update_slice(conv_all, w["conv_w"][None], (chip, zero, zero))
    conv_full = jnp.transpose(conv_all, (1, 0, 2)).reshape(4, DR)
    proj = _fwd_inproj_rest(h1, w_in_f, proj, idx_big)
    fl_mix = to_sibling("gather_mix", fl_mix, after=proj)
    (ylru, hs), _ = _fwd_lru(proj, conv_full, conv_b, wa, lru_b_a, wx, lru_b_x, lru_lambda,
                             stages=[_after(passed_on[-1])])
    got = arrived("gather_mix", fl_mix, after=ylru)
    fl_ff1 = to_sibling("gather_ff1", fl_ff1, after=ylru)
    w_lru_up_f, w_pool_up_f, w_o_f = got[0].reshape(DR, D), got[1], got[2].reshape(D, D)
    ypool = _fwd_pool(proj, pw, pool_scale)
    (x2, h2, m, mrg, bra, brb), _ = _fwd_merge(xs, ylru, ypool, proj, b_gate, g2, g3, w_lru_up_f, w_pool_up_f, w_o_f,
                                               stages=[_after(passed_on[-1])])
    fl_ff2 = to_sibling("gather_ff2", fl_ff2, after=h2)
    ff1, = arrived("gather_ff1", fl_ff1, after=h2)
    ff2, = arrived("gather_ff2", fl_ff2)
    ff2 = ff2.reshape(DF, D)
    a1, f = _fwd_mlp(h2, ff1, ff2)
    lossp, dy, df, dg4 = _loss_head(f, x2, target, g4)

    dh2, df1 = _bwd_mlp_x(df, a1, ff1, ff2)
    dw_ff1, dw_ff2 = _bwd_mlp_w(df, h2, a1, df1)
    g_ff = [_halves(dw_ff1), _halves(dw_ff2)]
    (dxres, dgates, dylru, dypool, dm, dbra, dbrb, dg2, dg3, dbg), (l_ff,) = _bwd_merge(
        dh2, dy, x2, m, bra, brb, proj, b_gate, g2, g3, w_lru_up_f, w_pool_up_f, w_o_f, stages=[_to_sibling(g_ff)])
    p_ff = _add_sibling_multi(g_ff, l_ff, cidx, "add_sibling_ff")
    (fl_ff,), sent_ff = _split_call("reduce_ff_start", start=[_to_chips(p_ff)])
    (dw_o, dw_lru_up, dw_pool_up), _ = _dw_merge(mrg, dm, ylru, dbra, ypool, dbrb, stages=[_after(sent_ff)])
    g_mix = [_halves(dw_lru_up), _halves(dw_pool_up), _halves(dw_o)]
    (dxp, dgl, dcw, dcb, dwa, dba, dwx, dbx, dlam), (l_mix,) = _bwd_lru(
        proj, hs, dylru, conv_full, conv_b, wa, lru_b_a, wx, lru_b_x, lru_lambda, stages=[_to_sibling(g_mix)])
    p_mix = _add_sibling_multi(g_mix, l_mix, cidx, "add_sibling_mix")
    dxpool, dpw, dsc = _bwd_pool(proj, dypool, pw, pool_scale)
    dproj = [dxp, dgl, dxpool, dgates]
    small = jnp.concatenate([
        jnp.zeros((1, D), F32), dg2, dg3, dg4, dbg.reshape(2, D), dcb, dba, dbx, dlam,
        jnp.pad(dsc, ((0, 0), (0, D - DP))), jnp.pad(lossp, ((0, 0), (0, D - 1))), dcw,
        _block_diag_extract(dwa, hd).reshape(-1, D), _block_diag_extract(dwx, hd).reshape(-1, D),
        dpw.reshape(-1, D)], axis=0)
    (fl_mixr, fl_smalls), sent_mix = _split_call("reduce_mix_start", start=[_to_chips(p_mix), _to_sibling([small])])
    dw_in = _bwd_inproj_w(h1, dproj, sent_mix)
    _split_call("reduce_small_sibling_done", finish=[fl_smalls], after=dw_in)
    small, l_small = fl_smalls.bufs
    small2 = _add_pair(small, l_small, "add_sibling_small").reshape(2, SMALL_ROWS // 2, D)
    g_in = _halves(dw_in)
    done = ["w_ff1", "w_ff2"] + mix
    (fl_gin, fl_small), sib_started = _split_call("reduce_in_sibling_start",
                                                  start=[_to_sibling([g_in]), _to_chips([small2])])
    _, chips_done = _split_call("reduce_chips_done", finish=[fl_ff, fl_mixr], after=sib_started)
    p_ff1, p_ff2, c_ff1, c_ff2 = fl_ff.bufs
    p_mix, c_mix = fl_mixr.bufs[:3], fl_mixr.bufs[3:]
    pairs, _ = _add_chips_multi([p_ff1, p_ff2] + p_mix, [c_ff1, c_ff2] + c_mix, idx_big, "add_chips_done",
                                stages=[_after(chips_done)])
    _split_call("reduce_in_sibling_done", finish=[fl_gin], after=pairs[-1])
    g_in, l_in = fl_gin.bufs
    p_in = _add_sibling(g_in, l_in, cidx, "add_sibling_w_in")[0]
    ssem, rsem, p_in, c_in, token = _chips_start(p_in)
    _split_call("reduce_small_done", finish=[fl_small], after=token)
    small2, c_small = fl_small.bufs
    own_small = lax.dynamic_index_in_dim(small2, core, 0, keepdims=True)
    c_small = lax.dynamic_update_slice(c_small, own_small, (chip, zero, zero))
    pair_small = _add_chips(c_small, c_small, jnp.stack([zero, zero + 1, zero + 2, zero + 3, core]), "add_chips_small")
    (fl_share,), shared_start = _split_call("reduce_share_start", start=[_share(pairs + [pair_small])])
    grad_x, dg1 = _bwd_inproj_x(dproj, full["w_in"], xs, dxres, g1, stages=[_after(shared_start)])
    _split_call("reduce_share_done", finish=[fl_share], after=dg1)
    shared = fl_share.landed()
    pairs, pair_small = shared[:-1], shared[-1]

    grads, delta, new_m, new_v = {}, {}, {}, {}
    for n, p in zip(done, pairs):
        grads[n] = p.reshape(-1, p.shape[-1])

    def update(n, stages=()):
        (grads[n], delta[n], new_m[n], new_v[n]), landed = _adamw(w[n], grads[n], mom[n], var[n], "adamw_" + n,
                                                                  stages=stages)
        return landed

    p_in, c_in = _chips_wait(ssem, rsem, p_in, c_in, pairs[0])
    pair_in = _add_chips(p_in, c_in, idx_big, "add_chips_w_in")
    (fl_last, fl_dg1), last_start = _split_call("reduce_last_share_start", start=[_share([pair_in]), _to_everyone(dg1)])
    updated, _ = _adamw_multi(["w_ff1", "w_ff2", "w_o", "w_lru_up"], w, grads, mom, var, stages=[_after(last_start)])
    for n, (go, d, mo, vo) in updated.items():
        grads[n], delta[n], new_m[n], new_v[n] = go, d, mo, vo
    _split_call("reduce_last_share_done", finish=[fl_last, fl_dg1], after=new_v["w_lru_up"])
    (pair_in,), (dg1, dg1_all) = fl_last.landed(), fl_dg1.bufs
    dg1_all = lax.dynamic_update_slice(dg1_all, dg1[None], (2 * chip + core, zero, zero)).reshape(2 * NCHIP, D)
    grads["w_in"] = pair_in.reshape(-1, pair_in.shape[-1])
    update("w_pool_up")
    update("w_in")
    small_sum = pair_small.reshape(SMALL_ROWS, D)
    loss = 0.5 * small_sum[LOSS_ROW, 0]
    ccols = DR // NCHIP
    sep = [lax.dynamic_slice(small_sum[12:16], (zero, chip * ccols), (4, ccols)),
           small_sum[16:80].reshape(-1, hd), small_sum[80:144].reshape(-1, hd), small_sum[144:208].reshape(-1, PG)]
    g_s, d_s, m_s, v_s = _adamw_small(small_sum, dg1_all, sep, w, mom, var)
    grads.update(g_s)
    grads.update(dict(zip(SMALL_SEPARATE, sep)))
    delta.update(d_s)
    new_m.update(m_s)
    new_v.update(v_s)

    out = lambda d: [d[n].reshape(args[n].shape) for n in W_NAMES]
    return (loss, grad_x[None], *out(grads), *out(delta), *out(new_m), *out(new_v))
```

```python
import functools
import math

import jax
import jax.numpy as jnp
from jax import lax
from jax.experimental import pallas as pl
from jax.experimental.pallas import tpu as pltpu

F32 = jnp.float32
BF = jnp.bfloat16

T = 2048
D = 1024
DR = 1024
DP = 512
DF = 4096
DIN = 4608
NCHIP = 4
CW_IN = DIN // NCHIP
LANE = 128
CB = 128
NG = DR // CB
PG = 128
POOL_WINDOWS = (2, 4, 8, 16)
NORM_EPS = 1e-6
LRU_C = 8.0
GELU_C = math.sqrt(2.0 / math.pi)
ADAM_LR = 0.001
ADAM_B1 = 0.9
ADAM_B2 = 0.999
ADAM_EPS = 1e-08
ADAM_WD = 0.01
ADAM_STEP = 10
MESH_ID = pl.DeviceIdType.MESH
ANY = pl.BlockSpec(memory_space=pl.ANY)
SMALL_ROWS = 208
LOSS_ROW = 11
MIB = 1 << 20


def _cp(vmem_mib=None):
    if vmem_mib is None:
        return pltpu.CompilerParams()
    return pltpu.CompilerParams(vmem_limit_bytes=vmem_mib * MIB)


def _hbm(*arrays):
    return [pltpu.with_memory_space_constraint(a, pltpu.HBM) for a in arrays]


def _hbm_out(shapes):
    return [pltpu.HBM(s.shape, s.dtype) for s in shapes]


class _Stage:
    def __init__(self, operands, out_shape, alias, sems, start, finish):
        self.operands, self.out_shape, self.alias, self.sems = list(operands), list(out_shape), dict(alias), list(sems)
        self.start, self.finish = start, finish


def _call(body, *, name, grid, in_specs, out_specs, out_shape, args, vmem=None, stages=(), prefetch=None,
          scratch=()):
    nin, nout = len(in_specs), len(out_specs)
    npre = 0 if prefetch is None else 1
    st_args, st_shapes, st_sems, aliases = [], [], list(scratch), {}
    for st in stages:
        for k, v in st.alias.items():
            aliases[npre + nin + len(st_args) + k] = nout + len(st_shapes) + v
        st_args += st.operands
        st_shapes += st.out_shape
        st_sems += st.sems

    def wrapped(*refs):
        pre, refs = refs[:npre], refs[npre:]
        ins, pos = refs[:nin], nin
        st_ins = []
        for st in stages:
            st_ins.append(refs[pos:pos + len(st.operands)])
            pos += len(st.operands)
        outs, pos = refs[pos:pos + nout], pos + nout
        st_outs = []
        for st in stages:
            st_outs.append(refs[pos:pos + len(st.out_shape)])
            pos += len(st.out_shape)
        work, pos = refs[pos:pos + len(scratch)], pos + len(scratch)
        sems = []
        for st in stages:
            sems.append(refs[pos:pos + len(st.sems)])
            pos += len(st.sems)
        if stages:
            first = functools.reduce(jnp.logical_and, [pl.program_id(a) == 0 for a in range(len(grid))])

            @pl.when(first)
            def _():
                for st, a, b, s in zip(stages, st_ins, st_outs, sems):
                    st.start(a, b, s)

        body(*pre, *ins, *outs, *work)
        if stages:
            last = functools.reduce(jnp.logical_and, [pl.program_id(a) == g - 1 for a, g in enumerate(grid)])

            @pl.when(last)
            def _():
                for st, a, b, s in zip(stages, st_ins, st_outs, sems):
                    st.finish(a, b, s)

    all_in = list(in_specs) + [ANY] * len(st_args)
    all_out = list(out_specs) + [ANY] * len(st_shapes)
    kw = dict(has_side_effects=True) if stages else {}
    if vmem is not None:
        kw["vmem_limit_bytes"] = vmem * MIB
    if prefetch is None:
        gkw = dict(grid=grid, in_specs=all_in, out_specs=all_out, scratch_shapes=st_sems)
    else:
        gkw = dict(grid_spec=pltpu.PrefetchScalarGridSpec(
            num_scalar_prefetch=1, grid=grid, in_specs=all_in, out_specs=all_out, scratch_shapes=st_sems))
    res = pl.pallas_call(
        wrapped, name=name, out_shape=_hbm_out(list(out_shape) + st_shapes), input_output_aliases=aliases,
        compiler_params=pltpu.CompilerParams(**kw), **gkw,
    )(*([prefetch] if npre else []), *_hbm(*args, *st_args))
    outs, rest, st_res = list(res[:nout]), list(res[nout:]), []
    for st in stages:
        st_res.append(rest[:len(st.out_shape)])
        rest = rest[len(st.out_shape):]
    return outs, st_res


def _mm(a, b):
    return jnp.dot(a.astype(BF), b.astype(BF), preferred_element_type=F32)


def _mm_nt(a, b):
    return lax.dot_general(a.astype(BF), b.astype(BF), (((1,), (1,)), ((), ())),
                           preferred_element_type=F32)


def _mm_tn(a, b):
    return lax.dot_general(a.astype(BF), b.astype(BF), (((0,), (0,)), ((), ())),
                           preferred_element_type=F32)


def _rows(v):
    return lax.broadcasted_iota(jnp.int32, v.shape, 0)


def _sd(v, s, fill=0.0):
    return jnp.where(_rows(v) >= s, pltpu.roll(v, s, axis=0), fill)


def _su(v, s, fill=0.0):
    n = v.shape[0]
    return jnp.where(_rows(v) < n - s, pltpu.roll(v, n - s, axis=0), fill)


def _sigmoid(z):
    return 1.0 / (1.0 + jnp.exp(-z))


def _softplus(z):
    e = jnp.exp(-jnp.abs(z))
    u = 1.0 + e
    d = u - 1.0
    log1p = jnp.where(d == 0.0, e, jnp.log(u) * (e / jnp.where(d == 0.0, 1.0, d)))
    return jnp.maximum(z, 0.0) + log1p


def _mean(v):
    return jnp.mean(v, axis=-1, keepdims=True)


def _colsum(v):
    return jnp.sum(v, axis=0, keepdims=True)


def _acc(ref, val, first):
    @pl.when(first)
    def _():
        ref[...] = val

    @pl.when(jnp.logical_not(first))
    def _():
        ref[...] += val


def _conv(xp, cw, cb):
    x1, x2, x3 = _sd(xp, 1), _sd(xp, 2), _sd(xp, 3)
    xc = cb + cw[0:1] * x3 + cw[1:2] * x2 + cw[2:3] * x1 + cw[3:4] * xp
    return xc, x1, x2, x3


def _lru_gates(xc, wa, ba, wx, bx, lam):
    xcb = xc.astype(BF)
    r = _sigmoid(_mm(xcb, wa) + ba)
    ii = _sigmoid(_mm(xcb, wx) + bx)
    sp = _softplus(-lam)
    la = (-LRU_C) * r * sp
    a = jnp.exp(la)
    mult = jnp.sqrt(-jnp.tanh(la) * (a * a + 1.0))
    return xcb, r, ii, sp, a, mult


def _gelu_parts(g):
    th = jnp.tanh(GELU_C * (g + 0.044715 * (g * g * g)))
    gel = 0.5 * g * (1.0 + th)
    dgel = 0.5 * (1.0 + th) + 0.5 * g * (1.0 - th * th) * (GELU_C * (1.0 + 3.0 * 0.044715 * (g * g)))
    return gel, dgel


def _tile_scan(a, b, a_s, b_s, out_ref, reverse):
    n, lanes = a.shape
    nt = n // 8
    a, b = a.reshape(nt, 8, lanes), b.reshape(nt, 8, lanes)
    sub = lax.broadcasted_iota(jnp.int32, a.shape, 1)
    s = 1
    while s < 8:
        keep = sub < 8 - s if reverse else sub >= s
        amount = 8 - s if reverse else s
        b = b + a * jnp.where(keep, pltpu.roll(b, amount, axis=1), 0.0)
        a = a * jnp.where(keep, pltpu.roll(a, amount, axis=1), 1.0)
        s *= 2
    a_s[...] = a.reshape(n, lanes)
    b_s[...] = b.reshape(n, lanes)
    edge = pl.ds(0 if reverse else 7, nt, stride=8)
    ta, tb = a_s[edge, :], b_s[edge, :]
    shift = _su if reverse else _sd
    s = 1
    while s < nt:
        tb = tb + ta * shift(tb, s, 0.0)
        if 2 * s < nt:
            ta = ta * shift(ta, s, 1.0)
        s *= 2
    enters = shift(tb, 1, 0.0)
    for o in range(8):
        rows = pl.ds(o, nt, stride=8)
        out_ref[rows, :] = b_s[rows, :] + a_s[rows, :] * enters


def _pool_window(x, steps, shift):
    s, sh = x, 1
    for _ in range(steps):
        s = s + shift(s, sh)
        sh *= 2
    return s


def _fwd_inproj_own(x, g1, w_in, slots, stages=()):
    tm = 512

    def body(s_ref, x_ref, g_ref, w_ref, proj_ref, h_ref):
        xv = x_ref[...]
        r = lax.rsqrt(_mean(xv * xv) + NORM_EPS)
        h = ((xv * r) * g_ref[...]).astype(BF)
        h_ref[...] = h
        proj_ref[...] = jnp.dot(h, w_ref[0], preferred_element_type=F32)

    return _call(
        body, name="fwd_inproj_own", grid=(T // tm,), prefetch=slots,
        in_specs=[pl.BlockSpec((tm, D), lambda i, s: (i, 0)),
                  pl.BlockSpec((1, D), lambda i, s: (0, 0)),
                  pl.BlockSpec((1, D, CW_IN), lambda i, s: (s[0], 0, 0))],
        out_specs=[pl.BlockSpec((tm, CW_IN), lambda i, s: (i, s[0])),
                   pl.BlockSpec((tm, D), lambda i, s: (i, 0))],
        out_shape=[jax.ShapeDtypeStruct((T, DIN), F32), jax.ShapeDtypeStruct((T, D), BF)],
        vmem=40, args=[x, g1, w_in], stages=stages)[0]


def _fwd_inproj_rest(h1, w_in, proj, slots):
    tm = 1024

    def body(s_ref, h_ref, w_ref, p_in, proj_ref):
        proj_ref[...] = jnp.dot(h_ref[...], w_ref[0], preferred_element_type=F32)

    res = pl.pallas_call(
        body, name="fwd_inproj_rest",
        grid_spec=pltpu.PrefetchScalarGridSpec(
            num_scalar_prefetch=1, grid=(NCHIP - 1, T // tm),
            in_specs=[pl.BlockSpec((tm, D), lambda k, i, s: (i, 0)),
                      pl.BlockSpec((1, D, CW_IN), lambda k, i, s: (s[1 + k], 0, 0)), ANY],
            out_specs=pl.BlockSpec((tm, CW_IN), lambda k, i, s: (i, s[1 + k]))),
        out_shape=pltpu.HBM((T, DIN), F32), input_output_aliases={3: 0},
        compiler_params=_cp(40),
    )(slots, *_hbm(h1, w_in, proj))
    return res


def _vec_spec():
    return pl.BlockSpec((1, CB), lambda j: (0, j))


def _fwd_lru(proj, conv_w, conv_b, wa, ba, wx, bx, lam, stages=()):
    def body(xp_ref, g_ref, cw_ref, cb_ref, wa_ref, ba_ref, wx_ref, bx_ref, lam_ref, y_ref, h_ref, a_s, b_s):
        xc, _, _, _ = _conv(xp_ref[...], cw_ref[...], cb_ref[...])
        _, _, ii, _, a, mult = _lru_gates(xc, wa_ref[0], ba_ref[...], wx_ref[0], bx_ref[...], lam_ref[...])
        _tile_scan(a, mult * (ii * xc), a_s, b_s, h_ref, reverse=False)
        gel, _ = _gelu_parts(g_ref[...])
        y_ref[...] = (h_ref[...] * gel).astype(BF)

    return _call(
        body, name="fwd_lru", grid=(NG,),
        in_specs=[pl.BlockSpec((T, CB), lambda j: (0, j)),
                  pl.BlockSpec((T, CB), lambda j: (0, NG + j)),
                  pl.BlockSpec((4, CB), lambda j: (0, j)),
                  _vec_spec(),
                  pl.BlockSpec((1, CB, CB), lambda j: (j, 0, 0)), _vec_spec(),
                  pl.BlockSpec((1, CB, CB), lambda j: (j, 0, 0)), _vec_spec(),
                  _vec_spec()],
        out_specs=[pl.BlockSpec((T, CB), lambda j: (0, j)), pl.BlockSpec((T, CB), lambda j: (0, j))],
        out_shape=[jax.ShapeDtypeStruct((T, DR), BF), jax.ShapeDtypeStruct((T, DR), F32)],
        vmem=48, args=[proj, proj, conv_w, conv_b, wa, ba, wx, bx, lam], stages=stages,
        scratch=[pltpu.VMEM((T, CB), F32)] * 2)


def _pool_cnt(w):
    t = lax.broadcasted_iota(jnp.int32, (T, 1), 0)
    return jnp.minimum(t + 1, w).astype(F32)


def _fwd_pool(proj, pool_w, pool_scale):
    def body(xp_ref, pw_ref, sc_ref, y_ref):
        for g, w in enumerate(POOL_WINDOWS):
            cols = slice(g * PG, (g + 1) * PG)
            x = xp_ref[:, cols]
            p = _pool_window(x, g + 1, _sd) / _pool_cnt(w) - x
            y_ref[:, cols] = (_mm(p, pw_ref[g]) * sc_ref[:, cols]).astype(BF)

    return pl.pallas_call(
        body, name="fwd_pool", grid=(1,),
        in_specs=[pl.BlockSpec((T, DP), lambda i: (0, 2 * DR // DP)),
                  pl.BlockSpec((4, PG, PG), lambda i: (0, 0, 0)),
                  pl.BlockSpec((1, DP), lambda i: (0, 0))],
        out_specs=pl.BlockSpec((T, DP), lambda i: (0, 0)),
        out_shape=pltpu.HBM((T, DP), BF),
        compiler_params=_cp(48),
    )(*_hbm(proj, pool_w, pool_scale))


GATE_BLK = 512
GATE_BLK0 = (2 * DR + DP) // GATE_BLK


def _gate_specs(tm):
    return [pl.BlockSpec((tm, GATE_BLK), functools.partial(lambda i, q: (i, GATE_BLK0 + q), q=q))
            for q in range(4)]


def _fwd_merge(x, ylru, ypool, proj, b_gate, g2, g3, w_lru_up, w_pool_up, w_o, stages=()):
    tm = 512

    def body(x_ref, yl_ref, yp_ref, p0, p1, p2, p3, bg_ref, g2_ref, g3_ref, wl_ref, wp_ref, wo_ref,
             x2_ref, h2_ref, m_ref, mrg_ref, bra_ref, brb_ref):
        bra = jnp.dot(yl_ref[...], wl_ref[...], preferred_element_type=F32)
        yp = yp_ref[...]
        brb = jnp.concatenate([jnp.dot(yp, wp_ref[k], preferred_element_type=F32) for k in range(NCHIP)], axis=1)
        bg = bg_ref[...]
        ga = _sigmoid(jnp.concatenate([p0[...], p1[...]], axis=1) + bg[:, :D])
        gb = _sigmoid(jnp.concatenate([p2[...], p3[...]], axis=1) + bg[:, D:])
        mrg = (ga * bra + gb * brb).astype(BF)
        m = jnp.dot(mrg, wo_ref[...], preferred_element_type=F32)
        r2 = lax.rsqrt(_mean(m * m) + NORM_EPS)
        x2 = x_ref[...] + (m * r2) * g2_ref[...]
        r3 = lax.rsqrt(_mean(x2 * x2) + NORM_EPS)
        x2_ref[...] = x2
        h2_ref[...] = ((x2 * r3) * g3_ref[...]).astype(BF)
        m_ref[...] = m
        mrg_ref[...] = mrg
        bra_ref[...] = bra.astype(BF)
        brb_ref[...] = brb.astype(BF)

    row = lambda w: pl.BlockSpec((tm, w), lambda i: (i, 0))
    full2 = lambda a, b: pl.BlockSpec((a, b), lambda i: (0, 0))
    return _call(
        body, name="fwd_merge", grid=(T // tm,),
        in_specs=[row(D), row(DR), row(DP)] + _gate_specs(tm) +
                 [full2(1, 2 * D), full2(1, D), full2(1, D), full2(DR, D),
                  pl.BlockSpec((NCHIP, DP, D // NCHIP), lambda i: (0, 0, 0)), full2(D, D)],
        out_specs=[row(D)] * 6,
        out_shape=[jax.ShapeDtypeStruct((T, D), F32), jax.ShapeDtypeStruct((T, D), BF),
                   jax.ShapeDtypeStruct((T, D), F32), jax.ShapeDtypeStruct((T, D), BF),
                   jax.ShapeDtypeStruct((T, D), BF), jax.ShapeDtypeStruct((T, D), BF)],
        vmem=48, args=[x, ylru, ypool, proj, proj, proj, proj, b_gate, g2, g3, w_lru_up, w_pool_up, w_o],
        stages=stages)


def _fwd_mlp(h2, w_ff1, w_ff2):
    tm = 512
    fk = DF // NCHIP

    def body(h_ref, w1_ref, w2_ref, a1_ref, f_ref):
        h = h_ref[...]
        f = None
        for k in range(NCHIP):
            a1 = jnp.maximum(jnp.dot(h, w1_ref[k], preferred_element_type=F32), 0.0)
            a1_ref[:, k * fk:(k + 1) * fk] = a1.astype(BF)
            part = jnp.dot((a1 * a1).astype(BF), w2_ref[k * fk:(k + 1) * fk, :], preferred_element_type=F32)
            f = part if f is None else f + part
        f_ref[...] = f

    return pl.pallas_call(
        body, name="fwd_mlp", grid=(T // tm,),
        in_specs=[pl.BlockSpec((tm, D), lambda i: (i, 0)),
                  pl.BlockSpec((NCHIP, D, fk), lambda i: (0, 0, 0)),
                  pl.BlockSpec((DF, D), lambda i: (0, 0))],
        out_specs=[pl.BlockSpec((tm, DF), lambda i: (i, 0)), pl.BlockSpec((tm, D), lambda i: (i, 0))],
        out_shape=_hbm_out([jax.ShapeDtypeStruct((T, DF), BF), jax.ShapeDtypeStruct((T, D), F32)]),
        compiler_params=_cp(56),
    )(*_hbm(h2, w_ff1, w_ff2))


def _loss_head(f, x2, target, g4):
    tm = 512

    def body(f_ref, x2_ref, t_ref, g_ref, loss_ref, dy_ref, df_ref, dg_ref):
        first = pl.program_id(0) == 0
        f = f_ref[...]
        g4v = g_ref[...]
        r4 = lax.rsqrt(_mean(f * f) + NORM_EPS)
        fn = f * r4
        e = (x2_ref[...] + fn * g4v) - t_ref[...]
        _acc(loss_ref, jnp.sum(_mean(e * e), axis=0, keepdims=True), first)
        dy = e * (1.0 / D)
        dy_ref[...] = dy
        _acc(dg_ref, _colsum(dy * fn), first)
        dfn = dy * g4v
        df_ref[...] = (r4 * (dfn - fn * _mean(dfn * fn))).astype(BF)

    row = pl.BlockSpec((tm, D), lambda i: (i, 0))
    return pl.pallas_call(
        body, name="loss_head", grid=(T // tm,),
        in_specs=[row, row, row, pl.BlockSpec((1, D), lambda i: (0, 0))],
        out_specs=[pl.BlockSpec((1, 1), lambda i: (0, 0)), row, row, pl.BlockSpec((1, D), lambda i: (0, 0))],
        out_shape=_hbm_out([jax.ShapeDtypeStruct((1, 1), F32), jax.ShapeDtypeStruct((T, D), F32),
                            jax.ShapeDtypeStruct((T, D), BF), jax.ShapeDtypeStruct((1, D), F32)]),
        compiler_params=_cp(48),
    )(*_hbm(f, x2, target, g4))


def _bwd_mlp_x(df, a1, w_ff1, w_ff2):
    tm = 512
    fk = DF // NCHIP

    def body(df_ref, a1_ref, w1_ref, w2_ref, dh_ref, df1_ref):
        df = df_ref[...]
        dh = None
        for k in range(NCHIP):
            cols = slice(k * fk, (k + 1) * fk)
            dact = _mm_nt(df, w2_ref[cols, :])
            df1 = (dact * (2.0 * a1_ref[:, cols].astype(F32))).astype(BF)
            df1_ref[:, cols] = df1
            part = _mm_nt(df1, w1_ref[k])
            dh = part if dh is None else dh + part
        dh_ref[...] = dh

    return pl.pallas_call(
        body, name="bwd_mlp_x", grid=(T // tm,),
        in_specs=[pl.BlockSpec((tm, D), lambda i: (i, 0)),
                  pl.BlockSpec((tm, DF), lambda i: (i, 0)),
                  pl.BlockSpec((NCHIP, D, fk), lambda i: (0, 0, 0)),
                  pl.BlockSpec((DF, D), lambda i: (0, 0))],
        out_specs=[pl.BlockSpec((tm, D), lambda i: (i, 0)), pl.BlockSpec((tm, DF), lambda i: (i, 0))],
        out_shape=_hbm_out([jax.ShapeDtypeStruct((T, D), F32), jax.ShapeDtypeStruct((T, DF), BF)]),
        compiler_params=_cp(56),
    )(*_hbm(df, a1, w_ff1, w_ff2))


def _bwd_mlp_w(df, h2, a1, df1):
    fc = 512
    per = (DF // NCHIP) // fc

    def body(df_ref, h_ref, a1_ref, df1_ref, dw1_ref, dw2_ref):
        a1 = a1_ref[...].astype(F32)
        dw2_ref[...] = _mm_tn((a1 * a1).astype(BF), df_ref[...]).astype(BF)
        dw1_ref[0] = _mm_tn(h_ref[...], df1_ref[...]).astype(BF)

    return pl.pallas_call(
        body, name="bwd_mlp_w", grid=(DF // fc,),
        in_specs=[pl.BlockSpec((T, D), lambda j: (0, 0)),
                  pl.BlockSpec((T, D), lambda j: (0, 0)),
                  pl.BlockSpec((T, fc), lambda j: (0, j)),
                  pl.BlockSpec((T, fc), lambda j: (0, j))],
        out_specs=[pl.BlockSpec((1, D, fc), lambda j: (j // per, 0, j % per)),
                   pl.BlockSpec((fc, D), lambda j: (j, 0))],
        out_shape=_hbm_out([jax.ShapeDtypeStruct((NCHIP, D, DF // NCHIP), BF),
                            jax.ShapeDtypeStruct((DF, D), BF)]),
        compiler_params=_cp(56),
    )(*_hbm(df, h2, a1, df1))


def _bwd_merge(dh2, dy, x2, m, bra, brb, proj, b_gate, g2, g3, w_lru_up, w_pool_up, w_o, stages=()):
    tm = 256
    cpu = D // NCHIP

    def body(dh2_ref, dy_ref, x2_ref, m_ref, bra_ref, brb_ref, p0, p1, p2, p3, bg_ref,
             g2_ref, g3_ref, wl_ref, wp_ref, wo_ref,
             dx_ref, dgt_ref, dyl_ref, dyp_ref, dm_ref, dbra_ref, dbrb_ref, dg2_ref, dg3_ref, dbg_ref):
        first = pl.program_id(0) == 0
        x2 = x2_ref[...]
        r3 = lax.rsqrt(_mean(x2 * x2) + NORM_EPS)
        x2n = x2 * r3
        dh2 = dh2_ref[...]
        t3 = dh2 * g3_ref[...]
        dx2 = dy_ref[...] + r3 * (t3 - x2n * _mean(t3 * x2n))
        dx_ref[...] = dx2
        _acc(dg3_ref, _colsum(dh2 * x2n), first)
        m = m_ref[...]
        r2 = lax.rsqrt(_mean(m * m) + NORM_EPS)
        mn = m * r2
        _acc(dg2_ref, _colsum(dx2 * mn), first)
        dmn = dx2 * g2_ref[...]
        dm = (r2 * (dmn - mn * _mean(dmn * mn))).astype(BF)
        dm_ref[...] = dm
        dmrg = _mm_nt(dm, wo_ref[...])
        bg = bg_ref[...]
        ga = _sigmoid(jnp.concatenate([p0[...], p1[...]], axis=1) + bg[:, :D])
        gb = _sigmoid(jnp.concatenate([p2[...], p3[...]], axis=1) + bg[:, D:])
        dga = dmrg * bra_ref[...].astype(F32) * (ga * (1.0 - ga))
        dgb = dmrg * brb_ref[...].astype(F32) * (gb * (1.0 - gb))
        dgt_ref[:, :D] = dga.astype(BF)
        dgt_ref[:, D:] = dgb.astype(BF)
        _acc(dbg_ref, jnp.concatenate([_colsum(dga), _colsum(dgb)], axis=1), first)
        dbra = (dmrg * ga).astype(BF)
        dbrb = (dmrg * gb).astype(BF)
        dbra_ref[...] = dbra
        dbrb_ref[...] = dbrb
        dyl_ref[...] = _mm_nt(dbra, wl_ref[...])
        dyp = None
        for k in range(NCHIP):
            part = _mm_nt(dbrb[:, k * cpu:(k + 1) * cpu], wp_ref[k])
            dyp = part if dyp is None else dyp + part
        dyp_ref[...] = dyp

    row = lambda w: pl.BlockSpec((tm, w), lambda i: (i, 0))
    full2 = lambda a, b: pl.BlockSpec((a, b), lambda i: (0, 0))
    wp_spec = pl.BlockSpec((NCHIP, DP, cpu), lambda i: (0, 0, 0))
    return _call(
        body, name="bwd_merge", grid=(T // tm,),
        in_specs=[row(D)] * 6 + _gate_specs(tm) +
                 [full2(1, 2 * D), full2(1, D), full2(1, D), full2(DR, D), wp_spec, full2(D, D)],
        out_specs=[row(D), row(2 * D), row(DR), row(DP), row(D), row(D), row(D),
                   full2(1, D), full2(1, D), full2(1, 2 * D)],
        out_shape=[jax.ShapeDtypeStruct((T, D), F32), jax.ShapeDtypeStruct((T, 2 * D), BF),
                   jax.ShapeDtypeStruct((T, DR), F32), jax.ShapeDtypeStruct((T, DP), F32),
                   jax.ShapeDtypeStruct((T, D), BF), jax.ShapeDtypeStruct((T, D), BF),
                   jax.ShapeDtypeStruct((T, D), BF),
                   jax.ShapeDtypeStruct((1, D), F32), jax.ShapeDtypeStruct((1, D), F32),
                   jax.ShapeDtypeStruct((1, 2 * D), F32)],
        vmem=56, args=[dh2, dy, x2, m, bra, brb, proj, proj, proj, proj, b_gate, g2, g3, w_lru_up, w_pool_up, w_o],
        stages=stages)


def _dw_merge(mrg, dm, ylru, dbra, ypool, dbrb, stages=()):
    nb = NCHIP
    rb, pb, cpu = D // nb, DP // nb, D // NCHIP

    def body(mrg_ref, dm_ref, yl_ref, dbra_ref, yp_ref, dbrb_ref, dwo_ref, dwl_ref, dwp_ref):
        dwo_ref[...] = _mm_tn(mrg_ref[...], dm_ref[...]).astype(BF)
        dwl_ref[...] = _mm_tn(yl_ref[...], dbra_ref[...]).astype(BF)
        dwp = _mm_tn(yp_ref[...], dbrb_ref[...]).astype(BF)
        for k in range(NCHIP):
            dwp_ref[k] = dwp[:, k * cpu:(k + 1) * cpu]

    cols = lambda w: pl.BlockSpec((T, w), lambda r: (0, r))
    whole = pl.BlockSpec((T, D), lambda r: (0, 0))
    return _call(
        body, name="dw_merge", grid=(nb,),
        in_specs=[cols(rb), whole, cols(rb), whole, cols(pb), whole],
        out_specs=[pl.BlockSpec((rb, D), lambda r: (r, 0)), pl.BlockSpec((rb, D), lambda r: (r, 0)),
                   pl.BlockSpec((NCHIP, pb, cpu), lambda r: (0, r, 0))],
        out_shape=[jax.ShapeDtypeStruct((D, D), BF), jax.ShapeDtypeStruct((DR, D), BF),
                   jax.ShapeDtypeStruct((NCHIP, DP, cpu), BF)],
        vmem=56, args=[mrg, dm, ylru, dbra, ypool, dbrb], stages=stages)


def _bwd_lru(proj, h, dylru, conv_w, conv_b, wa, ba, wx, bx, lam, stages=()):
    def body(xp_ref, g_ref, h_ref, dy_ref, cw_ref, cb_ref, wa_ref, ba_ref, wx_ref, bx_ref, lam_ref,
             dxp_ref, dg_ref, dcw_ref, dcb_ref, dwa_ref, dba_ref, dwx_ref, dbx_ref, dlam_ref, a_s, b_s, l_s):
        xp = xp_ref[...]
        cw = cw_ref[...]
        lam = lam_ref[...]
        xc, x1, x2, x3 = _conv(xp, cw, cb_ref[...])
        wa, wx = wa_ref[0], wx_ref[0]
        xcb, r, ii, sp, a, mult = _lru_gates(xc, wa, ba_ref[...], wx, bx_ref[...], lam)
        g = g_ref[...]
        gel, dgel = _gelu_parts(g)
        h = h_ref[...]
        dy = dy_ref[...]
        dg_ref[...] = (dy * h * dgel).astype(BF)
        _tile_scan(_su(a, 1, 0.0), dy * gel, a_s, b_s, l_s, reverse=True)
        b = l_s[...]
        da = b * _sd(h, 1, 0.0)
        dmult = b * (ii * xc)
        dii = b * (mult * xc)
        dxc = b * (mult * ii)
        dla = da * a - dmult * ((a * a) / mult)
        dr = dla * ((-LRU_C) * sp)
        dsp = _colsum(dla * ((-LRU_C) * r))
        dlam_ref[...] = -dsp / (1.0 + jnp.exp(lam))
        dzr = dr * (r * (1.0 - r))
        dzi = dii * (ii * (1.0 - ii))
        dzrb, dzib = dzr.astype(BF), dzi.astype(BF)
        dxc = dxc + _mm_nt(dzrb, wa) + _mm_nt(dzib, wx)
        dwa_ref[0] = _mm_tn(xcb, dzrb)
        dwx_ref[0] = _mm_tn(xcb, dzib)
        dba_ref[...] = _colsum(dzr)
        dbx_ref[...] = _colsum(dzi)
        dcb_ref[...] = _colsum(dxc)
        dcw_ref[...] = jnp.concatenate([_colsum(dxc * x3), _colsum(dxc * x2), _colsum(dxc * x1),
                                        _colsum(dxc * xp)], axis=0)
        dxp = cw[3:4] * dxc + cw[2:3] * _su(dxc, 1) + cw[1:2] * _su(dxc, 2) + cw[0:1] * _su(dxc, 3)
        dxp_ref[...] = dxp.astype(BF)

    blk = pl.BlockSpec((T, CB), lambda j: (0, j))
    wsp = pl.BlockSpec((1, CB, CB), lambda j: (j, 0, 0))
    return _call(
        body, name="bwd_lru", grid=(NG,),
        in_specs=[blk, pl.BlockSpec((T, CB), lambda j: (0, NG + j)), blk, blk,
                  pl.BlockSpec((4, CB), lambda j: (0, j)), _vec_spec(), wsp, _vec_spec(), wsp, _vec_spec(),
                  _vec_spec()],
        out_specs=[blk, blk, pl.BlockSpec((4, CB), lambda j: (0, j)), _vec_spec(), wsp, _vec_spec(), wsp,
                   _vec_spec(), _vec_spec()],
        out_shape=[jax.ShapeDtypeStruct((T, DR), BF), jax.ShapeDtypeStruct((T, DR), BF),
                   jax.ShapeDtypeStruct((4, DR), F32), jax.ShapeDtypeStruct((1, DR), F32),
                   jax.ShapeDtypeStruct((NG, CB, CB), F32), jax.ShapeDtypeStruct((1, DR), F32),
                   jax.ShapeDtypeStruct((NG, CB, CB), F32), jax.ShapeDtypeStruct((1, DR), F32),
                   jax.ShapeDtypeStruct((1, DR), F32)],
        vmem=56, args=[proj, proj, h, dylru, conv_w, conv_b, wa, ba, wx, bx, lam], stages=stages,
        scratch=[pltpu.VMEM((T, CB), F32)] * 3)


def _bwd_pool(proj, dypool, pool_w, pool_scale):
    def body(xp_ref, dy_ref, pw_ref, sc_ref, dx_ref, dw_ref, dsc_ref):
        for g, w in enumerate(POOL_WINDOWS):
            cols = slice(g * PG, (g + 1) * PG)
            cnt = _pool_cnt(w)
            x = xp_ref[:, cols]
            pb = (_pool_window(x, g + 1, _sd) / cnt - x).astype(BF)
            wg = pw_ref[g]
            dy = dy_ref[:, cols]
            dsc_ref[:, cols] = _colsum(dy * _mm(pb, wg))
            dyp = (dy * sc_ref[:, cols]).astype(BF)
            dw_ref[g] = _mm_tn(pb, dyp)
            dp = _mm_nt(dyp, wg)
            dx_ref[:, cols] = (_pool_window(dp / cnt, g + 1, _su) - dp).astype(BF)

    return pl.pallas_call(
        body, name="bwd_pool", grid=(1,),
        in_specs=[pl.BlockSpec((T, DP), lambda i: (0, 2 * DR // DP)),
                  pl.BlockSpec((T, DP), lambda i: (0, 0)),
                  pl.BlockSpec((4, PG, PG), lambda i: (0, 0, 0)),
                  pl.BlockSpec((1, DP), lambda i: (0, 0))],
        out_specs=[pl.BlockSpec((T, DP), lambda i: (0, 0)),
                   pl.BlockSpec((4, PG, PG), lambda i: (0, 0, 0)),
                   pl.BlockSpec((1, DP), lambda i: (0, 0))],
        out_shape=_hbm_out([jax.ShapeDtypeStruct((T, DP), BF), jax.ShapeDtypeStruct((4, PG, PG), F32),
                            jax.ShapeDtypeStruct((1, DP), F32)]),
        compiler_params=_cp(48),
    )(*_hbm(proj, dypool, pool_w, pool_scale))


PART_COLS = (DR, DR, DP, 2 * D)


def _shard_pieces():
    starts = [sum(PART_COLS[:p]) for p in range(len(PART_COLS))]
    shards = []
    for k in range(NCHIP):
        lo, hi = k * CW_IN, (k + 1) * CW_IN
        shards.append([(p, max(lo, s) - s, min(hi, s + wd) - s, max(lo, s) - lo)
                       for p, (s, wd) in enumerate(zip(starts, PART_COLS)) if max(lo, s) < min(hi, s + wd)])
    return shards


def _bwd_inproj_w(h1, parts, after):
    def body(h_ref, p0, p1, p2, p3, after_ref, dw_ref):
        part_refs = (p0, p1, p2, p3)
        for k, pieces in enumerate(_shard_pieces()):
            for p, a, b, c0 in pieces:
                dw_ref[k, :, c0:c0 + b - a] = _mm_tn(h_ref[...], part_refs[p][:, a:b]).astype(BF)

    vmem = pl.BlockSpec(memory_space=pltpu.VMEM)
    return pl.pallas_call(
        body, name="bwd_inproj_w", in_specs=[vmem] * 5 + [ANY], out_specs=vmem,
        out_shape=pltpu.HBM((NCHIP, D, CW_IN), BF), compiler_params=_cp(48),
    )(*_hbm(h1, *parts), after)


def _bwd_inproj_x(parts, w_in, x, dxres, g1, stages=()):
    tm = 512

    def body(p0, p1, p2, p3, w_ref, x_ref, dr_ref, g_ref, dx_ref, dg_ref):
        part_refs = (p0, p1, p2, p3)
        dh = None
        for k, pieces in enumerate(_shard_pieces()):
            for p, a, b, c0 in pieces:
                part = _mm_nt(part_refs[p][:, a:b], w_ref[k, :, c0:c0 + b - a])
                dh = part if dh is None else dh + part
        xv = x_ref[...]
        r = lax.rsqrt(_mean(xv * xv) + NORM_EPS)
        xn = xv * r
        t = dh * g_ref[...]
        dx_ref[...] = dr_ref[...] + r * (t - xn * _mean(t * xn))
        _acc(dg_ref, _colsum(dh * xn), pl.program_id(0) == 0)

    row = pl.BlockSpec((tm, D), lambda i: (i, 0))
    vec = pl.BlockSpec((1, D), lambda i: (0, 0))
    return _call(
        body, name="bwd_inproj_x", grid=(T // tm,),
        in_specs=[pl.BlockSpec((tm, wd), lambda i: (i, 0)) for wd in PART_COLS] +
                 [pl.BlockSpec((NCHIP, D, CW_IN), lambda i: (0, 0, 0)), row, row, vec],
        out_specs=[row, vec],
        out_shape=[jax.ShapeDtypeStruct((T, D), F32), jax.ShapeDtypeStruct((1, D), F32)],
        vmem=56, args=[*parts, w_in, x, dxres, g1], stages=stages)[0]


def _place():
    x, y, c = lax.axis_index("x"), lax.axis_index("y"), lax.axis_index("c")
    chips = [(1 - x, y), (x, 1 - y), (1 - x, 1 - y)]
    return x, y, c, chips


def _rcopy(src, dst, ssem, rsem, dev):
    return pltpu.make_async_remote_copy(src_ref=src, dst_ref=dst, send_sem=ssem, recv_sem=rsem,
                                        device_id=dev, device_id_type=MESH_ID)


def _sds(a):
    return jax.ShapeDtypeStruct(a.shape, a.dtype)


def _sem2(n, m):
    return [pltpu.SemaphoreType.DMA((n * m,)), pltpu.SemaphoreType.DMA((n * m,))]


ALL = (0, 1, 1)


def _piece(ref, k, half, part):
    hr = ref.shape[1] // 2
    r0, r1 = hr * part[0] // part[2], hr * part[1] // part[2]
    return ref.at[k, pl.ds(half * hr + r0, r1 - r0), :]


def _gather(fulls, ici=(), d2d=()):
    n = len(fulls)
    ici, d2d = list(ici), list(d2d)
    pieces = [("ici", i, part) for i, part in ici] + [("d2d", i, part) for i, part in d2d]

    def copies(outs, sems):
        x, y, c, chips = _place()
        me = 2 * x + y
        sib = (x, y, 1 - c)
        send, recv = [], []
        for q, (kind, i, part) in enumerate(pieces):
            for j, chip in enumerate(chips):
                k, s = 2 * chip[0] + chip[1], 3 * q + j
                if kind == "ici":
                    mine, theirs, dev = _piece(outs[i], me, c, part), _piece(outs[i], k, c, part), (*chip, c)
                else:
                    mine, theirs, dev = _piece(outs[i], k, c, part), _piece(outs[i], k, 1 - c, part), sib
                send.append(_rcopy(mine, mine, sems[0].at[s], sems[1].at[s], dev))
                recv.append(_rcopy(theirs, theirs, sems[0].at[s], sems[1].at[s], dev))
        return send, recv

    def start(ins, outs, sems):
        for cp in copies(outs, sems)[0]:
            cp.start()

    def finish(ins, outs, sems):
        send, recv = copies(outs, sems)
        for cp in recv:
            cp.wait_recv()
        for cp in send:
            cp.wait_send()

    sems = [pltpu.SemaphoreType.DMA((3 * len(pieces),)), pltpu.SemaphoreType.DMA((3 * len(pieces),))]
    return _Stage(fulls, [_sds(f) for f in fulls], {i: i for i in range(n)}, sems, start, finish)


def _gather_whole(v):
    def copies(ins, outs, sems):
        x, y, c, chips = _place()
        me = 2 * x + y
        send = [_rcopy(ins[0], outs[0].at[me], sems[0].at[j], sems[1].at[j], (*chip, c))
                for j, chip in enumerate(chips)]
        recv = [_rcopy(ins[0], outs[0].at[2 * chip[0] + chip[1]], sems[0].at[j], sems[1].at[j], (*chip, c))
                for j, chip in enumerate(chips)]
        return send, recv

    def start(ins, outs, sems):
        for cp in copies(ins, outs, sems)[0]:
            cp.start()

    def finish(ins, outs, sems):
        send, recv = copies(ins, outs, sems)
        for cp in recv:
            cp.wait_recv()
        for cp in send:
            cp.wait_send()

    return _Stage([v], [jax.ShapeDtypeStruct((NCHIP,) + v.shape, v.dtype)], {},
                  [pltpu.SemaphoreType.DMA((3,)), pltpu.SemaphoreType.DMA((3,))], start, finish)


def _to_sibling(srcs):
    n = len(srcs)

    def copies(ins, outs, sems):
        x, y, c, _ = _place()
        sib = (x, y, 1 - c)
        return [_rcopy(ins[i].at[:, 1 - c] if srcs[i].ndim == 4 else ins[i], outs[i], sems[0].at[i], sems[1].at[i], sib)
                for i in range(n)]

    def start(ins, outs, sems):
        for cp in copies(ins, outs, sems):
            cp.start()

    def finish(ins, outs, sems):
        for cp in copies(ins, outs, sems):
            cp.wait()

    shapes = [jax.ShapeDtypeStruct((NCHIP,) + s.shape[2:] if s.ndim == 4 else s.shape, s.dtype) for s in srcs]
    return _Stage(srcs, shapes, {}, [pltpu.SemaphoreType.DMA((n,)), pltpu.SemaphoreType.DMA((n,))], start, finish)


def _to_chips(srcs, parts=None, lands=None):
    n = len(srcs)
    parts = [ALL] * n if parts is None else parts
    lands = [None] * n if lands is None else lands
    given = [i for i in range(n) if lands[i] is not None]

    def rows(ref, i):
        hr = srcs[i].shape[1]
        r0, r1 = hr * parts[i][0] // parts[i][2], hr * parts[i][1] // parts[i][2]
        return ref.at[pl.ds(r0, r1 - r0), :]

    def copies(ins, outs, sems):
        x, y, c, chips = _place()
        me = 2 * x + y
        return [_rcopy(rows(ins[i].at[2 * chip[0] + chip[1]] if srcs[i].shape[0] == NCHIP else ins[i].at[c], i),
                       rows(outs[i].at[me], i), sems[0].at[3 * i + j], sems[1].at[3 * i + j], (*chip, c))
                for i in range(n) for j, chip in enumerate(chips)]

    def start(ins, outs, sems):
        for cp in copies(ins, outs, sems):
            cp.start()

    def finish(ins, outs, sems):
        for cp in copies(ins, outs, sems):
            cp.wait()

    shapes = [jax.ShapeDtypeStruct((NCHIP,) + s.shape[1:], s.dtype) for s in srcs]
    alias = {n + q: i for q, i in enumerate(given)}
    return _Stage(list(srcs) + [lands[i] for i in given], shapes, alias, _sem2(n, 3), start, finish)


HBM_REF = pl.BlockSpec(memory_space=pltpu.HBM)
SEM_REF = pl.BlockSpec(memory_space=pltpu.SEMAPHORE)
DATAFLOW = pltpu.SideEffectType.DATAFLOW_SIDE_EFFECTING


def _after(x):
    return _Stage([x], [], {}, [], lambda *a: None, lambda *a: None)


class _Flight:
    def __init__(self, stage, sems, bufs):
        self.stage, self.sems, self.bufs = stage, list(sems), list(bufs)

    def landed(self):
        st, n = self.stage, len(self.stage.operands)
        fresh = [j for j in range(len(st.out_shape)) if j not in st.alias.values()]
        back = {v: k for k, v in st.alias.items()}
        return [self.bufs[back[j]] if j in back else self.bufs[n + fresh.index(j)] for j in range(len(st.out_shape))]


def _split_call(name, finish=(), start=(), after=None):
    bufs, stage_bufs = [], []

    def slot(a):
        for i, b in enumerate(bufs):
            if b is a:
                return i
        bufs.append(a)
        return len(bufs) - 1

    fin_slots = [[slot(b) for b in fl.bufs] for fl in finish]
    for st in start:
        fresh = [lax.empty(o.shape, o.dtype) for j, o in enumerate(st.out_shape) if j not in st.alias.values()]
        stage_bufs.append([slot(a) for a in list(st.operands) + fresh])
    old_sems = [s for fl in finish for s in fl.sems]
    new_sems = [s for st in start for s in st.sems]
    nb, no, nn = len(bufs), len(old_sems), len(new_sems)

    def refs_of(st, slots, buf_refs):
        n = len(st.operands)
        ins = [buf_refs[i] for i in slots[:n]]
        fresh = [j for j in range(len(st.out_shape)) if j not in st.alias.values()]
        back = {v: k for k, v in st.alias.items()}
        outs = [ins[back[j]] if j in back else buf_refs[slots[n + fresh.index(j)]] for j in range(len(st.out_shape))]
        return ins, outs

    def body(*refs):
        buf_refs, sem_in = refs[:nb], refs[nb:nb + no]
        sem_out = refs[nb + no + (after is not None):][:nn]
        token = refs[-1]
        pos = 0
        for fl, slots in zip(finish, fin_slots):
            ins, outs = refs_of(fl.stage, slots, buf_refs)
            fl.stage.finish(ins, outs, sem_in[pos:pos + len(fl.sems)])
            pos += len(fl.sems)
        pos = 0
        for st, slots in zip(start, stage_bufs):
            ins, outs = refs_of(st, slots, buf_refs)
            st.start(ins, outs, sem_out[pos:pos + len(st.sems)])
            pos += len(st.sems)
        token[...] = jnp.zeros_like(token)

    res = pl.pallas_call(
        body, name=name,
        out_shape=tuple(new_sems) + tuple(pltpu.HBM(b.shape, b.dtype) for b in bufs) +
                  (jax.ShapeDtypeStruct((8, LANE), F32),),
        in_specs=(HBM_REF,) * nb + (SEM_REF,) * no + ((pl.BlockSpec(memory_space=pl.ANY),) if after is not None else ()),
        out_specs=(SEM_REF,) * nn + (HBM_REF,) * nb + (pl.BlockSpec(memory_space=pltpu.VMEM),),
        input_output_aliases={i: nn + i for i in range(nb)},
        compiler_params=pltpu.CompilerParams(has_side_effects=DATAFLOW),
    )(*_hbm(*bufs), *old_sems, *([after] if after is not None else []))
    sems, thru, token = res[:nn], res[nn:nn + nb], res[-1]
    for fl, slots in zip(finish, fin_slots):
        fl.bufs = [thru[i] for i in slots]
    flights, pos = [], 0
    for st, slots in zip(start, stage_bufs):
        flights.append(_Flight(st, sems[pos:pos + len(st.sems)], [thru[i] for i in slots]))
        pos += len(st.sems)
    return flights, token


def _share(pairs):
    n = len(pairs)

    def start(ins, outs, sems):
        x, y, c, _ = _place()
        for i in range(n):
            _rcopy(outs[i].at[c], outs[i].at[c], sems[0].at[i], sems[1].at[i], (x, y, 1 - c)).start()

    def finish(ins, outs, sems):
        x, y, c, _ = _place()
        for i in range(n):
            _rcopy(outs[i].at[c], outs[i].at[c], sems[0].at[i], sems[1].at[i], (x, y, 1 - c)).wait_send()
            _rcopy(outs[i].at[1 - c], outs[i].at[1 - c], sems[0].at[i], sems[1].at[i], (x, y, 1 - c)).wait_recv()

    return _Stage(pairs, [_sds(p) for p in pairs], {i: i for i in range(n)},
                  [pltpu.SemaphoreType.DMA((n,)), pltpu.SemaphoreType.DMA((n,))], start, finish)


def _row_block(rows, cols, itemsize=4, target=2 * MIB):
    br = rows
    while br * cols * itemsize > target and br % 32 == 0:
        br //= 2
    return br


def _cast_place(w, chip_idx, name):
    rows, cols = w.shape
    br = _row_block(rows, cols)

    def body(k_ref, w_ref, o_ref):
        o_ref[0] = w_ref[...].astype(BF)

    return _call(
        body, name=name, grid=(rows // br,), prefetch=chip_idx,
        in_specs=[pl.BlockSpec((br, cols), lambda r, k: (r, 0))],
        out_specs=[pl.BlockSpec((1, br, cols), lambda r, k: (k[0], r, 0))],
        out_shape=[jax.ShapeDtypeStruct((NCHIP, rows, cols), BF)], vmem=32, args=[w])[0][0]


def _cast_place_multi(ws, chip_idx, stages=()):
    br = 128
    nblk = [a.shape[0] // br for a in ws]
    starts = [sum(nblk[:i]) for i in range(len(ws))]

    def body(k_ref, *refs):
        r = pl.program_id(0)
        for i in range(len(ws)):
            @pl.when(jnp.logical_and(r >= starts[i], r < starts[i] + nblk[i]))
            def _(i=i):
                refs[len(ws) + i][0] = refs[i][...].astype(BF)

    def at(i):
        return functools.partial(lambda r, s, nb: jnp.clip(r - s, 0, nb - 1), s=starts[i], nb=nblk[i])

    outs, landed = _call(
        body, name="cast_rest", grid=(sum(nblk),), prefetch=chip_idx,
        in_specs=[pl.BlockSpec((br, a.shape[1]), functools.partial(lambda r, k, f: (f(r), 0), f=at(i)))
                  for i, a in enumerate(ws)],
        out_specs=[pl.BlockSpec((1, br, a.shape[1]), functools.partial(lambda r, k, f: (k[0], f(r), 0), f=at(i)))
                   for i, a in enumerate(ws)],
        out_shape=[jax.ShapeDtypeStruct((NCHIP,) + a.shape, BF) for a in ws], vmem=32, args=list(ws), stages=stages)
    return outs, landed


def _add_sibling(g, land, cidx, name, stages=()):
    _, _, hr, cols = g.shape
    br = _row_block(hr, cols)

    def body(c_ref, g_ref, l_ref, o_ref):
        o_ref[...] = (g_ref[0, 0].astype(F32) + l_ref[0].astype(F32)).astype(BF)[None]

    outs, st = _call(
        body, name=name, grid=(NCHIP, hr // br), prefetch=cidx,
        in_specs=[pl.BlockSpec((1, 1, br, cols), lambda k, r, c: (k, c[0], r, 0)),
                  pl.BlockSpec((1, br, cols), lambda k, r, c: (k, r, 0))],
        out_specs=[pl.BlockSpec((1, br, cols), lambda k, r, c: (k, r, 0))],
        out_shape=[jax.ShapeDtypeStruct((NCHIP, hr, cols), BF)], vmem=32, args=[g, land], stages=stages)
    return outs[0], st


def _add_sibling_multi(gs, lands, cidx, name):
    n = len(gs)
    brs = [_row_block(g.shape[2], g.shape[3]) for g in gs]
    nrb = [g.shape[2] // b for g, b in zip(gs, brs)]
    nblk = [NCHIP * q for q in nrb]
    starts = [sum(nblk[:i]) for i in range(n)]

    def body(c_ref, *refs):
        r = pl.program_id(0)
        for i in range(n):
            g_ref, l_ref, o_ref = refs[2 * i], refs[2 * i + 1], refs[2 * n + i]

            @pl.when(jnp.logical_and(r >= starts[i], r < starts[i] + nblk[i]))
            def _():
                o_ref[...] = (g_ref[0, 0].astype(F32) + l_ref[0].astype(F32)).astype(BF)[None]

    def at(i, r):
        q = jnp.clip(r - starts[i], 0, nblk[i] - 1)
        return q // nrb[i], q % nrb[i]

    def g_spec(i):
        return pl.BlockSpec((1, 1, brs[i], gs[i].shape[3]),
                            functools.partial(lambda r, c, i: (at(i, r)[0], c[0], at(i, r)[1], 0), i=i))

    def l_spec(i):
        return pl.BlockSpec((1, brs[i], gs[i].shape[3]),
                            functools.partial(lambda r, c, i: (at(i, r)[0], at(i, r)[1], 0), i=i))

    return _call(
        body, name=name, grid=(sum(nblk),), prefetch=cidx,
        in_specs=[s for i in range(n) for s in (g_spec(i), l_spec(i))], out_specs=[l_spec(i) for i in range(n)],
        out_shape=[jax.ShapeDtypeStruct(l.shape, BF) for l in lands], vmem=32,
        args=[a for i in range(n) for a in (gs[i], lands[i])])[0]


def _add_pair(a, b, name):
    rows, cols = a.shape

    def body(a_ref, b_ref, o_ref):
        o_ref[...] = a_ref[...] + b_ref[...]

    spec = pl.BlockSpec((rows, cols), lambda r: (0, 0))
    return _call(body, name=name, grid=(1,), in_specs=[spec, spec], out_specs=[spec], out_shape=[_sds(a)],
                 vmem=32, args=[a, b])[0][0]


def _add_chips(own, land, idx, name, stages=None):
    _, hr, cols = land.shape
    br = _row_block(hr, cols)

    def body(s_ref, a_ref, b_ref, c_ref, d_ref, o_ref):
        o_ref[...] = (a_ref[...].astype(F32) + b_ref[...].astype(F32)) + (c_ref[...].astype(F32) +
                                                                           d_ref[...].astype(F32))

    spec = lambda q: pl.BlockSpec((1, br, cols), functools.partial(lambda r, s, q: (s[q], r, 0), q=q))
    outs, landed = _call(
        body, name=name, grid=(hr // br,), prefetch=idx,
        in_specs=[spec(0), spec(1), spec(2), spec(3)], out_specs=[spec(4)],
        out_shape=[jax.ShapeDtypeStruct((2, hr, cols), F32)], vmem=48, args=[own, land, land, land],
        stages=stages or ())
    return outs[0] if stages is None else (outs[0], landed)


def _add_chips_multi(owns, lands, idx, name, stages=()):
    n = len(owns)
    brs = [_row_block(l.shape[1], l.shape[2]) for l in lands]
    nblk = [l.shape[1] // b for l, b in zip(lands, brs)]
    starts = [sum(nblk[:i]) for i in range(n)]

    def body(s_ref, *refs):
        r = pl.program_id(0)
        for i in range(n):
            a_ref, b_ref, c_ref, d_ref = refs[4 * i:4 * i + 4]
            o_ref = refs[4 * n + i]

            @pl.when(jnp.logical_and(r >= starts[i], r < starts[i] + nblk[i]))
            def _():
                o_ref[...] = (a_ref[...].astype(F32) + b_ref[...].astype(F32)) + (c_ref[...].astype(F32) +
                                                                                   d_ref[...].astype(F32))

    def spec(i, q):
        return pl.BlockSpec((1, brs[i], lands[i].shape[2]), functools.partial(
            lambda r, s, q, st, nb: (s[q], jnp.clip(r - st, 0, nb - 1), 0), q=q, st=starts[i], nb=nblk[i]))

    outs, landed = _call(
        body, name=name, grid=(sum(nblk),), prefetch=idx,
        in_specs=[spec(i, q) for i in range(n) for q in range(4)], out_specs=[spec(i, 4) for i in range(n)],
        out_shape=[jax.ShapeDtypeStruct((2,) + l.shape[1:], F32) for l in lands], vmem=48,
        args=[a for i in range(n) for a in (owns[i], lands[i], lands[i], lands[i])], stages=stages)
    return outs, landed


def _adamw_math(w, g, m, v):
    mn = ADAM_B1 * m + (1.0 - ADAM_B1) * g
    vn = ADAM_B2 * v + (1.0 - ADAM_B2) * (g * g)
    m_hat = mn / (1.0 - ADAM_B1 ** ADAM_STEP)
    v_hat = vn / (1.0 - ADAM_B2 ** ADAM_STEP)
    return -ADAM_LR * (m_hat / (jnp.sqrt(v_hat) + ADAM_EPS) + ADAM_WD * w), mn, vn


def _adamw(w, g, m, v, name, stages=()):
    rows, cols = w.shape
    br = _row_block(rows, cols)

    def body(w_ref, g_ref, m_ref, v_ref, go_ref, d_ref, mo_ref, vo_ref):
        gv = g_ref[...]
        go_ref[...] = gv
        d_ref[...], mo_ref[...], vo_ref[...] = _adamw_math(w_ref[...], gv, m_ref[...], v_ref[...])

    spec = pl.BlockSpec((br, cols), lambda r: (r, 0))
    return _call(body, name=name, grid=(rows // br,), in_specs=[spec] * 4, out_specs=[spec] * 4,
                 out_shape=[_sds(w)] * 4, vmem=56, args=[w, g, m, v], stages=stages)


def _adamw_multi(names, w, g, m, v, stages=()):
    cols = w[names[0]].shape[1]
    br = 128
    nblk = [w[n].shape[0] // br for n in names]
    starts = [sum(nblk[:i]) for i in range(len(names))]

    def body(*refs):
        r = pl.program_id(0)
        for i in range(len(names)):
            w_ref, g_ref, m_ref, v_ref = refs[4 * i:4 * i + 4]
            go_ref, d_ref, mo_ref, vo_ref = refs[4 * len(names) + 4 * i:4 * len(names) + 4 * i + 4]

            @pl.when(jnp.logical_and(r >= starts[i], r < starts[i] + nblk[i]))
            def _():
                gv = g_ref[...]
                go_ref[...] = gv
                d_ref[...], mo_ref[...], vo_ref[...] = _adamw_math(w_ref[...], gv, m_ref[...], v_ref[...])

    def spec(i):
        return pl.BlockSpec((br, cols), functools.partial(
            lambda r, s, nb: (jnp.clip(r - s, 0, nb - 1), 0), s=starts[i], nb=nblk[i]))

    outs, landed = _call(
        body, name="adamw_" + "_".join(names), grid=(sum(nblk),),
        in_specs=[spec(i) for i in range(len(names)) for _ in range(4)],
        out_specs=[spec(i) for i in range(len(names)) for _ in range(4)],
        out_shape=[_sds(w[n]) for n in names for _ in range(4)], vmem=56,
        args=[a[n] for n in names for a in (w, g, m, v)], stages=stages)
    return {n: outs[4 * i:4 * i + 4] for i, n in enumerate(names)}, landed


def _to_everyone(v):
    deltas = [(a, b, e) for a in (0, 1) for b in (0, 1) for e in (0, 1)][1:]

    def copies(ins, outs, sems):
        x, y, c, _ = _place()
        me = 4 * x + 2 * y + c
        flip = lambda p, f: 1 - p if f else p
        return [_rcopy(ins[0], outs[0].at[me], sems[0].at[q], sems[1].at[q], (flip(x, a), flip(y, b), flip(c, e)))
                for q, (a, b, e) in enumerate(deltas)]

    def start(ins, outs, sems):
        for cp in copies(ins, outs, sems):
            cp.start()

    def finish(ins, outs, sems):
        for cp in copies(ins, outs, sems):
            cp.wait()

    n = len(deltas)
    return _Stage([v], [jax.ShapeDtypeStruct((2 * NCHIP,) + v.shape, v.dtype)], {},
                  [pltpu.SemaphoreType.DMA((n,)), pltpu.SemaphoreType.DMA((n,))], start, finish)


SMALL_AT = {"norm_mix_pre": (0, 1, D), "norm_mix_post": (1, 1, D), "norm_mlp_pre": (2, 1, D),
            "norm_mlp_post": (3, 1, D), "b_gate": (4, 2, D), "conv_b": (6, 1, D), "lru_b_a": (7, 1, D),
            "lru_b_x": (8, 1, D), "lru_lambda": (9, 1, D), "pool_scale": (10, 1, DP)}
SMALL_SEPARATE = ["conv_w", "lru_w_a", "lru_w_x", "pool_w"]


def _adamw_small(small_sum, first_all, sep_grads, w, m, v):
    packed, sep = list(SMALL_AT), list(SMALL_SEPARATE)
    names = packed + sep

    def body(*refs):
        s_ref, a_ref, refs = refs[0], refs[1], refs[2:]
        g_sep, refs = refs[:len(sep)], refs[len(sep):]
        nn = len(names)
        w_r, m_r, v_r, refs = refs[:nn], refs[nn:2 * nn], refs[2 * nn:3 * nn], refs[3 * nn:]
        g_out, refs = refs[:len(packed)], refs[len(packed):]
        d_o, m_o, v_o = refs[:nn], refs[nn:2 * nn], refs[2 * nn:3 * nn]
        for i, n in enumerate(names):
            if i == 0:
                g = a_ref[0:1, :]
                for q in range(1, 2 * NCHIP):
                    g = g + a_ref[q:q + 1, :]
                g_out[i][...] = g
            elif n in SMALL_AT:
                r0, nr, nc = SMALL_AT[n]
                g = jnp.concatenate([s_ref[r0 + q:r0 + q + 1, :nc] for q in range(nr)], axis=1)
                g_out[i][...] = g
            else:
                g = g_sep[i - len(packed)][...]
            d_o[i][...], m_o[i][...], v_o[i][...] = _adamw_math(w_r[i][...], g, m_r[i][...], v_r[i][...])

    ws = [w[n] for n in names]
    res = pl.pallas_call(
        body, name="adamw_small",
        out_shape=[_sds(w[n]) for n in packed] + [_sds(a) for a in ws] * 3,
        compiler_params=_cp(32),
    )(*_hbm(small_sum, first_all, *sep_grads, *ws, *[m[n] for n in names], *[v[n] for n in names]))
    nn, npk = len(names), len(packed)
    grad = dict(zip(packed, res[:npk]))
    delta = dict(zip(names, res[npk:npk + nn]))
    new_m = dict(zip(names, res[npk + nn:npk + 2 * nn]))
    new_v = dict(zip(names, res[npk + 2 * nn:]))
    return grad, delta, new_m, new_v


W_NAMES = ["norm_mix_pre", "norm_mix_post", "norm_mlp_pre", "norm_mlp_post", "w_in", "b_gate", "conv_w", "conv_b",
           "lru_w_a", "lru_b_a", "lru_w_x", "lru_b_x", "lru_lambda", "pool_w", "pool_scale", "w_lru_up",
           "w_pool_up", "w_o", "w_ff1", "w_ff2"]
BIG = ["w_in", "w_lru_up", "w_pool_up", "w_o", "w_ff1", "w_ff2"]


def _block_diag(w):
    hd = w.shape[-1]
    per = CB // hd
    w4 = w.reshape(NG, per, hd, hd)
    eye = jnp.eye(per, dtype=w.dtype)
    return jnp.einsum("gpij,pq->gpiqj", w4, eye).reshape(NG, CB, CB)


def _block_diag_extract(d, hd):
    per = CB // hd
    d5 = d.reshape(NG, per, hd, per, hd)
    return jnp.stack([d5[:, p, :, p, :] for p in range(per)], axis=1).reshape(NG * per, hd, hd)


def _halves(g):
    return g.reshape(NCHIP, 2, g.size // (g.shape[-1] * 2 * NCHIP), g.shape[-1])


def kernel(x, norm_mix_pre, norm_mix_post, norm_mlp_pre, norm_mlp_post, w_in, b_gate, conv_w, conv_b, lru_w_a, lru_b_a, lru_w_x, lru_b_x, lru_lambda, pool_w, pool_scale, w_lru_up, w_pool_up, w_o, w_ff1, w_ff2, loss_target, m_norm_mix_pre, m_norm_mix_post, m_norm_mlp_pre, m_norm_mlp_post, m_w_in, m_b_gate, m_conv_w, m_conv_b, m_lru_w_a, m_lru_b_a, m_lru_w_x, m_lru_b_x, m_lru_lambda, m_pool_w, m_pool_scale, m_w_lru_up, m_w_pool_up, m_w_o, m_w_ff1, m_w_ff2, v_norm_mix_pre, v_norm_mix_post, v_norm_mlp_pre, v_norm_mlp_post, v_w_in, v_b_gate, v_conv_w, v_conv_b, v_lru_w_a, v_lru_b_a, v_lru_w_x, v_lru_b_x, v_lru_lambda, v_pool_w, v_pool_scale, v_w_lru_up, v_w_pool_up, v_w_o, v_w_ff1, v_w_ff2):
    args = dict(locals())
    two_d = lambda a: a.reshape(-1, a.shape[-1])
    w = {n: two_d(args[n]) for n in W_NAMES}
    mom = {n: two_d(args["m_" + n]) for n in W_NAMES}
    var = {n: two_d(args["v_" + n]) for n in W_NAMES}
    i32 = lambda val: jnp.asarray(val, jnp.int32)
    chip = i32(2 * lax.axis_index("x") + lax.axis_index("y"))
    core = i32(lax.axis_index("c"))
    cidx = core.reshape(1)
    zero = i32(0)
    hd = lru_w_a.shape[-1]
    xs, target = x[0], loss_target[0]
    g1, g2, g3, g4 = norm_mix_pre, norm_mix_post, norm_mlp_pre, norm_mlp_post

    mix = ["w_lru_up", "w_pool_up", "w_o"]
    full = {"w_in": _cast_place(w["w_in"], chip.reshape(1), "cast_w_in")}
    (fl_in, fl_conv), first = _split_call("gather_start_first", start=[
        _gather([full["w_in"]], ici=[(0, ALL)]), _gather_whole(w["conv_w"])])
    casts, _ = _cast_place_multi([w[n] for n in BIG[1:]], chip.reshape(1), stages=[_after(first)])
    full.update(zip(BIG[1:], casts))
    (fl_mix, fl_ff1, fl_ff2), started = _split_call("gather_start_rest", start=[
        _gather([full[n] for n in mix], ici=[(0, ALL), (1, ALL), (2, ALL)]),
        _gather([full["w_ff1"]], ici=[(0, ALL)]), _gather([full["w_ff2"]], ici=[(0, ALL)])])
    wa = _block_diag(lru_w_a[0]).astype(BF)
    wx = _block_diag(lru_w_x[0]).astype(BF)
    pw = pool_w[0].astype(BF)

    def to_sibling(name, flight, after=None):
        (fl,), passed = _split_call(name + "_pass", finish=[flight], after=after,
                                    start=[_gather(flight.landed(), d2d=[(i, ALL) for i in range(len(flight.bufs))])])
        passed_on.append(passed)
        return fl

    passed_on = []

    def arrived(name, flight, after=None):
        _split_call(name + "_done", finish=[flight], after=after)
        return flight.landed()

    idx_big = jnp.stack([chip, (chip + 1) % NCHIP, (chip + 2) % NCHIP, (chip + 3) % NCHIP, core])
    proj, h1 = _fwd_inproj_own(xs, g1, fl_in.bufs[0], idx_big, stages=[_after(started)])
    fl_in = to_sibling("gather_w_in", fl_in, after=h1)
    _split_call("gather_w_in_done", finish=[fl_in, fl_conv])
    (w_in_f,), (conv_all,) = fl_in.landed(), fl_conv.landed()
    full["w_in"] = w_in_f
    conv_all = lax.dynamic_update_slice(conv_all, w["conv_w"][None], (chip, zero, zero))
    conv_full = jnp.transpose(conv_all, (1, 0, 2)).reshape(4, DR)
    proj = _fwd_inproj_rest(h1, w_in_f, proj, idx_big)
    fl_mix = to_sibling("gather_mix", fl_mix, after=proj)
    (ylru, hs), _ = _fwd_lru(proj, conv_full, conv_b, wa, lru_b_a, wx, lru_b_x, lru_lambda,
                             stages=[_after(passed_on[-1])])
    got = arrived("gather_mix", fl_mix, after=ylru)
    fl_ff1 = to_sibling("gather_ff1", fl_ff1, after=ylru)
    w_lru_up_f, w_pool_up_f, w_o_f = got[0].reshape(DR, D), got[1], got[2].reshape(D, D)
    ypool = _fwd_pool(proj, pw, pool_scale)
    (x2, h2, m, mrg, bra, brb), _ = _fwd_merge(xs, ylru, ypool, proj, b_gate, g2, g3, w_lru_up_f, w_pool_up_f, w_o_f,
                                               stages=[_after(passed_on[-1])])
    fl_ff2 = to_sibling("gather_ff2", fl_ff2, after=h2)
    _split_call("gather_ff_done", finish=[fl_ff1, fl_ff2])
    (ff1,), (ff2,) = fl_ff1.landed(), fl_ff2.landed()
    ff2 = ff2.reshape(DF, D)
    a1, f = _fwd_mlp(h2, ff1, ff2)
    lossp, dy, df, dg4 = _loss_head(f, x2, target, g4)

    dh2, df1 = _bwd_mlp_x(df, a1, ff1, ff2)
    dw_ff1, dw_ff2 = _bwd_mlp_w(df, h2, a1, df1)
    g_ff = [_halves(dw_ff1), _halves(dw_ff2)]
    (dxres, dgates, dylru, dypool, dm, dbra, dbrb, dg2, dg3, dbg), (l_ff,) = _bwd_merge(
        dh2, dy, x2, m, bra, brb, proj, b_gate, g2, g3, w_lru_up_f, w_pool_up_f, w_o_f, stages=[_to_sibling(g_ff)])
    p_ff = _add_sibling_multi(g_ff, l_ff, cidx, "add_sibling_ff")
    (fl_ff,), sent_ff = _split_call("reduce_ff_start", start=[_to_chips(p_ff)])
    (dw_o, dw_lru_up, dw_pool_up), _ = _dw_merge(mrg, dm, ylru, dbra, ypool, dbrb, stages=[_after(sent_ff)])
    g_mix = [_halves(dw_lru_up), _halves(dw_pool_up), _halves(dw_o)]
    (dxp, dgl, dcw, dcb, dwa, dba, dwx, dbx, dlam), (l_mix,) = _bwd_lru(
        proj, hs, dylru, conv_full, conv_b, wa, lru_b_a, wx, lru_b_x, lru_lambda, stages=[_to_sibling(g_mix)])
    p_mix = _add_sibling_multi(g_mix, l_mix, cidx, "add_sibling_mix")
    dxpool, dpw, dsc = _bwd_pool(proj, dypool, pw, pool_scale)
    dproj = [dxp, dgl, dxpool, dgates]
    small = jnp.concatenate([
        jnp.zeros((1, D), F32), dg2, dg3, dg4, dbg.reshape(2, D), dcb, dba, dbx, dlam,
        jnp.pad(dsc, ((0, 0), (0, D - DP))), jnp.pad(lossp, ((0, 0), (0, D - 1))), dcw,
        _block_diag_extract(dwa, hd).reshape(-1, D), _block_diag_extract(dwx, hd).reshape(-1, D),
        dpw.reshape(-1, D)], axis=0)
    (fl_mixr, fl_smalls), sent_mix = _split_call("reduce_mix_start", start=[_to_chips(p_mix), _to_sibling([small])])
    dw_in = _bwd_inproj_w(h1, dproj, sent_mix)
    _split_call("reduce_small_sibling_done", finish=[fl_smalls], after=dw_in)
    small, l_small = fl_smalls.bufs
    small2 = _add_pair(small, l_small, "add_sibling_small").reshape(2, SMALL_ROWS // 2, D)
    g_in = _halves(dw_in)
    done = ["w_ff1", "w_ff2"] + mix
    (fl_gin, fl_small), sib_started = _split_call("reduce_in_sibling_start", finish=[fl_ff, fl_mixr],
                                                  start=[_to_sibling([g_in]), _to_chips([small2])])
    p_ff1, p_ff2, c_ff1, c_ff2 = fl_ff.bufs
    p_mix, c_mix = fl_mixr.bufs[:3], fl_mixr.bufs[3:]
    pairs, _ = _add_chips_multi([p_ff1, p_ff2] + p_mix, [c_ff1, c_ff2] + c_mix, idx_big, "add_chips_done",
                                stages=[_after(sib_started)])
    _split_call("reduce_in_sibling_done", finish=[fl_gin], after=pairs[-1])
    g_in, l_in = fl_gin.bufs
    p_in = _add_sibling(g_in, l_in, cidx, "add_sibling_w_in")[0]
    (fl_pin,), token = _split_call("reduce_last_start", start=[_to_chips([p_in])])
    _split_call("reduce_small_done", finish=[fl_small], after=token)
    small2, c_small = fl_small.bufs
    own_small = lax.dynamic_index_in_dim(small2, core, 0, keepdims=True)
    c_small = lax.dynamic_update_slice(c_small, own_small, (chip, zero, zero))
    pair_small = _add_chips(c_small, c_small, jnp.stack([zero, zero + 1, zero + 2, zero + 3, core]), "add_chips_small")
    (fl_share,), shared_start = _split_call("reduce_share_start", start=[_share(pairs + [pair_small])])
    grad_x, dg1 = _bwd_inproj_x(dproj, full["w_in"], xs, dxres, g1, stages=[_after(shared_start)])
    _split_call("reduce_share_done", finish=[fl_share, fl_pin], after=dg1)
    shared, (p_in, c_in) = fl_share.landed(), fl_pin.bufs
    pairs, pair_small = shared[:-1], shared[-1]

    grads, delta, new_m, new_v = {}, {}, {}, {}
    for n, p in zip(done, pairs):
        grads[n] = p.reshape(-1, p.shape[-1])

    def update(n, stages=()):
        (grads[n], delta[n], new_m[n], new_v[n]), landed = _adamw(w[n], grads[n], mom[n], var[n], "adamw_" + n,
                                                                  stages=stages)
        return landed

    pair_in = _add_chips(p_in, c_in, idx_big, "add_chips_w_in")
    (fl_last, fl_dg1), last_start = _split_call("reduce_last_share_start", start=[_share([pair_in]), _to_everyone(dg1)])
    updated, _ = _adamw_multi(["w_ff1", "w_ff2", "w_o", "w_lru_up"], w, grads, mom, var, stages=[_after(last_start)])
    for n, (go, d, mo, vo) in updated.items():
        grads[n], delta[n], new_m[n], new_v[n] = go, d, mo, vo
    _split_call("reduce_last_share_done", finish=[fl_last, fl_dg1], after=new_v["w_lru_up"])
    (pair_in,), (dg1, dg1_all) = fl_last.landed(), fl_dg1.bufs
    dg1_all = lax.dynamic_update_slice(dg1_all, dg1[None], (2 * chip + core, zero, zero)).reshape(2 * NCHIP, D)
    grads["w_in"] = pair_in.reshape(-1, pair_in.shape[-1])
    update("w_pool_up")
    update("w_in")
    small_sum = pair_small.reshape(SMALL_ROWS, D)
    loss = 0.5 * small_sum[LOSS_ROW, 0]
    ccols = DR // NCHIP
    sep = [lax.dynamic_slice(small_sum[12:16], (zero, chip * ccols), (4, ccols)),
           small_sum[16:80].reshape(-1, hd), small_sum[80:144].reshape(-1, hd), small_sum[144:208].reshape(-1, PG)]
    g_s, d_s, m_s, v_s = _adamw_small(small_sum, dg1_all, sep, w, mom, var)
    grads.update(g_s)
    grads.update(dict(zip(SMALL_SEPARATE, sep)))
    delta.update(d_s)
    new_m.update(m_s)
    new_v.update(v_s)

    out = lambda d: [d[n].reshape(args[n].shape) for n in W_NAMES]
    return (loss, grad_x[None], *out(grads), *out(delta), *out(new_m), *out(new_v))
```

```python
import functools
import math

import jax
import jax.numpy as jnp
from jax import lax
from jax.experimental import pallas as pl
from jax.experimental.pallas import tpu as pltpu

F32 = jnp.float32
BF = jnp.bfloat16

T = 2048
D = 1024
DR = 1024
DP = 512
DF = 4096
DIN = 4608
NCHIP = 4
CW_IN = DIN // NCHIP
LANE = 128
CB = 128
NG = DR // CB
PG = 128
POOL_WINDOWS = (2, 4, 8, 16)
NORM_EPS = 1e-6
LRU_C = 8.0
GELU_C = math.sqrt(2.0 / math.pi)
ADAM_LR = 0.001
ADAM_B1 = 0.9
ADAM_B2 = 0.999
ADAM_EPS = 1e-08
ADAM_WD = 0.01
ADAM_STEP = 10
MESH_ID = pl.DeviceIdType.MESH
ANY = pl.BlockSpec(memory_space=pl.ANY)
SMALL_ROWS = 208
LOSS_ROW = 11
MIB = 1 << 20


def _cp(vmem_mib=None):
    if vmem_mib is None:
        return pltpu.CompilerParams()
    return pltpu.CompilerParams(vmem_limit_bytes=vmem_mib * MIB)


def _hbm(*arrays):
    return [pltpu.with_memory_space_constraint(a, pltpu.HBM) for a in arrays]


def _hbm_out(shapes):
    return [pltpu.HBM(s.shape, s.dtype) for s in shapes]


class _Stage:
    def __init__(self, operands, out_shape, alias, sems, start, finish):
        self.operands, self.out_shape, self.alias, self.sems = list(operands), list(out_shape), dict(alias), list(sems)
        self.start, self.finish = start, finish


def _call(body, *, name, grid, in_specs, out_specs, out_shape, args, vmem=None, stages=(), prefetch=None,
          scratch=()):
    nin, nout = len(in_specs), len(out_specs)
    npre = 0 if prefetch is None else 1
    st_args, st_shapes, st_sems, aliases = [], [], list(scratch), {}
    for st in stages:
        for k, v in st.alias.items():
            aliases[npre + nin + len(st_args) + k] = nout + len(st_shapes) + v
        st_args += st.operands
        st_shapes += st.out_shape
        st_sems += st.sems

    def wrapped(*refs):
        pre, refs = refs[:npre], refs[npre:]
        ins, pos = refs[:nin], nin
        st_ins = []
        for st in stages:
            st_ins.append(refs[pos:pos + len(st.operands)])
            pos += len(st.operands)
        outs, pos = refs[pos:pos + nout], pos + nout
        st_outs = []
        for st in stages:
            st_outs.append(refs[pos:pos + len(st.out_shape)])
            pos += len(st.out_shape)
        work, pos = refs[pos:pos + len(scratch)], pos + len(scratch)
        sems = []
        for st in stages:
            sems.append(refs[pos:pos + len(st.sems)])
            pos += len(st.sems)
        if stages:
            first = functools.reduce(jnp.logical_and, [pl.program_id(a) == 0 for a in range(len(grid))])

            @pl.when(first)
            def _():
                for st, a, b, s in zip(stages, st_ins, st_outs, sems):
                    st.start(a, b, s)

        body(*pre, *ins, *outs, *work)
        if stages:
            last = functools.reduce(jnp.logical_and, [pl.program_id(a) == g - 1 for a, g in enumerate(grid)])

            @pl.when(last)
            def _():
                for st, a, b, s in zip(stages, st_ins, st_outs, sems):
                    st.finish(a, b, s)

    all_in = list(in_specs) + [ANY] * len(st_args)
    all_out = list(out_specs) + [ANY] * len(st_shapes)
    kw = dict(has_side_effects=True) if stages else {}
    if vmem is not None:
        kw["vmem_limit_bytes"] = vmem * MIB
    if prefetch is None:
        gkw = dict(grid=grid, in_specs=all_in, out_specs=all_out, scratch_shapes=st_sems)
    else:
        gkw = dict(grid_spec=pltpu.PrefetchScalarGridSpec(
            num_scalar_prefetch=1, grid=grid, in_specs=all_in, out_specs=all_out, scratch_shapes=st_sems))
    res = pl.pallas_call(
        wrapped, name=name, out_shape=_hbm_out(list(out_shape) + st_shapes), input_output_aliases=aliases,
        compiler_params=pltpu.CompilerParams(**kw), **gkw,
    )(*([prefetch] if npre else []), *_hbm(*args, *st_args))
    outs, rest, st_res = list(res[:nout]), list(res[nout:]), []
    for st in stages:
        st_res.append(rest[:len(st.out_shape)])
        rest = rest[len(st.out_shape):]
    return outs, st_res


def _mm(a, b):
    return jnp.dot(a.astype(BF), b.astype(BF), preferred_element_type=F32)


def _mm_nt(a, b):
    return lax.dot_general(a.astype(BF), b.astype(BF), (((1,), (1,)), ((), ())),
                           preferred_element_type=F32)


def _mm_tn(a, b):
    return lax.dot_general(a.astype(BF), b.astype(BF), (((0,), (0,)), ((), ())),
                           preferred_element_type=F32)


def _rows(v):
    return lax.broadcasted_iota(jnp.int32, v.shape, 0)


def _sd(v, s, fill=0.0):
    return jnp.where(_rows(v) >= s, pltpu.roll(v, s, axis=0), fill)


def _su(v, s, fill=0.0):
    n = v.shape[0]
    return jnp.where(_rows(v) < n - s, pltpu.roll(v, n - s, axis=0), fill)


def _sigmoid(z):
    return 1.0 / (1.0 + jnp.exp(-z))


def _softplus(z):
    e = jnp.exp(-jnp.abs(z))
    u = 1.0 + e
    d = u - 1.0
    log1p = jnp.where(d == 0.0, e, jnp.log(u) * (e / jnp.where(d == 0.0, 1.0, d)))
    return jnp.maximum(z, 0.0) + log1p


def _mean(v):
    return jnp.mean(v, axis=-1, keepdims=True)


def _colsum(v):
    return jnp.sum(v, axis=0, keepdims=True)


def _acc(ref, val, first):
    @pl.when(first)
    def _():
        ref[...] = val

    @pl.when(jnp.logical_not(first))
    def _():
        ref[...] += val


def _conv(xp, cw, cb):
    x1, x2, x3 = _sd(xp, 1), _sd(xp, 2), _sd(xp, 3)
    xc = cb + cw[0:1] * x3 + cw[1:2] * x2 + cw[2:3] * x1 + cw[3:4] * xp
    return xc, x1, x2, x3


def _lru_gates(xc, wa, ba, wx, bx, lam):
    xcb = xc.astype(BF)
    r = _sigmoid(_mm(xcb, wa) + ba)
    ii = _sigmoid(_mm(xcb, wx) + bx)
    sp = _softplus(-lam)
    la = (-LRU_C) * r * sp
    a = jnp.exp(la)
    mult = jnp.sqrt(-jnp.tanh(la) * (a * a + 1.0))
    return xcb, r, ii, sp, a, mult


def _gelu_parts(g):
    th = jnp.tanh(GELU_C * (g + 0.044715 * (g * g * g)))
    gel = 0.5 * g * (1.0 + th)
    dgel = 0.5 * (1.0 + th) + 0.5 * g * (1.0 - th * th) * (GELU_C * (1.0 + 3.0 * 0.044715 * (g * g)))
    return gel, dgel


def _tile_scan(a, b, a_s, b_s, out_ref, reverse):
    n, lanes = a.shape
    nt = n // 8
    a, b = a.reshape(nt, 8, lanes), b.reshape(nt, 8, lanes)
    sub = lax.broadcasted_iota(jnp.int32, a.shape, 1)
    s = 1
    while s < 8:
        keep = sub < 8 - s if reverse else sub >= s
        amount = 8 - s if reverse else s
        b = b + a * jnp.where(keep, pltpu.roll(b, amount, axis=1), 0.0)
        a = a * jnp.where(keep, pltpu.roll(a, amount, axis=1), 1.0)
        s *= 2
    a_s[...] = a.reshape(n, lanes)
    b_s[...] = b.reshape(n, lanes)
    edge = pl.ds(0 if reverse else 7, nt, stride=8)
    ta, tb = a_s[edge, :], b_s[edge, :]
    shift = _su if reverse else _sd
    s = 1
    while s < nt:
        tb = tb + ta * shift(tb, s, 0.0)
        if 2 * s < nt:
            ta = ta * shift(ta, s, 1.0)
        s *= 2
    enters = shift(tb, 1, 0.0)
    for o in range(8):
        rows = pl.ds(o, nt, stride=8)
        out_ref[rows, :] = b_s[rows, :] + a_s[rows, :] * enters


def _pool_window(x, steps, shift):
    s, sh = x, 1
    for _ in range(steps):
        s = s + shift(s, sh)
        sh *= 2
    return s


def _fwd_inproj_own(x, g1, w_in, slots, stages=()):
    tm = 1024

    def body(s_ref, x_ref, g_ref, w_ref, proj_ref, h_ref):
        xv = x_ref[...]
        r = lax.rsqrt(_mean(xv * xv) + NORM_EPS)
        h = ((xv * r) * g_ref[...]).astype(BF)
        h_ref[...] = h
        proj_ref[...] = jnp.dot(h, w_ref[0], preferred_element_type=F32)

    return _call(
        body, name="fwd_inproj_own", grid=(T // tm,), prefetch=slots,
        in_specs=[pl.BlockSpec((tm, D), lambda i, s: (i, 0)),
                  pl.BlockSpec((1, D), lambda i, s: (0, 0)),
                  pl.BlockSpec((1, D, CW_IN), lambda i, s: (s[0], 0, 0))],
        out_specs=[pl.BlockSpec((tm, CW_IN), lambda i, s: (i, s[0])),
                   pl.BlockSpec((tm, D), lambda i, s: (i, 0))],
        out_shape=[jax.ShapeDtypeStruct((T, DIN), F32), jax.ShapeDtypeStruct((T, D), BF)],
        vmem=40, args=[x, g1, w_in], stages=stages)[0]


def _fwd_inproj_rest(h1, w_in, proj, slots):
    tm = 1024

    def body(s_ref, h_ref, w_ref, p_in, proj_ref):
        proj_ref[...] = jnp.dot(h_ref[...], w_ref[0], preferred_element_type=F32)

    res = pl.pallas_call(
        body, name="fwd_inproj_rest",
        grid_spec=pltpu.PrefetchScalarGridSpec(
            num_scalar_prefetch=1, grid=(NCHIP - 1, T // tm),
            in_specs=[pl.BlockSpec((tm, D), lambda k, i, s: (i, 0)),
                      pl.BlockSpec((1, D, CW_IN), lambda k, i, s: (s[1 + k], 0, 0)), ANY],
            out_specs=pl.BlockSpec((tm, CW_IN), lambda k, i, s: (i, s[1 + k]))),
        out_shape=pltpu.HBM((T, DIN), F32), input_output_aliases={3: 0},
        compiler_params=_cp(40),
    )(slots, *_hbm(h1, w_in, proj))
    return res


def _vec_spec():
    return pl.BlockSpec((1, CB), lambda j: (0, j))


def _fwd_lru(proj, conv_w, conv_b, wa, ba, wx, bx, lam, stages=()):
    def body(xp_ref, g_ref, cw_ref, cb_ref, wa_ref, ba_ref, wx_ref, bx_ref, lam_ref, y_ref, h_ref, a_s, b_s):
        xc, _, _, _ = _conv(xp_ref[...], cw_ref[...], cb_ref[...])
        _, _, ii, _, a, mult = _lru_gates(xc, wa_ref[0], ba_ref[...], wx_ref[0], bx_ref[...], lam_ref[...])
        _tile_scan(a, mult * (ii * xc), a_s, b_s, h_ref, reverse=False)
        gel, _ = _gelu_parts(g_ref[...])
        y_ref[...] = (h_ref[...] * gel).astype(BF)

    return _call(
        body, name="fwd_lru", grid=(NG,),
        in_specs=[pl.BlockSpec((T, CB), lambda j: (0, j)),
                  pl.BlockSpec((T, CB), lambda j: (0, NG + j)),
                  pl.BlockSpec((4, CB), lambda j: (0, j)),
                  _vec_spec(),
                  pl.BlockSpec((1, CB, CB), lambda j: (j, 0, 0)), _vec_spec(),
                  pl.BlockSpec((1, CB, CB), lambda j: (j, 0, 0)), _vec_spec(),
                  _vec_spec()],
        out_specs=[pl.BlockSpec((T, CB), lambda j: (0, j)), pl.BlockSpec((T, CB), lambda j: (0, j))],
        out_shape=[jax.ShapeDtypeStruct((T, DR), BF), jax.ShapeDtypeStruct((T, DR), F32)],
        vmem=48, args=[proj, proj, conv_w, conv_b, wa, ba, wx, bx, lam], stages=stages,
        scratch=[pltpu.VMEM((T, CB), F32)] * 2)


def _pool_cnt(w):
    t = lax.broadcasted_iota(jnp.int32, (T, 1), 0)
    return jnp.minimum(t + 1, w).astype(F32)


def _fwd_pool(proj, pool_w, pool_scale):
    def body(xp_ref, pw_ref, sc_ref, y_ref):
        for g, w in enumerate(POOL_WINDOWS):
            cols = slice(g * PG, (g + 1) * PG)
            x = xp_ref[:, cols]
            p = _pool_window(x, g + 1, _sd) / _pool_cnt(w) - x
            y_ref[:, cols] = (_mm(p, pw_ref[g]) * sc_ref[:, cols]).astype(BF)

    return pl.pallas_call(
        body, name="fwd_pool", grid=(1,),
        in_specs=[pl.BlockSpec((T, DP), lambda i: (0, 2 * DR // DP)),
                  pl.BlockSpec((4, PG, PG), lambda i: (0, 0, 0)),
                  pl.BlockSpec((1, DP), lambda i: (0, 0))],
        out_specs=pl.BlockSpec((T, DP), lambda i: (0, 0)),
        out_shape=pltpu.HBM((T, DP), BF),
        compiler_params=_cp(48),
    )(*_hbm(proj, pool_w, pool_scale))


GATE_BLK = 512
GATE_BLK0 = (2 * DR + DP) // GATE_BLK


def _gate_specs(tm):
    return [pl.BlockSpec((tm, GATE_BLK), functools.partial(lambda i, q: (i, GATE_BLK0 + q), q=q))
            for q in range(4)]


def _fwd_merge(x, ylru, ypool, proj, b_gate, g2, g3, w_lru_up, w_pool_up, w_o, stages=()):
    tm = 512

    def body(x_ref, yl_ref, yp_ref, p0, p1, p2, p3, bg_ref, g2_ref, g3_ref, wl_ref, wp_ref, wo_ref,
             x2_ref, h2_ref, m_ref, mrg_ref, bra_ref, brb_ref):
        bra = jnp.dot(yl_ref[...], wl_ref[...], preferred_element_type=F32)
        yp = yp_ref[...]
        brb = jnp.concatenate([jnp.dot(yp, wp_ref[k], preferred_element_type=F32) for k in range(NCHIP)], axis=1)
        bg = bg_ref[...]
        ga = _sigmoid(jnp.concatenate([p0[...], p1[...]], axis=1) + bg[:, :D])
        gb = _sigmoid(jnp.concatenate([p2[...], p3[...]], axis=1) + bg[:, D:])
        mrg = (ga * bra + gb * brb).astype(BF)
        m = jnp.dot(mrg, wo_ref[...], preferred_element_type=F32)
        r2 = lax.rsqrt(_mean(m * m) + NORM_EPS)
        x2 = x_ref[...] + (m * r2) * g2_ref[...]
        r3 = lax.rsqrt(_mean(x2 * x2) + NORM_EPS)
        x2_ref[...] = x2
        h2_ref[...] = ((x2 * r3) * g3_ref[...]).astype(BF)
        m_ref[...] = m
        mrg_ref[...] = mrg
        bra_ref[...] = bra.astype(BF)
        brb_ref[...] = brb.astype(BF)

    row = lambda w: pl.BlockSpec((tm, w), lambda i: (i, 0))
    full2 = lambda a, b: pl.BlockSpec((a, b), lambda i: (0, 0))
    return _call(
        body, name="fwd_merge", grid=(T // tm,),
        in_specs=[row(D), row(DR), row(DP)] + _gate_specs(tm) +
                 [full2(1, 2 * D), full2(1, D), full2(1, D), full2(DR, D),
                  pl.BlockSpec((NCHIP, DP, D // NCHIP), lambda i: (0, 0, 0)), full2(D, D)],
        out_specs=[row(D)] * 6,
        out_shape=[jax.ShapeDtypeStruct((T, D), F32), jax.ShapeDtypeStruct((T, D), BF),
                   jax.ShapeDtypeStruct((T, D), F32), jax.ShapeDtypeStruct((T, D), BF),
                   jax.ShapeDtypeStruct((T, D), BF), jax.ShapeDtypeStruct((T, D), BF)],
        vmem=48, args=[x, ylru, ypool, proj, proj, proj, proj, b_gate, g2, g3, w_lru_up, w_pool_up, w_o],
        stages=stages)


def _fwd_mlp(h2, w_ff1, w_ff2):
    tm = 512
    fk = DF // NCHIP

    def body(h_ref, w1_ref, w2_ref, a1_ref, f_ref):
        h = h_ref[...]
        f = None
        for k in range(NCHIP):
            a1 = jnp.maximum(jnp.dot(h, w1_ref[k], preferred_element_type=F32), 0.0)
            a1_ref[:, k * fk:(k + 1) * fk] = a1.astype(BF)
            part = jnp.dot((a1 * a1).astype(BF), w2_ref[k * fk:(k + 1) * fk, :], preferred_element_type=F32)
            f = part if f is None else f + part
        f_ref[...] = f

    return pl.pallas_call(
        body, name="fwd_mlp", grid=(T // tm,),
        in_specs=[pl.BlockSpec((tm, D), lambda i: (i, 0)),
                  pl.BlockSpec((NCHIP, D, fk), lambda i: (0, 0, 0)),
                  pl.BlockSpec((DF, D), lambda i: (0, 0))],
        out_specs=[pl.BlockSpec((tm, DF), lambda i: (i, 0)), pl.BlockSpec((tm, D), lambda i: (i, 0))],
        out_shape=_hbm_out([jax.ShapeDtypeStruct((T, DF), BF), jax.ShapeDtypeStruct((T, D), F32)]),
        compiler_params=_cp(56),
    )(*_hbm(h2, w_ff1, w_ff2))


def _loss_head(f, x2, target, g4):
    tm = 512

    def body(f_ref, x2_ref, t_ref, g_ref, loss_ref, dy_ref, df_ref, dg_ref):
        first = pl.program_id(0) == 0
        f = f_ref[...]
        g4v = g_ref[...]
        r4 = lax.rsqrt(_mean(f * f) + NORM_EPS)
        fn = f * r4
        e = (x2_ref[...] + fn * g4v) - t_ref[...]
        _acc(loss_ref, jnp.sum(_mean(e * e), axis=0, keepdims=True), first)
        dy = e * (1.0 / D)
        dy_ref[...] = dy
        _acc(dg_ref, _colsum(dy * fn), first)
        dfn = dy * g4v
        df_ref[...] = (r4 * (dfn - fn * _mean(dfn * fn))).astype(BF)

    row = pl.BlockSpec((tm, D), lambda i: (i, 0))
    return pl.pallas_call(
        body, name="loss_head", grid=(T // tm,),
        in_specs=[row, row, row, pl.BlockSpec((1, D), lambda i: (0, 0))],
        out_specs=[pl.BlockSpec((1, 1), lambda i: (0, 0)), row, row, pl.BlockSpec((1, D), lambda i: (0, 0))],
        out_shape=_hbm_out([jax.ShapeDtypeStruct((1, 1), F32), jax.ShapeDtypeStruct((T, D), F32),
                            jax.ShapeDtypeStruct((T, D), BF), jax.ShapeDtypeStruct((1, D), F32)]),
        compiler_params=_cp(48),
    )(*_hbm(f, x2, target, g4))


def _bwd_mlp_x(df, a1, w_ff1, w_ff2):
    tm = 512
    fk = DF // NCHIP

    def body(df_ref, a1_ref, w1_ref, w2_ref, dh_ref, df1_ref):
        df = df_ref[...]
        dh = None
        for k in range(NCHIP):
            cols = slice(k * fk, (k + 1) * fk)
            dact = _mm_nt(df, w2_ref[cols, :])
            df1 = (dact * (2.0 * a1_ref[:, cols].astype(F32))).astype(BF)
            df1_ref[:, cols] = df1
            part = _mm_nt(df1, w1_ref[k])
            dh = part if dh is None else dh + part
        dh_ref[...] = dh

    return pl.pallas_call(
        body, name="bwd_mlp_x", grid=(T // tm,),
        in_specs=[pl.BlockSpec((tm, D), lambda i: (i, 0)),
                  pl.BlockSpec((tm, DF), lambda i: (i, 0)),
                  pl.BlockSpec((NCHIP, D, fk), lambda i: (0, 0, 0)),
                  pl.BlockSpec((DF, D), lambda i: (0, 0))],
        out_specs=[pl.BlockSpec((tm, D), lambda i: (i, 0)), pl.BlockSpec((tm, DF), lambda i: (i, 0))],
        out_shape=_hbm_out([jax.ShapeDtypeStruct((T, D), F32), jax.ShapeDtypeStruct((T, DF), BF)]),
        compiler_params=_cp(56),
    )(*_hbm(df, a1, w_ff1, w_ff2))


def _bwd_mlp_w(df, h2, a1, df1):
    fc = 512
    per = (DF // NCHIP) // fc

    def body(df_ref, h_ref, a1_ref, df1_ref, dw1_ref, dw2_ref):
        a1 = a1_ref[...].astype(F32)
        dw2_ref[...] = _mm_tn((a1 * a1).astype(BF), df_ref[...]).astype(BF)
        dw1_ref[0] = _mm_tn(h_ref[...], df1_ref[...]).astype(BF)

    return pl.pallas_call(
        body, name="bwd_mlp_w", grid=(DF // fc,),
        in_specs=[pl.BlockSpec((T, D), lambda j: (0, 0)),
                  pl.BlockSpec((T, D), lambda j: (0, 0)),
                  pl.BlockSpec((T, fc), lambda j: (0, j)),
                  pl.BlockSpec((T, fc), lambda j: (0, j))],
        out_specs=[pl.BlockSpec((1, D, fc), lambda j: (j // per, 0, j % per)),
                   pl.BlockSpec((fc, D), lambda j: (j, 0))],
        out_shape=_hbm_out([jax.ShapeDtypeStruct((NCHIP, D, DF // NCHIP), BF),
                            jax.ShapeDtypeStruct((DF, D), BF)]),
        compiler_params=_cp(56),
    )(*_hbm(df, h2, a1, df1))


def _bwd_merge(dh2, dy, x2, m, bra, brb, proj, b_gate, g2, g3, w_lru_up, w_pool_up, w_o, stages=()):
    tm = 256
    cpu = D // NCHIP

    def body(dh2_ref, dy_ref, x2_ref, m_ref, bra_ref, brb_ref, p0, p1, p2, p3, bg_ref,
             g2_ref, g3_ref, wl_ref, wp_ref, wo_ref,
             dx_ref, dgt_ref, dyl_ref, dyp_ref, dm_ref, dbra_ref, dbrb_ref, dg2_ref, dg3_ref, dbg_ref):
        first = pl.program_id(0) == 0
        x2 = x2_ref[...]
        r3 = lax.rsqrt(_mean(x2 * x2) + NORM_EPS)
        x2n = x2 * r3
        dh2 = dh2_ref[...]
        t3 = dh2 * g3_ref[...]
        dx2 = dy_ref[...] + r3 * (t3 - x2n * _mean(t3 * x2n))
        dx_ref[...] = dx2
        _acc(dg3_ref, _colsum(dh2 * x2n), first)
        m = m_ref[...]
        r2 = lax.rsqrt(_mean(m * m) + NORM_EPS)
        mn = m * r2
        _acc(dg2_ref, _colsum(dx2 * mn), first)
        dmn = dx2 * g2_ref[...]
        dm = (r2 * (dmn - mn * _mean(dmn * mn))).astype(BF)
        dm_ref[...] = dm
        dmrg = _mm_nt(dm, wo_ref[...])
        bg = bg_ref[...]
        ga = _sigmoid(jnp.concatenate([p0[...], p1[...]], axis=1) + bg[:, :D])
        gb = _sigmoid(jnp.concatenate([p2[...], p3[...]], axis=1) + bg[:, D:])
        dga = dmrg * bra_ref[...].astype(F32) * (ga * (1.0 - ga))
        dgb = dmrg * brb_ref[...].astype(F32) * (gb * (1.0 - gb))
        dgt_ref[:, :D] = dga.astype(BF)
        dgt_ref[:, D:] = dgb.astype(BF)
        _acc(dbg_ref, jnp.concatenate([_colsum(dga), _colsum(dgb)], axis=1), first)
        dbra = (dmrg * ga).astype(BF)
        dbrb = (dmrg * gb).astype(BF)
        dbra_ref[...] = dbra
        dbrb_ref[...] = dbrb
        dyl_ref[...] = _mm_nt(dbra, wl_ref[...])
        dyp = None
        for k in range(NCHIP):
            part = _mm_nt(dbrb[:, k * cpu:(k + 1) * cpu], wp_ref[k])
            dyp = part if dyp is None else dyp + part
        dyp_ref[...] = dyp

    row = lambda w: pl.BlockSpec((tm, w), lambda i: (i, 0))
    full2 = lambda a, b: pl.BlockSpec((a, b), lambda i: (0, 0))
    wp_spec = pl.BlockSpec((NCHIP, DP, cpu), lambda i: (0, 0, 0))
    return _call(
        body, name="bwd_merge", grid=(T // tm,),
        in_specs=[row(D)] * 6 + _gate_specs(tm) +
                 [full2(1, 2 * D), full2(1, D), full2(1, D), full2(DR, D), wp_spec, full2(D, D)],
        out_specs=[row(D), row(2 * D), row(DR), row(DP), row(D), row(D), row(D),
                   full2(1, D), full2(1, D), full2(1, 2 * D)],
        out_shape=[jax.ShapeDtypeStruct((T, D), F32), jax.ShapeDtypeStruct((T, 2 * D), BF),
                   jax.ShapeDtypeStruct((T, DR), F32), jax.ShapeDtypeStruct((T, DP), F32),
                   jax.ShapeDtypeStruct((T, D), BF), jax.ShapeDtypeStruct((T, D), BF),
                   jax.ShapeDtypeStruct((T, D), BF),
                   jax.ShapeDtypeStruct((1, D), F32), jax.ShapeDtypeStruct((1, D), F32),
                   jax.ShapeDtypeStruct((1, 2 * D), F32)],
        vmem=56, args=[dh2, dy, x2, m, bra, brb, proj, proj, proj, proj, b_gate, g2, g3, w_lru_up, w_pool_up, w_o],
        stages=stages)


def _dw_merge(mrg, dm, ylru, dbra, ypool, dbrb, stages=()):
    nb = NCHIP
    rb, pb, cpu = D // nb, DP // nb, D // NCHIP

    def body(mrg_ref, dm_ref, yl_ref, dbra_ref, yp_ref, dbrb_ref, dwo_ref, dwl_ref, dwp_ref):
        dwo_ref[...] = _mm_tn(mrg_ref[...], dm_ref[...]).astype(BF)
        dwl_ref[...] = _mm_tn(yl_ref[...], dbra_ref[...]).astype(BF)
        dwp = _mm_tn(yp_ref[...], dbrb_ref[...]).astype(BF)
        for k in range(NCHIP):
            dwp_ref[k] = dwp[:, k * cpu:(k + 1) * cpu]

    cols = lambda w: pl.BlockSpec((T, w), lambda r: (0, r))
    whole = pl.BlockSpec((T, D), lambda r: (0, 0))
    return _call(
        body, name="dw_merge", grid=(nb,),
        in_specs=[cols(rb), whole, cols(rb), whole, cols(pb), whole],
        out_specs=[pl.BlockSpec((rb, D), lambda r: (r, 0)), pl.BlockSpec((rb, D), lambda r: (r, 0)),
                   pl.BlockSpec((NCHIP, pb, cpu), lambda r: (0, r, 0))],
        out_shape=[jax.ShapeDtypeStruct((D, D), BF), jax.ShapeDtypeStruct((DR, D), BF),
                   jax.ShapeDtypeStruct((NCHIP, DP, cpu), BF)],
        vmem=56, args=[mrg, dm, ylru, dbra, ypool, dbrb], stages=stages)


def _bwd_lru(proj, h, dylru, conv_w, conv_b, wa, ba, wx, bx, lam, stages=()):
    def body(xp_ref, g_ref, h_ref, dy_ref, cw_ref, cb_ref, wa_ref, ba_ref, wx_ref, bx_ref, lam_ref,
             dxp_ref, dg_ref, dcw_ref, dcb_ref, dwa_ref, dba_ref, dwx_ref, dbx_ref, dlam_ref, a_s, b_s, l_s):
        xp = xp_ref[...]
        cw = cw_ref[...]
        lam = lam_ref[...]
        xc, x1, x2, x3 = _conv(xp, cw, cb_ref[...])
        wa, wx = wa_ref[0], wx_ref[0]
        xcb, r, ii, sp, a, mult = _lru_gates(xc, wa, ba_ref[...], wx, bx_ref[...], lam)
        g = g_ref[...]
        gel, dgel = _gelu_parts(g)
        h = h_ref[...]
        dy = dy_ref[...]
        dg_ref[...] = (dy * h * dgel).astype(BF)
        _tile_scan(_su(a, 1, 0.0), dy * gel, a_s, b_s, l_s, reverse=True)
        b = l_s[...]
        da = b * _sd(h, 1, 0.0)
        dmult = b * (ii * xc)
        dii = b * (mult * xc)
        dxc = b * (mult * ii)
        dla = da * a - dmult * ((a * a) / mult)
        dr = dla * ((-LRU_C) * sp)
        dsp = _colsum(dla * ((-LRU_C) * r))
        dlam_ref[...] = -dsp / (1.0 + jnp.exp(lam))
        dzr = dr * (r * (1.0 - r))
        dzi = dii * (ii * (1.0 - ii))
        dzrb, dzib = dzr.astype(BF), dzi.astype(BF)
        dxc = dxc + _mm_nt(dzrb, wa) + _mm_nt(dzib, wx)
        dwa_ref[0] = _mm_tn(xcb, dzrb)
        dwx_ref[0] = _mm_tn(xcb, dzib)
        dba_ref[...] = _colsum(dzr)
        dbx_ref[...] = _colsum(dzi)
        dcb_ref[...] = _colsum(dxc)
        dcw_ref[...] = jnp.concatenate([_colsum(dxc * x3), _colsum(dxc * x2), _colsum(dxc * x1),
                                        _colsum(dxc * xp)], axis=0)
        dxp = cw[3:4] * dxc + cw[2:3] * _su(dxc, 1) + cw[1:2] * _su(dxc, 2) + cw[0:1] * _su(dxc, 3)
        dxp_ref[...] = dxp.astype(BF)

    blk = pl.BlockSpec((T, CB), lambda j: (0, j))
    wsp = pl.BlockSpec((1, CB, CB), lambda j: (j, 0, 0))
    return _call(
        body, name="bwd_lru", grid=(NG,),
        in_specs=[blk, pl.BlockSpec((T, CB), lambda j: (0, NG + j)), blk, blk,
                  pl.BlockSpec((4, CB), lambda j: (0, j)), _vec_spec(), wsp, _vec_spec(), wsp, _vec_spec(),
                  _vec_spec()],
        out_specs=[blk, blk, pl.BlockSpec((4, CB), lambda j: (0, j)), _vec_spec(), wsp, _vec_spec(), wsp,
                   _vec_spec(), _vec_spec()],
        out_shape=[jax.ShapeDtypeStruct((T, DR), BF), jax.ShapeDtypeStruct((T, DR), BF),
                   jax.ShapeDtypeStruct((4, DR), F32), jax.ShapeDtypeStruct((1, DR), F32),
                   jax.ShapeDtypeStruct((NG, CB, CB), F32), jax.ShapeDtypeStruct((1, DR), F32),
                   jax.ShapeDtypeStruct((NG, CB, CB), F32), jax.ShapeDtypeStruct((1, DR), F32),
                   jax.ShapeDtypeStruct((1, DR), F32)],
        vmem=56, args=[proj, proj, h, dylru, conv_w, conv_b, wa, ba, wx, bx, lam], stages=stages,
        scratch=[pltpu.VMEM((T, CB), F32)] * 3)


def _bwd_pool(proj, dypool, pool_w, pool_scale):
    def body(xp_ref, dy_ref, pw_ref, sc_ref, dx_ref, dw_ref, dsc_ref):
        for g, w in enumerate(POOL_WINDOWS):
            cols = slice(g * PG, (g + 1) * PG)
            cnt = _pool_cnt(w)
            x = xp_ref[:, cols]
            pb = (_pool_window(x, g + 1, _sd) / cnt - x).astype(BF)
            wg = pw_ref[g]
            dy = dy_ref[:, cols]
            dsc_ref[:, cols] = _colsum(dy * _mm(pb, wg))
            dyp = (dy * sc_ref[:, cols]).astype(BF)
            dw_ref[g] = _mm_tn(pb, dyp)
            dp = _mm_nt(dyp, wg)
            dx_ref[:, cols] = (_pool_window(dp / cnt, g + 1, _su) - dp).astype(BF)

    return pl.pallas_call(
        body, name="bwd_pool", grid=(1,),
        in_specs=[pl.BlockSpec((T, DP), lambda i: (0, 2 * DR // DP)),
                  pl.BlockSpec((T, DP), lambda i: (0, 0)),
                  pl.BlockSpec((4, PG, PG), lambda i: (0, 0, 0)),
                  pl.BlockSpec((1, DP), lambda i: (0, 0))],
        out_specs=[pl.BlockSpec((T, DP), lambda i: (0, 0)),
                   pl.BlockSpec((4, PG, PG), lambda i: (0, 0, 0)),
                   pl.BlockSpec((1, DP), lambda i: (0, 0))],
        out_shape=_hbm_out([jax.ShapeDtypeStruct((T, DP), BF), jax.ShapeDtypeStruct((4, PG, PG), F32),
                            jax.ShapeDtypeStruct((1, DP), F32)]),
        compiler_params=_cp(48),
    )(*_hbm(proj, dypool, pool_w, pool_scale))


PART_COLS = (DR, DR, DP, 2 * D)


def _shard_pieces():
    starts = [sum(PART_COLS[:p]) for p in range(len(PART_COLS))]
    shards = []
    for k in range(NCHIP):
        lo, hi = k * CW_IN, (k + 1) * CW_IN
        shards.append([(p, max(lo, s) - s, min(hi, s + wd) - s, max(lo, s) - lo)
                       for p, (s, wd) in enumerate(zip(starts, PART_COLS)) if max(lo, s) < min(hi, s + wd)])
    return shards


def _bwd_inproj_w(h1, parts, after):
    def body(h_ref, p0, p1, p2, p3, after_ref, dw_ref):
        part_refs = (p0, p1, p2, p3)
        for k, pieces in enumerate(_shard_pieces()):
            for p, a, b, c0 in pieces:
                dw_ref[k, :, c0:c0 + b - a] = _mm_tn(h_ref[...], part_refs[p][:, a:b]).astype(BF)

    vmem = pl.BlockSpec(memory_space=pltpu.VMEM)
    return pl.pallas_call(
        body, name="bwd_inproj_w", in_specs=[vmem] * 5 + [ANY], out_specs=vmem,
        out_shape=pltpu.HBM((NCHIP, D, CW_IN), BF), compiler_params=_cp(48),
    )(*_hbm(h1, *parts), after)


def _bwd_inproj_x(parts, w_in, x, dxres, g1, stages=()):
    tm = 512

    def body(p0, p1, p2, p3, w_ref, x_ref, dr_ref, g_ref, dx_ref, dg_ref):
        part_refs = (p0, p1, p2, p3)
        dh = None
        for k, pieces in enumerate(_shard_pieces()):
            for p, a, b, c0 in pieces:
                part = _mm_nt(part_refs[p][:, a:b], w_ref[k, :, c0:c0 + b - a])
                dh = part if dh is None else dh + part
        xv = x_ref[...]
        r = lax.rsqrt(_mean(xv * xv) + NORM_EPS)
        xn = xv * r
        t = dh * g_ref[...]
        dx_ref[...] = dr_ref[...] + r * (t - xn * _mean(t * xn))
        _acc(dg_ref, _colsum(dh * xn), pl.program_id(0) == 0)

    row = pl.BlockSpec((tm, D), lambda i: (i, 0))
    vec = pl.BlockSpec((1, D), lambda i: (0, 0))
    return _call(
        body, name="bwd_inproj_x", grid=(T // tm,),
        in_specs=[pl.BlockSpec((tm, wd), lambda i: (i, 0)) for wd in PART_COLS] +
                 [pl.BlockSpec((NCHIP, D, CW_IN), lambda i: (0, 0, 0)), row, row, vec],
        out_specs=[row, vec],
        out_shape=[jax.ShapeDtypeStruct((T, D), F32), jax.ShapeDtypeStruct((1, D), F32)],
        vmem=56, args=[*parts, w_in, x, dxres, g1], stages=stages)[0]


def _place():
    x, y, c = lax.axis_index("x"), lax.axis_index("y"), lax.axis_index("c")
    chips = [(1 - x, y), (x, 1 - y), (1 - x, 1 - y)]
    return x, y, c, chips


def _rcopy(src, dst, ssem, rsem, dev):
    return pltpu.make_async_remote_copy(src_ref=src, dst_ref=dst, send_sem=ssem, recv_sem=rsem,
                                        device_id=dev, device_id_type=MESH_ID)


def _sds(a):
    return jax.ShapeDtypeStruct(a.shape, a.dtype)


def _sem2(n, m):
    return [pltpu.SemaphoreType.DMA((n * m,)), pltpu.SemaphoreType.DMA((n * m,))]


ALL = (0, 1, 1)


def _piece(ref, k, half, part):
    hr = ref.shape[1] // 2
    r0, r1 = hr * part[0] // part[2], hr * part[1] // part[2]
    return ref.at[k, pl.ds(half * hr + r0, r1 - r0), :]


def _gather(fulls, ici=(), d2d=()):
    n = len(fulls)
    ici, d2d = list(ici), list(d2d)
    pieces = [("ici", i, part) for i, part in ici] + [("d2d", i, part) for i, part in d2d]

    def copies(outs, sems):
        x, y, c, chips = _place()
        me = 2 * x + y
        sib = (x, y, 1 - c)
        send, recv = [], []
        for q, (kind, i, part) in enumerate(pieces):
            for j, chip in enumerate(chips):
                k, s = 2 * chip[0] + chip[1], 3 * q + j
                if kind == "ici":
                    mine, theirs, dev = _piece(outs[i], me, c, part), _piece(outs[i], k, c, part), (*chip, c)
                else:
                    mine, theirs, dev = _piece(outs[i], k, c, part), _piece(outs[i], k, 1 - c, part), sib
                send.append(_rcopy(mine, mine, sems[0].at[s], sems[1].at[s], dev))
                recv.append(_rcopy(theirs, theirs, sems[0].at[s], sems[1].at[s], dev))
        return send, recv

    def start(ins, outs, sems):
        for cp in copies(outs, sems)[0]:
            cp.start()

    def finish(ins, outs, sems):
        send, recv = copies(outs, sems)
        for cp in recv:
            cp.wait_recv()
        for cp in send:
            cp.wait_send()

    sems = [pltpu.SemaphoreType.DMA((3 * len(pieces),)), pltpu.SemaphoreType.DMA((3 * len(pieces),))]
    return _Stage(fulls, [_sds(f) for f in fulls], {i: i for i in range(n)}, sems, start, finish)


def _gather_whole(v):
    def copies(ins, outs, sems):
        x, y, c, chips = _place()
        me = 2 * x + y
        send = [_rcopy(ins[0], outs[0].at[me], sems[0].at[j], sems[1].at[j], (*chip, c))
                for j, chip in enumerate(chips)]
        recv = [_rcopy(ins[0], outs[0].at[2 * chip[0] + chip[1]], sems[0].at[j], sems[1].at[j], (*chip, c))
                for j, chip in enumerate(chips)]
        return send, recv

    def start(ins, outs, sems):
        for cp in copies(ins, outs, sems)[0]:
            cp.start()

    def finish(ins, outs, sems):
        send, recv = copies(ins, outs, sems)
        for cp in recv:
            cp.wait_recv()
        for cp in send:
            cp.wait_send()

    return _Stage([v], [jax.ShapeDtypeStruct((NCHIP,) + v.shape, v.dtype)], {},
                  [pltpu.SemaphoreType.DMA((3,)), pltpu.SemaphoreType.DMA((3,))], start, finish)


def _to_sibling(srcs):
    n = len(srcs)

    def copies(ins, outs, sems):
        x, y, c, _ = _place()
        sib = (x, y, 1 - c)
        return [_rcopy(ins[i].at[:, 1 - c] if srcs[i].ndim == 4 else ins[i], outs[i], sems[0].at[i], sems[1].at[i], sib)
                for i in range(n)]

    def start(ins, outs, sems):
        for cp in copies(ins, outs, sems):
            cp.start()

    def finish(ins, outs, sems):
        for cp in copies(ins, outs, sems):
            cp.wait()

    shapes = [jax.ShapeDtypeStruct((NCHIP,) + s.shape[2:] if s.ndim == 4 else s.shape, s.dtype) for s in srcs]
    return _Stage(srcs, shapes, {}, [pltpu.SemaphoreType.DMA((n,)), pltpu.SemaphoreType.DMA((n,))], start, finish)


def _to_chips(srcs, parts=None, lands=None):
    n = len(srcs)
    parts = [ALL] * n if parts is None else parts
    lands = [None] * n if lands is None else lands
    given = [i for i in range(n) if lands[i] is not None]

    def rows(ref, i):
        hr = srcs[i].shape[1]
        r0, r1 = hr * parts[i][0] // parts[i][2], hr * parts[i][1] // parts[i][2]
        return ref.at[pl.ds(r0, r1 - r0), :]

    def copies(ins, outs, sems):
        x, y, c, chips = _place()
        me = 2 * x + y
        return [_rcopy(rows(ins[i].at[2 * chip[0] + chip[1]] if srcs[i].shape[0] == NCHIP else ins[i].at[c], i),
                       rows(outs[i].at[me], i), sems[0].at[3 * i + j], sems[1].at[3 * i + j], (*chip, c))
                for i in range(n) for j, chip in enumerate(chips)]

    def start(ins, outs, sems):
        for cp in copies(ins, outs, sems):
            cp.start()

    def finish(ins, outs, sems):
        for cp in copies(ins, outs, sems):
            cp.wait()

    shapes = [jax.ShapeDtypeStruct((NCHIP,) + s.shape[1:], s.dtype) for s in srcs]
    alias = {n + q: i for q, i in enumerate(given)}
    return _Stage(list(srcs) + [lands[i] for i in given], shapes, alias, _sem2(n, 3), start, finish)


HBM_REF = pl.BlockSpec(memory_space=pltpu.HBM)
SEM_REF = pl.BlockSpec(memory_space=pltpu.SEMAPHORE)
DATAFLOW = pltpu.SideEffectType.DATAFLOW_SIDE_EFFECTING


def _after(x):
    return _Stage([x], [], {}, [], lambda *a: None, lambda *a: None)


class _Flight:
    def __init__(self, stage, sems, bufs):
        self.stage, self.sems, self.bufs = stage, list(sems), list(bufs)

    def landed(self):
        st, n = self.stage, len(self.stage.operands)
        fresh = [j for j in range(len(st.out_shape)) if j not in st.alias.values()]
        back = {v: k for k, v in st.alias.items()}
        return [self.bufs[back[j]] if j in back else self.bufs[n + fresh.index(j)] for j in range(len(st.out_shape))]


def _split_call(name, finish=(), start=(), after=None):
    bufs, stage_bufs = [], []

    def slot(a):
        for i, b in enumerate(bufs):
            if b is a:
                return i
        bufs.append(a)
        return len(bufs) - 1

    fin_slots = [[slot(b) for b in fl.bufs] for fl in finish]
    for st in start:
        fresh = [lax.empty(o.shape, o.dtype) for j, o in enumerate(st.out_shape) if j not in st.alias.values()]
        stage_bufs.append([slot(a) for a in list(st.operands) + fresh])
    old_sems = [s for fl in finish for s in fl.sems]
    new_sems = [s for st in start for s in st.sems]
    nb, no, nn = len(bufs), len(old_sems), len(new_sems)

    def refs_of(st, slots, buf_refs):
        n = len(st.operands)
        ins = [buf_refs[i] for i in slots[:n]]
        fresh = [j for j in range(len(st.out_shape)) if j not in st.alias.values()]
        back = {v: k for k, v in st.alias.items()}
        outs = [ins[back[j]] if j in back else buf_refs[slots[n + fresh.index(j)]] for j in range(len(st.out_shape))]
        return ins, outs

    def body(*refs):
        buf_refs, sem_in = refs[:nb], refs[nb:nb + no]
        sem_out = refs[nb + no + (after is not None):][:nn]
        token = refs[-1]
        pos = 0
        for fl, slots in zip(finish, fin_slots):
            ins, outs = refs_of(fl.stage, slots, buf_refs)
            fl.stage.finish(ins, outs, sem_in[pos:pos + len(fl.sems)])
            pos += len(fl.sems)
        pos = 0
        for st, slots in zip(start, stage_bufs):
            ins, outs = refs_of(st, slots, buf_refs)
            st.start(ins, outs, sem_out[pos:pos + len(st.sems)])
            pos += len(st.sems)
        token[...] = jnp.zeros_like(token)

    res = pl.pallas_call(
        body, name=name,
        out_shape=tuple(new_sems) + tuple(pltpu.HBM(b.shape, b.dtype) for b in bufs) +
                  (jax.ShapeDtypeStruct((8, LANE), F32),),
        in_specs=(HBM_REF,) * nb + (SEM_REF,) * no + ((pl.BlockSpec(memory_space=pl.ANY),) if after is not None else ()),
        out_specs=(SEM_REF,) * nn + (HBM_REF,) * nb + (pl.BlockSpec(memory_space=pltpu.VMEM),),
        input_output_aliases={i: nn + i for i in range(nb)},
        compiler_params=pltpu.CompilerParams(has_side_effects=DATAFLOW),
    )(*_hbm(*bufs), *old_sems, *([after] if after is not None else []))
    sems, thru, token = res[:nn], res[nn:nn + nb], res[-1]
    for fl, slots in zip(finish, fin_slots):
        fl.bufs = [thru[i] for i in slots]
    flights, pos = [], 0
    for st, slots in zip(start, stage_bufs):
        flights.append(_Flight(st, sems[pos:pos + len(st.sems)], [thru[i] for i in slots]))
        pos += len(st.sems)
    return flights, token


def _share(pairs):
    n = len(pairs)

    def start(ins, outs, sems):
        x, y, c, _ = _place()
        for i in range(n):
            _rcopy(outs[i].at[c], outs[i].at[c], sems[0].at[i], sems[1].at[i], (x, y, 1 - c)).start()

    def finish(ins, outs, sems):
        x, y, c, _ = _place()
        for i in range(n):
            _rcopy(outs[i].at[c], outs[i].at[c], sems[0].at[i], sems[1].at[i], (x, y, 1 - c)).wait_send()
            _rcopy(outs[i].at[1 - c], outs[i].at[1 - c], sems[0].at[i], sems[1].at[i], (x, y, 1 - c)).wait_recv()

    return _Stage(pairs, [_sds(p) for p in pairs], {i: i for i in range(n)},
                  [pltpu.SemaphoreType.DMA((n,)), pltpu.SemaphoreType.DMA((n,))], start, finish)


def _row_block(rows, cols, itemsize=4, target=2 * MIB):
    br = rows
    while br * cols * itemsize > target and br % 32 == 0:
        br //= 2
    return br


def _cast_place(w, chip_idx, name):
    rows, cols = w.shape
    br = _row_block(rows, cols)

    def body(k_ref, w_ref, o_ref):
        o_ref[0] = w_ref[...].astype(BF)

    return _call(
        body, name=name, grid=(rows // br,), prefetch=chip_idx,
        in_specs=[pl.BlockSpec((br, cols), lambda r, k: (r, 0))],
        out_specs=[pl.BlockSpec((1, br, cols), lambda r, k: (k[0], r, 0))],
        out_shape=[jax.ShapeDtypeStruct((NCHIP, rows, cols), BF)], vmem=32, args=[w])[0][0]


def _cast_place_multi(ws, chip_idx, stages=()):
    br = 128
    nblk = [a.shape[0] // br for a in ws]
    starts = [sum(nblk[:i]) for i in range(len(ws))]

    def body(k_ref, *refs):
        r = pl.program_id(0)
        for i in range(len(ws)):
            @pl.when(jnp.logical_and(r >= starts[i], r < starts[i] + nblk[i]))
            def _(i=i):
                refs[len(ws) + i][0] = refs[i][...].astype(BF)

    def at(i):
        return functools.partial(lambda r, s, nb: jnp.clip(r - s, 0, nb - 1), s=starts[i], nb=nblk[i])

    outs, landed = _call(
        body, name="cast_rest", grid=(sum(nblk),), prefetch=chip_idx,
        in_specs=[pl.BlockSpec((br, a.shape[1]), functools.partial(lambda r, k, f: (f(r), 0), f=at(i)))
                  for i, a in enumerate(ws)],
        out_specs=[pl.BlockSpec((1, br, a.shape[1]), functools.partial(lambda r, k, f: (k[0], f(r), 0), f=at(i)))
                   for i, a in enumerate(ws)],
        out_shape=[jax.ShapeDtypeStruct((NCHIP,) + a.shape, BF) for a in ws], vmem=32, args=list(ws), stages=stages)
    return outs, landed


def _add_sibling(g, land, cidx, name, stages=()):
    _, _, hr, cols = g.shape
    br = _row_block(hr, cols)

    def body(c_ref, g_ref, l_ref, o_ref):
        o_ref[...] = (g_ref[0, 0].astype(F32) + l_ref[0].astype(F32)).astype(BF)[None]

    outs, st = _call(
        body, name=name, grid=(NCHIP, hr // br), prefetch=cidx,
        in_specs=[pl.BlockSpec((1, 1, br, cols), lambda k, r, c: (k, c[0], r, 0)),
                  pl.BlockSpec((1, br, cols), lambda k, r, c: (k, r, 0))],
        out_specs=[pl.BlockSpec((1, br, cols), lambda k, r, c: (k, r, 0))],
        out_shape=[jax.ShapeDtypeStruct((NCHIP, hr, cols), BF)], vmem=32, args=[g, land], stages=stages)
    return outs[0], st


def _add_sibling_multi(gs, lands, cidx, name):
    n = len(gs)
    brs = [_row_block(g.shape[2], g.shape[3]) for g in gs]
    nrb = [g.shape[2] // b for g, b in zip(gs, brs)]
    nblk = [NCHIP * q for q in nrb]
    starts = [sum(nblk[:i]) for i in range(n)]

    def body(c_ref, *refs):
        r = pl.program_id(0)
        for i in range(n):
            g_ref, l_ref, o_ref = refs[2 * i], refs[2 * i + 1], refs[2 * n + i]

            @pl.when(jnp.logical_and(r >= starts[i], r < starts[i] + nblk[i]))
            def _():
                o_ref[...] = (g_ref[0, 0].astype(F32) + l_ref[0].astype(F32)).astype(BF)[None]

    def at(i, r):
        q = jnp.clip(r - starts[i], 0, nblk[i] - 1)
        return q // nrb[i], q % nrb[i]

    def g_spec(i):
        return pl.BlockSpec((1, 1, brs[i], gs[i].shape[3]),
                            functools.partial(lambda r, c, i: (at(i, r)[0], c[0], at(i, r)[1], 0), i=i))

    def l_spec(i):
        return pl.BlockSpec((1, brs[i], gs[i].shape[3]),
                            functools.partial(lambda r, c, i: (at(i, r)[0], at(i, r)[1], 0), i=i))

    return _call(
        body, name=name, grid=(sum(nblk),), prefetch=cidx,
        in_specs=[s for i in range(n) for s in (g_spec(i), l_spec(i))], out_specs=[l_spec(i) for i in range(n)],
        out_shape=[jax.ShapeDtypeStruct(l.shape, BF) for l in lands], vmem=32,
        args=[a for i in range(n) for a in (gs[i], lands[i])])[0]


def _add_pair(a, b, name):
    rows, cols = a.shape

    def body(a_ref, b_ref, o_ref):
        o_ref[...] = a_ref[...] + b_ref[...]

    spec = pl.BlockSpec((rows, cols), lambda r: (0, 0))
    return _call(body, name=name, grid=(1,), in_specs=[spec, spec], out_specs=[spec], out_shape=[_sds(a)],
                 vmem=32, args=[a, b])[0][0]


def _add_chips(own, land, idx, name, stages=None):
    _, hr, cols = land.shape
    br = _row_block(hr, cols)

    def body(s_ref, a_ref, b_ref, c_ref, d_ref, o_ref):
        o_ref[...] = (a_ref[...].astype(F32) + b_ref[...].astype(F32)) + (c_ref[...].astype(F32) +
                                                                           d_ref[...].astype(F32))

    spec = lambda q: pl.BlockSpec((1, br, cols), functools.partial(lambda r, s, q: (s[q], r, 0), q=q))
    outs, landed = _call(
        body, name=name, grid=(hr // br,), prefetch=idx,
        in_specs=[spec(0), spec(1), spec(2), spec(3)], out_specs=[spec(4)],
        out_shape=[jax.ShapeDtypeStruct((2, hr, cols), F32)], vmem=48, args=[own, land, land, land],
        stages=stages or ())
    return outs[0] if stages is None else (outs[0], landed)


def _add_chips_multi(owns, lands, idx, name, stages=()):
    n = len(owns)
    brs = [_row_block(l.shape[1], l.shape[2]) for l in lands]
    nblk = [l.shape[1] // b for l, b in zip(lands, brs)]
    starts = [sum(nblk[:i]) for i in range(n)]

    def body(s_ref, *refs):
        r = pl.program_id(0)
        for i in range(n):
            a_ref, b_ref, c_ref, d_ref = refs[4 * i:4 * i + 4]
            o_ref = refs[4 * n + i]

            @pl.when(jnp.logical_and(r >= starts[i], r < starts[i] + nblk[i]))
            def _():
                o_ref[...] = (a_ref[...].astype(F32) + b_ref[...].astype(F32)) + (c_ref[...].astype(F32) +
                                                                                   d_ref[...].astype(F32))

    def spec(i, q):
        return pl.BlockSpec((1, brs[i], lands[i].shape[2]), functools.partial(
            lambda r, s, q, st, nb: (s[q], jnp.clip(r - st, 0, nb - 1), 0), q=q, st=starts[i], nb=nblk[i]))

    outs, landed = _call(
        body, name=name, grid=(sum(nblk),), prefetch=idx,
        in_specs=[spec(i, q) for i in range(n) for q in range(4)], out_specs=[spec(i, 4) for i in range(n)],
        out_shape=[jax.ShapeDtypeStruct((2,) + l.shape[1:], F32) for l in lands], vmem=48,
        args=[a for i in range(n) for a in (owns[i], lands[i], lands[i], lands[i])], stages=stages)
    return outs, landed


def _adamw_math(w, g, m, v):
    mn = ADAM_B1 * m + (1.0 - ADAM_B1) * g
    vn = ADAM_B2 * v + (1.0 - ADAM_B2) * (g * g)
    m_hat = mn / (1.0 - ADAM_B1 ** ADAM_STEP)
    v_hat = vn / (1.0 - ADAM_B2 ** ADAM_STEP)
    return -ADAM_LR * (m_hat / (jnp.sqrt(v_hat) + ADAM_EPS) + ADAM_WD * w), mn, vn


def _adamw(w, g, m, v, name, stages=()):
    rows, cols = w.shape
    br = _row_block(rows, cols)

    def body(w_ref, g_ref, m_ref, v_ref, go_ref, d_ref, mo_ref, vo_ref):
        gv = g_ref[...]
        go_ref[...] = gv
        d_ref[...], mo_ref[...], vo_ref[...] = _adamw_math(w_ref[...], gv, m_ref[...], v_ref[...])

    spec = pl.BlockSpec((br, cols), lambda r: (r, 0))
    return _call(body, name=name, grid=(rows // br,), in_specs=[spec] * 4, out_specs=[spec] * 4,
                 out_shape=[_sds(w)] * 4, vmem=56, args=[w, g, m, v], stages=stages)


def _adamw_multi(names, w, g, m, v, stages=()):
    cols = w[names[0]].shape[1]
    br = 128
    nblk = [w[n].shape[0] // br for n in names]
    starts = [sum(nblk[:i]) for i in range(len(names))]

    def body(*refs):
        r = pl.program_id(0)
        for i in range(len(names)):
            w_ref, g_ref, m_ref, v_ref = refs[4 * i:4 * i + 4]
            go_ref, d_ref, mo_ref, vo_ref = refs[4 * len(names) + 4 * i:4 * len(names) + 4 * i + 4]

            @pl.when(jnp.logical_and(r >= starts[i], r < starts[i] + nblk[i]))
            def _():
                gv = g_ref[...]
                go_ref[...] = gv
                d_ref[...], mo_ref[...], vo_ref[...] = _adamw_math(w_ref[...], gv, m_ref[...], v_ref[...])

    def spec(i):
        return pl.BlockSpec((br, cols), functools.partial(
            lambda r, s, nb: (jnp.clip(r - s, 0, nb - 1), 0), s=starts[i], nb=nblk[i]))

    outs, landed = _call(
        body, name="adamw_" + "_".join(names), grid=(sum(nblk),),
        in_specs=[spec(i) for i in range(len(names)) for _ in range(4)],
        out_specs=[spec(i) for i in range(len(names)) for _ in range(4)],
        out_shape=[_sds(w[n]) for n in names for _ in range(4)], vmem=56,
        args=[a[n] for n in names for a in (w, g, m, v)], stages=stages)
    return {n: outs[4 * i:4 * i + 4] for i, n in enumerate(names)}, landed


def _to_everyone(v):
    deltas = [(a, b, e) for a in (0, 1) for b in (0, 1) for e in (0, 1)][1:]

    def copies(ins, outs, sems):
        x, y, c, _ = _place()
        me = 4 * x + 2 * y + c
        flip = lambda p, f: 1 - p if f else p
        return [_rcopy(ins[0], outs[0].at[me], sems[0].at[q], sems[1].at[q], (flip(x, a), flip(y, b), flip(c, e)))
                for q, (a, b, e) in enumerate(deltas)]

    def start(ins, outs, sems):
        for cp in copies(ins, outs, sems):
            cp.start()

    def finish(ins, outs, sems):
        for cp in copies(ins, outs, sems):
            cp.wait()

    n = len(deltas)
    return _Stage([v], [jax.ShapeDtypeStruct((2 * NCHIP,) + v.shape, v.dtype)], {},
                  [pltpu.SemaphoreType.DMA((n,)), pltpu.SemaphoreType.DMA((n,))], start, finish)


SMALL_AT = {"norm_mix_pre": (0, 1, D), "norm_mix_post": (1, 1, D), "norm_mlp_pre": (2, 1, D),
            "norm_mlp_post": (3, 1, D), "b_gate": (4, 2, D), "conv_b": (6, 1, D), "lru_b_a": (7, 1, D),
            "lru_b_x": (8, 1, D), "lru_lambda": (9, 1, D), "pool_scale": (10, 1, DP)}
SMALL_SEPARATE = ["conv_w", "lru_w_a", "lru_w_x", "pool_w"]


def _adamw_small(small_sum, first_all, sep_grads, w, m, v):
    packed, sep = list(SMALL_AT), list(SMALL_SEPARATE)
    names = packed + sep

    def body(*refs):
        s_ref, a_ref, refs = refs[0], refs[1], refs[2:]
        g_sep, refs = refs[:len(sep)], refs[len(sep):]
        nn = len(names)
        w_r, m_r, v_r, refs = refs[:nn], refs[nn:2 * nn], refs[2 * nn:3 * nn], refs[3 * nn:]
        g_out, refs = refs[:len(packed)], refs[len(packed):]
        d_o, m_o, v_o = refs[:nn], refs[nn:2 * nn], refs[2 * nn:3 * nn]
        for i, n in enumerate(names):
            if i == 0:
                g = a_ref[0:1, :]
                for q in range(1, 2 * NCHIP):
                    g = g + a_ref[q:q + 1, :]
                g_out[i][...] = g
            elif n in SMALL_AT:
                r0, nr, nc = SMALL_AT[n]
                g = jnp.concatenate([s_ref[r0 + q:r0 + q + 1, :nc] for q in range(nr)], axis=1)
                g_out[i][...] = g
            else:
                g = g_sep[i - len(packed)][...]
            d_o[i][...], m_o[i][...], v_o[i][...] = _adamw_math(w_r[i][...], g, m_r[i][...], v_r[i][...])

    ws = [w[n] for n in names]
    res = pl.pallas_call(
        body, name="adamw_small",
        out_shape=[_sds(w[n]) for n in packed] + [_sds(a) for a in ws] * 3,
        compiler_params=_cp(32),
    )(*_hbm(small_sum, first_all, *sep_grads, *ws, *[m[n] for n in names], *[v[n] for n in names]))
    nn, npk = len(names), len(packed)
    grad = dict(zip(packed, res[:npk]))
    delta = dict(zip(names, res[npk:npk + nn]))
    new_m = dict(zip(names, res[npk + nn:npk + 2 * nn]))
    new_v = dict(zip(names, res[npk + 2 * nn:]))
    return grad, delta, new_m, new_v


W_NAMES = ["norm_mix_pre", "norm_mix_post", "norm_mlp_pre", "norm_mlp_post", "w_in", "b_gate", "conv_w", "conv_b",
           "lru_w_a", "lru_b_a", "lru_w_x", "lru_b_x", "lru_lambda", "pool_w", "pool_scale", "w_lru_up",
           "w_pool_up", "w_o", "w_ff1", "w_ff2"]
BIG = ["w_in", "w_lru_up", "w_pool_up", "w_o", "w_ff1", "w_ff2"]


def _block_diag(w):
    hd = w.shape[-1]
    per = CB // hd
    w4 = w.reshape(NG, per, hd, hd)
    eye = jnp.eye(per, dtype=w.dtype)
    return jnp.einsum("gpij,pq->gpiqj", w4, eye).reshape(NG, CB, CB)


def _block_diag_extract(d, hd):
    per = CB // hd
    d5 = d.reshape(NG, per, hd, per, hd)
    return jnp.stack([d5[:, p, :, p, :] for p in range(per)], axis=1).reshape(NG * per, hd, hd)


def _halves(g):
    return g.reshape(NCHIP, 2, g.size // (g.shape[-1] * 2 * NCHIP), g.shape[-1])


def kernel(x, norm_mix_pre, norm_mix_post, norm_mlp_pre, norm_mlp_post, w_in, b_gate, conv_w, conv_b, lru_w_a, lru_b_a, lru_w_x, lru_b_x, lru_lambda, pool_w, pool_scale, w_lru_up, w_pool_up, w_o, w_ff1, w_ff2, loss_target, m_norm_mix_pre, m_norm_mix_post, m_norm_mlp_pre, m_norm_mlp_post, m_w_in, m_b_gate, m_conv_w, m_conv_b, m_lru_w_a, m_lru_b_a, m_lru_w_x, m_lru_b_x, m_lru_lambda, m_pool_w, m_pool_scale, m_w_lru_up, m_w_pool_up, m_w_o, m_w_ff1, m_w_ff2, v_norm_mix_pre, v_norm_mix_post, v_norm_mlp_pre, v_norm_mlp_post, v_w_in, v_b_gate, v_conv_w, v_conv_b, v_lru_w_a, v_lru_b_a, v_lru_w_x, v_lru_b_x, v_lru_lambda, v_pool_w, v_pool_scale, v_w_lru_up, v_w_pool_up, v_w_o, v_w_ff1, v_w_ff2):
    args = dict(locals())
    two_d = lambda a: a.reshape(-1, a.shape[-1])
    w = {n: two_d(args[n]) for n in W_NAMES}
    mom = {n: two_d(args["m_" + n]) for n in W_NAMES}
    var = {n: two_d(args["v_" + n]) for n in W_NAMES}
    i32 = lambda val: jnp.asarray(val, jnp.int32)
    chip = i32(2 * lax.axis_index("x") + lax.axis_index("y"))
    core = i32(lax.axis_index("c"))
    cidx = core.reshape(1)
    zero = i32(0)
    hd = lru_w_a.shape[-1]
    xs, target = x[0], loss_target[0]
    g1, g2, g3, g4 = norm_mix_pre, norm_mix_post, norm_mlp_pre, norm_mlp_post

    mix = ["w_lru_up", "w_pool_up", "w_o"]
    full = {"w_in": _cast_place(w["w_in"], chip.reshape(1), "cast_w_in")}
    (fl_in, fl_conv), first = _split_call("gather_start_first", start=[
        _gather([full["w_in"]], ici=[(0, ALL)]), _gather_whole(w["conv_w"])])
    casts, _ = _cast_place_multi([w[n] for n in BIG[1:]], chip.reshape(1), stages=[_after(first)])
    full.update(zip(BIG[1:], casts))
    (fl_mix, fl_ff1, fl_ff2), started = _split_call("gather_start_rest", start=[
        _gather([full[n] for n in mix], ici=[(0, ALL), (1, ALL), (2, ALL)]),
        _gather([full["w_ff1"]], ici=[(0, ALL)]), _gather([full["w_ff2"]], ici=[(0, ALL)])])
    wa = _block_diag(lru_w_a[0]).astype(BF)
    wx = _block_diag(lru_w_x[0]).astype(BF)
    pw = pool_w[0].astype(BF)

    def to_sibling(name, flight, after=None):
        (fl,), passed = _split_call(name + "_pass", finish=[flight], after=after,
                                    start=[_gather(flight.landed(), d2d=[(i, ALL) for i in range(len(flight.bufs))])])
        passed_on.append(passed)
        return fl

    passed_on = []

    def arrived(name, flight, after=None):
        _split_call(name + "_done", finish=[flight], after=after)
        return flight.landed()

    idx_big = jnp.stack([chip, (chip + 1) % NCHIP, (chip + 2) % NCHIP, (chip + 3) % NCHIP, core])
    proj, h1 = _fwd_inproj_own(xs, g1, fl_in.bufs[0], idx_big, stages=[_after(started)])
    fl_in = to_sibling("gather_w_in", fl_in, after=h1)
    _split_call("gather_w_in_done", finish=[fl_in, fl_conv])
    (w_in_f,), (conv_all,) = fl_in.landed(), fl_conv.landed()
    full["w_in"] = w_in_f
    conv_all = lax.dynamic_update_slice(conv_all, w["conv_w"][None], (chip, zero, zero))
    conv_full = jnp.transpose(conv_all, (1, 0, 2)).reshape(4, DR)
    proj = _fwd_inproj_rest(h1, w_in_f, proj, idx_big)
    fl_mix = to_sibling("gather_mix", fl_mix, after=proj)
    (ylru, hs), _ = _fwd_lru(proj, conv_full, conv_b, wa, lru_b_a, wx, lru_b_x, lru_lambda,
                             stages=[_after(passed_on[-1])])
    got = arrived("gather_mix", fl_mix, after=ylru)
    fl_ff1 = to_sibling("gather_ff1", fl_ff1, after=ylru)
    w_lru_up_f, w_pool_up_f, w_o_f = got[0].reshape(DR, D), got[1], got[2].reshape(D, D)
    ypool = _fwd_pool(proj, pw, pool_scale)
    (x2, h2, m, mrg, bra, brb), _ = _fwd_merge(xs, ylru, ypool, proj, b_gate, g2, g3, w_lru_up_f, w_pool_up_f, w_o_f,
                                               stages=[_after(passed_on[-1])])
    fl_ff2 = to_sibling("gather_ff2", fl_ff2, after=h2)
    _split_call("gather_ff_done", finish=[fl_ff1, fl_ff2])
    (ff1,), (ff2,) = fl_ff1.landed(), fl_ff2.landed()
    ff2 = ff2.reshape(DF, D)
    a1, f = _fwd_mlp(h2, ff1, ff2)
    lossp, dy, df, dg4 = _loss_head(f, x2, target, g4)

    dh2, df1 = _bwd_mlp_x(df, a1, ff1, ff2)
    dw_ff1, dw_ff2 = _bwd_mlp_w(df, h2, a1, df1)
    g_ff = [_halves(dw_ff1), _halves(dw_ff2)]
    (dxres, dgates, dylru, dypool, dm, dbra, dbrb, dg2, dg3, dbg), (l_ff,) = _bwd_merge(
        dh2, dy, x2, m, bra, brb, proj, b_gate, g2, g3, w_lru_up_f, w_pool_up_f, w_o_f, stages=[_to_sibling(g_ff)])
    p_ff = _add_sibling_multi(g_ff, l_ff, cidx, "add_sibling_ff")
    (fl_ff,), sent_ff = _split_call("reduce_ff_start", start=[_to_chips(p_ff)])
    (dw_o, dw_lru_up, dw_pool_up), _ = _dw_merge(mrg, dm, ylru, dbra, ypool, dbrb, stages=[_after(sent_ff)])
    g_mix = [_halves(dw_lru_up), _halves(dw_pool_up), _halves(dw_o)]
    (dxp, dgl, dcw, dcb, dwa, dba, dwx, dbx, dlam), (l_mix,) = _bwd_lru(
        proj, hs, dylru, conv_full, conv_b, wa, lru_b_a, wx, lru_b_x, lru_lambda, stages=[_to_sibling(g_mix)])
    p_mix = _add_sibling_multi(g_mix, l_mix, cidx, "add_sibling_mix")
    dxpool, dpw, dsc = _bwd_pool(proj, dypool, pw, pool_scale)
    dproj = [dxp, dgl, dxpool, dgates]
    small = jnp.concatenate([
        jnp.zeros((1, D), F32), dg2, dg3, dg4, dbg.reshape(2, D), dcb, dba, dbx, dlam,
        jnp.pad(dsc, ((0, 0), (0, D - DP))), jnp.pad(lossp, ((0, 0), (0, D - 1))), dcw,
        _block_diag_extract(dwa, hd).reshape(-1, D), _block_diag_extract(dwx, hd).reshape(-1, D),
        dpw.reshape(-1, D)], axis=0)
    (fl_mixr, fl_smalls), sent_mix = _split_call("reduce_mix_start", start=[_to_chips(p_mix), _to_sibling([small])])
    dw_in = _bwd_inproj_w(h1, dproj, sent_mix)
    _split_call("reduce_small_sibling_done", finish=[fl_smalls], after=dw_in)
    small, l_small = fl_smalls.bufs
    small2 = _add_pair(small, l_small, "add_sibling_small").reshape(2, SMALL_ROWS // 2, D)
    g_in = _halves(dw_in)
    done = ["w_ff1", "w_ff2"] + mix
    (fl_gin, fl_small), sib_started = _split_call("reduce_in_sibling_start", finish=[fl_ff, fl_mixr],
                                                  start=[_to_sibling([g_in]), _to_chips([small2])])
    p_ff1, p_ff2, c_ff1, c_ff2 = fl_ff.bufs
    p_mix, c_mix = fl_mixr.bufs[:3], fl_mixr.bufs[3:]
    pairs, _ = _add_chips_multi([p_ff1, p_ff2] + p_mix, [c_ff1, c_ff2] + c_mix, idx_big, "add_chips_done",
                                stages=[_after(sib_started)])
    _split_call("reduce_in_sibling_done", finish=[fl_gin], after=pairs[-1])
    g_in, l_in = fl_gin.bufs
    p_in = _add_sibling(g_in, l_in, cidx, "add_sibling_w_in")[0]
    (fl_pin,), token = _split_call("reduce_last_start", start=[_to_chips([p_in])])
    _split_call("reduce_small_done", finish=[fl_small], after=token)
    small2, c_small = fl_small.bufs
    own_small = lax.dynamic_index_in_dim(small2, core, 0, keepdims=True)
    c_small = lax.dynamic_update_slice(c_small, own_small, (chip, zero, zero))
    pair_small = _add_chips(c_small, c_small, jnp.stack([zero, zero + 1, zero + 2, zero + 3, core]), "add_chips_small")
    (fl_share,), shared_start = _split_call("reduce_share_start", start=[_share(pairs + [pair_small])])
    grad_x, dg1 = _bwd_inproj_x(dproj, full["w_in"], xs, dxres, g1, stages=[_after(shared_start)])
    _split_call("reduce_share_done", finish=[fl_share, fl_pin], after=dg1)
    shared, (p_in, c_in) = fl_share.landed(), fl_pin.bufs
    pairs, pair_small = shared[:-1], shared[-1]

    grads, delta, new_m, new_v = {}, {}, {}, {}
    for n, p in zip(done, pairs):
        grads[n] = p.reshape(-1, p.shape[-1])

    def update(n, stages=()):
        (grads[n], delta[n], new_m[n], new_v[n]), landed = _adamw(w[n], grads[n], mom[n], var[n], "adamw_" + n,
                                                                  stages=stages)
        return landed

    pair_in = _add_chips(p_in, c_in, idx_big, "add_chips_w_in")
    (fl_last, fl_dg1), last_start = _split_call("reduce_last_share_start", start=[_share([pair_in]), _to_everyone(dg1)])
    updated, _ = _adamw_multi(["w_ff1", "w_ff2", "w_o", "w_lru_up"], w, grads, mom, var, stages=[_after(last_start)])
    for n, (go, d, mo, vo) in updated.items():
        grads[n], delta[n], new_m[n], new_v[n] = go, d, mo, vo
    _split_call("reduce_last_share_done", finish=[fl_last, fl_dg1], after=new_v["w_lru_up"])
    (pair_in,), (dg1, dg1_all) = fl_last.landed(), fl_dg1.bufs
    dg1_all = lax.dynamic_update_slice(dg1_all, dg1[None], (2 * chip + core, zero, zero)).reshape(2 * NCHIP, D)
    grads["w_in"] = pair_in.reshape(-1, pair_in.shape[-1])
    update("w_pool_up")
    update("w_in")
    small_sum = pair_small.reshape(SMALL_ROWS, D)
    loss = 0.5 * small_sum[LOSS_ROW, 0]
    ccols = DR // NCHIP
    sep = [lax.dynamic_slice(small_sum[12:16], (zero, chip * ccols), (4, ccols)),
           small_sum[16:80].reshape(-1, hd), small_sum[80:144].reshape(-1, hd), small_sum[144:208].reshape(-1, PG)]
    g_s, d_s, m_s, v_s = _adamw_small(small_sum, dg1_all, sep, w, mom, var)
    grads.update(g_s)
    grads.update(dict(zip(SMALL_SEPARATE, sep)))
    delta.update(d_s)
    new_m.update(m_s)
    new_v.update(v_s)

    out = lambda d: [d[n].reshape(args[n].shape) for n in W_NAMES]
    return (loss, grad_x[None], *out(grads), *out(delta), *out(new_m), *out(new_v))
```

```python
import functools
import math

import jax
import jax.numpy as jnp
from jax import lax
from jax.experimental import pallas as pl
from jax.experimental.pallas import tpu as pltpu

F32 = jnp.float32
BF = jnp.bfloat16

T = 2048
D = 1024
DR = 1024
DP = 512
DF = 4096
DIN = 4608
NCHIP = 4
CW_IN = DIN // NCHIP
LANE = 128
CB = 128
NG = DR // CB
PG = 128
POOL_WINDOWS = (2, 4, 8, 16)
NORM_EPS = 1e-6
LRU_C = 8.0
GELU_C = math.sqrt(2.0 / math.pi)
ADAM_LR = 0.001
ADAM_B1 = 0.9
ADAM_B2 = 0.999
ADAM_EPS = 1e-08
ADAM_WD = 0.01
ADAM_STEP = 10
MESH_ID = pl.DeviceIdType.MESH
ANY = pl.BlockSpec(memory_space=pl.ANY)
SMALL_ROWS = 208
LOSS_ROW = 11
MIB = 1 << 20


def _cp(vmem_mib=None):
    if vmem_mib is None:
        return pltpu.CompilerParams()
    return pltpu.CompilerParams(vmem_limit_bytes=vmem_mib * MIB)


def _hbm(*arrays):
    return [pltpu.with_memory_space_constraint(a, pltpu.HBM) for a in arrays]


def _hbm_out(shapes):
    return [pltpu.HBM(s.shape, s.dtype) for s in shapes]


class _Stage:
    def __init__(self, operands, out_shape, alias, sems, start, finish):
        self.operands, self.out_shape, self.alias, self.sems = list(operands), list(out_shape), dict(alias), list(sems)
        self.start, self.finish = start, finish


def _call(body, *, name, grid, in_specs, out_specs, out_shape, args, vmem=None, stages=(), prefetch=None,
          scratch=()):
    nin, nout = len(in_specs), len(out_specs)
    npre = 0 if prefetch is None else 1
    st_args, st_shapes, st_sems, aliases = [], [], list(scratch), {}
    for st in stages:
        for k, v in st.alias.items():
            aliases[npre + nin + len(st_args) + k] = nout + len(st_shapes) + v
        st_args += st.operands
        st_shapes += st.out_shape
        st_sems += st.sems

    def wrapped(*refs):
        pre, refs = refs[:npre], refs[npre:]
        ins, pos = refs[:nin], nin
        st_ins = []
        for st in stages:
            st_ins.append(refs[pos:pos + len(st.operands)])
            pos += len(st.operands)
        outs, pos = refs[pos:pos + nout], pos + nout
        st_outs = []
        for st in stages:
            st_outs.append(refs[pos:pos + len(st.out_shape)])
            pos += len(st.out_shape)
        work, pos = refs[pos:pos + len(scratch)], pos + len(scratch)
        sems = []
        for st in stages:
            sems.append(refs[pos:pos + len(st.sems)])
            pos += len(st.sems)
        if stages:
            first = functools.reduce(jnp.logical_and, [pl.program_id(a) == 0 for a in range(len(grid))])

            @pl.when(first)
            def _():
                for st, a, b, s in zip(stages, st_ins, st_outs, sems):
                    st.start(a, b, s)

        body(*pre, *ins, *outs, *work)
        if stages:
            last = functools.reduce(jnp.logical_and, [pl.program_id(a) == g - 1 for a, g in enumerate(grid)])

            @pl.when(last)
            def _():
                for st, a, b, s in zip(stages, st_ins, st_outs, sems):
                    st.finish(a, b, s)

    all_in = list(in_specs) + [ANY] * len(st_args)
    all_out = list(out_specs) + [ANY] * len(st_shapes)
    kw = dict(has_side_effects=True) if stages else {}
    if vmem is not None:
        kw["vmem_limit_bytes"] = vmem * MIB
    if prefetch is None:
        gkw = dict(grid=grid, in_specs=all_in, out_specs=all_out, scratch_shapes=st_sems)
    else:
        gkw = dict(grid_spec=pltpu.PrefetchScalarGridSpec(
            num_scalar_prefetch=1, grid=grid, in_specs=all_in, out_specs=all_out, scratch_shapes=st_sems))
    res = pl.pallas_call(
        wrapped, name=name, out_shape=_hbm_out(list(out_shape) + st_shapes), input_output_aliases=aliases,
        compiler_params=pltpu.CompilerParams(**kw), **gkw,
    )(*([prefetch] if npre else []), *_hbm(*args, *st_args))
    outs, rest, st_res = list(res[:nout]), list(res[nout:]), []
    for st in stages:
        st_res.append(rest[:len(st.out_shape)])
        rest = rest[len(st.out_shape):]
    return outs, st_res


def _mm(a, b):
    return jnp.dot(a.astype(BF), b.astype(BF), preferred_element_type=F32)


def _mm_nt(a, b):
    return lax.dot_general(a.astype(BF), b.astype(BF), (((1,), (1,)), ((), ())),
                           preferred_element_type=F32)


def _mm_tn(a, b):
    return lax.dot_general(a.astype(BF), b.astype(BF), (((0,), (0,)), ((), ())),
                           preferred_element_type=F32)


def _rows(v):
    return lax.broadcasted_iota(jnp.int32, v.shape, 0)


def _sd(v, s, fill=0.0):
    return jnp.where(_rows(v) >= s, pltpu.roll(v, s, axis=0), fill)


def _su(v, s, fill=0.0):
    n = v.shape[0]
    return jnp.where(_rows(v) < n - s, pltpu.roll(v, n - s, axis=0), fill)


def _sigmoid(z):
    return 1.0 / (1.0 + jnp.exp(-z))


def _softplus(z):
    e = jnp.exp(-jnp.abs(z))
    u = 1.0 + e
    d = u - 1.0
    log1p = jnp.where(d == 0.0, e, jnp.log(u) * (e / jnp.where(d == 0.0, 1.0, d)))
    return jnp.maximum(z, 0.0) + log1p


def _mean(v):
    return jnp.mean(v, axis=-1, keepdims=True)


def _colsum(v):
    return jnp.sum(v, axis=0, keepdims=True)


def _acc(ref, val, first):
    @pl.when(first)
    def _():
        ref[...] = val

    @pl.when(jnp.logical_not(first))
    def _():
        ref[...] += val


def _conv(xp, cw, cb):
    x1, x2, x3 = _sd(xp, 1), _sd(xp, 2), _sd(xp, 3)
    xc = cb + cw[0:1] * x3 + cw[1:2] * x2 + cw[2:3] * x1 + cw[3:4] * xp
    return xc, x1, x2, x3


def _lru_gates(xc, wa, ba, wx, bx, lam):
    xcb = xc.astype(BF)
    r = _sigmoid(_mm(xcb, wa) + ba)
    ii = _sigmoid(_mm(xcb, wx) + bx)
    sp = _softplus(-lam)
    la = (-LRU_C) * r * sp
    a = jnp.exp(la)
    mult = jnp.sqrt(-jnp.tanh(la) * (a * a + 1.0))
    return xcb, r, ii, sp, a, mult


def _gelu_parts(g):
    th = jnp.tanh(GELU_C * (g + 0.044715 * (g * g * g)))
    gel = 0.5 * g * (1.0 + th)
    dgel = 0.5 * (1.0 + th) + 0.5 * g * (1.0 - th * th) * (GELU_C * (1.0 + 3.0 * 0.044715 * (g * g)))
    return gel, dgel


def _tile_scan(a, b, a_s, b_s, out_ref, reverse):
    n, lanes = a.shape
    nt = n // 8
    a, b = a.reshape(nt, 8, lanes), b.reshape(nt, 8, lanes)
    sub = lax.broadcasted_iota(jnp.int32, a.shape, 1)
    s = 1
    while s < 8:
        keep = sub < 8 - s if reverse else sub >= s
        amount = 8 - s if reverse else s
        b = b + a * jnp.where(keep, pltpu.roll(b, amount, axis=1), 0.0)
        a = a * jnp.where(keep, pltpu.roll(a, amount, axis=1), 1.0)
        s *= 2
    a_s[...] = a.reshape(n, lanes)
    b_s[...] = b.reshape(n, lanes)
    edge = pl.ds(0 if reverse else 7, nt, stride=8)
    ta, tb = a_s[edge, :], b_s[edge, :]
    shift = _su if reverse else _sd
    s = 1
    while s < nt:
        tb = tb + ta * shift(tb, s, 0.0)
        if 2 * s < nt:
            ta = ta * shift(ta, s, 1.0)
        s *= 2
    enters = shift(tb, 1, 0.0)
    for o in range(8):
        rows = pl.ds(o, nt, stride=8)
        out_ref[rows, :] = b_s[rows, :] + a_s[rows, :] * enters


def _pool_window(x, steps, shift):
    s, sh = x, 1
    for _ in range(steps):
        s = s + shift(s, sh)
        sh *= 2
    return s


def _fwd_inproj_own(x, g1, w_in, slots, stages=()):
    tm = 1024

    def body(s_ref, x_ref, g_ref, w_ref, proj_ref, h_ref):
        xv = x_ref[...]
        r = lax.rsqrt(_mean(xv * xv) + NORM_EPS)
        h = ((xv * r) * g_ref[...]).astype(BF)
        h_ref[...] = h
        proj_ref[...] = jnp.dot(h, w_ref[0], preferred_element_type=F32)

    return _call(
        body, name="fwd_inproj_own", grid=(T // tm,), prefetch=slots,
        in_specs=[pl.BlockSpec((tm, D), lambda i, s: (i, 0)),
                  pl.BlockSpec((1, D), lambda i, s: (0, 0)),
                  pl.BlockSpec((1, D, CW_IN), lambda i, s: (s[0], 0, 0))],
        out_specs=[pl.BlockSpec((tm, CW_IN), lambda i, s: (i, s[0])),
                   pl.BlockSpec((tm, D), lambda i, s: (i, 0))],
        out_shape=[jax.ShapeDtypeStruct((T, DIN), F32), jax.ShapeDtypeStruct((T, D), BF)],
        vmem=40, args=[x, g1, w_in], stages=stages)[0]


def _fwd_inproj_rest(h1, w_in, proj, slots):
    tm = 1024

    def body(s_ref, h_ref, w_ref, p_in, proj_ref):
        proj_ref[...] = jnp.dot(h_ref[...], w_ref[0], preferred_element_type=F32)

    res = pl.pallas_call(
        body, name="fwd_inproj_rest",
        grid_spec=pltpu.PrefetchScalarGridSpec(
            num_scalar_prefetch=1, grid=(NCHIP - 1, T // tm),
            in_specs=[pl.BlockSpec((tm, D), lambda k, i, s: (i, 0)),
                      pl.BlockSpec((1, D, CW_IN), lambda k, i, s: (s[1 + k], 0, 0)), ANY],
            out_specs=pl.BlockSpec((tm, CW_IN), lambda k, i, s: (i, s[1 + k]))),
        out_shape=pltpu.HBM((T, DIN), F32), input_output_aliases={3: 0},
        compiler_params=_cp(40),
    )(slots, *_hbm(h1, w_in, proj))
    return res


def _vec_spec():
    return pl.BlockSpec((1, CB), lambda j: (0, j))


def _fwd_lru(proj, conv_w, conv_b, wa, ba, wx, bx, lam, stages=()):
    def body(xp_ref, g_ref, cw_ref, cb_ref, wa_ref, ba_ref, wx_ref, bx_ref, lam_ref, y_ref, h_ref, a_s, b_s):
        xc, _, _, _ = _conv(xp_ref[...], cw_ref[...], cb_ref[...])
        _, _, ii, _, a, mult = _lru_gates(xc, wa_ref[0], ba_ref[...], wx_ref[0], bx_ref[...], lam_ref[...])
        _tile_scan(a, mult * (ii * xc), a_s, b_s, h_ref, reverse=False)
        gel, _ = _gelu_parts(g_ref[...])
        y_ref[...] = (h_ref[...] * gel).astype(BF)

    return _call(
        body, name="fwd_lru", grid=(NG,),
        in_specs=[pl.BlockSpec((T, CB), lambda j: (0, j)),
                  pl.BlockSpec((T, CB), lambda j: (0, NG + j)),
                  pl.BlockSpec((4, CB), lambda j: (0, j)),
                  _vec_spec(),
                  pl.BlockSpec((1, CB, CB), lambda j: (j, 0, 0)), _vec_spec(),
                  pl.BlockSpec((1, CB, CB), lambda j: (j, 0, 0)), _vec_spec(),
                  _vec_spec()],
        out_specs=[pl.BlockSpec((T, CB), lambda j: (0, j)), pl.BlockSpec((T, CB), lambda j: (0, j))],
        out_shape=[jax.ShapeDtypeStruct((T, DR), BF), jax.ShapeDtypeStruct((T, DR), F32)],
        vmem=48, args=[proj, proj, conv_w, conv_b, wa, ba, wx, bx, lam], stages=stages,
        scratch=[pltpu.VMEM((T, CB), F32)] * 2)


def _pool_cnt(w):
    t = lax.broadcasted_iota(jnp.int32, (T, 1), 0)
    return jnp.minimum(t + 1, w).astype(F32)


def _fwd_pool(proj, pool_w, pool_scale):
    def body(xp_ref, pw_ref, sc_ref, y_ref):
        for g, w in enumerate(POOL_WINDOWS):
            cols = slice(g * PG, (g + 1) * PG)
            x = xp_ref[:, cols]
            p = _pool_window(x, g + 1, _sd) / _pool_cnt(w) - x
            y_ref[:, cols] = (_mm(p, pw_ref[g]) * sc_ref[:, cols]).astype(BF)

    return pl.pallas_call(
        body, name="fwd_pool", grid=(1,),
        in_specs=[pl.BlockSpec((T, DP), lambda i: (0, 2 * DR // DP)),
                  pl.BlockSpec((4, PG, PG), lambda i: (0, 0, 0)),
                  pl.BlockSpec((1, DP), lambda i: (0, 0))],
        out_specs=pl.BlockSpec((T, DP), lambda i: (0, 0)),
        out_shape=pltpu.HBM((T, DP), BF),
        compiler_params=_cp(48),
    )(*_hbm(proj, pool_w, pool_scale))


GATE_BLK = 512
GATE_BLK0 = (2 * DR + DP) // GATE_BLK


def _gate_specs(tm):
    return [pl.BlockSpec((tm, GATE_BLK), functools.partial(lambda i, q: (i, GATE_BLK0 + q), q=q))
            for q in range(4)]


def _fwd_merge(x, ylru, ypool, proj, b_gate, g2, g3, w_lru_up, w_pool_up, w_o, stages=()):
    tm = 512

    def body(x_ref, yl_ref, yp_ref, p0, p1, p2, p3, bg_ref, g2_ref, g3_ref, wl_ref, wp_ref, wo_ref,
             x2_ref, h2_ref, m_ref, mrg_ref, bra_ref, brb_ref):
        bra = jnp.dot(yl_ref[...], wl_ref[...], preferred_element_type=F32)
        yp = yp_ref[...]
        brb = jnp.concatenate([jnp.dot(yp, wp_ref[k], preferred_element_type=F32) for k in range(NCHIP)], axis=1)
        bg = bg_ref[...]
        ga = _sigmoid(jnp.concatenate([p0[...], p1[...]], axis=1) + bg[:, :D])
        gb = _sigmoid(jnp.concatenate([p2[...], p3[...]], axis=1) + bg[:, D:])
        mrg = (ga * bra + gb * brb).astype(BF)
        m = jnp.dot(mrg, wo_ref[...], preferred_element_type=F32)
        r2 = lax.rsqrt(_mean(m * m) + NORM_EPS)
        x2 = x_ref[...] + (m * r2) * g2_ref[...]
        r3 = lax.rsqrt(_mean(x2 * x2) + NORM_EPS)
        x2_ref[...] = x2
        h2_ref[...] = ((x2 * r3) * g3_ref[...]).astype(BF)
        m_ref[...] = m
        mrg_ref[...] = mrg
        bra_ref[...] = bra.astype(BF)
        brb_ref[...] = brb.astype(BF)

    row = lambda w: pl.BlockSpec((tm, w), lambda i: (i, 0))
    full2 = lambda a, b: pl.BlockSpec((a, b), lambda i: (0, 0))
    return _call(
        body, name="fwd_merge", grid=(T // tm,),
        in_specs=[row(D), row(DR), row(DP)] + _gate_specs(tm) +
                 [full2(1, 2 * D), full2(1, D), full2(1, D), full2(DR, D),
                  pl.BlockSpec((NCHIP, DP, D // NCHIP), lambda i: (0, 0, 0)), full2(D, D)],
        out_specs=[row(D)] * 6,
        out_shape=[jax.ShapeDtypeStruct((T, D), F32), jax.ShapeDtypeStruct((T, D), BF),
                   jax.ShapeDtypeStruct((T, D), F32), jax.ShapeDtypeStruct((T, D), BF),
                   jax.ShapeDtypeStruct((T, D), BF), jax.ShapeDtypeStruct((T, D), BF)],
        vmem=48, args=[x, ylru, ypool, proj, proj, proj, proj, b_gate, g2, g3, w_lru_up, w_pool_up, w_o],
        stages=stages)


def _fwd_mlp_loss(h2, w_ff1, w_ff2, x2, target, g4):
    tm = 512
    fk = DF // NCHIP

    def body(h_ref, w1_ref, w2_ref, x2_ref, t_ref, g_ref, a1_ref, loss_ref, dy_ref, df_ref, dg_ref):
        first = pl.program_id(0) == 0
        h = h_ref[...]
        f = None
        for k in range(NCHIP):
            a1 = jnp.maximum(jnp.dot(h, w1_ref[k], preferred_element_type=F32), 0.0)
            a1_ref[:, k * fk:(k + 1) * fk] = a1.astype(BF)
            part = jnp.dot((a1 * a1).astype(BF), w2_ref[k * fk:(k + 1) * fk, :], preferred_element_type=F32)
            f = part if f is None else f + part
        g4v = g_ref[...]
        r4 = lax.rsqrt(_mean(f * f) + NORM_EPS)
        fn = f * r4
        e = (x2_ref[...] + fn * g4v) - t_ref[...]
        _acc(loss_ref, jnp.sum(_mean(e * e), axis=0, keepdims=True), first)
        dy = e * (1.0 / D)
        dy_ref[...] = dy
        _acc(dg_ref, _colsum(dy * fn), first)
        dfn = dy * g4v
        df_ref[...] = (r4 * (dfn - fn * _mean(dfn * fn))).astype(BF)

    row = pl.BlockSpec((tm, D), lambda i: (i, 0))
    return pl.pallas_call(
        body, name="fwd_mlp_loss", grid=(T // tm,),
        in_specs=[row, pl.BlockSpec((NCHIP, D, fk), lambda i: (0, 0, 0)), pl.BlockSpec((DF, D), lambda i: (0, 0)),
                  row, row, pl.BlockSpec((1, D), lambda i: (0, 0))],
        out_specs=[pl.BlockSpec((tm, DF), lambda i: (i, 0)), pl.BlockSpec((1, 1), lambda i: (0, 0)), row, row,
                   pl.BlockSpec((1, D), lambda i: (0, 0))],
        out_shape=_hbm_out([jax.ShapeDtypeStruct((T, DF), BF), jax.ShapeDtypeStruct((1, 1), F32),
                            jax.ShapeDtypeStruct((T, D), F32), jax.ShapeDtypeStruct((T, D), BF),
                            jax.ShapeDtypeStruct((1, D), F32)]),
        compiler_params=_cp(56),
    )(*_hbm(h2, w_ff1, w_ff2, x2, target, g4))


def _bwd_mlp_x(df, a1, w_ff1, w_ff2):
    tm = 512
    fk = DF // NCHIP

    def body(df_ref, a1_ref, w1_ref, w2_ref, dh_ref, df1_ref):
        df = df_ref[...]
        dh = None
        for k in range(NCHIP):
            cols = slice(k * fk, (k + 1) * fk)
            dact = _mm_nt(df, w2_ref[cols, :])
            df1 = (dact * (2.0 * a1_ref[:, cols].astype(F32))).astype(BF)
            df1_ref[:, cols] = df1
            part = _mm_nt(df1, w1_ref[k])
            dh = part if dh is None else dh + part
        dh_ref[...] = dh

    return pl.pallas_call(
        body, name="bwd_mlp_x", grid=(T // tm,),
        in_specs=[pl.BlockSpec((tm, D), lambda i: (i, 0)),
                  pl.BlockSpec((tm, DF), lambda i: (i, 0)),
                  pl.BlockSpec((NCHIP, D, fk), lambda i: (0, 0, 0)),
                  pl.BlockSpec((DF, D), lambda i: (0, 0))],
        out_specs=[pl.BlockSpec((tm, D), lambda i: (i, 0)), pl.BlockSpec((tm, DF), lambda i: (i, 0))],
        out_shape=_hbm_out([jax.ShapeDtypeStruct((T, D), F32), jax.ShapeDtypeStruct((T, DF), BF)]),
        compiler_params=_cp(56),
    )(*_hbm(df, a1, w_ff1, w_ff2))


def _bwd_mlp_w(df, h2, a1, df1):
    fc = 512
    per = (DF // NCHIP) // fc

    def body(df_ref, h_ref, a1_ref, df1_ref, dw1_ref, dw2_ref):
        a1 = a1_ref[...].astype(F32)
        dw2_ref[...] = _mm_tn((a1 * a1).astype(BF), df_ref[...]).astype(BF)
        dw1_ref[0] = _mm_tn(h_ref[...], df1_ref[...]).astype(BF)

    return pl.pallas_call(
        body, name="bwd_mlp_w", grid=(DF // fc,),
        in_specs=[pl.BlockSpec((T, D), lambda j: (0, 0)),
                  pl.BlockSpec((T, D), lambda j: (0, 0)),
                  pl.BlockSpec((T, fc), lambda j: (0, j)),
                  pl.BlockSpec((T, fc), lambda j: (0, j))],
        out_specs=[pl.BlockSpec((1, D, fc), lambda j: (j // per, 0, j % per)),
                   pl.BlockSpec((fc, D), lambda j: (j, 0))],
        out_shape=_hbm_out([jax.ShapeDtypeStruct((NCHIP, D, DF // NCHIP), BF),
                            jax.ShapeDtypeStruct((DF, D), BF)]),
        compiler_params=_cp(56),
    )(*_hbm(df, h2, a1, df1))


def _bwd_merge(dh2, dy, x2, m, bra, brb, proj, b_gate, g2, g3, w_lru_up, w_pool_up, w_o, stages=()):
    tm = 256
    cpu = D // NCHIP

    def body(dh2_ref, dy_ref, x2_ref, m_ref, bra_ref, brb_ref, p0, p1, p2, p3, bg_ref,
             g2_ref, g3_ref, wl_ref, wp_ref, wo_ref,
             dx_ref, dgt_ref, dyl_ref, dyp_ref, dm_ref, dbra_ref, dbrb_ref, dg2_ref, dg3_ref, dbg_ref):
        first = pl.program_id(0) == 0
        x2 = x2_ref[...]
        r3 = lax.rsqrt(_mean(x2 * x2) + NORM_EPS)
        x2n = x2 * r3
        dh2 = dh2_ref[...]
        t3 = dh2 * g3_ref[...]
        dx2 = dy_ref[...] + r3 * (t3 - x2n * _mean(t3 * x2n))
        dx_ref[...] = dx2
        _acc(dg3_ref, _colsum(dh2 * x2n), first)
        m = m_ref[...]
        r2 = lax.rsqrt(_mean(m * m) + NORM_EPS)
        mn = m * r2
        _acc(dg2_ref, _colsum(dx2 * mn), first)
        dmn = dx2 * g2_ref[...]
        dm = (r2 * (dmn - mn * _mean(dmn * mn))).astype(BF)
        dm_ref[...] = dm
        dmrg = _mm_nt(dm, wo_ref[...])
        bg = bg_ref[...]
        ga = _sigmoid(jnp.concatenate([p0[...], p1[...]], axis=1) + bg[:, :D])
        gb = _sigmoid(jnp.concatenate([p2[...], p3[...]], axis=1) + bg[:, D:])
        dga = dmrg * bra_ref[...].astype(F32) * (ga * (1.0 - ga))
        dgb = dmrg * brb_ref[...].astype(F32) * (gb * (1.0 - gb))
        dgt_ref[:, :D] = dga.astype(BF)
        dgt_ref[:, D:] = dgb.astype(BF)
        _acc(dbg_ref, jnp.concatenate([_colsum(dga), _colsum(dgb)], axis=1), first)
        dbra = (dmrg * ga).astype(BF)
        dbrb = (dmrg * gb).astype(BF)
        dbra_ref[...] = dbra
        dbrb_ref[...] = dbrb
        dyl_ref[...] = _mm_nt(dbra, wl_ref[...])
        dyp = None
        for k in range(NCHIP):
            part = _mm_nt(dbrb[:, k * cpu:(k + 1) * cpu], wp_ref[k])
            dyp = part if dyp is None else dyp + part
        dyp_ref[...] = dyp

    row = lambda w: pl.BlockSpec((tm, w), lambda i: (i, 0))
    full2 = lambda a, b: pl.BlockSpec((a, b), lambda i: (0, 0))
    wp_spec = pl.BlockSpec((NCHIP, DP, cpu), lambda i: (0, 0, 0))
    return _call(
        body, name="bwd_merge", grid=(T // tm,),
        in_specs=[row(D)] * 6 + _gate_specs(tm) +
                 [full2(1, 2 * D), full2(1, D), full2(1, D), full2(DR, D), wp_spec, full2(D, D)],
        out_specs=[row(D), row(2 * D), row(DR), row(DP), row(D), row(D), row(D),
                   full2(1, D), full2(1, D), full2(1, 2 * D)],
        out_shape=[jax.ShapeDtypeStruct((T, D), F32), jax.ShapeDtypeStruct((T, 2 * D), BF),
                   jax.ShapeDtypeStruct((T, DR), F32), jax.ShapeDtypeStruct((T, DP), F32),
                   jax.ShapeDtypeStruct((T, D), BF), jax.ShapeDtypeStruct((T, D), BF),
                   jax.ShapeDtypeStruct((T, D), BF),
                   jax.ShapeDtypeStruct((1, D), F32), jax.ShapeDtypeStruct((1, D), F32),
                   jax.ShapeDtypeStruct((1, 2 * D), F32)],
        vmem=56, args=[dh2, dy, x2, m, bra, brb, proj, proj, proj, proj, b_gate, g2, g3, w_lru_up, w_pool_up, w_o],
        stages=stages)


def _dw_merge(mrg, dm, ylru, dbra, ypool, dbrb, stages=()):
    nb = NCHIP
    rb, pb, cpu = D // nb, DP // nb, D // NCHIP

    def body(mrg_ref, dm_ref, yl_ref, dbra_ref, yp_ref, dbrb_ref, dwo_ref, dwl_ref, dwp_ref):
        dwo_ref[...] = _mm_tn(mrg_ref[...], dm_ref[...]).astype(BF)
        dwl_ref[...] = _mm_tn(yl_ref[...], dbra_ref[...]).astype(BF)
        dwp = _mm_tn(yp_ref[...], dbrb_ref[...]).astype(BF)
        for k in range(NCHIP):
            dwp_ref[k] = dwp[:, k * cpu:(k + 1) * cpu]

    cols = lambda w: pl.BlockSpec((T, w), lambda r: (0, r))
    whole = pl.BlockSpec((T, D), lambda r: (0, 0))
    return _call(
        body, name="dw_merge", grid=(nb,),
        in_specs=[cols(rb), whole, cols(rb), whole, cols(pb), whole],
        out_specs=[pl.BlockSpec((rb, D), lambda r: (r, 0)), pl.BlockSpec((rb, D), lambda r: (r, 0)),
                   pl.BlockSpec((NCHIP, pb, cpu), lambda r: (0, r, 0))],
        out_shape=[jax.ShapeDtypeStruct((D, D), BF), jax.ShapeDtypeStruct((DR, D), BF),
                   jax.ShapeDtypeStruct((NCHIP, DP, cpu), BF)],
        vmem=56, args=[mrg, dm, ylru, dbra, ypool, dbrb], stages=stages)


def _bwd_lru(proj, h, dylru, conv_w, conv_b, wa, ba, wx, bx, lam, stages=()):
    def body(xp_ref, g_ref, h_ref, dy_ref, cw_ref, cb_ref, wa_ref, ba_ref, wx_ref, bx_ref, lam_ref,
             dxp_ref, dg_ref, dcw_ref, dcb_ref, dwa_ref, dba_ref, dwx_ref, dbx_ref, dlam_ref, a_s, b_s, l_s):
        xp = xp_ref[...]
        cw = cw_ref[...]
        lam = lam_ref[...]
        xc, x1, x2, x3 = _conv(xp, cw, cb_ref[...])
        wa, wx = wa_ref[0], wx_ref[0]
        xcb, r, ii, sp, a, mult = _lru_gates(xc, wa, ba_ref[...], wx, bx_ref[...], lam)
        g = g_ref[...]
        gel, dgel = _gelu_parts(g)
        h = h_ref[...]
        dy = dy_ref[...]
        dg_ref[...] = (dy * h * dgel).astype(BF)
        _tile_scan(_su(a, 1, 0.0), dy * gel, a_s, b_s, l_s, reverse=True)
        b = l_s[...]
        da = b * _sd(h, 1, 0.0)
        dmult = b * (ii * xc)
        dii = b * (mult * xc)
        dxc = b * (mult * ii)
        dla = da * a - dmult * ((a * a) / mult)
        dr = dla * ((-LRU_C) * sp)
        dsp = _colsum(dla * ((-LRU_C) * r))
        dlam_ref[...] = -dsp / (1.0 + jnp.exp(lam))
        dzr = dr * (r * (1.0 - r))
        dzi = dii * (ii * (1.0 - ii))
        dzrb, dzib = dzr.astype(BF), dzi.astype(BF)
        dxc = dxc + _mm_nt(dzrb, wa) + _mm_nt(dzib, wx)
        dwa_ref[0] = _mm_tn(xcb, dzrb)
        dwx_ref[0] = _mm_tn(xcb, dzib)
        dba_ref[...] = _colsum(dzr)
        dbx_ref[...] = _colsum(dzi)
        dcb_ref[...] = _colsum(dxc)
        dcw_ref[...] = jnp.concatenate([_colsum(dxc * x3), _colsum(dxc * x2), _colsum(dxc * x1),
                                        _colsum(dxc * xp)], axis=0)
        dxp = cw[3:4] * dxc + cw[2:3] * _su(dxc, 1) + cw[1:2] * _su(dxc, 2) + cw[0:1] * _su(dxc, 3)
        dxp_ref[...] = dxp.astype(BF)

    blk = pl.BlockSpec((T, CB), lambda j: (0, j))
    wsp = pl.BlockSpec((1, CB, CB), lambda j: (j, 0, 0))
    return _call(
        body, name="bwd_lru", grid=(NG,),
        in_specs=[blk, pl.BlockSpec((T, CB), lambda j: (0, NG + j)), blk, blk,
                  pl.BlockSpec((4, CB), lambda j: (0, j)), _vec_spec(), wsp, _vec_spec(), wsp, _vec_spec(),
                  _vec_spec()],
        out_specs=[blk, blk, pl.BlockSpec((4, CB), lambda j: (0, j)), _vec_spec(), wsp, _vec_spec(), wsp,
                   _vec_spec(), _vec_spec()],
        out_shape=[jax.ShapeDtypeStruct((T, DR), BF), jax.ShapeDtypeStruct((T, DR), BF),
                   jax.ShapeDtypeStruct((4, DR), F32), jax.ShapeDtypeStruct((1, DR), F32),
                   jax.ShapeDtypeStruct((NG, CB, CB), F32), jax.ShapeDtypeStruct((1, DR), F32),
                   jax.ShapeDtypeStruct((NG, CB, CB), F32), jax.ShapeDtypeStruct((1, DR), F32),
                   jax.ShapeDtypeStruct((1, DR), F32)],
        vmem=56, args=[proj, proj, h, dylru, conv_w, conv_b, wa, ba, wx, bx, lam], stages=stages,
        scratch=[pltpu.VMEM((T, CB), F32)] * 3)


def _bwd_pool(proj, dypool, pool_w, pool_scale):
    def body(xp_ref, dy_ref, pw_ref, sc_ref, dx_ref, dw_ref, dsc_ref):
        for g, w in enumerate(POOL_WINDOWS):
            cols = slice(g * PG, (g + 1) * PG)
            cnt = _pool_cnt(w)
            x = xp_ref[:, cols]
            pb = (_pool_window(x, g + 1, _sd) / cnt - x).astype(BF)
            wg = pw_ref[g]
            dy = dy_ref[:, cols]
            dsc_ref[:, cols] = _colsum(dy * _mm(pb, wg))
            dyp = (dy * sc_ref[:, cols]).astype(BF)
            dw_ref[g] = _mm_tn(pb, dyp)
            dp = _mm_nt(dyp, wg)
            dx_ref[:, cols] = (_pool_window(dp / cnt, g + 1, _su) - dp).astype(BF)

    return pl.pallas_call(
        body, name="bwd_pool", grid=(1,),
        in_specs=[pl.BlockSpec((T, DP), lambda i: (0, 2 * DR // DP)),
                  pl.BlockSpec((T, DP), lambda i: (0, 0)),
                  pl.BlockSpec((4, PG, PG), lambda i: (0, 0, 0)),
                  pl.BlockSpec((1, DP), lambda i: (0, 0))],
        out_specs=[pl.BlockSpec((T, DP), lambda i: (0, 0)),
                   pl.BlockSpec((4, PG, PG), lambda i: (0, 0, 0)),
                   pl.BlockSpec((1, DP), lambda i: (0, 0))],
        out_shape=_hbm_out([jax.ShapeDtypeStruct((T, DP), BF), jax.ShapeDtypeStruct((4, PG, PG), F32),
                            jax.ShapeDtypeStruct((1, DP), F32)]),
        compiler_params=_cp(48),
    )(*_hbm(proj, dypool, pool_w, pool_scale))


PART_COLS = (DR, DR, DP, 2 * D)


def _shard_pieces():
    starts = [sum(PART_COLS[:p]) for p in range(len(PART_COLS))]
    shards = []
    for k in range(NCHIP):
        lo, hi = k * CW_IN, (k + 1) * CW_IN
        shards.append([(p, max(lo, s) - s, min(hi, s + wd) - s, max(lo, s) - lo)
                       for p, (s, wd) in enumerate(zip(starts, PART_COLS)) if max(lo, s) < min(hi, s + wd)])
    return shards


def _bwd_inproj_w(h1, parts, after):
    def body(h_ref, p0, p1, p2, p3, after_ref, dw_ref):
        part_refs = (p0, p1, p2, p3)
        for k, pieces in enumerate(_shard_pieces()):
            for p, a, b, c0 in pieces:
                dw_ref[k, :, c0:c0 + b - a] = _mm_tn(h_ref[...], part_refs[p][:, a:b]).astype(BF)

    vmem = pl.BlockSpec(memory_space=pltpu.VMEM)
    return pl.pallas_call(
        body, name="bwd_inproj_w", in_specs=[vmem] * 5 + [ANY], out_specs=vmem,
        out_shape=pltpu.HBM((NCHIP, D, CW_IN), BF), compiler_params=_cp(48),
    )(*_hbm(h1, *parts), after)


def _bwd_inproj_x(parts, w_in, x, dxres, g1, stages=()):
    tm = 512

    def body(p0, p1, p2, p3, w_ref, x_ref, dr_ref, g_ref, dx_ref, dg_ref):
        part_refs = (p0, p1, p2, p3)
        dh = None
        for k, pieces in enumerate(_shard_pieces()):
            for p, a, b, c0 in pieces:
                part = _mm_nt(part_refs[p][:, a:b], w_ref[k, :, c0:c0 + b - a])
                dh = part if dh is None else dh + part
        xv = x_ref[...]
        r = lax.rsqrt(_mean(xv * xv) + NORM_EPS)
        xn = xv * r
        t = dh * g_ref[...]
        dx_ref[...] = dr_ref[...] + r * (t - xn * _mean(t * xn))
        _acc(dg_ref, _colsum(dh * xn), pl.program_id(0) == 0)

    row = pl.BlockSpec((tm, D), lambda i: (i, 0))
    vec = pl.BlockSpec((1, D), lambda i: (0, 0))
    return _call(
        body, name="bwd_inproj_x", grid=(T // tm,),
        in_specs=[pl.BlockSpec((tm, wd), lambda i: (i, 0)) for wd in PART_COLS] +
                 [pl.BlockSpec((NCHIP, D, CW_IN), lambda i: (0, 0, 0)), row, row, vec],
        out_specs=[row, vec],
        out_shape=[jax.ShapeDtypeStruct((T, D), F32), jax.ShapeDtypeStruct((1, D), F32)],
        vmem=56, args=[*parts, w_in, x, dxres, g1], stages=stages)[0]


def _place():
    x, y, c = lax.axis_index("x"), lax.axis_index("y"), lax.axis_index("c")
    chips = [(1 - x, y), (x, 1 - y), (1 - x, 1 - y)]
    return x, y, c, chips


def _rcopy(src, dst, ssem, rsem, dev):
    return pltpu.make_async_remote_copy(src_ref=src, dst_ref=dst, send_sem=ssem, recv_sem=rsem,
                                        device_id=dev, device_id_type=MESH_ID)


def _sds(a):
    return jax.ShapeDtypeStruct(a.shape, a.dtype)


def _sem2(n, m):
    return [pltpu.SemaphoreType.DMA((n * m,)), pltpu.SemaphoreType.DMA((n * m,))]


ALL = (0, 1, 1)


def _piece(ref, k, half, part):
    hr = ref.shape[1] // 2
    r0, r1 = hr * part[0] // part[2], hr * part[1] // part[2]
    return ref.at[k, pl.ds(half * hr + r0, r1 - r0), :]


def _gather(fulls, ici=(), d2d=()):
    n = len(fulls)
    ici, d2d = list(ici), list(d2d)
    pieces = [("ici", i, part) for i, part in ici] + [("d2d", i, part) for i, part in d2d]

    def copies(outs, sems):
        x, y, c, chips = _place()
        me = 2 * x + y
        sib = (x, y, 1 - c)
        send, recv = [], []
        for q, (kind, i, part) in enumerate(pieces):
            for j, chip in enumerate(chips):
                k, s = 2 * chip[0] + chip[1], 3 * q + j
                if kind == "ici":
                    mine, theirs, dev = _piece(outs[i], me, c, part), _piece(outs[i], k, c, part), (*chip, c)
                else:
                    mine, theirs, dev = _piece(outs[i], k, c, part), _piece(outs[i], k, 1 - c, part), sib
                send.append(_rcopy(mine, mine, sems[0].at[s], sems[1].at[s], dev))
                recv.append(_rcopy(theirs, theirs, sems[0].at[s], sems[1].at[s], dev))
        return send, recv

    def start(ins, outs, sems):
        for cp in copies(outs, sems)[0]:
            cp.start()

    def finish(ins, outs, sems):
        send, recv = copies(outs, sems)
        for cp in recv:
            cp.wait_recv()
        for cp in send:
            cp.wait_send()

    sems = [pltpu.SemaphoreType.DMA((3 * len(pieces),)), pltpu.SemaphoreType.DMA((3 * len(pieces),))]
    return _Stage(fulls, [_sds(f) for f in fulls], {i: i for i in range(n)}, sems, start, finish)


def _gather_whole(v):
    def copies(ins, outs, sems):
        x, y, c, chips = _place()
        me = 2 * x + y
        send = [_rcopy(ins[0], outs[0].at[me], sems[0].at[j], sems[1].at[j], (*chip, c))
                for j, chip in enumerate(chips)]
        recv = [_rcopy(ins[0], outs[0].at[2 * chip[0] + chip[1]], sems[0].at[j], sems[1].at[j], (*chip, c))
                for j, chip in enumerate(chips)]
        return send, recv

    def start(ins, outs, sems):
        for cp in copies(ins, outs, sems)[0]:
            cp.start()

    def finish(ins, outs, sems):
        send, recv = copies(ins, outs, sems)
        for cp in recv:
            cp.wait_recv()
        for cp in send:
            cp.wait_send()

    return _Stage([v], [jax.ShapeDtypeStruct((NCHIP,) + v.shape, v.dtype)], {},
                  [pltpu.SemaphoreType.DMA((3,)), pltpu.SemaphoreType.DMA((3,))], start, finish)


def _to_sibling(srcs):
    n = len(srcs)

    def copies(ins, outs, sems):
        x, y, c, _ = _place()
        sib = (x, y, 1 - c)
        return [_rcopy(ins[i].at[:, 1 - c] if srcs[i].ndim == 4 else ins[i], outs[i], sems[0].at[i], sems[1].at[i], sib)
                for i in range(n)]

    def start(ins, outs, sems):
        for cp in copies(ins, outs, sems):
            cp.start()

    def finish(ins, outs, sems):
        for cp in copies(ins, outs, sems):
            cp.wait()

    shapes = [jax.ShapeDtypeStruct((NCHIP,) + s.shape[2:] if s.ndim == 4 else s.shape, s.dtype) for s in srcs]
    return _Stage(srcs, shapes, {}, [pltpu.SemaphoreType.DMA((n,)), pltpu.SemaphoreType.DMA((n,))], start, finish)


def _to_chips(srcs, parts=None, lands=None):
    n = len(srcs)
    parts = [ALL] * n if parts is None else parts
    lands = [None] * n if lands is None else lands
    given = [i for i in range(n) if lands[i] is not None]

    def rows(ref, i):
        hr = srcs[i].shape[1]
        r0, r1 = hr * parts[i][0] // parts[i][2], hr * parts[i][1] // parts[i][2]
        return ref.at[pl.ds(r0, r1 - r0), :]

    def copies(ins, outs, sems):
        x, y, c, chips = _place()
        me = 2 * x + y
        return [_rcopy(rows(ins[i].at[2 * chip[0] + chip[1]] if srcs[i].shape[0] == NCHIP else ins[i].at[c], i),
                       rows(outs[i].at[me], i), sems[0].at[3 * i + j], sems[1].at[3 * i + j], (*chip, c))
                for i in range(n) for j, chip in enumerate(chips)]

    def start(ins, outs, sems):
        for cp in copies(ins, outs, sems):
            cp.start()

    def finish(ins, outs, sems):
        for cp in copies(ins, outs, sems):
            cp.wait()

    shapes = [jax.ShapeDtypeStruct((NCHIP,) + s.shape[1:], s.dtype) for s in srcs]
    alias = {n + q: i for q, i in enumerate(given)}
    return _Stage(list(srcs) + [lands[i] for i in given], shapes, alias, _sem2(n, 3), start, finish)


HBM_REF = pl.BlockSpec(memory_space=pltpu.HBM)
SEM_REF = pl.BlockSpec(memory_space=pltpu.SEMAPHORE)
DATAFLOW = pltpu.SideEffectType.DATAFLOW_SIDE_EFFECTING


def _after(x):
    return _Stage([x], [], {}, [], lambda *a: None, lambda *a: None)


class _Flight:
    def __init__(self, stage, sems, bufs):
        self.stage, self.sems, self.bufs = stage, list(sems), list(bufs)

    def landed(self):
        st, n = self.stage, len(self.stage.operands)
        fresh = [j for j in range(len(st.out_shape)) if j not in st.alias.values()]
        back = {v: k for k, v in st.alias.items()}
        return [self.bufs[back[j]] if j in back else self.bufs[n + fresh.index(j)] for j in range(len(st.out_shape))]


def _split_call(name, finish=(), start=(), after=None):
    bufs, stage_bufs = [], []

    def slot(a):
        for i, b in enumerate(bufs):
            if b is a:
                return i
        bufs.append(a)
        return len(bufs) - 1

    fin_slots = [[slot(b) for b in fl.bufs] for fl in finish]
    for st in start:
        fresh = [lax.empty(o.shape, o.dtype) for j, o in enumerate(st.out_shape) if j not in st.alias.values()]
        stage_bufs.append([slot(a) for a in list(st.operands) + fresh])
    old_sems = [s for fl in finish for s in fl.sems]
    new_sems = [s for st in start for s in st.sems]
    nb, no, nn = len(bufs), len(old_sems), len(new_sems)

    def refs_of(st, slots, buf_refs):
        n = len(st.operands)
        ins = [buf_refs[i] for i in slots[:n]]
        fresh = [j for j in range(len(st.out_shape)) if j not in st.alias.values()]
        back = {v: k for k, v in st.alias.items()}
        outs = [ins[back[j]] if j in back else buf_refs[slots[n + fresh.index(j)]] for j in range(len(st.out_shape))]
        return ins, outs

    def body(*refs):
        buf_refs, sem_in = refs[:nb], refs[nb:nb + no]
        sem_out = refs[nb + no + (after is not None):][:nn]
        token = refs[-1]
        pos = 0
        for fl, slots in zip(finish, fin_slots):
            ins, outs = refs_of(fl.stage, slots, buf_refs)
            fl.stage.finish(ins, outs, sem_in[pos:pos + len(fl.sems)])
            pos += len(fl.sems)
        pos = 0
        for st, slots in zip(start, stage_bufs):
            ins, outs = refs_of(st, slots, buf_refs)
            st.start(ins, outs, sem_out[pos:pos + len(st.sems)])
            pos += len(st.sems)
        token[...] = jnp.zeros_like(token)

    res = pl.pallas_call(
        body, name=name,
        out_shape=tuple(new_sems) + tuple(pltpu.HBM(b.shape, b.dtype) for b in bufs) +
                  (jax.ShapeDtypeStruct((8, LANE), F32),),
        in_specs=(HBM_REF,) * nb + (SEM_REF,) * no + ((pl.BlockSpec(memory_space=pl.ANY),) if after is not None else ()),
        out_specs=(SEM_REF,) * nn + (HBM_REF,) * nb + (pl.BlockSpec(memory_space=pltpu.VMEM),),
        input_output_aliases={i: nn + i for i in range(nb)},
        compiler_params=pltpu.CompilerParams(has_side_effects=DATAFLOW),
    )(*_hbm(*bufs), *old_sems, *([after] if after is not None else []))
    sems, thru, token = res[:nn], res[nn:nn + nb], res[-1]
    for fl, slots in zip(finish, fin_slots):
        fl.bufs = [thru[i] for i in slots]
    flights, pos = [], 0
    for st, slots in zip(start, stage_bufs):
        flights.append(_Flight(st, sems[pos:pos + len(st.sems)], [thru[i] for i in slots]))
        pos += len(st.sems)
    return flights, token


def _share(pairs):
    n = len(pairs)

    def start(ins, outs, sems):
        x, y, c, _ = _place()
        for i in range(n):
            _rcopy(outs[i].at[c], outs[i].at[c], sems[0].at[i], sems[1].at[i], (x, y, 1 - c)).start()

    def finish(ins, outs, sems):
        x, y, c, _ = _place()
        for i in range(n):
            _rcopy(outs[i].at[c], outs[i].at[c], sems[0].at[i], sems[1].at[i], (x, y, 1 - c)).wait_send()
            _rcopy(outs[i].at[1 - c], outs[i].at[1 - c], sems[0].at[i], sems[1].at[i], (x, y, 1 - c)).wait_recv()

    return _Stage(pairs, [_sds(p) for p in pairs], {i: i for i in range(n)},
                  [pltpu.SemaphoreType.DMA((n,)), pltpu.SemaphoreType.DMA((n,))], start, finish)


def _row_block(rows, cols, itemsize=4, target=2 * MIB):
    br = rows
    while br * cols * itemsize > target and br % 32 == 0:
        br //= 2
    return br


def _cast_place(w, chip_idx, name):
    rows, cols = w.shape
    br = _row_block(rows, cols)

    def body(k_ref, w_ref, o_ref):
        o_ref[0] = w_ref[...].astype(BF)

    return _call(
        body, name=name, grid=(rows // br,), prefetch=chip_idx,
        in_specs=[pl.BlockSpec((br, cols), lambda r, k: (r, 0))],
        out_specs=[pl.BlockSpec((1, br, cols), lambda r, k: (k[0], r, 0))],
        out_shape=[jax.ShapeDtypeStruct((NCHIP, rows, cols), BF)], vmem=32, args=[w])[0][0]


def _cast_place_multi(ws, chip_idx, stages=()):
    br = 128
    nblk = [a.shape[0] // br for a in ws]
    starts = [sum(nblk[:i]) for i in range(len(ws))]

    def body(k_ref, *refs):
        r = pl.program_id(0)
        for i in range(len(ws)):
            @pl.when(jnp.logical_and(r >= starts[i], r < starts[i] + nblk[i]))
            def _(i=i):
                refs[len(ws) + i][0] = refs[i][...].astype(BF)

    def at(i):
        return functools.partial(lambda r, s, nb: jnp.clip(r - s, 0, nb - 1), s=starts[i], nb=nblk[i])

    outs, landed = _call(
        body, name="cast_rest", grid=(sum(nblk),), prefetch=chip_idx,
        in_specs=[pl.BlockSpec((br, a.shape[1]), functools.partial(lambda r, k, f: (f(r), 0), f=at(i)))
                  for i, a in enumerate(ws)],
        out_specs=[pl.BlockSpec((1, br, a.shape[1]), functools.partial(lambda r, k, f: (k[0], f(r), 0), f=at(i)))
                   for i, a in enumerate(ws)],
        out_shape=[jax.ShapeDtypeStruct((NCHIP,) + a.shape, BF) for a in ws], vmem=32, args=list(ws), stages=stages)
    return outs, landed


def _add_sibling(g, land, cidx, name, stages=()):
    _, _, hr, cols = g.shape
    br = _row_block(hr, cols)

    def body(c_ref, g_ref, l_ref, o_ref):
        o_ref[...] = (g_ref[0, 0].astype(F32) + l_ref[0].astype(F32)).astype(BF)[None]

    outs, st = _call(
        body, name=name, grid=(NCHIP, hr // br), prefetch=cidx,
        in_specs=[pl.BlockSpec((1, 1, br, cols), lambda k, r, c: (k, c[0], r, 0)),
                  pl.BlockSpec((1, br, cols), lambda k, r, c: (k, r, 0))],
        out_specs=[pl.BlockSpec((1, br, cols), lambda k, r, c: (k, r, 0))],
        out_shape=[jax.ShapeDtypeStruct((NCHIP, hr, cols), BF)], vmem=32, args=[g, land], stages=stages)
    return outs[0], st


def _add_sibling_multi(gs, lands, cidx, name):
    n = len(gs)
    brs = [_row_block(g.shape[2], g.shape[3]) for g in gs]
    nrb = [g.shape[2] // b for g, b in zip(gs, brs)]
    nblk = [NCHIP * q for q in nrb]
    starts = [sum(nblk[:i]) for i in range(n)]

    def body(c_ref, *refs):
        r = pl.program_id(0)
        for i in range(n):
            g_ref, l_ref, o_ref = refs[2 * i], refs[2 * i + 1], refs[2 * n + i]

            @pl.when(jnp.logical_and(r >= starts[i], r < starts[i] + nblk[i]))
            def _():
                o_ref[...] = (g_ref[0, 0].astype(F32) + l_ref[0].astype(F32)).astype(BF)[None]

    def at(i, r):
        q = jnp.clip(r - starts[i], 0, nblk[i] - 1)
        return q // nrb[i], q % nrb[i]

    def g_spec(i):
        return pl.BlockSpec((1, 1, brs[i], gs[i].shape[3]),
                            functools.partial(lambda r, c, i: (at(i, r)[0], c[0], at(i, r)[1], 0), i=i))

    def l_spec(i):
        return pl.BlockSpec((1, brs[i], gs[i].shape[3]),
                            functools.partial(lambda r, c, i: (at(i, r)[0], at(i, r)[1], 0), i=i))

    return _call(
        body, name=name, grid=(sum(nblk),), prefetch=cidx,
        in_specs=[s for i in range(n) for s in (g_spec(i), l_spec(i))], out_specs=[l_spec(i) for i in range(n)],
        out_shape=[jax.ShapeDtypeStruct(l.shape, BF) for l in lands], vmem=32,
        args=[a for i in range(n) for a in (gs[i], lands[i])])[0]


def _add_pair(a, b, name):
    rows, cols = a.shape

    def body(a_ref, b_ref, o_ref):
        o_ref[...] = a_ref[...] + b_ref[...]

    spec = pl.BlockSpec((rows, cols), lambda r: (0, 0))
    return _call(body, name=name, grid=(1,), in_specs=[spec, spec], out_specs=[spec], out_shape=[_sds(a)],
                 vmem=32, args=[a, b])[0][0]


def _add_chips(own, land, idx, name, stages=None):
    _, hr, cols = land.shape
    br = _row_block(hr, cols)

    def body(s_ref, a_ref, b_ref, c_ref, d_ref, o_ref):
        o_ref[...] = (a_ref[...].astype(F32) + b_ref[...].astype(F32)) + (c_ref[...].astype(F32) +
                                                                           d_ref[...].astype(F32))

    spec = lambda q: pl.BlockSpec((1, br, cols), functools.partial(lambda r, s, q: (s[q], r, 0), q=q))
    outs, landed = _call(
        body, name=name, grid=(hr // br,), prefetch=idx,
        in_specs=[spec(0), spec(1), spec(2), spec(3)], out_specs=[spec(4)],
        out_shape=[jax.ShapeDtypeStruct((2, hr, cols), F32)], vmem=48, args=[own, land, land, land],
        stages=stages or ())
    return outs[0] if stages is None else (outs[0], landed)


def _add_chips_multi(owns, lands, idx, name, stages=()):
    n = len(owns)
    brs = [_row_block(l.shape[1], l.shape[2]) for l in lands]
    nblk = [l.shape[1] // b for l, b in zip(lands, brs)]
    starts = [sum(nblk[:i]) for i in range(n)]

    def body(s_ref, *refs):
        r = pl.program_id(0)
        for i in range(n):
            a_ref, b_ref, c_ref, d_ref = refs[4 * i:4 * i + 4]
            o_ref = refs[4 * n + i]

            @pl.when(jnp.logical_and(r >= starts[i], r < starts[i] + nblk[i]))
            def _():
                o_ref[...] = (a_ref[...].astype(F32) + b_ref[...].astype(F32)) + (c_ref[...].astype(F32) +
                                                                                   d_ref[...].astype(F32))

    def spec(i, q):
        return pl.BlockSpec((1, brs[i], lands[i].shape[2]), functools.partial(
            lambda r, s, q, st, nb: (s[q], jnp.clip(r - st, 0, nb - 1), 0), q=q, st=starts[i], nb=nblk[i]))

    outs, landed = _call(
        body, name=name, grid=(sum(nblk),), prefetch=idx,
        in_specs=[spec(i, q) for i in range(n) for q in range(4)], out_specs=[spec(i, 4) for i in range(n)],
        out_shape=[jax.ShapeDtypeStruct((2,) + l.shape[1:], F32) for l in lands], vmem=48,
        args=[a for i in range(n) for a in (owns[i], lands[i], lands[i], lands[i])], stages=stages)
    return outs, landed


def _adamw_math(w, g, m, v):
    mn = ADAM_B1 * m + (1.0 - ADAM_B1) * g
    vn = ADAM_B2 * v + (1.0 - ADAM_B2) * (g * g)
    m_hat = mn / (1.0 - ADAM_B1 ** ADAM_STEP)
    v_hat = vn / (1.0 - ADAM_B2 ** ADAM_STEP)
    return -ADAM_LR * (m_hat / (jnp.sqrt(v_hat) + ADAM_EPS) + ADAM_WD * w), mn, vn


def _adamw(w, g, m, v, name, stages=()):
    rows, cols = w.shape
    br = _row_block(rows, cols)

    def body(w_ref, g_ref, m_ref, v_ref, go_ref, d_ref, mo_ref, vo_ref):
        gv = g_ref[...]
        go_ref[...] = gv
        d_ref[...], mo_ref[...], vo_ref[...] = _adamw_math(w_ref[...], gv, m_ref[...], v_ref[...])

    spec = pl.BlockSpec((br, cols), lambda r: (r, 0))
    return _call(body, name=name, grid=(rows // br,), in_specs=[spec] * 4, out_specs=[spec] * 4,
                 out_shape=[_sds(w)] * 4, vmem=56, args=[w, g, m, v], stages=stages)


def _adamw_multi(names, w, g, m, v, stages=()):
    cols = w[names[0]].shape[1]
    br = 128
    nblk = [w[n].shape[0] // br for n in names]
    starts = [sum(nblk[:i]) for i in range(len(names))]

    def body(*refs):
        r = pl.program_id(0)
        for i in range(len(names)):
            w_ref, g_ref, m_ref, v_ref = refs[4 * i:4 * i + 4]
            go_ref, d_ref, mo_ref, vo_ref = refs[4 * len(names) + 4 * i:4 * len(names) + 4 * i + 4]

            @pl.when(jnp.logical_and(r >= starts[i], r < starts[i] + nblk[i]))
            def _():
                gv = g_ref[...]
                go_ref[...] = gv
                d_ref[...], mo_ref[...], vo_ref[...] = _adamw_math(w_ref[...], gv, m_ref[...], v_ref[...])

    def spec(i):
        return pl.BlockSpec((br, cols), functools.partial(
            lambda r, s, nb: (jnp.clip(r - s, 0, nb - 1), 0), s=starts[i], nb=nblk[i]))

    outs, landed = _call(
        body, name="adamw_" + "_".join(names), grid=(sum(nblk),),
        in_specs=[spec(i) for i in range(len(names)) for _ in range(4)],
        out_specs=[spec(i) for i in range(len(names)) for _ in range(4)],
        out_shape=[_sds(w[n]) for n in names for _ in range(4)], vmem=56,
        args=[a[n] for n in names for a in (w, g, m, v)], stages=stages)
    return {n: outs[4 * i:4 * i + 4] for i, n in enumerate(names)}, landed


def _to_everyone(v):
    deltas = [(a, b, e) for a in (0, 1) for b in (0, 1) for e in (0, 1)][1:]

    def copies(ins, outs, sems):
        x, y, c, _ = _place()
        me = 4 * x + 2 * y + c
        flip = lambda p, f: 1 - p if f else p
        return [_rcopy(ins[0], outs[0].at[me], sems[0].at[q], sems[1].at[q], (flip(x, a), flip(y, b), flip(c, e)))
                for q, (a, b, e) in enumerate(deltas)]

    def start(ins, outs, sems):
        for cp in copies(ins, outs, sems):
            cp.start()

    def finish(ins, outs, sems):
        for cp in copies(ins, outs, sems):
            cp.wait()

    n = len(deltas)
    return _Stage([v], [jax.ShapeDtypeStruct((2 * NCHIP,) + v.shape, v.dtype)], {},
                  [pltpu.SemaphoreType.DMA((n,)), pltpu.SemaphoreType.DMA((n,))], start, finish)


SMALL_AT = {"norm_mix_pre": (0, 1, D), "norm_mix_post": (1, 1, D), "norm_mlp_pre": (2, 1, D),
            "norm_mlp_post": (3, 1, D), "b_gate": (4, 2, D), "conv_b": (6, 1, D), "lru_b_a": (7, 1, D),
            "lru_b_x": (8, 1, D), "lru_lambda": (9, 1, D), "pool_scale": (10, 1, DP)}
SMALL_SEPARATE = ["conv_w", "lru_w_a", "lru_w_x", "pool_w"]


def _adamw_small(small_sum, first_all, sep_grads, w, m, v):
    packed, sep = list(SMALL_AT), list(SMALL_SEPARATE)
    names = packed + sep

    def body(*refs):
        s_ref, a_ref, refs = refs[0], refs[1], refs[2:]
        g_sep, refs = refs[:len(sep)], refs[len(sep):]
        nn = len(names)
        w_r, m_r, v_r, refs = refs[:nn], refs[nn:2 * nn], refs[2 * nn:3 * nn], refs[3 * nn:]
        g_out, refs = refs[:len(packed)], refs[len(packed):]
        d_o, m_o, v_o = refs[:nn], refs[nn:2 * nn], refs[2 * nn:3 * nn]
        for i, n in enumerate(names):
            if i == 0:
                g = a_ref[0:1, :]
                for q in range(1, 2 * NCHIP):
                    g = g + a_ref[q:q + 1, :]
                g_out[i][...] = g
            elif n in SMALL_AT:
                r0, nr, nc = SMALL_AT[n]
                g = jnp.concatenate([s_ref[r0 + q:r0 + q + 1, :nc] for q in range(nr)], axis=1)
                g_out[i][...] = g
            else:
                g = g_sep[i - len(packed)][...]
            d_o[i][...], m_o[i][...], v_o[i][...] = _adamw_math(w_r[i][...], g, m_r[i][...], v_r[i][...])

    ws = [w[n] for n in names]
    res = pl.pallas_call(
        body, name="adamw_small",
        out_shape=[_sds(w[n]) for n in packed] + [_sds(a) for a in ws] * 3,
        compiler_params=_cp(32),
    )(*_hbm(small_sum, first_all, *sep_grads, *ws, *[m[n] for n in names], *[v[n] for n in names]))
    nn, npk = len(names), len(packed)
    grad = dict(zip(packed, res[:npk]))
    delta = dict(zip(names, res[npk:npk + nn]))
    new_m = dict(zip(names, res[npk + nn:npk + 2 * nn]))
    new_v = dict(zip(names, res[npk + 2 * nn:]))
    return grad, delta, new_m, new_v


W_NAMES = ["norm_mix_pre", "norm_mix_post", "norm_mlp_pre", "norm_mlp_post", "w_in", "b_gate", "conv_w", "conv_b",
           "lru_w_a", "lru_b_a", "lru_w_x", "lru_b_x", "lru_lambda", "pool_w", "pool_scale", "w_lru_up",
           "w_pool_up", "w_o", "w_ff1", "w_ff2"]
BIG = ["w_in", "w_lru_up", "w_pool_up", "w_o", "w_ff1", "w_ff2"]


def _block_diag(w):
    hd = w.shape[-1]
    per = CB // hd
    w4 = w.reshape(NG, per, hd, hd)
    eye = jnp.eye(per, dtype=w.dtype)
    return jnp.einsum("gpij,pq->gpiqj", w4, eye).reshape(NG, CB, CB)


def _block_diag_extract(d, hd):
    per = CB // hd
    d5 = d.reshape(NG, per, hd, per, hd)
    return jnp.stack([d5[:, p, :, p, :] for p in range(per)], axis=1).reshape(NG * per, hd, hd)


def _halves(g):
    return g.reshape(NCHIP, 2, g.size // (g.shape[-1] * 2 * NCHIP), g.shape[-1])


def kernel(x, norm_mix_pre, norm_mix_post, norm_mlp_pre, norm_mlp_post, w_in, b_gate, conv_w, conv_b, lru_w_a, lru_b_a, lru_w_x, lru_b_x, lru_lambda, pool_w, pool_scale, w_lru_up, w_pool_up, w_o, w_ff1, w_ff2, loss_target, m_norm_mix_pre, m_norm_mix_post, m_norm_mlp_pre, m_norm_mlp_post, m_w_in, m_b_gate, m_conv_w, m_conv_b, m_lru_w_a, m_lru_b_a, m_lru_w_x, m_lru_b_x, m_lru_lambda, m_pool_w, m_pool_scale, m_w_lru_up, m_w_pool_up, m_w_o, m_w_ff1, m_w_ff2, v_norm_mix_pre, v_norm_mix_post, v_norm_mlp_pre, v_norm_mlp_post, v_w_in, v_b_gate, v_conv_w, v_conv_b, v_lru_w_a, v_lru_b_a, v_lru_w_x, v_lru_b_x, v_lru_lambda, v_pool_w, v_pool_scale, v_w_lru_up, v_w_pool_up, v_w_o, v_w_ff1, v_w_ff2):
    args = dict(locals())
    two_d = lambda a: a.reshape(-1, a.shape[-1])
    w = {n: two_d(args[n]) for n in W_NAMES}
    mom = {n: two_d(args["m_" + n]) for n in W_NAMES}
    var = {n: two_d(args["v_" + n]) for n in W_NAMES}
    i32 = lambda val: jnp.asarray(val, jnp.int32)
    chip = i32(2 * lax.axis_index("x") + lax.axis_index("y"))
    core = i32(lax.axis_index("c"))
    cidx = core.reshape(1)
    zero = i32(0)
    hd = lru_w_a.shape[-1]
    xs, target = x[0], loss_target[0]
    g1, g2, g3, g4 = norm_mix_pre, norm_mix_post, norm_mlp_pre, norm_mlp_post

    mix = ["w_lru_up", "w_pool_up", "w_o"]
    full = {"w_in": _cast_place(w["w_in"], chip.reshape(1), "cast_w_in")}
    (fl_in, fl_conv), first = _split_call("gather_start_first", start=[
        _gather([full["w_in"]], ici=[(0, ALL)]), _gather_whole(w["conv_w"])])
    casts, _ = _cast_place_multi([w[n] for n in BIG[1:]], chip.reshape(1), stages=[_after(first)])
    full.update(zip(BIG[1:], casts))
    (fl_mix, fl_ff1, fl_ff2), started = _split_call("gather_start_rest", start=[
        _gather([full[n] for n in mix], ici=[(0, ALL), (1, ALL), (2, ALL)]),
        _gather([full["w_ff1"]], ici=[(0, ALL)]), _gather([full["w_ff2"]], ici=[(0, ALL)])])
    wa = _block_diag(lru_w_a[0]).astype(BF)
    wx = _block_diag(lru_w_x[0]).astype(BF)
    pw = pool_w[0].astype(BF)

    def to_sibling(name, flight, after=None):
        (fl,), passed = _split_call(name + "_pass", finish=[flight], after=after,
                                    start=[_gather(flight.landed(), d2d=[(i, ALL) for i in range(len(flight.bufs))])])
        passed_on.append(passed)
        return fl

    passed_on = []

    def arrived(name, flight, after=None):
        _split_call(name + "_done", finish=[flight], after=after)
        return flight.landed()

    idx_big = jnp.stack([chip, (chip + 1) % NCHIP, (chip + 2) % NCHIP, (chip + 3) % NCHIP, core])
    proj, h1 = _fwd_inproj_own(xs, g1, fl_in.bufs[0], idx_big, stages=[_after(started)])
    fl_in = to_sibling("gather_w_in", fl_in, after=h1)
    _split_call("gather_w_in_done", finish=[fl_in, fl_conv])
    (w_in_f,), (conv_all,) = fl_in.landed(), fl_conv.landed()
    full["w_in"] = w_in_f
    conv_all = lax.dynamic_update_slice(conv_all, w["conv_w"][None], (chip, zero, zero))
    conv_full = jnp.transpose(conv_all, (1, 0, 2)).reshape(4, DR)
    proj = _fwd_inproj_rest(h1, w_in_f, proj, idx_big)
    fl_mix = to_sibling("gather_mix", fl_mix, after=proj)
    (ylru, hs), _ = _fwd_lru(proj, conv_full, conv_b, wa, lru_b_a, wx, lru_b_x, lru_lambda,
                             stages=[_after(passed_on[-1])])
    got = arrived("gather_mix", fl_mix, after=ylru)
    fl_ff1 = to_sibling("gather_ff1", fl_ff1, after=ylru)
    w_lru_up_f, w_pool_up_f, w_o_f = got[0].reshape(DR, D), got[1], got[2].reshape(D, D)
    ypool = _fwd_pool(proj, pw, pool_scale)
    (x2, h2, m, mrg, bra, brb), _ = _fwd_merge(xs, ylru, ypool, proj, b_gate, g2, g3, w_lru_up_f, w_pool_up_f, w_o_f,
                                               stages=[_after(passed_on[-1])])
    fl_ff2 = to_sibling("gather_ff2", fl_ff2, after=h2)
    _split_call("gather_ff_done", finish=[fl_ff1, fl_ff2])
    (ff1,), (ff2,) = fl_ff1.landed(), fl_ff2.landed()
    ff2 = ff2.reshape(DF, D)
    a1, lossp, dy, df, dg4 = _fwd_mlp_loss(h2, ff1, ff2, x2, target, g4)

    dh2, df1 = _bwd_mlp_x(df, a1, ff1, ff2)
    dw_ff1, dw_ff2 = _bwd_mlp_w(df, h2, a1, df1)
    g_ff = [_halves(dw_ff1), _halves(dw_ff2)]
    (dxres, dgates, dylru, dypool, dm, dbra, dbrb, dg2, dg3, dbg), (l_ff,) = _bwd_merge(
        dh2, dy, x2, m, bra, brb, proj, b_gate, g2, g3, w_lru_up_f, w_pool_up_f, w_o_f, stages=[_to_sibling(g_ff)])
    p_ff = _add_sibling_multi(g_ff, l_ff, cidx, "add_sibling_ff")
    (fl_ff,), sent_ff = _split_call("reduce_ff_start", start=[_to_chips(p_ff)])
    (dw_o, dw_lru_up, dw_pool_up), _ = _dw_merge(mrg, dm, ylru, dbra, ypool, dbrb, stages=[_after(sent_ff)])
    g_mix = [_halves(dw_lru_up), _halves(dw_pool_up), _halves(dw_o)]
    (dxp, dgl, dcw, dcb, dwa, dba, dwx, dbx, dlam), (l_mix,) = _bwd_lru(
        proj, hs, dylru, conv_full, conv_b, wa, lru_b_a, wx, lru_b_x, lru_lambda, stages=[_to_sibling(g_mix)])
    p_mix = _add_sibling_multi(g_mix, l_mix, cidx, "add_sibling_mix")
    dxpool, dpw, dsc = _bwd_pool(proj, dypool, pw, pool_scale)
    dproj = [dxp, dgl, dxpool, dgates]
    small = jnp.concatenate([
        jnp.zeros((1, D), F32), dg2, dg3, dg4, dbg.reshape(2, D), dcb, dba, dbx, dlam,
        jnp.pad(dsc, ((0, 0), (0, D - DP))), jnp.pad(lossp, ((0, 0), (0, D - 1))), dcw,
        _block_diag_extract(dwa, hd).reshape(-1, D), _block_diag_extract(dwx, hd).reshape(-1, D),
        dpw.reshape(-1, D)], axis=0)
    (fl_mixr, fl_smalls), sent_mix = _split_call("reduce_mix_start", start=[_to_chips(p_mix), _to_sibling([small])])
    dw_in = _bwd_inproj_w(h1, dproj, sent_mix)
    _split_call("reduce_small_sibling_done", finish=[fl_smalls], after=dw_in)
    small, l_small = fl_smalls.bufs
    small2 = _add_pair(small, l_small, "add_sibling_small").reshape(2, SMALL_ROWS // 2, D)
    g_in = _halves(dw_in)
    done = ["w_ff1", "w_ff2"] + mix
    (fl_gin, fl_small), sib_started = _split_call("reduce_in_sibling_start", finish=[fl_ff, fl_mixr],
                                                  start=[_to_sibling([g_in]), _to_chips([small2])])
    p_ff1, p_ff2, c_ff1, c_ff2 = fl_ff.bufs
    p_mix, c_mix = fl_mixr.bufs[:3], fl_mixr.bufs[3:]
    pairs, _ = _add_chips_multi([p_ff1, p_ff2] + p_mix, [c_ff1, c_ff2] + c_mix, idx_big, "add_chips_done",
                                stages=[_after(sib_started)])
    _split_call("reduce_in_sibling_done", finish=[fl_gin], after=pairs[-1])
    g_in, l_in = fl_gin.bufs
    p_in = _add_sibling(g_in, l_in, cidx, "add_sibling_w_in")[0]
    (fl_pin,), token = _split_call("reduce_last_start", start=[_to_chips([p_in])])
    _split_call("reduce_small_done", finish=[fl_small], after=token)
    small2, c_small = fl_small.bufs
    own_small = lax.dynamic_index_in_dim(small2, core, 0, keepdims=True)
    c_small = lax.dynamic_update_slice(c_small, own_small, (chip, zero, zero))
    pair_small = _add_chips(c_small, c_small, jnp.stack([zero, zero + 1, zero + 2, zero + 3, core]), "add_chips_small")
    (fl_share,), shared_start = _split_call("reduce_share_start", start=[_share(pairs + [pair_small])])
    grad_x, dg1 = _bwd_inproj_x(dproj, full["w_in"], xs, dxres, g1, stages=[_after(shared_start)])
    _split_call("reduce_share_done", finish=[fl_share, fl_pin], after=dg1)
    shared, (p_in, c_in) = fl_share.landed(), fl_pin.bufs
    pairs, pair_small = shared[:-1], shared[-1]

    grads, delta, new_m, new_v = {}, {}, {}, {}
    for n, p in zip(done, pairs):
        grads[n] = p.reshape(-1, p.shape[-1])

    def update(n, stages=()):
        (grads[n], delta[n], new_m[n], new_v[n]), landed = _adamw(w[n], grads[n], mom[n], var[n], "adamw_" + n,
                                                                  stages=stages)
        return landed

    pair_in = _add_chips(p_in, c_in, idx_big, "add_chips_w_in")
    (fl_last, fl_dg1), last_start = _split_call("reduce_last_share_start", start=[_share([pair_in]), _to_everyone(dg1)])
    updated, _ = _adamw_multi(["w_ff1", "w_ff2", "w_o", "w_lru_up"], w, grads, mom, var, stages=[_after(last_start)])
    for n, (go, d, mo, vo) in updated.items():
        grads[n], delta[n], new_m[n], new_v[n] = go, d, mo, vo
    _split_call("reduce_last_share_done", finish=[fl_last, fl_dg1], after=new_v["w_lru_up"])
    (pair_in,), (dg1, dg1_all) = fl_last.landed(), fl_dg1.bufs
    dg1_all = lax.dynamic_update_slice(dg1_all, dg1[None], (2 * chip + core, zero, zero)).reshape(2 * NCHIP, D)
    grads["w_in"] = pair_in.reshape(-1, pair_in.shape[-1])
    update("w_pool_up")
    update("w_in")
    small_sum = pair_small.reshape(SMALL_ROWS, D)
    loss = 0.5 * small_sum[LOSS_ROW, 0]
    ccols = DR // NCHIP
    sep = [lax.dynamic_slice(small_sum[12:16], (zero, chip * ccols), (4, ccols)),
           small_sum[16:80].reshape(-1, hd), small_sum[80:144].reshape(-1, hd), small_sum[144:208].reshape(-1, PG)]
    g_s, d_s, m_s, v_s = _adamw_small(small_sum, dg1_all, sep, w, mom, var)
    grads.update(g_s)
    grads.update(dict(zip(SMALL_SEPARATE, sep)))
    delta.update(d_s)
    new_m.update(m_s)
    new_v.update(v_s)

    out = lambda d: [d[n].reshape(args[n].shape) for n in W_NAMES]
    return (loss, grad_x[None], *out(grads), *out(delta), *out(new_m), *out(new_v))
```

```python
import functools
import math

import jax
import jax.numpy as jnp
from jax import lax
from jax.experimental import pallas as pl
from jax.experimental.pallas import tpu as pltpu

F32 = jnp.float32
BF = jnp.bfloat16

T = 2048
D = 1024
DR = 1024
DP = 512
DF = 4096
DIN = 4608
NCHIP = 4
CW_IN = DIN // NCHIP
LANE = 128
CB = 128
NG = DR // CB
PG = 128
POOL_WINDOWS = (2, 4, 8, 16)
NORM_EPS = 1e-6
LRU_C = 8.0
GELU_C = math.sqrt(2.0 / math.pi)
ADAM_LR = 0.001
ADAM_B1 = 0.9
ADAM_B2 = 0.999
ADAM_EPS = 1e-08
ADAM_WD = 0.01
ADAM_STEP = 10
MESH_ID = pl.DeviceIdType.MESH
ANY = pl.BlockSpec(memory_space=pl.ANY)
SMALL_ROWS = 208
LOSS_ROW = 11
MIB = 1 << 20


def _cp(vmem_mib=None):
    if vmem_mib is None:
        return pltpu.CompilerParams()
    return pltpu.CompilerParams(vmem_limit_bytes=vmem_mib * MIB)


def _hbm(*arrays):
    return [pltpu.with_memory_space_constraint(a, pltpu.HBM) for a in arrays]


def _hbm_out(shapes):
    return [pltpu.HBM(s.shape, s.dtype) for s in shapes]


class _Stage:
    def __init__(self, operands, out_shape, alias, sems, start, finish):
        self.operands, self.out_shape, self.alias, self.sems = list(operands), list(out_shape), dict(alias), list(sems)
        self.start, self.finish = start, finish


def _call(body, *, name, grid, in_specs, out_specs, out_shape, args, vmem=None, stages=(), prefetch=None,
          scratch=()):
    nin, nout = len(in_specs), len(out_specs)
    npre = 0 if prefetch is None else 1
    st_args, st_shapes, st_sems, aliases = [], [], list(scratch), {}
    for st in stages:
        for k, v in st.alias.items():
            aliases[npre + nin + len(st_args) + k] = nout + len(st_shapes) + v
        st_args += st.operands
        st_shapes += st.out_shape
        st_sems += st.sems

    def wrapped(*refs):
        pre, refs = refs[:npre], refs[npre:]
        ins, pos = refs[:nin], nin
        st_ins = []
        for st in stages:
            st_ins.append(refs[pos:pos + len(st.operands)])
            pos += len(st.operands)
        outs, pos = refs[pos:pos + nout], pos + nout
        st_outs = []
        for st in stages:
            st_outs.append(refs[pos:pos + len(st.out_shape)])
            pos += len(st.out_shape)
        work, pos = refs[pos:pos + len(scratch)], pos + len(scratch)
        sems = []
        for st in stages:
            sems.append(refs[pos:pos + len(st.sems)])
            pos += len(st.sems)
        if stages:
            first = functools.reduce(jnp.logical_and, [pl.program_id(a) == 0 for a in range(len(grid))])

            @pl.when(first)
            def _():
                for st, a, b, s in zip(stages, st_ins, st_outs, sems):
                    st.start(a, b, s)

        body(*pre, *ins, *outs, *work)
        if stages:
            last = functools.reduce(jnp.logical_and, [pl.program_id(a) == g - 1 for a, g in enumerate(grid)])

            @pl.when(last)
            def _():
                for st, a, b, s in zip(stages, st_ins, st_outs, sems):
                    st.finish(a, b, s)

    all_in = list(in_specs) + [ANY] * len(st_args)
    all_out = list(out_specs) + [ANY] * len(st_shapes)
    kw = dict(has_side_effects=True) if stages else {}
    if vmem is not None:
        kw["vmem_limit_bytes"] = vmem * MIB
    if prefetch is None:
        gkw = dict(grid=grid, in_specs=all_in, out_specs=all_out, scratch_shapes=st_sems)
    else:
        gkw = dict(grid_spec=pltpu.PrefetchScalarGridSpec(
            num_scalar_prefetch=1, grid=grid, in_specs=all_in, out_specs=all_out, scratch_shapes=st_sems))
    res = pl.pallas_call(
        wrapped, name=name, out_shape=_hbm_out(list(out_shape) + st_shapes), input_output_aliases=aliases,
        compiler_params=pltpu.CompilerParams(**kw), **gkw,
    )(*([prefetch] if npre else []), *_hbm(*args, *st_args))
    outs, rest, st_res = list(res[:nout]), list(res[nout:]), []
    for st in stages:
        st_res.append(rest[:len(st.out_shape)])
        rest = rest[len(st.out_shape):]
    return outs, st_res


def _mm(a, b):
    return jnp.dot(a.astype(BF), b.astype(BF), preferred_element_type=F32)


def _mm_nt(a, b):
    return lax.dot_general(a.astype(BF), b.astype(BF), (((1,), (1,)), ((), ())),
                           preferred_element_type=F32)


def _mm_tn(a, b):
    return lax.dot_general(a.astype(BF), b.astype(BF), (((0,), (0,)), ((), ())),
                           preferred_element_type=F32)


def _rows(v):
    return lax.broadcasted_iota(jnp.int32, v.shape, 0)


def _sd(v, s, fill=0.0):
    return jnp.where(_rows(v) >= s, pltpu.roll(v, s, axis=0), fill)


def _su(v, s, fill=0.0):
    n = v.shape[0]
    return jnp.where(_rows(v) < n - s, pltpu.roll(v, n - s, axis=0), fill)


def _sigmoid(z):
    return 1.0 / (1.0 + jnp.exp(-z))


def _softplus(z):
    e = jnp.exp(-jnp.abs(z))
    u = 1.0 + e
    d = u - 1.0
    log1p = jnp.where(d == 0.0, e, jnp.log(u) * (e / jnp.where(d == 0.0, 1.0, d)))
    return jnp.maximum(z, 0.0) + log1p


def _mean(v):
    return jnp.mean(v, axis=-1, keepdims=True)


def _colsum(v):
    return jnp.sum(v, axis=0, keepdims=True)


def _acc(ref, val, first):
    @pl.when(first)
    def _():
        ref[...] = val

    @pl.when(jnp.logical_not(first))
    def _():
        ref[...] += val


def _conv(xp, cw, cb):
    x1, x2, x3 = _sd(xp, 1), _sd(xp, 2), _sd(xp, 3)
    xc = cb + cw[0:1] * x3 + cw[1:2] * x2 + cw[2:3] * x1 + cw[3:4] * xp
    return xc, x1, x2, x3


def _lru_gates(xc, wa, ba, wx, bx, lam):
    xcb = xc.astype(BF)
    r = _sigmoid(_mm(xcb, wa) + ba)
    ii = _sigmoid(_mm(xcb, wx) + bx)
    sp = _softplus(-lam)
    la = (-LRU_C) * r * sp
    a = jnp.exp(la)
    mult = jnp.sqrt(-jnp.tanh(la) * (a * a + 1.0))
    return xcb, r, ii, sp, a, mult


def _gelu_parts(g):
    th = jnp.tanh(GELU_C * (g + 0.044715 * (g * g * g)))
    gel = 0.5 * g * (1.0 + th)
    dgel = 0.5 * (1.0 + th) + 0.5 * g * (1.0 - th * th) * (GELU_C * (1.0 + 3.0 * 0.044715 * (g * g)))
    return gel, dgel


def _tile_scan(a, b, a_s, b_s, out_ref, reverse):
    n, lanes = a.shape
    nt = n // 8
    a, b = a.reshape(nt, 8, lanes), b.reshape(nt, 8, lanes)
    sub = lax.broadcasted_iota(jnp.int32, a.shape, 1)
    s = 1
    while s < 8:
        keep = sub < 8 - s if reverse else sub >= s
        amount = 8 - s if reverse else s
        b = b + a * jnp.where(keep, pltpu.roll(b, amount, axis=1), 0.0)
        a = a * jnp.where(keep, pltpu.roll(a, amount, axis=1), 1.0)
        s *= 2
    a_s[...] = a.reshape(n, lanes)
    b_s[...] = b.reshape(n, lanes)
    edge = pl.ds(0 if reverse else 7, nt, stride=8)
    ta, tb = a_s[edge, :], b_s[edge, :]
    shift = _su if reverse else _sd
    s = 1
    while s < nt:
        tb = tb + ta * shift(tb, s, 0.0)
        if 2 * s < nt:
            ta = ta * shift(ta, s, 1.0)
        s *= 2
    enters = shift(tb, 1, 0.0)
    for o in range(8):
        rows = pl.ds(o, nt, stride=8)
        out_ref[rows, :] = b_s[rows, :] + a_s[rows, :] * enters


def _pool_window(x, steps, shift):
    s, sh = x, 1
    for _ in range(steps):
        s = s + shift(s, sh)
        sh *= 2
    return s


def _fwd_inproj_own(x, g1, w_in, slots, stages=()):
    tm = 1024

    def body(s_ref, x_ref, g_ref, w_ref, proj_ref, h_ref):
        xv = x_ref[...]
        r = lax.rsqrt(_mean(xv * xv) + NORM_EPS)
        h = ((xv * r) * g_ref[...]).astype(BF)
        h_ref[...] = h
        proj_ref[...] = jnp.dot(h, w_ref[0], preferred_element_type=F32)

    return _call(
        body, name="fwd_inproj_own", grid=(T // tm,), prefetch=slots,
        in_specs=[pl.BlockSpec((tm, D), lambda i, s: (i, 0)),
                  pl.BlockSpec((1, D), lambda i, s: (0, 0)),
                  pl.BlockSpec((1, D, CW_IN), lambda i, s: (s[0], 0, 0))],
        out_specs=[pl.BlockSpec((tm, CW_IN), lambda i, s: (i, s[0])),
                   pl.BlockSpec((tm, D), lambda i, s: (i, 0))],
        out_shape=[jax.ShapeDtypeStruct((T, DIN), F32), jax.ShapeDtypeStruct((T, D), BF)],
        vmem=40, args=[x, g1, w_in], stages=stages)[0]


def _fwd_inproj_rest(h1, w_in, proj, slots, name):
    tm = 1024

    def body(s_ref, h_ref, w_ref, p_in, proj_ref):
        proj_ref[...] = jnp.dot(h_ref[...], w_ref[0], preferred_element_type=F32)

    return pl.pallas_call(
        body, name=name,
        grid_spec=pltpu.PrefetchScalarGridSpec(
            num_scalar_prefetch=1, grid=(slots.shape[0], T // tm),
            in_specs=[pl.BlockSpec((tm, D), lambda k, i, s: (i, 0)),
                      pl.BlockSpec((1, D, CW_IN), lambda k, i, s: (s[k], 0, 0)), ANY],
            out_specs=pl.BlockSpec((tm, CW_IN), lambda k, i, s: (i, s[k]))),
        out_shape=pltpu.HBM((T, DIN), F32), input_output_aliases={3: 0},
        compiler_params=_cp(40),
    )(slots, *_hbm(h1, w_in, proj))


def _vec_spec():
    return pl.BlockSpec((1, CB), lambda j: (0, j))


def _fwd_lru(proj, conv_w, conv_b, wa, ba, wx, bx, lam, stages=()):
    def body(xp_ref, g_ref, cw_ref, cb_ref, wa_ref, ba_ref, wx_ref, bx_ref, lam_ref, y_ref, h_ref, a_s, b_s):
        xc, _, _, _ = _conv(xp_ref[...], cw_ref[...], cb_ref[...])
        _, _, ii, _, a, mult = _lru_gates(xc, wa_ref[0], ba_ref[...], wx_ref[0], bx_ref[...], lam_ref[...])
        _tile_scan(a, mult * (ii * xc), a_s, b_s, h_ref, reverse=False)
        gel, _ = _gelu_parts(g_ref[...])
        y_ref[...] = (h_ref[...] * gel).astype(BF)

    return _call(
        body, name="fwd_lru", grid=(NG,),
        in_specs=[pl.BlockSpec((T, CB), lambda j: (0, j)),
                  pl.BlockSpec((T, CB), lambda j: (0, NG + j)),
                  pl.BlockSpec((4, CB), lambda j: (0, j)),
                  _vec_spec(),
                  pl.BlockSpec((1, CB, CB), lambda j: (j, 0, 0)), _vec_spec(),
                  pl.BlockSpec((1, CB, CB), lambda j: (j, 0, 0)), _vec_spec(),
                  _vec_spec()],
        out_specs=[pl.BlockSpec((T, CB), lambda j: (0, j)), pl.BlockSpec((T, CB), lambda j: (0, j))],
        out_shape=[jax.ShapeDtypeStruct((T, DR), BF), jax.ShapeDtypeStruct((T, DR), F32)],
        vmem=48, args=[proj, proj, conv_w, conv_b, wa, ba, wx, bx, lam], stages=stages,
        scratch=[pltpu.VMEM((T, CB), F32)] * 2)


def _pool_cnt(w):
    t = lax.broadcasted_iota(jnp.int32, (T, 1), 0)
    return jnp.minimum(t + 1, w).astype(F32)


def _fwd_pool(proj, pool_w, pool_scale):
    def body(xp_ref, pw_ref, sc_ref, y_ref):
        for g, w in enumerate(POOL_WINDOWS):
            cols = slice(g * PG, (g + 1) * PG)
            x = xp_ref[:, cols]
            p = _pool_window(x, g + 1, _sd) / _pool_cnt(w) - x
            y_ref[:, cols] = (_mm(p, pw_ref[g]) * sc_ref[:, cols]).astype(BF)

    return pl.pallas_call(
        body, name="fwd_pool", grid=(1,),
        in_specs=[pl.BlockSpec((T, DP), lambda i: (0, 2 * DR // DP)),
                  pl.BlockSpec((4, PG, PG), lambda i: (0, 0, 0)),
                  pl.BlockSpec((1, DP), lambda i: (0, 0))],
        out_specs=pl.BlockSpec((T, DP), lambda i: (0, 0)),
        out_shape=pltpu.HBM((T, DP), BF),
        compiler_params=_cp(48),
    )(*_hbm(proj, pool_w, pool_scale))


GATE_BLK = 512
GATE_BLK0 = (2 * DR + DP) // GATE_BLK


def _gate_specs(tm):
    return [pl.BlockSpec((tm, GATE_BLK), functools.partial(lambda i, q: (i, GATE_BLK0 + q), q=q))
            for q in range(4)]


def _fwd_merge(x, ylru, ypool, proj, b_gate, g2, g3, w_lru_up, w_pool_up, w_o, stages=()):
    tm = 512

    def body(x_ref, yl_ref, yp_ref, p0, p1, p2, p3, bg_ref, g2_ref, g3_ref, wl_ref, wp_ref, wo_ref,
             x2_ref, h2_ref, m_ref, mrg_ref, bra_ref, brb_ref):
        bra = jnp.dot(yl_ref[...], wl_ref[...], preferred_element_type=F32)
        yp = yp_ref[...]
        brb = jnp.concatenate([jnp.dot(yp, wp_ref[k], preferred_element_type=F32) for k in range(NCHIP)], axis=1)
        bg = bg_ref[...]
        ga = _sigmoid(jnp.concatenate([p0[...], p1[...]], axis=1) + bg[:, :D])
        gb = _sigmoid(jnp.concatenate([p2[...], p3[...]], axis=1) + bg[:, D:])
        mrg = (ga * bra + gb * brb).astype(BF)
        m = jnp.dot(mrg, wo_ref[...], preferred_element_type=F32)
        r2 = lax.rsqrt(_mean(m * m) + NORM_EPS)
        x2 = x_ref[...] + (m * r2) * g2_ref[...]
        r3 = lax.rsqrt(_mean(x2 * x2) + NORM_EPS)
        x2_ref[...] = x2
        h2_ref[...] = ((x2 * r3) * g3_ref[...]).astype(BF)
        m_ref[...] = m
        mrg_ref[...] = mrg
        bra_ref[...] = bra.astype(BF)
        brb_ref[...] = brb.astype(BF)

    row = lambda w: pl.BlockSpec((tm, w), lambda i: (i, 0))
    full2 = lambda a, b: pl.BlockSpec((a, b), lambda i: (0, 0))
    return _call(
        body, name="fwd_merge", grid=(T // tm,),
        in_specs=[row(D), row(DR), row(DP)] + _gate_specs(tm) +
                 [full2(1, 2 * D), full2(1, D), full2(1, D), full2(DR, D),
                  pl.BlockSpec((NCHIP, DP, D // NCHIP), lambda i: (0, 0, 0)), full2(D, D)],
        out_specs=[row(D)] * 6,
        out_shape=[jax.ShapeDtypeStruct((T, D), F32), jax.ShapeDtypeStruct((T, D), BF),
                   jax.ShapeDtypeStruct((T, D), F32), jax.ShapeDtypeStruct((T, D), BF),
                   jax.ShapeDtypeStruct((T, D), BF), jax.ShapeDtypeStruct((T, D), BF)],
        vmem=48, args=[x, ylru, ypool, proj, proj, proj, proj, b_gate, g2, g3, w_lru_up, w_pool_up, w_o],
        stages=stages)


def _fwd_mlp_loss(h2, w_ff1, w_ff2, x2, target, g4):
    tm = 512
    fk = DF // NCHIP

    def body(h_ref, w1_ref, w2_ref, x2_ref, t_ref, g_ref, a1_ref, loss_ref, dy_ref, df_ref, dg_ref):
        first = pl.program_id(0) == 0
        h = h_ref[...]
        f = None
        for k in range(NCHIP):
            a1 = jnp.maximum(jnp.dot(h, w1_ref[k], preferred_element_type=F32), 0.0)
            a1_ref[:, k * fk:(k + 1) * fk] = a1.astype(BF)
            part = jnp.dot((a1 * a1).astype(BF), w2_ref[k * fk:(k + 1) * fk, :], preferred_element_type=F32)
            f = part if f is None else f + part
        g4v = g_ref[...]
        r4 = lax.rsqrt(_mean(f * f) + NORM_EPS)
        fn = f * r4
        e = (x2_ref[...] + fn * g4v) - t_ref[...]
        _acc(loss_ref, jnp.sum(_mean(e * e), axis=0, keepdims=True), first)
        dy = e * (1.0 / D)
        dy_ref[...] = dy
        _acc(dg_ref, _colsum(dy * fn), first)
        dfn = dy * g4v
        df_ref[...] = (r4 * (dfn - fn * _mean(dfn * fn))).astype(BF)

    row = pl.BlockSpec((tm, D), lambda i: (i, 0))
    return pl.pallas_call(
        body, name="fwd_mlp_loss", grid=(T // tm,),
        in_specs=[row, pl.BlockSpec((NCHIP, D, fk), lambda i: (0, 0, 0)), pl.BlockSpec((DF, D), lambda i: (0, 0)),
                  row, row, pl.BlockSpec((1, D), lambda i: (0, 0))],
        out_specs=[pl.BlockSpec((tm, DF), lambda i: (i, 0)), pl.BlockSpec((1, 1), lambda i: (0, 0)), row, row,
                   pl.BlockSpec((1, D), lambda i: (0, 0))],
        out_shape=_hbm_out([jax.ShapeDtypeStruct((T, DF), BF), jax.ShapeDtypeStruct((1, 1), F32),
                            jax.ShapeDtypeStruct((T, D), F32), jax.ShapeDtypeStruct((T, D), BF),
                            jax.ShapeDtypeStruct((1, D), F32)]),
        compiler_params=_cp(56),
    )(*_hbm(h2, w_ff1, w_ff2, x2, target, g4))


def _bwd_mlp_x(df, a1, w_ff1, w_ff2):
    tm = 512
    fk = DF // NCHIP

    def body(df_ref, a1_ref, w1_ref, w2_ref, dh_ref, df1_ref):
        df = df_ref[...]
        dh = None
        for k in range(NCHIP):
            cols = slice(k * fk, (k + 1) * fk)
            dact = _mm_nt(df, w2_ref[cols, :])
            df1 = (dact * (2.0 * a1_ref[:, cols].astype(F32))).astype(BF)
            df1_ref[:, cols] = df1
            part = _mm_nt(df1, w1_ref[k])
            dh = part if dh is None else dh + part
        dh_ref[...] = dh

    return pl.pallas_call(
        body, name="bwd_mlp_x", grid=(T // tm,),
        in_specs=[pl.BlockSpec((tm, D), lambda i: (i, 0)),
                  pl.BlockSpec((tm, DF), lambda i: (i, 0)),
                  pl.BlockSpec((NCHIP, D, fk), lambda i: (0, 0, 0)),
                  pl.BlockSpec((DF, D), lambda i: (0, 0))],
        out_specs=[pl.BlockSpec((tm, D), lambda i: (i, 0)), pl.BlockSpec((tm, DF), lambda i: (i, 0))],
        out_shape=_hbm_out([jax.ShapeDtypeStruct((T, D), F32), jax.ShapeDtypeStruct((T, DF), BF)]),
        compiler_params=_cp(56),
    )(*_hbm(df, a1, w_ff1, w_ff2))


def _bwd_mlp_w(df, h2, a1, df1):
    fc = 512
    per = (DF // NCHIP) // fc

    def body(df_ref, h_ref, a1_ref, df1_ref, dw1_ref, dw2_ref):
        a1 = a1_ref[...].astype(F32)
        dw2_ref[...] = _mm_tn((a1 * a1).astype(BF), df_ref[...]).astype(BF)
        dw1_ref[0] = _mm_tn(h_ref[...], df1_ref[...]).astype(BF)

    return pl.pallas_call(
        body, name="bwd_mlp_w", grid=(DF // fc,),
        in_specs=[pl.BlockSpec((T, D), lambda j: (0, 0)),
                  pl.BlockSpec((T, D), lambda j: (0, 0)),
                  pl.BlockSpec((T, fc), lambda j: (0, j)),
                  pl.BlockSpec((T, fc), lambda j: (0, j))],
        out_specs=[pl.BlockSpec((1, D, fc), lambda j: (j // per, 0, j % per)),
                   pl.BlockSpec((fc, D), lambda j: (j, 0))],
        out_shape=_hbm_out([jax.ShapeDtypeStruct((NCHIP, D, DF // NCHIP), BF),
                            jax.ShapeDtypeStruct((DF, D), BF)]),
        compiler_params=_cp(56),
    )(*_hbm(df, h2, a1, df1))


def _bwd_merge(dh2, dy, x2, m, bra, brb, proj, b_gate, g2, g3, w_lru_up, w_pool_up, w_o, stages=()):
    tm = 256
    cpu = D // NCHIP

    def body(dh2_ref, dy_ref, x2_ref, m_ref, bra_ref, brb_ref, p0, p1, p2, p3, bg_ref,
             g2_ref, g3_ref, wl_ref, wp_ref, wo_ref,
             dx_ref, dgt_ref, dyl_ref, dyp_ref, dm_ref, dbra_ref, dbrb_ref, dg2_ref, dg3_ref, dbg_ref):
        first = pl.program_id(0) == 0
        x2 = x2_ref[...]
        r3 = lax.rsqrt(_mean(x2 * x2) + NORM_EPS)
        x2n = x2 * r3
        dh2 = dh2_ref[...]
        t3 = dh2 * g3_ref[...]
        dx2 = dy_ref[...] + r3 * (t3 - x2n * _mean(t3 * x2n))
        dx_ref[...] = dx2
        _acc(dg3_ref, _colsum(dh2 * x2n), first)
        m = m_ref[...]
        r2 = lax.rsqrt(_mean(m * m) + NORM_EPS)
        mn = m * r2
        _acc(dg2_ref, _colsum(dx2 * mn), first)
        dmn = dx2 * g2_ref[...]
        dm = (r2 * (dmn - mn * _mean(dmn * mn))).astype(BF)
        dm_ref[...] = dm
        dmrg = _mm_nt(dm, wo_ref[...])
        bg = bg_ref[...]
        ga = _sigmoid(jnp.concatenate([p0[...], p1[...]], axis=1) + bg[:, :D])
        gb = _sigmoid(jnp.concatenate([p2[...], p3[...]], axis=1) + bg[:, D:])
        dga = dmrg * bra_ref[...].astype(F32) * (ga * (1.0 - ga))
        dgb = dmrg * brb_ref[...].astype(F32) * (gb * (1.0 - gb))
        dgt_ref[:, :D] = dga.astype(BF)
        dgt_ref[:, D:] = dgb.astype(BF)
        _acc(dbg_ref, jnp.concatenate([_colsum(dga), _colsum(dgb)], axis=1), first)
        dbra = (dmrg * ga).astype(BF)
        dbrb = (dmrg * gb).astype(BF)
        dbra_ref[...] = dbra
        dbrb_ref[...] = dbrb
        dyl_ref[...] = _mm_nt(dbra, wl_ref[...])
        dyp = None
        for k in range(NCHIP):
            part = _mm_nt(dbrb[:, k * cpu:(k + 1) * cpu], wp_ref[k])
            dyp = part if dyp is None else dyp + part
        dyp_ref[...] = dyp

    row = lambda w: pl.BlockSpec((tm, w), lambda i: (i, 0))
    full2 = lambda a, b: pl.BlockSpec((a, b), lambda i: (0, 0))
    wp_spec = pl.BlockSpec((NCHIP, DP, cpu), lambda i: (0, 0, 0))
    return _call(
        body, name="bwd_merge", grid=(T // tm,),
        in_specs=[row(D)] * 6 + _gate_specs(tm) +
                 [full2(1, 2 * D), full2(1, D), full2(1, D), full2(DR, D), wp_spec, full2(D, D)],
        out_specs=[row(D), row(2 * D), row(DR), row(DP), row(D), row(D), row(D),
                   full2(1, D), full2(1, D), full2(1, 2 * D)],
        out_shape=[jax.ShapeDtypeStruct((T, D), F32), jax.ShapeDtypeStruct((T, 2 * D), BF),
                   jax.ShapeDtypeStruct((T, DR), F32), jax.ShapeDtypeStruct((T, DP), F32),
                   jax.ShapeDtypeStruct((T, D), BF), jax.ShapeDtypeStruct((T, D), BF),
                   jax.ShapeDtypeStruct((T, D), BF),
                   jax.ShapeDtypeStruct((1, D), F32), jax.ShapeDtypeStruct((1, D), F32),
                   jax.ShapeDtypeStruct((1, 2 * D), F32)],
        vmem=56, args=[dh2, dy, x2, m, bra, brb, proj, proj, proj, proj, b_gate, g2, g3, w_lru_up, w_pool_up, w_o],
        stages=stages)


def _dw_merge(mrg, dm, ylru, dbra, ypool, dbrb, stages=()):
    nb = NCHIP
    rb, pb, cpu = D // nb, DP // nb, D // NCHIP

    def body(mrg_ref, dm_ref, yl_ref, dbra_ref, yp_ref, dbrb_ref, dwo_ref, dwl_ref, dwp_ref):
        dwo_ref[...] = _mm_tn(mrg_ref[...], dm_ref[...]).astype(BF)
        dwl_ref[...] = _mm_tn(yl_ref[...], dbra_ref[...]).astype(BF)
        dwp = _mm_tn(yp_ref[...], dbrb_ref[...]).astype(BF)
        for k in range(NCHIP):
            dwp_ref[k] = dwp[:, k * cpu:(k + 1) * cpu]

    cols = lambda w: pl.BlockSpec((T, w), lambda r: (0, r))
    whole = pl.BlockSpec((T, D), lambda r: (0, 0))
    return _call(
        body, name="dw_merge", grid=(nb,),
        in_specs=[cols(rb), whole, cols(rb), whole, cols(pb), whole],
        out_specs=[pl.BlockSpec((rb, D), lambda r: (r, 0)), pl.BlockSpec((rb, D), lambda r: (r, 0)),
                   pl.BlockSpec((NCHIP, pb, cpu), lambda r: (0, r, 0))],
        out_shape=[jax.ShapeDtypeStruct((D, D), BF), jax.ShapeDtypeStruct((DR, D), BF),
                   jax.ShapeDtypeStruct((NCHIP, DP, cpu), BF)],
        vmem=56, args=[mrg, dm, ylru, dbra, ypool, dbrb], stages=stages)


def _bwd_lru(proj, h, dylru, conv_w, conv_b, wa, ba, wx, bx, lam, stages=()):
    def body(xp_ref, g_ref, h_ref, dy_ref, cw_ref, cb_ref, wa_ref, ba_ref, wx_ref, bx_ref, lam_ref,
             dxp_ref, dg_ref, dcw_ref, dcb_ref, dwa_ref, dba_ref, dwx_ref, dbx_ref, dlam_ref, a_s, b_s, l_s):
        xp = xp_ref[...]
        cw = cw_ref[...]
        lam = lam_ref[...]
        xc, x1, x2, x3 = _conv(xp, cw, cb_ref[...])
        wa, wx = wa_ref[0], wx_ref[0]
        xcb, r, ii, sp, a, mult = _lru_gates(xc, wa, ba_ref[...], wx, bx_ref[...], lam)
        g = g_ref[...]
        gel, dgel = _gelu_parts(g)
        h = h_ref[...]
        dy = dy_ref[...]
        dg_ref[...] = (dy * h * dgel).astype(BF)
        _tile_scan(_su(a, 1, 0.0), dy * gel, a_s, b_s, l_s, reverse=True)
        b = l_s[...]
        da = b * _sd(h, 1, 0.0)
        dmult = b * (ii * xc)
        dii = b * (mult * xc)
        dxc = b * (mult * ii)
        dla = da * a - dmult * ((a * a) / mult)
        dr = dla * ((-LRU_C) * sp)
        dsp = _colsum(dla * ((-LRU_C) * r))
        dlam_ref[...] = -dsp / (1.0 + jnp.exp(lam))
        dzr = dr * (r * (1.0 - r))
        dzi = dii * (ii * (1.0 - ii))
        dzrb, dzib = dzr.astype(BF), dzi.astype(BF)
        dxc = dxc + _mm_nt(dzrb, wa) + _mm_nt(dzib, wx)
        dwa_ref[0] = _mm_tn(xcb, dzrb)
        dwx_ref[0] = _mm_tn(xcb, dzib)
        dba_ref[...] = _colsum(dzr)
        dbx_ref[...] = _colsum(dzi)
        dcb_ref[...] = _colsum(dxc)
        dcw_ref[...] = jnp.concatenate([_colsum(dxc * x3), _colsum(dxc * x2), _colsum(dxc * x1),
                                        _colsum(dxc * xp)], axis=0)
        dxp = cw[3:4] * dxc + cw[2:3] * _su(dxc, 1) + cw[1:2] * _su(dxc, 2) + cw[0:1] * _su(dxc, 3)
        dxp_ref[...] = dxp.astype(BF)

    blk = pl.BlockSpec((T, CB), lambda j: (0, j))
    wsp = pl.BlockSpec((1, CB, CB), lambda j: (j, 0, 0))
    return _call(
        body, name="bwd_lru", grid=(NG,),
        in_specs=[blk, pl.BlockSpec((T, CB), lambda j: (0, NG + j)), blk, blk,
                  pl.BlockSpec((4, CB), lambda j: (0, j)), _vec_spec(), wsp, _vec_spec(), wsp, _vec_spec(),
                  _vec_spec()],
        out_specs=[blk, blk, pl.BlockSpec((4, CB), lambda j: (0, j)), _vec_spec(), wsp, _vec_spec(), wsp,
                   _vec_spec(), _vec_spec()],
        out_shape=[jax.ShapeDtypeStruct((T, DR), BF), jax.ShapeDtypeStruct((T, DR), BF),
                   jax.ShapeDtypeStruct((4, DR), F32), jax.ShapeDtypeStruct((1, DR), F32),
                   jax.ShapeDtypeStruct((NG, CB, CB), F32), jax.ShapeDtypeStruct((1, DR), F32),
                   jax.ShapeDtypeStruct((NG, CB, CB), F32), jax.ShapeDtypeStruct((1, DR), F32),
                   jax.ShapeDtypeStruct((1, DR), F32)],
        vmem=56, args=[proj, proj, h, dylru, conv_w, conv_b, wa, ba, wx, bx, lam], stages=stages,
        scratch=[pltpu.VMEM((T, CB), F32)] * 3)


def _bwd_pool(proj, dypool, pool_w, pool_scale):
    def body(xp_ref, dy_ref, pw_ref, sc_ref, dx_ref, dw_ref, dsc_ref):
        for g, w in enumerate(POOL_WINDOWS):
            cols = slice(g * PG, (g + 1) * PG)
            cnt = _pool_cnt(w)
            x = xp_ref[:, cols]
            pb = (_pool_window(x, g + 1, _sd) / cnt - x).astype(BF)
            wg = pw_ref[g]
            dy = dy_ref[:, cols]
            dsc_ref[:, cols] = _colsum(dy * _mm(pb, wg))
            dyp = (dy * sc_ref[:, cols]).astype(BF)
            dw_ref[g] = _mm_tn(pb, dyp)
            dp = _mm_nt(dyp, wg)
            dx_ref[:, cols] = (_pool_window(dp / cnt, g + 1, _su) - dp).astype(BF)

    return pl.pallas_call(
        body, name="bwd_pool", grid=(1,),
        in_specs=[pl.BlockSpec((T, DP), lambda i: (0, 2 * DR // DP)),
                  pl.BlockSpec((T, DP), lambda i: (0, 0)),
                  pl.BlockSpec((4, PG, PG), lambda i: (0, 0, 0)),
                  pl.BlockSpec((1, DP), lambda i: (0, 0))],
        out_specs=[pl.BlockSpec((T, DP), lambda i: (0, 0)),
                   pl.BlockSpec((4, PG, PG), lambda i: (0, 0, 0)),
                   pl.BlockSpec((1, DP), lambda i: (0, 0))],
        out_shape=_hbm_out([jax.ShapeDtypeStruct((T, DP), BF), jax.ShapeDtypeStruct((4, PG, PG), F32),
                            jax.ShapeDtypeStruct((1, DP), F32)]),
        compiler_params=_cp(48),
    )(*_hbm(proj, dypool, pool_w, pool_scale))


PART_COLS = (DR, DR, DP, 2 * D)


def _shard_pieces():
    starts = [sum(PART_COLS[:p]) for p in range(len(PART_COLS))]
    shards = []
    for k in range(NCHIP):
        lo, hi = k * CW_IN, (k + 1) * CW_IN
        shards.append([(p, max(lo, s) - s, min(hi, s + wd) - s, max(lo, s) - lo)
                       for p, (s, wd) in enumerate(zip(starts, PART_COLS)) if max(lo, s) < min(hi, s + wd)])
    return shards


def _bwd_inproj_w(h1, parts, after):
    def body(h_ref, p0, p1, p2, p3, after_ref, dw_ref):
        part_refs = (p0, p1, p2, p3)
        for k, pieces in enumerate(_shard_pieces()):
            for p, a, b, c0 in pieces:
                dw_ref[k, :, c0:c0 + b - a] = _mm_tn(h_ref[...], part_refs[p][:, a:b]).astype(BF)

    vmem = pl.BlockSpec(memory_space=pltpu.VMEM)
    return pl.pallas_call(
        body, name="bwd_inproj_w", in_specs=[vmem] * 5 + [ANY], out_specs=vmem,
        out_shape=pltpu.HBM((NCHIP, D, CW_IN), BF), compiler_params=_cp(48),
    )(*_hbm(h1, *parts), after)


def _bwd_inproj_x(parts, w_in, x, dxres, g1, stages=()):
    tm = 512

    def body(p0, p1, p2, p3, w_ref, x_ref, dr_ref, g_ref, dx_ref, dg_ref):
        part_refs = (p0, p1, p2, p3)
        dh = None
        for k, pieces in enumerate(_shard_pieces()):
            for p, a, b, c0 in pieces:
                part = _mm_nt(part_refs[p][:, a:b], w_ref[k, :, c0:c0 + b - a])
                dh = part if dh is None else dh + part
        xv = x_ref[...]
        r = lax.rsqrt(_mean(xv * xv) + NORM_EPS)
        xn = xv * r
        t = dh * g_ref[...]
        dx_ref[...] = dr_ref[...] + r * (t - xn * _mean(t * xn))
        _acc(dg_ref, _colsum(dh * xn), pl.program_id(0) == 0)

    row = pl.BlockSpec((tm, D), lambda i: (i, 0))
    vec = pl.BlockSpec((1, D), lambda i: (0, 0))
    return _call(
        body, name="bwd_inproj_x", grid=(T // tm,),
        in_specs=[pl.BlockSpec((tm, wd), lambda i: (i, 0)) for wd in PART_COLS] +
                 [pl.BlockSpec((NCHIP, D, CW_IN), lambda i: (0, 0, 0)), row, row, vec],
        out_specs=[row, vec],
        out_shape=[jax.ShapeDtypeStruct((T, D), F32), jax.ShapeDtypeStruct((1, D), F32)],
        vmem=56, args=[*parts, w_in, x, dxres, g1], stages=stages)[0]


def _place():
    x, y, c = lax.axis_index("x"), lax.axis_index("y"), lax.axis_index("c")
    chips = [(1 - x, y), (x, 1 - y), (1 - x, 1 - y)]
    return x, y, c, chips


def _rcopy(src, dst, ssem, rsem, dev):
    return pltpu.make_async_remote_copy(src_ref=src, dst_ref=dst, send_sem=ssem, recv_sem=rsem,
                                        device_id=dev, device_id_type=MESH_ID)


def _sds(a):
    return jax.ShapeDtypeStruct(a.shape, a.dtype)


def _sem2(n, m):
    return [pltpu.SemaphoreType.DMA((n * m,)), pltpu.SemaphoreType.DMA((n * m,))]


ALL = (0, 1, 1)


def _piece(ref, k, half, part):
    hr = ref.shape[1] // 2
    r0, r1 = hr * part[0] // part[2], hr * part[1] // part[2]
    return ref.at[k, pl.ds(half * hr + r0, r1 - r0), :]


def _gather(fulls, ici=(), d2d=(), peers=(0, 1, 2)):
    n = len(fulls)
    ici, d2d = list(ici), list(d2d)
    pieces = [("ici", i, part) for i, part in ici] + [("d2d", i, part) for i, part in d2d]

    def copies(outs, sems):
        x, y, c, chips = _place()
        me = 2 * x + y
        sib = (x, y, 1 - c)
        send, recv = [], []
        for q, (kind, i, part) in enumerate(pieces):
            for jj, j in enumerate(peers):
                chip = chips[j]
                k, s = 2 * chip[0] + chip[1], len(peers) * q + jj
                if kind == "ici":
                    mine, theirs, dev = _piece(outs[i], me, c, part), _piece(outs[i], k, c, part), (*chip, c)
                else:
                    mine, theirs, dev = _piece(outs[i], k, c, part), _piece(outs[i], k, 1 - c, part), sib
                send.append(_rcopy(mine, mine, sems[0].at[s], sems[1].at[s], dev))
                recv.append(_rcopy(theirs, theirs, sems[0].at[s], sems[1].at[s], dev))
        return send, recv

    def start(ins, outs, sems):
        for cp in copies(outs, sems)[0]:
            cp.start()

    def finish(ins, outs, sems):
        send, recv = copies(outs, sems)
        for cp in recv:
            cp.wait_recv()
        for cp in send:
            cp.wait_send()

    count = len(peers) * len(pieces)
    sems = [pltpu.SemaphoreType.DMA((count,)), pltpu.SemaphoreType.DMA((count,))]
    return _Stage(fulls, [_sds(f) for f in fulls], {i: i for i in range(n)}, sems, start, finish)


def _gather_whole(v):
    def copies(ins, outs, sems):
        x, y, c, chips = _place()
        me = 2 * x + y
        send = [_rcopy(ins[0], outs[0].at[me], sems[0].at[j], sems[1].at[j], (*chip, c))
                for j, chip in enumerate(chips)]
        recv = [_rcopy(ins[0], outs[0].at[2 * chip[0] + chip[1]], sems[0].at[j], sems[1].at[j], (*chip, c))
                for j, chip in enumerate(chips)]
        return send, recv

    def start(ins, outs, sems):
        for cp in copies(ins, outs, sems)[0]:
            cp.start()

    def finish(ins, outs, sems):
        send, recv = copies(ins, outs, sems)
        for cp in recv:
            cp.wait_recv()
        for cp in send:
            cp.wait_send()

    return _Stage([v], [jax.ShapeDtypeStruct((NCHIP,) + v.shape, v.dtype)], {},
                  [pltpu.SemaphoreType.DMA((3,)), pltpu.SemaphoreType.DMA((3,))], start, finish)


def _to_sibling(srcs):
    n = len(srcs)

    def copies(ins, outs, sems):
        x, y, c, _ = _place()
        sib = (x, y, 1 - c)
        return [_rcopy(ins[i].at[:, 1 - c] if srcs[i].ndim == 4 else ins[i], outs[i], sems[0].at[i], sems[1].at[i], sib)
                for i in range(n)]

    def start(ins, outs, sems):
        for cp in copies(ins, outs, sems):
            cp.start()

    def finish(ins, outs, sems):
        for cp in copies(ins, outs, sems):
            cp.wait()

    shapes = [jax.ShapeDtypeStruct((NCHIP,) + s.shape[2:] if s.ndim == 4 else s.shape, s.dtype) for s in srcs]
    return _Stage(srcs, shapes, {}, [pltpu.SemaphoreType.DMA((n,)), pltpu.SemaphoreType.DMA((n,))], start, finish)


def _to_chips(srcs, parts=None, lands=None):
    n = len(srcs)
    parts = [ALL] * n if parts is None else parts
    lands = [None] * n if lands is None else lands
    given = [i for i in range(n) if lands[i] is not None]

    def rows(ref, i):
        hr = srcs[i].shape[1]
        r0, r1 = hr * parts[i][0] // parts[i][2], hr * parts[i][1] // parts[i][2]
        return ref.at[pl.ds(r0, r1 - r0), :]

    def copies(ins, outs, sems):
        x, y, c, chips = _place()
        me = 2 * x + y
        return [_rcopy(rows(ins[i].at[2 * chip[0] + chip[1]] if srcs[i].shape[0] == NCHIP else ins[i].at[c], i),
                       rows(outs[i].at[me], i), sems[0].at[3 * i + j], sems[1].at[3 * i + j], (*chip, c))
                for i in range(n) for j, chip in enumerate(chips)]

    def start(ins, outs, sems):
        for cp in copies(ins, outs, sems):
            cp.start()

    def finish(ins, outs, sems):
        for cp in copies(ins, outs, sems):
            cp.wait()

    shapes = [jax.ShapeDtypeStruct((NCHIP,) + s.shape[1:], s.dtype) for s in srcs]
    alias = {n + q: i for q, i in enumerate(given)}
    return _Stage(list(srcs) + [lands[i] for i in given], shapes, alias, _sem2(n, 3), start, finish)


HBM_REF = pl.BlockSpec(memory_space=pltpu.HBM)
SEM_REF = pl.BlockSpec(memory_space=pltpu.SEMAPHORE)
DATAFLOW = pltpu.SideEffectType.DATAFLOW_SIDE_EFFECTING


def _after(x):
    return _Stage([x], [], {}, [], lambda *a: None, lambda *a: None)


class _Flight:
    def __init__(self, stage, sems, bufs):
        self.stage, self.sems, self.bufs = stage, list(sems), list(bufs)

    def landed(self):
        st, n = self.stage, len(self.stage.operands)
        fresh = [j for j in range(len(st.out_shape)) if j not in st.alias.values()]
        back = {v: k for k, v in st.alias.items()}
        return [self.bufs[back[j]] if j in back else self.bufs[n + fresh.index(j)] for j in range(len(st.out_shape))]


def _split_call(name, finish=(), start=(), after=None, carry=()):
    bufs, stage_bufs = [], []

    def slot(a):
        for i, b in enumerate(bufs):
            if b is a:
                return i
        bufs.append(a)
        return len(bufs) - 1

    fin_slots = [[slot(b) for b in fl.bufs] for fl in list(finish) + list(carry)]
    for st in start:
        fresh = [lax.empty(o.shape, o.dtype) for j, o in enumerate(st.out_shape) if j not in st.alias.values()]
        stage_bufs.append([slot(a) for a in list(st.operands) + fresh])
    old_sems = [s for fl in finish for s in fl.sems]
    new_sems = [s for st in start for s in st.sems]
    nb, no, nn = len(bufs), len(old_sems), len(new_sems)

    def refs_of(st, slots, buf_refs):
        n = len(st.operands)
        ins = [buf_refs[i] for i in slots[:n]]
        fresh = [j for j in range(len(st.out_shape)) if j not in st.alias.values()]
        back = {v: k for k, v in st.alias.items()}
        outs = [ins[back[j]] if j in back else buf_refs[slots[n + fresh.index(j)]] for j in range(len(st.out_shape))]
        return ins, outs

    def body(*refs):
        buf_refs, sem_in = refs[:nb], refs[nb:nb + no]
        sem_out = refs[nb + no + (after is not None):][:nn]
        token = refs[-1]
        pos = 0
        for fl, slots in zip(finish, fin_slots):
            ins, outs = refs_of(fl.stage, slots, buf_refs)
            fl.stage.finish(ins, outs, sem_in[pos:pos + len(fl.sems)])
            pos += len(fl.sems)
        pos = 0
        for st, slots in zip(start, stage_bufs):
            ins, outs = refs_of(st, slots, buf_refs)
            st.start(ins, outs, sem_out[pos:pos + len(st.sems)])
            pos += len(st.sems)
        token[...] = jnp.zeros_like(token)

    res = pl.pallas_call(
        body, name=name,
        out_shape=tuple(new_sems) + tuple(pltpu.HBM(b.shape, b.dtype) for b in bufs) +
                  (jax.ShapeDtypeStruct((8, LANE), F32),),
        in_specs=(HBM_REF,) * nb + (SEM_REF,) * no + ((pl.BlockSpec(memory_space=pl.ANY),) if after is not None else ()),
        out_specs=(SEM_REF,) * nn + (HBM_REF,) * nb + (pl.BlockSpec(memory_space=pltpu.VMEM),),
        input_output_aliases={i: nn + i for i in range(nb)},
        compiler_params=pltpu.CompilerParams(has_side_effects=DATAFLOW),
    )(*_hbm(*bufs), *old_sems, *([after] if after is not None else []))
    sems, thru, token = res[:nn], res[nn:nn + nb], res[-1]
    for fl, slots in zip(list(finish) + list(carry), fin_slots):
        fl.bufs = [thru[i] for i in slots]
    flights, pos = [], 0
    for st, slots in zip(start, stage_bufs):
        flights.append(_Flight(st, sems[pos:pos + len(st.sems)], [thru[i] for i in slots]))
        pos += len(st.sems)
    return flights, token


def _share(pairs):
    n = len(pairs)

    def start(ins, outs, sems):
        x, y, c, _ = _place()
        for i in range(n):
            _rcopy(outs[i].at[c], outs[i].at[c], sems[0].at[i], sems[1].at[i], (x, y, 1 - c)).start()

    def finish(ins, outs, sems):
        x, y, c, _ = _place()
        for i in range(n):
            _rcopy(outs[i].at[c], outs[i].at[c], sems[0].at[i], sems[1].at[i], (x, y, 1 - c)).wait_send()
            _rcopy(outs[i].at[1 - c], outs[i].at[1 - c], sems[0].at[i], sems[1].at[i], (x, y, 1 - c)).wait_recv()

    return _Stage(pairs, [_sds(p) for p in pairs], {i: i for i in range(n)},
                  [pltpu.SemaphoreType.DMA((n,)), pltpu.SemaphoreType.DMA((n,))], start, finish)


def _row_block(rows, cols, itemsize=4, target=2 * MIB):
    br = rows
    while br * cols * itemsize > target and br % 32 == 0:
        br //= 2
    return br


def _cast_place(w, chip_idx, name):
    rows, cols = w.shape
    br = _row_block(rows, cols)

    def body(k_ref, w_ref, o_ref):
        o_ref[0] = w_ref[...].astype(BF)

    return _call(
        body, name=name, grid=(rows // br,), prefetch=chip_idx,
        in_specs=[pl.BlockSpec((br, cols), lambda r, k: (r, 0))],
        out_specs=[pl.BlockSpec((1, br, cols), lambda r, k: (k[0], r, 0))],
        out_shape=[jax.ShapeDtypeStruct((NCHIP, rows, cols), BF)], vmem=32, args=[w])[0][0]


def _cast_place_multi(ws, chip_idx, stages=()):
    br = 128
    nblk = [a.shape[0] // br for a in ws]
    starts = [sum(nblk[:i]) for i in range(len(ws))]

    def body(k_ref, *refs):
        r = pl.program_id(0)
        for i in range(len(ws)):
            @pl.when(jnp.logical_and(r >= starts[i], r < starts[i] + nblk[i]))
            def _(i=i):
                refs[len(ws) + i][0] = refs[i][...].astype(BF)

    def at(i):
        return functools.partial(lambda r, s, nb: jnp.clip(r - s, 0, nb - 1), s=starts[i], nb=nblk[i])

    outs, landed = _call(
        body, name="cast_rest", grid=(sum(nblk),), prefetch=chip_idx,
        in_specs=[pl.BlockSpec((br, a.shape[1]), functools.partial(lambda r, k, f: (f(r), 0), f=at(i)))
                  for i, a in enumerate(ws)],
        out_specs=[pl.BlockSpec((1, br, a.shape[1]), functools.partial(lambda r, k, f: (k[0], f(r), 0), f=at(i)))
                   for i, a in enumerate(ws)],
        out_shape=[jax.ShapeDtypeStruct((NCHIP,) + a.shape, BF) for a in ws], vmem=32, args=list(ws), stages=stages)
    return outs, landed


def _add_sibling(g, land, cidx, name, stages=()):
    _, _, hr, cols = g.shape
    br = _row_block(hr, cols)

    def body(c_ref, g_ref, l_ref, o_ref):
        o_ref[...] = (g_ref[0, 0].astype(F32) + l_ref[0].astype(F32)).astype(BF)[None]

    outs, st = _call(
        body, name=name, grid=(NCHIP, hr // br), prefetch=cidx,
        in_specs=[pl.BlockSpec((1, 1, br, cols), lambda k, r, c: (k, c[0], r, 0)),
                  pl.BlockSpec((1, br, cols), lambda k, r, c: (k, r, 0))],
        out_specs=[pl.BlockSpec((1, br, cols), lambda k, r, c: (k, r, 0))],
        out_shape=[jax.ShapeDtypeStruct((NCHIP, hr, cols), BF)], vmem=32, args=[g, land], stages=stages)
    return outs[0], st


def _add_sibling_multi(gs, lands, cidx, name):
    n = len(gs)
    brs = [_row_block(g.shape[2], g.shape[3]) for g in gs]
    nrb = [g.shape[2] // b for g, b in zip(gs, brs)]
    nblk = [NCHIP * q for q in nrb]
    starts = [sum(nblk[:i]) for i in range(n)]

    def body(c_ref, *refs):
        r = pl.program_id(0)
        for i in range(n):
            g_ref, l_ref, o_ref = refs[2 * i], refs[2 * i + 1], refs[2 * n + i]

            @pl.when(jnp.logical_and(r >= starts[i], r < starts[i] + nblk[i]))
            def _():
                o_ref[...] = (g_ref[0, 0].astype(F32) + l_ref[0].astype(F32)).astype(BF)[None]

    def at(i, r):
        q = jnp.clip(r - starts[i], 0, nblk[i] - 1)
        return q // nrb[i], q % nrb[i]

    def g_spec(i):
        return pl.BlockSpec((1, 1, brs[i], gs[i].shape[3]),
                            functools.partial(lambda r, c, i: (at(i, r)[0], c[0], at(i, r)[1], 0), i=i))

    def l_spec(i):
        return pl.BlockSpec((1, brs[i], gs[i].shape[3]),
                            functools.partial(lambda r, c, i: (at(i, r)[0], at(i, r)[1], 0), i=i))

    return _call(
        body, name=name, grid=(sum(nblk),), prefetch=cidx,
        in_specs=[s for i in range(n) for s in (g_spec(i), l_spec(i))], out_specs=[l_spec(i) for i in range(n)],
        out_shape=[jax.ShapeDtypeStruct(l.shape, BF) for l in lands], vmem=32,
        args=[a for i in range(n) for a in (gs[i], lands[i])])[0]


def _add_pair(a, b, name):
    rows, cols = a.shape

    def body(a_ref, b_ref, o_ref):
        o_ref[...] = a_ref[...] + b_ref[...]

    spec = pl.BlockSpec((rows, cols), lambda r: (0, 0))
    return _call(body, name=name, grid=(1,), in_specs=[spec, spec], out_specs=[spec], out_shape=[_sds(a)],
                 vmem=32, args=[a, b])[0][0]


def _add_chips(own, land, idx, name, stages=None):
    _, hr, cols = land.shape
    br = _row_block(hr, cols)

    def body(s_ref, a_ref, b_ref, c_ref, d_ref, o_ref):
        o_ref[...] = (a_ref[...].astype(F32) + b_ref[...].astype(F32)) + (c_ref[...].astype(F32) +
                                                                           d_ref[...].astype(F32))

    spec = lambda q: pl.BlockSpec((1, br, cols), functools.partial(lambda r, s, q: (s[q], r, 0), q=q))
    outs, landed = _call(
        body, name=name, grid=(hr // br,), prefetch=idx,
        in_specs=[spec(0), spec(1), spec(2), spec(3)], out_specs=[spec(4)],
        out_shape=[jax.ShapeDtypeStruct((2, hr, cols), F32)], vmem=48, args=[own, land, land, land],
        stages=stages or ())
    return outs[0] if stages is None else (outs[0], landed)


def _add_chips_multi(owns, lands, idx, name, stages=()):
    n = len(owns)
    brs = [_row_block(l.shape[1], l.shape[2]) for l in lands]
    nblk = [l.shape[1] // b for l, b in zip(lands, brs)]
    starts = [sum(nblk[:i]) for i in range(n)]

    def body(s_ref, *refs):
        r = pl.program_id(0)
        for i in range(n):
            a_ref, b_ref, c_ref, d_ref = refs[4 * i:4 * i + 4]
            o_ref = refs[4 * n + i]

            @pl.when(jnp.logical_and(r >= starts[i], r < starts[i] + nblk[i]))
            def _():
                o_ref[...] = (a_ref[...].astype(F32) + b_ref[...].astype(F32)) + (c_ref[...].astype(F32) +
                                                                                   d_ref[...].astype(F32))

    def spec(i, q):
        return pl.BlockSpec((1, brs[i], lands[i].shape[2]), functools.partial(
            lambda r, s, q, st, nb: (s[q], jnp.clip(r - st, 0, nb - 1), 0), q=q, st=starts[i], nb=nblk[i]))

    outs, landed = _call(
        body, name=name, grid=(sum(nblk),), prefetch=idx,
        in_specs=[spec(i, q) for i in range(n) for q in range(4)], out_specs=[spec(i, 4) for i in range(n)],
        out_shape=[jax.ShapeDtypeStruct((2,) + l.shape[1:], F32) for l in lands], vmem=48,
        args=[a for i in range(n) for a in (owns[i], lands[i], lands[i], lands[i])], stages=stages)
    return outs, landed


def _adamw_math(w, g, m, v):
    mn = ADAM_B1 * m + (1.0 - ADAM_B1) * g
    vn = ADAM_B2 * v + (1.0 - ADAM_B2) * (g * g)
    m_hat = mn / (1.0 - ADAM_B1 ** ADAM_STEP)
    v_hat = vn / (1.0 - ADAM_B2 ** ADAM_STEP)
    return -ADAM_LR * (m_hat / (jnp.sqrt(v_hat) + ADAM_EPS) + ADAM_WD * w), mn, vn


def _adamw(w, g, m, v, name, stages=()):
    rows, cols = w.shape
    br = _row_block(rows, cols)

    def body(w_ref, g_ref, m_ref, v_ref, go_ref, d_ref, mo_ref, vo_ref):
        gv = g_ref[...]
        go_ref[...] = gv
        d_ref[...], mo_ref[...], vo_ref[...] = _adamw_math(w_ref[...], gv, m_ref[...], v_ref[...])

    spec = pl.BlockSpec((br, cols), lambda r: (r, 0))
    return _call(body, name=name, grid=(rows // br,), in_specs=[spec] * 4, out_specs=[spec] * 4,
                 out_shape=[_sds(w)] * 4, vmem=56, args=[w, g, m, v], stages=stages)


def _adamw_multi(names, w, g, m, v, stages=()):
    cols = w[names[0]].shape[1]
    br = 128
    nblk = [w[n].shape[0] // br for n in names]
    starts = [sum(nblk[:i]) for i in range(len(names))]

    def body(*refs):
        r = pl.program_id(0)
        for i in range(len(names)):
            w_ref, g_ref, m_ref, v_ref = refs[4 * i:4 * i + 4]
            go_ref, d_ref, mo_ref, vo_ref = refs[4 * len(names) + 4 * i:4 * len(names) + 4 * i + 4]

            @pl.when(jnp.logical_and(r >= starts[i], r < starts[i] + nblk[i]))
            def _():
                gv = g_ref[...]
                go_ref[...] = gv
                d_ref[...], mo_ref[...], vo_ref[...] = _adamw_math(w_ref[...], gv, m_ref[...], v_ref[...])

    def spec(i):
        return pl.BlockSpec((br, cols), functools.partial(
            lambda r, s, nb: (jnp.clip(r - s, 0, nb - 1), 0), s=starts[i], nb=nblk[i]))

    outs, landed = _call(
        body, name="adamw_" + "_".join(names), grid=(sum(nblk),),
        in_specs=[spec(i) for i in range(len(names)) for _ in range(4)],
        out_specs=[spec(i) for i in range(len(names)) for _ in range(4)],
        out_shape=[_sds(w[n]) for n in names for _ in range(4)], vmem=56,
        args=[a[n] for n in names for a in (w, g, m, v)], stages=stages)
    return {n: outs[4 * i:4 * i + 4] for i, n in enumerate(names)}, landed


def _to_everyone(v):
    deltas = [(a, b, e) for a in (0, 1) for b in (0, 1) for e in (0, 1)][1:]

    def copies(ins, outs, sems):
        x, y, c, _ = _place()
        me = 4 * x + 2 * y + c
        flip = lambda p, f: 1 - p if f else p
        return [_rcopy(ins[0], outs[0].at[me], sems[0].at[q], sems[1].at[q], (flip(x, a), flip(y, b), flip(c, e)))
                for q, (a, b, e) in enumerate(deltas)]

    def start(ins, outs, sems):
        for cp in copies(ins, outs, sems):
            cp.start()

    def finish(ins, outs, sems):
        for cp in copies(ins, outs, sems):
            cp.wait()

    n = len(deltas)
    return _Stage([v], [jax.ShapeDtypeStruct((2 * NCHIP,) + v.shape, v.dtype)], {},
                  [pltpu.SemaphoreType.DMA((n,)), pltpu.SemaphoreType.DMA((n,))], start, finish)


SMALL_AT = {"norm_mix_pre": (0, 1, D), "norm_mix_post": (1, 1, D), "norm_mlp_pre": (2, 1, D),
            "norm_mlp_post": (3, 1, D), "b_gate": (4, 2, D), "conv_b": (6, 1, D), "lru_b_a": (7, 1, D),
            "lru_b_x": (8, 1, D), "lru_lambda": (9, 1, D), "pool_scale": (10, 1, DP)}
SMALL_SEPARATE = ["conv_w", "lru_w_a", "lru_w_x", "pool_w"]


def _adamw_small(small_sum, first_all, sep_grads, w, m, v):
    packed, sep = list(SMALL_AT), list(SMALL_SEPARATE)
    names = packed + sep

    def body(*refs):
        s_ref, a_ref, refs = refs[0], refs[1], refs[2:]
        g_sep, refs = refs[:len(sep)], refs[len(sep):]
        nn = len(names)
        w_r, m_r, v_r, refs = refs[:nn], refs[nn:2 * nn], refs[2 * nn:3 * nn], refs[3 * nn:]
        g_out, refs = refs[:len(packed)], refs[len(packed):]
        d_o, m_o, v_o = refs[:nn], refs[nn:2 * nn], refs[2 * nn:3 * nn]
        for i, n in enumerate(names):
            if i == 0:
                g = a_ref[0:1, :]
                for q in range(1, 2 * NCHIP):
                    g = g + a_ref[q:q + 1, :]
                g_out[i][...] = g
            elif n in SMALL_AT:
                r0, nr, nc = SMALL_AT[n]
                g = jnp.concatenate([s_ref[r0 + q:r0 + q + 1, :nc] for q in range(nr)], axis=1)
                g_out[i][...] = g
            else:
                g = g_sep[i - len(packed)][...]
            d_o[i][...], m_o[i][...], v_o[i][...] = _adamw_math(w_r[i][...], g, m_r[i][...], v_r[i][...])

    ws = [w[n] for n in names]
    res = pl.pallas_call(
        body, name="adamw_small",
        out_shape=[_sds(w[n]) for n in packed] + [_sds(a) for a in ws] * 3,
        compiler_params=_cp(32),
    )(*_hbm(small_sum, first_all, *sep_grads, *ws, *[m[n] for n in names], *[v[n] for n in names]))
    nn, npk = len(names), len(packed)
    grad = dict(zip(packed, res[:npk]))
    delta = dict(zip(names, res[npk:npk + nn]))
    new_m = dict(zip(names, res[npk + nn:npk + 2 * nn]))
    new_v = dict(zip(names, res[npk + 2 * nn:]))
    return grad, delta, new_m, new_v


W_NAMES = ["norm_mix_pre", "norm_mix_post", "norm_mlp_pre", "norm_mlp_post", "w_in", "b_gate", "conv_w", "conv_b",
           "lru_w_a", "lru_b_a", "lru_w_x", "lru_b_x", "lru_lambda", "pool_w", "pool_scale", "w_lru_up",
           "w_pool_up", "w_o", "w_ff1", "w_ff2"]
BIG = ["w_in", "w_lru_up", "w_pool_up", "w_o", "w_ff1", "w_ff2"]


def _block_diag(w):
    hd = w.shape[-1]
    per = CB // hd
    w4 = w.reshape(NG, per, hd, hd)
    eye = jnp.eye(per, dtype=w.dtype)
    return jnp.einsum("gpij,pq->gpiqj", w4, eye).reshape(NG, CB, CB)


def _block_diag_extract(d, hd):
    per = CB // hd
    d5 = d.reshape(NG, per, hd, per, hd)
    return jnp.stack([d5[:, p, :, p, :] for p in range(per)], axis=1).reshape(NG * per, hd, hd)


def _halves(g):
    return g.reshape(NCHIP, 2, g.size // (g.shape[-1] * 2 * NCHIP), g.shape[-1])


def kernel(x, norm_mix_pre, norm_mix_post, norm_mlp_pre, norm_mlp_post, w_in, b_gate, conv_w, conv_b, lru_w_a, lru_b_a, lru_w_x, lru_b_x, lru_lambda, pool_w, pool_scale, w_lru_up, w_pool_up, w_o, w_ff1, w_ff2, loss_target, m_norm_mix_pre, m_norm_mix_post, m_norm_mlp_pre, m_norm_mlp_post, m_w_in, m_b_gate, m_conv_w, m_conv_b, m_lru_w_a, m_lru_b_a, m_lru_w_x, m_lru_b_x, m_lru_lambda, m_pool_w, m_pool_scale, m_w_lru_up, m_w_pool_up, m_w_o, m_w_ff1, m_w_ff2, v_norm_mix_pre, v_norm_mix_post, v_norm_mlp_pre, v_norm_mlp_post, v_w_in, v_b_gate, v_conv_w, v_conv_b, v_lru_w_a, v_lru_b_a, v_lru_w_x, v_lru_b_x, v_lru_lambda, v_pool_w, v_pool_scale, v_w_lru_up, v_w_pool_up, v_w_o, v_w_ff1, v_w_ff2):
    args = dict(locals())
    two_d = lambda a: a.reshape(-1, a.shape[-1])
    w = {n: two_d(args[n]) for n in W_NAMES}
    mom = {n: two_d(args["m_" + n]) for n in W_NAMES}
    var = {n: two_d(args["v_" + n]) for n in W_NAMES}
    i32 = lambda val: jnp.asarray(val, jnp.int32)
    chip = i32(2 * lax.axis_index("x") + lax.axis_index("y"))
    core = i32(lax.axis_index("c"))
    cidx = core.reshape(1)
    zero = i32(0)
    hd = lru_w_a.shape[-1]
    xs, target = x[0], loss_target[0]
    g1, g2, g3, g4 = norm_mix_pre, norm_mix_post, norm_mlp_pre, norm_mlp_post

    mix = ["w_lru_up", "w_pool_up", "w_o"]
    full = {"w_in": _cast_place(w["w_in"], chip.reshape(1), "cast_w_in")}
    near, far = (0, 1), (2,)
    (fl_in, fl_far, fl_conv), first = _split_call("gather_start_first", start=[
        _gather([full["w_in"]], ici=[(0, ALL)], peers=near), _gather([full["w_in"]], ici=[(0, ALL)], peers=far),
        _gather_whole(w["conv_w"])])
    casts, _ = _cast_place_multi([w[n] for n in BIG[1:]], chip.reshape(1), stages=[_after(first)])
    full.update(zip(BIG[1:], casts))
    (fl_mix, fl_ff1, fl_ff2), started = _split_call("gather_start_rest", start=[
        _gather([full[n] for n in mix], ici=[(0, ALL), (1, ALL), (2, ALL)]),
        _gather([full["w_ff1"]], ici=[(0, ALL)]), _gather([full["w_ff2"]], ici=[(0, ALL)])])
    wa = _block_diag(lru_w_a[0]).astype(BF)
    wx = _block_diag(lru_w_x[0]).astype(BF)
    pw = pool_w[0].astype(BF)

    def to_sibling(name, flight, after=None, peers=(0, 1, 2), carry=()):
        pieces = [(i, ALL) for i in range(len(flight.bufs))]
        (fl,), passed = _split_call(name + "_pass", finish=[flight], after=after, carry=carry,
                                    start=[_gather(flight.landed(), d2d=pieces, peers=peers)])
        passed_on.append(passed)
        return fl

    passed_on = []

    def arrived(name, flight, after=None):
        _split_call(name + "_done", finish=[flight], after=after)
        return flight.landed()

    idx_big = jnp.stack([chip, (chip + 1) % NCHIP, (chip + 2) % NCHIP, (chip + 3) % NCHIP, core])
    proj, h1 = _fwd_inproj_own(xs, g1, fl_in.bufs[0], idx_big, stages=[_after(started)])
    fl_in = to_sibling("gather_w_in", fl_in, after=h1, peers=near, carry=[fl_far])
    _split_call("gather_w_in_done", finish=[fl_in, fl_conv], carry=[fl_far])
    conv_all, = fl_conv.landed()
    conv_all = lax.dynamic_update_slice(conv_all, w["conv_w"][None], (chip, zero, zero))
    conv_full = jnp.transpose(conv_all, (1, 0, 2)).reshape(4, DR)
    proj = _fwd_inproj_rest(h1, fl_far.bufs[0], proj, jnp.stack([chip ^ 2, chip ^ 1]), "fwd_inproj_near")
    fl_far = to_sibling("gather_w_in_far", fl_far, after=proj, peers=far)
    w_in_f, = arrived("gather_w_in_far", fl_far)
    full["w_in"] = w_in_f
    proj = _fwd_inproj_rest(h1, w_in_f, proj, (chip ^ 3).reshape(1), "fwd_inproj_far")
    fl_mix = to_sibling("gather_mix", fl_mix, after=proj)
    (ylru, hs), _ = _fwd_lru(proj, conv_full, conv_b, wa, lru_b_a, wx, lru_b_x, lru_lambda,
                             stages=[_after(passed_on[-1])])
    got = arrived("gather_mix", fl_mix, after=ylru)
    fl_ff1 = to_sibling("gather_ff1", fl_ff1, after=ylru)
    w_lru_up_f, w_pool_up_f, w_o_f = got[0].reshape(DR, D), got[1], got[2].reshape(D, D)
    ypool = _fwd_pool(proj, pw, pool_scale)
    (x2, h2, m, mrg, bra, brb), _ = _fwd_merge(xs, ylru, ypool, proj, b_gate, g2, g3, w_lru_up_f, w_pool_up_f, w_o_f,
                                               stages=[_after(passed_on[-1])])
    fl_ff2 = to_sibling("gather_ff2", fl_ff2, after=h2)
    _split_call("gather_ff_done", finish=[fl_ff1, fl_ff2])
    (ff1,), (ff2,) = fl_ff1.landed(), fl_ff2.landed()
    ff2 = ff2.reshape(DF, D)
    a1, lossp, dy, df, dg4 = _fwd_mlp_loss(h2, ff1, ff2, x2, target, g4)

    dh2, df1 = _bwd_mlp_x(df, a1, ff1, ff2)
    dw_ff1, dw_ff2 = _bwd_mlp_w(df, h2, a1, df1)
    g_ff = [_halves(dw_ff1), _halves(dw_ff2)]
    (dxres, dgates, dylru, dypool, dm, dbra, dbrb, dg2, dg3, dbg), (l_ff,) = _bwd_merge(
        dh2, dy, x2, m, bra, brb, proj, b_gate, g2, g3, w_lru_up_f, w_pool_up_f, w_o_f, stages=[_to_sibling(g_ff)])
    p_ff = _add_sibling_multi(g_ff, l_ff, cidx, "add_sibling_ff")
    (fl_ff,), sent_ff = _split_call("reduce_ff_start", start=[_to_chips(p_ff)])
    (dw_o, dw_lru_up, dw_pool_up), _ = _dw_merge(mrg, dm, ylru, dbra, ypool, dbrb, stages=[_after(sent_ff)])
    g_mix = [_halves(dw_lru_up), _halves(dw_pool_up), _halves(dw_o)]
    (dxp, dgl, dcw, dcb, dwa, dba, dwx, dbx, dlam), (l_mix,) = _bwd_lru(
        proj, hs, dylru, conv_full, conv_b, wa, lru_b_a, wx, lru_b_x, lru_lambda, stages=[_to_sibling(g_mix)])
    p_mix = _add_sibling_multi(g_mix, l_mix, cidx, "add_sibling_mix")
    dxpool, dpw, dsc = _bwd_pool(proj, dypool, pw, pool_scale)
    dproj = [dxp, dgl, dxpool, dgates]
    small = jnp.concatenate([
        jnp.zeros((1, D), F32), dg2, dg3, dg4, dbg.reshape(2, D), dcb, dba, dbx, dlam,
        jnp.pad(dsc, ((0, 0), (0, D - DP))), jnp.pad(lossp, ((0, 0), (0, D - 1))), dcw,
        _block_diag_extract(dwa, hd).reshape(-1, D), _block_diag_extract(dwx, hd).reshape(-1, D),
        dpw.reshape(-1, D)], axis=0)
    (fl_mixr, fl_smalls), sent_mix = _split_call("reduce_mix_start", start=[_to_chips(p_mix), _to_sibling([small])])
    dw_in = _bwd_inproj_w(h1, dproj, sent_mix)
    _split_call("reduce_small_sibling_done", finish=[fl_smalls], after=dw_in)
    small, l_small = fl_smalls.bufs
    small2 = _add_pair(small, l_small, "add_sibling_small").reshape(2, SMALL_ROWS // 2, D)
    g_in = _halves(dw_in)
    done = ["w_ff1", "w_ff2"] + mix
    (fl_gin, fl_small), sib_started = _split_call("reduce_in_sibling_start", finish=[fl_ff, fl_mixr],
                                                  start=[_to_sibling([g_in]), _to_chips([small2])])
    p_ff1, p_ff2, c_ff1, c_ff2 = fl_ff.bufs
    p_mix, c_mix = fl_mixr.bufs[:3], fl_mixr.bufs[3:]
    pairs, _ = _add_chips_multi([p_ff1, p_ff2] + p_mix, [c_ff1, c_ff2] + c_mix, idx_big, "add_chips_done",
                                stages=[_after(sib_started)])
    _split_call("reduce_in_sibling_done", finish=[fl_gin], after=pairs[-1])
    g_in, l_in = fl_gin.bufs
    p_in = _add_sibling(g_in, l_in, cidx, "add_sibling_w_in")[0]
    (fl_pin,), token = _split_call("reduce_last_start", start=[_to_chips([p_in])])
    _split_call("reduce_small_done", finish=[fl_small], after=token)
    small2, c_small = fl_small.bufs
    own_small = lax.dynamic_index_in_dim(small2, core, 0, keepdims=True)
    c_small = lax.dynamic_update_slice(c_small, own_small, (chip, zero, zero))
    pair_small = _add_chips(c_small, c_small, jnp.stack([zero, zero + 1, zero + 2, zero + 3, core]), "add_chips_small")
    (fl_share,), shared_start = _split_call("reduce_share_start", start=[_share(pairs + [pair_small])])
    grad_x, dg1 = _bwd_inproj_x(dproj, full["w_in"], xs, dxres, g1, stages=[_after(shared_start)])
    _split_call("reduce_share_done", finish=[fl_share, fl_pin], after=dg1)
    shared, (p_in, c_in) = fl_share.landed(), fl_pin.bufs
    pairs, pair_small = shared[:-1], shared[-1]

    grads, delta, new_m, new_v = {}, {}, {}, {}
    for n, p in zip(done, pairs):
        grads[n] = p.reshape(-1, p.shape[-1])

    def update(n, stages=()):
        (grads[n], delta[n], new_m[n], new_v[n]), landed = _adamw(w[n], grads[n], mom[n], var[n], "adamw_" + n,
                                                                  stages=stages)
        return landed

    pair_in = _add_chips(p_in, c_in, idx_big, "add_chips_w_in")
    (fl_last, fl_dg1), last_start = _split_call("reduce_last_share_start", start=[_share([pair_in]), _to_everyone(dg1)])
    updated, _ = _adamw_multi(["w_ff1", "w_ff2", "w_o", "w_lru_up"], w, grads, mom, var, stages=[_after(last_start)])
    for n, (go, d, mo, vo) in updated.items():
        grads[n], delta[n], new_m[n], new_v[n] = go, d, mo, vo
    _split_call("reduce_last_share_done", finish=[fl_last, fl_dg1], after=new_v["w_lru_up"])
    (pair_in,), (dg1, dg1_all) = fl_last.landed(), fl_dg1.bufs
    dg1_all = lax.dynamic_update_slice(dg1_all, dg1[None], (2 * chip + core, zero, zero)).reshape(2 * NCHIP, D)
    grads["w_in"] = pair_in.reshape(-1, pair_in.shape[-1])
    update("w_pool_up")
    update("w_in")
    small_sum = pair_small.reshape(SMALL_ROWS, D)
    loss = 0.5 * small_sum[LOSS_ROW, 0]
    ccols = DR // NCHIP
    sep = [lax.dynamic_slice(small_sum[12:16], (zero, chip * ccols), (4, ccols)),
           small_sum[16:80].reshape(-1, hd), small_sum[80:144].reshape(-1, hd), small_sum[144:208].reshape(-1, PG)]
    g_s, d_s, m_s, v_s = _adamw_small(small_sum, dg1_all, sep, w, mom, var)
    grads.update(g_s)
    grads.update(dict(zip(SMALL_SEPARATE, sep)))
    delta.update(d_s)
    new_m.update(m_s)
    new_v.update(v_s)

    out = lambda d: [d[n].reshape(args[n].shape) for n in W_NAMES]
    return (loss, grad_x[None], *out(grads), *out(delta), *out(new_m), *out(new_v))
```

```python
import functools
import math

import jax
import jax.numpy as jnp
from jax import lax
from jax.experimental import pallas as pl
from jax.experimental.pallas import tpu as pltpu

F32 = jnp.float32
BF = jnp.bfloat16

T = 2048
D = 1024
DR = 1024
DP = 512
DF = 4096
DIN = 4608
NCHIP = 4
CW_IN = DIN // NCHIP
LANE = 128
CB = 128
NG = DR // CB
PG = 128
POOL_WINDOWS = (2, 4, 8, 16)
NORM_EPS = 1e-6
LRU_C = 8.0
GELU_C = math.sqrt(2.0 / math.pi)
ADAM_LR = 0.001
ADAM_B1 = 0.9
ADAM_B2 = 0.999
ADAM_EPS = 1e-08
ADAM_WD = 0.01
ADAM_STEP = 10
MESH_ID = pl.DeviceIdType.MESH
ANY = pl.BlockSpec(memory_space=pl.ANY)
SMALL_ROWS = 208
LOSS_ROW = 11
MIB = 1 << 20


def _cp(vmem_mib=None):
    if vmem_mib is None:
        return pltpu.CompilerParams()
    return pltpu.CompilerParams(vmem_limit_bytes=vmem_mib * MIB)


def _hbm(*arrays):
    return [pltpu.with_memory_space_constraint(a, pltpu.HBM) for a in arrays]


def _hbm_out(shapes):
    return [pltpu.HBM(s.shape, s.dtype) for s in shapes]


class _Stage:
    def __init__(self, operands, out_shape, alias, sems, start, finish):
        self.operands, self.out_shape, self.alias, self.sems = list(operands), list(out_shape), dict(alias), list(sems)
        self.start, self.finish = start, finish


def _call(body, *, name, grid, in_specs, out_specs, out_shape, args, vmem=None, stages=(), prefetch=None,
          scratch=()):
    nin, nout = len(in_specs), len(out_specs)
    npre = 0 if prefetch is None else 1
    st_args, st_shapes, st_sems, aliases = [], [], list(scratch), {}
    for st in stages:
        for k, v in st.alias.items():
            aliases[npre + nin + len(st_args) + k] = nout + len(st_shapes) + v
        st_args += st.operands
        st_shapes += st.out_shape
        st_sems += st.sems

    def wrapped(*refs):
        pre, refs = refs[:npre], refs[npre:]
        ins, pos = refs[:nin], nin
        st_ins = []
        for st in stages:
            st_ins.append(refs[pos:pos + len(st.operands)])
            pos += len(st.operands)
        outs, pos = refs[pos:pos + nout], pos + nout
        st_outs = []
        for st in stages:
            st_outs.append(refs[pos:pos + len(st.out_shape)])
            pos += len(st.out_shape)
        work, pos = refs[pos:pos + len(scratch)], pos + len(scratch)
        sems = []
        for st in stages:
            sems.append(refs[pos:pos + len(st.sems)])
            pos += len(st.sems)
        if stages:
            first = functools.reduce(jnp.logical_and, [pl.program_id(a) == 0 for a in range(len(grid))])

            @pl.when(first)
            def _():
                for st, a, b, s in zip(stages, st_ins, st_outs, sems):
                    st.start(a, b, s)

        body(*pre, *ins, *outs, *work)
        if stages:
            last = functools.reduce(jnp.logical_and, [pl.program_id(a) == g - 1 for a, g in enumerate(grid)])

            @pl.when(last)
            def _():
                for st, a, b, s in zip(stages, st_ins, st_outs, sems):
                    st.finish(a, b, s)

    all_in = list(in_specs) + [ANY] * len(st_args)
    all_out = list(out_specs) + [ANY] * len(st_shapes)
    kw = dict(has_side_effects=True) if stages else {}
    if vmem is not None:
        kw["vmem_limit_bytes"] = vmem * MIB
    if prefetch is None:
        gkw = dict(grid=grid, in_specs=all_in, out_specs=all_out, scratch_shapes=st_sems)
    else:
        gkw = dict(grid_spec=pltpu.PrefetchScalarGridSpec(
            num_scalar_prefetch=1, grid=grid, in_specs=all_in, out_specs=all_out, scratch_shapes=st_sems))
    res = pl.pallas_call(
        wrapped, name=name, out_shape=_hbm_out(list(out_shape) + st_shapes), input_output_aliases=aliases,
        compiler_params=pltpu.CompilerParams(**kw), **gkw,
    )(*([prefetch] if npre else []), *_hbm(*args, *st_args))
    outs, rest, st_res = list(res[:nout]), list(res[nout:]), []
    for st in stages:
        st_res.append(rest[:len(st.out_shape)])
        rest = rest[len(st.out_shape):]
    return outs, st_res


def _mm(a, b):
    return jnp.dot(a.astype(BF), b.astype(BF), preferred_element_type=F32)


def _mm_nt(a, b):
    return lax.dot_general(a.astype(BF), b.astype(BF), (((1,), (1,)), ((), ())),
                           preferred_element_type=F32)


def _mm_tn(a, b):
    return lax.dot_general(a.astype(BF), b.astype(BF), (((0,), (0,)), ((), ())),
                           preferred_element_type=F32)


def _rows(v):
    return lax.broadcasted_iota(jnp.int32, v.shape, 0)


def _sd(v, s, fill=0.0):
    return jnp.where(_rows(v) >= s, pltpu.roll(v, s, axis=0), fill)


def _su(v, s, fill=0.0):
    n = v.shape[0]
    return jnp.where(_rows(v) < n - s, pltpu.roll(v, n - s, axis=0), fill)


def _sigmoid(z):
    return 1.0 / (1.0 + jnp.exp(-z))


def _softplus(z):
    e = jnp.exp(-jnp.abs(z))
    u = 1.0 + e
    d = u - 1.0
    log1p = jnp.where(d == 0.0, e, jnp.log(u) * (e / jnp.where(d == 0.0, 1.0, d)))
    return jnp.maximum(z, 0.0) + log1p


def _mean(v):
    return jnp.mean(v, axis=-1, keepdims=True)


def _colsum(v):
    return jnp.sum(v, axis=0, keepdims=True)


def _acc(ref, val, first):
    @pl.when(first)
    def _():
        ref[...] = val

    @pl.when(jnp.logical_not(first))
    def _():
        ref[...] += val


def _conv(xp, cw, cb):
    x1, x2, x3 = _sd(xp, 1), _sd(xp, 2), _sd(xp, 3)
    xc = cb + cw[0:1] * x3 + cw[1:2] * x2 + cw[2:3] * x1 + cw[3:4] * xp
    return xc, x1, x2, x3


def _lru_gates(xc, wa, ba, wx, bx, lam):
    xcb = xc.astype(BF)
    r = _sigmoid(_mm(xcb, wa) + ba)
    ii = _sigmoid(_mm(xcb, wx) + bx)
    sp = _softplus(-lam)
    la = (-LRU_C) * r * sp
    a = jnp.exp(la)
    mult = jnp.sqrt(-jnp.tanh(la) * (a * a + 1.0))
    return xcb, r, ii, sp, a, mult


def _gelu_parts(g):
    th = jnp.tanh(GELU_C * (g + 0.044715 * (g * g * g)))
    gel = 0.5 * g * (1.0 + th)
    dgel = 0.5 * (1.0 + th) + 0.5 * g * (1.0 - th * th) * (GELU_C * (1.0 + 3.0 * 0.044715 * (g * g)))
    return gel, dgel


def _tile_scan(a, b, a_s, b_s, out_ref, reverse):
    n, lanes = a.shape
    nt = n // 8
    a, b = a.reshape(nt, 8, lanes), b.reshape(nt, 8, lanes)
    sub = lax.broadcasted_iota(jnp.int32, a.shape, 1)
    s = 1
    while s < 8:
        keep = sub < 8 - s if reverse else sub >= s
        amount = 8 - s if reverse else s
        b = b + a * jnp.where(keep, pltpu.roll(b, amount, axis=1), 0.0)
        a = a * jnp.where(keep, pltpu.roll(a, amount, axis=1), 1.0)
        s *= 2
    a_s[...] = a.reshape(n, lanes)
    b_s[...] = b.reshape(n, lanes)
    edge = pl.ds(0 if reverse else 7, nt, stride=8)
    ta, tb = a_s[edge, :], b_s[edge, :]
    shift = _su if reverse else _sd
    s = 1
    while s < nt:
        tb = tb + ta * shift(tb, s, 0.0)
        if 2 * s < nt:
            ta = ta * shift(ta, s, 1.0)
        s *= 2
    enters = shift(tb, 1, 0.0)
    for o in range(8):
        rows = pl.ds(o, nt, stride=8)
        out_ref[rows, :] = b_s[rows, :] + a_s[rows, :] * enters


def _pool_window(x, steps, shift):
    s, sh = x, 1
    for _ in range(steps):
        s = s + shift(s, sh)
        sh *= 2
    return s


def _fwd_inproj_own(x, g1, w_in, slots, stages=()):
    tm = 1024

    def body(s_ref, x_ref, g_ref, w_ref, proj_ref, h_ref):
        xv = x_ref[...]
        r = lax.rsqrt(_mean(xv * xv) + NORM_EPS)
        h = ((xv * r) * g_ref[...]).astype(BF)
        h_ref[...] = h
        proj_ref[...] = jnp.dot(h, w_ref[0], preferred_element_type=F32)

    return _call(
        body, name="fwd_inproj_own", grid=(T // tm,), prefetch=slots,
        in_specs=[pl.BlockSpec((tm, D), lambda i, s: (i, 0)),
                  pl.BlockSpec((1, D), lambda i, s: (0, 0)),
                  pl.BlockSpec((1, D, CW_IN), lambda i, s: (s[0], 0, 0))],
        out_specs=[pl.BlockSpec((tm, CW_IN), lambda i, s: (i, s[0])),
                   pl.BlockSpec((tm, D), lambda i, s: (i, 0))],
        out_shape=[jax.ShapeDtypeStruct((T, DIN), F32), jax.ShapeDtypeStruct((T, D), BF)],
        vmem=40, args=[x, g1, w_in], stages=stages)[0]


def _fwd_inproj_rest(h1, w_in, proj, slots):
    tm = 1024

    def body(s_ref, h_ref, w_ref, p_in, proj_ref):
        proj_ref[...] = jnp.dot(h_ref[...], w_ref[0], preferred_element_type=F32)

    res = pl.pallas_call(
        body, name="fwd_inproj_rest",
        grid_spec=pltpu.PrefetchScalarGridSpec(
            num_scalar_prefetch=1, grid=(NCHIP - 1, T // tm),
            in_specs=[pl.BlockSpec((tm, D), lambda k, i, s: (i, 0)),
                      pl.BlockSpec((1, D, CW_IN), lambda k, i, s: (s[1 + k], 0, 0)), ANY],
            out_specs=pl.BlockSpec((tm, CW_IN), lambda k, i, s: (i, s[1 + k]))),
        out_shape=pltpu.HBM((T, DIN), F32), input_output_aliases={3: 0},
        compiler_params=_cp(40),
    )(slots, *_hbm(h1, w_in, proj))
    return res


def _vec_spec():
    return pl.BlockSpec((1, CB), lambda j: (0, j))


def _fwd_lru(proj, conv_w, conv_b, wa, ba, wx, bx, lam, stages=()):
    def body(xp_ref, g_ref, cw_ref, cb_ref, wa_ref, ba_ref, wx_ref, bx_ref, lam_ref, y_ref, h_ref, a_s, b_s):
        xc, _, _, _ = _conv(xp_ref[...], cw_ref[...], cb_ref[...])
        _, _, ii, _, a, mult = _lru_gates(xc, wa_ref[0], ba_ref[...], wx_ref[0], bx_ref[...], lam_ref[...])
        _tile_scan(a, mult * (ii * xc), a_s, b_s, h_ref, reverse=False)
        gel, _ = _gelu_parts(g_ref[...])
        y_ref[...] = (h_ref[...] * gel).astype(BF)

    return _call(
        body, name="fwd_lru", grid=(NG,),
        in_specs=[pl.BlockSpec((T, CB), lambda j: (0, j)),
                  pl.BlockSpec((T, CB), lambda j: (0, NG + j)),
                  pl.BlockSpec((4, CB), lambda j: (0, j)),
                  _vec_spec(),
                  pl.BlockSpec((1, CB, CB), lambda j: (j, 0, 0)), _vec_spec(),
                  pl.BlockSpec((1, CB, CB), lambda j: (j, 0, 0)), _vec_spec(),
                  _vec_spec()],
        out_specs=[pl.BlockSpec((T, CB), lambda j: (0, j)), pl.BlockSpec((T, CB), lambda j: (0, j))],
        out_shape=[jax.ShapeDtypeStruct((T, DR), BF), jax.ShapeDtypeStruct((T, DR), F32)],
        vmem=48, args=[proj, proj, conv_w, conv_b, wa, ba, wx, bx, lam], stages=stages,
        scratch=[pltpu.VMEM((T, CB), F32)] * 2)


def _pool_cnt(w):
    t = lax.broadcasted_iota(jnp.int32, (T, 1), 0)
    return jnp.minimum(t + 1, w).astype(F32)


def _fwd_pool(proj, pool_w, pool_scale):
    def body(xp_ref, pw_ref, sc_ref, y_ref):
        for g, w in enumerate(POOL_WINDOWS):
            cols = slice(g * PG, (g + 1) * PG)
            x = xp_ref[:, cols]
            p = _pool_window(x, g + 1, _sd) / _pool_cnt(w) - x
            y_ref[:, cols] = (_mm(p, pw_ref[g]) * sc_ref[:, cols]).astype(BF)

    return pl.pallas_call(
        body, name="fwd_pool", grid=(1,),
        in_specs=[pl.BlockSpec((T, DP), lambda i: (0, 2 * DR // DP)),
                  pl.BlockSpec((4, PG, PG), lambda i: (0, 0, 0)),
                  pl.BlockSpec((1, DP), lambda i: (0, 0))],
        out_specs=pl.BlockSpec((T, DP), lambda i: (0, 0)),
        out_shape=pltpu.HBM((T, DP), BF),
        compiler_params=_cp(48),
    )(*_hbm(proj, pool_w, pool_scale))


GATE_BLK = 512
GATE_BLK0 = (2 * DR + DP) // GATE_BLK


def _gate_specs(tm):
    return [pl.BlockSpec((tm, GATE_BLK), functools.partial(lambda i, q: (i, GATE_BLK0 + q), q=q))
            for q in range(4)]


def _fwd_merge(x, ylru, ypool, proj, b_gate, g2, g3, w_lru_up, w_pool_up, w_o, stages=()):
    tm = 512

    def body(x_ref, yl_ref, yp_ref, p0, p1, p2, p3, bg_ref, g2_ref, g3_ref, wl_ref, wp_ref, wo_ref,
             x2_ref, h2_ref, m_ref, mrg_ref, bra_ref, brb_ref):
        bra = jnp.dot(yl_ref[...], wl_ref[...], preferred_element_type=F32)
        yp = yp_ref[...]
        brb = jnp.concatenate([jnp.dot(yp, wp_ref[k], preferred_element_type=F32) for k in range(NCHIP)], axis=1)
        bg = bg_ref[...]
        ga = _sigmoid(jnp.concatenate([p0[...], p1[...]], axis=1) + bg[:, :D])
        gb = _sigmoid(jnp.concatenate([p2[...], p3[...]], axis=1) + bg[:, D:])
        mrg = (ga * bra + gb * brb).astype(BF)
        m = jnp.dot(mrg, wo_ref[...], preferred_element_type=F32)
        r2 = lax.rsqrt(_mean(m * m) + NORM_EPS)
        x2 = x_ref[...] + (m * r2) * g2_ref[...]
        r3 = lax.rsqrt(_mean(x2 * x2) + NORM_EPS)
        x2_ref[...] = x2
        h2_ref[...] = ((x2 * r3) * g3_ref[...]).astype(BF)
        m_ref[...] = m
        mrg_ref[...] = mrg
        bra_ref[...] = bra.astype(BF)
        brb_ref[...] = brb.astype(BF)

    row = lambda w: pl.BlockSpec((tm, w), lambda i: (i, 0))
    full2 = lambda a, b: pl.BlockSpec((a, b), lambda i: (0, 0))
    return _call(
        body, name="fwd_merge", grid=(T // tm,),
        in_specs=[row(D), row(DR), row(DP)] + _gate_specs(tm) +
                 [full2(1, 2 * D), full2(1, D), full2(1, D), full2(DR, D),
                  pl.BlockSpec((NCHIP, DP, D // NCHIP), lambda i: (0, 0, 0)), full2(D, D)],
        out_specs=[row(D)] * 6,
        out_shape=[jax.ShapeDtypeStruct((T, D), F32), jax.ShapeDtypeStruct((T, D), BF),
                   jax.ShapeDtypeStruct((T, D), F32), jax.ShapeDtypeStruct((T, D), BF),
                   jax.ShapeDtypeStruct((T, D), BF), jax.ShapeDtypeStruct((T, D), BF)],
        vmem=48, args=[x, ylru, ypool, proj, proj, proj, proj, b_gate, g2, g3, w_lru_up, w_pool_up, w_o],
        stages=stages)


def _fwd_mlp_loss(h2, w_ff1, w_ff2, x2, target, g4):
    tm = 512
    fk = DF // NCHIP

    def body(h_ref, w1_ref, w2_ref, x2_ref, t_ref, g_ref, a1_ref, loss_ref, dy_ref, df_ref, dg_ref):
        first = pl.program_id(0) == 0
        h = h_ref[...]
        f = None
        for k in range(NCHIP):
            a1 = jnp.maximum(jnp.dot(h, w1_ref[k], preferred_element_type=F32), 0.0)
            a1_ref[:, k * fk:(k + 1) * fk] = a1.astype(BF)
            part = jnp.dot((a1 * a1).astype(BF), w2_ref[k * fk:(k + 1) * fk, :], preferred_element_type=F32)
            f = part if f is None else f + part
        g4v = g_ref[...]
        r4 = lax.rsqrt(_mean(f * f) + NORM_EPS)
        fn = f * r4
        e = (x2_ref[...] + fn * g4v) - t_ref[...]
        _acc(loss_ref, jnp.sum(_mean(e * e), axis=0, keepdims=True), first)
        dy = e * (1.0 / D)
        dy_ref[...] = dy
        _acc(dg_ref, _colsum(dy * fn), first)
        dfn = dy * g4v
        df_ref[...] = (r4 * (dfn - fn * _mean(dfn * fn))).astype(BF)

    row = pl.BlockSpec((tm, D), lambda i: (i, 0))
    return pl.pallas_call(
        body, name="fwd_mlp_loss", grid=(T // tm,),
        in_specs=[row, pl.BlockSpec((NCHIP, D, fk), lambda i: (0, 0, 0)), pl.BlockSpec((DF, D), lambda i: (0, 0)),
                  row, row, pl.BlockSpec((1, D), lambda i: (0, 0))],
        out_specs=[pl.BlockSpec((tm, DF), lambda i: (i, 0)), pl.BlockSpec((1, 1), lambda i: (0, 0)), row, row,
                   pl.BlockSpec((1, D), lambda i: (0, 0))],
        out_shape=_hbm_out([jax.ShapeDtypeStruct((T, DF), BF), jax.ShapeDtypeStruct((1, 1), F32),
                            jax.ShapeDtypeStruct((T, D), F32), jax.ShapeDtypeStruct((T, D), BF),
                            jax.ShapeDtypeStruct((1, D), F32)]),
        compiler_params=_cp(56),
    )(*_hbm(h2, w_ff1, w_ff2, x2, target, g4))


def _bwd_mlp_x(df, a1, w_ff1, w_ff2):
    tm = 512
    fk = DF // NCHIP

    def body(df_ref, a1_ref, w1_ref, w2_ref, dh_ref, df1_ref):
        df = df_ref[...]
        dh = None
        for k in range(NCHIP):
            cols = slice(k * fk, (k + 1) * fk)
            dact = _mm_nt(df, w2_ref[cols, :])
            df1 = (dact * (2.0 * a1_ref[:, cols].astype(F32))).astype(BF)
            df1_ref[:, cols] = df1
            part = _mm_nt(df1, w1_ref[k])
            dh = part if dh is None else dh + part
        dh_ref[...] = dh

    return pl.pallas_call(
        body, name="bwd_mlp_x", grid=(T // tm,),
        in_specs=[pl.BlockSpec((tm, D), lambda i: (i, 0)),
                  pl.BlockSpec((tm, DF), lambda i: (i, 0)),
                  pl.BlockSpec((NCHIP, D, fk), lambda i: (0, 0, 0)),
                  pl.BlockSpec((DF, D), lambda i: (0, 0))],
        out_specs=[pl.BlockSpec((tm, D), lambda i: (i, 0)), pl.BlockSpec((tm, DF), lambda i: (i, 0))],
        out_shape=_hbm_out([jax.ShapeDtypeStruct((T, D), F32), jax.ShapeDtypeStruct((T, DF), BF)]),
        compiler_params=_cp(56),
    )(*_hbm(df, a1, w_ff1, w_ff2))


def _bwd_mlp_w(df, h2, a1, df1):
    fc = 512
    per = (DF // NCHIP) // fc

    def body(df_ref, h_ref, a1_ref, df1_ref, dw1_ref, dw2_ref):
        a1 = a1_ref[...].astype(F32)
        dw2_ref[...] = _mm_tn((a1 * a1).astype(BF), df_ref[...]).astype(BF)
        dw1_ref[0] = _mm_tn(h_ref[...], df1_ref[...]).astype(BF)

    return pl.pallas_call(
        body, name="bwd_mlp_w", grid=(DF // fc,),
        in_specs=[pl.BlockSpec((T, D), lambda j: (0, 0)),
                  pl.BlockSpec((T, D), lambda j: (0, 0)),
                  pl.BlockSpec((T, fc), lambda j: (0, j)),
                  pl.BlockSpec((T, fc), lambda j: (0, j))],
        out_specs=[pl.BlockSpec((1, D, fc), lambda j: (j // per, 0, j % per)),
                   pl.BlockSpec((fc, D), lambda j: (j, 0))],
        out_shape=_hbm_out([jax.ShapeDtypeStruct((NCHIP, D, DF // NCHIP), BF),
                            jax.ShapeDtypeStruct((DF, D), BF)]),
        compiler_params=_cp(56),
    )(*_hbm(df, h2, a1, df1))


def _bwd_merge(dh2, dy, x2, m, bra, brb, proj, b_gate, g2, g3, w_lru_up, w_pool_up, w_o, stages=()):
    tm = 256
    cpu = D // NCHIP

    def body(dh2_ref, dy_ref, x2_ref, m_ref, bra_ref, brb_ref, p0, p1, p2, p3, bg_ref,
             g2_ref, g3_ref, wl_ref, wp_ref, wo_ref,
             dx_ref, dgt_ref, dyl_ref, dyp_ref, dm_ref, dbra_ref, dbrb_ref, dg2_ref, dg3_ref, dbg_ref):
        first = pl.program_id(0) == 0
        x2 = x2_ref[...]
        r3 = lax.rsqrt(_mean(x2 * x2) + NORM_EPS)
        x2n = x2 * r3
        dh2 = dh2_ref[...]
        t3 = dh2 * g3_ref[...]
        dx2 = dy_ref[...] + r3 * (t3 - x2n * _mean(t3 * x2n))
        dx_ref[...] = dx2
        _acc(dg3_ref, _colsum(dh2 * x2n), first)
        m = m_ref[...]
        r2 = lax.rsqrt(_mean(m * m) + NORM_EPS)
        mn = m * r2
        _acc(dg2_ref, _colsum(dx2 * mn), first)
        dmn = dx2 * g2_ref[...]
        dm = (r2 * (dmn - mn * _mean(dmn * mn))).astype(BF)
        dm_ref[...] = dm
        dmrg = _mm_nt(dm, wo_ref[...])
        bg = bg_ref[...]
        ga = _sigmoid(jnp.concatenate([p0[...], p1[...]], axis=1) + bg[:, :D])
        gb = _sigmoid(jnp.concatenate([p2[...], p3[...]], axis=1) + bg[:, D:])
        dga = dmrg * bra_ref[...].astype(F32) * (ga * (1.0 - ga))
        dgb = dmrg * brb_ref[...].astype(F32) * (gb * (1.0 - gb))
        dgt_ref[:, :D] = dga.astype(BF)
        dgt_ref[:, D:] = dgb.astype(BF)
        _acc(dbg_ref, jnp.concatenate([_colsum(dga), _colsum(dgb)], axis=1), first)
        dbra = (dmrg * ga).astype(BF)
        dbrb = (dmrg * gb).astype(BF)
        dbra_ref[...] = dbra
        dbrb_ref[...] = dbrb
        dyl_ref[...] = _mm_nt(dbra, wl_ref[...])
        dyp = None
        for k in range(NCHIP):
            part = _mm_nt(dbrb[:, k * cpu:(k + 1) * cpu], wp_ref[k])
            dyp = part if dyp is None else dyp + part
        dyp_ref[...] = dyp

    row = lambda w: pl.BlockSpec((tm, w), lambda i: (i, 0))
    full2 = lambda a, b: pl.BlockSpec((a, b), lambda i: (0, 0))
    wp_spec = pl.BlockSpec((NCHIP, DP, cpu), lambda i: (0, 0, 0))
    return _call(
        body, name="bwd_merge", grid=(T // tm,),
        in_specs=[row(D)] * 6 + _gate_specs(tm) +
                 [full2(1, 2 * D), full2(1, D), full2(1, D), full2(DR, D), wp_spec, full2(D, D)],
        out_specs=[row(D), row(2 * D), row(DR), row(DP), row(D), row(D), row(D),
                   full2(1, D), full2(1, D), full2(1, 2 * D)],
        out_shape=[jax.ShapeDtypeStruct((T, D), F32), jax.ShapeDtypeStruct((T, 2 * D), BF),
                   jax.ShapeDtypeStruct((T, DR), F32), jax.ShapeDtypeStruct((T, DP), F32),
                   jax.ShapeDtypeStruct((T, D), BF), jax.ShapeDtypeStruct((T, D), BF),
                   jax.ShapeDtypeStruct((T, D), BF),
                   jax.ShapeDtypeStruct((1, D), F32), jax.ShapeDtypeStruct((1, D), F32),
                   jax.ShapeDtypeStruct((1, 2 * D), F32)],
        vmem=56, args=[dh2, dy, x2, m, bra, brb, proj, proj, proj, proj, b_gate, g2, g3, w_lru_up, w_pool_up, w_o],
        stages=stages)


def _dw_merge(mrg, dm, ylru, dbra, ypool, dbrb, stages=()):
    nb = NCHIP
    rb, pb, cpu = D // nb, DP // nb, D // NCHIP

    def body(mrg_ref, dm_ref, yl_ref, dbra_ref, yp_ref, dbrb_ref, dwo_ref, dwl_ref, dwp_ref):
        dwo_ref[...] = _mm_tn(mrg_ref[...], dm_ref[...]).astype(BF)
        dwl_ref[...] = _mm_tn(yl_ref[...], dbra_ref[...]).astype(BF)
        dwp = _mm_tn(yp_ref[...], dbrb_ref[...]).astype(BF)
        for k in range(NCHIP):
            dwp_ref[k] = dwp[:, k * cpu:(k + 1) * cpu]

    cols = lambda w: pl.BlockSpec((T, w), lambda r: (0, r))
    whole = pl.BlockSpec((T, D), lambda r: (0, 0))
    return _call(
        body, name="dw_merge", grid=(nb,),
        in_specs=[cols(rb), whole, cols(rb), whole, cols(pb), whole],
        out_specs=[pl.BlockSpec((rb, D), lambda r: (r, 0)), pl.BlockSpec((rb, D), lambda r: (r, 0)),
                   pl.BlockSpec((NCHIP, pb, cpu), lambda r: (0, r, 0))],
        out_shape=[jax.ShapeDtypeStruct((D, D), BF), jax.ShapeDtypeStruct((DR, D), BF),
                   jax.ShapeDtypeStruct((NCHIP, DP, cpu), BF)],
        vmem=56, args=[mrg, dm, ylru, dbra, ypool, dbrb], stages=stages)


def _bwd_lru(proj, h, dylru, conv_w, conv_b, wa, ba, wx, bx, lam, stages=()):
    def body(xp_ref, g_ref, h_ref, dy_ref, cw_ref, cb_ref, wa_ref, ba_ref, wx_ref, bx_ref, lam_ref,
             dxp_ref, dg_ref, dcw_ref, dcb_ref, dwa_ref, dba_ref, dwx_ref, dbx_ref, dlam_ref, a_s, b_s, l_s):
        xp = xp_ref[...]
        cw = cw_ref[...]
        lam = lam_ref[...]
        xc, x1, x2, x3 = _conv(xp, cw, cb_ref[...])
        wa, wx = wa_ref[0], wx_ref[0]
        xcb, r, ii, sp, a, mult = _lru_gates(xc, wa, ba_ref[...], wx, bx_ref[...], lam)
        g = g_ref[...]
        gel, dgel = _gelu_parts(g)
        h = h_ref[...]
        dy = dy_ref[...]
        dg_ref[...] = (dy * h * dgel).astype(BF)
        _tile_scan(_su(a, 1, 0.0), dy * gel, a_s, b_s, l_s, reverse=True)
        b = l_s[...]
        da = b * _sd(h, 1, 0.0)
        dmult = b * (ii * xc)
        dii = b * (mult * xc)
        dxc = b * (mult * ii)
        dla = da * a - dmult * ((a * a) / mult)
        dr = dla * ((-LRU_C) * sp)
        dsp = _colsum(dla * ((-LRU_C) * r))
        dlam_ref[...] = -dsp / (1.0 + jnp.exp(lam))
        dzr = dr * (r * (1.0 - r))
        dzi = dii * (ii * (1.0 - ii))
        dzrb, dzib = dzr.astype(BF), dzi.astype(BF)
        dxc = dxc + _mm_nt(dzrb, wa) + _mm_nt(dzib, wx)
        dwa_ref[0] = _mm_tn(xcb, dzrb)
        dwx_ref[0] = _mm_tn(xcb, dzib)
        dba_ref[...] = _colsum(dzr)
        dbx_ref[...] = _colsum(dzi)
        dcb_ref[...] = _colsum(dxc)
        dcw_ref[...] = jnp.concatenate([_colsum(dxc * x3), _colsum(dxc * x2), _colsum(dxc * x1),
                                        _colsum(dxc * xp)], axis=0)
        dxp = cw[3:4] * dxc + cw[2:3] * _su(dxc, 1) + cw[1:2] * _su(dxc, 2) + cw[0:1] * _su(dxc, 3)
        dxp_ref[...] = dxp.astype(BF)

    blk = pl.BlockSpec((T, CB), lambda j: (0, j))
    wsp = pl.BlockSpec((1, CB, CB), lambda j: (j, 0, 0))
    return _call(
        body, name="bwd_lru", grid=(NG,),
        in_specs=[blk, pl.BlockSpec((T, CB), lambda j: (0, NG + j)), blk, blk,
                  pl.BlockSpec((4, CB), lambda j: (0, j)), _vec_spec(), wsp, _vec_spec(), wsp, _vec_spec(),
                  _vec_spec()],
        out_specs=[blk, blk, pl.BlockSpec((4, CB), lambda j: (0, j)), _vec_spec(), wsp, _vec_spec(), wsp,
                   _vec_spec(), _vec_spec()],
        out_shape=[jax.ShapeDtypeStruct((T, DR), BF), jax.ShapeDtypeStruct((T, DR), BF),
                   jax.ShapeDtypeStruct((4, DR), F32), jax.ShapeDtypeStruct((1, DR), F32),
                   jax.ShapeDtypeStruct((NG, CB, CB), F32), jax.ShapeDtypeStruct((1, DR), F32),
                   jax.ShapeDtypeStruct((NG, CB, CB), F32), jax.ShapeDtypeStruct((1, DR), F32),
                   jax.ShapeDtypeStruct((1, DR), F32)],
        vmem=56, args=[proj, proj, h, dylru, conv_w, conv_b, wa, ba, wx, bx, lam], stages=stages,
        scratch=[pltpu.VMEM((T, CB), F32)] * 3)


def _bwd_pool(proj, dypool, pool_w, pool_scale):
    def body(xp_ref, dy_ref, pw_ref, sc_ref, dx_ref, dw_ref, dsc_ref):
        for g, w in enumerate(POOL_WINDOWS):
            cols = slice(g * PG, (g + 1) * PG)
            cnt = _pool_cnt(w)
            x = xp_ref[:, cols]
            pb = (_pool_window(x, g + 1, _sd) / cnt - x).astype(BF)
            wg = pw_ref[g]
            dy = dy_ref[:, cols]
            dsc_ref[:, cols] = _colsum(dy * _mm(pb, wg))
            dyp = (dy * sc_ref[:, cols]).astype(BF)
            dw_ref[g] = _mm_tn(pb, dyp)
            dp = _mm_nt(dyp, wg)
            dx_ref[:, cols] = (_pool_window(dp / cnt, g + 1, _su) - dp).astype(BF)

    return pl.pallas_call(
        body, name="bwd_pool", grid=(1,),
        in_specs=[pl.BlockSpec((T, DP), lambda i: (0, 2 * DR // DP)),
                  pl.BlockSpec((T, DP), lambda i: (0, 0)),
                  pl.BlockSpec((4, PG, PG), lambda i: (0, 0, 0)),
                  pl.BlockSpec((1, DP), lambda i: (0, 0))],
        out_specs=[pl.BlockSpec((T, DP), lambda i: (0, 0)),
                   pl.BlockSpec((4, PG, PG), lambda i: (0, 0, 0)),
                   pl.BlockSpec((1, DP), lambda i: (0, 0))],
        out_shape=_hbm_out([jax.ShapeDtypeStruct((T, DP), BF), jax.ShapeDtypeStruct((4, PG, PG), F32),
                            jax.ShapeDtypeStruct((1, DP), F32)]),
        compiler_params=_cp(48),
    )(*_hbm(proj, dypool, pool_w, pool_scale))


PART_COLS = (DR, DR, DP, 2 * D)


def _shard_pieces():
    starts = [sum(PART_COLS[:p]) for p in range(len(PART_COLS))]
    shards = []
    for k in range(NCHIP):
        lo, hi = k * CW_IN, (k + 1) * CW_IN
        shards.append([(p, max(lo, s) - s, min(hi, s + wd) - s, max(lo, s) - lo)
                       for p, (s, wd) in enumerate(zip(starts, PART_COLS)) if max(lo, s) < min(hi, s + wd)])
    return shards


def _bwd_inproj_w(h1, parts, after):
    flat = [(k, *piece) for k, pieces in enumerate(_shard_pieces()) for piece in pieces]

    def body(h_hbm, p0, p1, p2, p3, after_ref, dw_hbm, h_v, dw_v, *rest):
        bufs, sem_in, sem_out = rest[:len(flat)], rest[len(flat)], rest[len(flat) + 1]
        part_refs = (p0, p1, p2, p3)
        loads = [pltpu.make_async_copy(h_hbm, h_v, sem_in.at[0])]
        for i, (k, p, a, b, c0) in enumerate(flat):
            loads.append(pltpu.make_async_copy(part_refs[p].at[:, pl.ds(a, b - a)], bufs[i], sem_in.at[1 + i]))
        for cp in loads:
            cp.start()
        loads[0].wait()
        stores = []
        for i, (k, p, a, b, c0) in enumerate(flat):
            loads[1 + i].wait()
            dw_v[k, :, c0:c0 + b - a] = _mm_tn(h_v[...], bufs[i][...]).astype(BF)
            if i + 1 == len(flat) or flat[i + 1][0] != k:
                stores.append(pltpu.make_async_copy(dw_v.at[k], dw_hbm.at[k], sem_out.at[k]))
                stores[-1].start()
        for cp in stores:
            cp.wait()

    scratch = [pltpu.VMEM((T, D), BF), pltpu.VMEM((NCHIP, D, CW_IN), BF)]
    scratch += [pltpu.VMEM((T, b - a), parts[p].dtype) for k, p, a, b, c0 in flat]
    scratch += [pltpu.SemaphoreType.DMA((1 + len(flat),)), pltpu.SemaphoreType.DMA((NCHIP,))]
    return pl.pallas_call(
        body, name="bwd_inproj_w", in_specs=[ANY] * 6, out_specs=ANY, scratch_shapes=scratch,
        out_shape=pltpu.HBM((NCHIP, D, CW_IN), BF), compiler_params=_cp(48),
    )(*_hbm(h1, *parts), after)


def _bwd_inproj_x(parts, w_in, x, dxres, g1, stages=()):
    tm = 512

    def body(p0, p1, p2, p3, w_ref, x_ref, dr_ref, g_ref, dx_ref, dg_ref):
        part_refs = (p0, p1, p2, p3)
        dh = None
        for k, pieces in enumerate(_shard_pieces()):
            for p, a, b, c0 in pieces:
                part = _mm_nt(part_refs[p][:, a:b], w_ref[k, :, c0:c0 + b - a])
                dh = part if dh is None else dh + part
        xv = x_ref[...]
        r = lax.rsqrt(_mean(xv * xv) + NORM_EPS)
        xn = xv * r
        t = dh * g_ref[...]
        dx_ref[...] = dr_ref[...] + r * (t - xn * _mean(t * xn))
        _acc(dg_ref, _colsum(dh * xn), pl.program_id(0) == 0)

    row = pl.BlockSpec((tm, D), lambda i: (i, 0))
    vec = pl.BlockSpec((1, D), lambda i: (0, 0))
    return _call(
        body, name="bwd_inproj_x", grid=(T // tm,),
        in_specs=[pl.BlockSpec((tm, wd), lambda i: (i, 0)) for wd in PART_COLS] +
                 [pl.BlockSpec((NCHIP, D, CW_IN), lambda i: (0, 0, 0)), row, row, vec],
        out_specs=[row, vec],
        out_shape=[jax.ShapeDtypeStruct((T, D), F32), jax.ShapeDtypeStruct((1, D), F32)],
        vmem=56, args=[*parts, w_in, x, dxres, g1], stages=stages)[0]


def _place():
    x, y, c = lax.axis_index("x"), lax.axis_index("y"), lax.axis_index("c")
    chips = [(1 - x, y), (x, 1 - y), (1 - x, 1 - y)]
    return x, y, c, chips


def _rcopy(src, dst, ssem, rsem, dev):
    return pltpu.make_async_remote_copy(src_ref=src, dst_ref=dst, send_sem=ssem, recv_sem=rsem,
                                        device_id=dev, device_id_type=MESH_ID)


def _sds(a):
    return jax.ShapeDtypeStruct(a.shape, a.dtype)


def _sem2(n, m):
    return [pltpu.SemaphoreType.DMA((n * m,)), pltpu.SemaphoreType.DMA((n * m,))]


ALL = (0, 1, 1)


def _piece(ref, k, half, part):
    hr = ref.shape[1] // 2
    r0, r1 = hr * part[0] // part[2], hr * part[1] // part[2]
    return ref.at[k, pl.ds(half * hr + r0, r1 - r0), :]


def _gather(fulls, ici=(), d2d=()):
    n = len(fulls)
    ici, d2d = list(ici), list(d2d)
    pieces = [("ici", i, part) for i, part in ici] + [("d2d", i, part) for i, part in d2d]

    def copies(outs, sems):
        x, y, c, chips = _place()
        me = 2 * x + y
        sib = (x, y, 1 - c)
        send, recv = [], []
        for q, (kind, i, part) in enumerate(pieces):
            for j, chip in enumerate(chips):
                k, s = 2 * chip[0] + chip[1], 3 * q + j
                if kind == "ici":
                    mine, theirs, dev = _piece(outs[i], me, c, part), _piece(outs[i], k, c, part), (*chip, c)
                else:
                    mine, theirs, dev = _piece(outs[i], k, c, part), _piece(outs[i], k, 1 - c, part), sib
                send.append(_rcopy(mine, mine, sems[0].at[s], sems[1].at[s], dev))
                recv.append(_rcopy(theirs, theirs, sems[0].at[s], sems[1].at[s], dev))
        return send, recv

    def start(ins, outs, sems):
        for cp in copies(outs, sems)[0]:
            cp.start()

    def finish(ins, outs, sems):
        send, recv = copies(outs, sems)
        for cp in recv:
            cp.wait_recv()
        for cp in send:
            cp.wait_send()

    sems = [pltpu.SemaphoreType.DMA((3 * len(pieces),)), pltpu.SemaphoreType.DMA((3 * len(pieces),))]
    return _Stage(fulls, [_sds(f) for f in fulls], {i: i for i in range(n)}, sems, start, finish)


def _gather_whole(v):
    def copies(ins, outs, sems):
        x, y, c, chips = _place()
        me = 2 * x + y
        send = [_rcopy(ins[0], outs[0].at[me], sems[0].at[j], sems[1].at[j], (*chip, c))
                for j, chip in enumerate(chips)]
        recv = [_rcopy(ins[0], outs[0].at[2 * chip[0] + chip[1]], sems[0].at[j], sems[1].at[j], (*chip, c))
                for j, chip in enumerate(chips)]
        return send, recv

    def start(ins, outs, sems):
        for cp in copies(ins, outs, sems)[0]:
            cp.start()

    def finish(ins, outs, sems):
        send, recv = copies(ins, outs, sems)
        for cp in recv:
            cp.wait_recv()
        for cp in send:
            cp.wait_send()

    return _Stage([v], [jax.ShapeDtypeStruct((NCHIP,) + v.shape, v.dtype)], {},
                  [pltpu.SemaphoreType.DMA((3,)), pltpu.SemaphoreType.DMA((3,))], start, finish)


def _to_sibling(srcs):
    n = len(srcs)

    def copies(ins, outs, sems):
        x, y, c, _ = _place()
        sib = (x, y, 1 - c)
        return [_rcopy(ins[i].at[:, 1 - c] if srcs[i].ndim == 4 else ins[i], outs[i], sems[0].at[i], sems[1].at[i], sib)
                for i in range(n)]

    def start(ins, outs, sems):
        for cp in copies(ins, outs, sems):
            cp.start()

    def finish(ins, outs, sems):
        for cp in copies(ins, outs, sems):
            cp.wait()

    shapes = [jax.ShapeDtypeStruct((NCHIP,) + s.shape[2:] if s.ndim == 4 else s.shape, s.dtype) for s in srcs]
    return _Stage(srcs, shapes, {}, [pltpu.SemaphoreType.DMA((n,)), pltpu.SemaphoreType.DMA((n,))], start, finish)


def _to_chips(srcs, parts=None, lands=None):
    n = len(srcs)
    parts = [ALL] * n if parts is None else parts
    lands = [None] * n if lands is None else lands
    given = [i for i in range(n) if lands[i] is not None]

    def rows(ref, i):
        hr = srcs[i].shape[1]
        r0, r1 = hr * parts[i][0] // parts[i][2], hr * parts[i][1] // parts[i][2]
        return ref.at[pl.ds(r0, r1 - r0), :]

    def copies(ins, outs, sems):
        x, y, c, chips = _place()
        me = 2 * x + y
        return [_rcopy(rows(ins[i].at[2 * chip[0] + chip[1]] if srcs[i].shape[0] == NCHIP else ins[i].at[c], i),
                       rows(outs[i].at[me], i), sems[0].at[3 * i + j], sems[1].at[3 * i + j], (*chip, c))
                for i in range(n) for j, chip in enumerate(chips)]

    def start(ins, outs, sems):
        for cp in copies(ins, outs, sems):
            cp.start()

    def finish(ins, outs, sems):
        for cp in copies(ins, outs, sems):
            cp.wait()

    shapes = [jax.ShapeDtypeStruct((NCHIP,) + s.shape[1:], s.dtype) for s in srcs]
    alias = {n + q: i for q, i in enumerate(given)}
    return _Stage(list(srcs) + [lands[i] for i in given], shapes, alias, _sem2(n, 3), start, finish)


HBM_REF = pl.BlockSpec(memory_space=pltpu.HBM)
SEM_REF = pl.BlockSpec(memory_space=pltpu.SEMAPHORE)
DATAFLOW = pltpu.SideEffectType.DATAFLOW_SIDE_EFFECTING


def _after(x):
    return _Stage([x], [], {}, [], lambda *a: None, lambda *a: None)


class _Flight:
    def __init__(self, stage, sems, bufs):
        self.stage, self.sems, self.bufs = stage, list(sems), list(bufs)

    def landed(self):
        st, n = self.stage, len(self.stage.operands)
        fresh = [j for j in range(len(st.out_shape)) if j not in st.alias.values()]
        back = {v: k for k, v in st.alias.items()}
        return [self.bufs[back[j]] if j in back else self.bufs[n + fresh.index(j)] for j in range(len(st.out_shape))]


def _split_call(name, finish=(), start=(), after=None):
    bufs, stage_bufs = [], []

    def slot(a):
        for i, b in enumerate(bufs):
            if b is a:
                return i
        bufs.append(a)
        return len(bufs) - 1

    fin_slots = [[slot(b) for b in fl.bufs] for fl in finish]
    for st in start:
        fresh = [lax.empty(o.shape, o.dtype) for j, o in enumerate(st.out_shape) if j not in st.alias.values()]
        stage_bufs.append([slot(a) for a in list(st.operands) + fresh])
    old_sems = [s for fl in finish for s in fl.sems]
    new_sems = [s for st in start for s in st.sems]
    nb, no, nn = len(bufs), len(old_sems), len(new_sems)

    def refs_of(st, slots, buf_refs):
        n = len(st.operands)
        ins = [buf_refs[i] for i in slots[:n]]
        fresh = [j for j in range(len(st.out_shape)) if j not in st.alias.values()]
        back = {v: k for k, v in st.alias.items()}
        outs = [ins[back[j]] if j in back else buf_refs[slots[n + fresh.index(j)]] for j in range(len(st.out_shape))]
        return ins, outs

    def body(*refs):
        buf_refs, sem_in = refs[:nb], refs[nb:nb + no]
        sem_out = refs[nb + no + (after is not None):][:nn]
        token = refs[-1]
        pos = 0
        for fl, slots in zip(finish, fin_slots):
            ins, outs = refs_of(fl.stage, slots, buf_refs)
            fl.stage.finish(ins, outs, sem_in[pos:pos + len(fl.sems)])
            pos += len(fl.sems)
        pos = 0
        for st, slots in zip(start, stage_bufs):
            ins, outs = refs_of(st, slots, buf_refs)
            st.start(ins, outs, sem_out[pos:pos + len(st.sems)])
            pos += len(st.sems)
        token[...] = jnp.zeros_like(token)

    res = pl.pallas_call(
        body, name=name,
        out_shape=tuple(new_sems) + tuple(pltpu.HBM(b.shape, b.dtype) for b in bufs) +
                  (jax.ShapeDtypeStruct((8, LANE), F32),),
        in_specs=(HBM_REF,) * nb + (SEM_REF,) * no + ((pl.BlockSpec(memory_space=pl.ANY),) if after is not None else ()),
        out_specs=(SEM_REF,) * nn + (HBM_REF,) * nb + (pl.BlockSpec(memory_space=pltpu.VMEM),),
        input_output_aliases={i: nn + i for i in range(nb)},
        compiler_params=pltpu.CompilerParams(has_side_effects=DATAFLOW),
    )(*_hbm(*bufs), *old_sems, *([after] if after is not None else []))
    sems, thru, token = res[:nn], res[nn:nn + nb], res[-1]
    for fl, slots in zip(finish, fin_slots):
        fl.bufs = [thru[i] for i in slots]
    flights, pos = [], 0
    for st, slots in zip(start, stage_bufs):
        flights.append(_Flight(st, sems[pos:pos + len(st.sems)], [thru[i] for i in slots]))
        pos += len(st.sems)
    return flights, token


def _share(pairs):
    n = len(pairs)

    def start(ins, outs, sems):
        x, y, c, _ = _place()
        for i in range(n):
            _rcopy(outs[i].at[c], outs[i].at[c], sems[0].at[i], sems[1].at[i], (x, y, 1 - c)).start()

    def finish(ins, outs, sems):
        x, y, c, _ = _place()
        for i in range(n):
            _rcopy(outs[i].at[c], outs[i].at[c], sems[0].at[i], sems[1].at[i], (x, y, 1 - c)).wait_send()
            _rcopy(outs[i].at[1 - c], outs[i].at[1 - c], sems[0].at[i], sems[1].at[i], (x, y, 1 - c)).wait_recv()

    return _Stage(pairs, [_sds(p) for p in pairs], {i: i for i in range(n)},
                  [pltpu.SemaphoreType.DMA((n,)), pltpu.SemaphoreType.DMA((n,))], start, finish)


def _row_block(rows, cols, itemsize=4, target=2 * MIB):
    br = rows
    while br * cols * itemsize > target and br % 32 == 0:
        br //= 2
    return br


def _cast_place(w, chip_idx, name):
    rows, cols = w.shape
    br = _row_block(rows, cols)

    def body(k_ref, w_ref, o_ref):
        o_ref[0] = w_ref[...].astype(BF)

    return _call(
        body, name=name, grid=(rows // br,), prefetch=chip_idx,
        in_specs=[pl.BlockSpec((br, cols), lambda r, k: (r, 0))],
        out_specs=[pl.BlockSpec((1, br, cols), lambda r, k: (k[0], r, 0))],
        out_shape=[jax.ShapeDtypeStruct((NCHIP, rows, cols), BF)], vmem=32, args=[w])[0][0]


def _cast_place_multi(ws, chip_idx, stages=()):
    br = 128
    nblk = [a.shape[0] // br for a in ws]
    starts = [sum(nblk[:i]) for i in range(len(ws))]

    def body(k_ref, *refs):
        r = pl.program_id(0)
        for i in range(len(ws)):
            @pl.when(jnp.logical_and(r >= starts[i], r < starts[i] + nblk[i]))
            def _(i=i):
                refs[len(ws) + i][0] = refs[i][...].astype(BF)

    def at(i):
        return functools.partial(lambda r, s, nb: jnp.clip(r - s, 0, nb - 1), s=starts[i], nb=nblk[i])

    outs, landed = _call(
        body, name="cast_rest", grid=(sum(nblk),), prefetch=chip_idx,
        in_specs=[pl.BlockSpec((br, a.shape[1]), functools.partial(lambda r, k, f: (f(r), 0), f=at(i)))
                  for i, a in enumerate(ws)],
        out_specs=[pl.BlockSpec((1, br, a.shape[1]), functools.partial(lambda r, k, f: (k[0], f(r), 0), f=at(i)))
                   for i, a in enumerate(ws)],
        out_shape=[jax.ShapeDtypeStruct((NCHIP,) + a.shape, BF) for a in ws], vmem=32, args=list(ws), stages=stages)
    return outs, landed


def _add_sibling(g, land, cidx, name, stages=()):
    _, _, hr, cols = g.shape
    br = _row_block(hr, cols)

    def body(c_ref, g_ref, l_ref, o_ref):
        o_ref[...] = (g_ref[0, 0].astype(F32) + l_ref[0].astype(F32)).astype(BF)[None]

    outs, st = _call(
        body, name=name, grid=(NCHIP, hr // br), prefetch=cidx,
        in_specs=[pl.BlockSpec((1, 1, br, cols), lambda k, r, c: (k, c[0], r, 0)),
                  pl.BlockSpec((1, br, cols), lambda k, r, c: (k, r, 0))],
        out_specs=[pl.BlockSpec((1, br, cols), lambda k, r, c: (k, r, 0))],
        out_shape=[jax.ShapeDtypeStruct((NCHIP, hr, cols), BF)], vmem=32, args=[g, land], stages=stages)
    return outs[0], st


def _add_sibling_multi(gs, lands, cidx, name):
    n = len(gs)
    brs = [_row_block(g.shape[2], g.shape[3]) for g in gs]
    nrb = [g.shape[2] // b for g, b in zip(gs, brs)]
    nblk = [NCHIP * q for q in nrb]
    starts = [sum(nblk[:i]) for i in range(n)]

    def body(c_ref, *refs):
        r = pl.program_id(0)
        for i in range(n):
            g_ref, l_ref, o_ref = refs[2 * i], refs[2 * i + 1], refs[2 * n + i]

            @pl.when(jnp.logical_and(r >= starts[i], r < starts[i] + nblk[i]))
            def _():
                o_ref[...] = (g_ref[0, 0].astype(F32) + l_ref[0].astype(F32)).astype(BF)[None]

    def at(i, r):
        q = jnp.clip(r - starts[i], 0, nblk[i] - 1)
        return q // nrb[i], q % nrb[i]

    def g_spec(i):
        return pl.BlockSpec((1, 1, brs[i], gs[i].shape[3]),
                            functools.partial(lambda r, c, i: (at(i, r)[0], c[0], at(i, r)[1], 0), i=i))

    def l_spec(i):
        return pl.BlockSpec((1, brs[i], gs[i].shape[3]),
                            functools.partial(lambda r, c, i: (at(i, r)[0], at(i, r)[1], 0), i=i))

    return _call(
        body, name=name, grid=(sum(nblk),), prefetch=cidx,
        in_specs=[s for i in range(n) for s in (g_spec(i), l_spec(i))], out_specs=[l_spec(i) for i in range(n)],
        out_shape=[jax.ShapeDtypeStruct(l.shape, BF) for l in lands], vmem=32,
        args=[a for i in range(n) for a in (gs[i], lands[i])])[0]


def _add_pair(a, b, name):
    rows, cols = a.shape

    def body(a_ref, b_ref, o_ref):
        o_ref[...] = a_ref[...] + b_ref[...]

    spec = pl.BlockSpec((rows, cols), lambda r: (0, 0))
    return _call(body, name=name, grid=(1,), in_specs=[spec, spec], out_specs=[spec], out_shape=[_sds(a)],
                 vmem=32, args=[a, b])[0][0]


def _add_chips(own, land, idx, name, stages=None):
    _, hr, cols = land.shape
    br = _row_block(hr, cols)

    def body(s_ref, a_ref, b_ref, c_ref, d_ref, o_ref):
        o_ref[...] = (a_ref[...].astype(F32) + b_ref[...].astype(F32)) + (c_ref[...].astype(F32) +
                                                                           d_ref[...].astype(F32))

    spec = lambda q: pl.BlockSpec((1, br, cols), functools.partial(lambda r, s, q: (s[q], r, 0), q=q))
    outs, landed = _call(
        body, name=name, grid=(hr // br,), prefetch=idx,
        in_specs=[spec(0), spec(1), spec(2), spec(3)], out_specs=[spec(4)],
        out_shape=[jax.ShapeDtypeStruct((2, hr, cols), F32)], vmem=48, args=[own, land, land, land],
        stages=stages or ())
    return outs[0] if stages is None else (outs[0], landed)


def _add_chips_multi(owns, lands, idx, name, stages=()):
    n = len(owns)
    brs = [_row_block(l.shape[1], l.shape[2]) for l in lands]
    nblk = [l.shape[1] // b for l, b in zip(lands, brs)]
    starts = [sum(nblk[:i]) for i in range(n)]

    def body(s_ref, *refs):
        r = pl.program_id(0)
        for i in range(n):
            a_ref, b_ref, c_ref, d_ref = refs[4 * i:4 * i + 4]
            o_ref = refs[4 * n + i]

            @pl.when(jnp.logical_and(r >= starts[i], r < starts[i] + nblk[i]))
            def _():
                o_ref[...] = (a_ref[...].astype(F32) + b_ref[...].astype(F32)) + (c_ref[...].astype(F32) +
                                                                                   d_ref[...].astype(F32))

    def spec(i, q):
        return pl.BlockSpec((1, brs[i], lands[i].shape[2]), functools.partial(
            lambda r, s, q, st, nb: (s[q], jnp.clip(r - st, 0, nb - 1), 0), q=q, st=starts[i], nb=nblk[i]))

    outs, landed = _call(
        body, name=name, grid=(sum(nblk),), prefetch=idx,
        in_specs=[spec(i, q) for i in range(n) for q in range(4)], out_specs=[spec(i, 4) for i in range(n)],
        out_shape=[jax.ShapeDtypeStruct((2,) + l.shape[1:], F32) for l in lands], vmem=48,
        args=[a for i in range(n) for a in (owns[i], lands[i], lands[i], lands[i])], stages=stages)
    return outs, landed


def _adamw_math(w, g, m, v):
    mn = ADAM_B1 * m + (1.0 - ADAM_B1) * g
    vn = ADAM_B2 * v + (1.0 - ADAM_B2) * (g * g)
    m_hat = mn / (1.0 - ADAM_B1 ** ADAM_STEP)
    v_hat = vn / (1.0 - ADAM_B2 ** ADAM_STEP)
    return -ADAM_LR * (m_hat / (jnp.sqrt(v_hat) + ADAM_EPS) + ADAM_WD * w), mn, vn


def _adamw(w, g, m, v, name, stages=()):
    rows, cols = w.shape
    br = _row_block(rows, cols)

    def body(w_ref, g_ref, m_ref, v_ref, go_ref, d_ref, mo_ref, vo_ref):
        gv = g_ref[...]
        go_ref[...] = gv
        d_ref[...], mo_ref[...], vo_ref[...] = _adamw_math(w_ref[...], gv, m_ref[...], v_ref[...])

    spec = pl.BlockSpec((br, cols), lambda r: (r, 0))
    return _call(body, name=name, grid=(rows // br,), in_specs=[spec] * 4, out_specs=[spec] * 4,
                 out_shape=[_sds(w)] * 4, vmem=56, args=[w, g, m, v], stages=stages)


def _adamw_multi(names, w, g, m, v, stages=()):
    cols = w[names[0]].shape[1]
    br = 128
    nblk = [w[n].shape[0] // br for n in names]
    starts = [sum(nblk[:i]) for i in range(len(names))]

    def body(*refs):
        r = pl.program_id(0)
        for i in range(len(names)):
            w_ref, g_ref, m_ref, v_ref = refs[4 * i:4 * i + 4]
            go_ref, d_ref, mo_ref, vo_ref = refs[4 * len(names) + 4 * i:4 * len(names) + 4 * i + 4]

            @pl.when(jnp.logical_and(r >= starts[i], r < starts[i] + nblk[i]))
            def _():
                gv = g_ref[...]
                go_ref[...] = gv
                d_ref[...], mo_ref[...], vo_ref[...] = _adamw_math(w_ref[...], gv, m_ref[...], v_ref[...])

    def spec(i):
        return pl.BlockSpec((br, cols), functools.partial(
            lambda r, s, nb: (jnp.clip(r - s, 0, nb - 1), 0), s=starts[i], nb=nblk[i]))

    outs, landed = _call(
        body, name="adamw_" + "_".join(names), grid=(sum(nblk),),
        in_specs=[spec(i) for i in range(len(names)) for _ in range(4)],
        out_specs=[spec(i) for i in range(len(names)) for _ in range(4)],
        out_shape=[_sds(w[n]) for n in names for _ in range(4)], vmem=56,
        args=[a[n] for n in names for a in (w, g, m, v)], stages=stages)
    return {n: outs[4 * i:4 * i + 4] for i, n in enumerate(names)}, landed


def _to_everyone(v):
    deltas = [(a, b, e) for a in (0, 1) for b in (0, 1) for e in (0, 1)][1:]

    def copies(ins, outs, sems):
        x, y, c, _ = _place()
        me = 4 * x + 2 * y + c
        flip = lambda p, f: 1 - p if f else p
        return [_rcopy(ins[0], outs[0].at[me], sems[0].at[q], sems[1].at[q], (flip(x, a), flip(y, b), flip(c, e)))
                for q, (a, b, e) in enumerate(deltas)]

    def start(ins, outs, sems):
        for cp in copies(ins, outs, sems):
            cp.start()

    def finish(ins, outs, sems):
        for cp in copies(ins, outs, sems):
            cp.wait()

    n = len(deltas)
    return _Stage([v], [jax.ShapeDtypeStruct((2 * NCHIP,) + v.shape, v.dtype)], {},
                  [pltpu.SemaphoreType.DMA((n,)), pltpu.SemaphoreType.DMA((n,))], start, finish)


SMALL_AT = {"norm_mix_pre": (0, 1, D), "norm_mix_post": (1, 1, D), "norm_mlp_pre": (2, 1, D),
            "norm_mlp_post": (3, 1, D), "b_gate": (4, 2, D), "conv_b": (6, 1, D), "lru_b_a": (7, 1, D),
            "lru_b_x": (8, 1, D), "lru_lambda": (9, 1, D), "pool_scale": (10, 1, DP)}
SMALL_SEPARATE = ["conv_w", "lru_w_a", "lru_w_x", "pool_w"]


def _adamw_small(small_sum, first_all, sep_grads, w, m, v):
    packed, sep = list(SMALL_AT), list(SMALL_SEPARATE)
    names = packed + sep

    def body(*refs):
        s_ref, a_ref, refs = refs[0], refs[1], refs[2:]
        g_sep, refs = refs[:len(sep)], refs[len(sep):]
        nn = len(names)
        w_r, m_r, v_r, refs = refs[:nn], refs[nn:2 * nn], refs[2 * nn:3 * nn], refs[3 * nn:]
        g_out, refs = refs[:len(packed)], refs[len(packed):]
        d_o, m_o, v_o = refs[:nn], refs[nn:2 * nn], refs[2 * nn:3 * nn]
        for i, n in enumerate(names):
            if i == 0:
                g = a_ref[0:1, :]
                for q in range(1, 2 * NCHIP):
                    g = g + a_ref[q:q + 1, :]
                g_out[i][...] = g
            elif n in SMALL_AT:
                r0, nr, nc = SMALL_AT[n]
                g = jnp.concatenate([s_ref[r0 + q:r0 + q + 1, :nc] for q in range(nr)], axis=1)
                g_out[i][...] = g
            else:
                g = g_sep[i - len(packed)][...]
            d_o[i][...], m_o[i][...], v_o[i][...] = _adamw_math(w_r[i][...], g, m_r[i][...], v_r[i][...])

    ws = [w[n] for n in names]
    res = pl.pallas_call(
        body, name="adamw_small",
        out_shape=[_sds(w[n]) for n in packed] + [_sds(a) for a in ws] * 3,
        compiler_params=_cp(32),
    )(*_hbm(small_sum, first_all, *sep_grads, *ws, *[m[n] for n in names], *[v[n] for n in names]))
    nn, npk = len(names), len(packed)
    grad = dict(zip(packed, res[:npk]))
    delta = dict(zip(names, res[npk:npk + nn]))
    new_m = dict(zip(names, res[npk + nn:npk + 2 * nn]))
    new_v = dict(zip(names, res[npk + 2 * nn:]))
    return grad, delta, new_m, new_v


W_NAMES = ["norm_mix_pre", "norm_mix_post", "norm_mlp_pre", "norm_mlp_post", "w_in", "b_gate", "conv_w", "conv_b",
           "lru_w_a", "lru_b_a", "lru_w_x", "lru_b_x", "lru_lambda", "pool_w", "pool_scale", "w_lru_up",
           "w_pool_up", "w_o", "w_ff1", "w_ff2"]
BIG = ["w_in", "w_lru_up", "w_pool_up", "w_o", "w_ff1", "w_ff2"]


def _block_diag(w):
    hd = w.shape[-1]
    per = CB // hd
    w4 = w.reshape(NG, per, hd, hd)
    eye = jnp.eye(per, dtype=w.dtype)
    return jnp.einsum("gpij,pq->gpiqj", w4, eye).reshape(NG, CB, CB)


def _block_diag_extract(d, hd):
    per = CB // hd
    d5 = d.reshape(NG, per, hd, per, hd)
    return jnp.stack([d5[:, p, :, p, :] for p in range(per)], axis=1).reshape(NG * per, hd, hd)


def _halves(g):
    return g.reshape(NCHIP, 2, g.size // (g.shape[-1] * 2 * NCHIP), g.shape[-1])


def kernel(x, norm_mix_pre, norm_mix_post, norm_mlp_pre, norm_mlp_post, w_in, b_gate, conv_w, conv_b, lru_w_a, lru_b_a, lru_w_x, lru_b_x, lru_lambda, pool_w, pool_scale, w_lru_up, w_pool_up, w_o, w_ff1, w_ff2, loss_target, m_norm_mix_pre, m_norm_mix_post, m_norm_mlp_pre, m_norm_mlp_post, m_w_in, m_b_gate, m_conv_w, m_conv_b, m_lru_w_a, m_lru_b_a, m_lru_w_x, m_lru_b_x, m_lru_lambda, m_pool_w, m_pool_scale, m_w_lru_up, m_w_pool_up, m_w_o, m_w_ff1, m_w_ff2, v_norm_mix_pre, v_norm_mix_post, v_norm_mlp_pre, v_norm_mlp_post, v_w_in, v_b_gate, v_conv_w, v_conv_b, v_lru_w_a, v_lru_b_a, v_lru_w_x, v_lru_b_x, v_lru_lambda, v_pool_w, v_pool_scale, v_w_lru_up, v_w_pool_up, v_w_o, v_w_ff1, v_w_ff2):
    args = dict(locals())
    two_d = lambda a: a.reshape(-1, a.shape[-1])
    w = {n: two_d(args[n]) for n in W_NAMES}
    mom = {n: two_d(args["m_" + n]) for n in W_NAMES}
    var = {n: two_d(args["v_" + n]) for n in W_NAMES}
    i32 = lambda val: jnp.asarray(val, jnp.int32)
    chip = i32(2 * lax.axis_index("x") + lax.axis_index("y"))
    core = i32(lax.axis_index("c"))
    cidx = core.reshape(1)
    zero = i32(0)
    hd = lru_w_a.shape[-1]
    xs, target = x[0], loss_target[0]
    g1, g2, g3, g4 = norm_mix_pre, norm_mix_post, norm_mlp_pre, norm_mlp_post

    mix = ["w_lru_up", "w_pool_up", "w_o"]
    full = {"w_in": _cast_place(w["w_in"], chip.reshape(1), "cast_w_in")}
    (fl_in, fl_conv), first = _split_call("gather_start_first", start=[
        _gather([full["w_in"]], ici=[(0, ALL)]), _gather_whole(w["conv_w"])])
    casts, _ = _cast_place_multi([w[n] for n in BIG[1:]], chip.reshape(1), stages=[_after(first)])
    full.update(zip(BIG[1:], casts))
    (fl_mix, fl_ff1, fl_ff2), started = _split_call("gather_start_rest", start=[
        _gather([full[n] for n in mix], ici=[(0, ALL), (1, ALL), (2, ALL)]),
        _gather([full["w_ff1"]], ici=[(0, ALL)]), _gather([full["w_ff2"]], ici=[(0, ALL)])])
    wa = _block_diag(lru_w_a[0]).astype(BF)
    wx = _block_diag(lru_w_x[0]).astype(BF)
    pw = pool_w[0].astype(BF)

    def to_sibling(name, flight, after=None):
        (fl,), passed = _split_call(name + "_pass", finish=[flight], after=after,
                                    start=[_gather(flight.landed(), d2d=[(i, ALL) for i in range(len(flight.bufs))])])
        passed_on.append(passed)
        return fl

    passed_on = []

    def arrived(name, flight, after=None):
        _split_call(name + "_done", finish=[flight], after=after)
        return flight.landed()

    idx_big = jnp.stack([chip, (chip + 1) % NCHIP, (chip + 2) % NCHIP, (chip + 3) % NCHIP, core])
    proj, h1 = _fwd_inproj_own(xs, g1, fl_in.bufs[0], idx_big, stages=[_after(started)])
    fl_in = to_sibling("gather_w_in", fl_in, after=h1)
    _split_call("gather_w_in_done", finish=[fl_in, fl_conv])
    (w_in_f,), (conv_all,) = fl_in.landed(), fl_conv.landed()
    full["w_in"] = w_in_f
    conv_all = lax.dynamic_update_slice(conv_all, w["conv_w"][None], (chip, zero, zero))
    conv_full = jnp.transpose(conv_all, (1, 0, 2)).reshape(4, DR)
    proj = _fwd_inproj_rest(h1, w_in_f, proj, idx_big)
    fl_mix = to_sibling("gather_mix", fl_mix, after=proj)
    (ylru, hs), _ = _fwd_lru(proj, conv_full, conv_b, wa, lru_b_a, wx, lru_b_x, lru_lambda,
                             stages=[_after(passed_on[-1])])
    got = arrived("gather_mix", fl_mix, after=ylru)
    fl_ff1 = to_sibling("gather_ff1", fl_ff1, after=ylru)
    w_lru_up_f, w_pool_up_f, w_o_f = got[0].reshape(DR, D), got[1], got[2].reshape(D, D)
    ypool = _fwd_pool(proj, pw, pool_scale)
    (x2, h2, m, mrg, bra, brb), _ = _fwd_merge(xs, ylru, ypool, proj, b_gate, g2, g3, w_lru_up_f, w_pool_up_f, w_o_f,
                                               stages=[_after(passed_on[-1])])
    fl_ff2 = to_sibling("gather_ff2", fl_ff2, after=h2)
    _split_call("gather_ff_done", finish=[fl_ff1, fl_ff2])
    (ff1,), (ff2,) = fl_ff1.landed(), fl_ff2.landed()
    ff2 = ff2.reshape(DF, D)
    a1, lossp, dy, df, dg4 = _fwd_mlp_loss(h2, ff1, ff2, x2, target, g4)

    dh2, df1 = _bwd_mlp_x(df, a1, ff1, ff2)
    dw_ff1, dw_ff2 = _bwd_mlp_w(df, h2, a1, df1)
    g_ff = [_halves(dw_ff1), _halves(dw_ff2)]
    (dxres, dgates, dylru, dypool, dm, dbra, dbrb, dg2, dg3, dbg), (l_ff,) = _bwd_merge(
        dh2, dy, x2, m, bra, brb, proj, b_gate, g2, g3, w_lru_up_f, w_pool_up_f, w_o_f, stages=[_to_sibling(g_ff)])
    p_ff = _add_sibling_multi(g_ff, l_ff, cidx, "add_sibling_ff")
    (fl_ff,), sent_ff = _split_call("reduce_ff_start", start=[_to_chips(p_ff)])
    (dw_o, dw_lru_up, dw_pool_up), _ = _dw_merge(mrg, dm, ylru, dbra, ypool, dbrb, stages=[_after(sent_ff)])
    g_mix = [_halves(dw_lru_up), _halves(dw_pool_up), _halves(dw_o)]
    (dxp, dgl, dcw, dcb, dwa, dba, dwx, dbx, dlam), (l_mix,) = _bwd_lru(
        proj, hs, dylru, conv_full, conv_b, wa, lru_b_a, wx, lru_b_x, lru_lambda, stages=[_to_sibling(g_mix)])
    p_mix = _add_sibling_multi(g_mix, l_mix, cidx, "add_sibling_mix")
    dxpool, dpw, dsc = _bwd_pool(proj, dypool, pw, pool_scale)
    dproj = [dxp, dgl, dxpool, dgates]
    small = jnp.concatenate([
        jnp.zeros((1, D), F32), dg2, dg3, dg4, dbg.reshape(2, D), dcb, dba, dbx, dlam,
        jnp.pad(dsc, ((0, 0), (0, D - DP))), jnp.pad(lossp, ((0, 0), (0, D - 1))), dcw,
        _block_diag_extract(dwa, hd).reshape(-1, D), _block_diag_extract(dwx, hd).reshape(-1, D),
        dpw.reshape(-1, D)], axis=0)
    (fl_mixr, fl_smalls), sent_mix = _split_call("reduce_mix_start", start=[_to_chips(p_mix), _to_sibling([small])])
    dw_in = _bwd_inproj_w(h1, dproj, sent_mix)
    _split_call("reduce_small_sibling_done", finish=[fl_smalls], after=dw_in)
    small, l_small = fl_smalls.bufs
    small2 = _add_pair(small, l_small, "add_sibling_small").reshape(2, SMALL_ROWS // 2, D)
    g_in = _halves(dw_in)
    done = ["w_ff1", "w_ff2"] + mix
    (fl_gin, fl_small), sib_started = _split_call("reduce_in_sibling_start", finish=[fl_ff, fl_mixr],
                                                  start=[_to_sibling([g_in]), _to_chips([small2])])
    p_ff1, p_ff2, c_ff1, c_ff2 = fl_ff.bufs
    p_mix, c_mix = fl_mixr.bufs[:3], fl_mixr.bufs[3:]
    pairs, _ = _add_chips_multi([p_ff1, p_ff2] + p_mix, [c_ff1, c_ff2] + c_mix, idx_big, "add_chips_done",
                                stages=[_after(sib_started)])
    _split_call("reduce_in_sibling_done", finish=[fl_gin], after=pairs[-1])
    g_in, l_in = fl_gin.bufs
    p_in = _add_sibling(g_in, l_in, cidx, "add_sibling_w_in")[0]
    (fl_pin,), token = _split_call("reduce_last_start", start=[_to_chips([p_in])])
    _split_call("reduce_small_done", finish=[fl_small], after=token)
    small2, c_small = fl_small.bufs
    own_small = lax.dynamic_index_in_dim(small2, core, 0, keepdims=True)
    c_small = lax.dynamic_update_slice(c_small, own_small, (chip, zero, zero))
    pair_small = _add_chips(c_small, c_small, jnp.stack([zero, zero + 1, zero + 2, zero + 3, core]), "add_chips_small")
    (fl_share,), shared_start = _split_call("reduce_share_start", start=[_share(pairs + [pair_small])])
    grad_x, dg1 = _bwd_inproj_x(dproj, full["w_in"], xs, dxres, g1, stages=[_after(shared_start)])
    _split_call("reduce_share_done", finish=[fl_share, fl_pin], after=dg1)
    shared, (p_in, c_in) = fl_share.landed(), fl_pin.bufs
    pairs, pair_small = shared[:-1], shared[-1]

    grads, delta, new_m, new_v = {}, {}, {}, {}
    for n, p in zip(done, pairs):
        grads[n] = p.reshape(-1, p.shape[-1])

    def update(n, stages=()):
        (grads[n], delta[n], new_m[n], new_v[n]), landed = _adamw(w[n], grads[n], mom[n], var[n], "adamw_" + n,
                                                                  stages=stages)
        return landed

    pair_in = _add_chips(p_in, c_in, idx_big, "add_chips_w_in")
    (fl_last, fl_dg1), last_start = _split_call("reduce_last_share_start", start=[_share([pair_in]), _to_everyone(dg1)])
    updated, _ = _adamw_multi(["w_ff1", "w_ff2", "w_o", "w_lru_up"], w, grads, mom, var, stages=[_after(last_start)])
    for n, (go, d, mo, vo) in updated.items():
        grads[n], delta[n], new_m[n], new_v[n] = go, d, mo, vo
    _split_call("reduce_last_share_done", finish=[fl_last, fl_dg1], after=new_v["w_lru_up"])
    (pair_in,), (dg1, dg1_all) = fl_last.landed(), fl_dg1.bufs
    dg1_all = lax.dynamic_update_slice(dg1_all, dg1[None], (2 * chip + core, zero, zero)).reshape(2 * NCHIP, D)
    grads["w_in"] = pair_in.reshape(-1, pair_in.shape[-1])
    update("w_pool_up")
    update("w_in")
    small_sum = pair_small.reshape(SMALL_ROWS, D)
    loss = 0.5 * small_sum[LOSS_ROW, 0]
    ccols = DR // NCHIP
    sep = [lax.dynamic_slice(small_sum[12:16], (zero, chip * ccols), (4, ccols)),
           small_sum[16:80].reshape(-1, hd), small_sum[80:144].reshape(-1, hd), small_sum[144:208].reshape(-1, PG)]
    g_s, d_s, m_s, v_s = _adamw_small(small_sum, dg1_all, sep, w, mom, var)
    grads.update(g_s)
    grads.update(dict(zip(SMALL_SEPARATE, sep)))
    delta.update(d_s)
    new_m.update(m_s)
    new_v.update(v_s)

    out = lambda d: [d[n].reshape(args[n].shape) for n in W_NAMES]
    return (loss, grad_x[None], *out(grads), *out(delta), *out(new_m), *out(new_v))
```

```python
import functools
import math

import jax
import jax.numpy as jnp
from jax import lax
from jax.experimental import pallas as pl
from jax.experimental.pallas import tpu as pltpu

F32 = jnp.float32
BF = jnp.bfloat16

T = 2048
D = 1024
DR = 1024
DP = 512
DF = 4096
DIN = 4608
NCHIP = 4
CW_IN = DIN // NCHIP
LANE = 128
CB = 128
NG = DR // CB
PG = 128
POOL_WINDOWS = (2, 4, 8, 16)
NORM_EPS = 1e-6
LRU_C = 8.0
GELU_C = math.sqrt(2.0 / math.pi)
ADAM_LR = 0.001
ADAM_B1 = 0.9
ADAM_B2 = 0.999
ADAM_EPS = 1e-08
ADAM_WD = 0.01
ADAM_STEP = 10
MESH_ID = pl.DeviceIdType.MESH
ANY = pl.BlockSpec(memory_space=pl.ANY)
SMALL_ROWS = 208
LOSS_ROW = 11
MIB = 1 << 20


def _cp(vmem_mib=None):
    if vmem_mib is None:
        return pltpu.CompilerParams()
    return pltpu.CompilerParams(vmem_limit_bytes=vmem_mib * MIB)


def _hbm(*arrays):
    return [pltpu.with_memory_space_constraint(a, pltpu.HBM) for a in arrays]


def _hbm_out(shapes):
    return [pltpu.HBM(s.shape, s.dtype) for s in shapes]


class _Stage:
    def __init__(self, operands, out_shape, alias, sems, start, finish):
        self.operands, self.out_shape, self.alias, self.sems = list(operands), list(out_shape), dict(alias), list(sems)
        self.start, self.finish = start, finish


def _call(body, *, name, grid, in_specs, out_specs, out_shape, args, vmem=None, stages=(), prefetch=None,
          scratch=()):
    nin, nout = len(in_specs), len(out_specs)
    npre = 0 if prefetch is None else 1
    st_args, st_shapes, st_sems, aliases = [], [], list(scratch), {}
    for st in stages:
        for k, v in st.alias.items():
            aliases[npre + nin + len(st_args) + k] = nout + len(st_shapes) + v
        st_args += st.operands
        st_shapes += st.out_shape
        st_sems += st.sems

    def wrapped(*refs):
        pre, refs = refs[:npre], refs[npre:]
        ins, pos = refs[:nin], nin
        st_ins = []
        for st in stages:
            st_ins.append(refs[pos:pos + len(st.operands)])
            pos += len(st.operands)
        outs, pos = refs[pos:pos + nout], pos + nout
        st_outs = []
        for st in stages:
            st_outs.append(refs[pos:pos + len(st.out_shape)])
            pos += len(st.out_shape)
        work, pos = refs[pos:pos + len(scratch)], pos + len(scratch)
        sems = []
        for st in stages:
            sems.append(refs[pos:pos + len(st.sems)])
            pos += len(st.sems)
        if stages:
            first = functools.reduce(jnp.logical_and, [pl.program_id(a) == 0 for a in range(len(grid))])

            @pl.when(first)
            def _():
                for st, a, b, s in zip(stages, st_ins, st_outs, sems):
                    st.start(a, b, s)

        body(*pre, *ins, *outs, *work)
        if stages:
            last = functools.reduce(jnp.logical_and, [pl.program_id(a) == g - 1 for a, g in enumerate(grid)])

            @pl.when(last)
            def _():
                for st, a, b, s in zip(stages, st_ins, st_outs, sems):
                    st.finish(a, b, s)

    all_in = list(in_specs) + [ANY] * len(st_args)
    all_out = list(out_specs) + [ANY] * len(st_shapes)
    kw = dict(has_side_effects=True) if stages else {}
    if vmem is not None:
        kw["vmem_limit_bytes"] = vmem * MIB
    if prefetch is None:
        gkw = dict(grid=grid, in_specs=all_in, out_specs=all_out, scratch_shapes=st_sems)
    else:
        gkw = dict(grid_spec=pltpu.PrefetchScalarGridSpec(
            num_scalar_prefetch=1, grid=grid, in_specs=all_in, out_specs=all_out, scratch_shapes=st_sems))
    res = pl.pallas_call(
        wrapped, name=name, out_shape=_hbm_out(list(out_shape) + st_shapes), input_output_aliases=aliases,
        compiler_params=pltpu.CompilerParams(**kw), **gkw,
    )(*([prefetch] if npre else []), *_hbm(*args, *st_args))
    outs, rest, st_res = list(res[:nout]), list(res[nout:]), []
    for st in stages:
        st_res.append(rest[:len(st.out_shape)])
        rest = rest[len(st.out_shape):]
    return outs, st_res


def _mm(a, b):
    return jnp.dot(a.astype(BF), b.astype(BF), preferred_element_type=F32)


def _mm_nt(a, b):
    return lax.dot_general(a.astype(BF), b.astype(BF), (((1,), (1,)), ((), ())),
                           preferred_element_type=F32)


def _mm_tn(a, b):
    return lax.dot_general(a.astype(BF), b.astype(BF), (((0,), (0,)), ((), ())),
                           preferred_element_type=F32)


def _rows(v):
    return lax.broadcasted_iota(jnp.int32, v.shape, 0)


def _sd(v, s, fill=0.0):
    return jnp.where(_rows(v) >= s, pltpu.roll(v, s, axis=0), fill)


def _su(v, s, fill=0.0):
    n = v.shape[0]
    return jnp.where(_rows(v) < n - s, pltpu.roll(v, n - s, axis=0), fill)


def _sigmoid(z):
    return 1.0 / (1.0 + jnp.exp(-z))


def _softplus(z):
    e = jnp.exp(-jnp.abs(z))
    u = 1.0 + e
    d = u - 1.0
    log1p = jnp.where(d == 0.0, e, jnp.log(u) * (e / jnp.where(d == 0.0, 1.0, d)))
    return jnp.maximum(z, 0.0) + log1p


def _mean(v):
    return jnp.mean(v, axis=-1, keepdims=True)


def _colsum(v):
    return jnp.sum(v, axis=0, keepdims=True)


def _acc(ref, val, first):
    @pl.when(first)
    def _():
        ref[...] = val

    @pl.when(jnp.logical_not(first))
    def _():
        ref[...] += val


def _conv(xp, cw, cb):
    x1, x2, x3 = _sd(xp, 1), _sd(xp, 2), _sd(xp, 3)
    xc = cb + cw[0:1] * x3 + cw[1:2] * x2 + cw[2:3] * x1 + cw[3:4] * xp
    return xc, x1, x2, x3


def _lru_gates(xc, wa, ba, wx, bx, lam):
    xcb = xc.astype(BF)
    r = _sigmoid(_mm(xcb, wa) + ba)
    ii = _sigmoid(_mm(xcb, wx) + bx)
    sp = _softplus(-lam)
    la = (-LRU_C) * r * sp
    a = jnp.exp(la)
    mult = jnp.sqrt(-jnp.tanh(la) * (a * a + 1.0))
    return xcb, r, ii, sp, a, mult


def _gelu_parts(g):
    th = jnp.tanh(GELU_C * (g + 0.044715 * (g * g * g)))
    gel = 0.5 * g * (1.0 + th)
    dgel = 0.5 * (1.0 + th) + 0.5 * g * (1.0 - th * th) * (GELU_C * (1.0 + 3.0 * 0.044715 * (g * g)))
    return gel, dgel


def _tile_scan(a, b, a_s, b_s, out_ref, reverse):
    n, lanes = a.shape
    nt = n // 8
    a, b = a.reshape(nt, 8, lanes), b.reshape(nt, 8, lanes)
    sub = lax.broadcasted_iota(jnp.int32, a.shape, 1)
    s = 1
    while s < 8:
        keep = sub < 8 - s if reverse else sub >= s
        amount = 8 - s if reverse else s
        b = b + a * jnp.where(keep, pltpu.roll(b, amount, axis=1), 0.0)
        a = a * jnp.where(keep, pltpu.roll(a, amount, axis=1), 1.0)
        s *= 2
    a_s[...] = a.reshape(n, lanes)
    b_s[...] = b.reshape(n, lanes)
    edge = pl.ds(0 if reverse else 7, nt, stride=8)
    ta, tb = a_s[edge, :], b_s[edge, :]
    shift = _su if reverse else _sd
    s = 1
    while s < nt:
        tb = tb + ta * shift(tb, s, 0.0)
        if 2 * s < nt:
            ta = ta * shift(ta, s, 1.0)
        s *= 2
    enters = shift(tb, 1, 0.0)
    for o in range(8):
        rows = pl.ds(o, nt, stride=8)
        out_ref[rows, :] = b_s[rows, :] + a_s[rows, :] * enters


def _pool_window(x, steps, shift):
    s, sh = x, 1
    for _ in range(steps):
        s = s + shift(s, sh)
        sh *= 2
    return s


def _fwd_inproj_own(x, g1, w_in, slots, stages=()):
    tm = 1024

    def body(s_ref, x_ref, g_ref, w_ref, proj_ref, h_ref, ht_ref):
        xv = x_ref[...]
        r = lax.rsqrt(_mean(xv * xv) + NORM_EPS)
        hf = (xv * r) * g_ref[...]
        h = hf.astype(BF)
        h_ref[...] = h
        ht_ref[...] = hf.T.astype(BF)
        proj_ref[...] = jnp.dot(h, w_ref[0], preferred_element_type=F32)

    return _call(
        body, name="fwd_inproj_own", grid=(T // tm,), prefetch=slots,
        in_specs=[pl.BlockSpec((tm, D), lambda i, s: (i, 0)),
                  pl.BlockSpec((1, D), lambda i, s: (0, 0)),
                  pl.BlockSpec((1, D, CW_IN), lambda i, s: (s[0], 0, 0))],
        out_specs=[pl.BlockSpec((tm, CW_IN), lambda i, s: (i, s[0])),
                   pl.BlockSpec((tm, D), lambda i, s: (i, 0)),
                   pl.BlockSpec((D, tm), lambda i, s: (0, i))],
        out_shape=[jax.ShapeDtypeStruct((T, DIN), F32), jax.ShapeDtypeStruct((T, D), BF),
                   jax.ShapeDtypeStruct((D, T), BF)],
        vmem=48, args=[x, g1, w_in], stages=stages)[0]


def _fwd_inproj_rest(h1, w_in, proj, slots):
    tm = 1024

    def body(s_ref, h_ref, w_ref, p_in, proj_ref):
        proj_ref[...] = jnp.dot(h_ref[...], w_ref[0], preferred_element_type=F32)

    res = pl.pallas_call(
        body, name="fwd_inproj_rest",
        grid_spec=pltpu.PrefetchScalarGridSpec(
            num_scalar_prefetch=1, grid=(NCHIP - 1, T // tm),
            in_specs=[pl.BlockSpec((tm, D), lambda k, i, s: (i, 0)),
                      pl.BlockSpec((1, D, CW_IN), lambda k, i, s: (s[1 + k], 0, 0)), ANY],
            out_specs=pl.BlockSpec((tm, CW_IN), lambda k, i, s: (i, s[1 + k]))),
        out_shape=pltpu.HBM((T, DIN), F32), input_output_aliases={3: 0},
        compiler_params=_cp(40),
    )(slots, *_hbm(h1, w_in, proj))
    return res


def _vec_spec():
    return pl.BlockSpec((1, CB), lambda j: (0, j))


def _fwd_lru(proj, conv_w, conv_b, wa, ba, wx, bx, lam, stages=()):
    def body(xp_ref, g_ref, cw_ref, cb_ref, wa_ref, ba_ref, wx_ref, bx_ref, lam_ref, y_ref, h_ref, a_s, b_s):
        xc, _, _, _ = _conv(xp_ref[...], cw_ref[...], cb_ref[...])
        _, _, ii, _, a, mult = _lru_gates(xc, wa_ref[0], ba_ref[...], wx_ref[0], bx_ref[...], lam_ref[...])
        _tile_scan(a, mult * (ii * xc), a_s, b_s, h_ref, reverse=False)
        gel, _ = _gelu_parts(g_ref[...])
        y_ref[...] = (h_ref[...] * gel).astype(BF)

    return _call(
        body, name="fwd_lru", grid=(NG,),
        in_specs=[pl.BlockSpec((T, CB), lambda j: (0, j)),
                  pl.BlockSpec((T, CB), lambda j: (0, NG + j)),
                  pl.BlockSpec((4, CB), lambda j: (0, j)),
                  _vec_spec(),
                  pl.BlockSpec((1, CB, CB), lambda j: (j, 0, 0)), _vec_spec(),
                  pl.BlockSpec((1, CB, CB), lambda j: (j, 0, 0)), _vec_spec(),
                  _vec_spec()],
        out_specs=[pl.BlockSpec((T, CB), lambda j: (0, j)), pl.BlockSpec((T, CB), lambda j: (0, j))],
        out_shape=[jax.ShapeDtypeStruct((T, DR), BF), jax.ShapeDtypeStruct((T, DR), F32)],
        vmem=48, args=[proj, proj, conv_w, conv_b, wa, ba, wx, bx, lam], stages=stages,
        scratch=[pltpu.VMEM((T, CB), F32)] * 2)


def _pool_cnt(w):
    t = lax.broadcasted_iota(jnp.int32, (T, 1), 0)
    return jnp.minimum(t + 1, w).astype(F32)


def _fwd_pool(proj, pool_w, pool_scale):
    def body(xp_ref, pw_ref, sc_ref, y_ref):
        for g, w in enumerate(POOL_WINDOWS):
            cols = slice(g * PG, (g + 1) * PG)
            x = xp_ref[:, cols]
            p = _pool_window(x, g + 1, _sd) / _pool_cnt(w) - x
            y_ref[:, cols] = (_mm(p, pw_ref[g]) * sc_ref[:, cols]).astype(BF)

    return pl.pallas_call(
        body, name="fwd_pool", grid=(1,),
        in_specs=[pl.BlockSpec((T, DP), lambda i: (0, 2 * DR // DP)),
                  pl.BlockSpec((4, PG, PG), lambda i: (0, 0, 0)),
                  pl.BlockSpec((1, DP), lambda i: (0, 0))],
        out_specs=pl.BlockSpec((T, DP), lambda i: (0, 0)),
        out_shape=pltpu.HBM((T, DP), BF),
        compiler_params=_cp(48),
    )(*_hbm(proj, pool_w, pool_scale))


GATE_BLK = 512
GATE_BLK0 = (2 * DR + DP) // GATE_BLK


def _gate_specs(tm):
    return [pl.BlockSpec((tm, GATE_BLK), functools.partial(lambda i, q: (i, GATE_BLK0 + q), q=q))
            for q in range(4)]


def _fwd_merge(x, ylru, ypool, proj, b_gate, g2, g3, w_lru_up, w_pool_up, w_o, stages=()):
    tm = 512

    def body(x_ref, yl_ref, yp_ref, p0, p1, p2, p3, bg_ref, g2_ref, g3_ref, wl_ref, wp_ref, wo_ref,
             x2_ref, h2_ref, m_ref, mrg_ref, bra_ref, brb_ref):
        bra = jnp.dot(yl_ref[...], wl_ref[...], preferred_element_type=F32)
        yp = yp_ref[...]
        brb = jnp.concatenate([jnp.dot(yp, wp_ref[k], preferred_element_type=F32) for k in range(NCHIP)], axis=1)
        bg = bg_ref[...]
        ga = _sigmoid(jnp.concatenate([p0[...], p1[...]], axis=1) + bg[:, :D])
        gb = _sigmoid(jnp.concatenate([p2[...], p3[...]], axis=1) + bg[:, D:])
        mrg = (ga * bra + gb * brb).astype(BF)
        m = jnp.dot(mrg, wo_ref[...], preferred_element_type=F32)
        r2 = lax.rsqrt(_mean(m * m) + NORM_EPS)
        x2 = x_ref[...] + (m * r2) * g2_ref[...]
        r3 = lax.rsqrt(_mean(x2 * x2) + NORM_EPS)
        x2_ref[...] = x2
        h2_ref[...] = ((x2 * r3) * g3_ref[...]).astype(BF)
        m_ref[...] = m
        mrg_ref[...] = mrg
        bra_ref[...] = bra.astype(BF)
        brb_ref[...] = brb.astype(BF)

    row = lambda w: pl.BlockSpec((tm, w), lambda i: (i, 0))
    full2 = lambda a, b: pl.BlockSpec((a, b), lambda i: (0, 0))
    return _call(
        body, name="fwd_merge", grid=(T // tm,),
        in_specs=[row(D), row(DR), row(DP)] + _gate_specs(tm) +
                 [full2(1, 2 * D), full2(1, D), full2(1, D), full2(DR, D),
                  pl.BlockSpec((NCHIP, DP, D // NCHIP), lambda i: (0, 0, 0)), full2(D, D)],
        out_specs=[row(D)] * 6,
        out_shape=[jax.ShapeDtypeStruct((T, D), F32), jax.ShapeDtypeStruct((T, D), BF),
                   jax.ShapeDtypeStruct((T, D), F32), jax.ShapeDtypeStruct((T, D), BF),
                   jax.ShapeDtypeStruct((T, D), BF), jax.ShapeDtypeStruct((T, D), BF)],
        vmem=48, args=[x, ylru, ypool, proj, proj, proj, proj, b_gate, g2, g3, w_lru_up, w_pool_up, w_o],
        stages=stages)


def _fwd_mlp_loss(h2, w_ff1, w_ff2, x2, target, g4):
    tm = 512
    fk = DF // NCHIP

    def body(h_ref, w1_ref, w2_ref, x2_ref, t_ref, g_ref, a1_ref, loss_ref, dy_ref, df_ref, dg_ref):
        first = pl.program_id(0) == 0
        h = h_ref[...]
        f = None
        for k in range(NCHIP):
            a1 = jnp.maximum(jnp.dot(h, w1_ref[k], preferred_element_type=F32), 0.0)
            a1_ref[:, k * fk:(k + 1) * fk] = a1.astype(BF)
            part = jnp.dot((a1 * a1).astype(BF), w2_ref[k * fk:(k + 1) * fk, :], preferred_element_type=F32)
            f = part if f is None else f + part
        g4v = g_ref[...]
        r4 = lax.rsqrt(_mean(f * f) + NORM_EPS)
        fn = f * r4
        e = (x2_ref[...] + fn * g4v) - t_ref[...]
        _acc(loss_ref, jnp.sum(_mean(e * e), axis=0, keepdims=True), first)
        dy = e * (1.0 / D)
        dy_ref[...] = dy
        _acc(dg_ref, _colsum(dy * fn), first)
        dfn = dy * g4v
        df_ref[...] = (r4 * (dfn - fn * _mean(dfn * fn))).astype(BF)

    row = pl.BlockSpec((tm, D), lambda i: (i, 0))
    return pl.pallas_call(
        body, name="fwd_mlp_loss", grid=(T // tm,),
        in_specs=[row, pl.BlockSpec((NCHIP, D, fk), lambda i: (0, 0, 0)), pl.BlockSpec((DF, D), lambda i: (0, 0)),
                  row, row, pl.BlockSpec((1, D), lambda i: (0, 0))],
        out_specs=[pl.BlockSpec((tm, DF), lambda i: (i, 0)), pl.BlockSpec((1, 1), lambda i: (0, 0)), row, row,
                   pl.BlockSpec((1, D), lambda i: (0, 0))],
        out_shape=_hbm_out([jax.ShapeDtypeStruct((T, DF), BF), jax.ShapeDtypeStruct((1, 1), F32),
                            jax.ShapeDtypeStruct((T, D), F32), jax.ShapeDtypeStruct((T, D), BF),
                            jax.ShapeDtypeStruct((1, D), F32)]),
        compiler_params=_cp(56),
    )(*_hbm(h2, w_ff1, w_ff2, x2, target, g4))


def _bwd_mlp_x(df, a1, w_ff1, w_ff2):
    tm = 512
    fk = DF // NCHIP

    def body(df_ref, a1_ref, w1_ref, w2_ref, dh_ref, df1_ref):
        df = df_ref[...]
        dh = None
        for k in range(NCHIP):
            cols = slice(k * fk, (k + 1) * fk)
            dact = _mm_nt(df, w2_ref[cols, :])
            df1 = (dact * (2.0 * a1_ref[:, cols].astype(F32))).astype(BF)
            df1_ref[:, cols] = df1
            part = _mm_nt(df1, w1_ref[k])
            dh = part if dh is None else dh + part
        dh_ref[...] = dh

    return pl.pallas_call(
        body, name="bwd_mlp_x", grid=(T // tm,),
        in_specs=[pl.BlockSpec((tm, D), lambda i: (i, 0)),
                  pl.BlockSpec((tm, DF), lambda i: (i, 0)),
                  pl.BlockSpec((NCHIP, D, fk), lambda i: (0, 0, 0)),
                  pl.BlockSpec((DF, D), lambda i: (0, 0))],
        out_specs=[pl.BlockSpec((tm, D), lambda i: (i, 0)), pl.BlockSpec((tm, DF), lambda i: (i, 0))],
        out_shape=_hbm_out([jax.ShapeDtypeStruct((T, D), F32), jax.ShapeDtypeStruct((T, DF), BF)]),
        compiler_params=_cp(56),
    )(*_hbm(df, a1, w_ff1, w_ff2))


def _bwd_mlp_w(df, h2, a1, df1):
    fc = 512
    per = (DF // NCHIP) // fc

    def body(df_ref, h_ref, a1_ref, df1_ref, dw1_ref, dw2_ref):
        a1 = a1_ref[...].astype(F32)
        dw2_ref[...] = _mm_tn((a1 * a1).astype(BF), df_ref[...]).astype(BF)
        dw1_ref[0] = _mm_tn(h_ref[...], df1_ref[...]).astype(BF)

    return pl.pallas_call(
        body, name="bwd_mlp_w", grid=(DF // fc,),
        in_specs=[pl.BlockSpec((T, D), lambda j: (0, 0)),
                  pl.BlockSpec((T, D), lambda j: (0, 0)),
                  pl.BlockSpec((T, fc), lambda j: (0, j)),
                  pl.BlockSpec((T, fc), lambda j: (0, j))],
        out_specs=[pl.BlockSpec((1, D, fc), lambda j: (j // per, 0, j % per)),
                   pl.BlockSpec((fc, D), lambda j: (j, 0))],
        out_shape=_hbm_out([jax.ShapeDtypeStruct((NCHIP, D, DF // NCHIP), BF),
                            jax.ShapeDtypeStruct((DF, D), BF)]),
        compiler_params=_cp(56),
    )(*_hbm(df, h2, a1, df1))


def _bwd_merge(dh2, dy, x2, m, bra, brb, proj, b_gate, g2, g3, w_lru_up, w_pool_up, w_o, stages=()):
    tm = 256
    cpu = D // NCHIP

    def body(dh2_ref, dy_ref, x2_ref, m_ref, bra_ref, brb_ref, p0, p1, p2, p3, bg_ref,
             g2_ref, g3_ref, wl_ref, wp_ref, wo_ref,
             dx_ref, dgt_ref, dyl_ref, dyp_ref, dm_ref, dbra_ref, dbrb_ref, dg2_ref, dg3_ref, dbg_ref):
        first = pl.program_id(0) == 0
        x2 = x2_ref[...]
        r3 = lax.rsqrt(_mean(x2 * x2) + NORM_EPS)
        x2n = x2 * r3
        dh2 = dh2_ref[...]
        t3 = dh2 * g3_ref[...]
        dx2 = dy_ref[...] + r3 * (t3 - x2n * _mean(t3 * x2n))
        dx_ref[...] = dx2
        _acc(dg3_ref, _colsum(dh2 * x2n), first)
        m = m_ref[...]
        r2 = lax.rsqrt(_mean(m * m) + NORM_EPS)
        mn = m * r2
        _acc(dg2_ref, _colsum(dx2 * mn), first)
        dmn = dx2 * g2_ref[...]
        dm = (r2 * (dmn - mn * _mean(dmn * mn))).astype(BF)
        dm_ref[...] = dm
        dmrg = _mm_nt(dm, wo_ref[...])
        bg = bg_ref[...]
        ga = _sigmoid(jnp.concatenate([p0[...], p1[...]], axis=1) + bg[:, :D])
        gb = _sigmoid(jnp.concatenate([p2[...], p3[...]], axis=1) + bg[:, D:])
        dga = dmrg * bra_ref[...].astype(F32) * (ga * (1.0 - ga))
        dgb = dmrg * brb_ref[...].astype(F32) * (gb * (1.0 - gb))
        dgt_ref[:, :D] = dga.astype(BF)
        dgt_ref[:, D:] = dgb.astype(BF)
        _acc(dbg_ref, jnp.concatenate([_colsum(dga), _colsum(dgb)], axis=1), first)
        dbra = (dmrg * ga).astype(BF)
        dbrb = (dmrg * gb).astype(BF)
        dbra_ref[...] = dbra
        dbrb_ref[...] = dbrb
        dyl_ref[...] = _mm_nt(dbra, wl_ref[...])
        dyp = None
        for k in range(NCHIP):
            part = _mm_nt(dbrb[:, k * cpu:(k + 1) * cpu], wp_ref[k])
            dyp = part if dyp is None else dyp + part
        dyp_ref[...] = dyp

    row = lambda w: pl.BlockSpec((tm, w), lambda i: (i, 0))
    full2 = lambda a, b: pl.BlockSpec((a, b), lambda i: (0, 0))
    wp_spec = pl.BlockSpec((NCHIP, DP, cpu), lambda i: (0, 0, 0))
    return _call(
        body, name="bwd_merge", grid=(T // tm,),
        in_specs=[row(D)] * 6 + _gate_specs(tm) +
                 [full2(1, 2 * D), full2(1, D), full2(1, D), full2(DR, D), wp_spec, full2(D, D)],
        out_specs=[row(D), row(2 * D), row(DR), row(DP), row(D), row(D), row(D),
                   full2(1, D), full2(1, D), full2(1, 2 * D)],
        out_shape=[jax.ShapeDtypeStruct((T, D), F32), jax.ShapeDtypeStruct((T, 2 * D), BF),
                   jax.ShapeDtypeStruct((T, DR), F32), jax.ShapeDtypeStruct((T, DP), F32),
                   jax.ShapeDtypeStruct((T, D), BF), jax.ShapeDtypeStruct((T, D), BF),
                   jax.ShapeDtypeStruct((T, D), BF),
                   jax.ShapeDtypeStruct((1, D), F32), jax.ShapeDtypeStruct((1, D), F32),
                   jax.ShapeDtypeStruct((1, 2 * D), F32)],
        vmem=56, args=[dh2, dy, x2, m, bra, brb, proj, proj, proj, proj, b_gate, g2, g3, w_lru_up, w_pool_up, w_o],
        stages=stages)


def _dw_merge(mrg, dm, ylru, dbra, ypool, dbrb, stages=()):
    nb = NCHIP
    rb, pb, cpu = D // nb, DP // nb, D // NCHIP

    def body(mrg_ref, dm_ref, yl_ref, dbra_ref, yp_ref, dbrb_ref, dwo_ref, dwl_ref, dwp_ref):
        dwo_ref[...] = _mm_tn(mrg_ref[...], dm_ref[...]).astype(BF)
        dwl_ref[...] = _mm_tn(yl_ref[...], dbra_ref[...]).astype(BF)
        dwp = _mm_tn(yp_ref[...], dbrb_ref[...]).astype(BF)
        for k in range(NCHIP):
            dwp_ref[k] = dwp[:, k * cpu:(k + 1) * cpu]

    cols = lambda w: pl.BlockSpec((T, w), lambda r: (0, r))
    whole = pl.BlockSpec((T, D), lambda r: (0, 0))
    return _call(
        body, name="dw_merge", grid=(nb,),
        in_specs=[cols(rb), whole, cols(rb), whole, cols(pb), whole],
        out_specs=[pl.BlockSpec((rb, D), lambda r: (r, 0)), pl.BlockSpec((rb, D), lambda r: (r, 0)),
                   pl.BlockSpec((NCHIP, pb, cpu), lambda r: (0, r, 0))],
        out_shape=[jax.ShapeDtypeStruct((D, D), BF), jax.ShapeDtypeStruct((DR, D), BF),
                   jax.ShapeDtypeStruct((NCHIP, DP, cpu), BF)],
        vmem=56, args=[mrg, dm, ylru, dbra, ypool, dbrb], stages=stages)


def _bwd_lru(proj, h, dylru, conv_w, conv_b, wa, ba, wx, bx, lam, stages=()):
    def body(xp_ref, g_ref, h_ref, dy_ref, cw_ref, cb_ref, wa_ref, ba_ref, wx_ref, bx_ref, lam_ref,
             dxp_ref, dg_ref, dcw_ref, dcb_ref, dwa_ref, dba_ref, dwx_ref, dbx_ref, dlam_ref, a_s, b_s, l_s):
        xp = xp_ref[...]
        cw = cw_ref[...]
        lam = lam_ref[...]
        xc, x1, x2, x3 = _conv(xp, cw, cb_ref[...])
        wa, wx = wa_ref[0], wx_ref[0]
        xcb, r, ii, sp, a, mult = _lru_gates(xc, wa, ba_ref[...], wx, bx_ref[...], lam)
        g = g_ref[...]
        gel, dgel = _gelu_parts(g)
        h = h_ref[...]
        dy = dy_ref[...]
        dg_ref[...] = (dy * h * dgel).astype(BF)
        _tile_scan(_su(a, 1, 0.0), dy * gel, a_s, b_s, l_s, reverse=True)
        b = l_s[...]
        da = b * _sd(h, 1, 0.0)
        dmult = b * (ii * xc)
        dii = b * (mult * xc)
        dxc = b * (mult * ii)
        dla = da * a - dmult * ((a * a) / mult)
        dr = dla * ((-LRU_C) * sp)
        dsp = _colsum(dla * ((-LRU_C) * r))
        dlam_ref[...] = -dsp / (1.0 + jnp.exp(lam))
        dzr = dr * (r * (1.0 - r))
        dzi = dii * (ii * (1.0 - ii))
        dzrb, dzib = dzr.astype(BF), dzi.astype(BF)
        dxc = dxc + _mm_nt(dzrb, wa) + _mm_nt(dzib, wx)
        dwa_ref[0] = _mm_tn(xcb, dzrb)
        dwx_ref[0] = _mm_tn(xcb, dzib)
        dba_ref[...] = _colsum(dzr)
        dbx_ref[...] = _colsum(dzi)
        dcb_ref[...] = _colsum(dxc)
        dcw_ref[...] = jnp.concatenate([_colsum(dxc * x3), _colsum(dxc * x2), _colsum(dxc * x1),
                                        _colsum(dxc * xp)], axis=0)
        dxp = cw[3:4] * dxc + cw[2:3] * _su(dxc, 1) + cw[1:2] * _su(dxc, 2) + cw[0:1] * _su(dxc, 3)
        dxp_ref[...] = dxp.astype(BF)

    blk = pl.BlockSpec((T, CB), lambda j: (0, j))
    wsp = pl.BlockSpec((1, CB, CB), lambda j: (j, 0, 0))
    return _call(
        body, name="bwd_lru", grid=(NG,),
        in_specs=[blk, pl.BlockSpec((T, CB), lambda j: (0, NG + j)), blk, blk,
                  pl.BlockSpec((4, CB), lambda j: (0, j)), _vec_spec(), wsp, _vec_spec(), wsp, _vec_spec(),
                  _vec_spec()],
        out_specs=[blk, blk, pl.BlockSpec((4, CB), lambda j: (0, j)), _vec_spec(), wsp, _vec_spec(), wsp,
                   _vec_spec(), _vec_spec()],
        out_shape=[jax.ShapeDtypeStruct((T, DR), BF), jax.ShapeDtypeStruct((T, DR), BF),
                   jax.ShapeDtypeStruct((4, DR), F32), jax.ShapeDtypeStruct((1, DR), F32),
                   jax.ShapeDtypeStruct((NG, CB, CB), F32), jax.ShapeDtypeStruct((1, DR), F32),
                   jax.ShapeDtypeStruct((NG, CB, CB), F32), jax.ShapeDtypeStruct((1, DR), F32),
                   jax.ShapeDtypeStruct((1, DR), F32)],
        vmem=56, args=[proj, proj, h, dylru, conv_w, conv_b, wa, ba, wx, bx, lam], stages=stages,
        scratch=[pltpu.VMEM((T, CB), F32)] * 3)


def _bwd_pool(proj, dypool, pool_w, pool_scale):
    def body(xp_ref, dy_ref, pw_ref, sc_ref, dx_ref, dw_ref, dsc_ref):
        for g, w in enumerate(POOL_WINDOWS):
            cols = slice(g * PG, (g + 1) * PG)
            cnt = _pool_cnt(w)
            x = xp_ref[:, cols]
            pb = (_pool_window(x, g + 1, _sd) / cnt - x).astype(BF)
            wg = pw_ref[g]
            dy = dy_ref[:, cols]
            dsc_ref[:, cols] = _colsum(dy * _mm(pb, wg))
            dyp = (dy * sc_ref[:, cols]).astype(BF)
            dw_ref[g] = _mm_tn(pb, dyp)
            dp = _mm_nt(dyp, wg)
            dx_ref[:, cols] = (_pool_window(dp / cnt, g + 1, _su) - dp).astype(BF)

    return pl.pallas_call(
        body, name="bwd_pool", grid=(1,),
        in_specs=[pl.BlockSpec((T, DP), lambda i: (0, 2 * DR // DP)),
                  pl.BlockSpec((T, DP), lambda i: (0, 0)),
                  pl.BlockSpec((4, PG, PG), lambda i: (0, 0, 0)),
                  pl.BlockSpec((1, DP), lambda i: (0, 0))],
        out_specs=[pl.BlockSpec((T, DP), lambda i: (0, 0)),
                   pl.BlockSpec((4, PG, PG), lambda i: (0, 0, 0)),
                   pl.BlockSpec((1, DP), lambda i: (0, 0))],
        out_shape=_hbm_out([jax.ShapeDtypeStruct((T, DP), BF), jax.ShapeDtypeStruct((4, PG, PG), F32),
                            jax.ShapeDtypeStruct((1, DP), F32)]),
        compiler_params=_cp(48),
    )(*_hbm(proj, dypool, pool_w, pool_scale))


PART_COLS = (DR, DR, DP, 2 * D)


def _shard_pieces():
    starts = [sum(PART_COLS[:p]) for p in range(len(PART_COLS))]
    shards = []
    for k in range(NCHIP):
        lo, hi = k * CW_IN, (k + 1) * CW_IN
        shards.append([(p, max(lo, s) - s, min(hi, s + wd) - s, max(lo, s) - lo)
                       for p, (s, wd) in enumerate(zip(starts, PART_COLS)) if max(lo, s) < min(hi, s + wd)])
    return shards


def _bwd_inproj_w(h1, parts, after):
    flat = [(k, *piece) for k, pieces in enumerate(_shard_pieces()) for piece in pieces]

    def body(h_hbm, p0, p1, p2, p3, after_ref, dw_hbm, h_v, dw_v, *rest):
        bufs, sem_in, sem_out = rest[:len(flat)], rest[len(flat)], rest[len(flat) + 1]
        part_refs = (p0, p1, p2, p3)
        loads = [pltpu.make_async_copy(h_hbm, h_v, sem_in.at[0])]
        for i, (k, p, a, b, c0) in enumerate(flat):
            loads.append(pltpu.make_async_copy(part_refs[p].at[:, pl.ds(a, b - a)], bufs[i], sem_in.at[1 + i]))
        for cp in loads:
            cp.start()
        loads[0].wait()
        stores = []
        for i, (k, p, a, b, c0) in enumerate(flat):
            loads[1 + i].wait()
            dw_v[k, :, c0:c0 + b - a] = _mm(h_v[...], bufs[i][...]).astype(BF)
            if i + 1 == len(flat) or flat[i + 1][0] != k:
                stores.append(pltpu.make_async_copy(dw_v.at[k], dw_hbm.at[k], sem_out.at[k]))
                stores[-1].start()
        for cp in stores:
            cp.wait()

    scratch = [pltpu.VMEM((D, T), BF), pltpu.VMEM((NCHIP, D, CW_IN), BF)]
    scratch += [pltpu.VMEM((T, b - a), parts[p].dtype) for k, p, a, b, c0 in flat]
    scratch += [pltpu.SemaphoreType.DMA((1 + len(flat),)), pltpu.SemaphoreType.DMA((NCHIP,))]
    return pl.pallas_call(
        body, name="bwd_inproj_w", in_specs=[ANY] * 6, out_specs=ANY, scratch_shapes=scratch,
        out_shape=pltpu.HBM((NCHIP, D, CW_IN), BF), compiler_params=_cp(48),
    )(*_hbm(h1, *parts), after)


def _bwd_inproj_x(parts, w_in, x, dxres, g1, stages=()):
    tm = 512

    def body(p0, p1, p2, p3, w_ref, x_ref, dr_ref, g_ref, dx_ref, dg_ref):
        part_refs = (p0, p1, p2, p3)
        dh = None
        for k, pieces in enumerate(_shard_pieces()):
            for p, a, b, c0 in pieces:
                part = _mm_nt(part_refs[p][:, a:b], w_ref[k, :, c0:c0 + b - a])
                dh = part if dh is None else dh + part
        xv = x_ref[...]
        r = lax.rsqrt(_mean(xv * xv) + NORM_EPS)
        xn = xv * r
        t = dh * g_ref[...]
        dx_ref[...] = dr_ref[...] + r * (t - xn * _mean(t * xn))
        _acc(dg_ref, _colsum(dh * xn), pl.program_id(0) == 0)

    row = pl.BlockSpec((tm, D), lambda i: (i, 0))
    vec = pl.BlockSpec((1, D), lambda i: (0, 0))
    return _call(
        body, name="bwd_inproj_x", grid=(T // tm,),
        in_specs=[pl.BlockSpec((tm, wd), lambda i: (i, 0)) for wd in PART_COLS] +
                 [pl.BlockSpec((NCHIP, D, CW_IN), lambda i: (0, 0, 0)), row, row, vec],
        out_specs=[row, vec],
        out_shape=[jax.ShapeDtypeStruct((T, D), F32), jax.ShapeDtypeStruct((1, D), F32)],
        vmem=56, args=[*parts, w_in, x, dxres, g1], stages=stages)[0]


def _place():
    x, y, c = lax.axis_index("x"), lax.axis_index("y"), lax.axis_index("c")
    chips = [(1 - x, y), (x, 1 - y), (1 - x, 1 - y)]
    return x, y, c, chips


def _rcopy(src, dst, ssem, rsem, dev):
    return pltpu.make_async_remote_copy(src_ref=src, dst_ref=dst, send_sem=ssem, recv_sem=rsem,
                                        device_id=dev, device_id_type=MESH_ID)


def _sds(a):
    return jax.ShapeDtypeStruct(a.shape, a.dtype)


def _sem2(n, m):
    return [pltpu.SemaphoreType.DMA((n * m,)), pltpu.SemaphoreType.DMA((n * m,))]


ALL = (0, 1, 1)


def _piece(ref, k, half, part):
    hr = ref.shape[1] // 2
    r0, r1 = hr * part[0] // part[2], hr * part[1] // part[2]
    return ref.at[k, pl.ds(half * hr + r0, r1 - r0), :]


def _gather(fulls, ici=(), d2d=()):
    n = len(fulls)
    ici, d2d = list(ici), list(d2d)
    pieces = [("ici", i, part) for i, part in ici] + [("d2d", i, part) for i, part in d2d]

    def copies(outs, sems):
        x, y, c, chips = _place()
        me = 2 * x + y
        sib = (x, y, 1 - c)
        send, recv = [], []
        for q, (kind, i, part) in enumerate(pieces):
            for j, chip in enumerate(chips):
                k, s = 2 * chip[0] + chip[1], 3 * q + j
                if kind == "ici":
                    mine, theirs, dev = _piece(outs[i], me, c, part), _piece(outs[i], k, c, part), (*chip, c)
                else:
                    mine, theirs, dev = _piece(outs[i], k, c, part), _piece(outs[i], k, 1 - c, part), sib
                send.append(_rcopy(mine, mine, sems[0].at[s], sems[1].at[s], dev))
                recv.append(_rcopy(theirs, theirs, sems[0].at[s], sems[1].at[s], dev))
        return send, recv

    def start(ins, outs, sems):
        for cp in copies(outs, sems)[0]:
            cp.start()

    def finish(ins, outs, sems):
        send, recv = copies(outs, sems)
        for cp in recv:
            cp.wait_recv()
        for cp in send:
            cp.wait_send()

    sems = [pltpu.SemaphoreType.DMA((3 * len(pieces),)), pltpu.SemaphoreType.DMA((3 * len(pieces),))]
    return _Stage(fulls, [_sds(f) for f in fulls], {i: i for i in range(n)}, sems, start, finish)


def _gather_whole(v):
    def copies(ins, outs, sems):
        x, y, c, chips = _place()
        me = 2 * x + y
        send = [_rcopy(ins[0], outs[0].at[me], sems[0].at[j], sems[1].at[j], (*chip, c))
                for j, chip in enumerate(chips)]
        recv = [_rcopy(ins[0], outs[0].at[2 * chip[0] + chip[1]], sems[0].at[j], sems[1].at[j], (*chip, c))
                for j, chip in enumerate(chips)]
        return send, recv

    def start(ins, outs, sems):
        for cp in copies(ins, outs, sems)[0]:
            cp.start()

    def finish(ins, outs, sems):
        send, recv = copies(ins, outs, sems)
        for cp in recv:
            cp.wait_recv()
        for cp in send:
            cp.wait_send()

    return _Stage([v], [jax.ShapeDtypeStruct((NCHIP,) + v.shape, v.dtype)], {},
                  [pltpu.SemaphoreType.DMA((3,)), pltpu.SemaphoreType.DMA((3,))], start, finish)


def _to_sibling(srcs):
    n = len(srcs)

    def copies(ins, outs, sems):
        x, y, c, _ = _place()
        sib = (x, y, 1 - c)
        return [_rcopy(ins[i].at[:, 1 - c] if srcs[i].ndim == 4 else ins[i], outs[i], sems[0].at[i], sems[1].at[i], sib)
                for i in range(n)]

    def start(ins, outs, sems):
        for cp in copies(ins, outs, sems):
            cp.start()

    def finish(ins, outs, sems):
        for cp in copies(ins, outs, sems):
            cp.wait()

    shapes = [jax.ShapeDtypeStruct((NCHIP,) + s.shape[2:] if s.ndim == 4 else s.shape, s.dtype) for s in srcs]
    return _Stage(srcs, shapes, {}, [pltpu.SemaphoreType.DMA((n,)), pltpu.SemaphoreType.DMA((n,))], start, finish)


def _to_chips(srcs, parts=None, lands=None):
    n = len(srcs)
    parts = [ALL] * n if parts is None else parts
    lands = [None] * n if lands is None else lands
    given = [i for i in range(n) if lands[i] is not None]

    def rows(ref, i):
        hr = srcs[i].shape[1]
        r0, r1 = hr * parts[i][0] // parts[i][2], hr * parts[i][1] // parts[i][2]
        return ref.at[pl.ds(r0, r1 - r0), :]

    def copies(ins, outs, sems):
        x, y, c, chips = _place()
        me = 2 * x + y
        return [_rcopy(rows(ins[i].at[2 * chip[0] + chip[1]] if srcs[i].shape[0] == NCHIP else ins[i].at[c], i),
                       rows(outs[i].at[me], i), sems[0].at[3 * i + j], sems[1].at[3 * i + j], (*chip, c))
                for i in range(n) for j, chip in enumerate(chips)]

    def start(ins, outs, sems):
        for cp in copies(ins, outs, sems):
            cp.start()

    def finish(ins, outs, sems):
        for cp in copies(ins, outs, sems):
            cp.wait()

    shapes = [jax.ShapeDtypeStruct((NCHIP,) + s.shape[1:], s.dtype) for s in srcs]
    alias = {n + q: i for q, i in enumerate(given)}
    return _Stage(list(srcs) + [lands[i] for i in given], shapes, alias, _sem2(n, 3), start, finish)


HBM_REF = pl.BlockSpec(memory_space=pltpu.HBM)
SEM_REF = pl.BlockSpec(memory_space=pltpu.SEMAPHORE)
DATAFLOW = pltpu.SideEffectType.DATAFLOW_SIDE_EFFECTING


def _after(x):
    return _Stage([x], [], {}, [], lambda *a: None, lambda *a: None)


class _Flight:
    def __init__(self, stage, sems, bufs):
        self.stage, self.sems, self.bufs = stage, list(sems), list(bufs)

    def landed(self):
        st, n = self.stage, len(self.stage.operands)
        fresh = [j for j in range(len(st.out_shape)) if j not in st.alias.values()]
        back = {v: k for k, v in st.alias.items()}
        return [self.bufs[back[j]] if j in back else self.bufs[n + fresh.index(j)] for j in range(len(st.out_shape))]


def _split_call(name, finish=(), start=(), after=None):
    bufs, stage_bufs = [], []

    def slot(a):
        for i, b in enumerate(bufs):
            if b is a:
                return i
        bufs.append(a)
        return len(bufs) - 1

    fin_slots = [[slot(b) for b in fl.bufs] for fl in finish]
    for st in start:
        fresh = [lax.empty(o.shape, o.dtype) for j, o in enumerate(st.out_shape) if j not in st.alias.values()]
        stage_bufs.append([slot(a) for a in list(st.operands) + fresh])
    old_sems = [s for fl in finish for s in fl.sems]
    new_sems = [s for st in start for s in st.sems]
    nb, no, nn = len(bufs), len(old_sems), len(new_sems)

    def refs_of(st, slots, buf_refs):
        n = len(st.operands)
        ins = [buf_refs[i] for i in slots[:n]]
        fresh = [j for j in range(len(st.out_shape)) if j not in st.alias.values()]
        back = {v: k for k, v in st.alias.items()}
        outs = [ins[back[j]] if j in back else buf_refs[slots[n + fresh.index(j)]] for j in range(len(st.out_shape))]
        return ins, outs

    def body(*refs):
        buf_refs, sem_in = refs[:nb], refs[nb:nb + no]
        sem_out = refs[nb + no + (after is not None):][:nn]
        token = refs[-1]
        pos = 0
        for fl, slots in zip(finish, fin_slots):
            ins, outs = refs_of(fl.stage, slots, buf_refs)
            fl.stage.finish(ins, outs, sem_in[pos:pos + len(fl.sems)])
            pos += len(fl.sems)
        pos = 0
        for st, slots in zip(start, stage_bufs):
            ins, outs = refs_of(st, slots, buf_refs)
            st.start(ins, outs, sem_out[pos:pos + len(st.sems)])
            pos += len(st.sems)
        token[...] = jnp.zeros_like(token)

    res = pl.pallas_call(
        body, name=name,
        out_shape=tuple(new_sems) + tuple(pltpu.HBM(b.shape, b.dtype) for b in bufs) +
                  (jax.ShapeDtypeStruct((8, LANE), F32),),
        in_specs=(HBM_REF,) * nb + (SEM_REF,) * no + ((pl.BlockSpec(memory_space=pl.ANY),) if after is not None else ()),
        out_specs=(SEM_REF,) * nn + (HBM_REF,) * nb + (pl.BlockSpec(memory_space=pltpu.VMEM),),
        input_output_aliases={i: nn + i for i in range(nb)},
        compiler_params=pltpu.CompilerParams(has_side_effects=DATAFLOW),
    )(*_hbm(*bufs), *old_sems, *([after] if after is not None else []))
    sems, thru, token = res[:nn], res[nn:nn + nb], res[-1]
    for fl, slots in zip(finish, fin_slots):
        fl.bufs = [thru[i] for i in slots]
    flights, pos = [], 0
    for st, slots in zip(start, stage_bufs):
        flights.append(_Flight(st, sems[pos:pos + len(st.sems)], [thru[i] for i in slots]))
        pos += len(st.sems)
    return flights, token


def _share(pairs):
    n = len(pairs)

    def start(ins, outs, sems):
        x, y, c, _ = _place()
        for i in range(n):
            _rcopy(outs[i].at[c], outs[i].at[c], sems[0].at[i], sems[1].at[i], (x, y, 1 - c)).start()

    def finish(ins, outs, sems):
        x, y, c, _ = _place()
        for i in range(n):
            _rcopy(outs[i].at[c], outs[i].at[c], sems[0].at[i], sems[1].at[i], (x, y, 1 - c)).wait_send()
            _rcopy(outs[i].at[1 - c], outs[i].at[1 - c], sems[0].at[i], sems[1].at[i], (x, y, 1 - c)).wait_recv()

    return _Stage(pairs, [_sds(p) for p in pairs], {i: i for i in range(n)},
                  [pltpu.SemaphoreType.DMA((n,)), pltpu.SemaphoreType.DMA((n,))], start, finish)


def _row_block(rows, cols, itemsize=4, target=2 * MIB):
    br = rows
    while br * cols * itemsize > target and br % 32 == 0:
        br //= 2
    return br


def _cast_place(w, chip_idx, name):
    rows, cols = w.shape
    br = _row_block(rows, cols)

    def body(k_ref, w_ref, o_ref):
        o_ref[0] = w_ref[...].astype(BF)

    return _call(
        body, name=name, grid=(rows // br,), prefetch=chip_idx,
        in_specs=[pl.BlockSpec((br, cols), lambda r, k: (r, 0))],
        out_specs=[pl.BlockSpec((1, br, cols), lambda r, k: (k[0], r, 0))],
        out_shape=[jax.ShapeDtypeStruct((NCHIP, rows, cols), BF)], vmem=32, args=[w])[0][0]


def _cast_place_multi(ws, chip_idx, stages=()):
    br = 128
    nblk = [a.shape[0] // br for a in ws]
    starts = [sum(nblk[:i]) for i in range(len(ws))]

    def body(k_ref, *refs):
        r = pl.program_id(0)
        for i in range(len(ws)):
            @pl.when(jnp.logical_and(r >= starts[i], r < starts[i] + nblk[i]))
            def _(i=i):
                refs[len(ws) + i][0] = refs[i][...].astype(BF)

    def at(i):
        return functools.partial(lambda r, s, nb: jnp.clip(r - s, 0, nb - 1), s=starts[i], nb=nblk[i])

    outs, landed = _call(
        body, name="cast_rest", grid=(sum(nblk),), prefetch=chip_idx,
        in_specs=[pl.BlockSpec((br, a.shape[1]), functools.partial(lambda r, k, f: (f(r), 0), f=at(i)))
                  for i, a in enumerate(ws)],
        out_specs=[pl.BlockSpec((1, br, a.shape[1]), functools.partial(lambda r, k, f: (k[0], f(r), 0), f=at(i)))
                   for i, a in enumerate(ws)],
        out_shape=[jax.ShapeDtypeStruct((NCHIP,) + a.shape, BF) for a in ws], vmem=32, args=list(ws), stages=stages)
    return outs, landed


def _add_sibling(g, land, cidx, name, stages=()):
    _, _, hr, cols = g.shape
    br = _row_block(hr, cols)

    def body(c_ref, g_ref, l_ref, o_ref):
        o_ref[...] = (g_ref[0, 0].astype(F32) + l_ref[0].astype(F32)).astype(BF)[None]

    outs, st = _call(
        body, name=name, grid=(NCHIP, hr // br), prefetch=cidx,
        in_specs=[pl.BlockSpec((1, 1, br, cols), lambda k, r, c: (k, c[0], r, 0)),
                  pl.BlockSpec((1, br, cols), lambda k, r, c: (k, r, 0))],
        out_specs=[pl.BlockSpec((1, br, cols), lambda k, r, c: (k, r, 0))],
        out_shape=[jax.ShapeDtypeStruct((NCHIP, hr, cols), BF)], vmem=32, args=[g, land], stages=stages)
    return outs[0], st


def _add_sibling_multi(gs, lands, cidx, name):
    n = len(gs)
    brs = [_row_block(g.shape[2], g.shape[3]) for g in gs]
    nrb = [g.shape[2] // b for g, b in zip(gs, brs)]
    nblk = [NCHIP * q for q in nrb]
    starts = [sum(nblk[:i]) for i in range(n)]

    def body(c_ref, *refs):
        r = pl.program_id(0)
        for i in range(n):
            g_ref, l_ref, o_ref = refs[2 * i], refs[2 * i + 1], refs[2 * n + i]

            @pl.when(jnp.logical_and(r >= starts[i], r < starts[i] + nblk[i]))
            def _():
                o_ref[...] = (g_ref[0, 0].astype(F32) + l_ref[0].astype(F32)).astype(BF)[None]

    def at(i, r):
        q = jnp.clip(r - starts[i], 0, nblk[i] - 1)
        return q // nrb[i], q % nrb[i]

    def g_spec(i):
        return pl.BlockSpec((1, 1, brs[i], gs[i].shape[3]),
                            functools.partial(lambda r, c, i: (at(i, r)[0], c[0], at(i, r)[1], 0), i=i))

    def l_spec(i):
        return pl.BlockSpec((1, brs[i], gs[i].shape[3]),
                            functools.partial(lambda r, c, i: (at(i, r)[0], at(i, r)[1], 0), i=i))

    return _call(
        body, name=name, grid=(sum(nblk),), prefetch=cidx,
        in_specs=[s for i in range(n) for s in (g_spec(i), l_spec(i))], out_specs=[l_spec(i) for i in range(n)],
        out_shape=[jax.ShapeDtypeStruct(l.shape, BF) for l in lands], vmem=32,
        args=[a for i in range(n) for a in (gs[i], lands[i])])[0]


def _add_pair(a, b, name):
    rows, cols = a.shape

    def body(a_ref, b_ref, o_ref):
        o_ref[...] = a_ref[...] + b_ref[...]

    spec = pl.BlockSpec((rows, cols), lambda r: (0, 0))
    return _call(body, name=name, grid=(1,), in_specs=[spec, spec], out_specs=[spec], out_shape=[_sds(a)],
                 vmem=32, args=[a, b])[0][0]


def _add_chips(own, land, idx, name, stages=None):
    _, hr, cols = land.shape
    br = _row_block(hr, cols)

    def body(s_ref, a_ref, b_ref, c_ref, d_ref, o_ref):
        o_ref[...] = (a_ref[...].astype(F32) + b_ref[...].astype(F32)) + (c_ref[...].astype(F32) +
                                                                           d_ref[...].astype(F32))

    spec = lambda q: pl.BlockSpec((1, br, cols), functools.partial(lambda r, s, q: (s[q], r, 0), q=q))
    outs, landed = _call(
        body, name=name, grid=(hr // br,), prefetch=idx,
        in_specs=[spec(0), spec(1), spec(2), spec(3)], out_specs=[spec(4)],
        out_shape=[jax.ShapeDtypeStruct((2, hr, cols), F32)], vmem=48, args=[own, land, land, land],
        stages=stages or ())
    return outs[0] if stages is None else (outs[0], landed)


def _add_chips_multi(owns, lands, idx, name, stages=()):
    n = len(owns)
    brs = [_row_block(l.shape[1], l.shape[2]) for l in lands]
    nblk = [l.shape[1] // b for l, b in zip(lands, brs)]
    starts = [sum(nblk[:i]) for i in range(n)]

    def body(s_ref, *refs):
        r = pl.program_id(0)
        for i in range(n):
            a_ref, b_ref, c_ref, d_ref = refs[4 * i:4 * i + 4]
            o_ref = refs[4 * n + i]

            @pl.when(jnp.logical_and(r >= starts[i], r < starts[i] + nblk[i]))
            def _():
                o_ref[...] = (a_ref[...].astype(F32) + b_ref[...].astype(F32)) + (c_ref[...].astype(F32) +
                                                                                   d_ref[...].astype(F32))

    def spec(i, q):
        return pl.BlockSpec((1, brs[i], lands[i].shape[2]), functools.partial(
            lambda r, s, q, st, nb: (s[q], jnp.clip(r - st, 0, nb - 1), 0), q=q, st=starts[i], nb=nblk[i]))

    outs, landed = _call(
        body, name=name, grid=(sum(nblk),), prefetch=idx,
        in_specs=[spec(i, q) for i in range(n) for q in range(4)], out_specs=[spec(i, 4) for i in range(n)],
        out_shape=[jax.ShapeDtypeStruct((2,) + l.shape[1:], F32) for l in lands], vmem=48,
        args=[a for i in range(n) for a in (owns[i], lands[i], lands[i], lands[i])], stages=stages)
    return outs, landed


def _adamw_math(w, g, m, v):
    mn = ADAM_B1 * m + (1.0 - ADAM_B1) * g
    vn = ADAM_B2 * v + (1.0 - ADAM_B2) * (g * g)
    m_hat = mn / (1.0 - ADAM_B1 ** ADAM_STEP)
    v_hat = vn / (1.0 - ADAM_B2 ** ADAM_STEP)
    return -ADAM_LR * (m_hat / (jnp.sqrt(v_hat) + ADAM_EPS) + ADAM_WD * w), mn, vn


def _adamw(w, g, m, v, name, stages=()):
    rows, cols = w.shape
    br = _row_block(rows, cols)

    def body(w_ref, g_ref, m_ref, v_ref, go_ref, d_ref, mo_ref, vo_ref):
        gv = g_ref[...]
        go_ref[...] = gv
        d_ref[...], mo_ref[...], vo_ref[...] = _adamw_math(w_ref[...], gv, m_ref[...], v_ref[...])

    spec = pl.BlockSpec((br, cols), lambda r: (r, 0))
    return _call(body, name=name, grid=(rows // br,), in_specs=[spec] * 4, out_specs=[spec] * 4,
                 out_shape=[_sds(w)] * 4, vmem=56, args=[w, g, m, v], stages=stages)


def _adamw_multi(names, w, g, m, v, stages=()):
    cols = w[names[0]].shape[1]
    br = 128
    nblk = [w[n].shape[0] // br for n in names]
    starts = [sum(nblk[:i]) for i in range(len(names))]

    def body(*refs):
        r = pl.program_id(0)
        for i in range(len(names)):
            w_ref, g_ref, m_ref, v_ref = refs[4 * i:4 * i + 4]
            go_ref, d_ref, mo_ref, vo_ref = refs[4 * len(names) + 4 * i:4 * len(names) + 4 * i + 4]

            @pl.when(jnp.logical_and(r >= starts[i], r < starts[i] + nblk[i]))
            def _():
                gv = g_ref[...]
                go_ref[...] = gv
                d_ref[...], mo_ref[...], vo_ref[...] = _adamw_math(w_ref[...], gv, m_ref[...], v_ref[...])

    def spec(i):
        return pl.BlockSpec((br, cols), functools.partial(
            lambda r, s, nb: (jnp.clip(r - s, 0, nb - 1), 0), s=starts[i], nb=nblk[i]))

    outs, landed = _call(
        body, name="adamw_" + "_".join(names), grid=(sum(nblk),),
        in_specs=[spec(i) for i in range(len(names)) for _ in range(4)],
        out_specs=[spec(i) for i in range(len(names)) for _ in range(4)],
        out_shape=[_sds(w[n]) for n in names for _ in range(4)], vmem=56,
        args=[a[n] for n in names for a in (w, g, m, v)], stages=stages)
    return {n: outs[4 * i:4 * i + 4] for i, n in enumerate(names)}, landed


def _to_everyone(v):
    deltas = [(a, b, e) for a in (0, 1) for b in (0, 1) for e in (0, 1)][1:]

    def copies(ins, outs, sems):
        x, y, c, _ = _place()
        me = 4 * x + 2 * y + c
        flip = lambda p, f: 1 - p if f else p
        return [_rcopy(ins[0], outs[0].at[me], sems[0].at[q], sems[1].at[q], (flip(x, a), flip(y, b), flip(c, e)))
                for q, (a, b, e) in enumerate(deltas)]

    def start(ins, outs, sems):
        for cp in copies(ins, outs, sems):
            cp.start()

    def finish(ins, outs, sems):
        for cp in copies(ins, outs, sems):
            cp.wait()

    n = len(deltas)
    return _Stage([v], [jax.ShapeDtypeStruct((2 * NCHIP,) + v.shape, v.dtype)], {},
                  [pltpu.SemaphoreType.DMA((n,)), pltpu.SemaphoreType.DMA((n,))], start, finish)


SMALL_AT = {"norm_mix_pre": (0, 1, D), "norm_mix_post": (1, 1, D), "norm_mlp_pre": (2, 1, D),
            "norm_mlp_post": (3, 1, D), "b_gate": (4, 2, D), "conv_b": (6, 1, D), "lru_b_a": (7, 1, D),
            "lru_b_x": (8, 1, D), "lru_lambda": (9, 1, D), "pool_scale": (10, 1, DP)}
SMALL_SEPARATE = ["conv_w", "lru_w_a", "lru_w_x", "pool_w"]


def _adamw_small(small_sum, first_all, sep_grads, w, m, v):
    packed, sep = list(SMALL_AT), list(SMALL_SEPARATE)
    names = packed + sep

    def body(*refs):
        s_ref, a_ref, refs = refs[0], refs[1], refs[2:]
        g_sep, refs = refs[:len(sep)], refs[len(sep):]
        nn = len(names)
        w_r, m_r, v_r, refs = refs[:nn], refs[nn:2 * nn], refs[2 * nn:3 * nn], refs[3 * nn:]
        g_out, refs = refs[:len(packed)], refs[len(packed):]
        d_o, m_o, v_o = refs[:nn], refs[nn:2 * nn], refs[2 * nn:3 * nn]
        for i, n in enumerate(names):
            if i == 0:
                g = a_ref[0:1, :]
                for q in range(1, 2 * NCHIP):
                    g = g + a_ref[q:q + 1, :]
                g_out[i][...] = g
            elif n in SMALL_AT:
                r0, nr, nc = SMALL_AT[n]
                g = jnp.concatenate([s_ref[r0 + q:r0 + q + 1, :nc] for q in range(nr)], axis=1)
                g_out[i][...] = g
            else:
                g = g_sep[i - len(packed)][...]
            d_o[i][...], m_o[i][...], v_o[i][...] = _adamw_math(w_r[i][...], g, m_r[i][...], v_r[i][...])

    ws = [w[n] for n in names]
    res = pl.pallas_call(
        body, name="adamw_small",
        out_shape=[_sds(w[n]) for n in packed] + [_sds(a) for a in ws] * 3,
        compiler_params=_cp(32),
    )(*_hbm(small_sum, first_all, *sep_grads, *ws, *[m[n] for n in names], *[v[n] for n in names]))
    nn, npk = len(names), len(packed)
    grad = dict(zip(packed, res[:npk]))
    delta = dict(zip(names, res[npk:npk + nn]))
    new_m = dict(zip(names, res[npk + nn:npk + 2 * nn]))
    new_v = dict(zip(names, res[npk + 2 * nn:]))
    return grad, delta, new_m, new_v


W_NAMES = ["norm_mix_pre", "norm_mix_post", "norm_mlp_pre", "norm_mlp_post", "w_in", "b_gate", "conv_w", "conv_b",
           "lru_w_a", "lru_b_a", "lru_w_x", "lru_b_x", "lru_lambda", "pool_w", "pool_scale", "w_lru_up",
           "w_pool_up", "w_o", "w_ff1", "w_ff2"]
BIG = ["w_in", "w_lru_up", "w_pool_up", "w_o", "w_ff1", "w_ff2"]


def _block_diag(w):
    hd = w.shape[-1]
    per = CB // hd
    w4 = w.reshape(NG, per, hd, hd)
    eye = jnp.eye(per, dtype=w.dtype)
    return jnp.einsum("gpij,pq->gpiqj", w4, eye).reshape(NG, CB, CB)


def _block_diag_extract(d, hd):
    per = CB // hd
    d5 = d.reshape(NG, per, hd, per, hd)
    return jnp.stack([d5[:, p, :, p, :] for p in range(per)], axis=1).reshape(NG * per, hd, hd)


def _halves(g):
    return g.reshape(NCHIP, 2, g.size // (g.shape[-1] * 2 * NCHIP), g.shape[-1])


def kernel(x, norm_mix_pre, norm_mix_post, norm_mlp_pre, norm_mlp_post, w_in, b_gate, conv_w, conv_b, lru_w_a, lru_b_a, lru_w_x, lru_b_x, lru_lambda, pool_w, pool_scale, w_lru_up, w_pool_up, w_o, w_ff1, w_ff2, loss_target, m_norm_mix_pre, m_norm_mix_post, m_norm_mlp_pre, m_norm_mlp_post, m_w_in, m_b_gate, m_conv_w, m_conv_b, m_lru_w_a, m_lru_b_a, m_lru_w_x, m_lru_b_x, m_lru_lambda, m_pool_w, m_pool_scale, m_w_lru_up, m_w_pool_up, m_w_o, m_w_ff1, m_w_ff2, v_norm_mix_pre, v_norm_mix_post, v_norm_mlp_pre, v_norm_mlp_post, v_w_in, v_b_gate, v_conv_w, v_conv_b, v_lru_w_a, v_lru_b_a, v_lru_w_x, v_lru_b_x, v_lru_lambda, v_pool_w, v_pool_scale, v_w_lru_up, v_w_pool_up, v_w_o, v_w_ff1, v_w_ff2):
    args = dict(locals())
    two_d = lambda a: a.reshape(-1, a.shape[-1])
    w = {n: two_d(args[n]) for n in W_NAMES}
    mom = {n: two_d(args["m_" + n]) for n in W_NAMES}
    var = {n: two_d(args["v_" + n]) for n in W_NAMES}
    i32 = lambda val: jnp.asarray(val, jnp.int32)
    chip = i32(2 * lax.axis_index("x") + lax.axis_index("y"))
    core = i32(lax.axis_index("c"))
    cidx = core.reshape(1)
    zero = i32(0)
    hd = lru_w_a.shape[-1]
    xs, target = x[0], loss_target[0]
    g1, g2, g3, g4 = norm_mix_pre, norm_mix_post, norm_mlp_pre, norm_mlp_post

    mix = ["w_lru_up", "w_pool_up", "w_o"]
    full = {"w_in": _cast_place(w["w_in"], chip.reshape(1), "cast_w_in")}
    (fl_in, fl_conv), first = _split_call("gather_start_first", start=[
        _gather([full["w_in"]], ici=[(0, ALL)]), _gather_whole(w["conv_w"])])
    casts, _ = _cast_place_multi([w[n] for n in BIG[1:]], chip.reshape(1), stages=[_after(first)])
    full.update(zip(BIG[1:], casts))
    (fl_mix, fl_ff1, fl_ff2), started = _split_call("gather_start_rest", start=[
        _gather([full[n] for n in mix], ici=[(0, ALL), (1, ALL), (2, ALL)]),
        _gather([full["w_ff1"]], ici=[(0, ALL)]), _gather([full["w_ff2"]], ici=[(0, ALL)])])
    wa = _block_diag(lru_w_a[0]).astype(BF)
    wx = _block_diag(lru_w_x[0]).astype(BF)
    pw = pool_w[0].astype(BF)

    def to_sibling(name, flight, after=None):
        (fl,), passed = _split_call(name + "_pass", finish=[flight], after=after,
                                    start=[_gather(flight.landed(), d2d=[(i, ALL) for i in range(len(flight.bufs))])])
        passed_on.append(passed)
        return fl

    passed_on = []

    def arrived(name, flight, after=None):
        _split_call(name + "_done", finish=[flight], after=after)
        return flight.landed()

    idx_big = jnp.stack([chip, (chip + 1) % NCHIP, (chip + 2) % NCHIP, (chip + 3) % NCHIP, core])
    proj, h1, h1_t = _fwd_inproj_own(xs, g1, fl_in.bufs[0], idx_big, stages=[_after(started)])
    fl_in = to_sibling("gather_w_in", fl_in, after=h1)
    _split_call("gather_w_in_done", finish=[fl_in, fl_conv])
    (w_in_f,), (conv_all,) = fl_in.landed(), fl_conv.landed()
    full["w_in"] = w_in_f
    conv_all = lax.dynamic_update_slice(conv_all, w["conv_w"][None], (chip, zero, zero))
    conv_full = jnp.transpose(conv_all, (1, 0, 2)).reshape(4, DR)
    proj = _fwd_inproj_rest(h1, w_in_f, proj, idx_big)
    fl_mix = to_sibling("gather_mix", fl_mix, after=proj)
    (ylru, hs), _ = _fwd_lru(proj, conv_full, conv_b, wa, lru_b_a, wx, lru_b_x, lru_lambda,
                             stages=[_after(passed_on[-1])])
    got = arrived("gather_mix", fl_mix, after=ylru)
    fl_ff1 = to_sibling("gather_ff1", fl_ff1, after=ylru)
    w_lru_up_f, w_pool_up_f, w_o_f = got[0].reshape(DR, D), got[1], got[2].reshape(D, D)
    ypool = _fwd_pool(proj, pw, pool_scale)
    (x2, h2, m, mrg, bra, brb), _ = _fwd_merge(xs, ylru, ypool, proj, b_gate, g2, g3, w_lru_up_f, w_pool_up_f, w_o_f,
                                               stages=[_after(passed_on[-1])])
    fl_ff2 = to_sibling("gather_ff2", fl_ff2, after=h2)
    _split_call("gather_ff_done", finish=[fl_ff1, fl_ff2])
    (ff1,), (ff2,) = fl_ff1.landed(), fl_ff2.landed()
    ff2 = ff2.reshape(DF, D)
    a1, lossp, dy, df, dg4 = _fwd_mlp_loss(h2, ff1, ff2, x2, target, g4)

    dh2, df1 = _bwd_mlp_x(df, a1, ff1, ff2)
    dw_ff1, dw_ff2 = _bwd_mlp_w(df, h2, a1, df1)
    g_ff = [_halves(dw_ff1), _halves(dw_ff2)]
    (dxres, dgates, dylru, dypool, dm, dbra, dbrb, dg2, dg3, dbg), (l_ff,) = _bwd_merge(
        dh2, dy, x2, m, bra, brb, proj, b_gate, g2, g3, w_lru_up_f, w_pool_up_f, w_o_f, stages=[_to_sibling(g_ff)])
    p_ff = _add_sibling_multi(g_ff, l_ff, cidx, "add_sibling_ff")
    (fl_ff,), sent_ff = _split_call("reduce_ff_start", start=[_to_chips(p_ff)])
    (dw_o, dw_lru_up, dw_pool_up), _ = _dw_merge(mrg, dm, ylru, dbra, ypool, dbrb, stages=[_after(sent_ff)])
    g_mix = [_halves(dw_lru_up), _halves(dw_pool_up), _halves(dw_o)]
    (dxp, dgl, dcw, dcb, dwa, dba, dwx, dbx, dlam), (l_mix,) = _bwd_lru(
        proj, hs, dylru, conv_full, conv_b, wa, lru_b_a, wx, lru_b_x, lru_lambda, stages=[_to_sibling(g_mix)])
    p_mix = _add_sibling_multi(g_mix, l_mix, cidx, "add_sibling_mix")
    dxpool, dpw, dsc = _bwd_pool(proj, dypool, pw, pool_scale)
    dproj = [dxp, dgl, dxpool, dgates]
    small = jnp.concatenate([
        jnp.zeros((1, D), F32), dg2, dg3, dg4, dbg.reshape(2, D), dcb, dba, dbx, dlam,
        jnp.pad(dsc, ((0, 0), (0, D - DP))), jnp.pad(lossp, ((0, 0), (0, D - 1))), dcw,
        _block_diag_extract(dwa, hd).reshape(-1, D), _block_diag_extract(dwx, hd).reshape(-1, D),
        dpw.reshape(-1, D)], axis=0)
    (fl_mixr, fl_smalls), sent_mix = _split_call("reduce_mix_start", start=[_to_chips(p_mix), _to_sibling([small])])
    dw_in = _bwd_inproj_w(h1_t, dproj, sent_mix)
    _split_call("reduce_small_sibling_done", finish=[fl_smalls], after=dw_in)
    small, l_small = fl_smalls.bufs
    small2 = _add_pair(small, l_small, "add_sibling_small").reshape(2, SMALL_ROWS // 2, D)
    g_in = _halves(dw_in)
    done = ["w_ff1", "w_ff2"] + mix
    (fl_gin, fl_small), sib_started = _split_call("reduce_in_sibling_start", finish=[fl_ff, fl_mixr],
                                                  start=[_to_sibling([g_in]), _to_chips([small2])])
    p_ff1, p_ff2, c_ff1, c_ff2 = fl_ff.bufs
    p_mix, c_mix = fl_mixr.bufs[:3], fl_mixr.bufs[3:]
    pairs, _ = _add_chips_multi([p_ff1, p_ff2] + p_mix, [c_ff1, c_ff2] + c_mix, idx_big, "add_chips_done",
                                stages=[_after(sib_started)])
    _split_call("reduce_in_sibling_done", finish=[fl_gin], after=pairs[-1])
    g_in, l_in = fl_gin.bufs
    p_in = _add_sibling(g_in, l_in, cidx, "add_sibling_w_in")[0]
    (fl_pin,), token = _split_call("reduce_last_start", start=[_to_chips([p_in])])
    _split_call("reduce_small_done", finish=[fl_small], after=token)
    small2, c_small = fl_small.bufs
    own_small = lax.dynamic_index_in_dim(small2, core, 0, keepdims=True)
    c_small = lax.dynamic_update_slice(c_small, own_small, (chip, zero, zero))
    pair_small = _add_chips(c_small, c_small, jnp.stack([zero, zero + 1, zero + 2, zero + 3, core]), "add_chips_small")
    (fl_share,), shared_start = _split_call("reduce_share_start", start=[_share(pairs + [pair_small])])
    grad_x, dg1 = _bwd_inproj_x(dproj, full["w_in"], xs, dxres, g1, stages=[_after(shared_start)])
    _split_call("reduce_share_done", finish=[fl_share, fl_pin], after=dg1)
    shared, (p_in, c_in) = fl_share.landed(), fl_pin.bufs
    pairs, pair_small = shared[:-1], shared[-1]

    grads, delta, new_m, new_v = {}, {}, {}, {}
    for n, p in zip(done, pairs):
        grads[n] = p.reshape(-1, p.shape[-1])

    def update(n, stages=()):
        (grads[n], delta[n], new_m[n], new_v[n]), landed = _adamw(w[n], grads[n], mom[n], var[n], "adamw_" + n,
                                                                  stages=stages)
        return landed

    pair_in = _add_chips(p_in, c_in, idx_big, "add_chips_w_in")
    (fl_last, fl_dg1), last_start = _split_call("reduce_last_share_start", start=[_share([pair_in]), _to_everyone(dg1)])
    updated, _ = _adamw_multi(["w_ff1", "w_ff2", "w_o", "w_lru_up"], w, grads, mom, var, stages=[_after(last_start)])
    for n, (go, d, mo, vo) in updated.items():
        grads[n], delta[n], new_m[n], new_v[n] = go, d, mo, vo
    _split_call("reduce_last_share_done", finish=[fl_last, fl_dg1], after=new_v["w_lru_up"])
    (pair_in,), (dg1, dg1_all) = fl_last.landed(), fl_dg1.bufs
    dg1_all = lax.dynamic_update_slice(dg1_all, dg1[None], (2 * chip + core, zero, zero)).reshape(2 * NCHIP, D)
    grads["w_in"] = pair_in.reshape(-1, pair_in.shape[-1])
    update("w_pool_up")
    update("w_in")
    small_sum = pair_small.reshape(SMALL_ROWS, D)
    loss = 0.5 * small_sum[LOSS_ROW, 0]
    ccols = DR // NCHIP
    sep = [lax.dynamic_slice(small_sum[12:16], (zero, chip * ccols), (4, ccols)),
           small_sum[16:80].reshape(-1, hd), small_sum[80:144].reshape(-1, hd), small_sum[144:208].reshape(-1, PG)]
    g_s, d_s, m_s, v_s = _adamw_small(small_sum, dg1_all, sep, w, mom, var)
    grads.update(g_s)
    grads.update(dict(zip(SMALL_SEPARATE, sep)))
    delta.update(d_s)
    new_m.update(m_s)
    new_v.update(v_s)

    out = lambda d: [d[n].reshape(args[n].shape) for n in W_NAMES]
    return (loss, grad_x[None], *out(grads), *out(delta), *out(new_m), *out(new_v))
```

```python
import functools
import math

import jax
import jax.numpy as jnp
from jax import lax
from jax.experimental import pallas as pl
from jax.experimental.pallas import tpu as pltpu

F32 = jnp.float32
BF = jnp.bfloat16

T = 2048
D = 1024
DR = 1024
DP = 512
DF = 4096
DIN = 4608
NCHIP = 4
CW_IN = DIN // NCHIP
LANE = 128
CB = 128
NG = DR // CB
PG = 128
POOL_WINDOWS = (2, 4, 8, 16)
NORM_EPS = 1e-6
LRU_C = 8.0
GELU_C = math.sqrt(2.0 / math.pi)
ADAM_LR = 0.001
ADAM_B1 = 0.9
ADAM_B2 = 0.999
ADAM_EPS = 1e-08
ADAM_WD = 0.01
ADAM_STEP = 10
MESH_ID = pl.DeviceIdType.MESH
ANY = pl.BlockSpec(memory_space=pl.ANY)
SMALL_ROWS = 208
LOSS_ROW = 11
MIB = 1 << 20


def _cp(vmem_mib=None):
    if vmem_mib is None:
        return pltpu.CompilerParams()
    return pltpu.CompilerParams(vmem_limit_bytes=vmem_mib * MIB)


def _hbm(*arrays):
    return [pltpu.with_memory_space_constraint(a, pltpu.HBM) for a in arrays]


def _hbm_out(shapes):
    return [pltpu.HBM(s.shape, s.dtype) for s in shapes]


class _Stage:
    def __init__(self, operands, out_shape, alias, sems, start, finish):
        self.operands, self.out_shape, self.alias, self.sems = list(operands), list(out_shape), dict(alias), list(sems)
        self.start, self.finish = start, finish


def _call(body, *, name, grid, in_specs, out_specs, out_shape, args, vmem=None, stages=(), prefetch=None,
          scratch=()):
    nin, nout = len(in_specs), len(out_specs)
    npre = 0 if prefetch is None else 1
    st_args, st_shapes, st_sems, aliases = [], [], list(scratch), {}
    for st in stages:
        for k, v in st.alias.items():
            aliases[npre + nin + len(st_args) + k] = nout + len(st_shapes) + v
        st_args += st.operands
        st_shapes += st.out_shape
        st_sems += st.sems

    def wrapped(*refs):
        pre, refs = refs[:npre], refs[npre:]
        ins, pos = refs[:nin], nin
        st_ins = []
        for st in stages:
            st_ins.append(refs[pos:pos + len(st.operands)])
            pos += len(st.operands)
        outs, pos = refs[pos:pos + nout], pos + nout
        st_outs = []
        for st in stages:
            st_outs.append(refs[pos:pos + len(st.out_shape)])
            pos += len(st.out_shape)
        work, pos = refs[pos:pos + len(scratch)], pos + len(scratch)
        sems = []
        for st in stages:
            sems.append(refs[pos:pos + len(st.sems)])
            pos += len(st.sems)
        if stages:
            first = functools.reduce(jnp.logical_and, [pl.program_id(a) == 0 for a in range(len(grid))])

            @pl.when(first)
            def _():
                for st, a, b, s in zip(stages, st_ins, st_outs, sems):
                    st.start(a, b, s)

        body(*pre, *ins, *outs, *work)
        if stages:
            last = functools.reduce(jnp.logical_and, [pl.program_id(a) == g - 1 for a, g in enumerate(grid)])

            @pl.when(last)
            def _():
                for st, a, b, s in zip(stages, st_ins, st_outs, sems):
                    st.finish(a, b, s)

    all_in = list(in_specs) + [ANY] * len(st_args)
    all_out = list(out_specs) + [ANY] * len(st_shapes)
    kw = dict(has_side_effects=True) if stages else {}
    if vmem is not None:
        kw["vmem_limit_bytes"] = vmem * MIB
    if prefetch is None:
        gkw = dict(grid=grid, in_specs=all_in, out_specs=all_out, scratch_shapes=st_sems)
    else:
        gkw = dict(grid_spec=pltpu.PrefetchScalarGridSpec(
            num_scalar_prefetch=1, grid=grid, in_specs=all_in, out_specs=all_out, scratch_shapes=st_sems))
    res = pl.pallas_call(
        wrapped, name=name, out_shape=_hbm_out(list(out_shape) + st_shapes), input_output_aliases=aliases,
        compiler_params=pltpu.CompilerParams(**kw), **gkw,
    )(*([prefetch] if npre else []), *_hbm(*args, *st_args))
    outs, rest, st_res = list(res[:nout]), list(res[nout:]), []
    for st in stages:
        st_res.append(rest[:len(st.out_shape)])
        rest = rest[len(st.out_shape):]
    return outs, st_res


def _mm(a, b):
    return jnp.dot(a.astype(BF), b.astype(BF), preferred_element_type=F32)


def _mm_nt(a, b):
    return lax.dot_general(a.astype(BF), b.astype(BF), (((1,), (1,)), ((), ())),
                           preferred_element_type=F32)


def _mm_tn(a, b):
    return lax.dot_general(a.astype(BF), b.astype(BF), (((0,), (0,)), ((), ())),
                           preferred_element_type=F32)


def _rows(v):
    return lax.broadcasted_iota(jnp.int32, v.shape, 0)


def _sd(v, s, fill=0.0):
    return jnp.where(_rows(v) >= s, pltpu.roll(v, s, axis=0), fill)


def _su(v, s, fill=0.0):
    n = v.shape[0]
    return jnp.where(_rows(v) < n - s, pltpu.roll(v, n - s, axis=0), fill)


def _sigmoid(z):
    return 1.0 / (1.0 + jnp.exp(-z))


def _softplus(z):
    e = jnp.exp(-jnp.abs(z))
    u = 1.0 + e
    d = u - 1.0
    log1p = jnp.where(d == 0.0, e, jnp.log(u) * (e / jnp.where(d == 0.0, 1.0, d)))
    return jnp.maximum(z, 0.0) + log1p


def _mean(v):
    return jnp.mean(v, axis=-1, keepdims=True)


def _colsum(v):
    return jnp.sum(v, axis=0, keepdims=True)


def _acc(ref, val, first):
    @pl.when(first)
    def _():
        ref[...] = val

    @pl.when(jnp.logical_not(first))
    def _():
        ref[...] += val


def _conv(xp, cw, cb):
    x1, x2, x3 = _sd(xp, 1), _sd(xp, 2), _sd(xp, 3)
    xc = cb + cw[0:1] * x3 + cw[1:2] * x2 + cw[2:3] * x1 + cw[3:4] * xp
    return xc, x1, x2, x3


def _lru_gates(xc, wa, ba, wx, bx, lam):
    xcb = xc.astype(BF)
    r = _sigmoid(_mm(xcb, wa) + ba)
    ii = _sigmoid(_mm(xcb, wx) + bx)
    sp = _softplus(-lam)
    la = (-LRU_C) * r * sp
    a = jnp.exp(la)
    mult = jnp.sqrt(-jnp.tanh(la) * (a * a + 1.0))
    return xcb, r, ii, sp, a, mult


def _gelu_parts(g):
    th = jnp.tanh(GELU_C * (g + 0.044715 * (g * g * g)))
    gel = 0.5 * g * (1.0 + th)
    dgel = 0.5 * (1.0 + th) + 0.5 * g * (1.0 - th * th) * (GELU_C * (1.0 + 3.0 * 0.044715 * (g * g)))
    return gel, dgel


def _tile_scan(a, b, a_s, b_s, out_ref, reverse):
    n, lanes = a.shape
    nt = n // 8
    a, b = a.reshape(nt, 8, lanes), b.reshape(nt, 8, lanes)
    sub = lax.broadcasted_iota(jnp.int32, a.shape, 1)
    s = 1
    while s < 8:
        keep = sub < 8 - s if reverse else sub >= s
        amount = 8 - s if reverse else s
        b = b + a * jnp.where(keep, pltpu.roll(b, amount, axis=1), 0.0)
        a = a * jnp.where(keep, pltpu.roll(a, amount, axis=1), 1.0)
        s *= 2
    a_s[...] = a.reshape(n, lanes)
    b_s[...] = b.reshape(n, lanes)
    edge = pl.ds(0 if reverse else 7, nt, stride=8)
    ta, tb = a_s[edge, :], b_s[edge, :]
    shift = _su if reverse else _sd
    s = 1
    while s < nt:
        tb = tb + ta * shift(tb, s, 0.0)
        if 2 * s < nt:
            ta = ta * shift(ta, s, 1.0)
        s *= 2
    enters = shift(tb, 1, 0.0)
    for o in range(8):
        rows = pl.ds(o, nt, stride=8)
        out_ref[rows, :] = b_s[rows, :] + a_s[rows, :] * enters


def _pool_window(x, steps, shift):
    s, sh = x, 1
    for _ in range(steps):
        s = s + shift(s, sh)
        sh *= 2
    return s


def _fwd_inproj_own(x, g1, w_in, slots, stages=()):
    tm = 1024

    def body(s_ref, x_ref, g_ref, w_ref, proj_ref, h_ref):
        xv = x_ref[...]
        r = lax.rsqrt(_mean(xv * xv) + NORM_EPS)
        h = ((xv * r) * g_ref[...]).astype(BF)
        h_ref[...] = h
        proj_ref[...] = jnp.dot(h, w_ref[0], preferred_element_type=F32)

    return _call(
        body, name="fwd_inproj_own", grid=(T // tm,), prefetch=slots,
        in_specs=[pl.BlockSpec((tm, D), lambda i, s: (i, 0)),
                  pl.BlockSpec((1, D), lambda i, s: (0, 0)),
                  pl.BlockSpec((1, D, CW_IN), lambda i, s: (s[0], 0, 0))],
        out_specs=[pl.BlockSpec((tm, CW_IN), lambda i, s: (i, s[0])),
                   pl.BlockSpec((tm, D), lambda i, s: (i, 0))],
        out_shape=[jax.ShapeDtypeStruct((T, DIN), F32), jax.ShapeDtypeStruct((T, D), BF)],
        vmem=40, args=[x, g1, w_in], stages=stages)[0]


def _fwd_inproj_rest(h1, w_in, proj, slots):
    tm = 1024

    def body(s_ref, h_ref, w_ref, p_in, proj_ref):
        proj_ref[...] = jnp.dot(h_ref[...], w_ref[0], preferred_element_type=F32)

    res = pl.pallas_call(
        body, name="fwd_inproj_rest",
        grid_spec=pltpu.PrefetchScalarGridSpec(
            num_scalar_prefetch=1, grid=(NCHIP - 1, T // tm),
            in_specs=[pl.BlockSpec((tm, D), lambda k, i, s: (i, 0)),
                      pl.BlockSpec((1, D, CW_IN), lambda k, i, s: (s[1 + k], 0, 0)), ANY],
            out_specs=pl.BlockSpec((tm, CW_IN), lambda k, i, s: (i, s[1 + k]))),
        out_shape=pltpu.HBM((T, DIN), F32), input_output_aliases={3: 0},
        compiler_params=_cp(40),
    )(slots, *_hbm(h1, w_in, proj))
    return res


def _vec_spec():
    return pl.BlockSpec((1, CB), lambda j: (0, j))


def _fwd_lru(proj, conv_w, conv_b, wa, ba, wx, bx, lam, stages=()):
    def body(xp_ref, g_ref, cw_ref, cb_ref, wa_ref, ba_ref, wx_ref, bx_ref, lam_ref, y_ref, h_ref, a_s, b_s):
        xc, _, _, _ = _conv(xp_ref[...], cw_ref[...], cb_ref[...])
        _, _, ii, _, a, mult = _lru_gates(xc, wa_ref[0], ba_ref[...], wx_ref[0], bx_ref[...], lam_ref[...])
        _tile_scan(a, mult * (ii * xc), a_s, b_s, h_ref, reverse=False)
        gel, _ = _gelu_parts(g_ref[...])
        y_ref[...] = (h_ref[...] * gel).astype(BF)

    return _call(
        body, name="fwd_lru", grid=(NG,),
        in_specs=[pl.BlockSpec((T, CB), lambda j: (0, j)),
                  pl.BlockSpec((T, CB), lambda j: (0, NG + j)),
                  pl.BlockSpec((4, CB), lambda j: (0, j)),
                  _vec_spec(),
                  pl.BlockSpec((1, CB, CB), lambda j: (j, 0, 0)), _vec_spec(),
                  pl.BlockSpec((1, CB, CB), lambda j: (j, 0, 0)), _vec_spec(),
                  _vec_spec()],
        out_specs=[pl.BlockSpec((T, CB), lambda j: (0, j)), pl.BlockSpec((T, CB), lambda j: (0, j))],
        out_shape=[jax.ShapeDtypeStruct((T, DR), BF), jax.ShapeDtypeStruct((T, DR), F32)],
        vmem=48, args=[proj, proj, conv_w, conv_b, wa, ba, wx, bx, lam], stages=stages,
        scratch=[pltpu.VMEM((T, CB), F32)] * 2)


def _pool_cnt(w):
    t = lax.broadcasted_iota(jnp.int32, (T, 1), 0)
    return jnp.minimum(t + 1, w).astype(F32)


def _fwd_pool(proj, pool_w, pool_scale):
    def body(xp_ref, pw_ref, sc_ref, y_ref):
        for g, w in enumerate(POOL_WINDOWS):
            cols = slice(g * PG, (g + 1) * PG)
            x = xp_ref[:, cols]
            p = _pool_window(x, g + 1, _sd) / _pool_cnt(w) - x
            y_ref[:, cols] = (_mm(p, pw_ref[g]) * sc_ref[:, cols]).astype(BF)

    return pl.pallas_call(
        body, name="fwd_pool", grid=(1,),
        in_specs=[pl.BlockSpec((T, DP), lambda i: (0, 2 * DR // DP)),
                  pl.BlockSpec((4, PG, PG), lambda i: (0, 0, 0)),
                  pl.BlockSpec((1, DP), lambda i: (0, 0))],
        out_specs=pl.BlockSpec((T, DP), lambda i: (0, 0)),
        out_shape=pltpu.HBM((T, DP), BF),
        compiler_params=_cp(48),
    )(*_hbm(proj, pool_w, pool_scale))


GATE_BLK = 512
GATE_BLK0 = (2 * DR + DP) // GATE_BLK


def _gate_specs(tm):
    return [pl.BlockSpec((tm, GATE_BLK), functools.partial(lambda i, q: (i, GATE_BLK0 + q), q=q))
            for q in range(4)]


def _fwd_merge(x, ylru, ypool, proj, b_gate, g2, g3, w_lru_up, w_pool_up, w_o, stages=()):
    tm = 512

    def body(x_ref, yl_ref, yp_ref, p0, p1, p2, p3, bg_ref, g2_ref, g3_ref, wl_ref, wp_ref, wo_ref,
             x2_ref, h2_ref, m_ref, mrg_ref, bra_ref, brb_ref):
        bra = jnp.dot(yl_ref[...], wl_ref[...], preferred_element_type=F32)
        yp = yp_ref[...]
        brb = jnp.concatenate([jnp.dot(yp, wp_ref[k], preferred_element_type=F32) for k in range(NCHIP)], axis=1)
        bg = bg_ref[...]
        ga = _sigmoid(jnp.concatenate([p0[...], p1[...]], axis=1) + bg[:, :D])
        gb = _sigmoid(jnp.concatenate([p2[...], p3[...]], axis=1) + bg[:, D:])
        mrg = (ga * bra + gb * brb).astype(BF)
        m = jnp.dot(mrg, wo_ref[...], preferred_element_type=F32)
        r2 = lax.rsqrt(_mean(m * m) + NORM_EPS)
        x2 = x_ref[...] + (m * r2) * g2_ref[...]
        r3 = lax.rsqrt(_mean(x2 * x2) + NORM_EPS)
        x2_ref[...] = x2
        h2_ref[...] = ((x2 * r3) * g3_ref[...]).astype(BF)
        m_ref[...] = m
        mrg_ref[...] = mrg
        bra_ref[...] = bra.astype(BF)
        brb_ref[...] = brb.astype(BF)

    row = lambda w: pl.BlockSpec((tm, w), lambda i: (i, 0))
    full2 = lambda a, b: pl.BlockSpec((a, b), lambda i: (0, 0))
    return _call(
        body, name="fwd_merge", grid=(T // tm,),
        in_specs=[row(D), row(DR), row(DP)] + _gate_specs(tm) +
                 [full2(1, 2 * D), full2(1, D), full2(1, D), full2(DR, D),
                  pl.BlockSpec((NCHIP, DP, D // NCHIP), lambda i: (0, 0, 0)), full2(D, D)],
        out_specs=[row(D)] * 6,
        out_shape=[jax.ShapeDtypeStruct((T, D), F32), jax.ShapeDtypeStruct((T, D), BF),
                   jax.ShapeDtypeStruct((T, D), F32), jax.ShapeDtypeStruct((T, D), BF),
                   jax.ShapeDtypeStruct((T, D), BF), jax.ShapeDtypeStruct((T, D), BF)],
        vmem=48, args=[x, ylru, ypool, proj, proj, proj, proj, b_gate, g2, g3, w_lru_up, w_pool_up, w_o],
        stages=stages)


def _fwd_mlp_loss(h2, w_ff1, w_ff2, x2, target, g4):
    tm = 512
    fk = DF // NCHIP

    def body(h_ref, w1_ref, w2_ref, x2_ref, t_ref, g_ref, a1_ref, loss_ref, dy_ref, df_ref, dg_ref):
        first = pl.program_id(0) == 0
        h = h_ref[...]
        f = None
        for k in range(NCHIP):
            a1 = jnp.maximum(jnp.dot(h, w1_ref[k], preferred_element_type=F32), 0.0)
            a1_ref[:, k * fk:(k + 1) * fk] = a1.astype(BF)
            part = jnp.dot((a1 * a1).astype(BF), w2_ref[k * fk:(k + 1) * fk, :], preferred_element_type=F32)
            f = part if f is None else f + part
        g4v = g_ref[...]
        r4 = lax.rsqrt(_mean(f * f) + NORM_EPS)
        fn = f * r4
        e = (x2_ref[...] + fn * g4v) - t_ref[...]
        _acc(loss_ref, jnp.sum(_mean(e * e), axis=0, keepdims=True), first)
        dy = e * (1.0 / D)
        dy_ref[...] = dy
        _acc(dg_ref, _colsum(dy * fn), first)
        dfn = dy * g4v
        df_ref[...] = (r4 * (dfn - fn * _mean(dfn * fn))).astype(BF)

    row = pl.BlockSpec((tm, D), lambda i: (i, 0))
    return pl.pallas_call(
        body, name="fwd_mlp_loss", grid=(T // tm,),
        in_specs=[row, pl.BlockSpec((NCHIP, D, fk), lambda i: (0, 0, 0)), pl.BlockSpec((DF, D), lambda i: (0, 0)),
                  row, row, pl.BlockSpec((1, D), lambda i: (0, 0))],
        out_specs=[pl.BlockSpec((tm, DF), lambda i: (i, 0)), pl.BlockSpec((1, 1), lambda i: (0, 0)), row, row,
                   pl.BlockSpec((1, D), lambda i: (0, 0))],
        out_shape=_hbm_out([jax.ShapeDtypeStruct((T, DF), BF), jax.ShapeDtypeStruct((1, 1), F32),
                            jax.ShapeDtypeStruct((T, D), F32), jax.ShapeDtypeStruct((T, D), BF),
                            jax.ShapeDtypeStruct((1, D), F32)]),
        compiler_params=_cp(56),
    )(*_hbm(h2, w_ff1, w_ff2, x2, target, g4))


def _bwd_mlp_x(df, a1, w_ff1, w_ff2):
    tm = 512
    fk = DF // NCHIP

    def body(df_ref, a1_ref, w1_ref, w2_ref, dh_ref, df1_ref):
        df = df_ref[...]
        dh = None
        for k in range(NCHIP):
            cols = slice(k * fk, (k + 1) * fk)
            dact = _mm_nt(df, w2_ref[cols, :])
            df1 = (dact * (2.0 * a1_ref[:, cols].astype(F32))).astype(BF)
            df1_ref[:, cols] = df1
            part = _mm_nt(df1, w1_ref[k])
            dh = part if dh is None else dh + part
        dh_ref[...] = dh

    return pl.pallas_call(
        body, name="bwd_mlp_x", grid=(T // tm,),
        in_specs=[pl.BlockSpec((tm, D), lambda i: (i, 0)),
                  pl.BlockSpec((tm, DF), lambda i: (i, 0)),
                  pl.BlockSpec((NCHIP, D, fk), lambda i: (0, 0, 0)),
                  pl.BlockSpec((DF, D), lambda i: (0, 0))],
        out_specs=[pl.BlockSpec((tm, D), lambda i: (i, 0)), pl.BlockSpec((tm, DF), lambda i: (i, 0))],
        out_shape=_hbm_out([jax.ShapeDtypeStruct((T, D), F32), jax.ShapeDtypeStruct((T, DF), BF)]),
        compiler_params=_cp(56),
    )(*_hbm(df, a1, w_ff1, w_ff2))


def _bwd_mlp_w(df, h2, a1, df1):
    fc = 512
    per = (DF // NCHIP) // fc

    def body(df_ref, h_ref, a1_ref, df1_ref, dw1_ref, dw2_ref):
        a1 = a1_ref[...].astype(F32)
        dw2_ref[...] = _mm_tn((a1 * a1).astype(BF), df_ref[...]).astype(BF)
        dw1_ref[0] = _mm_tn(h_ref[...], df1_ref[...]).astype(BF)

    return pl.pallas_call(
        body, name="bwd_mlp_w", grid=(DF // fc,),
        in_specs=[pl.BlockSpec((T, D), lambda j: (0, 0)),
                  pl.BlockSpec((T, D), lambda j: (0, 0)),
                  pl.BlockSpec((T, fc), lambda j: (0, j)),
                  pl.BlockSpec((T, fc), lambda j: (0, j))],
        out_specs=[pl.BlockSpec((1, D, fc), lambda j: (j // per, 0, j % per)),
                   pl.BlockSpec((fc, D), lambda j: (j, 0))],
        out_shape=_hbm_out([jax.ShapeDtypeStruct((NCHIP, D, DF // NCHIP), BF),
                            jax.ShapeDtypeStruct((DF, D), BF)]),
        compiler_params=_cp(56),
    )(*_hbm(df, h2, a1, df1))


def _bwd_merge(dh2, dy, x2, m, bra, brb, proj, b_gate, g2, g3, w_lru_up, w_pool_up, w_o, stages=()):
    tm = 256
    cpu = D // NCHIP

    def body(dh2_ref, dy_ref, x2_ref, m_ref, bra_ref, brb_ref, p0, p1, p2, p3, bg_ref,
             g2_ref, g3_ref, wl_ref, wp_ref, wo_ref,
             dx_ref, dgt_ref, dyl_ref, dyp_ref, dm_ref, dbra_ref, dbrb_ref, dg2_ref, dg3_ref, dbg_ref):
        first = pl.program_id(0) == 0
        x2 = x2_ref[...]
        r3 = lax.rsqrt(_mean(x2 * x2) + NORM_EPS)
        x2n = x2 * r3
        dh2 = dh2_ref[...]
        t3 = dh2 * g3_ref[...]
        dx2 = dy_ref[...] + r3 * (t3 - x2n * _mean(t3 * x2n))
        dx_ref[...] = dx2
        _acc(dg3_ref, _colsum(dh2 * x2n), first)
        m = m_ref[...]
        r2 = lax.rsqrt(_mean(m * m) + NORM_EPS)
        mn = m * r2
        _acc(dg2_ref, _colsum(dx2 * mn), first)
        dmn = dx2 * g2_ref[...]
        dm = (r2 * (dmn - mn * _mean(dmn * mn))).astype(BF)
        dm_ref[...] = dm
        dmrg = _mm_nt(dm, wo_ref[...])
        bg = bg_ref[...]
        ga = _sigmoid(jnp.concatenate([p0[...], p1[...]], axis=1) + bg[:, :D])
        gb = _sigmoid(jnp.concatenate([p2[...], p3[...]], axis=1) + bg[:, D:])
        dga = dmrg * bra_ref[...].astype(F32) * (ga * (1.0 - ga))
        dgb = dmrg * brb_ref[...].astype(F32) * (gb * (1.0 - gb))
        dgt_ref[:, :D] = dga.astype(BF)
        dgt_ref[:, D:] = dgb.astype(BF)
        _acc(dbg_ref, jnp.concatenate([_colsum(dga), _colsum(dgb)], axis=1), first)
        dbra = (dmrg * ga).astype(BF)
        dbrb = (dmrg * gb).astype(BF)
        dbra_ref[...] = dbra
        dbrb_ref[...] = dbrb
        dyl_ref[...] = _mm_nt(dbra, wl_ref[...])
        dyp = None
        for k in range(NCHIP):
            part = _mm_nt(dbrb[:, k * cpu:(k + 1) * cpu], wp_ref[k])
            dyp = part if dyp is None else dyp + part
        dyp_ref[...] = dyp

    row = lambda w: pl.BlockSpec((tm, w), lambda i: (i, 0))
    full2 = lambda a, b: pl.BlockSpec((a, b), lambda i: (0, 0))
    wp_spec = pl.BlockSpec((NCHIP, DP, cpu), lambda i: (0, 0, 0))
    return _call(
        body, name="bwd_merge", grid=(T // tm,),
        in_specs=[row(D)] * 6 + _gate_specs(tm) +
                 [full2(1, 2 * D), full2(1, D), full2(1, D), full2(DR, D), wp_spec, full2(D, D)],
        out_specs=[row(D), row(2 * D), row(DR), row(DP), row(D), row(D), row(D),
                   full2(1, D), full2(1, D), full2(1, 2 * D)],
        out_shape=[jax.ShapeDtypeStruct((T, D), F32), jax.ShapeDtypeStruct((T, 2 * D), BF),
                   jax.ShapeDtypeStruct((T, DR), F32), jax.ShapeDtypeStruct((T, DP), F32),
                   jax.ShapeDtypeStruct((T, D), BF), jax.ShapeDtypeStruct((T, D), BF),
                   jax.ShapeDtypeStruct((T, D), BF),
                   jax.ShapeDtypeStruct((1, D), F32), jax.ShapeDtypeStruct((1, D), F32),
                   jax.ShapeDtypeStruct((1, 2 * D), F32)],
        vmem=56, args=[dh2, dy, x2, m, bra, brb, proj, proj, proj, proj, b_gate, g2, g3, w_lru_up, w_pool_up, w_o],
        stages=stages)


def _dw_merge(mrg, dm, ylru, dbra, ypool, dbrb, stages=()):
    nb = NCHIP
    rb, pb, cpu = D // nb, DP // nb, D // NCHIP

    def body(mrg_ref, dm_ref, yl_ref, dbra_ref, yp_ref, dbrb_ref, dwo_ref, dwl_ref, dwp_ref):
        dwo_ref[...] = _mm_tn(mrg_ref[...], dm_ref[...]).astype(BF)
        dwl_ref[...] = _mm_tn(yl_ref[...], dbra_ref[...]).astype(BF)
        dwp = _mm_tn(yp_ref[...], dbrb_ref[...]).astype(BF)
        for k in range(NCHIP):
            dwp_ref[k] = dwp[:, k * cpu:(k + 1) * cpu]

    cols = lambda w: pl.BlockSpec((T, w), lambda r: (0, r))
    whole = pl.BlockSpec((T, D), lambda r: (0, 0))
    return _call(
        body, name="dw_merge", grid=(nb,),
        in_specs=[cols(rb), whole, cols(rb), whole, cols(pb), whole],
        out_specs=[pl.BlockSpec((rb, D), lambda r: (r, 0)), pl.BlockSpec((rb, D), lambda r: (r, 0)),
                   pl.BlockSpec((NCHIP, pb, cpu), lambda r: (0, r, 0))],
        out_shape=[jax.ShapeDtypeStruct((D, D), BF), jax.ShapeDtypeStruct((DR, D), BF),
                   jax.ShapeDtypeStruct((NCHIP, DP, cpu), BF)],
        vmem=56, args=[mrg, dm, ylru, dbra, ypool, dbrb], stages=stages)


def _bwd_lru(proj, h, dylru, conv_w, conv_b, wa, ba, wx, bx, lam, stages=()):
    def body(xp_ref, g_ref, h_ref, dy_ref, cw_ref, cb_ref, wa_ref, ba_ref, wx_ref, bx_ref, lam_ref,
             dxp_ref, dg_ref, dcw_ref, dcb_ref, dwa_ref, dba_ref, dwx_ref, dbx_ref, dlam_ref, a_s, b_s, l_s):
        xp = xp_ref[...]
        cw = cw_ref[...]
        lam = lam_ref[...]
        xc, x1, x2, x3 = _conv(xp, cw, cb_ref[...])
        wa, wx = wa_ref[0], wx_ref[0]
        xcb, r, ii, sp, a, mult = _lru_gates(xc, wa, ba_ref[...], wx, bx_ref[...], lam)
        g = g_ref[...]
        gel, dgel = _gelu_parts(g)
        h = h_ref[...]
        dy = dy_ref[...]
        dg_ref[...] = (dy * h * dgel).astype(BF)
        _tile_scan(_su(a, 1, 0.0), dy * gel, a_s, b_s, l_s, reverse=True)
        b = l_s[...]
        da = b * _sd(h, 1, 0.0)
        dmult = b * (ii * xc)
        dii = b * (mult * xc)
        dxc = b * (mult * ii)
        dla = da * a - dmult * ((a * a) / mult)
        dr = dla * ((-LRU_C) * sp)
        dsp = _colsum(dla * ((-LRU_C) * r))
        dlam_ref[...] = -dsp / (1.0 + jnp.exp(lam))
        dzr = dr * (r * (1.0 - r))
        dzi = dii * (ii * (1.0 - ii))
        dzrb, dzib = dzr.astype(BF), dzi.astype(BF)
        dxc = dxc + _mm_nt(dzrb, wa) + _mm_nt(dzib, wx)
        dwa_ref[0] = _mm_tn(xcb, dzrb)
        dwx_ref[0] = _mm_tn(xcb, dzib)
        dba_ref[...] = _colsum(dzr)
        dbx_ref[...] = _colsum(dzi)
        dcb_ref[...] = _colsum(dxc)
        dcw_ref[...] = jnp.concatenate([_colsum(dxc * x3), _colsum(dxc * x2), _colsum(dxc * x1),
                                        _colsum(dxc * xp)], axis=0)
        dxp = cw[3:4] * dxc + cw[2:3] * _su(dxc, 1) + cw[1:2] * _su(dxc, 2) + cw[0:1] * _su(dxc, 3)
        dxp_ref[...] = dxp.astype(BF)

    blk = pl.BlockSpec((T, CB), lambda j: (0, j))
    wsp = pl.BlockSpec((1, CB, CB), lambda j: (j, 0, 0))
    return _call(
        body, name="bwd_lru", grid=(NG,),
        in_specs=[blk, pl.BlockSpec((T, CB), lambda j: (0, NG + j)), blk, blk,
                  pl.BlockSpec((4, CB), lambda j: (0, j)), _vec_spec(), wsp, _vec_spec(), wsp, _vec_spec(),
                  _vec_spec()],
        out_specs=[blk, blk, pl.BlockSpec((4, CB), lambda j: (0, j)), _vec_spec(), wsp, _vec_spec(), wsp,
                   _vec_spec(), _vec_spec()],
        out_shape=[jax.ShapeDtypeStruct((T, DR), BF), jax.ShapeDtypeStruct((T, DR), BF),
                   jax.ShapeDtypeStruct((4, DR), F32), jax.ShapeDtypeStruct((1, DR), F32),
                   jax.ShapeDtypeStruct((NG, CB, CB), F32), jax.ShapeDtypeStruct((1, DR), F32),
                   jax.ShapeDtypeStruct((NG, CB, CB), F32), jax.ShapeDtypeStruct((1, DR), F32),
                   jax.ShapeDtypeStruct((1, DR), F32)],
        vmem=56, args=[proj, proj, h, dylru, conv_w, conv_b, wa, ba, wx, bx, lam], stages=stages,
        scratch=[pltpu.VMEM((T, CB), F32)] * 3)


def _bwd_pool(proj, dypool, pool_w, pool_scale):
    def body(xp_ref, dy_ref, pw_ref, sc_ref, dx_ref, dw_ref, dsc_ref):
        for g, w in enumerate(POOL_WINDOWS):
            cols = slice(g * PG, (g + 1) * PG)
            cnt = _pool_cnt(w)
            x = xp_ref[:, cols]
            pb = (_pool_window(x, g + 1, _sd) / cnt - x).astype(BF)
            wg = pw_ref[g]
            dy = dy_ref[:, cols]
            dsc_ref[:, cols] = _colsum(dy * _mm(pb, wg))
            dyp = (dy * sc_ref[:, cols]).astype(BF)
            dw_ref[g] = _mm_tn(pb, dyp)
            dp = _mm_nt(dyp, wg)
            dx_ref[:, cols] = (_pool_window(dp / cnt, g + 1, _su) - dp).astype(BF)

    return pl.pallas_call(
        body, name="bwd_pool", grid=(1,),
        in_specs=[pl.BlockSpec((T, DP), lambda i: (0, 2 * DR // DP)),
                  pl.BlockSpec((T, DP), lambda i: (0, 0)),
                  pl.BlockSpec((4, PG, PG), lambda i: (0, 0, 0)),
                  pl.BlockSpec((1, DP), lambda i: (0, 0))],
        out_specs=[pl.BlockSpec((T, DP), lambda i: (0, 0)),
                   pl.BlockSpec((4, PG, PG), lambda i: (0, 0, 0)),
                   pl.BlockSpec((1, DP), lambda i: (0, 0))],
        out_shape=_hbm_out([jax.ShapeDtypeStruct((T, DP), BF), jax.ShapeDtypeStruct((4, PG, PG), F32),
                            jax.ShapeDtypeStruct((1, DP), F32)]),
        compiler_params=_cp(48),
    )(*_hbm(proj, dypool, pool_w, pool_scale))


PART_COLS = (DR, DR, DP, 2 * D)


def _shard_pieces():
    starts = [sum(PART_COLS[:p]) for p in range(len(PART_COLS))]
    shards = []
    for k in range(NCHIP):
        lo, hi = k * CW_IN, (k + 1) * CW_IN
        shards.append([(p, max(lo, s) - s, min(hi, s + wd) - s, max(lo, s) - lo)
                       for p, (s, wd) in enumerate(zip(starts, PART_COLS)) if max(lo, s) < min(hi, s + wd)])
    return shards


def _bwd_inproj_w(h1, parts, after):
    flat = [(k, *piece) for k, pieces in enumerate(_shard_pieces()) for piece in pieces]

    def body(h_hbm, p0, p1, p2, p3, after_ref, dw_hbm, h_v, dw_v, *rest):
        bufs, sem_in, sem_out = rest[:len(flat)], rest[len(flat)], rest[len(flat) + 1]
        part_refs = (p0, p1, p2, p3)
        loads = [pltpu.make_async_copy(h_hbm, h_v, sem_in.at[0])]
        for i, (k, p, a, b, c0) in enumerate(flat):
            loads.append(pltpu.make_async_copy(part_refs[p].at[:, pl.ds(a, b - a)], bufs[i], sem_in.at[1 + i]))
        ahead = 2
        for cp in loads[:1 + ahead]:
            cp.start()
        loads[0].wait()
        stores = []
        for i, (k, p, a, b, c0) in enumerate(flat):
            loads[1 + i].wait()
            if 1 + i + ahead < len(loads):
                loads[1 + i + ahead].start()
            dw_v[k, :, c0:c0 + b - a] = _mm_tn(h_v[...], bufs[i][...]).astype(BF)
            if i + 1 == len(flat) or flat[i + 1][0] != k:
                stores.append(pltpu.make_async_copy(dw_v.at[k], dw_hbm.at[k], sem_out.at[k]))
                stores[-1].start()
        for cp in stores:
            cp.wait()

    scratch = [pltpu.VMEM((T, D), BF), pltpu.VMEM((NCHIP, D, CW_IN), BF)]
    scratch += [pltpu.VMEM((T, b - a), parts[p].dtype) for k, p, a, b, c0 in flat]
    scratch += [pltpu.SemaphoreType.DMA((1 + len(flat),)), pltpu.SemaphoreType.DMA((NCHIP,))]
    return pl.pallas_call(
        body, name="bwd_inproj_w", in_specs=[ANY] * 6, out_specs=ANY, scratch_shapes=scratch,
        out_shape=pltpu.HBM((NCHIP, D, CW_IN), BF), compiler_params=_cp(48),
    )(*_hbm(h1, *parts), after)


def _bwd_inproj_x(parts, w_in, x, dxres, g1, stages=()):
    tm = 512

    def body(p0, p1, p2, p3, w_ref, x_ref, dr_ref, g_ref, dx_ref, dg_ref):
        part_refs = (p0, p1, p2, p3)
        dh = None
        for k, pieces in enumerate(_shard_pieces()):
            for p, a, b, c0 in pieces:
                part = _mm_nt(part_refs[p][:, a:b], w_ref[k, :, c0:c0 + b - a])
                dh = part if dh is None else dh + part
        xv = x_ref[...]
        r = lax.rsqrt(_mean(xv * xv) + NORM_EPS)
        xn = xv * r
        t = dh * g_ref[...]
        dx_ref[...] = dr_ref[...] + r * (t - xn * _mean(t * xn))
        _acc(dg_ref, _colsum(dh * xn), pl.program_id(0) == 0)

    row = pl.BlockSpec((tm, D), lambda i: (i, 0))
    vec = pl.BlockSpec((1, D), lambda i: (0, 0))
    return _call(
        body, name="bwd_inproj_x", grid=(T // tm,),
        in_specs=[pl.BlockSpec((tm, wd), lambda i: (i, 0)) for wd in PART_COLS] +
                 [pl.BlockSpec((NCHIP, D, CW_IN), lambda i: (0, 0, 0)), row, row, vec],
        out_specs=[row, vec],
        out_shape=[jax.ShapeDtypeStruct((T, D), F32), jax.ShapeDtypeStruct((1, D), F32)],
        vmem=56, args=[*parts, w_in, x, dxres, g1], stages=stages)[0]


def _place():
    x, y, c = lax.axis_index("x"), lax.axis_index("y"), lax.axis_index("c")
    chips = [(1 - x, y), (x, 1 - y), (1 - x, 1 - y)]
    return x, y, c, chips


def _rcopy(src, dst, ssem, rsem, dev):
    return pltpu.make_async_remote_copy(src_ref=src, dst_ref=dst, send_sem=ssem, recv_sem=rsem,
                                        device_id=dev, device_id_type=MESH_ID)


def _sds(a):
    return jax.ShapeDtypeStruct(a.shape, a.dtype)


def _sem2(n, m):
    return [pltpu.SemaphoreType.DMA((n * m,)), pltpu.SemaphoreType.DMA((n * m,))]


ALL = (0, 1, 1)


def _piece(ref, k, half, part):
    hr = ref.shape[1] // 2
    r0, r1 = hr * part[0] // part[2], hr * part[1] // part[2]
    return ref.at[k, pl.ds(half * hr + r0, r1 - r0), :]


def _gather(fulls, ici=(), d2d=()):
    n = len(fulls)
    ici, d2d = list(ici), list(d2d)
    pieces = [("ici", i, part) for i, part in ici] + [("d2d", i, part) for i, part in d2d]

    def copies(outs, sems):
        x, y, c, chips = _place()
        me = 2 * x + y
        sib = (x, y, 1 - c)
        send, recv = [], []
        for q, (kind, i, part) in enumerate(pieces):
            for j, chip in enumerate(chips):
                k, s = 2 * chip[0] + chip[1], 3 * q + j
                if kind == "ici":
                    mine, theirs, dev = _piece(outs[i], me, c, part), _piece(outs[i], k, c, part), (*chip, c)
                else:
                    mine, theirs, dev = _piece(outs[i], k, c, part), _piece(outs[i], k, 1 - c, part), sib
                send.append(_rcopy(mine, mine, sems[0].at[s], sems[1].at[s], dev))
                recv.append(_rcopy(theirs, theirs, sems[0].at[s], sems[1].at[s], dev))
        return send, recv

    def start(ins, outs, sems):
        for cp in copies(outs, sems)[0]:
            cp.start()

    def finish(ins, outs, sems):
        send, recv = copies(outs, sems)
        for cp in recv:
            cp.wait_recv()
        for cp in send:
            cp.wait_send()

    sems = [pltpu.SemaphoreType.DMA((3 * len(pieces),)), pltpu.SemaphoreType.DMA((3 * len(pieces),))]
    return _Stage(fulls, [_sds(f) for f in fulls], {i: i for i in range(n)}, sems, start, finish)


def _gather_whole(v):
    def copies(ins, outs, sems):
        x, y, c, chips = _place()
        me = 2 * x + y
        send = [_rcopy(ins[0], outs[0].at[me], sems[0].at[j], sems[1].at[j], (*chip, c))
                for j, chip in enumerate(chips)]
        recv = [_rcopy(ins[0], outs[0].at[2 * chip[0] + chip[1]], sems[0].at[j], sems[1].at[j], (*chip, c))
                for j, chip in enumerate(chips)]
        return send, recv

    def start(ins, outs, sems):
        for cp in copies(ins, outs, sems)[0]:
            cp.start()

    def finish(ins, outs, sems):
        send, recv = copies(ins, outs, sems)
        for cp in recv:
            cp.wait_recv()
        for cp in send:
            cp.wait_send()

    return _Stage([v], [jax.ShapeDtypeStruct((NCHIP,) + v.shape, v.dtype)], {},
                  [pltpu.SemaphoreType.DMA((3,)), pltpu.SemaphoreType.DMA((3,))], start, finish)


def _to_sibling(srcs):
    n = len(srcs)

    def copies(ins, outs, sems):
        x, y, c, _ = _place()
        sib = (x, y, 1 - c)
        return [_rcopy(ins[i].at[:, 1 - c] if srcs[i].ndim == 4 else ins[i], outs[i], sems[0].at[i], sems[1].at[i], sib)
                for i in range(n)]

    def start(ins, outs, sems):
        for cp in copies(ins, outs, sems):
            cp.start()

    def finish(ins, outs, sems):
        for cp in copies(ins, outs, sems):
            cp.wait()

    shapes = [jax.ShapeDtypeStruct((NCHIP,) + s.shape[2:] if s.ndim == 4 else s.shape, s.dtype) for s in srcs]
    return _Stage(srcs, shapes, {}, [pltpu.SemaphoreType.DMA((n,)), pltpu.SemaphoreType.DMA((n,))], start, finish)


def _to_chips(srcs, parts=None, lands=None):
    n = len(srcs)
    parts = [ALL] * n if parts is None else parts
    lands = [None] * n if lands is None else lands
    given = [i for i in range(n) if lands[i] is not None]

    def rows(ref, i):
        hr = srcs[i].shape[1]
        r0, r1 = hr * parts[i][0] // parts[i][2], hr * parts[i][1] // parts[i][2]
        return ref.at[pl.ds(r0, r1 - r0), :]

    def copies(ins, outs, sems):
        x, y, c, chips = _place()
        me = 2 * x + y
        return [_rcopy(rows(ins[i].at[2 * chip[0] + chip[1]] if srcs[i].shape[0] == NCHIP else ins[i].at[c], i),
                       rows(outs[i].at[me], i), sems[0].at[3 * i + j], sems[1].at[3 * i + j], (*chip, c))
                for i in range(n) for j, chip in enumerate(chips)]

    def start(ins, outs, sems):
        for cp in copies(ins, outs, sems):
            cp.start()

    def finish(ins, outs, sems):
        for cp in copies(ins, outs, sems):
            cp.wait()

    shapes = [jax.ShapeDtypeStruct((NCHIP,) + s.shape[1:], s.dtype) for s in srcs]
    alias = {n + q: i for q, i in enumerate(given)}
    return _Stage(list(srcs) + [lands[i] for i in given], shapes, alias, _sem2(n, 3), start, finish)


HBM_REF = pl.BlockSpec(memory_space=pltpu.HBM)
SEM_REF = pl.BlockSpec(memory_space=pltpu.SEMAPHORE)
DATAFLOW = pltpu.SideEffectType.DATAFLOW_SIDE_EFFECTING


def _after(x):
    return _Stage([x], [], {}, [], lambda *a: None, lambda *a: None)


class _Flight:
    def __init__(self, stage, sems, bufs):
        self.stage, self.sems, self.bufs = stage, list(sems), list(bufs)

    def landed(self):
        st, n = self.stage, len(self.stage.operands)
        fresh = [j for j in range(len(st.out_shape)) if j not in st.alias.values()]
        back = {v: k for k, v in st.alias.items()}
        return [self.bufs[back[j]] if j in back else self.bufs[n + fresh.index(j)] for j in range(len(st.out_shape))]


def _split_call(name, finish=(), start=(), after=None):
    bufs, stage_bufs = [], []

    def slot(a):
        for i, b in enumerate(bufs):
            if b is a:
                return i
        bufs.append(a)
        return len(bufs) - 1

    fin_slots = [[slot(b) for b in fl.bufs] for fl in finish]
    for st in start:
        fresh = [lax.empty(o.shape, o.dtype) for j, o in enumerate(st.out_shape) if j not in st.alias.values()]
        stage_bufs.append([slot(a) for a in list(st.operands) + fresh])
    old_sems = [s for fl in finish for s in fl.sems]
    new_sems = [s for st in start for s in st.sems]
    nb, no, nn = len(bufs), len(old_sems), len(new_sems)

    def refs_of(st, slots, buf_refs):
        n = len(st.operands)
        ins = [buf_refs[i] for i in slots[:n]]
        fresh = [j for j in range(len(st.out_shape)) if j not in st.alias.values()]
        back = {v: k for k, v in st.alias.items()}
        outs = [ins[back[j]] if j in back else buf_refs[slots[n + fresh.index(j)]] for j in range(len(st.out_shape))]
        return ins, outs

    def body(*refs):
        buf_refs, sem_in = refs[:nb], refs[nb:nb + no]
        sem_out = refs[nb + no + (after is not None):][:nn]
        token = refs[-1]
        pos = 0
        for fl, slots in zip(finish, fin_slots):
            ins, outs = refs_of(fl.stage, slots, buf_refs)
            fl.stage.finish(ins, outs, sem_in[pos:pos + len(fl.sems)])
            pos += len(fl.sems)
        pos = 0
        for st, slots in zip(start, stage_bufs):
            ins, outs = refs_of(st, slots, buf_refs)
            st.start(ins, outs, sem_out[pos:pos + len(st.sems)])
            pos += len(st.sems)
        token[...] = jnp.zeros_like(token)

    res = pl.pallas_call(
        body, name=name,
        out_shape=tuple(new_sems) + tuple(pltpu.HBM(b.shape, b.dtype) for b in bufs) +
                  (jax.ShapeDtypeStruct((8, LANE), F32),),
        in_specs=(HBM_REF,) * nb + (SEM_REF,) * no + ((pl.BlockSpec(memory_space=pl.ANY),) if after is not None else ()),
        out_specs=(SEM_REF,) * nn + (HBM_REF,) * nb + (pl.BlockSpec(memory_space=pltpu.VMEM),),
        input_output_aliases={i: nn + i for i in range(nb)},
        compiler_params=pltpu.CompilerParams(has_side_effects=DATAFLOW),
    )(*_hbm(*bufs), *old_sems, *([after] if after is not None else []))
    sems, thru, token = res[:nn], res[nn:nn + nb], res[-1]
    for fl, slots in zip(finish, fin_slots):
        fl.bufs = [thru[i] for i in slots]
    flights, pos = [], 0
    for st, slots in zip(start, stage_bufs):
        flights.append(_Flight(st, sems[pos:pos + len(st.sems)], [thru[i] for i in slots]))
        pos += len(st.sems)
    return flights, token


def _share(pairs):
    n = len(pairs)

    def start(ins, outs, sems):
        x, y, c, _ = _place()
        for i in range(n):
            _rcopy(outs[i].at[c], outs[i].at[c], sems[0].at[i], sems[1].at[i], (x, y, 1 - c)).start()

    def finish(ins, outs, sems):
        x, y, c, _ = _place()
        for i in range(n):
            _rcopy(outs[i].at[c], outs[i].at[c], sems[0].at[i], sems[1].at[i], (x, y, 1 - c)).wait_send()
            _rcopy(outs[i].at[1 - c], outs[i].at[1 - c], sems[0].at[i], sems[1].at[i], (x, y, 1 - c)).wait_recv()

    return _Stage(pairs, [_sds(p) for p in pairs], {i: i for i in range(n)},
                  [pltpu.SemaphoreType.DMA((n,)), pltpu.SemaphoreType.DMA((n,))], start, finish)


def _row_block(rows, cols, itemsize=4, target=2 * MIB):
    br = rows
    while br * cols * itemsize > target and br % 32 == 0:
        br //= 2
    return br


def _cast_place(w, chip_idx, name):
    rows, cols = w.shape
    br = _row_block(rows, cols)

    def body(k_ref, w_ref, o_ref):
        o_ref[0] = w_ref[...].astype(BF)

    return _call(
        body, name=name, grid=(rows // br,), prefetch=chip_idx,
        in_specs=[pl.BlockSpec((br, cols), lambda r, k: (r, 0))],
        out_specs=[pl.BlockSpec((1, br, cols), lambda r, k: (k[0], r, 0))],
        out_shape=[jax.ShapeDtypeStruct((NCHIP, rows, cols), BF)], vmem=32, args=[w])[0][0]


def _cast_place_multi(ws, chip_idx, stages=()):
    br = 128
    nblk = [a.shape[0] // br for a in ws]
    starts = [sum(nblk[:i]) for i in range(len(ws))]

    def body(k_ref, *refs):
        r = pl.program_id(0)
        for i in range(len(ws)):
            @pl.when(jnp.logical_and(r >= starts[i], r < starts[i] + nblk[i]))
            def _(i=i):
                refs[len(ws) + i][0] = refs[i][...].astype(BF)

    def at(i):
        return functools.partial(lambda r, s, nb: jnp.clip(r - s, 0, nb - 1), s=starts[i], nb=nblk[i])

    outs, landed = _call(
        body, name="cast_rest", grid=(sum(nblk),), prefetch=chip_idx,
        in_specs=[pl.BlockSpec((br, a.shape[1]), functools.partial(lambda r, k, f: (f(r), 0), f=at(i)))
                  for i, a in enumerate(ws)],
        out_specs=[pl.BlockSpec((1, br, a.shape[1]), functools.partial(lambda r, k, f: (k[0], f(r), 0), f=at(i)))
                   for i, a in enumerate(ws)],
        out_shape=[jax.ShapeDtypeStruct((NCHIP,) + a.shape, BF) for a in ws], vmem=32, args=list(ws), stages=stages)
    return outs, landed


def _add_sibling(g, land, cidx, name, stages=()):
    _, _, hr, cols = g.shape
    br = _row_block(hr, cols)

    def body(c_ref, g_ref, l_ref, o_ref):
        o_ref[...] = (g_ref[0, 0].astype(F32) + l_ref[0].astype(F32)).astype(BF)[None]

    outs, st = _call(
        body, name=name, grid=(NCHIP, hr // br), prefetch=cidx,
        in_specs=[pl.BlockSpec((1, 1, br, cols), lambda k, r, c: (k, c[0], r, 0)),
                  pl.BlockSpec((1, br, cols), lambda k, r, c: (k, r, 0))],
        out_specs=[pl.BlockSpec((1, br, cols), lambda k, r, c: (k, r, 0))],
        out_shape=[jax.ShapeDtypeStruct((NCHIP, hr, cols), BF)], vmem=32, args=[g, land], stages=stages)
    return outs[0], st


def _add_sibling_multi(gs, lands, cidx, name):
    n = len(gs)
    brs = [_row_block(g.shape[2], g.shape[3]) for g in gs]
    nrb = [g.shape[2] // b for g, b in zip(gs, brs)]
    nblk = [NCHIP * q for q in nrb]
    starts = [sum(nblk[:i]) for i in range(n)]

    def body(c_ref, *refs):
        r = pl.program_id(0)
        for i in range(n):
            g_ref, l_ref, o_ref = refs[2 * i], refs[2 * i + 1], refs[2 * n + i]

            @pl.when(jnp.logical_and(r >= starts[i], r < starts[i] + nblk[i]))
            def _():
                o_ref[...] = (g_ref[0, 0].astype(F32) + l_ref[0].astype(F32)).astype(BF)[None]

    def at(i, r):
        q = jnp.clip(r - starts[i], 0, nblk[i] - 1)
        return q // nrb[i], q % nrb[i]

    def g_spec(i):
        return pl.BlockSpec((1, 1, brs[i], gs[i].shape[3]),
                            functools.partial(lambda r, c, i: (at(i, r)[0], c[0], at(i, r)[1], 0), i=i))

    def l_spec(i):
        return pl.BlockSpec((1, brs[i], gs[i].shape[3]),
                            functools.partial(lambda r, c, i: (at(i, r)[0], at(i, r)[1], 0), i=i))

    return _call(
        body, name=name, grid=(sum(nblk),), prefetch=cidx,
        in_specs=[s for i in range(n) for s in (g_spec(i), l_spec(i))], out_specs=[l_spec(i) for i in range(n)],
        out_shape=[jax.ShapeDtypeStruct(l.shape, BF) for l in lands], vmem=32,
        args=[a for i in range(n) for a in (gs[i], lands[i])])[0]


def _add_pair(a, b, name):
    rows, cols = a.shape

    def body(a_ref, b_ref, o_ref):
        o_ref[...] = a_ref[...] + b_ref[...]

    spec = pl.BlockSpec((rows, cols), lambda r: (0, 0))
    return _call(body, name=name, grid=(1,), in_specs=[spec, spec], out_specs=[spec], out_shape=[_sds(a)],
                 vmem=32, args=[a, b])[0][0]


def _add_chips(own, land, idx, name, stages=None):
    _, hr, cols = land.shape
    br = _row_block(hr, cols)

    def body(s_ref, a_ref, b_ref, c_ref, d_ref, o_ref):
        o_ref[...] = (a_ref[...].astype(F32) + b_ref[...].astype(F32)) + (c_ref[...].astype(F32) +
                                                                           d_ref[...].astype(F32))

    spec = lambda q: pl.BlockSpec((1, br, cols), functools.partial(lambda r, s, q: (s[q], r, 0), q=q))
    outs, landed = _call(
        body, name=name, grid=(hr // br,), prefetch=idx,
        in_specs=[spec(0), spec(1), spec(2), spec(3)], out_specs=[spec(4)],
        out_shape=[jax.ShapeDtypeStruct((2, hr, cols), F32)], vmem=48, args=[own, land, land, land],
        stages=stages or ())
    return outs[0] if stages is None else (outs[0], landed)


def _add_chips_multi(owns, lands, idx, name, stages=()):
    n = len(owns)
    brs = [_row_block(l.shape[1], l.shape[2]) for l in lands]
    nblk = [l.shape[1] // b for l, b in zip(lands, brs)]
    starts = [sum(nblk[:i]) for i in range(n)]

    def body(s_ref, *refs):
        r = pl.program_id(0)
        for i in range(n):
            a_ref, b_ref, c_ref, d_ref = refs[4 * i:4 * i + 4]
            o_ref = refs[4 * n + i]

            @pl.when(jnp.logical_and(r >= starts[i], r < starts[i] + nblk[i]))
            def _():
                o_ref[...] = (a_ref[...].astype(F32) + b_ref[...].astype(F32)) + (c_ref[...].astype(F32) +
                                                                                   d_ref[...].astype(F32))

    def spec(i, q):
        return pl.BlockSpec((1, brs[i], lands[i].shape[2]), functools.partial(
            lambda r, s, q, st, nb: (s[q], jnp.clip(r - st, 0, nb - 1), 0), q=q, st=starts[i], nb=nblk[i]))

    outs, landed = _call(
        body, name=name, grid=(sum(nblk),), prefetch=idx,
        in_specs=[spec(i, q) for i in range(n) for q in range(4)], out_specs=[spec(i, 4) for i in range(n)],
        out_shape=[jax.ShapeDtypeStruct((2,) + l.shape[1:], F32) for l in lands], vmem=48,
        args=[a for i in range(n) for a in (owns[i], lands[i], lands[i], lands[i])], stages=stages)
    return outs, landed


def _adamw_math(w, g, m, v):
    mn = ADAM_B1 * m + (1.0 - ADAM_B1) * g
    vn = ADAM_B2 * v + (1.0 - ADAM_B2) * (g * g)
    m_hat = mn / (1.0 - ADAM_B1 ** ADAM_STEP)
    v_hat = vn / (1.0 - ADAM_B2 ** ADAM_STEP)
    return -ADAM_LR * (m_hat / (jnp.sqrt(v_hat) + ADAM_EPS) + ADAM_WD * w), mn, vn


def _adamw(w, g, m, v, name, stages=()):
    rows, cols = w.shape
    br = _row_block(rows, cols)

    def body(w_ref, g_ref, m_ref, v_ref, go_ref, d_ref, mo_ref, vo_ref):
        gv = g_ref[...]
        go_ref[...] = gv
        d_ref[...], mo_ref[...], vo_ref[...] = _adamw_math(w_ref[...], gv, m_ref[...], v_ref[...])

    spec = pl.BlockSpec((br, cols), lambda r: (r, 0))
    return _call(body, name=name, grid=(rows // br,), in_specs=[spec] * 4, out_specs=[spec] * 4,
                 out_shape=[_sds(w)] * 4, vmem=56, args=[w, g, m, v], stages=stages)


def _adamw_multi(names, w, g, m, v, stages=()):
    cols = w[names[0]].shape[1]
    br = 128
    nblk = [w[n].shape[0] // br for n in names]
    starts = [sum(nblk[:i]) for i in range(len(names))]

    def body(*refs):
        r = pl.program_id(0)
        for i in range(len(names)):
            w_ref, g_ref, m_ref, v_ref = refs[4 * i:4 * i + 4]
            go_ref, d_ref, mo_ref, vo_ref = refs[4 * len(names) + 4 * i:4 * len(names) + 4 * i + 4]

            @pl.when(jnp.logical_and(r >= starts[i], r < starts[i] + nblk[i]))
            def _():
                gv = g_ref[...]
                go_ref[...] = gv
                d_ref[...], mo_ref[...], vo_ref[...] = _adamw_math(w_ref[...], gv, m_ref[...], v_ref[...])

    def spec(i):
        return pl.BlockSpec((br, cols), functools.partial(
            lambda r, s, nb: (jnp.clip(r - s, 0, nb - 1), 0), s=starts[i], nb=nblk[i]))

    outs, landed = _call(
        body, name="adamw_" + "_".join(names), grid=(sum(nblk),),
        in_specs=[spec(i) for i in range(len(names)) for _ in range(4)],
        out_specs=[spec(i) for i in range(len(names)) for _ in range(4)],
        out_shape=[_sds(w[n]) for n in names for _ in range(4)], vmem=56,
        args=[a[n] for n in names for a in (w, g, m, v)], stages=stages)
    return {n: outs[4 * i:4 * i + 4] for i, n in enumerate(names)}, landed


def _to_everyone(v):
    deltas = [(a, b, e) for a in (0, 1) for b in (0, 1) for e in (0, 1)][1:]

    def copies(ins, outs, sems):
        x, y, c, _ = _place()
        me = 4 * x + 2 * y + c
        flip = lambda p, f: 1 - p if f else p
        return [_rcopy(ins[0], outs[0].at[me], sems[0].at[q], sems[1].at[q], (flip(x, a), flip(y, b), flip(c, e)))
                for q, (a, b, e) in enumerate(deltas)]

    def start(ins, outs, sems):
        for cp in copies(ins, outs, sems):
            cp.start()

    def finish(ins, outs, sems):
        for cp in copies(ins, outs, sems):
            cp.wait()

    n = len(deltas)
    return _Stage([v], [jax.ShapeDtypeStruct((2 * NCHIP,) + v.shape, v.dtype)], {},
                  [pltpu.SemaphoreType.DMA((n,)), pltpu.SemaphoreType.DMA((n,))], start, finish)


SMALL_AT = {"norm_mix_pre": (0, 1, D), "norm_mix_post": (1, 1, D), "norm_mlp_pre": (2, 1, D),
            "norm_mlp_post": (3, 1, D), "b_gate": (4, 2, D), "conv_b": (6, 1, D), "lru_b_a": (7, 1, D),
            "lru_b_x": (8, 1, D), "lru_lambda": (9, 1, D), "pool_scale": (10, 1, DP)}
SMALL_SEPARATE = ["conv_w", "lru_w_a", "lru_w_x", "pool_w"]


def _adamw_small(small_sum, first_all, sep_grads, w, m, v):
    packed, sep = list(SMALL_AT), list(SMALL_SEPARATE)
    names = packed + sep

    def body(*refs):
        s_ref, a_ref, refs = refs[0], refs[1], refs[2:]
        g_sep, refs = refs[:len(sep)], refs[len(sep):]
        nn = len(names)
        w_r, m_r, v_r, refs = refs[:nn], refs[nn:2 * nn], refs[2 * nn:3 * nn], refs[3 * nn:]
        g_out, refs = refs[:len(packed)], refs[len(packed):]
        d_o, m_o, v_o = refs[:nn], refs[nn:2 * nn], refs[2 * nn:3 * nn]
        for i, n in enumerate(names):
            if i == 0:
                g = a_ref[0:1, :]
                for q in range(1, 2 * NCHIP):
                    g = g + a_ref[q:q + 1, :]
                g_out[i][...] = g
            elif n in SMALL_AT:
                r0, nr, nc = SMALL_AT[n]
                g = jnp.concatenate([s_ref[r0 + q:r0 + q + 1, :nc] for q in range(nr)], axis=1)
                g_out[i][...] = g
            else:
                g = g_sep[i - len(packed)][...]
            d_o[i][...], m_o[i][...], v_o[i][...] = _adamw_math(w_r[i][...], g, m_r[i][...], v_r[i][...])

    ws = [w[n] for n in names]
    res = pl.pallas_call(
        body, name="adamw_small",
        out_shape=[_sds(w[n]) for n in packed] + [_sds(a) for a in ws] * 3,
        compiler_params=_cp(32),
    )(*_hbm(small_sum, first_all, *sep_grads, *ws, *[m[n] for n in names], *[v[n] for n in names]))
    nn, npk = len(names), len(packed)
    grad = dict(zip(packed, res[:npk]))
    delta = dict(zip(names, res[npk:npk + nn]))
    new_m = dict(zip(names, res[npk + nn:npk + 2 * nn]))
    new_v = dict(zip(names, res[npk + 2 * nn:]))
    return grad, delta, new_m, new_v


W_NAMES = ["norm_mix_pre", "norm_mix_post", "norm_mlp_pre", "norm_mlp_post", "w_in", "b_gate", "conv_w", "conv_b",
           "lru_w_a", "lru_b_a", "lru_w_x", "lru_b_x", "lru_lambda", "pool_w", "pool_scale", "w_lru_up",
           "w_pool_up", "w_o", "w_ff1", "w_ff2"]
BIG = ["w_in", "w_lru_up", "w_pool_up", "w_o", "w_ff1", "w_ff2"]


def _block_diag(w):
    hd = w.shape[-1]
    per = CB // hd
    w4 = w.reshape(NG, per, hd, hd)
    eye = jnp.eye(per, dtype=w.dtype)
    return jnp.einsum("gpij,pq->gpiqj", w4, eye).reshape(NG, CB, CB)


def _block_diag_extract(d, hd):
    per = CB // hd
    d5 = d.reshape(NG, per, hd, per, hd)
    return jnp.stack([d5[:, p, :, p, :] for p in range(per)], axis=1).reshape(NG * per, hd, hd)


def _halves(g):
    return g.reshape(NCHIP, 2, g.size // (g.shape[-1] * 2 * NCHIP), g.shape[-1])


def kernel(x, norm_mix_pre, norm_mix_post, norm_mlp_pre, norm_mlp_post, w_in, b_gate, conv_w, conv_b, lru_w_a, lru_b_a, lru_w_x, lru_b_x, lru_lambda, pool_w, pool_scale, w_lru_up, w_pool_up, w_o, w_ff1, w_ff2, loss_target, m_norm_mix_pre, m_norm_mix_post, m_norm_mlp_pre, m_norm_mlp_post, m_w_in, m_b_gate, m_conv_w, m_conv_b, m_lru_w_a, m_lru_b_a, m_lru_w_x, m_lru_b_x, m_lru_lambda, m_pool_w, m_pool_scale, m_w_lru_up, m_w_pool_up, m_w_o, m_w_ff1, m_w_ff2, v_norm_mix_pre, v_norm_mix_post, v_norm_mlp_pre, v_norm_mlp_post, v_w_in, v_b_gate, v_conv_w, v_conv_b, v_lru_w_a, v_lru_b_a, v_lru_w_x, v_lru_b_x, v_lru_lambda, v_pool_w, v_pool_scale, v_w_lru_up, v_w_pool_up, v_w_o, v_w_ff1, v_w_ff2):
    args = dict(locals())
    two_d = lambda a: a.reshape(-1, a.shape[-1])
    w = {n: two_d(args[n]) for n in W_NAMES}
    mom = {n: two_d(args["m_" + n]) for n in W_NAMES}
    var = {n: two_d(args["v_" + n]) for n in W_NAMES}
    i32 = lambda val: jnp.asarray(val, jnp.int32)
    chip = i32(2 * lax.axis_index("x") + lax.axis_index("y"))
    core = i32(lax.axis_index("c"))
    cidx = core.reshape(1)
    zero = i32(0)
    hd = lru_w_a.shape[-1]
    xs, target = x[0], loss_target[0]
    g1, g2, g3, g4 = norm_mix_pre, norm_mix_post, norm_mlp_pre, norm_mlp_post

    mix = ["w_lru_up", "w_pool_up", "w_o"]
    full = {"w_in": _cast_place(w["w_in"], chip.reshape(1), "cast_w_in")}
    (fl_in, fl_conv), first = _split_call("gather_start_first", start=[
        _gather([full["w_in"]], ici=[(0, ALL)]), _gather_whole(w["conv_w"])])
    casts, _ = _cast_place_multi([w[n] for n in BIG[1:]], chip.reshape(1), stages=[_after(first)])
    full.update(zip(BIG[1:], casts))
    (fl_mix, fl_ff1, fl_ff2), started = _split_call("gather_start_rest", start=[
        _gather([full[n] for n in mix], ici=[(0, ALL), (1, ALL), (2, ALL)]),
        _gather([full["w_ff1"]], ici=[(0, ALL)]), _gather([full["w_ff2"]], ici=[(0, ALL)])])
    wa = _block_diag(lru_w_a[0]).astype(BF)
    wx = _block_diag(lru_w_x[0]).astype(BF)
    pw = pool_w[0].astype(BF)

    def to_sibling(name, flight, after=None):
        (fl,), passed = _split_call(name + "_pass", finish=[flight], after=after,
                                    start=[_gather(flight.landed(), d2d=[(i, ALL) for i in range(len(flight.bufs))])])
        passed_on.append(passed)
        return fl

    passed_on = []

    def arrived(name, flight, after=None):
        _split_call(name + "_done", finish=[flight], after=after)
        return flight.landed()

    idx_big = jnp.stack([chip, (chip + 1) % NCHIP, (chip + 2) % NCHIP, (chip + 3) % NCHIP, core])
    proj, h1 = _fwd_inproj_own(xs, g1, fl_in.bufs[0], idx_big, stages=[_after(started)])
    fl_in = to_sibling("gather_w_in", fl_in, after=h1)
    _split_call("gather_w_in_done", finish=[fl_in, fl_conv])
    (w_in_f,), (conv_all,) = fl_in.landed(), fl_conv.landed()
    full["w_in"] = w_in_f
    conv_all = lax.dynamic_update_slice(conv_all, w["conv_w"][None], (chip, zero, zero))
    conv_full = jnp.transpose(conv_all, (1, 0, 2)).reshape(4, DR)
    proj = _fwd_inproj_rest(h1, w_in_f, proj, idx_big)
    fl_mix = to_sibling("gather_mix", fl_mix, after=proj)
    (ylru, hs), _ = _fwd_lru(proj, conv_full, conv_b, wa, lru_b_a, wx, lru_b_x, lru_lambda,
                             stages=[_after(passed_on[-1])])
    got = arrived("gather_mix", fl_mix, after=ylru)
    fl_ff1 = to_sibling("gather_ff1", fl_ff1, after=ylru)
    w_lru_up_f, w_pool_up_f, w_o_f = got[0].reshape(DR, D), got[1], got[2].reshape(D, D)
    ypool = _fwd_pool(proj, pw, pool_scale)
    (x2, h2, m, mrg, bra, brb), _ = _fwd_merge(xs, ylru, ypool, proj, b_gate, g2, g3, w_lru_up_f, w_pool_up_f, w_o_f,
                                               stages=[_after(passed_on[-1])])
    fl_ff2 = to_sibling("gather_ff2", fl_ff2, after=h2)
    _split_call("gather_ff_done", finish=[fl_ff1, fl_ff2])
    (ff1,), (ff2,) = fl_ff1.landed(), fl_ff2.landed()
    ff2 = ff2.reshape(DF, D)
    a1, lossp, dy, df, dg4 = _fwd_mlp_loss(h2, ff1, ff2, x2, target, g4)

    dh2, df1 = _bwd_mlp_x(df, a1, ff1, ff2)
    dw_ff1, dw_ff2 = _bwd_mlp_w(df, h2, a1, df1)
    g_ff = [_halves(dw_ff1), _halves(dw_ff2)]
    (dxres, dgates, dylru, dypool, dm, dbra, dbrb, dg2, dg3, dbg), (l_ff,) = _bwd_merge(
        dh2, dy, x2, m, bra, brb, proj, b_gate, g2, g3, w_lru_up_f, w_pool_up_f, w_o_f, stages=[_to_sibling(g_ff)])
    p_ff = _add_sibling_multi(g_ff, l_ff, cidx, "add_sibling_ff")
    (fl_ff,), sent_ff = _split_call("reduce_ff_start", start=[_to_chips(p_ff)])
    (dw_o, dw_lru_up, dw_pool_up), _ = _dw_merge(mrg, dm, ylru, dbra, ypool, dbrb, stages=[_after(sent_ff)])
    g_mix = [_halves(dw_lru_up), _halves(dw_pool_up), _halves(dw_o)]
    (dxp, dgl, dcw, dcb, dwa, dba, dwx, dbx, dlam), (l_mix,) = _bwd_lru(
        proj, hs, dylru, conv_full, conv_b, wa, lru_b_a, wx, lru_b_x, lru_lambda, stages=[_to_sibling(g_mix)])
    p_mix = _add_sibling_multi(g_mix, l_mix, cidx, "add_sibling_mix")
    dxpool, dpw, dsc = _bwd_pool(proj, dypool, pw, pool_scale)
    dproj = [dxp, dgl, dxpool, dgates]
    small = jnp.concatenate([
        jnp.zeros((1, D), F32), dg2, dg3, dg4, dbg.reshape(2, D), dcb, dba, dbx, dlam,
        jnp.pad(dsc, ((0, 0), (0, D - DP))), jnp.pad(lossp, ((0, 0), (0, D - 1))), dcw,
        _block_diag_extract(dwa, hd).reshape(-1, D), _block_diag_extract(dwx, hd).reshape(-1, D),
        dpw.reshape(-1, D)], axis=0)
    (fl_mixr, fl_smalls), sent_mix = _split_call("reduce_mix_start", start=[_to_chips(p_mix), _to_sibling([small])])
    dw_in = _bwd_inproj_w(h1, dproj, sent_mix)
    _split_call("reduce_small_sibling_done", finish=[fl_smalls], after=dw_in)
    small, l_small = fl_smalls.bufs
    small2 = _add_pair(small, l_small, "add_sibling_small").reshape(2, SMALL_ROWS // 2, D)
    g_in = _halves(dw_in)
    done = ["w_ff1", "w_ff2"] + mix
    (fl_gin, fl_small), sib_started = _split_call("reduce_in_sibling_start", finish=[fl_ff, fl_mixr],
                                                  start=[_to_sibling([g_in]), _to_chips([small2])])
    p_ff1, p_ff2, c_ff1, c_ff2 = fl_ff.bufs
    p_mix, c_mix = fl_mixr.bufs[:3], fl_mixr.bufs[3:]
    pairs, _ = _add_chips_multi([p_ff1, p_ff2] + p_mix, [c_ff1, c_ff2] + c_mix, idx_big, "add_chips_done",
                                stages=[_after(sib_started)])
    _split_call("reduce_in_sibling_done", finish=[fl_gin], after=pairs[-1])
    g_in, l_in = fl_gin.bufs
    p_in = _add_sibling(g_in, l_in, cidx, "add_sibling_w_in")[0]
    (fl_pin,), token = _split_call("reduce_last_start", start=[_to_chips([p_in])])
    _split_call("reduce_small_done", finish=[fl_small], after=token)
    small2, c_small = fl_small.bufs
    own_small = lax.dynamic_index_in_dim(small2, core, 0, keepdims=True)
    c_small = lax.dynamic_update_slice(c_small, own_small, (chip, zero, zero))
    pair_small = _add_chips(c_small, c_small, jnp.stack([zero, zero + 1, zero + 2, zero + 3, core]), "add_chips_small")
    (fl_share,), shared_start = _split_call("reduce_share_start", start=[_share(pairs + [pair_small])])
    grad_x, dg1 = _bwd_inproj_x(dproj, full["w_in"], xs, dxres, g1, stages=[_after(shared_start)])
    _split_call("reduce_share_done", finish=[fl_share, fl_pin], after=dg1)
    shared, (p_in, c_in) = fl_share.landed(), fl_pin.bufs
    pairs, pair_small = shared[:-1], shared[-1]

    grads, delta, new_m, new_v = {}, {}, {}, {}
    for n, p in zip(done, pairs):
        grads[n] = p.reshape(-1, p.shape[-1])

    def update(n, stages=()):
        (grads[n], delta[n], new_m[n], new_v[n]), landed = _adamw(w[n], grads[n], mom[n], var[n], "adamw_" + n,
                                                                  stages=stages)
        return landed

    pair_in = _add_chips(p_in, c_in, idx_big, "add_chips_w_in")
    (fl_last, fl_dg1), last_start = _split_call("reduce_last_share_start", start=[_share([pair_in]), _to_everyone(dg1)])
    updated, _ = _adamw_multi(["w_ff1", "w_ff2", "w_o", "w_lru_up"], w, grads, mom, var, stages=[_after(last_start)])
    for n, (go, d, mo, vo) in updated.items():
        grads[n], delta[n], new_m[n], new_v[n] = go, d, mo, vo
    _split_call("reduce_last_share_done", finish=[fl_last, fl_dg1], after=new_v["w_lru_up"])
    (pair_in,), (dg1, dg1_all) = fl_last.landed(), fl_dg1.bufs
    dg1_all = lax.dynamic_update_slice(dg1_all, dg1[None], (2 * chip + core, zero, zero)).reshape(2 * NCHIP, D)
    grads["w_in"] = pair_in.reshape(-1, pair_in.shape[-1])
    update("w_pool_up")
    update("w_in")
    small_sum = pair_small.reshape(SMALL_ROWS, D)
    loss = 0.5 * small_sum[LOSS_ROW, 0]
    ccols = DR // NCHIP
    sep = [lax.dynamic_slice(small_sum[12:16], (zero, chip * ccols), (4, ccols)),
           small_sum[16:80].reshape(-1, hd), small_sum[80:144].reshape(-1, hd), small_sum[144:208].reshape(-1, PG)]
    g_s, d_s, m_s, v_s = _adamw_small(small_sum, dg1_all, sep, w, mom, var)
    grads.update(g_s)
    grads.update(dict(zip(SMALL_SEPARATE, sep)))
    delta.update(d_s)
    new_m.update(m_s)
    new_v.update(v_s)

    out = lambda d: [d[n].reshape(args[n].shape) for n in W_NAMES]
    return (loss, grad_x[None], *out(grads), *out(delta), *out(new_m), *out(new_v))
```

```python
import functools
import math

import jax
import jax.numpy as jnp
from jax import lax
from jax.experimental import pallas as pl
from jax.experimental.pallas import tpu as pltpu

F32 = jnp.float32
BF = jnp.bfloat16

T = 2048
D = 1024
DR = 1024
DP = 512
DF = 4096
DIN = 4608
NCHIP = 4
CW_IN = DIN // NCHIP
LANE = 128
CB = 128
NG = DR // CB
PG = 128
POOL_WINDOWS = (2, 4, 8, 16)
NORM_EPS = 1e-6
LRU_C = 8.0
GELU_C = math.sqrt(2.0 / math.pi)
ADAM_LR = 0.001
ADAM_B1 = 0.9
ADAM_B2 = 0.999
ADAM_EPS = 1e-08
ADAM_WD = 0.01
ADAM_STEP = 10
MESH_ID = pl.DeviceIdType.MESH
ANY = pl.BlockSpec(memory_space=pl.ANY)
SMALL_ROWS = 208
LOSS_ROW = 11
MIB = 1 << 20


def _cp(vmem_mib=None):
    if vmem_mib is None:
        return pltpu.CompilerParams()
    return pltpu.CompilerParams(vmem_limit_bytes=vmem_mib * MIB)


def _hbm(*arrays):
    return [pltpu.with_memory_space_constraint(a, pltpu.HBM) for a in arrays]


def _hbm_out(shapes):
    return [pltpu.HBM(s.shape, s.dtype) for s in shapes]


class _Stage:
    def __init__(self, operands, out_shape, alias, sems, start, finish):
        self.operands, self.out_shape, self.alias, self.sems = list(operands), list(out_shape), dict(alias), list(sems)
        self.start, self.finish = start, finish


def _call(body, *, name, grid, in_specs, out_specs, out_shape, args, vmem=None, stages=(), prefetch=None,
          scratch=()):
    nin, nout = len(in_specs), len(out_specs)
    npre = 0 if prefetch is None else 1
    st_args, st_shapes, st_sems, aliases = [], [], list(scratch), {}
    for st in stages:
        for k, v in st.alias.items():
            aliases[npre + nin + len(st_args) + k] = nout + len(st_shapes) + v
        st_args += st.operands
        st_shapes += st.out_shape
        st_sems += st.sems

    def wrapped(*refs):
        pre, refs = refs[:npre], refs[npre:]
        ins, pos = refs[:nin], nin
        st_ins = []
        for st in stages:
            st_ins.append(refs[pos:pos + len(st.operands)])
            pos += len(st.operands)
        outs, pos = refs[pos:pos + nout], pos + nout
        st_outs = []
        for st in stages:
            st_outs.append(refs[pos:pos + len(st.out_shape)])
            pos += len(st.out_shape)
        work, pos = refs[pos:pos + len(scratch)], pos + len(scratch)
        sems = []
        for st in stages:
            sems.append(refs[pos:pos + len(st.sems)])
            pos += len(st.sems)
        if stages:
            first = functools.reduce(jnp.logical_and, [pl.program_id(a) == 0 for a in range(len(grid))])

            @pl.when(first)
            def _():
                for st, a, b, s in zip(stages, st_ins, st_outs, sems):
                    st.start(a, b, s)

        body(*pre, *ins, *outs, *work)
        if stages:
            last = functools.reduce(jnp.logical_and, [pl.program_id(a) == g - 1 for a, g in enumerate(grid)])

            @pl.when(last)
            def _():
                for st, a, b, s in zip(stages, st_ins, st_outs, sems):
                    st.finish(a, b, s)

    all_in = list(in_specs) + [ANY] * len(st_args)
    all_out = list(out_specs) + [ANY] * len(st_shapes)
    kw = dict(has_side_effects=True) if stages else {}
    if vmem is not None:
        kw["vmem_limit_bytes"] = vmem * MIB
    if prefetch is None:
        gkw = dict(grid=grid, in_specs=all_in, out_specs=all_out, scratch_shapes=st_sems)
    else:
        gkw = dict(grid_spec=pltpu.PrefetchScalarGridSpec(
            num_scalar_prefetch=1, grid=grid, in_specs=all_in, out_specs=all_out, scratch_shapes=st_sems))
    res = pl.pallas_call(
        wrapped, name=name, out_shape=_hbm_out(list(out_shape) + st_shapes), input_output_aliases=aliases,
        compiler_params=pltpu.CompilerParams(**kw), **gkw,
    )(*([prefetch] if npre else []), *_hbm(*args, *st_args))
    outs, rest, st_res = list(res[:nout]), list(res[nout:]), []
    for st in stages:
        st_res.append(rest[:len(st.out_shape)])
        rest = rest[len(st.out_shape):]
    return outs, st_res


def _mm(a, b):
    return jnp.dot(a.astype(BF), b.astype(BF), preferred_element_type=F32)


def _mm_nt(a, b):
    return lax.dot_general(a.astype(BF), b.astype(BF), (((1,), (1,)), ((), ())),
                           preferred_element_type=F32)


def _mm_tn(a, b):
    return lax.dot_general(a.astype(BF), b.astype(BF), (((0,), (0,)), ((), ())),
                           preferred_element_type=F32)


def _rows(v):
    return lax.broadcasted_iota(jnp.int32, v.shape, 0)


def _sd(v, s, fill=0.0):
    return jnp.where(_rows(v) >= s, pltpu.roll(v, s, axis=0), fill)


def _su(v, s, fill=0.0):
    n = v.shape[0]
    return jnp.where(_rows(v) < n - s, pltpu.roll(v, n - s, axis=0), fill)


def _sigmoid(z):
    return 1.0 / (1.0 + jnp.exp(-z))


def _softplus(z):
    e = jnp.exp(-jnp.abs(z))
    u = 1.0 + e
    d = u - 1.0
    log1p = jnp.where(d == 0.0, e, jnp.log(u) * (e / jnp.where(d == 0.0, 1.0, d)))
    return jnp.maximum(z, 0.0) + log1p


def _mean(v):
    return jnp.mean(v, axis=-1, keepdims=True)


def _colsum(v):
    return jnp.sum(v, axis=0, keepdims=True)


def _acc(ref, val, first):
    @pl.when(first)
    def _():
        ref[...] = val

    @pl.when(jnp.logical_not(first))
    def _():
        ref[...] += val


def _conv(xp, cw, cb):
    x1, x2, x3 = _sd(xp, 1), _sd(xp, 2), _sd(xp, 3)
    xc = cb + cw[0:1] * x3 + cw[1:2] * x2 + cw[2:3] * x1 + cw[3:4] * xp
    return xc, x1, x2, x3


def _lru_gates(xc, wa, ba, wx, bx, lam):
    xcb = xc.astype(BF)
    r = _sigmoid(_mm(xcb, wa) + ba)
    ii = _sigmoid(_mm(xcb, wx) + bx)
    sp = _softplus(-lam)
    la = (-LRU_C) * r * sp
    a = jnp.exp(la)
    mult = jnp.sqrt(-jnp.tanh(la) * (a * a + 1.0))
    return xcb, r, ii, sp, a, mult


def _gelu_parts(g):
    th = jnp.tanh(GELU_C * (g + 0.044715 * (g * g * g)))
    gel = 0.5 * g * (1.0 + th)
    dgel = 0.5 * (1.0 + th) + 0.5 * g * (1.0 - th * th) * (GELU_C * (1.0 + 3.0 * 0.044715 * (g * g)))
    return gel, dgel


def _tile_scan(a, b, a_s, b_s, out_ref, reverse):
    n, lanes = a.shape
    nt = n // 8
    a, b = a.reshape(nt, 8, lanes), b.reshape(nt, 8, lanes)
    sub = lax.broadcasted_iota(jnp.int32, a.shape, 1)
    s = 1
    while s < 8:
        keep = sub < 8 - s if reverse else sub >= s
        amount = 8 - s if reverse else s
        b = b + a * jnp.where(keep, pltpu.roll(b, amount, axis=1), 0.0)
        a = a * jnp.where(keep, pltpu.roll(a, amount, axis=1), 1.0)
        s *= 2
    a_s[...] = a.reshape(n, lanes)
    b_s[...] = b.reshape(n, lanes)
    edge = pl.ds(0 if reverse else 7, nt, stride=8)
    ta, tb = a_s[edge, :], b_s[edge, :]
    shift = _su if reverse else _sd
    s = 1
    while s < nt:
        tb = tb + ta * shift(tb, s, 0.0)
        if 2 * s < nt:
            ta = ta * shift(ta, s, 1.0)
        s *= 2
    enters = shift(tb, 1, 0.0)
    for o in range(8):
        rows = pl.ds(o, nt, stride=8)
        out_ref[rows, :] = b_s[rows, :] + a_s[rows, :] * enters


def _pool_window(x, steps, shift):
    s, sh = x, 1
    for _ in range(steps):
        s = s + shift(s, sh)
        sh *= 2
    return s


def _fwd_inproj_own(x, g1, w_in, slots, stages=()):
    tm = 1024

    def body(s_ref, x_ref, g_ref, w_ref, proj_ref, h_ref):
        xv = x_ref[...]
        r = lax.rsqrt(_mean(xv * xv) + NORM_EPS)
        h = ((xv * r) * g_ref[...]).astype(BF)
        h_ref[...] = h
        proj_ref[...] = jnp.dot(h, w_ref[0], preferred_element_type=F32)

    return _call(
        body, name="fwd_inproj_own", grid=(T // tm,), prefetch=slots,
        in_specs=[pl.BlockSpec((tm, D), lambda i, s: (i, 0)),
                  pl.BlockSpec((1, D), lambda i, s: (0, 0)),
                  pl.BlockSpec((1, D, CW_IN), lambda i, s: (s[0], 0, 0))],
        out_specs=[pl.BlockSpec((tm, CW_IN), lambda i, s: (i, s[0])),
                   pl.BlockSpec((tm, D), lambda i, s: (i, 0))],
        out_shape=[jax.ShapeDtypeStruct((T, DIN), F32), jax.ShapeDtypeStruct((T, D), BF)],
        vmem=40, args=[x, g1, w_in], stages=stages)[0]


def _fwd_inproj_rest(h1, w_in, proj, slots):
    tm = 1024

    def body(s_ref, h_ref, w_ref, p_in, proj_ref):
        proj_ref[...] = jnp.dot(h_ref[...], w_ref[0], preferred_element_type=F32)

    res = pl.pallas_call(
        body, name="fwd_inproj_rest",
        grid_spec=pltpu.PrefetchScalarGridSpec(
            num_scalar_prefetch=1, grid=(NCHIP - 1, T // tm),
            in_specs=[pl.BlockSpec((tm, D), lambda k, i, s: (i, 0)),
                      pl.BlockSpec((1, D, CW_IN), lambda k, i, s: (s[1 + k], 0, 0)), ANY],
            out_specs=pl.BlockSpec((tm, CW_IN), lambda k, i, s: (i, s[1 + k]))),
        out_shape=pltpu.HBM((T, DIN), F32), input_output_aliases={3: 0},
        compiler_params=_cp(40),
    )(slots, *_hbm(h1, w_in, proj))
    return res


def _vec_spec():
    return pl.BlockSpec((1, CB), lambda j: (0, j))


def _fwd_lru(proj, conv_w, conv_b, wa, ba, wx, bx, lam, stages=()):
    def body(xp_ref, g_ref, cw_ref, cb_ref, wa_ref, ba_ref, wx_ref, bx_ref, lam_ref, y_ref, h_ref, a_s, b_s):
        xc, _, _, _ = _conv(xp_ref[...], cw_ref[...], cb_ref[...])
        _, _, ii, _, a, mult = _lru_gates(xc, wa_ref[0], ba_ref[...], wx_ref[0], bx_ref[...], lam_ref[...])
        _tile_scan(a, mult * (ii * xc), a_s, b_s, h_ref, reverse=False)
        gel, _ = _gelu_parts(g_ref[...])
        y_ref[...] = (h_ref[...] * gel).astype(BF)

    return _call(
        body, name="fwd_lru", grid=(NG,),
        in_specs=[pl.BlockSpec((T, CB), lambda j: (0, j)),
                  pl.BlockSpec((T, CB), lambda j: (0, NG + j)),
                  pl.BlockSpec((4, CB), lambda j: (0, j)),
                  _vec_spec(),
                  pl.BlockSpec((1, CB, CB), lambda j: (j, 0, 0)), _vec_spec(),
                  pl.BlockSpec((1, CB, CB), lambda j: (j, 0, 0)), _vec_spec(),
                  _vec_spec()],
        out_specs=[pl.BlockSpec((T, CB), lambda j: (0, j)), pl.BlockSpec((T, CB), lambda j: (0, j))],
        out_shape=[jax.ShapeDtypeStruct((T, DR), BF), jax.ShapeDtypeStruct((T, DR), F32)],
        vmem=48, args=[proj, proj, conv_w, conv_b, wa, ba, wx, bx, lam], stages=stages,
        scratch=[pltpu.VMEM((T, CB), F32)] * 2)


def _pool_cnt(w):
    t = lax.broadcasted_iota(jnp.int32, (T, 1), 0)
    return jnp.minimum(t + 1, w).astype(F32)


def _fwd_pool(proj, pool_w, pool_scale):
    def body(xp_ref, pw_ref, sc_ref, y_ref):
        for g, w in enumerate(POOL_WINDOWS):
            cols = slice(g * PG, (g + 1) * PG)
            x = xp_ref[:, cols]
            p = _pool_window(x, g + 1, _sd) / _pool_cnt(w) - x
            y_ref[:, cols] = (_mm(p, pw_ref[g]) * sc_ref[:, cols]).astype(BF)

    return pl.pallas_call(
        body, name="fwd_pool", grid=(1,),
        in_specs=[pl.BlockSpec((T, DP), lambda i: (0, 2 * DR // DP)),
                  pl.BlockSpec((4, PG, PG), lambda i: (0, 0, 0)),
                  pl.BlockSpec((1, DP), lambda i: (0, 0))],
        out_specs=pl.BlockSpec((T, DP), lambda i: (0, 0)),
        out_shape=pltpu.HBM((T, DP), BF),
        compiler_params=_cp(48),
    )(*_hbm(proj, pool_w, pool_scale))


GATE_BLK = 512
GATE_BLK0 = (2 * DR + DP) // GATE_BLK


def _gate_specs(tm):
    return [pl.BlockSpec((tm, GATE_BLK), functools.partial(lambda i, q: (i, GATE_BLK0 + q), q=q))
            for q in range(4)]


def _fwd_merge(x, ylru, ypool, proj, b_gate, g2, g3, w_lru_up, w_pool_up, w_o, stages=()):
    tm = 512

    def body(x_ref, yl_ref, yp_ref, p0, p1, p2, p3, bg_ref, g2_ref, g3_ref, wl_ref, wp_ref, wo_ref,
             x2_ref, h2_ref, m_ref, mrg_ref, bra_ref, brb_ref):
        bra = jnp.dot(yl_ref[...], wl_ref[...], preferred_element_type=F32)
        yp = yp_ref[...]
        brb = jnp.concatenate([jnp.dot(yp, wp_ref[k], preferred_element_type=F32) for k in range(NCHIP)], axis=1)
        bg = bg_ref[...]
        ga = _sigmoid(jnp.concatenate([p0[...], p1[...]], axis=1) + bg[:, :D])
        gb = _sigmoid(jnp.concatenate([p2[...], p3[...]], axis=1) + bg[:, D:])
        mrg = (ga * bra + gb * brb).astype(BF)
        m = jnp.dot(mrg, wo_ref[...], preferred_element_type=F32)
        r2 = lax.rsqrt(_mean(m * m) + NORM_EPS)
        x2 = x_ref[...] + (m * r2) * g2_ref[...]
        r3 = lax.rsqrt(_mean(x2 * x2) + NORM_EPS)
        x2_ref[...] = x2
        h2_ref[...] = ((x2 * r3) * g3_ref[...]).astype(BF)
        m_ref[...] = m
        mrg_ref[...] = mrg
        bra_ref[...] = bra.astype(BF)
        brb_ref[...] = brb.astype(BF)

    row = lambda w: pl.BlockSpec((tm, w), lambda i: (i, 0))
    full2 = lambda a, b: pl.BlockSpec((a, b), lambda i: (0, 0))
    return _call(
        body, name="fwd_merge", grid=(T // tm,),
        in_specs=[row(D), row(DR), row(DP)] + _gate_specs(tm) +
                 [full2(1, 2 * D), full2(1, D), full2(1, D), full2(DR, D),
                  pl.BlockSpec((NCHIP, DP, D // NCHIP), lambda i: (0, 0, 0)), full2(D, D)],
        out_specs=[row(D)] * 6,
        out_shape=[jax.ShapeDtypeStruct((T, D), F32), jax.ShapeDtypeStruct((T, D), BF),
                   jax.ShapeDtypeStruct((T, D), F32), jax.ShapeDtypeStruct((T, D), BF),
                   jax.ShapeDtypeStruct((T, D), BF), jax.ShapeDtypeStruct((T, D), BF)],
        vmem=48, args=[x, ylru, ypool, proj, proj, proj, proj, b_gate, g2, g3, w_lru_up, w_pool_up, w_o],
        stages=stages)


def _fwd_mlp_loss(h2, w_ff1, w_ff2, x2, target, g4):
    tm = 512
    fk = DF // NCHIP

    def body(h_ref, w1_ref, w2_ref, x2_ref, t_ref, g_ref, a1_ref, loss_ref, dy_ref, df_ref, dg_ref):
        first = pl.program_id(0) == 0
        h = h_ref[...]
        f = None
        for k in range(NCHIP):
            a1 = jnp.maximum(jnp.dot(h, w1_ref[k], preferred_element_type=F32), 0.0)
            a1_ref[:, k * fk:(k + 1) * fk] = a1.astype(BF)
            part = jnp.dot((a1 * a1).astype(BF), w2_ref[k * fk:(k + 1) * fk, :], preferred_element_type=F32)
            f = part if f is None else f + part
        g4v = g_ref[...]
        r4 = lax.rsqrt(_mean(f * f) + NORM_EPS)
        fn = f * r4
        e = (x2_ref[...] + fn * g4v) - t_ref[...]
        _acc(loss_ref, jnp.sum(_mean(e * e), axis=0, keepdims=True), first)
        dy = e * (1.0 / D)
        dy_ref[...] = dy
        _acc(dg_ref, _colsum(dy * fn), first)
        dfn = dy * g4v
        df_ref[...] = (r4 * (dfn - fn * _mean(dfn * fn))).astype(BF)

    row = pl.BlockSpec((tm, D), lambda i: (i, 0))
    return pl.pallas_call(
        body, name="fwd_mlp_loss", grid=(T // tm,),
        in_specs=[row, pl.BlockSpec((NCHIP, D, fk), lambda i: (0, 0, 0)), pl.BlockSpec((DF, D), lambda i: (0, 0)),
                  row, row, pl.BlockSpec((1, D), lambda i: (0, 0))],
        out_specs=[pl.BlockSpec((tm, DF), lambda i: (i, 0)), pl.BlockSpec((1, 1), lambda i: (0, 0)), row, row,
                   pl.BlockSpec((1, D), lambda i: (0, 0))],
        out_shape=_hbm_out([jax.ShapeDtypeStruct((T, DF), BF), jax.ShapeDtypeStruct((1, 1), F32),
                            jax.ShapeDtypeStruct((T, D), F32), jax.ShapeDtypeStruct((T, D), BF),
                            jax.ShapeDtypeStruct((1, D), F32)]),
        compiler_params=_cp(56),
    )(*_hbm(h2, w_ff1, w_ff2, x2, target, g4))


def _bwd_mlp_x(df, a1, w_ff1, w_ff2):
    tm = 512
    fk = DF // NCHIP

    def body(df_ref, a1_ref, w1_ref, w2_ref, dh_ref, df1_ref):
        df = df_ref[...]
        dh = None
        for k in range(NCHIP):
            cols = slice(k * fk, (k + 1) * fk)
            dact = _mm_nt(df, w2_ref[cols, :])
            df1 = (dact * (2.0 * a1_ref[:, cols].astype(F32))).astype(BF)
            df1_ref[:, cols] = df1
            part = _mm_nt(df1, w1_ref[k])
            dh = part if dh is None else dh + part
        dh_ref[...] = dh

    return pl.pallas_call(
        body, name="bwd_mlp_x", grid=(T // tm,),
        in_specs=[pl.BlockSpec((tm, D), lambda i: (i, 0)),
                  pl.BlockSpec((tm, DF), lambda i: (i, 0)),
                  pl.BlockSpec((NCHIP, D, fk), lambda i: (0, 0, 0)),
                  pl.BlockSpec((DF, D), lambda i: (0, 0))],
        out_specs=[pl.BlockSpec((tm, D), lambda i: (i, 0)), pl.BlockSpec((tm, DF), lambda i: (i, 0))],
        out_shape=_hbm_out([jax.ShapeDtypeStruct((T, D), F32), jax.ShapeDtypeStruct((T, DF), BF)]),
        compiler_params=_cp(56),
    )(*_hbm(df, a1, w_ff1, w_ff2))


def _bwd_mlp_w(df, h2, a1, df1):
    fc = 512
    per = (DF // NCHIP) // fc

    def body(df_ref, h_ref, a1_ref, df1_ref, dw1_ref, dw2_ref):
        a1 = a1_ref[...].astype(F32)
        dw2_ref[...] = _mm_tn((a1 * a1).astype(BF), df_ref[...]).astype(BF)
        dw1_ref[0] = _mm_tn(h_ref[...], df1_ref[...]).astype(BF)

    return pl.pallas_call(
        body, name="bwd_mlp_w", grid=(DF // fc,),
        in_specs=[pl.BlockSpec((T, D), lambda j: (0, 0)),
                  pl.BlockSpec((T, D), lambda j: (0, 0)),
                  pl.BlockSpec((T, fc), lambda j: (0, j)),
                  pl.BlockSpec((T, fc), lambda j: (0, j))],
        out_specs=[pl.BlockSpec((1, D, fc), lambda j: (j // per, 0, j % per)),
                   pl.BlockSpec((fc, D), lambda j: (j, 0))],
        out_shape=_hbm_out([jax.ShapeDtypeStruct((NCHIP, D, DF // NCHIP), BF),
                            jax.ShapeDtypeStruct((DF, D), BF)]),
        compiler_params=_cp(56),
    )(*_hbm(df, h2, a1, df1))


def _bwd_merge(dh2, dy, x2, m, bra, brb, proj, b_gate, g2, g3, w_lru_up, w_pool_up, w_o, stages=()):
    tm = 256
    cpu = D // NCHIP

    def body(dh2_ref, dy_ref, x2_ref, m_ref, bra_ref, brb_ref, p0, p1, p2, p3, bg_ref,
             g2_ref, g3_ref, wl_ref, wp_ref, wo_ref,
             dx_ref, dgt_ref, dyl_ref, dyp_ref, dm_ref, dbra_ref, dbrb_ref, dg2_ref, dg3_ref, dbg_ref):
        first = pl.program_id(0) == 0
        x2 = x2_ref[...]
        r3 = lax.rsqrt(_mean(x2 * x2) + NORM_EPS)
        x2n = x2 * r3
        dh2 = dh2_ref[...]
        t3 = dh2 * g3_ref[...]
        dx2 = dy_ref[...] + r3 * (t3 - x2n * _mean(t3 * x2n))
        dx_ref[...] = dx2
        _acc(dg3_ref, _colsum(dh2 * x2n), first)
        m = m_ref[...]
        r2 = lax.rsqrt(_mean(m * m) + NORM_EPS)
        mn = m * r2
        _acc(dg2_ref, _colsum(dx2 * mn), first)
        dmn = dx2 * g2_ref[...]
        dm = (r2 * (dmn - mn * _mean(dmn * mn))).astype(BF)
        dm_ref[...] = dm
        dmrg = _mm_nt(dm, wo_ref[...])
        bg = bg_ref[...]
        ga = _sigmoid(jnp.concatenate([p0[...], p1[...]], axis=1) + bg[:, :D])
        gb = _sigmoid(jnp.concatenate([p2[...], p3[...]], axis=1) + bg[:, D:])
        dga = dmrg * bra_ref[...].astype(F32) * (ga * (1.0 - ga))
        dgb = dmrg * brb_ref[...].astype(F32) * (gb * (1.0 - gb))
        dgt_ref[:, :D] = dga.astype(BF)
        dgt_ref[:, D:] = dgb.astype(BF)
        _acc(dbg_ref, jnp.concatenate([_colsum(dga), _colsum(dgb)], axis=1), first)
        dbra = (dmrg * ga).astype(BF)
        dbrb = (dmrg * gb).astype(BF)
        dbra_ref[...] = dbra
        dbrb_ref[...] = dbrb
        dyl_ref[...] = _mm_nt(dbra, wl_ref[...])
        dyp = None
        for k in range(NCHIP):
            part = _mm_nt(dbrb[:, k * cpu:(k + 1) * cpu], wp_ref[k])
            dyp = part if dyp is None else dyp + part
        dyp_ref[...] = dyp

    row = lambda w: pl.BlockSpec((tm, w), lambda i: (i, 0))
    full2 = lambda a, b: pl.BlockSpec((a, b), lambda i: (0, 0))
    wp_spec = pl.BlockSpec((NCHIP, DP, cpu), lambda i: (0, 0, 0))
    return _call(
        body, name="bwd_merge", grid=(T // tm,),
        in_specs=[row(D)] * 6 + _gate_specs(tm) +
                 [full2(1, 2 * D), full2(1, D), full2(1, D), full2(DR, D), wp_spec, full2(D, D)],
        out_specs=[row(D), row(2 * D), row(DR), row(DP), row(D), row(D), row(D),
                   full2(1, D), full2(1, D), full2(1, 2 * D)],
        out_shape=[jax.ShapeDtypeStruct((T, D), F32), jax.ShapeDtypeStruct((T, 2 * D), BF),
                   jax.ShapeDtypeStruct((T, DR), F32), jax.ShapeDtypeStruct((T, DP), F32),
                   jax.ShapeDtypeStruct((T, D), BF), jax.ShapeDtypeStruct((T, D), BF),
                   jax.ShapeDtypeStruct((T, D), BF),
                   jax.ShapeDtypeStruct((1, D), F32), jax.ShapeDtypeStruct((1, D), F32),
                   jax.ShapeDtypeStruct((1, 2 * D), F32)],
        vmem=56, args=[dh2, dy, x2, m, bra, brb, proj, proj, proj, proj, b_gate, g2, g3, w_lru_up, w_pool_up, w_o],
        stages=stages)


def _dw_merge(mrg, dm, ylru, dbra, ypool, dbrb, stages=()):
    nb = NCHIP
    rb, pb, cpu = D // nb, DP // nb, D // NCHIP

    def body(mrg_ref, dm_ref, yl_ref, dbra_ref, yp_ref, dbrb_ref, dwo_ref, dwl_ref, dwp_ref):
        dwo_ref[...] = _mm_tn(mrg_ref[...], dm_ref[...]).astype(BF)
        dwl_ref[...] = _mm_tn(yl_ref[...], dbra_ref[...]).astype(BF)
        dwp = _mm_tn(yp_ref[...], dbrb_ref[...]).astype(BF)
        for k in range(NCHIP):
            dwp_ref[k] = dwp[:, k * cpu:(k + 1) * cpu]

    cols = lambda w: pl.BlockSpec((T, w), lambda r: (0, r))
    whole = pl.BlockSpec((T, D), lambda r: (0, 0))
    return _call(
        body, name="dw_merge", grid=(nb,),
        in_specs=[cols(rb), whole, cols(rb), whole, cols(pb), whole],
        out_specs=[pl.BlockSpec((rb, D), lambda r: (r, 0)), pl.BlockSpec((rb, D), lambda r: (r, 0)),
                   pl.BlockSpec((NCHIP, pb, cpu), lambda r: (0, r, 0))],
        out_shape=[jax.ShapeDtypeStruct((D, D), BF), jax.ShapeDtypeStruct((DR, D), BF),
                   jax.ShapeDtypeStruct((NCHIP, DP, cpu), BF)],
        vmem=56, args=[mrg, dm, ylru, dbra, ypool, dbrb], stages=stages)


def _bwd_lru(proj, h, dylru, conv_w, conv_b, wa, ba, wx, bx, lam, stages=()):
    def body(xp_ref, g_ref, h_ref, dy_ref, cw_ref, cb_ref, wa_ref, ba_ref, wx_ref, bx_ref, lam_ref,
             dxp_ref, dg_ref, dcw_ref, dcb_ref, dwa_ref, dba_ref, dwx_ref, dbx_ref, dlam_ref, a_s, b_s, l_s):
        xp = xp_ref[...]
        cw = cw_ref[...]
        lam = lam_ref[...]
        xc, x1, x2, x3 = _conv(xp, cw, cb_ref[...])
        wa, wx = wa_ref[0], wx_ref[0]
        xcb, r, ii, sp, a, mult = _lru_gates(xc, wa, ba_ref[...], wx, bx_ref[...], lam)
        g = g_ref[...]
        gel, dgel = _gelu_parts(g)
        h = h_ref[...]
        dy = dy_ref[...]
        dg_ref[...] = (dy * h * dgel).astype(BF)
        _tile_scan(_su(a, 1, 0.0), dy * gel, a_s, b_s, l_s, reverse=True)
        b = l_s[...]
        da = b * _sd(h, 1, 0.0)
        dmult = b * (ii * xc)
        dii = b * (mult * xc)
        dxc = b * (mult * ii)
        dla = da * a - dmult * ((a * a) / mult)
        dr = dla * ((-LRU_C) * sp)
        dsp = _colsum(dla * ((-LRU_C) * r))
        dlam_ref[...] = -dsp / (1.0 + jnp.exp(lam))
        dzr = dr * (r * (1.0 - r))
        dzi = dii * (ii * (1.0 - ii))
        dzrb, dzib = dzr.astype(BF), dzi.astype(BF)
        dxc = dxc + _mm_nt(dzrb, wa) + _mm_nt(dzib, wx)
        dwa_ref[0] = _mm_tn(xcb, dzrb)
        dwx_ref[0] = _mm_tn(xcb, dzib)
        dba_ref[...] = _colsum(dzr)
        dbx_ref[...] = _colsum(dzi)
        dcb_ref[...] = _colsum(dxc)
        dcw_ref[...] = jnp.concatenate([_colsum(dxc * x3), _colsum(dxc * x2), _colsum(dxc * x1),
                                        _colsum(dxc * xp)], axis=0)
        dxp = cw[3:4] * dxc + cw[2:3] * _su(dxc, 1) + cw[1:2] * _su(dxc, 2) + cw[0:1] * _su(dxc, 3)
        dxp_ref[...] = dxp.astype(BF)

    blk = pl.BlockSpec((T, CB), lambda j: (0, j))
    wsp = pl.BlockSpec((1, CB, CB), lambda j: (j, 0, 0))
    return _call(
        body, name="bwd_lru", grid=(NG,),
        in_specs=[blk, pl.BlockSpec((T, CB), lambda j: (0, NG + j)), blk, blk,
                  pl.BlockSpec((4, CB), lambda j: (0, j)), _vec_spec(), wsp, _vec_spec(), wsp, _vec_spec(),
                  _vec_spec()],
        out_specs=[blk, blk, pl.BlockSpec((4, CB), lambda j: (0, j)), _vec_spec(), wsp, _vec_spec(), wsp,
                   _vec_spec(), _vec_spec()],
        out_shape=[jax.ShapeDtypeStruct((T, DR), BF), jax.ShapeDtypeStruct((T, DR), BF),
                   jax.ShapeDtypeStruct((4, DR), F32), jax.ShapeDtypeStruct((1, DR), F32),
                   jax.ShapeDtypeStruct((NG, CB, CB), F32), jax.ShapeDtypeStruct((1, DR), F32),
                   jax.ShapeDtypeStruct((NG, CB, CB), F32), jax.ShapeDtypeStruct((1, DR), F32),
                   jax.ShapeDtypeStruct((1, DR), F32)],
        vmem=56, args=[proj, proj, h, dylru, conv_w, conv_b, wa, ba, wx, bx, lam], stages=stages,
        scratch=[pltpu.VMEM((T, CB), F32)] * 3)


def _bwd_pool(proj, dypool, pool_w, pool_scale):
    def body(xp_ref, dy_ref, pw_ref, sc_ref, dx_ref, dw_ref, dsc_ref):
        for g, w in enumerate(POOL_WINDOWS):
            cols = slice(g * PG, (g + 1) * PG)
            cnt = _pool_cnt(w)
            x = xp_ref[:, cols]
            pb = (_pool_window(x, g + 1, _sd) / cnt - x).astype(BF)
            wg = pw_ref[g]
            dy = dy_ref[:, cols]
            dsc_ref[:, cols] = _colsum(dy * _mm(pb, wg))
            dyp = (dy * sc_ref[:, cols]).astype(BF)
            dw_ref[g] = _mm_tn(pb, dyp)
            dp = _mm_nt(dyp, wg)
            dx_ref[:, cols] = (_pool_window(dp / cnt, g + 1, _su) - dp).astype(BF)

    return pl.pallas_call(
        body, name="bwd_pool", grid=(1,),
        in_specs=[pl.BlockSpec((T, DP), lambda i: (0, 2 * DR // DP)),
                  pl.BlockSpec((T, DP), lambda i: (0, 0)),
                  pl.BlockSpec((4, PG, PG), lambda i: (0, 0, 0)),
                  pl.BlockSpec((1, DP), lambda i: (0, 0))],
        out_specs=[pl.BlockSpec((T, DP), lambda i: (0, 0)),
                   pl.BlockSpec((4, PG, PG), lambda i: (0, 0, 0)),
                   pl.BlockSpec((1, DP), lambda i: (0, 0))],
        out_shape=_hbm_out([jax.ShapeDtypeStruct((T, DP), BF), jax.ShapeDtypeStruct((4, PG, PG), F32),
                            jax.ShapeDtypeStruct((1, DP), F32)]),
        compiler_params=_cp(48),
    )(*_hbm(proj, dypool, pool_w, pool_scale))


PART_COLS = (DR, DR, DP, 2 * D)


def _shard_pieces():
    starts = [sum(PART_COLS[:p]) for p in range(len(PART_COLS))]
    shards = []
    for k in range(NCHIP):
        lo, hi = k * CW_IN, (k + 1) * CW_IN
        shards.append([(p, max(lo, s) - s, min(hi, s + wd) - s, max(lo, s) - lo)
                       for p, (s, wd) in enumerate(zip(starts, PART_COLS)) if max(lo, s) < min(hi, s + wd)])
    return shards


def _bwd_inproj_w(h1, parts, after):
    flat = [(k, *piece) for k, pieces in enumerate(_shard_pieces()) for piece in pieces]

    def body(h_hbm, p0, p1, p2, p3, after_ref, dw_hbm, h_v, dw_v, *rest):
        bufs, sem_in, sem_out = rest[:len(flat)], rest[len(flat)], rest[len(flat) + 1]
        part_refs = (p0, p1, p2, p3)
        loads = [pltpu.make_async_copy(h_hbm, h_v, sem_in.at[0])]
        for i, (k, p, a, b, c0) in enumerate(flat):
            loads.append(pltpu.make_async_copy(part_refs[p].at[:, pl.ds(a, b - a)], bufs[i], sem_in.at[1 + i]))
        for cp in loads:
            cp.start()
        loads[0].wait()
        stores = []
        for i, (k, p, a, b, c0) in enumerate(flat):
            loads[1 + i].wait()
            dw_v[k, :, c0:c0 + b - a] = _mm_tn(h_v[...], bufs[i][...]).astype(BF)
            if i + 1 == len(flat) or flat[i + 1][0] != k:
                stores.append(pltpu.make_async_copy(dw_v.at[k], dw_hbm.at[k], sem_out.at[k]))
                stores[-1].start()
        for cp in stores:
            cp.wait()

    scratch = [pltpu.VMEM((T, D), BF), pltpu.VMEM((NCHIP, D, CW_IN), BF)]
    scratch += [pltpu.VMEM((T, b - a), parts[p].dtype) for k, p, a, b, c0 in flat]
    scratch += [pltpu.SemaphoreType.DMA((1 + len(flat),)), pltpu.SemaphoreType.DMA((NCHIP,))]
    return pl.pallas_call(
        body, name="bwd_inproj_w", in_specs=[ANY] * 6, out_specs=ANY, scratch_shapes=scratch,
        out_shape=pltpu.HBM((NCHIP, D, CW_IN), BF), compiler_params=_cp(48),
    )(*_hbm(h1, *parts), after)


def _bwd_inproj_x(parts, w_in, x, dxres, g1, stages=()):
    tm = 512

    def body(p0, p1, p2, p3, w_ref, x_ref, dr_ref, g_ref, dx_ref, dg_ref):
        part_refs = (p0, p1, p2, p3)
        dh = None
        for k, pieces in enumerate(_shard_pieces()):
            for p, a, b, c0 in pieces:
                part = _mm_nt(part_refs[p][:, a:b], w_ref[k, :, c0:c0 + b - a])
                dh = part if dh is None else dh + part
        xv = x_ref[...]
        r = lax.rsqrt(_mean(xv * xv) + NORM_EPS)
        xn = xv * r
        t = dh * g_ref[...]
        dx_ref[...] = dr_ref[...] + r * (t - xn * _mean(t * xn))
        _acc(dg_ref, _colsum(dh * xn), pl.program_id(0) == 0)

    row = pl.BlockSpec((tm, D), lambda i: (i, 0))
    vec = pl.BlockSpec((1, D), lambda i: (0, 0))
    return _call(
        body, name="bwd_inproj_x", grid=(T // tm,),
        in_specs=[pl.BlockSpec((tm, wd), lambda i: (i, 0)) for wd in PART_COLS] +
                 [pl.BlockSpec((NCHIP, D, CW_IN), lambda i: (0, 0, 0)), row, row, vec],
        out_specs=[row, vec],
        out_shape=[jax.ShapeDtypeStruct((T, D), F32), jax.ShapeDtypeStruct((1, D), F32)],
        vmem=56, args=[*parts, w_in, x, dxres, g1], stages=stages)[0]


def _place():
    x, y, c = lax.axis_index("x"), lax.axis_index("y"), lax.axis_index("c")
    chips = [(1 - x, y), (x, 1 - y), (1 - x, 1 - y)]
    return x, y, c, chips


def _rcopy(src, dst, ssem, rsem, dev):
    return pltpu.make_async_remote_copy(src_ref=src, dst_ref=dst, send_sem=ssem, recv_sem=rsem,
                                        device_id=dev, device_id_type=MESH_ID)


def _sds(a):
    return jax.ShapeDtypeStruct(a.shape, a.dtype)


def _sem2(n, m):
    return [pltpu.SemaphoreType.DMA((n * m,)), pltpu.SemaphoreType.DMA((n * m,))]


ALL = (0, 1, 1)


def _piece(ref, k, half, part):
    hr = ref.shape[1] // 2
    r0, r1 = hr * part[0] // part[2], hr * part[1] // part[2]
    return ref.at[k, pl.ds(half * hr + r0, r1 - r0), :]


def _gather(fulls, ici=(), d2d=()):
    n = len(fulls)
    ici, d2d = list(ici), list(d2d)
    pieces = [("ici", i, part) for i, part in ici] + [("d2d", i, part) for i, part in d2d]

    def copies(outs, sems):
        x, y, c, chips = _place()
        me = 2 * x + y
        sib = (x, y, 1 - c)
        send, recv = [], []
        for q, (kind, i, part) in enumerate(pieces):
            for j, chip in enumerate(chips):
                k, s = 2 * chip[0] + chip[1], 3 * q + j
                if kind == "ici":
                    mine, theirs, dev = _piece(outs[i], me, c, part), _piece(outs[i], k, c, part), (*chip, c)
                else:
                    mine, theirs, dev = _piece(outs[i], k, c, part), _piece(outs[i], k, 1 - c, part), sib
                send.append(_rcopy(mine, mine, sems[0].at[s], sems[1].at[s], dev))
                recv.append(_rcopy(theirs, theirs, sems[0].at[s], sems[1].at[s], dev))
        return send, recv

    def start(ins, outs, sems):
        for cp in copies(outs, sems)[0]:
            cp.start()

    def finish(ins, outs, sems):
        send, recv = copies(outs, sems)
        for cp in recv:
            cp.wait_recv()
        for cp in send:
            cp.wait_send()

    sems = [pltpu.SemaphoreType.DMA((3 * len(pieces),)), pltpu.SemaphoreType.DMA((3 * len(pieces),))]
    return _Stage(fulls, [_sds(f) for f in fulls], {i: i for i in range(n)}, sems, start, finish)


def _gather_whole(v):
    def copies(ins, outs, sems):
        x, y, c, chips = _place()
        me = 2 * x + y
        send = [_rcopy(ins[0], outs[0].at[me], sems[0].at[j], sems[1].at[j], (*chip, c))
                for j, chip in enumerate(chips)]
        recv = [_rcopy(ins[0], outs[0].at[2 * chip[0] + chip[1]], sems[0].at[j], sems[1].at[j], (*chip, c))
                for j, chip in enumerate(chips)]
        return send, recv

    def start(ins, outs, sems):
        for cp in copies(ins, outs, sems)[0]:
            cp.start()

    def finish(ins, outs, sems):
        send, recv = copies(ins, outs, sems)
        for cp in recv:
            cp.wait_recv()
        for cp in send:
            cp.wait_send()

    return _Stage([v], [jax.ShapeDtypeStruct((NCHIP,) + v.shape, v.dtype)], {},
                  [pltpu.SemaphoreType.DMA((3,)), pltpu.SemaphoreType.DMA((3,))], start, finish)


def _to_sibling(srcs):
    n = len(srcs)

    def copies(ins, outs, sems):
        x, y, c, _ = _place()
        sib = (x, y, 1 - c)
        return [_rcopy(ins[i].at[:, 1 - c] if srcs[i].ndim == 4 else ins[i], outs[i], sems[0].at[i], sems[1].at[i], sib)
                for i in range(n)]

    def start(ins, outs, sems):
        for cp in copies(ins, outs, sems):
            cp.start()

    def finish(ins, outs, sems):
        for cp in copies(ins, outs, sems):
            cp.wait()

    shapes = [jax.ShapeDtypeStruct((NCHIP,) + s.shape[2:] if s.ndim == 4 else s.shape, s.dtype) for s in srcs]
    return _Stage(srcs, shapes, {}, [pltpu.SemaphoreType.DMA((n,)), pltpu.SemaphoreType.DMA((n,))], start, finish)


def _to_chips(srcs, parts=None, lands=None):
    n = len(srcs)
    parts = [ALL] * n if parts is None else parts
    lands = [None] * n if lands is None else lands
    given = [i for i in range(n) if lands[i] is not None]

    def rows(ref, i):
        hr = srcs[i].shape[1]
        r0, r1 = hr * parts[i][0] // parts[i][2], hr * parts[i][1] // parts[i][2]
        return ref.at[pl.ds(r0, r1 - r0), :]

    def copies(ins, outs, sems):
        x, y, c, chips = _place()
        me = 2 * x + y
        return [_rcopy(rows(ins[i].at[2 * chip[0] + chip[1]] if srcs[i].shape[0] == NCHIP else ins[i].at[c], i),
                       rows(outs[i].at[me], i), sems[0].at[3 * i + j], sems[1].at[3 * i + j], (*chip, c))
                for i in range(n) for j, chip in enumerate(chips)]

    def start(ins, outs, sems):
        for cp in copies(ins, outs, sems):
            cp.start()

    def finish(ins, outs, sems):
        for cp in copies(ins, outs, sems):
            cp.wait()

    shapes = [jax.ShapeDtypeStruct((NCHIP,) + s.shape[1:], s.dtype) for s in srcs]
    alias = {n + q: i for q, i in enumerate(given)}
    return _Stage(list(srcs) + [lands[i] for i in given], shapes, alias, _sem2(n, 3), start, finish)


HBM_REF = pl.BlockSpec(memory_space=pltpu.HBM)
SEM_REF = pl.BlockSpec(memory_space=pltpu.SEMAPHORE)
DATAFLOW = pltpu.SideEffectType.DATAFLOW_SIDE_EFFECTING


def _after(x):
    return _Stage([x], [], {}, [], lambda *a: None, lambda *a: None)


class _Flight:
    def __init__(self, stage, sems, bufs):
        self.stage, self.sems, self.bufs = stage, list(sems), list(bufs)

    def landed(self):
        st, n = self.stage, len(self.stage.operands)
        fresh = [j for j in range(len(st.out_shape)) if j not in st.alias.values()]
        back = {v: k for k, v in st.alias.items()}
        return [self.bufs[back[j]] if j in back else self.bufs[n + fresh.index(j)] for j in range(len(st.out_shape))]


def _split_call(name, finish=(), start=(), after=None):
    bufs, stage_bufs = [], []

    def slot(a):
        for i, b in enumerate(bufs):
            if b is a:
                return i
        bufs.append(a)
        return len(bufs) - 1

    fin_slots = [[slot(b) for b in fl.bufs] for fl in finish]
    for st in start:
        fresh = [lax.empty(o.shape, o.dtype) for j, o in enumerate(st.out_shape) if j not in st.alias.values()]
        stage_bufs.append([slot(a) for a in list(st.operands) + fresh])
    old_sems = [s for fl in finish for s in fl.sems]
    new_sems = [s for st in start for s in st.sems]
    nb, no, nn = len(bufs), len(old_sems), len(new_sems)

    def refs_of(st, slots, buf_refs):
        n = len(st.operands)
        ins = [buf_refs[i] for i in slots[:n]]
        fresh = [j for j in range(len(st.out_shape)) if j not in st.alias.values()]
        back = {v: k for k, v in st.alias.items()}
        outs = [ins[back[j]] if j in back else buf_refs[slots[n + fresh.index(j)]] for j in range(len(st.out_shape))]
        return ins, outs

    def body(*refs):
        buf_refs, sem_in = refs[:nb], refs[nb:nb + no]
        sem_out = refs[nb + no + (after is not None):][:nn]
        token = refs[-1]
        pos = 0
        for fl, slots in zip(finish, fin_slots):
            ins, outs = refs_of(fl.stage, slots, buf_refs)
            fl.stage.finish(ins, outs, sem_in[pos:pos + len(fl.sems)])
            pos += len(fl.sems)
        pos = 0
        for st, slots in zip(start, stage_bufs):
            ins, outs = refs_of(st, slots, buf_refs)
            st.start(ins, outs, sem_out[pos:pos + len(st.sems)])
            pos += len(st.sems)
        token[...] = jnp.zeros_like(token)

    res = pl.pallas_call(
        body, name=name,
        out_shape=tuple(new_sems) + tuple(pltpu.HBM(b.shape, b.dtype) for b in bufs) +
                  (jax.ShapeDtypeStruct((8, LANE), F32),),
        in_specs=(HBM_REF,) * nb + (SEM_REF,) * no + ((pl.BlockSpec(memory_space=pl.ANY),) if after is not None else ()),
        out_specs=(SEM_REF,) * nn + (HBM_REF,) * nb + (pl.BlockSpec(memory_space=pltpu.VMEM),),
        input_output_aliases={i: nn + i for i in range(nb)},
        compiler_params=pltpu.CompilerParams(has_side_effects=DATAFLOW),
    )(*_hbm(*bufs), *old_sems, *([after] if after is not None else []))
    sems, thru, token = res[:nn], res[nn:nn + nb], res[-1]
    for fl, slots in zip(finish, fin_slots):
        fl.bufs = [thru[i] for i in slots]
    flights, pos = [], 0
    for st, slots in zip(start, stage_bufs):
        flights.append(_Flight(st, sems[pos:pos + len(st.sems)], [thru[i] for i in slots]))
        pos += len(st.sems)
    return flights, token


def _share(pairs):
    n = len(pairs)

    def start(ins, outs, sems):
        x, y, c, _ = _place()
        for i in range(n):
            _rcopy(outs[i].at[c], outs[i].at[c], sems[0].at[i], sems[1].at[i], (x, y, 1 - c)).start()

    def finish(ins, outs, sems):
        x, y, c, _ = _place()
        for i in range(n):
            _rcopy(outs[i].at[c], outs[i].at[c], sems[0].at[i], sems[1].at[i], (x, y, 1 - c)).wait_send()
            _rcopy(outs[i].at[1 - c], outs[i].at[1 - c], sems[0].at[i], sems[1].at[i], (x, y, 1 - c)).wait_recv()

    return _Stage(pairs, [_sds(p) for p in pairs], {i: i for i in range(n)},
                  [pltpu.SemaphoreType.DMA((n,)), pltpu.SemaphoreType.DMA((n,))], start, finish)


def _row_block(rows, cols, itemsize=4, target=2 * MIB):
    br = rows
    while br * cols * itemsize > target and br % 32 == 0:
        br //= 2
    return br


def _cast_place(w, chip_idx, name):
    rows, cols = w.shape
    br = _row_block(rows, cols)

    def body(k_ref, w_ref, o_ref):
        o_ref[0] = w_ref[...].astype(BF)

    return _call(
        body, name=name, grid=(rows // br,), prefetch=chip_idx,
        in_specs=[pl.BlockSpec((br, cols), lambda r, k: (r, 0))],
        out_specs=[pl.BlockSpec((1, br, cols), lambda r, k: (k[0], r, 0))],
        out_shape=[jax.ShapeDtypeStruct((NCHIP, rows, cols), BF)], vmem=32, args=[w])[0][0]


def _cast_place_multi(ws, chip_idx, stages=()):
    br = 128
    nblk = [a.shape[0] // br for a in ws]
    starts = [sum(nblk[:i]) for i in range(len(ws))]

    def body(k_ref, *refs):
        r = pl.program_id(0)
        for i in range(len(ws)):
            @pl.when(jnp.logical_and(r >= starts[i], r < starts[i] + nblk[i]))
            def _(i=i):
                refs[len(ws) + i][0] = refs[i][...].astype(BF)

    def at(i):
        return functools.partial(lambda r, s, nb: jnp.clip(r - s, 0, nb - 1), s=starts[i], nb=nblk[i])

    outs, landed = _call(
        body, name="cast_rest", grid=(sum(nblk),), prefetch=chip_idx,
        in_specs=[pl.BlockSpec((br, a.shape[1]), functools.partial(lambda r, k, f: (f(r), 0), f=at(i)))
                  for i, a in enumerate(ws)],
        out_specs=[pl.BlockSpec((1, br, a.shape[1]), functools.partial(lambda r, k, f: (k[0], f(r), 0), f=at(i)))
                   for i, a in enumerate(ws)],
        out_shape=[jax.ShapeDtypeStruct((NCHIP,) + a.shape, BF) for a in ws], vmem=32, args=list(ws), stages=stages)
    return outs, landed


def _add_sibling(g, land, cidx, name, stages=()):
    _, _, hr, cols = g.shape
    br = _row_block(hr, cols)

    def body(c_ref, g_ref, l_ref, o_ref):
        o_ref[...] = (g_ref[0, 0].astype(F32) + l_ref[0].astype(F32)).astype(BF)[None]

    outs, st = _call(
        body, name=name, grid=(NCHIP, hr // br), prefetch=cidx,
        in_specs=[pl.BlockSpec((1, 1, br, cols), lambda k, r, c: (k, c[0], r, 0)),
                  pl.BlockSpec((1, br, cols), lambda k, r, c: (k, r, 0))],
        out_specs=[pl.BlockSpec((1, br, cols), lambda k, r, c: (k, r, 0))],
        out_shape=[jax.ShapeDtypeStruct((NCHIP, hr, cols), BF)], vmem=32, args=[g, land], stages=stages)
    return outs[0], st


def _add_sibling_multi(gs, lands, cidx, name):
    n = len(gs)
    brs = [_row_block(g.shape[2], g.shape[3]) for g in gs]
    nrb = [g.shape[2] // b for g, b in zip(gs, brs)]
    nblk = [NCHIP * q for q in nrb]
    starts = [sum(nblk[:i]) for i in range(n)]

    def body(c_ref, *refs):
        r = pl.program_id(0)
        for i in range(n):
            g_ref, l_ref, o_ref = refs[2 * i], refs[2 * i + 1], refs[2 * n + i]

            @pl.when(jnp.logical_and(r >= starts[i], r < starts[i] + nblk[i]))
            def _():
                o_ref[...] = (g_ref[0, 0].astype(F32) + l_ref[0].astype(F32)).astype(BF)[None]

    def at(i, r):
        q = jnp.clip(r - starts[i], 0, nblk[i] - 1)
        return q // nrb[i], q % nrb[i]

    def g_spec(i):
        return pl.BlockSpec((1, 1, brs[i], gs[i].shape[3]),
                            functools.partial(lambda r, c, i: (at(i, r)[0], c[0], at(i, r)[1], 0), i=i))

    def l_spec(i):
        return pl.BlockSpec((1, brs[i], gs[i].shape[3]),
                            functools.partial(lambda r, c, i: (at(i, r)[0], at(i, r)[1], 0), i=i))

    return _call(
        body, name=name, grid=(sum(nblk),), prefetch=cidx,
        in_specs=[s for i in range(n) for s in (g_spec(i), l_spec(i))], out_specs=[l_spec(i) for i in range(n)],
        out_shape=[jax.ShapeDtypeStruct(l.shape, BF) for l in lands], vmem=32,
        args=[a for i in range(n) for a in (gs[i], lands[i])])[0]


def _add_pair(a, b, name):
    rows, cols = a.shape

    def body(a_ref, b_ref, o_ref):
        o_ref[...] = a_ref[...] + b_ref[...]

    spec = pl.BlockSpec((rows, cols), lambda r: (0, 0))
    return _call(body, name=name, grid=(1,), in_specs=[spec, spec], out_specs=[spec], out_shape=[_sds(a)],
                 vmem=32, args=[a, b])[0][0]


def _add_chips(own, land, idx, name, stages=None):
    _, hr, cols = land.shape
    br = _row_block(hr, cols)

    def body(s_ref, a_ref, b_ref, c_ref, d_ref, o_ref):
        o_ref[...] = (a_ref[...].astype(F32) + b_ref[...].astype(F32)) + (c_ref[...].astype(F32) +
                                                                           d_ref[...].astype(F32))

    spec = lambda q: pl.BlockSpec((1, br, cols), functools.partial(lambda r, s, q: (s[q], r, 0), q=q))
    outs, landed = _call(
        body, name=name, grid=(hr // br,), prefetch=idx,
        in_specs=[spec(0), spec(1), spec(2), spec(3)], out_specs=[spec(4)],
        out_shape=[jax.ShapeDtypeStruct((2, hr, cols), F32)], vmem=48, args=[own, land, land, land],
        stages=stages or ())
    return outs[0] if stages is None else (outs[0], landed)


def _add_chips_multi(owns, lands, idx, name, stages=()):
    n = len(owns)
    brs = [_row_block(l.shape[1], l.shape[2]) for l in lands]
    nblk = [l.shape[1] // b for l, b in zip(lands, brs)]
    starts = [sum(nblk[:i]) for i in range(n)]

    def body(s_ref, *refs):
        r = pl.program_id(0)
        for i in range(n):
            a_ref, b_ref, c_ref, d_ref = refs[4 * i:4 * i + 4]
            o_ref = refs[4 * n + i]

            @pl.when(jnp.logical_and(r >= starts[i], r < starts[i] + nblk[i]))
            def _():
                o_ref[...] = (a_ref[...].astype(F32) + b_ref[...].astype(F32)) + (c_ref[...].astype(F32) +
                                                                                   d_ref[...].astype(F32))

    def spec(i, q):
        return pl.BlockSpec((1, brs[i], lands[i].shape[2]), functools.partial(
            lambda r, s, q, st, nb: (s[q], jnp.clip(r - st, 0, nb - 1), 0), q=q, st=starts[i], nb=nblk[i]))

    outs, landed = _call(
        body, name=name, grid=(sum(nblk),), prefetch=idx,
        in_specs=[spec(i, q) for i in range(n) for q in range(4)], out_specs=[spec(i, 4) for i in range(n)],
        out_shape=[jax.ShapeDtypeStruct((2,) + l.shape[1:], F32) for l in lands], vmem=48,
        args=[a for i in range(n) for a in (owns[i], lands[i], lands[i], lands[i])], stages=stages)
    return outs, landed


def _adamw_math(w, g, m, v):
    mn = ADAM_B1 * m + (1.0 - ADAM_B1) * g
    vn = ADAM_B2 * v + (1.0 - ADAM_B2) * (g * g)
    m_hat = mn / (1.0 - ADAM_B1 ** ADAM_STEP)
    v_hat = vn / (1.0 - ADAM_B2 ** ADAM_STEP)
    return -ADAM_LR * (m_hat / (jnp.sqrt(v_hat) + ADAM_EPS) + ADAM_WD * w), mn, vn


def _adamw(w, g, m, v, name, stages=()):
    rows, cols = w.shape
    br = _row_block(rows, cols)

    def body(w_ref, g_ref, m_ref, v_ref, go_ref, d_ref, mo_ref, vo_ref):
        gv = g_ref[...]
        go_ref[...] = gv
        d_ref[...], mo_ref[...], vo_ref[...] = _adamw_math(w_ref[...], gv, m_ref[...], v_ref[...])

    spec = pl.BlockSpec((br, cols), lambda r: (r, 0))
    return _call(body, name=name, grid=(rows // br,), in_specs=[spec] * 4, out_specs=[spec] * 4,
                 out_shape=[_sds(w)] * 4, vmem=56, args=[w, g, m, v], stages=stages)


def _adamw_multi(names, w, g, m, v, stages=()):
    cols = w[names[0]].shape[1]
    br = 128
    nblk = [w[n].shape[0] // br for n in names]
    starts = [sum(nblk[:i]) for i in range(len(names))]

    def body(*refs):
        r = pl.program_id(0)
        for i in range(len(names)):
            w_ref, g_ref, m_ref, v_ref = refs[4 * i:4 * i + 4]
            go_ref, d_ref, mo_ref, vo_ref = refs[4 * len(names) + 4 * i:4 * len(names) + 4 * i + 4]

            @pl.when(jnp.logical_and(r >= starts[i], r < starts[i] + nblk[i]))
            def _():
                gv = g_ref[...]
                go_ref[...] = gv
                d_ref[...], mo_ref[...], vo_ref[...] = _adamw_math(w_ref[...], gv, m_ref[...], v_ref[...])

    def spec(i):
        return pl.BlockSpec((br, cols), functools.partial(
            lambda r, s, nb: (jnp.clip(r - s, 0, nb - 1), 0), s=starts[i], nb=nblk[i]))

    outs, landed = _call(
        body, name="adamw_" + "_".join(names), grid=(sum(nblk),),
        in_specs=[spec(i) for i in range(len(names)) for _ in range(4)],
        out_specs=[spec(i) for i in range(len(names)) for _ in range(4)],
        out_shape=[_sds(w[n]) for n in names for _ in range(4)], vmem=56,
        args=[a[n] for n in names for a in (w, g, m, v)], stages=stages)
    return {n: outs[4 * i:4 * i + 4] for i, n in enumerate(names)}, landed


def _to_everyone(v):
    deltas = [(a, b, e) for a in (0, 1) for b in (0, 1) for e in (0, 1)][1:]

    def copies(ins, outs, sems):
        x, y, c, _ = _place()
        me = 4 * x + 2 * y + c
        flip = lambda p, f: 1 - p if f else p
        return [_rcopy(ins[0], outs[0].at[me], sems[0].at[q], sems[1].at[q], (flip(x, a), flip(y, b), flip(c, e)))
                for q, (a, b, e) in enumerate(deltas)]

    def start(ins, outs, sems):
        for cp in copies(ins, outs, sems):
            cp.start()

    def finish(ins, outs, sems):
        for cp in copies(ins, outs, sems):
            cp.wait()

    n = len(deltas)
    return _Stage([v], [jax.ShapeDtypeStruct((2 * NCHIP,) + v.shape, v.dtype)], {},
                  [pltpu.SemaphoreType.DMA((n,)), pltpu.SemaphoreType.DMA((n,))], start, finish)


SMALL_AT = {"norm_mix_pre": (0, 1, D), "norm_mix_post": (1, 1, D), "norm_mlp_pre": (2, 1, D),
            "norm_mlp_post": (3, 1, D), "b_gate": (4, 2, D), "conv_b": (6, 1, D), "lru_b_a": (7, 1, D),
            "lru_b_x": (8, 1, D), "lru_lambda": (9, 1, D), "pool_scale": (10, 1, DP)}
SMALL_SEPARATE = ["conv_w", "lru_w_a", "lru_w_x", "pool_w"]


def _adamw_small(small_sum, first_all, sep_grads, w, m, v):
    packed, sep = list(SMALL_AT), list(SMALL_SEPARATE)
    names = packed + sep

    def body(*refs):
        s_ref, a_ref, refs = refs[0], refs[1], refs[2:]
        g_sep, refs = refs[:len(sep)], refs[len(sep):]
        nn = len(names)
        w_r, m_r, v_r, refs = refs[:nn], refs[nn:2 * nn], refs[2 * nn:3 * nn], refs[3 * nn:]
        g_out, refs = refs[:len(packed)], refs[len(packed):]
        d_o, m_o, v_o = refs[:nn], refs[nn:2 * nn], refs[2 * nn:3 * nn]
        for i, n in enumerate(names):
            if i == 0:
                g = a_ref[0:1, :]
                for q in range(1, 2 * NCHIP):
                    g = g + a_ref[q:q + 1, :]
                g_out[i][...] = g
            elif n in SMALL_AT:
                r0, nr, nc = SMALL_AT[n]
                g = jnp.concatenate([s_ref[r0 + q:r0 + q + 1, :nc] for q in range(nr)], axis=1)
                g_out[i][...] = g
            else:
                g = g_sep[i - len(packed)][...]
            d_o[i][...], m_o[i][...], v_o[i][...] = _adamw_math(w_r[i][...], g, m_r[i][...], v_r[i][...])

    ws = [w[n] for n in names]
    res = pl.pallas_call(
        body, name="adamw_small",
        out_shape=[_sds(w[n]) for n in packed] + [_sds(a) for a in ws] * 3,
        compiler_params=_cp(32),
    )(*_hbm(small_sum, first_all, *sep_grads, *ws, *[m[n] for n in names], *[v[n] for n in names]))
    nn, npk = len(names), len(packed)
    grad = dict(zip(packed, res[:npk]))
    delta = dict(zip(names, res[npk:npk + nn]))
    new_m = dict(zip(names, res[npk + nn:npk + 2 * nn]))
    new_v = dict(zip(names, res[npk + 2 * nn:]))
    return grad, delta, new_m, new_v


W_NAMES = ["norm_mix_pre", "norm_mix_post", "norm_mlp_pre", "norm_mlp_post", "w_in", "b_gate", "conv_w", "conv_b",
           "lru_w_a", "lru_b_a", "lru_w_x", "lru_b_x", "lru_lambda", "pool_w", "pool_scale", "w_lru_up",
           "w_pool_up", "w_o", "w_ff1", "w_ff2"]
BIG = ["w_in", "w_lru_up", "w_pool_up", "w_o", "w_ff1", "w_ff2"]


def _block_diag(w):
    hd = w.shape[-1]
    per = CB // hd
    w4 = w.reshape(NG, per, hd, hd)
    eye = jnp.eye(per, dtype=w.dtype)
    return jnp.einsum("gpij,pq->gpiqj", w4, eye).reshape(NG, CB, CB)


def _block_diag_extract(d, hd):
    per = CB // hd
    d5 = d.reshape(NG, per, hd, per, hd)
    return jnp.stack([d5[:, p, :, p, :] for p in range(per)], axis=1).reshape(NG * per, hd, hd)


def _halves(g):
    return g.reshape(NCHIP, 2, g.size // (g.shape[-1] * 2 * NCHIP), g.shape[-1])


def kernel(x, norm_mix_pre, norm_mix_post, norm_mlp_pre, norm_mlp_post, w_in, b_gate, conv_w, conv_b, lru_w_a, lru_b_a, lru_w_x, lru_b_x, lru_lambda, pool_w, pool_scale, w_lru_up, w_pool_up, w_o, w_ff1, w_ff2, loss_target, m_norm_mix_pre, m_norm_mix_post, m_norm_mlp_pre, m_norm_mlp_post, m_w_in, m_b_gate, m_conv_w, m_conv_b, m_lru_w_a, m_lru_b_a, m_lru_w_x, m_lru_b_x, m_lru_lambda, m_pool_w, m_pool_scale, m_w_lru_up, m_w_pool_up, m_w_o, m_w_ff1, m_w_ff2, v_norm_mix_pre, v_norm_mix_post, v_norm_mlp_pre, v_norm_mlp_post, v_w_in, v_b_gate, v_conv_w, v_conv_b, v_lru_w_a, v_lru_b_a, v_lru_w_x, v_lru_b_x, v_lru_lambda, v_pool_w, v_pool_scale, v_w_lru_up, v_w_pool_up, v_w_o, v_w_ff1, v_w_ff2):
    args = dict(locals())
    two_d = lambda a: a.reshape(-1, a.shape[-1])
    w = {n: two_d(args[n]) for n in W_NAMES}
    mom = {n: two_d(args["m_" + n]) for n in W_NAMES}
    var = {n: two_d(args["v_" + n]) for n in W_NAMES}
    i32 = lambda val: jnp.asarray(val, jnp.int32)
    chip = i32(2 * lax.axis_index("x") + lax.axis_index("y"))
    core = i32(lax.axis_index("c"))
    cidx = core.reshape(1)
    zero = i32(0)
    hd = lru_w_a.shape[-1]
    xs, target = x[0], loss_target[0]
    g1, g2, g3, g4 = norm_mix_pre, norm_mix_post, norm_mlp_pre, norm_mlp_post

    mix = ["w_lru_up", "w_pool_up", "w_o"]
    full = {"w_in": _cast_place(w["w_in"], chip.reshape(1), "cast_w_in")}
    (fl_in, fl_conv), first = _split_call("gather_start_first", start=[
        _gather([full["w_in"]], ici=[(0, ALL)]), _gather_whole(w["conv_w"])])
    casts, _ = _cast_place_multi([w[n] for n in BIG[1:]], chip.reshape(1), stages=[_after(first)])
    full.update(zip(BIG[1:], casts))
    (fl_mix, fl_ff1, fl_ff2), started = _split_call("gather_start_rest", start=[
        _gather([full[n] for n in mix], ici=[(0, ALL), (1, ALL), (2, ALL)]),
        _gather([full["w_ff1"]], ici=[(0, ALL)]), _gather([full["w_ff2"]], ici=[(0, ALL)])])
    wa = _block_diag(lru_w_a[0]).astype(BF)
    wx = _block_diag(lru_w_x[0]).astype(BF)
    pw = pool_w[0].astype(BF)

    def to_sibling(name, flight, after=None):
        (fl,), passed = _split_call(name + "_pass", finish=[flight], after=after,
                                    start=[_gather(flight.landed(), d2d=[(i, ALL) for i in range(len(flight.bufs))])])
        passed_on.append(passed)
        return fl

    passed_on = []

    def arrived(name, flight, after=None):
        _split_call(name + "_done", finish=[flight], after=after)
        return flight.landed()

    idx_big = jnp.stack([chip, (chip + 1) % NCHIP, (chip + 2) % NCHIP, (chip + 3) % NCHIP, core])
    proj, h1 = _fwd_inproj_own(xs, g1, fl_in.bufs[0], idx_big, stages=[_after(started)])
    fl_in = to_sibling("gather_w_in", fl_in, after=h1)
    _split_call("gather_w_in_done", finish=[fl_in, fl_conv])
    (w_in_f,), (conv_all,) = fl_in.landed(), fl_conv.landed()
    full["w_in"] = w_in_f
    conv_all = lax.dynamic_update_slice(conv_all, w["conv_w"][None], (chip, zero, zero))
    conv_full = jnp.transpose(conv_all, (1, 0, 2)).reshape(4, DR)
    proj = _fwd_inproj_rest(h1, w_in_f, proj, idx_big)
    fl_mix = to_sibling("gather_mix", fl_mix, after=proj)
    (ylru, hs), _ = _fwd_lru(proj, conv_full, conv_b, wa, lru_b_a, wx, lru_b_x, lru_lambda,
                             stages=[_after(passed_on[-1])])
    got = arrived("gather_mix", fl_mix, after=ylru)
    fl_ff1 = to_sibling("gather_ff1", fl_ff1, after=ylru)
    w_lru_up_f, w_pool_up_f, w_o_f = got[0].reshape(DR, D), got[1], got[2].reshape(D, D)
    ypool = _fwd_pool(proj, pw, pool_scale)
    (x2, h2, m, mrg, bra, brb), _ = _fwd_merge(xs, ylru, ypool, proj, b_gate, g2, g3, w_lru_up_f, w_pool_up_f, w_o_f,
                                               stages=[_after(passed_on[-1])])
    fl_ff2 = to_sibling("gather_ff2", fl_ff2, after=h2)
    _split_call("gather_ff_done", finish=[fl_ff1, fl_ff2])
    (ff1,), (ff2,) = fl_ff1.landed(), fl_ff2.landed()
    ff2 = ff2.reshape(DF, D)
    a1, lossp, dy, df, dg4 = _fwd_mlp_loss(h2, ff1, ff2, x2, target, g4)

    dh2, df1 = _bwd_mlp_x(df, a1, ff1, ff2)
    dw_ff1, dw_ff2 = _bwd_mlp_w(df, h2, a1, df1)
    g_ff = [_halves(dw_ff1), _halves(dw_ff2)]
    (dxres, dgates, dylru, dypool, dm, dbra, dbrb, dg2, dg3, dbg), (l_ff,) = _bwd_merge(
        dh2, dy, x2, m, bra, brb, proj, b_gate, g2, g3, w_lru_up_f, w_pool_up_f, w_o_f, stages=[_to_sibling(g_ff)])
    p_ff = _add_sibling_multi(g_ff, l_ff, cidx, "add_sibling_ff")
    (fl_ff,), sent_ff = _split_call("reduce_ff_start", start=[_to_chips(p_ff)])
    (dw_o, dw_lru_up, dw_pool_up), _ = _dw_merge(mrg, dm, ylru, dbra, ypool, dbrb, stages=[_after(sent_ff)])
    g_mix = [_halves(dw_lru_up), _halves(dw_pool_up), _halves(dw_o)]
    (dxp, dgl, dcw, dcb, dwa, dba, dwx, dbx, dlam), (l_mix,) = _bwd_lru(
        proj, hs, dylru, conv_full, conv_b, wa, lru_b_a, wx, lru_b_x, lru_lambda, stages=[_to_sibling(g_mix)])
    p_mix = _add_sibling_multi(g_mix, l_mix, cidx, "add_sibling_mix")
    dxpool, dpw, dsc = _bwd_pool(proj, dypool, pw, pool_scale)
    dproj = [dxp, dgl, dxpool, dgates]
    small = jnp.concatenate([
        jnp.zeros((1, D), F32), dg2, dg3, dg4, dbg.reshape(2, D), dcb, dba, dbx, dlam,
        jnp.pad(dsc, ((0, 0), (0, D - DP))), jnp.pad(lossp, ((0, 0), (0, D - 1))), dcw,
        _block_diag_extract(dwa, hd).reshape(-1, D), _block_diag_extract(dwx, hd).reshape(-1, D),
        dpw.reshape(-1, D)], axis=0)
    (fl_mixr, fl_smalls), sent_mix = _split_call("reduce_mix_start", start=[_to_chips(p_mix), _to_sibling([small])])
    dw_in = _bwd_inproj_w(h1, dproj, sent_mix)
    _split_call("reduce_small_sibling_done", finish=[fl_smalls], after=dw_in)
    small, l_small = fl_smalls.bufs
    small2 = _add_pair(small, l_small, "add_sibling_small").reshape(2, SMALL_ROWS // 2, D)
    g_in = _halves(dw_in)
    done = ["w_ff1", "w_ff2"] + mix
    (fl_gin, fl_small), sib_started = _split_call("reduce_in_sibling_start", finish=[fl_ff, fl_mixr],
                                                  start=[_to_sibling([g_in]), _to_chips([small2])])
    p_ff1, p_ff2, c_ff1, c_ff2 = fl_ff.bufs
    p_mix, c_mix = fl_mixr.bufs[:3], fl_mixr.bufs[3:]
    pairs, _ = _add_chips_multi([p_ff1, p_ff2] + p_mix, [c_ff1, c_ff2] + c_mix, idx_big, "add_chips_done",
                                stages=[_after(sib_started)])
    _split_call("reduce_in_sibling_done", finish=[fl_gin], after=pairs[-1])
    g_in, l_in = fl_gin.bufs
    p_in = _add_sibling(g_in, l_in, cidx, "add_sibling_w_in")[0]
    (fl_pin,), token = _split_call("reduce_last_start", start=[_to_chips([p_in])])
    _split_call("reduce_small_done", finish=[fl_small], after=token)
    small2, c_small = fl_small.bufs
    own_small = lax.dynamic_index_in_dim(small2, core, 0, keepdims=True)
    c_small = lax.dynamic_update_slice(c_small, own_small, (chip, zero, zero))
    pair_small = _add_chips(c_small, c_small, jnp.stack([zero, zero + 1, zero + 2, zero + 3, core]), "add_chips_small")
    (fl_share,), shared_start = _split_call("reduce_share_start", start=[_share(pairs + [pair_small])])
    grad_x, dg1 = _bwd_inproj_x(dproj, full["w_in"], xs, dxres, g1, stages=[_after(shared_start)])
    (fl_dg1,), share_done = _split_call("reduce_share_done", finish=[fl_share], start=[_to_everyone(dg1)])
    shared = fl_share.landed()
    pairs, pair_small = shared[:-1], shared[-1]

    grads, delta, new_m, new_v = {}, {}, {}, {}
    for n, p in zip(done, pairs):
        grads[n] = p.reshape(-1, p.shape[-1])

    def update(n, stages=()):
        (grads[n], delta[n], new_m[n], new_v[n]), landed = _adamw(w[n], grads[n], mom[n], var[n], "adamw_" + n,
                                                                  stages=stages)
        return landed

    update("w_pool_up")
    updated, _ = _adamw_multi(["w_ff1", "w_ff2", "w_o", "w_lru_up"], w, grads, mom, var, stages=[_after(share_done)])
    for n, (go, d, mo, vo) in updated.items():
        grads[n], delta[n], new_m[n], new_v[n] = go, d, mo, vo
    _split_call("reduce_last_done", finish=[fl_pin, fl_dg1], after=new_v["w_lru_up"])
    (p_in, c_in), (dg1, dg1_all) = fl_pin.bufs, fl_dg1.bufs
    pair_in = _add_chips(p_in, c_in, idx_big, "add_chips_w_in")
    (fl_last,), last_start = _split_call("reduce_last_share_start", start=[_share([pair_in])])
    dg1_all = lax.dynamic_update_slice(dg1_all, dg1[None] + last_start[0, 0], (2 * chip + core, zero, zero))
    dg1_all = dg1_all.reshape(2 * NCHIP, D)
    small_sum = pair_small.reshape(SMALL_ROWS, D)
    loss = 0.5 * small_sum[LOSS_ROW, 0]
    ccols = DR // NCHIP
    sep = [lax.dynamic_slice(small_sum[12:16], (zero, chip * ccols), (4, ccols)),
           small_sum[16:80].reshape(-1, hd), small_sum[80:144].reshape(-1, hd), small_sum[144:208].reshape(-1, PG)]
    g_s, d_s, m_s, v_s = _adamw_small(small_sum, dg1_all, sep, w, mom, var)
    grads.update(g_s)
    grads.update(dict(zip(SMALL_SEPARATE, sep)))
    delta.update(d_s)
    new_m.update(m_s)
    new_v.update(v_s)
    _split_call("reduce_last_share_done", finish=[fl_last], after=v_s["pool_w"])
    pair_in, = fl_last.landed()
    grads["w_in"] = pair_in.reshape(-1, pair_in.shape[-1])
    update("w_in")

    out = lambda d: [d[n].reshape(args[n].shape) for n in W_NAMES]
    return (loss, grad_x[None], *out(grads), *out(delta), *out(new_m), *out(new_v))
```

```python
import functools
import math

import jax
import jax.numpy as jnp
from jax import lax
from jax.experimental import pallas as pl
from jax.experimental.pallas import tpu as pltpu

F32 = jnp.float32
BF = jnp.bfloat16

T = 2048
D = 1024
DR = 1024
DP = 512
DF = 4096
DIN = 4608
NCHIP = 4
CW_IN = DIN // NCHIP
LANE = 128
CB = 128
NG = DR // CB
PG = 128
POOL_WINDOWS = (2, 4, 8, 16)
NORM_EPS = 1e-6
LRU_C = 8.0
GELU_C = math.sqrt(2.0 / math.pi)
ADAM_LR = 0.001
ADAM_B1 = 0.9
ADAM_B2 = 0.999
ADAM_EPS = 1e-08
ADAM_WD = 0.01
ADAM_STEP = 10
MESH_ID = pl.DeviceIdType.MESH
ANY = pl.BlockSpec(memory_space=pl.ANY)
SMALL_ROWS = 208
LOSS_ROW = 11
MIB = 1 << 20


def _cp(vmem_mib=None):
    if vmem_mib is None:
        return pltpu.CompilerParams()
    return pltpu.CompilerParams(vmem_limit_bytes=vmem_mib * MIB)


def _hbm(*arrays):
    return [pltpu.with_memory_space_constraint(a, pltpu.HBM) for a in arrays]


def _hbm_out(shapes):
    return [pltpu.HBM(s.shape, s.dtype) for s in shapes]


class _Stage:
    def __init__(self, operands, out_shape, alias, sems, start, finish):
        self.operands, self.out_shape, self.alias, self.sems = list(operands), list(out_shape), dict(alias), list(sems)
        self.start, self.finish = start, finish


def _call(body, *, name, grid, in_specs, out_specs, out_shape, args, vmem=None, stages=(), prefetch=None,
          scratch=()):
    nin, nout = len(in_specs), len(out_specs)
    npre = 0 if prefetch is None else 1
    st_args, st_shapes, st_sems, aliases = [], [], list(scratch), {}
    for st in stages:
        for k, v in st.alias.items():
            aliases[npre + nin + len(st_args) + k] = nout + len(st_shapes) + v
        st_args += st.operands
        st_shapes += st.out_shape
        st_sems += st.sems

    def wrapped(*refs):
        pre, refs = refs[:npre], refs[npre:]
        ins, pos = refs[:nin], nin
        st_ins = []
        for st in stages:
            st_ins.append(refs[pos:pos + len(st.operands)])
            pos += len(st.operands)
        outs, pos = refs[pos:pos + nout], pos + nout
        st_outs = []
        for st in stages:
            st_outs.append(refs[pos:pos + len(st.out_shape)])
            pos += len(st.out_shape)
        work, pos = refs[pos:pos + len(scratch)], pos + len(scratch)
        sems = []
        for st in stages:
            sems.append(refs[pos:pos + len(st.sems)])
            pos += len(st.sems)
        if stages:
            first = functools.reduce(jnp.logical_and, [pl.program_id(a) == 0 for a in range(len(grid))])

            @pl.when(first)
            def _():
                for st, a, b, s in zip(stages, st_ins, st_outs, sems):
                    st.start(a, b, s)

        body(*pre, *ins, *outs, *work)
        if stages:
            last = functools.reduce(jnp.logical_and, [pl.program_id(a) == g - 1 for a, g in enumerate(grid)])

            @pl.when(last)
            def _():
                for st, a, b, s in zip(stages, st_ins, st_outs, sems):
                    st.finish(a, b, s)

    all_in = list(in_specs) + [ANY] * len(st_args)
    all_out = list(out_specs) + [ANY] * len(st_shapes)
    kw = dict(has_side_effects=True) if stages else {}
    if vmem is not None:
        kw["vmem_limit_bytes"] = vmem * MIB
    if prefetch is None:
        gkw = dict(grid=grid, in_specs=all_in, out_specs=all_out, scratch_shapes=st_sems)
    else:
        gkw = dict(grid_spec=pltpu.PrefetchScalarGridSpec(
            num_scalar_prefetch=1, grid=grid, in_specs=all_in, out_specs=all_out, scratch_shapes=st_sems))
    res = pl.pallas_call(
        wrapped, name=name, out_shape=_hbm_out(list(out_shape) + st_shapes), input_output_aliases=aliases,
        compiler_params=pltpu.CompilerParams(**kw), **gkw,
    )(*([prefetch] if npre else []), *_hbm(*args, *st_args))
    outs, rest, st_res = list(res[:nout]), list(res[nout:]), []
    for st in stages:
        st_res.append(rest[:len(st.out_shape)])
        rest = rest[len(st.out_shape):]
    return outs, st_res


def _mm(a, b):
    return jnp.dot(a.astype(BF), b.astype(BF), preferred_element_type=F32)


def _mm_nt(a, b):
    return lax.dot_general(a.astype(BF), b.astype(BF), (((1,), (1,)), ((), ())),
                           preferred_element_type=F32)


def _mm_tn(a, b):
    return lax.dot_general(a.astype(BF), b.astype(BF), (((0,), (0,)), ((), ())),
                           preferred_element_type=F32)


def _rows(v):
    return lax.broadcasted_iota(jnp.int32, v.shape, 0)


def _sd(v, s, fill=0.0):
    return jnp.where(_rows(v) >= s, pltpu.roll(v, s, axis=0), fill)


def _su(v, s, fill=0.0):
    n = v.shape[0]
    return jnp.where(_rows(v) < n - s, pltpu.roll(v, n - s, axis=0), fill)


def _sigmoid(z):
    return 1.0 / (1.0 + jnp.exp(-z))


def _softplus(z):
    e = jnp.exp(-jnp.abs(z))
    u = 1.0 + e
    d = u - 1.0
    log1p = jnp.where(d == 0.0, e, jnp.log(u) * (e / jnp.where(d == 0.0, 1.0, d)))
    return jnp.maximum(z, 0.0) + log1p


def _mean(v):
    return jnp.mean(v, axis=-1, keepdims=True)


def _colsum(v):
    return jnp.sum(v, axis=0, keepdims=True)


def _acc(ref, val, first):
    @pl.when(first)
    def _():
        ref[...] = val

    @pl.when(jnp.logical_not(first))
    def _():
        ref[...] += val


def _conv(xp, cw, cb):
    x1, x2, x3 = _sd(xp, 1), _sd(xp, 2), _sd(xp, 3)
    xc = cb + cw[0:1] * x3 + cw[1:2] * x2 + cw[2:3] * x1 + cw[3:4] * xp
    return xc, x1, x2, x3


def _lru_gates(xc, wa, ba, wx, bx, lam):
    xcb = xc.astype(BF)
    r = _sigmoid(_mm(xcb, wa) + ba)
    ii = _sigmoid(_mm(xcb, wx) + bx)
    sp = _softplus(-lam)
    la = (-LRU_C) * r * sp
    a = jnp.exp(la)
    mult = jnp.sqrt(-jnp.tanh(la) * (a * a + 1.0))
    return xcb, r, ii, sp, a, mult


def _gelu_parts(g):
    th = jnp.tanh(GELU_C * (g + 0.044715 * (g * g * g)))
    gel = 0.5 * g * (1.0 + th)
    dgel = 0.5 * (1.0 + th) + 0.5 * g * (1.0 - th * th) * (GELU_C * (1.0 + 3.0 * 0.044715 * (g * g)))
    return gel, dgel


def _tile_scan(a, b, a_s, b_s, out_ref, reverse):
    n, lanes = a.shape
    nt = n // 8
    a, b = a.reshape(nt, 8, lanes), b.reshape(nt, 8, lanes)
    sub = lax.broadcasted_iota(jnp.int32, a.shape, 1)
    s = 1
    while s < 8:
        keep = sub < 8 - s if reverse else sub >= s
        amount = 8 - s if reverse else s
        b = b + a * jnp.where(keep, pltpu.roll(b, amount, axis=1), 0.0)
        a = a * jnp.where(keep, pltpu.roll(a, amount, axis=1), 1.0)
        s *= 2
    a_s[...] = a.reshape(n, lanes)
    b_s[...] = b.reshape(n, lanes)
    edge = pl.ds(0 if reverse else 7, nt, stride=8)
    ta, tb = a_s[edge, :], b_s[edge, :]
    shift = _su if reverse else _sd
    s = 1
    while s < nt:
        tb = tb + ta * shift(tb, s, 0.0)
        if 2 * s < nt:
            ta = ta * shift(ta, s, 1.0)
        s *= 2
    enters = shift(tb, 1, 0.0)
    for o in range(8):
        rows = pl.ds(o, nt, stride=8)
        out_ref[rows, :] = b_s[rows, :] + a_s[rows, :] * enters


def _pool_window(x, steps, shift):
    s, sh = x, 1
    for _ in range(steps):
        s = s + shift(s, sh)
        sh *= 2
    return s


def _fwd_inproj_own(x, g1, w_in, slots, stages=()):
    tm = 1024

    def body(s_ref, x_ref, g_ref, w_ref, proj_ref, h_ref):
        xv = x_ref[...]
        r = lax.rsqrt(_mean(xv * xv) + NORM_EPS)
        h = ((xv * r) * g_ref[...]).astype(BF)
        h_ref[...] = h
        proj_ref[...] = jnp.dot(h, w_ref[0], preferred_element_type=F32)

    return _call(
        body, name="fwd_inproj_own", grid=(T // tm,), prefetch=slots,
        in_specs=[pl.BlockSpec((tm, D), lambda i, s: (i, 0)),
                  pl.BlockSpec((1, D), lambda i, s: (0, 0)),
                  pl.BlockSpec((1, D, CW_IN), lambda i, s: (s[0], 0, 0))],
        out_specs=[pl.BlockSpec((tm, CW_IN), lambda i, s: (i, s[0])),
                   pl.BlockSpec((tm, D), lambda i, s: (i, 0))],
        out_shape=[jax.ShapeDtypeStruct((T, DIN), F32), jax.ShapeDtypeStruct((T, D), BF)],
        vmem=40, args=[x, g1, w_in], stages=stages)[0]


def _fwd_inproj_rest(h1, w_in, proj, slots):
    tm = 1024

    def body(s_ref, h_ref, w_ref, p_in, proj_ref):
        for lo, hi in _col_chunks(CW_IN):
            proj_ref[:, lo:hi] = jnp.dot(h_ref[...], w_ref[0, :, lo:hi], preferred_element_type=F32)

    res = pl.pallas_call(
        body, name="fwd_inproj_rest",
        grid_spec=pltpu.PrefetchScalarGridSpec(
            num_scalar_prefetch=1, grid=(NCHIP - 1, T // tm),
            in_specs=[pl.BlockSpec((tm, D), lambda k, i, s: (i, 0)),
                      pl.BlockSpec((1, D, CW_IN), lambda k, i, s: (s[1 + k], 0, 0)), ANY],
            out_specs=pl.BlockSpec((tm, CW_IN), lambda k, i, s: (i, s[1 + k]))),
        out_shape=pltpu.HBM((T, DIN), F32), input_output_aliases={3: 0},
        compiler_params=_cp(40),
    )(slots, *_hbm(h1, w_in, proj))
    return res


def _vec_spec():
    return pl.BlockSpec((1, CB), lambda j: (0, j))


def _fwd_lru(proj, conv_w, conv_b, wa, ba, wx, bx, lam, stages=()):
    def body(xp_ref, g_ref, cw_ref, cb_ref, wa_ref, ba_ref, wx_ref, bx_ref, lam_ref, y_ref, h_ref, a_s, b_s):
        xc, _, _, _ = _conv(xp_ref[...], cw_ref[...], cb_ref[...])
        _, _, ii, _, a, mult = _lru_gates(xc, wa_ref[0], ba_ref[...], wx_ref[0], bx_ref[...], lam_ref[...])
        _tile_scan(a, mult * (ii * xc), a_s, b_s, h_ref, reverse=False)
        gel, _ = _gelu_parts(g_ref[...])
        y_ref[...] = (h_ref[...] * gel).astype(BF)

    return _call(
        body, name="fwd_lru", grid=(NG,),
        in_specs=[pl.BlockSpec((T, CB), lambda j: (0, j)),
                  pl.BlockSpec((T, CB), lambda j: (0, NG + j)),
                  pl.BlockSpec((4, CB), lambda j: (0, j)),
                  _vec_spec(),
                  pl.BlockSpec((1, CB, CB), lambda j: (j, 0, 0)), _vec_spec(),
                  pl.BlockSpec((1, CB, CB), lambda j: (j, 0, 0)), _vec_spec(),
                  _vec_spec()],
        out_specs=[pl.BlockSpec((T, CB), lambda j: (0, j)), pl.BlockSpec((T, CB), lambda j: (0, j))],
        out_shape=[jax.ShapeDtypeStruct((T, DR), BF), jax.ShapeDtypeStruct((T, DR), F32)],
        vmem=48, args=[proj, proj, conv_w, conv_b, wa, ba, wx, bx, lam], stages=stages,
        scratch=[pltpu.VMEM((T, CB), F32)] * 2)


def _pool_cnt(w):
    t = lax.broadcasted_iota(jnp.int32, (T, 1), 0)
    return jnp.minimum(t + 1, w).astype(F32)


def _fwd_pool(proj, pool_w, pool_scale):
    def body(xp_ref, pw_ref, sc_ref, y_ref):
        for g, w in enumerate(POOL_WINDOWS):
            cols = slice(g * PG, (g + 1) * PG)
            x = xp_ref[:, cols]
            p = _pool_window(x, g + 1, _sd) / _pool_cnt(w) - x
            y_ref[:, cols] = (_mm(p, pw_ref[g]) * sc_ref[:, cols]).astype(BF)

    return pl.pallas_call(
        body, name="fwd_pool", grid=(1,),
        in_specs=[pl.BlockSpec((T, DP), lambda i: (0, 2 * DR // DP)),
                  pl.BlockSpec((4, PG, PG), lambda i: (0, 0, 0)),
                  pl.BlockSpec((1, DP), lambda i: (0, 0))],
        out_specs=pl.BlockSpec((T, DP), lambda i: (0, 0)),
        out_shape=pltpu.HBM((T, DP), BF),
        compiler_params=_cp(48),
    )(*_hbm(proj, pool_w, pool_scale))


GATE_BLK = 512
GATE_BLK0 = (2 * DR + DP) // GATE_BLK


def _gate_specs(tm):
    return [pl.BlockSpec((tm, GATE_BLK), functools.partial(lambda i, q: (i, GATE_BLK0 + q), q=q))
            for q in range(4)]


def _fwd_merge(x, ylru, ypool, proj, b_gate, g2, g3, w_lru_up, w_pool_up, w_o, stages=()):
    tm = 512

    def body(x_ref, yl_ref, yp_ref, p0, p1, p2, p3, bg_ref, g2_ref, g3_ref, wl_ref, wp_ref, wo_ref,
             x2_ref, h2_ref, m_ref, mrg_ref, bra_ref, brb_ref):
        bra = jnp.dot(yl_ref[...], wl_ref[...], preferred_element_type=F32)
        yp = yp_ref[...]
        brb = jnp.concatenate([jnp.dot(yp, wp_ref[k], preferred_element_type=F32) for k in range(NCHIP)], axis=1)
        bg = bg_ref[...]
        ga = _sigmoid(jnp.concatenate([p0[...], p1[...]], axis=1) + bg[:, :D])
        gb = _sigmoid(jnp.concatenate([p2[...], p3[...]], axis=1) + bg[:, D:])
        mrg = (ga * bra + gb * brb).astype(BF)
        m = jnp.dot(mrg, wo_ref[...], preferred_element_type=F32)
        r2 = lax.rsqrt(_mean(m * m) + NORM_EPS)
        x2 = x_ref[...] + (m * r2) * g2_ref[...]
        r3 = lax.rsqrt(_mean(x2 * x2) + NORM_EPS)
        x2_ref[...] = x2
        h2_ref[...] = ((x2 * r3) * g3_ref[...]).astype(BF)
        m_ref[...] = m
        mrg_ref[...] = mrg
        bra_ref[...] = bra.astype(BF)
        brb_ref[...] = brb.astype(BF)

    row = lambda w: pl.BlockSpec((tm, w), lambda i: (i, 0))
    full2 = lambda a, b: pl.BlockSpec((a, b), lambda i: (0, 0))
    return _call(
        body, name="fwd_merge", grid=(T // tm,),
        in_specs=[row(D), row(DR), row(DP)] + _gate_specs(tm) +
                 [full2(1, 2 * D), full2(1, D), full2(1, D), full2(DR, D),
                  pl.BlockSpec((NCHIP, DP, D // NCHIP), lambda i: (0, 0, 0)), full2(D, D)],
        out_specs=[row(D)] * 6,
        out_shape=[jax.ShapeDtypeStruct((T, D), F32), jax.ShapeDtypeStruct((T, D), BF),
                   jax.ShapeDtypeStruct((T, D), F32), jax.ShapeDtypeStruct((T, D), BF),
                   jax.ShapeDtypeStruct((T, D), BF), jax.ShapeDtypeStruct((T, D), BF)],
        vmem=48, args=[x, ylru, ypool, proj, proj, proj, proj, b_gate, g2, g3, w_lru_up, w_pool_up, w_o],
        stages=stages)


def _fwd_mlp_loss(h2, w_ff1, w_ff2, x2, target, g4):
    tm = 512
    fk = DF // NCHIP

    def body(h_ref, w1_ref, w2_ref, x2_ref, t_ref, g_ref, a1_ref, loss_ref, dy_ref, df_ref, dg_ref):
        first = pl.program_id(0) == 0
        h = h_ref[...]
        f = None
        for k in range(NCHIP):
            a1 = jnp.maximum(jnp.dot(h, w1_ref[k], preferred_element_type=F32), 0.0)
            a1_ref[:, k * fk:(k + 1) * fk] = a1.astype(BF)
            part = jnp.dot((a1 * a1).astype(BF), w2_ref[k * fk:(k + 1) * fk, :], preferred_element_type=F32)
            f = part if f is None else f + part
        g4v = g_ref[...]
        r4 = lax.rsqrt(_mean(f * f) + NORM_EPS)
        fn = f * r4
        e = (x2_ref[...] + fn * g4v) - t_ref[...]
        _acc(loss_ref, jnp.sum(_mean(e * e), axis=0, keepdims=True), first)
        dy = e * (1.0 / D)
        dy_ref[...] = dy
        _acc(dg_ref, _colsum(dy * fn), first)
        dfn = dy * g4v
        df_ref[...] = (r4 * (dfn - fn * _mean(dfn * fn))).astype(BF)

    row = pl.BlockSpec((tm, D), lambda i: (i, 0))
    return pl.pallas_call(
        body, name="fwd_mlp_loss", grid=(T // tm,),
        in_specs=[row, pl.BlockSpec((NCHIP, D, fk), lambda i: (0, 0, 0)), pl.BlockSpec((DF, D), lambda i: (0, 0)),
                  row, row, pl.BlockSpec((1, D), lambda i: (0, 0))],
        out_specs=[pl.BlockSpec((tm, DF), lambda i: (i, 0)), pl.BlockSpec((1, 1), lambda i: (0, 0)), row, row,
                   pl.BlockSpec((1, D), lambda i: (0, 0))],
        out_shape=_hbm_out([jax.ShapeDtypeStruct((T, DF), BF), jax.ShapeDtypeStruct((1, 1), F32),
                            jax.ShapeDtypeStruct((T, D), F32), jax.ShapeDtypeStruct((T, D), BF),
                            jax.ShapeDtypeStruct((1, D), F32)]),
        compiler_params=_cp(56),
    )(*_hbm(h2, w_ff1, w_ff2, x2, target, g4))


def _bwd_mlp_x(df, a1, w_ff1, w_ff2):
    tm = 512
    fk = DF // NCHIP

    def body(df_ref, a1_ref, w1_ref, w2_ref, dh_ref, df1_ref):
        df = df_ref[...]
        dh = None
        for k in range(NCHIP):
            cols = slice(k * fk, (k + 1) * fk)
            dact = _mm_nt(df, w2_ref[cols, :])
            df1 = (dact * (2.0 * a1_ref[:, cols].astype(F32))).astype(BF)
            df1_ref[:, cols] = df1
            part = _mm_nt(df1, w1_ref[k])
            dh = part if dh is None else dh + part
        dh_ref[...] = dh

    return pl.pallas_call(
        body, name="bwd_mlp_x", grid=(T // tm,),
        in_specs=[pl.BlockSpec((tm, D), lambda i: (i, 0)),
                  pl.BlockSpec((tm, DF), lambda i: (i, 0)),
                  pl.BlockSpec((NCHIP, D, fk), lambda i: (0, 0, 0)),
                  pl.BlockSpec((DF, D), lambda i: (0, 0))],
        out_specs=[pl.BlockSpec((tm, D), lambda i: (i, 0)), pl.BlockSpec((tm, DF), lambda i: (i, 0))],
        out_shape=_hbm_out([jax.ShapeDtypeStruct((T, D), F32), jax.ShapeDtypeStruct((T, DF), BF)]),
        compiler_params=_cp(56),
    )(*_hbm(df, a1, w_ff1, w_ff2))


def _bwd_mlp_w(df, h2, a1, df1):
    fc = 512
    per = (DF // NCHIP) // fc

    def body(df_ref, h_ref, a1_ref, df1_ref, dw1_ref, dw2_ref):
        a1 = a1_ref[...].astype(F32)
        dw2_ref[...] = _mm_tn((a1 * a1).astype(BF), df_ref[...]).astype(BF)
        dw1_ref[0] = _mm_tn(h_ref[...], df1_ref[...]).astype(BF)

    return pl.pallas_call(
        body, name="bwd_mlp_w", grid=(DF // fc,),
        in_specs=[pl.BlockSpec((T, D), lambda j: (0, 0)),
                  pl.BlockSpec((T, D), lambda j: (0, 0)),
                  pl.BlockSpec((T, fc), lambda j: (0, j)),
                  pl.BlockSpec((T, fc), lambda j: (0, j))],
        out_specs=[pl.BlockSpec((1, D, fc), lambda j: (j // per, 0, j % per)),
                   pl.BlockSpec((fc, D), lambda j: (j, 0))],
        out_shape=_hbm_out([jax.ShapeDtypeStruct((NCHIP, D, DF // NCHIP), BF),
                            jax.ShapeDtypeStruct((DF, D), BF)]),
        compiler_params=_cp(56),
    )(*_hbm(df, h2, a1, df1))


def _bwd_merge(dh2, dy, x2, m, bra, brb, proj, b_gate, g2, g3, w_lru_up, w_pool_up, w_o, stages=()):
    tm = 256
    cpu = D // NCHIP

    def body(dh2_ref, dy_ref, x2_ref, m_ref, bra_ref, brb_ref, p0, p1, p2, p3, bg_ref,
             g2_ref, g3_ref, wl_ref, wp_ref, wo_ref,
             dx_ref, dgt_ref, dyl_ref, dyp_ref, dm_ref, dbra_ref, dbrb_ref, dg2_ref, dg3_ref, dbg_ref):
        first = pl.program_id(0) == 0
        x2 = x2_ref[...]
        r3 = lax.rsqrt(_mean(x2 * x2) + NORM_EPS)
        x2n = x2 * r3
        dh2 = dh2_ref[...]
        t3 = dh2 * g3_ref[...]
        dx2 = dy_ref[...] + r3 * (t3 - x2n * _mean(t3 * x2n))
        dx_ref[...] = dx2
        _acc(dg3_ref, _colsum(dh2 * x2n), first)
        m = m_ref[...]
        r2 = lax.rsqrt(_mean(m * m) + NORM_EPS)
        mn = m * r2
        _acc(dg2_ref, _colsum(dx2 * mn), first)
        dmn = dx2 * g2_ref[...]
        dm = (r2 * (dmn - mn * _mean(dmn * mn))).astype(BF)
        dm_ref[...] = dm
        dmrg = _mm_nt(dm, wo_ref[...])
        bg = bg_ref[...]
        ga = _sigmoid(jnp.concatenate([p0[...], p1[...]], axis=1) + bg[:, :D])
        gb = _sigmoid(jnp.concatenate([p2[...], p3[...]], axis=1) + bg[:, D:])
        dga = dmrg * bra_ref[...].astype(F32) * (ga * (1.0 - ga))
        dgb = dmrg * brb_ref[...].astype(F32) * (gb * (1.0 - gb))
        dgt_ref[:, :D] = dga.astype(BF)
        dgt_ref[:, D:] = dgb.astype(BF)
        _acc(dbg_ref, jnp.concatenate([_colsum(dga), _colsum(dgb)], axis=1), first)
        dbra = (dmrg * ga).astype(BF)
        dbrb = (dmrg * gb).astype(BF)
        dbra_ref[...] = dbra
        dbrb_ref[...] = dbrb
        dyl_ref[...] = _mm_nt(dbra, wl_ref[...])
        dyp = None
        for k in range(NCHIP):
            part = _mm_nt(dbrb[:, k * cpu:(k + 1) * cpu], wp_ref[k])
            dyp = part if dyp is None else dyp + part
        dyp_ref[...] = dyp

    row = lambda w: pl.BlockSpec((tm, w), lambda i: (i, 0))
    full2 = lambda a, b: pl.BlockSpec((a, b), lambda i: (0, 0))
    wp_spec = pl.BlockSpec((NCHIP, DP, cpu), lambda i: (0, 0, 0))
    return _call(
        body, name="bwd_merge", grid=(T // tm,),
        in_specs=[row(D)] * 6 + _gate_specs(tm) +
                 [full2(1, 2 * D), full2(1, D), full2(1, D), full2(DR, D), wp_spec, full2(D, D)],
        out_specs=[row(D), row(2 * D), row(DR), row(DP), row(D), row(D), row(D),
                   full2(1, D), full2(1, D), full2(1, 2 * D)],
        out_shape=[jax.ShapeDtypeStruct((T, D), F32), jax.ShapeDtypeStruct((T, 2 * D), BF),
                   jax.ShapeDtypeStruct((T, DR), F32), jax.ShapeDtypeStruct((T, DP), F32),
                   jax.ShapeDtypeStruct((T, D), BF), jax.ShapeDtypeStruct((T, D), BF),
                   jax.ShapeDtypeStruct((T, D), BF),
                   jax.ShapeDtypeStruct((1, D), F32), jax.ShapeDtypeStruct((1, D), F32),
                   jax.ShapeDtypeStruct((1, 2 * D), F32)],
        vmem=56, args=[dh2, dy, x2, m, bra, brb, proj, proj, proj, proj, b_gate, g2, g3, w_lru_up, w_pool_up, w_o],
        stages=stages)


def _dw_merge(mrg, dm, ylru, dbra, ypool, dbrb, stages=()):
    nb = NCHIP
    rb, pb, cpu = D // nb, DP // nb, D // NCHIP

    def body(mrg_ref, dm_ref, yl_ref, dbra_ref, yp_ref, dbrb_ref, dwo_ref, dwl_ref, dwp_ref):
        dwo_ref[...] = _mm_tn(mrg_ref[...], dm_ref[...]).astype(BF)
        dwl_ref[...] = _mm_tn(yl_ref[...], dbra_ref[...]).astype(BF)
        dwp = _mm_tn(yp_ref[...], dbrb_ref[...]).astype(BF)
        for k in range(NCHIP):
            dwp_ref[k] = dwp[:, k * cpu:(k + 1) * cpu]

    cols = lambda w: pl.BlockSpec((T, w), lambda r: (0, r))
    whole = pl.BlockSpec((T, D), lambda r: (0, 0))
    return _call(
        body, name="dw_merge", grid=(nb,),
        in_specs=[cols(rb), whole, cols(rb), whole, cols(pb), whole],
        out_specs=[pl.BlockSpec((rb, D), lambda r: (r, 0)), pl.BlockSpec((rb, D), lambda r: (r, 0)),
                   pl.BlockSpec((NCHIP, pb, cpu), lambda r: (0, r, 0))],
        out_shape=[jax.ShapeDtypeStruct((D, D), BF), jax.ShapeDtypeStruct((DR, D), BF),
                   jax.ShapeDtypeStruct((NCHIP, DP, cpu), BF)],
        vmem=56, args=[mrg, dm, ylru, dbra, ypool, dbrb], stages=stages)


def _bwd_lru(proj, h, dylru, conv_w, conv_b, wa, ba, wx, bx, lam, stages=()):
    def body(xp_ref, g_ref, h_ref, dy_ref, cw_ref, cb_ref, wa_ref, ba_ref, wx_ref, bx_ref, lam_ref,
             dxp_ref, dg_ref, dcw_ref, dcb_ref, dwa_ref, dba_ref, dwx_ref, dbx_ref, dlam_ref, a_s, b_s, l_s):
        xp = xp_ref[...]
        cw = cw_ref[...]
        lam = lam_ref[...]
        xc, x1, x2, x3 = _conv(xp, cw, cb_ref[...])
        wa, wx = wa_ref[0], wx_ref[0]
        xcb, r, ii, sp, a, mult = _lru_gates(xc, wa, ba_ref[...], wx, bx_ref[...], lam)
        g = g_ref[...]
        gel, dgel = _gelu_parts(g)
        h = h_ref[...]
        dy = dy_ref[...]
        dg_ref[...] = (dy * h * dgel).astype(BF)
        _tile_scan(_su(a, 1, 0.0), dy * gel, a_s, b_s, l_s, reverse=True)
        b = l_s[...]
        da = b * _sd(h, 1, 0.0)
        dmult = b * (ii * xc)
        dii = b * (mult * xc)
        dxc = b * (mult * ii)
        dla = da * a - dmult * ((a * a) / mult)
        dr = dla * ((-LRU_C) * sp)
        dsp = _colsum(dla * ((-LRU_C) * r))
        dlam_ref[...] = -dsp / (1.0 + jnp.exp(lam))
        dzr = dr * (r * (1.0 - r))
        dzi = dii * (ii * (1.0 - ii))
        dzrb, dzib = dzr.astype(BF), dzi.astype(BF)
        dxc = dxc + _mm_nt(dzrb, wa) + _mm_nt(dzib, wx)
        dwa_ref[0] = _mm_tn(xcb, dzrb)
        dwx_ref[0] = _mm_tn(xcb, dzib)
        dba_ref[...] = _colsum(dzr)
        dbx_ref[...] = _colsum(dzi)
        dcb_ref[...] = _colsum(dxc)
        dcw_ref[...] = jnp.concatenate([_colsum(dxc * x3), _colsum(dxc * x2), _colsum(dxc * x1),
                                        _colsum(dxc * xp)], axis=0)
        dxp = cw[3:4] * dxc + cw[2:3] * _su(dxc, 1) + cw[1:2] * _su(dxc, 2) + cw[0:1] * _su(dxc, 3)
        dxp_ref[...] = dxp.astype(BF)

    blk = pl.BlockSpec((T, CB), lambda j: (0, j))
    wsp = pl.BlockSpec((1, CB, CB), lambda j: (j, 0, 0))
    return _call(
        body, name="bwd_lru", grid=(NG,),
        in_specs=[blk, pl.BlockSpec((T, CB), lambda j: (0, NG + j)), blk, blk,
                  pl.BlockSpec((4, CB), lambda j: (0, j)), _vec_spec(), wsp, _vec_spec(), wsp, _vec_spec(),
                  _vec_spec()],
        out_specs=[blk, blk, pl.BlockSpec((4, CB), lambda j: (0, j)), _vec_spec(), wsp, _vec_spec(), wsp,
                   _vec_spec(), _vec_spec()],
        out_shape=[jax.ShapeDtypeStruct((T, DR), BF), jax.ShapeDtypeStruct((T, DR), BF),
                   jax.ShapeDtypeStruct((4, DR), F32), jax.ShapeDtypeStruct((1, DR), F32),
                   jax.ShapeDtypeStruct((NG, CB, CB), F32), jax.ShapeDtypeStruct((1, DR), F32),
                   jax.ShapeDtypeStruct((NG, CB, CB), F32), jax.ShapeDtypeStruct((1, DR), F32),
                   jax.ShapeDtypeStruct((1, DR), F32)],
        vmem=56, args=[proj, proj, h, dylru, conv_w, conv_b, wa, ba, wx, bx, lam], stages=stages,
        scratch=[pltpu.VMEM((T, CB), F32)] * 3)


def _bwd_pool(proj, dypool, pool_w, pool_scale):
    def body(xp_ref, dy_ref, pw_ref, sc_ref, dx_ref, dw_ref, dsc_ref):
        for g, w in enumerate(POOL_WINDOWS):
            cols = slice(g * PG, (g + 1) * PG)
            cnt = _pool_cnt(w)
            x = xp_ref[:, cols]
            pb = (_pool_window(x, g + 1, _sd) / cnt - x).astype(BF)
            wg = pw_ref[g]
            dy = dy_ref[:, cols]
            dsc_ref[:, cols] = _colsum(dy * _mm(pb, wg))
            dyp = (dy * sc_ref[:, cols]).astype(BF)
            dw_ref[g] = _mm_tn(pb, dyp)
            dp = _mm_nt(dyp, wg)
            dx_ref[:, cols] = (_pool_window(dp / cnt, g + 1, _su) - dp).astype(BF)

    return pl.pallas_call(
        body, name="bwd_pool", grid=(1,),
        in_specs=[pl.BlockSpec((T, DP), lambda i: (0, 2 * DR // DP)),
                  pl.BlockSpec((T, DP), lambda i: (0, 0)),
                  pl.BlockSpec((4, PG, PG), lambda i: (0, 0, 0)),
                  pl.BlockSpec((1, DP), lambda i: (0, 0))],
        out_specs=[pl.BlockSpec((T, DP), lambda i: (0, 0)),
                   pl.BlockSpec((4, PG, PG), lambda i: (0, 0, 0)),
                   pl.BlockSpec((1, DP), lambda i: (0, 0))],
        out_shape=_hbm_out([jax.ShapeDtypeStruct((T, DP), BF), jax.ShapeDtypeStruct((4, PG, PG), F32),
                            jax.ShapeDtypeStruct((1, DP), F32)]),
        compiler_params=_cp(48),
    )(*_hbm(proj, dypool, pool_w, pool_scale))


PART_COLS = (DR, DR, DP, 2 * D)


def _shard_pieces():
    starts = [sum(PART_COLS[:p]) for p in range(len(PART_COLS))]
    shards = []
    for k in range(NCHIP):
        lo, hi = k * CW_IN, (k + 1) * CW_IN
        shards.append([(p, max(lo, s) - s, min(hi, s + wd) - s, max(lo, s) - lo)
                       for p, (s, wd) in enumerate(zip(starts, PART_COLS)) if max(lo, s) < min(hi, s + wd)])
    return shards


def _col_chunks(width, most=512):
    n = -(-width // most)
    step = -(-width // (n * LANE)) * LANE
    return [(lo, min(lo + step, width)) for lo in range(0, width, step)]


def _bwd_inproj_w(h1, parts, after):
    flat = [(k, *piece) for k, pieces in enumerate(_shard_pieces()) for piece in pieces]

    def body(h_hbm, p0, p1, p2, p3, after_ref, dw_hbm, h_v, dw_v, *rest):
        bufs, sem_in, sem_out = rest[:len(flat)], rest[len(flat)], rest[len(flat) + 1]
        part_refs = (p0, p1, p2, p3)
        loads = [pltpu.make_async_copy(h_hbm, h_v, sem_in.at[0])]
        for i, (k, p, a, b, c0) in enumerate(flat):
            loads.append(pltpu.make_async_copy(part_refs[p].at[:, pl.ds(a, b - a)], bufs[i], sem_in.at[1 + i]))
        for cp in loads:
            cp.start()
        loads[0].wait()
        stores = []
        for i, (k, p, a, b, c0) in enumerate(flat):
            loads[1 + i].wait()
            for lo, hi in _col_chunks(b - a):
                dw_v[k, :, c0 + lo:c0 + hi] = _mm_tn(h_v[...], bufs[i][:, lo:hi]).astype(BF)
            if i + 1 == len(flat) or flat[i + 1][0] != k:
                stores.append(pltpu.make_async_copy(dw_v.at[k], dw_hbm.at[k], sem_out.at[k]))
                stores[-1].start()
        for cp in stores:
            cp.wait()

    scratch = [pltpu.VMEM((T, D), BF), pltpu.VMEM((NCHIP, D, CW_IN), BF)]
    scratch += [pltpu.VMEM((T, b - a), parts[p].dtype) for k, p, a, b, c0 in flat]
    scratch += [pltpu.SemaphoreType.DMA((1 + len(flat),)), pltpu.SemaphoreType.DMA((NCHIP,))]
    return pl.pallas_call(
        body, name="bwd_inproj_w", in_specs=[ANY] * 6, out_specs=ANY, scratch_shapes=scratch,
        out_shape=pltpu.HBM((NCHIP, D, CW_IN), BF), compiler_params=_cp(48),
    )(*_hbm(h1, *parts), after)


def _bwd_inproj_x(parts, w_in, x, dxres, g1, stages=()):
    tm = 512

    def body(p0, p1, p2, p3, w_ref, x_ref, dr_ref, g_ref, dx_ref, dg_ref):
        part_refs = (p0, p1, p2, p3)
        dh = None
        for k, pieces in enumerate(_shard_pieces()):
            for p, a, b, c0 in pieces:
                part = _mm_nt(part_refs[p][:, a:b], w_ref[k, :, c0:c0 + b - a])
                dh = part if dh is None else dh + part
        xv = x_ref[...]
        r = lax.rsqrt(_mean(xv * xv) + NORM_EPS)
        xn = xv * r
        t = dh * g_ref[...]
        dx_ref[...] = dr_ref[...] + r * (t - xn * _mean(t * xn))
        _acc(dg_ref, _colsum(dh * xn), pl.program_id(0) == 0)

    row = pl.BlockSpec((tm, D), lambda i: (i, 0))
    vec = pl.BlockSpec((1, D), lambda i: (0, 0))
    return _call(
        body, name="bwd_inproj_x", grid=(T // tm,),
        in_specs=[pl.BlockSpec((tm, wd), lambda i: (i, 0)) for wd in PART_COLS] +
                 [pl.BlockSpec((NCHIP, D, CW_IN), lambda i: (0, 0, 0)), row, row, vec],
        out_specs=[row, vec],
        out_shape=[jax.ShapeDtypeStruct((T, D), F32), jax.ShapeDtypeStruct((1, D), F32)],
        vmem=56, args=[*parts, w_in, x, dxres, g1], stages=stages)[0]


def _place():
    x, y, c = lax.axis_index("x"), lax.axis_index("y"), lax.axis_index("c")
    chips = [(1 - x, y), (x, 1 - y), (1 - x, 1 - y)]
    return x, y, c, chips


def _rcopy(src, dst, ssem, rsem, dev):
    return pltpu.make_async_remote_copy(src_ref=src, dst_ref=dst, send_sem=ssem, recv_sem=rsem,
                                        device_id=dev, device_id_type=MESH_ID)


def _sds(a):
    return jax.ShapeDtypeStruct(a.shape, a.dtype)


def _sem2(n, m):
    return [pltpu.SemaphoreType.DMA((n * m,)), pltpu.SemaphoreType.DMA((n * m,))]


ALL = (0, 1, 1)


def _piece(ref, k, half, part):
    hr = ref.shape[1] // 2
    r0, r1 = hr * part[0] // part[2], hr * part[1] // part[2]
    return ref.at[k, pl.ds(half * hr + r0, r1 - r0), :]


def _gather(fulls, ici=(), d2d=()):
    n = len(fulls)
    ici, d2d = list(ici), list(d2d)
    pieces = [("ici", i, part) for i, part in ici] + [("d2d", i, part) for i, part in d2d]

    def copies(outs, sems):
        x, y, c, chips = _place()
        me = 2 * x + y
        sib = (x, y, 1 - c)
        send, recv = [], []
        for q, (kind, i, part) in enumerate(pieces):
            for j, chip in enumerate(chips):
                k, s = 2 * chip[0] + chip[1], 3 * q + j
                if kind == "ici":
                    mine, theirs, dev = _piece(outs[i], me, c, part), _piece(outs[i], k, c, part), (*chip, c)
                else:
                    mine, theirs, dev = _piece(outs[i], k, c, part), _piece(outs[i], k, 1 - c, part), sib
                send.append(_rcopy(mine, mine, sems[0].at[s], sems[1].at[s], dev))
                recv.append(_rcopy(theirs, theirs, sems[0].at[s], sems[1].at[s], dev))
        return send, recv

    def start(ins, outs, sems):
        for cp in copies(outs, sems)[0]:
            cp.start()

    def finish(ins, outs, sems):
        send, recv = copies(outs, sems)
        for cp in recv:
            cp.wait_recv()
        for cp in send:
            cp.wait_send()

    sems = [pltpu.SemaphoreType.DMA((3 * len(pieces),)), pltpu.SemaphoreType.DMA((3 * len(pieces),))]
    return _Stage(fulls, [_sds(f) for f in fulls], {i: i for i in range(n)}, sems, start, finish)


def _gather_whole(v):
    def copies(ins, outs, sems):
        x, y, c, chips = _place()
        me = 2 * x + y
        send = [_rcopy(ins[0], outs[0].at[me], sems[0].at[j], sems[1].at[j], (*chip, c))
                for j, chip in enumerate(chips)]
        recv = [_rcopy(ins[0], outs[0].at[2 * chip[0] + chip[1]], sems[0].at[j], sems[1].at[j], (*chip, c))
                for j, chip in enumerate(chips)]
        return send, recv

    def start(ins, outs, sems):
        for cp in copies(ins, outs, sems)[0]:
            cp.start()

    def finish(ins, outs, sems):
        send, recv = copies(ins, outs, sems)
        for cp in recv:
            cp.wait_recv()
        for cp in send:
            cp.wait_send()

    return _Stage([v], [jax.ShapeDtypeStruct((NCHIP,) + v.shape, v.dtype)], {},
                  [pltpu.SemaphoreType.DMA((3,)), pltpu.SemaphoreType.DMA((3,))], start, finish)


def _to_sibling(srcs):
    n = len(srcs)

    def copies(ins, outs, sems):
        x, y, c, _ = _place()
        sib = (x, y, 1 - c)
        return [_rcopy(ins[i].at[:, 1 - c] if srcs[i].ndim == 4 else ins[i], outs[i], sems[0].at[i], sems[1].at[i], sib)
                for i in range(n)]

    def start(ins, outs, sems):
        for cp in copies(ins, outs, sems):
            cp.start()

    def finish(ins, outs, sems):
        for cp in copies(ins, outs, sems):
            cp.wait()

    shapes = [jax.ShapeDtypeStruct((NCHIP,) + s.shape[2:] if s.ndim == 4 else s.shape, s.dtype) for s in srcs]
    return _Stage(srcs, shapes, {}, [pltpu.SemaphoreType.DMA((n,)), pltpu.SemaphoreType.DMA((n,))], start, finish)


def _to_chips(srcs, parts=None, lands=None):
    n = len(srcs)
    parts = [ALL] * n if parts is None else parts
    lands = [None] * n if lands is None else lands
    given = [i for i in range(n) if lands[i] is not None]

    def rows(ref, i):
        hr = srcs[i].shape[1]
        r0, r1 = hr * parts[i][0] // parts[i][2], hr * parts[i][1] // parts[i][2]
        return ref.at[pl.ds(r0, r1 - r0), :]

    def copies(ins, outs, sems):
        x, y, c, chips = _place()
        me = 2 * x + y
        return [_rcopy(rows(ins[i].at[2 * chip[0] + chip[1]] if srcs[i].shape[0] == NCHIP else ins[i].at[c], i),
                       rows(outs[i].at[me], i), sems[0].at[3 * i + j], sems[1].at[3 * i + j], (*chip, c))
                for i in range(n) for j, chip in enumerate(chips)]

    def start(ins, outs, sems):
        for cp in copies(ins, outs, sems):
            cp.start()

    def finish(ins, outs, sems):
        for cp in copies(ins, outs, sems):
            cp.wait()

    shapes = [jax.ShapeDtypeStruct((NCHIP,) + s.shape[1:], s.dtype) for s in srcs]
    alias = {n + q: i for q, i in enumerate(given)}
    return _Stage(list(srcs) + [lands[i] for i in given], shapes, alias, _sem2(n, 3), start, finish)


HBM_REF = pl.BlockSpec(memory_space=pltpu.HBM)
SEM_REF = pl.BlockSpec(memory_space=pltpu.SEMAPHORE)
DATAFLOW = pltpu.SideEffectType.DATAFLOW_SIDE_EFFECTING


def _after(x):
    return _Stage([x], [], {}, [], lambda *a: None, lambda *a: None)


class _Flight:
    def __init__(self, stage, sems, bufs):
        self.stage, self.sems, self.bufs = stage, list(sems), list(bufs)

    def landed(self):
        st, n = self.stage, len(self.stage.operands)
        fresh = [j for j in range(len(st.out_shape)) if j not in st.alias.values()]
        back = {v: k for k, v in st.alias.items()}
        return [self.bufs[back[j]] if j in back else self.bufs[n + fresh.index(j)] for j in range(len(st.out_shape))]


def _split_call(name, finish=(), start=(), after=None):
    bufs, stage_bufs = [], []

    def slot(a):
        for i, b in enumerate(bufs):
            if b is a:
                return i
        bufs.append(a)
        return len(bufs) - 1

    fin_slots = [[slot(b) for b in fl.bufs] for fl in finish]
    for st in start:
        fresh = [lax.empty(o.shape, o.dtype) for j, o in enumerate(st.out_shape) if j not in st.alias.values()]
        stage_bufs.append([slot(a) for a in list(st.operands) + fresh])
    old_sems = [s for fl in finish for s in fl.sems]
    new_sems = [s for st in start for s in st.sems]
    nb, no, nn = len(bufs), len(old_sems), len(new_sems)

    def refs_of(st, slots, buf_refs):
        n = len(st.operands)
        ins = [buf_refs[i] for i in slots[:n]]
        fresh = [j for j in range(len(st.out_shape)) if j not in st.alias.values()]
        back = {v: k for k, v in st.alias.items()}
        outs = [ins[back[j]] if j in back else buf_refs[slots[n + fresh.index(j)]] for j in range(len(st.out_shape))]
        return ins, outs

    def body(*refs):
        buf_refs, sem_in = refs[:nb], refs[nb:nb + no]
        sem_out = refs[nb + no + (after is not None):][:nn]
        token = refs[-1]
        pos = 0
        for fl, slots in zip(finish, fin_slots):
            ins, outs = refs_of(fl.stage, slots, buf_refs)
            fl.stage.finish(ins, outs, sem_in[pos:pos + len(fl.sems)])
            pos += len(fl.sems)
        pos = 0
        for st, slots in zip(start, stage_bufs):
            ins, outs = refs_of(st, slots, buf_refs)
            st.start(ins, outs, sem_out[pos:pos + len(st.sems)])
            pos += len(st.sems)
        token[...] = jnp.zeros_like(token)

    res = pl.pallas_call(
        body, name=name,
        out_shape=tuple(new_sems) + tuple(pltpu.HBM(b.shape, b.dtype) for b in bufs) +
                  (jax.ShapeDtypeStruct((8, LANE), F32),),
        in_specs=(HBM_REF,) * nb + (SEM_REF,) * no + ((pl.BlockSpec(memory_space=pl.ANY),) if after is not None else ()),
        out_specs=(SEM_REF,) * nn + (HBM_REF,) * nb + (pl.BlockSpec(memory_space=pltpu.VMEM),),
        input_output_aliases={i: nn + i for i in range(nb)},
        compiler_params=pltpu.CompilerParams(has_side_effects=DATAFLOW),
    )(*_hbm(*bufs), *old_sems, *([after] if after is not None else []))
    sems, thru, token = res[:nn], res[nn:nn + nb], res[-1]
    for fl, slots in zip(finish, fin_slots):
        fl.bufs = [thru[i] for i in slots]
    flights, pos = [], 0
    for st, slots in zip(start, stage_bufs):
        flights.append(_Flight(st, sems[pos:pos + len(st.sems)], [thru[i] for i in slots]))
        pos += len(st.sems)
    return flights, token


def _share(pairs):
    n = len(pairs)

    def start(ins, outs, sems):
        x, y, c, _ = _place()
        for i in range(n):
            _rcopy(outs[i].at[c], outs[i].at[c], sems[0].at[i], sems[1].at[i], (x, y, 1 - c)).start()

    def finish(ins, outs, sems):
        x, y, c, _ = _place()
        for i in range(n):
            _rcopy(outs[i].at[c], outs[i].at[c], sems[0].at[i], sems[1].at[i], (x, y, 1 - c)).wait_send()
            _rcopy(outs[i].at[1 - c], outs[i].at[1 - c], sems[0].at[i], sems[1].at[i], (x, y, 1 - c)).wait_recv()

    return _Stage(pairs, [_sds(p) for p in pairs], {i: i for i in range(n)},
                  [pltpu.SemaphoreType.DMA((n,)), pltpu.SemaphoreType.DMA((n,))], start, finish)


def _row_block(rows, cols, itemsize=4, target=2 * MIB):
    br = rows
    while br * cols * itemsize > target and br % 32 == 0:
        br //= 2
    return br


def _cast_place(w, chip_idx, name):
    rows, cols = w.shape
    br = _row_block(rows, cols)

    def body(k_ref, w_ref, o_ref):
        o_ref[0] = w_ref[...].astype(BF)

    return _call(
        body, name=name, grid=(rows // br,), prefetch=chip_idx,
        in_specs=[pl.BlockSpec((br, cols), lambda r, k: (r, 0))],
        out_specs=[pl.BlockSpec((1, br, cols), lambda r, k: (k[0], r, 0))],
        out_shape=[jax.ShapeDtypeStruct((NCHIP, rows, cols), BF)], vmem=32, args=[w])[0][0]


def _cast_place_multi(ws, chip_idx, stages=()):
    br = 128
    nblk = [a.shape[0] // br for a in ws]
    starts = [sum(nblk[:i]) for i in range(len(ws))]

    def body(k_ref, *refs):
        r = pl.program_id(0)
        for i in range(len(ws)):
            @pl.when(jnp.logical_and(r >= starts[i], r < starts[i] + nblk[i]))
            def _(i=i):
                refs[len(ws) + i][0] = refs[i][...].astype(BF)

    def at(i):
        return functools.partial(lambda r, s, nb: jnp.clip(r - s, 0, nb - 1), s=starts[i], nb=nblk[i])

    outs, landed = _call(
        body, name="cast_rest", grid=(sum(nblk),), prefetch=chip_idx,
        in_specs=[pl.BlockSpec((br, a.shape[1]), functools.partial(lambda r, k, f: (f(r), 0), f=at(i)))
                  for i, a in enumerate(ws)],
        out_specs=[pl.BlockSpec((1, br, a.shape[1]), functools.partial(lambda r, k, f: (k[0], f(r), 0), f=at(i)))
                   for i, a in enumerate(ws)],
        out_shape=[jax.ShapeDtypeStruct((NCHIP,) + a.shape, BF) for a in ws], vmem=32, args=list(ws), stages=stages)
    return outs, landed


def _add_sibling(g, land, cidx, name, stages=()):
    _, _, hr, cols = g.shape
    br = _row_block(hr, cols)

    def body(c_ref, g_ref, l_ref, o_ref):
        o_ref[...] = (g_ref[0, 0].astype(F32) + l_ref[0].astype(F32)).astype(BF)[None]

    outs, st = _call(
        body, name=name, grid=(NCHIP, hr // br), prefetch=cidx,
        in_specs=[pl.BlockSpec((1, 1, br, cols), lambda k, r, c: (k, c[0], r, 0)),
                  pl.BlockSpec((1, br, cols), lambda k, r, c: (k, r, 0))],
        out_specs=[pl.BlockSpec((1, br, cols), lambda k, r, c: (k, r, 0))],
        out_shape=[jax.ShapeDtypeStruct((NCHIP, hr, cols), BF)], vmem=32, args=[g, land], stages=stages)
    return outs[0], st


def _add_sibling_multi(gs, lands, cidx, name):
    n = len(gs)
    brs = [_row_block(g.shape[2], g.shape[3]) for g in gs]
    nrb = [g.shape[2] // b for g, b in zip(gs, brs)]
    nblk = [NCHIP * q for q in nrb]
    starts = [sum(nblk[:i]) for i in range(n)]

    def body(c_ref, *refs):
        r = pl.program_id(0)
        for i in range(n):
            g_ref, l_ref, o_ref = refs[2 * i], refs[2 * i + 1], refs[2 * n + i]

            @pl.when(jnp.logical_and(r >= starts[i], r < starts[i] + nblk[i]))
            def _():
                o_ref[...] = (g_ref[0, 0].astype(F32) + l_ref[0].astype(F32)).astype(BF)[None]

    def at(i, r):
        q = jnp.clip(r - starts[i], 0, nblk[i] - 1)
        return q // nrb[i], q % nrb[i]

    def g_spec(i):
        return pl.BlockSpec((1, 1, brs[i], gs[i].shape[3]),
                            functools.partial(lambda r, c, i: (at(i, r)[0], c[0], at(i, r)[1], 0), i=i))

    def l_spec(i):
        return pl.BlockSpec((1, brs[i], gs[i].shape[3]),
                            functools.partial(lambda r, c, i: (at(i, r)[0], at(i, r)[1], 0), i=i))

    return _call(
        body, name=name, grid=(sum(nblk),), prefetch=cidx,
        in_specs=[s for i in range(n) for s in (g_spec(i), l_spec(i))], out_specs=[l_spec(i) for i in range(n)],
        out_shape=[jax.ShapeDtypeStruct(l.shape, BF) for l in lands], vmem=32,
        args=[a for i in range(n) for a in (gs[i], lands[i])])[0]


def _add_pair(a, b, name):
    rows, cols = a.shape

    def body(a_ref, b_ref, o_ref):
        o_ref[...] = a_ref[...] + b_ref[...]

    spec = pl.BlockSpec((rows, cols), lambda r: (0, 0))
    return _call(body, name=name, grid=(1,), in_specs=[spec, spec], out_specs=[spec], out_shape=[_sds(a)],
                 vmem=32, args=[a, b])[0][0]


def _add_chips(own, land, idx, name, stages=None):
    _, hr, cols = land.shape
    br = _row_block(hr, cols)

    def body(s_ref, a_ref, b_ref, c_ref, d_ref, o_ref):
        o_ref[...] = (a_ref[...].astype(F32) + b_ref[...].astype(F32)) + (c_ref[...].astype(F32) +
                                                                           d_ref[...].astype(F32))

    spec = lambda q: pl.BlockSpec((1, br, cols), functools.partial(lambda r, s, q: (s[q], r, 0), q=q))
    outs, landed = _call(
        body, name=name, grid=(hr // br,), prefetch=idx,
        in_specs=[spec(0), spec(1), spec(2), spec(3)], out_specs=[spec(4)],
        out_shape=[jax.ShapeDtypeStruct((2, hr, cols), F32)], vmem=48, args=[own, land, land, land],
        stages=stages or ())
    return outs[0] if stages is None else (outs[0], landed)


def _add_chips_multi(owns, lands, idx, name, stages=()):
    n = len(owns)
    brs = [_row_block(l.shape[1], l.shape[2]) for l in lands]
    nblk = [l.shape[1] // b for l, b in zip(lands, brs)]
    starts = [sum(nblk[:i]) for i in range(n)]

    def body(s_ref, *refs):
        r = pl.program_id(0)
        for i in range(n):
            a_ref, b_ref, c_ref, d_ref = refs[4 * i:4 * i + 4]
            o_ref = refs[4 * n + i]

            @pl.when(jnp.logical_and(r >= starts[i], r < starts[i] + nblk[i]))
            def _():
                o_ref[...] = (a_ref[...].astype(F32) + b_ref[...].astype(F32)) + (c_ref[...].astype(F32) +
                                                                                   d_ref[...].astype(F32))

    def spec(i, q):
        return pl.BlockSpec((1, brs[i], lands[i].shape[2]), functools.partial(
            lambda r, s, q, st, nb: (s[q], jnp.clip(r - st, 0, nb - 1), 0), q=q, st=starts[i], nb=nblk[i]))

    outs, landed = _call(
        body, name=name, grid=(sum(nblk),), prefetch=idx,
        in_specs=[spec(i, q) for i in range(n) for q in range(4)], out_specs=[spec(i, 4) for i in range(n)],
        out_shape=[jax.ShapeDtypeStruct((2,) + l.shape[1:], F32) for l in lands], vmem=48,
        args=[a for i in range(n) for a in (owns[i], lands[i], lands[i], lands[i])], stages=stages)
    return outs, landed


def _adamw_math(w, g, m, v):
    mn = ADAM_B1 * m + (1.0 - ADAM_B1) * g
    vn = ADAM_B2 * v + (1.0 - ADAM_B2) * (g * g)
    m_hat = mn / (1.0 - ADAM_B1 ** ADAM_STEP)
    v_hat = vn / (1.0 - ADAM_B2 ** ADAM_STEP)
    return -ADAM_LR * (m_hat / (jnp.sqrt(v_hat) + ADAM_EPS) + ADAM_WD * w), mn, vn


def _adamw(w, g, m, v, name, stages=()):
    rows, cols = w.shape
    br = _row_block(rows, cols)

    def body(w_ref, g_ref, m_ref, v_ref, go_ref, d_ref, mo_ref, vo_ref):
        gv = g_ref[...]
        go_ref[...] = gv
        d_ref[...], mo_ref[...], vo_ref[...] = _adamw_math(w_ref[...], gv, m_ref[...], v_ref[...])

    spec = pl.BlockSpec((br, cols), lambda r: (r, 0))
    return _call(body, name=name, grid=(rows // br,), in_specs=[spec] * 4, out_specs=[spec] * 4,
                 out_shape=[_sds(w)] * 4, vmem=56, args=[w, g, m, v], stages=stages)


def _adamw_multi(names, w, g, m, v, stages=()):
    cols = w[names[0]].shape[1]
    br = 128
    nblk = [w[n].shape[0] // br for n in names]
    starts = [sum(nblk[:i]) for i in range(len(names))]

    def body(*refs):
        r = pl.program_id(0)
        for i in range(len(names)):
            w_ref, g_ref, m_ref, v_ref = refs[4 * i:4 * i + 4]
            go_ref, d_ref, mo_ref, vo_ref = refs[4 * len(names) + 4 * i:4 * len(names) + 4 * i + 4]

            @pl.when(jnp.logical_and(r >= starts[i], r < starts[i] + nblk[i]))
            def _():
                gv = g_ref[...]
                go_ref[...] = gv
                d_ref[...], mo_ref[...], vo_ref[...] = _adamw_math(w_ref[...], gv, m_ref[...], v_ref[...])

    def spec(i):
        return pl.BlockSpec((br, cols), functools.partial(
            lambda r, s, nb: (jnp.clip(r - s, 0, nb - 1), 0), s=starts[i], nb=nblk[i]))

    outs, landed = _call(
        body, name="adamw_" + "_".join(names), grid=(sum(nblk),),
        in_specs=[spec(i) for i in range(len(names)) for _ in range(4)],
        out_specs=[spec(i) for i in range(len(names)) for _ in range(4)],
        out_shape=[_sds(w[n]) for n in names for _ in range(4)], vmem=56,
        args=[a[n] for n in names for a in (w, g, m, v)], stages=stages)
    return {n: outs[4 * i:4 * i + 4] for i, n in enumerate(names)}, landed


def _to_everyone(v):
    deltas = [(a, b, e) for a in (0, 1) for b in (0, 1) for e in (0, 1)][1:]

    def copies(ins, outs, sems):
        x, y, c, _ = _place()
        me = 4 * x + 2 * y + c
        flip = lambda p, f: 1 - p if f else p
        return [_rcopy(ins[0], outs[0].at[me], sems[0].at[q], sems[1].at[q], (flip(x, a), flip(y, b), flip(c, e)))
                for q, (a, b, e) in enumerate(deltas)]

    def start(ins, outs, sems):
        for cp in copies(ins, outs, sems):
            cp.start()

    def finish(ins, outs, sems):
        for cp in copies(ins, outs, sems):
            cp.wait()

    n = len(deltas)
    return _Stage([v], [jax.ShapeDtypeStruct((2 * NCHIP,) + v.shape, v.dtype)], {},
                  [pltpu.SemaphoreType.DMA((n,)), pltpu.SemaphoreType.DMA((n,))], start, finish)


SMALL_AT = {"norm_mix_pre": (0, 1, D), "norm_mix_post": (1, 1, D), "norm_mlp_pre": (2, 1, D),
            "norm_mlp_post": (3, 1, D), "b_gate": (4, 2, D), "conv_b": (6, 1, D), "lru_b_a": (7, 1, D),
            "lru_b_x": (8, 1, D), "lru_lambda": (9, 1, D), "pool_scale": (10, 1, DP)}
SMALL_SEPARATE = ["conv_w", "lru_w_a", "lru_w_x", "pool_w"]


def _adamw_small(small_sum, first_all, sep_grads, w, m, v):
    packed, sep = list(SMALL_AT), list(SMALL_SEPARATE)
    names = packed + sep

    def body(*refs):
        s_ref, a_ref, refs = refs[0], refs[1], refs[2:]
        g_sep, refs = refs[:len(sep)], refs[len(sep):]
        nn = len(names)
        w_r, m_r, v_r, refs = refs[:nn], refs[nn:2 * nn], refs[2 * nn:3 * nn], refs[3 * nn:]
        g_out, refs = refs[:len(packed)], refs[len(packed):]
        d_o, m_o, v_o = refs[:nn], refs[nn:2 * nn], refs[2 * nn:3 * nn]
        for i, n in enumerate(names):
            if i == 0:
                g = a_ref[0:1, :]
                for q in range(1, 2 * NCHIP):
                    g = g + a_ref[q:q + 1, :]
                g_out[i][...] = g
            elif n in SMALL_AT:
                r0, nr, nc = SMALL_AT[n]
                g = jnp.concatenate([s_ref[r0 + q:r0 + q + 1, :nc] for q in range(nr)], axis=1)
                g_out[i][...] = g
            else:
                g = g_sep[i - len(packed)][...]
            d_o[i][...], m_o[i][...], v_o[i][...] = _adamw_math(w_r[i][...], g, m_r[i][...], v_r[i][...])

    ws = [w[n] for n in names]
    res = pl.pallas_call(
        body, name="adamw_small",
        out_shape=[_sds(w[n]) for n in packed] + [_sds(a) for a in ws] * 3,
        compiler_params=_cp(32),
    )(*_hbm(small_sum, first_all, *sep_grads, *ws, *[m[n] for n in names], *[v[n] for n in names]))
    nn, npk = len(names), len(packed)
    grad = dict(zip(packed, res[:npk]))
    delta = dict(zip(names, res[npk:npk + nn]))
    new_m = dict(zip(names, res[npk + nn:npk + 2 * nn]))
    new_v = dict(zip(names, res[npk + 2 * nn:]))
    return grad, delta, new_m, new_v


W_NAMES = ["norm_mix_pre", "norm_mix_post", "norm_mlp_pre", "norm_mlp_post", "w_in", "b_gate", "conv_w", "conv_b",
           "lru_w_a", "lru_b_a", "lru_w_x", "lru_b_x", "lru_lambda", "pool_w", "pool_scale", "w_lru_up",
           "w_pool_up", "w_o", "w_ff1", "w_ff2"]
BIG = ["w_in", "w_lru_up", "w_pool_up", "w_o", "w_ff1", "w_ff2"]


def _block_diag(w):
    hd = w.shape[-1]
    per = CB // hd
    w4 = w.reshape(NG, per, hd, hd)
    eye = jnp.eye(per, dtype=w.dtype)
    return jnp.einsum("gpij,pq->gpiqj", w4, eye).reshape(NG, CB, CB)


def _block_diag_extract(d, hd):
    per = CB // hd
    d5 = d.reshape(NG, per, hd, per, hd)
    return jnp.stack([d5[:, p, :, p, :] for p in range(per)], axis=1).reshape(NG * per, hd, hd)


def _halves(g):
    return g.reshape(NCHIP, 2, g.size // (g.shape[-1] * 2 * NCHIP), g.shape[-1])


def kernel(x, norm_mix_pre, norm_mix_post, norm_mlp_pre, norm_mlp_post, w_in, b_gate, conv_w, conv_b, lru_w_a, lru_b_a, lru_w_x, lru_b_x, lru_lambda, pool_w, pool_scale, w_lru_up, w_pool_up, w_o, w_ff1, w_ff2, loss_target, m_norm_mix_pre, m_norm_mix_post, m_norm_mlp_pre, m_norm_mlp_post, m_w_in, m_b_gate, m_conv_w, m_conv_b, m_lru_w_a, m_lru_b_a, m_lru_w_x, m_lru_b_x, m_lru_lambda, m_pool_w, m_pool_scale, m_w_lru_up, m_w_pool_up, m_w_o, m_w_ff1, m_w_ff2, v_norm_mix_pre, v_norm_mix_post, v_norm_mlp_pre, v_norm_mlp_post, v_w_in, v_b_gate, v_conv_w, v_conv_b, v_lru_w_a, v_lru_b_a, v_lru_w_x, v_lru_b_x, v_lru_lambda, v_pool_w, v_pool_scale, v_w_lru_up, v_w_pool_up, v_w_o, v_w_ff1, v_w_ff2):
    args = dict(locals())
    two_d = lambda a: a.reshape(-1, a.shape[-1])
    w = {n: two_d(args[n]) for n in W_NAMES}
    mom = {n: two_d(args["m_" + n]) for n in W_NAMES}
    var = {n: two_d(args["v_" + n]) for n in W_NAMES}
    i32 = lambda val: jnp.asarray(val, jnp.int32)
    chip = i32(2 * lax.axis_index("x") + lax.axis_index("y"))
    core = i32(lax.axis_index("c"))
    cidx = core.reshape(1)
    zero = i32(0)
    hd = lru_w_a.shape[-1]
    xs, target = x[0], loss_target[0]
    g1, g2, g3, g4 = norm_mix_pre, norm_mix_post, norm_mlp_pre, norm_mlp_post

    mix = ["w_lru_up", "w_pool_up", "w_o"]
    full = {"w_in": _cast_place(w["w_in"], chip.reshape(1), "cast_w_in")}
    (fl_in, fl_conv), first = _split_call("gather_start_first", start=[
        _gather([full["w_in"]], ici=[(0, ALL)]), _gather_whole(w["conv_w"])])
    casts, _ = _cast_place_multi([w[n] for n in BIG[1:]], chip.reshape(1), stages=[_after(first)])
    full.update(zip(BIG[1:], casts))
    (fl_mix, fl_ff1, fl_ff2), started = _split_call("gather_start_rest", start=[
        _gather([full[n] for n in mix], ici=[(0, ALL), (1, ALL), (2, ALL)]),
        _gather([full["w_ff1"]], ici=[(0, ALL)]), _gather([full["w_ff2"]], ici=[(0, ALL)])])
    wa = _block_diag(lru_w_a[0]).astype(BF)
    wx = _block_diag(lru_w_x[0]).astype(BF)
    pw = pool_w[0].astype(BF)

    def to_sibling(name, flight, after=None):
        (fl,), passed = _split_call(name + "_pass", finish=[flight], after=after,
                                    start=[_gather(flight.landed(), d2d=[(i, ALL) for i in range(len(flight.bufs))])])
        passed_on.append(passed)
        return fl

    passed_on = []

    def arrived(name, flight, after=None):
        _split_call(name + "_done", finish=[flight], after=after)
        return flight.landed()

    idx_big = jnp.stack([chip, (chip + 1) % NCHIP, (chip + 2) % NCHIP, (chip + 3) % NCHIP, core])
    proj, h1 = _fwd_inproj_own(xs, g1, fl_in.bufs[0], idx_big, stages=[_after(started)])
    fl_in = to_sibling("gather_w_in", fl_in, after=h1)
    _split_call("gather_w_in_done", finish=[fl_in, fl_conv])
    (w_in_f,), (conv_all,) = fl_in.landed(), fl_conv.landed()
    full["w_in"] = w_in_f
    conv_all = lax.dynamic_update_slice(conv_all, w["conv_w"][None], (chip, zero, zero))
    conv_full = jnp.transpose(conv_all, (1, 0, 2)).reshape(4, DR)
    proj = _fwd_inproj_rest(h1, w_in_f, proj, idx_big)
    fl_mix = to_sibling("gather_mix", fl_mix, after=proj)
    (ylru, hs), _ = _fwd_lru(proj, conv_full, conv_b, wa, lru_b_a, wx, lru_b_x, lru_lambda,
                             stages=[_after(passed_on[-1])])
    got = arrived("gather_mix", fl_mix, after=ylru)
    fl_ff1 = to_sibling("gather_ff1", fl_ff1, after=ylru)
    w_lru_up_f, w_pool_up_f, w_o_f = got[0].reshape(DR, D), got[1], got[2].reshape(D, D)
    ypool = _fwd_pool(proj, pw, pool_scale)
    (x2, h2, m, mrg, bra, brb), _ = _fwd_merge(xs, ylru, ypool, proj, b_gate, g2, g3, w_lru_up_f, w_pool_up_f, w_o_f,
                                               stages=[_after(passed_on[-1])])
    fl_ff2 = to_sibling("gather_ff2", fl_ff2, after=h2)
    _split_call("gather_ff_done", finish=[fl_ff1, fl_ff2])
    (ff1,), (ff2,) = fl_ff1.landed(), fl_ff2.landed()
    ff2 = ff2.reshape(DF, D)
    a1, lossp, dy, df, dg4 = _fwd_mlp_loss(h2, ff1, ff2, x2, target, g4)

    dh2, df1 = _bwd_mlp_x(df, a1, ff1, ff2)
    dw_ff1, dw_ff2 = _bwd_mlp_w(df, h2, a1, df1)
    g_ff = [_halves(dw_ff1), _halves(dw_ff2)]
    (dxres, dgates, dylru, dypool, dm, dbra, dbrb, dg2, dg3, dbg), (l_ff,) = _bwd_merge(
        dh2, dy, x2, m, bra, brb, proj, b_gate, g2, g3, w_lru_up_f, w_pool_up_f, w_o_f, stages=[_to_sibling(g_ff)])
    p_ff = _add_sibling_multi(g_ff, l_ff, cidx, "add_sibling_ff")
    (fl_ff,), sent_ff = _split_call("reduce_ff_start", start=[_to_chips(p_ff)])
    (dw_o, dw_lru_up, dw_pool_up), _ = _dw_merge(mrg, dm, ylru, dbra, ypool, dbrb, stages=[_after(sent_ff)])
    g_mix = [_halves(dw_lru_up), _halves(dw_pool_up), _halves(dw_o)]
    (dxp, dgl, dcw, dcb, dwa, dba, dwx, dbx, dlam), (l_mix,) = _bwd_lru(
        proj, hs, dylru, conv_full, conv_b, wa, lru_b_a, wx, lru_b_x, lru_lambda, stages=[_to_sibling(g_mix)])
    p_mix = _add_sibling_multi(g_mix, l_mix, cidx, "add_sibling_mix")
    dxpool, dpw, dsc = _bwd_pool(proj, dypool, pw, pool_scale)
    dproj = [dxp, dgl, dxpool, dgates]
    small = jnp.concatenate([
        jnp.zeros((1, D), F32), dg2, dg3, dg4, dbg.reshape(2, D), dcb, dba, dbx, dlam,
        jnp.pad(dsc, ((0, 0), (0, D - DP))), jnp.pad(lossp, ((0, 0), (0, D - 1))), dcw,
        _block_diag_extract(dwa, hd).reshape(-1, D), _block_diag_extract(dwx, hd).reshape(-1, D),
        dpw.reshape(-1, D)], axis=0)
    (fl_mixr, fl_smalls), sent_mix = _split_call("reduce_mix_start", start=[_to_chips(p_mix), _to_sibling([small])])
    dw_in = _bwd_inproj_w(h1, dproj, sent_mix)
    _split_call("reduce_small_sibling_done", finish=[fl_smalls], after=dw_in)
    small, l_small = fl_smalls.bufs
    small2 = _add_pair(small, l_small, "add_sibling_small").reshape(2, SMALL_ROWS // 2, D)
    g_in = _halves(dw_in)
    done = ["w_ff1", "w_ff2"] + mix
    (fl_gin, fl_small), sib_started = _split_call("reduce_in_sibling_start", finish=[fl_ff, fl_mixr],
                                                  start=[_to_sibling([g_in]), _to_chips([small2])])
    p_ff1, p_ff2, c_ff1, c_ff2 = fl_ff.bufs
    p_mix, c_mix = fl_mixr.bufs[:3], fl_mixr.bufs[3:]
    pairs, _ = _add_chips_multi([p_ff1, p_ff2] + p_mix, [c_ff1, c_ff2] + c_mix, idx_big, "add_chips_done",
                                stages=[_after(sib_started)])
    _split_call("reduce_in_sibling_done", finish=[fl_gin], after=pairs[-1])
    g_in, l_in = fl_gin.bufs
    p_in = _add_sibling(g_in, l_in, cidx, "add_sibling_w_in")[0]
    (fl_pin,), token = _split_call("reduce_last_start", start=[_to_chips([p_in])])
    _split_call("reduce_small_done", finish=[fl_small], after=token)
    small2, c_small = fl_small.bufs
    own_small = lax.dynamic_index_in_dim(small2, core, 0, keepdims=True)
    c_small = lax.dynamic_update_slice(c_small, own_small, (chip, zero, zero))
    pair_small = _add_chips(c_small, c_small, jnp.stack([zero, zero + 1, zero + 2, zero + 3, core]), "add_chips_small")
    (fl_share,), shared_start = _split_call("reduce_share_start", start=[_share(pairs + [pair_small])])
    grad_x, dg1 = _bwd_inproj_x(dproj, full["w_in"], xs, dxres, g1, stages=[_after(shared_start)])
    _split_call("reduce_share_done", finish=[fl_share, fl_pin], after=dg1)
    shared, (p_in, c_in) = fl_share.landed(), fl_pin.bufs
    pairs, pair_small = shared[:-1], shared[-1]

    grads, delta, new_m, new_v = {}, {}, {}, {}
    for n, p in zip(done, pairs):
        grads[n] = p.reshape(-1, p.shape[-1])

    def update(n, stages=()):
        (grads[n], delta[n], new_m[n], new_v[n]), landed = _adamw(w[n], grads[n], mom[n], var[n], "adamw_" + n,
                                                                  stages=stages)
        return landed

    pair_in = _add_chips(p_in, c_in, idx_big, "add_chips_w_in")
    (fl_last, fl_dg1), last_start = _split_call("reduce_last_share_start", start=[_share([pair_in]), _to_everyone(dg1)])
    updated, _ = _adamw_multi(["w_ff1", "w_ff2", "w_o", "w_lru_up"], w, grads, mom, var, stages=[_after(last_start)])
    for n, (go, d, mo, vo) in updated.items():
        grads[n], delta[n], new_m[n], new_v[n] = go, d, mo, vo
    _split_call("reduce_last_share_done", finish=[fl_last, fl_dg1], after=new_v["w_lru_up"])
    (pair_in,), (dg1, dg1_all) = fl_last.landed(), fl_dg1.bufs
    dg1_all = lax.dynamic_update_slice(dg1_all, dg1[None], (2 * chip + core, zero, zero)).reshape(2 * NCHIP, D)
    grads["w_in"] = pair_in.reshape(-1, pair_in.shape[-1])
    update("w_pool_up")
    update("w_in")
    small_sum = pair_small.reshape(SMALL_ROWS, D)
    loss = 0.5 * small_sum[LOSS_ROW, 0]
    ccols = DR // NCHIP
    sep = [lax.dynamic_slice(small_sum[12:16], (zero, chip * ccols), (4, ccols)),
           small_sum[16:80].reshape(-1, hd), small_sum[80:144].reshape(-1, hd), small_sum[144:208].reshape(-1, PG)]
    g_s, d_s, m_s, v_s = _adamw_small(small_sum, dg1_all, sep, w, mom, var)
    grads.update(g_s)
    grads.update(dict(zip(SMALL_SEPARATE, sep)))
    delta.update(d_s)
    new_m.update(m_s)
    new_v.update(v_s)

    out = lambda d: [d[n].reshape(args[n].shape) for n in W_NAMES]
    return (loss, grad_x[None], *out(grads), *out(delta), *out(new_m), *out(new_v))
```

```python
import functools
import math

import jax
import jax.numpy as jnp
from jax import lax
from jax.experimental import pallas as pl
from jax.experimental.pallas import tpu as pltpu

F32 = jnp.float32
BF = jnp.bfloat16

T = 2048
D = 1024
DR = 1024
DP = 512
DF = 4096
DIN = 4608
NCHIP = 4
CW_IN = DIN // NCHIP
LANE = 128
CB = 128
NG = DR // CB
PG = 128
POOL_WINDOWS = (2, 4, 8, 16)
NORM_EPS = 1e-6
LRU_C = 8.0
GELU_C = math.sqrt(2.0 / math.pi)
ADAM_LR = 0.001
ADAM_B1 = 0.9
ADAM_B2 = 0.999
ADAM_EPS = 1e-08
ADAM_WD = 0.01
ADAM_STEP = 10
MESH_ID = pl.DeviceIdType.MESH
ANY = pl.BlockSpec(memory_space=pl.ANY)
SMALL_ROWS = 208
LOSS_ROW = 11
MIB = 1 << 20


def _cp(vmem_mib=None):
    if vmem_mib is None:
        return pltpu.CompilerParams()
    return pltpu.CompilerParams(vmem_limit_bytes=vmem_mib * MIB)


def _hbm(*arrays):
    return [pltpu.with_memory_space_constraint(a, pltpu.HBM) for a in arrays]


def _hbm_out(shapes):
    return [pltpu.HBM(s.shape, s.dtype) for s in shapes]


class _Stage:
    def __init__(self, operands, out_shape, alias, sems, start, finish):
        self.operands, self.out_shape, self.alias, self.sems = list(operands), list(out_shape), dict(alias), list(sems)
        self.start, self.finish = start, finish


def _call(body, *, name, grid, in_specs, out_specs, out_shape, args, vmem=None, stages=(), prefetch=None,
          scratch=()):
    nin, nout = len(in_specs), len(out_specs)
    npre = 0 if prefetch is None else 1
    st_args, st_shapes, st_sems, aliases = [], [], list(scratch), {}
    for st in stages:
        for k, v in st.alias.items():
            aliases[npre + nin + len(st_args) + k] = nout + len(st_shapes) + v
        st_args += st.operands
        st_shapes += st.out_shape
        st_sems += st.sems

    def wrapped(*refs):
        pre, refs = refs[:npre], refs[npre:]
        ins, pos = refs[:nin], nin
        st_ins = []
        for st in stages:
            st_ins.append(refs[pos:pos + len(st.operands)])
            pos += len(st.operands)
        outs, pos = refs[pos:pos + nout], pos + nout
        st_outs = []
        for st in stages:
            st_outs.append(refs[pos:pos + len(st.out_shape)])
            pos += len(st.out_shape)
        work, pos = refs[pos:pos + len(scratch)], pos + len(scratch)
        sems = []
        for st in stages:
            sems.append(refs[pos:pos + len(st.sems)])
            pos += len(st.sems)
        if stages:
            first = functools.reduce(jnp.logical_and, [pl.program_id(a) == 0 for a in range(len(grid))])

            @pl.when(first)
            def _():
                for st, a, b, s in zip(stages, st_ins, st_outs, sems):
                    st.start(a, b, s)

        body(*pre, *ins, *outs, *work)
        if stages:
            last = functools.reduce(jnp.logical_and, [pl.program_id(a) == g - 1 for a, g in enumerate(grid)])

            @pl.when(last)
            def _():
                for st, a, b, s in zip(stages, st_ins, st_outs, sems):
                    st.finish(a, b, s)

    all_in = list(in_specs) + [ANY] * len(st_args)
    all_out = list(out_specs) + [ANY] * len(st_shapes)
    kw = dict(has_side_effects=True) if stages else {}
    if vmem is not None:
        kw["vmem_limit_bytes"] = vmem * MIB
    if prefetch is None:
        gkw = dict(grid=grid, in_specs=all_in, out_specs=all_out, scratch_shapes=st_sems)
    else:
        gkw = dict(grid_spec=pltpu.PrefetchScalarGridSpec(
            num_scalar_prefetch=1, grid=grid, in_specs=all_in, out_specs=all_out, scratch_shapes=st_sems))
    res = pl.pallas_call(
        wrapped, name=name, out_shape=_hbm_out(list(out_shape) + st_shapes), input_output_aliases=aliases,
        compiler_params=pltpu.CompilerParams(**kw), **gkw,
    )(*([prefetch] if npre else []), *_hbm(*args, *st_args))
    outs, rest, st_res = list(res[:nout]), list(res[nout:]), []
    for st in stages:
        st_res.append(rest[:len(st.out_shape)])
        rest = rest[len(st.out_shape):]
    return outs, st_res


def _mm(a, b):
    return jnp.dot(a.astype(BF), b.astype(BF), preferred_element_type=F32)


def _mm_nt(a, b):
    return lax.dot_general(a.astype(BF), b.astype(BF), (((1,), (1,)), ((), ())),
                           preferred_element_type=F32)


def _mm_tn(a, b):
    return lax.dot_general(a.astype(BF), b.astype(BF), (((0,), (0,)), ((), ())),
                           preferred_element_type=F32)


def _rows(v):
    return lax.broadcasted_iota(jnp.int32, v.shape, 0)


def _sd(v, s, fill=0.0):
    return jnp.where(_rows(v) >= s, pltpu.roll(v, s, axis=0), fill)


def _su(v, s, fill=0.0):
    n = v.shape[0]
    return jnp.where(_rows(v) < n - s, pltpu.roll(v, n - s, axis=0), fill)


def _sigmoid(z):
    return 1.0 / (1.0 + jnp.exp(-z))


def _softplus(z):
    e = jnp.exp(-jnp.abs(z))
    u = 1.0 + e
    d = u - 1.0
    log1p = jnp.where(d == 0.0, e, jnp.log(u) * (e / jnp.where(d == 0.0, 1.0, d)))
    return jnp.maximum(z, 0.0) + log1p


def _mean(v):
    return jnp.mean(v, axis=-1, keepdims=True)


def _colsum(v):
    return jnp.sum(v, axis=0, keepdims=True)


def _acc(ref, val, first):
    @pl.when(first)
    def _():
        ref[...] = val

    @pl.when(jnp.logical_not(first))
    def _():
        ref[...] += val


def _conv(xp, cw, cb):
    x1, x2, x3 = _sd(xp, 1), _sd(xp, 2), _sd(xp, 3)
    xc = cb + cw[0:1] * x3 + cw[1:2] * x2 + cw[2:3] * x1 + cw[3:4] * xp
    return xc, x1, x2, x3


def _lru_gates(xc, wa, ba, wx, bx, lam):
    xcb = xc.astype(BF)
    r = _sigmoid(_mm(xcb, wa) + ba)
    ii = _sigmoid(_mm(xcb, wx) + bx)
    sp = _softplus(-lam)
    la = (-LRU_C) * r * sp
    a = jnp.exp(la)
    mult = jnp.sqrt(-jnp.tanh(la) * (a * a + 1.0))
    return xcb, r, ii, sp, a, mult


def _gelu_parts(g):
    th = jnp.tanh(GELU_C * (g + 0.044715 * (g * g * g)))
    gel = 0.5 * g * (1.0 + th)
    dgel = 0.5 * (1.0 + th) + 0.5 * g * (1.0 - th * th) * (GELU_C * (1.0 + 3.0 * 0.044715 * (g * g)))
    return gel, dgel


def _tile_scan(a, b, a_s, b_s, out_ref, reverse):
    n, lanes = a.shape
    nt = n // 8
    a, b = a.reshape(nt, 8, lanes), b.reshape(nt, 8, lanes)
    sub = lax.broadcasted_iota(jnp.int32, a.shape, 1)
    s = 1
    while s < 8:
        keep = sub < 8 - s if reverse else sub >= s
        amount = 8 - s if reverse else s
        b = b + a * jnp.where(keep, pltpu.roll(b, amount, axis=1), 0.0)
        a = a * jnp.where(keep, pltpu.roll(a, amount, axis=1), 1.0)
        s *= 2
    a_s[...] = a.reshape(n, lanes)
    b_s[...] = b.reshape(n, lanes)
    edge = pl.ds(0 if reverse else 7, nt, stride=8)
    ta, tb = a_s[edge, :], b_s[edge, :]
    shift = _su if reverse else _sd
    s = 1
    while s < nt:
        tb = tb + ta * shift(tb, s, 0.0)
        if 2 * s < nt:
            ta = ta * shift(ta, s, 1.0)
        s *= 2
    enters = shift(tb, 1, 0.0)
    for o in range(8):
        rows = pl.ds(o, nt, stride=8)
        out_ref[rows, :] = b_s[rows, :] + a_s[rows, :] * enters


def _pool_window(x, steps, shift):
    s, sh = x, 1
    for _ in range(steps):
        s = s + shift(s, sh)
        sh *= 2
    return s


def _fwd_inproj_own(x, g1, w_in, slots, stages=()):
    tm = 1024

    def body(s_ref, x_ref, g_ref, w_ref, proj_ref, h_ref):
        xv = x_ref[...]
        r = lax.rsqrt(_mean(xv * xv) + NORM_EPS)
        h = ((xv * r) * g_ref[...]).astype(BF)
        h_ref[...] = h
        proj_ref[...] = jnp.dot(h, w_ref[0], preferred_element_type=F32)

    return _call(
        body, name="fwd_inproj_own", grid=(T // tm,), prefetch=slots,
        in_specs=[pl.BlockSpec((tm, D), lambda i, s: (i, 0)),
                  pl.BlockSpec((1, D), lambda i, s: (0, 0)),
                  pl.BlockSpec((1, D, CW_IN), lambda i, s: (s[0], 0, 0))],
        out_specs=[pl.BlockSpec((tm, CW_IN), lambda i, s: (i, s[0])),
                   pl.BlockSpec((tm, D), lambda i, s: (i, 0))],
        out_shape=[jax.ShapeDtypeStruct((T, DIN), F32), jax.ShapeDtypeStruct((T, D), BF)],
        vmem=40, args=[x, g1, w_in], stages=stages)[0]


def _fwd_inproj_rest(h1, w_in, proj, slots):
    tm = 1024

    def body(s_ref, h_ref, w_ref, p_in, proj_ref):
        proj_ref[...] = jnp.dot(h_ref[...], w_ref[0], preferred_element_type=F32)

    res = pl.pallas_call(
        body, name="fwd_inproj_rest",
        grid_spec=pltpu.PrefetchScalarGridSpec(
            num_scalar_prefetch=1, grid=(NCHIP - 1, T // tm),
            in_specs=[pl.BlockSpec((tm, D), lambda k, i, s: (i, 0)),
                      pl.BlockSpec((1, D, CW_IN), lambda k, i, s: (s[1 + k], 0, 0)), ANY],
            out_specs=pl.BlockSpec((tm, CW_IN), lambda k, i, s: (i, s[1 + k]))),
        out_shape=pltpu.HBM((T, DIN), F32), input_output_aliases={3: 0},
        compiler_params=_cp(40),
    )(slots, *_hbm(h1, w_in, proj))
    return res


def _vec_spec():
    return pl.BlockSpec((1, CB), lambda j: (0, j))


def _fwd_lru(proj, conv_w, conv_b, wa, ba, wx, bx, lam, stages=()):
    def body(xp_ref, g_ref, cw_ref, cb_ref, wa_ref, ba_ref, wx_ref, bx_ref, lam_ref, y_ref, h_ref, a_s, b_s):
        xc, _, _, _ = _conv(xp_ref[...], cw_ref[...], cb_ref[...])
        _, _, ii, _, a, mult = _lru_gates(xc, wa_ref[0], ba_ref[...], wx_ref[0], bx_ref[...], lam_ref[...])
        _tile_scan(a, mult * (ii * xc), a_s, b_s, h_ref, reverse=False)
        gel, _ = _gelu_parts(g_ref[...])
        y_ref[...] = (h_ref[...] * gel).astype(BF)

    return _call(
        body, name="fwd_lru", grid=(NG,),
        in_specs=[pl.BlockSpec((T, CB), lambda j: (0, j)),
                  pl.BlockSpec((T, CB), lambda j: (0, NG + j)),
                  pl.BlockSpec((4, CB), lambda j: (0, j)),
                  _vec_spec(),
                  pl.BlockSpec((1, CB, CB), lambda j: (j, 0, 0)), _vec_spec(),
                  pl.BlockSpec((1, CB, CB), lambda j: (j, 0, 0)), _vec_spec(),
                  _vec_spec()],
        out_specs=[pl.BlockSpec((T, CB), lambda j: (0, j)), pl.BlockSpec((T, CB), lambda j: (0, j))],
        out_shape=[jax.ShapeDtypeStruct((T, DR), BF), jax.ShapeDtypeStruct((T, DR), F32)],
        vmem=48, args=[proj, proj, conv_w, conv_b, wa, ba, wx, bx, lam], stages=stages,
        scratch=[pltpu.VMEM((T, CB), F32)] * 2)


def _pool_cnt(w):
    t = lax.broadcasted_iota(jnp.int32, (T, 1), 0)
    return jnp.minimum(t + 1, w).astype(F32)


def _fwd_pool(proj, pool_w, pool_scale):
    def body(xp_ref, pw_ref, sc_ref, y_ref):
        for g, w in enumerate(POOL_WINDOWS):
            cols = slice(g * PG, (g + 1) * PG)
            x = xp_ref[:, cols]
            p = _pool_window(x, g + 1, _sd) / _pool_cnt(w) - x
            y_ref[:, cols] = (_mm(p, pw_ref[g]) * sc_ref[:, cols]).astype(BF)

    return pl.pallas_call(
        body, name="fwd_pool", grid=(1,),
        in_specs=[pl.BlockSpec((T, DP), lambda i: (0, 2 * DR // DP)),
                  pl.BlockSpec((4, PG, PG), lambda i: (0, 0, 0)),
                  pl.BlockSpec((1, DP), lambda i: (0, 0))],
        out_specs=pl.BlockSpec((T, DP), lambda i: (0, 0)),
        out_shape=pltpu.HBM((T, DP), BF),
        compiler_params=_cp(48),
    )(*_hbm(proj, pool_w, pool_scale))


GATE_BLK = 512
GATE_BLK0 = (2 * DR + DP) // GATE_BLK


def _gate_specs(tm):
    return [pl.BlockSpec((tm, GATE_BLK), functools.partial(lambda i, q: (i, GATE_BLK0 + q), q=q))
            for q in range(4)]


def _fwd_merge(x, ylru, ypool, proj, b_gate, g2, g3, w_lru_up, w_pool_up, w_o, stages=()):
    tm = 512

    def body(x_ref, yl_ref, yp_ref, p0, p1, p2, p3, bg_ref, g2_ref, g3_ref, wl_ref, wp_ref, wo_ref,
             x2_ref, h2_ref, m_ref, mrg_ref, bra_ref, brb_ref):
        bra = jnp.dot(yl_ref[...], wl_ref[...], preferred_element_type=F32)
        yp = yp_ref[...]
        brb = jnp.concatenate([jnp.dot(yp, wp_ref[k], preferred_element_type=F32) for k in range(NCHIP)], axis=1)
        bg = bg_ref[...]
        ga = _sigmoid(jnp.concatenate([p0[...], p1[...]], axis=1) + bg[:, :D])
        gb = _sigmoid(jnp.concatenate([p2[...], p3[...]], axis=1) + bg[:, D:])
        mrg = (ga * bra + gb * brb).astype(BF)
        m = jnp.dot(mrg, wo_ref[...], preferred_element_type=F32)
        r2 = lax.rsqrt(_mean(m * m) + NORM_EPS)
        x2 = x_ref[...] + (m * r2) * g2_ref[...]
        r3 = lax.rsqrt(_mean(x2 * x2) + NORM_EPS)
        x2_ref[...] = x2
        h2_ref[...] = ((x2 * r3) * g3_ref[...]).astype(BF)
        m_ref[...] = m
        mrg_ref[...] = mrg
        bra_ref[...] = bra.astype(BF)
        brb_ref[...] = brb.astype(BF)

    row = lambda w: pl.BlockSpec((tm, w), lambda i: (i, 0))
    full2 = lambda a, b: pl.BlockSpec((a, b), lambda i: (0, 0))
    return _call(
        body, name="fwd_merge", grid=(T // tm,),
        in_specs=[row(D), row(DR), row(DP)] + _gate_specs(tm) +
                 [full2(1, 2 * D), full2(1, D), full2(1, D), full2(DR, D),
                  pl.BlockSpec((NCHIP, DP, D // NCHIP), lambda i: (0, 0, 0)), full2(D, D)],
        out_specs=[row(D)] * 6,
        out_shape=[jax.ShapeDtypeStruct((T, D), F32), jax.ShapeDtypeStruct((T, D), BF),
                   jax.ShapeDtypeStruct((T, D), F32), jax.ShapeDtypeStruct((T, D), BF),
                   jax.ShapeDtypeStruct((T, D), BF), jax.ShapeDtypeStruct((T, D), BF)],
        vmem=48, args=[x, ylru, ypool, proj, proj, proj, proj, b_gate, g2, g3, w_lru_up, w_pool_up, w_o],
        stages=stages)


def _fwd_mlp_loss(h2, w_ff1, w_ff2, x2, target, g4):
    tm = 512
    fk = DF // NCHIP

    def body(h_ref, w1_ref, w2_ref, x2_ref, t_ref, g_ref, a1_ref, loss_ref, dy_ref, df_ref, dg_ref):
        first = pl.program_id(0) == 0
        h = h_ref[...]
        f = None
        for k in range(NCHIP):
            a1 = jnp.maximum(jnp.dot(h, w1_ref[k], preferred_element_type=F32), 0.0)
            a1_ref[:, k * fk:(k + 1) * fk] = a1.astype(BF)
            part = jnp.dot((a1 * a1).astype(BF), w2_ref[k * fk:(k + 1) * fk, :], preferred_element_type=F32)
            f = part if f is None else f + part
        g4v = g_ref[...]
        r4 = lax.rsqrt(_mean(f * f) + NORM_EPS)
        fn = f * r4
        e = (x2_ref[...] + fn * g4v) - t_ref[...]
        _acc(loss_ref, jnp.sum(_mean(e * e), axis=0, keepdims=True), first)
        dy = e * (1.0 / D)
        dy_ref[...] = dy
        _acc(dg_ref, _colsum(dy * fn), first)
        dfn = dy * g4v
        df_ref[...] = (r4 * (dfn - fn * _mean(dfn * fn))).astype(BF)

    row = pl.BlockSpec((tm, D), lambda i: (i, 0))
    return pl.pallas_call(
        body, name="fwd_mlp_loss", grid=(T // tm,),
        in_specs=[row, pl.BlockSpec((NCHIP, D, fk), lambda i: (0, 0, 0)), pl.BlockSpec((DF, D), lambda i: (0, 0)),
                  row, row, pl.BlockSpec((1, D), lambda i: (0, 0))],
        out_specs=[pl.BlockSpec((tm, DF), lambda i: (i, 0)), pl.BlockSpec((1, 1), lambda i: (0, 0)), row, row,
                   pl.BlockSpec((1, D), lambda i: (0, 0))],
        out_shape=_hbm_out([jax.ShapeDtypeStruct((T, DF), BF), jax.ShapeDtypeStruct((1, 1), F32),
                            jax.ShapeDtypeStruct((T, D), F32), jax.ShapeDtypeStruct((T, D), BF),
                            jax.ShapeDtypeStruct((1, D), F32)]),
        compiler_params=_cp(56),
    )(*_hbm(h2, w_ff1, w_ff2, x2, target, g4))


def _bwd_mlp_x(df, a1, w_ff1, w_ff2):
    tm = 512
    fk = DF // NCHIP

    def body(df_ref, a1_ref, w1_ref, w2_ref, dh_ref, df1_ref):
        df = df_ref[...]
        dh = None
        for k in range(NCHIP):
            cols = slice(k * fk, (k + 1) * fk)
            dact = _mm_nt(df, w2_ref[cols, :])
            df1 = (dact * (2.0 * a1_ref[:, cols].astype(F32))).astype(BF)
            df1_ref[:, cols] = df1
            part = _mm_nt(df1, w1_ref[k])
            dh = part if dh is None else dh + part
        dh_ref[...] = dh

    return pl.pallas_call(
        body, name="bwd_mlp_x", grid=(T // tm,),
        in_specs=[pl.BlockSpec((tm, D), lambda i: (i, 0)),
                  pl.BlockSpec((tm, DF), lambda i: (i, 0)),
                  pl.BlockSpec((NCHIP, D, fk), lambda i: (0, 0, 0)),
                  pl.BlockSpec((DF, D), lambda i: (0, 0))],
        out_specs=[pl.BlockSpec((tm, D), lambda i: (i, 0)), pl.BlockSpec((tm, DF), lambda i: (i, 0))],
        out_shape=_hbm_out([jax.ShapeDtypeStruct((T, D), F32), jax.ShapeDtypeStruct((T, DF), BF)]),
        compiler_params=_cp(56),
    )(*_hbm(df, a1, w_ff1, w_ff2))


def _bwd_mlp_w(df, h2, a1, df1):
    fc = 512
    per = (DF // NCHIP) // fc

    def body(df_ref, h_ref, a1_ref, df1_ref, dw1_ref, dw2_ref):
        a1 = a1_ref[...].astype(F32)
        dw2_ref[...] = _mm_tn((a1 * a1).astype(BF), df_ref[...]).astype(BF)
        dw1_ref[0] = _mm_tn(h_ref[...], df1_ref[...]).astype(BF)

    return pl.pallas_call(
        body, name="bwd_mlp_w", grid=(DF // fc,),
        in_specs=[pl.BlockSpec((T, D), lambda j: (0, 0)),
                  pl.BlockSpec((T, D), lambda j: (0, 0)),
                  pl.BlockSpec((T, fc), lambda j: (0, j)),
                  pl.BlockSpec((T, fc), lambda j: (0, j))],
        out_specs=[pl.BlockSpec((1, D, fc), lambda j: (j // per, 0, j % per)),
                   pl.BlockSpec((fc, D), lambda j: (j, 0))],
        out_shape=_hbm_out([jax.ShapeDtypeStruct((NCHIP, D, DF // NCHIP), BF),
                            jax.ShapeDtypeStruct((DF, D), BF)]),
        compiler_params=_cp(56),
    )(*_hbm(df, h2, a1, df1))


def _bwd_merge(dh2, dy, x2, m, bra, brb, proj, b_gate, g2, g3, w_lru_up, w_pool_up, w_o, stages=()):
    tm = 256
    cpu = D // NCHIP

    def body(dh2_ref, dy_ref, x2_ref, m_ref, bra_ref, brb_ref, p0, p1, p2, p3, bg_ref,
             g2_ref, g3_ref, wl_ref, wp_ref, wo_ref,
             dx_ref, dgt_ref, dyl_ref, dyp_ref, dm_ref, dbra_ref, dbrb_ref, dg2_ref, dg3_ref, dbg_ref):
        first = pl.program_id(0) == 0
        x2 = x2_ref[...]
        r3 = lax.rsqrt(_mean(x2 * x2) + NORM_EPS)
        x2n = x2 * r3
        dh2 = dh2_ref[...]
        t3 = dh2 * g3_ref[...]
        dx2 = dy_ref[...] + r3 * (t3 - x2n * _mean(t3 * x2n))
        dx_ref[...] = dx2
        _acc(dg3_ref, _colsum(dh2 * x2n), first)
        m = m_ref[...]
        r2 = lax.rsqrt(_mean(m * m) + NORM_EPS)
        mn = m * r2
        _acc(dg2_ref, _colsum(dx2 * mn), first)
        dmn = dx2 * g2_ref[...]
        dm = (r2 * (dmn - mn * _mean(dmn * mn))).astype(BF)
        dm_ref[...] = dm
        dmrg = _mm_nt(dm, wo_ref[...])
        bg = bg_ref[...]
        ga = _sigmoid(jnp.concatenate([p0[...], p1[...]], axis=1) + bg[:, :D])
        gb = _sigmoid(jnp.concatenate([p2[...], p3[...]], axis=1) + bg[:, D:])
        dga = dmrg * bra_ref[...].astype(F32) * (ga * (1.0 - ga))
        dgb = dmrg * brb_ref[...].astype(F32) * (gb * (1.0 - gb))
        dgt_ref[:, :D] = dga.astype(BF)
        dgt_ref[:, D:] = dgb.astype(BF)
        _acc(dbg_ref, jnp.concatenate([_colsum(dga), _colsum(dgb)], axis=1), first)
        dbra = (dmrg * ga).astype(BF)
        dbrb = (dmrg * gb).astype(BF)
        dbra_ref[...] = dbra
        dbrb_ref[...] = dbrb
        dyl_ref[...] = _mm_nt(dbra, wl_ref[...])
        dyp = None
        for k in range(NCHIP):
            part = _mm_nt(dbrb[:, k * cpu:(k + 1) * cpu], wp_ref[k])
            dyp = part if dyp is None else dyp + part
        dyp_ref[...] = dyp

    row = lambda w: pl.BlockSpec((tm, w), lambda i: (i, 0))
    full2 = lambda a, b: pl.BlockSpec((a, b), lambda i: (0, 0))
    wp_spec = pl.BlockSpec((NCHIP, DP, cpu), lambda i: (0, 0, 0))
    return _call(
        body, name="bwd_merge", grid=(T // tm,),
        in_specs=[row(D)] * 6 + _gate_specs(tm) +
                 [full2(1, 2 * D), full2(1, D), full2(1, D), full2(DR, D), wp_spec, full2(D, D)],
        out_specs=[row(D), row(2 * D), row(DR), row(DP), row(D), row(D), row(D),
                   full2(1, D), full2(1, D), full2(1, 2 * D)],
        out_shape=[jax.ShapeDtypeStruct((T, D), F32), jax.ShapeDtypeStruct((T, 2 * D), BF),
                   jax.ShapeDtypeStruct((T, DR), F32), jax.ShapeDtypeStruct((T, DP), F32),
                   jax.ShapeDtypeStruct((T, D), BF), jax.ShapeDtypeStruct((T, D), BF),
                   jax.ShapeDtypeStruct((T, D), BF),
                   jax.ShapeDtypeStruct((1, D), F32), jax.ShapeDtypeStruct((1, D), F32),
                   jax.ShapeDtypeStruct((1, 2 * D), F32)],
        vmem=56, args=[dh2, dy, x2, m, bra, brb, proj, proj, proj, proj, b_gate, g2, g3, w_lru_up, w_pool_up, w_o],
        stages=stages)


def _dw_merge(mrg, dm, ylru, dbra, ypool, dbrb, stages=()):
    nb = NCHIP
    rb, pb, cpu = D // nb, DP // nb, D // NCHIP

    def body(mrg_ref, dm_ref, yl_ref, dbra_ref, yp_ref, dbrb_ref, dwo_ref, dwl_ref, dwp_ref):
        dwo_ref[...] = _mm_tn(mrg_ref[...], dm_ref[...]).astype(BF)
        dwl_ref[...] = _mm_tn(yl_ref[...], dbra_ref[...]).astype(BF)
        dwp = _mm_tn(yp_ref[...], dbrb_ref[...]).astype(BF)
        for k in range(NCHIP):
            dwp_ref[k] = dwp[:, k * cpu:(k + 1) * cpu]

    cols = lambda w: pl.BlockSpec((T, w), lambda r: (0, r))
    whole = pl.BlockSpec((T, D), lambda r: (0, 0))
    return _call(
        body, name="dw_merge", grid=(nb,),
        in_specs=[cols(rb), whole, cols(rb), whole, cols(pb), whole],
        out_specs=[pl.BlockSpec((rb, D), lambda r: (r, 0)), pl.BlockSpec((rb, D), lambda r: (r, 0)),
                   pl.BlockSpec((NCHIP, pb, cpu), lambda r: (0, r, 0))],
        out_shape=[jax.ShapeDtypeStruct((D, D), BF), jax.ShapeDtypeStruct((DR, D), BF),
                   jax.ShapeDtypeStruct((NCHIP, DP, cpu), BF)],
        vmem=56, args=[mrg, dm, ylru, dbra, ypool, dbrb], stages=stages)


def _bwd_lru(proj, h, dylru, conv_w, conv_b, wa, ba, wx, bx, lam, stages=()):
    def body(xp_ref, g_ref, h_ref, dy_ref, cw_ref, cb_ref, wa_ref, ba_ref, wx_ref, bx_ref, lam_ref,
             dxp_ref, dg_ref, dcw_ref, dcb_ref, dwa_ref, dba_ref, dwx_ref, dbx_ref, dlam_ref, a_s, b_s, l_s):
        xp = xp_ref[...]
        cw = cw_ref[...]
        lam = lam_ref[...]
        xc, x1, x2, x3 = _conv(xp, cw, cb_ref[...])
        wa, wx = wa_ref[0], wx_ref[0]
        xcb, r, ii, sp, a, mult = _lru_gates(xc, wa, ba_ref[...], wx, bx_ref[...], lam)
        g = g_ref[...]
        gel, dgel = _gelu_parts(g)
        h = h_ref[...]
        dy = dy_ref[...]
        dg_ref[...] = (dy * h * dgel).astype(BF)
        _tile_scan(_su(a, 1, 0.0), dy * gel, a_s, b_s, l_s, reverse=True)
        b = l_s[...]
        da = b * _sd(h, 1, 0.0)
        dmult = b * (ii * xc)
        dii = b * (mult * xc)
        dxc = b * (mult * ii)
        dla = da * a - dmult * ((a * a) / mult)
        dr = dla * ((-LRU_C) * sp)
        dsp = _colsum(dla * ((-LRU_C) * r))
        dlam_ref[...] = -dsp / (1.0 + jnp.exp(lam))
        dzr = dr * (r * (1.0 - r))
        dzi = dii * (ii * (1.0 - ii))
        dzrb, dzib = dzr.astype(BF), dzi.astype(BF)
        dxc = dxc + _mm_nt(dzrb, wa) + _mm_nt(dzib, wx)
        dwa_ref[0] = _mm_tn(xcb, dzrb)
        dwx_ref[0] = _mm_tn(xcb, dzib)
        dba_ref[...] = _colsum(dzr)
        dbx_ref[...] = _colsum(dzi)
        dcb_ref[...] = _colsum(dxc)
        dcw_ref[...] = jnp.concatenate([_colsum(dxc * x3), _colsum(dxc * x2), _colsum(dxc * x1),
                                        _colsum(dxc * xp)], axis=0)
        dxp = cw[3:4] * dxc + cw[2:3] * _su(dxc, 1) + cw[1:2] * _su(dxc, 2) + cw[0:1] * _su(dxc, 3)
        dxp_ref[...] = dxp.astype(BF)

    blk = pl.BlockSpec((T, CB), lambda j: (0, j))
    wsp = pl.BlockSpec((1, CB, CB), lambda j: (j, 0, 0))
    return _call(
        body, name="bwd_lru", grid=(NG,),
        in_specs=[blk, pl.BlockSpec((T, CB), lambda j: (0, NG + j)), blk, blk,
                  pl.BlockSpec((4, CB), lambda j: (0, j)), _vec_spec(), wsp, _vec_spec(), wsp, _vec_spec(),
                  _vec_spec()],
        out_specs=[blk, blk, pl.BlockSpec((4, CB), lambda j: (0, j)), _vec_spec(), wsp, _vec_spec(), wsp,
                   _vec_spec(), _vec_spec()],
        out_shape=[jax.ShapeDtypeStruct((T, DR), BF), jax.ShapeDtypeStruct((T, DR), BF),
                   jax.ShapeDtypeStruct((4, DR), F32), jax.ShapeDtypeStruct((1, DR), F32),
                   jax.ShapeDtypeStruct((NG, CB, CB), F32), jax.ShapeDtypeStruct((1, DR), F32),
                   jax.ShapeDtypeStruct((NG, CB, CB), F32), jax.ShapeDtypeStruct((1, DR), F32),
                   jax.ShapeDtypeStruct((1, DR), F32)],
        vmem=56, args=[proj, proj, h, dylru, conv_w, conv_b, wa, ba, wx, bx, lam], stages=stages,
        scratch=[pltpu.VMEM((T, CB), F32)] * 3)


def _bwd_pool(proj, dypool, pool_w, pool_scale):
    def body(xp_ref, dy_ref, pw_ref, sc_ref, dx_ref, dw_ref, dsc_ref):
        for g, w in enumerate(POOL_WINDOWS):
            cols = slice(g * PG, (g + 1) * PG)
            cnt = _pool_cnt(w)
            x = xp_ref[:, cols]
            pb = (_pool_window(x, g + 1, _sd) / cnt - x).astype(BF)
            wg = pw_ref[g]
            dy = dy_ref[:, cols]
            dsc_ref[:, cols] = _colsum(dy * _mm(pb, wg))
            dyp = (dy * sc_ref[:, cols]).astype(BF)
            dw_ref[g] = _mm_tn(pb, dyp)
            dp = _mm_nt(dyp, wg)
            dx_ref[:, cols] = (_pool_window(dp / cnt, g + 1, _su) - dp).astype(BF)

    return pl.pallas_call(
        body, name="bwd_pool", grid=(1,),
        in_specs=[pl.BlockSpec((T, DP), lambda i: (0, 2 * DR // DP)),
                  pl.BlockSpec((T, DP), lambda i: (0, 0)),
                  pl.BlockSpec((4, PG, PG), lambda i: (0, 0, 0)),
                  pl.BlockSpec((1, DP), lambda i: (0, 0))],
        out_specs=[pl.BlockSpec((T, DP), lambda i: (0, 0)),
                   pl.BlockSpec((4, PG, PG), lambda i: (0, 0, 0)),
                   pl.BlockSpec((1, DP), lambda i: (0, 0))],
        out_shape=_hbm_out([jax.ShapeDtypeStruct((T, DP), BF), jax.ShapeDtypeStruct((4, PG, PG), F32),
                            jax.ShapeDtypeStruct((1, DP), F32)]),
        compiler_params=_cp(48),
    )(*_hbm(proj, dypool, pool_w, pool_scale))


PART_COLS = (DR, DR, DP, 2 * D)


def _shard_pieces():
    starts = [sum(PART_COLS[:p]) for p in range(len(PART_COLS))]
    shards = []
    for k in range(NCHIP):
        lo, hi = k * CW_IN, (k + 1) * CW_IN
        shards.append([(p, max(lo, s) - s, min(hi, s + wd) - s, max(lo, s) - lo)
                       for p, (s, wd) in enumerate(zip(starts, PART_COLS)) if max(lo, s) < min(hi, s + wd)])
    return shards


def _bwd_inproj_w(h1, parts, after):
    flat = [(k, *piece) for k, pieces in enumerate(_shard_pieces()) for piece in pieces]

    def body(h_hbm, p0, p1, p2, p3, after_ref, dw_hbm, h_v, dw_v, *rest):
        bufs, sem_in, sem_out = rest[:len(flat)], rest[len(flat)], rest[len(flat) + 1]
        part_refs = (p0, p1, p2, p3)
        loads = [pltpu.make_async_copy(h_hbm, h_v, sem_in.at[0])]
        for i, (k, p, a, b, c0) in enumerate(flat):
            loads.append(pltpu.make_async_copy(part_refs[p].at[:, pl.ds(a, b - a)], bufs[i], sem_in.at[1 + i]))
        for cp in loads:
            cp.start()
        loads[0].wait()
        stores = []
        for i, (k, p, a, b, c0) in enumerate(flat):
            loads[1 + i].wait()
            dw_v[k, :, c0:c0 + b - a] = _mm_tn(h_v[...], bufs[i][...]).astype(BF)
            if i + 1 == len(flat) or flat[i + 1][0] != k:
                stores.append(pltpu.make_async_copy(dw_v.at[k], dw_hbm.at[k], sem_out.at[k]))
                stores[-1].start()
        for cp in stores:
            cp.wait()

    scratch = [pltpu.VMEM((T, D), BF), pltpu.VMEM((NCHIP, D, CW_IN), BF)]
    scratch += [pltpu.VMEM((T, b - a), parts[p].dtype) for k, p, a, b, c0 in flat]
    scratch += [pltpu.SemaphoreType.DMA((1 + len(flat),)), pltpu.SemaphoreType.DMA((NCHIP,))]
    return pl.pallas_call(
        body, name="bwd_inproj_w", in_specs=[ANY] * 6, out_specs=ANY, scratch_shapes=scratch,
        out_shape=pltpu.HBM((NCHIP, D, CW_IN), BF), compiler_params=_cp(48),
    )(*_hbm(h1, *parts), after)


def _bwd_inproj_x(parts, w_in, x, dxres, g1, stages=()):
    tm = 1024

    def body(p0, p1, p2, p3, w_ref, x_ref, dr_ref, g_ref, dx_ref, dg_ref):
        part_refs = (p0, p1, p2, p3)
        dh = None
        for k, pieces in enumerate(_shard_pieces()):
            for p, a, b, c0 in pieces:
                part = _mm_nt(part_refs[p][:, a:b], w_ref[k, :, c0:c0 + b - a])
                dh = part if dh is None else dh + part
        xv = x_ref[...]
        r = lax.rsqrt(_mean(xv * xv) + NORM_EPS)
        xn = xv * r
        t = dh * g_ref[...]
        dx_ref[...] = dr_ref[...] + r * (t - xn * _mean(t * xn))
        _acc(dg_ref, _colsum(dh * xn), pl.program_id(0) == 0)

    row = pl.BlockSpec((tm, D), lambda i: (i, 0))
    vec = pl.BlockSpec((1, D), lambda i: (0, 0))
    return _call(
        body, name="bwd_inproj_x", grid=(T // tm,),
        in_specs=[pl.BlockSpec((tm, wd), lambda i: (i, 0)) for wd in PART_COLS] +
                 [pl.BlockSpec((NCHIP, D, CW_IN), lambda i: (0, 0, 0)), row, row, vec],
        out_specs=[row, vec],
        out_shape=[jax.ShapeDtypeStruct((T, D), F32), jax.ShapeDtypeStruct((1, D), F32)],
        vmem=60, args=[*parts, w_in, x, dxres, g1], stages=stages)[0]


def _place():
    x, y, c = lax.axis_index("x"), lax.axis_index("y"), lax.axis_index("c")
    chips = [(1 - x, y), (x, 1 - y), (1 - x, 1 - y)]
    return x, y, c, chips


def _rcopy(src, dst, ssem, rsem, dev):
    return pltpu.make_async_remote_copy(src_ref=src, dst_ref=dst, send_sem=ssem, recv_sem=rsem,
                                        device_id=dev, device_id_type=MESH_ID)


def _sds(a):
    return jax.ShapeDtypeStruct(a.shape, a.dtype)


def _sem2(n, m):
    return [pltpu.SemaphoreType.DMA((n * m,)), pltpu.SemaphoreType.DMA((n * m,))]


ALL = (0, 1, 1)


def _piece(ref, k, half, part):
    hr = ref.shape[1] // 2
    r0, r1 = hr * part[0] // part[2], hr * part[1] // part[2]
    return ref.at[k, pl.ds(half * hr + r0, r1 - r0), :]


def _gather(fulls, ici=(), d2d=()):
    n = len(fulls)
    ici, d2d = list(ici), list(d2d)
    pieces = [("ici", i, part) for i, part in ici] + [("d2d", i, part) for i, part in d2d]

    def copies(outs, sems):
        x, y, c, chips = _place()
        me = 2 * x + y
        sib = (x, y, 1 - c)
        send, recv = [], []
        for q, (kind, i, part) in enumerate(pieces):
            for j, chip in enumerate(chips):
                k, s = 2 * chip[0] + chip[1], 3 * q + j
                if kind == "ici":
                    mine, theirs, dev = _piece(outs[i], me, c, part), _piece(outs[i], k, c, part), (*chip, c)
                else:
                    mine, theirs, dev = _piece(outs[i], k, c, part), _piece(outs[i], k, 1 - c, part), sib
                send.append(_rcopy(mine, mine, sems[0].at[s], sems[1].at[s], dev))
                recv.append(_rcopy(theirs, theirs, sems[0].at[s], sems[1].at[s], dev))
        return send, recv

    def start(ins, outs, sems):
        for cp in copies(outs, sems)[0]:
            cp.start()

    def finish(ins, outs, sems):
        send, recv = copies(outs, sems)
        for cp in recv:
            cp.wait_recv()
        for cp in send:
            cp.wait_send()

    sems = [pltpu.SemaphoreType.DMA((3 * len(pieces),)), pltpu.SemaphoreType.DMA((3 * len(pieces),))]
    return _Stage(fulls, [_sds(f) for f in fulls], {i: i for i in range(n)}, sems, start, finish)


def _gather_whole(v):
    def copies(ins, outs, sems):
        x, y, c, chips = _place()
        me = 2 * x + y
        send = [_rcopy(ins[0], outs[0].at[me], sems[0].at[j], sems[1].at[j], (*chip, c))
                for j, chip in enumerate(chips)]
        recv = [_rcopy(ins[0], outs[0].at[2 * chip[0] + chip[1]], sems[0].at[j], sems[1].at[j], (*chip, c))
                for j, chip in enumerate(chips)]
        return send, recv

    def start(ins, outs, sems):
        for cp in copies(ins, outs, sems)[0]:
            cp.start()

    def finish(ins, outs, sems):
        send, recv = copies(ins, outs, sems)
        for cp in recv:
            cp.wait_recv()
        for cp in send:
            cp.wait_send()

    return _Stage([v], [jax.ShapeDtypeStruct((NCHIP,) + v.shape, v.dtype)], {},
                  [pltpu.SemaphoreType.DMA((3,)), pltpu.SemaphoreType.DMA((3,))], start, finish)


def _to_sibling(srcs):
    n = len(srcs)

    def copies(ins, outs, sems):
        x, y, c, _ = _place()
        sib = (x, y, 1 - c)
        return [_rcopy(ins[i].at[:, 1 - c] if srcs[i].ndim == 4 else ins[i], outs[i], sems[0].at[i], sems[1].at[i], sib)
                for i in range(n)]

    def start(ins, outs, sems):
        for cp in copies(ins, outs, sems):
            cp.start()

    def finish(ins, outs, sems):
        for cp in copies(ins, outs, sems):
            cp.wait()

    shapes = [jax.ShapeDtypeStruct((NCHIP,) + s.shape[2:] if s.ndim == 4 else s.shape, s.dtype) for s in srcs]
    return _Stage(srcs, shapes, {}, [pltpu.SemaphoreType.DMA((n,)), pltpu.SemaphoreType.DMA((n,))], start, finish)


def _to_chips(srcs, parts=None, lands=None):
    n = len(srcs)
    parts = [ALL] * n if parts is None else parts
    lands = [None] * n if lands is None else lands
    given = [i for i in range(n) if lands[i] is not None]

    def rows(ref, i):
        hr = srcs[i].shape[1]
        r0, r1 = hr * parts[i][0] // parts[i][2], hr * parts[i][1] // parts[i][2]
        return ref.at[pl.ds(r0, r1 - r0), :]

    def copies(ins, outs, sems):
        x, y, c, chips = _place()
        me = 2 * x + y
        return [_rcopy(rows(ins[i].at[2 * chip[0] + chip[1]] if srcs[i].shape[0] == NCHIP else ins[i].at[c], i),
                       rows(outs[i].at[me], i), sems[0].at[3 * i + j], sems[1].at[3 * i + j], (*chip, c))
                for i in range(n) for j, chip in enumerate(chips)]

    def start(ins, outs, sems):
        for cp in copies(ins, outs, sems):
            cp.start()

    def finish(ins, outs, sems):
        for cp in copies(ins, outs, sems):
            cp.wait()

    shapes = [jax.ShapeDtypeStruct((NCHIP,) + s.shape[1:], s.dtype) for s in srcs]
    alias = {n + q: i for q, i in enumerate(given)}
    return _Stage(list(srcs) + [lands[i] for i in given], shapes, alias, _sem2(n, 3), start, finish)


HBM_REF = pl.BlockSpec(memory_space=pltpu.HBM)
SEM_REF = pl.BlockSpec(memory_space=pltpu.SEMAPHORE)
DATAFLOW = pltpu.SideEffectType.DATAFLOW_SIDE_EFFECTING


def _after(x):
    return _Stage([x], [], {}, [], lambda *a: None, lambda *a: None)


class _Flight:
    def __init__(self, stage, sems, bufs):
        self.stage, self.sems, self.bufs = stage, list(sems), list(bufs)

    def landed(self):
        st, n = self.stage, len(self.stage.operands)
        fresh = [j for j in range(len(st.out_shape)) if j not in st.alias.values()]
        back = {v: k for k, v in st.alias.items()}
        return [self.bufs[back[j]] if j in back else self.bufs[n + fresh.index(j)] for j in range(len(st.out_shape))]


def _split_call(name, finish=(), start=(), after=None):
    bufs, stage_bufs = [], []

    def slot(a):
        for i, b in enumerate(bufs):
            if b is a:
                return i
        bufs.append(a)
        return len(bufs) - 1

    fin_slots = [[slot(b) for b in fl.bufs] for fl in finish]
    for st in start:
        fresh = [lax.empty(o.shape, o.dtype) for j, o in enumerate(st.out_shape) if j not in st.alias.values()]
        stage_bufs.append([slot(a) for a in list(st.operands) + fresh])
    old_sems = [s for fl in finish for s in fl.sems]
    new_sems = [s for st in start for s in st.sems]
    nb, no, nn = len(bufs), len(old_sems), len(new_sems)

    def refs_of(st, slots, buf_refs):
        n = len(st.operands)
        ins = [buf_refs[i] for i in slots[:n]]
        fresh = [j for j in range(len(st.out_shape)) if j not in st.alias.values()]
        back = {v: k for k, v in st.alias.items()}
        outs = [ins[back[j]] if j in back else buf_refs[slots[n + fresh.index(j)]] for j in range(len(st.out_shape))]
        return ins, outs

    def body(*refs):
        buf_refs, sem_in = refs[:nb], refs[nb:nb + no]
        sem_out = refs[nb + no + (after is not None):][:nn]
        token = refs[-1]
        pos = 0
        for fl, slots in zip(finish, fin_slots):
            ins, outs = refs_of(fl.stage, slots, buf_refs)
            fl.stage.finish(ins, outs, sem_in[pos:pos + len(fl.sems)])
            pos += len(fl.sems)
        pos = 0
        for st, slots in zip(start, stage_bufs):
            ins, outs = refs_of(st, slots, buf_refs)
            st.start(ins, outs, sem_out[pos:pos + len(st.sems)])
            pos += len(st.sems)
        token[...] = jnp.zeros_like(token)

    res = pl.pallas_call(
        body, name=name,
        out_shape=tuple(new_sems) + tuple(pltpu.HBM(b.shape, b.dtype) for b in bufs) +
                  (jax.ShapeDtypeStruct((8, LANE), F32),),
        in_specs=(HBM_REF,) * nb + (SEM_REF,) * no + ((pl.BlockSpec(memory_space=pl.ANY),) if after is not None else ()),
        out_specs=(SEM_REF,) * nn + (HBM_REF,) * nb + (pl.BlockSpec(memory_space=pltpu.VMEM),),
        input_output_aliases={i: nn + i for i in range(nb)},
        compiler_params=pltpu.CompilerParams(has_side_effects=DATAFLOW),
    )(*_hbm(*bufs), *old_sems, *([after] if after is not None else []))
    sems, thru, token = res[:nn], res[nn:nn + nb], res[-1]
    for fl, slots in zip(finish, fin_slots):
        fl.bufs = [thru[i] for i in slots]
    flights, pos = [], 0
    for st, slots in zip(start, stage_bufs):
        flights.append(_Flight(st, sems[pos:pos + len(st.sems)], [thru[i] for i in slots]))
        pos += len(st.sems)
    return flights, token


def _share(pairs):
    n = len(pairs)

    def start(ins, outs, sems):
        x, y, c, _ = _place()
        for i in range(n):
            _rcopy(outs[i].at[c], outs[i].at[c], sems[0].at[i], sems[1].at[i], (x, y, 1 - c)).start()

    def finish(ins, outs, sems):
        x, y, c, _ = _place()
        for i in range(n):
            _rcopy(outs[i].at[c], outs[i].at[c], sems[0].at[i], sems[1].at[i], (x, y, 1 - c)).wait_send()
            _rcopy(outs[i].at[1 - c], outs[i].at[1 - c], sems[0].at[i], sems[1].at[i], (x, y, 1 - c)).wait_recv()

    return _Stage(pairs, [_sds(p) for p in pairs], {i: i for i in range(n)},
                  [pltpu.SemaphoreType.DMA((n,)), pltpu.SemaphoreType.DMA((n,))], start, finish)


def _row_block(rows, cols, itemsize=4, target=2 * MIB):
    br = rows
    while br * cols * itemsize > target and br % 32 == 0:
        br //= 2
    return br


def _cast_place(w, chip_idx, name):
    rows, cols = w.shape
    br = _row_block(rows, cols)

    def body(k_ref, w_ref, o_ref):
        o_ref[0] = w_ref[...].astype(BF)

    return _call(
        body, name=name, grid=(rows // br,), prefetch=chip_idx,
        in_specs=[pl.BlockSpec((br, cols), lambda r, k: (r, 0))],
        out_specs=[pl.BlockSpec((1, br, cols), lambda r, k: (k[0], r, 0))],
        out_shape=[jax.ShapeDtypeStruct((NCHIP, rows, cols), BF)], vmem=32, args=[w])[0][0]


def _cast_place_multi(ws, chip_idx, stages=()):
    br = 128
    nblk = [a.shape[0] // br for a in ws]
    starts = [sum(nblk[:i]) for i in range(len(ws))]

    def body(k_ref, *refs):
        r = pl.program_id(0)
        for i in range(len(ws)):
            @pl.when(jnp.logical_and(r >= starts[i], r < starts[i] + nblk[i]))
            def _(i=i):
                refs[len(ws) + i][0] = refs[i][...].astype(BF)

    def at(i):
        return functools.partial(lambda r, s, nb: jnp.clip(r - s, 0, nb - 1), s=starts[i], nb=nblk[i])

    outs, landed = _call(
        body, name="cast_rest", grid=(sum(nblk),), prefetch=chip_idx,
        in_specs=[pl.BlockSpec((br, a.shape[1]), functools.partial(lambda r, k, f: (f(r), 0), f=at(i)))
                  for i, a in enumerate(ws)],
        out_specs=[pl.BlockSpec((1, br, a.shape[1]), functools.partial(lambda r, k, f: (k[0], f(r), 0), f=at(i)))
                   for i, a in enumerate(ws)],
        out_shape=[jax.ShapeDtypeStruct((NCHIP,) + a.shape, BF) for a in ws], vmem=32, args=list(ws), stages=stages)
    return outs, landed


def _add_sibling(g, land, cidx, name, stages=()):
    _, _, hr, cols = g.shape
    br = _row_block(hr, cols)

    def body(c_ref, g_ref, l_ref, o_ref):
        o_ref[...] = (g_ref[0, 0].astype(F32) + l_ref[0].astype(F32)).astype(BF)[None]

    outs, st = _call(
        body, name=name, grid=(NCHIP, hr // br), prefetch=cidx,
        in_specs=[pl.BlockSpec((1, 1, br, cols), lambda k, r, c: (k, c[0], r, 0)),
                  pl.BlockSpec((1, br, cols), lambda k, r, c: (k, r, 0))],
        out_specs=[pl.BlockSpec((1, br, cols), lambda k, r, c: (k, r, 0))],
        out_shape=[jax.ShapeDtypeStruct((NCHIP, hr, cols), BF)], vmem=32, args=[g, land], stages=stages)
    return outs[0], st


def _add_sibling_multi(gs, lands, cidx, name):
    n = len(gs)
    brs = [_row_block(g.shape[2], g.shape[3]) for g in gs]
    nrb = [g.shape[2] // b for g, b in zip(gs, brs)]
    nblk = [NCHIP * q for q in nrb]
    starts = [sum(nblk[:i]) for i in range(n)]

    def body(c_ref, *refs):
        r = pl.program_id(0)
        for i in range(n):
            g_ref, l_ref, o_ref = refs[2 * i], refs[2 * i + 1], refs[2 * n + i]

            @pl.when(jnp.logical_and(r >= starts[i], r < starts[i] + nblk[i]))
            def _():
                o_ref[...] = (g_ref[0, 0].astype(F32) + l_ref[0].astype(F32)).astype(BF)[None]

    def at(i, r):
        q = jnp.clip(r - starts[i], 0, nblk[i] - 1)
        return q // nrb[i], q % nrb[i]

    def g_spec(i):
        return pl.BlockSpec((1, 1, brs[i], gs[i].shape[3]),
                            functools.partial(lambda r, c, i: (at(i, r)[0], c[0], at(i, r)[1], 0), i=i))

    def l_spec(i):
        return pl.BlockSpec((1, brs[i], gs[i].shape[3]),
                            functools.partial(lambda r, c, i: (at(i, r)[0], at(i, r)[1], 0), i=i))

    return _call(
        body, name=name, grid=(sum(nblk),), prefetch=cidx,
        in_specs=[s for i in range(n) for s in (g_spec(i), l_spec(i))], out_specs=[l_spec(i) for i in range(n)],
        out_shape=[jax.ShapeDtypeStruct(l.shape, BF) for l in lands], vmem=32,
        args=[a for i in range(n) for a in (gs[i], lands[i])])[0]


def _add_pair(a, b, name):
    rows, cols = a.shape

    def body(a_ref, b_ref, o_ref):
        o_ref[...] = a_ref[...] + b_ref[...]

    spec = pl.BlockSpec((rows, cols), lambda r: (0, 0))
    return _call(body, name=name, grid=(1,), in_specs=[spec, spec], out_specs=[spec], out_shape=[_sds(a)],
                 vmem=32, args=[a, b])[0][0]


def _add_chips(own, land, idx, name, stages=None):
    _, hr, cols = land.shape
    br = _row_block(hr, cols)

    def body(s_ref, a_ref, b_ref, c_ref, d_ref, o_ref):
        o_ref[...] = (a_ref[...].astype(F32) + b_ref[...].astype(F32)) + (c_ref[...].astype(F32) +
                                                                           d_ref[...].astype(F32))

    spec = lambda q: pl.BlockSpec((1, br, cols), functools.partial(lambda r, s, q: (s[q], r, 0), q=q))
    outs, landed = _call(
        body, name=name, grid=(hr // br,), prefetch=idx,
        in_specs=[spec(0), spec(1), spec(2), spec(3)], out_specs=[spec(4)],
        out_shape=[jax.ShapeDtypeStruct((2, hr, cols), F32)], vmem=48, args=[own, land, land, land],
        stages=stages or ())
    return outs[0] if stages is None else (outs[0], landed)


def _add_chips_multi(owns, lands, idx, name, stages=()):
    n = len(owns)
    brs = [_row_block(l.shape[1], l.shape[2]) for l in lands]
    nblk = [l.shape[1] // b for l, b in zip(lands, brs)]
    starts = [sum(nblk[:i]) for i in range(n)]

    def body(s_ref, *refs):
        r = pl.program_id(0)
        for i in range(n):
            a_ref, b_ref, c_ref, d_ref = refs[4 * i:4 * i + 4]
            o_ref = refs[4 * n + i]

            @pl.when(jnp.logical_and(r >= starts[i], r < starts[i] + nblk[i]))
            def _():
                o_ref[...] = (a_ref[...].astype(F32) + b_ref[...].astype(F32)) + (c_ref[...].astype(F32) +
                                                                                   d_ref[...].astype(F32))

    def spec(i, q):
        return pl.BlockSpec((1, brs[i], lands[i].shape[2]), functools.partial(
            lambda r, s, q, st, nb: (s[q], jnp.clip(r - st, 0, nb - 1), 0), q=q, st=starts[i], nb=nblk[i]))

    outs, landed = _call(
        body, name=name, grid=(sum(nblk),), prefetch=idx,
        in_specs=[spec(i, q) for i in range(n) for q in range(4)], out_specs=[spec(i, 4) for i in range(n)],
        out_shape=[jax.ShapeDtypeStruct((2,) + l.shape[1:], F32) for l in lands], vmem=48,
        args=[a for i in range(n) for a in (owns[i], lands[i], lands[i], lands[i])], stages=stages)
    return outs, landed


def _adamw_math(w, g, m, v):
    mn = ADAM_B1 * m + (1.0 - ADAM_B1) * g
    vn = ADAM_B2 * v + (1.0 - ADAM_B2) * (g * g)
    m_hat = mn / (1.0 - ADAM_B1 ** ADAM_STEP)
    v_hat = vn / (1.0 - ADAM_B2 ** ADAM_STEP)
    return -ADAM_LR * (m_hat / (jnp.sqrt(v_hat) + ADAM_EPS) + ADAM_WD * w), mn, vn


def _adamw(w, g, m, v, name, stages=()):
    rows, cols = w.shape
    br = _row_block(rows, cols)

    def body(w_ref, g_ref, m_ref, v_ref, go_ref, d_ref, mo_ref, vo_ref):
        gv = g_ref[...]
        go_ref[...] = gv
        d_ref[...], mo_ref[...], vo_ref[...] = _adamw_math(w_ref[...], gv, m_ref[...], v_ref[...])

    spec = pl.BlockSpec((br, cols), lambda r: (r, 0))
    return _call(body, name=name, grid=(rows // br,), in_specs=[spec] * 4, out_specs=[spec] * 4,
                 out_shape=[_sds(w)] * 4, vmem=56, args=[w, g, m, v], stages=stages)


def _adamw_multi(names, w, g, m, v, stages=()):
    cols = w[names[0]].shape[1]
    br = 128
    nblk = [w[n].shape[0] // br for n in names]
    starts = [sum(nblk[:i]) for i in range(len(names))]

    def body(*refs):
        r = pl.program_id(0)
        for i in range(len(names)):
            w_ref, g_ref, m_ref, v_ref = refs[4 * i:4 * i + 4]
            go_ref, d_ref, mo_ref, vo_ref = refs[4 * len(names) + 4 * i:4 * len(names) + 4 * i + 4]

            @pl.when(jnp.logical_and(r >= starts[i], r < starts[i] + nblk[i]))
            def _():
                gv = g_ref[...]
                go_ref[...] = gv
                d_ref[...], mo_ref[...], vo_ref[...] = _adamw_math(w_ref[...], gv, m_ref[...], v_ref[...])

    def spec(i):
        return pl.BlockSpec((br, cols), functools.partial(
            lambda r, s, nb: (jnp.clip(r - s, 0, nb - 1), 0), s=starts[i], nb=nblk[i]))

    outs, landed = _call(
        body, name="adamw_" + "_".join(names), grid=(sum(nblk),),
        in_specs=[spec(i) for i in range(len(names)) for _ in range(4)],
        out_specs=[spec(i) for i in range(len(names)) for _ in range(4)],
        out_shape=[_sds(w[n]) for n in names for _ in range(4)], vmem=56,
        args=[a[n] for n in names for a in (w, g, m, v)], stages=stages)
    return {n: outs[4 * i:4 * i + 4] for i, n in enumerate(names)}, landed


def _to_everyone(v):
    deltas = [(a, b, e) for a in (0, 1) for b in (0, 1) for e in (0, 1)][1:]

    def copies(ins, outs, sems):
        x, y, c, _ = _place()
        me = 4 * x + 2 * y + c
        flip = lambda p, f: 1 - p if f else p
        return [_rcopy(ins[0], outs[0].at[me], sems[0].at[q], sems[1].at[q], (flip(x, a), flip(y, b), flip(c, e)))
                for q, (a, b, e) in enumerate(deltas)]

    def start(ins, outs, sems):
        for cp in copies(ins, outs, sems):
            cp.start()

    def finish(ins, outs, sems):
        for cp in copies(ins, outs, sems):
            cp.wait()

    n = len(deltas)
    return _Stage([v], [jax.ShapeDtypeStruct((2 * NCHIP,) + v.shape, v.dtype)], {},
                  [pltpu.SemaphoreType.DMA((n,)), pltpu.SemaphoreType.DMA((n,))], start, finish)


SMALL_AT = {"norm_mix_pre": (0, 1, D), "norm_mix_post": (1, 1, D), "norm_mlp_pre": (2, 1, D),
            "norm_mlp_post": (3, 1, D), "b_gate": (4, 2, D), "conv_b": (6, 1, D), "lru_b_a": (7, 1, D),
            "lru_b_x": (8, 1, D), "lru_lambda": (9, 1, D), "pool_scale": (10, 1, DP)}
SMALL_SEPARATE = ["conv_w", "lru_w_a", "lru_w_x", "pool_w"]


def _adamw_small(small_sum, first_all, sep_grads, w, m, v):
    packed, sep = list(SMALL_AT), list(SMALL_SEPARATE)
    names = packed + sep

    def body(*refs):
        s_ref, a_ref, refs = refs[0], refs[1], refs[2:]
        g_sep, refs = refs[:len(sep)], refs[len(sep):]
        nn = len(names)
        w_r, m_r, v_r, refs = refs[:nn], refs[nn:2 * nn], refs[2 * nn:3 * nn], refs[3 * nn:]
        g_out, refs = refs[:len(packed)], refs[len(packed):]
        d_o, m_o, v_o = refs[:nn], refs[nn:2 * nn], refs[2 * nn:3 * nn]
        for i, n in enumerate(names):
            if i == 0:
                g = a_ref[0:1, :]
                for q in range(1, 2 * NCHIP):
                    g = g + a_ref[q:q + 1, :]
                g_out[i][...] = g
            elif n in SMALL_AT:
                r0, nr, nc = SMALL_AT[n]
                g = jnp.concatenate([s_ref[r0 + q:r0 + q + 1, :nc] for q in range(nr)], axis=1)
                g_out[i][...] = g
            else:
                g = g_sep[i - len(packed)][...]
            d_o[i][...], m_o[i][...], v_o[i][...] = _adamw_math(w_r[i][...], g, m_r[i][...], v_r[i][...])

    ws = [w[n] for n in names]
    res = pl.pallas_call(
        body, name="adamw_small",
        out_shape=[_sds(w[n]) for n in packed] + [_sds(a) for a in ws] * 3,
        compiler_params=_cp(32),
    )(*_hbm(small_sum, first_all, *sep_grads, *ws, *[m[n] for n in names], *[v[n] for n in names]))
    nn, npk = len(names), len(packed)
    grad = dict(zip(packed, res[:npk]))
    delta = dict(zip(names, res[npk:npk + nn]))
    new_m = dict(zip(names, res[npk + nn:npk + 2 * nn]))
    new_v = dict(zip(names, res[npk + 2 * nn:]))
    return grad, delta, new_m, new_v


W_NAMES = ["norm_mix_pre", "norm_mix_post", "norm_mlp_pre", "norm_mlp_post", "w_in", "b_gate", "conv_w", "conv_b",
           "lru_w_a", "lru_b_a", "lru_w_x", "lru_b_x", "lru_lambda", "pool_w", "pool_scale", "w_lru_up",
           "w_pool_up", "w_o", "w_ff1", "w_ff2"]
BIG = ["w_in", "w_lru_up", "w_pool_up", "w_o", "w_ff1", "w_ff2"]


def _block_diag(w):
    hd = w.shape[-1]
    per = CB // hd
    w4 = w.reshape(NG, per, hd, hd)
    eye = jnp.eye(per, dtype=w.dtype)
    return jnp.einsum("gpij,pq->gpiqj", w4, eye).reshape(NG, CB, CB)


def _block_diag_extract(d, hd):
    per = CB // hd
    d5 = d.reshape(NG, per, hd, per, hd)
    return jnp.stack([d5[:, p, :, p, :] for p in range(per)], axis=1).reshape(NG * per, hd, hd)


def _halves(g):
    return g.reshape(NCHIP, 2, g.size // (g.shape[-1] * 2 * NCHIP), g.shape[-1])


def kernel(x, norm_mix_pre, norm_mix_post, norm_mlp_pre, norm_mlp_post, w_in, b_gate, conv_w, conv_b, lru_w_a, lru_b_a, lru_w_x, lru_b_x, lru_lambda, pool_w, pool_scale, w_lru_up, w_pool_up, w_o, w_ff1, w_ff2, loss_target, m_norm_mix_pre, m_norm_mix_post, m_norm_mlp_pre, m_norm_mlp_post, m_w_in, m_b_gate, m_conv_w, m_conv_b, m_lru_w_a, m_lru_b_a, m_lru_w_x, m_lru_b_x, m_lru_lambda, m_pool_w, m_pool_scale, m_w_lru_up, m_w_pool_up, m_w_o, m_w_ff1, m_w_ff2, v_norm_mix_pre, v_norm_mix_post, v_norm_mlp_pre, v_norm_mlp_post, v_w_in, v_b_gate, v_conv_w, v_conv_b, v_lru_w_a, v_lru_b_a, v_lru_w_x, v_lru_b_x, v_lru_lambda, v_pool_w, v_pool_scale, v_w_lru_up, v_w_pool_up, v_w_o, v_w_ff1, v_w_ff2):
    args = dict(locals())
    two_d = lambda a: a.reshape(-1, a.shape[-1])
    w = {n: two_d(args[n]) for n in W_NAMES}
    mom = {n: two_d(args["m_" + n]) for n in W_NAMES}
    var = {n: two_d(args["v_" + n]) for n in W_NAMES}
    i32 = lambda val: jnp.asarray(val, jnp.int32)
    chip = i32(2 * lax.axis_index("x") + lax.axis_index("y"))
    core = i32(lax.axis_index("c"))
    cidx = core.reshape(1)
    zero = i32(0)
    hd = lru_w_a.shape[-1]
    xs, target = x[0], loss_target[0]
    g1, g2, g3, g4 = norm_mix_pre, norm_mix_post, norm_mlp_pre, norm_mlp_post

    mix = ["w_lru_up", "w_pool_up", "w_o"]
    full = {"w_in": _cast_place(w["w_in"], chip.reshape(1), "cast_w_in")}
    (fl_in, fl_conv), first = _split_call("gather_start_first", start=[
        _gather([full["w_in"]], ici=[(0, ALL)]), _gather_whole(w["conv_w"])])
    casts, _ = _cast_place_multi([w[n] for n in BIG[1:]], chip.reshape(1), stages=[_after(first)])
    full.update(zip(BIG[1:], casts))
    (fl_mix, fl_ff1, fl_ff2), started = _split_call("gather_start_rest", start=[
        _gather([full[n] for n in mix], ici=[(0, ALL), (1, ALL), (2, ALL)]),
        _gather([full["w_ff1"]], ici=[(0, ALL)]), _gather([full["w_ff2"]], ici=[(0, ALL)])])
    wa = _block_diag(lru_w_a[0]).astype(BF)
    wx = _block_diag(lru_w_x[0]).astype(BF)
    pw = pool_w[0].astype(BF)

    def to_sibling(name, flight, after=None):
        (fl,), passed = _split_call(name + "_pass", finish=[flight], after=after,
                                    start=[_gather(flight.landed(), d2d=[(i, ALL) for i in range(len(flight.bufs))])])
        passed_on.append(passed)
        return fl

    passed_on = []

    def arrived(name, flight, after=None):
        _split_call(name + "_done", finish=[flight], after=after)
        return flight.landed()

    idx_big = jnp.stack([chip, (chip + 1) % NCHIP, (chip + 2) % NCHIP, (chip + 3) % NCHIP, core])
    proj, h1 = _fwd_inproj_own(xs, g1, fl_in.bufs[0], idx_big, stages=[_after(started)])
    fl_in = to_sibling("gather_w_in", fl_in, after=h1)
    _split_call("gather_w_in_done", finish=[fl_in, fl_conv])
    (w_in_f,), (conv_all,) = fl_in.landed(), fl_conv.landed()
    full["w_in"] = w_in_f
    conv_all = lax.dynamic_update_slice(conv_all, w["conv_w"][None], (chip, zero, zero))
    conv_full = jnp.transpose(conv_all, (1, 0, 2)).reshape(4, DR)
    proj = _fwd_inproj_rest(h1, w_in_f, proj, idx_big)
    fl_mix = to_sibling("gather_mix", fl_mix, after=proj)
    (ylru, hs), _ = _fwd_lru(proj, conv_full, conv_b, wa, lru_b_a, wx, lru_b_x, lru_lambda,
                             stages=[_after(passed_on[-1])])
    got = arrived("gather_mix", fl_mix, after=ylru)
    fl_ff1 = to_sibling("gather_ff1", fl_ff1, after=ylru)
    w_lru_up_f, w_pool_up_f, w_o_f = got[0].reshape(DR, D), got[1], got[2].reshape(D, D)
    ypool = _fwd_pool(proj, pw, pool_scale)
    (x2, h2, m, mrg, bra, brb), _ = _fwd_merge(xs, ylru, ypool, proj, b_gate, g2, g3, w_lru_up_f, w_pool_up_f, w_o_f,
                                               stages=[_after(passed_on[-1])])
    fl_ff2 = to_sibling("gather_ff2", fl_ff2, after=h2)
    _split_call("gather_ff_done", finish=[fl_ff1, fl_ff2])
    (ff1,), (ff2,) = fl_ff1.landed(), fl_ff2.landed()
    ff2 = ff2.reshape(DF, D)
    a1, lossp, dy, df, dg4 = _fwd_mlp_loss(h2, ff1, ff2, x2, target, g4)

    dh2, df1 = _bwd_mlp_x(df, a1, ff1, ff2)
    dw_ff1, dw_ff2 = _bwd_mlp_w(df, h2, a1, df1)
    g_ff = [_halves(dw_ff1), _halves(dw_ff2)]
    (dxres, dgates, dylru, dypool, dm, dbra, dbrb, dg2, dg3, dbg), (l_ff,) = _bwd_merge(
        dh2, dy, x2, m, bra, brb, proj, b_gate, g2, g3, w_lru_up_f, w_pool_up_f, w_o_f, stages=[_to_sibling(g_ff)])
    p_ff = _add_sibling_multi(g_ff, l_ff, cidx, "add_sibling_ff")
    (fl_ff,), sent_ff = _split_call("reduce_ff_start", start=[_to_chips(p_ff)])
    (dw_o, dw_lru_up, dw_pool_up), _ = _dw_merge(mrg, dm, ylru, dbra, ypool, dbrb, stages=[_after(sent_ff)])
    g_mix = [_halves(dw_lru_up), _halves(dw_pool_up), _halves(dw_o)]
    (dxp, dgl, dcw, dcb, dwa, dba, dwx, dbx, dlam), (l_mix,) = _bwd_lru(
        proj, hs, dylru, conv_full, conv_b, wa, lru_b_a, wx, lru_b_x, lru_lambda, stages=[_to_sibling(g_mix)])
    p_mix = _add_sibling_multi(g_mix, l_mix, cidx, "add_sibling_mix")
    dxpool, dpw, dsc = _bwd_pool(proj, dypool, pw, pool_scale)
    dproj = [dxp, dgl, dxpool, dgates]
    small = jnp.concatenate([
        jnp.zeros((1, D), F32), dg2, dg3, dg4, dbg.reshape(2, D), dcb, dba, dbx, dlam,
        jnp.pad(dsc, ((0, 0), (0, D - DP))), jnp.pad(lossp, ((0, 0), (0, D - 1))), dcw,
        _block_diag_extract(dwa, hd).reshape(-1, D), _block_diag_extract(dwx, hd).reshape(-1, D),
        dpw.reshape(-1, D)], axis=0)
    (fl_mixr, fl_smalls), sent_mix = _split_call("reduce_mix_start", start=[_to_chips(p_mix), _to_sibling([small])])
    dw_in = _bwd_inproj_w(h1, dproj, sent_mix)
    _split_call("reduce_small_sibling_done", finish=[fl_smalls], after=dw_in)
    small, l_small = fl_smalls.bufs
    small2 = _add_pair(small, l_small, "add_sibling_small").reshape(2, SMALL_ROWS // 2, D)
    g_in = _halves(dw_in)
    done = ["w_ff1", "w_ff2"] + mix
    (fl_gin, fl_small), sib_started = _split_call("reduce_in_sibling_start", finish=[fl_ff, fl_mixr],
                                                  start=[_to_sibling([g_in]), _to_chips([small2])])
    p_ff1, p_ff2, c_ff1, c_ff2 = fl_ff.bufs
    p_mix, c_mix = fl_mixr.bufs[:3], fl_mixr.bufs[3:]
    pairs, _ = _add_chips_multi([p_ff1, p_ff2] + p_mix, [c_ff1, c_ff2] + c_mix, idx_big, "add_chips_done",
                                stages=[_after(sib_started)])
    _split_call("reduce_in_sibling_done", finish=[fl_gin], after=pairs[-1])
    g_in, l_in = fl_gin.bufs
    p_in = _add_sibling(g_in, l_in, cidx, "add_sibling_w_in")[0]
    (fl_pin,), token = _split_call("reduce_last_start", start=[_to_chips([p_in])])
    _split_call("reduce_small_done", finish=[fl_small], after=token)
    small2, c_small = fl_small.bufs
    own_small = lax.dynamic_index_in_dim(small2, core, 0, keepdims=True)
    c_small = lax.dynamic_update_slice(c_small, own_small, (chip, zero, zero))
    pair_small = _add_chips(c_small, c_small, jnp.stack([zero, zero + 1, zero + 2, zero + 3, core]), "add_chips_small")
    (fl_share,), shared_start = _split_call("reduce_share_start", start=[_share(pairs + [pair_small])])
    grad_x, dg1 = _bwd_inproj_x(dproj, full["w_in"], xs, dxres, g1, stages=[_after(shared_start)])
    _split_call("reduce_share_done", finish=[fl_share, fl_pin], after=dg1)
    shared, (p_in, c_in) = fl_share.landed(), fl_pin.bufs
    pairs, pair_small = shared[:-1], shared[-1]

    grads, delta, new_m, new_v = {}, {}, {}, {}
    for n, p in zip(done, pairs):
        grads[n] = p.reshape(-1, p.shape[-1])

    def update(n, stages=()):
        (grads[n], delta[n], new_m[n], new_v[n]), landed = _adamw(w[n], grads[n], mom[n], var[n], "adamw_" + n,
                                                                  stages=stages)
        return landed

    pair_in = _add_chips(p_in, c_in, idx_big, "add_chips_w_in")
    (fl_last, fl_dg1), last_start = _split_call("reduce_last_share_start", start=[_share([pair_in]), _to_everyone(dg1)])
    updated, _ = _adamw_multi(["w_ff1", "w_ff2", "w_o", "w_lru_up"], w, grads, mom, var, stages=[_after(last_start)])
    for n, (go, d, mo, vo) in updated.items():
        grads[n], delta[n], new_m[n], new_v[n] = go, d, mo, vo
    _split_call("reduce_last_share_done", finish=[fl_last, fl_dg1], after=new_v["w_lru_up"])
    (pair_in,), (dg1, dg1_all) = fl_last.landed(), fl_dg1.bufs
    dg1_all = lax.dynamic_update_slice(dg1_all, dg1[None], (2 * chip + core, zero, zero)).reshape(2 * NCHIP, D)
    grads["w_in"] = pair_in.reshape(-1, pair_in.shape[-1])
    update("w_pool_up")
    update("w_in")
    small_sum = pair_small.reshape(SMALL_ROWS, D)
    loss = 0.5 * small_sum[LOSS_ROW, 0]
    ccols = DR // NCHIP
    sep = [lax.dynamic_slice(small_sum[12:16], (zero, chip * ccols), (4, ccols)),
           small_sum[16:80].reshape(-1, hd), small_sum[80:144].reshape(-1, hd), small_sum[144:208].reshape(-1, PG)]
    g_s, d_s, m_s, v_s = _adamw_small(small_sum, dg1_all, sep, w, mom, var)
    grads.update(g_s)
    grads.update(dict(zip(SMALL_SEPARATE, sep)))
    delta.update(d_s)
    new_m.update(m_s)
    new_v.update(v_s)

    out = lambda d: [d[n].reshape(args[n].shape) for n in W_NAMES]
    return (loss, grad_x[None], *out(grads), *out(delta), *out(new_m), *out(new_v))
```

```python
import functools
import math

import jax
import jax.numpy as jnp
from jax import lax
from jax.experimental import pallas as pl
from jax.experimental.pallas import tpu as pltpu

F32 = jnp.float32
BF = jnp.bfloat16

T = 2048
D = 1024
DR = 1024
DP = 512
DF = 4096
DIN = 4608
NCHIP = 4
CW_IN = DIN // NCHIP
LANE = 128
CB = 128
NG = DR // CB
PG = 128
POOL_WINDOWS = (2, 4, 8, 16)
NORM_EPS = 1e-6
LRU_C = 8.0
GELU_C = math.sqrt(2.0 / math.pi)
ADAM_LR = 0.001
ADAM_B1 = 0.9
ADAM_B2 = 0.999
ADAM_EPS = 1e-08
ADAM_WD = 0.01
ADAM_STEP = 10
MESH_ID = pl.DeviceIdType.MESH
ANY = pl.BlockSpec(memory_space=pl.ANY)
SMALL_ROWS = 208
LOSS_ROW = 11
MIB = 1 << 20


def _cp(vmem_mib=None):
    if vmem_mib is None:
        return pltpu.CompilerParams()
    return pltpu.CompilerParams(vmem_limit_bytes=vmem_mib * MIB)


def _hbm(*arrays):
    return [pltpu.with_memory_space_constraint(a, pltpu.HBM) for a in arrays]


def _hbm_out(shapes):
    return [pltpu.HBM(s.shape, s.dtype) for s in shapes]


class _Stage:
    def __init__(self, operands, out_shape, alias, sems, start, finish):
        self.operands, self.out_shape, self.alias, self.sems = list(operands), list(out_shape), dict(alias), list(sems)
        self.start, self.finish = start, finish


def _call(body, *, name, grid, in_specs, out_specs, out_shape, args, vmem=None, stages=(), prefetch=None,
          scratch=()):
    nin, nout = len(in_specs), len(out_specs)
    npre = 0 if prefetch is None else 1
    st_args, st_shapes, st_sems, aliases = [], [], list(scratch), {}
    for st in stages:
        for k, v in st.alias.items():
            aliases[npre + nin + len(st_args) + k] = nout + len(st_shapes) + v
        st_args += st.operands
        st_shapes += st.out_shape
        st_sems += st.sems

    def wrapped(*refs):
        pre, refs = refs[:npre], refs[npre:]
        ins, pos = refs[:nin], nin
        st_ins = []
        for st in stages:
            st_ins.append(refs[pos:pos + len(st.operands)])
            pos += len(st.operands)
        outs, pos = refs[pos:pos + nout], pos + nout
        st_outs = []
        for st in stages:
            st_outs.append(refs[pos:pos + len(st.out_shape)])
            pos += len(st.out_shape)
        work, pos = refs[pos:pos + len(scratch)], pos + len(scratch)
        sems = []
        for st in stages:
            sems.append(refs[pos:pos + len(st.sems)])
            pos += len(st.sems)
        if stages:
            first = functools.reduce(jnp.logical_and, [pl.program_id(a) == 0 for a in range(len(grid))])

            @pl.when(first)
            def _():
                for st, a, b, s in zip(stages, st_ins, st_outs, sems):
                    st.start(a, b, s)

        body(*pre, *ins, *outs, *work)
        if stages:
            last = functools.reduce(jnp.logical_and, [pl.program_id(a) == g - 1 for a, g in enumerate(grid)])

            @pl.when(last)
            def _():
                for st, a, b, s in zip(stages, st_ins, st_outs, sems):
                    st.finish(a, b, s)

    all_in = list(in_specs) + [ANY] * len(st_args)
    all_out = list(out_specs) + [ANY] * len(st_shapes)
    kw = dict(has_side_effects=True) if stages else {}
    if vmem is not None:
        kw["vmem_limit_bytes"] = vmem * MIB
    if prefetch is None:
        gkw = dict(grid=grid, in_specs=all_in, out_specs=all_out, scratch_shapes=st_sems)
    else:
        gkw = dict(grid_spec=pltpu.PrefetchScalarGridSpec(
            num_scalar_prefetch=1, grid=grid, in_specs=all_in, out_specs=all_out, scratch_shapes=st_sems))
    res = pl.pallas_call(
        wrapped, name=name, out_shape=_hbm_out(list(out_shape) + st_shapes), input_output_aliases=aliases,
        compiler_params=pltpu.CompilerParams(**kw), **gkw,
    )(*([prefetch] if npre else []), *_hbm(*args, *st_args))
    outs, rest, st_res = list(res[:nout]), list(res[nout:]), []
    for st in stages:
        st_res.append(rest[:len(st.out_shape)])
        rest = rest[len(st.out_shape):]
    return outs, st_res


def _mm(a, b):
    return jnp.dot(a.astype(BF), b.astype(BF), preferred_element_type=F32)


def _mm_nt(a, b):
    return lax.dot_general(a.astype(BF), b.astype(BF), (((1,), (1,)), ((), ())),
                           preferred_element_type=F32)


def _mm_tn(a, b):
    return lax.dot_general(a.astype(BF), b.astype(BF), (((0,), (0,)), ((), ())),
                           preferred_element_type=F32)


def _rows(v):
    return lax.broadcasted_iota(jnp.int32, v.shape, 0)


def _sd(v, s, fill=0.0):
    return jnp.where(_rows(v) >= s, pltpu.roll(v, s, axis=0), fill)


def _su(v, s, fill=0.0):
    n = v.shape[0]
    return jnp.where(_rows(v) < n - s, pltpu.roll(v, n - s, axis=0), fill)


def _sigmoid(z):
    return 1.0 / (1.0 + jnp.exp(-z))


def _softplus(z):
    e = jnp.exp(-jnp.abs(z))
    u = 1.0 + e
    d = u - 1.0
    log1p = jnp.where(d == 0.0, e, jnp.log(u) * (e / jnp.where(d == 0.0, 1.0, d)))
    return jnp.maximum(z, 0.0) + log1p


def _mean(v):
    return jnp.mean(v, axis=-1, keepdims=True)


def _colsum(v):
    return jnp.sum(v, axis=0, keepdims=True)


def _acc(ref, val, first):
    @pl.when(first)
    def _():
        ref[...] = val

    @pl.when(jnp.logical_not(first))
    def _():
        ref[...] += val


def _conv(xp, cw, cb):
    x1, x2, x3 = _sd(xp, 1), _sd(xp, 2), _sd(xp, 3)
    xc = cb + cw[0:1] * x3 + cw[1:2] * x2 + cw[2:3] * x1 + cw[3:4] * xp
    return xc, x1, x2, x3


def _lru_gates(xc, wa, ba, wx, bx, lam):
    xcb = xc.astype(BF)
    r = _sigmoid(_mm(xcb, wa) + ba)
    ii = _sigmoid(_mm(xcb, wx) + bx)
    sp = _softplus(-lam)
    la = (-LRU_C) * r * sp
    a = jnp.exp(la)
    mult = jnp.sqrt(-jnp.tanh(la) * (a * a + 1.0))
    return xcb, r, ii, sp, a, mult


def _gelu_parts(g):
    th = jnp.tanh(GELU_C * (g + 0.044715 * (g * g * g)))
    gel = 0.5 * g * (1.0 + th)
    dgel = 0.5 * (1.0 + th) + 0.5 * g * (1.0 - th * th) * (GELU_C * (1.0 + 3.0 * 0.044715 * (g * g)))
    return gel, dgel


def _tile_scan(a, b, a_s, b_s, out_ref, reverse):
    n, lanes = a.shape
    nt = n // 8
    a, b = a.reshape(nt, 8, lanes), b.reshape(nt, 8, lanes)
    sub = lax.broadcasted_iota(jnp.int32, a.shape, 1)
    s = 1
    while s < 8:
        keep = sub < 8 - s if reverse else sub >= s
        amount = 8 - s if reverse else s
        b = b + a * jnp.where(keep, pltpu.roll(b, amount, axis=1), 0.0)
        a = a * jnp.where(keep, pltpu.roll(a, amount, axis=1), 1.0)
        s *= 2
    a_s[...] = a.reshape(n, lanes)
    b_s[...] = b.reshape(n, lanes)
    edge = pl.ds(0 if reverse else 7, nt, stride=8)
    ta, tb = a_s[edge, :], b_s[edge, :]
    shift = _su if reverse else _sd
    s = 1
    while s < nt:
        tb = tb + ta * shift(tb, s, 0.0)
        if 2 * s < nt:
            ta = ta * shift(ta, s, 1.0)
        s *= 2
    enters = shift(tb, 1, 0.0)
    for o in range(8):
        rows = pl.ds(o, nt, stride=8)
        out_ref[rows, :] = b_s[rows, :] + a_s[rows, :] * enters


def _pool_window(x, steps, shift):
    s, sh = x, 1
    for _ in range(steps):
        s = s + shift(s, sh)
        sh *= 2
    return s


def _fwd_inproj_own(x, g1, w_in, slots, stages=()):
    tm = 1024

    def body(s_ref, x_ref, g_ref, w_ref, proj_ref, h_ref):
        xv = x_ref[...]
        r = lax.rsqrt(_mean(xv * xv) + NORM_EPS)
        h = ((xv * r) * g_ref[...]).astype(BF)
        h_ref[...] = h
        proj_ref[...] = jnp.dot(h, w_ref[0], preferred_element_type=F32)

    return _call(
        body, name="fwd_inproj_own", grid=(T // tm,), prefetch=slots,
        in_specs=[pl.BlockSpec((tm, D), lambda i, s: (i, 0)),
                  pl.BlockSpec((1, D), lambda i, s: (0, 0)),
                  pl.BlockSpec((1, D, CW_IN), lambda i, s: (s[0], 0, 0))],
        out_specs=[pl.BlockSpec((tm, CW_IN), lambda i, s: (i, s[0])),
                   pl.BlockSpec((tm, D), lambda i, s: (i, 0))],
        out_shape=[jax.ShapeDtypeStruct((T, DIN), F32), jax.ShapeDtypeStruct((T, D), BF)],
        vmem=40, args=[x, g1, w_in], stages=stages)[0]


def _fwd_inproj_rest(h1, w_in, proj, slots):
    tm = 1024

    def body(s_ref, h_ref, w_ref, p_in, proj_ref):
        proj_ref[...] = jnp.dot(h_ref[...], w_ref[0], preferred_element_type=F32)

    res = pl.pallas_call(
        body, name="fwd_inproj_rest",
        grid_spec=pltpu.PrefetchScalarGridSpec(
            num_scalar_prefetch=1, grid=(NCHIP - 1, T // tm),
            in_specs=[pl.BlockSpec((tm, D), lambda k, i, s: (i, 0)),
                      pl.BlockSpec((1, D, CW_IN), lambda k, i, s: (s[1 + k], 0, 0)), ANY],
            out_specs=pl.BlockSpec((tm, CW_IN), lambda k, i, s: (i, s[1 + k]))),
        out_shape=pltpu.HBM((T, DIN), F32), input_output_aliases={3: 0},
        compiler_params=_cp(40),
    )(slots, *_hbm(h1, w_in, proj))
    return res


def _vec_spec():
    return pl.BlockSpec((1, CB), lambda j: (0, j))


def _fwd_lru(proj, conv_w, conv_b, wa, ba, wx, bx, lam, stages=()):
    def body(xp_ref, g_ref, cw_ref, cb_ref, wa_ref, ba_ref, wx_ref, bx_ref, lam_ref, y_ref, h_ref, a_s, b_s):
        xc, _, _, _ = _conv(xp_ref[...], cw_ref[...], cb_ref[...])
        _, _, ii, _, a, mult = _lru_gates(xc, wa_ref[0], ba_ref[...], wx_ref[0], bx_ref[...], lam_ref[...])
        _tile_scan(a, mult * (ii * xc), a_s, b_s, h_ref, reverse=False)
        gel, _ = _gelu_parts(g_ref[...])
        y_ref[...] = (h_ref[...] * gel).astype(BF)

    return _call(
        body, name="fwd_lru", grid=(NG,),
        in_specs=[pl.BlockSpec((T, CB), lambda j: (0, j)),
                  pl.BlockSpec((T, CB), lambda j: (0, NG + j)),
                  pl.BlockSpec((4, CB), lambda j: (0, j)),
                  _vec_spec(),
                  pl.BlockSpec((1, CB, CB), lambda j: (j, 0, 0)), _vec_spec(),
                  pl.BlockSpec((1, CB, CB), lambda j: (j, 0, 0)), _vec_spec(),
                  _vec_spec()],
        out_specs=[pl.BlockSpec((T, CB), lambda j: (0, j)), pl.BlockSpec((T, CB), lambda j: (0, j))],
        out_shape=[jax.ShapeDtypeStruct((T, DR), BF), jax.ShapeDtypeStruct((T, DR), F32)],
        vmem=48, args=[proj, proj, conv_w, conv_b, wa, ba, wx, bx, lam], stages=stages,
        scratch=[pltpu.VMEM((T, CB), F32)] * 2)


def _pool_cnt(w):
    t = lax.broadcasted_iota(jnp.int32, (T, 1), 0)
    return jnp.minimum(t + 1, w).astype(F32)


def _fwd_pool(proj, pool_w, pool_scale):
    def body(xp_ref, pw_ref, sc_ref, y_ref):
        for g, w in enumerate(POOL_WINDOWS):
            cols = slice(g * PG, (g + 1) * PG)
            x = xp_ref[:, cols]
            p = _pool_window(x, g + 1, _sd) / _pool_cnt(w) - x
            y_ref[:, cols] = (_mm(p, pw_ref[g]) * sc_ref[:, cols]).astype(BF)

    return pl.pallas_call(
        body, name="fwd_pool", grid=(1,),
        in_specs=[pl.BlockSpec((T, DP), lambda i: (0, 2 * DR // DP)),
                  pl.BlockSpec((4, PG, PG), lambda i: (0, 0, 0)),
                  pl.BlockSpec((1, DP), lambda i: (0, 0))],
        out_specs=pl.BlockSpec((T, DP), lambda i: (0, 0)),
        out_shape=pltpu.HBM((T, DP), BF),
        compiler_params=_cp(48),
    )(*_hbm(proj, pool_w, pool_scale))


GATE_BLK = 512
GATE_BLK0 = (2 * DR + DP) // GATE_BLK


def _gate_specs(tm):
    return [pl.BlockSpec((tm, GATE_BLK), functools.partial(lambda i, q: (i, GATE_BLK0 + q), q=q))
            for q in range(4)]


def _fwd_merge(x, ylru, ypool, proj, b_gate, g2, g3, w_lru_up, w_pool_up, w_o, stages=()):
    tm = 512

    def body(x_ref, yl_ref, yp_ref, p0, p1, p2, p3, bg_ref, g2_ref, g3_ref, wl_ref, wp_ref, wo_ref,
             x2_ref, h2_ref, m_ref, mrg_ref, bra_ref, brb_ref):
        bra = jnp.dot(yl_ref[...], wl_ref[...], preferred_element_type=F32)
        yp = yp_ref[...]
        brb = jnp.concatenate([jnp.dot(yp, wp_ref[k], preferred_element_type=F32) for k in range(NCHIP)], axis=1)
        bg = bg_ref[...]
        ga = _sigmoid(jnp.concatenate([p0[...], p1[...]], axis=1) + bg[:, :D])
        gb = _sigmoid(jnp.concatenate([p2[...], p3[...]], axis=1) + bg[:, D:])
        mrg = (ga * bra + gb * brb).astype(BF)
        m = jnp.dot(mrg, wo_ref[...], preferred_element_type=F32)
        r2 = lax.rsqrt(_mean(m * m) + NORM_EPS)
        x2 = x_ref[...] + (m * r2) * g2_ref[...]
        r3 = lax.rsqrt(_mean(x2 * x2) + NORM_EPS)
        x2_ref[...] = x2
        h2_ref[...] = ((x2 * r3) * g3_ref[...]).astype(BF)
        m_ref[...] = m
        mrg_ref[...] = mrg
        bra_ref[...] = bra.astype(BF)
        brb_ref[...] = brb.astype(BF)

    row = lambda w: pl.BlockSpec((tm, w), lambda i: (i, 0))
    full2 = lambda a, b: pl.BlockSpec((a, b), lambda i: (0, 0))
    return _call(
        body, name="fwd_merge", grid=(T // tm,),
        in_specs=[row(D), row(DR), row(DP)] + _gate_specs(tm) +
                 [full2(1, 2 * D), full2(1, D), full2(1, D), full2(DR, D),
                  pl.BlockSpec((NCHIP, DP, D // NCHIP), lambda i: (0, 0, 0)), full2(D, D)],
        out_specs=[row(D)] * 6,
        out_shape=[jax.ShapeDtypeStruct((T, D), F32), jax.ShapeDtypeStruct((T, D), BF),
                   jax.ShapeDtypeStruct((T, D), F32), jax.ShapeDtypeStruct((T, D), BF),
                   jax.ShapeDtypeStruct((T, D), BF), jax.ShapeDtypeStruct((T, D), BF)],
        vmem=48, args=[x, ylru, ypool, proj, proj, proj, proj, b_gate, g2, g3, w_lru_up, w_pool_up, w_o],
        stages=stages)


def _fwd_mlp_loss(h2, w_ff1, w_ff2, x2, target, g4):
    tm = 512
    fk = DF // NCHIP

    def body(h_ref, w1_ref, w2_ref, x2_ref, t_ref, g_ref, a1_ref, loss_ref, dy_ref, df_ref, dg_ref):
        first = pl.program_id(0) == 0
        h = h_ref[...]
        f = None
        for k in range(NCHIP):
            a1 = jnp.maximum(jnp.dot(h, w1_ref[k], preferred_element_type=F32), 0.0)
            a1_ref[:, k * fk:(k + 1) * fk] = a1.astype(BF)
            part = jnp.dot((a1 * a1).astype(BF), w2_ref[k * fk:(k + 1) * fk, :], preferred_element_type=F32)
            f = part if f is None else f + part
        g4v = g_ref[...]
        r4 = lax.rsqrt(_mean(f * f) + NORM_EPS)
        fn = f * r4
        e = (x2_ref[...] + fn * g4v) - t_ref[...]
        _acc(loss_ref, jnp.sum(_mean(e * e), axis=0, keepdims=True), first)
        dy = e * (1.0 / D)
        dy_ref[...] = dy
        _acc(dg_ref, _colsum(dy * fn), first)
        dfn = dy * g4v
        df_ref[...] = (r4 * (dfn - fn * _mean(dfn * fn))).astype(BF)

    row = pl.BlockSpec((tm, D), lambda i: (i, 0))
    return pl.pallas_call(
        body, name="fwd_mlp_loss", grid=(T // tm,),
        in_specs=[row, pl.BlockSpec((NCHIP, D, fk), lambda i: (0, 0, 0)), pl.BlockSpec((DF, D), lambda i: (0, 0)),
                  row, row, pl.BlockSpec((1, D), lambda i: (0, 0))],
        out_specs=[pl.BlockSpec((tm, DF), lambda i: (i, 0)), pl.BlockSpec((1, 1), lambda i: (0, 0)), row, row,
                   pl.BlockSpec((1, D), lambda i: (0, 0))],
        out_shape=_hbm_out([jax.ShapeDtypeStruct((T, DF), BF), jax.ShapeDtypeStruct((1, 1), F32),
                            jax.ShapeDtypeStruct((T, D), F32), jax.ShapeDtypeStruct((T, D), BF),
                            jax.ShapeDtypeStruct((1, D), F32)]),
        compiler_params=_cp(56),
    )(*_hbm(h2, w_ff1, w_ff2, x2, target, g4))


def _bwd_mlp_x(df, a1, w_ff1, w_ff2):
    tm = 512
    fk = DF // NCHIP

    def body(df_ref, a1_ref, w1_ref, w2_ref, dh_ref, df1_ref):
        df = df_ref[...]
        dh = None
        for k in range(NCHIP):
            cols = slice(k * fk, (k + 1) * fk)
            dact = _mm_nt(df, w2_ref[cols, :])
            df1 = (dact * (2.0 * a1_ref[:, cols].astype(F32))).astype(BF)
            df1_ref[:, cols] = df1
            part = _mm_nt(df1, w1_ref[k])
            dh = part if dh is None else dh + part
        dh_ref[...] = dh

    return pl.pallas_call(
        body, name="bwd_mlp_x", grid=(T // tm,),
        in_specs=[pl.BlockSpec((tm, D), lambda i: (i, 0)),
                  pl.BlockSpec((tm, DF), lambda i: (i, 0)),
                  pl.BlockSpec((NCHIP, D, fk), lambda i: (0, 0, 0)),
                  pl.BlockSpec((DF, D), lambda i: (0, 0))],
        out_specs=[pl.BlockSpec((tm, D), lambda i: (i, 0)), pl.BlockSpec((tm, DF), lambda i: (i, 0))],
        out_shape=_hbm_out([jax.ShapeDtypeStruct((T, D), F32), jax.ShapeDtypeStruct((T, DF), BF)]),
        compiler_params=_cp(56),
    )(*_hbm(df, a1, w_ff1, w_ff2))


def _bwd_mlp_w(df, h2, a1, df1):
    fc = 512
    per = (DF // NCHIP) // fc

    def body(df_ref, h_ref, a1_ref, df1_ref, dw1_ref, dw2_ref):
        a1 = a1_ref[...].astype(F32)
        dw2_ref[...] = _mm_tn((a1 * a1).astype(BF), df_ref[...]).astype(BF)
        dw1_ref[0] = _mm_tn(h_ref[...], df1_ref[...]).astype(BF)

    return pl.pallas_call(
        body, name="bwd_mlp_w", grid=(DF // fc,),
        in_specs=[pl.BlockSpec((T, D), lambda j: (0, 0)),
                  pl.BlockSpec((T, D), lambda j: (0, 0)),
                  pl.BlockSpec((T, fc), lambda j: (0, j)),
                  pl.BlockSpec((T, fc), lambda j: (0, j))],
        out_specs=[pl.BlockSpec((1, D, fc), lambda j: (j // per, 0, j % per)),
                   pl.BlockSpec((fc, D), lambda j: (j, 0))],
        out_shape=_hbm_out([jax.ShapeDtypeStruct((NCHIP, D, DF // NCHIP), BF),
                            jax.ShapeDtypeStruct((DF, D), BF)]),
        compiler_params=_cp(56),
    )(*_hbm(df, h2, a1, df1))


def _bwd_merge(dh2, dy, x2, m, bra, brb, proj, b_gate, g2, g3, w_lru_up, w_pool_up, w_o, stages=()):
    tm = 256
    cpu = D // NCHIP

    def body(dh2_ref, dy_ref, x2_ref, m_ref, bra_ref, brb_ref, p0, p1, p2, p3, bg_ref,
             g2_ref, g3_ref, wl_ref, wp_ref, wo_ref,
             dx_ref, dgt_ref, dyl_ref, dyp_ref, dm_ref, dbra_ref, dbrb_ref, dg2_ref, dg3_ref, dbg_ref):
        first = pl.program_id(0) == 0
        x2 = x2_ref[...]
        r3 = lax.rsqrt(_mean(x2 * x2) + NORM_EPS)
        x2n = x2 * r3
        dh2 = dh2_ref[...]
        t3 = dh2 * g3_ref[...]
        dx2 = dy_ref[...] + r3 * (t3 - x2n * _mean(t3 * x2n))
        dx_ref[...] = dx2
        _acc(dg3_ref, _colsum(dh2 * x2n), first)
        m = m_ref[...]
        r2 = lax.rsqrt(_mean(m * m) + NORM_EPS)
        mn = m * r2
        _acc(dg2_ref, _colsum(dx2 * mn), first)
        dmn = dx2 * g2_ref[...]
        dm = (r2 * (dmn - mn * _mean(dmn * mn))).astype(BF)
        dm_ref[...] = dm
        dmrg = _mm_nt(dm, wo_ref[...])
        bg = bg_ref[...]
        ga = _sigmoid(jnp.concatenate([p0[...], p1[...]], axis=1) + bg[:, :D])
        gb = _sigmoid(jnp.concatenate([p2[...], p3[...]], axis=1) + bg[:, D:])
        dga = dmrg * bra_ref[...].astype(F32) * (ga * (1.0 - ga))
        dgb = dmrg * brb_ref[...].astype(F32) * (gb * (1.0 - gb))
        dgt_ref[:, :D] = dga.astype(BF)
        dgt_ref[:, D:] = dgb.astype(BF)
        _acc(dbg_ref, jnp.concatenate([_colsum(dga), _colsum(dgb)], axis=1), first)
        dbra = (dmrg * ga).astype(BF)
        dbrb = (dmrg * gb).astype(BF)
        dbra_ref[...] = dbra
        dbrb_ref[...] = dbrb
        dyl_ref[...] = _mm_nt(dbra, wl_ref[...])
        dyp = None
        for k in range(NCHIP):
            part = _mm_nt(dbrb[:, k * cpu:(k + 1) * cpu], wp_ref[k])
            dyp = part if dyp is None else dyp + part
        dyp_ref[...] = dyp

    row = lambda w: pl.BlockSpec((tm, w), lambda i: (i, 0))
    full2 = lambda a, b: pl.BlockSpec((a, b), lambda i: (0, 0))
    wp_spec = pl.BlockSpec((NCHIP, DP, cpu), lambda i: (0, 0, 0))
    return _call(
        body, name="bwd_merge", grid=(T // tm,),
        in_specs=[row(D)] * 6 + _gate_specs(tm) +
                 [full2(1, 2 * D), full2(1, D), full2(1, D), full2(DR, D), wp_spec, full2(D, D)],
        out_specs=[row(D), row(2 * D), row(DR), row(DP), row(D), row(D), row(D),
                   full2(1, D), full2(1, D), full2(1, 2 * D)],
        out_shape=[jax.ShapeDtypeStruct((T, D), F32), jax.ShapeDtypeStruct((T, 2 * D), BF),
                   jax.ShapeDtypeStruct((T, DR), F32), jax.ShapeDtypeStruct((T, DP), F32),
                   jax.ShapeDtypeStruct((T, D), BF), jax.ShapeDtypeStruct((T, D), BF),
                   jax.ShapeDtypeStruct((T, D), BF),
                   jax.ShapeDtypeStruct((1, D), F32), jax.ShapeDtypeStruct((1, D), F32),
                   jax.ShapeDtypeStruct((1, 2 * D), F32)],
        vmem=56, args=[dh2, dy, x2, m, bra, brb, proj, proj, proj, proj, b_gate, g2, g3, w_lru_up, w_pool_up, w_o],
        stages=stages)


def _dw_merge(mrg, dm, ylru, dbra, ypool, dbrb, stages=()):
    nb = NCHIP
    rb, pb, cpu = D // nb, DP // nb, D // NCHIP

    def body(mrg_ref, dm_ref, yl_ref, dbra_ref, yp_ref, dbrb_ref, dwo_ref, dwl_ref, dwp_ref):
        dwo_ref[...] = _mm_tn(mrg_ref[...], dm_ref[...]).astype(BF)
        dwl_ref[...] = _mm_tn(yl_ref[...], dbra_ref[...]).astype(BF)
        dwp = _mm_tn(yp_ref[...], dbrb_ref[...]).astype(BF)
        for k in range(NCHIP):
            dwp_ref[k] = dwp[:, k * cpu:(k + 1) * cpu]

    cols = lambda w: pl.BlockSpec((T, w), lambda r: (0, r))
    whole = pl.BlockSpec((T, D), lambda r: (0, 0))
    return _call(
        body, name="dw_merge", grid=(nb,),
        in_specs=[cols(rb), whole, cols(rb), whole, cols(pb), whole],
        out_specs=[pl.BlockSpec((rb, D), lambda r: (r, 0)), pl.BlockSpec((rb, D), lambda r: (r, 0)),
                   pl.BlockSpec((NCHIP, pb, cpu), lambda r: (0, r, 0))],
        out_shape=[jax.ShapeDtypeStruct((D, D), BF), jax.ShapeDtypeStruct((DR, D), BF),
                   jax.ShapeDtypeStruct((NCHIP, DP, cpu), BF)],
        vmem=56, args=[mrg, dm, ylru, dbra, ypool, dbrb], stages=stages)


def _bwd_lru(proj, h, dylru, conv_w, conv_b, wa, ba, wx, bx, lam, stages=()):
    def body(xp_ref, g_ref, h_ref, dy_ref, cw_ref, cb_ref, wa_ref, ba_ref, wx_ref, bx_ref, lam_ref,
             dxp_ref, dg_ref, dcw_ref, dcb_ref, dwa_ref, dba_ref, dwx_ref, dbx_ref, dlam_ref, a_s, b_s, l_s):
        xp = xp_ref[...]
        cw = cw_ref[...]
        lam = lam_ref[...]
        xc, x1, x2, x3 = _conv(xp, cw, cb_ref[...])
        wa, wx = wa_ref[0], wx_ref[0]
        xcb, r, ii, sp, a, mult = _lru_gates(xc, wa, ba_ref[...], wx, bx_ref[...], lam)
        g = g_ref[...]
        gel, dgel = _gelu_parts(g)
        h = h_ref[...]
        dy = dy_ref[...]
        dg_ref[...] = (dy * h * dgel).astype(BF)
        _tile_scan(_su(a, 1, 0.0), dy * gel, a_s, b_s, l_s, reverse=True)
        b = l_s[...]
        da = b * _sd(h, 1, 0.0)
        dmult = b * (ii * xc)
        dii = b * (mult * xc)
        dxc = b * (mult * ii)
        dla = da * a - dmult * ((a * a) / mult)
        dr = dla * ((-LRU_C) * sp)
        dsp = _colsum(dla * ((-LRU_C) * r))
        dlam_ref[...] = -dsp / (1.0 + jnp.exp(lam))
        dzr = dr * (r * (1.0 - r))
        dzi = dii * (ii * (1.0 - ii))
        dzrb, dzib = dzr.astype(BF), dzi.astype(BF)
        dxc = dxc + _mm_nt(dzrb, wa) + _mm_nt(dzib, wx)
        dwa_ref[0] = _mm_tn(xcb, dzrb)
        dwx_ref[0] = _mm_tn(xcb, dzib)
        dba_ref[...] = _colsum(dzr)
        dbx_ref[...] = _colsum(dzi)
        dcb_ref[...] = _colsum(dxc)
        dcw_ref[...] = jnp.concatenate([_colsum(dxc * x3), _colsum(dxc * x2), _colsum(dxc * x1),
                                        _colsum(dxc * xp)], axis=0)
        dxp = cw[3:4] * dxc + cw[2:3] * _su(dxc, 1) + cw[1:2] * _su(dxc, 2) + cw[0:1] * _su(dxc, 3)
        dxp_ref[...] = dxp.astype(BF)

    blk = pl.BlockSpec((T, CB), lambda j: (0, j))
    wsp = pl.BlockSpec((1, CB, CB), lambda j: (j, 0, 0))
    return _call(
        body, name="bwd_lru", grid=(NG,),
        in_specs=[blk, pl.BlockSpec((T, CB), lambda j: (0, NG + j)), blk, blk,
                  pl.BlockSpec((4, CB), lambda j: (0, j)), _vec_spec(), wsp, _vec_spec(), wsp, _vec_spec(),
                  _vec_spec()],
        out_specs=[blk, blk, pl.BlockSpec((4, CB), lambda j: (0, j)), _vec_spec(), wsp, _vec_spec(), wsp,
                   _vec_spec(), _vec_spec()],
        out_shape=[jax.ShapeDtypeStruct((T, DR), BF), jax.ShapeDtypeStruct((T, DR), BF),
                   jax.ShapeDtypeStruct((4, DR), F32), jax.ShapeDtypeStruct((1, DR), F32),
                   jax.ShapeDtypeStruct((NG, CB, CB), F32), jax.ShapeDtypeStruct((1, DR), F32),
                   jax.ShapeDtypeStruct((NG, CB, CB), F32), jax.ShapeDtypeStruct((1, DR), F32),
                   jax.ShapeDtypeStruct((1, DR), F32)],
        vmem=56, args=[proj, proj, h, dylru, conv_w, conv_b, wa, ba, wx, bx, lam], stages=stages,
        scratch=[pltpu.VMEM((T, CB), F32)] * 3)


def _bwd_pool(proj, dypool, pool_w, pool_scale):
    def body(xp_ref, dy_ref, pw_ref, sc_ref, dx_ref, dw_ref, dsc_ref):
        for g, w in enumerate(POOL_WINDOWS):
            cols = slice(g * PG, (g + 1) * PG)
            cnt = _pool_cnt(w)
            x = xp_ref[:, cols]
            pb = (_pool_window(x, g + 1, _sd) / cnt - x).astype(BF)
            wg = pw_ref[g]
            dy = dy_ref[:, cols]
            dsc_ref[:, cols] = _colsum(dy * _mm(pb, wg))
            dyp = (dy * sc_ref[:, cols]).astype(BF)
            dw_ref[g] = _mm_tn(pb, dyp)
            dp = _mm_nt(dyp, wg)
            dx_ref[:, cols] = (_pool_window(dp / cnt, g + 1, _su) - dp).astype(BF)

    return pl.pallas_call(
        body, name="bwd_pool", grid=(1,),
        in_specs=[pl.BlockSpec((T, DP), lambda i: (0, 2 * DR // DP)),
                  pl.BlockSpec((T, DP), lambda i: (0, 0)),
                  pl.BlockSpec((4, PG, PG), lambda i: (0, 0, 0)),
                  pl.BlockSpec((1, DP), lambda i: (0, 0))],
        out_specs=[pl.BlockSpec((T, DP), lambda i: (0, 0)),
                   pl.BlockSpec((4, PG, PG), lambda i: (0, 0, 0)),
                   pl.BlockSpec((1, DP), lambda i: (0, 0))],
        out_shape=_hbm_out([jax.ShapeDtypeStruct((T, DP), BF), jax.ShapeDtypeStruct((4, PG, PG), F32),
                            jax.ShapeDtypeStruct((1, DP), F32)]),
        compiler_params=_cp(48),
    )(*_hbm(proj, dypool, pool_w, pool_scale))


PART_COLS = (DR, DR, DP, 2 * D)


def _shard_pieces():
    starts = [sum(PART_COLS[:p]) for p in range(len(PART_COLS))]
    shards = []
    for k in range(NCHIP):
        lo, hi = k * CW_IN, (k + 1) * CW_IN
        shards.append([(p, max(lo, s) - s, min(hi, s + wd) - s, max(lo, s) - lo)
                       for p, (s, wd) in enumerate(zip(starts, PART_COLS)) if max(lo, s) < min(hi, s + wd)])
    return shards


def _bwd_inproj_w(h1, parts, after):
    flat = [(k, *piece) for k, pieces in enumerate(_shard_pieces()) for piece in pieces]

    def body(h_hbm, p0, p1, p2, p3, after_ref, dw_hbm, h_v, dw_v, *rest):
        bufs, sem_in, sem_out = rest[:len(flat)], rest[len(flat)], rest[len(flat) + 1]
        part_refs = (p0, p1, p2, p3)
        loads = [pltpu.make_async_copy(h_hbm, h_v, sem_in.at[0])]
        for i, (k, p, a, b, c0) in enumerate(flat):
            loads.append(pltpu.make_async_copy(part_refs[p].at[:, pl.ds(a, b - a)], bufs[i], sem_in.at[1 + i]))
        for cp in loads:
            cp.start()
        loads[0].wait()
        stores = []
        for i, (k, p, a, b, c0) in enumerate(flat):
            loads[1 + i].wait()
            dw_v[k, :, c0:c0 + b - a] = _mm_tn(h_v[...], bufs[i][...]).astype(BF)
            if i + 1 == len(flat) or flat[i + 1][0] != k:
                stores.append(pltpu.make_async_copy(dw_v.at[k], dw_hbm.at[k], sem_out.at[k]))
                stores[-1].start()
        for cp in stores:
            cp.wait()

    scratch = [pltpu.VMEM((T, D), BF), pltpu.VMEM((NCHIP, D, CW_IN), BF)]
    scratch += [pltpu.VMEM((T, b - a), parts[p].dtype) for k, p, a, b, c0 in flat]
    scratch += [pltpu.SemaphoreType.DMA((1 + len(flat),)), pltpu.SemaphoreType.DMA((NCHIP,))]
    return pl.pallas_call(
        body, name="bwd_inproj_w", in_specs=[ANY] * 6, out_specs=ANY, scratch_shapes=scratch,
        out_shape=pltpu.HBM((NCHIP, D, CW_IN), BF), compiler_params=_cp(48),
    )(*_hbm(h1, *parts), after)


def _bwd_inproj_x(parts, w_in, x, dxres, g1, stages=()):
    tm = 512

    def body(p0, p1, p2, p3, w_ref, x_ref, dr_ref, g_ref, dx_ref, dg_ref):
        part_refs = (p0, p1, p2, p3)
        dh = None
        for k, pieces in enumerate(_shard_pieces()):
            for p, a, b, c0 in pieces:
                part = _mm_nt(part_refs[p][:, a:b], w_ref[k, :, c0:c0 + b - a])
                dh = part if dh is None else dh + part
        xv = x_ref[...]
        r = lax.rsqrt(_mean(xv * xv) + NORM_EPS)
        xn = xv * r
        t = dh * g_ref[...]
        dx_ref[...] = dr_ref[...] + r * (t - xn * _mean(t * xn))
        _acc(dg_ref, _colsum(dh * xn), pl.program_id(0) == 0)

    row = pl.BlockSpec((tm, D), lambda i: (i, 0))
    vec = pl.BlockSpec((1, D), lambda i: (0, 0))
    return _call(
        body, name="bwd_inproj_x", grid=(T // tm,),
        in_specs=[pl.BlockSpec((tm, wd), lambda i: (i, 0)) for wd in PART_COLS] +
                 [pl.BlockSpec((NCHIP, D, CW_IN), lambda i: (0, 0, 0)), row, row, vec],
        out_specs=[row, vec],
        out_shape=[jax.ShapeDtypeStruct((T, D), F32), jax.ShapeDtypeStruct((1, D), F32)],
        vmem=56, args=[*parts, w_in, x, dxres, g1], stages=stages)[0]


def _place():
    x, y, c = lax.axis_index("x"), lax.axis_index("y"), lax.axis_index("c")
    chips = [(1 - x, y), (x, 1 - y), (1 - x, 1 - y)]
    return x, y, c, chips


def _rcopy(src, dst, ssem, rsem, dev):
    return pltpu.make_async_remote_copy(src_ref=src, dst_ref=dst, send_sem=ssem, recv_sem=rsem,
                                        device_id=dev, device_id_type=MESH_ID)


def _sds(a):
    return jax.ShapeDtypeStruct(a.shape, a.dtype)


def _sem2(n, m):
    return [pltpu.SemaphoreType.DMA((n * m,)), pltpu.SemaphoreType.DMA((n * m,))]


ALL = (0, 1, 1)


def _piece(ref, k, half, part):
    hr = ref.shape[1] // 2
    r0, r1 = hr * part[0] // part[2], hr * part[1] // part[2]
    return ref.at[k, pl.ds(half * hr + r0, r1 - r0), :]


def _gather(fulls, ici=(), d2d=()):
    n = len(fulls)
    ici, d2d = list(ici), list(d2d)
    pieces = [("ici", i, part) for i, part in ici] + [("d2d", i, part) for i, part in d2d]

    def copies(outs, sems):
        x, y, c, chips = _place()
        me = 2 * x + y
        sib = (x, y, 1 - c)
        send, recv = [], []
        for q, (kind, i, part) in enumerate(pieces):
            for j, chip in enumerate(chips):
                k, s = 2 * chip[0] + chip[1], 3 * q + j
                if kind == "ici":
                    mine, theirs, dev = _piece(outs[i], me, c, part), _piece(outs[i], k, c, part), (*chip, c)
                else:
                    mine, theirs, dev = _piece(outs[i], k, c, part), _piece(outs[i], k, 1 - c, part), sib
                send.append(_rcopy(mine, mine, sems[0].at[s], sems[1].at[s], dev))
                recv.append(_rcopy(theirs, theirs, sems[0].at[s], sems[1].at[s], dev))
        return send, recv

    def start(ins, outs, sems):
        for cp in copies(outs, sems)[0]:
            cp.start()

    def finish(ins, outs, sems):
        send, recv = copies(outs, sems)
        for cp in recv:
            cp.wait_recv()
        for cp in send:
            cp.wait_send()

    sems = [pltpu.SemaphoreType.DMA((3 * len(pieces),)), pltpu.SemaphoreType.DMA((3 * len(pieces),))]
    return _Stage(fulls, [_sds(f) for f in fulls], {i: i for i in range(n)}, sems, start, finish)


def _gather_whole(v):
    def copies(ins, outs, sems):
        x, y, c, chips = _place()
        me = 2 * x + y
        send = [_rcopy(ins[0], outs[0].at[me], sems[0].at[j], sems[1].at[j], (*chip, c))
                for j, chip in enumerate(chips)]
        recv = [_rcopy(ins[0], outs[0].at[2 * chip[0] + chip[1]], sems[0].at[j], sems[1].at[j], (*chip, c))
                for j, chip in enumerate(chips)]
        return send, recv

    def start(ins, outs, sems):
        for cp in copies(ins, outs, sems)[0]:
            cp.start()

    def finish(ins, outs, sems):
        send, recv = copies(ins, outs, sems)
        for cp in recv:
            cp.wait_recv()
        for cp in send:
            cp.wait_send()

    return _Stage([v], [jax.ShapeDtypeStruct((NCHIP,) + v.shape, v.dtype)], {},
                  [pltpu.SemaphoreType.DMA((3,)), pltpu.SemaphoreType.DMA((3,))], start, finish)


def _to_sibling(srcs):
    n = len(srcs)

    def copies(ins, outs, sems):
        x, y, c, _ = _place()
        sib = (x, y, 1 - c)
        return [_rcopy(ins[i].at[:, 1 - c] if srcs[i].ndim == 4 else ins[i], outs[i], sems[0].at[i], sems[1].at[i], sib)
                for i in range(n)]

    def start(ins, outs, sems):
        for cp in copies(ins, outs, sems):
            cp.start()

    def finish(ins, outs, sems):
        for cp in copies(ins, outs, sems):
            cp.wait()

    shapes = [jax.ShapeDtypeStruct((NCHIP,) + s.shape[2:] if s.ndim == 4 else s.shape, s.dtype) for s in srcs]
    return _Stage(srcs, shapes, {}, [pltpu.SemaphoreType.DMA((n,)), pltpu.SemaphoreType.DMA((n,))], start, finish)


def _to_chips(srcs, parts=None, lands=None):
    n = len(srcs)
    parts = [ALL] * n if parts is None else parts
    lands = [None] * n if lands is None else lands
    given = [i for i in range(n) if lands[i] is not None]

    def rows(ref, i):
        hr = srcs[i].shape[1]
        r0, r1 = hr * parts[i][0] // parts[i][2], hr * parts[i][1] // parts[i][2]
        return ref.at[pl.ds(r0, r1 - r0), :]

    def copies(ins, outs, sems):
        x, y, c, chips = _place()
        me = 2 * x + y
        return [_rcopy(rows(ins[i].at[2 * chip[0] + chip[1]] if srcs[i].shape[0] == NCHIP else ins[i].at[c], i),
                       rows(outs[i].at[me], i), sems[0].at[3 * i + j], sems[1].at[3 * i + j], (*chip, c))
                for i in range(n) for j, chip in enumerate(chips)]

    def start(ins, outs, sems):
        for cp in copies(ins, outs, sems):
            cp.start()

    def finish(ins, outs, sems):
        for cp in copies(ins, outs, sems):
            cp.wait()

    shapes = [jax.ShapeDtypeStruct((NCHIP,) + s.shape[1:], s.dtype) for s in srcs]
    alias = {n + q: i for q, i in enumerate(given)}
    return _Stage(list(srcs) + [lands[i] for i in given], shapes, alias, _sem2(n, 3), start, finish)


HBM_REF = pl.BlockSpec(memory_space=pltpu.HBM)
SEM_REF = pl.BlockSpec(memory_space=pltpu.SEMAPHORE)
DATAFLOW = pltpu.SideEffectType.DATAFLOW_SIDE_EFFECTING


def _after(x):
    return _Stage([x], [], {}, [], lambda *a: None, lambda *a: None)


class _Flight:
    def __init__(self, stage, sems, bufs):
        self.stage, self.sems, self.bufs = stage, list(sems), list(bufs)

    def landed(self):
        st, n = self.stage, len(self.stage.operands)
        fresh = [j for j in range(len(st.out_shape)) if j not in st.alias.values()]
        back = {v: k for k, v in st.alias.items()}
        return [self.bufs[back[j]] if j in back else self.bufs[n + fresh.index(j)] for j in range(len(st.out_shape))]


def _split_call(name, finish=(), start=(), after=None):
    bufs, stage_bufs = [], []

    def slot(a):
        for i, b in enumerate(bufs):
            if b is a:
                return i
        bufs.append(a)
        return len(bufs) - 1

    fin_slots = [[slot(b) for b in fl.bufs] for fl in finish]
    for st in start:
        fresh = [lax.empty(o.shape, o.dtype) for j, o in enumerate(st.out_shape) if j not in st.alias.values()]
        stage_bufs.append([slot(a) for a in list(st.operands) + fresh])
    old_sems = [s for fl in finish for s in fl.sems]
    new_sems = [s for st in start for s in st.sems]
    nb, no, nn = len(bufs), len(old_sems), len(new_sems)

    def refs_of(st, slots, buf_refs):
        n = len(st.operands)
        ins = [buf_refs[i] for i in slots[:n]]
        fresh = [j for j in range(len(st.out_shape)) if j not in st.alias.values()]
        back = {v: k for k, v in st.alias.items()}
        outs = [ins[back[j]] if j in back else buf_refs[slots[n + fresh.index(j)]] for j in range(len(st.out_shape))]
        return ins, outs

    def body(*refs):
        buf_refs, sem_in = refs[:nb], refs[nb:nb + no]
        sem_out = refs[nb + no + (after is not None):][:nn]
        token = refs[-1]
        pos = 0
        for fl, slots in zip(finish, fin_slots):
            ins, outs = refs_of(fl.stage, slots, buf_refs)
            fl.stage.finish(ins, outs, sem_in[pos:pos + len(fl.sems)])
            pos += len(fl.sems)
        pos = 0
        for st, slots in zip(start, stage_bufs):
            ins, outs = refs_of(st, slots, buf_refs)
            st.start(ins, outs, sem_out[pos:pos + len(st.sems)])
            pos += len(st.sems)
        token[...] = jnp.zeros_like(token)

    res = pl.pallas_call(
        body, name=name,
        out_shape=tuple(new_sems) + tuple(pltpu.HBM(b.shape, b.dtype) for b in bufs) +
                  (jax.ShapeDtypeStruct((8, LANE), F32),),
        in_specs=(HBM_REF,) * nb + (SEM_REF,) * no + ((pl.BlockSpec(memory_space=pl.ANY),) if after is not None else ()),
        out_specs=(SEM_REF,) * nn + (HBM_REF,) * nb + (pl.BlockSpec(memory_space=pltpu.VMEM),),
        input_output_aliases={i: nn + i for i in range(nb)},
        compiler_params=pltpu.CompilerParams(has_side_effects=DATAFLOW),
    )(*_hbm(*bufs), *old_sems, *([after] if after is not None else []))
    sems, thru, token = res[:nn], res[nn:nn + nb], res[-1]
    for fl, slots in zip(finish, fin_slots):
        fl.bufs = [thru[i] for i in slots]
    flights, pos = [], 0
    for st, slots in zip(start, stage_bufs):
        flights.append(_Flight(st, sems[pos:pos + len(st.sems)], [thru[i] for i in slots]))
        pos += len(st.sems)
    return flights, token


def _share(pairs):
    n = len(pairs)

    def start(ins, outs, sems):
        x, y, c, _ = _place()
        for i in range(n):
            _rcopy(outs[i].at[c], outs[i].at[c], sems[0].at[i], sems[1].at[i], (x, y, 1 - c)).start()

    def finish(ins, outs, sems):
        x, y, c, _ = _place()
        for i in range(n):
            _rcopy(outs[i].at[c], outs[i].at[c], sems[0].at[i], sems[1].at[i], (x, y, 1 - c)).wait_send()
            _rcopy(outs[i].at[1 - c], outs[i].at[1 - c], sems[0].at[i], sems[1].at[i], (x, y, 1 - c)).wait_recv()

    return _Stage(pairs, [_sds(p) for p in pairs], {i: i for i in range(n)},
                  [pltpu.SemaphoreType.DMA((n,)), pltpu.SemaphoreType.DMA((n,))], start, finish)


def _row_block(rows, cols, itemsize=4, target=2 * MIB):
    br = rows
    while br * cols * itemsize > target and br % 32 == 0:
        br //= 2
    return br


def _cast_place(w, chip_idx, name):
    rows, cols = w.shape
    br = _row_block(rows, cols)

    def body(k_ref, w_ref, o_ref):
        o_ref[0] = w_ref[...].astype(BF)

    return _call(
        body, name=name, grid=(rows // br,), prefetch=chip_idx,
        in_specs=[pl.BlockSpec((br, cols), lambda r, k: (r, 0))],
        out_specs=[pl.BlockSpec((1, br, cols), lambda r, k: (k[0], r, 0))],
        out_shape=[jax.ShapeDtypeStruct((NCHIP, rows, cols), BF)], vmem=32, args=[w])[0][0]


def _cast_place_multi(ws, chip_idx, stages=()):
    br = 128
    nblk = [a.shape[0] // br for a in ws]
    starts = [sum(nblk[:i]) for i in range(len(ws))]

    def body(k_ref, *refs):
        r = pl.program_id(0)
        for i in range(len(ws)):
            @pl.when(jnp.logical_and(r >= starts[i], r < starts[i] + nblk[i]))
            def _(i=i):
                refs[len(ws) + i][0] = refs[i][...].astype(BF)

    def at(i):
        return functools.partial(lambda r, s, nb: jnp.clip(r - s, 0, nb - 1), s=starts[i], nb=nblk[i])

    outs, landed = _call(
        body, name="cast_rest", grid=(sum(nblk),), prefetch=chip_idx,
        in_specs=[pl.BlockSpec((br, a.shape[1]), functools.partial(lambda r, k, f: (f(r), 0), f=at(i)))
                  for i, a in enumerate(ws)],
        out_specs=[pl.BlockSpec((1, br, a.shape[1]), functools.partial(lambda r, k, f: (k[0], f(r), 0), f=at(i)))
                   for i, a in enumerate(ws)],
        out_shape=[jax.ShapeDtypeStruct((NCHIP,) + a.shape, BF) for a in ws], vmem=32, args=list(ws), stages=stages)
    return outs, landed


def _add_sibling(g, land, cidx, name, stages=()):
    _, _, hr, cols = g.shape
    br = _row_block(hr, cols)

    def body(c_ref, g_ref, l_ref, o_ref):
        o_ref[...] = (g_ref[0, 0].astype(F32) + l_ref[0].astype(F32)).astype(BF)[None]

    outs, st = _call(
        body, name=name, grid=(NCHIP, hr // br), prefetch=cidx,
        in_specs=[pl.BlockSpec((1, 1, br, cols), lambda k, r, c: (k, c[0], r, 0)),
                  pl.BlockSpec((1, br, cols), lambda k, r, c: (k, r, 0))],
        out_specs=[pl.BlockSpec((1, br, cols), lambda k, r, c: (k, r, 0))],
        out_shape=[jax.ShapeDtypeStruct((NCHIP, hr, cols), BF)], vmem=32, args=[g, land], stages=stages)
    return outs[0], st


def _add_sibling_multi(gs, lands, cidx, name):
    n = len(gs)
    brs = [_row_block(g.shape[2], g.shape[3]) for g in gs]
    nrb = [g.shape[2] // b for g, b in zip(gs, brs)]
    nblk = [NCHIP * q for q in nrb]
    starts = [sum(nblk[:i]) for i in range(n)]

    def body(c_ref, *refs):
        r = pl.program_id(0)
        for i in range(n):
            g_ref, l_ref, o_ref = refs[2 * i], refs[2 * i + 1], refs[2 * n + i]

            @pl.when(jnp.logical_and(r >= starts[i], r < starts[i] + nblk[i]))
            def _():
                o_ref[...] = (g_ref[0, 0].astype(F32) + l_ref[0].astype(F32)).astype(BF)[None]

    def at(i, r):
        q = jnp.clip(r - starts[i], 0, nblk[i] - 1)
        return q // nrb[i], q % nrb[i]

    def g_spec(i):
        return pl.BlockSpec((1, 1, brs[i], gs[i].shape[3]),
                            functools.partial(lambda r, c, i: (at(i, r)[0], c[0], at(i, r)[1], 0), i=i))

    def l_spec(i):
        return pl.BlockSpec((1, brs[i], gs[i].shape[3]),
                            functools.partial(lambda r, c, i: (at(i, r)[0], at(i, r)[1], 0), i=i))

    return _call(
        body, name=name, grid=(sum(nblk),), prefetch=cidx,
        in_specs=[s for i in range(n) for s in (g_spec(i), l_spec(i))], out_specs=[l_spec(i) for i in range(n)],
        out_shape=[jax.ShapeDtypeStruct(l.shape, BF) for l in lands], vmem=32,
        args=[a for i in range(n) for a in (gs[i], lands[i])])[0]


def _add_pair(a, b, name):
    rows, cols = a.shape

    def body(a_ref, b_ref, o_ref):
        o_ref[...] = a_ref[...] + b_ref[...]

    spec = pl.BlockSpec((rows, cols), lambda r: (0, 0))
    return _call(body, name=name, grid=(1,), in_specs=[spec, spec], out_specs=[spec], out_shape=[_sds(a)],
                 vmem=32, args=[a, b])[0][0]


def _add_chips(own, land, idx, name, stages=None):
    _, hr, cols = land.shape
    br = _row_block(hr, cols)

    def body(s_ref, a_ref, b_ref, c_ref, d_ref, o_ref):
        o_ref[...] = (a_ref[...].astype(F32) + b_ref[...].astype(F32)) + (c_ref[...].astype(F32) +
                                                                           d_ref[...].astype(F32))

    spec = lambda q: pl.BlockSpec((1, br, cols), functools.partial(lambda r, s, q: (s[q], r, 0), q=q))
    outs, landed = _call(
        body, name=name, grid=(hr // br,), prefetch=idx,
        in_specs=[spec(0), spec(1), spec(2), spec(3)], out_specs=[spec(4)],
        out_shape=[jax.ShapeDtypeStruct((2, hr, cols), F32)], vmem=48, args=[own, land, land, land],
        stages=stages or ())
    return outs[0] if stages is None else (outs[0], landed)


def _add_chips_multi(owns, lands, idx, name, stages=()):
    n = len(owns)
    brs = [_row_block(l.shape[1], l.shape[2]) for l in lands]
    nblk = [l.shape[1] // b for l, b in zip(lands, brs)]
    starts = [sum(nblk[:i]) for i in range(n)]

    def body(s_ref, *refs):
        r = pl.program_id(0)
        for i in range(n):
            a_ref, b_ref, c_ref, d_ref = refs[4 * i:4 * i + 4]
            o_ref = refs[4 * n + i]

            @pl.when(jnp.logical_and(r >= starts[i], r < starts[i] + nblk[i]))
            def _():
                o_ref[...] = (a_ref[...].astype(F32) + b_ref[...].astype(F32)) + (c_ref[...].astype(F32) +
                                                                                   d_ref[...].astype(F32))

    def spec(i, q):
        return pl.BlockSpec((1, brs[i], lands[i].shape[2]), functools.partial(
            lambda r, s, q, st, nb: (s[q], jnp.clip(r - st, 0, nb - 1), 0), q=q, st=starts[i], nb=nblk[i]))

    outs, landed = _call(
        body, name=name, grid=(sum(nblk),), prefetch=idx,
        in_specs=[spec(i, q) for i in range(n) for q in range(4)], out_specs=[spec(i, 4) for i in range(n)],
        out_shape=[jax.ShapeDtypeStruct((2,) + l.shape[1:], F32) for l in lands], vmem=48,
        args=[a for i in range(n) for a in (owns[i], lands[i], lands[i], lands[i])], stages=stages)
    return outs, landed


def _adamw_math(w, g, m, v):
    mn = ADAM_B1 * m + (1.0 - ADAM_B1) * g
    vn = ADAM_B2 * v + (1.0 - ADAM_B2) * (g * g)
    m_hat = mn / (1.0 - ADAM_B1 ** ADAM_STEP)
    v_hat = vn / (1.0 - ADAM_B2 ** ADAM_STEP)
    return -ADAM_LR * (m_hat / (jnp.sqrt(v_hat) + ADAM_EPS) + ADAM_WD * w), mn, vn


def _adamw(w, g, m, v, name, stages=()):
    rows, cols = w.shape
    br = _row_block(rows, cols)

    def body(w_ref, g_ref, m_ref, v_ref, go_ref, d_ref, mo_ref, vo_ref):
        gv = g_ref[...]
        go_ref[...] = gv
        d_ref[...], mo_ref[...], vo_ref[...] = _adamw_math(w_ref[...], gv, m_ref[...], v_ref[...])

    spec = pl.BlockSpec((br, cols), lambda r: (r, 0))
    return _call(body, name=name, grid=(rows // br,), in_specs=[spec] * 4, out_specs=[spec] * 4,
                 out_shape=[_sds(w)] * 4, vmem=56, args=[w, g, m, v], stages=stages)


def _adamw_multi(names, w, g, m, v, stages=()):
    cols = w[names[0]].shape[1]
    br = 128
    nblk = [w[n].shape[0] // br for n in names]
    starts = [sum(nblk[:i]) for i in range(len(names))]

    def body(*refs):
        r = pl.program_id(0)
        for i in range(len(names)):
            w_ref, g_ref, m_ref, v_ref = refs[4 * i:4 * i + 4]
            go_ref, d_ref, mo_ref, vo_ref = refs[4 * len(names) + 4 * i:4 * len(names) + 4 * i + 4]

            @pl.when(jnp.logical_and(r >= starts[i], r < starts[i] + nblk[i]))
            def _():
                gv = g_ref[...]
                go_ref[...] = gv
                d_ref[...], mo_ref[...], vo_ref[...] = _adamw_math(w_ref[...], gv, m_ref[...], v_ref[...])

    def spec(i):
        return pl.BlockSpec((br, cols), functools.partial(
            lambda r, s, nb: (jnp.clip(r - s, 0, nb - 1), 0), s=starts[i], nb=nblk[i]))

    outs, landed = _call(
        body, name="adamw_" + "_".join(names), grid=(sum(nblk),),
        in_specs=[spec(i) for i in range(len(names)) for _ in range(4)],
        out_specs=[spec(i) for i in range(len(names)) for _ in range(4)],
        out_shape=[_sds(w[n]) for n in names for _ in range(4)], vmem=56,
        args=[a[n] for n in names for a in (w, g, m, v)], stages=stages)
    return {n: outs[4 * i:4 * i + 4] for i, n in enumerate(names)}, landed


def _to_everyone(v):
    deltas = [(a, b, e) for a in (0, 1) for b in (0, 1) for e in (0, 1)][1:]

    def copies(ins, outs, sems):
        x, y, c, _ = _place()
        me = 4 * x + 2 * y + c
        flip = lambda p, f: 1 - p if f else p
        return [_rcopy(ins[0], outs[0].at[me], sems[0].at[q], sems[1].at[q], (flip(x, a), flip(y, b), flip(c, e)))
                for q, (a, b, e) in enumerate(deltas)]

    def start(ins, outs, sems):
        for cp in copies(ins, outs, sems):
            cp.start()

    def finish(ins, outs, sems):
        for cp in copies(ins, outs, sems):
            cp.wait()

    n = len(deltas)
    return _Stage([v], [jax.ShapeDtypeStruct((2 * NCHIP,) + v.shape, v.dtype)], {},
                  [pltpu.SemaphoreType.DMA((n,)), pltpu.SemaphoreType.DMA((n,))], start, finish)


SMALL_AT = {"norm_mix_pre": (0, 1, D), "norm_mix_post": (1, 1, D), "norm_mlp_pre": (2, 1, D),
            "norm_mlp_post": (3, 1, D), "b_gate": (4, 2, D), "conv_b": (6, 1, D), "lru_b_a": (7, 1, D),
            "lru_b_x": (8, 1, D), "lru_lambda": (9, 1, D), "pool_scale": (10, 1, DP)}
SMALL_SEPARATE = ["conv_w", "lru_w_a", "lru_w_x", "pool_w"]


def _adamw_small(small_sum, first_all, sep_grads, w, m, v):
    packed, sep = list(SMALL_AT), list(SMALL_SEPARATE)
    names = packed + sep

    def body(*refs):
        s_ref, a_ref, refs = refs[0], refs[1], refs[2:]
        g_sep, refs = refs[:len(sep)], refs[len(sep):]
        nn = len(names)
        w_r, m_r, v_r, refs = refs[:nn], refs[nn:2 * nn], refs[2 * nn:3 * nn], refs[3 * nn:]
        g_out, refs = refs[:len(packed)], refs[len(packed):]
        d_o, m_o, v_o = refs[:nn], refs[nn:2 * nn], refs[2 * nn:3 * nn]
        for i, n in enumerate(names):
            if i == 0:
                g = a_ref[0:1, :]
                for q in range(1, 2 * NCHIP):
                    g = g + a_ref[q:q + 1, :]
                g_out[i][...] = g
            elif n in SMALL_AT:
                r0, nr, nc = SMALL_AT[n]
                g = jnp.concatenate([s_ref[r0 + q:r0 + q + 1, :nc] for q in range(nr)], axis=1)
                g_out[i][...] = g
            else:
                g = g_sep[i - len(packed)][...]
            d_o[i][...], m_o[i][...], v_o[i][...] = _adamw_math(w_r[i][...], g, m_r[i][...], v_r[i][...])

    ws = [w[n] for n in names]
    res = pl.pallas_call(
        body, name="adamw_small",
        out_shape=[_sds(w[n]) for n in packed] + [_sds(a) for a in ws] * 3,
        compiler_params=_cp(32),
    )(*_hbm(small_sum, first_all, *sep_grads, *ws, *[m[n] for n in names], *[v[n] for n in names]))
    nn, npk = len(names), len(packed)
    grad = dict(zip(packed, res[:npk]))
    delta = dict(zip(names, res[npk:npk + nn]))
    new_m = dict(zip(names, res[npk + nn:npk + 2 * nn]))
    new_v = dict(zip(names, res[npk + 2 * nn:]))
    return grad, delta, new_m, new_v


W_NAMES = ["norm_mix_pre", "norm_mix_post", "norm_mlp_pre", "norm_mlp_post", "w_in", "b_gate", "conv_w", "conv_b",
           "lru_w_a", "lru_b_a", "lru_w_x", "lru_b_x", "lru_lambda", "pool_w", "pool_scale", "w_lru_up",
           "w_pool_up", "w_o", "w_ff1", "w_ff2"]
BIG = ["w_in", "w_lru_up", "w_pool_up", "w_o", "w_ff1", "w_ff2"]


def _block_diag(w):
    hd = w.shape[-1]
    per = CB // hd
    w4 = w.reshape(NG, per, hd, hd)
    eye = jnp.eye(per, dtype=w.dtype)
    return jnp.einsum("gpij,pq->gpiqj", w4, eye).reshape(NG, CB, CB)


def _block_diag_extract(d, hd):
    per = CB // hd
    d5 = d.reshape(NG, per, hd, per, hd)
    return jnp.stack([d5[:, p, :, p, :] for p in range(per)], axis=1).reshape(NG * per, hd, hd)


def _halves(g):
    return g.reshape(NCHIP, 2, g.size // (g.shape[-1] * 2 * NCHIP), g.shape[-1])


def kernel(x, norm_mix_pre, norm_mix_post, norm_mlp_pre, norm_mlp_post, w_in, b_gate, conv_w, conv_b, lru_w_a, lru_b_a, lru_w_x, lru_b_x, lru_lambda, pool_w, pool_scale, w_lru_up, w_pool_up, w_o, w_ff1, w_ff2, loss_target, m_norm_mix_pre, m_norm_mix_post, m_norm_mlp_pre, m_norm_mlp_post, m_w_in, m_b_gate, m_conv_w, m_conv_b, m_lru_w_a, m_lru_b_a, m_lru_w_x, m_lru_b_x, m_lru_lambda, m_pool_w, m_pool_scale, m_w_lru_up, m_w_pool_up, m_w_o, m_w_ff1, m_w_ff2, v_norm_mix_pre, v_norm_mix_post, v_norm_mlp_pre, v_norm_mlp_post, v_w_in, v_b_gate, v_conv_w, v_conv_b, v_lru_w_a, v_lru_b_a, v_lru_w_x, v_lru_b_x, v_lru_lambda, v_pool_w, v_pool_scale, v_w_lru_up, v_w_pool_up, v_w_o, v_w_ff1, v_w_ff2):
    args = dict(locals())
    two_d = lambda a: a.reshape(-1, a.shape[-1])
    w = {n: two_d(args[n]) for n in W_NAMES}
    mom = {n: two_d(args["m_" + n]) for n in W_NAMES}
    var = {n: two_d(args["v_" + n]) for n in W_NAMES}
    i32 = lambda val: jnp.asarray(val, jnp.int32)
    chip = i32(2 * lax.axis_index("x") + lax.axis_index("y"))
    core = i32(lax.axis_index("c"))
    cidx = core.reshape(1)
    zero = i32(0)
    hd = lru_w_a.shape[-1]
    xs, target = x[0], loss_target[0]
    g1, g2, g3, g4 = norm_mix_pre, norm_mix_post, norm_mlp_pre, norm_mlp_post

    mix = ["w_lru_up", "w_pool_up", "w_o"]
    full = {"w_in": _cast_place(w["w_in"], chip.reshape(1), "cast_w_in")}
    (fl_in, fl_conv), first = _split_call("gather_start_first", start=[
        _gather([full["w_in"]], ici=[(0, ALL)]), _gather_whole(w["conv_w"])])
    casts, _ = _cast_place_multi([w[n] for n in BIG[1:]], chip.reshape(1), stages=[_after(first)])
    full.update(zip(BIG[1:], casts))
    (fl_mix, fl_ff1, fl_ff2), started = _split_call("gather_start_rest", start=[
        _gather([full[n] for n in mix], ici=[(0, ALL), (1, ALL), (2, ALL)]),
        _gather([full["w_ff1"]], ici=[(0, ALL)]), _gather([full["w_ff2"]], ici=[(0, ALL)])])
    wa = _block_diag(lru_w_a[0]).astype(BF)
    wx = _block_diag(lru_w_x[0]).astype(BF)
    pw = pool_w[0].astype(BF)

    def to_sibling(name, flight, after=None):
        (fl,), passed = _split_call(name + "_pass", finish=[flight], after=after,
                                    start=[_gather(flight.landed(), d2d=[(i, ALL) for i in range(len(flight.bufs))])])
        passed_on.append(passed)
        return fl

    passed_on = []

    def arrived(name, flight, after=None):
        _split_call(name + "_done", finish=[flight], after=after)
        return flight.landed()

    idx_big = jnp.stack([chip, (chip + 1) % NCHIP, (chip + 2) % NCHIP, (chip + 3) % NCHIP, core])
    proj, h1 = _fwd_inproj_own(xs, g1, fl_in.bufs[0], idx_big, stages=[_after(started)])
    fl_in = to_sibling("gather_w_in", fl_in, after=h1)
    _split_call("gather_w_in_done", finish=[fl_in, fl_conv])
    (w_in_f,), (conv_all,) = fl_in.landed(), fl_conv.landed()
    full["w_in"] = w_in_f
    conv_all = lax.dynamic_update_slice(conv_all, w["conv_w"][None], (chip, zero, zero))
    conv_full = jnp.transpose(conv_all, (1, 0, 2)).reshape(4, DR)
    proj = _fwd_inproj_rest(h1, w_in_f, proj, idx_big)
    fl_mix = to_sibling("gather_mix", fl_mix, after=proj)
    (ylru, hs), _ = _fwd_lru(proj, conv_full, conv_b, wa, lru_b_a, wx, lru_b_x, lru_lambda,
                             stages=[_after(passed_on[-1])])
    got = arrived("gather_mix", fl_mix, after=ylru)
    fl_ff1 = to_sibling("gather_ff1", fl_ff1, after=ylru)
    w_lru_up_f, w_pool_up_f, w_o_f = got[0].reshape(DR, D), got[1], got[2].reshape(D, D)
    ypool = _fwd_pool(proj, pw, pool_scale)
    (x2, h2, m, mrg, bra, brb), _ = _fwd_merge(xs, ylru, ypool, proj, b_gate, g2, g3, w_lru_up_f, w_pool_up_f, w_o_f,
                                               stages=[_after(passed_on[-1])])
    fl_ff2 = to_sibling("gather_ff2", fl_ff2, after=h2)
    _split_call("gather_ff_done", finish=[fl_ff1, fl_ff2])
    (ff1,), (ff2,) = fl_ff1.landed(), fl_ff2.landed()
    ff2 = ff2.reshape(DF, D)
    a1, lossp, dy, df, dg4 = _fwd_mlp_loss(h2, ff1, ff2, x2, target, g4)

    dh2, df1 = _bwd_mlp_x(df, a1, ff1, ff2)
    dw_ff1, dw_ff2 = _bwd_mlp_w(df, h2, a1, df1)
    g_ff = [_halves(dw_ff1), _halves(dw_ff2)]
    (dxres, dgates, dylru, dypool, dm, dbra, dbrb, dg2, dg3, dbg), (l_ff,) = _bwd_merge(
        dh2, dy, x2, m, bra, brb, proj, b_gate, g2, g3, w_lru_up_f, w_pool_up_f, w_o_f, stages=[_to_sibling(g_ff)])
    p_ff = _add_sibling_multi(g_ff, l_ff, cidx, "add_sibling_ff")
    (fl_ff,), sent_ff = _split_call("reduce_ff_start", start=[_to_chips(p_ff)])
    (dw_o, dw_lru_up, dw_pool_up), _ = _dw_merge(mrg, dm, ylru, dbra, ypool, dbrb, stages=[_after(sent_ff)])
    g_mix = [_halves(dw_lru_up), _halves(dw_pool_up), _halves(dw_o)]
    (dxp, dgl, dcw, dcb, dwa, dba, dwx, dbx, dlam), (l_mix,) = _bwd_lru(
        proj, hs, dylru, conv_full, conv_b, wa, lru_b_a, wx, lru_b_x, lru_lambda, stages=[_to_sibling(g_mix)])
    p_mix = _add_sibling_multi(g_mix, l_mix, cidx, "add_sibling_mix")
    dxpool, dpw, dsc = _bwd_pool(proj, dypool, pw, pool_scale)
    dproj = [dxp, dgl, dxpool, dgates]
    small = jnp.concatenate([
        jnp.zeros((1, D), F32), dg2, dg3, dg4, dbg.reshape(2, D), dcb, dba, dbx, dlam,
        jnp.pad(dsc, ((0, 0), (0, D - DP))), jnp.pad(lossp, ((0, 0), (0, D - 1))), dcw,
        _block_diag_extract(dwa, hd).reshape(-1, D), _block_diag_extract(dwx, hd).reshape(-1, D),
        dpw.reshape(-1, D)], axis=0)
    (fl_mixr, fl_smalls), sent_mix = _split_call("reduce_mix_start", start=[_to_chips(p_mix), _to_sibling([small])])
    dw_in = _bwd_inproj_w(h1, dproj, sent_mix)
    _split_call("reduce_small_sibling_done", finish=[fl_smalls], after=dw_in)
    small, l_small = fl_smalls.bufs
    small2 = _add_pair(small, l_small, "add_sibling_small").reshape(2, SMALL_ROWS // 2, D)
    g_in = _halves(dw_in)
    done = ["w_ff1", "w_ff2"] + mix
    (fl_gin, fl_small), sib_started = _split_call("reduce_in_sibling_start", finish=[fl_ff, fl_mixr],
                                                  start=[_to_sibling([g_in]), _to_chips([small2])])
    p_ff1, p_ff2, c_ff1, c_ff2 = fl_ff.bufs
    p_mix, c_mix = fl_mixr.bufs[:3], fl_mixr.bufs[3:]
    pairs_ff, _ = _add_chips_multi([p_ff1, p_ff2], [c_ff1, c_ff2], idx_big, "add_chips_ff", stages=[_after(sib_started)])
    _split_call("reduce_in_sibling_done", finish=[fl_gin], after=pairs_ff[-1])
    g_in, l_in = fl_gin.bufs
    p_in = _add_sibling(g_in, l_in, cidx, "add_sibling_w_in")[0]
    (fl_pin,), token = _split_call("reduce_last_start", start=[_to_chips([p_in])])
    pairs_mix, _ = _add_chips_multi(p_mix, c_mix, idx_big, "add_chips_mix", stages=[_after(token)])
    pairs = pairs_ff + pairs_mix
    _split_call("reduce_small_done", finish=[fl_small], after=pairs[-1])
    small2, c_small = fl_small.bufs
    own_small = lax.dynamic_index_in_dim(small2, core, 0, keepdims=True)
    c_small = lax.dynamic_update_slice(c_small, own_small, (chip, zero, zero))
    pair_small = _add_chips(c_small, c_small, jnp.stack([zero, zero + 1, zero + 2, zero + 3, core]), "add_chips_small")
    (fl_share,), shared_start = _split_call("reduce_share_start", start=[_share(pairs + [pair_small])])
    grad_x, dg1 = _bwd_inproj_x(dproj, full["w_in"], xs, dxres, g1, stages=[_after(shared_start)])
    _split_call("reduce_share_done", finish=[fl_share, fl_pin], after=dg1)
    shared, (p_in, c_in) = fl_share.landed(), fl_pin.bufs
    pairs, pair_small = shared[:-1], shared[-1]

    grads, delta, new_m, new_v = {}, {}, {}, {}
    for n, p in zip(done, pairs):
        grads[n] = p.reshape(-1, p.shape[-1])

    def update(n, stages=()):
        (grads[n], delta[n], new_m[n], new_v[n]), landed = _adamw(w[n], grads[n], mom[n], var[n], "adamw_" + n,
                                                                  stages=stages)
        return landed

    pair_in = _add_chips(p_in, c_in, idx_big, "add_chips_w_in")
    (fl_last, fl_dg1), last_start = _split_call("reduce_last_share_start", start=[_share([pair_in]), _to_everyone(dg1)])
    updated, _ = _adamw_multi(["w_ff1", "w_ff2", "w_o", "w_lru_up"], w, grads, mom, var, stages=[_after(last_start)])
    for n, (go, d, mo, vo) in updated.items():
        grads[n], delta[n], new_m[n], new_v[n] = go, d, mo, vo
    _split_call("reduce_last_share_done", finish=[fl_last, fl_dg1], after=new_v["w_lru_up"])
    (pair_in,), (dg1, dg1_all) = fl_last.landed(), fl_dg1.bufs
    dg1_all = lax.dynamic_update_slice(dg1_all, dg1[None], (2 * chip + core, zero, zero)).reshape(2 * NCHIP, D)
    grads["w_in"] = pair_in.reshape(-1, pair_in.shape[-1])
    update("w_pool_up")
    update("w_in")
    small_sum = pair_small.reshape(SMALL_ROWS, D)
    loss = 0.5 * small_sum[LOSS_ROW, 0]
    ccols = DR // NCHIP
    sep = [lax.dynamic_slice(small_sum[12:16], (zero, chip * ccols), (4, ccols)),
           small_sum[16:80].reshape(-1, hd), small_sum[80:144].reshape(-1, hd), small_sum[144:208].reshape(-1, PG)]
    g_s, d_s, m_s, v_s = _adamw_small(small_sum, dg1_all, sep, w, mom, var)
    grads.update(g_s)
    grads.update(dict(zip(SMALL_SEPARATE, sep)))
    delta.update(d_s)
    new_m.update(m_s)
    new_v.update(v_s)

    out = lambda d: [d[n].reshape(args[n].shape) for n in W_NAMES]
    return (loss, grad_x[None], *out(grads), *out(delta), *out(new_m), *out(new_v))
```

```python
import functools
import math

import jax
import jax.numpy as jnp
from jax import lax
from jax.experimental import pallas as pl
from jax.experimental.pallas import tpu as pltpu

F32 = jnp.float32
BF = jnp.bfloat16

T = 2048
D = 1024
DR = 1024
DP = 512
DF = 4096
DIN = 4608
NCHIP = 4
CW_IN = DIN // NCHIP
LANE = 128
CB = 128
NG = DR // CB
PG = 128
POOL_WINDOWS = (2, 4, 8, 16)
NORM_EPS = 1e-6
LRU_C = 8.0
GELU_C = math.sqrt(2.0 / math.pi)
ADAM_LR = 0.001
ADAM_B1 = 0.9
ADAM_B2 = 0.999
ADAM_EPS = 1e-08
ADAM_WD = 0.01
ADAM_STEP = 10
MESH_ID = pl.DeviceIdType.MESH
ANY = pl.BlockSpec(memory_space=pl.ANY)
SMALL_ROWS = 208
LOSS_ROW = 11
MIB = 1 << 20


def _cp(vmem_mib=None):
    if vmem_mib is None:
        return pltpu.CompilerParams()
    return pltpu.CompilerParams(vmem_limit_bytes=vmem_mib * MIB)


def _hbm(*arrays):
    return [pltpu.with_memory_space_constraint(a, pltpu.HBM) for a in arrays]


def _hbm_out(shapes):
    return [pltpu.HBM(s.shape, s.dtype) for s in shapes]


class _Stage:
    def __init__(self, operands, out_shape, alias, sems, start, finish):
        self.operands, self.out_shape, self.alias, self.sems = list(operands), list(out_shape), dict(alias), list(sems)
        self.start, self.finish = start, finish


def _call(body, *, name, grid, in_specs, out_specs, out_shape, args, vmem=None, stages=(), prefetch=None,
          scratch=()):
    nin, nout = len(in_specs), len(out_specs)
    npre = 0 if prefetch is None else 1
    st_args, st_shapes, st_sems, aliases = [], [], list(scratch), {}
    for st in stages:
        for k, v in st.alias.items():
            aliases[npre + nin + len(st_args) + k] = nout + len(st_shapes) + v
        st_args += st.operands
        st_shapes += st.out_shape
        st_sems += st.sems

    def wrapped(*refs):
        pre, refs = refs[:npre], refs[npre:]
        ins, pos = refs[:nin], nin
        st_ins = []
        for st in stages:
            st_ins.append(refs[pos:pos + len(st.operands)])
            pos += len(st.operands)
        outs, pos = refs[pos:pos + nout], pos + nout
        st_outs = []
        for st in stages:
            st_outs.append(refs[pos:pos + len(st.out_shape)])
            pos += len(st.out_shape)
        work, pos = refs[pos:pos + len(scratch)], pos + len(scratch)
        sems = []
        for st in stages:
            sems.append(refs[pos:pos + len(st.sems)])
            pos += len(st.sems)
        if stages:
            first = functools.reduce(jnp.logical_and, [pl.program_id(a) == 0 for a in range(len(grid))])

            @pl.when(first)
            def _():
                for st, a, b, s in zip(stages, st_ins, st_outs, sems):
                    st.start(a, b, s)

        body(*pre, *ins, *outs, *work)
        if stages:
            last = functools.reduce(jnp.logical_and, [pl.program_id(a) == g - 1 for a, g in enumerate(grid)])

            @pl.when(last)
            def _():
                for st, a, b, s in zip(stages, st_ins, st_outs, sems):
                    st.finish(a, b, s)

    all_in = list(in_specs) + [ANY] * len(st_args)
    all_out = list(out_specs) + [ANY] * len(st_shapes)
    kw = dict(has_side_effects=True) if stages else {}
    if vmem is not None:
        kw["vmem_limit_bytes"] = vmem * MIB
    if prefetch is None:
        gkw = dict(grid=grid, in_specs=all_in, out_specs=all_out, scratch_shapes=st_sems)
    else:
        gkw = dict(grid_spec=pltpu.PrefetchScalarGridSpec(
            num_scalar_prefetch=1, grid=grid, in_specs=all_in, out_specs=all_out, scratch_shapes=st_sems))
    res = pl.pallas_call(
        wrapped, name=name, out_shape=_hbm_out(list(out_shape) + st_shapes), input_output_aliases=aliases,
        compiler_params=pltpu.CompilerParams(**kw), **gkw,
    )(*([prefetch] if npre else []), *_hbm(*args, *st_args))
    outs, rest, st_res = list(res[:nout]), list(res[nout:]), []
    for st in stages:
        st_res.append(rest[:len(st.out_shape)])
        rest = rest[len(st.out_shape):]
    return outs, st_res


def _mm(a, b):
    return jnp.dot(a.astype(BF), b.astype(BF), preferred_element_type=F32)


def _mm_nt(a, b):
    return lax.dot_general(a.astype(BF), b.astype(BF), (((1,), (1,)), ((), ())),
                           preferred_element_type=F32)


def _mm_tn(a, b):
    return lax.dot_general(a.astype(BF), b.astype(BF), (((0,), (0,)), ((), ())),
                           preferred_element_type=F32)


def _rows(v):
    return lax.broadcasted_iota(jnp.int32, v.shape, 0)


def _sd(v, s, fill=0.0):
    return jnp.where(_rows(v) >= s, pltpu.roll(v, s, axis=0), fill)


def _su(v, s, fill=0.0):
    n = v.shape[0]
    return jnp.where(_rows(v) < n - s, pltpu.roll(v, n - s, axis=0), fill)


def _sigmoid(z):
    return 1.0 / (1.0 + jnp.exp(-z))


def _softplus(z):
    e = jnp.exp(-jnp.abs(z))
    u = 1.0 + e
    d = u - 1.0
    log1p = jnp.where(d == 0.0, e, jnp.log(u) * (e / jnp.where(d == 0.0, 1.0, d)))
    return jnp.maximum(z, 0.0) + log1p


def _mean(v):
    return jnp.mean(v, axis=-1, keepdims=True)


def _colsum(v):
    return jnp.sum(v, axis=0, keepdims=True)


def _acc(ref, val, first):
    @pl.when(first)
    def _():
        ref[...] = val

    @pl.when(jnp.logical_not(first))
    def _():
        ref[...] += val


def _conv(xp, cw, cb):
    x1, x2, x3 = _sd(xp, 1), _sd(xp, 2), _sd(xp, 3)
    xc = cb + cw[0:1] * x3 + cw[1:2] * x2 + cw[2:3] * x1 + cw[3:4] * xp
    return xc, x1, x2, x3


def _lru_gates(xc, wa, ba, wx, bx, lam):
    xcb = xc.astype(BF)
    r = _sigmoid(_mm(xcb, wa) + ba)
    ii = _sigmoid(_mm(xcb, wx) + bx)
    sp = _softplus(-lam)
    la = (-LRU_C) * r * sp
    a = jnp.exp(la)
    mult = jnp.sqrt(-jnp.tanh(la) * (a * a + 1.0))
    return xcb, r, ii, sp, a, mult


def _gelu_parts(g):
    th = jnp.tanh(GELU_C * (g + 0.044715 * (g * g * g)))
    gel = 0.5 * g * (1.0 + th)
    dgel = 0.5 * (1.0 + th) + 0.5 * g * (1.0 - th * th) * (GELU_C * (1.0 + 3.0 * 0.044715 * (g * g)))
    return gel, dgel


def _tile_scan(a, b, a_s, b_s, out_ref, reverse):
    n, lanes = a.shape
    nt = n // 8
    a, b = a.reshape(nt, 8, lanes), b.reshape(nt, 8, lanes)
    sub = lax.broadcasted_iota(jnp.int32, a.shape, 1)
    s = 1
    while s < 8:
        keep = sub < 8 - s if reverse else sub >= s
        amount = 8 - s if reverse else s
        b = b + a * jnp.where(keep, pltpu.roll(b, amount, axis=1), 0.0)
        a = a * jnp.where(keep, pltpu.roll(a, amount, axis=1), 1.0)
        s *= 2
    a_s[...] = a.reshape(n, lanes)
    b_s[...] = b.reshape(n, lanes)
    edge = pl.ds(0 if reverse else 7, nt, stride=8)
    ta, tb = a_s[edge, :], b_s[edge, :]
    shift = _su if reverse else _sd
    s = 1
    while s < nt:
        tb = tb + ta * shift(tb, s, 0.0)
        if 2 * s < nt:
            ta = ta * shift(ta, s, 1.0)
        s *= 2
    enters = shift(tb, 1, 0.0)
    for o in range(8):
        rows = pl.ds(o, nt, stride=8)
        out_ref[rows, :] = b_s[rows, :] + a_s[rows, :] * enters


def _pool_window(x, steps, shift):
    s, sh = x, 1
    for _ in range(steps):
        s = s + shift(s, sh)
        sh *= 2
    return s


def _fwd_inproj_own(x, g1, w_in, slots, stages=()):
    tm = 1024

    def body(s_ref, x_ref, g_ref, w_ref, proj_ref, h_ref):
        xv = x_ref[...]
        r = lax.rsqrt(_mean(xv * xv) + NORM_EPS)
        h = ((xv * r) * g_ref[...]).astype(BF)
        h_ref[...] = h
        proj_ref[...] = jnp.dot(h, w_ref[0], preferred_element_type=F32)

    return _call(
        body, name="fwd_inproj_own", grid=(T // tm,), prefetch=slots,
        in_specs=[pl.BlockSpec((tm, D), lambda i, s: (i, 0)),
                  pl.BlockSpec((1, D), lambda i, s: (0, 0)),
                  pl.BlockSpec((1, D, CW_IN), lambda i, s: (s[0], 0, 0))],
        out_specs=[pl.BlockSpec((tm, CW_IN), lambda i, s: (i, s[0])),
                   pl.BlockSpec((tm, D), lambda i, s: (i, 0))],
        out_shape=[jax.ShapeDtypeStruct((T, DIN), F32), jax.ShapeDtypeStruct((T, D), BF)],
        vmem=40, args=[x, g1, w_in], stages=stages)[0]


def _fwd_inproj_rest(h1, w_in, proj, slots):
    tm = 1024

    def body(s_ref, h_ref, w_ref, p_in, proj_ref):
        proj_ref[...] = jnp.dot(h_ref[...], w_ref[0], preferred_element_type=F32)

    res = pl.pallas_call(
        body, name="fwd_inproj_rest",
        grid_spec=pltpu.PrefetchScalarGridSpec(
            num_scalar_prefetch=1, grid=(NCHIP - 1, T // tm),
            in_specs=[pl.BlockSpec((tm, D), lambda k, i, s: (i, 0)),
                      pl.BlockSpec((1, D, CW_IN), lambda k, i, s: (s[1 + k], 0, 0)), ANY],
            out_specs=pl.BlockSpec((tm, CW_IN), lambda k, i, s: (i, s[1 + k]))),
        out_shape=pltpu.HBM((T, DIN), F32), input_output_aliases={3: 0},
        compiler_params=_cp(40),
    )(slots, *_hbm(h1, w_in, proj))
    return res


def _vec_spec():
    return pl.BlockSpec((1, CB), lambda j: (0, j))


def _fwd_lru(proj, conv_w, conv_b, wa, ba, wx, bx, lam, stages=()):
    def body(xp_ref, g_ref, cw_ref, cb_ref, wa_ref, ba_ref, wx_ref, bx_ref, lam_ref, y_ref, h_ref, a_s, b_s):
        xc, _, _, _ = _conv(xp_ref[...], cw_ref[...], cb_ref[...])
        _, _, ii, _, a, mult = _lru_gates(xc, wa_ref[0], ba_ref[...], wx_ref[0], bx_ref[...], lam_ref[...])
        _tile_scan(a, mult * (ii * xc), a_s, b_s, h_ref, reverse=False)
        gel, _ = _gelu_parts(g_ref[...])
        y_ref[...] = (h_ref[...] * gel).astype(BF)

    return _call(
        body, name="fwd_lru", grid=(NG,),
        in_specs=[pl.BlockSpec((T, CB), lambda j: (0, j)),
                  pl.BlockSpec((T, CB), lambda j: (0, NG + j)),
                  pl.BlockSpec((4, CB), lambda j: (0, j)),
                  _vec_spec(),
                  pl.BlockSpec((1, CB, CB), lambda j: (j, 0, 0)), _vec_spec(),
                  pl.BlockSpec((1, CB, CB), lambda j: (j, 0, 0)), _vec_spec(),
                  _vec_spec()],
        out_specs=[pl.BlockSpec((T, CB), lambda j: (0, j)), pl.BlockSpec((T, CB), lambda j: (0, j))],
        out_shape=[jax.ShapeDtypeStruct((T, DR), BF), jax.ShapeDtypeStruct((T, DR), F32)],
        vmem=48, args=[proj, proj, conv_w, conv_b, wa, ba, wx, bx, lam], stages=stages,
        scratch=[pltpu.VMEM((T, CB), F32)] * 2)


def _pool_cnt(w):
    t = lax.broadcasted_iota(jnp.int32, (T, 1), 0)
    return jnp.minimum(t + 1, w).astype(F32)


def _fwd_pool(proj, pool_w, pool_scale):
    def body(xp_ref, pw_ref, sc_ref, y_ref):
        for g, w in enumerate(POOL_WINDOWS):
            cols = slice(g * PG, (g + 1) * PG)
            x = xp_ref[:, cols]
            p = _pool_window(x, g + 1, _sd) / _pool_cnt(w) - x
            y_ref[:, cols] = (_mm(p, pw_ref[g]) * sc_ref[:, cols]).astype(BF)

    return pl.pallas_call(
        body, name="fwd_pool", grid=(1,),
        in_specs=[pl.BlockSpec((T, DP), lambda i: (0, 2 * DR // DP)),
                  pl.BlockSpec((4, PG, PG), lambda i: (0, 0, 0)),
                  pl.BlockSpec((1, DP), lambda i: (0, 0))],
        out_specs=pl.BlockSpec((T, DP), lambda i: (0, 0)),
        out_shape=pltpu.HBM((T, DP), BF),
        compiler_params=_cp(48),
    )(*_hbm(proj, pool_w, pool_scale))


GATE_BLK = 512
GATE_BLK0 = (2 * DR + DP) // GATE_BLK


def _gate_specs(tm):
    return [pl.BlockSpec((tm, GATE_BLK), functools.partial(lambda i, q: (i, GATE_BLK0 + q), q=q))
            for q in range(4)]


def _fwd_merge(x, ylru, ypool, proj, b_gate, g2, g3, w_lru_up, w_pool_up, w_o, stages=()):
    tm = 512

    def body(x_ref, yl_ref, yp_ref, p0, p1, p2, p3, bg_ref, g2_ref, g3_ref, wl_ref, wp_ref, wo_ref,
             x2_ref, h2_ref, m_ref, mrg_ref, bra_ref, brb_ref):
        bra = jnp.dot(yl_ref[...], wl_ref[...], preferred_element_type=F32)
        yp = yp_ref[...]
        brb = jnp.concatenate([jnp.dot(yp, wp_ref[k], preferred_element_type=F32) for k in range(NCHIP)], axis=1)
        bg = bg_ref[...]
        ga = _sigmoid(jnp.concatenate([p0[...], p1[...]], axis=1) + bg[:, :D])
        gb = _sigmoid(jnp.concatenate([p2[...], p3[...]], axis=1) + bg[:, D:])
        mrg = (ga * bra + gb * brb).astype(BF)
        m = jnp.dot(mrg, wo_ref[...], preferred_element_type=F32)
        r2 = lax.rsqrt(_mean(m * m) + NORM_EPS)
        x2 = x_ref[...] + (m * r2) * g2_ref[...]
        r3 = lax.rsqrt(_mean(x2 * x2) + NORM_EPS)
        x2_ref[...] = x2
        h2_ref[...] = ((x2 * r3) * g3_ref[...]).astype(BF)
        m_ref[...] = m
        mrg_ref[...] = mrg
        bra_ref[...] = bra.astype(BF)
        brb_ref[...] = brb.astype(BF)

    row = lambda w: pl.BlockSpec((tm, w), lambda i: (i, 0))
    full2 = lambda a, b: pl.BlockSpec((a, b), lambda i: (0, 0))
    return _call(
        body, name="fwd_merge", grid=(T // tm,),
        in_specs=[row(D), row(DR), row(DP)] + _gate_specs(tm) +
                 [full2(1, 2 * D), full2(1, D), full2(1, D), full2(DR, D),
                  pl.BlockSpec((NCHIP, DP, D // NCHIP), lambda i: (0, 0, 0)), full2(D, D)],
        out_specs=[row(D)] * 6,
        out_shape=[jax.ShapeDtypeStruct((T, D), F32), jax.ShapeDtypeStruct((T, D), BF),
                   jax.ShapeDtypeStruct((T, D), F32), jax.ShapeDtypeStruct((T, D), BF),
                   jax.ShapeDtypeStruct((T, D), BF), jax.ShapeDtypeStruct((T, D), BF)],
        vmem=48, args=[x, ylru, ypool, proj, proj, proj, proj, b_gate, g2, g3, w_lru_up, w_pool_up, w_o],
        stages=stages)


def _fwd_mlp_loss(h2, w_ff1, w_ff2, x2, target, g4):
    tm = 512
    fk = DF // NCHIP

    def body(h_ref, w1_ref, w2_ref, x2_ref, t_ref, g_ref, a1_ref, loss_ref, dy_ref, df_ref, dg_ref):
        first = pl.program_id(0) == 0
        h = h_ref[...]
        f = None
        for k in range(NCHIP):
            a1 = jnp.maximum(jnp.dot(h, w1_ref[k], preferred_element_type=F32), 0.0)
            a1_ref[:, k * fk:(k + 1) * fk] = a1.astype(BF)
            part = jnp.dot((a1 * a1).astype(BF), w2_ref[k * fk:(k + 1) * fk, :], preferred_element_type=F32)
            f = part if f is None else f + part
        g4v = g_ref[...]
        r4 = lax.rsqrt(_mean(f * f) + NORM_EPS)
        fn = f * r4
        e = (x2_ref[...] + fn * g4v) - t_ref[...]
        _acc(loss_ref, jnp.sum(_mean(e * e), axis=0, keepdims=True), first)
        dy = e * (1.0 / D)
        dy_ref[...] = dy
        _acc(dg_ref, _colsum(dy * fn), first)
        dfn = dy * g4v
        df_ref[...] = (r4 * (dfn - fn * _mean(dfn * fn))).astype(BF)

    row = pl.BlockSpec((tm, D), lambda i: (i, 0))
    return pl.pallas_call(
        body, name="fwd_mlp_loss", grid=(T // tm,),
        in_specs=[row, pl.BlockSpec((NCHIP, D, fk), lambda i: (0, 0, 0)), pl.BlockSpec((DF, D), lambda i: (0, 0)),
                  row, row, pl.BlockSpec((1, D), lambda i: (0, 0))],
        out_specs=[pl.BlockSpec((tm, DF), lambda i: (i, 0)), pl.BlockSpec((1, 1), lambda i: (0, 0)), row, row,
                   pl.BlockSpec((1, D), lambda i: (0, 0))],
        out_shape=_hbm_out([jax.ShapeDtypeStruct((T, DF), BF), jax.ShapeDtypeStruct((1, 1), F32),
                            jax.ShapeDtypeStruct((T, D), F32), jax.ShapeDtypeStruct((T, D), BF),
                            jax.ShapeDtypeStruct((1, D), F32)]),
        compiler_params=_cp(56),
    )(*_hbm(h2, w_ff1, w_ff2, x2, target, g4))


def _bwd_mlp_x(df, a1, w_ff1, w_ff2):
    tm = 512
    fk = DF // NCHIP

    def body(df_ref, a1_ref, w1_ref, w2_ref, dh_ref, df1_ref):
        df = df_ref[...]
        dh = None
        for k in range(NCHIP):
            cols = slice(k * fk, (k + 1) * fk)
            dact = _mm_nt(df, w2_ref[cols, :])
            df1 = (dact * (2.0 * a1_ref[:, cols].astype(F32))).astype(BF)
            df1_ref[:, cols] = df1
            part = _mm_nt(df1, w1_ref[k])
            dh = part if dh is None else dh + part
        dh_ref[...] = dh

    return pl.pallas_call(
        body, name="bwd_mlp_x", grid=(T // tm,),
        in_specs=[pl.BlockSpec((tm, D), lambda i: (i, 0)),
                  pl.BlockSpec((tm, DF), lambda i: (i, 0)),
                  pl.BlockSpec((NCHIP, D, fk), lambda i: (0, 0, 0)),
                  pl.BlockSpec((DF, D), lambda i: (0, 0))],
        out_specs=[pl.BlockSpec((tm, D), lambda i: (i, 0)), pl.BlockSpec((tm, DF), lambda i: (i, 0))],
        out_shape=_hbm_out([jax.ShapeDtypeStruct((T, D), F32), jax.ShapeDtypeStruct((T, DF), BF)]),
        compiler_params=_cp(56),
    )(*_hbm(df, a1, w_ff1, w_ff2))


def _bwd_mlp_w(df, h2, a1, df1):
    fc = 512
    per = (DF // NCHIP) // fc

    def body(df_ref, h_ref, a1_ref, df1_ref, dw1_ref, dw2_ref):
        a1 = a1_ref[...].astype(F32)
        dw2_ref[...] = _mm_tn((a1 * a1).astype(BF), df_ref[...]).astype(BF)
        dw1_ref[0] = _mm_tn(h_ref[...], df1_ref[...]).astype(BF)

    return pl.pallas_call(
        body, name="bwd_mlp_w", grid=(DF // fc,),
        in_specs=[pl.BlockSpec((T, D), lambda j: (0, 0)),
                  pl.BlockSpec((T, D), lambda j: (0, 0)),
                  pl.BlockSpec((T, fc), lambda j: (0, j)),
                  pl.BlockSpec((T, fc), lambda j: (0, j))],
        out_specs=[pl.BlockSpec((1, D, fc), lambda j: (j // per, 0, j % per)),
                   pl.BlockSpec((fc, D), lambda j: (j, 0))],
        out_shape=_hbm_out([jax.ShapeDtypeStruct((NCHIP, D, DF // NCHIP), BF),
                            jax.ShapeDtypeStruct((DF, D), BF)]),
        compiler_params=_cp(56),
    )(*_hbm(df, h2, a1, df1))


def _bwd_merge(dh2, dy, x2, m, bra, brb, proj, b_gate, g2, g3, w_lru_up, w_pool_up, w_o, stages=()):
    tm = 256
    cpu = D // NCHIP

    def body(dh2_ref, dy_ref, x2_ref, m_ref, bra_ref, brb_ref, p0, p1, p2, p3, bg_ref,
             g2_ref, g3_ref, wl_ref, wp_ref, wo_ref,
             dx_ref, dgt_ref, dyl_ref, dyp_ref, dm_ref, dbra_ref, dbrb_ref, dg2_ref, dg3_ref, dbg_ref):
        first = pl.program_id(0) == 0
        x2 = x2_ref[...]
        r3 = lax.rsqrt(_mean(x2 * x2) + NORM_EPS)
        x2n = x2 * r3
        dh2 = dh2_ref[...]
        t3 = dh2 * g3_ref[...]
        dx2 = dy_ref[...] + r3 * (t3 - x2n * _mean(t3 * x2n))
        dx_ref[...] = dx2
        _acc(dg3_ref, _colsum(dh2 * x2n), first)
        m = m_ref[...]
        r2 = lax.rsqrt(_mean(m * m) + NORM_EPS)
        mn = m * r2
        _acc(dg2_ref, _colsum(dx2 * mn), first)
        dmn = dx2 * g2_ref[...]
        dm = (r2 * (dmn - mn * _mean(dmn * mn))).astype(BF)
        dm_ref[...] = dm
        dmrg = _mm_nt(dm, wo_ref[...])
        bg = bg_ref[...]
        ga = _sigmoid(jnp.concatenate([p0[...], p1[...]], axis=1) + bg[:, :D])
        gb = _sigmoid(jnp.concatenate([p2[...], p3[...]], axis=1) + bg[:, D:])
        dga = dmrg * bra_ref[...].astype(F32) * (ga * (1.0 - ga))
        dgb = dmrg * brb_ref[...].astype(F32) * (gb * (1.0 - gb))
        dgt_ref[:, :D] = dga.astype(BF)
        dgt_ref[:, D:] = dgb.astype(BF)
        _acc(dbg_ref, jnp.concatenate([_colsum(dga), _colsum(dgb)], axis=1), first)
        dbra = (dmrg * ga).astype(BF)
        dbrb = (dmrg * gb).astype(BF)
        dbra_ref[...] = dbra
        dbrb_ref[...] = dbrb
        dyl_ref[...] = _mm_nt(dbra, wl_ref[...])
        dyp = None
        for k in range(NCHIP):
            part = _mm_nt(dbrb[:, k * cpu:(k + 1) * cpu], wp_ref[k])
            dyp = part if dyp is None else dyp + part
        dyp_ref[...] = dyp

    row = lambda w: pl.BlockSpec((tm, w), lambda i: (i, 0))
    full2 = lambda a, b: pl.BlockSpec((a, b), lambda i: (0, 0))
    wp_spec = pl.BlockSpec((NCHIP, DP, cpu), lambda i: (0, 0, 0))
    return _call(
        body, name="bwd_merge", grid=(T // tm,),
        in_specs=[row(D)] * 6 + _gate_specs(tm) +
                 [full2(1, 2 * D), full2(1, D), full2(1, D), full2(DR, D), wp_spec, full2(D, D)],
        out_specs=[row(D), row(2 * D), row(DR), row(DP), row(D), row(D), row(D),
                   full2(1, D), full2(1, D), full2(1, 2 * D)],
        out_shape=[jax.ShapeDtypeStruct((T, D), F32), jax.ShapeDtypeStruct((T, 2 * D), BF),
                   jax.ShapeDtypeStruct((T, DR), F32), jax.ShapeDtypeStruct((T, DP), F32),
                   jax.ShapeDtypeStruct((T, D), BF), jax.ShapeDtypeStruct((T, D), BF),
                   jax.ShapeDtypeStruct((T, D), BF),
                   jax.ShapeDtypeStruct((1, D), F32), jax.ShapeDtypeStruct((1, D), F32),
                   jax.ShapeDtypeStruct((1, 2 * D), F32)],
        vmem=56, args=[dh2, dy, x2, m, bra, brb, proj, proj, proj, proj, b_gate, g2, g3, w_lru_up, w_pool_up, w_o],
        stages=stages)


def _dw_merge(mrg, dm, ylru, dbra, ypool, dbrb, stages=()):
    nb = NCHIP
    rb, pb, cpu = D // nb, DP // nb, D // NCHIP

    def body(mrg_ref, dm_ref, yl_ref, dbra_ref, yp_ref, dbrb_ref, dwo_ref, dwl_ref, dwp_ref):
        dwo_ref[...] = _mm_tn(mrg_ref[...], dm_ref[...]).astype(BF)
        dwl_ref[...] = _mm_tn(yl_ref[...], dbra_ref[...]).astype(BF)
        dwp = _mm_tn(yp_ref[...], dbrb_ref[...]).astype(BF)
        for k in range(NCHIP):
            dwp_ref[k] = dwp[:, k * cpu:(k + 1) * cpu]

    cols = lambda w: pl.BlockSpec((T, w), lambda r: (0, r))
    whole = pl.BlockSpec((T, D), lambda r: (0, 0))
    return _call(
        body, name="dw_merge", grid=(nb,),
        in_specs=[cols(rb), whole, cols(rb), whole, cols(pb), whole],
        out_specs=[pl.BlockSpec((rb, D), lambda r: (r, 0)), pl.BlockSpec((rb, D), lambda r: (r, 0)),
                   pl.BlockSpec((NCHIP, pb, cpu), lambda r: (0, r, 0))],
        out_shape=[jax.ShapeDtypeStruct((D, D), BF), jax.ShapeDtypeStruct((DR, D), BF),
                   jax.ShapeDtypeStruct((NCHIP, DP, cpu), BF)],
        vmem=56, args=[mrg, dm, ylru, dbra, ypool, dbrb], stages=stages)


def _bwd_lru(proj, h, dylru, conv_w, conv_b, wa, ba, wx, bx, lam, stages=()):
    def body(xp_ref, g_ref, h_ref, dy_ref, cw_ref, cb_ref, wa_ref, ba_ref, wx_ref, bx_ref, lam_ref,
             dxp_ref, dg_ref, dcw_ref, dcb_ref, dwa_ref, dba_ref, dwx_ref, dbx_ref, dlam_ref, a_s, b_s, l_s):
        xp = xp_ref[...]
        cw = cw_ref[...]
        lam = lam_ref[...]
        xc, x1, x2, x3 = _conv(xp, cw, cb_ref[...])
        wa, wx = wa_ref[0], wx_ref[0]
        xcb, r, ii, sp, a, mult = _lru_gates(xc, wa, ba_ref[...], wx, bx_ref[...], lam)
        g = g_ref[...]
        gel, dgel = _gelu_parts(g)
        h = h_ref[...]
        dy = dy_ref[...]
        dg_ref[...] = (dy * h * dgel).astype(BF)
        _tile_scan(_su(a, 1, 0.0), dy * gel, a_s, b_s, l_s, reverse=True)
        b = l_s[...]
        da = b * _sd(h, 1, 0.0)
        dmult = b * (ii * xc)
        dii = b * (mult * xc)
        dxc = b * (mult * ii)
        dla = da * a - dmult * ((a * a) / mult)
        dr = dla * ((-LRU_C) * sp)
        dsp = _colsum(dla * ((-LRU_C) * r))
        dlam_ref[...] = -dsp / (1.0 + jnp.exp(lam))
        dzr = dr * (r * (1.0 - r))
        dzi = dii * (ii * (1.0 - ii))
        dzrb, dzib = dzr.astype(BF), dzi.astype(BF)
        dxc = dxc + _mm_nt(dzrb, wa) + _mm_nt(dzib, wx)
        dwa_ref[0] = _mm_tn(xcb, dzrb)
        dwx_ref[0] = _mm_tn(xcb, dzib)
        dba_ref[...] = _colsum(dzr)
        dbx_ref[...] = _colsum(dzi)
        dcb_ref[...] = _colsum(dxc)
        dcw_ref[...] = jnp.concatenate([_colsum(dxc * x3), _colsum(dxc * x2), _colsum(dxc * x1),
                                        _colsum(dxc * xp)], axis=0)
        dxp = cw[3:4] * dxc + cw[2:3] * _su(dxc, 1) + cw[1:2] * _su(dxc, 2) + cw[0:1] * _su(dxc, 3)
        dxp_ref[...] = dxp.astype(BF)

    blk = pl.BlockSpec((T, CB), lambda j: (0, j))
    wsp = pl.BlockSpec((1, CB, CB), lambda j: (j, 0, 0))
    return _call(
        body, name="bwd_lru", grid=(NG,),
        in_specs=[blk, pl.BlockSpec((T, CB), lambda j: (0, NG + j)), blk, blk,
                  pl.BlockSpec((4, CB), lambda j: (0, j)), _vec_spec(), wsp, _vec_spec(), wsp, _vec_spec(),
                  _vec_spec()],
        out_specs=[blk, blk, pl.BlockSpec((4, CB), lambda j: (0, j)), _vec_spec(), wsp, _vec_spec(), wsp,
                   _vec_spec(), _vec_spec()],
        out_shape=[jax.ShapeDtypeStruct((T, DR), BF), jax.ShapeDtypeStruct((T, DR), BF),
                   jax.ShapeDtypeStruct((4, DR), F32), jax.ShapeDtypeStruct((1, DR), F32),
                   jax.ShapeDtypeStruct((NG, CB, CB), F32), jax.ShapeDtypeStruct((1, DR), F32),
                   jax.ShapeDtypeStruct((NG, CB, CB), F32), jax.ShapeDtypeStruct((1, DR), F32),
                   jax.ShapeDtypeStruct((1, DR), F32)],
        vmem=56, args=[proj, proj, h, dylru, conv_w, conv_b, wa, ba, wx, bx, lam], stages=stages,
        scratch=[pltpu.VMEM((T, CB), F32)] * 3)


def _bwd_pool(proj, dypool, pool_w, pool_scale):
    def body(xp_ref, dy_ref, pw_ref, sc_ref, dx_ref, dw_ref, dsc_ref):
        for g, w in enumerate(POOL_WINDOWS):
            cols = slice(g * PG, (g + 1) * PG)
            cnt = _pool_cnt(w)
            x = xp_ref[:, cols]
            pb = (_pool_window(x, g + 1, _sd) / cnt - x).astype(BF)
            wg = pw_ref[g]
            dy = dy_ref[:, cols]
            dsc_ref[:, cols] = _colsum(dy * _mm(pb, wg))
            dyp = (dy * sc_ref[:, cols]).astype(BF)
            dw_ref[g] = _mm_tn(pb, dyp)
            dp = _mm_nt(dyp, wg)
            dx_ref[:, cols] = (_pool_window(dp / cnt, g + 1, _su) - dp).astype(BF)

    return pl.pallas_call(
        body, name="bwd_pool", grid=(1,),
        in_specs=[pl.BlockSpec((T, DP), lambda i: (0, 2 * DR // DP)),
                  pl.BlockSpec((T, DP), lambda i: (0, 0)),
                  pl.BlockSpec((4, PG, PG), lambda i: (0, 0, 0)),
                  pl.BlockSpec((1, DP), lambda i: (0, 0))],
        out_specs=[pl.BlockSpec((T, DP), lambda i: (0, 0)),
                   pl.BlockSpec((4, PG, PG), lambda i: (0, 0, 0)),
                   pl.BlockSpec((1, DP), lambda i: (0, 0))],
        out_shape=_hbm_out([jax.ShapeDtypeStruct((T, DP), BF), jax.ShapeDtypeStruct((4, PG, PG), F32),
                            jax.ShapeDtypeStruct((1, DP), F32)]),
        compiler_params=_cp(48),
    )(*_hbm(proj, dypool, pool_w, pool_scale))


PART_COLS = (DR, DR, DP, 2 * D)


def _shard_pieces():
    starts = [sum(PART_COLS[:p]) for p in range(len(PART_COLS))]
    shards = []
    for k in range(NCHIP):
        lo, hi = k * CW_IN, (k + 1) * CW_IN
        shards.append([(p, max(lo, s) - s, min(hi, s + wd) - s, max(lo, s) - lo)
                       for p, (s, wd) in enumerate(zip(starts, PART_COLS)) if max(lo, s) < min(hi, s + wd)])
    return shards


def _bwd_inproj_w(h1, parts, after):
    flat = [(k, *piece) for k, pieces in enumerate(_shard_pieces()) for piece in pieces]

    def body(h_hbm, p0, p1, p2, p3, after_ref, dw_hbm, h_v, dw_v, *rest):
        bufs, sem_in, sem_out = rest[:len(flat)], rest[len(flat)], rest[len(flat) + 1]
        part_refs = (p0, p1, p2, p3)
        loads = [pltpu.make_async_copy(h_hbm, h_v, sem_in.at[0])]
        for i, (k, p, a, b, c0) in enumerate(flat):
            loads.append(pltpu.make_async_copy(part_refs[p].at[:, pl.ds(a, b - a)], bufs[i], sem_in.at[1 + i]))
        for cp in loads:
            cp.start()
        loads[0].wait()
        stores = []
        for i, (k, p, a, b, c0) in enumerate(flat):
            loads[1 + i].wait()
            dw_v[k, :, c0:c0 + b - a] = _mm_tn(h_v[...], bufs[i][...]).astype(BF)
            if i + 1 == len(flat) or flat[i + 1][0] != k:
                stores.append(pltpu.make_async_copy(dw_v.at[k], dw_hbm.at[k], sem_out.at[k]))
                stores[-1].start()
        for cp in stores:
            cp.wait()

    scratch = [pltpu.VMEM((T, D), BF), pltpu.VMEM((NCHIP, D, CW_IN), BF)]
    scratch += [pltpu.VMEM((T, b - a), parts[p].dtype) for k, p, a, b, c0 in flat]
    scratch += [pltpu.SemaphoreType.DMA((1 + len(flat),)), pltpu.SemaphoreType.DMA((NCHIP,))]
    return pl.pallas_call(
        body, name="bwd_inproj_w", in_specs=[ANY] * 6, out_specs=ANY, scratch_shapes=scratch,
        out_shape=pltpu.HBM((NCHIP, D, CW_IN), BF), compiler_params=_cp(48),
    )(*_hbm(h1, *parts), after)


def _bwd_inproj_x(parts, w_in, x, dxres, g1, stages=()):
    tm = 512

    def body(p0, p1, p2, p3, w_ref, x_ref, dr_ref, g_ref, dx_ref, dg_ref):
        part_refs = (p0, p1, p2, p3)
        dh = None
        for k, pieces in enumerate(_shard_pieces()):
            for p, a, b, c0 in pieces:
                part = _mm_nt(part_refs[p][:, a:b], w_ref[k, :, c0:c0 + b - a])
                dh = part if dh is None else dh + part
        xv = x_ref[...]
        r = lax.rsqrt(_mean(xv * xv) + NORM_EPS)
        xn = xv * r
        t = dh * g_ref[...]
        dx_ref[...] = dr_ref[...] + r * (t - xn * _mean(t * xn))
        _acc(dg_ref, _colsum(dh * xn), pl.program_id(0) == 0)

    row = pl.BlockSpec((tm, D), lambda i: (i, 0))
    vec = pl.BlockSpec((1, D), lambda i: (0, 0))
    return _call(
        body, name="bwd_inproj_x", grid=(T // tm,),
        in_specs=[pl.BlockSpec((tm, wd), lambda i: (i, 0)) for wd in PART_COLS] +
                 [pl.BlockSpec((NCHIP, D, CW_IN), lambda i: (0, 0, 0)), row, row, vec],
        out_specs=[row, vec],
        out_shape=[jax.ShapeDtypeStruct((T, D), F32), jax.ShapeDtypeStruct((1, D), F32)],
        vmem=56, args=[*parts, w_in, x, dxres, g1], stages=stages)[0]


def _place():
    x, y, c = lax.axis_index("x"), lax.axis_index("y"), lax.axis_index("c")
    chips = [(1 - x, y), (x, 1 - y), (1 - x, 1 - y)]
    return x, y, c, chips


def _rcopy(src, dst, ssem, rsem, dev):
    return pltpu.make_async_remote_copy(src_ref=src, dst_ref=dst, send_sem=ssem, recv_sem=rsem,
                                        device_id=dev, device_id_type=MESH_ID)


def _sds(a):
    return jax.ShapeDtypeStruct(a.shape, a.dtype)


def _sem2(n, m):
    return [pltpu.SemaphoreType.DMA((n * m,)), pltpu.SemaphoreType.DMA((n * m,))]


ALL = (0, 1, 1)


def _piece(ref, k, half, part):
    hr = ref.shape[1] // 2
    r0, r1 = hr * part[0] // part[2], hr * part[1] // part[2]
    return ref.at[k, pl.ds(half * hr + r0, r1 - r0), :]


def _gather(fulls, ici=(), d2d=()):
    n = len(fulls)
    ici, d2d = list(ici), list(d2d)
    pieces = [("ici", i, part) for i, part in ici] + [("d2d", i, part) for i, part in d2d]

    def copies(outs, sems):
        x, y, c, chips = _place()
        me = 2 * x + y
        sib = (x, y, 1 - c)
        send, recv = [], []
        for q, (kind, i, part) in enumerate(pieces):
            for j, chip in enumerate(chips):
                k, s = 2 * chip[0] + chip[1], 3 * q + j
                if kind == "ici":
                    mine, theirs, dev = _piece(outs[i], me, c, part), _piece(outs[i], k, c, part), (*chip, c)
                else:
                    mine, theirs, dev = _piece(outs[i], k, c, part), _piece(outs[i], k, 1 - c, part), sib
                send.append(_rcopy(mine, mine, sems[0].at[s], sems[1].at[s], dev))
                recv.append(_rcopy(theirs, theirs, sems[0].at[s], sems[1].at[s], dev))
        return send, recv

    def start(ins, outs, sems):
        for cp in copies(outs, sems)[0]:
            cp.start()

    def finish(ins, outs, sems):
        send, recv = copies(outs, sems)
        for cp in recv:
            cp.wait_recv()
        for cp in send:
            cp.wait_send()

    sems = [pltpu.SemaphoreType.DMA((3 * len(pieces),)), pltpu.SemaphoreType.DMA((3 * len(pieces),))]
    return _Stage(fulls, [_sds(f) for f in fulls], {i: i for i in range(n)}, sems, start, finish)


def _gather_whole(v):
    def copies(ins, outs, sems):
        x, y, c, chips = _place()
        me = 2 * x + y
        send = [_rcopy(ins[0], outs[0].at[me], sems[0].at[j], sems[1].at[j], (*chip, c))
                for j, chip in enumerate(chips)]
        recv = [_rcopy(ins[0], outs[0].at[2 * chip[0] + chip[1]], sems[0].at[j], sems[1].at[j], (*chip, c))
                for j, chip in enumerate(chips)]
        return send, recv

    def start(ins, outs, sems):
        for cp in copies(ins, outs, sems)[0]:
            cp.start()

    def finish(ins, outs, sems):
        send, recv = copies(ins, outs, sems)
        for cp in recv:
            cp.wait_recv()
        for cp in send:
            cp.wait_send()

    return _Stage([v], [jax.ShapeDtypeStruct((NCHIP,) + v.shape, v.dtype)], {},
                  [pltpu.SemaphoreType.DMA((3,)), pltpu.SemaphoreType.DMA((3,))], start, finish)


def _to_sibling(srcs):
    n = len(srcs)

    def copies(ins, outs, sems):
        x, y, c, _ = _place()
        sib = (x, y, 1 - c)
        return [_rcopy(ins[i].at[:, 1 - c] if srcs[i].ndim == 4 else ins[i], outs[i], sems[0].at[i], sems[1].at[i], sib)
                for i in range(n)]

    def start(ins, outs, sems):
        for cp in copies(ins, outs, sems):
            cp.start()

    def finish(ins, outs, sems):
        for cp in copies(ins, outs, sems):
            cp.wait()

    shapes = [jax.ShapeDtypeStruct((NCHIP,) + s.shape[2:] if s.ndim == 4 else s.shape, s.dtype) for s in srcs]
    return _Stage(srcs, shapes, {}, [pltpu.SemaphoreType.DMA((n,)), pltpu.SemaphoreType.DMA((n,))], start, finish)


def _to_chips(srcs, parts=None, lands=None):
    n = len(srcs)
    parts = [ALL] * n if parts is None else parts
    lands = [None] * n if lands is None else lands
    given = [i for i in range(n) if lands[i] is not None]

    def rows(ref, i):
        hr = srcs[i].shape[1]
        r0, r1 = hr * parts[i][0] // parts[i][2], hr * parts[i][1] // parts[i][2]
        return ref.at[pl.ds(r0, r1 - r0), :]

    def copies(ins, outs, sems):
        x, y, c, chips = _place()
        me = 2 * x + y
        return [_rcopy(rows(ins[i].at[2 * chip[0] + chip[1]] if srcs[i].shape[0] == NCHIP else ins[i].at[c], i),
                       rows(outs[i].at[me], i), sems[0].at[3 * i + j], sems[1].at[3 * i + j], (*chip, c))
                for i in range(n) for j, chip in enumerate(chips)]

    def start(ins, outs, sems):
        for cp in copies(ins, outs, sems):
            cp.start()

    def finish(ins, outs, sems):
        for cp in copies(ins, outs, sems):
            cp.wait()

    shapes = [jax.ShapeDtypeStruct((NCHIP,) + s.shape[1:], s.dtype) for s in srcs]
    alias = {n + q: i for q, i in enumerate(given)}
    return _Stage(list(srcs) + [lands[i] for i in given], shapes, alias, _sem2(n, 3), start, finish)


HBM_REF = pl.BlockSpec(memory_space=pltpu.HBM)
SEM_REF = pl.BlockSpec(memory_space=pltpu.SEMAPHORE)
DATAFLOW = pltpu.SideEffectType.DATAFLOW_SIDE_EFFECTING


def _after(x):
    return _Stage([x], [], {}, [], lambda *a: None, lambda *a: None)


class _Flight:
    def __init__(self, stage, sems, bufs):
        self.stage, self.sems, self.bufs = stage, list(sems), list(bufs)

    def landed(self):
        st, n = self.stage, len(self.stage.operands)
        fresh = [j for j in range(len(st.out_shape)) if j not in st.alias.values()]
        back = {v: k for k, v in st.alias.items()}
        return [self.bufs[back[j]] if j in back else self.bufs[n + fresh.index(j)] for j in range(len(st.out_shape))]


def _split_call(name, finish=(), start=(), after=None):
    bufs, stage_bufs = [], []

    def slot(a):
        for i, b in enumerate(bufs):
            if b is a:
                return i
        bufs.append(a)
        return len(bufs) - 1

    fin_slots = [[slot(b) for b in fl.bufs] for fl in finish]
    for st in start:
        fresh = [lax.empty(o.shape, o.dtype) for j, o in enumerate(st.out_shape) if j not in st.alias.values()]
        stage_bufs.append([slot(a) for a in list(st.operands) + fresh])
    old_sems = [s for fl in finish for s in fl.sems]
    new_sems = [s for st in start for s in st.sems]
    nb, no, nn = len(bufs), len(old_sems), len(new_sems)

    def refs_of(st, slots, buf_refs):
        n = len(st.operands)
        ins = [buf_refs[i] for i in slots[:n]]
        fresh = [j for j in range(len(st.out_shape)) if j not in st.alias.values()]
        back = {v: k for k, v in st.alias.items()}
        outs = [ins[back[j]] if j in back else buf_refs[slots[n + fresh.index(j)]] for j in range(len(st.out_shape))]
        return ins, outs

    def body(*refs):
        buf_refs, sem_in = refs[:nb], refs[nb:nb + no]
        sem_out = refs[nb + no + (after is not None):][:nn]
        token = refs[-1]
        pos = 0
        for fl, slots in zip(finish, fin_slots):
            ins, outs = refs_of(fl.stage, slots, buf_refs)
            fl.stage.finish(ins, outs, sem_in[pos:pos + len(fl.sems)])
            pos += len(fl.sems)
        pos = 0
        for st, slots in zip(start, stage_bufs):
            ins, outs = refs_of(st, slots, buf_refs)
            st.start(ins, outs, sem_out[pos:pos + len(st.sems)])
            pos += len(st.sems)
        token[...] = jnp.zeros_like(token)

    res = pl.pallas_call(
        body, name=name,
        out_shape=tuple(new_sems) + tuple(pltpu.HBM(b.shape, b.dtype) for b in bufs) +
                  (jax.ShapeDtypeStruct((8, LANE), F32),),
        in_specs=(HBM_REF,) * nb + (SEM_REF,) * no + ((pl.BlockSpec(memory_space=pl.ANY),) if after is not None else ()),
        out_specs=(SEM_REF,) * nn + (HBM_REF,) * nb + (pl.BlockSpec(memory_space=pltpu.VMEM),),
        input_output_aliases={i: nn + i for i in range(nb)},
        compiler_params=pltpu.CompilerParams(has_side_effects=DATAFLOW),
    )(*_hbm(*bufs), *old_sems, *([after] if after is not None else []))
    sems, thru, token = res[:nn], res[nn:nn + nb], res[-1]
    for fl, slots in zip(finish, fin_slots):
        fl.bufs = [thru[i] for i in slots]
    flights, pos = [], 0
    for st, slots in zip(start, stage_bufs):
        flights.append(_Flight(st, sems[pos:pos + len(st.sems)], [thru[i] for i in slots]))
        pos += len(st.sems)
    return flights, token


def _share(pairs):
    n = len(pairs)

    def start(ins, outs, sems):
        x, y, c, _ = _place()
        for i in range(n):
            _rcopy(outs[i].at[c], outs[i].at[c], sems[0].at[i], sems[1].at[i], (x, y, 1 - c)).start()

    def finish(ins, outs, sems):
        x, y, c, _ = _place()
        for i in range(n):
            _rcopy(outs[i].at[c], outs[i].at[c], sems[0].at[i], sems[1].at[i], (x, y, 1 - c)).wait_send()
            _rcopy(outs[i].at[1 - c], outs[i].at[1 - c], sems[0].at[i], sems[1].at[i], (x, y, 1 - c)).wait_recv()

    return _Stage(pairs, [_sds(p) for p in pairs], {i: i for i in range(n)},
                  [pltpu.SemaphoreType.DMA((n,)), pltpu.SemaphoreType.DMA((n,))], start, finish)


def _row_block(rows, cols, itemsize=4, target=2 * MIB):
    br = rows
    while br * cols * itemsize > target and br % 32 == 0:
        br //= 2
    return br


def _cast_place(w, chip_idx, name):
    rows, cols = w.shape
    br = _row_block(rows, cols)

    def body(k_ref, w_ref, o_ref):
        o_ref[0] = w_ref[...].astype(BF)

    return _call(
        body, name=name, grid=(rows // br,), prefetch=chip_idx,
        in_specs=[pl.BlockSpec((br, cols), lambda r, k: (r, 0))],
        out_specs=[pl.BlockSpec((1, br, cols), lambda r, k: (k[0], r, 0))],
        out_shape=[jax.ShapeDtypeStruct((NCHIP, rows, cols), BF)], vmem=32, args=[w])[0][0]


def _cast_place_multi(ws, chip_idx, stages=()):
    br = 128
    nblk = [a.shape[0] // br for a in ws]
    starts = [sum(nblk[:i]) for i in range(len(ws))]

    def body(k_ref, *refs):
        r = pl.program_id(0)
        for i in range(len(ws)):
            @pl.when(jnp.logical_and(r >= starts[i], r < starts[i] + nblk[i]))
            def _(i=i):
                refs[len(ws) + i][0] = refs[i][...].astype(BF)

    def at(i):
        return functools.partial(lambda r, s, nb: jnp.clip(r - s, 0, nb - 1), s=starts[i], nb=nblk[i])

    outs, landed = _call(
        body, name="cast_rest", grid=(sum(nblk),), prefetch=chip_idx,
        in_specs=[pl.BlockSpec((br, a.shape[1]), functools.partial(lambda r, k, f: (f(r), 0), f=at(i)))
                  for i, a in enumerate(ws)],
        out_specs=[pl.BlockSpec((1, br, a.shape[1]), functools.partial(lambda r, k, f: (k[0], f(r), 0), f=at(i)))
                   for i, a in enumerate(ws)],
        out_shape=[jax.ShapeDtypeStruct((NCHIP,) + a.shape, BF) for a in ws], vmem=32, args=list(ws), stages=stages)
    return outs, landed


def _add_sibling(g, land, cidx, name, stages=()):
    _, _, hr, cols = g.shape
    br = _row_block(hr, cols)

    def body(c_ref, g_ref, l_ref, o_ref):
        o_ref[...] = (g_ref[0, 0].astype(F32) + l_ref[0].astype(F32)).astype(BF)[None]

    outs, st = _call(
        body, name=name, grid=(NCHIP, hr // br), prefetch=cidx,
        in_specs=[pl.BlockSpec((1, 1, br, cols), lambda k, r, c: (k, c[0], r, 0)),
                  pl.BlockSpec((1, br, cols), lambda k, r, c: (k, r, 0))],
        out_specs=[pl.BlockSpec((1, br, cols), lambda k, r, c: (k, r, 0))],
        out_shape=[jax.ShapeDtypeStruct((NCHIP, hr, cols), BF)], vmem=32, args=[g, land], stages=stages)
    return outs[0], st


def _add_sibling_multi(gs, lands, cidx, name):
    n = len(gs)
    brs = [_row_block(g.shape[2], g.shape[3]) for g in gs]
    nrb = [g.shape[2] // b for g, b in zip(gs, brs)]
    nblk = [NCHIP * q for q in nrb]
    starts = [sum(nblk[:i]) for i in range(n)]

    def body(c_ref, *refs):
        r = pl.program_id(0)
        for i in range(n):
            g_ref, l_ref, o_ref = refs[2 * i], refs[2 * i + 1], refs[2 * n + i]

            @pl.when(jnp.logical_and(r >= starts[i], r < starts[i] + nblk[i]))
            def _():
                o_ref[...] = (g_ref[0, 0].astype(F32) + l_ref[0].astype(F32)).astype(BF)[None]

    def at(i, r):
        q = jnp.clip(r - starts[i], 0, nblk[i] - 1)
        return q // nrb[i], q % nrb[i]

    def g_spec(i):
        return pl.BlockSpec((1, 1, brs[i], gs[i].shape[3]),
                            functools.partial(lambda r, c, i: (at(i, r)[0], c[0], at(i, r)[1], 0), i=i))

    def l_spec(i):
        return pl.BlockSpec((1, brs[i], gs[i].shape[3]),
                            functools.partial(lambda r, c, i: (at(i, r)[0], at(i, r)[1], 0), i=i))

    return _call(
        body, name=name, grid=(sum(nblk),), prefetch=cidx,
        in_specs=[s for i in range(n) for s in (g_spec(i), l_spec(i))], out_specs=[l_spec(i) for i in range(n)],
        out_shape=[jax.ShapeDtypeStruct(l.shape, BF) for l in lands], vmem=32,
        args=[a for i in range(n) for a in (gs[i], lands[i])])[0]


def _add_pair(a, b, name):
    rows, cols = a.shape

    def body(a_ref, b_ref, o_ref):
        o_ref[...] = a_ref[...] + b_ref[...]

    spec = pl.BlockSpec((rows, cols), lambda r: (0, 0))
    return _call(body, name=name, grid=(1,), in_specs=[spec, spec], out_specs=[spec], out_shape=[_sds(a)],
                 vmem=32, args=[a, b])[0][0]


def _add_chips(own, land, idx, name, stages=None):
    _, hr, cols = land.shape
    br = _row_block(hr, cols)

    def body(s_ref, a_ref, b_ref, c_ref, d_ref, o_ref):
        o_ref[...] = (a_ref[...].astype(F32) + b_ref[...].astype(F32)) + (c_ref[...].astype(F32) +
                                                                           d_ref[...].astype(F32))

    spec = lambda q: pl.BlockSpec((1, br, cols), functools.partial(lambda r, s, q: (s[q], r, 0), q=q))
    outs, landed = _call(
        body, name=name, grid=(hr // br,), prefetch=idx,
        in_specs=[spec(0), spec(1), spec(2), spec(3)], out_specs=[spec(4)],
        out_shape=[jax.ShapeDtypeStruct((2, hr, cols), F32)], vmem=48, args=[own, land, land, land],
        stages=stages or ())
    return outs[0] if stages is None else (outs[0], landed)


def _add_chips_multi(owns, lands, idx, name, stages=()):
    n = len(owns)
    brs = [_row_block(l.shape[1], l.shape[2]) for l in lands]
    nblk = [l.shape[1] // b for l, b in zip(lands, brs)]
    starts = [sum(nblk[:i]) for i in range(n)]

    def body(s_ref, *refs):
        r = pl.program_id(0)
        for i in range(n):
            a_ref, b_ref, c_ref, d_ref = refs[4 * i:4 * i + 4]
            o_ref = refs[4 * n + i]

            @pl.when(jnp.logical_and(r >= starts[i], r < starts[i] + nblk[i]))
            def _():
                o_ref[...] = (a_ref[...].astype(F32) + b_ref[...].astype(F32)) + (c_ref[...].astype(F32) +
                                                                                   d_ref[...].astype(F32))

    def spec(i, q):
        return pl.BlockSpec((1, brs[i], lands[i].shape[2]), functools.partial(
            lambda r, s, q, st, nb: (s[q], jnp.clip(r - st, 0, nb - 1), 0), q=q, st=starts[i], nb=nblk[i]))

    outs, landed = _call(
        body, name=name, grid=(sum(nblk),), prefetch=idx,
        in_specs=[spec(i, q) for i in range(n) for q in range(4)], out_specs=[spec(i, 4) for i in range(n)],
        out_shape=[jax.ShapeDtypeStruct((2,) + l.shape[1:], F32) for l in lands], vmem=48,
        args=[a for i in range(n) for a in (owns[i], lands[i], lands[i], lands[i])], stages=stages)
    return outs, landed


def _adamw_math(w, g, m, v):
    mn = ADAM_B1 * m + (1.0 - ADAM_B1) * g
    vn = ADAM_B2 * v + (1.0 - ADAM_B2) * (g * g)
    m_hat = mn / (1.0 - ADAM_B1 ** ADAM_STEP)
    v_hat = vn / (1.0 - ADAM_B2 ** ADAM_STEP)
    return -ADAM_LR * (m_hat / (jnp.sqrt(v_hat) + ADAM_EPS) + ADAM_WD * w), mn, vn


def _adamw(w, g, m, v, name, stages=()):
    rows, cols = w.shape
    br = _row_block(rows, cols)

    def body(w_ref, g_ref, m_ref, v_ref, go_ref, d_ref, mo_ref, vo_ref):
        gv = g_ref[...]
        go_ref[...] = gv
        d_ref[...], mo_ref[...], vo_ref[...] = _adamw_math(w_ref[...], gv, m_ref[...], v_ref[...])

    spec = pl.BlockSpec((br, cols), lambda r: (r, 0))
    return _call(body, name=name, grid=(rows // br,), in_specs=[spec] * 4, out_specs=[spec] * 4,
                 out_shape=[_sds(w)] * 4, vmem=56, args=[w, g, m, v], stages=stages)


def _adamw_multi(names, w, g, m, v, stages=()):
    cols = w[names[0]].shape[1]
    br = 128
    nblk = [w[n].shape[0] // br for n in names]
    starts = [sum(nblk[:i]) for i in range(len(names))]

    def body(*refs):
        r = pl.program_id(0)
        for i in range(len(names)):
            w_ref, g_ref, m_ref, v_ref = refs[4 * i:4 * i + 4]
            go_ref, d_ref, mo_ref, vo_ref = refs[4 * len(names) + 4 * i:4 * len(names) + 4 * i + 4]

            @pl.when(jnp.logical_and(r >= starts[i], r < starts[i] + nblk[i]))
            def _():
                gv = g_ref[...]
                go_ref[...] = gv
                d_ref[...], mo_ref[...], vo_ref[...] = _adamw_math(w_ref[...], gv, m_ref[...], v_ref[...])

    def spec(i):
        return pl.BlockSpec((br, cols), functools.partial(
            lambda r, s, nb: (jnp.clip(r - s, 0, nb - 1), 0), s=starts[i], nb=nblk[i]))

    outs, landed = _call(
        body, name="adamw_" + "_".join(names), grid=(sum(nblk),),
        in_specs=[spec(i) for i in range(len(names)) for _ in range(4)],
        out_specs=[spec(i) for i in range(len(names)) for _ in range(4)],
        out_shape=[_sds(w[n]) for n in names for _ in range(4)], vmem=56,
        args=[a[n] for n in names for a in (w, g, m, v)], stages=stages)
    return {n: outs[4 * i:4 * i + 4] for i, n in enumerate(names)}, landed


def _to_everyone(v):
    deltas = [(a, b, e) for a in (0, 1) for b in (0, 1) for e in (0, 1)][1:]

    def copies(ins, outs, sems):
        x, y, c, _ = _place()
        me = 4 * x + 2 * y + c
        flip = lambda p, f: 1 - p if f else p
        return [_rcopy(ins[0], outs[0].at[me], sems[0].at[q], sems[1].at[q], (flip(x, a), flip(y, b), flip(c, e)))
                for q, (a, b, e) in enumerate(deltas)]

    def start(ins, outs, sems):
        for cp in copies(ins, outs, sems):
            cp.start()

    def finish(ins, outs, sems):
        for cp in copies(ins, outs, sems):
            cp.wait()

    n = len(deltas)
    return _Stage([v], [jax.ShapeDtypeStruct((2 * NCHIP,) + v.shape, v.dtype)], {},
                  [pltpu.SemaphoreType.DMA((n,)), pltpu.SemaphoreType.DMA((n,))], start, finish)


SMALL_AT = {"norm_mix_pre": (0, 1, D), "norm_mix_post": (1, 1, D), "norm_mlp_pre": (2, 1, D),
            "norm_mlp_post": (3, 1, D), "b_gate": (4, 2, D), "conv_b": (6, 1, D), "lru_b_a": (7, 1, D),
            "lru_b_x": (8, 1, D), "lru_lambda": (9, 1, D), "pool_scale": (10, 1, DP)}
SMALL_SEPARATE = ["conv_w", "lru_w_a", "lru_w_x", "pool_w"]


def _adamw_small(small_sum, first_all, sep_grads, w, m, v):
    packed, sep = list(SMALL_AT), list(SMALL_SEPARATE)
    names = packed + sep

    def body(*refs):
        s_ref, a_ref, refs = refs[0], refs[1], refs[2:]
        g_sep, refs = refs[:len(sep)], refs[len(sep):]
        nn = len(names)
        w_r, m_r, v_r, refs = refs[:nn], refs[nn:2 * nn], refs[2 * nn:3 * nn], refs[3 * nn:]
        g_out, refs = refs[:len(packed)], refs[len(packed):]
        d_o, m_o, v_o = refs[:nn], refs[nn:2 * nn], refs[2 * nn:3 * nn]
        for i, n in enumerate(names):
            if i == 0:
                g = a_ref[0:1, :]
                for q in range(1, 2 * NCHIP):
                    g = g + a_ref[q:q + 1, :]
                g_out[i][...] = g
            elif n in SMALL_AT:
                r0, nr, nc = SMALL_AT[n]
                g = jnp.concatenate([s_ref[r0 + q:r0 + q + 1, :nc] for q in range(nr)], axis=1)
                g_out[i][...] = g
            else:
                g = g_sep[i - len(packed)][...]
            d_o[i][...], m_o[i][...], v_o[i][...] = _adamw_math(w_r[i][...], g, m_r[i][...], v_r[i][...])

    ws = [w[n] for n in names]
    res = pl.pallas_call(
        body, name="adamw_small",
        out_shape=[_sds(w[n]) for n in packed] + [_sds(a) for a in ws] * 3,
        compiler_params=_cp(32),
    )(*_hbm(small_sum, first_all, *sep_grads, *ws, *[m[n] for n in names], *[v[n] for n in names]))
    nn, npk = len(names), len(packed)
    grad = dict(zip(packed, res[:npk]))
    delta = dict(zip(names, res[npk:npk + nn]))
    new_m = dict(zip(names, res[npk + nn:npk + 2 * nn]))
    new_v = dict(zip(names, res[npk + 2 * nn:]))
    return grad, delta, new_m, new_v


W_NAMES = ["norm_mix_pre", "norm_mix_post", "norm_mlp_pre", "norm_mlp_post", "w_in", "b_gate", "conv_w", "conv_b",
           "lru_w_a", "lru_b_a", "lru_w_x", "lru_b_x", "lru_lambda", "pool_w", "pool_scale", "w_lru_up",
           "w_pool_up", "w_o", "w_ff1", "w_ff2"]
BIG = ["w_in", "w_lru_up", "w_pool_up", "w_o", "w_ff1", "w_ff2"]


def _block_diag(w):
    hd = w.shape[-1]
    per = CB // hd
    w4 = w.reshape(NG, per, hd, hd)
    eye = jnp.eye(per, dtype=w.dtype)
    return jnp.einsum("gpij,pq->gpiqj", w4, eye).reshape(NG, CB, CB)


def _block_diag_extract(d, hd):
    per = CB // hd
    d5 = d.reshape(NG, per, hd, per, hd)
    return jnp.stack([d5[:, p, :, p, :] for p in range(per)], axis=1).reshape(NG * per, hd, hd)


def _halves(g):
    return g.reshape(NCHIP, 2, g.size // (g.shape[-1] * 2 * NCHIP), g.shape[-1])


def kernel(x, norm_mix_pre, norm_mix_post, norm_mlp_pre, norm_mlp_post, w_in, b_gate, conv_w, conv_b, lru_w_a, lru_b_a, lru_w_x, lru_b_x, lru_lambda, pool_w, pool_scale, w_lru_up, w_pool_up, w_o, w_ff1, w_ff2, loss_target, m_norm_mix_pre, m_norm_mix_post, m_norm_mlp_pre, m_norm_mlp_post, m_w_in, m_b_gate, m_conv_w, m_conv_b, m_lru_w_a, m_lru_b_a, m_lru_w_x, m_lru_b_x, m_lru_lambda, m_pool_w, m_pool_scale, m_w_lru_up, m_w_pool_up, m_w_o, m_w_ff1, m_w_ff2, v_norm_mix_pre, v_norm_mix_post, v_norm_mlp_pre, v_norm_mlp_post, v_w_in, v_b_gate, v_conv_w, v_conv_b, v_lru_w_a, v_lru_b_a, v_lru_w_x, v_lru_b_x, v_lru_lambda, v_pool_w, v_pool_scale, v_w_lru_up, v_w_pool_up, v_w_o, v_w_ff1, v_w_ff2):
    args = dict(locals())
    two_d = lambda a: a.reshape(-1, a.shape[-1])
    w = {n: two_d(args[n]) for n in W_NAMES}
    mom = {n: two_d(args["m_" + n]) for n in W_NAMES}
    var = {n: two_d(args["v_" + n]) for n in W_NAMES}
    i32 = lambda val: jnp.asarray(val, jnp.int32)
    chip = i32(2 * lax.axis_index("x") + lax.axis_index("y"))
    core = i32(lax.axis_index("c"))
    cidx = core.reshape(1)
    zero = i32(0)
    hd = lru_w_a.shape[-1]
    xs, target = x[0], loss_target[0]
    g1, g2, g3, g4 = norm_mix_pre, norm_mix_post, norm_mlp_pre, norm_mlp_post

    mix = ["w_lru_up", "w_pool_up", "w_o"]
    full = {"w_in": _cast_place(w["w_in"], chip.reshape(1), "cast_w_in")}
    (fl_in, fl_conv), first = _split_call("gather_start_first", start=[
        _gather([full["w_in"]], ici=[(0, ALL)]), _gather_whole(w["conv_w"])])
    idx_big = jnp.stack([chip, (chip + 1) % NCHIP, (chip + 2) % NCHIP, (chip + 3) % NCHIP, core])
    proj, h1 = _fwd_inproj_own(xs, g1, fl_in.bufs[0], idx_big, stages=[_after(first)])
    casts, _ = _cast_place_multi([w[n] for n in BIG[1:]], chip.reshape(1), stages=[_after(h1)])
    full.update(zip(BIG[1:], casts))
    (fl_mix, fl_ff1, fl_ff2), started = _split_call("gather_start_rest", start=[
        _gather([full[n] for n in mix], ici=[(0, ALL), (1, ALL), (2, ALL)]),
        _gather([full["w_ff1"]], ici=[(0, ALL)]), _gather([full["w_ff2"]], ici=[(0, ALL)])])
    wa = _block_diag(lru_w_a[0]).astype(BF)
    wx = _block_diag(lru_w_x[0]).astype(BF)
    pw = pool_w[0].astype(BF)

    def to_sibling(name, flight, after=None):
        (fl,), passed = _split_call(name + "_pass", finish=[flight], after=after,
                                    start=[_gather(flight.landed(), d2d=[(i, ALL) for i in range(len(flight.bufs))])])
        passed_on.append(passed)
        return fl

    passed_on = []

    def arrived(name, flight, after=None):
        _split_call(name + "_done", finish=[flight], after=after)
        return flight.landed()

    fl_in = to_sibling("gather_w_in", fl_in, after=started)
    _split_call("gather_w_in_done", finish=[fl_in, fl_conv])
    (w_in_f,), (conv_all,) = fl_in.landed(), fl_conv.landed()
    full["w_in"] = w_in_f
    conv_all = lax.dynamic_update_slice(conv_all, w["conv_w"][None], (chip, zero, zero))
    conv_full = jnp.transpose(conv_all, (1, 0, 2)).reshape(4, DR)
    proj = _fwd_inproj_rest(h1, w_in_f, proj, idx_big)
    fl_mix = to_sibling("gather_mix", fl_mix, after=proj)
    (ylru, hs), _ = _fwd_lru(proj, conv_full, conv_b, wa, lru_b_a, wx, lru_b_x, lru_lambda,
                             stages=[_after(passed_on[-1])])
    got = arrived("gather_mix", fl_mix, after=ylru)
    fl_ff1 = to_sibling("gather_ff1", fl_ff1, after=ylru)
    w_lru_up_f, w_pool_up_f, w_o_f = got[0].reshape(DR, D), got[1], got[2].reshape(D, D)
    ypool = _fwd_pool(proj, pw, pool_scale)
    (x2, h2, m, mrg, bra, brb), _ = _fwd_merge(xs, ylru, ypool, proj, b_gate, g2, g3, w_lru_up_f, w_pool_up_f, w_o_f,
                                               stages=[_after(passed_on[-1])])
    fl_ff2 = to_sibling("gather_ff2", fl_ff2, after=h2)
    _split_call("gather_ff_done", finish=[fl_ff1, fl_ff2])
    (ff1,), (ff2,) = fl_ff1.landed(), fl_ff2.landed()
    ff2 = ff2.reshape(DF, D)
    a1, lossp, dy, df, dg4 = _fwd_mlp_loss(h2, ff1, ff2, x2, target, g4)

    dh2, df1 = _bwd_mlp_x(df, a1, ff1, ff2)
    dw_ff1, dw_ff2 = _bwd_mlp_w(df, h2, a1, df1)
    g_ff = [_halves(dw_ff1), _halves(dw_ff2)]
    (dxres, dgates, dylru, dypool, dm, dbra, dbrb, dg2, dg3, dbg), (l_ff,) = _bwd_merge(
        dh2, dy, x2, m, bra, brb, proj, b_gate, g2, g3, w_lru_up_f, w_pool_up_f, w_o_f, stages=[_to_sibling(g_ff)])
    p_ff = _add_sibling_multi(g_ff, l_ff, cidx, "add_sibling_ff")
    (fl_ff,), sent_ff = _split_call("reduce_ff_start", start=[_to_chips(p_ff)])
    (dw_o, dw_lru_up, dw_pool_up), _ = _dw_merge(mrg, dm, ylru, dbra, ypool, dbrb, stages=[_after(sent_ff)])
    g_mix = [_halves(dw_lru_up), _halves(dw_pool_up), _halves(dw_o)]
    (dxp, dgl, dcw, dcb, dwa, dba, dwx, dbx, dlam), (l_mix,) = _bwd_lru(
        proj, hs, dylru, conv_full, conv_b, wa, lru_b_a, wx, lru_b_x, lru_lambda, stages=[_to_sibling(g_mix)])
    p_mix = _add_sibling_multi(g_mix, l_mix, cidx, "add_sibling_mix")
    dxpool, dpw, dsc = _bwd_pool(proj, dypool, pw, pool_scale)
    dproj = [dxp, dgl, dxpool, dgates]
    small = jnp.concatenate([
        jnp.zeros((1, D), F32), dg2, dg3, dg4, dbg.reshape(2, D), dcb, dba, dbx, dlam,
        jnp.pad(dsc, ((0, 0), (0, D - DP))), jnp.pad(lossp, ((0, 0), (0, D - 1))), dcw,
        _block_diag_extract(dwa, hd).reshape(-1, D), _block_diag_extract(dwx, hd).reshape(-1, D),
        dpw.reshape(-1, D)], axis=0)
    (fl_mixr, fl_smalls), sent_mix = _split_call("reduce_mix_start", start=[_to_chips(p_mix), _to_sibling([small])])
    dw_in = _bwd_inproj_w(h1, dproj, sent_mix)
    _split_call("reduce_small_sibling_done", finish=[fl_smalls], after=dw_in)
    small, l_small = fl_smalls.bufs
    small2 = _add_pair(small, l_small, "add_sibling_small").reshape(2, SMALL_ROWS // 2, D)
    g_in = _halves(dw_in)
    done = ["w_ff1", "w_ff2"] + mix
    (fl_gin, fl_small), sib_started = _split_call("reduce_in_sibling_start", finish=[fl_ff, fl_mixr],
                                                  start=[_to_sibling([g_in]), _to_chips([small2])])
    p_ff1, p_ff2, c_ff1, c_ff2 = fl_ff.bufs
    p_mix, c_mix = fl_mixr.bufs[:3], fl_mixr.bufs[3:]
    pairs_ff, _ = _add_chips_multi([p_ff1, p_ff2], [c_ff1, c_ff2], idx_big, "add_chips_ff", stages=[_after(sib_started)])
    _split_call("reduce_in_sibling_done", finish=[fl_gin], after=pairs_ff[-1])
    g_in, l_in = fl_gin.bufs
    p_in = _add_sibling(g_in, l_in, cidx, "add_sibling_w_in")[0]
    (fl_pin,), token = _split_call("reduce_last_start", start=[_to_chips([p_in])])
    pairs_mix, _ = _add_chips_multi(p_mix, c_mix, idx_big, "add_chips_mix", stages=[_after(token)])
    pairs = pairs_ff + pairs_mix
    _split_call("reduce_small_done", finish=[fl_small], after=pairs[-1])
    small2, c_small = fl_small.bufs
    own_small = lax.dynamic_index_in_dim(small2, core, 0, keepdims=True)
    c_small = lax.dynamic_update_slice(c_small, own_small, (chip, zero, zero))
    pair_small = _add_chips(c_small, c_small, jnp.stack([zero, zero + 1, zero + 2, zero + 3, core]), "add_chips_small")
    (fl_share,), shared_start = _split_call("reduce_share_start", start=[_share(pairs + [pair_small])])
    grad_x, dg1 = _bwd_inproj_x(dproj, full["w_in"], xs, dxres, g1, stages=[_after(shared_start)])
    _split_call("reduce_share_done", finish=[fl_share, fl_pin], after=dg1)
    shared, (p_in, c_in) = fl_share.landed(), fl_pin.bufs
    pairs, pair_small = shared[:-1], shared[-1]

    grads, delta, new_m, new_v = {}, {}, {}, {}
    for n, p in zip(done, pairs):
        grads[n] = p.reshape(-1, p.shape[-1])

    def update(n, stages=()):
        (grads[n], delta[n], new_m[n], new_v[n]), landed = _adamw(w[n], grads[n], mom[n], var[n], "adamw_" + n,
                                                                  stages=stages)
        return landed

    pair_in = _add_chips(p_in, c_in, idx_big, "add_chips_w_in")
    (fl_last, fl_dg1), last_start = _split_call("reduce_last_share_start", start=[_share([pair_in]), _to_everyone(dg1)])
    updated, _ = _adamw_multi(["w_ff1", "w_ff2", "w_o", "w_lru_up"], w, grads, mom, var, stages=[_after(last_start)])
    for n, (go, d, mo, vo) in updated.items():
        grads[n], delta[n], new_m[n], new_v[n] = go, d, mo, vo
    _split_call("reduce_last_share_done", finish=[fl_last, fl_dg1], after=new_v["w_lru_up"])
    (pair_in,), (dg1, dg1_all) = fl_last.landed(), fl_dg1.bufs
    dg1_all = lax.dynamic_update_slice(dg1_all, dg1[None], (2 * chip + core, zero, zero)).reshape(2 * NCHIP, D)
    grads["w_in"] = pair_in.reshape(-1, pair_in.shape[-1])
    update("w_pool_up")
    update("w_in")
    small_sum = pair_small.reshape(SMALL_ROWS, D)
    loss = 0.5 * small_sum[LOSS_ROW, 0]
    ccols = DR // NCHIP
    sep = [lax.dynamic_slice(small_sum[12:16], (zero, chip * ccols), (4, ccols)),
           small_sum[16:80].reshape(-1, hd), small_sum[80:144].reshape(-1, hd), small_sum[144:208].reshape(-1, PG)]
    g_s, d_s, m_s, v_s = _adamw_small(small_sum, dg1_all, sep, w, mom, var)
    grads.update(g_s)
    grads.update(dict(zip(SMALL_SEPARATE, sep)))
    delta.update(d_s)
    new_m.update(m_s)
    new_v.update(v_s)

    out = lambda d: [d[n].reshape(args[n].shape) for n in W_NAMES]
    return (loss, grad_x[None], *out(grads), *out(delta), *out(new_m), *out(new_v))
```
